```python
import math
import jax, jax.numpy as jnp
from jax import lax
import numpy as np

D_MODEL = 1024
BATCH = 8
SEQ = 4096
DEPTH = 2

CHUNK = 64
MEM_LEN = 256
EPS = 1e-6
A_WIDTH = 512
A_GROUPS = 4
A_GROUP_DIM = A_WIDTH // A_GROUPS
GMLP_BLOCK = 128
B_WIDTH = 512
CONV_WIDTH = 31
MIX_WIDTH = A_WIDTH + B_WIDTH
IN_WIDTH = 2 * A_WIDTH + 2 * B_WIDTH
C_WIDTH = 512
C_GROUP_CH = 16
C_GROUPS = C_WIDTH // C_GROUP_CH
C_STATE = 64
DT_MIN = 1e-3
DT_MAX = 1e-1
CA_HEADS = 4
CA_HEAD_DIM = D_MODEL // CA_HEADS
FFN_HIDDEN = -(-8 * D_MODEL // (3 * 256)) * 256
N_EVEN = (DEPTH + 1) // 2
N_ODD = DEPTH // 2

kernel_name = "chunk_causal_hybrid_gmlp_conformer_s5_trunk"


def rmsnorm(x, g):
    xf = x.astype(jnp.float32)
    y = xf * lax.rsqrt(jnp.mean(xf * xf, axis=-1, keepdims=True) + EPS)
    return (y * g.astype(jnp.float32)).astype(x.dtype)


def layernorm(x, g=None, b=None):
    xf = x.astype(jnp.float32)
    mu = jnp.mean(xf, axis=-1, keepdims=True)
    xc = xf - mu
    y = xc * lax.rsqrt(jnp.mean(xc * xc, axis=-1, keepdims=True) + EPS)
    if g is not None:
        y = y * g.astype(jnp.float32) + b.astype(jnp.float32)
    return y.astype(x.dtype)


def gmlp_spatial_gate(u, v, w_s, b_s):
    bn, s, _ = v.shape
    v = layernorm(v)
    v = v.reshape(bn, s // GMLP_BLOCK, GMLP_BLOCK, A_GROUPS, A_GROUP_DIM)
    chunk_id = jnp.arange(GMLP_BLOCK) // CHUNK
    mask = chunk_id[None, :] <= chunk_id[:, None]
    w = jnp.where(mask[None], w_s, jnp.zeros_like(w_s))
    sg = jnp.einsum('gij,bnjgc->bnigc', w, v) + b_s.T[None, None, :, :, None]
    return u * sg.reshape(bn, s, A_WIDTH)


def conformer_conv(a, g, conv_w, conv_b, ln_g, ln_b):
    h = a * jax.nn.sigmoid(g)
    h = lax.conv_general_dilated(
        h, conv_w[:, None, :].astype(h.dtype), window_strides=(1,),
        padding=[(CONV_WIDTH - 1, 0)], dimension_numbers=('NWC', 'WIO', 'NWC'),
        feature_group_count=B_WIDTH) + conv_b
    h = layernorm(h, ln_g, ln_b)
    return jax.nn.silu(h)


def _complex_affine_combine(e1, e2):
    a1r, a1i, b1r, b1i = e1
    a2r, a2i, b2r, b2i = e2
    ar = a1r * a2r - a1i * a2i
    ai = a1r * a2i + a1i * a2r
    br = a2r * b1r - a2i * b1i + b2r
    bi = a2r * b1i + a2i * b1r + b2i
    return (ar, ai, br, bi)


def s5_layer(u, lam_re, lam_im, log_dt, b_re, b_im, c_re, c_im, d_skip):
    bn, s, _ = u.shape
    f32 = jnp.float32
    uf = u.astype(f32)
    dt = jnp.exp(log_dt.astype(f32))[:, None]
    lr = lam_re.astype(f32)
    li = lam_im.astype(f32)
    mag = jnp.exp(lr * dt)
    ar = mag * jnp.cos(li * dt)
    ai = mag * jnp.sin(li * dt)
    den = lr * lr + li * li
    qr = ((ar - 1.0) * lr + ai * li) / den
    qi = (ai * lr - (ar - 1.0) * li) / den
    br_ = b_re.astype(f32)
    bi_ = b_im.astype(f32)
    bbr = qr[..., None] * br_ - qi[..., None] * bi_
    bbi = qr[..., None] * bi_ + qi[..., None] * br_
    ug = uf.reshape(bn, s, C_GROUPS, C_GROUP_CH).transpose(1, 0, 2, 3)
    bu_r = jnp.einsum('sbgc,gpc->sbgp', ug, bbr)
    bu_i = jnp.einsum('sbgc,gpc->sbgp', ug, bbi)
    a_r = jnp.broadcast_to(ar[None, None], (s, 1, C_GROUPS, C_STATE))
    a_i = jnp.broadcast_to(ai[None, None], (s, 1, C_GROUPS, C_STATE))
    _, _, xr, xi = lax.associative_scan(_complex_affine_combine, (a_r, a_i, bu_r, bu_i), axis=0)
    y = (jnp.einsum('sbgp,gcp->sbgc', xr, c_re.astype(f32))
         - jnp.einsum('sbgp,gcp->sbgc', xi, c_im.astype(f32)))
    y = y.transpose(1, 0, 2, 3).reshape(bn, s, C_WIDTH) + d_skip.astype(f32) * uf
    return y.astype(u.dtype)


def cross_attention(xn, memn, wq, wk, wv, wo):
    bn, s, _ = xn.shape
    m = memn.shape[1]
    q = (xn @ wq).reshape(bn, s, CA_HEADS, CA_HEAD_DIM)
    k = (memn @ wk).reshape(bn, m, CA_HEADS, CA_HEAD_DIM)
    v = (memn @ wv).reshape(bn, m, CA_HEADS, CA_HEAD_DIM)
    sc = jnp.einsum('bshd,bmhd->bhsm', q, k).astype(jnp.float32) * (CA_HEAD_DIM ** -0.5)
    p = jax.nn.softmax(sc, axis=-1).astype(v.dtype)
    o = jnp.einsum('bhsm,bmhd->bshd', p, v).reshape(bn, s, D_MODEL)
    return o @ wo


def swiglu(xn, wg, wu, wd):
    return (jax.nn.silu(xn @ wg) * (xn @ wu)) @ wd


def _fwd_setup_inputs(seed: int = 0) -> dict:
    key = jax.random.key(seed)
    ks = iter(jax.random.split(key, 48))

    def nrm(shape, scale):
        return jax.random.normal(next(ks), shape, jnp.float32) * scale

    def gain(shape):
        return 1.0 + nrm(shape, 0.02)

    d, h = D_MODEL, FFN_HIDDEN
    inp = {}
    inp['x'] = nrm((BATCH, SEQ, d), 1.0)
    inp['mem'] = nrm((BATCH, MEM_LEN, d), 1.0)
    inp['e_norm'] = gain((N_EVEN, d))
    inp['e_w_in'] = nrm((N_EVEN, d, IN_WIDTH), d ** -0.5)
    inp['e_gmlp_w'] = nrm((N_EVEN, A_GROUPS, GMLP_BLOCK, GMLP_BLOCK), 0.5 * GMLP_BLOCK ** -0.5)
    inp['e_gmlp_b'] = gain((N_EVEN, A_GROUPS, GMLP_BLOCK))
    inp['e_conv_w'] = nrm((N_EVEN, CONV_WIDTH, B_WIDTH), CONV_WIDTH ** -0.5)
    inp['e_conv_b'] = nrm((N_EVEN, B_WIDTH), 0.02)
    inp['e_conv_ln_g'] = gain((N_EVEN, B_WIDTH))
    inp['e_conv_ln_b'] = nrm((N_EVEN, B_WIDTH), 0.02)
    inp['e_w_out'] = nrm((N_EVEN, MIX_WIDTH, d), MIX_WIDTH ** -0.5)
    inp['o_norm'] = gain((N_ODD, d))
    inp['o_w_in'] = nrm((N_ODD, d, C_WIDTH), d ** -0.5)
    inp['o_lam_re'] = -0.5 + nrm((N_ODD, C_GROUPS, C_STATE), 0.01)
    inp['o_lam_im'] = (math.pi * jnp.arange(C_STATE, dtype=jnp.float32))[None, None, :] + nrm((N_ODD, C_GROUPS, C_STATE), 0.01)
    inp['o_log_dt'] = jax.random.uniform(next(ks), (N_ODD, C_GROUPS), jnp.float32, math.log(DT_MIN), math.log(DT_MAX))
    inp['o_b_re'] = nrm((N_ODD, C_GROUPS, C_STATE, C_GROUP_CH), (2 * C_GROUP_CH) ** -0.5)
    inp['o_b_im'] = nrm((N_ODD, C_GROUPS, C_STATE, C_GROUP_CH), (2 * C_GROUP_CH) ** -0.5)
    inp['o_c_re'] = nrm((N_ODD, C_GROUPS, C_GROUP_CH, C_STATE), (2 * C_STATE) ** -0.5)
    inp['o_c_im'] = nrm((N_ODD, C_GROUPS, C_GROUP_CH, C_STATE), (2 * C_STATE) ** -0.5)
    inp['o_d'] = gain((N_ODD, C_WIDTH))
    inp['o_w_out'] = nrm((N_ODD, C_WIDTH, 2 * d), C_WIDTH ** -0.5)
    inp['ca_norm'] = gain((DEPTH, d))
    inp['ca_mem_norm'] = gain((DEPTH, d))
    inp['ca_wq'] = nrm((DEPTH, d, d), d ** -0.5)
    inp['ca_wk'] = nrm((DEPTH, d, d), d ** -0.5)
    inp['ca_wv'] = nrm((DEPTH, d, d), d ** -0.5)
    inp['ca_wo'] = nrm((DEPTH, d, d), d ** -0.5)
    inp['ffn_norm'] = gain((DEPTH, d))
    inp['ffn_w_gate'] = nrm((DEPTH, d, h), d ** -0.5)
    inp['ffn_w_up'] = nrm((DEPTH, d, h), d ** -0.5)
    inp['ffn_w_down'] = nrm((DEPTH, h, d), h ** -0.5)
    inp['final_norm'] = gain((d,))
    return inp


def _fwd_reference(x, mem, e_norm, e_w_in, e_gmlp_w, e_gmlp_b, e_conv_w, e_conv_b, e_conv_ln_g,
              e_conv_ln_b, e_w_out, o_norm, o_w_in, o_lam_re, o_lam_im, o_log_dt, o_b_re,
              o_b_im, o_c_re, o_c_im, o_d, o_w_out, ca_norm, ca_mem_norm, ca_wq, ca_wk, ca_wv,
              ca_wo, ffn_norm, ffn_w_gate, ffn_w_up, ffn_w_down, final_norm):
    for i in range(DEPTH):
        j = i // 2
        if i % 2 == 0:
            hn = rmsnorm(x, e_norm[j])
            proj = hn @ e_w_in[j]
            a_u, a_v, b_a, b_g = jnp.split(proj, [A_WIDTH, 2 * A_WIDTH, 2 * A_WIDTH + B_WIDTH], axis=-1)
            out_a = gmlp_spatial_gate(jax.nn.gelu(a_u), jax.nn.gelu(a_v), e_gmlp_w[j], e_gmlp_b[j])
            out_b = conformer_conv(b_a, b_g, e_conv_w[j], e_conv_b[j], e_conv_ln_g[j], e_conv_ln_b[j])
            mix = jnp.concatenate([out_a, out_b], axis=-1) @ e_w_out[j]
        else:
            hn = rmsnorm(x, o_norm[j])
            u = hn @ o_w_in[j]
            y = s5_layer(u, o_lam_re[j], o_lam_im[j], o_log_dt[j], o_b_re[j], o_b_im[j],
                         o_c_re[j], o_c_im[j], o_d[j])
            o = jax.nn.gelu(y) @ o_w_out[j]
            mix = o[..., :D_MODEL] * jax.nn.sigmoid(o[..., D_MODEL:])
        x = x + mix
        x = x + cross_attention(rmsnorm(x, ca_norm[i]), rmsnorm(mem, ca_mem_norm[i]),
                                ca_wq[i], ca_wk[i], ca_wv[i], ca_wo[i])
        x = x + swiglu(rmsnorm(x, ffn_norm[i]), ffn_w_gate[i], ffn_w_up[i], ffn_w_down[i])
    return rmsnorm(x, final_norm)


import jax as _jax
import jax.numpy as _jnp

TWIN_FORMAT = 'train_step'
FWD_PARAMS = ['x', 'mem', 'e_norm', 'e_w_in', 'e_gmlp_w', 'e_gmlp_b', 'e_conv_w', 'e_conv_b', 'e_conv_ln_g', 'e_conv_ln_b', 'e_w_out', 'o_norm', 'o_w_in', 'o_lam_re', 'o_lam_im', 'o_log_dt', 'o_b_re', 'o_b_im', 'o_c_re', 'o_c_im', 'o_d', 'o_w_out', 'ca_norm', 'ca_mem_norm', 'ca_wq', 'ca_wk', 'ca_wv', 'ca_wo', 'ffn_norm', 'ffn_w_gate', 'ffn_w_up', 'ffn_w_down', 'final_norm']
TWIN_WEIGHTS = ['e_norm', 'e_w_in', 'e_gmlp_w', 'e_gmlp_b', 'e_conv_w', 'e_conv_b', 'e_conv_ln_g', 'e_conv_ln_b', 'e_w_out', 'o_norm', 'o_w_in', 'o_lam_re', 'o_lam_im', 'o_log_dt', 'o_b_re', 'o_b_im', 'o_c_re', 'o_c_im', 'o_d', 'o_w_out', 'ca_norm', 'ca_mem_norm', 'ca_wq', 'ca_wk', 'ca_wv', 'ca_wo', 'ffn_norm', 'ffn_w_gate', 'ffn_w_up', 'ffn_w_down', 'final_norm']
TWIN_DIFF_INPUT = 'x'
TWIN_INPUTS = ['x', 'mem', 'e_norm', 'e_w_in', 'e_gmlp_w', 'e_gmlp_b', 'e_conv_w', 'e_conv_b', 'e_conv_ln_g', 'e_conv_ln_b', 'e_w_out', 'o_norm', 'o_w_in', 'o_lam_re', 'o_lam_im', 'o_log_dt', 'o_b_re', 'o_b_im', 'o_c_re', 'o_c_im', 'o_d', 'o_w_out', 'ca_norm', 'ca_mem_norm', 'ca_wq', 'ca_wk', 'ca_wv', 'ca_wo', 'ffn_norm', 'ffn_w_gate', 'ffn_w_up', 'ffn_w_down', 'final_norm', 'loss_target', 'm_e_norm', 'm_e_w_in', 'm_e_gmlp_w', 'm_e_gmlp_b', 'm_e_conv_w', 'm_e_conv_b', 'm_e_conv_ln_g', 'm_e_conv_ln_b', 'm_e_w_out', 'm_o_norm', 'm_o_w_in', 'm_o_lam_re', 'm_o_lam_im', 'm_o_log_dt', 'm_o_b_re', 'm_o_b_im', 'm_o_c_re', 'm_o_c_im', 'm_o_d', 'm_o_w_out', 'm_ca_norm', 'm_ca_mem_norm', 'm_ca_wq', 'm_ca_wk', 'm_ca_wv', 'm_ca_wo', 'm_ffn_norm', 'm_ffn_w_gate', 'm_ffn_w_up', 'm_ffn_w_down', 'm_final_norm', 'v_e_norm', 'v_e_w_in', 'v_e_gmlp_w', 'v_e_gmlp_b', 'v_e_conv_w', 'v_e_conv_b', 'v_e_conv_ln_g', 'v_e_conv_ln_b', 'v_e_w_out', 'v_o_norm', 'v_o_w_in', 'v_o_lam_re', 'v_o_lam_im', 'v_o_log_dt', 'v_o_b_re', 'v_o_b_im', 'v_o_c_re', 'v_o_c_im', 'v_o_d', 'v_o_w_out', 'v_ca_norm', 'v_ca_mem_norm', 'v_ca_wq', 'v_ca_wk', 'v_ca_wv', 'v_ca_wo', 'v_ffn_norm', 'v_ffn_w_gate', 'v_ffn_w_up', 'v_ffn_w_down', 'v_final_norm']
TWIN_OUTPUTS = ['loss', 'grad_x', 'grad_e_norm', 'grad_e_w_in', 'grad_e_gmlp_w', 'grad_e_gmlp_b', 'grad_e_conv_w', 'grad_e_conv_b', 'grad_e_conv_ln_g', 'grad_e_conv_ln_b', 'grad_e_w_out', 'grad_o_norm', 'grad_o_w_in', 'grad_o_lam_re', 'grad_o_lam_im', 'grad_o_log_dt', 'grad_o_b_re', 'grad_o_b_im', 'grad_o_c_re', 'grad_o_c_im', 'grad_o_d', 'grad_o_w_out', 'grad_ca_norm', 'grad_ca_mem_norm', 'grad_ca_wq', 'grad_ca_wk', 'grad_ca_wv', 'grad_ca_wo', 'grad_ffn_norm', 'grad_ffn_w_gate', 'grad_ffn_w_up', 'grad_ffn_w_down', 'grad_final_norm', 'delta_e_norm', 'delta_e_w_in', 'delta_e_gmlp_w', 'delta_e_gmlp_b', 'delta_e_conv_w', 'delta_e_conv_b', 'delta_e_conv_ln_g', 'delta_e_conv_ln_b', 'delta_e_w_out', 'delta_o_norm', 'delta_o_w_in', 'delta_o_lam_re', 'delta_o_lam_im', 'delta_o_log_dt', 'delta_o_b_re', 'delta_o_b_im', 'delta_o_c_re', 'delta_o_c_im', 'delta_o_d', 'delta_o_w_out', 'delta_ca_norm', 'delta_ca_mem_norm', 'delta_ca_wq', 'delta_ca_wk', 'delta_ca_wv', 'delta_ca_wo', 'delta_ffn_norm', 'delta_ffn_w_gate', 'delta_ffn_w_up', 'delta_ffn_w_down', 'delta_final_norm', 'new_m_e_norm', 'new_m_e_w_in', 'new_m_e_gmlp_w', 'new_m_e_gmlp_b', 'new_m_e_conv_w', 'new_m_e_conv_b', 'new_m_e_conv_ln_g', 'new_m_e_conv_ln_b', 'new_m_e_w_out', 'new_m_o_norm', 'new_m_o_w_in', 'new_m_o_lam_re', 'new_m_o_lam_im', 'new_m_o_log_dt', 'new_m_o_b_re', 'new_m_o_b_im', 'new_m_o_c_re', 'new_m_o_c_im', 'new_m_o_d', 'new_m_o_w_out', 'new_m_ca_norm', 'new_m_ca_mem_norm', 'new_m_ca_wq', 'new_m_ca_wk', 'new_m_ca_wv', 'new_m_ca_wo', 'new_m_ffn_norm', 'new_m_ffn_w_gate', 'new_m_ffn_w_up', 'new_m_ffn_w_down', 'new_m_final_norm', 'new_v_e_norm', 'new_v_e_w_in', 'new_v_e_gmlp_w', 'new_v_e_gmlp_b', 'new_v_e_conv_w', 'new_v_e_conv_b', 'new_v_e_conv_ln_g', 'new_v_e_conv_ln_b', 'new_v_e_w_out', 'new_v_o_norm', 'new_v_o_w_in', 'new_v_o_lam_re', 'new_v_o_lam_im', 'new_v_o_log_dt', 'new_v_o_b_re', 'new_v_o_b_im', 'new_v_o_c_re', 'new_v_o_c_im', 'new_v_o_d', 'new_v_o_w_out', 'new_v_ca_norm', 'new_v_ca_mem_norm', 'new_v_ca_wq', 'new_v_ca_wk', 'new_v_ca_wv', 'new_v_ca_wo', 'new_v_ffn_norm', 'new_v_ffn_w_gate', 'new_v_ffn_w_up', 'new_v_ffn_w_down', 'new_v_final_norm']
TWIN_LEAF_KINDS = {'loss': 'loss', 'grad_x': 'grad_x', 'grad_e_norm': 'grad_w', 'grad_e_w_in': 'grad_w', 'grad_e_gmlp_w': 'grad_w', 'grad_e_gmlp_b': 'grad_w', 'grad_e_conv_w': 'grad_w', 'grad_e_conv_b': 'grad_w', 'grad_e_conv_ln_g': 'grad_w', 'grad_e_conv_ln_b': 'grad_w', 'grad_e_w_out': 'grad_w', 'grad_o_norm': 'grad_w', 'grad_o_w_in': 'grad_w', 'grad_o_lam_re': 'grad_w', 'grad_o_lam_im': 'grad_w', 'grad_o_log_dt': 'grad_w', 'grad_o_b_re': 'grad_w', 'grad_o_b_im': 'grad_w', 'grad_o_c_re': 'grad_w', 'grad_o_c_im': 'grad_w', 'grad_o_d': 'grad_w', 'grad_o_w_out': 'grad_w', 'grad_ca_norm': 'grad_w', 'grad_ca_mem_norm': 'grad_w', 'grad_ca_wq': 'grad_w', 'grad_ca_wk': 'grad_w', 'grad_ca_wv': 'grad_w', 'grad_ca_wo': 'grad_w', 'grad_ffn_norm': 'grad_w', 'grad_ffn_w_gate': 'grad_w', 'grad_ffn_w_up': 'grad_w', 'grad_ffn_w_down': 'grad_w', 'grad_final_norm': 'grad_w', 'delta_e_norm': 'delta_w', 'delta_e_w_in': 'delta_w', 'delta_e_gmlp_w': 'delta_w', 'delta_e_gmlp_b': 'delta_w', 'delta_e_conv_w': 'delta_w', 'delta_e_conv_b': 'delta_w', 'delta_e_conv_ln_g': 'delta_w', 'delta_e_conv_ln_b': 'delta_w', 'delta_e_w_out': 'delta_w', 'delta_o_norm': 'delta_w', 'delta_o_w_in': 'delta_w', 'delta_o_lam_re': 'delta_w', 'delta_o_lam_im': 'delta_w', 'delta_o_log_dt': 'delta_w', 'delta_o_b_re': 'delta_w', 'delta_o_b_im': 'delta_w', 'delta_o_c_re': 'delta_w', 'delta_o_c_im': 'delta_w', 'delta_o_d': 'delta_w', 'delta_o_w_out': 'delta_w', 'delta_ca_norm': 'delta_w', 'delta_ca_mem_norm': 'delta_w', 'delta_ca_wq': 'delta_w', 'delta_ca_wk': 'delta_w', 'delta_ca_wv': 'delta_w', 'delta_ca_wo': 'delta_w', 'delta_ffn_norm': 'delta_w', 'delta_ffn_w_gate': 'delta_w', 'delta_ffn_w_up': 'delta_w', 'delta_ffn_w_down': 'delta_w', 'delta_final_norm': 'delta_w', 'new_m_e_norm': 'new_m', 'new_m_e_w_in': 'new_m', 'new_m_e_gmlp_w': 'new_m', 'new_m_e_gmlp_b': 'new_m', 'new_m_e_conv_w': 'new_m', 'new_m_e_conv_b': 'new_m', 'new_m_e_conv_ln_g': 'new_m', 'new_m_e_conv_ln_b': 'new_m', 'new_m_e_w_out': 'new_m', 'new_m_o_norm': 'new_m', 'new_m_o_w_in': 'new_m', 'new_m_o_lam_re': 'new_m', 'new_m_o_lam_im': 'new_m', 'new_m_o_log_dt': 'new_m', 'new_m_o_b_re': 'new_m', 'new_m_o_b_im': 'new_m', 'new_m_o_c_re': 'new_m', 'new_m_o_c_im': 'new_m', 'new_m_o_d': 'new_m', 'new_m_o_w_out': 'new_m', 'new_m_ca_norm': 'new_m', 'new_m_ca_mem_norm': 'new_m', 'new_m_ca_wq': 'new_m', 'new_m_ca_wk': 'new_m', 'new_m_ca_wv': 'new_m', 'new_m_ca_wo': 'new_m', 'new_m_ffn_norm': 'new_m', 'new_m_ffn_w_gate': 'new_m', 'new_m_ffn_w_up': 'new_m', 'new_m_ffn_w_down': 'new_m', 'new_m_final_norm': 'new_m', 'new_v_e_norm': 'new_v', 'new_v_e_w_in': 'new_v', 'new_v_e_gmlp_w': 'new_v', 'new_v_e_gmlp_b': 'new_v', 'new_v_e_conv_w': 'new_v', 'new_v_e_conv_b': 'new_v', 'new_v_e_conv_ln_g': 'new_v', 'new_v_e_conv_ln_b': 'new_v', 'new_v_e_w_out': 'new_v', 'new_v_o_norm': 'new_v', 'new_v_o_w_in': 'new_v', 'new_v_o_lam_re': 'new_v', 'new_v_o_lam_im': 'new_v', 'new_v_o_log_dt': 'new_v', 'new_v_o_b_re': 'new_v', 'new_v_o_b_im': 'new_v', 'new_v_o_c_re': 'new_v', 'new_v_o_c_im': 'new_v', 'new_v_o_d': 'new_v', 'new_v_o_w_out': 'new_v', 'new_v_ca_norm': 'new_v', 'new_v_ca_mem_norm': 'new_v', 'new_v_ca_wq': 'new_v', 'new_v_ca_wk': 'new_v', 'new_v_ca_wv': 'new_v', 'new_v_ca_wo': 'new_v', 'new_v_ffn_norm': 'new_v', 'new_v_ffn_w_gate': 'new_v', 'new_v_ffn_w_up': 'new_v', 'new_v_ffn_w_down': 'new_v', 'new_v_final_norm': 'new_v'}


def _forward(args):
    return _fwd_reference(*[args[k] for k in FWD_PARAMS])


def _output_shape():
    out = _jax.eval_shape(lambda: _forward(_fwd_setup_inputs(0)))
    return out.shape, out.dtype

N_MICROBATCH = 1
ADAM_LR = 0.001
ADAM_B1 = 0.9
ADAM_B2 = 0.999
ADAM_EPS = 1e-08
ADAM_WD = 0.01
ADAM_STEP = 10
PER_EXAMPLE_BATCH_AXIS = {'x': 0, 'mem': 0, 'loss_target': 0}
SHARED_INPUTS = []
_WEIGHT_DTYPES = {'e_norm': _jnp.float32, 'e_w_in': _jnp.float32, 'e_gmlp_w': _jnp.float32, 'e_gmlp_b': _jnp.float32, 'e_conv_w': _jnp.float32, 'e_conv_b': _jnp.float32, 'e_conv_ln_g': _jnp.float32, 'e_conv_ln_b': _jnp.float32, 'e_w_out': _jnp.float32, 'o_norm': _jnp.float32, 'o_w_in': _jnp.float32, 'o_lam_re': _jnp.float32, 'o_lam_im': _jnp.float32, 'o_log_dt': _jnp.float32, 'o_b_re': _jnp.float32, 'o_b_im': _jnp.float32, 'o_c_re': _jnp.float32, 'o_c_im': _jnp.float32, 'o_d': _jnp.float32, 'o_w_out': _jnp.float32, 'ca_norm': _jnp.float32, 'ca_mem_norm': _jnp.float32, 'ca_wq': _jnp.float32, 'ca_wk': _jnp.float32, 'ca_wv': _jnp.float32, 'ca_wo': _jnp.float32, 'ffn_norm': _jnp.float32, 'ffn_w_gate': _jnp.float32, 'ffn_w_up': _jnp.float32, 'ffn_w_down': _jnp.float32, 'final_norm': _jnp.float32}
MOMENT_SCALE = {'e_norm': 1.295298e-01, 'e_w_in': 9.243375e-02, 'e_gmlp_w': 9.918204e-02, 'e_gmlp_b': 1.261979e-01, 'e_conv_w': 1.105199e-01, 'e_conv_b': 2.298932e-01, 'e_conv_ln_g': 1.558036e-01, 'e_conv_ln_b': 1.167821e-01, 'e_w_out': 1.164300e-01, 'o_norm': 5.727645e-02, 'o_w_in': 7.201833e-02, 'o_lam_re': 5.372877e-03, 'o_lam_im': 6.269253e-03, 'o_log_dt': 3.592405e+00, 'o_b_re': 3.084327e-03, 'o_b_im': 3.065679e-03, 'o_c_re': 6.502791e-03, 'o_c_im': 6.004167e-03, 'o_d': 8.259783e-02, 'o_w_out': 3.631188e-02, 'ca_norm': 1.655557e-02, 'ca_mem_norm': 2.499939e-02, 'ca_wq': 1.647390e-02, 'ca_wk': 1.649343e-02, 'ca_wv': 1.680213e-02, 'ca_wo': 1.691794e-02, 'ffn_norm': 1.135507e-01, 'ffn_w_gate': 4.856844e-02, 'ffn_w_up': 4.704323e-02, 'ffn_w_down': 7.788958e-02, 'final_norm': 3.202046e+01}


def _to_microbatches(a, axis):
    t = _jnp.moveaxis(a, axis, 0)
    t = t.reshape((N_MICROBATCH, t.shape[0] // N_MICROBATCH) + t.shape[1:])
    return _jnp.moveaxis(t, 1, axis + 1)


def setup_inputs(seed: int = 0) -> dict:
    inp = _fwd_setup_inputs(seed)
    key = _jax.random.fold_in(_jax.random.key(seed), 7919)
    shape, _ = _output_shape()
    out = dict(inp)
    out["loss_target"] = _jax.random.normal(_jax.random.fold_in(key, 0), shape, _jnp.float32)
    for i, name in enumerate(TWIN_WEIGHTS):
        w = inp[name].astype(_jnp.float32)
        if MOMENT_SCALE is None:
            s = _jnp.sqrt(_jnp.mean(_jnp.square(w)) + 1e-30)
        else:
            s = MOMENT_SCALE[name]
        km, kv = _jax.random.split(_jax.random.fold_in(key, i + 1))
        out[name] = w
        out["m_" + name] = s * _jax.random.normal(km, w.shape, _jnp.float32)
        out["v_" + name] = (s * s) * _jax.random.uniform(kv, w.shape, _jnp.float32, 0.5, 1.5)
    if N_MICROBATCH > 1:
        for name, axis in PER_EXAMPLE_BATCH_AXIS.items():
            out[name] = _to_microbatches(out[name], axis)
    return {'x': out['x'], 'mem': out['mem'], 'e_norm': out['e_norm'], 'e_w_in': out['e_w_in'], 'e_gmlp_w': out['e_gmlp_w'], 'e_gmlp_b': out['e_gmlp_b'], 'e_conv_w': out['e_conv_w'], 'e_conv_b': out['e_conv_b'], 'e_conv_ln_g': out['e_conv_ln_g'], 'e_conv_ln_b': out['e_conv_ln_b'], 'e_w_out': out['e_w_out'], 'o_norm': out['o_norm'], 'o_w_in': out['o_w_in'], 'o_lam_re': out['o_lam_re'], 'o_lam_im': out['o_lam_im'], 'o_log_dt': out['o_log_dt'], 'o_b_re': out['o_b_re'], 'o_b_im': out['o_b_im'], 'o_c_re': out['o_c_re'], 'o_c_im': out['o_c_im'], 'o_d': out['o_d'], 'o_w_out': out['o_w_out'], 'ca_norm': out['ca_norm'], 'ca_mem_norm': out['ca_mem_norm'], 'ca_wq': out['ca_wq'], 'ca_wk': out['ca_wk'], 'ca_wv': out['ca_wv'], 'ca_wo': out['ca_wo'], 'ffn_norm': out['ffn_norm'], 'ffn_w_gate': out['ffn_w_gate'], 'ffn_w_up': out['ffn_w_up'], 'ffn_w_down': out['ffn_w_down'], 'final_norm': out['final_norm'], 'loss_target': out['loss_target'], 'm_e_norm': out['m_e_norm'], 'm_e_w_in': out['m_e_w_in'], 'm_e_gmlp_w': out['m_e_gmlp_w'], 'm_e_gmlp_b': out['m_e_gmlp_b'], 'm_e_conv_w': out['m_e_conv_w'], 'm_e_conv_b': out['m_e_conv_b'], 'm_e_conv_ln_g': out['m_e_conv_ln_g'], 'm_e_conv_ln_b': out['m_e_conv_ln_b'], 'm_e_w_out': out['m_e_w_out'], 'm_o_norm': out['m_o_norm'], 'm_o_w_in': out['m_o_w_in'], 'm_o_lam_re': out['m_o_lam_re'], 'm_o_lam_im': out['m_o_lam_im'], 'm_o_log_dt': out['m_o_log_dt'], 'm_o_b_re': out['m_o_b_re'], 'm_o_b_im': out['m_o_b_im'], 'm_o_c_re': out['m_o_c_re'], 'm_o_c_im': out['m_o_c_im'], 'm_o_d': out['m_o_d'], 'm_o_w_out': out['m_o_w_out'], 'm_ca_norm': out['m_ca_norm'], 'm_ca_mem_norm': out['m_ca_mem_norm'], 'm_ca_wq': out['m_ca_wq'], 'm_ca_wk': out['m_ca_wk'], 'm_ca_wv': out['m_ca_wv'], 'm_ca_wo': out['m_ca_wo'], 'm_ffn_norm': out['m_ffn_norm'], 'm_ffn_w_gate': out['m_ffn_w_gate'], 'm_ffn_w_up': out['m_ffn_w_up'], 'm_ffn_w_down': out['m_ffn_w_down'], 'm_final_norm': out['m_final_norm'], 'v_e_norm': out['v_e_norm'], 'v_e_w_in': out['v_e_w_in'], 'v_e_gmlp_w': out['v_e_gmlp_w'], 'v_e_gmlp_b': out['v_e_gmlp_b'], 'v_e_conv_w': out['v_e_conv_w'], 'v_e_conv_b': out['v_e_conv_b'], 'v_e_conv_ln_g': out['v_e_conv_ln_g'], 'v_e_conv_ln_b': out['v_e_conv_ln_b'], 'v_e_w_out': out['v_e_w_out'], 'v_o_norm': out['v_o_norm'], 'v_o_w_in': out['v_o_w_in'], 'v_o_lam_re': out['v_o_lam_re'], 'v_o_lam_im': out['v_o_lam_im'], 'v_o_log_dt': out['v_o_log_dt'], 'v_o_b_re': out['v_o_b_re'], 'v_o_b_im': out['v_o_b_im'], 'v_o_c_re': out['v_o_c_re'], 'v_o_c_im': out['v_o_c_im'], 'v_o_d': out['v_o_d'], 'v_o_w_out': out['v_o_w_out'], 'v_ca_norm': out['v_ca_norm'], 'v_ca_mem_norm': out['v_ca_mem_norm'], 'v_ca_wq': out['v_ca_wq'], 'v_ca_wk': out['v_ca_wk'], 'v_ca_wv': out['v_ca_wv'], 'v_ca_wo': out['v_ca_wo'], 'v_ffn_norm': out['v_ffn_norm'], 'v_ffn_w_gate': out['v_ffn_w_gate'], 'v_ffn_w_up': out['v_ffn_w_up'], 'v_ffn_w_down': out['v_ffn_w_down'], 'v_final_norm': out['v_final_norm']}


def _loss(weights, diff, rest, loss_target):
    with _jax.named_scope("forward"):
        args = {**rest, TWIN_DIFF_INPUT: diff, **{k: w.astype(_WEIGHT_DTYPES[k]) for k, w in weights.items()}}
        y = _forward(args)
    with _jax.named_scope("loss_head"):
        err = _jnp.square(y.astype(_jnp.float32) - loss_target)
        return 0.5 * _jnp.sum(_jnp.mean(err, axis=-1)) if err.ndim else 0.5 * err


def _adamw(w, g, m, v):
    m = ADAM_B1 * m + (1.0 - ADAM_B1) * g
    v = ADAM_B2 * v + (1.0 - ADAM_B2) * _jnp.square(g)
    m_hat = m / (1.0 - ADAM_B1 ** ADAM_STEP)
    v_hat = v / (1.0 - ADAM_B2 ** ADAM_STEP)
    delta = -ADAM_LR * (m_hat / (_jnp.sqrt(v_hat) + ADAM_EPS) + ADAM_WD * w)
    return delta, m, v


def reference(x, mem, e_norm, e_w_in, e_gmlp_w, e_gmlp_b, e_conv_w, e_conv_b, e_conv_ln_g, e_conv_ln_b, e_w_out, o_norm, o_w_in, o_lam_re, o_lam_im, o_log_dt, o_b_re, o_b_im, o_c_re, o_c_im, o_d, o_w_out, ca_norm, ca_mem_norm, ca_wq, ca_wk, ca_wv, ca_wo, ffn_norm, ffn_w_gate, ffn_w_up, ffn_w_down, final_norm, loss_target, m_e_norm, m_e_w_in, m_e_gmlp_w, m_e_gmlp_b, m_e_conv_w, m_e_conv_b, m_e_conv_ln_g, m_e_conv_ln_b, m_e_w_out, m_o_norm, m_o_w_in, m_o_lam_re, m_o_lam_im, m_o_log_dt, m_o_b_re, m_o_b_im, m_o_c_re, m_o_c_im, m_o_d, m_o_w_out, m_ca_norm, m_ca_mem_norm, m_ca_wq, m_ca_wk, m_ca_wv, m_ca_wo, m_ffn_norm, m_ffn_w_gate, m_ffn_w_up, m_ffn_w_down, m_final_norm, v_e_norm, v_e_w_in, v_e_gmlp_w, v_e_gmlp_b, v_e_conv_w, v_e_conv_b, v_e_conv_ln_g, v_e_conv_ln_b, v_e_w_out, v_o_norm, v_o_w_in, v_o_lam_re, v_o_lam_im, v_o_log_dt, v_o_b_re, v_o_b_im, v_o_c_re, v_o_c_im, v_o_d, v_o_w_out, v_ca_norm, v_ca_mem_norm, v_ca_wq, v_ca_wk, v_ca_wv, v_ca_wo, v_ffn_norm, v_ffn_w_gate, v_ffn_w_up, v_ffn_w_down, v_final_norm):
    given = dict(x=x, mem=mem, e_norm=e_norm, e_w_in=e_w_in, e_gmlp_w=e_gmlp_w, e_gmlp_b=e_gmlp_b, e_conv_w=e_conv_w, e_conv_b=e_conv_b, e_conv_ln_g=e_conv_ln_g, e_conv_ln_b=e_conv_ln_b, e_w_out=e_w_out, o_norm=o_norm, o_w_in=o_w_in, o_lam_re=o_lam_re, o_lam_im=o_lam_im, o_log_dt=o_log_dt, o_b_re=o_b_re, o_b_im=o_b_im, o_c_re=o_c_re, o_c_im=o_c_im, o_d=o_d, o_w_out=o_w_out, ca_norm=ca_norm, ca_mem_norm=ca_mem_norm, ca_wq=ca_wq, ca_wk=ca_wk, ca_wv=ca_wv, ca_wo=ca_wo, ffn_norm=ffn_norm, ffn_w_gate=ffn_w_gate, ffn_w_up=ffn_w_up, ffn_w_down=ffn_w_down, final_norm=final_norm, loss_target=loss_target, m_e_norm=m_e_norm, m_e_w_in=m_e_w_in, m_e_gmlp_w=m_e_gmlp_w, m_e_gmlp_b=m_e_gmlp_b, m_e_conv_w=m_e_conv_w, m_e_conv_b=m_e_conv_b, m_e_conv_ln_g=m_e_conv_ln_g, m_e_conv_ln_b=m_e_conv_ln_b, m_e_w_out=m_e_w_out, m_o_norm=m_o_norm, m_o_w_in=m_o_w_in, m_o_lam_re=m_o_lam_re, m_o_lam_im=m_o_lam_im, m_o_log_dt=m_o_log_dt, m_o_b_re=m_o_b_re, m_o_b_im=m_o_b_im, m_o_c_re=m_o_c_re, m_o_c_im=m_o_c_im, m_o_d=m_o_d, m_o_w_out=m_o_w_out, m_ca_norm=m_ca_norm, m_ca_mem_norm=m_ca_mem_norm, m_ca_wq=m_ca_wq, m_ca_wk=m_ca_wk, m_ca_wv=m_ca_wv, m_ca_wo=m_ca_wo, m_ffn_norm=m_ffn_norm, m_ffn_w_gate=m_ffn_w_gate, m_ffn_w_up=m_ffn_w_up, m_ffn_w_down=m_ffn_w_down, m_final_norm=m_final_norm, v_e_norm=v_e_norm, v_e_w_in=v_e_w_in, v_e_gmlp_w=v_e_gmlp_w, v_e_gmlp_b=v_e_gmlp_b, v_e_conv_w=v_e_conv_w, v_e_conv_b=v_e_conv_b, v_e_conv_ln_g=v_e_conv_ln_g, v_e_conv_ln_b=v_e_conv_ln_b, v_e_w_out=v_e_w_out, v_o_norm=v_o_norm, v_o_w_in=v_o_w_in, v_o_lam_re=v_o_lam_re, v_o_lam_im=v_o_lam_im, v_o_log_dt=v_o_log_dt, v_o_b_re=v_o_b_re, v_o_b_im=v_o_b_im, v_o_c_re=v_o_c_re, v_o_c_im=v_o_c_im, v_o_d=v_o_d, v_o_w_out=v_o_w_out, v_ca_norm=v_ca_norm, v_ca_mem_norm=v_ca_mem_norm, v_ca_wq=v_ca_wq, v_ca_wk=v_ca_wk, v_ca_wv=v_ca_wv, v_ca_wo=v_ca_wo, v_ffn_norm=v_ffn_norm, v_ffn_w_gate=v_ffn_w_gate, v_ffn_w_up=v_ffn_w_up, v_ffn_w_down=v_ffn_w_down, v_final_norm=v_final_norm)
    weights = {n: given[n] for n in TWIN_WEIGHTS}
    shared = {n: given[n] for n in SHARED_INPUTS}
    per_example = {n: given[n] for n in ['x', 'mem']}
    grad_fn = _jax.value_and_grad(_loss, argnums=(0, 1))

    def one_microbatch(ex, loss_target):
        ex = dict(ex)
        diff = ex.pop(TWIN_DIFF_INPUT)
        return grad_fn(weights, diff, {**shared, **ex}, loss_target)

    if N_MICROBATCH == 1:
        loss, (grad_w, grad_x) = one_microbatch(per_example, given["loss_target"])
    else:
        def body(carry, xs):
            loss_sum, grad_sum = carry
            l_k, (gw_k, gx_k) = one_microbatch(xs[0], xs[1])
            with _jax.named_scope("update"):
                return (loss_sum + l_k, _jax.tree.map(_jnp.add, grad_sum, gw_k)), gx_k

        init = (_jnp.zeros((), _jnp.float32), _jax.tree.map(_jnp.zeros_like, weights))
        (loss, grad_w), grad_x = _jax.lax.scan(body, init, (per_example, given["loss_target"]))
    with _jax.named_scope("update"):
        delta_w, new_m, new_v = {}, {}, {}
        for n in TWIN_WEIGHTS:
            delta_w[n], new_m[n], new_v[n] = _adamw(weights[n], grad_w[n], given["m_" + n], given["v_" + n])
    return (loss, grad_x, *[grad_w[n] for n in TWIN_WEIGHTS], *[delta_w[n] for n in TWIN_WEIGHTS],
            *[new_m[n] for n in TWIN_WEIGHTS], *[new_v[n] for n in TWIN_WEIGHTS])
```

```python
import jax
import jax.numpy as jnp
from jax import lax
from jax.experimental import pallas as pl
from jax.experimental.pallas import tpu as pltpu

F32 = jnp.float32
BF16 = jnp.bfloat16
S = jax.ShapeDtypeStruct

D_MODEL = 1024
A_WIDTH = 512
A_GROUPS = 4
GMLP_BLOCK = 128
CHUNK = 64
B_WIDTH = 512
IN_WIDTH = 2 * A_WIDTH + 2 * B_WIDTH
CONV_WIDTH = 31
CONV_PAD = 32
C_WIDTH = 512
C_GROUP_CH = 16
C_GROUPS = 32
C_STATE = 64
N_STATE = C_GROUPS * C_STATE
CA_HEADS = 4
CA_HEAD_DIM = 256
FFN_HIDDEN = 2816
EPS = 1e-6
ADAM_LR = 0.001
ADAM_B1 = 0.9
ADAM_B2 = 0.999
ADAM_EPS = 1e-08
ADAM_WD = 0.01
ADAM_STEP = 10
N_DEV = 8
LANES = 128
VMEM_LIMIT = 56 << 20
MESH = pl.DeviceIdType.MESH
ANY = pl.BlockSpec(memory_space=pl.ANY)


def _cp(*sem):
    return pltpu.CompilerParams(dimension_semantics=sem, vmem_limit_bytes=VMEM_LIMIT)


def _tile(n, pref):
    t = pref
    while n % t:
        t //= 2
    return t


def _bf(v):
    return v if v.dtype == BF16 else v.astype(BF16)


def _sigmoid(x):
    return 1.0 / (1.0 + jnp.exp(-x))


_GC = 0.7978845608028654


def _gelu(x):
    return 0.5 * x * (1.0 + jnp.tanh(_GC * (x + 0.044715 * x * x * x)))


def _gelu_grad(x):
    x2 = x * x
    t = jnp.tanh(_GC * (x + 0.044715 * x * x2))
    return 0.5 * (1.0 + t) + 0.5 * x * (1.0 - t * t) * _GC * (1.0 + 3.0 * 0.044715 * x2)


def _tspec(entry, tm):
    if isinstance(entry, tuple):
        arr, cb, width = entry
        return arr, pl.BlockSpec((tm, width), lambda i, cb=cb: (i, cb))
    return entry, pl.BlockSpec((tm, entry.shape[1]), lambda i: (i, 0))


def rows_call(name, fn, tiled, full, outs, accs, tm=256):
    pairs = [_tspec(e, tm) for e in tiled]
    arrs = [p[0] for p in pairs]
    rows = arrs[0].shape[0]
    tm = _tile(rows, tm)
    pairs = [_tspec(e, tm) for e in tiled]
    n_in = len(tiled) + len(full)
    n_out = len(outs)

    def body(*refs):
        vals = [r[...] for r in refs[:n_in]]
        o_refs = refs[n_in:n_in + n_out]
        a_refs = refs[n_in + n_out:]
        ov, av = fn(*vals)
        for r, v in zip(o_refs, ov):
            r[...] = v.astype(r.dtype)
        if a_refs:
            @pl.when(pl.program_id(0) == 0)
            def _():
                for r in a_refs:
                    r[...] = jnp.zeros(r.shape, r.dtype)
            for r, v in zip(a_refs, av):
                r[...] += v

    in_specs = [p[1] for p in pairs] + [pl.BlockSpec(a.shape, lambda i, nd=a.ndim: (0,) * nd) for a in full]
    out_specs = [pl.BlockSpec((tm, c), lambda i: (i, 0)) for c, _ in outs]
    out_specs += [pl.BlockSpec(s, lambda i, nd=len(s): (0,) * nd) for s in accs]
    out_shape = [S((rows, c), dt) for c, dt in outs] + [S(s, F32) for s in accs]
    return pl.pallas_call(body, grid=(rows // tm,), in_specs=in_specs, out_specs=out_specs, out_shape=out_shape,
                          compiler_params=_cp("arbitrary"), name=name)(*arrs, *full)


def mm_nn(name, m, n, pairs, n_acc, epi, outs, tiled=(), cols=(), rowv=(), tm=512, tn=512):
    tm = _tile(m, tm)
    tn = _tile(n, tn)
    a_arrs, a_specs, b_arrs, b_specs, idx, trans = [], [], [], [], [], []
    for pair in pairs:
        a, b, k = pair[:3]
        bt = len(pair) > 3
        if isinstance(a, tuple):
            arr, cb, kdim = a
            a_specs.append(pl.BlockSpec((tm, kdim), lambda i, j, cb=cb: (i, cb)))
        else:
            arr, kdim = a, a.shape[1]
            a_specs.append(pl.BlockSpec((tm, kdim), lambda i, j: (i, 0)))
        a_arrs.append(arr)
        b_arr, off = b if isinstance(b, tuple) else (b, 0)
        b_arrs.append(b_arr)
        if bt:
            assert off % tn == 0 and b_arr.shape[1] == kdim
            b_specs.append(pl.BlockSpec((tn, kdim), lambda i, j, o=off // tn: (j + o, 0)))
        else:
            b_specs.append(pl.BlockSpec((kdim, tn), lambda i, j, o=off: (o, j)))
        idx.append(k)
        trans.append(bt)
    n_p = len(pairs)
    n_in = 2 * n_p + len(tiled) + len(cols) + len(rowv)

    def body(*refs):
        accs = [None] * n_acc
        for p in range(n_p):
            av, bv = _bf(refs[p][...]), _bf(refs[n_p + p][...])
            if trans[p]:
                d = lax.dot_general(av, bv, (((1,), (1,)), ((), ())), preferred_element_type=F32)
            else:
                d = jnp.dot(av, bv, preferred_element_type=F32)
            accs[idx[p]] = d if accs[idx[p]] is None else accs[idx[p]] + d
        extra = [r[...] for r in refs[2 * n_p:n_in]]
        ov = epi(accs, *extra)
        for r, v in zip(refs[n_in:], ov):
            r[...] = v.astype(r.dtype)

    in_specs = a_specs + b_specs
    in_specs += [pl.BlockSpec((tm, tn), lambda i, j: (i, j)) for _ in tiled]
    in_specs += [pl.BlockSpec((tm, 1), lambda i, j: (i, 0)) for _ in cols]
    in_specs += [pl.BlockSpec((1, tn), lambda i, j: (0, j)) for _ in rowv]
    out_specs = [pl.BlockSpec((tm, tn), lambda i, j: (i, j)) for _ in outs]
    out_shape = [S((m, n), dt) for dt in outs]
    return pl.pallas_call(body, grid=(m // tm, n // tn), in_specs=in_specs, out_specs=out_specs, out_shape=out_shape,
                          compiler_params=_cp("parallel", "parallel"), name=name)(*a_arrs, *b_arrs, *tiled, *cols, *rowv)


def mm_tn(name, a, b, tm=512, tn=512, out_dtype=BF16):
    if isinstance(a, tuple):
        a_arr, a_cb, m = a
    else:
        a_arr, a_cb, m = a, None, a.shape[1]
    if isinstance(b, tuple):
        b_arr, b_cb, n = b
    else:
        b_arr, b_cb, n = b, None, b.shape[1]
    t = a_arr.shape[0]
    tm = _tile(m, tm)
    tn = _tile(n, tn)
    a_off = 0 if a_cb is None else a_cb * (m // tm)
    b_off = 0 if b_cb is None else b_cb * (n // tn)

    def body(a_ref, b_ref, o_ref):
        o_ref[...] = lax.dot_general(_bf(a_ref[...]), _bf(b_ref[...]), (((0,), (0,)), ((), ())),
                                     preferred_element_type=F32).astype(o_ref.dtype)

    return pl.pallas_call(
        body, grid=(m // tm, n // tn),
        in_specs=[pl.BlockSpec((t, tm), lambda i, j: (0, i + a_off)), pl.BlockSpec((t, tn), lambda i, j: (0, j + b_off))],
        out_specs=pl.BlockSpec((tm, tn), lambda i, j: (i, j)), out_shape=S((m, n), out_dtype),
        compiler_params=_cp("parallel", "parallel"), name=name)(a_arr, b_arr)


def rms_fwd(name, x, gain):
    def fn(xv, g):
        r = lax.rsqrt(jnp.mean(xv * xv, axis=-1, keepdims=True) + EPS)
        return [xv * r * g, r], []
    return rows_call(name, fn, [x], [gain], [(x.shape[1], BF16), (1, F32)], [])


def rms_bwd(name, dxn, x, r, gain, dres=None):
    d = x.shape[1]

    def fn(*vals):
        if dres is None:
            dv, xv, rv, g = vals
            base = 0.0
        else:
            dv, xv, rv, base, g = vals
        w = dv * g
        xh = xv * rv
        dx = base + rv * (w - xh * jnp.mean(w * xh, axis=-1, keepdims=True))
        return [dx, dx], [jnp.sum(dv * xh, axis=0, keepdims=True)]

    tiled = [dxn, x, r] + ([] if dres is None else [dres])
    return rows_call(name, fn, tiled, [gain], [(d, F32), (d, BF16)], [(1, d)])


def rms_bwd_gain_only(name, dxn, x, r):
    def fn(dv, xv, rv):
        return [], [jnp.sum(dv * xv * rv, axis=0, keepdims=True)]
    return rows_call(name, fn, [dxn, x, r], [], [], [(1, x.shape[1])])[0]


def final_loss(name, x, gain, target):
    d = x.shape[1]

    def fn(xv, tv, g):
        r = lax.rsqrt(jnp.mean(xv * xv, axis=-1, keepdims=True) + EPS)
        xh = xv * r
        err = xh * g - tv
        dy = err * (1.0 / d)
        w = dy * g
        dx = r * (w - xh * jnp.mean(w * xh, axis=-1, keepdims=True))
        part = jnp.sum(jnp.sum(err * err, axis=-1, keepdims=True), axis=0, keepdims=True) * (0.5 / d)
        return [dx, dx], [jnp.sum(dy * xh, axis=0, keepdims=True), part]

    return rows_call(name, fn, [x, target], [gain], [(d, F32), (d, BF16)], [(1, d), (1, 1)])


def _gmlp_mask():
    row = lax.broadcasted_iota(jnp.int32, (GMLP_BLOCK, GMLP_BLOCK), 0) // CHUNK
    col = lax.broadcasted_iota(jnp.int32, (GMLP_BLOCK, GMLP_BLOCK), 1) // CHUNK
    return col <= row


def _ln_plain(v):
    mu = jnp.mean(v, axis=-1, keepdims=True)
    vc = v - mu
    rstd = lax.rsqrt(jnp.mean(vc * vc, axis=-1, keepdims=True) + EPS)
    return vc * rstd, rstd


def gmlp_fwd(name, proj, w, b, tm=512):
    t = proj.shape[0]
    tm = _tile(t, tm)

    def body(au_ref, av_ref, w_ref, b_ref, o_ref):
        mask = _gmlp_mask()
        u = _gelu(au_ref[...])
        vn, _ = _ln_plain(_gelu(av_ref[...]))
        vnb = _bf(vn)
        for g in range(A_GROUPS):
            wg = _bf(jnp.where(mask, w_ref[g], 0.0))
            cs = slice(g * GMLP_BLOCK, (g + 1) * GMLP_BLOCK)
            for n in range(tm // GMLP_BLOCK):
                rs = slice(n * GMLP_BLOCK, (n + 1) * GMLP_BLOCK)
                sg = jnp.dot(wg, vnb[rs, cs], preferred_element_type=F32) + b_ref[g]
                o_ref[rs, cs] = (u[rs, cs] * sg).astype(o_ref.dtype)

    return pl.pallas_call(
        body, grid=(t // tm,),
        in_specs=[pl.BlockSpec((tm, A_WIDTH), lambda i: (i, 0)), pl.BlockSpec((tm, A_WIDTH), lambda i: (i, 1)),
                  pl.BlockSpec(w.shape, lambda i: (0, 0, 0)), pl.BlockSpec(b.shape, lambda i: (0, 0, 0))],
        out_specs=pl.BlockSpec((tm, A_WIDTH), lambda i: (i, 0)), out_shape=S((t, A_WIDTH), BF16),
        compiler_params=_cp("parallel"), name=name)(proj, proj, w, b)


def gmlp_bwd(name, proj, dcat, w, b, tm=512):
    t = proj.shape[0]
    tm = _tile(t, tm)

    def body(au_ref, av_ref, do_ref, w_ref, b_ref, dp_ref, dw_ref, db_ref):
        @pl.when(pl.program_id(0) == 0)
        def _():
            dw_ref[...] = jnp.zeros(dw_ref.shape, F32)
            db_ref[...] = jnp.zeros(db_ref.shape, F32)

        mask = _gmlp_mask()
        au = au_ref[...]
        av = av_ref[...]
        u = _gelu(au)
        vn, rstd = _ln_plain(_gelu(av))
        vnb = _bf(vn)
        dout = do_ref[...]
        dvn_cols = []
        for g in range(A_GROUPS):
            wm = jnp.where(mask, w_ref[g], 0.0)
            wg = _bf(wm)
            wgt = _bf(wm.T)
            cs = slice(g * GMLP_BLOCK, (g + 1) * GMLP_BLOCK)
            dwg = jnp.zeros((GMLP_BLOCK, GMLP_BLOCK), F32)
            dbg = jnp.zeros((GMLP_BLOCK, 1), F32)
            dvn_rows = []
            for n in range(tm // GMLP_BLOCK):
                rs = slice(n * GMLP_BLOCK, (n + 1) * GMLP_BLOCK)
                sg = jnp.dot(wg, vnb[rs, cs], preferred_element_type=F32) + b_ref[g]
                dp_ref[rs, cs] = (dout[rs, cs] * sg * _gelu_grad(au[rs, cs])).astype(dp_ref.dtype)
                dsg = dout[rs, cs] * u[rs, cs]
                dsgb = _bf(dsg)
                dbg = dbg + jnp.sum(dsg, axis=1, keepdims=True)
                dwg = dwg + lax.dot_general(dsgb, vnb[rs, cs], (((1,), (1,)), ((), ())), preferred_element_type=F32)
                dvn_rows.append(jnp.dot(wgt, dsgb, preferred_element_type=F32))
            dw_ref[g] += jnp.where(mask, dwg, 0.0)
            db_ref[g] += dbg
            dvn_cols.append(jnp.concatenate(dvn_rows, axis=0))
        dvn = jnp.concatenate(dvn_cols, axis=1)
        dv = rstd * (dvn - jnp.mean(dvn, axis=-1, keepdims=True) - vn * jnp.mean(dvn * vn, axis=-1, keepdims=True))
        dp_ref[:, A_WIDTH:] = (dv * _gelu_grad(av)).astype(dp_ref.dtype)

    return pl.pallas_call(
        body, grid=(t // tm,),
        in_specs=[pl.BlockSpec((tm, A_WIDTH), lambda i: (i, 0)), pl.BlockSpec((tm, A_WIDTH), lambda i: (i, 1)),
                  pl.BlockSpec((tm, A_WIDTH), lambda i: (i, 0)),
                  pl.BlockSpec(w.shape, lambda i: (0, 0, 0)), pl.BlockSpec(b.shape, lambda i: (0, 0, 0))],
        out_specs=[pl.BlockSpec((tm, 2 * A_WIDTH), lambda i: (i, 0)),
                   pl.BlockSpec(w.shape, lambda i: (0, 0, 0)), pl.BlockSpec(b.shape, lambda i: (0, 0, 0))],
        out_shape=[S((t, 2 * A_WIDTH), BF16), S(w.shape, F32), S(b.shape, F32)],
        compiler_params=_cp("arbitrary"), name=name)(proj, proj, dcat, w, b)


CONV_ROWS = 256


def conv_fwd(name, proj, w, cb):
    t = proj.shape[0]
    tc = LANES
    rows = _tile(t, CONV_ROWS)
    a_cb, g_cb = 2 * A_WIDTH // tc, (2 * A_WIDTH + B_WIDTH) // tc

    def body(a_ref, g_ref, w_ref, cb_ref, o_ref, hpad):
        hpad[0:CONV_PAD, :] = jnp.zeros((CONV_PAD, tc), F32)

        def fill(i, _):
            r0 = pl.multiple_of(i * rows, rows)
            hpad[pl.ds(CONV_PAD + r0, rows), :] = a_ref[pl.ds(r0, rows), :] * _sigmoid(g_ref[pl.ds(r0, rows), :])
            return 0
        lax.fori_loop(0, t // rows, fill, 0)

        def conv(i, _):
            r0 = pl.multiple_of(i * rows, rows)
            win = hpad[pl.ds(r0, rows + CONV_PAD), :]
            acc = jnp.zeros((rows, tc), F32) + cb_ref[...]
            for k in range(CONV_WIDTH):
                sh = CONV_WIDTH - 1 - k
                src = win if sh == 0 else pltpu.roll(win, sh, 0)
                acc = acc + src[CONV_PAD:, :] * w_ref[k:k + 1, :]
            o_ref[pl.ds(r0, rows), :] = acc
            return 0
        lax.fori_loop(0, t // rows, conv, 0)

    return pl.pallas_call(
        body, grid=(B_WIDTH // tc,),
        in_specs=[pl.BlockSpec((t, tc), lambda j: (0, a_cb + j)), pl.BlockSpec((t, tc), lambda j: (0, g_cb + j)),
                  pl.BlockSpec((CONV_WIDTH, tc), lambda j: (0, j)), pl.BlockSpec((1, tc), lambda j: (0, j))],
        out_specs=pl.BlockSpec((t, tc), lambda j: (0, j)), out_shape=S((t, B_WIDTH), F32),
        scratch_shapes=[pltpu.VMEM((t + CONV_PAD, tc), F32)],
        compiler_params=_cp("parallel"), name=name)(proj, proj, w, cb)


def conv_bwd(name, proj, dhc, w):
    t = proj.shape[0]
    tc = LANES
    rows = _tile(t, CONV_ROWS)
    a_cb, g_cb = 2 * A_WIDTH // tc, (2 * A_WIDTH + B_WIDTH) // tc
    win_rows = rows + CONV_PAD

    def body(a_ref, g_ref, d_ref, w_ref, da_ref, dg_ref, dw_ref, dcb_ref, hpad, dpad, dwacc):
        hpad[0:CONV_PAD, :] = jnp.zeros((CONV_PAD, tc), F32)
        dpad[t:t + CONV_PAD, :] = jnp.zeros((CONV_PAD, tc), F32)
        dwacc[...] = jnp.zeros(dwacc.shape, F32)

        def fill(i, _):
            r0 = pl.multiple_of(i * rows, rows)
            hpad[pl.ds(CONV_PAD + r0, rows), :] = a_ref[pl.ds(r0, rows), :] * _sigmoid(g_ref[pl.ds(r0, rows), :])
            dpad[pl.ds(r0, rows), :] = d_ref[pl.ds(r0, rows), :]
            return 0
        lax.fori_loop(0, t // rows, fill, 0)

        def step(i, dcb):
            r0 = pl.multiple_of(i * rows, rows)
            hwin = hpad[pl.ds(r0, win_rows), :]
            dwin = dpad[pl.ds(r0, win_rows), :]
            dchunk = dwin[:rows, :]
            dh = jnp.zeros((rows, tc), F32)
            for k in range(CONV_WIDTH):
                sh = CONV_WIDTH - 1 - k
                hsrc = hwin if sh == 0 else pltpu.roll(hwin, sh, 0)
                dsrc = dwin if sh == 0 else pltpu.roll(dwin, win_rows - sh, 0)
                dh = dh + dsrc[:rows, :] * w_ref[k:k + 1, :]
                prod = dchunk * hsrc[CONV_PAD:, :]
                dwacc[k] += jnp.sum(prod.reshape(rows // 8, 8, tc), axis=0)
            a = a_ref[pl.ds(r0, rows), :]
            sg = _sigmoid(g_ref[pl.ds(r0, rows), :])
            da_ref[pl.ds(r0, rows), :] = (dh * sg).astype(da_ref.dtype)
            dg_ref[pl.ds(r0, rows), :] = (dh * a * sg * (1.0 - sg)).astype(dg_ref.dtype)
            return dcb + jnp.sum(dchunk, axis=0, keepdims=True)
        dcb = lax.fori_loop(0, t // rows, step, jnp.zeros((1, tc), F32))
        dcb_ref[...] = dcb
        for k in range(CONV_WIDTH):
            dw_ref[k:k + 1, :] = jnp.sum(dwacc[k], axis=0, keepdims=True)

    return pl.pallas_call(
        body, grid=(B_WIDTH // tc,),
        in_specs=[pl.BlockSpec((t, tc), lambda j: (0, a_cb + j)), pl.BlockSpec((t, tc), lambda j: (0, g_cb + j)),
                  pl.BlockSpec((t, tc), lambda j: (0, j)), pl.BlockSpec((CONV_WIDTH, tc), lambda j: (0, j))],
        out_specs=[pl.BlockSpec((t, tc), lambda j: (0, j)), pl.BlockSpec((t, tc), lambda j: (0, j)),
                   pl.BlockSpec((CONV_WIDTH, tc), lambda j: (0, j)), pl.BlockSpec((1, tc), lambda j: (0, j))],
        out_shape=[S((t, B_WIDTH), BF16), S((t, B_WIDTH), BF16), S((CONV_WIDTH, B_WIDTH), F32), S((1, B_WIDTH), F32)],
        scratch_shapes=[pltpu.VMEM((t + CONV_PAD, tc), F32), pltpu.VMEM((t + CONV_PAD, tc), F32),
                        pltpu.VMEM((CONV_WIDTH, 8, tc), F32)],
        compiler_params=_cp("parallel"), name=name)(proj, proj, dhc, w)


def ln_silu_fwd(name, hc, g, b):
    def fn(h, gv, bv):
        y, _ = _ln_plain(h)
        z = y * gv + bv
        return [z * _sigmoid(z)], []
    return rows_call(name, fn, [hc], [g, b], [(hc.shape[1], BF16)], [])[0]


def ln_silu_bwd(name, hc, dcat, g, b):
    c = hc.shape[1]

    def fn(h, dout, gv, bv):
        y, rstd = _ln_plain(h)
        z = y * gv + bv
        s = _sigmoid(z)
        dz = dout * s * (1.0 + z * (1.0 - s))
        dyv = dz * gv
        dh = rstd * (dyv - jnp.mean(dyv, axis=-1, keepdims=True) - y * jnp.mean(dyv * y, axis=-1, keepdims=True))
        return [dh], [jnp.sum(dz * y, axis=0, keepdims=True), jnp.sum(dz, axis=0, keepdims=True)]

    return rows_call(name, fn, [hc, (dcat, 1, c)], [g, b], [(c, F32)], [(1, c), (1, c)])


_NT = (((1,), (1,)), ((), ()))
_TN = (((0,), (0,)), ((), ()))


def attn_fwd(name, q, k, v, tm=512):
    t, d = q.shape
    m = k.shape[0]
    tm = _tile(t, tm)
    scale = CA_HEAD_DIM ** -0.5

    def body(q_ref, k_ref, v_ref, o_ref):
        for h in range(CA_HEADS):
            cs = slice(h * CA_HEAD_DIM, (h + 1) * CA_HEAD_DIM)
            s = lax.dot_general(q_ref[:, cs], k_ref[:, cs], _NT, preferred_element_type=F32) * scale
            e = jnp.exp(s - jnp.max(s, axis=-1, keepdims=True))
            p = e / jnp.sum(e, axis=-1, keepdims=True)
            o_ref[:, cs] = jnp.dot(_bf(p), v_ref[:, cs], preferred_element_type=F32).astype(o_ref.dtype)

    return pl.pallas_call(
        body, grid=(t // tm,),
        in_specs=[pl.BlockSpec((tm, d), lambda i: (i, 0)), pl.BlockSpec((m, d), lambda i: (0, 0)),
                  pl.BlockSpec((m, d), lambda i: (0, 0))],
        out_specs=pl.BlockSpec((tm, d), lambda i: (i, 0)), out_shape=S((t, d), BF16),
        compiler_params=_cp("parallel"), name=name)(q, k, v)


def attn_bwd(name, q, k, v, do, tm=512):
    t, d = q.shape
    m = k.shape[0]
    tm = _tile(t, tm)
    scale = CA_HEAD_DIM ** -0.5

    def body(q_ref, k_ref, v_ref, do_ref, dq_ref, dk_ref, dv_ref):
        @pl.when(pl.program_id(0) == 0)
        def _():
            dk_ref[...] = jnp.zeros(dk_ref.shape, F32)
            dv_ref[...] = jnp.zeros(dv_ref.shape, F32)

        for h in range(CA_HEADS):
            cs = slice(h * CA_HEAD_DIM, (h + 1) * CA_HEAD_DIM)
            qh, kh, vh, doh = q_ref[:, cs], k_ref[:, cs], v_ref[:, cs], do_ref[:, cs]
            s = lax.dot_general(qh, kh, _NT, preferred_element_type=F32) * scale
            e = jnp.exp(s - jnp.max(s, axis=-1, keepdims=True))
            p = e / jnp.sum(e, axis=-1, keepdims=True)
            pb = _bf(p)
            dv_ref[:, cs] += lax.dot_general(pb, doh, _TN, preferred_element_type=F32)
            dp = lax.dot_general(doh, vh, _NT, preferred_element_type=F32)
            ds = _bf(p * (dp - jnp.sum(dp * p, axis=-1, keepdims=True)) * scale)
            dq_ref[:, cs] = jnp.dot(ds, kh, preferred_element_type=F32).astype(dq_ref.dtype)
            dk_ref[:, cs] += lax.dot_general(ds, qh, _TN, preferred_element_type=F32)

    return pl.pallas_call(
        body, grid=(t // tm,),
        in_specs=[pl.BlockSpec((tm, d), lambda i: (i, 0)), pl.BlockSpec((m, d), lambda i: (0, 0)),
                  pl.BlockSpec((m, d), lambda i: (0, 0)), pl.BlockSpec((tm, d), lambda i: (i, 0))],
        out_specs=[pl.BlockSpec((tm, d), lambda i: (i, 0)), pl.BlockSpec((m, d), lambda i: (0, 0)),
                   pl.BlockSpec((m, d), lambda i: (0, 0))],
        out_shape=[S((t, d), BF16), S((m, d), F32), S((m, d), F32)],
        compiler_params=_cp("arbitrary"), name=name)(q, k, v, do)


SUB = 8
S5_ROWS = 256


def s5_constants(lam_re, lam_im, log_dt, b_re, b_im, c_re, c_im):
    dt = jnp.exp(log_dt)[:, None]
    mag = jnp.exp(lam_re * dt)
    ar = mag * jnp.cos(lam_im * dt)
    ai = mag * jnp.sin(lam_im * dt)
    den = lam_re * lam_re + lam_im * lam_im
    qr = ((ar - 1.0) * lam_re + ai * lam_im) / den
    qi = (ai * lam_re - (ar - 1.0) * lam_im) / den
    bbr = qr[..., None] * b_re - qi[..., None] * b_im
    bbi = qr[..., None] * b_im + qi[..., None] * b_re
    eye = jnp.eye(C_GROUPS, dtype=F32)

    def in_mat(bb):
        return (bb.transpose(0, 2, 1)[:, :, None, :] * eye[:, None, :, None]).reshape(C_WIDTH, N_STATE)

    def out_mat(cc):
        return (cc.transpose(0, 2, 1)[:, :, None, :] * eye[:, None, :, None]).reshape(N_STATE, C_WIDTH)

    mb = jnp.concatenate([in_mat(bbr), in_mat(bbi)], axis=1)
    mc = jnp.concatenate([out_mat(c_re), -out_mat(c_im)], axis=0)
    a = jnp.stack([ar.reshape(N_STATE), ai.reshape(N_STATE)])
    return a, mb, mc


def _scan_powers(a, conj):
    ar, ai = a[0], (-a[1] if conj else a[1])
    pows = [(ar, ai)]
    for _ in range(SUB - 1):
        pr, pi = pows[-1]
        pows.append((pr * ar - pi * ai, pr * ai + pi * ar))
    rows = jnp.arange(SUB)[:, None]
    out = []
    for s in (1, 2, 4):
        keep = (rows + s <= SUB - 1) if conj else (rows >= s)
        out.append(jnp.stack([jnp.where(keep, pows[s - 1][0][None, :], 0.0), jnp.where(keep, pows[s - 1][1][None, :], 0.0)]))
    order = [SUB - 1 - i for i in range(SUB)] if conj else list(range(SUB))
    out.append(jnp.stack([jnp.stack([pows[i][0] for i in order]), jnp.stack([pows[i][1] for i in order])]))
    return jnp.stack(out)


def _cmul_add(xr, xi, pr, pi, zr, zi):
    return xr + pr * zr - pi * zi, xi + pr * zi + pi * zr


def s5_fwd(name, u, mb, mc, pw, dskip):
    t = u.shape[0]
    tm = _tile(t, S5_ROWS)
    ns = N_STATE

    def body(u_ref, mb_ref, mc_ref, pw_ref, d_ref, gy_ref, y_ref, xs_ref, xb_ref, carry):
        @pl.when(pl.program_id(0) == 0)
        def _():
            carry[...] = jnp.zeros(carry.shape, F32)

        uv = u_ref[...]
        xs_ref[...] = jnp.dot(_bf(uv), mb_ref[...], preferred_element_type=F32)

        def group(i, _):
            r0 = pl.multiple_of(i * SUB, SUB)
            xr = xs_ref[pl.ds(r0, SUB), 0:ns]
            xi = xs_ref[pl.ds(r0, SUB), ns:2 * ns]
            for k, s in enumerate((1, 2, 4)):
                xr, xi = _cmul_add(xr, xi, pw_ref[k, 0], pw_ref[k, 1], pltpu.roll(xr, s, 0), pltpu.roll(xi, s, 0))
            xr, xi = _cmul_add(xr, xi, pw_ref[3, 0], pw_ref[3, 1], carry[0], carry[1])
            xs_ref[pl.ds(r0, SUB), 0:ns] = xr
            xs_ref[pl.ds(r0, SUB), ns:2 * ns] = xi
            carry[0] = jnp.broadcast_to(xr[SUB - 1:SUB, :], (SUB, ns))
            carry[1] = jnp.broadcast_to(xi[SUB - 1:SUB, :], (SUB, ns))
            return 0
        lax.fori_loop(0, tm // SUB, group, 0)

        xb = _bf(xs_ref[...])
        xb_ref[...] = xb
        y = jnp.dot(xb, mc_ref[...], preferred_element_type=F32) + d_ref[...] * uv
        y_ref[...] = y
        gy_ref[...] = _gelu(y).astype(gy_ref.dtype)

    c = u.shape[1]
    return pl.pallas_call(
        body, grid=(t // tm,),
        in_specs=[pl.BlockSpec((tm, c), lambda i: (i, 0)), pl.BlockSpec(mb.shape, lambda i: (0, 0)),
                  pl.BlockSpec(mc.shape, lambda i: (0, 0)), pl.BlockSpec(pw.shape, lambda i: (0, 0, 0, 0)),
                  pl.BlockSpec((1, c), lambda i: (0, 0))],
        out_specs=[pl.BlockSpec((tm, c), lambda i: (i, 0)), pl.BlockSpec((tm, c), lambda i: (i, 0)),
                   pl.BlockSpec((tm, 2 * ns), lambda i: (i, 0)), pl.BlockSpec((tm, 2 * ns), lambda i: (i, 0))],
        out_shape=[S((t, c), BF16), S((t, c), F32), S((t, 2 * ns), F32), S((t, 2 * ns), BF16)],
        scratch_shapes=[pltpu.VMEM((2, SUB, ns), F32)],
        compiler_params=_cp("arbitrary"), name=name)(u, mb, mc, pw, dskip)


def s5_bwd(name, dgy, y, u, xs, mct, mbt, qw, dskip):
    t, c = u.shape
    tm = _tile(t, S5_ROWS)
    nt = t // tm
    ns = N_STATE
    ng = tm // SUB

    def body(dgy_ref, y_ref, u_ref, xs_ref, prev_ref, mct_ref, mbt_ref, qw_ref, d_ref,
             du_ref, dy_ref, lb_ref, da_ref, dd_ref, lam, carry):
        step = pl.program_id(0)

        @pl.when(step == 0)
        def _():
            carry[...] = jnp.zeros(carry.shape, F32)
            da_ref[...] = jnp.zeros(da_ref.shape, F32)
            dd_ref[...] = jnp.zeros(dd_ref.shape, F32)

        uv = u_ref[...]
        dy = dgy_ref[...] * _gelu_grad(y_ref[...])
        dyb = _bf(dy)
        dy_ref[...] = dyb
        dd_ref[...] += jnp.sum(dy * uv, axis=0, keepdims=True)
        lam[...] = jnp.dot(dyb, mct_ref[...], preferred_element_type=F32)
        first_tile = (step == nt - 1).astype(F32)
        row0 = lax.broadcasted_iota(jnp.int32, (SUB, ns), 0) == 0

        def group(j, _):
            i = ng - 1 - j
            r0 = pl.multiple_of(i * SUB, SUB)
            lr = lam[pl.ds(r0, SUB), 0:ns]
            li = lam[pl.ds(r0, SUB), ns:2 * ns]
            for k, s in enumerate((1, 2, 4)):
                lr, li = _cmul_add(lr, li, qw_ref[k, 0], qw_ref[k, 1],
                                   pltpu.roll(lr, SUB - s, 0), pltpu.roll(li, SUB - s, 0))
            lr, li = _cmul_add(lr, li, qw_ref[3, 0], qw_ref[3, 1], carry[0], carry[1])
            lam[pl.ds(r0, SUB), 0:ns] = lr
            lam[pl.ds(r0, SUB), ns:2 * ns] = li
            carry[0] = jnp.broadcast_to(lr[0:1, :], (SUB, ns))
            carry[1] = jnp.broadcast_to(li[0:1, :], (SUB, ns))
            rp = pl.multiple_of(jnp.maximum(i - 1, 0) * SUB, SUB)
            in_tile = (i > 0).astype(F32)
            out_tile = (1.0 - in_tile) * (1.0 - first_tile)
            pr = xs_ref[pl.ds(rp, SUB), 0:ns] * in_tile + prev_ref[:, 0:ns] * out_tile
            pi = xs_ref[pl.ds(rp, SUB), ns:2 * ns] * in_tile + prev_ref[:, ns:2 * ns] * out_tile
            xpr = jnp.where(row0, pltpu.roll(pr, 1, 0), pltpu.roll(xs_ref[pl.ds(r0, SUB), 0:ns], 1, 0))
            xpi = jnp.where(row0, pltpu.roll(pi, 1, 0), pltpu.roll(xs_ref[pl.ds(r0, SUB), ns:2 * ns], 1, 0))
            da_ref[0] += lr * xpr + li * xpi
            da_ref[1] += li * xpr - lr * xpi
            return 0
        lax.fori_loop(0, ng, group, 0)

        lb = _bf(lam[...])
        lb_ref[...] = lb
        du_ref[...] = (jnp.dot(lb, mbt_ref[...], preferred_element_type=F32) + d_ref[...] * dy).astype(du_ref.dtype)

    rev = lambda i: (nt - 1 - i, 0)
    prev = lambda i: (jnp.maximum((nt - 1 - i) * (tm // SUB) - 1, 0), 0)
    return pl.pallas_call(
        body, grid=(nt,),
        in_specs=[pl.BlockSpec((tm, c), rev), pl.BlockSpec((tm, c), rev), pl.BlockSpec((tm, c), rev),
                  pl.BlockSpec((tm, 2 * ns), rev), pl.BlockSpec((SUB, 2 * ns), prev),
                  pl.BlockSpec(mct.shape, lambda i: (0, 0)), pl.BlockSpec(mbt.shape, lambda i: (0, 0)),
                  pl.BlockSpec(qw.shape, lambda i: (0, 0, 0, 0)), pl.BlockSpec((1, c), lambda i: (0, 0))],
        out_specs=[pl.BlockSpec((tm, c), rev), pl.BlockSpec((tm, c), rev), pl.BlockSpec((tm, 2 * ns), rev),
                   pl.BlockSpec((2, SUB, ns), lambda i: (0, 0, 0)), pl.BlockSpec((1, c), lambda i: (0, 0))],
        out_shape=[S((t, c), BF16), S((t, c), BF16), S((t, 2 * ns), BF16), S((2, SUB, ns), F32), S((1, c), F32)],
        scratch_shapes=[pltpu.VMEM((tm, 2 * ns), F32), pltpu.VMEM((2, SUB, ns), F32)],
        compiler_params=_cp("arbitrary"), name=name)(dgy, y, u, xs, xs, mct, mbt, qw, dskip)


def _first(accs, *_):
    return [accs[0]]


def _add_res(accs, res):
    return [accs[0] + res]


def even_fwd(x, w):
    t = x.shape[0]
    hn, r = rms_fwd("e_norm_f", x, w["e_norm"])
    (proj,) = mm_nn("e_in_f", t, IN_WIDTH, [(hn, w["e_w_in_t"], 0, "t")], 1, _first, [F32])
    out_a = gmlp_fwd("e_gmlp_f", proj, w["e_gmlp_w"], w["e_gmlp_b"])
    hc = conv_fwd("e_conv_f", proj, w["e_conv_w"], w["e_conv_b"])
    out_b = ln_silu_fwd("e_ln_f", hc, w["e_conv_ln_g"], w["e_conv_ln_b"])
    (x1,) = mm_nn("e_out_f", t, D_MODEL, [(out_a, (w["e_w_out"], 0), 0), (out_b, (w["e_w_out"], 1), 0)],
                  1, _add_res, [F32], tiled=[x])
    return x1, (x, hn, r, proj, out_a, hc, out_b)


def even_bwd(dx, dxb, saved, w):
    x, hn, r, proj, out_a, hc, out_b = saved
    t = x.shape[0]
    (dcat,) = mm_nn("e_out_b", t, D_MODEL, [(dxb, w["e_w_out"], 0, "t")], 1, _first, [F32])
    g_w_out = jnp.concatenate([mm_tn("e_out_wa", out_a, dxb), mm_tn("e_out_wb", out_b, dxb)], axis=0)
    dab, g_gw, g_gb = gmlp_bwd("e_gmlp_b", proj, dcat, w["e_gmlp_w"], w["e_gmlp_b"])
    dhc, g_lg, g_lb = ln_silu_bwd("e_ln_b", hc, dcat, w["e_conv_ln_g"], w["e_conv_ln_b"])
    dba, dbg, g_cw, g_cb = conv_bwd("e_conv_b", proj, dhc, w["e_conv_w"])
    w_in_t = w["e_w_in_t"]
    (dhn,) = mm_nn("e_in_b", t, D_MODEL, [(dab, (w_in_t, 0), 0), (dba, (w_in_t, 2), 0), (dbg, (w_in_t, 3), 0)],
                   1, _first, [F32])
    g_w_in_t = jnp.concatenate([mm_tn("e_in_w0", dab, hn), mm_tn("e_in_w1", dba, hn), mm_tn("e_in_w2", dbg, hn)], axis=0)
    dx0, dx0b, g_norm = rms_bwd("e_norm_b", dhn, x, r, w["e_norm"], dres=dx)
    grads = dict(e_norm=g_norm, e_w_in_t=g_w_in_t, e_gmlp_w=g_gw[None], e_gmlp_b=g_gb.reshape(1, A_GROUPS, GMLP_BLOCK),
                 e_conv_w=g_cw[None], e_conv_b=g_cb, e_conv_ln_g=g_lg, e_conv_ln_b=g_lb, e_w_out=g_w_out)
    return dx0, dx0b, grads


def odd_fwd(x, w, consts):
    t = x.shape[0]
    _, mb, mc, pw, _ = consts
    hn, r = rms_fwd("o_norm_f", x, w["o_norm"])
    (u,) = mm_nn("o_in_f", t, C_WIDTH, [(hn, w["o_w_in"], 0)], 1, _first, [F32])
    gy, y, xs, xsb = s5_fwd("o_s5_f", u, _bf(mb), _bf(mc), pw, w["o_d"])
    w_out_t = w["o_w_out_t"]

    def epi(accs, res):
        return [res + accs[0] * _sigmoid(accs[1]), accs[0], accs[1]]

    x1, o1, o2 = mm_nn("o_out_f", t, D_MODEL, [(gy, (w_out_t, 0), 0, "t"), (gy, (w_out_t, D_MODEL), 1, "t")], 2, epi,
                       [F32, BF16, BF16], tiled=[x])
    return x1, (x, hn, r, u, gy, y, xs, xsb, o1, o2)


def odd_bwd(dx, dxb, saved, w, consts, consts_vjp):
    x, hn, r, u, gy, y, xs, xsb, o1, o2 = saved
    t = x.shape[0]
    _, mb, mc, _, qw = consts

    def gate_bwd(dv, a, b):
        a = a.astype(F32)
        sg = _sigmoid(b.astype(F32))
        return [jnp.concatenate([dv * sg, dv * a * sg * (1.0 - sg)], axis=1)], []

    (do12,) = rows_call("o_gate_b", gate_bwd, [dx, o1, o2], [], [(2 * D_MODEL, BF16)], [])
    (dgy,) = mm_nn("o_out_b", t, C_WIDTH, [(do12, w["o_w_out_t"], 0)], 1, _first, [F32])
    g_w_out_t = mm_tn("o_out_w", do12, gy)
    du, dyb, lamb, da8, g_d = s5_bwd("o_s5_b", dgy, y, u, xs, _bf(mc.T), _bf(mb.T), qw, w["o_d"])
    d_mb = mm_tn("o_s5_wb", u, lamb, out_dtype=F32)
    d_mc = mm_tn("o_s5_wc", xsb, dyb, out_dtype=F32)
    g_lr, g_li, g_dt, g_br, g_bi, g_cr, g_ci = consts_vjp((jnp.sum(da8, axis=1), d_mb, d_mc))
    g_w_in = mm_tn("o_in_w", hn, du)
    (dhn,) = mm_nn("o_in_b", t, D_MODEL, [(du, w["o_w_in"], 0, "t")], 1, _first, [F32])
    dx0, dx0b, g_norm = rms_bwd("o_norm_b", dhn, x, r, w["o_norm"], dres=dx)
    grads = dict(o_norm=g_norm, o_w_in=g_w_in, o_lam_re=g_lr[None], o_lam_im=g_li[None], o_log_dt=g_dt[None],
                 o_b_re=g_br[None], o_b_im=g_bi[None], o_c_re=g_cr[None], o_c_im=g_ci[None], o_d=g_d, o_w_out_t=g_w_out_t)
    return dx0, dx0b, grads


def ca_fwd(i, x, mem, w):
    t, m = x.shape[0], mem.shape[0]
    xn, r = rms_fwd(f"ca{i}_norm_f", x, w["ca_norm"][i:i + 1])
    mn, rm = rms_fwd(f"ca{i}_mnorm_f", mem, w["ca_mem_norm"][i:i + 1])
    (q,) = mm_nn(f"ca{i}_q_f", t, D_MODEL, [(xn, w["ca_wq"][i], 0)], 1, _first, [BF16])
    k, v = mm_nn(f"ca{i}_kv_f", m, D_MODEL, [(mn, w["ca_wk"][i], 0), (mn, w["ca_wv"][i], 1)], 2,
                 lambda accs: [accs[0], accs[1]], [BF16, BF16])
    o = attn_fwd(f"ca{i}_attn_f", q, k, v)
    (x1,) = mm_nn(f"ca{i}_o_f", t, D_MODEL, [(o, w["ca_wo"][i], 0)], 1, _add_res, [F32], tiled=[x])
    return x1, (x, xn, r, mn, rm, q, k, v, o)


def ca_bwd(i, dx, dxb, saved, mem, w):
    x, xn, r, mn, rm, q, k, v, o = saved
    t, m = x.shape[0], mem.shape[0]
    (do,) = mm_nn(f"ca{i}_o_b", t, D_MODEL, [(dxb, w["ca_wo"][i], 0, "t")], 1, _first, [BF16])
    g_wo = mm_tn(f"ca{i}_o_w", o, dxb)
    dq, dk, dv = attn_bwd(f"ca{i}_attn_b", q, k, v, do)
    g_wq = mm_tn(f"ca{i}_q_w", xn, dq)
    g_wk = mm_tn(f"ca{i}_k_w", mn, dk)
    g_wv = mm_tn(f"ca{i}_v_w", mn, dv)
    (dxn,) = mm_nn(f"ca{i}_q_b", t, D_MODEL, [(dq, w["ca_wq"][i], 0, "t")], 1, _first, [F32])
    (dmn,) = mm_nn(f"ca{i}_kv_b", m, D_MODEL, [(dk, w["ca_wk"][i], 0, "t"), (dv, w["ca_wv"][i], 0, "t")], 1, _first, [F32])
    g_mnorm = rms_bwd_gain_only(f"ca{i}_mnorm_b", dmn, mem, rm)
    dx0, dx0b, g_norm = rms_bwd(f"ca{i}_norm_b", dxn, x, r, w["ca_norm"][i:i + 1], dres=dx)
    return dx0, dx0b, dict(ca_norm=g_norm, ca_mem_norm=g_mnorm, ca_wq=g_wq, ca_wk=g_wk, ca_wv=g_wv, ca_wo=g_wo)


def ffn_fwd(i, x, w):
    t = x.shape[0]
    xn, r = rms_fwd(f"ffn{i}_norm_f", x, w["ffn_norm"][i:i + 1])

    def epi(accs):
        g, u = accs
        return [g, u, g * _sigmoid(g) * u]

    g, u, h = mm_nn(f"ffn{i}_up_f", t, FFN_HIDDEN, [(xn, w["ffn_w_gate_t"][i], 0, "t"), (xn, w["ffn_w_up_t"][i], 1, "t")],
                    2, epi, [BF16, BF16, BF16])
    (x1,) = mm_nn(f"ffn{i}_down_f", t, D_MODEL, [(h, w["ffn_w_down"][i], 0)], 1, _add_res, [F32], tiled=[x])
    return x1, (x, xn, r, g, u, h)


def ffn_bwd(i, dx, dxb, saved, w):
    x, xn, r, g, u, h = saved
    t = x.shape[0]

    def epi(accs, gv, uv):
        dh = accs[0]
        gv = gv.astype(F32)
        uv = uv.astype(F32)
        s = _sigmoid(gv)
        return [dh * uv * s * (1.0 + gv * (1.0 - s)), dh * gv * s]

    dg, du = mm_nn(f"ffn{i}_down_b", t, FFN_HIDDEN, [(dxb, w["ffn_w_down"][i], 0, "t")], 1, epi, [BF16, BF16], tiled=[g, u])
    g_wd = mm_tn(f"ffn{i}_down_w", h, dxb)
    g_wg_t = mm_tn(f"ffn{i}_gate_w", dg, xn)
    g_wu_t = mm_tn(f"ffn{i}_up_w", du, xn)
    (dxn,) = mm_nn(f"ffn{i}_up_b", t, D_MODEL, [(dg, w["ffn_w_gate_t"][i], 0), (du, w["ffn_w_up_t"][i], 0)], 1, _first, [F32])
    dx0, dx0b, g_norm = rms_bwd(f"ffn{i}_norm_b", dxn, x, r, w["ffn_norm"][i:i + 1], dres=dx)
    return dx0, dx0b, dict(ffn_norm=g_norm, ffn_w_gate_t=g_wg_t, ffn_w_up_t=g_wu_t, ffn_w_down=g_wd)


_S5_PARAMS = ("o_lam_re", "o_lam_im", "o_log_dt", "o_b_re", "o_b_im", "o_c_re", "o_c_im")


def local_step(x, mem, target, w):
    def consts_fn(*p):
        a, mb, mc = s5_constants(*p)
        return a, mb, mc

    (a, mb, mc), consts_vjp = jax.vjp(consts_fn, *[w[k] for k in _S5_PARAMS])
    consts = (a, mb, mc, _scan_powers(a, False), _scan_powers(a, True))

    x1, s_e = even_fwd(x, w)
    x2, s_c0 = ca_fwd(0, x1, mem, w)
    x3, s_f0 = ffn_fwd(0, x2, w)
    x4, s_o = odd_fwd(x3, w, consts)
    x5, s_c1 = ca_fwd(1, x4, mem, w)
    x6, s_f1 = ffn_fwd(1, x5, w)
    dx, dxb, g_final, loss = final_loss("final_loss", x6, w["final_norm"], target)

    dx, dxb, g_f1 = ffn_bwd(1, dx, dxb, s_f1, w)
    dx, dxb, g_c1 = ca_bwd(1, dx, dxb, s_c1, mem, w)
    dx, dxb, g_o = odd_bwd(dx, dxb, s_o, w, consts, consts_vjp)
    dx, dxb, g_f0 = ffn_bwd(0, dx, dxb, s_f0, w)
    dx, dxb, g_c0 = ca_bwd(0, dx, dxb, s_c0, mem, w)
    dx, dxb, g_e = even_bwd(dx, dxb, s_e, w)

    grads = dict(g_e)
    grads.update(g_o)
    for g0, g1 in ((g_c0, g_c1), (g_f0, g_f1)):
        for k in g0:
            grads[k] = jnp.concatenate([g0[k], g1[k]], axis=0) if k.endswith("norm") else (g0[k], g1[k])
    grads["final_norm"] = g_final
    return loss, dx, grads


def _group(axes):
    pos = {a: lax.axis_index(a) for a in ("x", "y", "c")}
    me = 0
    for a in axes:
        me = me * 2 + pos[a]
    peers = []
    for mask in range(1, 2 ** len(axes)):
        peer = dict(pos)
        for bit, a in enumerate(axes):
            if (mask >> (len(axes) - 1 - bit)) & 1:
                peer[a] = 1 - pos[a]
        idx = 0
        for a in axes:
            idx = idx * 2 + peer[a]
        peers.append((idx, (peer["x"], peer["y"], peer["c"])))
    return me, peers


def gather_stage(name, blks, axes, interleave):
    n = 2 ** len(axes)
    k_ops = len(blks)
    if interleave:
        out_shape = [S((b.shape[0], n) + tuple(b.shape[1:]), b.dtype) for b in blks]
    else:
        out_shape = [S((n,) + tuple(b.shape), b.dtype) for b in blks]

    def body(*refs):
        in_refs, out_refs = refs[:k_ops], refs[k_ops:2 * k_ops]
        send_sems, recv_sems, local_sems = refs[2 * k_ops:]
        me, peers = _group(axes)

        def slot(ref, j):
            return ref.at[:, j] if interleave else ref.at[j]

        local, sent, landed = [], [], []
        for i in range(k_ops):
            cp = pltpu.make_async_copy(in_refs[i], slot(out_refs[i], me), local_sems.at[i])
            cp.start()
            local.append(cp)
        for k, (idx, dev) in enumerate(peers):
            for i in range(k_ops):
                s = i * (n - 1) + k
                cp = pltpu.make_async_remote_copy(src_ref=in_refs[i], dst_ref=slot(out_refs[i], me), send_sem=send_sems.at[s],
                                                  recv_sem=recv_sems.at[s], device_id=dev, device_id_type=MESH)
                cp.start()
                sent.append(cp)
                landed.append(pltpu.make_async_remote_copy(src_ref=in_refs[i], dst_ref=slot(out_refs[i], idx),
                                                           send_sem=send_sems.at[s], recv_sem=recv_sems.at[s],
                                                           device_id=dev, device_id_type=MESH))
        for cp in landed:
            cp.wait_recv()
        for cp in sent:
            cp.wait_send()
        for cp in local:
            cp.wait()

    return pl.pallas_call(
        body, in_specs=[ANY] * k_ops, out_specs=[ANY] * k_ops, out_shape=out_shape,
        scratch_shapes=[pltpu.SemaphoreType.DMA((k_ops * (n - 1),)), pltpu.SemaphoreType.DMA((k_ops * (n - 1),)),
                        pltpu.SemaphoreType.DMA((k_ops,))],
        name=name)(*blks)


def all_gather(name, blks):
    quads = gather_stage(name + "_ici", blks, ("x", "y"), False)
    pairs = gather_stage(name + "_d2d", quads, ("c",), True)
    return [p.reshape((N_DEV * b.shape[0],) + tuple(b.shape[1:])) for p, b in zip(pairs, blks)]


def scatter_stage(name, arr, axes, chunks):
    n = 2 ** len(axes)
    strided = arr.ndim == 4
    if strided:
        q, _, rows, c = arr.shape
        out_shape = S((n, q, rows, c), arr.dtype)
    else:
        _, rows, c = arr.shape
        out_shape = S((n, rows, c), arr.dtype)
    rc = rows // chunks

    def body(in_ref, out_ref, send_sems, recv_sems, local_sem):
        me, peers = _group(axes)

        def src(j, ch):
            rs = pl.ds(ch * rc, rc)
            return in_ref.at[:, j, rs] if strided else in_ref.at[j, rs]

        def dst(j, ch):
            rs = pl.ds(ch * rc, rc)
            return out_ref.at[j, :, rs] if strided else out_ref.at[j, rs]

        local = pltpu.make_async_copy(in_ref.at[:, me] if strided else in_ref.at[me], out_ref.at[me], local_sem)
        local.start()
        sent, landed = [], []
        for k, (idx, dev) in enumerate(peers):
            for ch in range(chunks):
                s = k * chunks + ch
                cp = pltpu.make_async_remote_copy(src_ref=src(idx, ch), dst_ref=dst(me, ch), send_sem=send_sems.at[s],
                                                  recv_sem=recv_sems.at[s], device_id=dev, device_id_type=MESH)
                cp.start()
                sent.append(cp)
                landed.append(pltpu.make_async_remote_copy(src_ref=src(idx, ch), dst_ref=dst(idx, ch),
                                                           send_sem=send_sems.at[s], recv_sem=recv_sems.at[s],
                                                           device_id=dev, device_id_type=MESH))
        for cp in landed:
            cp.wait_recv()
        for cp in sent:
            cp.wait_send()
        local.wait()

    return pl.pallas_call(
        body, in_specs=[ANY], out_specs=ANY, out_shape=out_shape,
        scratch_shapes=[pltpu.SemaphoreType.DMA(((n - 1) * chunks,)), pltpu.SemaphoreType.DMA(((n - 1) * chunks,)),
                        pltpu.SemaphoreType.DMA],
        name=name)(arr)


def sum_slots(name, slots, out_dtype, tr=256):
    n, r, c = slots.shape
    tr = _tile(r, tr)

    def body(s_ref, o_ref):
        acc = s_ref[0].astype(F32)
        for j in range(1, n):
            acc = acc + s_ref[j].astype(F32)
        o_ref[...] = acc.astype(o_ref.dtype)

    return pl.pallas_call(body, grid=(r // tr,), in_specs=[pl.BlockSpec((n, tr, c), lambda i: (0, i, 0))],
                          out_specs=pl.BlockSpec((tr, c), lambda i: (i, 0)), out_shape=S((r, c), out_dtype),
                          compiler_params=_cp("parallel"), name=name)(slots)


def adamw_native(name, g, w, m, v, tr=512):
    shape = w.shape
    cols = shape[-1]
    rows = w.size // cols
    tr = _tile(rows, tr) if rows % 8 == 0 else rows
    c1 = 1.0 - ADAM_B1 ** ADAM_STEP
    c2 = 1.0 - ADAM_B2 ** ADAM_STEP

    def body(g_ref, w_ref, m_ref, v_ref, d_ref, m2_ref, v2_ref):
        gv = g_ref[...]
        m2 = ADAM_B1 * m_ref[...] + (1.0 - ADAM_B1) * gv
        v2 = ADAM_B2 * v_ref[...] + (1.0 - ADAM_B2) * (gv * gv)
        m2_ref[...] = m2
        v2_ref[...] = v2
        d_ref[...] = -ADAM_LR * ((m2 / c1) / (jnp.sqrt(v2 / c2) + ADAM_EPS) + ADAM_WD * w_ref[...])

    row = pl.BlockSpec((tr, cols), lambda i: (i, 0))
    outs = pl.pallas_call(body, grid=(rows // tr,), in_specs=[row] * 4, out_specs=[row] * 3,
                          out_shape=[S((rows, cols), F32)] * 3, compiler_params=_cp("parallel"),
                          name=name)(*[a.reshape(rows, cols) for a in (g, w, m, v)])
    return tuple(o.reshape(shape) for o in outs)


def adamw_call(name, slots, w, m, v, tr=1024):
    n, r, c = slots.shape
    tr = _tile(r, tr)
    c1 = 1.0 - ADAM_B1 ** ADAM_STEP
    c2 = 1.0 - ADAM_B2 ** ADAM_STEP

    def body(s_ref, w_ref, m_ref, v_ref, g_ref, d_ref, m2_ref, v2_ref):
        g = s_ref[0].astype(F32)
        for j in range(1, n):
            g = g + s_ref[j].astype(F32)
        m2 = ADAM_B1 * m_ref[...] + (1.0 - ADAM_B1) * g
        v2 = ADAM_B2 * v_ref[...] + (1.0 - ADAM_B2) * (g * g)
        g_ref[...] = g
        m2_ref[...] = m2
        v2_ref[...] = v2
        d_ref[...] = -ADAM_LR * ((m2 / c1) / (jnp.sqrt(v2 / c2) + ADAM_EPS) + ADAM_WD * w_ref[...])

    row = pl.BlockSpec((tr, c), lambda i: (i, 0))
    return pl.pallas_call(body, grid=(r // tr,), in_specs=[pl.BlockSpec((n, tr, c), lambda i: (0, i, 0)), row, row, row],
                          out_specs=[row, row, row, row], out_shape=[S((r, c), F32)] * 4,
                          compiler_params=_cp("parallel"), name=name)(slots, w, m, v)


_REPLICATED = ("e_norm", "e_gmlp_w", "e_gmlp_b", "e_conv_b", "e_conv_ln_g", "e_conv_ln_b", "o_lam_re", "o_lam_im", "o_log_dt",
               "o_b_re", "o_b_im", "o_c_re", "o_c_im", "ca_norm", "ca_mem_norm", "ffn_norm", "final_norm")
_ORDER = ("e_norm", "e_w_in", "e_gmlp_w", "e_gmlp_b", "e_conv_w", "e_conv_b", "e_conv_ln_g", "e_conv_ln_b", "e_w_out",
          "o_norm", "o_w_in", "o_lam_re", "o_lam_im", "o_log_dt", "o_b_re", "o_b_im", "o_c_re", "o_c_im", "o_d", "o_w_out",
          "ca_norm", "ca_mem_norm", "ca_wq", "ca_wk", "ca_wv", "ca_wo", "ffn_norm", "ffn_w_gate", "ffn_w_up", "ffn_w_down",
          "final_norm")


def _rows128(a, multiple=8):
    flat = a.reshape(-1)
    rows = -(-flat.shape[0] // (LANES * multiple)) * multiple
    return jnp.pad(flat, (0, rows * LANES - flat.shape[0])).reshape(rows, LANES)


def _shard(full, axis):
    s = full.shape
    return jnp.moveaxis(full.reshape(s[:axis] + (N_DEV, s[axis] // N_DEV) + s[axis + 1:]), axis, 0)


_UNITS = (("e_w_in", 0, True), ("e_w_out", 0, False), ("o_w_in", 0, False), ("o_w_out", 0, True),
          *[(n, i, False) for n in ("ca_wq", "ca_wk", "ca_wv", "ca_wo") for i in (0, 1)],
          *[(n, i, tr) for n, tr in (("ffn_w_gate", True), ("ffn_w_up", True), ("ffn_w_down", False)) for i in (0, 1)])
_LAYERED = ("ca_wq", "ca_wk", "ca_wv", "ca_wo", "ffn_w_gate", "ffn_w_up", "ffn_w_down")
_SMALL_SHARDED = (("e_conv_w", 2), ("o_norm", 1), ("o_d", 1))
RS_ROW = 1024
RS_ROWS = 3840
RS_CHUNKS = 8


def _unit_key(name, tr):
    return name + "_t" if tr else name


def gather_weights(local):
    blks = []
    for name, layer, tr in _UNITS:
        blk = local[name][layer]
        blks.append(_bf(blk.T if tr else blk))
    small = jnp.concatenate([local[name].reshape(-1) for name, _ in _SMALL_SHARDED])
    blks.append(_rows128(small))
    full = all_gather("ag_w", blks)
    w = {}
    for (name, layer, tr), arr in zip(_UNITS, full[:-1]):
        key = _unit_key(name, tr)
        w[key] = w.get(key, ()) + (arr,) if name in _LAYERED else arr
    flat = full[-1].reshape(N_DEV, -1)
    off = 0
    for name, axis in _SMALL_SHARDED:
        blk = local[name]
        seg = flat[:, off:off + blk.size].reshape((N_DEV,) + blk.shape)
        off += blk.size
        seg = jnp.moveaxis(seg, 0, axis)
        w[name] = seg.reshape(seg.shape[:axis] + (-1,) + seg.shape[axis + 2:])
    return w


def reduce_sharded(grads, local, mom, var):
    parts, spans = [], []
    for name, layer, tr in _UNITS:
        g = grads[_unit_key(name, tr)]
        g = g[layer] if name in _LAYERED else g
        part = g.reshape(4, 2, -1, RS_ROW)
        spans.append((part.shape[2], g.shape[0] // N_DEV, g.shape[1]))
        parts.append(part)
    small = jnp.concatenate([_shard(grads[name], axis).reshape(N_DEV, -1) for name, axis in _SMALL_SHARDED], axis=1)
    n_small = small.shape[1]
    small_rows = 16
    parts.append(jnp.pad(small, ((0, 0), (0, small_rows * RS_ROW - n_small))).astype(BF16).reshape(4, 2, small_rows, RS_ROW))
    used = sum(p.shape[2] for p in parts)
    parts.append(jnp.zeros((4, 2, RS_ROWS - used, RS_ROW), BF16))
    pack = jnp.concatenate(parts, axis=2)
    pair = scatter_stage("rs_g_d2d", pack, ("c",), RS_CHUNKS)
    chip_sum = sum_slots("rs_g_sum2", pair.reshape(2, 4 * RS_ROWS, RS_ROW), BF16)
    quad = scatter_stage("rs_g_ici", chip_sum.reshape(4, RS_ROWS, RS_ROW), ("x", "y"), 1)
    total = sum_slots("rs_g_sum4", quad, F32)

    res, off, per_layer = {}, 0, {}
    for (name, layer, tr), (rows, r, c) in zip(_UNITS, spans):
        g = total[off:off + rows].reshape(r, c)
        off += rows
        per_layer.setdefault(name, []).append(g.T if tr else g)
    for name, gs in per_layer.items():
        g = jnp.stack(gs) if name in _LAYERED else gs[0][None]
        res[name] = (g,) + adamw_native("adamw_" + name, g, local[name], mom[name], var[name])
    flat = total[off:off + small_rows].reshape(-1)
    off = 0
    for name, _ in _SMALL_SHARDED:
        blk = local[name]
        g = flat[off:off + blk.size].reshape(blk.shape)
        off += blk.size
        res[name] = (g,) + adamw_native("adamw_" + name, g, blk, mom[name], var[name])
    return res


def reduce_replicated(grads, loss, w, mom, var):
    def pack(src, last):
        return jnp.concatenate([_rows128(src[name]) for name in _REPLICATED] + [_rows128(last)], axis=0)

    (slots,) = all_gather("ag_g", [pack(grads, loss)])
    zero = jnp.zeros((1, 1), F32)
    rows = slots.shape[0] // N_DEV
    outs = adamw_call("adamw_replicated", slots.reshape(N_DEV, rows, LANES), pack(w, zero), pack(mom, zero), pack(var, zero),
                      tr=rows)
    res, off = {}, 0
    for name in _REPLICATED:
        n = w[name].size
        nr = _rows128(w[name]).shape[0]
        res[name] = tuple(o[off:off + nr].reshape(-1)[:n].reshape(w[name].shape) for o in outs)
        off += nr
    return res, outs[0][off, 0]


def kernel(x, mem, e_norm, e_w_in, e_gmlp_w, e_gmlp_b, e_conv_w, e_conv_b, e_conv_ln_g, e_conv_ln_b, e_w_out, o_norm, o_w_in, o_lam_re, o_lam_im, o_log_dt, o_b_re, o_b_im, o_c_re, o_c_im, o_d, o_w_out, ca_norm, ca_mem_norm, ca_wq, ca_wk, ca_wv, ca_wo, ffn_norm, ffn_w_gate, ffn_w_up, ffn_w_down, final_norm, loss_target, m_e_norm, m_e_w_in, m_e_gmlp_w, m_e_gmlp_b, m_e_conv_w, m_e_conv_b, m_e_conv_ln_g, m_e_conv_ln_b, m_e_w_out, m_o_norm, m_o_w_in, m_o_lam_re, m_o_lam_im, m_o_log_dt, m_o_b_re, m_o_b_im, m_o_c_re, m_o_c_im, m_o_d, m_o_w_out, m_ca_norm, m_ca_mem_norm, m_ca_wq, m_ca_wk, m_ca_wv, m_ca_wo, m_ffn_norm, m_ffn_w_gate, m_ffn_w_up, m_ffn_w_down, m_final_norm, v_e_norm, v_e_w_in, v_e_gmlp_w, v_e_gmlp_b, v_e_conv_w, v_e_conv_b, v_e_conv_ln_g, v_e_conv_ln_b, v_e_w_out, v_o_norm, v_o_w_in, v_o_lam_re, v_o_lam_im, v_o_log_dt, v_o_b_re, v_o_b_im, v_o_c_re, v_o_c_im, v_o_d, v_o_w_out, v_ca_norm, v_ca_mem_norm, v_ca_wq, v_ca_wk, v_ca_wv, v_ca_wo, v_ffn_norm, v_ffn_w_gate, v_ffn_w_up, v_ffn_w_down, v_final_norm):
    given = dict(locals())
    local = {k: given[k] for k in _ORDER}
    mom = {k: given["m_" + k] for k in _ORDER}
    var = {k: given["v_" + k] for k in _ORDER}

    w = gather_weights(local)
    w["e_conv_w"] = w["e_conv_w"][0]
    w.update({
        "e_norm": e_norm, "e_gmlp_w": e_gmlp_w[0], "e_gmlp_b": e_gmlp_b.reshape(A_GROUPS, GMLP_BLOCK, 1),
        "e_conv_b": e_conv_b, "e_conv_ln_g": e_conv_ln_g, "e_conv_ln_b": e_conv_ln_b,
        "o_lam_re": o_lam_re[0], "o_lam_im": o_lam_im[0], "o_log_dt": o_log_dt[0], "o_b_re": o_b_re[0], "o_b_im": o_b_im[0],
        "o_c_re": o_c_re[0], "o_c_im": o_c_im[0], "ca_norm": ca_norm, "ca_mem_norm": ca_mem_norm, "ffn_norm": ffn_norm,
        "final_norm": final_norm.reshape(1, D_MODEL),
    })
    loss_part, grad_x, grads = local_step(x[0], mem[0], loss_target[0], w)
    grads["final_norm"] = grads["final_norm"].reshape(D_MODEL)

    res = reduce_sharded(grads, local, mom, var)
    rep, loss = reduce_replicated(grads, loss_part, local, mom, var)
    res.update(rep)
    return (loss, grad_x[None], *[res[k][0] for k in _ORDER], *[res[k][1] for k in _ORDER],
            *[res[k][2] for k in _ORDER], *[res[k][3] for k in _ORDER])
```

```python
import jax
import jax.numpy as jnp
from jax import lax
from jax.experimental import pallas as pl
from jax.experimental.pallas import tpu as pltpu

F32 = jnp.float32
BF16 = jnp.bfloat16
S = jax.ShapeDtypeStruct

D_MODEL = 1024
A_WIDTH = 512
A_GROUPS = 4
GMLP_BLOCK = 128
CHUNK = 64
B_WIDTH = 512
IN_WIDTH = 2 * A_WIDTH + 2 * B_WIDTH
CONV_WIDTH = 31
CONV_PAD = 32
C_WIDTH = 512
C_GROUP_CH = 16
C_GROUPS = 32
C_STATE = 64
N_STATE = C_GROUPS * C_STATE
CA_HEADS = 4
CA_HEAD_DIM = 256
FFN_HIDDEN = 2816
EPS = 1e-6
ADAM_LR = 0.001
ADAM_B1 = 0.9
ADAM_B2 = 0.999
ADAM_EPS = 1e-08
ADAM_WD = 0.01
ADAM_STEP = 10
N_DEV = 8
LANES = 128
VMEM_LIMIT = 56 << 20
MESH = pl.DeviceIdType.MESH
ANY = pl.BlockSpec(memory_space=pl.ANY)


def _cp(*sem):
    return pltpu.CompilerParams(dimension_semantics=sem, vmem_limit_bytes=VMEM_LIMIT)


def _tile(n, pref):
    t = pref
    while n % t:
        t //= 2
    return t


def _bf(v):
    return v if v.dtype == BF16 else v.astype(BF16)


def _sigmoid(x):
    return 1.0 / (1.0 + jnp.exp(-x))


_GC = 0.7978845608028654


def _gelu(x):
    return 0.5 * x * (1.0 + jnp.tanh(_GC * (x + 0.044715 * x * x * x)))


def _gelu_grad(x):
    x2 = x * x
    t = jnp.tanh(_GC * (x + 0.044715 * x * x2))
    return 0.5 * (1.0 + t) + 0.5 * x * (1.0 - t * t) * _GC * (1.0 + 3.0 * 0.044715 * x2)


def _tspec(entry, tm):
    if isinstance(entry, tuple):
        arr, cb, width = entry
        return arr, pl.BlockSpec((tm, width), lambda i, cb=cb: (i, cb))
    return entry, pl.BlockSpec((tm, entry.shape[1]), lambda i: (i, 0))


def rows_call(name, fn, tiled, full, outs, accs, tm=256):
    pairs = [_tspec(e, tm) for e in tiled]
    arrs = [p[0] for p in pairs]
    rows = arrs[0].shape[0]
    tm = _tile(rows, tm)
    pairs = [_tspec(e, tm) for e in tiled]
    n_in = len(tiled) + len(full)
    n_out = len(outs)

    def body(*refs):
        vals = [r[...] for r in refs[:n_in]]
        o_refs = refs[n_in:n_in + n_out]
        a_refs = refs[n_in + n_out:]
        ov, av = fn(*vals)
        for r, v in zip(o_refs, ov):
            r[...] = v.astype(r.dtype)
        if a_refs:
            @pl.when(pl.program_id(0) == 0)
            def _():
                for r in a_refs:
                    r[...] = jnp.zeros(r.shape, r.dtype)
            for r, v in zip(a_refs, av):
                r[...] += v

    in_specs = [p[1] for p in pairs] + [pl.BlockSpec(a.shape, lambda i, nd=a.ndim: (0,) * nd) for a in full]
    out_specs = [pl.BlockSpec((tm, c), lambda i: (i, 0)) for c, _ in outs]
    out_specs += [pl.BlockSpec(s, lambda i, nd=len(s): (0,) * nd) for s in accs]
    out_shape = [S((rows, c), dt) for c, dt in outs] + [S(s, F32) for s in accs]
    return pl.pallas_call(body, grid=(rows // tm,), in_specs=in_specs, out_specs=out_specs, out_shape=out_shape,
                          compiler_params=_cp("arbitrary"), name=name)(*arrs, *full)


def mm_nn(name, m, n, pairs, n_acc, epi, outs, tiled=(), cols=(), rowv=(), tm=512, tn=512):
    tm = _tile(m, tm)
    tn = _tile(n, tn)
    a_arrs, a_specs, b_arrs, b_specs, idx, trans = [], [], [], [], [], []
    for pair in pairs:
        a, b, k = pair[:3]
        bt = len(pair) > 3
        if isinstance(a, tuple):
            arr, cb, kdim = a
            a_specs.append(pl.BlockSpec((tm, kdim), lambda i, j, cb=cb: (i, cb)))
        else:
            arr, kdim = a, a.shape[1]
            a_specs.append(pl.BlockSpec((tm, kdim), lambda i, j: (i, 0)))
        a_arrs.append(arr)
        b_arr, off = b if isinstance(b, tuple) else (b, 0)
        b_arrs.append(b_arr)
        if bt:
            assert off % tn == 0 and b_arr.shape[1] == kdim
            b_specs.append(pl.BlockSpec((tn, kdim), lambda i, j, o=off // tn: (j + o, 0)))
        else:
            b_specs.append(pl.BlockSpec((kdim, tn), lambda i, j, o=off: (o, j)))
        idx.append(k)
        trans.append(bt)
    n_p = len(pairs)
    n_in = 2 * n_p + len(tiled) + len(cols) + len(rowv)

    def body(*refs):
        accs = [None] * n_acc
        for p in range(n_p):
            av, bv = _bf(refs[p][...]), _bf(refs[n_p + p][...])
            if trans[p]:
                d = lax.dot_general(av, bv, (((1,), (1,)), ((), ())), preferred_element_type=F32)
            else:
                d = jnp.dot(av, bv, preferred_element_type=F32)
            accs[idx[p]] = d if accs[idx[p]] is None else accs[idx[p]] + d
        extra = [r[...] for r in refs[2 * n_p:n_in]]
        ov = epi(accs, *extra)
        for r, v in zip(refs[n_in:], ov):
            r[...] = v.astype(r.dtype)

    in_specs = a_specs + b_specs
    in_specs += [pl.BlockSpec((tm, tn), lambda i, j: (i, j)) for _ in tiled]
    in_specs += [pl.BlockSpec((tm, 1), lambda i, j: (i, 0)) for _ in cols]
    in_specs += [pl.BlockSpec((1, tn), lambda i, j: (0, j)) for _ in rowv]
    out_specs = [pl.BlockSpec((tm, tn), lambda i, j: (i, j)) for _ in outs]
    out_shape = [S((m, n), dt) for dt in outs]
    return pl.pallas_call(body, grid=(m // tm, n // tn), in_specs=in_specs, out_specs=out_specs, out_shape=out_shape,
                          compiler_params=_cp("parallel", "parallel"), name=name)(*a_arrs, *b_arrs, *tiled, *cols, *rowv)


def mm_tn(name, a, b, tm=512, tn=512, out_dtype=BF16):
    if isinstance(a, tuple):
        a_arr, a_cb, m = a
    else:
        a_arr, a_cb, m = a, None, a.shape[1]
    if isinstance(b, tuple):
        b_arr, b_cb, n = b
    else:
        b_arr, b_cb, n = b, None, b.shape[1]
    t = a_arr.shape[0]
    tm = _tile(m, tm)
    tn = _tile(n, tn)
    a_off = 0 if a_cb is None else a_cb * (m // tm)
    b_off = 0 if b_cb is None else b_cb * (n // tn)

    def body(a_ref, b_ref, o_ref):
        o_ref[...] = lax.dot_general(_bf(a_ref[...]), _bf(b_ref[...]), (((0,), (0,)), ((), ())),
                                     preferred_element_type=F32).astype(o_ref.dtype)

    return pl.pallas_call(
        body, grid=(m // tm, n // tn),
        in_specs=[pl.BlockSpec((t, tm), lambda i, j: (0, i + a_off)), pl.BlockSpec((t, tn), lambda i, j: (0, j + b_off))],
        out_specs=pl.BlockSpec((tm, tn), lambda i, j: (i, j)), out_shape=S((m, n), out_dtype),
        compiler_params=_cp("parallel", "parallel"), name=name)(a_arr, b_arr)


def rms_fwd(name, x, gain):
    def fn(xv, g):
        r = lax.rsqrt(jnp.mean(xv * xv, axis=-1, keepdims=True) + EPS)
        return [xv * r * g, r], []
    return rows_call(name, fn, [x], [gain], [(x.shape[1], BF16), (1, F32)], [])


def rms_bwd(name, dxn, x, r, gain, dres=None):
    d = x.shape[1]

    def fn(*vals):
        if dres is None:
            dv, xv, rv, g = vals
            base = 0.0
        else:
            dv, xv, rv, base, g = vals
        w = dv * g
        xh = xv * rv
        dx = base + rv * (w - xh * jnp.mean(w * xh, axis=-1, keepdims=True))
        return [dx, dx], [jnp.sum(dv * xh, axis=0, keepdims=True)]

    tiled = [dxn, x, r] + ([] if dres is None else [dres])
    return rows_call(name, fn, tiled, [gain], [(d, F32), (d, BF16)], [(1, d)])


def rms_bwd_gain_only(name, dxn, x, r):
    def fn(dv, xv, rv):
        return [], [jnp.sum(dv * xv * rv, axis=0, keepdims=True)]
    return rows_call(name, fn, [dxn, x, r], [], [], [(1, x.shape[1])])[0]


def final_loss(name, x, gain, target):
    d = x.shape[1]

    def fn(xv, tv, g):
        r = lax.rsqrt(jnp.mean(xv * xv, axis=-1, keepdims=True) + EPS)
        xh = xv * r
        err = xh * g - tv
        dy = err * (1.0 / d)
        w = dy * g
        dx = r * (w - xh * jnp.mean(w * xh, axis=-1, keepdims=True))
        part = jnp.sum(jnp.sum(err * err, axis=-1, keepdims=True), axis=0, keepdims=True) * (0.5 / d)
        return [dx, dx], [jnp.sum(dy * xh, axis=0, keepdims=True), part]

    return rows_call(name, fn, [x, target], [gain], [(d, F32), (d, BF16)], [(1, d), (1, 1)])


def _gmlp_mask():
    row = lax.broadcasted_iota(jnp.int32, (GMLP_BLOCK, GMLP_BLOCK), 0) // CHUNK
    col = lax.broadcasted_iota(jnp.int32, (GMLP_BLOCK, GMLP_BLOCK), 1) // CHUNK
    return col <= row


def _ln_plain(v):
    mu = jnp.mean(v, axis=-1, keepdims=True)
    vc = v - mu
    rstd = lax.rsqrt(jnp.mean(vc * vc, axis=-1, keepdims=True) + EPS)
    return vc * rstd, rstd


def gmlp_fwd(name, proj, w, b, tm=512):
    t = proj.shape[0]
    tm = _tile(t, tm)

    def body(au_ref, av_ref, w_ref, b_ref, o_ref):
        mask = _gmlp_mask()
        u = _gelu(au_ref[...])
        vn, _ = _ln_plain(_gelu(av_ref[...]))
        vnb = _bf(vn)
        for g in range(A_GROUPS):
            wg = _bf(jnp.where(mask, w_ref[g], 0.0))
            cs = slice(g * GMLP_BLOCK, (g + 1) * GMLP_BLOCK)
            for n in range(tm // GMLP_BLOCK):
                rs = slice(n * GMLP_BLOCK, (n + 1) * GMLP_BLOCK)
                sg = jnp.dot(wg, vnb[rs, cs], preferred_element_type=F32) + b_ref[g]
                o_ref[rs, cs] = (u[rs, cs] * sg).astype(o_ref.dtype)

    return pl.pallas_call(
        body, grid=(t // tm,),
        in_specs=[pl.BlockSpec((tm, A_WIDTH), lambda i: (i, 0)), pl.BlockSpec((tm, A_WIDTH), lambda i: (i, 1)),
                  pl.BlockSpec(w.shape, lambda i: (0, 0, 0)), pl.BlockSpec(b.shape, lambda i: (0, 0, 0))],
        out_specs=pl.BlockSpec((tm, A_WIDTH), lambda i: (i, 0)), out_shape=S((t, A_WIDTH), BF16),
        compiler_params=_cp("parallel"), name=name)(proj, proj, w, b)


def gmlp_bwd(name, proj, dcat, w, b, tm=512):
    t = proj.shape[0]
    tm = _tile(t, tm)

    def body(au_ref, av_ref, do_ref, w_ref, b_ref, dp_ref, dw_ref, db_ref):
        @pl.when(pl.program_id(0) == 0)
        def _():
            dw_ref[...] = jnp.zeros(dw_ref.shape, F32)
            db_ref[...] = jnp.zeros(db_ref.shape, F32)

        mask = _gmlp_mask()
        au = au_ref[...]
        av = av_ref[...]
        u = _gelu(au)
        vn, rstd = _ln_plain(_gelu(av))
        vnb = _bf(vn)
        dout = do_ref[...]
        dvn_cols = []
        for g in range(A_GROUPS):
            wm = jnp.where(mask, w_ref[g], 0.0)
            wg = _bf(wm)
            wgt = _bf(wm.T)
            cs = slice(g * GMLP_BLOCK, (g + 1) * GMLP_BLOCK)
            dwg = jnp.zeros((GMLP_BLOCK, GMLP_BLOCK), F32)
            dbg = jnp.zeros((GMLP_BLOCK, 1), F32)
            dvn_rows = []
            for n in range(tm // GMLP_BLOCK):
                rs = slice(n * GMLP_BLOCK, (n + 1) * GMLP_BLOCK)
                sg = jnp.dot(wg, vnb[rs, cs], preferred_element_type=F32) + b_ref[g]
                dp_ref[rs, cs] = (dout[rs, cs] * sg * _gelu_grad(au[rs, cs])).astype(dp_ref.dtype)
                dsg = dout[rs, cs] * u[rs, cs]
                dsgb = _bf(dsg)
                dbg = dbg + jnp.sum(dsg, axis=1, keepdims=True)
                dwg = dwg + lax.dot_general(dsgb, vnb[rs, cs], (((1,), (1,)), ((), ())), preferred_element_type=F32)
                dvn_rows.append(jnp.dot(wgt, dsgb, preferred_element_type=F32))
            dw_ref[g] += jnp.where(mask, dwg, 0.0)
            db_ref[g] += dbg
            dvn_cols.append(jnp.concatenate(dvn_rows, axis=0))
        dvn = jnp.concatenate(dvn_cols, axis=1)
        dv = rstd * (dvn - jnp.mean(dvn, axis=-1, keepdims=True) - vn * jnp.mean(dvn * vn, axis=-1, keepdims=True))
        dp_ref[:, A_WIDTH:] = (dv * _gelu_grad(av)).astype(dp_ref.dtype)

    return pl.pallas_call(
        body, grid=(t // tm,),
        in_specs=[pl.BlockSpec((tm, A_WIDTH), lambda i: (i, 0)), pl.BlockSpec((tm, A_WIDTH), lambda i: (i, 1)),
                  pl.BlockSpec((tm, A_WIDTH), lambda i: (i, 0)),
                  pl.BlockSpec(w.shape, lambda i: (0, 0, 0)), pl.BlockSpec(b.shape, lambda i: (0, 0, 0))],
        out_specs=[pl.BlockSpec((tm, 2 * A_WIDTH), lambda i: (i, 0)),
                   pl.BlockSpec(w.shape, lambda i: (0, 0, 0)), pl.BlockSpec(b.shape, lambda i: (0, 0, 0))],
        out_shape=[S((t, 2 * A_WIDTH), BF16), S(w.shape, F32), S(b.shape, F32)],
        compiler_params=_cp("arbitrary"), name=name)(proj, proj, dcat, w, b)


CONV_ROWS = 256


def conv_fwd(name, proj, w, cb):
    t = proj.shape[0]
    tc = LANES
    rows = _tile(t, CONV_ROWS)
    a_cb, g_cb = 2 * A_WIDTH // tc, (2 * A_WIDTH + B_WIDTH) // tc

    def body(a_ref, g_ref, w_ref, cb_ref, o_ref, hpad):
        hpad[0:CONV_PAD, :] = jnp.zeros((CONV_PAD, tc), F32)

        def fill(i, _):
            r0 = pl.multiple_of(i * rows, rows)
            hpad[pl.ds(CONV_PAD + r0, rows), :] = a_ref[pl.ds(r0, rows), :] * _sigmoid(g_ref[pl.ds(r0, rows), :])
            return 0
        lax.fori_loop(0, t // rows, fill, 0)

        def conv(i, _):
            r0 = pl.multiple_of(i * rows, rows)
            win = hpad[pl.ds(r0, rows + CONV_PAD), :]
            acc = jnp.zeros((rows, tc), F32) + cb_ref[...]
            for k in range(CONV_WIDTH):
                sh = CONV_WIDTH - 1 - k
                src = win if sh == 0 else pltpu.roll(win, sh, 0)
                acc = acc + src[CONV_PAD:, :] * w_ref[k:k + 1, :]
            o_ref[pl.ds(r0, rows), :] = acc
            return 0
        lax.fori_loop(0, t // rows, conv, 0)

    return pl.pallas_call(
        body, grid=(B_WIDTH // tc,),
        in_specs=[pl.BlockSpec((t, tc), lambda j: (0, a_cb + j)), pl.BlockSpec((t, tc), lambda j: (0, g_cb + j)),
                  pl.BlockSpec((CONV_WIDTH, tc), lambda j: (0, j)), pl.BlockSpec((1, tc), lambda j: (0, j))],
        out_specs=pl.BlockSpec((t, tc), lambda j: (0, j)), out_shape=S((t, B_WIDTH), F32),
        scratch_shapes=[pltpu.VMEM((t + CONV_PAD, tc), F32)],
        compiler_params=_cp("parallel"), name=name)(proj, proj, w, cb)


def conv_bwd(name, proj, dhc, w):
    t = proj.shape[0]
    tc = LANES
    rows = _tile(t, CONV_ROWS)
    a_cb, g_cb = 2 * A_WIDTH // tc, (2 * A_WIDTH + B_WIDTH) // tc
    win_rows = rows + CONV_PAD

    def body(a_ref, g_ref, d_ref, w_ref, da_ref, dg_ref, dw_ref, dcb_ref, hpad, dpad, dwacc):
        hpad[0:CONV_PAD, :] = jnp.zeros((CONV_PAD, tc), F32)
        dpad[t:t + CONV_PAD, :] = jnp.zeros((CONV_PAD, tc), F32)
        dwacc[...] = jnp.zeros(dwacc.shape, F32)

        def fill(i, _):
            r0 = pl.multiple_of(i * rows, rows)
            hpad[pl.ds(CONV_PAD + r0, rows), :] = a_ref[pl.ds(r0, rows), :] * _sigmoid(g_ref[pl.ds(r0, rows), :])
            dpad[pl.ds(r0, rows), :] = d_ref[pl.ds(r0, rows), :]
            return 0
        lax.fori_loop(0, t // rows, fill, 0)

        def step(i, dcb):
            r0 = pl.multiple_of(i * rows, rows)
            hwin = hpad[pl.ds(r0, win_rows), :]
            dwin = dpad[pl.ds(r0, win_rows), :]
            dchunk = dwin[:rows, :]
            dh = jnp.zeros((rows, tc), F32)
            for k in range(CONV_WIDTH):
                sh = CONV_WIDTH - 1 - k
                hsrc = hwin if sh == 0 else pltpu.roll(hwin, sh, 0)
                dsrc = dwin if sh == 0 else pltpu.roll(dwin, win_rows - sh, 0)
                dh = dh + dsrc[:rows, :] * w_ref[k:k + 1, :]
                prod = dchunk * hsrc[CONV_PAD:, :]
                dwacc[k] += jnp.sum(prod.reshape(rows // 8, 8, tc), axis=0)
            a = a_ref[pl.ds(r0, rows), :]
            sg = _sigmoid(g_ref[pl.ds(r0, rows), :])
            da_ref[pl.ds(r0, rows), :] = (dh * sg).astype(da_ref.dtype)
            dg_ref[pl.ds(r0, rows), :] = (dh * a * sg * (1.0 - sg)).astype(dg_ref.dtype)
            return dcb + jnp.sum(dchunk, axis=0, keepdims=True)
        dcb = lax.fori_loop(0, t // rows, step, jnp.zeros((1, tc), F32))
        dcb_ref[...] = dcb
        for k in range(CONV_WIDTH):
            dw_ref[k:k + 1, :] = jnp.sum(dwacc[k], axis=0, keepdims=True)

    return pl.pallas_call(
        body, grid=(B_WIDTH // tc,),
        in_specs=[pl.BlockSpec((t, tc), lambda j: (0, a_cb + j)), pl.BlockSpec((t, tc), lambda j: (0, g_cb + j)),
                  pl.BlockSpec((t, tc), lambda j: (0, j)), pl.BlockSpec((CONV_WIDTH, tc), lambda j: (0, j))],
        out_specs=[pl.BlockSpec((t, tc), lambda j: (0, j)), pl.BlockSpec((t, tc), lambda j: (0, j)),
                   pl.BlockSpec((CONV_WIDTH, tc), lambda j: (0, j)), pl.BlockSpec((1, tc), lambda j: (0, j))],
        out_shape=[S((t, B_WIDTH), BF16), S((t, B_WIDTH), BF16), S((CONV_WIDTH, B_WIDTH), F32), S((1, B_WIDTH), F32)],
        scratch_shapes=[pltpu.VMEM((t + CONV_PAD, tc), F32), pltpu.VMEM((t + CONV_PAD, tc), F32),
                        pltpu.VMEM((CONV_WIDTH, 8, tc), F32)],
        compiler_params=_cp("parallel"), name=name)(proj, proj, dhc, w)


def ln_silu_fwd(name, hc, g, b):
    def fn(h, gv, bv):
        y, _ = _ln_plain(h)
        z = y * gv + bv
        return [z * _sigmoid(z)], []
    return rows_call(name, fn, [hc], [g, b], [(hc.shape[1], BF16)], [])[0]


def ln_silu_bwd(name, hc, dcat, g, b):
    c = hc.shape[1]

    def fn(h, dout, gv, bv):
        y, rstd = _ln_plain(h)
        z = y * gv + bv
        s = _sigmoid(z)
        dz = dout * s * (1.0 + z * (1.0 - s))
        dyv = dz * gv
        dh = rstd * (dyv - jnp.mean(dyv, axis=-1, keepdims=True) - y * jnp.mean(dyv * y, axis=-1, keepdims=True))
        return [dh], [jnp.sum(dz * y, axis=0, keepdims=True), jnp.sum(dz, axis=0, keepdims=True)]

    return rows_call(name, fn, [hc, (dcat, 1, c)], [g, b], [(c, F32)], [(1, c), (1, c)])


_NT = (((1,), (1,)), ((), ()))
_TN = (((0,), (0,)), ((), ()))


def attn_fwd(name, q, k, v, tm=512):
    t, d = q.shape
    m = k.shape[0]
    tm = _tile(t, tm)
    scale = CA_HEAD_DIM ** -0.5

    def body(q_ref, k_ref, v_ref, o_ref):
        for h in range(CA_HEADS):
            cs = slice(h * CA_HEAD_DIM, (h + 1) * CA_HEAD_DIM)
            s = lax.dot_general(q_ref[:, cs], k_ref[:, cs], _NT, preferred_element_type=F32) * scale
            e = jnp.exp(s - jnp.max(s, axis=-1, keepdims=True))
            p = e / jnp.sum(e, axis=-1, keepdims=True)
            o_ref[:, cs] = jnp.dot(_bf(p), v_ref[:, cs], preferred_element_type=F32).astype(o_ref.dtype)

    return pl.pallas_call(
        body, grid=(t // tm,),
        in_specs=[pl.BlockSpec((tm, d), lambda i: (i, 0)), pl.BlockSpec((m, d), lambda i: (0, 0)),
                  pl.BlockSpec((m, d), lambda i: (0, 0))],
        out_specs=pl.BlockSpec((tm, d), lambda i: (i, 0)), out_shape=S((t, d), BF16),
        compiler_params=_cp("parallel"), name=name)(q, k, v)


def attn_bwd(name, q, k, v, do, tm=512):
    t, d = q.shape
    m = k.shape[0]
    tm = _tile(t, tm)
    scale = CA_HEAD_DIM ** -0.5

    def body(q_ref, k_ref, v_ref, do_ref, dq_ref, dk_ref, dv_ref):
        @pl.when(pl.program_id(0) == 0)
        def _():
            dk_ref[...] = jnp.zeros(dk_ref.shape, F32)
            dv_ref[...] = jnp.zeros(dv_ref.shape, F32)

        for h in range(CA_HEADS):
            cs = slice(h * CA_HEAD_DIM, (h + 1) * CA_HEAD_DIM)
            qh, kh, vh, doh = q_ref[:, cs], k_ref[:, cs], v_ref[:, cs], do_ref[:, cs]
            s = lax.dot_general(qh, kh, _NT, preferred_element_type=F32) * scale
            e = jnp.exp(s - jnp.max(s, axis=-1, keepdims=True))
            p = e / jnp.sum(e, axis=-1, keepdims=True)
            pb = _bf(p)
            dv_ref[:, cs] += lax.dot_general(pb, doh, _TN, preferred_element_type=F32)
            dp = lax.dot_general(doh, vh, _NT, preferred_element_type=F32)
            ds = _bf(p * (dp - jnp.sum(dp * p, axis=-1, keepdims=True)) * scale)
            dq_ref[:, cs] = jnp.dot(ds, kh, preferred_element_type=F32).astype(dq_ref.dtype)
            dk_ref[:, cs] += lax.dot_general(ds, qh, _TN, preferred_element_type=F32)

    return pl.pallas_call(
        body, grid=(t // tm,),
        in_specs=[pl.BlockSpec((tm, d), lambda i: (i, 0)), pl.BlockSpec((m, d), lambda i: (0, 0)),
                  pl.BlockSpec((m, d), lambda i: (0, 0)), pl.BlockSpec((tm, d), lambda i: (i, 0))],
        out_specs=[pl.BlockSpec((tm, d), lambda i: (i, 0)), pl.BlockSpec((m, d), lambda i: (0, 0)),
                   pl.BlockSpec((m, d), lambda i: (0, 0))],
        out_shape=[S((t, d), BF16), S((m, d), F32), S((m, d), F32)],
        compiler_params=_cp("arbitrary"), name=name)(q, k, v, do)


SUB = 8
S5_ROWS = 256


def s5_constants(lam_re, lam_im, log_dt, b_re, b_im, c_re, c_im):
    dt = jnp.exp(log_dt)[:, None]
    mag = jnp.exp(lam_re * dt)
    ar = mag * jnp.cos(lam_im * dt)
    ai = mag * jnp.sin(lam_im * dt)
    den = lam_re * lam_re + lam_im * lam_im
    qr = ((ar - 1.0) * lam_re + ai * lam_im) / den
    qi = (ai * lam_re - (ar - 1.0) * lam_im) / den
    bbr = qr[..., None] * b_re - qi[..., None] * b_im
    bbi = qr[..., None] * b_im + qi[..., None] * b_re
    eye = jnp.eye(C_GROUPS, dtype=F32)

    def in_mat(bb):
        return (bb.transpose(0, 2, 1)[:, :, None, :] * eye[:, None, :, None]).reshape(C_WIDTH, N_STATE)

    def out_mat(cc):
        return (cc.transpose(0, 2, 1)[:, :, None, :] * eye[:, None, :, None]).reshape(N_STATE, C_WIDTH)

    mb = jnp.concatenate([in_mat(bbr), in_mat(bbi)], axis=1)
    mc = jnp.concatenate([out_mat(c_re), -out_mat(c_im)], axis=0)
    a = jnp.stack([ar.reshape(N_STATE), ai.reshape(N_STATE)])
    return a, mb, mc


def _scan_powers(a, conj):
    ar, ai = a[0], (-a[1] if conj else a[1])
    pows = [(ar, ai)]
    for _ in range(SUB - 1):
        pr, pi = pows[-1]
        pows.append((pr * ar - pi * ai, pr * ai + pi * ar))
    rows = jnp.arange(SUB)[:, None]
    out = []
    for s in (1, 2, 4):
        keep = (rows + s <= SUB - 1) if conj else (rows >= s)
        out.append(jnp.stack([jnp.where(keep, pows[s - 1][0][None, :], 0.0), jnp.where(keep, pows[s - 1][1][None, :], 0.0)]))
    order = [SUB - 1 - i for i in range(SUB)] if conj else list(range(SUB))
    out.append(jnp.stack([jnp.stack([pows[i][0] for i in order]), jnp.stack([pows[i][1] for i in order])]))
    return jnp.stack(out)


def _cmul_add(xr, xi, pr, pi, zr, zi):
    return xr + pr * zr - pi * zi, xi + pr * zi + pi * zr


def s5_fwd(name, u, mb, mc, pw, dskip):
    t = u.shape[0]
    tm = _tile(t, S5_ROWS)
    ns = N_STATE

    def body(u_ref, mb_ref, mc_ref, pw_ref, d_ref, gy_ref, y_ref, xs_ref, xb_ref, carry):
        @pl.when(pl.program_id(0) == 0)
        def _():
            carry[...] = jnp.zeros(carry.shape, F32)

        uv = u_ref[...]
        xs_ref[...] = jnp.dot(_bf(uv), mb_ref[...], preferred_element_type=F32)

        def group(i, _):
            r0 = pl.multiple_of(i * SUB, SUB)
            xr = xs_ref[pl.ds(r0, SUB), 0:ns]
            xi = xs_ref[pl.ds(r0, SUB), ns:2 * ns]
            for k, s in enumerate((1, 2, 4)):
                xr, xi = _cmul_add(xr, xi, pw_ref[k, 0], pw_ref[k, 1], pltpu.roll(xr, s, 0), pltpu.roll(xi, s, 0))
            xr, xi = _cmul_add(xr, xi, pw_ref[3, 0], pw_ref[3, 1], carry[0], carry[1])
            xs_ref[pl.ds(r0, SUB), 0:ns] = xr
            xs_ref[pl.ds(r0, SUB), ns:2 * ns] = xi
            carry[0] = jnp.broadcast_to(xr[SUB - 1:SUB, :], (SUB, ns))
            carry[1] = jnp.broadcast_to(xi[SUB - 1:SUB, :], (SUB, ns))
            return 0
        lax.fori_loop(0, tm // SUB, group, 0)

        xb = _bf(xs_ref[...])
        xb_ref[...] = xb
        y = jnp.dot(xb, mc_ref[...], preferred_element_type=F32) + d_ref[...] * uv
        y_ref[...] = y
        gy_ref[...] = _gelu(y).astype(gy_ref.dtype)

    c = u.shape[1]
    return pl.pallas_call(
        body, grid=(t // tm,),
        in_specs=[pl.BlockSpec((tm, c), lambda i: (i, 0)), pl.BlockSpec(mb.shape, lambda i: (0, 0)),
                  pl.BlockSpec(mc.shape, lambda i: (0, 0)), pl.BlockSpec(pw.shape, lambda i: (0, 0, 0, 0)),
                  pl.BlockSpec((1, c), lambda i: (0, 0))],
        out_specs=[pl.BlockSpec((tm, c), lambda i: (i, 0)), pl.BlockSpec((tm, c), lambda i: (i, 0)),
                   pl.BlockSpec((tm, 2 * ns), lambda i: (i, 0)), pl.BlockSpec((tm, 2 * ns), lambda i: (i, 0))],
        out_shape=[S((t, c), BF16), S((t, c), F32), S((t, 2 * ns), F32), S((t, 2 * ns), BF16)],
        scratch_shapes=[pltpu.VMEM((2, SUB, ns), F32)],
        compiler_params=_cp("arbitrary"), name=name)(u, mb, mc, pw, dskip)


def s5_bwd(name, dgy, y, u, xs, mct, mbt, qw, dskip):
    t, c = u.shape
    tm = _tile(t, S5_ROWS)
    nt = t // tm
    ns = N_STATE
    ng = tm // SUB

    def body(dgy_ref, y_ref, u_ref, xs_ref, prev_ref, mct_ref, mbt_ref, qw_ref, d_ref,
             du_ref, dy_ref, lb_ref, da_ref, dd_ref, lam, carry):
        step = pl.program_id(0)

        @pl.when(step == 0)
        def _():
            carry[...] = jnp.zeros(carry.shape, F32)
            da_ref[...] = jnp.zeros(da_ref.shape, F32)
            dd_ref[...] = jnp.zeros(dd_ref.shape, F32)

        uv = u_ref[...]
        dy = dgy_ref[...] * _gelu_grad(y_ref[...])
        dyb = _bf(dy)
        dy_ref[...] = dyb
        dd_ref[...] += jnp.sum(dy * uv, axis=0, keepdims=True)
        lam[...] = jnp.dot(dyb, mct_ref[...], preferred_element_type=F32)
        first_tile = (step == nt - 1).astype(F32)
        row0 = lax.broadcasted_iota(jnp.int32, (SUB, ns), 0) == 0

        def group(j, _):
            i = ng - 1 - j
            r0 = pl.multiple_of(i * SUB, SUB)
            lr = lam[pl.ds(r0, SUB), 0:ns]
            li = lam[pl.ds(r0, SUB), ns:2 * ns]
            for k, s in enumerate((1, 2, 4)):
                lr, li = _cmul_add(lr, li, qw_ref[k, 0], qw_ref[k, 1],
                                   pltpu.roll(lr, SUB - s, 0), pltpu.roll(li, SUB - s, 0))
            lr, li = _cmul_add(lr, li, qw_ref[3, 0], qw_ref[3, 1], carry[0], carry[1])
            lam[pl.ds(r0, SUB), 0:ns] = lr
            lam[pl.ds(r0, SUB), ns:2 * ns] = li
            carry[0] = jnp.broadcast_to(lr[0:1, :], (SUB, ns))
            carry[1] = jnp.broadcast_to(li[0:1, :], (SUB, ns))
            rp = pl.multiple_of(jnp.maximum(i - 1, 0) * SUB, SUB)
            in_tile = (i > 0).astype(F32)
            out_tile = (1.0 - in_tile) * (1.0 - first_tile)
            pr = xs_ref[pl.ds(rp, SUB), 0:ns] * in_tile + prev_ref[:, 0:ns] * out_tile
            pi = xs_ref[pl.ds(rp, SUB), ns:2 * ns] * in_tile + prev_ref[:, ns:2 * ns] * out_tile
            xpr = jnp.where(row0, pltpu.roll(pr, 1, 0), pltpu.roll(xs_ref[pl.ds(r0, SUB), 0:ns], 1, 0))
            xpi = jnp.where(row0, pltpu.roll(pi, 1, 0), pltpu.roll(xs_ref[pl.ds(r0, SUB), ns:2 * ns], 1, 0))
            da_ref[0] += lr * xpr + li * xpi
            da_ref[1] += li * xpr - lr * xpi
            return 0
        lax.fori_loop(0, ng, group, 0)

        lb = _bf(lam[...])
        lb_ref[...] = lb
        du_ref[...] = (jnp.dot(lb, mbt_ref[...], preferred_element_type=F32) + d_ref[...] * dy).astype(du_ref.dtype)

    rev = lambda i: (nt - 1 - i, 0)
    prev = lambda i: (jnp.maximum((nt - 1 - i) * (tm // SUB) - 1, 0), 0)
    return pl.pallas_call(
        body, grid=(nt,),
        in_specs=[pl.BlockSpec((tm, c), rev), pl.BlockSpec((tm, c), rev), pl.BlockSpec((tm, c), rev),
                  pl.BlockSpec((tm, 2 * ns), rev), pl.BlockSpec((SUB, 2 * ns), prev),
                  pl.BlockSpec(mct.shape, lambda i: (0, 0)), pl.BlockSpec(mbt.shape, lambda i: (0, 0)),
                  pl.BlockSpec(qw.shape, lambda i: (0, 0, 0, 0)), pl.BlockSpec((1, c), lambda i: (0, 0))],
        out_specs=[pl.BlockSpec((tm, c), rev), pl.BlockSpec((tm, c), rev), pl.BlockSpec((tm, 2 * ns), rev),
                   pl.BlockSpec((2, SUB, ns), lambda i: (0, 0, 0)), pl.BlockSpec((1, c), lambda i: (0, 0))],
        out_shape=[S((t, c), BF16), S((t, c), BF16), S((t, 2 * ns), BF16), S((2, SUB, ns), F32), S((1, c), F32)],
        scratch_shapes=[pltpu.VMEM((tm, 2 * ns), F32), pltpu.VMEM((2, SUB, ns), F32)],
        compiler_params=_cp("arbitrary"), name=name)(dgy, y, u, xs, xs, mct, mbt, qw, dskip)


def _first(accs, *_):
    return [accs[0]]


def _add_res(accs, res):
    return [accs[0] + res]


def even_fwd(x, w):
    t = x.shape[0]
    hn, r = rms_fwd("e_norm_f", x, w["e_norm"])
    (proj,) = mm_nn("e_in_f", t, IN_WIDTH, [(hn, w["e_w_in_t"], 0, "t")], 1, _first, [F32])
    out_a = gmlp_fwd("e_gmlp_f", proj, w["e_gmlp_w"], w["e_gmlp_b"])
    hc = conv_fwd("e_conv_f", proj, w["e_conv_w"], w["e_conv_b"])
    out_b = ln_silu_fwd("e_ln_f", hc, w["e_conv_ln_g"], w["e_conv_ln_b"])
    (x1,) = mm_nn("e_out_f", t, D_MODEL, [(out_a, (w["e_w_out"], 0), 0), (out_b, (w["e_w_out"], 1), 0)],
                  1, _add_res, [F32], tiled=[x])
    return x1, (x, hn, r, proj, out_a, hc, out_b)


def even_bwd(dx, dxb, saved, w):
    x, hn, r, proj, out_a, hc, out_b = saved
    t = x.shape[0]
    (dcat,) = mm_nn("e_out_b", t, D_MODEL, [(dxb, w["e_w_out"], 0, "t")], 1, _first, [F32])
    g_w_out = jnp.concatenate([mm_tn("e_out_wa", out_a, dxb), mm_tn("e_out_wb", out_b, dxb)], axis=0)
    dab, g_gw, g_gb = gmlp_bwd("e_gmlp_b", proj, dcat, w["e_gmlp_w"], w["e_gmlp_b"])
    dhc, g_lg, g_lb = ln_silu_bwd("e_ln_b", hc, dcat, w["e_conv_ln_g"], w["e_conv_ln_b"])
    dba, dbg, g_cw, g_cb = conv_bwd("e_conv_b", proj, dhc, w["e_conv_w"])
    w_in_t = w["e_w_in_t"]
    (dhn,) = mm_nn("e_in_b", t, D_MODEL, [(dab, (w_in_t, 0), 0), (dba, (w_in_t, 2), 0), (dbg, (w_in_t, 3), 0)],
                   1, _first, [F32])
    g_w_in_t = jnp.concatenate([mm_tn("e_in_w0", dab, hn), mm_tn("e_in_w1", dba, hn), mm_tn("e_in_w2", dbg, hn)], axis=0)
    dx0, dx0b, g_norm = rms_bwd("e_norm_b", dhn, x, r, w["e_norm"], dres=dx)
    grads = dict(e_norm=g_norm, e_w_in_t=g_w_in_t, e_gmlp_w=g_gw[None], e_gmlp_b=g_gb.reshape(1, A_GROUPS, GMLP_BLOCK),
                 e_conv_w=g_cw[None], e_conv_b=g_cb, e_conv_ln_g=g_lg, e_conv_ln_b=g_lb, e_w_out=g_w_out)
    return dx0, dx0b, grads


def odd_fwd(x, w, consts):
    t = x.shape[0]
    _, mb, mc, pw, _ = consts
    hn, r = rms_fwd("o_norm_f", x, w["o_norm"])
    (u,) = mm_nn("o_in_f", t, C_WIDTH, [(hn, w["o_w_in"], 0)], 1, _first, [F32])
    gy, y, xs, xsb = s5_fwd("o_s5_f", u, _bf(mb), _bf(mc), pw, w["o_d"])
    w_out_t = w["o_w_out_t"]

    def epi(accs, res):
        return [res + accs[0] * _sigmoid(accs[1]), accs[0], accs[1]]

    x1, o1, o2 = mm_nn("o_out_f", t, D_MODEL, [(gy, (w_out_t, 0), 0, "t"), (gy, (w_out_t, D_MODEL), 1, "t")], 2, epi,
                       [F32, BF16, BF16], tiled=[x])
    return x1, (x, hn, r, u, gy, y, xs, xsb, o1, o2)


def odd_bwd(dx, dxb, saved, w, consts, consts_vjp):
    x, hn, r, u, gy, y, xs, xsb, o1, o2 = saved
    t = x.shape[0]
    _, mb, mc, _, qw = consts

    def gate_bwd(dv, a, b):
        a = a.astype(F32)
        sg = _sigmoid(b.astype(F32))
        return [jnp.concatenate([dv * sg, dv * a * sg * (1.0 - sg)], axis=1)], []

    (do12,) = rows_call("o_gate_b", gate_bwd, [dx, o1, o2], [], [(2 * D_MODEL, BF16)], [])
    (dgy,) = mm_nn("o_out_b", t, C_WIDTH, [(do12, w["o_w_out_t"], 0)], 1, _first, [F32])
    g_w_out_t = mm_tn("o_out_w", do12, gy)
    du, dyb, lamb, da8, g_d = s5_bwd("o_s5_b", dgy, y, u, xs, _bf(mc.T), _bf(mb.T), qw, w["o_d"])
    d_mb = mm_tn("o_s5_wb", u, lamb, out_dtype=F32)
    d_mc = mm_tn("o_s5_wc", xsb, dyb, out_dtype=F32)
    g_lr, g_li, g_dt, g_br, g_bi, g_cr, g_ci = consts_vjp((jnp.sum(da8, axis=1), d_mb, d_mc))
    g_w_in = mm_tn("o_in_w", hn, du)
    (dhn,) = mm_nn("o_in_b", t, D_MODEL, [(du, w["o_w_in"], 0, "t")], 1, _first, [F32])
    dx0, dx0b, g_norm = rms_bwd("o_norm_b", dhn, x, r, w["o_norm"], dres=dx)
    grads = dict(o_norm=g_norm, o_w_in=g_w_in, o_lam_re=g_lr[None], o_lam_im=g_li[None], o_log_dt=g_dt[None],
                 o_b_re=g_br[None], o_b_im=g_bi[None], o_c_re=g_cr[None], o_c_im=g_ci[None], o_d=g_d, o_w_out_t=g_w_out_t)
    return dx0, dx0b, grads


def ca_fwd(i, x, mem, w):
    t, m = x.shape[0], mem.shape[0]
    xn, r = rms_fwd(f"ca{i}_norm_f", x, w["ca_norm"][i:i + 1])
    mn, rm = rms_fwd(f"ca{i}_mnorm_f", mem, w["ca_mem_norm"][i:i + 1])
    (q,) = mm_nn(f"ca{i}_q_f", t, D_MODEL, [(xn, w["ca_wq"][i], 0)], 1, _first, [BF16])
    k, v = mm_nn(f"ca{i}_kv_f", m, D_MODEL, [(mn, w["ca_wk"][i], 0), (mn, w["ca_wv"][i], 1)], 2,
                 lambda accs: [accs[0], accs[1]], [BF16, BF16])
    o = attn_fwd(f"ca{i}_attn_f", q, k, v)
    (x1,) = mm_nn(f"ca{i}_o_f", t, D_MODEL, [(o, w["ca_wo"][i], 0)], 1, _add_res, [F32], tiled=[x])
    return x1, (x, xn, r, mn, rm, q, k, v, o)


def ca_bwd(i, dx, dxb, saved, mem, w):
    x, xn, r, mn, rm, q, k, v, o = saved
    t, m = x.shape[0], mem.shape[0]
    (do,) = mm_nn(f"ca{i}_o_b", t, D_MODEL, [(dxb, w["ca_wo"][i], 0, "t")], 1, _first, [BF16])
    g_wo = mm_tn(f"ca{i}_o_w", o, dxb)
    dq, dk, dv = attn_bwd(f"ca{i}_attn_b", q, k, v, do)
    g_wq = mm_tn(f"ca{i}_q_w", xn, dq)
    g_wk = mm_tn(f"ca{i}_k_w", mn, dk)
    g_wv = mm_tn(f"ca{i}_v_w", mn, dv)
    (dxn,) = mm_nn(f"ca{i}_q_b", t, D_MODEL, [(dq, w["ca_wq"][i], 0, "t")], 1, _first, [F32])
    (dmn,) = mm_nn(f"ca{i}_kv_b", m, D_MODEL, [(dk, w["ca_wk"][i], 0, "t"), (dv, w["ca_wv"][i], 0, "t")], 1, _first, [F32])
    g_mnorm = rms_bwd_gain_only(f"ca{i}_mnorm_b", dmn, mem, rm)
    dx0, dx0b, g_norm = rms_bwd(f"ca{i}_norm_b", dxn, x, r, w["ca_norm"][i:i + 1], dres=dx)
    return dx0, dx0b, dict(ca_norm=g_norm, ca_mem_norm=g_mnorm, ca_wq=g_wq, ca_wk=g_wk, ca_wv=g_wv, ca_wo=g_wo)


def ffn_fwd(i, x, w):
    t = x.shape[0]
    xn, r = rms_fwd(f"ffn{i}_norm_f", x, w["ffn_norm"][i:i + 1])

    def epi(accs):
        g, u = accs
        return [g, u, g * _sigmoid(g) * u]

    g, u, h = mm_nn(f"ffn{i}_up_f", t, FFN_HIDDEN, [(xn, w["ffn_w_gate_t"][i], 0, "t"), (xn, w["ffn_w_up_t"][i], 1, "t")],
                    2, epi, [BF16, BF16, BF16])
    (x1,) = mm_nn(f"ffn{i}_down_f", t, D_MODEL, [(h, w["ffn_w_down"][i], 0)], 1, _add_res, [F32], tiled=[x])
    return x1, (x, xn, r, g, u, h)


def ffn_bwd(i, dx, dxb, saved, w):
    x, xn, r, g, u, h = saved
    t = x.shape[0]

    def epi(accs, gv, uv):
        dh = accs[0]
        gv = gv.astype(F32)
        uv = uv.astype(F32)
        s = _sigmoid(gv)
        return [dh * uv * s * (1.0 + gv * (1.0 - s)), dh * gv * s]

    dg, du = mm_nn(f"ffn{i}_down_b", t, FFN_HIDDEN, [(dxb, w["ffn_w_down"][i], 0, "t")], 1, epi, [BF16, BF16], tiled=[g, u])
    g_wd = mm_tn(f"ffn{i}_down_w", h, dxb)
    g_wg_t = mm_tn(f"ffn{i}_gate_w", dg, xn)
    g_wu_t = mm_tn(f"ffn{i}_up_w", du, xn)
    (dxn,) = mm_nn(f"ffn{i}_up_b", t, D_MODEL, [(dg, w["ffn_w_gate_t"][i], 0), (du, w["ffn_w_up_t"][i], 0)], 1, _first, [F32])
    dx0, dx0b, g_norm = rms_bwd(f"ffn{i}_norm_b", dxn, x, r, w["ffn_norm"][i:i + 1], dres=dx)
    return dx0, dx0b, dict(ffn_norm=g_norm, ffn_w_gate_t=g_wg_t, ffn_w_up_t=g_wu_t, ffn_w_down=g_wd)


_S5_PARAMS = ("o_lam_re", "o_lam_im", "o_log_dt", "o_b_re", "o_b_im", "o_c_re", "o_c_im")


def local_step(x, mem, target, w):
    def consts_fn(*p):
        a, mb, mc = s5_constants(*p)
        return a, mb, mc

    (a, mb, mc), consts_vjp = jax.vjp(consts_fn, *[w[k] for k in _S5_PARAMS])
    consts = (a, mb, mc, _scan_powers(a, False), _scan_powers(a, True))

    x1, s_e = even_fwd(x, w)
    x2, s_c0 = ca_fwd(0, x1, mem, w)
    x3, s_f0 = ffn_fwd(0, x2, w)
    x4, s_o = odd_fwd(x3, w, consts)
    x5, s_c1 = ca_fwd(1, x4, mem, w)
    x6, s_f1 = ffn_fwd(1, x5, w)
    dx, dxb, g_final, loss = final_loss("final_loss", x6, w["final_norm"], target)

    dx, dxb, g_f1 = ffn_bwd(1, dx, dxb, s_f1, w)
    dx, dxb, g_c1 = ca_bwd(1, dx, dxb, s_c1, mem, w)
    dx, dxb, g_o = odd_bwd(dx, dxb, s_o, w, consts, consts_vjp)
    dx, dxb, g_f0 = ffn_bwd(0, dx, dxb, s_f0, w)
    dx, dxb, g_c0 = ca_bwd(0, dx, dxb, s_c0, mem, w)
    dx, dxb, g_e = even_bwd(dx, dxb, s_e, w)

    grads = dict(g_e)
    grads.update(g_o)
    for g0, g1 in ((g_c0, g_c1), (g_f0, g_f1)):
        for k in g0:
            grads[k] = jnp.concatenate([g0[k], g1[k]], axis=0) if k.endswith("norm") else (g0[k], g1[k])
    grads["final_norm"] = g_final
    return loss, dx, grads


def _group(axes):
    pos = {a: lax.axis_index(a) for a in ("x", "y", "c")}
    me = 0
    for a in axes:
        me = me * 2 + pos[a]
    peers = []
    for mask in range(1, 2 ** len(axes)):
        peer = dict(pos)
        for bit, a in enumerate(axes):
            if (mask >> (len(axes) - 1 - bit)) & 1:
                peer[a] = 1 - pos[a]
        idx = 0
        for a in axes:
            idx = idx * 2 + peer[a]
        peers.append((idx, (peer["x"], peer["y"], peer["c"])))
    return me, peers


def _sibling():
    x, y, c = lax.axis_index("x"), lax.axis_index("y"), lax.axis_index("c")
    return c, (x, y, 1 - c)


def gather_ici(name, blks):
    k_ops = len(blks)
    out_shape = [S((4, 2) + tuple(b.shape), b.dtype) for b in blks]

    def body(*refs):
        in_refs, out_refs = refs[:k_ops], refs[k_ops:2 * k_ops]
        send_sems, recv_sems, local_sems = refs[2 * k_ops:]
        me, peers = _group(("x", "y"))
        core = lax.axis_index("c")
        local, sent, landed = [], [], []
        for i in range(k_ops):
            cp = pltpu.make_async_copy(in_refs[i], out_refs[i].at[me, core], local_sems.at[i])
            cp.start()
            local.append(cp)
        for k, (idx, dev) in enumerate(peers):
            for i in range(k_ops):
                s = i * 3 + k
                cp = pltpu.make_async_remote_copy(src_ref=in_refs[i], dst_ref=out_refs[i].at[me, core], send_sem=send_sems.at[s],
                                                  recv_sem=recv_sems.at[s], device_id=dev, device_id_type=MESH)
                cp.start()
                sent.append(cp)
                landed.append(pltpu.make_async_remote_copy(src_ref=in_refs[i], dst_ref=out_refs[i].at[idx, core],
                                                           send_sem=send_sems.at[s], recv_sem=recv_sems.at[s],
                                                           device_id=dev, device_id_type=MESH))
        for cp in landed:
            cp.wait_recv()
        for cp in sent:
            cp.wait_send()
        for cp in local:
            cp.wait()

    return pl.pallas_call(
        body, in_specs=[ANY] * k_ops, out_specs=[ANY] * k_ops, out_shape=out_shape,
        scratch_shapes=[pltpu.SemaphoreType.DMA((k_ops * 3,)), pltpu.SemaphoreType.DMA((k_ops * 3,)),
                        pltpu.SemaphoreType.DMA((k_ops,))],
        name=name)(*blks)


def gather_d2d(name, bufs):
    k_ops = len(bufs)

    def body(*refs):
        in_refs, out_refs = refs[:k_ops], refs[k_ops:2 * k_ops]
        send_sems, recv_sems = refs[2 * k_ops:]
        core, sib = _sibling()
        sent, landed = [], []
        for i in range(k_ops):
            cp = pltpu.make_async_remote_copy(src_ref=in_refs[i].at[:, core], dst_ref=out_refs[i].at[:, core],
                                              send_sem=send_sems.at[i], recv_sem=recv_sems.at[i], device_id=sib, device_id_type=MESH)
            cp.start()
            sent.append(cp)
            landed.append(pltpu.make_async_remote_copy(src_ref=in_refs[i].at[:, core], dst_ref=out_refs[i].at[:, 1 - core],
                                                       send_sem=send_sems.at[i], recv_sem=recv_sems.at[i],
                                                       device_id=sib, device_id_type=MESH))
        for cp in landed:
            cp.wait_recv()
        for cp in sent:
            cp.wait_send()

    return pl.pallas_call(
        body, in_specs=[ANY] * k_ops, out_specs=[ANY] * k_ops, out_shape=[S(b.shape, b.dtype) for b in bufs],
        input_output_aliases={i: i for i in range(k_ops)},
        scratch_shapes=[pltpu.SemaphoreType.DMA((k_ops,)), pltpu.SemaphoreType.DMA((k_ops,))],
        name=name)(*bufs)


def all_gather(name, blks):
    bufs = gather_d2d(name + "_d2d", gather_ici(name + "_ici", blks))
    return [p.reshape((N_DEV * b.shape[0],) + tuple(b.shape[1:])) for p, b in zip(bufs, blks)]


def scatter_d2d(name, pack, chunks):
    q, _, rows, c = pack.shape
    rc = rows // chunks

    def body(in_ref, out_ref, send_sems, recv_sems):
        core, sib = _sibling()
        sent = []
        for ch in range(chunks):
            rs = pl.ds(ch * rc, rc)
            cp = pltpu.make_async_remote_copy(src_ref=in_ref.at[:, 1 - core, rs], dst_ref=out_ref.at[:, rs],
                                              send_sem=send_sems.at[ch], recv_sem=recv_sems.at[ch], device_id=sib, device_id_type=MESH)
            cp.start()
            sent.append(cp)
        for cp in sent:
            cp.wait_recv()
        for cp in sent:
            cp.wait_send()

    return pl.pallas_call(
        body, in_specs=[ANY], out_specs=ANY, out_shape=S((q, rows, c), pack.dtype),
        scratch_shapes=[pltpu.SemaphoreType.DMA((chunks,)), pltpu.SemaphoreType.DMA((chunks,))],
        name=name)(pack)


def scatter_ici(name, arr):
    def body(in_ref, out_ref, send_sems, recv_sems):
        me, peers = _group(("x", "y"))
        sent, landed = [], []
        for k, (idx, dev) in enumerate(peers):
            cp = pltpu.make_async_remote_copy(src_ref=in_ref.at[idx], dst_ref=out_ref.at[me], send_sem=send_sems.at[k],
                                              recv_sem=recv_sems.at[k], device_id=dev, device_id_type=MESH)
            cp.start()
            sent.append(cp)
            landed.append(pltpu.make_async_remote_copy(src_ref=in_ref.at[idx], dst_ref=out_ref.at[idx], send_sem=send_sems.at[k],
                                                       recv_sem=recv_sems.at[k], device_id=dev, device_id_type=MESH))
        for cp in landed:
            cp.wait_recv()
        for cp in sent:
            cp.wait_send()

    return pl.pallas_call(
        body, in_specs=[ANY], out_specs=ANY, out_shape=S(arr.shape, arr.dtype),
        scratch_shapes=[pltpu.SemaphoreType.DMA((3,)), pltpu.SemaphoreType.DMA((3,))],
        name=name)(arr)


def sum_pair(name, pack, recv, core, tr=256):
    q, rows, c = recv.shape
    tr = _tile(rows, tr)

    def body(core_ref, a_ref, b_ref, o_ref):
        o_ref[...] = (a_ref[...].astype(F32) + b_ref[...].astype(F32)).astype(o_ref.dtype)

    spec = pltpu.PrefetchScalarGridSpec(
        num_scalar_prefetch=1, grid=(q, rows // tr),
        in_specs=[pl.BlockSpec((None, None, tr, c), lambda j, i, core: (j, core[0], i, 0)),
                  pl.BlockSpec((None, tr, c), lambda j, i, core: (j, i, 0))],
        out_specs=pl.BlockSpec((None, tr, c), lambda j, i, core: (j, i, 0)))
    return pl.pallas_call(body, grid_spec=spec, out_shape=S(recv.shape, recv.dtype),
                          compiler_params=_cp("parallel", "parallel"), name=name)(core, pack, recv)


def sum_quad(name, own, recv, chip, tr=256):
    _, rows, c = recv.shape
    tr = _tile(rows, tr)

    def body(chip_ref, a_ref, r1_ref, r2_ref, r3_ref, o_ref):
        o_ref[...] = ((a_ref[...].astype(F32) + r1_ref[...].astype(F32)) + r2_ref[...].astype(F32)) + r3_ref[...].astype(F32)

    def slot(mask):
        return pl.BlockSpec((None, tr, c), lambda i, chip, mask=mask: (jnp.bitwise_xor(chip[0], mask), i, 0))

    spec = pltpu.PrefetchScalarGridSpec(
        num_scalar_prefetch=1, grid=(rows // tr,), in_specs=[slot(0), slot(1), slot(2), slot(3)],
        out_specs=pl.BlockSpec((tr, c), lambda i, chip: (i, 0)))
    return pl.pallas_call(body, grid_spec=spec, out_shape=S((rows, c), F32),
                          compiler_params=_cp("parallel"), name=name)(chip, own, recv, recv, recv)


def adamw_native(name, g, w, m, v, tr=512):
    shape = w.shape
    cols = shape[-1]
    rows = w.size // cols
    tr = _tile(rows, tr) if rows % 8 == 0 else rows
    c1 = 1.0 - ADAM_B1 ** ADAM_STEP
    c2 = 1.0 - ADAM_B2 ** ADAM_STEP

    def body(g_ref, w_ref, m_ref, v_ref, d_ref, m2_ref, v2_ref):
        gv = g_ref[...]
        m2 = ADAM_B1 * m_ref[...] + (1.0 - ADAM_B1) * gv
        v2 = ADAM_B2 * v_ref[...] + (1.0 - ADAM_B2) * (gv * gv)
        m2_ref[...] = m2
        v2_ref[...] = v2
        d_ref[...] = -ADAM_LR * ((m2 / c1) / (jnp.sqrt(v2 / c2) + ADAM_EPS) + ADAM_WD * w_ref[...])

    row = pl.BlockSpec((tr, cols), lambda i: (i, 0))
    outs = pl.pallas_call(body, grid=(rows // tr,), in_specs=[row] * 4, out_specs=[row] * 3,
                          out_shape=[S((rows, cols), F32)] * 3, compiler_params=_cp("parallel"),
                          name=name)(*[a.reshape(rows, cols) for a in (g, w, m, v)])
    return tuple(o.reshape(shape) for o in outs)


def adamw_call(name, slots, w, m, v, tr=1024):
    n, r, c = slots.shape
    tr = _tile(r, tr)
    c1 = 1.0 - ADAM_B1 ** ADAM_STEP
    c2 = 1.0 - ADAM_B2 ** ADAM_STEP

    def body(s_ref, w_ref, m_ref, v_ref, g_ref, d_ref, m2_ref, v2_ref):
        g = s_ref[0].astype(F32)
        for j in range(1, n):
            g = g + s_ref[j].astype(F32)
        m2 = ADAM_B1 * m_ref[...] + (1.0 - ADAM_B1) * g
        v2 = ADAM_B2 * v_ref[...] + (1.0 - ADAM_B2) * (g * g)
        g_ref[...] = g
        m2_ref[...] = m2
        v2_ref[...] = v2
        d_ref[...] = -ADAM_LR * ((m2 / c1) / (jnp.sqrt(v2 / c2) + ADAM_EPS) + ADAM_WD * w_ref[...])

    row = pl.BlockSpec((tr, c), lambda i: (i, 0))
    return pl.pallas_call(body, grid=(r // tr,), in_specs=[pl.BlockSpec((n, tr, c), lambda i: (0, i, 0)), row, row, row],
                          out_specs=[row, row, row, row], out_shape=[S((r, c), F32)] * 4,
                          compiler_params=_cp("parallel"), name=name)(slots, w, m, v)


_REPLICATED = ("e_norm", "e_gmlp_w", "e_gmlp_b", "e_conv_b", "e_conv_ln_g", "e_conv_ln_b", "o_lam_re", "o_lam_im", "o_log_dt",
               "o_b_re", "o_b_im", "o_c_re", "o_c_im", "ca_norm", "ca_mem_norm", "ffn_norm", "final_norm")
_ORDER = ("e_norm", "e_w_in", "e_gmlp_w", "e_gmlp_b", "e_conv_w", "e_conv_b", "e_conv_ln_g", "e_conv_ln_b", "e_w_out",
          "o_norm", "o_w_in", "o_lam_re", "o_lam_im", "o_log_dt", "o_b_re", "o_b_im", "o_c_re", "o_c_im", "o_d", "o_w_out",
          "ca_norm", "ca_mem_norm", "ca_wq", "ca_wk", "ca_wv", "ca_wo", "ffn_norm", "ffn_w_gate", "ffn_w_up", "ffn_w_down",
          "final_norm")


def _rows128(a, multiple=8):
    flat = a.reshape(-1)
    rows = -(-flat.shape[0] // (LANES * multiple)) * multiple
    return jnp.pad(flat, (0, rows * LANES - flat.shape[0])).reshape(rows, LANES)


def _shard(full, axis):
    s = full.shape
    return jnp.moveaxis(full.reshape(s[:axis] + (N_DEV, s[axis] // N_DEV) + s[axis + 1:]), axis, 0)


_UNITS = (("e_w_in", 0, True), ("e_w_out", 0, False), ("o_w_in", 0, False), ("o_w_out", 0, True),
          *[(n, i, False) for n in ("ca_wq", "ca_wk", "ca_wv", "ca_wo") for i in (0, 1)],
          *[(n, i, tr) for n, tr in (("ffn_w_gate", True), ("ffn_w_up", True), ("ffn_w_down", False)) for i in (0, 1)])
_LAYERED = ("ca_wq", "ca_wk", "ca_wv", "ca_wo", "ffn_w_gate", "ffn_w_up", "ffn_w_down")
_SMALL_SHARDED = (("e_conv_w", 2), ("o_norm", 1), ("o_d", 1))
RS_ROW = 1024
RS_ROWS = 3840
RS_CHUNKS = 8


def _unit_key(name, tr):
    return name + "_t" if tr else name


def gather_weights(local):
    blks = []
    for name, layer, tr in _UNITS:
        blk = local[name][layer]
        blks.append(_bf(blk.T if tr else blk))
    small = jnp.concatenate([local[name].reshape(-1) for name, _ in _SMALL_SHARDED])
    blks.append(_rows128(small))
    full = all_gather("ag_w", blks)
    w = {}
    for (name, layer, tr), arr in zip(_UNITS, full[:-1]):
        key = _unit_key(name, tr)
        w[key] = w.get(key, ()) + (arr,) if name in _LAYERED else arr
    flat = full[-1].reshape(N_DEV, -1)
    off = 0
    for name, axis in _SMALL_SHARDED:
        blk = local[name]
        seg = flat[:, off:off + blk.size].reshape((N_DEV,) + blk.shape)
        off += blk.size
        seg = jnp.moveaxis(seg, 0, axis)
        w[name] = seg.reshape(seg.shape[:axis] + (-1,) + seg.shape[axis + 2:])
    return w


def reduce_sharded(grads, local, mom, var):
    parts, spans = [], []
    for name, layer, tr in _UNITS:
        g = grads[_unit_key(name, tr)]
        g = g[layer] if name in _LAYERED else g
        part = g.reshape(4, 2, -1, RS_ROW)
        spans.append((part.shape[2], g.shape[0] // N_DEV, g.shape[1]))
        parts.append(part)
    small = jnp.concatenate([_shard(grads[name], axis).reshape(N_DEV, -1) for name, axis in _SMALL_SHARDED], axis=1)
    n_small = small.shape[1]
    small_rows = 16
    parts.append(jnp.pad(small, ((0, 0), (0, small_rows * RS_ROW - n_small))).astype(BF16).reshape(4, 2, small_rows, RS_ROW))
    used = sum(p.shape[2] for p in parts)
    parts.append(jnp.zeros((4, 2, RS_ROWS - used, RS_ROW), BF16))
    pack = jnp.concatenate(parts, axis=2)
    core = lax.axis_index("c").astype(jnp.int32).reshape(1)
    chip = (2 * lax.axis_index("x") + lax.axis_index("y")).astype(jnp.int32).reshape(1)
    from_sibling = scatter_d2d("rs_g_d2d", pack, RS_CHUNKS)
    chip_sum = sum_pair("rs_g_sum2", pack, from_sibling, core)
    from_chips = scatter_ici("rs_g_ici", chip_sum)
    total = sum_quad("rs_g_sum4", chip_sum, from_chips, chip)

    res, off, per_layer = {}, 0, {}
    for (name, layer, tr), (rows, r, c) in zip(_UNITS, spans):
        g = total[off:off + rows].reshape(r, c)
        off += rows
        per_layer.setdefault(name, []).append(g.T if tr else g)
    for name, gs in per_layer.items():
        g = jnp.stack(gs) if name in _LAYERED else gs[0][None]
        res[name] = (g,) + adamw_native("adamw_" + name, g, local[name], mom[name], var[name])
    flat = total[off:off + small_rows].reshape(-1)
    off = 0
    for name, _ in _SMALL_SHARDED:
        blk = local[name]
        g = flat[off:off + blk.size].reshape(blk.shape)
        off += blk.size
        res[name] = (g,) + adamw_native("adamw_" + name, g, blk, mom[name], var[name])
    return res


def reduce_replicated(grads, loss, w, mom, var):
    def pack(src, last):
        return jnp.concatenate([_rows128(src[name]) for name in _REPLICATED] + [_rows128(last)], axis=0)

    (slots,) = all_gather("ag_g", [pack(grads, loss)])
    zero = jnp.zeros((1, 1), F32)
    rows = slots.shape[0] // N_DEV
    outs = adamw_call("adamw_replicated", slots.reshape(N_DEV, rows, LANES), pack(w, zero), pack(mom, zero), pack(var, zero),
                      tr=rows)
    res, off = {}, 0
    for name in _REPLICATED:
        n = w[name].size
        nr = _rows128(w[name]).shape[0]
        res[name] = tuple(o[off:off + nr].reshape(-1)[:n].reshape(w[name].shape) for o in outs)
        off += nr
    return res, outs[0][off, 0]


def kernel(x, mem, e_norm, e_w_in, e_gmlp_w, e_gmlp_b, e_conv_w, e_conv_b, e_conv_ln_g, e_conv_ln_b, e_w_out, o_norm, o_w_in, o_lam_re, o_lam_im, o_log_dt, o_b_re, o_b_im, o_c_re, o_c_im, o_d, o_w_out, ca_norm, ca_mem_norm, ca_wq, ca_wk, ca_wv, ca_wo, ffn_norm, ffn_w_gate, ffn_w_up, ffn_w_down, final_norm, loss_target, m_e_norm, m_e_w_in, m_e_gmlp_w, m_e_gmlp_b, m_e_conv_w, m_e_conv_b, m_e_conv_ln_g, m_e_conv_ln_b, m_e_w_out, m_o_norm, m_o_w_in, m_o_lam_re, m_o_lam_im, m_o_log_dt, m_o_b_re, m_o_b_im, m_o_c_re, m_o_c_im, m_o_d, m_o_w_out, m_ca_norm, m_ca_mem_norm, m_ca_wq, m_ca_wk, m_ca_wv, m_ca_wo, m_ffn_norm, m_ffn_w_gate, m_ffn_w_up, m_ffn_w_down, m_final_norm, v_e_norm, v_e_w_in, v_e_gmlp_w, v_e_gmlp_b, v_e_conv_w, v_e_conv_b, v_e_conv_ln_g, v_e_conv_ln_b, v_e_w_out, v_o_norm, v_o_w_in, v_o_lam_re, v_o_lam_im, v_o_log_dt, v_o_b_re, v_o_b_im, v_o_c_re, v_o_c_im, v_o_d, v_o_w_out, v_ca_norm, v_ca_mem_norm, v_ca_wq, v_ca_wk, v_ca_wv, v_ca_wo, v_ffn_norm, v_ffn_w_gate, v_ffn_w_up, v_ffn_w_down, v_final_norm):
    given = dict(locals())
    local = {k: given[k] for k in _ORDER}
    mom = {k: given["m_" + k] for k in _ORDER}
    var = {k: given["v_" + k] for k in _ORDER}

    w = gather_weights(local)
    w["e_conv_w"] = w["e_conv_w"][0]
    w.update({
        "e_norm": e_norm, "e_gmlp_w": e_gmlp_w[0], "e_gmlp_b": e_gmlp_b.reshape(A_GROUPS, GMLP_BLOCK, 1),
        "e_conv_b": e_conv_b, "e_conv_ln_g": e_conv_ln_g, "e_conv_ln_b": e_conv_ln_b,
        "o_lam_re": o_lam_re[0], "o_lam_im": o_lam_im[0], "o_log_dt": o_log_dt[0], "o_b_re": o_b_re[0], "o_b_im": o_b_im[0],
        "o_c_re": o_c_re[0], "o_c_im": o_c_im[0], "ca_norm": ca_norm, "ca_mem_norm": ca_mem_norm, "ffn_norm": ffn_norm,
        "final_norm": final_norm.reshape(1, D_MODEL),
    })
    loss_part, grad_x, grads = local_step(x[0], mem[0], loss_target[0], w)
    grads["final_norm"] = grads["final_norm"].reshape(D_MODEL)

    res = reduce_sharded(grads, local, mom, var)
    rep, loss = reduce_replicated(grads, loss_part, local, mom, var)
    res.update(rep)
    return (loss, grad_x[None], *[res[k][0] for k in _ORDER], *[res[k][1] for k in _ORDER],
            *[res[k][2] for k in _ORDER], *[res[k][3] for k in _ORDER])
```

```python
import jax
import jax.numpy as jnp
from jax import lax
from jax.experimental import pallas as pl
from jax.experimental.pallas import tpu as pltpu

F32 = jnp.float32
BF16 = jnp.bfloat16
S = jax.ShapeDtypeStruct

D_MODEL = 1024
A_WIDTH = 512
A_GROUPS = 4
GMLP_BLOCK = 128
CHUNK = 64
B_WIDTH = 512
IN_WIDTH = 2 * A_WIDTH + 2 * B_WIDTH
CONV_WIDTH = 31
CONV_PAD = 32
C_WIDTH = 512
C_GROUP_CH = 16
C_GROUPS = 32
C_STATE = 64
N_STATE = C_GROUPS * C_STATE
CA_HEADS = 4
CA_HEAD_DIM = 256
FFN_HIDDEN = 2816
EPS = 1e-6
ADAM_LR = 0.001
ADAM_B1 = 0.9
ADAM_B2 = 0.999
ADAM_EPS = 1e-08
ADAM_WD = 0.01
ADAM_STEP = 10
N_DEV = 8
LANES = 128
VMEM_LIMIT = 56 << 20
VMEM_BUDGET = 40 << 20
MM_TN_RESIDENT = 8 << 20
MESH = pl.DeviceIdType.MESH
ANY = pl.BlockSpec(memory_space=pl.ANY)


def _cp(*sem):
    return pltpu.CompilerParams(dimension_semantics=sem, vmem_limit_bytes=VMEM_LIMIT)


def _tile(n, pref):
    t = pref
    while n % t:
        t //= 2
    return t


def _bf(v):
    return v if v.dtype == BF16 else v.astype(BF16)


def _sigmoid(x):
    return 1.0 / (1.0 + jnp.exp(-x))


_GC = 0.7978845608028654


def _gelu(x):
    return 0.5 * x * (1.0 + jnp.tanh(_GC * (x + 0.044715 * x * x * x)))


def _gelu_grad(x):
    x2 = x * x
    t = jnp.tanh(_GC * (x + 0.044715 * x * x2))
    return 0.5 * (1.0 + t) + 0.5 * x * (1.0 - t * t) * _GC * (1.0 + 3.0 * 0.044715 * x2)


def _tspec(entry, tm):
    if isinstance(entry, tuple):
        arr, cb, width = entry
        return arr, pl.BlockSpec((tm, width), lambda i, cb=cb: (i, cb))
    return entry, pl.BlockSpec((tm, entry.shape[1]), lambda i: (i, 0))


def rows_call(name, fn, tiled, full, outs, accs, tm=256):
    pairs = [_tspec(e, tm) for e in tiled]
    arrs = [p[0] for p in pairs]
    rows = arrs[0].shape[0]
    tm = _tile(rows, tm)
    pairs = [_tspec(e, tm) for e in tiled]
    n_in = len(tiled) + len(full)
    n_out = len(outs)

    def body(*refs):
        vals = [r[...] for r in refs[:n_in]]
        o_refs = refs[n_in:n_in + n_out]
        a_refs = refs[n_in + n_out:]
        ov, av = fn(*vals)
        for r, v in zip(o_refs, ov):
            r[...] = v.astype(r.dtype)
        if a_refs:
            @pl.when(pl.program_id(0) == 0)
            def _():
                for r in a_refs:
                    r[...] = jnp.zeros(r.shape, r.dtype)
            for r, v in zip(a_refs, av):
                r[...] += v

    in_specs = [p[1] for p in pairs] + [pl.BlockSpec(a.shape, lambda i, nd=a.ndim: (0,) * nd) for a in full]
    out_specs = [pl.BlockSpec((tm, c), lambda i: (i, 0)) for c, _ in outs]
    out_specs += [pl.BlockSpec(s, lambda i, nd=len(s): (0,) * nd) for s in accs]
    out_shape = [S((rows, c), dt) for c, dt in outs] + [S(s, F32) for s in accs]
    return pl.pallas_call(body, grid=(rows // tm,), in_specs=in_specs, out_specs=out_specs, out_shape=out_shape,
                          compiler_params=_cp("arbitrary"), name=name)(*arrs, *full)


def mm_nn(name, m, n, pairs, n_acc, epi, outs, tiled=(), cols=(), rowv=()):
    a_ops, a_slot, b_arrs, b_specs, idx, trans = [], [], [], [], [], []
    fixed = 0
    for pair in pairs:
        a, b, k = pair[:3]
        bt = len(pair) > 3
        arr, cb, kdim = a if isinstance(a, tuple) else (a, 0, a.shape[1])
        key = (id(arr), cb, kdim)
        if key not in [o[0] for o in a_ops]:
            a_ops.append((key, arr, cb, kdim))
        a_slot.append([o[0] for o in a_ops].index(key))
        b_arr, off = b if isinstance(b, tuple) else (b, 0)
        b_arrs.append(b_arr)
        if bt:
            assert off % n == 0 and b_arr.shape[1] == kdim
            b_specs.append(pl.BlockSpec((n, kdim), lambda i, o=off // n: (o, 0), pipeline_mode=pl.Buffered(1)))
        else:
            assert b_arr.shape[1] == n
            b_specs.append(pl.BlockSpec((kdim, n), lambda i, o=off: (o, 0), pipeline_mode=pl.Buffered(1)))
        fixed += kdim * n * b_arr.dtype.itemsize
        idx.append(k)
        trans.append(bt)
    per_row = sum(2 * kdim * arr.dtype.itemsize for _, arr, _, kdim in a_ops)
    per_row += sum(2 * n * t.dtype.itemsize for t in tiled) + sum(2 * n * jnp.dtype(dt).itemsize for dt in outs)
    per_row += (n_acc + 3) * n * 4
    tm = next((t for t in (1024, 512, 256, 128) if m % t == 0 and fixed + t * per_row <= VMEM_BUDGET), _tile(m, 128))
    n_a, n_p = len(a_ops), len(pairs)
    n_in = n_a + n_p + len(tiled) + len(cols) + len(rowv)

    def body(*refs):
        a_vals = [_bf(r[...]) for r in refs[:n_a]]
        accs = [None] * n_acc
        for p in range(n_p):
            av, bv = a_vals[a_slot[p]], _bf(refs[n_a + p][...])
            if trans[p]:
                d = lax.dot_general(av, bv, (((1,), (1,)), ((), ())), preferred_element_type=F32)
            else:
                d = jnp.dot(av, bv, preferred_element_type=F32)
            accs[idx[p]] = d if accs[idx[p]] is None else accs[idx[p]] + d
        extra = [r[...] for r in refs[n_a + n_p:n_in]]
        ov = epi(accs, *extra)
        for r, v in zip(refs[n_in:], ov):
            r[...] = v.astype(r.dtype)

    in_specs = [pl.BlockSpec((tm, kdim), lambda i, cb=cb: (i, cb)) for _, _, cb, kdim in a_ops] + b_specs
    in_specs += [pl.BlockSpec((tm, n), lambda i: (i, 0)) for _ in tiled]
    in_specs += [pl.BlockSpec((tm, 1), lambda i: (i, 0)) for _ in cols]
    in_specs += [pl.BlockSpec((1, n), lambda i: (0, 0)) for _ in rowv]
    out_specs = [pl.BlockSpec((tm, n), lambda i: (i, 0)) for _ in outs]
    out_shape = [S((m, n), dt) for dt in outs]
    return pl.pallas_call(body, grid=(m // tm,), in_specs=in_specs, out_specs=out_specs, out_shape=out_shape,
                          compiler_params=_cp("parallel"), name=name)(*[o[1] for o in a_ops], *b_arrs, *tiled, *cols, *rowv)


def mm_tn(name, a, b, out_dtype=BF16):
    if isinstance(a, tuple):
        a_arr, a_cb, m = a
    else:
        a_arr, a_cb, m = a, None, a.shape[1]
    if isinstance(b, tuple):
        b_arr, b_cb, n = b
    else:
        b_arr, b_cb, n = b, None, b.shape[1]
    t = a_arr.shape[0]
    whole_b = t * n * b_arr.dtype.itemsize <= MM_TN_RESIDENT and b_cb is None
    tn = n if whole_b else _tile(n, 512)
    tm = _tile(m, 512 if t * 512 * a_arr.dtype.itemsize * 2 + t * tn * b_arr.dtype.itemsize * 2 <= VMEM_BUDGET else 256)
    a_off = 0 if a_cb is None else a_cb * (m // tm)
    b_off = 0 if b_cb is None else b_cb * (n // tn)

    def body(a_ref, b_ref, o_ref):
        o_ref[...] = lax.dot_general(_bf(a_ref[...]), _bf(b_ref[...]), (((0,), (0,)), ((), ())),
                                     preferred_element_type=F32).astype(o_ref.dtype)

    if whole_b:
        b_spec = pl.BlockSpec((t, n), lambda i, j: (0, 0), pipeline_mode=pl.Buffered(1))
    else:
        b_spec = pl.BlockSpec((t, tn), lambda i, j: (0, j + b_off))
    return pl.pallas_call(
        body, grid=(m // tm, n // tn),
        in_specs=[pl.BlockSpec((t, tm), lambda i, j: (0, i + a_off)), b_spec],
        out_specs=pl.BlockSpec((tm, tn), lambda i, j: (i, j)), out_shape=S((m, n), out_dtype),
        compiler_params=_cp("parallel", "parallel"), name=name)(a_arr, b_arr)


def rms_fwd(name, x, gain):
    def fn(xv, g):
        r = lax.rsqrt(jnp.mean(xv * xv, axis=-1, keepdims=True) + EPS)
        return [xv * r * g, r], []
    return rows_call(name, fn, [x], [gain], [(x.shape[1], BF16), (1, F32)], [])


def rms_bwd(name, dxn, x, r, gain, dres=None):
    d = x.shape[1]

    def fn(*vals):
        if dres is None:
            dv, xv, rv, g = vals
            base = 0.0
        else:
            dv, xv, rv, base, g = vals
        w = dv * g
        xh = xv * rv
        dx = base + rv * (w - xh * jnp.mean(w * xh, axis=-1, keepdims=True))
        return [dx, dx], [jnp.sum(dv * xh, axis=0, keepdims=True)]

    tiled = [dxn, x, r] + ([] if dres is None else [dres])
    return rows_call(name, fn, tiled, [gain], [(d, F32), (d, BF16)], [(1, d)])


def rms_bwd_gain_only(name, dxn, x, r):
    def fn(dv, xv, rv):
        return [], [jnp.sum(dv * xv * rv, axis=0, keepdims=True)]
    return rows_call(name, fn, [dxn, x, r], [], [], [(1, x.shape[1])])[0]


def final_loss(name, x, gain, target):
    d = x.shape[1]

    def fn(xv, tv, g):
        r = lax.rsqrt(jnp.mean(xv * xv, axis=-1, keepdims=True) + EPS)
        xh = xv * r
        err = xh * g - tv
        dy = err * (1.0 / d)
        w = dy * g
        dx = r * (w - xh * jnp.mean(w * xh, axis=-1, keepdims=True))
        part = jnp.sum(jnp.sum(err * err, axis=-1, keepdims=True), axis=0, keepdims=True) * (0.5 / d)
        return [dx, dx], [jnp.sum(dy * xh, axis=0, keepdims=True), part]

    return rows_call(name, fn, [x, target], [gain], [(d, F32), (d, BF16)], [(1, d), (1, 1)])


def _gmlp_mask():
    row = lax.broadcasted_iota(jnp.int32, (GMLP_BLOCK, GMLP_BLOCK), 0) // CHUNK
    col = lax.broadcasted_iota(jnp.int32, (GMLP_BLOCK, GMLP_BLOCK), 1) // CHUNK
    return col <= row


def _ln_plain(v):
    mu = jnp.mean(v, axis=-1, keepdims=True)
    vc = v - mu
    rstd = lax.rsqrt(jnp.mean(vc * vc, axis=-1, keepdims=True) + EPS)
    return vc * rstd, rstd


def gmlp_fwd(name, proj, w, b, tm=512):
    t = proj.shape[0]
    tm = _tile(t, tm)

    def body(au_ref, av_ref, w_ref, b_ref, o_ref):
        mask = _gmlp_mask()
        u = _gelu(au_ref[...])
        vn, _ = _ln_plain(_gelu(av_ref[...]))
        vnb = _bf(vn)
        for g in range(A_GROUPS):
            wg = _bf(jnp.where(mask, w_ref[g], 0.0))
            cs = slice(g * GMLP_BLOCK, (g + 1) * GMLP_BLOCK)
            for n in range(tm // GMLP_BLOCK):
                rs = slice(n * GMLP_BLOCK, (n + 1) * GMLP_BLOCK)
                sg = jnp.dot(wg, vnb[rs, cs], preferred_element_type=F32) + b_ref[g]
                o_ref[rs, cs] = (u[rs, cs] * sg).astype(o_ref.dtype)

    return pl.pallas_call(
        body, grid=(t // tm,),
        in_specs=[pl.BlockSpec((tm, A_WIDTH), lambda i: (i, 0)), pl.BlockSpec((tm, A_WIDTH), lambda i: (i, 1)),
                  pl.BlockSpec(w.shape, lambda i: (0, 0, 0)), pl.BlockSpec(b.shape, lambda i: (0, 0, 0))],
        out_specs=pl.BlockSpec((tm, A_WIDTH), lambda i: (i, 0)), out_shape=S((t, A_WIDTH), BF16),
        compiler_params=_cp("parallel"), name=name)(proj, proj, w, b)


def gmlp_bwd(name, proj, dcat, w, b, tm=512):
    t = proj.shape[0]
    tm = _tile(t, tm)

    def body(au_ref, av_ref, do_ref, w_ref, b_ref, dp_ref, dw_ref, db_ref):
        @pl.when(pl.program_id(0) == 0)
        def _():
            dw_ref[...] = jnp.zeros(dw_ref.shape, F32)
            db_ref[...] = jnp.zeros(db_ref.shape, F32)

        mask = _gmlp_mask()
        au = au_ref[...]
        av = av_ref[...]
        u = _gelu(au)
        vn, rstd = _ln_plain(_gelu(av))
        vnb = _bf(vn)
        dout = do_ref[...]
        dvn_cols = []
        for g in range(A_GROUPS):
            wm = jnp.where(mask, w_ref[g], 0.0)
            wg = _bf(wm)
            wgt = _bf(wm.T)
            cs = slice(g * GMLP_BLOCK, (g + 1) * GMLP_BLOCK)
            dwg = jnp.zeros((GMLP_BLOCK, GMLP_BLOCK), F32)
            dbg = jnp.zeros((GMLP_BLOCK, 1), F32)
            dvn_rows = []
            for n in range(tm // GMLP_BLOCK):
                rs = slice(n * GMLP_BLOCK, (n + 1) * GMLP_BLOCK)
                sg = jnp.dot(wg, vnb[rs, cs], preferred_element_type=F32) + b_ref[g]
                dp_ref[rs, cs] = (dout[rs, cs] * sg * _gelu_grad(au[rs, cs])).astype(dp_ref.dtype)
                dsg = dout[rs, cs] * u[rs, cs]
                dsgb = _bf(dsg)
                dbg = dbg + jnp.sum(dsg, axis=1, keepdims=True)
                dwg = dwg + lax.dot_general(dsgb, vnb[rs, cs], (((1,), (1,)), ((), ())), preferred_element_type=F32)
                dvn_rows.append(jnp.dot(wgt, dsgb, preferred_element_type=F32))
            dw_ref[g] += jnp.where(mask, dwg, 0.0)
            db_ref[g] += dbg
            dvn_cols.append(jnp.concatenate(dvn_rows, axis=0))
        dvn = jnp.concatenate(dvn_cols, axis=1)
        dv = rstd * (dvn - jnp.mean(dvn, axis=-1, keepdims=True) - vn * jnp.mean(dvn * vn, axis=-1, keepdims=True))
        dp_ref[:, A_WIDTH:] = (dv * _gelu_grad(av)).astype(dp_ref.dtype)

    return pl.pallas_call(
        body, grid=(t // tm,),
        in_specs=[pl.BlockSpec((tm, A_WIDTH), lambda i: (i, 0)), pl.BlockSpec((tm, A_WIDTH), lambda i: (i, 1)),
                  pl.BlockSpec((tm, A_WIDTH), lambda i: (i, 0)),
                  pl.BlockSpec(w.shape, lambda i: (0, 0, 0)), pl.BlockSpec(b.shape, lambda i: (0, 0, 0))],
        out_specs=[pl.BlockSpec((tm, 2 * A_WIDTH), lambda i: (i, 0)),
                   pl.BlockSpec(w.shape, lambda i: (0, 0, 0)), pl.BlockSpec(b.shape, lambda i: (0, 0, 0))],
        out_shape=[S((t, 2 * A_WIDTH), BF16), S(w.shape, F32), S(b.shape, F32)],
        compiler_params=_cp("arbitrary"), name=name)(proj, proj, dcat, w, b)


CONV_ROWS = 256


def conv_fwd(name, proj, w, cb):
    t = proj.shape[0]
    tc = LANES
    rows = _tile(t, CONV_ROWS)
    a_cb, g_cb = 2 * A_WIDTH // tc, (2 * A_WIDTH + B_WIDTH) // tc

    def body(a_ref, g_ref, w_ref, cb_ref, o_ref, hpad):
        hpad[0:CONV_PAD, :] = jnp.zeros((CONV_PAD, tc), F32)

        def fill(i, _):
            r0 = pl.multiple_of(i * rows, rows)
            hpad[pl.ds(CONV_PAD + r0, rows), :] = a_ref[pl.ds(r0, rows), :] * _sigmoid(g_ref[pl.ds(r0, rows), :])
            return 0
        lax.fori_loop(0, t // rows, fill, 0)

        def conv(i, _):
            r0 = pl.multiple_of(i * rows, rows)
            win = hpad[pl.ds(r0, rows + CONV_PAD), :]
            acc = jnp.zeros((rows, tc), F32) + cb_ref[...]
            for k in range(CONV_WIDTH):
                sh = CONV_WIDTH - 1 - k
                src = win if sh == 0 else pltpu.roll(win, sh, 0)
                acc = acc + src[CONV_PAD:, :] * w_ref[k:k + 1, :]
            o_ref[pl.ds(r0, rows), :] = acc
            return 0
        lax.fori_loop(0, t // rows, conv, 0)

    return pl.pallas_call(
        body, grid=(B_WIDTH // tc,),
        in_specs=[pl.BlockSpec((t, tc), lambda j: (0, a_cb + j)), pl.BlockSpec((t, tc), lambda j: (0, g_cb + j)),
                  pl.BlockSpec((CONV_WIDTH, tc), lambda j: (0, j)), pl.BlockSpec((1, tc), lambda j: (0, j))],
        out_specs=pl.BlockSpec((t, tc), lambda j: (0, j)), out_shape=S((t, B_WIDTH), F32),
        scratch_shapes=[pltpu.VMEM((t + CONV_PAD, tc), F32)],
        compiler_params=_cp("parallel"), name=name)(proj, proj, w, cb)


def conv_bwd(name, proj, dhc, w):
    t = proj.shape[0]
    tc = LANES
    rows = _tile(t, CONV_ROWS)
    a_cb, g_cb = 2 * A_WIDTH // tc, (2 * A_WIDTH + B_WIDTH) // tc
    win_rows = rows + CONV_PAD

    def body(a_ref, g_ref, d_ref, w_ref, da_ref, dg_ref, dw_ref, dcb_ref, hpad, dpad, dwacc):
        hpad[0:CONV_PAD, :] = jnp.zeros((CONV_PAD, tc), F32)
        dpad[t:t + CONV_PAD, :] = jnp.zeros((CONV_PAD, tc), F32)
        dwacc[...] = jnp.zeros(dwacc.shape, F32)

        def fill(i, _):
            r0 = pl.multiple_of(i * rows, rows)
            hpad[pl.ds(CONV_PAD + r0, rows), :] = a_ref[pl.ds(r0, rows), :] * _sigmoid(g_ref[pl.ds(r0, rows), :])
            dpad[pl.ds(r0, rows), :] = d_ref[pl.ds(r0, rows), :]
            return 0
        lax.fori_loop(0, t // rows, fill, 0)

        def step(i, dcb):
            r0 = pl.multiple_of(i * rows, rows)
            hwin = hpad[pl.ds(r0, win_rows), :]
            dwin = dpad[pl.ds(r0, win_rows), :]
            dchunk = dwin[:rows, :]
            dh = jnp.zeros((rows, tc), F32)
            for k in range(CONV_WIDTH):
                sh = CONV_WIDTH - 1 - k
                hsrc = hwin if sh == 0 else pltpu.roll(hwin, sh, 0)
                dsrc = dwin if sh == 0 else pltpu.roll(dwin, win_rows - sh, 0)
                dh = dh + dsrc[:rows, :] * w_ref[k:k + 1, :]
                prod = dchunk * hsrc[CONV_PAD:, :]
                dwacc[k] += jnp.sum(prod.reshape(rows // 8, 8, tc), axis=0)
            a = a_ref[pl.ds(r0, rows), :]
            sg = _sigmoid(g_ref[pl.ds(r0, rows), :])
            da_ref[pl.ds(r0, rows), :] = (dh * sg).astype(da_ref.dtype)
            dg_ref[pl.ds(r0, rows), :] = (dh * a * sg * (1.0 - sg)).astype(dg_ref.dtype)
            return dcb + jnp.sum(dchunk, axis=0, keepdims=True)
        dcb = lax.fori_loop(0, t // rows, step, jnp.zeros((1, tc), F32))
        dcb_ref[...] = dcb
        for k in range(CONV_WIDTH):
            dw_ref[k:k + 1, :] = jnp.sum(dwacc[k], axis=0, keepdims=True)

    return pl.pallas_call(
        body, grid=(B_WIDTH // tc,),
        in_specs=[pl.BlockSpec((t, tc), lambda j: (0, a_cb + j)), pl.BlockSpec((t, tc), lambda j: (0, g_cb + j)),
                  pl.BlockSpec((t, tc), lambda j: (0, j)), pl.BlockSpec((CONV_WIDTH, tc), lambda j: (0, j))],
        out_specs=[pl.BlockSpec((t, tc), lambda j: (0, j)), pl.BlockSpec((t, tc), lambda j: (0, j)),
                   pl.BlockSpec((CONV_WIDTH, tc), lambda j: (0, j)), pl.BlockSpec((1, tc), lambda j: (0, j))],
        out_shape=[S((t, B_WIDTH), BF16), S((t, B_WIDTH), BF16), S((CONV_WIDTH, B_WIDTH), F32), S((1, B_WIDTH), F32)],
        scratch_shapes=[pltpu.VMEM((t + CONV_PAD, tc), F32), pltpu.VMEM((t + CONV_PAD, tc), F32),
                        pltpu.VMEM((CONV_WIDTH, 8, tc), F32)],
        compiler_params=_cp("parallel"), name=name)(proj, proj, dhc, w)


def ln_silu_fwd(name, hc, g, b):
    def fn(h, gv, bv):
        y, _ = _ln_plain(h)
        z = y * gv + bv
        return [z * _sigmoid(z)], []
    return rows_call(name, fn, [hc], [g, b], [(hc.shape[1], BF16)], [])[0]


def ln_silu_bwd(name, hc, dcat, g, b):
    c = hc.shape[1]

    def fn(h, dout, gv, bv):
        y, rstd = _ln_plain(h)
        z = y * gv + bv
        s = _sigmoid(z)
        dz = dout * s * (1.0 + z * (1.0 - s))
        dyv = dz * gv
        dh = rstd * (dyv - jnp.mean(dyv, axis=-1, keepdims=True) - y * jnp.mean(dyv * y, axis=-1, keepdims=True))
        return [dh], [jnp.sum(dz * y, axis=0, keepdims=True), jnp.sum(dz, axis=0, keepdims=True)]

    return rows_call(name, fn, [hc, (dcat, 1, c)], [g, b], [(c, F32)], [(1, c), (1, c)])


_NT = (((1,), (1,)), ((), ()))
_TN = (((0,), (0,)), ((), ()))


def attn_fwd(name, q, k, v, tm=512):
    t, d = q.shape
    m = k.shape[0]
    tm = _tile(t, tm)
    scale = CA_HEAD_DIM ** -0.5

    def body(q_ref, k_ref, v_ref, o_ref):
        for h in range(CA_HEADS):
            cs = slice(h * CA_HEAD_DIM, (h + 1) * CA_HEAD_DIM)
            s = lax.dot_general(q_ref[:, cs], k_ref[:, cs], _NT, preferred_element_type=F32) * scale
            e = jnp.exp(s - jnp.max(s, axis=-1, keepdims=True))
            p = e / jnp.sum(e, axis=-1, keepdims=True)
            o_ref[:, cs] = jnp.dot(_bf(p), v_ref[:, cs], preferred_element_type=F32).astype(o_ref.dtype)

    return pl.pallas_call(
        body, grid=(t // tm,),
        in_specs=[pl.BlockSpec((tm, d), lambda i: (i, 0)), pl.BlockSpec((m, d), lambda i: (0, 0)),
                  pl.BlockSpec((m, d), lambda i: (0, 0))],
        out_specs=pl.BlockSpec((tm, d), lambda i: (i, 0)), out_shape=S((t, d), BF16),
        compiler_params=_cp("parallel"), name=name)(q, k, v)


def attn_bwd(name, q, k, v, do, tm=512):
    t, d = q.shape
    m = k.shape[0]
    tm = _tile(t, tm)
    scale = CA_HEAD_DIM ** -0.5

    def body(q_ref, k_ref, v_ref, do_ref, dq_ref, dk_ref, dv_ref):
        @pl.when(pl.program_id(0) == 0)
        def _():
            dk_ref[...] = jnp.zeros(dk_ref.shape, F32)
            dv_ref[...] = jnp.zeros(dv_ref.shape, F32)

        for h in range(CA_HEADS):
            cs = slice(h * CA_HEAD_DIM, (h + 1) * CA_HEAD_DIM)
            qh, kh, vh, doh = q_ref[:, cs], k_ref[:, cs], v_ref[:, cs], do_ref[:, cs]
            s = lax.dot_general(qh, kh, _NT, preferred_element_type=F32) * scale
            e = jnp.exp(s - jnp.max(s, axis=-1, keepdims=True))
            p = e / jnp.sum(e, axis=-1, keepdims=True)
            pb = _bf(p)
            dv_ref[:, cs] += lax.dot_general(pb, doh, _TN, preferred_element_type=F32)
            dp = lax.dot_general(doh, vh, _NT, preferred_element_type=F32)
            ds = _bf(p * (dp - jnp.sum(dp * p, axis=-1, keepdims=True)) * scale)
            dq_ref[:, cs] = jnp.dot(ds, kh, preferred_element_type=F32).astype(dq_ref.dtype)
            dk_ref[:, cs] += lax.dot_general(ds, qh, _TN, preferred_element_type=F32)

    return pl.pallas_call(
        body, grid=(t // tm,),
        in_specs=[pl.BlockSpec((tm, d), lambda i: (i, 0)), pl.BlockSpec((m, d), lambda i: (0, 0)),
                  pl.BlockSpec((m, d), lambda i: (0, 0)), pl.BlockSpec((tm, d), lambda i: (i, 0))],
        out_specs=[pl.BlockSpec((tm, d), lambda i: (i, 0)), pl.BlockSpec((m, d), lambda i: (0, 0)),
                   pl.BlockSpec((m, d), lambda i: (0, 0))],
        out_shape=[S((t, d), BF16), S((m, d), F32), S((m, d), F32)],
        compiler_params=_cp("arbitrary"), name=name)(q, k, v, do)


SUB = 8
S5_ROWS = 256


def s5_constants(lam_re, lam_im, log_dt, b_re, b_im, c_re, c_im):
    dt = jnp.exp(log_dt)[:, None]
    mag = jnp.exp(lam_re * dt)
    ar = mag * jnp.cos(lam_im * dt)
    ai = mag * jnp.sin(lam_im * dt)
    den = lam_re * lam_re + lam_im * lam_im
    qr = ((ar - 1.0) * lam_re + ai * lam_im) / den
    qi = (ai * lam_re - (ar - 1.0) * lam_im) / den
    bbr = qr[..., None] * b_re - qi[..., None] * b_im
    bbi = qr[..., None] * b_im + qi[..., None] * b_re
    eye = jnp.eye(C_GROUPS, dtype=F32)

    def in_mat(bb):
        return (bb.transpose(0, 2, 1)[:, :, None, :] * eye[:, None, :, None]).reshape(C_WIDTH, N_STATE)

    def out_mat(cc):
        return (cc.transpose(0, 2, 1)[:, :, None, :] * eye[:, None, :, None]).reshape(N_STATE, C_WIDTH)

    mb = jnp.concatenate([in_mat(bbr), in_mat(bbi)], axis=1)
    mc = jnp.concatenate([out_mat(c_re), -out_mat(c_im)], axis=0)
    a = jnp.stack([ar.reshape(N_STATE), ai.reshape(N_STATE)])
    return a, mb, mc


def _scan_powers(a, conj):
    ar, ai = a[0], (-a[1] if conj else a[1])
    pows = [(ar, ai)]
    for _ in range(SUB - 1):
        pr, pi = pows[-1]
        pows.append((pr * ar - pi * ai, pr * ai + pi * ar))
    rows = jnp.arange(SUB)[:, None]
    out = []
    for s in (1, 2, 4):
        keep = (rows + s <= SUB - 1) if conj else (rows >= s)
        out.append(jnp.stack([jnp.where(keep, pows[s - 1][0][None, :], 0.0), jnp.where(keep, pows[s - 1][1][None, :], 0.0)]))
    order = [SUB - 1 - i for i in range(SUB)] if conj else list(range(SUB))
    out.append(jnp.stack([jnp.stack([pows[i][0] for i in order]), jnp.stack([pows[i][1] for i in order])]))
    return jnp.stack(out)


def _cmul_add(xr, xi, pr, pi, zr, zi):
    return xr + pr * zr - pi * zi, xi + pr * zi + pi * zr


def s5_fwd(name, u, mb, mc, pw, dskip):
    t = u.shape[0]
    tm = _tile(t, S5_ROWS)
    ns = N_STATE

    def body(u_ref, mb_ref, mc_ref, pw_ref, d_ref, gy_ref, y_ref, xs_ref, xb_ref, carry):
        @pl.when(pl.program_id(0) == 0)
        def _():
            carry[...] = jnp.zeros(carry.shape, F32)

        uv = u_ref[...]
        xs_ref[...] = jnp.dot(_bf(uv), mb_ref[...], preferred_element_type=F32)

        def group(i, _):
            r0 = pl.multiple_of(i * SUB, SUB)
            xr = xs_ref[pl.ds(r0, SUB), 0:ns]
            xi = xs_ref[pl.ds(r0, SUB), ns:2 * ns]
            for k, s in enumerate((1, 2, 4)):
                xr, xi = _cmul_add(xr, xi, pw_ref[k, 0], pw_ref[k, 1], pltpu.roll(xr, s, 0), pltpu.roll(xi, s, 0))
            xr, xi = _cmul_add(xr, xi, pw_ref[3, 0], pw_ref[3, 1], carry[0], carry[1])
            xs_ref[pl.ds(r0, SUB), 0:ns] = xr
            xs_ref[pl.ds(r0, SUB), ns:2 * ns] = xi
            carry[0] = jnp.broadcast_to(xr[SUB - 1:SUB, :], (SUB, ns))
            carry[1] = jnp.broadcast_to(xi[SUB - 1:SUB, :], (SUB, ns))
            return 0
        lax.fori_loop(0, tm // SUB, group, 0)

        xb = _bf(xs_ref[...])
        xb_ref[...] = xb
        y = jnp.dot(xb, mc_ref[...], preferred_element_type=F32) + d_ref[...] * uv
        y_ref[...] = y
        gy_ref[...] = _gelu(y).astype(gy_ref.dtype)

    c = u.shape[1]
    return pl.pallas_call(
        body, grid=(t // tm,),
        in_specs=[pl.BlockSpec((tm, c), lambda i: (i, 0)), pl.BlockSpec(mb.shape, lambda i: (0, 0)),
                  pl.BlockSpec(mc.shape, lambda i: (0, 0)), pl.BlockSpec(pw.shape, lambda i: (0, 0, 0, 0)),
                  pl.BlockSpec((1, c), lambda i: (0, 0))],
        out_specs=[pl.BlockSpec((tm, c), lambda i: (i, 0)), pl.BlockSpec((tm, c), lambda i: (i, 0)),
                   pl.BlockSpec((tm, 2 * ns), lambda i: (i, 0)), pl.BlockSpec((tm, 2 * ns), lambda i: (i, 0))],
        out_shape=[S((t, c), BF16), S((t, c), F32), S((t, 2 * ns), F32), S((t, 2 * ns), BF16)],
        scratch_shapes=[pltpu.VMEM((2, SUB, ns), F32)],
        compiler_params=_cp("arbitrary"), name=name)(u, mb, mc, pw, dskip)


def s5_bwd(name, dgy, y, u, xs, mct, mbt, qw, dskip):
    t, c = u.shape
    tm = _tile(t, S5_ROWS)
    nt = t // tm
    ns = N_STATE
    ng = tm // SUB

    def body(dgy_ref, y_ref, u_ref, xs_ref, prev_ref, mct_ref, mbt_ref, qw_ref, d_ref,
             du_ref, dy_ref, lb_ref, da_ref, dd_ref, lam, carry):
        step = pl.program_id(0)

        @pl.when(step == 0)
        def _():
            carry[...] = jnp.zeros(carry.shape, F32)
            da_ref[...] = jnp.zeros(da_ref.shape, F32)
            dd_ref[...] = jnp.zeros(dd_ref.shape, F32)

        uv = u_ref[...]
        dy = dgy_ref[...] * _gelu_grad(y_ref[...])
        dyb = _bf(dy)
        dy_ref[...] = dyb
        dd_ref[...] += jnp.sum(dy * uv, axis=0, keepdims=True)
        lam[...] = jnp.dot(dyb, mct_ref[...], preferred_element_type=F32)
        first_tile = (step == nt - 1).astype(F32)
        row0 = lax.broadcasted_iota(jnp.int32, (SUB, ns), 0) == 0

        def group(j, _):
            i = ng - 1 - j
            r0 = pl.multiple_of(i * SUB, SUB)
            lr = lam[pl.ds(r0, SUB), 0:ns]
            li = lam[pl.ds(r0, SUB), ns:2 * ns]
            for k, s in enumerate((1, 2, 4)):
                lr, li = _cmul_add(lr, li, qw_ref[k, 0], qw_ref[k, 1],
                                   pltpu.roll(lr, SUB - s, 0), pltpu.roll(li, SUB - s, 0))
            lr, li = _cmul_add(lr, li, qw_ref[3, 0], qw_ref[3, 1], carry[0], carry[1])
            lam[pl.ds(r0, SUB), 0:ns] = lr
            lam[pl.ds(r0, SUB), ns:2 * ns] = li
            carry[0] = jnp.broadcast_to(lr[0:1, :], (SUB, ns))
            carry[1] = jnp.broadcast_to(li[0:1, :], (SUB, ns))
            rp = pl.multiple_of(jnp.maximum(i - 1, 0) * SUB, SUB)
            in_tile = (i > 0).astype(F32)
            out_tile = (1.0 - in_tile) * (1.0 - first_tile)
            pr = xs_ref[pl.ds(rp, SUB), 0:ns] * in_tile + prev_ref[:, 0:ns] * out_tile
            pi = xs_ref[pl.ds(rp, SUB), ns:2 * ns] * in_tile + prev_ref[:, ns:2 * ns] * out_tile
            xpr = jnp.where(row0, pltpu.roll(pr, 1, 0), pltpu.roll(xs_ref[pl.ds(r0, SUB), 0:ns], 1, 0))
            xpi = jnp.where(row0, pltpu.roll(pi, 1, 0), pltpu.roll(xs_ref[pl.ds(r0, SUB), ns:2 * ns], 1, 0))
            da_ref[0] += lr * xpr + li * xpi
            da_ref[1] += li * xpr - lr * xpi
            return 0
        lax.fori_loop(0, ng, group, 0)

        lb = _bf(lam[...])
        lb_ref[...] = lb
        du_ref[...] = (jnp.dot(lb, mbt_ref[...], preferred_element_type=F32) + d_ref[...] * dy).astype(du_ref.dtype)

    rev = lambda i: (nt - 1 - i, 0)
    prev = lambda i: (jnp.maximum((nt - 1 - i) * (tm // SUB) - 1, 0), 0)
    return pl.pallas_call(
        body, grid=(nt,),
        in_specs=[pl.BlockSpec((tm, c), rev), pl.BlockSpec((tm, c), rev), pl.BlockSpec((tm, c), rev),
                  pl.BlockSpec((tm, 2 * ns), rev), pl.BlockSpec((SUB, 2 * ns), prev),
                  pl.BlockSpec(mct.shape, lambda i: (0, 0)), pl.BlockSpec(mbt.shape, lambda i: (0, 0)),
                  pl.BlockSpec(qw.shape, lambda i: (0, 0, 0, 0)), pl.BlockSpec((1, c), lambda i: (0, 0))],
        out_specs=[pl.BlockSpec((tm, c), rev), pl.BlockSpec((tm, c), rev), pl.BlockSpec((tm, 2 * ns), rev),
                   pl.BlockSpec((2, SUB, ns), lambda i: (0, 0, 0)), pl.BlockSpec((1, c), lambda i: (0, 0))],
        out_shape=[S((t, c), BF16), S((t, c), BF16), S((t, 2 * ns), BF16), S((2, SUB, ns), F32), S((1, c), F32)],
        scratch_shapes=[pltpu.VMEM((tm, 2 * ns), F32), pltpu.VMEM((2, SUB, ns), F32)],
        compiler_params=_cp("arbitrary"), name=name)(dgy, y, u, xs, xs, mct, mbt, qw, dskip)


def _first(accs, *_):
    return [accs[0]]


def _add_res(accs, res):
    return [accs[0] + res]


def even_fwd(x, w):
    t = x.shape[0]
    hn, r = rms_fwd("e_norm_f", x, w["e_norm"])
    (proj,) = mm_nn("e_in_f", t, IN_WIDTH, [(hn, w["e_w_in_t"], 0, "t")], 1, _first, [F32])
    out_a = gmlp_fwd("e_gmlp_f", proj, w["e_gmlp_w"], w["e_gmlp_b"])
    hc = conv_fwd("e_conv_f", proj, w["e_conv_w"], w["e_conv_b"])
    out_b = ln_silu_fwd("e_ln_f", hc, w["e_conv_ln_g"], w["e_conv_ln_b"])
    (x1,) = mm_nn("e_out_f", t, D_MODEL, [(out_a, (w["e_w_out"], 0), 0), (out_b, (w["e_w_out"], 1), 0)],
                  1, _add_res, [F32], tiled=[x])
    return x1, (x, hn, r, proj, out_a, hc, out_b)


def even_bwd(dx, dxb, saved, w):
    x, hn, r, proj, out_a, hc, out_b = saved
    t = x.shape[0]
    (dcat,) = mm_nn("e_out_b", t, D_MODEL, [(dxb, w["e_w_out"], 0, "t")], 1, _first, [F32])
    g_w_out = jnp.concatenate([mm_tn("e_out_wa", out_a, dxb), mm_tn("e_out_wb", out_b, dxb)], axis=0)
    dab, g_gw, g_gb = gmlp_bwd("e_gmlp_b", proj, dcat, w["e_gmlp_w"], w["e_gmlp_b"])
    dhc, g_lg, g_lb = ln_silu_bwd("e_ln_b", hc, dcat, w["e_conv_ln_g"], w["e_conv_ln_b"])
    dba, dbg, g_cw, g_cb = conv_bwd("e_conv_b", proj, dhc, w["e_conv_w"])
    w_in_t = w["e_w_in_t"]
    (dhn,) = mm_nn("e_in_b", t, D_MODEL, [(dab, (w_in_t, 0), 0), (dba, (w_in_t, 2), 0), (dbg, (w_in_t, 3), 0)],
                   1, _first, [F32])
    g_w_in_t = jnp.concatenate([mm_tn("e_in_w0", dab, hn), mm_tn("e_in_w1", dba, hn), mm_tn("e_in_w2", dbg, hn)], axis=0)
    dx0, dx0b, g_norm = rms_bwd("e_norm_b", dhn, x, r, w["e_norm"], dres=dx)
    grads = dict(e_norm=g_norm, e_w_in_t=g_w_in_t, e_gmlp_w=g_gw[None], e_gmlp_b=g_gb.reshape(1, A_GROUPS, GMLP_BLOCK),
                 e_conv_w=g_cw[None], e_conv_b=g_cb, e_conv_ln_g=g_lg, e_conv_ln_b=g_lb, e_w_out=g_w_out)
    return dx0, dx0b, grads


def odd_fwd(x, w, consts):
    t = x.shape[0]
    _, mb, mc, pw, _ = consts
    hn, r = rms_fwd("o_norm_f", x, w["o_norm"])
    (u,) = mm_nn("o_in_f", t, C_WIDTH, [(hn, w["o_w_in"], 0)], 1, _first, [F32])
    gy, y, xs, xsb = s5_fwd("o_s5_f", u, _bf(mb), _bf(mc), pw, w["o_d"])
    w_out_t = w["o_w_out_t"]

    def epi(accs, res):
        return [res + accs[0] * _sigmoid(accs[1]), accs[0], accs[1]]

    x1, o1, o2 = mm_nn("o_out_f", t, D_MODEL, [(gy, (w_out_t, 0), 0, "t"), (gy, (w_out_t, D_MODEL), 1, "t")], 2, epi,
                       [F32, BF16, BF16], tiled=[x])
    return x1, (x, hn, r, u, gy, y, xs, xsb, o1, o2)


def odd_bwd(dx, dxb, saved, w, consts, consts_vjp):
    x, hn, r, u, gy, y, xs, xsb, o1, o2 = saved
    t = x.shape[0]
    _, mb, mc, _, qw = consts

    def gate_bwd(dv, a, b):
        a = a.astype(F32)
        sg = _sigmoid(b.astype(F32))
        return [jnp.concatenate([dv * sg, dv * a * sg * (1.0 - sg)], axis=1)], []

    (do12,) = rows_call("o_gate_b", gate_bwd, [dx, o1, o2], [], [(2 * D_MODEL, BF16)], [])
    (dgy,) = mm_nn("o_out_b", t, C_WIDTH, [(do12, w["o_w_out_t"], 0)], 1, _first, [F32])
    g_w_out_t = mm_tn("o_out_w", do12, gy)
    du, dyb, lamb, da8, g_d = s5_bwd("o_s5_b", dgy, y, u, xs, _bf(mc.T), _bf(mb.T), qw, w["o_d"])
    d_mb = mm_tn("o_s5_wb", u, lamb, out_dtype=F32)
    d_mc = mm_tn("o_s5_wc", xsb, dyb, out_dtype=F32)
    g_lr, g_li, g_dt, g_br, g_bi, g_cr, g_ci = consts_vjp((jnp.sum(da8, axis=1), d_mb, d_mc))
    g_w_in = mm_tn("o_in_w", hn, du)
    (dhn,) = mm_nn("o_in_b", t, D_MODEL, [(du, w["o_w_in"], 0, "t")], 1, _first, [F32])
    dx0, dx0b, g_norm = rms_bwd("o_norm_b", dhn, x, r, w["o_norm"], dres=dx)
    grads = dict(o_norm=g_norm, o_w_in=g_w_in, o_lam_re=g_lr[None], o_lam_im=g_li[None], o_log_dt=g_dt[None],
                 o_b_re=g_br[None], o_b_im=g_bi[None], o_c_re=g_cr[None], o_c_im=g_ci[None], o_d=g_d, o_w_out_t=g_w_out_t)
    return dx0, dx0b, grads


def ca_fwd(i, x, mem, w):
    t, m = x.shape[0], mem.shape[0]
    xn, r = rms_fwd(f"ca{i}_norm_f", x, w["ca_norm"][i:i + 1])
    mn, rm = rms_fwd(f"ca{i}_mnorm_f", mem, w["ca_mem_norm"][i:i + 1])
    (q,) = mm_nn(f"ca{i}_q_f", t, D_MODEL, [(xn, w["ca_wq"][i], 0)], 1, _first, [BF16])
    k, v = mm_nn(f"ca{i}_kv_f", m, D_MODEL, [(mn, w["ca_wk"][i], 0), (mn, w["ca_wv"][i], 1)], 2,
                 lambda accs: [accs[0], accs[1]], [BF16, BF16])
    o = attn_fwd(f"ca{i}_attn_f", q, k, v)
    (x1,) = mm_nn(f"ca{i}_o_f", t, D_MODEL, [(o, w["ca_wo"][i], 0)], 1, _add_res, [F32], tiled=[x])
    return x1, (x, xn, r, mn, rm, q, k, v, o)


def ca_bwd(i, dx, dxb, saved, mem, w):
    x, xn, r, mn, rm, q, k, v, o = saved
    t, m = x.shape[0], mem.shape[0]
    (do,) = mm_nn(f"ca{i}_o_b", t, D_MODEL, [(dxb, w["ca_wo"][i], 0, "t")], 1, _first, [BF16])
    g_wo = mm_tn(f"ca{i}_o_w", o, dxb)
    dq, dk, dv = attn_bwd(f"ca{i}_attn_b", q, k, v, do)
    g_wq = mm_tn(f"ca{i}_q_w", xn, dq)
    g_wk = mm_tn(f"ca{i}_k_w", mn, dk)
    g_wv = mm_tn(f"ca{i}_v_w", mn, dv)
    (dxn,) = mm_nn(f"ca{i}_q_b", t, D_MODEL, [(dq, w["ca_wq"][i], 0, "t")], 1, _first, [F32])
    (dmn,) = mm_nn(f"ca{i}_kv_b", m, D_MODEL, [(dk, w["ca_wk"][i], 0, "t"), (dv, w["ca_wv"][i], 0, "t")], 1, _first, [F32])
    g_mnorm = rms_bwd_gain_only(f"ca{i}_mnorm_b", dmn, mem, rm)
    dx0, dx0b, g_norm = rms_bwd(f"ca{i}_norm_b", dxn, x, r, w["ca_norm"][i:i + 1], dres=dx)
    return dx0, dx0b, dict(ca_norm=g_norm, ca_mem_norm=g_mnorm, ca_wq=g_wq, ca_wk=g_wk, ca_wv=g_wv, ca_wo=g_wo)


def ffn_fwd(i, x, w):
    t = x.shape[0]
    xn, r = rms_fwd(f"ffn{i}_norm_f", x, w["ffn_norm"][i:i + 1])

    def epi(accs):
        g, u = accs
        return [g, u, g * _sigmoid(g) * u]

    g, u, h = mm_nn(f"ffn{i}_up_f", t, FFN_HIDDEN, [(xn, w["ffn_w_gate_t"][i], 0, "t"), (xn, w["ffn_w_up_t"][i], 1, "t")],
                    2, epi, [BF16, BF16, BF16])
    (x1,) = mm_nn(f"ffn{i}_down_f", t, D_MODEL, [(h, w["ffn_w_down"][i], 0)], 1, _add_res, [F32], tiled=[x])
    return x1, (x, xn, r, g, u, h)


def ffn_bwd(i, dx, dxb, saved, w):
    x, xn, r, g, u, h = saved
    t = x.shape[0]

    def epi(accs, gv, uv):
        dh = accs[0]
        gv = gv.astype(F32)
        uv = uv.astype(F32)
        s = _sigmoid(gv)
        return [dh * uv * s * (1.0 + gv * (1.0 - s)), dh * gv * s]

    dg, du = mm_nn(f"ffn{i}_down_b", t, FFN_HIDDEN, [(dxb, w["ffn_w_down"][i], 0, "t")], 1, epi, [BF16, BF16], tiled=[g, u])
    g_wd = mm_tn(f"ffn{i}_down_w", h, dxb)
    g_wg_t = mm_tn(f"ffn{i}_gate_w", dg, xn)
    g_wu_t = mm_tn(f"ffn{i}_up_w", du, xn)
    (dxn,) = mm_nn(f"ffn{i}_up_b", t, D_MODEL, [(dg, w["ffn_w_gate_t"][i], 0), (du, w["ffn_w_up_t"][i], 0)], 1, _first, [F32])
    dx0, dx0b, g_norm = rms_bwd(f"ffn{i}_norm_b", dxn, x, r, w["ffn_norm"][i:i + 1], dres=dx)
    return dx0, dx0b, dict(ffn_norm=g_norm, ffn_w_gate_t=g_wg_t, ffn_w_up_t=g_wu_t, ffn_w_down=g_wd)


_S5_PARAMS = ("o_lam_re", "o_lam_im", "o_log_dt", "o_b_re", "o_b_im", "o_c_re", "o_c_im")


def local_step(x, mem, target, w):
    def consts_fn(*p):
        a, mb, mc = s5_constants(*p)
        return a, mb, mc

    (a, mb, mc), consts_vjp = jax.vjp(consts_fn, *[w[k] for k in _S5_PARAMS])
    consts = (a, mb, mc, _scan_powers(a, False), _scan_powers(a, True))

    x1, s_e = even_fwd(x, w)
    x2, s_c0 = ca_fwd(0, x1, mem, w)
    x3, s_f0 = ffn_fwd(0, x2, w)
    x4, s_o = odd_fwd(x3, w, consts)
    x5, s_c1 = ca_fwd(1, x4, mem, w)
    x6, s_f1 = ffn_fwd(1, x5, w)
    dx, dxb, g_final, loss = final_loss("final_loss", x6, w["final_norm"], target)

    dx, dxb, g_f1 = ffn_bwd(1, dx, dxb, s_f1, w)
    dx, dxb, g_c1 = ca_bwd(1, dx, dxb, s_c1, mem, w)
    dx, dxb, g_o = odd_bwd(dx, dxb, s_o, w, consts, consts_vjp)
    dx, dxb, g_f0 = ffn_bwd(0, dx, dxb, s_f0, w)
    dx, dxb, g_c0 = ca_bwd(0, dx, dxb, s_c0, mem, w)
    dx, dxb, g_e = even_bwd(dx, dxb, s_e, w)

    grads = dict(g_e)
    grads.update(g_o)
    for g0, g1 in ((g_c0, g_c1), (g_f0, g_f1)):
        for k in g0:
            grads[k] = jnp.concatenate([g0[k], g1[k]], axis=0) if k.endswith("norm") else (g0[k], g1[k])
    grads["final_norm"] = g_final
    return loss, dx, grads


def _group(axes):
    pos = {a: lax.axis_index(a) for a in ("x", "y", "c")}
    me = 0
    for a in axes:
        me = me * 2 + pos[a]
    peers = []
    for mask in range(1, 2 ** len(axes)):
        peer = dict(pos)
        for bit, a in enumerate(axes):
            if (mask >> (len(axes) - 1 - bit)) & 1:
                peer[a] = 1 - pos[a]
        idx = 0
        for a in axes:
            idx = idx * 2 + peer[a]
        peers.append((idx, (peer["x"], peer["y"], peer["c"])))
    return me, peers


def _sibling():
    x, y, c = lax.axis_index("x"), lax.axis_index("y"), lax.axis_index("c")
    return c, (x, y, 1 - c)


def gather_ici(name, blks):
    k_ops = len(blks)
    out_shape = [S((4, 2) + tuple(b.shape), b.dtype) for b in blks]

    def body(*refs):
        in_refs, out_refs = refs[:k_ops], refs[k_ops:2 * k_ops]
        send_sems, recv_sems, local_sems = refs[2 * k_ops:]
        me, peers = _group(("x", "y"))
        core = lax.axis_index("c")
        local, sent, landed = [], [], []
        for i in range(k_ops):
            cp = pltpu.make_async_copy(in_refs[i], out_refs[i].at[me, core], local_sems.at[i])
            cp.start()
            local.append(cp)
        for k, (idx, dev) in enumerate(peers):
            for i in range(k_ops):
                s = i * 3 + k
                cp = pltpu.make_async_remote_copy(src_ref=in_refs[i], dst_ref=out_refs[i].at[me, core], send_sem=send_sems.at[s],
                                                  recv_sem=recv_sems.at[s], device_id=dev, device_id_type=MESH)
                cp.start()
                sent.append(cp)
                landed.append(pltpu.make_async_remote_copy(src_ref=in_refs[i], dst_ref=out_refs[i].at[idx, core],
                                                           send_sem=send_sems.at[s], recv_sem=recv_sems.at[s],
                                                           device_id=dev, device_id_type=MESH))
        for cp in landed:
            cp.wait_recv()
        for cp in sent:
            cp.wait_send()
        for cp in local:
            cp.wait()

    return pl.pallas_call(
        body, in_specs=[ANY] * k_ops, out_specs=[ANY] * k_ops, out_shape=out_shape,
        scratch_shapes=[pltpu.SemaphoreType.DMA((k_ops * 3,)), pltpu.SemaphoreType.DMA((k_ops * 3,)),
                        pltpu.SemaphoreType.DMA((k_ops,))],
        name=name)(*blks)


def gather_d2d(name, bufs):
    k_ops = len(bufs)

    def body(*refs):
        in_refs, out_refs = refs[:k_ops], refs[k_ops:2 * k_ops]
        send_sems, recv_sems = refs[2 * k_ops:]
        core, sib = _sibling()
        sent, landed = [], []
        for i in range(k_ops):
            cp = pltpu.make_async_remote_copy(src_ref=in_refs[i].at[:, core], dst_ref=out_refs[i].at[:, core],
                                              send_sem=send_sems.at[i], recv_sem=recv_sems.at[i], device_id=sib, device_id_type=MESH)
            cp.start()
            sent.append(cp)
            landed.append(pltpu.make_async_remote_copy(src_ref=in_refs[i].at[:, core], dst_ref=out_refs[i].at[:, 1 - core],
                                                       send_sem=send_sems.at[i], recv_sem=recv_sems.at[i],
                                                       device_id=sib, device_id_type=MESH))
        for cp in landed:
            cp.wait_recv()
        for cp in sent:
            cp.wait_send()

    return pl.pallas_call(
        body, in_specs=[ANY] * k_ops, out_specs=[ANY] * k_ops, out_shape=[S(b.shape, b.dtype) for b in bufs],
        input_output_aliases={i: i for i in range(k_ops)},
        scratch_shapes=[pltpu.SemaphoreType.DMA((k_ops,)), pltpu.SemaphoreType.DMA((k_ops,))],
        name=name)(*bufs)


def all_gather(name, blks):
    bufs = gather_d2d(name + "_d2d", gather_ici(name + "_ici", blks))
    return [p.reshape((N_DEV * b.shape[0],) + tuple(b.shape[1:])) for p, b in zip(bufs, blks)]


def scatter_d2d(name, pack, chunks):
    q, _, rows, c = pack.shape
    rc = rows // chunks

    def body(in_ref, out_ref, send_sems, recv_sems):
        core, sib = _sibling()
        sent = []
        for ch in range(chunks):
            rs = pl.ds(ch * rc, rc)
            cp = pltpu.make_async_remote_copy(src_ref=in_ref.at[:, 1 - core, rs], dst_ref=out_ref.at[:, rs],
                                              send_sem=send_sems.at[ch], recv_sem=recv_sems.at[ch], device_id=sib, device_id_type=MESH)
            cp.start()
            sent.append(cp)
        for cp in sent:
            cp.wait_recv()
        for cp in sent:
            cp.wait_send()

    return pl.pallas_call(
        body, in_specs=[ANY], out_specs=ANY, out_shape=S((q, rows, c), pack.dtype),
        scratch_shapes=[pltpu.SemaphoreType.DMA((chunks,)), pltpu.SemaphoreType.DMA((chunks,))],
        name=name)(pack)


def scatter_ici(name, arr):
    def body(in_ref, out_ref, send_sems, recv_sems):
        me, peers = _group(("x", "y"))
        sent, landed = [], []
        for k, (idx, dev) in enumerate(peers):
            cp = pltpu.make_async_remote_copy(src_ref=in_ref.at[idx], dst_ref=out_ref.at[me], send_sem=send_sems.at[k],
                                              recv_sem=recv_sems.at[k], device_id=dev, device_id_type=MESH)
            cp.start()
            sent.append(cp)
            landed.append(pltpu.make_async_remote_copy(src_ref=in_ref.at[idx], dst_ref=out_ref.at[idx], send_sem=send_sems.at[k],
                                                       recv_sem=recv_sems.at[k], device_id=dev, device_id_type=MESH))
        for cp in landed:
            cp.wait_recv()
        for cp in sent:
            cp.wait_send()

    return pl.pallas_call(
        body, in_specs=[ANY], out_specs=ANY, out_shape=S(arr.shape, arr.dtype),
        scratch_shapes=[pltpu.SemaphoreType.DMA((3,)), pltpu.SemaphoreType.DMA((3,))],
        name=name)(arr)


def sum_pair(name, pack, recv, core, tr=256):
    q, rows, c = recv.shape
    tr = _tile(rows, tr)

    def body(core_ref, a_ref, b_ref, o_ref):
        o_ref[...] = (a_ref[...].astype(F32) + b_ref[...].astype(F32)).astype(o_ref.dtype)

    spec = pltpu.PrefetchScalarGridSpec(
        num_scalar_prefetch=1, grid=(q, rows // tr),
        in_specs=[pl.BlockSpec((None, None, tr, c), lambda j, i, core: (j, core[0], i, 0)),
                  pl.BlockSpec((None, tr, c), lambda j, i, core: (j, i, 0))],
        out_specs=pl.BlockSpec((None, tr, c), lambda j, i, core: (j, i, 0)))
    return pl.pallas_call(body, grid_spec=spec, out_shape=S(recv.shape, recv.dtype),
                          compiler_params=_cp("parallel", "parallel"), name=name)(core, pack, recv)


def sum_quad(name, own, recv, chip, tr=256):
    _, rows, c = recv.shape
    tr = _tile(rows, tr)

    def body(chip_ref, a_ref, r1_ref, r2_ref, r3_ref, o_ref):
        o_ref[...] = ((a_ref[...].astype(F32) + r1_ref[...].astype(F32)) + r2_ref[...].astype(F32)) + r3_ref[...].astype(F32)

    def slot(mask):
        return pl.BlockSpec((None, tr, c), lambda i, chip, mask=mask: (jnp.bitwise_xor(chip[0], mask), i, 0))

    spec = pltpu.PrefetchScalarGridSpec(
        num_scalar_prefetch=1, grid=(rows // tr,), in_specs=[slot(0), slot(1), slot(2), slot(3)],
        out_specs=pl.BlockSpec((tr, c), lambda i, chip: (i, 0)))
    return pl.pallas_call(body, grid_spec=spec, out_shape=S((rows, c), F32),
                          compiler_params=_cp("parallel"), name=name)(chip, own, recv, recv, recv)


def adamw_native(name, g, w, m, v, tr=512):
    shape = w.shape
    cols = shape[-1]
    rows = w.size // cols
    tr = _tile(rows, tr) if rows % 8 == 0 else rows
    c1 = 1.0 - ADAM_B1 ** ADAM_STEP
    c2 = 1.0 - ADAM_B2 ** ADAM_STEP

    def body(g_ref, w_ref, m_ref, v_ref, d_ref, m2_ref, v2_ref):
        gv = g_ref[...]
        m2 = ADAM_B1 * m_ref[...] + (1.0 - ADAM_B1) * gv
        v2 = ADAM_B2 * v_ref[...] + (1.0 - ADAM_B2) * (gv * gv)
        m2_ref[...] = m2
        v2_ref[...] = v2
        d_ref[...] = -ADAM_LR * ((m2 / c1) / (jnp.sqrt(v2 / c2) + ADAM_EPS) + ADAM_WD * w_ref[...])

    row = pl.BlockSpec((tr, cols), lambda i: (i, 0))
    outs = pl.pallas_call(body, grid=(rows // tr,), in_specs=[row] * 4, out_specs=[row] * 3,
                          out_shape=[S((rows, cols), F32)] * 3, compiler_params=_cp("parallel"),
                          name=name)(*[a.reshape(rows, cols) for a in (g, w, m, v)])
    return tuple(o.reshape(shape) for o in outs)


def adamw_call(name, slots, w, m, v, tr=1024):
    n, r, c = slots.shape
    tr = _tile(r, tr)
    c1 = 1.0 - ADAM_B1 ** ADAM_STEP
    c2 = 1.0 - ADAM_B2 ** ADAM_STEP

    def body(s_ref, w_ref, m_ref, v_ref, g_ref, d_ref, m2_ref, v2_ref):
        g = s_ref[0].astype(F32)
        for j in range(1, n):
            g = g + s_ref[j].astype(F32)
        m2 = ADAM_B1 * m_ref[...] + (1.0 - ADAM_B1) * g
        v2 = ADAM_B2 * v_ref[...] + (1.0 - ADAM_B2) * (g * g)
        g_ref[...] = g
        m2_ref[...] = m2
        v2_ref[...] = v2
        d_ref[...] = -ADAM_LR * ((m2 / c1) / (jnp.sqrt(v2 / c2) + ADAM_EPS) + ADAM_WD * w_ref[...])

    row = pl.BlockSpec((tr, c), lambda i: (i, 0))
    return pl.pallas_call(body, grid=(r // tr,), in_specs=[pl.BlockSpec((n, tr, c), lambda i: (0, i, 0)), row, row, row],
                          out_specs=[row, row, row, row], out_shape=[S((r, c), F32)] * 4,
                          compiler_params=_cp("parallel"), name=name)(slots, w, m, v)


_REPLICATED = ("e_norm", "e_gmlp_w", "e_gmlp_b", "e_conv_b", "e_conv_ln_g", "e_conv_ln_b", "o_lam_re", "o_lam_im", "o_log_dt",
               "o_b_re", "o_b_im", "o_c_re", "o_c_im", "ca_norm", "ca_mem_norm", "ffn_norm", "final_norm")
_ORDER = ("e_norm", "e_w_in", "e_gmlp_w", "e_gmlp_b", "e_conv_w", "e_conv_b", "e_conv_ln_g", "e_conv_ln_b", "e_w_out",
          "o_norm", "o_w_in", "o_lam_re", "o_lam_im", "o_log_dt", "o_b_re", "o_b_im", "o_c_re", "o_c_im", "o_d", "o_w_out",
          "ca_norm", "ca_mem_norm", "ca_wq", "ca_wk", "ca_wv", "ca_wo", "ffn_norm", "ffn_w_gate", "ffn_w_up", "ffn_w_down",
          "final_norm")


def _rows128(a, multiple=8):
    flat = a.reshape(-1)
    rows = -(-flat.shape[0] // (LANES * multiple)) * multiple
    return jnp.pad(flat, (0, rows * LANES - flat.shape[0])).reshape(rows, LANES)


def _shard(full, axis):
    s = full.shape
    return jnp.moveaxis(full.reshape(s[:axis] + (N_DEV, s[axis] // N_DEV) + s[axis + 1:]), axis, 0)


_UNITS = (("e_w_in", 0, True), ("e_w_out", 0, False), ("o_w_in", 0, False), ("o_w_out", 0, True),
          *[(n, i, False) for n in ("ca_wq", "ca_wk", "ca_wv", "ca_wo") for i in (0, 1)],
          *[(n, i, tr) for n, tr in (("ffn_w_gate", True), ("ffn_w_up", True), ("ffn_w_down", False)) for i in (0, 1)])
_LAYERED = ("ca_wq", "ca_wk", "ca_wv", "ca_wo", "ffn_w_gate", "ffn_w_up", "ffn_w_down")
_SMALL_SHARDED = (("e_conv_w", 2), ("o_norm", 1), ("o_d", 1))
RS_ROW = 1024
RS_ROWS = 3840
RS_CHUNKS = 8


def _unit_key(name, tr):
    return name + "_t" if tr else name


def gather_weights(local):
    blks = []
    for name, layer, tr in _UNITS:
        blk = local[name][layer]
        blks.append(_bf(blk.T if tr else blk))
    small = jnp.concatenate([local[name].reshape(-1) for name, _ in _SMALL_SHARDED])
    blks.append(_rows128(small))
    full = all_gather("ag_w", blks)
    w = {}
    for (name, layer, tr), arr in zip(_UNITS, full[:-1]):
        key = _unit_key(name, tr)
        w[key] = w.get(key, ()) + (arr,) if name in _LAYERED else arr
    flat = full[-1].reshape(N_DEV, -1)
    off = 0
    for name, axis in _SMALL_SHARDED:
        blk = local[name]
        seg = flat[:, off:off + blk.size].reshape((N_DEV,) + blk.shape)
        off += blk.size
        seg = jnp.moveaxis(seg, 0, axis)
        w[name] = seg.reshape(seg.shape[:axis] + (-1,) + seg.shape[axis + 2:])
    return w


def reduce_sharded(grads, local, mom, var):
    parts, spans = [], []
    for name, layer, tr in _UNITS:
        g = grads[_unit_key(name, tr)]
        g = g[layer] if name in _LAYERED else g
        part = g.reshape(4, 2, -1, RS_ROW)
        spans.append((part.shape[2], g.shape[0] // N_DEV, g.shape[1]))
        parts.append(part)
    small = jnp.concatenate([_shard(grads[name], axis).reshape(N_DEV, -1) for name, axis in _SMALL_SHARDED], axis=1)
    n_small = small.shape[1]
    small_rows = 16
    parts.append(jnp.pad(small, ((0, 0), (0, small_rows * RS_ROW - n_small))).astype(BF16).reshape(4, 2, small_rows, RS_ROW))
    used = sum(p.shape[2] for p in parts)
    parts.append(jnp.zeros((4, 2, RS_ROWS - used, RS_ROW), BF16))
    pack = jnp.concatenate(parts, axis=2)
    core = lax.axis_index("c").astype(jnp.int32).reshape(1)
    chip = (2 * lax.axis_index("x") + lax.axis_index("y")).astype(jnp.int32).reshape(1)
    from_sibling = scatter_d2d("rs_g_d2d", pack, RS_CHUNKS)
    chip_sum = sum_pair("rs_g_sum2", pack, from_sibling, core)
    from_chips = scatter_ici("rs_g_ici", chip_sum)
    total = sum_quad("rs_g_sum4", chip_sum, from_chips, chip)

    res, off, per_layer = {}, 0, {}
    for (name, layer, tr), (rows, r, c) in zip(_UNITS, spans):
        g = total[off:off + rows].reshape(r, c)
        off += rows
        per_layer.setdefault(name, []).append(g.T if tr else g)
    for name, gs in per_layer.items():
        g = jnp.stack(gs) if name in _LAYERED else gs[0][None]
        res[name] = (g,) + adamw_native("adamw_" + name, g, local[name], mom[name], var[name])
    flat = total[off:off + small_rows].reshape(-1)
    off = 0
    for name, _ in _SMALL_SHARDED:
        blk = local[name]
        g = flat[off:off + blk.size].reshape(blk.shape)
        off += blk.size
        res[name] = (g,) + adamw_native("adamw_" + name, g, blk, mom[name], var[name])
    return res


def reduce_replicated(grads, loss, w, mom, var):
    def pack(src, last):
        return jnp.concatenate([_rows128(src[name]) for name in _REPLICATED] + [_rows128(last)], axis=0)

    (slots,) = all_gather("ag_g", [pack(grads, loss)])
    zero = jnp.zeros((1, 1), F32)
    rows = slots.shape[0] // N_DEV
    outs = adamw_call("adamw_replicated", slots.reshape(N_DEV, rows, LANES), pack(w, zero), pack(mom, zero), pack(var, zero),
                      tr=rows)
    res, off = {}, 0
    for name in _REPLICATED:
        n = w[name].size
        nr = _rows128(w[name]).shape[0]
        res[name] = tuple(o[off:off + nr].reshape(-1)[:n].reshape(w[name].shape) for o in outs)
        off += nr
    return res, outs[0][off, 0]


def kernel(x, mem, e_norm, e_w_in, e_gmlp_w, e_gmlp_b, e_conv_w, e_conv_b, e_conv_ln_g, e_conv_ln_b, e_w_out, o_norm, o_w_in, o_lam_re, o_lam_im, o_log_dt, o_b_re, o_b_im, o_c_re, o_c_im, o_d, o_w_out, ca_norm, ca_mem_norm, ca_wq, ca_wk, ca_wv, ca_wo, ffn_norm, ffn_w_gate, ffn_w_up, ffn_w_down, final_norm, loss_target, m_e_norm, m_e_w_in, m_e_gmlp_w, m_e_gmlp_b, m_e_conv_w, m_e_conv_b, m_e_conv_ln_g, m_e_conv_ln_b, m_e_w_out, m_o_norm, m_o_w_in, m_o_lam_re, m_o_lam_im, m_o_log_dt, m_o_b_re, m_o_b_im, m_o_c_re, m_o_c_im, m_o_d, m_o_w_out, m_ca_norm, m_ca_mem_norm, m_ca_wq, m_ca_wk, m_ca_wv, m_ca_wo, m_ffn_norm, m_ffn_w_gate, m_ffn_w_up, m_ffn_w_down, m_final_norm, v_e_norm, v_e_w_in, v_e_gmlp_w, v_e_gmlp_b, v_e_conv_w, v_e_conv_b, v_e_conv_ln_g, v_e_conv_ln_b, v_e_w_out, v_o_norm, v_o_w_in, v_o_lam_re, v_o_lam_im, v_o_log_dt, v_o_b_re, v_o_b_im, v_o_c_re, v_o_c_im, v_o_d, v_o_w_out, v_ca_norm, v_ca_mem_norm, v_ca_wq, v_ca_wk, v_ca_wv, v_ca_wo, v_ffn_norm, v_ffn_w_gate, v_ffn_w_up, v_ffn_w_down, v_final_norm):
    given = dict(locals())
    local = {k: given[k] for k in _ORDER}
    mom = {k: given["m_" + k] for k in _ORDER}
    var = {k: given["v_" + k] for k in _ORDER}

    w = gather_weights(local)
    w["e_conv_w"] = w["e_conv_w"][0]
    w.update({
        "e_norm": e_norm, "e_gmlp_w": e_gmlp_w[0], "e_gmlp_b": e_gmlp_b.reshape(A_GROUPS, GMLP_BLOCK, 1),
        "e_conv_b": e_conv_b, "e_conv_ln_g": e_conv_ln_g, "e_conv_ln_b": e_conv_ln_b,
        "o_lam_re": o_lam_re[0], "o_lam_im": o_lam_im[0], "o_log_dt": o_log_dt[0], "o_b_re": o_b_re[0], "o_b_im": o_b_im[0],
        "o_c_re": o_c_re[0], "o_c_im": o_c_im[0], "ca_norm": ca_norm, "ca_mem_norm": ca_mem_norm, "ffn_norm": ffn_norm,
        "final_norm": final_norm.reshape(1, D_MODEL),
    })
    loss_part, grad_x, grads = local_step(x[0], mem[0], loss_target[0], w)
    grads["final_norm"] = grads["final_norm"].reshape(D_MODEL)

    res = reduce_sharded(grads, local, mom, var)
    rep, loss = reduce_replicated(grads, loss_part, local, mom, var)
    res.update(rep)
    return (loss, grad_x[None], *[res[k][0] for k in _ORDER], *[res[k][1] for k in _ORDER],
            *[res[k][2] for k in _ORDER], *[res[k][3] for k in _ORDER])
```

```python
import jax
import jax.numpy as jnp
from jax import lax
from jax.experimental import pallas as pl
from jax.experimental.pallas import tpu as pltpu

F32 = jnp.float32
BF16 = jnp.bfloat16
S = jax.ShapeDtypeStruct

D_MODEL = 1024
A_WIDTH = 512
A_GROUPS = 4
GMLP_BLOCK = 128
CHUNK = 64
B_WIDTH = 512
IN_WIDTH = 2 * A_WIDTH + 2 * B_WIDTH
CONV_WIDTH = 31
CONV_PAD = 32
C_WIDTH = 512
C_GROUP_CH = 16
C_GROUPS = 32
C_STATE = 64
N_STATE = C_GROUPS * C_STATE
CA_HEADS = 4
CA_HEAD_DIM = 256
FFN_HIDDEN = 2816
EPS = 1e-6
ADAM_LR = 0.001
ADAM_B1 = 0.9
ADAM_B2 = 0.999
ADAM_EPS = 1e-08
ADAM_WD = 0.01
ADAM_STEP = 10
N_DEV = 8
LANES = 128
VMEM_LIMIT = 56 << 20
VMEM_BUDGET = 40 << 20
MM_TN_RESIDENT = 8 << 20
MESH = pl.DeviceIdType.MESH
ANY = pl.BlockSpec(memory_space=pl.ANY)


def _cp(*sem):
    return pltpu.CompilerParams(dimension_semantics=sem, vmem_limit_bytes=VMEM_LIMIT)


def _tile(n, pref):
    t = pref
    while n % t:
        t //= 2
    return t


def _bf(v):
    return v if v.dtype == BF16 else v.astype(BF16)


def _sigmoid(x):
    return 1.0 / (1.0 + jnp.exp(-x))


_GC = 0.7978845608028654


def _gelu(x):
    return 0.5 * x * (1.0 + jnp.tanh(_GC * (x + 0.044715 * x * x * x)))


def _gelu_grad(x):
    x2 = x * x
    t = jnp.tanh(_GC * (x + 0.044715 * x * x2))
    return 0.5 * (1.0 + t) + 0.5 * x * (1.0 - t * t) * _GC * (1.0 + 3.0 * 0.044715 * x2)


def _tspec(entry, tm):
    if isinstance(entry, tuple):
        arr, cb, width = entry
        return arr, pl.BlockSpec((tm, width), lambda i, cb=cb: (i, cb))
    return entry, pl.BlockSpec((tm, entry.shape[1]), lambda i: (i, 0))


def rows_call(name, fn, tiled, full, outs, accs, tm=256):
    pairs = [_tspec(e, tm) for e in tiled]
    arrs = [p[0] for p in pairs]
    rows = arrs[0].shape[0]
    tm = _tile(rows, tm)
    pairs = [_tspec(e, tm) for e in tiled]
    n_in = len(tiled) + len(full)
    n_out = len(outs)

    def body(*refs):
        vals = [r[...] for r in refs[:n_in]]
        o_refs = refs[n_in:n_in + n_out]
        a_refs = refs[n_in + n_out:]
        ov, av = fn(*vals)
        for r, v in zip(o_refs, ov):
            r[...] = v.astype(r.dtype)
        if a_refs:
            @pl.when(pl.program_id(0) == 0)
            def _():
                for r in a_refs:
                    r[...] = jnp.zeros(r.shape, r.dtype)
            for r, v in zip(a_refs, av):
                r[...] += v

    in_specs = [p[1] for p in pairs] + [pl.BlockSpec(a.shape, lambda i, nd=a.ndim: (0,) * nd) for a in full]
    out_specs = [pl.BlockSpec((tm, c), lambda i: (i, 0)) for c, _ in outs]
    out_specs += [pl.BlockSpec(s, lambda i, nd=len(s): (0,) * nd) for s in accs]
    out_shape = [S((rows, c), dt) for c, dt in outs] + [S(s, F32) for s in accs]
    return pl.pallas_call(body, grid=(rows // tm,), in_specs=in_specs, out_specs=out_specs, out_shape=out_shape,
                          compiler_params=_cp("arbitrary"), name=name)(*arrs, *full)


def mm_nn(name, m, n, pairs, n_acc, epi, outs, tiled=(), cols=(), rowv=()):
    a_ops, a_slot, b_arrs, b_specs, idx, trans = [], [], [], [], [], []
    fixed = 0
    for pair in pairs:
        a, b, k = pair[:3]
        bt = len(pair) > 3
        arr, cb, kdim = a if isinstance(a, tuple) else (a, 0, a.shape[1])
        key = (id(arr), cb, kdim)
        if key not in [o[0] for o in a_ops]:
            a_ops.append((key, arr, cb, kdim))
        a_slot.append([o[0] for o in a_ops].index(key))
        b_arr, off = b if isinstance(b, tuple) else (b, 0)
        b_arrs.append(b_arr)
        if bt:
            assert off % n == 0 and b_arr.shape[1] == kdim
            b_specs.append(pl.BlockSpec((n, kdim), lambda i, o=off // n: (o, 0), pipeline_mode=pl.Buffered(1)))
        else:
            assert b_arr.shape[1] == n
            b_specs.append(pl.BlockSpec((kdim, n), lambda i, o=off: (o, 0), pipeline_mode=pl.Buffered(1)))
        fixed += kdim * n * b_arr.dtype.itemsize
        idx.append(k)
        trans.append(bt)
    per_row = sum(2 * kdim * arr.dtype.itemsize for _, arr, _, kdim in a_ops)
    per_row += sum(2 * n * t.dtype.itemsize for t in tiled) + sum(2 * n * jnp.dtype(dt).itemsize for dt in outs)
    per_row += (n_acc + 3) * n * 4
    tm = next((t for t in (1024, 512, 256, 128) if m % t == 0 and fixed + t * per_row <= VMEM_BUDGET), _tile(m, 128))
    n_a, n_p = len(a_ops), len(pairs)
    n_in = n_a + n_p + len(tiled) + len(cols) + len(rowv)

    def body(*refs):
        a_vals = [_bf(r[...]) for r in refs[:n_a]]
        accs = [None] * n_acc
        for p in range(n_p):
            av, bv = a_vals[a_slot[p]], _bf(refs[n_a + p][...])
            if trans[p]:
                d = lax.dot_general(av, bv, (((1,), (1,)), ((), ())), preferred_element_type=F32)
            else:
                d = jnp.dot(av, bv, preferred_element_type=F32)
            accs[idx[p]] = d if accs[idx[p]] is None else accs[idx[p]] + d
        extra = [r[...] for r in refs[n_a + n_p:n_in]]
        ov = epi(accs, *extra)
        for r, v in zip(refs[n_in:], ov):
            r[...] = v.astype(r.dtype)

    in_specs = [pl.BlockSpec((tm, kdim), lambda i, cb=cb: (i, cb)) for _, _, cb, kdim in a_ops] + b_specs
    in_specs += [pl.BlockSpec((tm, n), lambda i: (i, 0)) for _ in tiled]
    in_specs += [pl.BlockSpec((tm, 1), lambda i: (i, 0)) for _ in cols]
    in_specs += [pl.BlockSpec((1, n), lambda i: (0, 0)) for _ in rowv]
    out_specs = [pl.BlockSpec((tm, n), lambda i: (i, 0)) for _ in outs]
    out_shape = [S((m, n), dt) for dt in outs]
    return pl.pallas_call(body, grid=(m // tm,), in_specs=in_specs, out_specs=out_specs, out_shape=out_shape,
                          compiler_params=_cp("parallel"), name=name)(*[o[1] for o in a_ops], *b_arrs, *tiled, *cols, *rowv)


def mm_tn(name, a, b, out_dtype=BF16):
    if isinstance(a, tuple):
        a_arr, a_cb, m = a
    else:
        a_arr, a_cb, m = a, None, a.shape[1]
    if isinstance(b, tuple):
        b_arr, b_cb, n = b
    else:
        b_arr, b_cb, n = b, None, b.shape[1]
    t = a_arr.shape[0]
    whole_b = t * n * b_arr.dtype.itemsize <= MM_TN_RESIDENT and b_cb is None
    tn = n if whole_b else _tile(n, 512)
    tm = _tile(m, 512 if t * 512 * a_arr.dtype.itemsize * 2 + t * tn * b_arr.dtype.itemsize * 2 <= VMEM_BUDGET else 256)
    a_off = 0 if a_cb is None else a_cb * (m // tm)
    b_off = 0 if b_cb is None else b_cb * (n // tn)

    def body(a_ref, b_ref, o_ref):
        o_ref[...] = lax.dot_general(_bf(a_ref[...]), _bf(b_ref[...]), (((0,), (0,)), ((), ())),
                                     preferred_element_type=F32).astype(o_ref.dtype)

    if whole_b:
        b_spec = pl.BlockSpec((t, n), lambda i, j: (0, 0), pipeline_mode=pl.Buffered(1))
    else:
        b_spec = pl.BlockSpec((t, tn), lambda i, j: (0, j + b_off))
    return pl.pallas_call(
        body, grid=(m // tm, n // tn),
        in_specs=[pl.BlockSpec((t, tm), lambda i, j: (0, i + a_off)), b_spec],
        out_specs=pl.BlockSpec((tm, tn), lambda i, j: (i, j)), out_shape=S((m, n), out_dtype),
        compiler_params=_cp("parallel", "parallel"), name=name)(a_arr, b_arr)


def rms_fwd(name, x, gain):
    def fn(xv, g):
        r = lax.rsqrt(jnp.mean(xv * xv, axis=-1, keepdims=True) + EPS)
        return [xv * r * g, r], []
    return rows_call(name, fn, [x], [gain], [(x.shape[1], BF16), (1, F32)], [])


def rms_bwd(name, dxn, x, r, gain, dres=None):
    d = x.shape[1]

    def fn(*vals):
        if dres is None:
            dv, xv, rv, g = vals
            base = 0.0
        else:
            dv, xv, rv, base, g = vals
        w = dv * g
        xh = xv * rv
        dx = base + rv * (w - xh * jnp.mean(w * xh, axis=-1, keepdims=True))
        return [dx, dx], [jnp.sum(dv * xh, axis=0, keepdims=True)]

    tiled = [dxn, x, r] + ([] if dres is None else [dres])
    return rows_call(name, fn, tiled, [gain], [(d, F32), (d, BF16)], [(1, d)])


def rms_bwd_gain_only(name, dxn, x, r):
    def fn(dv, xv, rv):
        return [], [jnp.sum(dv * xv * rv, axis=0, keepdims=True)]
    return rows_call(name, fn, [dxn, x, r], [], [], [(1, x.shape[1])])[0]


def final_loss(name, x, gain, target):
    d = x.shape[1]

    def fn(xv, tv, g):
        r = lax.rsqrt(jnp.mean(xv * xv, axis=-1, keepdims=True) + EPS)
        xh = xv * r
        err = xh * g - tv
        dy = err * (1.0 / d)
        w = dy * g
        dx = r * (w - xh * jnp.mean(w * xh, axis=-1, keepdims=True))
        part = jnp.sum(jnp.sum(err * err, axis=-1, keepdims=True), axis=0, keepdims=True) * (0.5 / d)
        return [dx, dx], [jnp.sum(dy * xh, axis=0, keepdims=True), part]

    return rows_call(name, fn, [x, target], [gain], [(d, F32), (d, BF16)], [(1, d), (1, 1)])


def _gmlp_mask():
    row = lax.broadcasted_iota(jnp.int32, (GMLP_BLOCK, GMLP_BLOCK), 0) // CHUNK
    col = lax.broadcasted_iota(jnp.int32, (GMLP_BLOCK, GMLP_BLOCK), 1) // CHUNK
    return col <= row


def _ln_plain(v):
    mu = jnp.mean(v, axis=-1, keepdims=True)
    vc = v - mu
    rstd = lax.rsqrt(jnp.mean(vc * vc, axis=-1, keepdims=True) + EPS)
    return vc * rstd, rstd


def gmlp_fwd(name, proj, w, b, tm=512):
    t = proj.shape[0]
    tm = _tile(t, tm)

    def body(au_ref, av_ref, w_ref, b_ref, o_ref):
        mask = _gmlp_mask()
        u = _gelu(au_ref[...])
        vn, _ = _ln_plain(_gelu(av_ref[...]))
        vnb = _bf(vn)
        for g in range(A_GROUPS):
            wg = _bf(jnp.where(mask, w_ref[g], 0.0))
            cs = slice(g * GMLP_BLOCK, (g + 1) * GMLP_BLOCK)
            for n in range(tm // GMLP_BLOCK):
                rs = slice(n * GMLP_BLOCK, (n + 1) * GMLP_BLOCK)
                sg = jnp.dot(wg, vnb[rs, cs], preferred_element_type=F32) + b_ref[g]
                o_ref[rs, cs] = (u[rs, cs] * sg).astype(o_ref.dtype)

    return pl.pallas_call(
        body, grid=(t // tm,),
        in_specs=[pl.BlockSpec((tm, A_WIDTH), lambda i: (i, 0)), pl.BlockSpec((tm, A_WIDTH), lambda i: (i, 1)),
                  pl.BlockSpec(w.shape, lambda i: (0, 0, 0)), pl.BlockSpec(b.shape, lambda i: (0, 0, 0))],
        out_specs=pl.BlockSpec((tm, A_WIDTH), lambda i: (i, 0)), out_shape=S((t, A_WIDTH), BF16),
        compiler_params=_cp("parallel"), name=name)(proj, proj, w, b)


def gmlp_bwd(name, proj, dcat, w, b, tm=512):
    t = proj.shape[0]
    tm = _tile(t, tm)

    def body(au_ref, av_ref, do_ref, w_ref, b_ref, dp_ref, dw_ref, db_ref):
        @pl.when(pl.program_id(0) == 0)
        def _():
            dw_ref[...] = jnp.zeros(dw_ref.shape, F32)
            db_ref[...] = jnp.zeros(db_ref.shape, F32)

        mask = _gmlp_mask()
        au = au_ref[...]
        av = av_ref[...]
        u = _gelu(au)
        vn, rstd = _ln_plain(_gelu(av))
        vnb = _bf(vn)
        dout = do_ref[...]
        dvn_cols = []
        for g in range(A_GROUPS):
            wm = jnp.where(mask, w_ref[g], 0.0)
            wg = _bf(wm)
            wgt = _bf(wm.T)
            cs = slice(g * GMLP_BLOCK, (g + 1) * GMLP_BLOCK)
            dwg = jnp.zeros((GMLP_BLOCK, GMLP_BLOCK), F32)
            dbg = jnp.zeros((GMLP_BLOCK, 1), F32)
            dvn_rows = []
            for n in range(tm // GMLP_BLOCK):
                rs = slice(n * GMLP_BLOCK, (n + 1) * GMLP_BLOCK)
                sg = jnp.dot(wg, vnb[rs, cs], preferred_element_type=F32) + b_ref[g]
                dp_ref[rs, cs] = (dout[rs, cs] * sg * _gelu_grad(au[rs, cs])).astype(dp_ref.dtype)
                dsg = dout[rs, cs] * u[rs, cs]
                dsgb = _bf(dsg)
                dbg = dbg + jnp.sum(dsg, axis=1, keepdims=True)
                dwg = dwg + lax.dot_general(dsgb, vnb[rs, cs], (((1,), (1,)), ((), ())), preferred_element_type=F32)
                dvn_rows.append(jnp.dot(wgt, dsgb, preferred_element_type=F32))
            dw_ref[g] += jnp.where(mask, dwg, 0.0)
            db_ref[g] += dbg
            dvn_cols.append(jnp.concatenate(dvn_rows, axis=0))
        dvn = jnp.concatenate(dvn_cols, axis=1)
        dv = rstd * (dvn - jnp.mean(dvn, axis=-1, keepdims=True) - vn * jnp.mean(dvn * vn, axis=-1, keepdims=True))
        dp_ref[:, A_WIDTH:] = (dv * _gelu_grad(av)).astype(dp_ref.dtype)

    return pl.pallas_call(
        body, grid=(t // tm,),
        in_specs=[pl.BlockSpec((tm, A_WIDTH), lambda i: (i, 0)), pl.BlockSpec((tm, A_WIDTH), lambda i: (i, 1)),
                  pl.BlockSpec((tm, A_WIDTH), lambda i: (i, 0)),
                  pl.BlockSpec(w.shape, lambda i: (0, 0, 0)), pl.BlockSpec(b.shape, lambda i: (0, 0, 0))],
        out_specs=[pl.BlockSpec((tm, 2 * A_WIDTH), lambda i: (i, 0)),
                   pl.BlockSpec(w.shape, lambda i: (0, 0, 0)), pl.BlockSpec(b.shape, lambda i: (0, 0, 0))],
        out_shape=[S((t, 2 * A_WIDTH), BF16), S(w.shape, F32), S(b.shape, F32)],
        compiler_params=_cp("arbitrary"), name=name)(proj, proj, dcat, w, b)


CONV_ROWS = 256


def conv_fwd(name, proj, w, cb):
    t = proj.shape[0]
    tc = LANES
    rows = _tile(t, CONV_ROWS)
    a_cb, g_cb = 2 * A_WIDTH // tc, (2 * A_WIDTH + B_WIDTH) // tc

    def body(a_ref, g_ref, w_ref, cb_ref, o_ref, hpad):
        hpad[0:CONV_PAD, :] = jnp.zeros((CONV_PAD, tc), F32)

        def fill(i, _):
            r0 = pl.multiple_of(i * rows, rows)
            hpad[pl.ds(CONV_PAD + r0, rows), :] = a_ref[pl.ds(r0, rows), :] * _sigmoid(g_ref[pl.ds(r0, rows), :])
            return 0
        lax.fori_loop(0, t // rows, fill, 0)

        def conv(i, _):
            r0 = pl.multiple_of(i * rows, rows)
            win = hpad[pl.ds(r0, rows + CONV_PAD), :]
            acc = jnp.zeros((rows, tc), F32) + cb_ref[...]
            for k in range(CONV_WIDTH):
                sh = CONV_WIDTH - 1 - k
                src = win if sh == 0 else pltpu.roll(win, sh, 0)
                acc = acc + src[CONV_PAD:, :] * w_ref[k:k + 1, :]
            o_ref[pl.ds(r0, rows), :] = acc
            return 0
        lax.fori_loop(0, t // rows, conv, 0)

    return pl.pallas_call(
        body, grid=(B_WIDTH // tc,),
        in_specs=[pl.BlockSpec((t, tc), lambda j: (0, a_cb + j)), pl.BlockSpec((t, tc), lambda j: (0, g_cb + j)),
                  pl.BlockSpec((CONV_WIDTH, tc), lambda j: (0, j)), pl.BlockSpec((1, tc), lambda j: (0, j))],
        out_specs=pl.BlockSpec((t, tc), lambda j: (0, j)), out_shape=S((t, B_WIDTH), F32),
        scratch_shapes=[pltpu.VMEM((t + CONV_PAD, tc), F32)],
        compiler_params=_cp("parallel"), name=name)(proj, proj, w, cb)


def conv_bwd(name, proj, dhc, w):
    t = proj.shape[0]
    tc = LANES
    rows = _tile(t, CONV_ROWS)
    a_cb, g_cb = 2 * A_WIDTH // tc, (2 * A_WIDTH + B_WIDTH) // tc
    win_rows = rows + CONV_PAD

    def body(a_ref, g_ref, d_ref, w_ref, da_ref, dg_ref, dw_ref, dcb_ref, hpad, dpad, dwacc):
        hpad[0:CONV_PAD, :] = jnp.zeros((CONV_PAD, tc), F32)
        dpad[t:t + CONV_PAD, :] = jnp.zeros((CONV_PAD, tc), F32)
        dwacc[...] = jnp.zeros(dwacc.shape, F32)

        def fill(i, _):
            r0 = pl.multiple_of(i * rows, rows)
            hpad[pl.ds(CONV_PAD + r0, rows), :] = a_ref[pl.ds(r0, rows), :] * _sigmoid(g_ref[pl.ds(r0, rows), :])
            dpad[pl.ds(r0, rows), :] = d_ref[pl.ds(r0, rows), :]
            return 0
        lax.fori_loop(0, t // rows, fill, 0)

        def step(i, dcb):
            r0 = pl.multiple_of(i * rows, rows)
            hwin = hpad[pl.ds(r0, win_rows), :]
            dwin = dpad[pl.ds(r0, win_rows), :]
            dchunk = dwin[:rows, :]
            dh = jnp.zeros((rows, tc), F32)
            for k in range(CONV_WIDTH):
                sh = CONV_WIDTH - 1 - k
                hsrc = hwin if sh == 0 else pltpu.roll(hwin, sh, 0)
                dsrc = dwin if sh == 0 else pltpu.roll(dwin, win_rows - sh, 0)
                dh = dh + dsrc[:rows, :] * w_ref[k:k + 1, :]
                prod = dchunk * hsrc[CONV_PAD:, :]
                dwacc[k] += jnp.sum(prod.reshape(rows // 8, 8, tc), axis=0)
            a = a_ref[pl.ds(r0, rows), :]
            sg = _sigmoid(g_ref[pl.ds(r0, rows), :])
            da_ref[pl.ds(r0, rows), :] = (dh * sg).astype(da_ref.dtype)
            dg_ref[pl.ds(r0, rows), :] = (dh * a * sg * (1.0 - sg)).astype(dg_ref.dtype)
            return dcb + jnp.sum(dchunk, axis=0, keepdims=True)
        dcb = lax.fori_loop(0, t // rows, step, jnp.zeros((1, tc), F32))
        dcb_ref[...] = dcb
        for k in range(CONV_WIDTH):
            dw_ref[k:k + 1, :] = jnp.sum(dwacc[k], axis=0, keepdims=True)

    return pl.pallas_call(
        body, grid=(B_WIDTH // tc,),
        in_specs=[pl.BlockSpec((t, tc), lambda j: (0, a_cb + j)), pl.BlockSpec((t, tc), lambda j: (0, g_cb + j)),
                  pl.BlockSpec((t, tc), lambda j: (0, j)), pl.BlockSpec((CONV_WIDTH, tc), lambda j: (0, j))],
        out_specs=[pl.BlockSpec((t, tc), lambda j: (0, j)), pl.BlockSpec((t, tc), lambda j: (0, j)),
                   pl.BlockSpec((CONV_WIDTH, tc), lambda j: (0, j)), pl.BlockSpec((1, tc), lambda j: (0, j))],
        out_shape=[S((t, B_WIDTH), BF16), S((t, B_WIDTH), BF16), S((CONV_WIDTH, B_WIDTH), F32), S((1, B_WIDTH), F32)],
        scratch_shapes=[pltpu.VMEM((t + CONV_PAD, tc), F32), pltpu.VMEM((t + CONV_PAD, tc), F32),
                        pltpu.VMEM((CONV_WIDTH, 8, tc), F32)],
        compiler_params=_cp("parallel"), name=name)(proj, proj, dhc, w)


def ln_silu_fwd(name, hc, g, b):
    def fn(h, gv, bv):
        y, _ = _ln_plain(h)
        z = y * gv + bv
        return [z * _sigmoid(z)], []
    return rows_call(name, fn, [hc], [g, b], [(hc.shape[1], BF16)], [])[0]


def ln_silu_bwd(name, hc, dcat, g, b):
    c = hc.shape[1]

    def fn(h, dout, gv, bv):
        y, rstd = _ln_plain(h)
        z = y * gv + bv
        s = _sigmoid(z)
        dz = dout * s * (1.0 + z * (1.0 - s))
        dyv = dz * gv
        dh = rstd * (dyv - jnp.mean(dyv, axis=-1, keepdims=True) - y * jnp.mean(dyv * y, axis=-1, keepdims=True))
        return [dh], [jnp.sum(dz * y, axis=0, keepdims=True), jnp.sum(dz, axis=0, keepdims=True)]

    return rows_call(name, fn, [hc, (dcat, 1, c)], [g, b], [(c, F32)], [(1, c), (1, c)])


_NT = (((1,), (1,)), ((), ()))
_TN = (((0,), (0,)), ((), ()))


def attn_fwd(name, q, k, v, tm=512):
    t, d = q.shape
    m = k.shape[0]
    tm = _tile(t, tm)
    scale = CA_HEAD_DIM ** -0.5

    def body(q_ref, k_ref, v_ref, o_ref):
        for h in range(CA_HEADS):
            cs = slice(h * CA_HEAD_DIM, (h + 1) * CA_HEAD_DIM)
            s = lax.dot_general(q_ref[:, cs], k_ref[:, cs], _NT, preferred_element_type=F32) * scale
            e = jnp.exp(s - jnp.max(s, axis=-1, keepdims=True))
            p = e / jnp.sum(e, axis=-1, keepdims=True)
            o_ref[:, cs] = jnp.dot(_bf(p), v_ref[:, cs], preferred_element_type=F32).astype(o_ref.dtype)

    return pl.pallas_call(
        body, grid=(t // tm,),
        in_specs=[pl.BlockSpec((tm, d), lambda i: (i, 0)), pl.BlockSpec((m, d), lambda i: (0, 0)),
                  pl.BlockSpec((m, d), lambda i: (0, 0))],
        out_specs=pl.BlockSpec((tm, d), lambda i: (i, 0)), out_shape=S((t, d), BF16),
        compiler_params=_cp("parallel"), name=name)(q, k, v)


def attn_bwd(name, q, k, v, do, tm=512):
    t, d = q.shape
    m = k.shape[0]
    tm = _tile(t, tm)
    scale = CA_HEAD_DIM ** -0.5

    def body(q_ref, k_ref, v_ref, do_ref, dq_ref, dk_ref, dv_ref):
        @pl.when(pl.program_id(0) == 0)
        def _():
            dk_ref[...] = jnp.zeros(dk_ref.shape, F32)
            dv_ref[...] = jnp.zeros(dv_ref.shape, F32)

        for h in range(CA_HEADS):
            cs = slice(h * CA_HEAD_DIM, (h + 1) * CA_HEAD_DIM)
            qh, kh, vh, doh = q_ref[:, cs], k_ref[:, cs], v_ref[:, cs], do_ref[:, cs]
            s = lax.dot_general(qh, kh, _NT, preferred_element_type=F32) * scale
            e = jnp.exp(s - jnp.max(s, axis=-1, keepdims=True))
            p = e / jnp.sum(e, axis=-1, keepdims=True)
            pb = _bf(p)
            dv_ref[:, cs] += lax.dot_general(pb, doh, _TN, preferred_element_type=F32)
            dp = lax.dot_general(doh, vh, _NT, preferred_element_type=F32)
            ds = _bf(p * (dp - jnp.sum(dp * p, axis=-1, keepdims=True)) * scale)
            dq_ref[:, cs] = jnp.dot(ds, kh, preferred_element_type=F32).astype(dq_ref.dtype)
            dk_ref[:, cs] += lax.dot_general(ds, qh, _TN, preferred_element_type=F32)

    return pl.pallas_call(
        body, grid=(t // tm,),
        in_specs=[pl.BlockSpec((tm, d), lambda i: (i, 0)), pl.BlockSpec((m, d), lambda i: (0, 0)),
                  pl.BlockSpec((m, d), lambda i: (0, 0)), pl.BlockSpec((tm, d), lambda i: (i, 0))],
        out_specs=[pl.BlockSpec((tm, d), lambda i: (i, 0)), pl.BlockSpec((m, d), lambda i: (0, 0)),
                   pl.BlockSpec((m, d), lambda i: (0, 0))],
        out_shape=[S((t, d), BF16), S((m, d), F32), S((m, d), F32)],
        compiler_params=_cp("arbitrary"), name=name)(q, k, v, do)


SUB = 8
S5_ROWS = 256


def s5_constants(lam_re, lam_im, log_dt, b_re, b_im, c_re, c_im):
    dt = jnp.exp(log_dt)[:, None]
    mag = jnp.exp(lam_re * dt)
    ar = mag * jnp.cos(lam_im * dt)
    ai = mag * jnp.sin(lam_im * dt)
    den = lam_re * lam_re + lam_im * lam_im
    qr = ((ar - 1.0) * lam_re + ai * lam_im) / den
    qi = (ai * lam_re - (ar - 1.0) * lam_im) / den
    bbr = qr[..., None] * b_re - qi[..., None] * b_im
    bbi = qr[..., None] * b_im + qi[..., None] * b_re
    eye = jnp.eye(C_GROUPS, dtype=F32)

    def in_mat(bb):
        return (bb.transpose(0, 2, 1)[:, :, None, :] * eye[:, None, :, None]).reshape(C_WIDTH, N_STATE)

    def out_mat(cc):
        return (cc.transpose(0, 2, 1)[:, :, None, :] * eye[:, None, :, None]).reshape(N_STATE, C_WIDTH)

    mb = jnp.concatenate([in_mat(bbr), in_mat(bbi)], axis=1)
    mc = jnp.concatenate([out_mat(c_re), -out_mat(c_im)], axis=0)
    a = jnp.stack([ar.reshape(N_STATE), ai.reshape(N_STATE)])
    return a, mb, mc


def _scan_powers(a, conj):
    ar, ai = a[0], (-a[1] if conj else a[1])
    pows = [(ar, ai)]
    for _ in range(SUB - 1):
        pr, pi = pows[-1]
        pows.append((pr * ar - pi * ai, pr * ai + pi * ar))
    rows = jnp.arange(SUB)[:, None]
    out = []
    for s in (1, 2, 4):
        keep = (rows + s <= SUB - 1) if conj else (rows >= s)
        out.append(jnp.stack([jnp.where(keep, pows[s - 1][0][None, :], 0.0), jnp.where(keep, pows[s - 1][1][None, :], 0.0)]))
    order = [SUB - 1 - i for i in range(SUB)] if conj else list(range(SUB))
    out.append(jnp.stack([jnp.stack([pows[i][0] for i in order]), jnp.stack([pows[i][1] for i in order])]))
    return jnp.stack(out)


def _cmul_add(xr, xi, pr, pi, zr, zi):
    return xr + pr * zr - pi * zi, xi + pr * zi + pi * zr


def s5_fwd(name, u, mb, mc, pw, dskip):
    t = u.shape[0]
    tm = _tile(t, S5_ROWS)
    ns = N_STATE

    def body(u_ref, mb_ref, mc_ref, pw_ref, d_ref, gy_ref, y_ref, xs_ref, xb_ref, carry):
        @pl.when(pl.program_id(0) == 0)
        def _():
            carry[...] = jnp.zeros(carry.shape, F32)

        uv = u_ref[...]
        xs_ref[...] = jnp.dot(_bf(uv), mb_ref[...], preferred_element_type=F32)

        def group(i, _):
            r0 = pl.multiple_of(i * SUB, SUB)
            xr = xs_ref[pl.ds(r0, SUB), 0:ns]
            xi = xs_ref[pl.ds(r0, SUB), ns:2 * ns]
            for k, s in enumerate((1, 2, 4)):
                xr, xi = _cmul_add(xr, xi, pw_ref[k, 0], pw_ref[k, 1], pltpu.roll(xr, s, 0), pltpu.roll(xi, s, 0))
            xr, xi = _cmul_add(xr, xi, pw_ref[3, 0], pw_ref[3, 1], carry[0], carry[1])
            xs_ref[pl.ds(r0, SUB), 0:ns] = xr
            xs_ref[pl.ds(r0, SUB), ns:2 * ns] = xi
            carry[0] = jnp.broadcast_to(xr[SUB - 1:SUB, :], (SUB, ns))
            carry[1] = jnp.broadcast_to(xi[SUB - 1:SUB, :], (SUB, ns))
            return 0
        lax.fori_loop(0, tm // SUB, group, 0)

        xb = _bf(xs_ref[...])
        xb_ref[...] = xb
        y = jnp.dot(xb, mc_ref[...], preferred_element_type=F32) + d_ref[...] * uv
        y_ref[...] = y
        gy_ref[...] = _gelu(y).astype(gy_ref.dtype)

    c = u.shape[1]
    return pl.pallas_call(
        body, grid=(t // tm,),
        in_specs=[pl.BlockSpec((tm, c), lambda i: (i, 0)), pl.BlockSpec(mb.shape, lambda i: (0, 0)),
                  pl.BlockSpec(mc.shape, lambda i: (0, 0)), pl.BlockSpec(pw.shape, lambda i: (0, 0, 0, 0)),
                  pl.BlockSpec((1, c), lambda i: (0, 0))],
        out_specs=[pl.BlockSpec((tm, c), lambda i: (i, 0)), pl.BlockSpec((tm, c), lambda i: (i, 0)),
                   pl.BlockSpec((tm, 2 * ns), lambda i: (i, 0)), pl.BlockSpec((tm, 2 * ns), lambda i: (i, 0))],
        out_shape=[S((t, c), BF16), S((t, c), F32), S((t, 2 * ns), F32), S((t, 2 * ns), BF16)],
        scratch_shapes=[pltpu.VMEM((2, SUB, ns), F32)],
        compiler_params=_cp("arbitrary"), name=name)(u, mb, mc, pw, dskip)


def s5_bwd(name, dgy, y, u, xs, mct, mbt, qw, dskip):
    t, c = u.shape
    tm = _tile(t, S5_ROWS)
    nt = t // tm
    ns = N_STATE
    ng = tm // SUB

    def body(dgy_ref, y_ref, u_ref, xs_ref, prev_ref, mct_ref, mbt_ref, qw_ref, d_ref,
             du_ref, dy_ref, lb_ref, da_ref, dd_ref, lam, carry):
        step = pl.program_id(0)

        @pl.when(step == 0)
        def _():
            carry[...] = jnp.zeros(carry.shape, F32)
            da_ref[...] = jnp.zeros(da_ref.shape, F32)
            dd_ref[...] = jnp.zeros(dd_ref.shape, F32)

        uv = u_ref[...]
        dy = dgy_ref[...] * _gelu_grad(y_ref[...])
        dyb = _bf(dy)
        dy_ref[...] = dyb
        dd_ref[...] += jnp.sum(dy * uv, axis=0, keepdims=True)
        lam[...] = jnp.dot(dyb, mct_ref[...], preferred_element_type=F32)
        first_tile = (step == nt - 1).astype(F32)
        row0 = lax.broadcasted_iota(jnp.int32, (SUB, ns), 0) == 0

        def group(j, _):
            i = ng - 1 - j
            r0 = pl.multiple_of(i * SUB, SUB)
            lr = lam[pl.ds(r0, SUB), 0:ns]
            li = lam[pl.ds(r0, SUB), ns:2 * ns]
            for k, s in enumerate((1, 2, 4)):
                lr, li = _cmul_add(lr, li, qw_ref[k, 0], qw_ref[k, 1],
                                   pltpu.roll(lr, SUB - s, 0), pltpu.roll(li, SUB - s, 0))
            lr, li = _cmul_add(lr, li, qw_ref[3, 0], qw_ref[3, 1], carry[0], carry[1])
            lam[pl.ds(r0, SUB), 0:ns] = lr
            lam[pl.ds(r0, SUB), ns:2 * ns] = li
            carry[0] = jnp.broadcast_to(lr[0:1, :], (SUB, ns))
            carry[1] = jnp.broadcast_to(li[0:1, :], (SUB, ns))
            rp = pl.multiple_of(jnp.maximum(i - 1, 0) * SUB, SUB)
            in_tile = (i > 0).astype(F32)
            out_tile = (1.0 - in_tile) * (1.0 - first_tile)
            pr = xs_ref[pl.ds(rp, SUB), 0:ns] * in_tile + prev_ref[:, 0:ns] * out_tile
            pi = xs_ref[pl.ds(rp, SUB), ns:2 * ns] * in_tile + prev_ref[:, ns:2 * ns] * out_tile
            xpr = jnp.where(row0, pltpu.roll(pr, 1, 0), pltpu.roll(xs_ref[pl.ds(r0, SUB), 0:ns], 1, 0))
            xpi = jnp.where(row0, pltpu.roll(pi, 1, 0), pltpu.roll(xs_ref[pl.ds(r0, SUB), ns:2 * ns], 1, 0))
            da_ref[0] += lr * xpr + li * xpi
            da_ref[1] += li * xpr - lr * xpi
            return 0
        lax.fori_loop(0, ng, group, 0)

        lb = _bf(lam[...])
        lb_ref[...] = lb
        du_ref[...] = (jnp.dot(lb, mbt_ref[...], preferred_element_type=F32) + d_ref[...] * dy).astype(du_ref.dtype)

    rev = lambda i: (nt - 1 - i, 0)
    prev = lambda i: (jnp.maximum((nt - 1 - i) * (tm // SUB) - 1, 0), 0)
    return pl.pallas_call(
        body, grid=(nt,),
        in_specs=[pl.BlockSpec((tm, c), rev), pl.BlockSpec((tm, c), rev), pl.BlockSpec((tm, c), rev),
                  pl.BlockSpec((tm, 2 * ns), rev), pl.BlockSpec((SUB, 2 * ns), prev),
                  pl.BlockSpec(mct.shape, lambda i: (0, 0)), pl.BlockSpec(mbt.shape, lambda i: (0, 0)),
                  pl.BlockSpec(qw.shape, lambda i: (0, 0, 0, 0)), pl.BlockSpec((1, c), lambda i: (0, 0))],
        out_specs=[pl.BlockSpec((tm, c), rev), pl.BlockSpec((tm, c), rev), pl.BlockSpec((tm, 2 * ns), rev),
                   pl.BlockSpec((2, SUB, ns), lambda i: (0, 0, 0)), pl.BlockSpec((1, c), lambda i: (0, 0))],
        out_shape=[S((t, c), BF16), S((t, c), BF16), S((t, 2 * ns), BF16), S((2, SUB, ns), F32), S((1, c), F32)],
        scratch_shapes=[pltpu.VMEM((tm, 2 * ns), F32), pltpu.VMEM((2, SUB, ns), F32)],
        compiler_params=_cp("arbitrary"), name=name)(dgy, y, u, xs, xs, mct, mbt, qw, dskip)


def _first(accs, *_):
    return [accs[0]]


def _add_res(accs, res):
    return [accs[0] + res]


def even_fwd(x, w):
    t = x.shape[0]
    hn, r = rms_fwd("e_norm_f", x, w["e_norm"])
    (proj,) = mm_nn("e_in_f", t, IN_WIDTH, [(hn, w["e_w_in_t"], 0, "t")], 1, _first, [F32])
    out_a = gmlp_fwd("e_gmlp_f", proj, w["e_gmlp_w"], w["e_gmlp_b"])
    hc = conv_fwd("e_conv_f", proj, w["e_conv_w"], w["e_conv_b"])
    out_b = ln_silu_fwd("e_ln_f", hc, w["e_conv_ln_g"], w["e_conv_ln_b"])
    (x1,) = mm_nn("e_out_f", t, D_MODEL, [(out_a, (w["e_w_out"], 0), 0), (out_b, (w["e_w_out"], 1), 0)],
                  1, _add_res, [F32], tiled=[x])
    return x1, (x, hn, r, proj, out_a, hc, out_b)


def even_bwd(dx, dxb, saved, w):
    x, hn, r, proj, out_a, hc, out_b = saved
    t = x.shape[0]
    (dcat,) = mm_nn("e_out_b", t, D_MODEL, [(dxb, w["e_w_out"], 0, "t")], 1, _first, [F32])
    g_w_out = jnp.concatenate([mm_tn("e_out_wa", out_a, dxb), mm_tn("e_out_wb", out_b, dxb)], axis=0)
    dab, g_gw, g_gb = gmlp_bwd("e_gmlp_b", proj, dcat, w["e_gmlp_w"], w["e_gmlp_b"])
    dhc, g_lg, g_lb = ln_silu_bwd("e_ln_b", hc, dcat, w["e_conv_ln_g"], w["e_conv_ln_b"])
    dba, dbg, g_cw, g_cb = conv_bwd("e_conv_b", proj, dhc, w["e_conv_w"])
    w_in_t = w["e_w_in_t"]
    (dhn,) = mm_nn("e_in_b", t, D_MODEL, [(dab, (w_in_t, 0), 0), (dba, (w_in_t, 2), 0), (dbg, (w_in_t, 3), 0)],
                   1, _first, [F32])
    g_w_in_t = jnp.concatenate([mm_tn("e_in_w0", dab, hn), mm_tn("e_in_w1", dba, hn), mm_tn("e_in_w2", dbg, hn)], axis=0)
    dx0, dx0b, g_norm = rms_bwd("e_norm_b", dhn, x, r, w["e_norm"], dres=dx)
    grads = dict(e_norm=g_norm, e_w_in_t=g_w_in_t, e_gmlp_w=g_gw[None], e_gmlp_b=g_gb.reshape(1, A_GROUPS, GMLP_BLOCK),
                 e_conv_w=g_cw[None], e_conv_b=g_cb, e_conv_ln_g=g_lg, e_conv_ln_b=g_lb, e_w_out=g_w_out)
    return dx0, dx0b, grads


def odd_fwd(x, w, consts):
    t = x.shape[0]
    _, mb, mc, pw, _ = consts
    hn, r = rms_fwd("o_norm_f", x, w["o_norm"])
    (u,) = mm_nn("o_in_f", t, C_WIDTH, [(hn, w["o_w_in"], 0)], 1, _first, [F32])
    gy, y, xs, xsb = s5_fwd("o_s5_f", u, _bf(mb), _bf(mc), pw, w["o_d"])
    w_out_t = w["o_w_out_t"]

    def epi(accs, res):
        return [res + accs[0] * _sigmoid(accs[1]), accs[0], accs[1]]

    x1, o1, o2 = mm_nn("o_out_f", t, D_MODEL, [(gy, (w_out_t, 0), 0, "t"), (gy, (w_out_t, D_MODEL), 1, "t")], 2, epi,
                       [F32, BF16, BF16], tiled=[x])
    return x1, (x, hn, r, u, gy, y, xs, xsb, o1, o2)


def odd_bwd(dx, dxb, saved, w, consts, consts_vjp):
    x, hn, r, u, gy, y, xs, xsb, o1, o2 = saved
    t = x.shape[0]
    _, mb, mc, _, qw = consts

    def gate_bwd(dv, a, b):
        a = a.astype(F32)
        sg = _sigmoid(b.astype(F32))
        return [jnp.concatenate([dv * sg, dv * a * sg * (1.0 - sg)], axis=1)], []

    (do12,) = rows_call("o_gate_b", gate_bwd, [dx, o1, o2], [], [(2 * D_MODEL, BF16)], [])
    (dgy,) = mm_nn("o_out_b", t, C_WIDTH, [(do12, w["o_w_out_t"], 0)], 1, _first, [F32])
    g_w_out_t = mm_tn("o_out_w", do12, gy)
    du, dyb, lamb, da8, g_d = s5_bwd("o_s5_b", dgy, y, u, xs, _bf(mc.T), _bf(mb.T), qw, w["o_d"])
    d_mb = mm_tn("o_s5_wb", u, lamb, out_dtype=F32)
    d_mc = mm_tn("o_s5_wc", xsb, dyb, out_dtype=F32)
    g_lr, g_li, g_dt, g_br, g_bi, g_cr, g_ci = consts_vjp((jnp.sum(da8, axis=1), d_mb, d_mc))
    g_w_in = mm_tn("o_in_w", hn, du)
    (dhn,) = mm_nn("o_in_b", t, D_MODEL, [(du, w["o_w_in"], 0, "t")], 1, _first, [F32])
    dx0, dx0b, g_norm = rms_bwd("o_norm_b", dhn, x, r, w["o_norm"], dres=dx)
    grads = dict(o_norm=g_norm, o_w_in=g_w_in, o_lam_re=g_lr[None], o_lam_im=g_li[None], o_log_dt=g_dt[None],
                 o_b_re=g_br[None], o_b_im=g_bi[None], o_c_re=g_cr[None], o_c_im=g_ci[None], o_d=g_d, o_w_out_t=g_w_out_t)
    return dx0, dx0b, grads


def ca_fwd(i, x, mem, w):
    t, m = x.shape[0], mem.shape[0]
    xn, r = rms_fwd(f"ca{i}_norm_f", x, w["ca_norm"][i:i + 1])
    mn, rm = rms_fwd(f"ca{i}_mnorm_f", mem, w["ca_mem_norm"][i:i + 1])
    (q,) = mm_nn(f"ca{i}_q_f", t, D_MODEL, [(xn, w["ca_wq"][i], 0)], 1, _first, [BF16])
    k, v = mm_nn(f"ca{i}_kv_f", m, D_MODEL, [(mn, w["ca_wk"][i], 0), (mn, w["ca_wv"][i], 1)], 2,
                 lambda accs: [accs[0], accs[1]], [BF16, BF16])
    o = attn_fwd(f"ca{i}_attn_f", q, k, v)
    (x1,) = mm_nn(f"ca{i}_o_f", t, D_MODEL, [(o, w["ca_wo"][i], 0)], 1, _add_res, [F32], tiled=[x])
    return x1, (x, xn, r, mn, rm, q, k, v, o)


def ca_bwd(i, dx, dxb, saved, mem, w):
    x, xn, r, mn, rm, q, k, v, o = saved
    t, m = x.shape[0], mem.shape[0]
    (do,) = mm_nn(f"ca{i}_o_b", t, D_MODEL, [(dxb, w["ca_wo"][i], 0, "t")], 1, _first, [BF16])
    g_wo = mm_tn(f"ca{i}_o_w", o, dxb)
    dq, dk, dv = attn_bwd(f"ca{i}_attn_b", q, k, v, do)
    g_wq = mm_tn(f"ca{i}_q_w", xn, dq)
    g_wk = mm_tn(f"ca{i}_k_w", mn, dk)
    g_wv = mm_tn(f"ca{i}_v_w", mn, dv)
    (dxn,) = mm_nn(f"ca{i}_q_b", t, D_MODEL, [(dq, w["ca_wq"][i], 0, "t")], 1, _first, [F32])
    (dmn,) = mm_nn(f"ca{i}_kv_b", m, D_MODEL, [(dk, w["ca_wk"][i], 0, "t"), (dv, w["ca_wv"][i], 0, "t")], 1, _first, [F32])
    g_mnorm = rms_bwd_gain_only(f"ca{i}_mnorm_b", dmn, mem, rm)
    dx0, dx0b, g_norm = rms_bwd(f"ca{i}_norm_b", dxn, x, r, w["ca_norm"][i:i + 1], dres=dx)
    return dx0, dx0b, dict(ca_norm=g_norm, ca_mem_norm=g_mnorm, ca_wq=g_wq, ca_wk=g_wk, ca_wv=g_wv, ca_wo=g_wo)


def ffn_fwd(i, x, w):
    t = x.shape[0]
    xn, r = rms_fwd(f"ffn{i}_norm_f", x, w["ffn_norm"][i:i + 1])

    def epi(accs):
        g, u = accs
        return [g, u, g * _sigmoid(g) * u]

    g, u, h = mm_nn(f"ffn{i}_up_f", t, FFN_HIDDEN, [(xn, w["ffn_w_gate_t"][i], 0, "t"), (xn, w["ffn_w_up_t"][i], 1, "t")],
                    2, epi, [BF16, BF16, BF16])
    (x1,) = mm_nn(f"ffn{i}_down_f", t, D_MODEL, [(h, w["ffn_w_down"][i], 0)], 1, _add_res, [F32], tiled=[x])
    return x1, (x, xn, r, g, u, h)


def ffn_bwd(i, dx, dxb, saved, w):
    x, xn, r, g, u, h = saved
    t = x.shape[0]

    def epi(accs, gv, uv):
        dh = accs[0]
        gv = gv.astype(F32)
        uv = uv.astype(F32)
        s = _sigmoid(gv)
        return [dh * uv * s * (1.0 + gv * (1.0 - s)), dh * gv * s]

    dg, du = mm_nn(f"ffn{i}_down_b", t, FFN_HIDDEN, [(dxb, w["ffn_w_down"][i], 0, "t")], 1, epi, [BF16, BF16], tiled=[g, u])
    g_wd = mm_tn(f"ffn{i}_down_w", h, dxb)
    g_wg_t = mm_tn(f"ffn{i}_gate_w", dg, xn)
    g_wu_t = mm_tn(f"ffn{i}_up_w", du, xn)
    (dxn,) = mm_nn(f"ffn{i}_up_b", t, D_MODEL, [(dg, w["ffn_w_gate_t"][i], 0), (du, w["ffn_w_up_t"][i], 0)], 1, _first, [F32])
    dx0, dx0b, g_norm = rms_bwd(f"ffn{i}_norm_b", dxn, x, r, w["ffn_norm"][i:i + 1], dres=dx)
    return dx0, dx0b, dict(ffn_norm=g_norm, ffn_w_gate_t=g_wg_t, ffn_w_up_t=g_wu_t, ffn_w_down=g_wd)


_S5_PARAMS = ("o_lam_re", "o_lam_im", "o_log_dt", "o_b_re", "o_b_im", "o_c_re", "o_c_im")


def local_step(x, mem, target, w, fetch=None):
    def consts_fn(*p):
        a, mb, mc = s5_constants(*p)
        return a, mb, mc

    (a, mb, mc), consts_vjp = jax.vjp(consts_fn, *[w[k] for k in _S5_PARAMS])
    consts = (a, mb, mc, _scan_powers(a, False), _scan_powers(a, True))

    def need(stage, after):
        if fetch is not None:
            for k, v in fetch(stage, after).items():
                if isinstance(k, tuple):
                    w.setdefault(k[0], {})[k[1]] = v
                else:
                    w[k] = v

    need(0, x)
    x1, s_e = even_fwd(x, w)
    need(1, x1)
    x2, s_c0 = ca_fwd(0, x1, mem, w)
    need(2, x2)
    x3, s_f0 = ffn_fwd(0, x2, w)
    x4, s_o = odd_fwd(x3, w, consts)
    need(3, x4)
    x5, s_c1 = ca_fwd(1, x4, mem, w)
    x6, s_f1 = ffn_fwd(1, x5, w)
    dx, dxb, g_final, loss = final_loss("final_loss", x6, w["final_norm"], target)

    dx, dxb, g_f1 = ffn_bwd(1, dx, dxb, s_f1, w)
    dx, dxb, g_c1 = ca_bwd(1, dx, dxb, s_c1, mem, w)
    dx, dxb, g_o = odd_bwd(dx, dxb, s_o, w, consts, consts_vjp)
    dx, dxb, g_f0 = ffn_bwd(0, dx, dxb, s_f0, w)
    dx, dxb, g_c0 = ca_bwd(0, dx, dxb, s_c0, mem, w)
    dx, dxb, g_e = even_bwd(dx, dxb, s_e, w)

    grads = dict(g_e)
    grads.update(g_o)
    for g0, g1 in ((g_c0, g_c1), (g_f0, g_f1)):
        for k in g0:
            grads[k] = jnp.concatenate([g0[k], g1[k]], axis=0) if k.endswith("norm") else (g0[k], g1[k])
    grads["final_norm"] = g_final
    return loss, dx, grads


def _group(axes):
    pos = {a: lax.axis_index(a) for a in ("x", "y", "c")}
    me = 0
    for a in axes:
        me = me * 2 + pos[a]
    peers = []
    for mask in range(1, 2 ** len(axes)):
        peer = dict(pos)
        for bit, a in enumerate(axes):
            if (mask >> (len(axes) - 1 - bit)) & 1:
                peer[a] = 1 - pos[a]
        idx = 0
        for a in axes:
            idx = idx * 2 + peer[a]
        peers.append((idx, (peer["x"], peer["y"], peer["c"])))
    return me, peers


def _sibling():
    x, y, c = lax.axis_index("x"), lax.axis_index("y"), lax.axis_index("c")
    return c, (x, y, 1 - c)


def gather_ici(name, blks):
    k_ops = len(blks)
    out_shape = [S((4, 2) + tuple(b.shape), b.dtype) for b in blks]

    def body(*refs):
        in_refs, out_refs = refs[:k_ops], refs[k_ops:2 * k_ops]
        send_sems, recv_sems, local_sems = refs[2 * k_ops:]
        me, peers = _group(("x", "y"))
        core = lax.axis_index("c")
        local, sent, landed = [], [], []
        for i in range(k_ops):
            cp = pltpu.make_async_copy(in_refs[i], out_refs[i].at[me, core], local_sems.at[i])
            cp.start()
            local.append(cp)
        for k, (idx, dev) in enumerate(peers):
            for i in range(k_ops):
                s = i * 3 + k
                cp = pltpu.make_async_remote_copy(src_ref=in_refs[i], dst_ref=out_refs[i].at[me, core], send_sem=send_sems.at[s],
                                                  recv_sem=recv_sems.at[s], device_id=dev, device_id_type=MESH)
                cp.start()
                sent.append(cp)
                landed.append(pltpu.make_async_remote_copy(src_ref=in_refs[i], dst_ref=out_refs[i].at[idx, core],
                                                           send_sem=send_sems.at[s], recv_sem=recv_sems.at[s],
                                                           device_id=dev, device_id_type=MESH))
        for cp in landed:
            cp.wait_recv()
        for cp in sent:
            cp.wait_send()
        for cp in local:
            cp.wait()

    return pl.pallas_call(
        body, in_specs=[ANY] * k_ops, out_specs=[ANY] * k_ops, out_shape=out_shape,
        scratch_shapes=[pltpu.SemaphoreType.DMA((k_ops * 3,)), pltpu.SemaphoreType.DMA((k_ops * 3,)),
                        pltpu.SemaphoreType.DMA((k_ops,))],
        name=name)(*blks)


_HBM = pl.BlockSpec(memory_space=pltpu.HBM)
_SEM = pl.BlockSpec(memory_space=pltpu.SEMAPHORE)
_EFFECT = pltpu.SideEffectType.DATAFLOW_SIDE_EFFECTING


def gather_ici_start(name, groups):
    flat = [b for g in groups for b in g]
    sizes = [len(g) for g in groups]
    k_ops, n_g = len(flat), len(groups)
    lands = [lax.empty((4, 2) + tuple(b.shape), b.dtype) for b in flat]

    def body(*refs):
        src, land = refs[:k_ops], refs[k_ops:2 * k_ops]
        sems = refs[2 * k_ops:2 * k_ops + 3 * n_g]
        token = refs[-1]
        me, peers = _group(("x", "y"))
        core = lax.axis_index("c")
        i = 0
        for g in range(n_g):
            send, recv, loc = sems[3 * g:3 * g + 3]
            for j in range(sizes[g]):
                pltpu.make_async_copy(src[i], land[i].at[me, core], loc.at[j]).start()
                for k, (_, dev) in enumerate(peers):
                    pltpu.make_async_remote_copy(src_ref=src[i], dst_ref=land[i].at[me, core], send_sem=send.at[3 * j + k],
                                                 recv_sem=recv.at[3 * j + k], device_id=dev, device_id_type=MESH).start()
                i += 1
        token[...] = jnp.zeros(token.shape, token.dtype)

    sem_shapes = []
    for s in sizes:
        sem_shapes += [pltpu.SemaphoreType.DMA((3 * s,)), pltpu.SemaphoreType.DMA((3 * s,)), pltpu.SemaphoreType.DMA((s,))]
    thru = [pltpu.HBM(a.shape, a.dtype) for a in flat + lands]
    outs = pl.pallas_call(
        body, name=name, out_shape=tuple(sem_shapes) + tuple(thru) + (S((8, LANES), F32),),
        in_specs=[_HBM] * (2 * k_ops), out_specs=[_SEM] * (3 * n_g) + [_HBM] * (2 * k_ops) + [pl.BlockSpec(memory_space=pltpu.VMEM)],
        input_output_aliases={i: 3 * n_g + i for i in range(2 * k_ops)},
        compiler_params=pltpu.CompilerParams(has_side_effects=_EFFECT),
    )(*[pltpu.with_memory_space_constraint(a, pltpu.HBM) for a in flat + lands])
    sems = [tuple(outs[3 * g:3 * g + 3]) for g in range(n_g)]
    srcs_thru, lands_thru, off = [], [], 3 * n_g
    for s in sizes:
        srcs_thru.append(list(outs[off:off + s]))
        off += s
    for s in sizes:
        lands_thru.append(list(outs[off:off + s]))
        off += s
    return sems, srcs_thru, lands_thru, outs[-1]


def gather_ici_wait(name, srcs, lands, sems, after):
    n = len(srcs)

    def body(*refs):
        src, land = refs[:n], refs[n:2 * n]
        send, recv, loc = refs[2 * n:2 * n + 3]
        me, peers = _group(("x", "y"))
        core = lax.axis_index("c")
        for j in range(n):
            for k, (idx, dev) in enumerate(peers):
                cp = pltpu.make_async_remote_copy(src_ref=src[j], dst_ref=land[j].at[idx, core], send_sem=send.at[3 * j + k],
                                                  recv_sem=recv.at[3 * j + k], device_id=dev, device_id_type=MESH)
                cp.wait_send()
                cp.wait_recv()
            pltpu.make_async_copy(src[j], land[j].at[me, core], loc.at[j]).wait()

    outs = pl.pallas_call(
        body, name=name, out_shape=tuple(pltpu.HBM(a.shape, a.dtype) for a in list(srcs) + list(lands)),
        in_specs=[_HBM] * (2 * n) + [_SEM] * 3 + [ANY], out_specs=[_HBM] * (2 * n),
        input_output_aliases={i: i for i in range(2 * n)},
        compiler_params=pltpu.CompilerParams(has_side_effects=_EFFECT),
    )(*srcs, *lands, *sems, after)
    return list(outs[n:])


def gather_d2d(name, bufs):
    k_ops = len(bufs)

    def body(*refs):
        in_refs, out_refs = refs[:k_ops], refs[k_ops:2 * k_ops]
        send_sems, recv_sems = refs[2 * k_ops:]
        core, sib = _sibling()
        sent, landed = [], []
        for i in range(k_ops):
            cp = pltpu.make_async_remote_copy(src_ref=in_refs[i].at[:, core], dst_ref=out_refs[i].at[:, core],
                                              send_sem=send_sems.at[i], recv_sem=recv_sems.at[i], device_id=sib, device_id_type=MESH)
            cp.start()
            sent.append(cp)
            landed.append(pltpu.make_async_remote_copy(src_ref=in_refs[i].at[:, core], dst_ref=out_refs[i].at[:, 1 - core],
                                                       send_sem=send_sems.at[i], recv_sem=recv_sems.at[i],
                                                       device_id=sib, device_id_type=MESH))
        for cp in landed:
            cp.wait_recv()
        for cp in sent:
            cp.wait_send()

    return pl.pallas_call(
        body, in_specs=[ANY] * k_ops, out_specs=[ANY] * k_ops, out_shape=[S(b.shape, b.dtype) for b in bufs],
        input_output_aliases={i: i for i in range(k_ops)},
        scratch_shapes=[pltpu.SemaphoreType.DMA((k_ops,)), pltpu.SemaphoreType.DMA((k_ops,))],
        name=name)(*bufs)


def all_gather(name, blks):
    bufs = gather_d2d(name + "_d2d", gather_ici(name + "_ici", blks))
    return [p.reshape((N_DEV * b.shape[0],) + tuple(b.shape[1:])) for p, b in zip(bufs, blks)]


def scatter_d2d(name, pack, chunks):
    q, _, rows, c = pack.shape
    rc = rows // chunks

    def body(in_ref, out_ref, send_sems, recv_sems):
        core, sib = _sibling()
        sent = []
        for ch in range(chunks):
            rs = pl.ds(ch * rc, rc)
            cp = pltpu.make_async_remote_copy(src_ref=in_ref.at[:, 1 - core, rs], dst_ref=out_ref.at[:, rs],
                                              send_sem=send_sems.at[ch], recv_sem=recv_sems.at[ch], device_id=sib, device_id_type=MESH)
            cp.start()
            sent.append(cp)
        for cp in sent:
            cp.wait_recv()
        for cp in sent:
            cp.wait_send()

    return pl.pallas_call(
        body, in_specs=[ANY], out_specs=ANY, out_shape=S((q, rows, c), pack.dtype),
        scratch_shapes=[pltpu.SemaphoreType.DMA((chunks,)), pltpu.SemaphoreType.DMA((chunks,))],
        name=name)(pack)


def scatter_ici(name, arr):
    def body(in_ref, out_ref, send_sems, recv_sems):
        me, peers = _group(("x", "y"))
        sent, landed = [], []
        for k, (idx, dev) in enumerate(peers):
            cp = pltpu.make_async_remote_copy(src_ref=in_ref.at[idx], dst_ref=out_ref.at[me], send_sem=send_sems.at[k],
                                              recv_sem=recv_sems.at[k], device_id=dev, device_id_type=MESH)
            cp.start()
            sent.append(cp)
            landed.append(pltpu.make_async_remote_copy(src_ref=in_ref.at[idx], dst_ref=out_ref.at[idx], send_sem=send_sems.at[k],
                                                       recv_sem=recv_sems.at[k], device_id=dev, device_id_type=MESH))
        for cp in landed:
            cp.wait_recv()
        for cp in sent:
            cp.wait_send()

    return pl.pallas_call(
        body, in_specs=[ANY], out_specs=ANY, out_shape=S(arr.shape, arr.dtype),
        scratch_shapes=[pltpu.SemaphoreType.DMA((3,)), pltpu.SemaphoreType.DMA((3,))],
        name=name)(arr)


def sum_pair(name, pack, recv, core, tr=256):
    q, rows, c = recv.shape
    tr = _tile(rows, tr)

    def body(core_ref, a_ref, b_ref, o_ref):
        o_ref[...] = (a_ref[...].astype(F32) + b_ref[...].astype(F32)).astype(o_ref.dtype)

    spec = pltpu.PrefetchScalarGridSpec(
        num_scalar_prefetch=1, grid=(q, rows // tr),
        in_specs=[pl.BlockSpec((None, None, tr, c), lambda j, i, core: (j, core[0], i, 0)),
                  pl.BlockSpec((None, tr, c), lambda j, i, core: (j, i, 0))],
        out_specs=pl.BlockSpec((None, tr, c), lambda j, i, core: (j, i, 0)))
    return pl.pallas_call(body, grid_spec=spec, out_shape=S(recv.shape, recv.dtype),
                          compiler_params=_cp("parallel", "parallel"), name=name)(core, pack, recv)


def sum_quad(name, own, recv, chip, tr=256):
    _, rows, c = recv.shape
    tr = _tile(rows, tr)

    def body(chip_ref, a_ref, r1_ref, r2_ref, r3_ref, o_ref):
        o_ref[...] = ((a_ref[...].astype(F32) + r1_ref[...].astype(F32)) + r2_ref[...].astype(F32)) + r3_ref[...].astype(F32)

    def slot(mask):
        return pl.BlockSpec((None, tr, c), lambda i, chip, mask=mask: (jnp.bitwise_xor(chip[0], mask), i, 0))

    spec = pltpu.PrefetchScalarGridSpec(
        num_scalar_prefetch=1, grid=(rows // tr,), in_specs=[slot(0), slot(1), slot(2), slot(3)],
        out_specs=pl.BlockSpec((tr, c), lambda i, chip: (i, 0)))
    return pl.pallas_call(body, grid_spec=spec, out_shape=S((rows, c), F32),
                          compiler_params=_cp("parallel"), name=name)(chip, own, recv, recv, recv)


def adamw_native(name, g, w, m, v, tr=512):
    shape = w.shape
    cols = shape[-1]
    rows = w.size // cols
    tr = _tile(rows, tr) if rows % 8 == 0 else rows
    c1 = 1.0 - ADAM_B1 ** ADAM_STEP
    c2 = 1.0 - ADAM_B2 ** ADAM_STEP

    def body(g_ref, w_ref, m_ref, v_ref, d_ref, m2_ref, v2_ref):
        gv = g_ref[...]
        m2 = ADAM_B1 * m_ref[...] + (1.0 - ADAM_B1) * gv
        v2 = ADAM_B2 * v_ref[...] + (1.0 - ADAM_B2) * (gv * gv)
        m2_ref[...] = m2
        v2_ref[...] = v2
        d_ref[...] = -ADAM_LR * ((m2 / c1) / (jnp.sqrt(v2 / c2) + ADAM_EPS) + ADAM_WD * w_ref[...])

    row = pl.BlockSpec((tr, cols), lambda i: (i, 0))
    outs = pl.pallas_call(body, grid=(rows // tr,), in_specs=[row] * 4, out_specs=[row] * 3,
                          out_shape=[S((rows, cols), F32)] * 3, compiler_params=_cp("parallel"),
                          name=name)(*[a.reshape(rows, cols) for a in (g, w, m, v)])
    return tuple(o.reshape(shape) for o in outs)


def adamw_call(name, slots, w, m, v, tr=1024):
    n, r, c = slots.shape
    tr = _tile(r, tr)
    c1 = 1.0 - ADAM_B1 ** ADAM_STEP
    c2 = 1.0 - ADAM_B2 ** ADAM_STEP

    def body(s_ref, w_ref, m_ref, v_ref, g_ref, d_ref, m2_ref, v2_ref):
        g = s_ref[0].astype(F32)
        for j in range(1, n):
            g = g + s_ref[j].astype(F32)
        m2 = ADAM_B1 * m_ref[...] + (1.0 - ADAM_B1) * g
        v2 = ADAM_B2 * v_ref[...] + (1.0 - ADAM_B2) * (g * g)
        g_ref[...] = g
        m2_ref[...] = m2
        v2_ref[...] = v2
        d_ref[...] = -ADAM_LR * ((m2 / c1) / (jnp.sqrt(v2 / c2) + ADAM_EPS) + ADAM_WD * w_ref[...])

    row = pl.BlockSpec((tr, c), lambda i: (i, 0))
    return pl.pallas_call(body, grid=(r // tr,), in_specs=[pl.BlockSpec((n, tr, c), lambda i: (0, i, 0)), row, row, row],
                          out_specs=[row, row, row, row], out_shape=[S((r, c), F32)] * 4,
                          compiler_params=_cp("parallel"), name=name)(slots, w, m, v)


_REPLICATED = ("e_norm", "e_gmlp_w", "e_gmlp_b", "e_conv_b", "e_conv_ln_g", "e_conv_ln_b", "o_lam_re", "o_lam_im", "o_log_dt",
               "o_b_re", "o_b_im", "o_c_re", "o_c_im", "ca_norm", "ca_mem_norm", "ffn_norm", "final_norm")
_ORDER = ("e_norm", "e_w_in", "e_gmlp_w", "e_gmlp_b", "e_conv_w", "e_conv_b", "e_conv_ln_g", "e_conv_ln_b", "e_w_out",
          "o_norm", "o_w_in", "o_lam_re", "o_lam_im", "o_log_dt", "o_b_re", "o_b_im", "o_c_re", "o_c_im", "o_d", "o_w_out",
          "ca_norm", "ca_mem_norm", "ca_wq", "ca_wk", "ca_wv", "ca_wo", "ffn_norm", "ffn_w_gate", "ffn_w_up", "ffn_w_down",
          "final_norm")


def _rows128(a, multiple=8):
    flat = a.reshape(-1)
    rows = -(-flat.shape[0] // (LANES * multiple)) * multiple
    return jnp.pad(flat, (0, rows * LANES - flat.shape[0])).reshape(rows, LANES)


def _shard(full, axis):
    s = full.shape
    return jnp.moveaxis(full.reshape(s[:axis] + (N_DEV, s[axis] // N_DEV) + s[axis + 1:]), axis, 0)


_UNITS = (("e_w_in", 0, True), ("e_w_out", 0, False), ("o_w_in", 0, False), ("o_w_out", 0, True),
          *[(n, i, False) for n in ("ca_wq", "ca_wk", "ca_wv", "ca_wo") for i in (0, 1)],
          *[(n, i, tr) for n, tr in (("ffn_w_gate", True), ("ffn_w_up", True), ("ffn_w_down", False)) for i in (0, 1)])
_LAYERED = ("ca_wq", "ca_wk", "ca_wv", "ca_wo", "ffn_w_gate", "ffn_w_up", "ffn_w_down")
_SMALL_SHARDED = (("e_conv_w", 2), ("o_norm", 1), ("o_d", 1))
RS_ROW = 1024
RS_ROWS = 3840
RS_CHUNKS = 8


def _unit_key(name, tr):
    return name + "_t" if tr else name


def _stage_of(name, layer):
    if name.startswith("e_"):
        return 0
    if name.startswith("o_"):
        return 2
    if name.startswith("ca_"):
        return 1 if layer == 0 else 3
    return 2 if layer == 0 else 3


def weight_fetcher(local):
    groups, meta = [[] for _ in range(4)], [[] for _ in range(4)]
    for name, layer, tr in _UNITS:
        blk = local[name][layer]
        st = _stage_of(name, layer)
        groups[st].append(_bf(blk.T if tr else blk))
        meta[st].append((name, layer, tr))
    small = jnp.concatenate([local[name].reshape(-1) for name, _ in _SMALL_SHARDED])
    groups[0].append(_rows128(small))
    sems, srcs, lands, token = gather_ici_start("ag_w_start", groups)

    def fetch(stage, after):
        if stage == 0:
            after = token
        landed = gather_ici_wait(f"ag_w_wait{stage}", srcs[stage], lands[stage], sems[stage], after)
        bufs = gather_d2d(f"ag_w_d2d{stage}", landed)
        got = {}
        for (name, layer, tr), blk, buf in zip(meta[stage], groups[stage], bufs):
            arr = buf.reshape((N_DEV * blk.shape[0],) + tuple(blk.shape[1:]))
            if name in _LAYERED:
                got[(_unit_key(name, tr), layer)] = arr
            else:
                got[_unit_key(name, tr)] = arr
        if stage == 0:
            flat = bufs[-1].reshape(N_DEV, -1)
            off = 0
            for name, axis in _SMALL_SHARDED:
                blk = local[name]
                seg = flat[:, off:off + blk.size].reshape((N_DEV,) + blk.shape)
                off += blk.size
                seg = jnp.moveaxis(seg, 0, axis)
                got[name] = seg.reshape(seg.shape[:axis] + (-1,) + seg.shape[axis + 2:])
            got["e_conv_w"] = got["e_conv_w"][0]
        return got

    return fetch


def reduce_sharded(grads, local, mom, var):
    parts, spans = [], []
    for name, layer, tr in _UNITS:
        g = grads[_unit_key(name, tr)]
        g = g[layer] if name in _LAYERED else g
        part = g.reshape(4, 2, -1, RS_ROW)
        spans.append((part.shape[2], g.shape[0] // N_DEV, g.shape[1]))
        parts.append(part)
    small = jnp.concatenate([_shard(grads[name], axis).reshape(N_DEV, -1) for name, axis in _SMALL_SHARDED], axis=1)
    n_small = small.shape[1]
    small_rows = 16
    parts.append(jnp.pad(small, ((0, 0), (0, small_rows * RS_ROW - n_small))).astype(BF16).reshape(4, 2, small_rows, RS_ROW))
    used = sum(p.shape[2] for p in parts)
    parts.append(jnp.zeros((4, 2, RS_ROWS - used, RS_ROW), BF16))
    pack = jnp.concatenate(parts, axis=2)
    core = lax.axis_index("c").astype(jnp.int32).reshape(1)
    chip = (2 * lax.axis_index("x") + lax.axis_index("y")).astype(jnp.int32).reshape(1)
    from_sibling = scatter_d2d("rs_g_d2d", pack, RS_CHUNKS)
    chip_sum = sum_pair("rs_g_sum2", pack, from_sibling, core)
    from_chips = scatter_ici("rs_g_ici", chip_sum)
    total = sum_quad("rs_g_sum4", chip_sum, from_chips, chip)

    res, off, per_layer = {}, 0, {}
    for (name, layer, tr), (rows, r, c) in zip(_UNITS, spans):
        g = total[off:off + rows].reshape(r, c)
        off += rows
        per_layer.setdefault(name, []).append(g.T if tr else g)
    for name, gs in per_layer.items():
        g = jnp.stack(gs) if name in _LAYERED else gs[0][None]
        res[name] = (g,) + adamw_native("adamw_" + name, g, local[name], mom[name], var[name])
    flat = total[off:off + small_rows].reshape(-1)
    off = 0
    for name, _ in _SMALL_SHARDED:
        blk = local[name]
        g = flat[off:off + blk.size].reshape(blk.shape)
        off += blk.size
        res[name] = (g,) + adamw_native("adamw_" + name, g, blk, mom[name], var[name])
    return res


def reduce_replicated(grads, loss, w, mom, var):
    def pack(src, last):
        return jnp.concatenate([_rows128(src[name]) for name in _REPLICATED] + [_rows128(last)], axis=0)

    (slots,) = all_gather("ag_g", [pack(grads, loss)])
    zero = jnp.zeros((1, 1), F32)
    rows = slots.shape[0] // N_DEV
    outs = adamw_call("adamw_replicated", slots.reshape(N_DEV, rows, LANES), pack(w, zero), pack(mom, zero), pack(var, zero),
                      tr=rows)
    res, off = {}, 0
    for name in _REPLICATED:
        n = w[name].size
        nr = _rows128(w[name]).shape[0]
        res[name] = tuple(o[off:off + nr].reshape(-1)[:n].reshape(w[name].shape) for o in outs)
        off += nr
    return res, outs[0][off, 0]


def kernel(x, mem, e_norm, e_w_in, e_gmlp_w, e_gmlp_b, e_conv_w, e_conv_b, e_conv_ln_g, e_conv_ln_b, e_w_out, o_norm, o_w_in, o_lam_re, o_lam_im, o_log_dt, o_b_re, o_b_im, o_c_re, o_c_im, o_d, o_w_out, ca_norm, ca_mem_norm, ca_wq, ca_wk, ca_wv, ca_wo, ffn_norm, ffn_w_gate, ffn_w_up, ffn_w_down, final_norm, loss_target, m_e_norm, m_e_w_in, m_e_gmlp_w, m_e_gmlp_b, m_e_conv_w, m_e_conv_b, m_e_conv_ln_g, m_e_conv_ln_b, m_e_w_out, m_o_norm, m_o_w_in, m_o_lam_re, m_o_lam_im, m_o_log_dt, m_o_b_re, m_o_b_im, m_o_c_re, m_o_c_im, m_o_d, m_o_w_out, m_ca_norm, m_ca_mem_norm, m_ca_wq, m_ca_wk, m_ca_wv, m_ca_wo, m_ffn_norm, m_ffn_w_gate, m_ffn_w_up, m_ffn_w_down, m_final_norm, v_e_norm, v_e_w_in, v_e_gmlp_w, v_e_gmlp_b, v_e_conv_w, v_e_conv_b, v_e_conv_ln_g, v_e_conv_ln_b, v_e_w_out, v_o_norm, v_o_w_in, v_o_lam_re, v_o_lam_im, v_o_log_dt, v_o_b_re, v_o_b_im, v_o_c_re, v_o_c_im, v_o_d, v_o_w_out, v_ca_norm, v_ca_mem_norm, v_ca_wq, v_ca_wk, v_ca_wv, v_ca_wo, v_ffn_norm, v_ffn_w_gate, v_ffn_w_up, v_ffn_w_down, v_final_norm):
    given = dict(locals())
    local = {k: given[k] for k in _ORDER}
    mom = {k: given["m_" + k] for k in _ORDER}
    var = {k: given["v_" + k] for k in _ORDER}

    w = {}
    w.update({
        "e_norm": e_norm, "e_gmlp_w": e_gmlp_w[0], "e_gmlp_b": e_gmlp_b.reshape(A_GROUPS, GMLP_BLOCK, 1),
        "e_conv_b": e_conv_b, "e_conv_ln_g": e_conv_ln_g, "e_conv_ln_b": e_conv_ln_b,
        "o_lam_re": o_lam_re[0], "o_lam_im": o_lam_im[0], "o_log_dt": o_log_dt[0], "o_b_re": o_b_re[0], "o_b_im": o_b_im[0],
        "o_c_re": o_c_re[0], "o_c_im": o_c_im[0], "ca_norm": ca_norm, "ca_mem_norm": ca_mem_norm, "ffn_norm": ffn_norm,
        "final_norm": final_norm.reshape(1, D_MODEL),
    })
    loss_part, grad_x, grads = local_step(x[0], mem[0], loss_target[0], w, weight_fetcher(local))
    grads["final_norm"] = grads["final_norm"].reshape(D_MODEL)

    res = reduce_sharded(grads, local, mom, var)
    rep, loss = reduce_replicated(grads, loss_part, local, mom, var)
    res.update(rep)
    return (loss, grad_x[None], *[res[k][0] for k in _ORDER], *[res[k][1] for k in _ORDER],
            *[res[k][2] for k in _ORDER], *[res[k][3] for k in _ORDER])
```

```python
import jax
import jax.numpy as jnp
from jax import lax
from jax.experimental import pallas as pl
from jax.experimental.pallas import tpu as pltpu

F32 = jnp.float32
BF16 = jnp.bfloat16
S = jax.ShapeDtypeStruct

D_MODEL = 1024
A_WIDTH = 512
A_GROUPS = 4
GMLP_BLOCK = 128
CHUNK = 64
B_WIDTH = 512
IN_WIDTH = 2 * A_WIDTH + 2 * B_WIDTH
CONV_WIDTH = 31
CONV_PAD = 32
C_WIDTH = 512
C_GROUP_CH = 16
C_GROUPS = 32
C_STATE = 64
N_STATE = C_GROUPS * C_STATE
CA_HEADS = 4
CA_HEAD_DIM = 256
FFN_HIDDEN = 2816
EPS = 1e-6
ADAM_LR = 0.001
ADAM_B1 = 0.9
ADAM_B2 = 0.999
ADAM_EPS = 1e-08
ADAM_WD = 0.01
ADAM_STEP = 10
N_DEV = 8
LANES = 128
VMEM_LIMIT = 56 << 20
VMEM_BUDGET = 40 << 20
MM_TN_RESIDENT = 8 << 20
MESH = pl.DeviceIdType.MESH
ANY = pl.BlockSpec(memory_space=pl.ANY)


def _cp(*sem):
    return pltpu.CompilerParams(dimension_semantics=sem, vmem_limit_bytes=VMEM_LIMIT)


def _tile(n, pref):
    t = pref
    while n % t:
        t //= 2
    return t


def _bf(v):
    return v if v.dtype == BF16 else v.astype(BF16)


def _sigmoid(x):
    return 1.0 / (1.0 + jnp.exp(-x))


_GC = 0.7978845608028654


def _gelu(x):
    return 0.5 * x * (1.0 + jnp.tanh(_GC * (x + 0.044715 * x * x * x)))


def _gelu_grad(x):
    x2 = x * x
    t = jnp.tanh(_GC * (x + 0.044715 * x * x2))
    return 0.5 * (1.0 + t) + 0.5 * x * (1.0 - t * t) * _GC * (1.0 + 3.0 * 0.044715 * x2)


def _tspec(entry, tm):
    if isinstance(entry, tuple):
        arr, cb, width = entry
        return arr, pl.BlockSpec((tm, width), lambda i, cb=cb: (i, cb))
    return entry, pl.BlockSpec((tm, entry.shape[1]), lambda i: (i, 0))


def rows_call(name, fn, tiled, full, outs, accs, tm=256):
    pairs = [_tspec(e, tm) for e in tiled]
    arrs = [p[0] for p in pairs]
    rows = arrs[0].shape[0]
    tm = _tile(rows, tm)
    pairs = [_tspec(e, tm) for e in tiled]
    n_in = len(tiled) + len(full)
    n_out = len(outs)

    def body(*refs):
        vals = [r[...] for r in refs[:n_in]]
        o_refs = refs[n_in:n_in + n_out]
        a_refs = refs[n_in + n_out:]
        ov, av = fn(*vals)
        for r, v in zip(o_refs, ov):
            r[...] = v.astype(r.dtype)
        if a_refs:
            @pl.when(pl.program_id(0) == 0)
            def _():
                for r in a_refs:
                    r[...] = jnp.zeros(r.shape, r.dtype)
            for r, v in zip(a_refs, av):
                r[...] += v

    in_specs = [p[1] for p in pairs] + [pl.BlockSpec(a.shape, lambda i, nd=a.ndim: (0,) * nd) for a in full]
    out_specs = [pl.BlockSpec((tm, c), lambda i: (i, 0)) for c, _ in outs]
    out_specs += [pl.BlockSpec(s, lambda i, nd=len(s): (0,) * nd) for s in accs]
    out_shape = [S((rows, c), dt) for c, dt in outs] + [S(s, F32) for s in accs]
    return pl.pallas_call(body, grid=(rows // tm,), in_specs=in_specs, out_specs=out_specs, out_shape=out_shape,
                          compiler_params=_cp("arbitrary"), name=name)(*arrs, *full)


def mm_nn(name, m, n, pairs, n_acc, epi, outs, tiled=(), cols=(), rowv=()):
    a_ops, a_slot, b_arrs, b_specs, idx, trans = [], [], [], [], [], []
    fixed = 0
    for pair in pairs:
        a, b, k = pair[:3]
        bt = len(pair) > 3
        arr, cb, kdim = a if isinstance(a, tuple) else (a, 0, a.shape[1])
        key = (id(arr), cb, kdim)
        if key not in [o[0] for o in a_ops]:
            a_ops.append((key, arr, cb, kdim))
        a_slot.append([o[0] for o in a_ops].index(key))
        b_arr, off = b if isinstance(b, tuple) else (b, 0)
        b_arrs.append(b_arr)
        if bt:
            assert off % n == 0 and b_arr.shape[1] == kdim
            b_specs.append(pl.BlockSpec((n, kdim), lambda i, o=off // n: (o, 0), pipeline_mode=pl.Buffered(1)))
        else:
            assert b_arr.shape[1] == n
            b_specs.append(pl.BlockSpec((kdim, n), lambda i, o=off: (o, 0), pipeline_mode=pl.Buffered(1)))
        fixed += kdim * n * b_arr.dtype.itemsize
        idx.append(k)
        trans.append(bt)
    per_row = sum(2 * kdim * arr.dtype.itemsize for _, arr, _, kdim in a_ops)
    per_row += sum(2 * n * t.dtype.itemsize for t in tiled) + sum(2 * n * jnp.dtype(dt).itemsize for dt in outs)
    per_row += (n_acc + 3) * n * 4
    tm = next((t for t in (1024, 512, 256, 128) if m % t == 0 and fixed + t * per_row <= VMEM_BUDGET), _tile(m, 128))
    n_a, n_p = len(a_ops), len(pairs)
    n_in = n_a + n_p + len(tiled) + len(cols) + len(rowv)

    def body(*refs):
        a_vals = [_bf(r[...]) for r in refs[:n_a]]
        accs = [None] * n_acc
        for p in range(n_p):
            av, bv = a_vals[a_slot[p]], _bf(refs[n_a + p][...])
            if trans[p]:
                d = lax.dot_general(av, bv, (((1,), (1,)), ((), ())), preferred_element_type=F32)
            else:
                d = jnp.dot(av, bv, preferred_element_type=F32)
            accs[idx[p]] = d if accs[idx[p]] is None else accs[idx[p]] + d
        extra = [r[...] for r in refs[n_a + n_p:n_in]]
        ov = epi(accs, *extra)
        for r, v in zip(refs[n_in:], ov):
            r[...] = v.astype(r.dtype)

    in_specs = [pl.BlockSpec((tm, kdim), lambda i, cb=cb: (i, cb)) for _, _, cb, kdim in a_ops] + b_specs
    in_specs += [pl.BlockSpec((tm, n), lambda i: (i, 0)) for _ in tiled]
    in_specs += [pl.BlockSpec((tm, 1), lambda i: (i, 0)) for _ in cols]
    in_specs += [pl.BlockSpec((1, n), lambda i: (0, 0)) for _ in rowv]
    out_specs = [pl.BlockSpec((tm, n), lambda i: (i, 0)) for _ in outs]
    out_shape = [S((m, n), dt) for dt in outs]
    return pl.pallas_call(body, grid=(m // tm,), in_specs=in_specs, out_specs=out_specs, out_shape=out_shape,
                          compiler_params=_cp("parallel"), name=name)(*[o[1] for o in a_ops], *b_arrs, *tiled, *cols, *rowv)


def mm_tn(name, a, b, out_dtype=BF16):
    if isinstance(a, tuple):
        a_arr, a_cb, m = a
    else:
        a_arr, a_cb, m = a, None, a.shape[1]
    if isinstance(b, tuple):
        b_arr, b_cb, n = b
    else:
        b_arr, b_cb, n = b, None, b.shape[1]
    t = a_arr.shape[0]
    whole_b = t * n * b_arr.dtype.itemsize <= MM_TN_RESIDENT and b_cb is None
    tn = n if whole_b else _tile(n, 512)
    tm = _tile(m, 512 if t * 512 * a_arr.dtype.itemsize * 2 + t * tn * b_arr.dtype.itemsize * 2 <= VMEM_BUDGET else 256)
    a_off = 0 if a_cb is None else a_cb * (m // tm)
    b_off = 0 if b_cb is None else b_cb * (n // tn)

    def body(a_ref, b_ref, o_ref):
        o_ref[...] = lax.dot_general(_bf(a_ref[...]), _bf(b_ref[...]), (((0,), (0,)), ((), ())),
                                     preferred_element_type=F32).astype(o_ref.dtype)

    if whole_b:
        b_spec = pl.BlockSpec((t, n), lambda i, j: (0, 0), pipeline_mode=pl.Buffered(1))
    else:
        b_spec = pl.BlockSpec((t, tn), lambda i, j: (0, j + b_off))
    return pl.pallas_call(
        body, grid=(m // tm, n // tn),
        in_specs=[pl.BlockSpec((t, tm), lambda i, j: (0, i + a_off)), b_spec],
        out_specs=pl.BlockSpec((tm, tn), lambda i, j: (i, j)), out_shape=S((m, n), out_dtype),
        compiler_params=_cp("parallel", "parallel"), name=name)(a_arr, b_arr)


def rms_fwd(name, x, gain):
    def fn(xv, g):
        r = lax.rsqrt(jnp.mean(xv * xv, axis=-1, keepdims=True) + EPS)
        return [xv * r * g, r], []
    return rows_call(name, fn, [x], [gain], [(x.shape[1], BF16), (1, F32)], [])


def rms_bwd(name, dxn, x, r, gain, dres=None):
    d = x.shape[1]

    def fn(*vals):
        if dres is None:
            dv, xv, rv, g = vals
            base = 0.0
        else:
            dv, xv, rv, base, g = vals
        w = dv * g
        xh = xv * rv
        dx = base + rv * (w - xh * jnp.mean(w * xh, axis=-1, keepdims=True))
        return [dx, dx], [jnp.sum(dv * xh, axis=0, keepdims=True)]

    tiled = [dxn, x, r] + ([] if dres is None else [dres])
    return rows_call(name, fn, tiled, [gain], [(d, F32), (d, BF16)], [(1, d)])


def rms_bwd_gain_only(name, dxn, x, r):
    def fn(dv, xv, rv):
        return [], [jnp.sum(dv * xv * rv, axis=0, keepdims=True)]
    return rows_call(name, fn, [dxn, x, r], [], [], [(1, x.shape[1])])[0]


def final_loss(name, x, gain, target):
    d = x.shape[1]

    def fn(xv, tv, g):
        r = lax.rsqrt(jnp.mean(xv * xv, axis=-1, keepdims=True) + EPS)
        xh = xv * r
        err = xh * g - tv
        dy = err * (1.0 / d)
        w = dy * g
        dx = r * (w - xh * jnp.mean(w * xh, axis=-1, keepdims=True))
        part = jnp.sum(jnp.sum(err * err, axis=-1, keepdims=True), axis=0, keepdims=True) * (0.5 / d)
        return [dx, dx], [jnp.sum(dy * xh, axis=0, keepdims=True), part]

    return rows_call(name, fn, [x, target], [gain], [(d, F32), (d, BF16)], [(1, d), (1, 1)])


def _gmlp_mask():
    row = lax.broadcasted_iota(jnp.int32, (GMLP_BLOCK, GMLP_BLOCK), 0) // CHUNK
    col = lax.broadcasted_iota(jnp.int32, (GMLP_BLOCK, GMLP_BLOCK), 1) // CHUNK
    return col <= row


def _ln_plain(v):
    mu = jnp.mean(v, axis=-1, keepdims=True)
    vc = v - mu
    rstd = lax.rsqrt(jnp.mean(vc * vc, axis=-1, keepdims=True) + EPS)
    return vc * rstd, rstd


def gmlp_fwd(name, proj, w, b, tm=512):
    t = proj.shape[0]
    tm = _tile(t, tm)

    def body(au_ref, av_ref, w_ref, b_ref, o_ref):
        mask = _gmlp_mask()
        u = _gelu(au_ref[...])
        vn, _ = _ln_plain(_gelu(av_ref[...]))
        vnb = _bf(vn)
        for g in range(A_GROUPS):
            wg = _bf(jnp.where(mask, w_ref[g], 0.0))
            cs = slice(g * GMLP_BLOCK, (g + 1) * GMLP_BLOCK)
            for n in range(tm // GMLP_BLOCK):
                rs = slice(n * GMLP_BLOCK, (n + 1) * GMLP_BLOCK)
                sg = jnp.dot(wg, vnb[rs, cs], preferred_element_type=F32) + b_ref[g]
                o_ref[rs, cs] = (u[rs, cs] * sg).astype(o_ref.dtype)

    return pl.pallas_call(
        body, grid=(t // tm,),
        in_specs=[pl.BlockSpec((tm, A_WIDTH), lambda i: (i, 0)), pl.BlockSpec((tm, A_WIDTH), lambda i: (i, 1)),
                  pl.BlockSpec(w.shape, lambda i: (0, 0, 0)), pl.BlockSpec(b.shape, lambda i: (0, 0, 0))],
        out_specs=pl.BlockSpec((tm, A_WIDTH), lambda i: (i, 0)), out_shape=S((t, A_WIDTH), BF16),
        compiler_params=_cp("parallel"), name=name)(proj, proj, w, b)


def gmlp_bwd(name, proj, dcat, w, b, tm=512):
    t = proj.shape[0]
    tm = _tile(t, tm)

    def body(au_ref, av_ref, do_ref, w_ref, b_ref, dp_ref, dw_ref, db_ref):
        @pl.when(pl.program_id(0) == 0)
        def _():
            dw_ref[...] = jnp.zeros(dw_ref.shape, F32)
            db_ref[...] = jnp.zeros(db_ref.shape, F32)

        mask = _gmlp_mask()
        au = au_ref[...]
        av = av_ref[...]
        u = _gelu(au)
        vn, rstd = _ln_plain(_gelu(av))
        vnb = _bf(vn)
        dout = do_ref[...]
        dvn_cols = []
        for g in range(A_GROUPS):
            wm = jnp.where(mask, w_ref[g], 0.0)
            wg = _bf(wm)
            wgt = _bf(wm.T)
            cs = slice(g * GMLP_BLOCK, (g + 1) * GMLP_BLOCK)
            dwg = jnp.zeros((GMLP_BLOCK, GMLP_BLOCK), F32)
            dbg = jnp.zeros((GMLP_BLOCK, 1), F32)
            dvn_rows = []
            for n in range(tm // GMLP_BLOCK):
                rs = slice(n * GMLP_BLOCK, (n + 1) * GMLP_BLOCK)
                sg = jnp.dot(wg, vnb[rs, cs], preferred_element_type=F32) + b_ref[g]
                dp_ref[rs, cs] = (dout[rs, cs] * sg * _gelu_grad(au[rs, cs])).astype(dp_ref.dtype)
                dsg = dout[rs, cs] * u[rs, cs]
                dsgb = _bf(dsg)
                dbg = dbg + jnp.sum(dsg, axis=1, keepdims=True)
                dwg = dwg + lax.dot_general(dsgb, vnb[rs, cs], (((1,), (1,)), ((), ())), preferred_element_type=F32)
                dvn_rows.append(jnp.dot(wgt, dsgb, preferred_element_type=F32))
            dw_ref[g] += jnp.where(mask, dwg, 0.0)
            db_ref[g] += dbg
            dvn_cols.append(jnp.concatenate(dvn_rows, axis=0))
        dvn = jnp.concatenate(dvn_cols, axis=1)
        dv = rstd * (dvn - jnp.mean(dvn, axis=-1, keepdims=True) - vn * jnp.mean(dvn * vn, axis=-1, keepdims=True))
        dp_ref[:, A_WIDTH:] = (dv * _gelu_grad(av)).astype(dp_ref.dtype)

    return pl.pallas_call(
        body, grid=(t // tm,),
        in_specs=[pl.BlockSpec((tm, A_WIDTH), lambda i: (i, 0)), pl.BlockSpec((tm, A_WIDTH), lambda i: (i, 1)),
                  pl.BlockSpec((tm, A_WIDTH), lambda i: (i, 0)),
                  pl.BlockSpec(w.shape, lambda i: (0, 0, 0)), pl.BlockSpec(b.shape, lambda i: (0, 0, 0))],
        out_specs=[pl.BlockSpec((tm, 2 * A_WIDTH), lambda i: (i, 0)),
                   pl.BlockSpec(w.shape, lambda i: (0, 0, 0)), pl.BlockSpec(b.shape, lambda i: (0, 0, 0))],
        out_shape=[S((t, 2 * A_WIDTH), BF16), S(w.shape, F32), S(b.shape, F32)],
        compiler_params=_cp("arbitrary"), name=name)(proj, proj, dcat, w, b)


CONV_ROWS = 256


def conv_fwd(name, proj, w, cb):
    t = proj.shape[0]
    tc = LANES
    rows = _tile(t, CONV_ROWS)
    a_cb, g_cb = 2 * A_WIDTH // tc, (2 * A_WIDTH + B_WIDTH) // tc

    def body(a_ref, g_ref, w_ref, cb_ref, o_ref, hpad):
        hpad[0:CONV_PAD, :] = jnp.zeros((CONV_PAD, tc), F32)

        def fill(i, _):
            r0 = pl.multiple_of(i * rows, rows)
            hpad[pl.ds(CONV_PAD + r0, rows), :] = a_ref[pl.ds(r0, rows), :] * _sigmoid(g_ref[pl.ds(r0, rows), :])
            return 0
        lax.fori_loop(0, t // rows, fill, 0)

        def conv(i, _):
            r0 = pl.multiple_of(i * rows, rows)
            win = hpad[pl.ds(r0, rows + CONV_PAD), :]
            acc = jnp.zeros((rows, tc), F32) + cb_ref[...]
            for k in range(CONV_WIDTH):
                sh = CONV_WIDTH - 1 - k
                src = win if sh == 0 else pltpu.roll(win, sh, 0)
                acc = acc + src[CONV_PAD:, :] * w_ref[k:k + 1, :]
            o_ref[pl.ds(r0, rows), :] = acc
            return 0
        lax.fori_loop(0, t // rows, conv, 0)

    return pl.pallas_call(
        body, grid=(B_WIDTH // tc,),
        in_specs=[pl.BlockSpec((t, tc), lambda j: (0, a_cb + j)), pl.BlockSpec((t, tc), lambda j: (0, g_cb + j)),
                  pl.BlockSpec((CONV_WIDTH, tc), lambda j: (0, j)), pl.BlockSpec((1, tc), lambda j: (0, j))],
        out_specs=pl.BlockSpec((t, tc), lambda j: (0, j)), out_shape=S((t, B_WIDTH), F32),
        scratch_shapes=[pltpu.VMEM((t + CONV_PAD, tc), F32)],
        compiler_params=_cp("parallel"), name=name)(proj, proj, w, cb)


def conv_bwd(name, proj, dhc, w):
    t = proj.shape[0]
    tc = LANES
    rows = _tile(t, CONV_ROWS)
    a_cb, g_cb = 2 * A_WIDTH // tc, (2 * A_WIDTH + B_WIDTH) // tc
    win_rows = rows + CONV_PAD

    def body(a_ref, g_ref, d_ref, w_ref, da_ref, dg_ref, dw_ref, dcb_ref, hpad, dpad, dwacc):
        hpad[0:CONV_PAD, :] = jnp.zeros((CONV_PAD, tc), F32)
        dpad[t:t + CONV_PAD, :] = jnp.zeros((CONV_PAD, tc), F32)
        dwacc[...] = jnp.zeros(dwacc.shape, F32)

        def fill(i, _):
            r0 = pl.multiple_of(i * rows, rows)
            hpad[pl.ds(CONV_PAD + r0, rows), :] = a_ref[pl.ds(r0, rows), :] * _sigmoid(g_ref[pl.ds(r0, rows), :])
            dpad[pl.ds(r0, rows), :] = d_ref[pl.ds(r0, rows), :]
            return 0
        lax.fori_loop(0, t // rows, fill, 0)

        def step(i, dcb):
            r0 = pl.multiple_of(i * rows, rows)
            hwin = hpad[pl.ds(r0, win_rows), :]
            dwin = dpad[pl.ds(r0, win_rows), :]
            dchunk = dwin[:rows, :]
            dh = jnp.zeros((rows, tc), F32)
            for k in range(CONV_WIDTH):
                sh = CONV_WIDTH - 1 - k
                hsrc = hwin if sh == 0 else pltpu.roll(hwin, sh, 0)
                dsrc = dwin if sh == 0 else pltpu.roll(dwin, win_rows - sh, 0)
                dh = dh + dsrc[:rows, :] * w_ref[k:k + 1, :]
                prod = dchunk * hsrc[CONV_PAD:, :]
                dwacc[k] += jnp.sum(prod.reshape(rows // 8, 8, tc), axis=0)
            a = a_ref[pl.ds(r0, rows), :]
            sg = _sigmoid(g_ref[pl.ds(r0, rows), :])
            da_ref[pl.ds(r0, rows), :] = (dh * sg).astype(da_ref.dtype)
            dg_ref[pl.ds(r0, rows), :] = (dh * a * sg * (1.0 - sg)).astype(dg_ref.dtype)
            return dcb + jnp.sum(dchunk, axis=0, keepdims=True)
        dcb = lax.fori_loop(0, t // rows, step, jnp.zeros((1, tc), F32))
        dcb_ref[...] = dcb
        for k in range(CONV_WIDTH):
            dw_ref[k:k + 1, :] = jnp.sum(dwacc[k], axis=0, keepdims=True)

    return pl.pallas_call(
        body, grid=(B_WIDTH // tc,),
        in_specs=[pl.BlockSpec((t, tc), lambda j: (0, a_cb + j)), pl.BlockSpec((t, tc), lambda j: (0, g_cb + j)),
                  pl.BlockSpec((t, tc), lambda j: (0, j)), pl.BlockSpec((CONV_WIDTH, tc), lambda j: (0, j))],
        out_specs=[pl.BlockSpec((t, tc), lambda j: (0, j)), pl.BlockSpec((t, tc), lambda j: (0, j)),
                   pl.BlockSpec((CONV_WIDTH, tc), lambda j: (0, j)), pl.BlockSpec((1, tc), lambda j: (0, j))],
        out_shape=[S((t, B_WIDTH), BF16), S((t, B_WIDTH), BF16), S((CONV_WIDTH, B_WIDTH), F32), S((1, B_WIDTH), F32)],
        scratch_shapes=[pltpu.VMEM((t + CONV_PAD, tc), F32), pltpu.VMEM((t + CONV_PAD, tc), F32),
                        pltpu.VMEM((CONV_WIDTH, 8, tc), F32)],
        compiler_params=_cp("parallel"), name=name)(proj, proj, dhc, w)


def ln_silu_fwd(name, hc, g, b):
    def fn(h, gv, bv):
        y, _ = _ln_plain(h)
        z = y * gv + bv
        return [z * _sigmoid(z)], []
    return rows_call(name, fn, [hc], [g, b], [(hc.shape[1], BF16)], [])[0]


def ln_silu_bwd(name, hc, dcat, g, b):
    c = hc.shape[1]

    def fn(h, dout, gv, bv):
        y, rstd = _ln_plain(h)
        z = y * gv + bv
        s = _sigmoid(z)
        dz = dout * s * (1.0 + z * (1.0 - s))
        dyv = dz * gv
        dh = rstd * (dyv - jnp.mean(dyv, axis=-1, keepdims=True) - y * jnp.mean(dyv * y, axis=-1, keepdims=True))
        return [dh], [jnp.sum(dz * y, axis=0, keepdims=True), jnp.sum(dz, axis=0, keepdims=True)]

    return rows_call(name, fn, [hc, (dcat, 1, c)], [g, b], [(c, F32)], [(1, c), (1, c)])


_NT = (((1,), (1,)), ((), ()))
_TN = (((0,), (0,)), ((), ()))


def attn_fwd(name, q, k, v, tm=512):
    t, d = q.shape
    m = k.shape[0]
    tm = _tile(t, tm)
    scale = CA_HEAD_DIM ** -0.5

    def body(q_ref, k_ref, v_ref, o_ref):
        for h in range(CA_HEADS):
            cs = slice(h * CA_HEAD_DIM, (h + 1) * CA_HEAD_DIM)
            s = lax.dot_general(q_ref[:, cs], k_ref[:, cs], _NT, preferred_element_type=F32) * scale
            e = jnp.exp(s - jnp.max(s, axis=-1, keepdims=True))
            p = e / jnp.sum(e, axis=-1, keepdims=True)
            o_ref[:, cs] = jnp.dot(_bf(p), v_ref[:, cs], preferred_element_type=F32).astype(o_ref.dtype)

    return pl.pallas_call(
        body, grid=(t // tm,),
        in_specs=[pl.BlockSpec((tm, d), lambda i: (i, 0)), pl.BlockSpec((m, d), lambda i: (0, 0)),
                  pl.BlockSpec((m, d), lambda i: (0, 0))],
        out_specs=pl.BlockSpec((tm, d), lambda i: (i, 0)), out_shape=S((t, d), BF16),
        compiler_params=_cp("parallel"), name=name)(q, k, v)


def attn_bwd(name, q, k, v, do, tm=512):
    t, d = q.shape
    m = k.shape[0]
    tm = _tile(t, tm)
    scale = CA_HEAD_DIM ** -0.5

    def body(q_ref, k_ref, v_ref, do_ref, dq_ref, dk_ref, dv_ref):
        @pl.when(pl.program_id(0) == 0)
        def _():
            dk_ref[...] = jnp.zeros(dk_ref.shape, F32)
            dv_ref[...] = jnp.zeros(dv_ref.shape, F32)

        for h in range(CA_HEADS):
            cs = slice(h * CA_HEAD_DIM, (h + 1) * CA_HEAD_DIM)
            qh, kh, vh, doh = q_ref[:, cs], k_ref[:, cs], v_ref[:, cs], do_ref[:, cs]
            s = lax.dot_general(qh, kh, _NT, preferred_element_type=F32) * scale
            e = jnp.exp(s - jnp.max(s, axis=-1, keepdims=True))
            p = e / jnp.sum(e, axis=-1, keepdims=True)
            pb = _bf(p)
            dv_ref[:, cs] += lax.dot_general(pb, doh, _TN, preferred_element_type=F32)
            dp = lax.dot_general(doh, vh, _NT, preferred_element_type=F32)
            ds = _bf(p * (dp - jnp.sum(dp * p, axis=-1, keepdims=True)) * scale)
            dq_ref[:, cs] = jnp.dot(ds, kh, preferred_element_type=F32).astype(dq_ref.dtype)
            dk_ref[:, cs] += lax.dot_general(ds, qh, _TN, preferred_element_type=F32)

    return pl.pallas_call(
        body, grid=(t // tm,),
        in_specs=[pl.BlockSpec((tm, d), lambda i: (i, 0)), pl.BlockSpec((m, d), lambda i: (0, 0)),
                  pl.BlockSpec((m, d), lambda i: (0, 0)), pl.BlockSpec((tm, d), lambda i: (i, 0))],
        out_specs=[pl.BlockSpec((tm, d), lambda i: (i, 0)), pl.BlockSpec((m, d), lambda i: (0, 0)),
                   pl.BlockSpec((m, d), lambda i: (0, 0))],
        out_shape=[S((t, d), BF16), S((m, d), F32), S((m, d), F32)],
        compiler_params=_cp("arbitrary"), name=name)(q, k, v, do)


SUB = 8
S5_ROWS = 256


def s5_constants(lam_re, lam_im, log_dt, b_re, b_im, c_re, c_im):
    dt = jnp.exp(log_dt)[:, None]
    mag = jnp.exp(lam_re * dt)
    ar = mag * jnp.cos(lam_im * dt)
    ai = mag * jnp.sin(lam_im * dt)
    den = lam_re * lam_re + lam_im * lam_im
    qr = ((ar - 1.0) * lam_re + ai * lam_im) / den
    qi = (ai * lam_re - (ar - 1.0) * lam_im) / den
    bbr = qr[..., None] * b_re - qi[..., None] * b_im
    bbi = qr[..., None] * b_im + qi[..., None] * b_re
    eye = jnp.eye(C_GROUPS, dtype=F32)

    def in_mat(bb):
        return (bb.transpose(0, 2, 1)[:, :, None, :] * eye[:, None, :, None]).reshape(C_WIDTH, N_STATE)

    def out_mat(cc):
        return (cc.transpose(0, 2, 1)[:, :, None, :] * eye[:, None, :, None]).reshape(N_STATE, C_WIDTH)

    mb = jnp.concatenate([in_mat(bbr), in_mat(bbi)], axis=1)
    mc = jnp.concatenate([out_mat(c_re), -out_mat(c_im)], axis=0)
    a = jnp.stack([ar.reshape(N_STATE), ai.reshape(N_STATE)])
    return a, mb, mc


def _scan_powers(a, conj):
    ar, ai = a[0], (-a[1] if conj else a[1])
    pows = [(ar, ai)]
    for _ in range(SUB - 1):
        pr, pi = pows[-1]
        pows.append((pr * ar - pi * ai, pr * ai + pi * ar))
    rows = jnp.arange(SUB)[:, None]
    out = []
    for s in (1, 2, 4):
        keep = (rows + s <= SUB - 1) if conj else (rows >= s)
        out.append(jnp.stack([jnp.where(keep, pows[s - 1][0][None, :], 0.0), jnp.where(keep, pows[s - 1][1][None, :], 0.0)]))
    order = [SUB - 1 - i for i in range(SUB)] if conj else list(range(SUB))
    out.append(jnp.stack([jnp.stack([pows[i][0] for i in order]), jnp.stack([pows[i][1] for i in order])]))
    return jnp.stack(out)


def _cmul_add(xr, xi, pr, pi, zr, zi):
    return xr + pr * zr - pi * zi, xi + pr * zi + pi * zr


def s5_fwd(name, u, mb, mc, pw, dskip):
    t = u.shape[0]
    tm = _tile(t, S5_ROWS)
    ns = N_STATE

    def body(u_ref, mb_ref, mc_ref, pw_ref, d_ref, gy_ref, y_ref, xs_ref, xb_ref, carry):
        @pl.when(pl.program_id(0) == 0)
        def _():
            carry[...] = jnp.zeros(carry.shape, F32)

        uv = u_ref[...]
        xs_ref[...] = jnp.dot(_bf(uv), mb_ref[...], preferred_element_type=F32)

        def group(i, _):
            r0 = pl.multiple_of(i * SUB, SUB)
            xr = xs_ref[pl.ds(r0, SUB), 0:ns]
            xi = xs_ref[pl.ds(r0, SUB), ns:2 * ns]
            for k, s in enumerate((1, 2, 4)):
                xr, xi = _cmul_add(xr, xi, pw_ref[k, 0], pw_ref[k, 1], pltpu.roll(xr, s, 0), pltpu.roll(xi, s, 0))
            xr, xi = _cmul_add(xr, xi, pw_ref[3, 0], pw_ref[3, 1], carry[0], carry[1])
            xs_ref[pl.ds(r0, SUB), 0:ns] = xr
            xs_ref[pl.ds(r0, SUB), ns:2 * ns] = xi
            carry[0] = jnp.broadcast_to(xr[SUB - 1:SUB, :], (SUB, ns))
            carry[1] = jnp.broadcast_to(xi[SUB - 1:SUB, :], (SUB, ns))
            return 0
        lax.fori_loop(0, tm // SUB, group, 0)

        xb = _bf(xs_ref[...])
        xb_ref[...] = xb
        y = jnp.dot(xb, mc_ref[...], preferred_element_type=F32) + d_ref[...] * uv
        y_ref[...] = y
        gy_ref[...] = _gelu(y).astype(gy_ref.dtype)

    c = u.shape[1]
    return pl.pallas_call(
        body, grid=(t // tm,),
        in_specs=[pl.BlockSpec((tm, c), lambda i: (i, 0)), pl.BlockSpec(mb.shape, lambda i: (0, 0)),
                  pl.BlockSpec(mc.shape, lambda i: (0, 0)), pl.BlockSpec(pw.shape, lambda i: (0, 0, 0, 0)),
                  pl.BlockSpec((1, c), lambda i: (0, 0))],
        out_specs=[pl.BlockSpec((tm, c), lambda i: (i, 0)), pl.BlockSpec((tm, c), lambda i: (i, 0)),
                   pl.BlockSpec((tm, 2 * ns), lambda i: (i, 0)), pl.BlockSpec((tm, 2 * ns), lambda i: (i, 0))],
        out_shape=[S((t, c), BF16), S((t, c), F32), S((t, 2 * ns), F32), S((t, 2 * ns), BF16)],
        scratch_shapes=[pltpu.VMEM((2, SUB, ns), F32)],
        compiler_params=_cp("arbitrary"), name=name)(u, mb, mc, pw, dskip)


def s5_bwd(name, dgy, y, u, xs, mct, mbt, qw, dskip):
    t, c = u.shape
    tm = _tile(t, S5_ROWS)
    nt = t // tm
    ns = N_STATE
    ng = tm // SUB

    def body(dgy_ref, y_ref, u_ref, xs_ref, prev_ref, mct_ref, mbt_ref, qw_ref, d_ref,
             du_ref, dy_ref, lb_ref, da_ref, dd_ref, lam, carry):
        step = pl.program_id(0)

        @pl.when(step == 0)
        def _():
            carry[...] = jnp.zeros(carry.shape, F32)
            da_ref[...] = jnp.zeros(da_ref.shape, F32)
            dd_ref[...] = jnp.zeros(dd_ref.shape, F32)

        uv = u_ref[...]
        dy = dgy_ref[...] * _gelu_grad(y_ref[...])
        dyb = _bf(dy)
        dy_ref[...] = dyb
        dd_ref[...] += jnp.sum(dy * uv, axis=0, keepdims=True)
        lam[...] = jnp.dot(dyb, mct_ref[...], preferred_element_type=F32)
        first_tile = (step == nt - 1).astype(F32)
        row0 = lax.broadcasted_iota(jnp.int32, (SUB, ns), 0) == 0

        def group(j, _):
            i = ng - 1 - j
            r0 = pl.multiple_of(i * SUB, SUB)
            lr = lam[pl.ds(r0, SUB), 0:ns]
            li = lam[pl.ds(r0, SUB), ns:2 * ns]
            for k, s in enumerate((1, 2, 4)):
                lr, li = _cmul_add(lr, li, qw_ref[k, 0], qw_ref[k, 1],
                                   pltpu.roll(lr, SUB - s, 0), pltpu.roll(li, SUB - s, 0))
            lr, li = _cmul_add(lr, li, qw_ref[3, 0], qw_ref[3, 1], carry[0], carry[1])
            lam[pl.ds(r0, SUB), 0:ns] = lr
            lam[pl.ds(r0, SUB), ns:2 * ns] = li
            carry[0] = jnp.broadcast_to(lr[0:1, :], (SUB, ns))
            carry[1] = jnp.broadcast_to(li[0:1, :], (SUB, ns))
            rp = pl.multiple_of(jnp.maximum(i - 1, 0) * SUB, SUB)
            in_tile = (i > 0).astype(F32)
            out_tile = (1.0 - in_tile) * (1.0 - first_tile)
            pr = xs_ref[pl.ds(rp, SUB), 0:ns] * in_tile + prev_ref[:, 0:ns] * out_tile
            pi = xs_ref[pl.ds(rp, SUB), ns:2 * ns] * in_tile + prev_ref[:, ns:2 * ns] * out_tile
            xpr = jnp.where(row0, pltpu.roll(pr, 1, 0), pltpu.roll(xs_ref[pl.ds(r0, SUB), 0:ns], 1, 0))
            xpi = jnp.where(row0, pltpu.roll(pi, 1, 0), pltpu.roll(xs_ref[pl.ds(r0, SUB), ns:2 * ns], 1, 0))
            da_ref[0] += lr * xpr + li * xpi
            da_ref[1] += li * xpr - lr * xpi
            return 0
        lax.fori_loop(0, ng, group, 0)

        lb = _bf(lam[...])
        lb_ref[...] = lb
        du_ref[...] = (jnp.dot(lb, mbt_ref[...], preferred_element_type=F32) + d_ref[...] * dy).astype(du_ref.dtype)

    rev = lambda i: (nt - 1 - i, 0)
    prev = lambda i: (jnp.maximum((nt - 1 - i) * (tm // SUB) - 1, 0), 0)
    return pl.pallas_call(
        body, grid=(nt,),
        in_specs=[pl.BlockSpec((tm, c), rev), pl.BlockSpec((tm, c), rev), pl.BlockSpec((tm, c), rev),
                  pl.BlockSpec((tm, 2 * ns), rev), pl.BlockSpec((SUB, 2 * ns), prev),
                  pl.BlockSpec(mct.shape, lambda i: (0, 0)), pl.BlockSpec(mbt.shape, lambda i: (0, 0)),
                  pl.BlockSpec(qw.shape, lambda i: (0, 0, 0, 0)), pl.BlockSpec((1, c), lambda i: (0, 0))],
        out_specs=[pl.BlockSpec((tm, c), rev), pl.BlockSpec((tm, c), rev), pl.BlockSpec((tm, 2 * ns), rev),
                   pl.BlockSpec((2, SUB, ns), lambda i: (0, 0, 0)), pl.BlockSpec((1, c), lambda i: (0, 0))],
        out_shape=[S((t, c), BF16), S((t, c), BF16), S((t, 2 * ns), BF16), S((2, SUB, ns), F32), S((1, c), F32)],
        scratch_shapes=[pltpu.VMEM((tm, 2 * ns), F32), pltpu.VMEM((2, SUB, ns), F32)],
        compiler_params=_cp("arbitrary"), name=name)(dgy, y, u, xs, xs, mct, mbt, qw, dskip)


def _first(accs, *_):
    return [accs[0]]


def _add_res(accs, res):
    return [accs[0] + res]


def even_fwd(x, w):
    t = x.shape[0]
    hn, r = rms_fwd("e_norm_f", x, w["e_norm"])
    (proj,) = mm_nn("e_in_f", t, IN_WIDTH, [(hn, w["e_w_in_t"], 0, "t")], 1, _first, [F32])
    out_a = gmlp_fwd("e_gmlp_f", proj, w["e_gmlp_w"], w["e_gmlp_b"])
    hc = conv_fwd("e_conv_f", proj, w["e_conv_w"], w["e_conv_b"])
    out_b = ln_silu_fwd("e_ln_f", hc, w["e_conv_ln_g"], w["e_conv_ln_b"])
    (x1,) = mm_nn("e_out_f", t, D_MODEL, [(out_a, (w["e_w_out"], 0), 0), (out_b, (w["e_w_out"], 1), 0)],
                  1, _add_res, [F32], tiled=[x])
    return x1, (x, hn, r, proj, out_a, hc, out_b)


def even_bwd(dx, dxb, saved, w):
    x, hn, r, proj, out_a, hc, out_b = saved
    t = x.shape[0]
    (dcat,) = mm_nn("e_out_b", t, D_MODEL, [(dxb, w["e_w_out"], 0, "t")], 1, _first, [F32])
    g_w_out = jnp.concatenate([mm_tn("e_out_wa", out_a, dxb), mm_tn("e_out_wb", out_b, dxb)], axis=0)
    dab, g_gw, g_gb = gmlp_bwd("e_gmlp_b", proj, dcat, w["e_gmlp_w"], w["e_gmlp_b"])
    dhc, g_lg, g_lb = ln_silu_bwd("e_ln_b", hc, dcat, w["e_conv_ln_g"], w["e_conv_ln_b"])
    dba, dbg, g_cw, g_cb = conv_bwd("e_conv_b", proj, dhc, w["e_conv_w"])
    w_in_t = w["e_w_in_t"]
    (dhn,) = mm_nn("e_in_b", t, D_MODEL, [(dab, (w_in_t, 0), 0), (dba, (w_in_t, 2), 0), (dbg, (w_in_t, 3), 0)],
                   1, _first, [F32])
    g_w_in_t = jnp.concatenate([mm_tn("e_in_w0", dab, hn), mm_tn("e_in_w1", dba, hn), mm_tn("e_in_w2", dbg, hn)], axis=0)
    dx0, dx0b, g_norm = rms_bwd("e_norm_b", dhn, x, r, w["e_norm"], dres=dx)
    grads = dict(e_norm=g_norm, e_w_in_t=g_w_in_t, e_gmlp_w=g_gw[None], e_gmlp_b=g_gb.reshape(1, A_GROUPS, GMLP_BLOCK),
                 e_conv_w=g_cw[None], e_conv_b=g_cb, e_conv_ln_g=g_lg, e_conv_ln_b=g_lb, e_w_out=g_w_out)
    return dx0, dx0b, grads


def odd_fwd(x, w, consts):
    t = x.shape[0]
    _, mb, mc, pw, _ = consts
    hn, r = rms_fwd("o_norm_f", x, w["o_norm"])
    (u,) = mm_nn("o_in_f", t, C_WIDTH, [(hn, w["o_w_in"], 0)], 1, _first, [F32])
    gy, y, xs, xsb = s5_fwd("o_s5_f", u, _bf(mb), _bf(mc), pw, w["o_d"])
    w_out_t = w["o_w_out_t"]

    def epi(accs, res):
        return [res + accs[0] * _sigmoid(accs[1]), accs[0], accs[1]]

    x1, o1, o2 = mm_nn("o_out_f", t, D_MODEL, [(gy, (w_out_t, 0), 0, "t"), (gy, (w_out_t, D_MODEL), 1, "t")], 2, epi,
                       [F32, BF16, BF16], tiled=[x])
    return x1, (x, hn, r, u, gy, y, xs, xsb, o1, o2)


def odd_bwd(dx, dxb, saved, w, consts, consts_vjp):
    x, hn, r, u, gy, y, xs, xsb, o1, o2 = saved
    t = x.shape[0]
    _, mb, mc, _, qw = consts

    def gate_bwd(dv, a, b):
        a = a.astype(F32)
        sg = _sigmoid(b.astype(F32))
        return [jnp.concatenate([dv * sg, dv * a * sg * (1.0 - sg)], axis=1)], []

    (do12,) = rows_call("o_gate_b", gate_bwd, [dx, o1, o2], [], [(2 * D_MODEL, BF16)], [])
    (dgy,) = mm_nn("o_out_b", t, C_WIDTH, [(do12, w["o_w_out_t"], 0)], 1, _first, [F32])
    g_w_out_t = mm_tn("o_out_w", do12, gy)
    du, dyb, lamb, da8, g_d = s5_bwd("o_s5_b", dgy, y, u, xs, _bf(mc.T), _bf(mb.T), qw, w["o_d"])
    d_mb = mm_tn("o_s5_wb", u, lamb, out_dtype=F32)
    d_mc = mm_tn("o_s5_wc", xsb, dyb, out_dtype=F32)
    g_lr, g_li, g_dt, g_br, g_bi, g_cr, g_ci = consts_vjp((jnp.sum(da8, axis=1), d_mb, d_mc))
    g_w_in = mm_tn("o_in_w", hn, du)
    (dhn,) = mm_nn("o_in_b", t, D_MODEL, [(du, w["o_w_in"], 0, "t")], 1, _first, [F32])
    dx0, dx0b, g_norm = rms_bwd("o_norm_b", dhn, x, r, w["o_norm"], dres=dx)
    grads = dict(o_norm=g_norm, o_w_in=g_w_in, o_lam_re=g_lr[None], o_lam_im=g_li[None], o_log_dt=g_dt[None],
                 o_b_re=g_br[None], o_b_im=g_bi[None], o_c_re=g_cr[None], o_c_im=g_ci[None], o_d=g_d, o_w_out_t=g_w_out_t)
    return dx0, dx0b, grads


def ca_fwd(i, x, mem, w):
    t, m = x.shape[0], mem.shape[0]
    xn, r = rms_fwd(f"ca{i}_norm_f", x, w["ca_norm"][i:i + 1])
    mn, rm = rms_fwd(f"ca{i}_mnorm_f", mem, w["ca_mem_norm"][i:i + 1])
    (q,) = mm_nn(f"ca{i}_q_f", t, D_MODEL, [(xn, w["ca_wq"][i], 0)], 1, _first, [BF16])
    k, v = mm_nn(f"ca{i}_kv_f", m, D_MODEL, [(mn, w["ca_wk"][i], 0), (mn, w["ca_wv"][i], 1)], 2,
                 lambda accs: [accs[0], accs[1]], [BF16, BF16])
    o = attn_fwd(f"ca{i}_attn_f", q, k, v)
    (x1,) = mm_nn(f"ca{i}_o_f", t, D_MODEL, [(o, w["ca_wo"][i], 0)], 1, _add_res, [F32], tiled=[x])
    return x1, (x, xn, r, mn, rm, q, k, v, o)


def ca_bwd(i, dx, dxb, saved, mem, w):
    x, xn, r, mn, rm, q, k, v, o = saved
    t, m = x.shape[0], mem.shape[0]
    (do,) = mm_nn(f"ca{i}_o_b", t, D_MODEL, [(dxb, w["ca_wo"][i], 0, "t")], 1, _first, [BF16])
    g_wo = mm_tn(f"ca{i}_o_w", o, dxb)
    dq, dk, dv = attn_bwd(f"ca{i}_attn_b", q, k, v, do)
    g_wq = mm_tn(f"ca{i}_q_w", xn, dq)
    g_wk = mm_tn(f"ca{i}_k_w", mn, dk)
    g_wv = mm_tn(f"ca{i}_v_w", mn, dv)
    (dxn,) = mm_nn(f"ca{i}_q_b", t, D_MODEL, [(dq, w["ca_wq"][i], 0, "t")], 1, _first, [F32])
    (dmn,) = mm_nn(f"ca{i}_kv_b", m, D_MODEL, [(dk, w["ca_wk"][i], 0, "t"), (dv, w["ca_wv"][i], 0, "t")], 1, _first, [F32])
    g_mnorm = rms_bwd_gain_only(f"ca{i}_mnorm_b", dmn, mem, rm)
    dx0, dx0b, g_norm = rms_bwd(f"ca{i}_norm_b", dxn, x, r, w["ca_norm"][i:i + 1], dres=dx)
    return dx0, dx0b, dict(ca_norm=g_norm, ca_mem_norm=g_mnorm, ca_wq=g_wq, ca_wk=g_wk, ca_wv=g_wv, ca_wo=g_wo)


def ffn_fwd(i, x, w):
    t = x.shape[0]
    xn, r = rms_fwd(f"ffn{i}_norm_f", x, w["ffn_norm"][i:i + 1])

    def epi(accs):
        g, u = accs
        return [g, u, g * _sigmoid(g) * u]

    g, u, h = mm_nn(f"ffn{i}_up_f", t, FFN_HIDDEN, [(xn, w["ffn_w_gate_t"][i], 0, "t"), (xn, w["ffn_w_up_t"][i], 1, "t")],
                    2, epi, [BF16, BF16, BF16])
    (x1,) = mm_nn(f"ffn{i}_down_f", t, D_MODEL, [(h, w["ffn_w_down"][i], 0)], 1, _add_res, [F32], tiled=[x])
    return x1, (x, xn, r, g, u, h)


def ffn_bwd(i, dx, dxb, saved, w):
    x, xn, r, g, u, h = saved
    t = x.shape[0]

    def epi(accs, gv, uv):
        dh = accs[0]
        gv = gv.astype(F32)
        uv = uv.astype(F32)
        s = _sigmoid(gv)
        return [dh * uv * s * (1.0 + gv * (1.0 - s)), dh * gv * s]

    dg, du = mm_nn(f"ffn{i}_down_b", t, FFN_HIDDEN, [(dxb, w["ffn_w_down"][i], 0, "t")], 1, epi, [BF16, BF16], tiled=[g, u])
    g_wd = mm_tn(f"ffn{i}_down_w", h, dxb)
    g_wg_t = mm_tn(f"ffn{i}_gate_w", dg, xn)
    g_wu_t = mm_tn(f"ffn{i}_up_w", du, xn)
    (dxn,) = mm_nn(f"ffn{i}_up_b", t, D_MODEL, [(dg, w["ffn_w_gate_t"][i], 0), (du, w["ffn_w_up_t"][i], 0)], 1, _first, [F32])
    dx0, dx0b, g_norm = rms_bwd(f"ffn{i}_norm_b", dxn, x, r, w["ffn_norm"][i:i + 1], dres=dx)
    return dx0, dx0b, dict(ffn_norm=g_norm, ffn_w_gate_t=g_wg_t, ffn_w_up_t=g_wu_t, ffn_w_down=g_wd)


_S5_PARAMS = ("o_lam_re", "o_lam_im", "o_log_dt", "o_b_re", "o_b_im", "o_c_re", "o_c_im")


def local_step(x, mem, target, w, fetch=None, on_grads=None):
    def consts_fn(*p):
        a, mb, mc = s5_constants(*p)
        return a, mb, mc

    (a, mb, mc), consts_vjp = jax.vjp(consts_fn, *[w[k] for k in _S5_PARAMS])
    consts = (a, mb, mc, _scan_powers(a, False), _scan_powers(a, True))

    def need(stage, after):
        if fetch is not None:
            for k, v in fetch(stage, after).items():
                if isinstance(k, tuple):
                    w.setdefault(k[0], {})[k[1]] = v
                else:
                    w[k] = v

    need(0, x)
    x1, s_e = even_fwd(x, w)
    need(1, x1)
    x2, s_c0 = ca_fwd(0, x1, mem, w)
    need(2, x2)
    x3, s_f0 = ffn_fwd(0, x2, w)
    x4, s_o = odd_fwd(x3, w, consts)
    need(3, x4)
    x5, s_c1 = ca_fwd(1, x4, mem, w)
    x6, s_f1 = ffn_fwd(1, x5, w)
    dx, dxb, g_final, loss = final_loss("final_loss", x6, w["final_norm"], target)

    def emit(stage, plain, layered=None, layer=0):
        if on_grads is not None:
            out = dict(plain)
            out.update({(k, layer): v for k, v in (layered or {}).items()})
            on_grads(stage, out)

    dx, dxb, g_f1 = ffn_bwd(1, dx, dxb, s_f1, w)
    emit(0, {}, g_f1, 1)
    dx, dxb, g_c1 = ca_bwd(1, dx, dxb, s_c1, mem, w)
    dx, dxb, g_o = odd_bwd(dx, dxb, s_o, w, consts, consts_vjp)
    emit(1, g_o, g_c1, 1)
    dx, dxb, g_f0 = ffn_bwd(0, dx, dxb, s_f0, w)
    emit(2, {}, g_f0, 0)
    dx, dxb, g_c0 = ca_bwd(0, dx, dxb, s_c0, mem, w)
    emit(3, {}, g_c0, 0)
    dx, dxb, g_e = even_bwd(dx, dxb, s_e, w)
    emit(4, {**g_e, "o_norm": g_o["o_norm"], "o_d": g_o["o_d"]})

    grads = dict(g_e)
    grads.update(g_o)
    for g0, g1 in ((g_c0, g_c1), (g_f0, g_f1)):
        for k in g0:
            grads[k] = jnp.concatenate([g0[k], g1[k]], axis=0) if k.endswith("norm") else (g0[k], g1[k])
    grads["final_norm"] = g_final
    return loss, dx, grads


def _group(axes):
    pos = {a: lax.axis_index(a) for a in ("x", "y", "c")}
    me = 0
    for a in axes:
        me = me * 2 + pos[a]
    peers = []
    for mask in range(1, 2 ** len(axes)):
        peer = dict(pos)
        for bit, a in enumerate(axes):
            if (mask >> (len(axes) - 1 - bit)) & 1:
                peer[a] = 1 - pos[a]
        idx = 0
        for a in axes:
            idx = idx * 2 + peer[a]
        peers.append((idx, (peer["x"], peer["y"], peer["c"])))
    return me, peers


def _sibling():
    x, y, c = lax.axis_index("x"), lax.axis_index("y"), lax.axis_index("c")
    return c, (x, y, 1 - c)


def gather_ici(name, blks):
    k_ops = len(blks)
    out_shape = [S((4, 2) + tuple(b.shape), b.dtype) for b in blks]

    def body(*refs):
        in_refs, out_refs = refs[:k_ops], refs[k_ops:2 * k_ops]
        send_sems, recv_sems, local_sems = refs[2 * k_ops:]
        me, peers = _group(("x", "y"))
        core = lax.axis_index("c")
        local, sent, landed = [], [], []
        for i in range(k_ops):
            cp = pltpu.make_async_copy(in_refs[i], out_refs[i].at[me, core], local_sems.at[i])
            cp.start()
            local.append(cp)
        for k, (idx, dev) in enumerate(peers):
            for i in range(k_ops):
                s = i * 3 + k
                cp = pltpu.make_async_remote_copy(src_ref=in_refs[i], dst_ref=out_refs[i].at[me, core], send_sem=send_sems.at[s],
                                                  recv_sem=recv_sems.at[s], device_id=dev, device_id_type=MESH)
                cp.start()
                sent.append(cp)
                landed.append(pltpu.make_async_remote_copy(src_ref=in_refs[i], dst_ref=out_refs[i].at[idx, core],
                                                           send_sem=send_sems.at[s], recv_sem=recv_sems.at[s],
                                                           device_id=dev, device_id_type=MESH))
        for cp in landed:
            cp.wait_recv()
        for cp in sent:
            cp.wait_send()
        for cp in local:
            cp.wait()

    return pl.pallas_call(
        body, in_specs=[ANY] * k_ops, out_specs=[ANY] * k_ops, out_shape=out_shape,
        scratch_shapes=[pltpu.SemaphoreType.DMA((k_ops * 3,)), pltpu.SemaphoreType.DMA((k_ops * 3,)),
                        pltpu.SemaphoreType.DMA((k_ops,))],
        name=name)(*blks)


_HBM = pl.BlockSpec(memory_space=pltpu.HBM)
_SEM = pl.BlockSpec(memory_space=pltpu.SEMAPHORE)
_EFFECT = pltpu.SideEffectType.DATAFLOW_SIDE_EFFECTING


def gather_ici_start(name, groups):
    flat = [b for g in groups for b in g]
    sizes = [len(g) for g in groups]
    k_ops, n_g = len(flat), len(groups)
    lands = [lax.empty((4, 2) + tuple(b.shape), b.dtype) for b in flat]

    def body(*refs):
        src, land = refs[:k_ops], refs[k_ops:2 * k_ops]
        sems = refs[2 * k_ops:2 * k_ops + 3 * n_g]
        token = refs[-1]
        me, peers = _group(("x", "y"))
        core = lax.axis_index("c")
        i = 0
        for g in range(n_g):
            send, recv, loc = sems[3 * g:3 * g + 3]
            for j in range(sizes[g]):
                pltpu.make_async_copy(src[i], land[i].at[me, core], loc.at[j]).start()
                for k, (_, dev) in enumerate(peers):
                    pltpu.make_async_remote_copy(src_ref=src[i], dst_ref=land[i].at[me, core], send_sem=send.at[3 * j + k],
                                                 recv_sem=recv.at[3 * j + k], device_id=dev, device_id_type=MESH).start()
                i += 1
        token[...] = jnp.zeros(token.shape, token.dtype)

    sem_shapes = []
    for s in sizes:
        sem_shapes += [pltpu.SemaphoreType.DMA((3 * s,)), pltpu.SemaphoreType.DMA((3 * s,)), pltpu.SemaphoreType.DMA((s,))]
    thru = [pltpu.HBM(a.shape, a.dtype) for a in flat + lands]
    outs = pl.pallas_call(
        body, name=name, out_shape=tuple(sem_shapes) + tuple(thru) + (S((8, LANES), F32),),
        in_specs=[_HBM] * (2 * k_ops), out_specs=[_SEM] * (3 * n_g) + [_HBM] * (2 * k_ops) + [pl.BlockSpec(memory_space=pltpu.VMEM)],
        input_output_aliases={i: 3 * n_g + i for i in range(2 * k_ops)},
        compiler_params=pltpu.CompilerParams(has_side_effects=_EFFECT),
    )(*[pltpu.with_memory_space_constraint(a, pltpu.HBM) for a in flat + lands])
    sems = [tuple(outs[3 * g:3 * g + 3]) for g in range(n_g)]
    srcs_thru, lands_thru, off = [], [], 3 * n_g
    for s in sizes:
        srcs_thru.append(list(outs[off:off + s]))
        off += s
    for s in sizes:
        lands_thru.append(list(outs[off:off + s]))
        off += s
    return sems, srcs_thru, lands_thru, outs[-1]


def gather_ici_wait(name, srcs, lands, sems, after):
    n = len(srcs)

    def body(*refs):
        src, land = refs[:n], refs[n:2 * n]
        send, recv, loc = refs[2 * n:2 * n + 3]
        me, peers = _group(("x", "y"))
        core = lax.axis_index("c")
        for j in range(n):
            for k, (idx, dev) in enumerate(peers):
                cp = pltpu.make_async_remote_copy(src_ref=src[j], dst_ref=land[j].at[idx, core], send_sem=send.at[3 * j + k],
                                                  recv_sem=recv.at[3 * j + k], device_id=dev, device_id_type=MESH)
                cp.wait_send()
                cp.wait_recv()
            pltpu.make_async_copy(src[j], land[j].at[me, core], loc.at[j]).wait()

    outs = pl.pallas_call(
        body, name=name, out_shape=tuple(pltpu.HBM(a.shape, a.dtype) for a in list(srcs) + list(lands)),
        in_specs=[_HBM] * (2 * n) + [_SEM] * 3 + [ANY], out_specs=[_HBM] * (2 * n),
        input_output_aliases={i: i for i in range(2 * n)},
        compiler_params=pltpu.CompilerParams(has_side_effects=_EFFECT),
    )(*srcs, *lands, *sems, after)
    return list(outs[n:])


def gather_d2d(name, bufs):
    k_ops = len(bufs)

    def body(*refs):
        in_refs, out_refs = refs[:k_ops], refs[k_ops:2 * k_ops]
        send_sems, recv_sems = refs[2 * k_ops:]
        core, sib = _sibling()
        sent, landed = [], []
        for i in range(k_ops):
            cp = pltpu.make_async_remote_copy(src_ref=in_refs[i].at[:, core], dst_ref=out_refs[i].at[:, core],
                                              send_sem=send_sems.at[i], recv_sem=recv_sems.at[i], device_id=sib, device_id_type=MESH)
            cp.start()
            sent.append(cp)
            landed.append(pltpu.make_async_remote_copy(src_ref=in_refs[i].at[:, core], dst_ref=out_refs[i].at[:, 1 - core],
                                                       send_sem=send_sems.at[i], recv_sem=recv_sems.at[i],
                                                       device_id=sib, device_id_type=MESH))
        for cp in landed:
            cp.wait_recv()
        for cp in sent:
            cp.wait_send()

    return pl.pallas_call(
        body, in_specs=[ANY] * k_ops, out_specs=[ANY] * k_ops, out_shape=[S(b.shape, b.dtype) for b in bufs],
        input_output_aliases={i: i for i in range(k_ops)},
        scratch_shapes=[pltpu.SemaphoreType.DMA((k_ops,)), pltpu.SemaphoreType.DMA((k_ops,))],
        name=name)(*bufs)


def all_gather(name, blks):
    bufs = gather_d2d(name + "_d2d", gather_ici(name + "_ici", blks))
    return [p.reshape((N_DEV * b.shape[0],) + tuple(b.shape[1:])) for p, b in zip(bufs, blks)]


def scatter_d2d(name, pack):
    q, _, rows, c = pack.shape

    def body(in_ref, out_ref, send_sem, recv_sem):
        core, sib = _sibling()
        cp = pltpu.make_async_remote_copy(src_ref=in_ref.at[:, 1 - core], dst_ref=out_ref, send_sem=send_sem, recv_sem=recv_sem,
                                          device_id=sib, device_id_type=MESH)
        cp.start()
        cp.wait_recv()
        cp.wait_send()

    return pl.pallas_call(
        body, in_specs=[ANY], out_specs=ANY, out_shape=S((q, rows, c), pack.dtype),
        scratch_shapes=[pltpu.SemaphoreType.DMA, pltpu.SemaphoreType.DMA], name=name)(pack)


def scatter_ici_start(name, arr):
    land = lax.empty(arr.shape, arr.dtype)

    def body(in_ref, land_ref, send, recv, in_thru, land_thru):
        me, peers = _group(("x", "y"))
        for k, (idx, dev) in enumerate(peers):
            pltpu.make_async_remote_copy(src_ref=in_ref.at[idx], dst_ref=land_ref.at[me], send_sem=send.at[k], recv_sem=recv.at[k],
                                         device_id=dev, device_id_type=MESH).start()

    outs = pl.pallas_call(
        body, name=name,
        out_shape=(pltpu.SemaphoreType.DMA((3,)), pltpu.SemaphoreType.DMA((3,)), pltpu.HBM(arr.shape, arr.dtype),
                   pltpu.HBM(arr.shape, arr.dtype)),
        in_specs=[_HBM, _HBM], out_specs=[_SEM, _SEM, _HBM, _HBM], input_output_aliases={0: 2, 1: 3},
        compiler_params=pltpu.CompilerParams(has_side_effects=_EFFECT),
    )(pltpu.with_memory_space_constraint(arr, pltpu.HBM), pltpu.with_memory_space_constraint(land, pltpu.HBM))
    return (outs[0], outs[1]), outs[2], outs[3]


def scatter_ici_wait(name, arr, land, sems, after):
    def body(in_ref, land_ref, send, recv, after_ref, in_thru, land_thru):
        _, peers = _group(("x", "y"))
        for k, (idx, dev) in enumerate(peers):
            cp = pltpu.make_async_remote_copy(src_ref=in_ref.at[idx], dst_ref=land_ref.at[idx], send_sem=send.at[k],
                                              recv_sem=recv.at[k], device_id=dev, device_id_type=MESH)
            cp.wait_send()
            cp.wait_recv()

    outs = pl.pallas_call(
        body, name=name, out_shape=(pltpu.HBM(arr.shape, arr.dtype), pltpu.HBM(arr.shape, arr.dtype)),
        in_specs=[_HBM, _HBM, _SEM, _SEM, ANY], out_specs=[_HBM, _HBM], input_output_aliases={0: 0, 1: 1},
        compiler_params=pltpu.CompilerParams(has_side_effects=_EFFECT),
    )(arr, land, sems[0], sems[1], after)
    return outs[0], outs[1]


def _row_tile(rows, cap=512):
    return next(t for t in range(cap - cap % 16, 0, -16) if rows % t == 0)


def sum_pair(name, pack, recv, core):
    q, rows, c = recv.shape
    tr = _row_tile(rows)

    def body(core_ref, a_ref, b_ref, o_ref):
        o_ref[...] = (a_ref[...].astype(F32) + b_ref[...].astype(F32)).astype(o_ref.dtype)

    spec = pltpu.PrefetchScalarGridSpec(
        num_scalar_prefetch=1, grid=(q, rows // tr),
        in_specs=[pl.BlockSpec((None, None, tr, c), lambda j, i, core: (j, core[0], i, 0)),
                  pl.BlockSpec((None, tr, c), lambda j, i, core: (j, i, 0))],
        out_specs=pl.BlockSpec((None, tr, c), lambda j, i, core: (j, i, 0)))
    return pl.pallas_call(body, grid_spec=spec, out_shape=S(recv.shape, recv.dtype),
                          compiler_params=_cp("parallel", "parallel"), name=name)(core, pack, recv)


def sum_quad(name, own, recv, chip):
    _, rows, c = recv.shape
    tr = _row_tile(rows)

    def body(chip_ref, a_ref, r1_ref, r2_ref, r3_ref, o_ref):
        o_ref[...] = ((a_ref[...].astype(F32) + r1_ref[...].astype(F32)) + r2_ref[...].astype(F32)) + r3_ref[...].astype(F32)

    def slot(mask):
        return pl.BlockSpec((None, tr, c), lambda i, chip, mask=mask: (jnp.bitwise_xor(chip[0], mask), i, 0))

    spec = pltpu.PrefetchScalarGridSpec(
        num_scalar_prefetch=1, grid=(rows // tr,), in_specs=[slot(0), slot(1), slot(2), slot(3)],
        out_specs=pl.BlockSpec((tr, c), lambda i, chip: (i, 0)))
    return pl.pallas_call(body, grid_spec=spec, out_shape=S((rows, c), F32),
                          compiler_params=_cp("parallel"), name=name)(chip, own, recv, recv, recv)


def adamw_native(name, g, w, m, v, tr=512):
    shape = w.shape
    cols = shape[-1]
    rows = w.size // cols
    tr = _tile(rows, tr) if rows % 8 == 0 else rows
    c1 = 1.0 - ADAM_B1 ** ADAM_STEP
    c2 = 1.0 - ADAM_B2 ** ADAM_STEP

    def body(g_ref, w_ref, m_ref, v_ref, d_ref, m2_ref, v2_ref):
        gv = g_ref[...]
        m2 = ADAM_B1 * m_ref[...] + (1.0 - ADAM_B1) * gv
        v2 = ADAM_B2 * v_ref[...] + (1.0 - ADAM_B2) * (gv * gv)
        m2_ref[...] = m2
        v2_ref[...] = v2
        d_ref[...] = -ADAM_LR * ((m2 / c1) / (jnp.sqrt(v2 / c2) + ADAM_EPS) + ADAM_WD * w_ref[...])

    row = pl.BlockSpec((tr, cols), lambda i: (i, 0))
    outs = pl.pallas_call(body, grid=(rows // tr,), in_specs=[row] * 4, out_specs=[row] * 3,
                          out_shape=[S((rows, cols), F32)] * 3, compiler_params=_cp("parallel"),
                          name=name)(*[a.reshape(rows, cols) for a in (g, w, m, v)])
    return tuple(o.reshape(shape) for o in outs)


def adamw_call(name, slots, w, m, v, tr=1024):
    n, r, c = slots.shape
    tr = _tile(r, tr)
    c1 = 1.0 - ADAM_B1 ** ADAM_STEP
    c2 = 1.0 - ADAM_B2 ** ADAM_STEP

    def body(s_ref, w_ref, m_ref, v_ref, g_ref, d_ref, m2_ref, v2_ref):
        g = s_ref[0].astype(F32)
        for j in range(1, n):
            g = g + s_ref[j].astype(F32)
        m2 = ADAM_B1 * m_ref[...] + (1.0 - ADAM_B1) * g
        v2 = ADAM_B2 * v_ref[...] + (1.0 - ADAM_B2) * (g * g)
        g_ref[...] = g
        m2_ref[...] = m2
        v2_ref[...] = v2
        d_ref[...] = -ADAM_LR * ((m2 / c1) / (jnp.sqrt(v2 / c2) + ADAM_EPS) + ADAM_WD * w_ref[...])

    row = pl.BlockSpec((tr, c), lambda i: (i, 0))
    return pl.pallas_call(body, grid=(r // tr,), in_specs=[pl.BlockSpec((n, tr, c), lambda i: (0, i, 0)), row, row, row],
                          out_specs=[row, row, row, row], out_shape=[S((r, c), F32)] * 4,
                          compiler_params=_cp("parallel"), name=name)(slots, w, m, v)


_REPLICATED = ("e_norm", "e_gmlp_w", "e_gmlp_b", "e_conv_b", "e_conv_ln_g", "e_conv_ln_b", "o_lam_re", "o_lam_im", "o_log_dt",
               "o_b_re", "o_b_im", "o_c_re", "o_c_im", "ca_norm", "ca_mem_norm", "ffn_norm", "final_norm")
_ORDER = ("e_norm", "e_w_in", "e_gmlp_w", "e_gmlp_b", "e_conv_w", "e_conv_b", "e_conv_ln_g", "e_conv_ln_b", "e_w_out",
          "o_norm", "o_w_in", "o_lam_re", "o_lam_im", "o_log_dt", "o_b_re", "o_b_im", "o_c_re", "o_c_im", "o_d", "o_w_out",
          "ca_norm", "ca_mem_norm", "ca_wq", "ca_wk", "ca_wv", "ca_wo", "ffn_norm", "ffn_w_gate", "ffn_w_up", "ffn_w_down",
          "final_norm")


def _rows128(a, multiple=8):
    flat = a.reshape(-1)
    rows = -(-flat.shape[0] // (LANES * multiple)) * multiple
    return jnp.pad(flat, (0, rows * LANES - flat.shape[0])).reshape(rows, LANES)


def _shard(full, axis):
    s = full.shape
    return jnp.moveaxis(full.reshape(s[:axis] + (N_DEV, s[axis] // N_DEV) + s[axis + 1:]), axis, 0)


_UNITS = (("e_w_in", 0, True), ("e_w_out", 0, False), ("o_w_in", 0, False), ("o_w_out", 0, True),
          *[(n, i, False) for n in ("ca_wq", "ca_wk", "ca_wv", "ca_wo") for i in (0, 1)],
          *[(n, i, tr) for n, tr in (("ffn_w_gate", True), ("ffn_w_up", True), ("ffn_w_down", False)) for i in (0, 1)])
_LAYERED = ("ca_wq", "ca_wk", "ca_wv", "ca_wo", "ffn_w_gate", "ffn_w_up", "ffn_w_down")
_SMALL_SHARDED = (("e_conv_w", 2), ("o_norm", 1), ("o_d", 1))
RS_ROW = 1024


def _unit_key(name, tr):
    return name + "_t" if tr else name


def _stage_of(name, layer):
    if name.startswith("e_"):
        return 0
    if name.startswith("o_"):
        return 2
    if name.startswith("ca_"):
        return 1 if layer == 0 else 3
    return 2 if layer == 0 else 3


def weight_fetcher(local):
    groups, meta = [[] for _ in range(4)], [[] for _ in range(4)]
    for name, layer, tr in _UNITS:
        blk = local[name][layer]
        st = _stage_of(name, layer)
        groups[st].append(_bf(blk.T if tr else blk))
        meta[st].append((name, layer, tr))
    small = jnp.concatenate([local[name].reshape(-1) for name, _ in _SMALL_SHARDED])
    groups[0].append(_rows128(small))
    sems, srcs, lands, token = gather_ici_start("ag_w_start", groups)

    def fetch(stage, after):
        if stage == 0:
            after = token
        landed = gather_ici_wait(f"ag_w_wait{stage}", srcs[stage], lands[stage], sems[stage], after)
        bufs = gather_d2d(f"ag_w_d2d{stage}", landed)
        got = {}
        for (name, layer, tr), blk, buf in zip(meta[stage], groups[stage], bufs):
            arr = buf.reshape((N_DEV * blk.shape[0],) + tuple(blk.shape[1:]))
            if name in _LAYERED:
                got[(_unit_key(name, tr), layer)] = arr
            else:
                got[_unit_key(name, tr)] = arr
        if stage == 0:
            flat = bufs[-1].reshape(N_DEV, -1)
            off = 0
            for name, axis in _SMALL_SHARDED:
                blk = local[name]
                seg = flat[:, off:off + blk.size].reshape((N_DEV,) + blk.shape)
                off += blk.size
                seg = jnp.moveaxis(seg, 0, axis)
                got[name] = seg.reshape(seg.shape[:axis] + (-1,) + seg.shape[axis + 2:])
            got["e_conv_w"] = got["e_conv_w"][0]
        return got

    return fetch


def _grad_stage_of(name, layer):
    if name.startswith("e_"):
        return 4
    if name.startswith("o_"):
        return 1
    if name.startswith("ca_"):
        return 3 if layer == 0 else 1
    return 2 if layer == 0 else 0


GRAD_STAGES = 5
SMALL_ROWS = 16


def gradient_reducer(local, mom, var):
    core = lax.axis_index("c").astype(jnp.int32).reshape(1)
    chip = (2 * lax.axis_index("x") + lax.axis_index("y")).astype(jnp.int32).reshape(1)
    pending = []

    def start(stage, grads):
        units = [u for u in _UNITS if _grad_stage_of(u[0], u[1]) == stage]
        parts, spans = [], []
        for name, layer, tr in units:
            key = _unit_key(name, tr)
            g = grads[(key, layer)] if name in _LAYERED else grads[key]
            part = g.reshape(4, 2, -1, RS_ROW)
            spans.append((part.shape[2], g.shape[0] // N_DEV, g.shape[1]))
            parts.append(part)
        if stage == GRAD_STAGES - 1:
            small = jnp.concatenate([_shard(grads[name], axis).reshape(N_DEV, -1) for name, axis in _SMALL_SHARDED], axis=1)
            small = jnp.pad(small, ((0, 0), (0, SMALL_ROWS * RS_ROW - small.shape[1])))
            parts.append(small.astype(BF16).reshape(4, 2, SMALL_ROWS, RS_ROW))
        pack = jnp.concatenate(parts, axis=2)
        from_sibling = scatter_d2d(f"rs_d2d{stage}", pack)
        chip_sum = sum_pair(f"rs_pair{stage}", pack, from_sibling, core)
        sems, own, land = scatter_ici_start(f"rs_start{stage}", chip_sum)
        pending.append((stage, units, spans, sems, own, land))

    def finish(after):
        res, per_layer, small_flat = {}, {}, None
        for stage, units, spans, sems, own, land in pending:
            own, land = scatter_ici_wait(f"rs_wait{stage}", own, land, sems, after)
            total = sum_quad(f"rs_quad{stage}", own, land, chip)
            off = 0
            for (name, layer, tr), (rows, r, c) in zip(units, spans):
                g = total[off:off + rows].reshape(r, c)
                off += rows
                per_layer.setdefault(name, {})[layer] = g.T if tr else g
            if stage == GRAD_STAGES - 1:
                small_flat = total[off:off + SMALL_ROWS].reshape(-1)
        for name, by_layer in per_layer.items():
            g = jnp.stack([by_layer[i] for i in sorted(by_layer)]) if name in _LAYERED else by_layer[0][None]
            res[name] = (g,) + adamw_native("adamw_" + name, g, local[name], mom[name], var[name])
        off = 0
        for name, _ in _SMALL_SHARDED:
            blk = local[name]
            g = small_flat[off:off + blk.size].reshape(blk.shape)
            off += blk.size
            res[name] = (g,) + adamw_native("adamw_" + name, g, blk, mom[name], var[name])
        return res

    return start, finish


def reduce_replicated(grads, loss, w, mom, var):
    def pack(src, last):
        return jnp.concatenate([_rows128(src[name]) for name in _REPLICATED] + [_rows128(last)], axis=0)

    (slots,) = all_gather("ag_g", [pack(grads, loss)])
    zero = jnp.zeros((1, 1), F32)
    rows = slots.shape[0] // N_DEV
    outs = adamw_call("adamw_replicated", slots.reshape(N_DEV, rows, LANES), pack(w, zero), pack(mom, zero), pack(var, zero),
                      tr=rows)
    res, off = {}, 0
    for name in _REPLICATED:
        n = w[name].size
        nr = _rows128(w[name]).shape[0]
        res[name] = tuple(o[off:off + nr].reshape(-1)[:n].reshape(w[name].shape) for o in outs)
        off += nr
    return res, outs[0][off, 0]


def kernel(x, mem, e_norm, e_w_in, e_gmlp_w, e_gmlp_b, e_conv_w, e_conv_b, e_conv_ln_g, e_conv_ln_b, e_w_out, o_norm, o_w_in, o_lam_re, o_lam_im, o_log_dt, o_b_re, o_b_im, o_c_re, o_c_im, o_d, o_w_out, ca_norm, ca_mem_norm, ca_wq, ca_wk, ca_wv, ca_wo, ffn_norm, ffn_w_gate, ffn_w_up, ffn_w_down, final_norm, loss_target, m_e_norm, m_e_w_in, m_e_gmlp_w, m_e_gmlp_b, m_e_conv_w, m_e_conv_b, m_e_conv_ln_g, m_e_conv_ln_b, m_e_w_out, m_o_norm, m_o_w_in, m_o_lam_re, m_o_lam_im, m_o_log_dt, m_o_b_re, m_o_b_im, m_o_c_re, m_o_c_im, m_o_d, m_o_w_out, m_ca_norm, m_ca_mem_norm, m_ca_wq, m_ca_wk, m_ca_wv, m_ca_wo, m_ffn_norm, m_ffn_w_gate, m_ffn_w_up, m_ffn_w_down, m_final_norm, v_e_norm, v_e_w_in, v_e_gmlp_w, v_e_gmlp_b, v_e_conv_w, v_e_conv_b, v_e_conv_ln_g, v_e_conv_ln_b, v_e_w_out, v_o_norm, v_o_w_in, v_o_lam_re, v_o_lam_im, v_o_log_dt, v_o_b_re, v_o_b_im, v_o_c_re, v_o_c_im, v_o_d, v_o_w_out, v_ca_norm, v_ca_mem_norm, v_ca_wq, v_ca_wk, v_ca_wv, v_ca_wo, v_ffn_norm, v_ffn_w_gate, v_ffn_w_up, v_ffn_w_down, v_final_norm):
    given = dict(locals())
    local = {k: given[k] for k in _ORDER}
    mom = {k: given["m_" + k] for k in _ORDER}
    var = {k: given["v_" + k] for k in _ORDER}

    w = {}
    w.update({
        "e_norm": e_norm, "e_gmlp_w": e_gmlp_w[0], "e_gmlp_b": e_gmlp_b.reshape(A_GROUPS, GMLP_BLOCK, 1),
        "e_conv_b": e_conv_b, "e_conv_ln_g": e_conv_ln_g, "e_conv_ln_b": e_conv_ln_b,
        "o_lam_re": o_lam_re[0], "o_lam_im": o_lam_im[0], "o_log_dt": o_log_dt[0], "o_b_re": o_b_re[0], "o_b_im": o_b_im[0],
        "o_c_re": o_c_re[0], "o_c_im": o_c_im[0], "ca_norm": ca_norm, "ca_mem_norm": ca_mem_norm, "ffn_norm": ffn_norm,
        "final_norm": final_norm.reshape(1, D_MODEL),
    })
    start_reduce, finish_reduce = gradient_reducer(local, mom, var)
    loss_part, grad_x, grads = local_step(x[0], mem[0], loss_target[0], w, weight_fetcher(local), start_reduce)
    grads["final_norm"] = grads["final_norm"].reshape(D_MODEL)

    res = finish_reduce(grad_x)
    rep, loss = reduce_replicated(grads, loss_part, local, mom, var)
    res.update(rep)
    return (loss, grad_x[None], *[res[k][0] for k in _ORDER], *[res[k][1] for k in _ORDER],
            *[res[k][2] for k in _ORDER], *[res[k][3] for k in _ORDER])
```

```python
import jax
import jax.numpy as jnp
from jax import lax
from jax.experimental import pallas as pl
from jax.experimental.pallas import tpu as pltpu

F32 = jnp.float32
BF16 = jnp.bfloat16
S = jax.ShapeDtypeStruct

D_MODEL = 1024
A_WIDTH = 512
A_GROUPS = 4
GMLP_BLOCK = 128
CHUNK = 64
B_WIDTH = 512
IN_WIDTH = 2 * A_WIDTH + 2 * B_WIDTH
CONV_WIDTH = 31
CONV_PAD = 32
C_WIDTH = 512
C_GROUP_CH = 16
C_GROUPS = 32
C_STATE = 64
N_STATE = C_GROUPS * C_STATE
CA_HEADS = 4
CA_HEAD_DIM = 256
FFN_HIDDEN = 2816
EPS = 1e-6
ADAM_LR = 0.001
ADAM_B1 = 0.9
ADAM_B2 = 0.999
ADAM_EPS = 1e-08
ADAM_WD = 0.01
ADAM_STEP = 10
N_DEV = 8
LANES = 128
VMEM_LIMIT = 56 << 20
VMEM_BUDGET = 40 << 20
MM_TN_RESIDENT = 8 << 20
MESH = pl.DeviceIdType.MESH
ANY = pl.BlockSpec(memory_space=pl.ANY)


def _cp(*sem):
    return pltpu.CompilerParams(dimension_semantics=sem, vmem_limit_bytes=VMEM_LIMIT)


def _tile(n, pref):
    t = pref
    while n % t:
        t //= 2
    return t


def _bf(v):
    return v if v.dtype == BF16 else v.astype(BF16)


def _sigmoid(x):
    return 1.0 / (1.0 + jnp.exp(-x))


_GC = 0.7978845608028654


def _gelu(x):
    return 0.5 * x * (1.0 + jnp.tanh(_GC * (x + 0.044715 * x * x * x)))


def _gelu_grad(x):
    x2 = x * x
    t = jnp.tanh(_GC * (x + 0.044715 * x * x2))
    return 0.5 * (1.0 + t) + 0.5 * x * (1.0 - t * t) * _GC * (1.0 + 3.0 * 0.044715 * x2)


def _tspec(entry, tm):
    if isinstance(entry, tuple):
        arr, cb, width = entry
        return arr, pl.BlockSpec((tm, width), lambda i, cb=cb: (i, cb))
    return entry, pl.BlockSpec((tm, entry.shape[1]), lambda i: (i, 0))


def rows_call(name, fn, tiled, full, outs, accs, tm=256):
    pairs = [_tspec(e, tm) for e in tiled]
    arrs = [p[0] for p in pairs]
    rows = arrs[0].shape[0]
    tm = _tile(rows, tm)
    pairs = [_tspec(e, tm) for e in tiled]
    n_in = len(tiled) + len(full)
    n_out = len(outs)

    def body(*refs):
        vals = [r[...] for r in refs[:n_in]]
        o_refs = refs[n_in:n_in + n_out]
        a_refs = refs[n_in + n_out:]
        ov, av = fn(*vals)
        for r, v in zip(o_refs, ov):
            r[...] = v.astype(r.dtype)
        if a_refs:
            @pl.when(pl.program_id(0) == 0)
            def _():
                for r in a_refs:
                    r[...] = jnp.zeros(r.shape, r.dtype)
            for r, v in zip(a_refs, av):
                r[...] += v

    in_specs = [p[1] for p in pairs] + [pl.BlockSpec(a.shape, lambda i, nd=a.ndim: (0,) * nd) for a in full]
    out_specs = [pl.BlockSpec((tm, c), lambda i: (i, 0)) for c, _ in outs]
    out_specs += [pl.BlockSpec(s, lambda i, nd=len(s): (0,) * nd) for s in accs]
    out_shape = [S((rows, c), dt) for c, dt in outs] + [S(s, F32) for s in accs]
    return pl.pallas_call(body, grid=(rows // tm,), in_specs=in_specs, out_specs=out_specs, out_shape=out_shape,
                          compiler_params=_cp("arbitrary"), name=name)(*arrs, *full)


def mm_nn(name, m, n, pairs, n_acc, epi, outs, tiled=(), cols=(), rowv=()):
    a_ops, a_slot, b_arrs, b_specs, idx, trans = [], [], [], [], [], []
    fixed = 0
    for pair in pairs:
        a, b, k = pair[:3]
        bt = len(pair) > 3
        arr, cb, kdim = a if isinstance(a, tuple) else (a, 0, a.shape[1])
        key = (id(arr), cb, kdim)
        if key not in [o[0] for o in a_ops]:
            a_ops.append((key, arr, cb, kdim))
        a_slot.append([o[0] for o in a_ops].index(key))
        b_arr, off = b if isinstance(b, tuple) else (b, 0)
        b_arrs.append(b_arr)
        if bt:
            assert off % n == 0 and b_arr.shape[1] == kdim
            b_specs.append(pl.BlockSpec((n, kdim), lambda i, o=off // n: (o, 0), pipeline_mode=pl.Buffered(1)))
        else:
            assert b_arr.shape[1] == n
            b_specs.append(pl.BlockSpec((kdim, n), lambda i, o=off: (o, 0), pipeline_mode=pl.Buffered(1)))
        fixed += kdim * n * b_arr.dtype.itemsize
        idx.append(k)
        trans.append(bt)
    per_row = sum(2 * kdim * arr.dtype.itemsize for _, arr, _, kdim in a_ops)
    per_row += sum(2 * n * t.dtype.itemsize for t in tiled) + sum(2 * n * jnp.dtype(dt).itemsize for dt in outs)
    per_row += (n_acc + 3) * n * 4
    tm = next((t for t in (1024, 512, 256, 128) if m % t == 0 and fixed + t * per_row <= VMEM_BUDGET), _tile(m, 128))
    n_a, n_p = len(a_ops), len(pairs)
    n_in = n_a + n_p + len(tiled) + len(cols) + len(rowv)

    def body(*refs):
        a_vals = [_bf(r[...]) for r in refs[:n_a]]
        accs = [None] * n_acc
        for p in range(n_p):
            av, bv = a_vals[a_slot[p]], _bf(refs[n_a + p][...])
            if trans[p]:
                d = lax.dot_general(av, bv, (((1,), (1,)), ((), ())), preferred_element_type=F32)
            else:
                d = jnp.dot(av, bv, preferred_element_type=F32)
            accs[idx[p]] = d if accs[idx[p]] is None else accs[idx[p]] + d
        extra = [r[...] for r in refs[n_a + n_p:n_in]]
        ov = epi(accs, *extra)
        for r, v in zip(refs[n_in:], ov):
            r[...] = v.astype(r.dtype)

    in_specs = [pl.BlockSpec((tm, kdim), lambda i, cb=cb: (i, cb)) for _, _, cb, kdim in a_ops] + b_specs
    in_specs += [pl.BlockSpec((tm, n), lambda i: (i, 0)) for _ in tiled]
    in_specs += [pl.BlockSpec((tm, 1), lambda i: (i, 0)) for _ in cols]
    in_specs += [pl.BlockSpec((1, n), lambda i: (0, 0)) for _ in rowv]
    out_specs = [pl.BlockSpec((tm, n), lambda i: (i, 0)) for _ in outs]
    out_shape = [S((m, n), dt) for dt in outs]
    return pl.pallas_call(body, grid=(m // tm,), in_specs=in_specs, out_specs=out_specs, out_shape=out_shape,
                          compiler_params=_cp("parallel"), name=name)(*[o[1] for o in a_ops], *b_arrs, *tiled, *cols, *rowv)


def mm_tn(name, a, b, out_dtype=BF16):
    if isinstance(a, tuple):
        a_arr, a_cb, m = a
    else:
        a_arr, a_cb, m = a, None, a.shape[1]
    if isinstance(b, tuple):
        b_arr, b_cb, n = b
    else:
        b_arr, b_cb, n = b, None, b.shape[1]
    t = a_arr.shape[0]
    whole_b = t * n * b_arr.dtype.itemsize <= MM_TN_RESIDENT and b_cb is None
    tn = n if whole_b else _tile(n, 512)
    tm = _tile(m, 512 if t * 512 * a_arr.dtype.itemsize * 2 + t * tn * b_arr.dtype.itemsize * 2 <= VMEM_BUDGET else 256)
    a_off = 0 if a_cb is None else a_cb * (m // tm)
    b_off = 0 if b_cb is None else b_cb * (n // tn)

    def body(a_ref, b_ref, o_ref):
        o_ref[...] = lax.dot_general(_bf(a_ref[...]), _bf(b_ref[...]), (((0,), (0,)), ((), ())),
                                     preferred_element_type=F32).astype(o_ref.dtype)

    if whole_b:
        b_spec = pl.BlockSpec((t, n), lambda i, j: (0, 0), pipeline_mode=pl.Buffered(1))
    else:
        b_spec = pl.BlockSpec((t, tn), lambda i, j: (0, j + b_off))
    return pl.pallas_call(
        body, grid=(m // tm, n // tn),
        in_specs=[pl.BlockSpec((t, tm), lambda i, j: (0, i + a_off)), b_spec],
        out_specs=pl.BlockSpec((tm, tn), lambda i, j: (i, j)), out_shape=S((m, n), out_dtype),
        compiler_params=_cp("parallel", "parallel"), name=name)(a_arr, b_arr)


def rms_fwd(name, x, gain):
    def fn(xv, g):
        r = lax.rsqrt(jnp.mean(xv * xv, axis=-1, keepdims=True) + EPS)
        return [xv * r * g, r], []
    return rows_call(name, fn, [x], [gain], [(x.shape[1], BF16), (1, F32)], [])


def rms_bwd(name, dxn, x, r, gain, dres=None):
    d = x.shape[1]

    def fn(*vals):
        if dres is None:
            dv, xv, rv, g = vals
            base = 0.0
        else:
            dv, xv, rv, base, g = vals
        w = dv * g
        xh = xv * rv
        dx = base + rv * (w - xh * jnp.mean(w * xh, axis=-1, keepdims=True))
        return [dx, dx], [jnp.sum(dv * xh, axis=0, keepdims=True)]

    tiled = [dxn, x, r] + ([] if dres is None else [dres])
    return rows_call(name, fn, tiled, [gain], [(d, F32), (d, BF16)], [(1, d)])


def rms_bwd_gain_only(name, dxn, x, r):
    def fn(dv, xv, rv):
        return [], [jnp.sum(dv * xv * rv, axis=0, keepdims=True)]
    return rows_call(name, fn, [dxn, x, r], [], [], [(1, x.shape[1])])[0]


def final_loss(name, x, gain, target):
    d = x.shape[1]

    def fn(xv, tv, g):
        r = lax.rsqrt(jnp.mean(xv * xv, axis=-1, keepdims=True) + EPS)
        xh = xv * r
        err = xh * g - tv
        dy = err * (1.0 / d)
        w = dy * g
        dx = r * (w - xh * jnp.mean(w * xh, axis=-1, keepdims=True))
        part = jnp.sum(jnp.sum(err * err, axis=-1, keepdims=True), axis=0, keepdims=True) * (0.5 / d)
        return [dx, dx], [jnp.sum(dy * xh, axis=0, keepdims=True), part]

    return rows_call(name, fn, [x, target], [gain], [(d, F32), (d, BF16)], [(1, d), (1, 1)])


def _gmlp_mask():
    row = lax.broadcasted_iota(jnp.int32, (GMLP_BLOCK, GMLP_BLOCK), 0) // CHUNK
    col = lax.broadcasted_iota(jnp.int32, (GMLP_BLOCK, GMLP_BLOCK), 1) // CHUNK
    return col <= row


def _ln_plain(v):
    mu = jnp.mean(v, axis=-1, keepdims=True)
    vc = v - mu
    rstd = lax.rsqrt(jnp.mean(vc * vc, axis=-1, keepdims=True) + EPS)
    return vc * rstd, rstd


def gmlp_fwd(name, proj, w, b, tm=512):
    t = proj.shape[0]
    tm = _tile(t, tm)

    def body(au_ref, av_ref, w_ref, b_ref, o_ref):
        mask = _gmlp_mask()
        u = _gelu(au_ref[...])
        vn, _ = _ln_plain(_gelu(av_ref[...]))
        vnb = _bf(vn)
        for g in range(A_GROUPS):
            wg = _bf(jnp.where(mask, w_ref[g], 0.0))
            cs = slice(g * GMLP_BLOCK, (g + 1) * GMLP_BLOCK)
            for n in range(tm // GMLP_BLOCK):
                rs = slice(n * GMLP_BLOCK, (n + 1) * GMLP_BLOCK)
                sg = jnp.dot(wg, vnb[rs, cs], preferred_element_type=F32) + b_ref[g]
                o_ref[rs, cs] = (u[rs, cs] * sg).astype(o_ref.dtype)

    return pl.pallas_call(
        body, grid=(t // tm,),
        in_specs=[pl.BlockSpec((tm, A_WIDTH), lambda i: (i, 0)), pl.BlockSpec((tm, A_WIDTH), lambda i: (i, 1)),
                  pl.BlockSpec(w.shape, lambda i: (0, 0, 0)), pl.BlockSpec(b.shape, lambda i: (0, 0, 0))],
        out_specs=pl.BlockSpec((tm, A_WIDTH), lambda i: (i, 0)), out_shape=S((t, A_WIDTH), BF16),
        compiler_params=_cp("parallel"), name=name)(proj, proj, w, b)


def gmlp_bwd(name, proj, dcat, w, b, tm=512):
    t = proj.shape[0]
    tm = _tile(t, tm)

    def body(au_ref, av_ref, do_ref, w_ref, b_ref, dp_ref, dw_ref, db_ref):
        @pl.when(pl.program_id(0) == 0)
        def _():
            dw_ref[...] = jnp.zeros(dw_ref.shape, F32)
            db_ref[...] = jnp.zeros(db_ref.shape, F32)

        mask = _gmlp_mask()
        au = au_ref[...]
        av = av_ref[...]
        u = _gelu(au)
        vn, rstd = _ln_plain(_gelu(av))
        vnb = _bf(vn)
        dout = do_ref[...]
        dvn_cols = []
        for g in range(A_GROUPS):
            wm = jnp.where(mask, w_ref[g], 0.0)
            wg = _bf(wm)
            wgt = _bf(wm.T)
            cs = slice(g * GMLP_BLOCK, (g + 1) * GMLP_BLOCK)
            dwg = jnp.zeros((GMLP_BLOCK, GMLP_BLOCK), F32)
            dbg = jnp.zeros((GMLP_BLOCK, 1), F32)
            dvn_rows = []
            for n in range(tm // GMLP_BLOCK):
                rs = slice(n * GMLP_BLOCK, (n + 1) * GMLP_BLOCK)
                sg = jnp.dot(wg, vnb[rs, cs], preferred_element_type=F32) + b_ref[g]
                dp_ref[rs, cs] = (dout[rs, cs] * sg * _gelu_grad(au[rs, cs])).astype(dp_ref.dtype)
                dsg = dout[rs, cs] * u[rs, cs]
                dsgb = _bf(dsg)
                dbg = dbg + jnp.sum(dsg, axis=1, keepdims=True)
                dwg = dwg + lax.dot_general(dsgb, vnb[rs, cs], (((1,), (1,)), ((), ())), preferred_element_type=F32)
                dvn_rows.append(jnp.dot(wgt, dsgb, preferred_element_type=F32))
            dw_ref[g] += jnp.where(mask, dwg, 0.0)
            db_ref[g] += dbg
            dvn_cols.append(jnp.concatenate(dvn_rows, axis=0))
        dvn = jnp.concatenate(dvn_cols, axis=1)
        dv = rstd * (dvn - jnp.mean(dvn, axis=-1, keepdims=True) - vn * jnp.mean(dvn * vn, axis=-1, keepdims=True))
        dp_ref[:, A_WIDTH:] = (dv * _gelu_grad(av)).astype(dp_ref.dtype)

    return pl.pallas_call(
        body, grid=(t // tm,),
        in_specs=[pl.BlockSpec((tm, A_WIDTH), lambda i: (i, 0)), pl.BlockSpec((tm, A_WIDTH), lambda i: (i, 1)),
                  pl.BlockSpec((tm, A_WIDTH), lambda i: (i, 0)),
                  pl.BlockSpec(w.shape, lambda i: (0, 0, 0)), pl.BlockSpec(b.shape, lambda i: (0, 0, 0))],
        out_specs=[pl.BlockSpec((tm, 2 * A_WIDTH), lambda i: (i, 0)),
                   pl.BlockSpec(w.shape, lambda i: (0, 0, 0)), pl.BlockSpec(b.shape, lambda i: (0, 0, 0))],
        out_shape=[S((t, 2 * A_WIDTH), BF16), S(w.shape, F32), S(b.shape, F32)],
        compiler_params=_cp("arbitrary"), name=name)(proj, proj, dcat, w, b)


CONV_ROWS = 256


def conv_fwd(name, proj, w, cb):
    t = proj.shape[0]
    tc = LANES
    rows = _tile(t, CONV_ROWS)
    a_cb, g_cb = 2 * A_WIDTH // tc, (2 * A_WIDTH + B_WIDTH) // tc

    def body(a_ref, g_ref, w_ref, cb_ref, o_ref, hpad):
        hpad[0:CONV_PAD, :] = jnp.zeros((CONV_PAD, tc), F32)

        def fill(i, _):
            r0 = pl.multiple_of(i * rows, rows)
            hpad[pl.ds(CONV_PAD + r0, rows), :] = a_ref[pl.ds(r0, rows), :] * _sigmoid(g_ref[pl.ds(r0, rows), :])
            return 0
        lax.fori_loop(0, t // rows, fill, 0)

        def conv(i, _):
            r0 = pl.multiple_of(i * rows, rows)
            win = hpad[pl.ds(r0, rows + CONV_PAD), :]
            acc = jnp.zeros((rows, tc), F32) + cb_ref[...]
            for k in range(CONV_WIDTH):
                sh = CONV_WIDTH - 1 - k
                src = win if sh == 0 else pltpu.roll(win, sh, 0)
                acc = acc + src[CONV_PAD:, :] * w_ref[k:k + 1, :]
            o_ref[pl.ds(r0, rows), :] = acc
            return 0
        lax.fori_loop(0, t // rows, conv, 0)

    return pl.pallas_call(
        body, grid=(B_WIDTH // tc,),
        in_specs=[pl.BlockSpec((t, tc), lambda j: (0, a_cb + j)), pl.BlockSpec((t, tc), lambda j: (0, g_cb + j)),
                  pl.BlockSpec((CONV_WIDTH, tc), lambda j: (0, j)), pl.BlockSpec((1, tc), lambda j: (0, j))],
        out_specs=pl.BlockSpec((t, tc), lambda j: (0, j)), out_shape=S((t, B_WIDTH), F32),
        scratch_shapes=[pltpu.VMEM((t + CONV_PAD, tc), F32)],
        compiler_params=_cp("parallel"), name=name)(proj, proj, w, cb)


def conv_bwd(name, proj, dhc, w):
    t = proj.shape[0]
    tc = LANES
    rows = _tile(t, CONV_ROWS)
    a_cb, g_cb = 2 * A_WIDTH // tc, (2 * A_WIDTH + B_WIDTH) // tc
    win_rows = rows + CONV_PAD

    def body(a_ref, g_ref, d_ref, w_ref, da_ref, dg_ref, dw_ref, dcb_ref, hpad, dpad, dwacc):
        hpad[0:CONV_PAD, :] = jnp.zeros((CONV_PAD, tc), F32)
        dpad[t:t + CONV_PAD, :] = jnp.zeros((CONV_PAD, tc), F32)
        dwacc[...] = jnp.zeros(dwacc.shape, F32)

        def fill(i, _):
            r0 = pl.multiple_of(i * rows, rows)
            hpad[pl.ds(CONV_PAD + r0, rows), :] = a_ref[pl.ds(r0, rows), :] * _sigmoid(g_ref[pl.ds(r0, rows), :])
            dpad[pl.ds(r0, rows), :] = d_ref[pl.ds(r0, rows), :]
            return 0
        lax.fori_loop(0, t // rows, fill, 0)

        def step(i, dcb):
            r0 = pl.multiple_of(i * rows, rows)
            hwin = hpad[pl.ds(r0, win_rows), :]
            dwin = dpad[pl.ds(r0, win_rows), :]
            dchunk = dwin[:rows, :]
            dh = jnp.zeros((rows, tc), F32)
            for k in range(CONV_WIDTH):
                sh = CONV_WIDTH - 1 - k
                hsrc = hwin if sh == 0 else pltpu.roll(hwin, sh, 0)
                dsrc = dwin if sh == 0 else pltpu.roll(dwin, win_rows - sh, 0)
                dh = dh + dsrc[:rows, :] * w_ref[k:k + 1, :]
                prod = dchunk * hsrc[CONV_PAD:, :]
                dwacc[k] += jnp.sum(prod.reshape(rows // 8, 8, tc), axis=0)
            a = a_ref[pl.ds(r0, rows), :]
            sg = _sigmoid(g_ref[pl.ds(r0, rows), :])
            da_ref[pl.ds(r0, rows), :] = (dh * sg).astype(da_ref.dtype)
            dg_ref[pl.ds(r0, rows), :] = (dh * a * sg * (1.0 - sg)).astype(dg_ref.dtype)
            return dcb + jnp.sum(dchunk, axis=0, keepdims=True)
        dcb = lax.fori_loop(0, t // rows, step, jnp.zeros((1, tc), F32))
        dcb_ref[...] = dcb
        for k in range(CONV_WIDTH):
            dw_ref[k:k + 1, :] = jnp.sum(dwacc[k], axis=0, keepdims=True)

    return pl.pallas_call(
        body, grid=(B_WIDTH // tc,),
        in_specs=[pl.BlockSpec((t, tc), lambda j: (0, a_cb + j)), pl.BlockSpec((t, tc), lambda j: (0, g_cb + j)),
                  pl.BlockSpec((t, tc), lambda j: (0, j)), pl.BlockSpec((CONV_WIDTH, tc), lambda j: (0, j))],
        out_specs=[pl.BlockSpec((t, tc), lambda j: (0, j)), pl.BlockSpec((t, tc), lambda j: (0, j)),
                   pl.BlockSpec((CONV_WIDTH, tc), lambda j: (0, j)), pl.BlockSpec((1, tc), lambda j: (0, j))],
        out_shape=[S((t, B_WIDTH), BF16), S((t, B_WIDTH), BF16), S((CONV_WIDTH, B_WIDTH), F32), S((1, B_WIDTH), F32)],
        scratch_shapes=[pltpu.VMEM((t + CONV_PAD, tc), F32), pltpu.VMEM((t + CONV_PAD, tc), F32),
                        pltpu.VMEM((CONV_WIDTH, 8, tc), F32)],
        compiler_params=_cp("parallel"), name=name)(proj, proj, dhc, w)


def ln_silu_fwd(name, hc, g, b):
    def fn(h, gv, bv):
        y, _ = _ln_plain(h)
        z = y * gv + bv
        return [z * _sigmoid(z)], []
    return rows_call(name, fn, [hc], [g, b], [(hc.shape[1], BF16)], [])[0]


def ln_silu_bwd(name, hc, dcat, g, b):
    c = hc.shape[1]

    def fn(h, dout, gv, bv):
        y, rstd = _ln_plain(h)
        z = y * gv + bv
        s = _sigmoid(z)
        dz = dout * s * (1.0 + z * (1.0 - s))
        dyv = dz * gv
        dh = rstd * (dyv - jnp.mean(dyv, axis=-1, keepdims=True) - y * jnp.mean(dyv * y, axis=-1, keepdims=True))
        return [dh], [jnp.sum(dz * y, axis=0, keepdims=True), jnp.sum(dz, axis=0, keepdims=True)]

    return rows_call(name, fn, [hc, (dcat, 1, c)], [g, b], [(c, F32)], [(1, c), (1, c)])


_NT = (((1,), (1,)), ((), ()))
_TN = (((0,), (0,)), ((), ()))


def attn_fwd(name, q, k, v, tm=512):
    t, d = q.shape
    m = k.shape[0]
    tm = _tile(t, tm)
    scale = CA_HEAD_DIM ** -0.5

    def body(q_ref, k_ref, v_ref, o_ref):
        for h in range(CA_HEADS):
            cs = slice(h * CA_HEAD_DIM, (h + 1) * CA_HEAD_DIM)
            s = lax.dot_general(q_ref[:, cs], k_ref[:, cs], _NT, preferred_element_type=F32) * scale
            e = jnp.exp(s - jnp.max(s, axis=-1, keepdims=True))
            p = e / jnp.sum(e, axis=-1, keepdims=True)
            o_ref[:, cs] = jnp.dot(_bf(p), v_ref[:, cs], preferred_element_type=F32).astype(o_ref.dtype)

    return pl.pallas_call(
        body, grid=(t // tm,),
        in_specs=[pl.BlockSpec((tm, d), lambda i: (i, 0)), pl.BlockSpec((m, d), lambda i: (0, 0)),
                  pl.BlockSpec((m, d), lambda i: (0, 0))],
        out_specs=pl.BlockSpec((tm, d), lambda i: (i, 0)), out_shape=S((t, d), BF16),
        compiler_params=_cp("parallel"), name=name)(q, k, v)


def attn_bwd(name, q, k, v, do, tm=512):
    t, d = q.shape
    m = k.shape[0]
    tm = _tile(t, tm)
    scale = CA_HEAD_DIM ** -0.5

    def body(q_ref, k_ref, v_ref, do_ref, dq_ref, dk_ref, dv_ref):
        @pl.when(pl.program_id(0) == 0)
        def _():
            dk_ref[...] = jnp.zeros(dk_ref.shape, F32)
            dv_ref[...] = jnp.zeros(dv_ref.shape, F32)

        for h in range(CA_HEADS):
            cs = slice(h * CA_HEAD_DIM, (h + 1) * CA_HEAD_DIM)
            qh, kh, vh, doh = q_ref[:, cs], k_ref[:, cs], v_ref[:, cs], do_ref[:, cs]
            s = lax.dot_general(qh, kh, _NT, preferred_element_type=F32) * scale
            e = jnp.exp(s - jnp.max(s, axis=-1, keepdims=True))
            p = e / jnp.sum(e, axis=-1, keepdims=True)
            pb = _bf(p)
            dv_ref[:, cs] += lax.dot_general(pb, doh, _TN, preferred_element_type=F32)
            dp = lax.dot_general(doh, vh, _NT, preferred_element_type=F32)
            ds = _bf(p * (dp - jnp.sum(dp * p, axis=-1, keepdims=True)) * scale)
            dq_ref[:, cs] = jnp.dot(ds, kh, preferred_element_type=F32).astype(dq_ref.dtype)
            dk_ref[:, cs] += lax.dot_general(ds, qh, _TN, preferred_element_type=F32)

    return pl.pallas_call(
        body, grid=(t // tm,),
        in_specs=[pl.BlockSpec((tm, d), lambda i: (i, 0)), pl.BlockSpec((m, d), lambda i: (0, 0)),
                  pl.BlockSpec((m, d), lambda i: (0, 0)), pl.BlockSpec((tm, d), lambda i: (i, 0))],
        out_specs=[pl.BlockSpec((tm, d), lambda i: (i, 0)), pl.BlockSpec((m, d), lambda i: (0, 0)),
                   pl.BlockSpec((m, d), lambda i: (0, 0))],
        out_shape=[S((t, d), BF16), S((m, d), F32), S((m, d), F32)],
        compiler_params=_cp("arbitrary"), name=name)(q, k, v, do)


SUB = 8
S5_ROWS = 256


def s5_constants(lam_re, lam_im, log_dt, b_re, b_im, c_re, c_im):
    dt = jnp.exp(log_dt)[:, None]
    mag = jnp.exp(lam_re * dt)
    ar = mag * jnp.cos(lam_im * dt)
    ai = mag * jnp.sin(lam_im * dt)
    den = lam_re * lam_re + lam_im * lam_im
    qr = ((ar - 1.0) * lam_re + ai * lam_im) / den
    qi = (ai * lam_re - (ar - 1.0) * lam_im) / den
    bbr = qr[..., None] * b_re - qi[..., None] * b_im
    bbi = qr[..., None] * b_im + qi[..., None] * b_re
    eye = jnp.eye(C_GROUPS, dtype=F32)

    def in_mat(bb):
        return (bb.transpose(0, 2, 1)[:, :, None, :] * eye[:, None, :, None]).reshape(C_WIDTH, N_STATE)

    def out_mat(cc):
        return (cc.transpose(0, 2, 1)[:, :, None, :] * eye[:, None, :, None]).reshape(N_STATE, C_WIDTH)

    mb = jnp.concatenate([in_mat(bbr), in_mat(bbi)], axis=1)
    mc = jnp.concatenate([out_mat(c_re), -out_mat(c_im)], axis=0)
    a = jnp.stack([ar.reshape(N_STATE), ai.reshape(N_STATE)])
    return a, mb, mc


def _scan_powers(a, conj):
    ar, ai = a[0], (-a[1] if conj else a[1])
    pows = [(ar, ai)]
    for _ in range(SUB - 1):
        pr, pi = pows[-1]
        pows.append((pr * ar - pi * ai, pr * ai + pi * ar))
    rows = jnp.arange(SUB)[:, None]
    out = []
    for s in (1, 2, 4):
        keep = (rows + s <= SUB - 1) if conj else (rows >= s)
        out.append(jnp.stack([jnp.where(keep, pows[s - 1][0][None, :], 0.0), jnp.where(keep, pows[s - 1][1][None, :], 0.0)]))
    order = [SUB - 1 - i for i in range(SUB)] if conj else list(range(SUB))
    out.append(jnp.stack([jnp.stack([pows[i][0] for i in order]), jnp.stack([pows[i][1] for i in order])]))
    return jnp.stack(out)


def _cmul_add(xr, xi, pr, pi, zr, zi):
    return xr + pr * zr - pi * zi, xi + pr * zi + pi * zr


def s5_fwd(name, u, mb, mc, pw, dskip):
    t = u.shape[0]
    tm = _tile(t, S5_ROWS)
    ns = N_STATE

    def body(u_ref, mb_ref, mc_ref, pw_ref, d_ref, gy_ref, y_ref, xs_ref, xb_ref, carry):
        @pl.when(pl.program_id(0) == 0)
        def _():
            carry[...] = jnp.zeros(carry.shape, F32)

        uv = u_ref[...]
        xs_ref[...] = jnp.dot(_bf(uv), mb_ref[...], preferred_element_type=F32)

        def group(i, _):
            r0 = pl.multiple_of(i * SUB, SUB)
            xr = xs_ref[pl.ds(r0, SUB), 0:ns]
            xi = xs_ref[pl.ds(r0, SUB), ns:2 * ns]
            for k, s in enumerate((1, 2, 4)):
                xr, xi = _cmul_add(xr, xi, pw_ref[k, 0], pw_ref[k, 1], pltpu.roll(xr, s, 0), pltpu.roll(xi, s, 0))
            xr, xi = _cmul_add(xr, xi, pw_ref[3, 0], pw_ref[3, 1], carry[0], carry[1])
            xs_ref[pl.ds(r0, SUB), 0:ns] = xr
            xs_ref[pl.ds(r0, SUB), ns:2 * ns] = xi
            carry[0] = jnp.broadcast_to(xr[SUB - 1:SUB, :], (SUB, ns))
            carry[1] = jnp.broadcast_to(xi[SUB - 1:SUB, :], (SUB, ns))
            return 0
        lax.fori_loop(0, tm // SUB, group, 0)

        xb = _bf(xs_ref[...])
        xb_ref[...] = xb
        y = jnp.dot(xb, mc_ref[...], preferred_element_type=F32) + d_ref[...] * uv
        y_ref[...] = y
        gy_ref[...] = _gelu(y).astype(gy_ref.dtype)

    c = u.shape[1]
    return pl.pallas_call(
        body, grid=(t // tm,),
        in_specs=[pl.BlockSpec((tm, c), lambda i: (i, 0)), pl.BlockSpec(mb.shape, lambda i: (0, 0)),
                  pl.BlockSpec(mc.shape, lambda i: (0, 0)), pl.BlockSpec(pw.shape, lambda i: (0, 0, 0, 0)),
                  pl.BlockSpec((1, c), lambda i: (0, 0))],
        out_specs=[pl.BlockSpec((tm, c), lambda i: (i, 0)), pl.BlockSpec((tm, c), lambda i: (i, 0)),
                   pl.BlockSpec((tm, 2 * ns), lambda i: (i, 0)), pl.BlockSpec((tm, 2 * ns), lambda i: (i, 0))],
        out_shape=[S((t, c), BF16), S((t, c), F32), S((t, 2 * ns), F32), S((t, 2 * ns), BF16)],
        scratch_shapes=[pltpu.VMEM((2, SUB, ns), F32)],
        compiler_params=_cp("arbitrary"), name=name)(u, mb, mc, pw, dskip)


def s5_bwd(name, dgy, y, u, xs, mct, mbt, qw, dskip):
    t, c = u.shape
    tm = _tile(t, S5_ROWS)
    nt = t // tm
    ns = N_STATE
    ng = tm // SUB

    def body(dgy_ref, y_ref, u_ref, xs_ref, prev_ref, mct_ref, mbt_ref, qw_ref, d_ref,
             du_ref, dy_ref, lb_ref, da_ref, dd_ref, lam, carry):
        step = pl.program_id(0)

        @pl.when(step == 0)
        def _():
            carry[...] = jnp.zeros(carry.shape, F32)
            da_ref[...] = jnp.zeros(da_ref.shape, F32)
            dd_ref[...] = jnp.zeros(dd_ref.shape, F32)

        uv = u_ref[...]
        dy = dgy_ref[...] * _gelu_grad(y_ref[...])
        dyb = _bf(dy)
        dy_ref[...] = dyb
        dd_ref[...] += jnp.sum(dy * uv, axis=0, keepdims=True)
        lam[...] = jnp.dot(dyb, mct_ref[...], preferred_element_type=F32)
        first_tile = (step == nt - 1).astype(F32)
        row0 = lax.broadcasted_iota(jnp.int32, (SUB, ns), 0) == 0

        def group(j, _):
            i = ng - 1 - j
            r0 = pl.multiple_of(i * SUB, SUB)
            lr = lam[pl.ds(r0, SUB), 0:ns]
            li = lam[pl.ds(r0, SUB), ns:2 * ns]
            for k, s in enumerate((1, 2, 4)):
                lr, li = _cmul_add(lr, li, qw_ref[k, 0], qw_ref[k, 1],
                                   pltpu.roll(lr, SUB - s, 0), pltpu.roll(li, SUB - s, 0))
            lr, li = _cmul_add(lr, li, qw_ref[3, 0], qw_ref[3, 1], carry[0], carry[1])
            lam[pl.ds(r0, SUB), 0:ns] = lr
            lam[pl.ds(r0, SUB), ns:2 * ns] = li
            carry[0] = jnp.broadcast_to(lr[0:1, :], (SUB, ns))
            carry[1] = jnp.broadcast_to(li[0:1, :], (SUB, ns))
            rp = pl.multiple_of(jnp.maximum(i - 1, 0) * SUB, SUB)
            in_tile = (i > 0).astype(F32)
            out_tile = (1.0 - in_tile) * (1.0 - first_tile)
            pr = xs_ref[pl.ds(rp, SUB), 0:ns] * in_tile + prev_ref[:, 0:ns] * out_tile
            pi = xs_ref[pl.ds(rp, SUB), ns:2 * ns] * in_tile + prev_ref[:, ns:2 * ns] * out_tile
            xpr = jnp.where(row0, pltpu.roll(pr, 1, 0), pltpu.roll(xs_ref[pl.ds(r0, SUB), 0:ns], 1, 0))
            xpi = jnp.where(row0, pltpu.roll(pi, 1, 0), pltpu.roll(xs_ref[pl.ds(r0, SUB), ns:2 * ns], 1, 0))
            da_ref[0] += lr * xpr + li * xpi
            da_ref[1] += li * xpr - lr * xpi
            return 0
        lax.fori_loop(0, ng, group, 0)

        lb = _bf(lam[...])
        lb_ref[...] = lb
        du_ref[...] = (jnp.dot(lb, mbt_ref[...], preferred_element_type=F32) + d_ref[...] * dy).astype(du_ref.dtype)

    rev = lambda i: (nt - 1 - i, 0)
    prev = lambda i: (jnp.maximum((nt - 1 - i) * (tm // SUB) - 1, 0), 0)
    return pl.pallas_call(
        body, grid=(nt,),
        in_specs=[pl.BlockSpec((tm, c), rev), pl.BlockSpec((tm, c), rev), pl.BlockSpec((tm, c), rev),
                  pl.BlockSpec((tm, 2 * ns), rev), pl.BlockSpec((SUB, 2 * ns), prev),
                  pl.BlockSpec(mct.shape, lambda i: (0, 0)), pl.BlockSpec(mbt.shape, lambda i: (0, 0)),
                  pl.BlockSpec(qw.shape, lambda i: (0, 0, 0, 0)), pl.BlockSpec((1, c), lambda i: (0, 0))],
        out_specs=[pl.BlockSpec((tm, c), rev), pl.BlockSpec((tm, c), rev), pl.BlockSpec((tm, 2 * ns), rev),
                   pl.BlockSpec((2, SUB, ns), lambda i: (0, 0, 0)), pl.BlockSpec((1, c), lambda i: (0, 0))],
        out_shape=[S((t, c), BF16), S((t, c), BF16), S((t, 2 * ns), BF16), S((2, SUB, ns), F32), S((1, c), F32)],
        scratch_shapes=[pltpu.VMEM((tm, 2 * ns), F32), pltpu.VMEM((2, SUB, ns), F32)],
        compiler_params=_cp("arbitrary"), name=name)(dgy, y, u, xs, xs, mct, mbt, qw, dskip)


def _first(accs, *_):
    return [accs[0]]


def _add_res(accs, res):
    return [accs[0] + res]


def even_fwd(x, w):
    t = x.shape[0]
    hn, r = rms_fwd("e_norm_f", x, w["e_norm"])
    (proj,) = mm_nn("e_in_f", t, IN_WIDTH, [(hn, w["e_w_in_t"], 0, "t")], 1, _first, [F32])
    out_a = gmlp_fwd("e_gmlp_f", proj, w["e_gmlp_w"], w["e_gmlp_b"])
    hc = conv_fwd("e_conv_f", proj, w["e_conv_w"], w["e_conv_b"])
    out_b = ln_silu_fwd("e_ln_f", hc, w["e_conv_ln_g"], w["e_conv_ln_b"])
    (x1,) = mm_nn("e_out_f", t, D_MODEL, [(out_a, (w["e_w_out"], 0), 0), (out_b, (w["e_w_out"], 1), 0)],
                  1, _add_res, [F32], tiled=[x])
    return x1, (x, hn, r, proj, out_a, hc, out_b)


def even_bwd(dx, dxb, saved, w):
    x, hn, r, proj, out_a, hc, out_b = saved
    t = x.shape[0]
    (dcat,) = mm_nn("e_out_b", t, D_MODEL, [(dxb, w["e_w_out"], 0, "t")], 1, _first, [F32])
    g_w_out = jnp.concatenate([mm_tn("e_out_wa", out_a, dxb), mm_tn("e_out_wb", out_b, dxb)], axis=0)
    dab, g_gw, g_gb = gmlp_bwd("e_gmlp_b", proj, dcat, w["e_gmlp_w"], w["e_gmlp_b"])
    dhc, g_lg, g_lb = ln_silu_bwd("e_ln_b", hc, dcat, w["e_conv_ln_g"], w["e_conv_ln_b"])
    dba, dbg, g_cw, g_cb = conv_bwd("e_conv_b", proj, dhc, w["e_conv_w"])
    w_in_t = w["e_w_in_t"]
    (dhn,) = mm_nn("e_in_b", t, D_MODEL, [(dab, (w_in_t, 0), 0), (dba, (w_in_t, 2), 0), (dbg, (w_in_t, 3), 0)],
                   1, _first, [F32])
    g_w_in_t = jnp.concatenate([mm_tn("e_in_w0", dab, hn), mm_tn("e_in_w1", dba, hn), mm_tn("e_in_w2", dbg, hn)], axis=0)
    dx0, dx0b, g_norm = rms_bwd("e_norm_b", dhn, x, r, w["e_norm"], dres=dx)
    grads = dict(e_norm=g_norm, e_w_in_t=g_w_in_t, e_gmlp_w=g_gw[None], e_gmlp_b=g_gb.reshape(1, A_GROUPS, GMLP_BLOCK),
                 e_conv_w=g_cw[None], e_conv_b=g_cb, e_conv_ln_g=g_lg, e_conv_ln_b=g_lb, e_w_out=g_w_out)
    return dx0, dx0b, grads


def odd_fwd(x, w, consts):
    t = x.shape[0]
    _, mb, mc, pw, _ = consts
    hn, r = rms_fwd("o_norm_f", x, w["o_norm"])
    (u,) = mm_nn("o_in_f", t, C_WIDTH, [(hn, w["o_w_in"], 0)], 1, _first, [F32])
    gy, y, xs, xsb = s5_fwd("o_s5_f", u, _bf(mb), _bf(mc), pw, w["o_d"])
    w_out_t = w["o_w_out_t"]

    def epi(accs, res):
        return [res + accs[0] * _sigmoid(accs[1]), accs[0], accs[1]]

    x1, o1, o2 = mm_nn("o_out_f", t, D_MODEL, [(gy, (w_out_t, 0), 0, "t"), (gy, (w_out_t, D_MODEL), 1, "t")], 2, epi,
                       [F32, BF16, BF16], tiled=[x])
    return x1, (x, hn, r, u, gy, y, xs, xsb, o1, o2)


def odd_bwd(dx, dxb, saved, w, consts, consts_vjp):
    x, hn, r, u, gy, y, xs, xsb, o1, o2 = saved
    t = x.shape[0]
    _, mb, mc, _, qw = consts

    def gate_bwd(dv, a, b):
        a = a.astype(F32)
        sg = _sigmoid(b.astype(F32))
        return [jnp.concatenate([dv * sg, dv * a * sg * (1.0 - sg)], axis=1)], []

    (do12,) = rows_call("o_gate_b", gate_bwd, [dx, o1, o2], [], [(2 * D_MODEL, BF16)], [])
    (dgy,) = mm_nn("o_out_b", t, C_WIDTH, [(do12, w["o_w_out_t"], 0)], 1, _first, [F32])
    g_w_out_t = mm_tn("o_out_w", do12, gy)
    du, dyb, lamb, da8, g_d = s5_bwd("o_s5_b", dgy, y, u, xs, _bf(mc.T), _bf(mb.T), qw, w["o_d"])
    d_mb = mm_tn("o_s5_wb", u, lamb, out_dtype=F32)
    d_mc = mm_tn("o_s5_wc", xsb, dyb, out_dtype=F32)
    g_lr, g_li, g_dt, g_br, g_bi, g_cr, g_ci = consts_vjp((jnp.sum(da8, axis=1), d_mb, d_mc))
    g_w_in = mm_tn("o_in_w", hn, du)
    (dhn,) = mm_nn("o_in_b", t, D_MODEL, [(du, w["o_w_in"], 0, "t")], 1, _first, [F32])
    dx0, dx0b, g_norm = rms_bwd("o_norm_b", dhn, x, r, w["o_norm"], dres=dx)
    grads = dict(o_norm=g_norm, o_w_in=g_w_in, o_lam_re=g_lr[None], o_lam_im=g_li[None], o_log_dt=g_dt[None],
                 o_b_re=g_br[None], o_b_im=g_bi[None], o_c_re=g_cr[None], o_c_im=g_ci[None], o_d=g_d, o_w_out_t=g_w_out_t)
    return dx0, dx0b, grads


def ca_fwd(i, x, mem, w):
    t, m = x.shape[0], mem.shape[0]
    xn, r = rms_fwd(f"ca{i}_norm_f", x, w["ca_norm"][i:i + 1])
    mn, rm = rms_fwd(f"ca{i}_mnorm_f", mem, w["ca_mem_norm"][i:i + 1])
    (q,) = mm_nn(f"ca{i}_q_f", t, D_MODEL, [(xn, w["ca_wq"][i], 0)], 1, _first, [BF16])
    k, v = mm_nn(f"ca{i}_kv_f", m, D_MODEL, [(mn, w["ca_wk"][i], 0), (mn, w["ca_wv"][i], 1)], 2,
                 lambda accs: [accs[0], accs[1]], [BF16, BF16])
    o = attn_fwd(f"ca{i}_attn_f", q, k, v)
    (x1,) = mm_nn(f"ca{i}_o_f", t, D_MODEL, [(o, w["ca_wo"][i], 0)], 1, _add_res, [F32], tiled=[x])
    return x1, (x, xn, r, mn, rm, q, k, v, o)


def ca_bwd(i, dx, dxb, saved, mem, w):
    x, xn, r, mn, rm, q, k, v, o = saved
    t, m = x.shape[0], mem.shape[0]
    (do,) = mm_nn(f"ca{i}_o_b", t, D_MODEL, [(dxb, w["ca_wo"][i], 0, "t")], 1, _first, [BF16])
    g_wo = mm_tn(f"ca{i}_o_w", o, dxb)
    dq, dk, dv = attn_bwd(f"ca{i}_attn_b", q, k, v, do)
    g_wq = mm_tn(f"ca{i}_q_w", xn, dq)
    g_wk = mm_tn(f"ca{i}_k_w", mn, dk)
    g_wv = mm_tn(f"ca{i}_v_w", mn, dv)
    (dxn,) = mm_nn(f"ca{i}_q_b", t, D_MODEL, [(dq, w["ca_wq"][i], 0, "t")], 1, _first, [F32])
    (dmn,) = mm_nn(f"ca{i}_kv_b", m, D_MODEL, [(dk, w["ca_wk"][i], 0, "t"), (dv, w["ca_wv"][i], 0, "t")], 1, _first, [F32])
    g_mnorm = rms_bwd_gain_only(f"ca{i}_mnorm_b", dmn, mem, rm)
    dx0, dx0b, g_norm = rms_bwd(f"ca{i}_norm_b", dxn, x, r, w["ca_norm"][i:i + 1], dres=dx)
    return dx0, dx0b, dict(ca_norm=g_norm, ca_mem_norm=g_mnorm, ca_wq=g_wq, ca_wk=g_wk, ca_wv=g_wv, ca_wo=g_wo)


def ffn_fwd(i, x, w):
    t = x.shape[0]
    xn, r = rms_fwd(f"ffn{i}_norm_f", x, w["ffn_norm"][i:i + 1])

    def epi(accs):
        g, u = accs
        return [g, u, g * _sigmoid(g) * u]

    g, u, h = mm_nn(f"ffn{i}_up_f", t, FFN_HIDDEN, [(xn, w["ffn_w_gate_t"][i], 0, "t"), (xn, w["ffn_w_up_t"][i], 1, "t")],
                    2, epi, [BF16, BF16, BF16])
    (x1,) = mm_nn(f"ffn{i}_down_f", t, D_MODEL, [(h, w["ffn_w_down"][i], 0)], 1, _add_res, [F32], tiled=[x])
    return x1, (x, xn, r, g, u, h)


def ffn_bwd(i, dx, dxb, saved, w):
    x, xn, r, g, u, h = saved
    t = x.shape[0]

    def epi(accs, gv, uv):
        dh = accs[0]
        gv = gv.astype(F32)
        uv = uv.astype(F32)
        s = _sigmoid(gv)
        return [dh * uv * s * (1.0 + gv * (1.0 - s)), dh * gv * s]

    dg, du = mm_nn(f"ffn{i}_down_b", t, FFN_HIDDEN, [(dxb, w["ffn_w_down"][i], 0, "t")], 1, epi, [BF16, BF16], tiled=[g, u])
    g_wd = mm_tn(f"ffn{i}_down_w", h, dxb)
    g_wg_t = mm_tn(f"ffn{i}_gate_w", dg, xn)
    g_wu_t = mm_tn(f"ffn{i}_up_w", du, xn)
    (dxn,) = mm_nn(f"ffn{i}_up_b", t, D_MODEL, [(dg, w["ffn_w_gate_t"][i], 0), (du, w["ffn_w_up_t"][i], 0)], 1, _first, [F32])
    dx0, dx0b, g_norm = rms_bwd(f"ffn{i}_norm_b", dxn, x, r, w["ffn_norm"][i:i + 1], dres=dx)
    return dx0, dx0b, dict(ffn_norm=g_norm, ffn_w_gate_t=g_wg_t, ffn_w_up_t=g_wu_t, ffn_w_down=g_wd)


_S5_PARAMS = ("o_lam_re", "o_lam_im", "o_log_dt", "o_b_re", "o_b_im", "o_c_re", "o_c_im")


def local_step(x, mem, target, w, fetch=None, on_grads=None):
    def consts_fn(*p):
        a, mb, mc = s5_constants(*p)
        return a, mb, mc

    (a, mb, mc), consts_vjp = jax.vjp(consts_fn, *[w[k] for k in _S5_PARAMS])
    consts = (a, mb, mc, _scan_powers(a, False), _scan_powers(a, True))

    def need(stage, after):
        if fetch is not None:
            for k, v in fetch(stage, after).items():
                if isinstance(k, tuple):
                    w.setdefault(k[0], {})[k[1]] = v
                else:
                    w[k] = v

    need(0, x)
    x1, s_e = even_fwd(x, w)
    need(1, x1)
    x2, s_c0 = ca_fwd(0, x1, mem, w)
    need(2, x2)
    x3, s_f0 = ffn_fwd(0, x2, w)
    x4, s_o = odd_fwd(x3, w, consts)
    need(3, x4)
    x5, s_c1 = ca_fwd(1, x4, mem, w)
    x6, s_f1 = ffn_fwd(1, x5, w)
    dx, dxb, g_final, loss = final_loss("final_loss", x6, w["final_norm"], target)

    def emit(stage, carry, plain, layered=None, layer=0):
        if on_grads is None:
            return carry
        out = dict(plain)
        out.update({(k, layer): v for k, v in (layered or {}).items()})
        return on_grads(stage, out, list(carry))

    dx, dxb, g_f1 = ffn_bwd(1, dx, dxb, s_f1, w)
    dx, dxb = emit(0, (dx, dxb), {}, g_f1, 1)
    dx, dxb, g_c1 = ca_bwd(1, dx, dxb, s_c1, mem, w)
    dx, dxb, g_o = odd_bwd(dx, dxb, s_o, w, consts, consts_vjp)
    dx, dxb = emit(1, (dx, dxb), g_o, g_c1, 1)
    dx, dxb, g_f0 = ffn_bwd(0, dx, dxb, s_f0, w)
    dx, dxb = emit(2, (dx, dxb), {}, g_f0, 0)
    dx, dxb, g_c0 = ca_bwd(0, dx, dxb, s_c0, mem, w)
    dx, dxb = emit(3, (dx, dxb), {}, g_c0, 0)
    dx, dxb, g_e = even_bwd(dx, dxb, s_e, w)
    (dx,) = emit(4, (dx,), {**g_e, "o_norm": g_o["o_norm"], "o_d": g_o["o_d"]})

    grads = dict(g_e)
    grads.update(g_o)
    for g0, g1 in ((g_c0, g_c1), (g_f0, g_f1)):
        for k in g0:
            grads[k] = jnp.concatenate([g0[k], g1[k]], axis=0) if k.endswith("norm") else (g0[k], g1[k])
    grads["final_norm"] = g_final
    return loss, dx, grads


def _group(axes):
    pos = {a: lax.axis_index(a) for a in ("x", "y", "c")}
    me = 0
    for a in axes:
        me = me * 2 + pos[a]
    peers = []
    for mask in range(1, 2 ** len(axes)):
        peer = dict(pos)
        for bit, a in enumerate(axes):
            if (mask >> (len(axes) - 1 - bit)) & 1:
                peer[a] = 1 - pos[a]
        idx = 0
        for a in axes:
            idx = idx * 2 + peer[a]
        peers.append((idx, (peer["x"], peer["y"], peer["c"])))
    return me, peers


def _sibling():
    x, y, c = lax.axis_index("x"), lax.axis_index("y"), lax.axis_index("c")
    return c, (x, y, 1 - c)


def gather_ici(name, blks):
    k_ops = len(blks)
    out_shape = [S((4, 2) + tuple(b.shape), b.dtype) for b in blks]

    def body(*refs):
        in_refs, out_refs = refs[:k_ops], refs[k_ops:2 * k_ops]
        send_sems, recv_sems, local_sems = refs[2 * k_ops:]
        me, peers = _group(("x", "y"))
        core = lax.axis_index("c")
        local, sent, landed = [], [], []
        for i in range(k_ops):
            cp = pltpu.make_async_copy(in_refs[i], out_refs[i].at[me, core], local_sems.at[i])
            cp.start()
            local.append(cp)
        for k, (idx, dev) in enumerate(peers):
            for i in range(k_ops):
                s = i * 3 + k
                cp = pltpu.make_async_remote_copy(src_ref=in_refs[i], dst_ref=out_refs[i].at[me, core], send_sem=send_sems.at[s],
                                                  recv_sem=recv_sems.at[s], device_id=dev, device_id_type=MESH)
                cp.start()
                sent.append(cp)
                landed.append(pltpu.make_async_remote_copy(src_ref=in_refs[i], dst_ref=out_refs[i].at[idx, core],
                                                           send_sem=send_sems.at[s], recv_sem=recv_sems.at[s],
                                                           device_id=dev, device_id_type=MESH))
        for cp in landed:
            cp.wait_recv()
        for cp in sent:
            cp.wait_send()
        for cp in local:
            cp.wait()

    return pl.pallas_call(
        body, in_specs=[ANY] * k_ops, out_specs=[ANY] * k_ops, out_shape=out_shape,
        scratch_shapes=[pltpu.SemaphoreType.DMA((k_ops * 3,)), pltpu.SemaphoreType.DMA((k_ops * 3,)),
                        pltpu.SemaphoreType.DMA((k_ops,))],
        name=name)(*blks)


_HBM = pl.BlockSpec(memory_space=pltpu.HBM)
_SEM = pl.BlockSpec(memory_space=pltpu.SEMAPHORE)
_EFFECT = pltpu.SideEffectType.DATAFLOW_SIDE_EFFECTING


def gather_ici_start(name, groups):
    flat = [b for g in groups for b in g]
    sizes = [len(g) for g in groups]
    k_ops, n_g = len(flat), len(groups)
    lands = [lax.empty((4, 2) + tuple(b.shape), b.dtype) for b in flat]

    def body(*refs):
        src, land = refs[:k_ops], refs[k_ops:2 * k_ops]
        sems = refs[2 * k_ops:2 * k_ops + 3 * n_g]
        token = refs[-1]
        me, peers = _group(("x", "y"))
        core = lax.axis_index("c")
        i = 0
        for g in range(n_g):
            send, recv, loc = sems[3 * g:3 * g + 3]
            for j in range(sizes[g]):
                pltpu.make_async_copy(src[i], land[i].at[me, core], loc.at[j]).start()
                for k, (_, dev) in enumerate(peers):
                    pltpu.make_async_remote_copy(src_ref=src[i], dst_ref=land[i].at[me, core], send_sem=send.at[3 * j + k],
                                                 recv_sem=recv.at[3 * j + k], device_id=dev, device_id_type=MESH).start()
                i += 1
        token[...] = jnp.zeros(token.shape, token.dtype)

    sem_shapes = []
    for s in sizes:
        sem_shapes += [pltpu.SemaphoreType.DMA((3 * s,)), pltpu.SemaphoreType.DMA((3 * s,)), pltpu.SemaphoreType.DMA((s,))]
    thru = [pltpu.HBM(a.shape, a.dtype) for a in flat + lands]
    outs = pl.pallas_call(
        body, name=name, out_shape=tuple(sem_shapes) + tuple(thru) + (S((8, LANES), F32),),
        in_specs=[_HBM] * (2 * k_ops), out_specs=[_SEM] * (3 * n_g) + [_HBM] * (2 * k_ops) + [pl.BlockSpec(memory_space=pltpu.VMEM)],
        input_output_aliases={i: 3 * n_g + i for i in range(2 * k_ops)},
        compiler_params=pltpu.CompilerParams(has_side_effects=_EFFECT),
    )(*[pltpu.with_memory_space_constraint(a, pltpu.HBM) for a in flat + lands])
    sems = [tuple(outs[3 * g:3 * g + 3]) for g in range(n_g)]
    srcs_thru, lands_thru, off = [], [], 3 * n_g
    for s in sizes:
        srcs_thru.append(list(outs[off:off + s]))
        off += s
    for s in sizes:
        lands_thru.append(list(outs[off:off + s]))
        off += s
    return sems, srcs_thru, lands_thru, outs[-1]


def gather_ici_wait(name, srcs, lands, sems, after):
    n = len(srcs)

    def body(*refs):
        src, land = refs[:n], refs[n:2 * n]
        send, recv, loc = refs[2 * n:2 * n + 3]
        me, peers = _group(("x", "y"))
        core = lax.axis_index("c")
        for j in range(n):
            for k, (idx, dev) in enumerate(peers):
                cp = pltpu.make_async_remote_copy(src_ref=src[j], dst_ref=land[j].at[idx, core], send_sem=send.at[3 * j + k],
                                                  recv_sem=recv.at[3 * j + k], device_id=dev, device_id_type=MESH)
                cp.wait_send()
                cp.wait_recv()
            pltpu.make_async_copy(src[j], land[j].at[me, core], loc.at[j]).wait()

    outs = pl.pallas_call(
        body, name=name, out_shape=tuple(pltpu.HBM(a.shape, a.dtype) for a in list(srcs) + list(lands)),
        in_specs=[_HBM] * (2 * n) + [_SEM] * 3 + [ANY], out_specs=[_HBM] * (2 * n),
        input_output_aliases={i: i for i in range(2 * n)},
        compiler_params=pltpu.CompilerParams(has_side_effects=_EFFECT),
    )(*srcs, *lands, *sems, after)
    return list(outs[n:])


def gather_d2d(name, bufs):
    k_ops = len(bufs)

    def body(*refs):
        in_refs, out_refs = refs[:k_ops], refs[k_ops:2 * k_ops]
        send_sems, recv_sems = refs[2 * k_ops:]
        core, sib = _sibling()
        sent, landed = [], []
        for i in range(k_ops):
            cp = pltpu.make_async_remote_copy(src_ref=in_refs[i].at[:, core], dst_ref=out_refs[i].at[:, core],
                                              send_sem=send_sems.at[i], recv_sem=recv_sems.at[i], device_id=sib, device_id_type=MESH)
            cp.start()
            sent.append(cp)
            landed.append(pltpu.make_async_remote_copy(src_ref=in_refs[i].at[:, core], dst_ref=out_refs[i].at[:, 1 - core],
                                                       send_sem=send_sems.at[i], recv_sem=recv_sems.at[i],
                                                       device_id=sib, device_id_type=MESH))
        for cp in landed:
            cp.wait_recv()
        for cp in sent:
            cp.wait_send()

    return pl.pallas_call(
        body, in_specs=[ANY] * k_ops, out_specs=[ANY] * k_ops, out_shape=[S(b.shape, b.dtype) for b in bufs],
        input_output_aliases={i: i for i in range(k_ops)},
        scratch_shapes=[pltpu.SemaphoreType.DMA((k_ops,)), pltpu.SemaphoreType.DMA((k_ops,))],
        name=name)(*bufs)


def all_gather(name, blks):
    bufs = gather_d2d(name + "_d2d", gather_ici(name + "_ici", blks))
    return [p.reshape((N_DEV * b.shape[0],) + tuple(b.shape[1:])) for p, b in zip(bufs, blks)]


def scatter_d2d(name, pack):
    q, _, rows, c = pack.shape

    def body(in_ref, out_ref, send_sem, recv_sem):
        core, sib = _sibling()
        cp = pltpu.make_async_remote_copy(src_ref=in_ref.at[:, 1 - core], dst_ref=out_ref, send_sem=send_sem, recv_sem=recv_sem,
                                          device_id=sib, device_id_type=MESH)
        cp.start()
        cp.wait_recv()
        cp.wait_send()

    return pl.pallas_call(
        body, in_specs=[ANY], out_specs=ANY, out_shape=S((q, rows, c), pack.dtype),
        scratch_shapes=[pltpu.SemaphoreType.DMA, pltpu.SemaphoreType.DMA], name=name)(pack)


def scatter_ici_start(name, arr, carry):
    land = lax.empty(arr.shape, arr.dtype)
    n_c = len(carry)

    def body(*refs):
        in_ref, land_ref = refs[0], refs[1]
        send, recv = refs[2 + n_c], refs[3 + n_c]
        me, peers = _group(("x", "y"))
        for k, (idx, dev) in enumerate(peers):
            pltpu.make_async_remote_copy(src_ref=in_ref.at[idx], dst_ref=land_ref.at[me], send_sem=send.at[k], recv_sem=recv.at[k],
                                         device_id=dev, device_id_type=MESH).start()

    thru = [arr, land] + list(carry)
    outs = pl.pallas_call(
        body, name=name,
        out_shape=(pltpu.SemaphoreType.DMA((3,)), pltpu.SemaphoreType.DMA((3,))) + tuple(pltpu.HBM(a.shape, a.dtype) for a in thru),
        in_specs=[_HBM] * len(thru), out_specs=[_SEM, _SEM] + [_HBM] * len(thru),
        input_output_aliases={i: 2 + i for i in range(len(thru))},
        compiler_params=pltpu.CompilerParams(has_side_effects=_EFFECT),
    )(*[pltpu.with_memory_space_constraint(a, pltpu.HBM) for a in thru])
    return (outs[0], outs[1]), outs[2], outs[3], list(outs[4:])


def scatter_ici_wait(name, arr, land, sems, after):
    def body(in_ref, land_ref, send, recv, after_ref, in_thru, land_thru):
        _, peers = _group(("x", "y"))
        for k, (idx, dev) in enumerate(peers):
            cp = pltpu.make_async_remote_copy(src_ref=in_ref.at[idx], dst_ref=land_ref.at[idx], send_sem=send.at[k],
                                              recv_sem=recv.at[k], device_id=dev, device_id_type=MESH)
            cp.wait_send()
            cp.wait_recv()

    outs = pl.pallas_call(
        body, name=name, out_shape=(pltpu.HBM(arr.shape, arr.dtype), pltpu.HBM(arr.shape, arr.dtype)),
        in_specs=[_HBM, _HBM, _SEM, _SEM, ANY], out_specs=[_HBM, _HBM], input_output_aliases={0: 0, 1: 1},
        compiler_params=pltpu.CompilerParams(has_side_effects=_EFFECT),
    )(arr, land, sems[0], sems[1], after)
    return outs[0], outs[1]


def _row_tile(rows, cap=512):
    return next(t for t in range(cap - cap % 16, 0, -16) if rows % t == 0)


def sum_pair(name, pack, recv, core):
    q, rows, c = recv.shape
    tr = _row_tile(rows)

    def body(core_ref, a_ref, b_ref, o_ref):
        o_ref[...] = (a_ref[...].astype(F32) + b_ref[...].astype(F32)).astype(o_ref.dtype)

    spec = pltpu.PrefetchScalarGridSpec(
        num_scalar_prefetch=1, grid=(q, rows // tr),
        in_specs=[pl.BlockSpec((None, None, tr, c), lambda j, i, core: (j, core[0], i, 0)),
                  pl.BlockSpec((None, tr, c), lambda j, i, core: (j, i, 0))],
        out_specs=pl.BlockSpec((None, tr, c), lambda j, i, core: (j, i, 0)))
    return pl.pallas_call(body, grid_spec=spec, out_shape=S(recv.shape, recv.dtype),
                          compiler_params=_cp("parallel", "parallel"), name=name)(core, pack, recv)


def sum_quad(name, own, recv, chip):
    _, rows, c = recv.shape
    tr = _row_tile(rows)

    def body(chip_ref, a_ref, r1_ref, r2_ref, r3_ref, o_ref):
        o_ref[...] = ((a_ref[...].astype(F32) + r1_ref[...].astype(F32)) + r2_ref[...].astype(F32)) + r3_ref[...].astype(F32)

    def slot(mask):
        return pl.BlockSpec((None, tr, c), lambda i, chip, mask=mask: (jnp.bitwise_xor(chip[0], mask), i, 0))

    spec = pltpu.PrefetchScalarGridSpec(
        num_scalar_prefetch=1, grid=(rows // tr,), in_specs=[slot(0), slot(1), slot(2), slot(3)],
        out_specs=pl.BlockSpec((tr, c), lambda i, chip: (i, 0)))
    return pl.pallas_call(body, grid_spec=spec, out_shape=S((rows, c), F32),
                          compiler_params=_cp("parallel"), name=name)(chip, own, recv, recv, recv)


def adamw_native(name, g, w, m, v, tr=512):
    shape = w.shape
    cols = shape[-1]
    rows = w.size // cols
    tr = _tile(rows, tr) if rows % 8 == 0 else rows
    c1 = 1.0 - ADAM_B1 ** ADAM_STEP
    c2 = 1.0 - ADAM_B2 ** ADAM_STEP

    def body(g_ref, w_ref, m_ref, v_ref, d_ref, m2_ref, v2_ref):
        gv = g_ref[...]
        m2 = ADAM_B1 * m_ref[...] + (1.0 - ADAM_B1) * gv
        v2 = ADAM_B2 * v_ref[...] + (1.0 - ADAM_B2) * (gv * gv)
        m2_ref[...] = m2
        v2_ref[...] = v2
        d_ref[...] = -ADAM_LR * ((m2 / c1) / (jnp.sqrt(v2 / c2) + ADAM_EPS) + ADAM_WD * w_ref[...])

    row = pl.BlockSpec((tr, cols), lambda i: (i, 0))
    outs = pl.pallas_call(body, grid=(rows // tr,), in_specs=[row] * 4, out_specs=[row] * 3,
                          out_shape=[S((rows, cols), F32)] * 3, compiler_params=_cp("parallel"),
                          name=name)(*[a.reshape(rows, cols) for a in (g, w, m, v)])
    return tuple(o.reshape(shape) for o in outs)


def adamw_call(name, slots, w, m, v, tr=1024):
    n, r, c = slots.shape
    tr = _tile(r, tr)
    c1 = 1.0 - ADAM_B1 ** ADAM_STEP
    c2 = 1.0 - ADAM_B2 ** ADAM_STEP

    def body(s_ref, w_ref, m_ref, v_ref, g_ref, d_ref, m2_ref, v2_ref):
        g = s_ref[0].astype(F32)
        for j in range(1, n):
            g = g + s_ref[j].astype(F32)
        m2 = ADAM_B1 * m_ref[...] + (1.0 - ADAM_B1) * g
        v2 = ADAM_B2 * v_ref[...] + (1.0 - ADAM_B2) * (g * g)
        g_ref[...] = g
        m2_ref[...] = m2
        v2_ref[...] = v2
        d_ref[...] = -ADAM_LR * ((m2 / c1) / (jnp.sqrt(v2 / c2) + ADAM_EPS) + ADAM_WD * w_ref[...])

    row = pl.BlockSpec((tr, c), lambda i: (i, 0))
    return pl.pallas_call(body, grid=(r // tr,), in_specs=[pl.BlockSpec((n, tr, c), lambda i: (0, i, 0)), row, row, row],
                          out_specs=[row, row, row, row], out_shape=[S((r, c), F32)] * 4,
                          compiler_params=_cp("parallel"), name=name)(slots, w, m, v)


_REPLICATED = ("e_norm", "e_gmlp_w", "e_gmlp_b", "e_conv_b", "e_conv_ln_g", "e_conv_ln_b", "o_lam_re", "o_lam_im", "o_log_dt",
               "o_b_re", "o_b_im", "o_c_re", "o_c_im", "ca_norm", "ca_mem_norm", "ffn_norm", "final_norm")
_ORDER = ("e_norm", "e_w_in", "e_gmlp_w", "e_gmlp_b", "e_conv_w", "e_conv_b", "e_conv_ln_g", "e_conv_ln_b", "e_w_out",
          "o_norm", "o_w_in", "o_lam_re", "o_lam_im", "o_log_dt", "o_b_re", "o_b_im", "o_c_re", "o_c_im", "o_d", "o_w_out",
          "ca_norm", "ca_mem_norm", "ca_wq", "ca_wk", "ca_wv", "ca_wo", "ffn_norm", "ffn_w_gate", "ffn_w_up", "ffn_w_down",
          "final_norm")


def _rows128(a, multiple=8):
    flat = a.reshape(-1)
    rows = -(-flat.shape[0] // (LANES * multiple)) * multiple
    return jnp.pad(flat, (0, rows * LANES - flat.shape[0])).reshape(rows, LANES)


def _shard(full, axis):
    s = full.shape
    return jnp.moveaxis(full.reshape(s[:axis] + (N_DEV, s[axis] // N_DEV) + s[axis + 1:]), axis, 0)


_UNITS = (("e_w_in", 0, True), ("e_w_out", 0, False), ("o_w_in", 0, False), ("o_w_out", 0, True),
          *[(n, i, False) for n in ("ca_wq", "ca_wk", "ca_wv", "ca_wo") for i in (0, 1)],
          *[(n, i, tr) for n, tr in (("ffn_w_gate", True), ("ffn_w_up", True), ("ffn_w_down", False)) for i in (0, 1)])
_LAYERED = ("ca_wq", "ca_wk", "ca_wv", "ca_wo", "ffn_w_gate", "ffn_w_up", "ffn_w_down")
_SMALL_SHARDED = (("e_conv_w", 2), ("o_norm", 1), ("o_d", 1))
RS_ROW = 1024


def _unit_key(name, tr):
    return name + "_t" if tr else name


def _stage_of(name, layer):
    if name.startswith("e_"):
        return 0
    if name.startswith("o_"):
        return 2
    if name.startswith("ca_"):
        return 1 if layer == 0 else 3
    return 2 if layer == 0 else 3


def weight_fetcher(local):
    groups, meta = [[] for _ in range(4)], [[] for _ in range(4)]
    for name, layer, tr in _UNITS:
        blk = local[name][layer]
        st = _stage_of(name, layer)
        groups[st].append(_bf(blk.T if tr else blk))
        meta[st].append((name, layer, tr))
    small = jnp.concatenate([local[name].reshape(-1) for name, _ in _SMALL_SHARDED])
    groups[0].append(_rows128(small))
    sems, srcs, lands, token = gather_ici_start("ag_w_start", groups)

    def fetch(stage, after):
        if stage == 0:
            after = token
        landed = gather_ici_wait(f"ag_w_wait{stage}", srcs[stage], lands[stage], sems[stage], after)
        bufs = gather_d2d(f"ag_w_d2d{stage}", landed)
        got = {}
        for (name, layer, tr), blk, buf in zip(meta[stage], groups[stage], bufs):
            arr = buf.reshape((N_DEV * blk.shape[0],) + tuple(blk.shape[1:]))
            if name in _LAYERED:
                got[(_unit_key(name, tr), layer)] = arr
            else:
                got[_unit_key(name, tr)] = arr
        if stage == 0:
            flat = bufs[-1].reshape(N_DEV, -1)
            off = 0
            for name, axis in _SMALL_SHARDED:
                blk = local[name]
                seg = flat[:, off:off + blk.size].reshape((N_DEV,) + blk.shape)
                off += blk.size
                seg = jnp.moveaxis(seg, 0, axis)
                got[name] = seg.reshape(seg.shape[:axis] + (-1,) + seg.shape[axis + 2:])
            got["e_conv_w"] = got["e_conv_w"][0]
        return got

    return fetch


def _grad_stage_of(name, layer):
    if name.startswith("e_"):
        return 4
    if name.startswith("o_"):
        return 1
    if name.startswith("ca_"):
        return 3 if layer == 0 else 1
    return 2 if layer == 0 else 0


GRAD_STAGES = 5
SMALL_ROWS = 16


def gradient_reducer(local, mom, var):
    core = lax.axis_index("c").astype(jnp.int32).reshape(1)
    chip = (2 * lax.axis_index("x") + lax.axis_index("y")).astype(jnp.int32).reshape(1)
    pending = []

    def start(stage, grads, carry):
        units = [u for u in _UNITS if _grad_stage_of(u[0], u[1]) == stage]
        parts, spans = [], []
        for name, layer, tr in units:
            key = _unit_key(name, tr)
            g = grads[(key, layer)] if name in _LAYERED else grads[key]
            part = g.reshape(4, 2, -1, RS_ROW)
            spans.append((part.shape[2], g.shape[0] // N_DEV, g.shape[1]))
            parts.append(part)
        if stage == GRAD_STAGES - 1:
            small = jnp.concatenate([_shard(grads[name], axis).reshape(N_DEV, -1) for name, axis in _SMALL_SHARDED], axis=1)
            small = jnp.pad(small, ((0, 0), (0, SMALL_ROWS * RS_ROW - small.shape[1])))
            parts.append(small.astype(BF16).reshape(4, 2, SMALL_ROWS, RS_ROW))
        pack = jnp.concatenate(parts, axis=2)
        from_sibling = scatter_d2d(f"rs_d2d{stage}", pack)
        chip_sum = sum_pair(f"rs_pair{stage}", pack, from_sibling, core)
        sems, own, land, carry = scatter_ici_start(f"rs_start{stage}", chip_sum, carry)
        pending.append((stage, units, spans, sems, own, land))
        return carry

    def finish(after):
        res, per_layer, small_flat = {}, {}, None
        for stage, units, spans, sems, own, land in pending:
            own, land = scatter_ici_wait(f"rs_wait{stage}", own, land, sems, after)
            total = sum_quad(f"rs_quad{stage}", own, land, chip)
            off = 0
            for (name, layer, tr), (rows, r, c) in zip(units, spans):
                g = total[off:off + rows].reshape(r, c)
                off += rows
                per_layer.setdefault(name, {})[layer] = g.T if tr else g
            if stage == GRAD_STAGES - 1:
                small_flat = total[off:off + SMALL_ROWS].reshape(-1)
        for name, by_layer in per_layer.items():
            g = jnp.stack([by_layer[i] for i in sorted(by_layer)]) if name in _LAYERED else by_layer[0][None]
            res[name] = (g,) + adamw_native("adamw_" + name, g, local[name], mom[name], var[name])
        off = 0
        for name, _ in _SMALL_SHARDED:
            blk = local[name]
            g = small_flat[off:off + blk.size].reshape(blk.shape)
            off += blk.size
            res[name] = (g,) + adamw_native("adamw_" + name, g, blk, mom[name], var[name])
        return res

    return start, finish


def reduce_replicated(grads, loss, w, mom, var):
    def pack(src, last):
        return jnp.concatenate([_rows128(src[name]) for name in _REPLICATED] + [_rows128(last)], axis=0)

    (slots,) = all_gather("ag_g", [pack(grads, loss)])
    zero = jnp.zeros((1, 1), F32)
    rows = slots.shape[0] // N_DEV
    outs = adamw_call("adamw_replicated", slots.reshape(N_DEV, rows, LANES), pack(w, zero), pack(mom, zero), pack(var, zero),
                      tr=rows)
    res, off = {}, 0
    for name in _REPLICATED:
        n = w[name].size
        nr = _rows128(w[name]).shape[0]
        res[name] = tuple(o[off:off + nr].reshape(-1)[:n].reshape(w[name].shape) for o in outs)
        off += nr
    return res, outs[0][off, 0]


def kernel(x, mem, e_norm, e_w_in, e_gmlp_w, e_gmlp_b, e_conv_w, e_conv_b, e_conv_ln_g, e_conv_ln_b, e_w_out, o_norm, o_w_in, o_lam_re, o_lam_im, o_log_dt, o_b_re, o_b_im, o_c_re, o_c_im, o_d, o_w_out, ca_norm, ca_mem_norm, ca_wq, ca_wk, ca_wv, ca_wo, ffn_norm, ffn_w_gate, ffn_w_up, ffn_w_down, final_norm, loss_target, m_e_norm, m_e_w_in, m_e_gmlp_w, m_e_gmlp_b, m_e_conv_w, m_e_conv_b, m_e_conv_ln_g, m_e_conv_ln_b, m_e_w_out, m_o_norm, m_o_w_in, m_o_lam_re, m_o_lam_im, m_o_log_dt, m_o_b_re, m_o_b_im, m_o_c_re, m_o_c_im, m_o_d, m_o_w_out, m_ca_norm, m_ca_mem_norm, m_ca_wq, m_ca_wk, m_ca_wv, m_ca_wo, m_ffn_norm, m_ffn_w_gate, m_ffn_w_up, m_ffn_w_down, m_final_norm, v_e_norm, v_e_w_in, v_e_gmlp_w, v_e_gmlp_b, v_e_conv_w, v_e_conv_b, v_e_conv_ln_g, v_e_conv_ln_b, v_e_w_out, v_o_norm, v_o_w_in, v_o_lam_re, v_o_lam_im, v_o_log_dt, v_o_b_re, v_o_b_im, v_o_c_re, v_o_c_im, v_o_d, v_o_w_out, v_ca_norm, v_ca_mem_norm, v_ca_wq, v_ca_wk, v_ca_wv, v_ca_wo, v_ffn_norm, v_ffn_w_gate, v_ffn_w_up, v_ffn_w_down, v_final_norm):
    given = dict(locals())
    local = {k: given[k] for k in _ORDER}
    mom = {k: given["m_" + k] for k in _ORDER}
    var = {k: given["v_" + k] for k in _ORDER}

    w = {}
    w.update({
        "e_norm": e_norm, "e_gmlp_w": e_gmlp_w[0], "e_gmlp_b": e_gmlp_b.reshape(A_GROUPS, GMLP_BLOCK, 1),
        "e_conv_b": e_conv_b, "e_conv_ln_g": e_conv_ln_g, "e_conv_ln_b": e_conv_ln_b,
        "o_lam_re": o_lam_re[0], "o_lam_im": o_lam_im[0], "o_log_dt": o_log_dt[0], "o_b_re": o_b_re[0], "o_b_im": o_b_im[0],
        "o_c_re": o_c_re[0], "o_c_im": o_c_im[0], "ca_norm": ca_norm, "ca_mem_norm": ca_mem_norm, "ffn_norm": ffn_norm,
        "final_norm": final_norm.reshape(1, D_MODEL),
    })
    start_reduce, finish_reduce = gradient_reducer(local, mom, var)
    loss_part, grad_x, grads = local_step(x[0], mem[0], loss_target[0], w, weight_fetcher(local), start_reduce)
    grads["final_norm"] = grads["final_norm"].reshape(D_MODEL)

    res = finish_reduce(grad_x)
    rep, loss = reduce_replicated(grads, loss_part, local, mom, var)
    res.update(rep)
    return (loss, grad_x[None], *[res[k][0] for k in _ORDER], *[res[k][1] for k in _ORDER],
            *[res[k][2] for k in _ORDER], *[res[k][3] for k in _ORDER])
```

```python
import jax
import jax.numpy as jnp
from jax import lax
from jax.experimental import pallas as pl
from jax.experimental.pallas import tpu as pltpu

F32 = jnp.float32
BF16 = jnp.bfloat16
S = jax.ShapeDtypeStruct

D_MODEL = 1024
A_WIDTH = 512
A_GROUPS = 4
GMLP_BLOCK = 128
CHUNK = 64
B_WIDTH = 512
IN_WIDTH = 2 * A_WIDTH + 2 * B_WIDTH
CONV_WIDTH = 31
CONV_PAD = 32
C_WIDTH = 512
C_GROUP_CH = 16
C_GROUPS = 32
C_STATE = 64
N_STATE = C_GROUPS * C_STATE
CA_HEADS = 4
CA_HEAD_DIM = 256
FFN_HIDDEN = 2816
EPS = 1e-6
ADAM_LR = 0.001
ADAM_B1 = 0.9
ADAM_B2 = 0.999
ADAM_EPS = 1e-08
ADAM_WD = 0.01
ADAM_STEP = 10
N_DEV = 8
LANES = 128
VMEM_LIMIT = 56 << 20
VMEM_BUDGET = 40 << 20
MM_TN_RESIDENT = 8 << 20
MESH = pl.DeviceIdType.MESH
ANY = pl.BlockSpec(memory_space=pl.ANY)


def _cp(*sem):
    return pltpu.CompilerParams(dimension_semantics=sem, vmem_limit_bytes=VMEM_LIMIT)


def _tile(n, pref):
    t = pref
    while n % t:
        t //= 2
    return t


def _bf(v):
    return v if v.dtype == BF16 else v.astype(BF16)


def _sigmoid(x):
    return 1.0 / (1.0 + jnp.exp(-x))


_GC = 0.7978845608028654


def _gelu(x):
    return 0.5 * x * (1.0 + jnp.tanh(_GC * (x + 0.044715 * x * x * x)))


def _gelu_grad(x):
    x2 = x * x
    t = jnp.tanh(_GC * (x + 0.044715 * x * x2))
    return 0.5 * (1.0 + t) + 0.5 * x * (1.0 - t * t) * _GC * (1.0 + 3.0 * 0.044715 * x2)


def _tspec(entry, tm):
    if isinstance(entry, tuple):
        arr, cb, width = entry
        return arr, pl.BlockSpec((tm, width), lambda i, cb=cb: (i, cb))
    return entry, pl.BlockSpec((tm, entry.shape[1]), lambda i: (i, 0))


def rows_call(name, fn, tiled, full, outs, accs, tm=256):
    pairs = [_tspec(e, tm) for e in tiled]
    arrs = [p[0] for p in pairs]
    rows = arrs[0].shape[0]
    tm = _tile(rows, tm)
    pairs = [_tspec(e, tm) for e in tiled]
    n_in = len(tiled) + len(full)
    n_out = len(outs)

    def body(*refs):
        vals = [r[...] for r in refs[:n_in]]
        o_refs = refs[n_in:n_in + n_out]
        a_refs = refs[n_in + n_out:]
        ov, av = fn(*vals)
        for r, v in zip(o_refs, ov):
            r[...] = v.astype(r.dtype)
        if a_refs:
            @pl.when(pl.program_id(0) == 0)
            def _():
                for r in a_refs:
                    r[...] = jnp.zeros(r.shape, r.dtype)
            for r, v in zip(a_refs, av):
                r[...] += v

    in_specs = [p[1] for p in pairs] + [pl.BlockSpec(a.shape, lambda i, nd=a.ndim: (0,) * nd) for a in full]
    out_specs = [pl.BlockSpec((tm, c), lambda i: (i, 0)) for c, _ in outs]
    out_specs += [pl.BlockSpec(s, lambda i, nd=len(s): (0,) * nd) for s in accs]
    out_shape = [S((rows, c), dt) for c, dt in outs] + [S(s, F32) for s in accs]
    return pl.pallas_call(body, grid=(rows // tm,), in_specs=in_specs, out_specs=out_specs, out_shape=out_shape,
                          compiler_params=_cp("arbitrary"), name=name)(*arrs, *full)


def mm_nn(name, m, n, pairs, n_acc, epi, outs, tiled=(), cols=(), rowv=(), sums=()):
    a_ops, a_slot, b_arrs, b_specs, idx, trans = [], [], [], [], [], []
    fixed = 0
    for pair in pairs:
        a, b, k = pair[:3]
        bt = len(pair) > 3
        arr, cb, kdim = a if isinstance(a, tuple) else (a, 0, a.shape[1])
        key = (id(arr), cb, kdim)
        if key not in [o[0] for o in a_ops]:
            a_ops.append((key, arr, cb, kdim))
        a_slot.append([o[0] for o in a_ops].index(key))
        b_arr, off = b if isinstance(b, tuple) else (b, 0)
        b_arrs.append(b_arr)
        if bt:
            assert off % n == 0 and b_arr.shape[1] == kdim
            b_specs.append(pl.BlockSpec((n, kdim), lambda i, o=off // n: (o, 0), pipeline_mode=pl.Buffered(1)))
        else:
            assert b_arr.shape[1] == n
            b_specs.append(pl.BlockSpec((kdim, n), lambda i, o=off: (o, 0), pipeline_mode=pl.Buffered(1)))
        fixed += kdim * n * b_arr.dtype.itemsize
        idx.append(k)
        trans.append(bt)
    per_row = sum(2 * kdim * arr.dtype.itemsize for _, arr, _, kdim in a_ops)
    per_row += sum(2 * n * t.dtype.itemsize for t in tiled) + sum(2 * n * jnp.dtype(dt).itemsize for dt in outs)
    per_row += (n_acc + 3) * n * 4
    tm = next((t for t in (1024, 512, 256, 128) if m % t == 0 and fixed + t * per_row <= VMEM_BUDGET), _tile(m, 128))
    n_a, n_p = len(a_ops), len(pairs)
    n_in = n_a + n_p + len(tiled) + len(cols) + len(rowv)

    def body(*refs):
        a_vals = [_bf(r[...]) for r in refs[:n_a]]
        accs = [None] * n_acc
        for p in range(n_p):
            av, bv = a_vals[a_slot[p]], _bf(refs[n_a + p][...])
            if trans[p]:
                d = lax.dot_general(av, bv, (((1,), (1,)), ((), ())), preferred_element_type=F32)
            else:
                d = jnp.dot(av, bv, preferred_element_type=F32)
            accs[idx[p]] = d if accs[idx[p]] is None else accs[idx[p]] + d
        extra = [r[...] for r in refs[n_a + n_p:n_in]]
        ov = epi(accs, *extra)
        sv = ov[len(outs):]
        for r, v in zip(refs[n_in:n_in + len(outs)], ov):
            r[...] = v.astype(r.dtype)
        if sums:
            s_refs = refs[n_in + len(outs):]

            @pl.when(pl.program_id(0) == 0)
            def _():
                for r in s_refs:
                    r[...] = jnp.zeros(r.shape, r.dtype)
            for r, v in zip(s_refs, sv):
                r[...] += v

    in_specs = [pl.BlockSpec((tm, kdim), lambda i, cb=cb: (i, cb)) for _, _, cb, kdim in a_ops] + b_specs
    in_specs += [pl.BlockSpec((tm, n), lambda i: (i, 0)) for _ in tiled]
    in_specs += [pl.BlockSpec((tm, 1), lambda i: (i, 0)) for _ in cols]
    in_specs += [pl.BlockSpec((1, n), lambda i: (0, 0)) for _ in rowv]
    out_specs = [pl.BlockSpec((tm, n), lambda i: (i, 0)) for _ in outs]
    out_specs += [pl.BlockSpec(s, lambda i, nd=len(s): (0,) * nd) for s in sums]
    out_shape = [S((m, n), dt) for dt in outs] + [S(s, F32) for s in sums]
    return pl.pallas_call(body, grid=(m // tm,), in_specs=in_specs, out_specs=out_specs, out_shape=out_shape,
                          compiler_params=_cp("arbitrary" if sums else "parallel"),
                          name=name)(*[o[1] for o in a_ops], *b_arrs, *tiled, *cols, *rowv)


def mm_tn(name, a, b, out_dtype=BF16):
    if isinstance(a, tuple):
        a_arr, a_cb, m = a
    else:
        a_arr, a_cb, m = a, None, a.shape[1]
    if isinstance(b, tuple):
        b_arr, b_cb, n = b
    else:
        b_arr, b_cb, n = b, None, b.shape[1]
    t = a_arr.shape[0]
    whole_b = t * n * b_arr.dtype.itemsize <= MM_TN_RESIDENT and b_cb is None
    tn = n if whole_b else _tile(n, 512)
    tm = _tile(m, 512 if t * 512 * a_arr.dtype.itemsize * 2 + t * tn * b_arr.dtype.itemsize * 2 <= VMEM_BUDGET else 256)
    a_off = 0 if a_cb is None else a_cb * (m // tm)
    b_off = 0 if b_cb is None else b_cb * (n // tn)

    def body(a_ref, b_ref, o_ref):
        o_ref[...] = lax.dot_general(_bf(a_ref[...]), _bf(b_ref[...]), (((0,), (0,)), ((), ())),
                                     preferred_element_type=F32).astype(o_ref.dtype)

    if whole_b:
        b_spec = pl.BlockSpec((t, n), lambda i, j: (0, 0), pipeline_mode=pl.Buffered(1))
    else:
        b_spec = pl.BlockSpec((t, tn), lambda i, j: (0, j + b_off))
    return pl.pallas_call(
        body, grid=(m // tm, n // tn),
        in_specs=[pl.BlockSpec((t, tm), lambda i, j: (0, i + a_off)), b_spec],
        out_specs=pl.BlockSpec((tm, tn), lambda i, j: (i, j)), out_shape=S((m, n), out_dtype),
        compiler_params=_cp("parallel", "parallel"), name=name)(a_arr, b_arr)


def rms_fwd(name, x, gain):
    def fn(xv, g):
        r = lax.rsqrt(jnp.mean(xv * xv, axis=-1, keepdims=True) + EPS)
        return [xv * r * g, r], []
    return rows_call(name, fn, [x], [gain], [(x.shape[1], BF16), (1, F32)], [])


def rms_bwd_gain_only(name, dxn, x, r):
    def fn(dv, xv, rv):
        return [], [jnp.sum(dv * xv * rv, axis=0, keepdims=True)]
    return rows_call(name, fn, [dxn, x, r], [], [], [(1, x.shape[1])])[0]


def final_loss(name, x, gain, target):
    d = x.shape[1]

    def fn(xv, tv, g):
        r = lax.rsqrt(jnp.mean(xv * xv, axis=-1, keepdims=True) + EPS)
        xh = xv * r
        err = xh * g - tv
        dy = err * (1.0 / d)
        w = dy * g
        dx = r * (w - xh * jnp.mean(w * xh, axis=-1, keepdims=True))
        part = jnp.sum(jnp.sum(err * err, axis=-1, keepdims=True), axis=0, keepdims=True) * (0.5 / d)
        return [dx, dx], [jnp.sum(dy * xh, axis=0, keepdims=True), part]

    return rows_call(name, fn, [x, target], [gain], [(d, F32), (d, BF16)], [(1, d), (1, 1)])


def _gmlp_mask():
    row = lax.broadcasted_iota(jnp.int32, (GMLP_BLOCK, GMLP_BLOCK), 0) // CHUNK
    col = lax.broadcasted_iota(jnp.int32, (GMLP_BLOCK, GMLP_BLOCK), 1) // CHUNK
    return col <= row


def _ln_plain(v):
    mu = jnp.mean(v, axis=-1, keepdims=True)
    vc = v - mu
    rstd = lax.rsqrt(jnp.mean(vc * vc, axis=-1, keepdims=True) + EPS)
    return vc * rstd, rstd


def gmlp_fwd(name, proj, w, b, tm=512):
    t = proj.shape[0]
    tm = _tile(t, tm)

    def body(au_ref, av_ref, w_ref, b_ref, o_ref):
        mask = _gmlp_mask()
        u = _gelu(au_ref[...])
        vn, _ = _ln_plain(_gelu(av_ref[...]))
        vnb = _bf(vn)
        for g in range(A_GROUPS):
            wg = _bf(jnp.where(mask, w_ref[g], 0.0))
            cs = slice(g * GMLP_BLOCK, (g + 1) * GMLP_BLOCK)
            for n in range(tm // GMLP_BLOCK):
                rs = slice(n * GMLP_BLOCK, (n + 1) * GMLP_BLOCK)
                sg = jnp.dot(wg, vnb[rs, cs], preferred_element_type=F32) + b_ref[g]
                o_ref[rs, cs] = (u[rs, cs] * sg).astype(o_ref.dtype)

    return pl.pallas_call(
        body, grid=(t // tm,),
        in_specs=[pl.BlockSpec((tm, A_WIDTH), lambda i: (i, 0)), pl.BlockSpec((tm, A_WIDTH), lambda i: (i, 1)),
                  pl.BlockSpec(w.shape, lambda i: (0, 0, 0)), pl.BlockSpec(b.shape, lambda i: (0, 0, 0))],
        out_specs=pl.BlockSpec((tm, A_WIDTH), lambda i: (i, 0)), out_shape=S((t, A_WIDTH), BF16),
        compiler_params=_cp("parallel"), name=name)(proj, proj, w, b)


def gmlp_bwd(name, proj, dcat, w, b, tm=512):
    t = proj.shape[0]
    tm = _tile(t, tm)

    def body(au_ref, av_ref, do_ref, w_ref, b_ref, dp_ref, dw_ref, db_ref):
        @pl.when(pl.program_id(0) == 0)
        def _():
            dw_ref[...] = jnp.zeros(dw_ref.shape, F32)
            db_ref[...] = jnp.zeros(db_ref.shape, F32)

        mask = _gmlp_mask()
        au = au_ref[...]
        av = av_ref[...]
        u = _gelu(au)
        vn, rstd = _ln_plain(_gelu(av))
        vnb = _bf(vn)
        dout = do_ref[...]
        dvn_cols = []
        for g in range(A_GROUPS):
            wm = jnp.where(mask, w_ref[g], 0.0)
            wg = _bf(wm)
            wgt = _bf(wm.T)
            cs = slice(g * GMLP_BLOCK, (g + 1) * GMLP_BLOCK)
            dwg = jnp.zeros((GMLP_BLOCK, GMLP_BLOCK), F32)
            dbg = jnp.zeros((GMLP_BLOCK, 1), F32)
            dvn_rows = []
            for n in range(tm // GMLP_BLOCK):
                rs = slice(n * GMLP_BLOCK, (n + 1) * GMLP_BLOCK)
                sg = jnp.dot(wg, vnb[rs, cs], preferred_element_type=F32) + b_ref[g]
                dp_ref[rs, cs] = (dout[rs, cs] * sg * _gelu_grad(au[rs, cs])).astype(dp_ref.dtype)
                dsg = dout[rs, cs] * u[rs, cs]
                dsgb = _bf(dsg)
                dbg = dbg + jnp.sum(dsg, axis=1, keepdims=True)
                dwg = dwg + lax.dot_general(dsgb, vnb[rs, cs], (((1,), (1,)), ((), ())), preferred_element_type=F32)
                dvn_rows.append(jnp.dot(wgt, dsgb, preferred_element_type=F32))
            dw_ref[g] += jnp.where(mask, dwg, 0.0)
            db_ref[g] += dbg
            dvn_cols.append(jnp.concatenate(dvn_rows, axis=0))
        dvn = jnp.concatenate(dvn_cols, axis=1)
        dv = rstd * (dvn - jnp.mean(dvn, axis=-1, keepdims=True) - vn * jnp.mean(dvn * vn, axis=-1, keepdims=True))
        dp_ref[:, A_WIDTH:] = (dv * _gelu_grad(av)).astype(dp_ref.dtype)

    return pl.pallas_call(
        body, grid=(t // tm,),
        in_specs=[pl.BlockSpec((tm, A_WIDTH), lambda i: (i, 0)), pl.BlockSpec((tm, A_WIDTH), lambda i: (i, 1)),
                  pl.BlockSpec((tm, A_WIDTH), lambda i: (i, 0)),
                  pl.BlockSpec(w.shape, lambda i: (0, 0, 0)), pl.BlockSpec(b.shape, lambda i: (0, 0, 0))],
        out_specs=[pl.BlockSpec((tm, 2 * A_WIDTH), lambda i: (i, 0)),
                   pl.BlockSpec(w.shape, lambda i: (0, 0, 0)), pl.BlockSpec(b.shape, lambda i: (0, 0, 0))],
        out_shape=[S((t, 2 * A_WIDTH), BF16), S(w.shape, F32), S(b.shape, F32)],
        compiler_params=_cp("arbitrary"), name=name)(proj, proj, dcat, w, b)


CONV_ROWS = 256


def conv_fwd(name, proj, w, cb):
    t = proj.shape[0]
    tc = LANES
    rows = _tile(t, CONV_ROWS)
    a_cb, g_cb = 2 * A_WIDTH // tc, (2 * A_WIDTH + B_WIDTH) // tc

    def body(a_ref, g_ref, w_ref, cb_ref, o_ref, hpad):
        hpad[0:CONV_PAD, :] = jnp.zeros((CONV_PAD, tc), F32)

        def fill(i, _):
            r0 = pl.multiple_of(i * rows, rows)
            hpad[pl.ds(CONV_PAD + r0, rows), :] = a_ref[pl.ds(r0, rows), :] * _sigmoid(g_ref[pl.ds(r0, rows), :])
            return 0
        lax.fori_loop(0, t // rows, fill, 0)

        def conv(i, _):
            r0 = pl.multiple_of(i * rows, rows)
            win = hpad[pl.ds(r0, rows + CONV_PAD), :]
            acc = jnp.zeros((rows, tc), F32) + cb_ref[...]
            for k in range(CONV_WIDTH):
                sh = CONV_WIDTH - 1 - k
                src = win if sh == 0 else pltpu.roll(win, sh, 0)
                acc = acc + src[CONV_PAD:, :] * w_ref[k:k + 1, :]
            o_ref[pl.ds(r0, rows), :] = acc
            return 0
        lax.fori_loop(0, t // rows, conv, 0)

    return pl.pallas_call(
        body, grid=(B_WIDTH // tc,),
        in_specs=[pl.BlockSpec((t, tc), lambda j: (0, a_cb + j)), pl.BlockSpec((t, tc), lambda j: (0, g_cb + j)),
                  pl.BlockSpec((CONV_WIDTH, tc), lambda j: (0, j)), pl.BlockSpec((1, tc), lambda j: (0, j))],
        out_specs=pl.BlockSpec((t, tc), lambda j: (0, j)), out_shape=S((t, B_WIDTH), F32),
        scratch_shapes=[pltpu.VMEM((t + CONV_PAD, tc), F32)],
        compiler_params=_cp("parallel"), name=name)(proj, proj, w, cb)


def conv_bwd(name, proj, dhc, w):
    t = proj.shape[0]
    tc = LANES
    rows = _tile(t, CONV_ROWS)
    a_cb, g_cb = 2 * A_WIDTH // tc, (2 * A_WIDTH + B_WIDTH) // tc
    win_rows = rows + CONV_PAD

    def body(a_ref, g_ref, d_ref, w_ref, da_ref, dg_ref, dw_ref, dcb_ref, hpad, dpad, dwacc):
        hpad[0:CONV_PAD, :] = jnp.zeros((CONV_PAD, tc), F32)
        dpad[t:t + CONV_PAD, :] = jnp.zeros((CONV_PAD, tc), F32)
        dwacc[...] = jnp.zeros(dwacc.shape, F32)

        def fill(i, _):
            r0 = pl.multiple_of(i * rows, rows)
            hpad[pl.ds(CONV_PAD + r0, rows), :] = a_ref[pl.ds(r0, rows), :] * _sigmoid(g_ref[pl.ds(r0, rows), :])
            dpad[pl.ds(r0, rows), :] = d_ref[pl.ds(r0, rows), :]
            return 0
        lax.fori_loop(0, t // rows, fill, 0)

        def step(i, dcb):
            r0 = pl.multiple_of(i * rows, rows)
            hwin = hpad[pl.ds(r0, win_rows), :]
            dwin = dpad[pl.ds(r0, win_rows), :]
            dchunk = dwin[:rows, :]
            dh = jnp.zeros((rows, tc), F32)
            for k in range(CONV_WIDTH):
                sh = CONV_WIDTH - 1 - k
                hsrc = hwin if sh == 0 else pltpu.roll(hwin, sh, 0)
                dsrc = dwin if sh == 0 else pltpu.roll(dwin, win_rows - sh, 0)
                dh = dh + dsrc[:rows, :] * w_ref[k:k + 1, :]
                prod = dchunk * hsrc[CONV_PAD:, :]
                dwacc[k] += jnp.sum(prod.reshape(rows // 8, 8, tc), axis=0)
            a = a_ref[pl.ds(r0, rows), :]
            sg = _sigmoid(g_ref[pl.ds(r0, rows), :])
            da_ref[pl.ds(r0, rows), :] = (dh * sg).astype(da_ref.dtype)
            dg_ref[pl.ds(r0, rows), :] = (dh * a * sg * (1.0 - sg)).astype(dg_ref.dtype)
            return dcb + jnp.sum(dchunk, axis=0, keepdims=True)
        dcb = lax.fori_loop(0, t // rows, step, jnp.zeros((1, tc), F32))
        dcb_ref[...] = dcb
        for k in range(CONV_WIDTH):
            dw_ref[k:k + 1, :] = jnp.sum(dwacc[k], axis=0, keepdims=True)

    return pl.pallas_call(
        body, grid=(B_WIDTH // tc,),
        in_specs=[pl.BlockSpec((t, tc), lambda j: (0, a_cb + j)), pl.BlockSpec((t, tc), lambda j: (0, g_cb + j)),
                  pl.BlockSpec((t, tc), lambda j: (0, j)), pl.BlockSpec((CONV_WIDTH, tc), lambda j: (0, j))],
        out_specs=[pl.BlockSpec((t, tc), lambda j: (0, j)), pl.BlockSpec((t, tc), lambda j: (0, j)),
                   pl.BlockSpec((CONV_WIDTH, tc), lambda j: (0, j)), pl.BlockSpec((1, tc), lambda j: (0, j))],
        out_shape=[S((t, B_WIDTH), BF16), S((t, B_WIDTH), BF16), S((CONV_WIDTH, B_WIDTH), F32), S((1, B_WIDTH), F32)],
        scratch_shapes=[pltpu.VMEM((t + CONV_PAD, tc), F32), pltpu.VMEM((t + CONV_PAD, tc), F32),
                        pltpu.VMEM((CONV_WIDTH, 8, tc), F32)],
        compiler_params=_cp("parallel"), name=name)(proj, proj, dhc, w)


def ln_silu_fwd(name, hc, g, b):
    def fn(h, gv, bv):
        y, _ = _ln_plain(h)
        z = y * gv + bv
        return [z * _sigmoid(z)], []
    return rows_call(name, fn, [hc], [g, b], [(hc.shape[1], BF16)], [])[0]


def ln_silu_bwd(name, hc, dcat, g, b):
    c = hc.shape[1]

    def fn(h, dout, gv, bv):
        y, rstd = _ln_plain(h)
        z = y * gv + bv
        s = _sigmoid(z)
        dz = dout * s * (1.0 + z * (1.0 - s))
        dyv = dz * gv
        dh = rstd * (dyv - jnp.mean(dyv, axis=-1, keepdims=True) - y * jnp.mean(dyv * y, axis=-1, keepdims=True))
        return [dh], [jnp.sum(dz * y, axis=0, keepdims=True), jnp.sum(dz, axis=0, keepdims=True)]

    return rows_call(name, fn, [hc, (dcat, 1, c)], [g, b], [(c, F32)], [(1, c), (1, c)])


_NT = (((1,), (1,)), ((), ()))
_TN = (((0,), (0,)), ((), ()))


def attn_fwd(name, q, k, v, tm=512):
    t, d = q.shape
    m = k.shape[0]
    tm = _tile(t, tm)
    scale = CA_HEAD_DIM ** -0.5

    def body(q_ref, k_ref, v_ref, o_ref):
        for h in range(CA_HEADS):
            cs = slice(h * CA_HEAD_DIM, (h + 1) * CA_HEAD_DIM)
            s = lax.dot_general(q_ref[:, cs], k_ref[:, cs], _NT, preferred_element_type=F32) * scale
            e = jnp.exp(s - jnp.max(s, axis=-1, keepdims=True))
            p = e / jnp.sum(e, axis=-1, keepdims=True)
            o_ref[:, cs] = jnp.dot(_bf(p), v_ref[:, cs], preferred_element_type=F32).astype(o_ref.dtype)

    return pl.pallas_call(
        body, grid=(t // tm,),
        in_specs=[pl.BlockSpec((tm, d), lambda i: (i, 0)), pl.BlockSpec((m, d), lambda i: (0, 0)),
                  pl.BlockSpec((m, d), lambda i: (0, 0))],
        out_specs=pl.BlockSpec((tm, d), lambda i: (i, 0)), out_shape=S((t, d), BF16),
        compiler_params=_cp("parallel"), name=name)(q, k, v)


def attn_bwd(name, q, k, v, do, tm=512):
    t, d = q.shape
    m = k.shape[0]
    tm = _tile(t, tm)
    scale = CA_HEAD_DIM ** -0.5

    def body(q_ref, k_ref, v_ref, do_ref, dq_ref, dk_ref, dv_ref):
        @pl.when(pl.program_id(0) == 0)
        def _():
            dk_ref[...] = jnp.zeros(dk_ref.shape, F32)
            dv_ref[...] = jnp.zeros(dv_ref.shape, F32)

        for h in range(CA_HEADS):
            cs = slice(h * CA_HEAD_DIM, (h + 1) * CA_HEAD_DIM)
            qh, kh, vh, doh = q_ref[:, cs], k_ref[:, cs], v_ref[:, cs], do_ref[:, cs]
            s = lax.dot_general(qh, kh, _NT, preferred_element_type=F32) * scale
            e = jnp.exp(s - jnp.max(s, axis=-1, keepdims=True))
            p = e / jnp.sum(e, axis=-1, keepdims=True)
            pb = _bf(p)
            dv_ref[:, cs] += lax.dot_general(pb, doh, _TN, preferred_element_type=F32)
            dp = lax.dot_general(doh, vh, _NT, preferred_element_type=F32)
            ds = _bf(p * (dp - jnp.sum(dp * p, axis=-1, keepdims=True)) * scale)
            dq_ref[:, cs] = jnp.dot(ds, kh, preferred_element_type=F32).astype(dq_ref.dtype)
            dk_ref[:, cs] += lax.dot_general(ds, qh, _TN, preferred_element_type=F32)

    return pl.pallas_call(
        body, grid=(t // tm,),
        in_specs=[pl.BlockSpec((tm, d), lambda i: (i, 0)), pl.BlockSpec((m, d), lambda i: (0, 0)),
                  pl.BlockSpec((m, d), lambda i: (0, 0)), pl.BlockSpec((tm, d), lambda i: (i, 0))],
        out_specs=[pl.BlockSpec((tm, d), lambda i: (i, 0)), pl.BlockSpec((m, d), lambda i: (0, 0)),
                   pl.BlockSpec((m, d), lambda i: (0, 0))],
        out_shape=[S((t, d), BF16), S((m, d), F32), S((m, d), F32)],
        compiler_params=_cp("arbitrary"), name=name)(q, k, v, do)


SUB = 8
S5_ROWS = 256


def s5_constants(lam_re, lam_im, log_dt, b_re, b_im, c_re, c_im):
    dt = jnp.exp(log_dt)[:, None]
    mag = jnp.exp(lam_re * dt)
    ar = mag * jnp.cos(lam_im * dt)
    ai = mag * jnp.sin(lam_im * dt)
    den = lam_re * lam_re + lam_im * lam_im
    qr = ((ar - 1.0) * lam_re + ai * lam_im) / den
    qi = (ai * lam_re - (ar - 1.0) * lam_im) / den
    bbr = qr[..., None] * b_re - qi[..., None] * b_im
    bbi = qr[..., None] * b_im + qi[..., None] * b_re
    eye = jnp.eye(C_GROUPS, dtype=F32)

    def in_mat(bb):
        return (bb.transpose(0, 2, 1)[:, :, None, :] * eye[:, None, :, None]).reshape(C_WIDTH, N_STATE)

    def out_mat(cc):
        return (cc.transpose(0, 2, 1)[:, :, None, :] * eye[:, None, :, None]).reshape(N_STATE, C_WIDTH)

    mb = jnp.concatenate([in_mat(bbr), in_mat(bbi)], axis=1)
    mc = jnp.concatenate([out_mat(c_re), -out_mat(c_im)], axis=0)
    a = jnp.stack([ar.reshape(N_STATE), ai.reshape(N_STATE)])
    return a, mb, mc


def _scan_powers(a, conj):
    ar, ai = a[0], (-a[1] if conj else a[1])
    pows = [(ar, ai)]
    for _ in range(SUB - 1):
        pr, pi = pows[-1]
        pows.append((pr * ar - pi * ai, pr * ai + pi * ar))
    rows = jnp.arange(SUB)[:, None]
    out = []
    for s in (1, 2, 4):
        keep = (rows + s <= SUB - 1) if conj else (rows >= s)
        out.append(jnp.stack([jnp.where(keep, pows[s - 1][0][None, :], 0.0), jnp.where(keep, pows[s - 1][1][None, :], 0.0)]))
    order = [SUB - 1 - i for i in range(SUB)] if conj else list(range(SUB))
    out.append(jnp.stack([jnp.stack([pows[i][0] for i in order]), jnp.stack([pows[i][1] for i in order])]))
    return jnp.stack(out)


def _cmul_add(xr, xi, pr, pi, zr, zi):
    return xr + pr * zr - pi * zi, xi + pr * zi + pi * zr


def s5_fwd(name, u, mb, mc, pw, dskip):
    t = u.shape[0]
    tm = _tile(t, S5_ROWS)
    ns = N_STATE

    def body(u_ref, mb_ref, mc_ref, pw_ref, d_ref, gy_ref, y_ref, xs_ref, xb_ref, carry):
        @pl.when(pl.program_id(0) == 0)
        def _():
            carry[...] = jnp.zeros(carry.shape, F32)

        uv = u_ref[...]
        xs_ref[...] = jnp.dot(_bf(uv), mb_ref[...], preferred_element_type=F32)

        def group(i, _):
            r0 = pl.multiple_of(i * SUB, SUB)
            xr = xs_ref[pl.ds(r0, SUB), 0:ns]
            xi = xs_ref[pl.ds(r0, SUB), ns:2 * ns]
            for k, s in enumerate((1, 2, 4)):
                xr, xi = _cmul_add(xr, xi, pw_ref[k, 0], pw_ref[k, 1], pltpu.roll(xr, s, 0), pltpu.roll(xi, s, 0))
            xr, xi = _cmul_add(xr, xi, pw_ref[3, 0], pw_ref[3, 1], carry[0], carry[1])
            xs_ref[pl.ds(r0, SUB), 0:ns] = xr
            xs_ref[pl.ds(r0, SUB), ns:2 * ns] = xi
            carry[0] = jnp.broadcast_to(xr[SUB - 1:SUB, :], (SUB, ns))
            carry[1] = jnp.broadcast_to(xi[SUB - 1:SUB, :], (SUB, ns))
            return 0
        lax.fori_loop(0, tm // SUB, group, 0, unroll=2)

        xb = _bf(xs_ref[...])
        xb_ref[...] = xb
        y = jnp.dot(xb, mc_ref[...], preferred_element_type=F32) + d_ref[...] * uv
        y_ref[...] = y
        gy_ref[...] = _gelu(y).astype(gy_ref.dtype)

    c = u.shape[1]
    return pl.pallas_call(
        body, grid=(t // tm,),
        in_specs=[pl.BlockSpec((tm, c), lambda i: (i, 0)), pl.BlockSpec(mb.shape, lambda i: (0, 0)),
                  pl.BlockSpec(mc.shape, lambda i: (0, 0)), pl.BlockSpec(pw.shape, lambda i: (0, 0, 0, 0)),
                  pl.BlockSpec((1, c), lambda i: (0, 0))],
        out_specs=[pl.BlockSpec((tm, c), lambda i: (i, 0)), pl.BlockSpec((tm, c), lambda i: (i, 0)),
                   pl.BlockSpec((tm, 2 * ns), lambda i: (i, 0)), pl.BlockSpec((tm, 2 * ns), lambda i: (i, 0))],
        out_shape=[S((t, c), BF16), S((t, c), F32), S((t, 2 * ns), F32), S((t, 2 * ns), BF16)],
        scratch_shapes=[pltpu.VMEM((2, SUB, ns), F32)],
        compiler_params=_cp("arbitrary"), name=name)(u, mb, mc, pw, dskip)


def s5_bwd(name, dgy, y, u, xs, mct, mbt, qw, dskip):
    t, c = u.shape
    tm = _tile(t, S5_ROWS)
    nt = t // tm
    ns = N_STATE
    ng = tm // SUB

    def body(dgy_ref, y_ref, u_ref, xs_ref, prev_ref, mct_ref, mbt_ref, qw_ref, d_ref,
             du_ref, dy_ref, lb_ref, da_ref, dd_ref, lam, carry):
        step = pl.program_id(0)

        @pl.when(step == 0)
        def _():
            carry[...] = jnp.zeros(carry.shape, F32)
            da_ref[...] = jnp.zeros(da_ref.shape, F32)
            dd_ref[...] = jnp.zeros(dd_ref.shape, F32)

        uv = u_ref[...]
        dy = dgy_ref[...] * _gelu_grad(y_ref[...])
        dyb = _bf(dy)
        dy_ref[...] = dyb
        dd_ref[...] += jnp.sum(dy * uv, axis=0, keepdims=True)
        lam[...] = jnp.dot(dyb, mct_ref[...], preferred_element_type=F32)
        first_tile = (step == nt - 1).astype(F32)
        row0 = lax.broadcasted_iota(jnp.int32, (SUB, ns), 0) == 0

        def group(j, _):
            i = ng - 1 - j
            r0 = pl.multiple_of(i * SUB, SUB)
            lr = lam[pl.ds(r0, SUB), 0:ns]
            li = lam[pl.ds(r0, SUB), ns:2 * ns]
            for k, s in enumerate((1, 2, 4)):
                lr, li = _cmul_add(lr, li, qw_ref[k, 0], qw_ref[k, 1],
                                   pltpu.roll(lr, SUB - s, 0), pltpu.roll(li, SUB - s, 0))
            lr, li = _cmul_add(lr, li, qw_ref[3, 0], qw_ref[3, 1], carry[0], carry[1])
            lam[pl.ds(r0, SUB), 0:ns] = lr
            lam[pl.ds(r0, SUB), ns:2 * ns] = li
            carry[0] = jnp.broadcast_to(lr[0:1, :], (SUB, ns))
            carry[1] = jnp.broadcast_to(li[0:1, :], (SUB, ns))
            rp = pl.multiple_of(jnp.maximum(i - 1, 0) * SUB, SUB)
            in_tile = (i > 0).astype(F32)
            out_tile = (1.0 - in_tile) * (1.0 - first_tile)
            pr = xs_ref[pl.ds(rp, SUB), 0:ns] * in_tile + prev_ref[:, 0:ns] * out_tile
            pi = xs_ref[pl.ds(rp, SUB), ns:2 * ns] * in_tile + prev_ref[:, ns:2 * ns] * out_tile
            xpr = jnp.where(row0, pltpu.roll(pr, 1, 0), pltpu.roll(xs_ref[pl.ds(r0, SUB), 0:ns], 1, 0))
            xpi = jnp.where(row0, pltpu.roll(pi, 1, 0), pltpu.roll(xs_ref[pl.ds(r0, SUB), ns:2 * ns], 1, 0))
            da_ref[0] += lr * xpr + li * xpi
            da_ref[1] += li * xpr - lr * xpi
            return 0
        lax.fori_loop(0, ng, group, 0, unroll=2)

        lb = _bf(lam[...])
        lb_ref[...] = lb
        du_ref[...] = (jnp.dot(lb, mbt_ref[...], preferred_element_type=F32) + d_ref[...] * dy).astype(du_ref.dtype)

    rev = lambda i: (nt - 1 - i, 0)
    prev = lambda i: (jnp.maximum((nt - 1 - i) * (tm // SUB) - 1, 0), 0)
    return pl.pallas_call(
        body, grid=(nt,),
        in_specs=[pl.BlockSpec((tm, c), rev), pl.BlockSpec((tm, c), rev), pl.BlockSpec((tm, c), rev),
                  pl.BlockSpec((tm, 2 * ns), rev), pl.BlockSpec((SUB, 2 * ns), prev),
                  pl.BlockSpec(mct.shape, lambda i: (0, 0)), pl.BlockSpec(mbt.shape, lambda i: (0, 0)),
                  pl.BlockSpec(qw.shape, lambda i: (0, 0, 0, 0)), pl.BlockSpec((1, c), lambda i: (0, 0))],
        out_specs=[pl.BlockSpec((tm, c), rev), pl.BlockSpec((tm, c), rev), pl.BlockSpec((tm, 2 * ns), rev),
                   pl.BlockSpec((2, SUB, ns), lambda i: (0, 0, 0)), pl.BlockSpec((1, c), lambda i: (0, 0))],
        out_shape=[S((t, c), BF16), S((t, c), BF16), S((t, 2 * ns), BF16), S((2, SUB, ns), F32), S((1, c), F32)],
        scratch_shapes=[pltpu.VMEM((tm, 2 * ns), F32), pltpu.VMEM((2, SUB, ns), F32)],
        compiler_params=_cp("arbitrary"), name=name)(dgy, y, u, xs, xs, mct, mbt, qw, dskip)


def _first(accs, *_):
    return [accs[0]]


def _rms_bwd_epi(accs, xv, base, rv, g):
    dv = accs[0]
    w = dv * g
    xh = xv * rv
    dx = base + rv * (w - xh * jnp.mean(w * xh, axis=-1, keepdims=True))
    return [dx, dx, jnp.sum(dv * xh, axis=0, keepdims=True)]


def mm_rms_bwd(name, pairs, x, r, gain, dres):
    t, d = x.shape
    return mm_nn(name, t, d, pairs, 1, _rms_bwd_epi, [F32, BF16], tiled=[x, dres], cols=[r], rowv=[gain], sums=[(1, d)])


def _add_res(accs, res):
    return [accs[0] + res]


def even_fwd(x, w):
    t = x.shape[0]
    hn, r = rms_fwd("e_norm_f", x, w["e_norm"])
    (proj,) = mm_nn("e_in_f", t, IN_WIDTH, [(hn, w["e_w_in_t"], 0, "t")], 1, _first, [F32])
    out_a = gmlp_fwd("e_gmlp_f", proj, w["e_gmlp_w"], w["e_gmlp_b"])
    hc = conv_fwd("e_conv_f", proj, w["e_conv_w"], w["e_conv_b"])
    out_b = ln_silu_fwd("e_ln_f", hc, w["e_conv_ln_g"], w["e_conv_ln_b"])
    (x1,) = mm_nn("e_out_f", t, D_MODEL, [(out_a, (w["e_w_out"], 0), 0), (out_b, (w["e_w_out"], 1), 0)],
                  1, _add_res, [F32], tiled=[x])
    return x1, (x, hn, r, proj, out_a, hc, out_b)


def even_bwd_mixers(dxb, saved, w):
    x, hn, r, proj, out_a, hc, out_b = saved
    t = x.shape[0]
    (dcat,) = mm_nn("e_out_b", t, D_MODEL, [(dxb, w["e_w_out"], 0, "t")], 1, _first, [F32])
    g_w_out = jnp.concatenate([mm_tn("e_out_wa", out_a, dxb), mm_tn("e_out_wb", out_b, dxb)], axis=0)
    dab, g_gw, g_gb = gmlp_bwd("e_gmlp_b", proj, dcat, w["e_gmlp_w"], w["e_gmlp_b"])
    dhc, g_lg, g_lb = ln_silu_bwd("e_ln_b", hc, dcat, w["e_conv_ln_g"], w["e_conv_ln_b"])
    dba, dbg, g_cw, g_cb = conv_bwd("e_conv_b", proj, dhc, w["e_conv_w"])
    g_w_in_t = jnp.concatenate([mm_tn("e_in_w0", dab, hn), mm_tn("e_in_w1", dba, hn), mm_tn("e_in_w2", dbg, hn)], axis=0)
    grads = dict(e_w_in_t=g_w_in_t, e_gmlp_w=g_gw[None], e_gmlp_b=g_gb.reshape(1, A_GROUPS, GMLP_BLOCK),
                 e_conv_w=g_cw[None], e_conv_b=g_cb, e_conv_ln_g=g_lg, e_conv_ln_b=g_lb, e_w_out=g_w_out)
    return (dab, dba, dbg), grads


def even_bwd_input(dx, dproj, saved, w):
    x, _, r = saved[:3]
    dab, dba, dbg = dproj
    w_in_t = w["e_w_in_t"]
    return mm_rms_bwd("e_in_b", [(dab, (w_in_t, 0), 0), (dba, (w_in_t, 2), 0), (dbg, (w_in_t, 3), 0)], x, r, w["e_norm"], dx)


def odd_fwd(x, w, consts):
    t = x.shape[0]
    _, mb, mc, pw, _ = consts
    hn, r = rms_fwd("o_norm_f", x, w["o_norm"])
    (u,) = mm_nn("o_in_f", t, C_WIDTH, [(hn, w["o_w_in"], 0)], 1, _first, [F32])
    gy, y, xs, xsb = s5_fwd("o_s5_f", u, _bf(mb), _bf(mc), pw, w["o_d"])
    w_out_t = w["o_w_out_t"]

    def epi(accs, res):
        return [res + accs[0] * _sigmoid(accs[1]), accs[0], accs[1]]

    x1, o1, o2 = mm_nn("o_out_f", t, D_MODEL, [(gy, (w_out_t, 0), 0, "t"), (gy, (w_out_t, D_MODEL), 1, "t")], 2, epi,
                       [F32, BF16, BF16], tiled=[x])
    return x1, (x, hn, r, u, gy, y, xs, xsb, o1, o2)


def odd_bwd(dx, dxb, saved, w, consts, consts_vjp):
    x, hn, r, u, gy, y, xs, xsb, o1, o2 = saved
    t = x.shape[0]
    _, mb, mc, _, qw = consts

    def gate_bwd(dv, a, b):
        a = a.astype(F32)
        sg = _sigmoid(b.astype(F32))
        return [jnp.concatenate([dv * sg, dv * a * sg * (1.0 - sg)], axis=1)], []

    (do12,) = rows_call("o_gate_b", gate_bwd, [dx, o1, o2], [], [(2 * D_MODEL, BF16)], [])
    (dgy,) = mm_nn("o_out_b", t, C_WIDTH, [(do12, w["o_w_out_t"], 0)], 1, _first, [F32])
    g_w_out_t = mm_tn("o_out_w", do12, gy)
    du, dyb, lamb, da8, g_d = s5_bwd("o_s5_b", dgy, y, u, xs, _bf(mc.T), _bf(mb.T), qw, w["o_d"])
    d_mb = mm_tn("o_s5_wb", u, lamb, out_dtype=F32)
    d_mc = mm_tn("o_s5_wc", xsb, dyb, out_dtype=F32)
    g_lr, g_li, g_dt, g_br, g_bi, g_cr, g_ci = consts_vjp((jnp.sum(da8, axis=1), d_mb, d_mc))
    g_w_in = mm_tn("o_in_w", hn, du)
    dx0, dx0b, g_norm = mm_rms_bwd("o_in_b", [(du, w["o_w_in"], 0, "t")], x, r, w["o_norm"], dx)
    grads = dict(o_norm=g_norm, o_w_in=g_w_in, o_lam_re=g_lr[None], o_lam_im=g_li[None], o_log_dt=g_dt[None],
                 o_b_re=g_br[None], o_b_im=g_bi[None], o_c_re=g_cr[None], o_c_im=g_ci[None], o_d=g_d, o_w_out_t=g_w_out_t)
    return dx0, dx0b, grads


def ca_fwd(i, x, mem, w):
    t, m = x.shape[0], mem.shape[0]
    xn, r = rms_fwd(f"ca{i}_norm_f", x, w["ca_norm"][i:i + 1])
    mn, rm = rms_fwd(f"ca{i}_mnorm_f", mem, w["ca_mem_norm"][i:i + 1])
    (q,) = mm_nn(f"ca{i}_q_f", t, D_MODEL, [(xn, w["ca_wq"][i], 0)], 1, _first, [BF16])
    k, v = mm_nn(f"ca{i}_kv_f", m, D_MODEL, [(mn, w["ca_wk"][i], 0), (mn, w["ca_wv"][i], 1)], 2,
                 lambda accs: [accs[0], accs[1]], [BF16, BF16])
    o = attn_fwd(f"ca{i}_attn_f", q, k, v)
    (x1,) = mm_nn(f"ca{i}_o_f", t, D_MODEL, [(o, w["ca_wo"][i], 0)], 1, _add_res, [F32], tiled=[x])
    return x1, (x, xn, r, mn, rm, q, k, v, o)


def ca_bwd(i, dx, dxb, saved, mem, w):
    x, xn, r, mn, rm, q, k, v, o = saved
    t, m = x.shape[0], mem.shape[0]
    (do,) = mm_nn(f"ca{i}_o_b", t, D_MODEL, [(dxb, w["ca_wo"][i], 0, "t")], 1, _first, [BF16])
    g_wo = mm_tn(f"ca{i}_o_w", o, dxb)
    dq, dk, dv = attn_bwd(f"ca{i}_attn_b", q, k, v, do)
    g_wq = mm_tn(f"ca{i}_q_w", xn, dq)
    g_wk = mm_tn(f"ca{i}_k_w", mn, dk)
    g_wv = mm_tn(f"ca{i}_v_w", mn, dv)
    (dmn,) = mm_nn(f"ca{i}_kv_b", m, D_MODEL, [(dk, w["ca_wk"][i], 0, "t"), (dv, w["ca_wv"][i], 0, "t")], 1, _first, [F32])
    g_mnorm = rms_bwd_gain_only(f"ca{i}_mnorm_b", dmn, mem, rm)
    dx0, dx0b, g_norm = mm_rms_bwd(f"ca{i}_q_b", [(dq, w["ca_wq"][i], 0, "t")], x, r, w["ca_norm"][i:i + 1], dx)
    return dx0, dx0b, dict(ca_norm=g_norm, ca_mem_norm=g_mnorm, ca_wq=g_wq, ca_wk=g_wk, ca_wv=g_wv, ca_wo=g_wo)


def ffn_fwd(i, x, w):
    t = x.shape[0]
    xn, r = rms_fwd(f"ffn{i}_norm_f", x, w["ffn_norm"][i:i + 1])

    def epi(accs):
        g, u = accs
        return [g, u, g * _sigmoid(g) * u]

    g, u, h = mm_nn(f"ffn{i}_up_f", t, FFN_HIDDEN, [(xn, w["ffn_w_gate_t"][i], 0, "t"), (xn, w["ffn_w_up_t"][i], 1, "t")],
                    2, epi, [BF16, BF16, BF16])
    (x1,) = mm_nn(f"ffn{i}_down_f", t, D_MODEL, [(h, w["ffn_w_down"][i], 0)], 1, _add_res, [F32], tiled=[x])
    return x1, (x, xn, r, g, u, h)


def ffn_bwd(i, dx, dxb, saved, w):
    x, xn, r, g, u, h = saved
    t = x.shape[0]

    def epi(accs, gv, uv):
        dh = accs[0]
        gv = gv.astype(F32)
        uv = uv.astype(F32)
        s = _sigmoid(gv)
        return [dh * uv * s * (1.0 + gv * (1.0 - s)), dh * gv * s]

    dg, du = mm_nn(f"ffn{i}_down_b", t, FFN_HIDDEN, [(dxb, w["ffn_w_down"][i], 0, "t")], 1, epi, [BF16, BF16], tiled=[g, u])
    g_wd = mm_tn(f"ffn{i}_down_w", h, dxb)
    g_wg_t = mm_tn(f"ffn{i}_gate_w", dg, xn)
    g_wu_t = mm_tn(f"ffn{i}_up_w", du, xn)
    dx0, dx0b, g_norm = mm_rms_bwd(f"ffn{i}_up_b", [(dg, w["ffn_w_gate_t"][i], 0), (du, w["ffn_w_up_t"][i], 0)], x, r,
                                   w["ffn_norm"][i:i + 1], dx)
    return dx0, dx0b, dict(ffn_norm=g_norm, ffn_w_gate_t=g_wg_t, ffn_w_up_t=g_wu_t, ffn_w_down=g_wd)


_S5_PARAMS = ("o_lam_re", "o_lam_im", "o_log_dt", "o_b_re", "o_b_im", "o_c_re", "o_c_im")


def local_step(x, mem, target, w, fetch=None, on_grads=None):
    def consts_fn(*p):
        a, mb, mc = s5_constants(*p)
        return a, mb, mc

    (a, mb, mc), consts_vjp = jax.vjp(consts_fn, *[w[k] for k in _S5_PARAMS])
    consts = (a, mb, mc, _scan_powers(a, False), _scan_powers(a, True))

    def need(stage, after):
        if fetch is not None:
            for k, v in fetch(stage, after).items():
                if isinstance(k, tuple):
                    w.setdefault(k[0], {})[k[1]] = v
                else:
                    w[k] = v

    need(0, x)
    x1, s_e = even_fwd(x, w)
    need(1, x1)
    x2, s_c0 = ca_fwd(0, x1, mem, w)
    need(2, x2)
    x3, s_f0 = ffn_fwd(0, x2, w)
    x4, s_o = odd_fwd(x3, w, consts)
    need(3, x4)
    x5, s_c1 = ca_fwd(1, x4, mem, w)
    x6, s_f1 = ffn_fwd(1, x5, w)
    dx, dxb, g_final, loss = final_loss("final_loss", x6, w["final_norm"], target)

    def emit(stage, carry, plain, layered=None, layer=0):
        if on_grads is None:
            return carry
        out = dict(plain)
        out.update({(k, layer): v for k, v in (layered or {}).items()})
        return on_grads(stage, out, list(carry))

    dx, dxb, g_f1 = ffn_bwd(1, dx, dxb, s_f1, w)
    dx, dxb = emit(0, (dx, dxb), {}, g_f1, 1)
    dx, dxb, g_c1 = ca_bwd(1, dx, dxb, s_c1, mem, w)
    dx, dxb, g_o = odd_bwd(dx, dxb, s_o, w, consts, consts_vjp)
    dx, dxb = emit(1, (dx, dxb), g_o, g_c1, 1)
    dx, dxb, g_f0 = ffn_bwd(0, dx, dxb, s_f0, w)
    dx, dxb = emit(2, (dx, dxb), {}, g_f0, 0)
    dx, dxb, g_c0 = ca_bwd(0, dx, dxb, s_c0, mem, w)
    dx, dxb = emit(3, (dx, dxb), {}, g_c0, 0)
    dproj, g_e = even_bwd_mixers(dxb, s_e, w)
    dproj = emit(4, dproj, {**g_e, "o_norm": g_o["o_norm"], "o_d": g_o["o_d"]})
    dx, dxb, g_e["e_norm"] = even_bwd_input(dx, dproj, s_e, w)

    grads = dict(g_e)
    grads.update(g_o)
    for g0, g1 in ((g_c0, g_c1), (g_f0, g_f1)):
        for k in g0:
            grads[k] = jnp.concatenate([g0[k], g1[k]], axis=0) if k.endswith("norm") else (g0[k], g1[k])
    grads["final_norm"] = g_final
    return loss, dx, grads


def _group(axes):
    pos = {a: lax.axis_index(a) for a in ("x", "y", "c")}
    me = 0
    for a in axes:
        me = me * 2 + pos[a]
    peers = []
    for mask in range(1, 2 ** len(axes)):
        peer = dict(pos)
        for bit, a in enumerate(axes):
            if (mask >> (len(axes) - 1 - bit)) & 1:
                peer[a] = 1 - pos[a]
        idx = 0
        for a in axes:
            idx = idx * 2 + peer[a]
        peers.append((idx, (peer["x"], peer["y"], peer["c"])))
    return me, peers


def _sibling():
    x, y, c = lax.axis_index("x"), lax.axis_index("y"), lax.axis_index("c")
    return c, (x, y, 1 - c)


_HBM =pl.BlockSpec(memory_space=pltpu.HBM)
_SEM = pl.BlockSpec(memory_space=pltpu.SEMAPHORE)
_EFFECT = pltpu.SideEffectType.DATAFLOW_SIDE_EFFECTING


def gather_ici_start(name, groups):
    flat = [b for g in groups for b in g]
    sizes = [len(g) for g in groups]
    k_ops, n_g = len(flat), len(groups)
    lands = [lax.empty((4, 2) + tuple(b.shape), b.dtype) for b in flat]

    def body(*refs):
        src, land = refs[:k_ops], refs[k_ops:2 * k_ops]
        sems = refs[2 * k_ops:2 * k_ops + 3 * n_g]
        token = refs[-1]
        me, peers = _group(("x", "y"))
        core = lax.axis_index("c")
        i = 0
        for g in range(n_g):
            send, recv, loc = sems[3 * g:3 * g + 3]
            for j in range(sizes[g]):
                pltpu.make_async_copy(src[i], land[i].at[me, core], loc.at[j]).start()
                for k, (_, dev) in enumerate(peers):
                    pltpu.make_async_remote_copy(src_ref=src[i], dst_ref=land[i].at[me, core], send_sem=send.at[3 * j + k],
                                                 recv_sem=recv.at[3 * j + k], device_id=dev, device_id_type=MESH).start()
                i += 1
        token[...] = jnp.zeros(token.shape, token.dtype)

    sem_shapes = []
    for s in sizes:
        sem_shapes += [pltpu.SemaphoreType.DMA((3 * s,)), pltpu.SemaphoreType.DMA((3 * s,)), pltpu.SemaphoreType.DMA((s,))]
    thru = [pltpu.HBM(a.shape, a.dtype) for a in flat + lands]
    outs = pl.pallas_call(
        body, name=name, out_shape=tuple(sem_shapes) + tuple(thru) + (S((8, LANES), F32),),
        in_specs=[_HBM] * (2 * k_ops), out_specs=[_SEM] * (3 * n_g) + [_HBM] * (2 * k_ops) + [pl.BlockSpec(memory_space=pltpu.VMEM)],
        input_output_aliases={i: 3 * n_g + i for i in range(2 * k_ops)},
        compiler_params=pltpu.CompilerParams(has_side_effects=_EFFECT),
    )(*[pltpu.with_memory_space_constraint(a, pltpu.HBM) for a in flat + lands])
    sems = [tuple(outs[3 * g:3 * g + 3]) for g in range(n_g)]
    srcs_thru, lands_thru, off = [], [], 3 * n_g
    for s in sizes:
        srcs_thru.append(list(outs[off:off + s]))
        off += s
    for s in sizes:
        lands_thru.append(list(outs[off:off + s]))
        off += s
    return sems, srcs_thru, lands_thru, outs[-1]


def gather_ici_wait(name, srcs, lands, sems, after):
    n = len(srcs)

    def body(*refs):
        src, land = refs[:n], refs[n:2 * n]
        send, recv, loc = refs[2 * n:2 * n + 3]
        me, peers = _group(("x", "y"))
        core = lax.axis_index("c")
        for j in range(n):
            for k, (idx, dev) in enumerate(peers):
                cp = pltpu.make_async_remote_copy(src_ref=src[j], dst_ref=land[j].at[idx, core], send_sem=send.at[3 * j + k],
                                                  recv_sem=recv.at[3 * j + k], device_id=dev, device_id_type=MESH)
                cp.wait_send()
                cp.wait_recv()
            pltpu.make_async_copy(src[j], land[j].at[me, core], loc.at[j]).wait()

    outs = pl.pallas_call(
        body, name=name, out_shape=tuple(pltpu.HBM(a.shape, a.dtype) for a in list(srcs) + list(lands)),
        in_specs=[_HBM] * (2 * n) + [_SEM] * 3 + [ANY], out_specs=[_HBM] * (2 * n),
        input_output_aliases={i: i for i in range(2 * n)},
        compiler_params=pltpu.CompilerParams(has_side_effects=_EFFECT),
    )(*srcs, *lands, *sems, after)
    return list(outs[n:])


def gather_d2d(name, bufs):
    k_ops = len(bufs)

    def body(*refs):
        in_refs, out_refs = refs[:k_ops], refs[k_ops:2 * k_ops]
        send_sems, recv_sems = refs[2 * k_ops:]
        core, sib = _sibling()
        sent, landed = [], []
        for i in range(k_ops):
            cp = pltpu.make_async_remote_copy(src_ref=in_refs[i].at[:, core], dst_ref=out_refs[i].at[:, core],
                                              send_sem=send_sems.at[i], recv_sem=recv_sems.at[i], device_id=sib, device_id_type=MESH)
            cp.start()
            sent.append(cp)
            landed.append(pltpu.make_async_remote_copy(src_ref=in_refs[i].at[:, core], dst_ref=out_refs[i].at[:, 1 - core],
                                                       send_sem=send_sems.at[i], recv_sem=recv_sems.at[i],
                                                       device_id=sib, device_id_type=MESH))
        for cp in landed:
            cp.wait_recv()
        for cp in sent:
            cp.wait_send()

    return pl.pallas_call(
        body, in_specs=[ANY] * k_ops, out_specs=[ANY] * k_ops, out_shape=[S(b.shape, b.dtype) for b in bufs],
        input_output_aliases={i: i for i in range(k_ops)},
        scratch_shapes=[pltpu.SemaphoreType.DMA((k_ops,)), pltpu.SemaphoreType.DMA((k_ops,))],
        name=name)(*bufs)


def scatter_d2d(name, pack):
    q, _, rows, c = pack.shape

    def body(in_ref, out_ref, send_sem, recv_sem):
        core, sib = _sibling()
        cp = pltpu.make_async_remote_copy(src_ref=in_ref.at[:, 1 - core], dst_ref=out_ref, send_sem=send_sem, recv_sem=recv_sem,
                                          device_id=sib, device_id_type=MESH)
        cp.start()
        cp.wait_recv()
        cp.wait_send()

    return pl.pallas_call(
        body, in_specs=[ANY], out_specs=ANY, out_shape=S((q, rows, c), pack.dtype),
        scratch_shapes=[pltpu.SemaphoreType.DMA, pltpu.SemaphoreType.DMA], name=name)(pack)


def scatter_ici_start(name, arr, carry):
    land = lax.empty(arr.shape, arr.dtype)
    n_c = len(carry)

    def body(*refs):
        in_ref, land_ref = refs[0], refs[1]
        send, recv = refs[2 + n_c], refs[3 + n_c]
        me, peers = _group(("x", "y"))
        for k, (idx, dev) in enumerate(peers):
            pltpu.make_async_remote_copy(src_ref=in_ref.at[idx], dst_ref=land_ref.at[me], send_sem=send.at[k], recv_sem=recv.at[k],
                                         device_id=dev, device_id_type=MESH).start()

    thru = [arr, land] + list(carry)
    outs = pl.pallas_call(
        body, name=name,
        out_shape=(pltpu.SemaphoreType.DMA((3,)), pltpu.SemaphoreType.DMA((3,))) + tuple(pltpu.HBM(a.shape, a.dtype) for a in thru),
        in_specs=[_HBM] * len(thru), out_specs=[_SEM, _SEM] + [_HBM] * len(thru),
        input_output_aliases={i: 2 + i for i in range(len(thru))},
        compiler_params=pltpu.CompilerParams(has_side_effects=_EFFECT),
    )(*[pltpu.with_memory_space_constraint(a, pltpu.HBM) for a in thru])
    return (outs[0], outs[1]), outs[2], outs[3], list(outs[4:])


def scatter_ici_wait(name, arr, land, sems, after):
    def body(in_ref, land_ref, send, recv, after_ref, in_thru, land_thru):
        _, peers = _group(("x", "y"))
        for k, (idx, dev) in enumerate(peers):
            cp = pltpu.make_async_remote_copy(src_ref=in_ref.at[idx], dst_ref=land_ref.at[idx], send_sem=send.at[k],
                                              recv_sem=recv.at[k], device_id=dev, device_id_type=MESH)
            cp.wait_send()
            cp.wait_recv()

    outs = pl.pallas_call(
        body, name=name, out_shape=(pltpu.HBM(arr.shape, arr.dtype), pltpu.HBM(arr.shape, arr.dtype)),
        in_specs=[_HBM, _HBM, _SEM, _SEM, ANY], out_specs=[_HBM, _HBM], input_output_aliases={0: 0, 1: 1},
        compiler_params=pltpu.CompilerParams(has_side_effects=_EFFECT),
    )(arr, land, sems[0], sems[1], after)
    return outs[0], outs[1]


def _row_tile(rows, cap=512):
    return next(t for t in range(cap - cap % 16, 0, -16) if rows % t == 0)


def sum_pair(name, pack, recv, core):
    q, rows, c = recv.shape
    tr = _row_tile(rows)

    def body(core_ref, a_ref, b_ref, o_ref):
        o_ref[...] = (a_ref[...].astype(F32) + b_ref[...].astype(F32)).astype(o_ref.dtype)

    spec = pltpu.PrefetchScalarGridSpec(
        num_scalar_prefetch=1, grid=(q, rows // tr),
        in_specs=[pl.BlockSpec((None, None, tr, c), lambda j, i, core: (j, core[0], i, 0)),
                  pl.BlockSpec((None, tr, c), lambda j, i, core: (j, i, 0))],
        out_specs=pl.BlockSpec((None, tr, c), lambda j, i, core: (j, i, 0)))
    return pl.pallas_call(body, grid_spec=spec, out_shape=S(recv.shape, recv.dtype),
                          compiler_params=_cp("parallel", "parallel"), name=name)(core, pack, recv)


def sum_quad(name, own, recv, chip):
    _, rows, c = recv.shape
    tr = _row_tile(rows)

    def body(chip_ref, a_ref, r1_ref, r2_ref, r3_ref, o_ref):
        o_ref[...] = ((a_ref[...].astype(F32) + r1_ref[...].astype(F32)) + r2_ref[...].astype(F32)) + r3_ref[...].astype(F32)

    def slot(mask):
        return pl.BlockSpec((None, tr, c), lambda i, chip, mask=mask: (jnp.bitwise_xor(chip[0], mask), i, 0))

    spec = pltpu.PrefetchScalarGridSpec(
        num_scalar_prefetch=1, grid=(rows // tr,), in_specs=[slot(0), slot(1), slot(2), slot(3)],
        out_specs=pl.BlockSpec((tr, c), lambda i, chip: (i, 0)))
    return pl.pallas_call(body, grid_spec=spec, out_shape=S((rows, c), F32),
                          compiler_params=_cp("parallel"), name=name)(chip, own, recv, recv, recv)


def adamw_native(name, g, w, m, v, tr=512):
    shape = w.shape
    cols = shape[-1]
    rows = w.size // cols
    tr = _tile(rows, tr) if rows % 8 == 0 else rows
    c1 = 1.0 - ADAM_B1 ** ADAM_STEP
    c2 = 1.0 - ADAM_B2 ** ADAM_STEP

    def body(g_ref, w_ref, m_ref, v_ref, d_ref, m2_ref, v2_ref):
        gv = g_ref[...]
        m2 = ADAM_B1 * m_ref[...] + (1.0 - ADAM_B1) * gv
        v2 = ADAM_B2 * v_ref[...] + (1.0 - ADAM_B2) * (gv * gv)
        m2_ref[...] = m2
        v2_ref[...] = v2
        d_ref[...] = -ADAM_LR * ((m2 / c1) / (jnp.sqrt(v2 / c2) + ADAM_EPS) + ADAM_WD * w_ref[...])

    row = pl.BlockSpec((tr, cols), lambda i: (i, 0))
    outs = pl.pallas_call(body, grid=(rows // tr,), in_specs=[row] * 4, out_specs=[row] * 3,
                          out_shape=[S((rows, cols), F32)] * 3, compiler_params=_cp("parallel"),
                          name=name)(*[a.reshape(rows, cols) for a in (g, w, m, v)])
    return tuple(o.reshape(shape) for o in outs)


def adamw_call(name, slots, w, m, v, tr=1024):
    n, r, c = slots.shape
    tr = _tile(r, tr)
    c1 = 1.0 - ADAM_B1 ** ADAM_STEP
    c2 = 1.0 - ADAM_B2 ** ADAM_STEP

    def body(s_ref, w_ref, m_ref, v_ref, g_ref, d_ref, m2_ref, v2_ref):
        g = s_ref[0].astype(F32)
        for j in range(1, n):
            g = g + s_ref[j].astype(F32)
        m2 = ADAM_B1 * m_ref[...] + (1.0 - ADAM_B1) * g
        v2 = ADAM_B2 * v_ref[...] + (1.0 - ADAM_B2) * (g * g)
        g_ref[...] = g
        m2_ref[...] = m2
        v2_ref[...] = v2
        d_ref[...] = -ADAM_LR * ((m2 / c1) / (jnp.sqrt(v2 / c2) + ADAM_EPS) + ADAM_WD * w_ref[...])

    row = pl.BlockSpec((tr, c), lambda i: (i, 0))
    return pl.pallas_call(body, grid=(r // tr,), in_specs=[pl.BlockSpec((n, tr, c), lambda i: (0, i, 0)), row, row, row],
                          out_specs=[row, row, row, row], out_shape=[S((r, c), F32)] * 4,
                          compiler_params=_cp("parallel"), name=name)(slots, w, m, v)


_REPLICATED = ("e_norm", "e_gmlp_w", "e_gmlp_b", "e_conv_b", "e_conv_ln_g", "e_conv_ln_b", "o_lam_re", "o_lam_im", "o_log_dt",
               "o_b_re", "o_b_im", "o_c_re", "o_c_im", "ca_norm", "ca_mem_norm", "ffn_norm", "final_norm")
_ORDER = ("e_norm", "e_w_in", "e_gmlp_w", "e_gmlp_b", "e_conv_w", "e_conv_b", "e_conv_ln_g", "e_conv_ln_b", "e_w_out",
          "o_norm", "o_w_in", "o_lam_re", "o_lam_im", "o_log_dt", "o_b_re", "o_b_im", "o_c_re", "o_c_im", "o_d", "o_w_out",
          "ca_norm", "ca_mem_norm", "ca_wq", "ca_wk", "ca_wv", "ca_wo", "ffn_norm", "ffn_w_gate", "ffn_w_up", "ffn_w_down",
          "final_norm")


def _rows128(a, multiple=8):
    flat = a.reshape(-1)
    rows = -(-flat.shape[0] // (LANES * multiple)) * multiple
    return jnp.pad(flat, (0, rows * LANES - flat.shape[0])).reshape(rows, LANES)


def _shard(full, axis):
    s = full.shape
    return jnp.moveaxis(full.reshape(s[:axis] + (N_DEV, s[axis] // N_DEV) + s[axis + 1:]), axis, 0)


_UNITS = (("e_w_in", 0, True), ("e_w_out", 0, False), ("o_w_in", 0, False), ("o_w_out", 0, True),
          *[(n, i, False) for n in ("ca_wq", "ca_wk", "ca_wv", "ca_wo") for i in (0, 1)],
          *[(n, i, tr) for n, tr in (("ffn_w_gate", True), ("ffn_w_up", True), ("ffn_w_down", False)) for i in (0, 1)])
_LAYERED = ("ca_wq", "ca_wk", "ca_wv", "ca_wo", "ffn_w_gate", "ffn_w_up", "ffn_w_down")
_SMALL_SHARDED = (("e_conv_w", 2), ("o_norm", 1), ("o_d", 1))
RS_ROW = 1024


def _unit_key(name, tr):
    return name + "_t" if tr else name


def _stage_of(name, layer):
    if name.startswith("e_"):
        return 0
    if name.startswith("o_"):
        return 2
    if name.startswith("ca_"):
        return 1 if layer == 0 else 3
    return 2 if layer == 0 else 3


def weight_fetcher(local):
    groups, meta = [[] for _ in range(4)], [[] for _ in range(4)]
    for name, layer, tr in _UNITS:
        blk = local[name][layer]
        st = _stage_of(name, layer)
        groups[st].append(_bf(blk.T if tr else blk))
        meta[st].append((name, layer, tr))
    small = jnp.concatenate([local[name].reshape(-1) for name, _ in _SMALL_SHARDED])
    groups[0].append(_rows128(small))
    sems, srcs, lands, token = gather_ici_start("ag_w_start", groups)

    def fetch(stage, after):
        if stage == 0:
            after = token
        landed = gather_ici_wait(f"ag_w_wait{stage}", srcs[stage], lands[stage], sems[stage], after)
        bufs = gather_d2d(f"ag_w_d2d{stage}", landed)
        got = {}
        for (name, layer, tr), blk, buf in zip(meta[stage], groups[stage], bufs):
            arr = buf.reshape((N_DEV * blk.shape[0],) + tuple(blk.shape[1:]))
            if name in _LAYERED:
                got[(_unit_key(name, tr), layer)] = arr
            else:
                got[_unit_key(name, tr)] = arr
        if stage == 0:
            flat = bufs[-1].reshape(N_DEV, -1)
            off = 0
            for name, axis in _SMALL_SHARDED:
                blk = local[name]
                seg = flat[:, off:off + blk.size].reshape((N_DEV,) + blk.shape)
                off += blk.size
                seg = jnp.moveaxis(seg, 0, axis)
                got[name] = seg.reshape(seg.shape[:axis] + (-1,) + seg.shape[axis + 2:])
            got["e_conv_w"] = got["e_conv_w"][0]
        return got

    return fetch


def _grad_stage_of(name, layer):
    if name.startswith("e_"):
        return 4
    if name.startswith("o_"):
        return 1
    if name.startswith("ca_"):
        return 3 if layer == 0 else 1
    return 2 if layer == 0 else 0


GRAD_STAGES = 5
SMALL_ROWS = 16


def gradient_reducer(local, mom, var):
    core = lax.axis_index("c").astype(jnp.int32).reshape(1)
    chip = (2 * lax.axis_index("x") + lax.axis_index("y")).astype(jnp.int32).reshape(1)
    pending = []

    def start(stage, grads, carry):
        units = [u for u in _UNITS if _grad_stage_of(u[0], u[1]) == stage]
        parts, spans = [], []
        for name, layer, tr in units:
            key = _unit_key(name, tr)
            g = grads[(key, layer)] if name in _LAYERED else grads[key]
            part = g.reshape(4, 2, -1, RS_ROW)
            spans.append((part.shape[2], g.shape[0] // N_DEV, g.shape[1]))
            parts.append(part)
        if stage == GRAD_STAGES - 1:
            small = jnp.concatenate([_shard(grads[name], axis).reshape(N_DEV, -1) for name, axis in _SMALL_SHARDED], axis=1)
            small = jnp.pad(small, ((0, 0), (0, SMALL_ROWS * RS_ROW - small.shape[1])))
            parts.append(small.astype(BF16).reshape(4, 2, SMALL_ROWS, RS_ROW))
        pack = jnp.concatenate(parts, axis=2)
        from_sibling = scatter_d2d(f"rs_d2d{stage}", pack)
        chip_sum = sum_pair(f"rs_pair{stage}", pack, from_sibling, core)
        sems, own, land, carry = scatter_ici_start(f"rs_start{stage}", chip_sum, carry)
        pending.append((stage, units, spans, sems, own, land))
        return carry

    def finish(after):
        res, per_layer, small_flat = {}, {}, None
        for stage, units, spans, sems, own, land in pending:
            own, land = scatter_ici_wait(f"rs_wait{stage}", own, land, sems, after)
            total = sum_quad(f"rs_quad{stage}", own, land, chip)
            off = 0
            for (name, layer, tr), (rows, r, c) in zip(units, spans):
                g = total[off:off + rows].reshape(r, c)
                off += rows
                per_layer.setdefault(name, {})[layer] = g.T if tr else g
            if stage == GRAD_STAGES - 1:
                small_flat = total[off:off + SMALL_ROWS].reshape(-1)
        for name, by_layer in per_layer.items():
            g = jnp.stack([by_layer[i] for i in sorted(by_layer)]) if name in _LAYERED else by_layer[0][None]
            res[name] = (g,) + adamw_native("adamw_" + name, g, local[name], mom[name], var[name])
        off = 0
        for name, _ in _SMALL_SHARDED:
            blk = local[name]
            g = small_flat[off:off + blk.size].reshape(blk.shape)
            off += blk.size
            res[name] = (g,) + adamw_native("adamw_" + name, g, blk, mom[name], var[name])
        return res

    return start, finish


def _pack_replicated(src, last):
    return jnp.concatenate([_rows128(src[name]) for name in _REPLICATED] + [_rows128(last)], axis=0)


def replicated_start(grads, loss):
    sems, srcs, lands, token = gather_ici_start("ag_g_start", [[_pack_replicated(grads, loss)]])
    return sems[0], srcs[0], lands[0], token


def replicated_finish(handle, after, w, mom, var):
    sems, srcs, lands, _ = handle
    (buf,) = gather_d2d("ag_g_d2d", gather_ici_wait("ag_g_wait", srcs, lands, sems, after))
    zero = jnp.zeros((1, 1), F32)
    rows = srcs[0].shape[0]
    outs = adamw_call("adamw_replicated", buf.reshape(N_DEV, rows, LANES), _pack_replicated(w, zero),
                      _pack_replicated(mom, zero), _pack_replicated(var, zero), tr=rows)
    res, off = {}, 0
    for name in _REPLICATED:
        n = w[name].size
        nr = _rows128(w[name]).shape[0]
        res[name] = tuple(o[off:off + nr].reshape(-1)[:n].reshape(w[name].shape) for o in outs)
        off += nr
    return res, outs[0][off, 0]


def kernel(x, mem, e_norm, e_w_in, e_gmlp_w, e_gmlp_b, e_conv_w, e_conv_b, e_conv_ln_g, e_conv_ln_b, e_w_out, o_norm, o_w_in, o_lam_re, o_lam_im, o_log_dt, o_b_re, o_b_im, o_c_re, o_c_im, o_d, o_w_out, ca_norm, ca_mem_norm, ca_wq, ca_wk, ca_wv, ca_wo, ffn_norm, ffn_w_gate, ffn_w_up, ffn_w_down, final_norm, loss_target, m_e_norm, m_e_w_in, m_e_gmlp_w, m_e_gmlp_b, m_e_conv_w, m_e_conv_b, m_e_conv_ln_g, m_e_conv_ln_b, m_e_w_out, m_o_norm, m_o_w_in, m_o_lam_re, m_o_lam_im, m_o_log_dt, m_o_b_re, m_o_b_im, m_o_c_re, m_o_c_im, m_o_d, m_o_w_out, m_ca_norm, m_ca_mem_norm, m_ca_wq, m_ca_wk, m_ca_wv, m_ca_wo, m_ffn_norm, m_ffn_w_gate, m_ffn_w_up, m_ffn_w_down, m_final_norm, v_e_norm, v_e_w_in, v_e_gmlp_w, v_e_gmlp_b, v_e_conv_w, v_e_conv_b, v_e_conv_ln_g, v_e_conv_ln_b, v_e_w_out, v_o_norm, v_o_w_in, v_o_lam_re, v_o_lam_im, v_o_log_dt, v_o_b_re, v_o_b_im, v_o_c_re, v_o_c_im, v_o_d, v_o_w_out, v_ca_norm, v_ca_mem_norm, v_ca_wq, v_ca_wk, v_ca_wv, v_ca_wo, v_ffn_norm, v_ffn_w_gate, v_ffn_w_up, v_ffn_w_down, v_final_norm):
    given = dict(locals())
    local = {k: given[k] for k in _ORDER}
    mom = {k: given["m_" + k] for k in _ORDER}
    var = {k: given["v_" + k] for k in _ORDER}

    w = {}
    w.update({
        "e_norm": e_norm, "e_gmlp_w": e_gmlp_w[0], "e_gmlp_b": e_gmlp_b.reshape(A_GROUPS, GMLP_BLOCK, 1),
        "e_conv_b": e_conv_b, "e_conv_ln_g": e_conv_ln_g, "e_conv_ln_b": e_conv_ln_b,
        "o_lam_re": o_lam_re[0], "o_lam_im": o_lam_im[0], "o_log_dt": o_log_dt[0], "o_b_re": o_b_re[0], "o_b_im": o_b_im[0],
        "o_c_re": o_c_re[0], "o_c_im": o_c_im[0], "ca_norm": ca_norm, "ca_mem_norm": ca_mem_norm, "ffn_norm": ffn_norm,
        "final_norm": final_norm.reshape(1, D_MODEL),
    })
    start_reduce, finish_reduce = gradient_reducer(local, mom, var)
    loss_part, grad_x, grads = local_step(x[0], mem[0], loss_target[0], w, weight_fetcher(local), start_reduce)
    grads["final_norm"] = grads["final_norm"].reshape(D_MODEL)

    handle = replicated_start(grads, loss_part)
    res = finish_reduce(handle[3])
    rep, loss = replicated_finish(handle, res["ffn_w_down"][1], local, mom, var)
    res.update(rep)
    return (loss, grad_x[None], *[res[k][0] for k in _ORDER], *[res[k][1] for k in _ORDER],
            *[res[k][2] for k in _ORDER], *[res[k][3] for k in _ORDER])
```

```python
import jax
import jax.numpy as jnp
from jax import lax
from jax.experimental import pallas as pl
from jax.experimental.pallas import tpu as pltpu

F32 = jnp.float32
BF16 = jnp.bfloat16
S = jax.ShapeDtypeStruct

D_MODEL = 1024
A_WIDTH = 512
A_GROUPS = 4
GMLP_BLOCK = 128
CHUNK = 64
B_WIDTH = 512
IN_WIDTH = 2 * A_WIDTH + 2 * B_WIDTH
CONV_WIDTH = 31
CONV_PAD = 32
C_WIDTH = 512
C_GROUP_CH = 16
C_GROUPS = 32
C_STATE = 64
N_STATE = C_GROUPS * C_STATE
CA_HEADS = 4
CA_HEAD_DIM = 256
FFN_HIDDEN = 2816
EPS = 1e-6
ADAM_LR = 0.001
ADAM_B1 = 0.9
ADAM_B2 = 0.999
ADAM_EPS = 1e-08
ADAM_WD = 0.01
ADAM_STEP = 10
N_DEV = 8
LANES = 128
VMEM_LIMIT = 56 << 20
VMEM_BUDGET = 40 << 20
MM_TN_RESIDENT = 8 << 20
MESH = pl.DeviceIdType.MESH
ANY = pl.BlockSpec(memory_space=pl.ANY)


def _cp(*sem):
    return pltpu.CompilerParams(dimension_semantics=sem, vmem_limit_bytes=VMEM_LIMIT)


def _tile(n, pref):
    t = pref
    while n % t:
        t //= 2
    return t


def _bf(v):
    return v if v.dtype == BF16 else v.astype(BF16)


def _sigmoid(x):
    return 1.0 / (1.0 + jnp.exp(-x))


_GC = 0.7978845608028654


def _gelu(x):
    return 0.5 * x * (1.0 + jnp.tanh(_GC * (x + 0.044715 * x * x * x)))


def _gelu_grad(x):
    x2 = x * x
    t = jnp.tanh(_GC * (x + 0.044715 * x * x2))
    return 0.5 * (1.0 + t) + 0.5 * x * (1.0 - t * t) * _GC * (1.0 + 3.0 * 0.044715 * x2)


def _tspec(entry, tm):
    if isinstance(entry, tuple):
        arr, cb, width = entry
        return arr, pl.BlockSpec((tm, width), lambda i, cb=cb: (i, cb))
    return entry, pl.BlockSpec((tm, entry.shape[1]), lambda i: (i, 0))


def rows_call(name, fn, tiled, full, outs, accs, tm=256):
    pairs = [_tspec(e, tm) for e in tiled]
    arrs = [p[0] for p in pairs]
    rows = arrs[0].shape[0]
    tm = _tile(rows, tm)
    pairs = [_tspec(e, tm) for e in tiled]
    n_in = len(tiled) + len(full)
    n_out = len(outs)

    def body(*refs):
        vals = [r[...] for r in refs[:n_in]]
        o_refs = refs[n_in:n_in + n_out]
        a_refs = refs[n_in + n_out:]
        ov, av = fn(*vals)
        for r, v in zip(o_refs, ov):
            r[...] = v.astype(r.dtype)
        if a_refs:
            @pl.when(pl.program_id(0) == 0)
            def _():
                for r in a_refs:
                    r[...] = jnp.zeros(r.shape, r.dtype)
            for r, v in zip(a_refs, av):
                r[...] += v

    in_specs = [p[1] for p in pairs] + [pl.BlockSpec(a.shape, lambda i, nd=a.ndim: (0,) * nd) for a in full]
    out_specs = [pl.BlockSpec((tm, c), lambda i: (i, 0)) for c, _ in outs]
    out_specs += [pl.BlockSpec(s, lambda i, nd=len(s): (0,) * nd) for s in accs]
    out_shape = [S((rows, c), dt) for c, dt in outs] + [S(s, F32) for s in accs]
    return pl.pallas_call(body, grid=(rows // tm,), in_specs=in_specs, out_specs=out_specs, out_shape=out_shape,
                          compiler_params=_cp("arbitrary"), name=name)(*arrs, *full)


def mm_nn(name, m, n, pairs, n_acc, epi, outs, tiled=(), cols=(), rowv=(), sums=()):
    a_ops, a_slot, b_arrs, b_specs, idx, trans = [], [], [], [], [], []
    fixed = 0
    for pair in pairs:
        a, b, k = pair[:3]
        bt = len(pair) > 3
        arr, cb, kdim = a if isinstance(a, tuple) else (a, 0, a.shape[1])
        key = (id(arr), cb, kdim)
        if key not in [o[0] for o in a_ops]:
            a_ops.append((key, arr, cb, kdim))
        a_slot.append([o[0] for o in a_ops].index(key))
        b_arr, off = b if isinstance(b, tuple) else (b, 0)
        b_arrs.append(b_arr)
        if bt:
            assert off % n == 0 and b_arr.shape[1] == kdim
            b_specs.append(pl.BlockSpec((n, kdim), lambda i, o=off // n: (o, 0), pipeline_mode=pl.Buffered(1)))
        else:
            assert b_arr.shape[1] == n
            b_specs.append(pl.BlockSpec((kdim, n), lambda i, o=off: (o, 0), pipeline_mode=pl.Buffered(1)))
        fixed += kdim * n * b_arr.dtype.itemsize
        idx.append(k)
        trans.append(bt)
    per_row = sum(2 * kdim * arr.dtype.itemsize for _, arr, _, kdim in a_ops)
    per_row += sum(2 * n * t.dtype.itemsize for t in tiled) + sum(2 * n * jnp.dtype(dt).itemsize for dt in outs)
    per_row += (n_acc + 3) * n * 4
    tm = next((t for t in (1024, 512, 256, 128) if m % t == 0 and fixed + t * per_row <= VMEM_BUDGET), _tile(m, 128))
    n_a, n_p = len(a_ops), len(pairs)
    n_in = n_a + n_p + len(tiled) + len(cols) + len(rowv)

    def body(*refs):
        a_vals = [_bf(r[...]) for r in refs[:n_a]]
        accs = [None] * n_acc
        for p in range(n_p):
            av, bv = a_vals[a_slot[p]], _bf(refs[n_a + p][...])
            if trans[p]:
                d = lax.dot_general(av, bv, (((1,), (1,)), ((), ())), preferred_element_type=F32)
            else:
                d = jnp.dot(av, bv, preferred_element_type=F32)
            accs[idx[p]] = d if accs[idx[p]] is None else accs[idx[p]] + d
        extra = [r[...] for r in refs[n_a + n_p:n_in]]
        ov = epi(accs, *extra)
        sv = ov[len(outs):]
        for r, v in zip(refs[n_in:n_in + len(outs)], ov):
            r[...] = v.astype(r.dtype)
        if sums:
            s_refs = refs[n_in + len(outs):]

            @pl.when(pl.program_id(0) == 0)
            def _():
                for r in s_refs:
                    r[...] = jnp.zeros(r.shape, r.dtype)
            for r, v in zip(s_refs, sv):
                r[...] += v

    in_specs = [pl.BlockSpec((tm, kdim), lambda i, cb=cb: (i, cb)) for _, _, cb, kdim in a_ops] + b_specs
    in_specs += [pl.BlockSpec((tm, n), lambda i: (i, 0)) for _ in tiled]
    in_specs += [pl.BlockSpec((tm, 1), lambda i: (i, 0)) for _ in cols]
    in_specs += [pl.BlockSpec((1, n), lambda i: (0, 0)) for _ in rowv]
    out_specs = [pl.BlockSpec((tm, n), lambda i: (i, 0)) for _ in outs]
    out_specs += [pl.BlockSpec(s, lambda i, nd=len(s): (0,) * nd) for s in sums]
    out_shape = [S((m, n), dt) for dt in outs] + [S(s, F32) for s in sums]
    return pl.pallas_call(body, grid=(m // tm,), in_specs=in_specs, out_specs=out_specs, out_shape=out_shape,
                          compiler_params=_cp("arbitrary" if sums else "parallel"),
                          name=name)(*[o[1] for o in a_ops], *b_arrs, *tiled, *cols, *rowv)


def mm_tn(name, a, b, out_dtype=BF16):
    if isinstance(a, tuple):
        a_arr, a_cb, m = a
    else:
        a_arr, a_cb, m = a, None, a.shape[1]
    if isinstance(b, tuple):
        b_arr, b_cb, n = b
    else:
        b_arr, b_cb, n = b, None, b.shape[1]
    t = a_arr.shape[0]
    whole_b = t * n * b_arr.dtype.itemsize <= MM_TN_RESIDENT and b_cb is None
    tn = n if whole_b else _tile(n, 512)
    tm = _tile(m, 512 if t * 512 * a_arr.dtype.itemsize * 2 + t * tn * b_arr.dtype.itemsize * 2 <= VMEM_BUDGET else 256)
    a_off = 0 if a_cb is None else a_cb * (m // tm)
    b_off = 0 if b_cb is None else b_cb * (n // tn)

    def body(a_ref, b_ref, o_ref):
        o_ref[...] = lax.dot_general(_bf(a_ref[...]), _bf(b_ref[...]), (((0,), (0,)), ((), ())),
                                     preferred_element_type=F32).astype(o_ref.dtype)

    if whole_b:
        b_spec = pl.BlockSpec((t, n), lambda i, j: (0, 0), pipeline_mode=pl.Buffered(1))
    else:
        b_spec = pl.BlockSpec((t, tn), lambda i, j: (0, j + b_off))
    return pl.pallas_call(
        body, grid=(m // tm, n // tn),
        in_specs=[pl.BlockSpec((t, tm), lambda i, j: (0, i + a_off)), b_spec],
        out_specs=pl.BlockSpec((tm, tn), lambda i, j: (i, j)), out_shape=S((m, n), out_dtype),
        compiler_params=_cp("parallel", "parallel"), name=name)(a_arr, b_arr)


def rms_fwd(name, x, gain):
    def fn(xv, g):
        r = lax.rsqrt(jnp.mean(xv * xv, axis=-1, keepdims=True) + EPS)
        return [xv * r * g, r], []
    return rows_call(name, fn, [x], [gain], [(x.shape[1], BF16), (1, F32)], [])


def rms_bwd_gain_only(name, dxn, x, r):
    def fn(dv, xv, rv):
        return [], [jnp.sum(dv * xv * rv, axis=0, keepdims=True)]
    return rows_call(name, fn, [dxn, x, r], [], [], [(1, x.shape[1])])[0]


def final_loss(name, x, gain, target):
    d = x.shape[1]

    def fn(xv, tv, g):
        r = lax.rsqrt(jnp.mean(xv * xv, axis=-1, keepdims=True) + EPS)
        xh = xv * r
        err = xh * g - tv
        dy = err * (1.0 / d)
        w = dy * g
        dx = r * (w - xh * jnp.mean(w * xh, axis=-1, keepdims=True))
        part = jnp.sum(jnp.sum(err * err, axis=-1, keepdims=True), axis=0, keepdims=True) * (0.5 / d)
        return [dx, dx], [jnp.sum(dy * xh, axis=0, keepdims=True), part]

    return rows_call(name, fn, [x, target], [gain], [(d, F32), (d, BF16)], [(1, d), (1, 1)])


def _gmlp_mask():
    row = lax.broadcasted_iota(jnp.int32, (GMLP_BLOCK, GMLP_BLOCK), 0) // CHUNK
    col = lax.broadcasted_iota(jnp.int32, (GMLP_BLOCK, GMLP_BLOCK), 1) // CHUNK
    return col <= row


def _ln_plain(v):
    mu = jnp.mean(v, axis=-1, keepdims=True)
    vc = v - mu
    rstd = lax.rsqrt(jnp.mean(vc * vc, axis=-1, keepdims=True) + EPS)
    return vc * rstd, rstd


def gmlp_fwd(name, proj, w, b, tm=512):
    t = proj.shape[0]
    tm = _tile(t, tm)

    def body(au_ref, av_ref, w_ref, b_ref, o_ref):
        mask = _gmlp_mask()
        u = _gelu(au_ref[...])
        vn, _ = _ln_plain(_gelu(av_ref[...]))
        vnb = _bf(vn)
        for g in range(A_GROUPS):
            wg = _bf(jnp.where(mask, w_ref[g], 0.0))
            cs = slice(g * GMLP_BLOCK, (g + 1) * GMLP_BLOCK)
            for n in range(tm // GMLP_BLOCK):
                rs = slice(n * GMLP_BLOCK, (n + 1) * GMLP_BLOCK)
                sg = jnp.dot(wg, vnb[rs, cs], preferred_element_type=F32) + b_ref[g]
                o_ref[rs, cs] = (u[rs, cs] * sg).astype(o_ref.dtype)

    return pl.pallas_call(
        body, grid=(t // tm,),
        in_specs=[pl.BlockSpec((tm, A_WIDTH), lambda i: (i, 0)), pl.BlockSpec((tm, A_WIDTH), lambda i: (i, 1)),
                  pl.BlockSpec(w.shape, lambda i: (0, 0, 0)), pl.BlockSpec(b.shape, lambda i: (0, 0, 0))],
        out_specs=pl.BlockSpec((tm, A_WIDTH), lambda i: (i, 0)), out_shape=S((t, A_WIDTH), BF16),
        compiler_params=_cp("parallel"), name=name)(proj, proj, w, b)


def gmlp_bwd(name, proj, dcat, w, b, tm=512):
    t = proj.shape[0]
    tm = _tile(t, tm)

    def body(au_ref, av_ref, do_ref, w_ref, b_ref, dp_ref, dw_ref, db_ref):
        @pl.when(pl.program_id(0) == 0)
        def _():
            dw_ref[...] = jnp.zeros(dw_ref.shape, F32)
            db_ref[...] = jnp.zeros(db_ref.shape, F32)

        mask = _gmlp_mask()
        au = au_ref[...]
        av = av_ref[...]
        u = _gelu(au)
        vn, rstd = _ln_plain(_gelu(av))
        vnb = _bf(vn)
        dout = do_ref[...]
        dvn_cols = []
        for g in range(A_GROUPS):
            wm = jnp.where(mask, w_ref[g], 0.0)
            wg = _bf(wm)
            wgt = _bf(wm.T)
            cs = slice(g * GMLP_BLOCK, (g + 1) * GMLP_BLOCK)
            dwg = jnp.zeros((GMLP_BLOCK, GMLP_BLOCK), F32)
            dbg = jnp.zeros((GMLP_BLOCK, 1), F32)
            dvn_rows = []
            for n in range(tm // GMLP_BLOCK):
                rs = slice(n * GMLP_BLOCK, (n + 1) * GMLP_BLOCK)
                sg = jnp.dot(wg, vnb[rs, cs], preferred_element_type=F32) + b_ref[g]
                dp_ref[rs, cs] = (dout[rs, cs] * sg * _gelu_grad(au[rs, cs])).astype(dp_ref.dtype)
                dsg = dout[rs, cs] * u[rs, cs]
                dsgb = _bf(dsg)
                dbg = dbg + jnp.sum(dsg, axis=1, keepdims=True)
                dwg = dwg + lax.dot_general(dsgb, vnb[rs, cs], (((1,), (1,)), ((), ())), preferred_element_type=F32)
                dvn_rows.append(jnp.dot(wgt, dsgb, preferred_element_type=F32))
            dw_ref[g] += jnp.where(mask, dwg, 0.0)
            db_ref[g] += dbg
            dvn_cols.append(jnp.concatenate(dvn_rows, axis=0))
        dvn = jnp.concatenate(dvn_cols, axis=1)
        dv = rstd * (dvn - jnp.mean(dvn, axis=-1, keepdims=True) - vn * jnp.mean(dvn * vn, axis=-1, keepdims=True))
        dp_ref[:, A_WIDTH:] = (dv * _gelu_grad(av)).astype(dp_ref.dtype)

    return pl.pallas_call(
        body, grid=(t // tm,),
        in_specs=[pl.BlockSpec((tm, A_WIDTH), lambda i: (i, 0)), pl.BlockSpec((tm, A_WIDTH), lambda i: (i, 1)),
                  pl.BlockSpec((tm, A_WIDTH), lambda i: (i, 0)),
                  pl.BlockSpec(w.shape, lambda i: (0, 0, 0)), pl.BlockSpec(b.shape, lambda i: (0, 0, 0))],
        out_specs=[pl.BlockSpec((tm, 2 * A_WIDTH), lambda i: (i, 0)),
                   pl.BlockSpec(w.shape, lambda i: (0, 0, 0)), pl.BlockSpec(b.shape, lambda i: (0, 0, 0))],
        out_shape=[S((t, 2 * A_WIDTH), BF16), S(w.shape, F32), S(b.shape, F32)],
        compiler_params=_cp("arbitrary"), name=name)(proj, proj, dcat, w, b)


CONV_ROWS = 256


def conv_fwd(name, proj, w, cb):
    t = proj.shape[0]
    tc = LANES
    rows = _tile(t, CONV_ROWS)
    a_cb, g_cb = 2 * A_WIDTH // tc, (2 * A_WIDTH + B_WIDTH) // tc

    def body(a_ref, g_ref, w_ref, cb_ref, o_ref, hpad):
        hpad[0:CONV_PAD, :] = jnp.zeros((CONV_PAD, tc), F32)

        def fill(i, _):
            r0 = pl.multiple_of(i * rows, rows)
            hpad[pl.ds(CONV_PAD + r0, rows), :] = a_ref[pl.ds(r0, rows), :] * _sigmoid(g_ref[pl.ds(r0, rows), :])
            return 0
        lax.fori_loop(0, t // rows, fill, 0)

        def conv(i, _):
            r0 = pl.multiple_of(i * rows, rows)
            win = hpad[pl.ds(r0, rows + CONV_PAD), :]
            acc = jnp.zeros((rows, tc), F32) + cb_ref[...]
            for b in range(SUB):
                wb = win if b == 0 else pltpu.roll(win, b, 0)
                for a in range(CONV_PAD // SUB):
                    k = CONV_WIDTH - 1 - (SUB * a + b)
                    if k >= 0:
                        lo = CONV_PAD - SUB * a
                        acc = acc + wb[lo:lo + rows, :] * w_ref[k:k + 1, :]
            o_ref[pl.ds(r0, rows), :] = acc
            return 0
        lax.fori_loop(0, t // rows, conv, 0)

    return pl.pallas_call(
        body, grid=(B_WIDTH // tc,),
        in_specs=[pl.BlockSpec((t, tc), lambda j: (0, a_cb + j)), pl.BlockSpec((t, tc), lambda j: (0, g_cb + j)),
                  pl.BlockSpec((CONV_WIDTH, tc), lambda j: (0, j)), pl.BlockSpec((1, tc), lambda j: (0, j))],
        out_specs=pl.BlockSpec((t, tc), lambda j: (0, j)), out_shape=S((t, B_WIDTH), F32),
        scratch_shapes=[pltpu.VMEM((t + CONV_PAD, tc), F32)],
        compiler_params=_cp("parallel"), name=name)(proj, proj, w, cb)


def conv_bwd(name, proj, dhc, w):
    t = proj.shape[0]
    tc = LANES
    rows = _tile(t, CONV_ROWS)
    a_cb, g_cb = 2 * A_WIDTH // tc, (2 * A_WIDTH + B_WIDTH) // tc
    win_rows = rows + CONV_PAD

    def body(a_ref, g_ref, d_ref, w_ref, da_ref, dg_ref, dw_ref, dcb_ref, hpad, dpad, dwacc):
        hpad[0:CONV_PAD, :] = jnp.zeros((CONV_PAD, tc), F32)
        dpad[t:t + CONV_PAD, :] = jnp.zeros((CONV_PAD, tc), F32)
        dwacc[...] = jnp.zeros(dwacc.shape, F32)

        def fill(i, _):
            r0 = pl.multiple_of(i * rows, rows)
            hpad[pl.ds(CONV_PAD + r0, rows), :] = a_ref[pl.ds(r0, rows), :] * _sigmoid(g_ref[pl.ds(r0, rows), :])
            dpad[pl.ds(r0, rows), :] = d_ref[pl.ds(r0, rows), :]
            return 0
        lax.fori_loop(0, t // rows, fill, 0)

        def step(i, dcb):
            r0 = pl.multiple_of(i * rows, rows)
            hwin = hpad[pl.ds(r0, win_rows), :]
            dwin = dpad[pl.ds(r0, win_rows), :]
            dchunk = dwin[:rows, :]
            dh = jnp.zeros((rows, tc), F32)
            for b in range(SUB):
                hb = hwin if b == 0 else pltpu.roll(hwin, b, 0)
                db = dwin if b == 0 else pltpu.roll(dwin, win_rows - b, 0)
                for a in range(CONV_PAD // SUB):
                    k = CONV_WIDTH - 1 - (SUB * a + b)
                    if k >= 0:
                        dh = dh + db[SUB * a:SUB * a + rows, :] * w_ref[k:k + 1, :]
                        lo = CONV_PAD - SUB * a
                        prod = dchunk * hb[lo:lo + rows, :]
                        dwacc[k] += jnp.sum(prod.reshape(rows // 8, 8, tc), axis=0)
            a = a_ref[pl.ds(r0, rows), :]
            sg = _sigmoid(g_ref[pl.ds(r0, rows), :])
            da_ref[pl.ds(r0, rows), :] = (dh * sg).astype(da_ref.dtype)
            dg_ref[pl.ds(r0, rows), :] = (dh * a * sg * (1.0 - sg)).astype(dg_ref.dtype)
            return dcb + jnp.sum(dchunk, axis=0, keepdims=True)
        dcb = lax.fori_loop(0, t // rows, step, jnp.zeros((1, tc), F32))
        dcb_ref[...] = dcb
        for k in range(CONV_WIDTH):
            dw_ref[k:k + 1, :] = jnp.sum(dwacc[k], axis=0, keepdims=True)

    return pl.pallas_call(
        body, grid=(B_WIDTH // tc,),
        in_specs=[pl.BlockSpec((t, tc), lambda j: (0, a_cb + j)), pl.BlockSpec((t, tc), lambda j: (0, g_cb + j)),
                  pl.BlockSpec((t, tc), lambda j: (0, j)), pl.BlockSpec((CONV_WIDTH, tc), lambda j: (0, j))],
        out_specs=[pl.BlockSpec((t, tc), lambda j: (0, j)), pl.BlockSpec((t, tc), lambda j: (0, j)),
                   pl.BlockSpec((CONV_WIDTH, tc), lambda j: (0, j)), pl.BlockSpec((1, tc), lambda j: (0, j))],
        out_shape=[S((t, B_WIDTH), BF16), S((t, B_WIDTH), BF16), S((CONV_WIDTH, B_WIDTH), F32), S((1, B_WIDTH), F32)],
        scratch_shapes=[pltpu.VMEM((t + CONV_PAD, tc), F32), pltpu.VMEM((t + CONV_PAD, tc), F32),
                        pltpu.VMEM((CONV_WIDTH, 8, tc), F32)],
        compiler_params=_cp("parallel"), name=name)(proj, proj, dhc, w)


def ln_silu_fwd(name, hc, g, b):
    def fn(h, gv, bv):
        y, _ = _ln_plain(h)
        z = y * gv + bv
        return [z * _sigmoid(z)], []
    return rows_call(name, fn, [hc], [g, b], [(hc.shape[1], BF16)], [])[0]


def ln_silu_bwd(name, hc, dcat, g, b):
    c = hc.shape[1]

    def fn(h, dout, gv, bv):
        y, rstd = _ln_plain(h)
        z = y * gv + bv
        s = _sigmoid(z)
        dz = dout * s * (1.0 + z * (1.0 - s))
        dyv = dz * gv
        dh = rstd * (dyv - jnp.mean(dyv, axis=-1, keepdims=True) - y * jnp.mean(dyv * y, axis=-1, keepdims=True))
        return [dh], [jnp.sum(dz * y, axis=0, keepdims=True), jnp.sum(dz, axis=0, keepdims=True)]

    return rows_call(name, fn, [hc, (dcat, 1, c)], [g, b], [(c, F32)], [(1, c), (1, c)])


_NT = (((1,), (1,)), ((), ()))
_TN = (((0,), (0,)), ((), ()))


def attn_fwd(name, q, k, v, tm=512):
    t, d = q.shape
    m = k.shape[0]
    tm = _tile(t, tm)
    scale = CA_HEAD_DIM ** -0.5

    def body(q_ref, k_ref, v_ref, o_ref):
        for h in range(CA_HEADS):
            cs = slice(h * CA_HEAD_DIM, (h + 1) * CA_HEAD_DIM)
            s = lax.dot_general(q_ref[:, cs], k_ref[:, cs], _NT, preferred_element_type=F32) * scale
            e = jnp.exp(s - jnp.max(s, axis=-1, keepdims=True))
            p = e / jnp.sum(e, axis=-1, keepdims=True)
            o_ref[:, cs] = jnp.dot(_bf(p), v_ref[:, cs], preferred_element_type=F32).astype(o_ref.dtype)

    return pl.pallas_call(
        body, grid=(t // tm,),
        in_specs=[pl.BlockSpec((tm, d), lambda i: (i, 0)), pl.BlockSpec((m, d), lambda i: (0, 0)),
                  pl.BlockSpec((m, d), lambda i: (0, 0))],
        out_specs=pl.BlockSpec((tm, d), lambda i: (i, 0)), out_shape=S((t, d), BF16),
        compiler_params=_cp("parallel"), name=name)(q, k, v)


def attn_bwd(name, q, k, v, do, tm=512):
    t, d = q.shape
    m = k.shape[0]
    tm = _tile(t, tm)
    scale = CA_HEAD_DIM ** -0.5

    def body(q_ref, k_ref, v_ref, do_ref, dq_ref, dk_ref, dv_ref):
        @pl.when(pl.program_id(0) == 0)
        def _():
            dk_ref[...] = jnp.zeros(dk_ref.shape, F32)
            dv_ref[...] = jnp.zeros(dv_ref.shape, F32)

        for h in range(CA_HEADS):
            cs = slice(h * CA_HEAD_DIM, (h + 1) * CA_HEAD_DIM)
            qh, kh, vh, doh = q_ref[:, cs], k_ref[:, cs], v_ref[:, cs], do_ref[:, cs]
            s = lax.dot_general(qh, kh, _NT, preferred_element_type=F32) * scale
            e = jnp.exp(s - jnp.max(s, axis=-1, keepdims=True))
            p = e / jnp.sum(e, axis=-1, keepdims=True)
            pb = _bf(p)
            dv_ref[:, cs] += lax.dot_general(pb, doh, _TN, preferred_element_type=F32)
            dp = lax.dot_general(doh, vh, _NT, preferred_element_type=F32)
            ds = _bf(p * (dp - jnp.sum(dp * p, axis=-1, keepdims=True)) * scale)
            dq_ref[:, cs] = jnp.dot(ds, kh, preferred_element_type=F32).astype(dq_ref.dtype)
            dk_ref[:, cs] += lax.dot_general(ds, qh, _TN, preferred_element_type=F32)

    return pl.pallas_call(
        body, grid=(t // tm,),
        in_specs=[pl.BlockSpec((tm, d), lambda i: (i, 0)), pl.BlockSpec((m, d), lambda i: (0, 0)),
                  pl.BlockSpec((m, d), lambda i: (0, 0)), pl.BlockSpec((tm, d), lambda i: (i, 0))],
        out_specs=[pl.BlockSpec((tm, d), lambda i: (i, 0)), pl.BlockSpec((m, d), lambda i: (0, 0)),
                   pl.BlockSpec((m, d), lambda i: (0, 0))],
        out_shape=[S((t, d), BF16), S((m, d), F32), S((m, d), F32)],
        compiler_params=_cp("arbitrary"), name=name)(q, k, v, do)


SUB = 8
S5_ROWS = 256


def s5_constants(lam_re, lam_im, log_dt, b_re, b_im, c_re, c_im):
    dt = jnp.exp(log_dt)[:, None]
    mag = jnp.exp(lam_re * dt)
    ar = mag * jnp.cos(lam_im * dt)
    ai = mag * jnp.sin(lam_im * dt)
    den = lam_re * lam_re + lam_im * lam_im
    qr = ((ar - 1.0) * lam_re + ai * lam_im) / den
    qi = (ai * lam_re - (ar - 1.0) * lam_im) / den
    bbr = qr[..., None] * b_re - qi[..., None] * b_im
    bbi = qr[..., None] * b_im + qi[..., None] * b_re
    eye = jnp.eye(C_GROUPS, dtype=F32)

    def in_mat(bb):
        return (bb.transpose(0, 2, 1)[:, :, None, :] * eye[:, None, :, None]).reshape(C_WIDTH, N_STATE)

    def out_mat(cc):
        return (cc.transpose(0, 2, 1)[:, :, None, :] * eye[:, None, :, None]).reshape(N_STATE, C_WIDTH)

    mb = jnp.concatenate([in_mat(bbr), in_mat(bbi)], axis=1)
    mc = jnp.concatenate([out_mat(c_re), -out_mat(c_im)], axis=0)
    a = jnp.stack([ar.reshape(N_STATE), ai.reshape(N_STATE)])
    return a, mb, mc


def _scan_powers(a, conj):
    ar, ai = a[0], (-a[1] if conj else a[1])
    pows = [(ar, ai)]
    for _ in range(SUB - 1):
        pr, pi = pows[-1]
        pows.append((pr * ar - pi * ai, pr * ai + pi * ar))
    rows = jnp.arange(SUB)[:, None]
    out = []
    for s in (1, 2, 4):
        keep = (rows + s <= SUB - 1) if conj else (rows >= s)
        out.append(jnp.stack([jnp.where(keep, pows[s - 1][0][None, :], 0.0), jnp.where(keep, pows[s - 1][1][None, :], 0.0)]))
    order = [SUB - 1 - i for i in range(SUB)] if conj else list(range(SUB))
    out.append(jnp.stack([jnp.stack([pows[i][0] for i in order]), jnp.stack([pows[i][1] for i in order])]))
    return jnp.stack(out)


def _cmul_add(xr, xi, pr, pi, zr, zi):
    return xr + pr * zr - pi * zi, xi + pr * zi + pi * zr


def s5_fwd(name, u, mb, mc, pw, dskip):
    t = u.shape[0]
    tm = _tile(t, S5_ROWS)
    ns = N_STATE

    def body(u_ref, mb_ref, mc_ref, pw_ref, d_ref, gy_ref, y_ref, xs_ref, xb_ref, carry):
        @pl.when(pl.program_id(0) == 0)
        def _():
            carry[...] = jnp.zeros(carry.shape, F32)

        uv = u_ref[...]
        xs_ref[...] = jnp.dot(_bf(uv), mb_ref[...], preferred_element_type=F32)

        def group(i, _):
            r0 = pl.multiple_of(i * SUB, SUB)
            xr = xs_ref[pl.ds(r0, SUB), 0:ns]
            xi = xs_ref[pl.ds(r0, SUB), ns:2 * ns]
            for k, s in enumerate((1, 2, 4)):
                xr, xi = _cmul_add(xr, xi, pw_ref[k, 0], pw_ref[k, 1], pltpu.roll(xr, s, 0), pltpu.roll(xi, s, 0))
            xr, xi = _cmul_add(xr, xi, pw_ref[3, 0], pw_ref[3, 1], carry[0], carry[1])
            xs_ref[pl.ds(r0, SUB), 0:ns] = xr
            xs_ref[pl.ds(r0, SUB), ns:2 * ns] = xi
            carry[0] = jnp.broadcast_to(xr[SUB - 1:SUB, :], (SUB, ns))
            carry[1] = jnp.broadcast_to(xi[SUB - 1:SUB, :], (SUB, ns))
            return 0
        lax.fori_loop(0, tm // SUB, group, 0)

        xb = _bf(xs_ref[...])
        xb_ref[...] = xb
        y = jnp.dot(xb, mc_ref[...], preferred_element_type=F32) + d_ref[...] * uv
        y_ref[...] = y
        gy_ref[...] = _gelu(y).astype(gy_ref.dtype)

    c = u.shape[1]
    return pl.pallas_call(
        body, grid=(t // tm,),
        in_specs=[pl.BlockSpec((tm, c), lambda i: (i, 0)), pl.BlockSpec(mb.shape, lambda i: (0, 0)),
                  pl.BlockSpec(mc.shape, lambda i: (0, 0)), pl.BlockSpec(pw.shape, lambda i: (0, 0, 0, 0)),
                  pl.BlockSpec((1, c), lambda i: (0, 0))],
        out_specs=[pl.BlockSpec((tm, c), lambda i: (i, 0)), pl.BlockSpec((tm, c), lambda i: (i, 0)),
                   pl.BlockSpec((tm, 2 * ns), lambda i: (i, 0)), pl.BlockSpec((tm, 2 * ns), lambda i: (i, 0))],
        out_shape=[S((t, c), BF16), S((t, c), F32), S((t, 2 * ns), F32), S((t, 2 * ns), BF16)],
        scratch_shapes=[pltpu.VMEM((2, SUB, ns), F32)],
        compiler_params=_cp("arbitrary"), name=name)(u, mb, mc, pw, dskip)


def s5_bwd(name, dgy, y, u, xs, mct, mbt, qw, dskip):
    t, c = u.shape
    tm = _tile(t, S5_ROWS)
    nt = t // tm
    ns = N_STATE
    ng = tm // SUB

    def body(dgy_ref, y_ref, u_ref, xs_ref, mct_ref, mbt_ref, qw_ref, d_ref,
             du_ref, dy_ref, lb_ref, da_ref, dd_ref, lam, carry):
        @pl.when(pl.program_id(0) == 0)
        def _():
            carry[...] = jnp.zeros(carry.shape, F32)
            da_ref[...] = jnp.zeros(da_ref.shape, F32)
            dd_ref[...] = jnp.zeros(dd_ref.shape, F32)

        uv = u_ref[...]
        dy = dgy_ref[...] * _gelu_grad(y_ref[...])
        dyb = _bf(dy)
        dy_ref[...] = dyb
        dd_ref[...] += jnp.sum(dy * uv, axis=0, keepdims=True)
        lam[...] = jnp.dot(dyb, mct_ref[...], preferred_element_type=F32)
        last_row = lax.broadcasted_iota(jnp.int32, (SUB, ns), 0) == SUB - 1

        def group(j, _):
            i = ng - 1 - j
            r0 = pl.multiple_of(i * SUB, SUB)
            lr = lam[pl.ds(r0, SUB), 0:ns]
            li = lam[pl.ds(r0, SUB), ns:2 * ns]
            for k, s in enumerate((1, 2, 4)):
                lr, li = _cmul_add(lr, li, qw_ref[k, 0], qw_ref[k, 1],
                                   pltpu.roll(lr, SUB - s, 0), pltpu.roll(li, SUB - s, 0))
            cr, ci = carry[0], carry[1]
            lr, li = _cmul_add(lr, li, qw_ref[3, 0], qw_ref[3, 1], cr, ci)
            lam[pl.ds(r0, SUB), 0:ns] = lr
            lam[pl.ds(r0, SUB), ns:2 * ns] = li
            carry[0] = jnp.broadcast_to(lr[0:1, :], (SUB, ns))
            carry[1] = jnp.broadcast_to(li[0:1, :], (SUB, ns))
            nr = jnp.where(last_row, cr, pltpu.roll(lr, SUB - 1, 0))
            ni = jnp.where(last_row, ci, pltpu.roll(li, SUB - 1, 0))
            xr = xs_ref[pl.ds(r0, SUB), 0:ns]
            xi = xs_ref[pl.ds(r0, SUB), ns:2 * ns]
            da_ref[0] += nr * xr + ni * xi
            da_ref[1] += ni * xr - nr * xi
            return 0
        lax.fori_loop(0, ng, group, 0)

        lb = _bf(lam[...])
        lb_ref[...] = lb
        du_ref[...] = (jnp.dot(lb, mbt_ref[...], preferred_element_type=F32) + d_ref[...] * dy).astype(du_ref.dtype)

    rev = lambda i: (nt - 1 - i, 0)
    return pl.pallas_call(
        body, grid=(nt,),
        in_specs=[pl.BlockSpec((tm, c), rev), pl.BlockSpec((tm, c), rev), pl.BlockSpec((tm, c), rev),
                  pl.BlockSpec((tm, 2 * ns), rev),
                  pl.BlockSpec(mct.shape, lambda i: (0, 0)), pl.BlockSpec(mbt.shape, lambda i: (0, 0)),
                  pl.BlockSpec(qw.shape, lambda i: (0, 0, 0, 0)), pl.BlockSpec((1, c), lambda i: (0, 0))],
        out_specs=[pl.BlockSpec((tm, c), rev), pl.BlockSpec((tm, c), rev), pl.BlockSpec((tm, 2 * ns), rev),
                   pl.BlockSpec((2, SUB, ns), lambda i: (0, 0, 0)), pl.BlockSpec((1, c), lambda i: (0, 0))],
        out_shape=[S((t, c), BF16), S((t, c), BF16), S((t, 2 * ns), BF16), S((2, SUB, ns), F32), S((1, c), F32)],
        scratch_shapes=[pltpu.VMEM((tm, 2 * ns), F32), pltpu.VMEM((2, SUB, ns), F32)],
        compiler_params=_cp("arbitrary"), name=name)(dgy, y, u, xs, mct, mbt, qw, dskip)


def _first(accs, *_):
    return [accs[0]]


def _rms_bwd_epi(accs, xv, base, rv, g):
    dv = accs[0]
    w = dv * g
    xh = xv * rv
    dx = base + rv * (w - xh * jnp.mean(w * xh, axis=-1, keepdims=True))
    return [dx, dx, jnp.sum(dv * xh, axis=0, keepdims=True)]


def mm_rms_bwd(name, pairs, x, r, gain, dres):
    t, d = x.shape
    return mm_nn(name, t, d, pairs, 1, _rms_bwd_epi, [F32, BF16], tiled=[x, dres], cols=[r], rowv=[gain], sums=[(1, d)])


def _add_res(accs, res):
    return [accs[0] + res]


def even_fwd(x, w):
    t = x.shape[0]
    hn, r = rms_fwd("e_norm_f", x, w["e_norm"])
    (proj,) = mm_nn("e_in_f", t, IN_WIDTH, [(hn, w["e_w_in_t"], 0, "t")], 1, _first, [F32])
    out_a = gmlp_fwd("e_gmlp_f", proj, w["e_gmlp_w"], w["e_gmlp_b"])
    hc = conv_fwd("e_conv_f", proj, w["e_conv_w"], w["e_conv_b"])
    out_b = ln_silu_fwd("e_ln_f", hc, w["e_conv_ln_g"], w["e_conv_ln_b"])
    (x1,) = mm_nn("e_out_f", t, D_MODEL, [(out_a, (w["e_w_out"], 0), 0), (out_b, (w["e_w_out"], 1), 0)],
                  1, _add_res, [F32], tiled=[x])
    return x1, (x, hn, r, proj, out_a, hc, out_b)


def even_bwd_mixers(dxb, saved, w):
    x, hn, r, proj, out_a, hc, out_b = saved
    t = x.shape[0]
    (dcat,) = mm_nn("e_out_b", t, D_MODEL, [(dxb, w["e_w_out"], 0, "t")], 1, _first, [F32])
    g_w_out = jnp.concatenate([mm_tn("e_out_wa", out_a, dxb), mm_tn("e_out_wb", out_b, dxb)], axis=0)
    dab, g_gw, g_gb = gmlp_bwd("e_gmlp_b", proj, dcat, w["e_gmlp_w"], w["e_gmlp_b"])
    dhc, g_lg, g_lb = ln_silu_bwd("e_ln_b", hc, dcat, w["e_conv_ln_g"], w["e_conv_ln_b"])
    dba, dbg, g_cw, g_cb = conv_bwd("e_conv_b", proj, dhc, w["e_conv_w"])
    g_w_in_t = jnp.concatenate([mm_tn("e_in_w0", dab, hn), mm_tn("e_in_w1", dba, hn), mm_tn("e_in_w2", dbg, hn)], axis=0)
    grads = dict(e_w_in_t=g_w_in_t, e_gmlp_w=g_gw[None], e_gmlp_b=g_gb.reshape(1, A_GROUPS, GMLP_BLOCK),
                 e_conv_w=g_cw[None], e_conv_b=g_cb, e_conv_ln_g=g_lg, e_conv_ln_b=g_lb, e_w_out=g_w_out)
    return (dab, dba, dbg), grads


def even_bwd_input(dx, dproj, saved, w):
    x, _, r = saved[:3]
    dab, dba, dbg = dproj
    w_in_t = w["e_w_in_t"]
    return mm_rms_bwd("e_in_b", [(dab, (w_in_t, 0), 0), (dba, (w_in_t, 2), 0), (dbg, (w_in_t, 3), 0)], x, r, w["e_norm"], dx)


def odd_fwd(x, w, consts):
    t = x.shape[0]
    _, mb, mc, pw, _ = consts
    hn, r = rms_fwd("o_norm_f", x, w["o_norm"])
    (u,) = mm_nn("o_in_f", t, C_WIDTH, [(hn, w["o_w_in"], 0)], 1, _first, [F32])
    gy, y, xs, xsb = s5_fwd("o_s5_f", u, _bf(mb), _bf(mc), pw, w["o_d"])
    w_out_t = w["o_w_out_t"]

    def epi(accs, res):
        return [res + accs[0] * _sigmoid(accs[1]), accs[0], accs[1]]

    x1, o1, o2 = mm_nn("o_out_f", t, D_MODEL, [(gy, (w_out_t, 0), 0, "t"), (gy, (w_out_t, D_MODEL), 1, "t")], 2, epi,
                       [F32, BF16, BF16], tiled=[x])
    return x1, (x, hn, r, u, gy, y, xs, xsb, o1, o2)


def odd_bwd(dx, dxb, saved, w, consts, consts_vjp):
    x, hn, r, u, gy, y, xs, xsb, o1, o2 = saved
    t = x.shape[0]
    _, mb, mc, _, qw = consts

    def gate_bwd(dv, a, b):
        a = a.astype(F32)
        sg = _sigmoid(b.astype(F32))
        return [jnp.concatenate([dv * sg, dv * a * sg * (1.0 - sg)], axis=1)], []

    (do12,) = rows_call("o_gate_b", gate_bwd, [dx, o1, o2], [], [(2 * D_MODEL, BF16)], [])
    (dgy,) = mm_nn("o_out_b", t, C_WIDTH, [(do12, w["o_w_out_t"], 0)], 1, _first, [F32])
    g_w_out_t = mm_tn("o_out_w", do12, gy)
    du, dyb, lamb, da8, g_d = s5_bwd("o_s5_b", dgy, y, u, xs, _bf(mc.T), _bf(mb.T), qw, w["o_d"])
    d_mb = mm_tn("o_s5_wb", u, lamb, out_dtype=F32)
    d_mc = mm_tn("o_s5_wc", xsb, dyb, out_dtype=F32)
    g_lr, g_li, g_dt, g_br, g_bi, g_cr, g_ci = consts_vjp((jnp.sum(da8, axis=1), d_mb, d_mc))
    g_w_in = mm_tn("o_in_w", hn, du)
    dx0, dx0b, g_norm = mm_rms_bwd("o_in_b", [(du, w["o_w_in"], 0, "t")], x, r, w["o_norm"], dx)
    grads = dict(o_norm=g_norm, o_w_in=g_w_in, o_lam_re=g_lr[None], o_lam_im=g_li[None], o_log_dt=g_dt[None],
                 o_b_re=g_br[None], o_b_im=g_bi[None], o_c_re=g_cr[None], o_c_im=g_ci[None], o_d=g_d, o_w_out_t=g_w_out_t)
    return dx0, dx0b, grads


def ca_fwd(i, x, mem, w):
    t, m = x.shape[0], mem.shape[0]
    xn, r = rms_fwd(f"ca{i}_norm_f", x, w["ca_norm"][i:i + 1])
    mn, rm = rms_fwd(f"ca{i}_mnorm_f", mem, w["ca_mem_norm"][i:i + 1])
    (q,) = mm_nn(f"ca{i}_q_f", t, D_MODEL, [(xn, w["ca_wq"][i], 0)], 1, _first, [BF16])
    k, v = mm_nn(f"ca{i}_kv_f", m, D_MODEL, [(mn, w["ca_wk"][i], 0), (mn, w["ca_wv"][i], 1)], 2,
                 lambda accs: [accs[0], accs[1]], [BF16, BF16])
    o = attn_fwd(f"ca{i}_attn_f", q, k, v)
    (x1,) = mm_nn(f"ca{i}_o_f", t, D_MODEL, [(o, w["ca_wo"][i], 0)], 1, _add_res, [F32], tiled=[x])
    return x1, (x, xn, r, mn, rm, q, k, v, o)


def ca_bwd(i, dx, dxb, saved, mem, w):
    x, xn, r, mn, rm, q, k, v, o = saved
    t, m = x.shape[0], mem.shape[0]
    (do,) = mm_nn(f"ca{i}_o_b", t, D_MODEL, [(dxb, w["ca_wo"][i], 0, "t")], 1, _first, [BF16])
    g_wo = mm_tn(f"ca{i}_o_w", o, dxb)
    dq, dk, dv = attn_bwd(f"ca{i}_attn_b", q, k, v, do)
    g_wq = mm_tn(f"ca{i}_q_w", xn, dq)
    g_wk = mm_tn(f"ca{i}_k_w", mn, dk)
    g_wv = mm_tn(f"ca{i}_v_w", mn, dv)
    (dmn,) = mm_nn(f"ca{i}_kv_b", m, D_MODEL, [(dk, w["ca_wk"][i], 0, "t"), (dv, w["ca_wv"][i], 0, "t")], 1, _first, [F32])
    g_mnorm = rms_bwd_gain_only(f"ca{i}_mnorm_b", dmn, mem, rm)
    dx0, dx0b, g_norm = mm_rms_bwd(f"ca{i}_q_b", [(dq, w["ca_wq"][i], 0, "t")], x, r, w["ca_norm"][i:i + 1], dx)
    return dx0, dx0b, dict(ca_norm=g_norm, ca_mem_norm=g_mnorm, ca_wq=g_wq, ca_wk=g_wk, ca_wv=g_wv, ca_wo=g_wo)


def ffn_fwd(i, x, w):
    t = x.shape[0]
    xn, r = rms_fwd(f"ffn{i}_norm_f", x, w["ffn_norm"][i:i + 1])

    def epi(accs):
        g, u = accs
        return [g, u, g * _sigmoid(g) * u]

    g, u, h = mm_nn(f"ffn{i}_up_f", t, FFN_HIDDEN, [(xn, w["ffn_w_gate_t"][i], 0, "t"), (xn, w["ffn_w_up_t"][i], 1, "t")],
                    2, epi, [BF16, BF16, BF16])
    (x1,) = mm_nn(f"ffn{i}_down_f", t, D_MODEL, [(h, w["ffn_w_down"][i], 0)], 1, _add_res, [F32], tiled=[x])
    return x1, (x, xn, r, g, u, h)


def ffn_bwd(i, dx, dxb, saved, w):
    x, xn, r, g, u, h = saved
    t = x.shape[0]

    def epi(accs, gv, uv):
        dh = accs[0]
        gv = gv.astype(F32)
        uv = uv.astype(F32)
        s = _sigmoid(gv)
        return [dh * uv * s * (1.0 + gv * (1.0 - s)), dh * gv * s]

    dg, du = mm_nn(f"ffn{i}_down_b", t, FFN_HIDDEN, [(dxb, w["ffn_w_down"][i], 0, "t")], 1, epi, [BF16, BF16], tiled=[g, u])
    g_wd = mm_tn(f"ffn{i}_down_w", h, dxb)
    g_wg_t = mm_tn(f"ffn{i}_gate_w", dg, xn)
    g_wu_t = mm_tn(f"ffn{i}_up_w", du, xn)
    dx0, dx0b, g_norm = mm_rms_bwd(f"ffn{i}_up_b", [(dg, w["ffn_w_gate_t"][i], 0), (du, w["ffn_w_up_t"][i], 0)], x, r,
                                   w["ffn_norm"][i:i + 1], dx)
    return dx0, dx0b, dict(ffn_norm=g_norm, ffn_w_gate_t=g_wg_t, ffn_w_up_t=g_wu_t, ffn_w_down=g_wd)


_S5_PARAMS = ("o_lam_re", "o_lam_im", "o_log_dt", "o_b_re", "o_b_im", "o_c_re", "o_c_im")


def local_step(x, mem, target, w, fetch=None, on_grads=None):
    def consts_fn(*p):
        a, mb, mc = s5_constants(*p)
        return a, mb, mc

    (a, mb, mc), consts_vjp = jax.vjp(consts_fn, *[w[k] for k in _S5_PARAMS])
    consts = (a, mb, mc, _scan_powers(a, False), _scan_powers(a, True))

    def need(stage, after):
        if fetch is not None:
            for k, v in fetch(stage, after).items():
                if isinstance(k, tuple):
                    w.setdefault(k[0], {})[k[1]] = v
                else:
                    w[k] = v

    need(0, x)
    x1, s_e = even_fwd(x, w)
    need(1, x1)
    x2, s_c0 = ca_fwd(0, x1, mem, w)
    need(2, x2)
    x3, s_f0 = ffn_fwd(0, x2, w)
    x4, s_o = odd_fwd(x3, w, consts)
    need(3, x4)
    x5, s_c1 = ca_fwd(1, x4, mem, w)
    x6, s_f1 = ffn_fwd(1, x5, w)
    dx, dxb, g_final, loss = final_loss("final_loss", x6, w["final_norm"], target)

    def emit(stage, carry, plain, layered=None, layer=0):
        if on_grads is None:
            return carry
        out = dict(plain)
        out.update({(k, layer): v for k, v in (layered or {}).items()})
        return on_grads(stage, out, list(carry))

    dx, dxb, g_f1 = ffn_bwd(1, dx, dxb, s_f1, w)
    dx, dxb = emit(0, (dx, dxb), {}, g_f1, 1)
    dx, dxb, g_c1 = ca_bwd(1, dx, dxb, s_c1, mem, w)
    dx, dxb, g_o = odd_bwd(dx, dxb, s_o, w, consts, consts_vjp)
    dx, dxb = emit(1, (dx, dxb), g_o, g_c1, 1)
    dx, dxb, g_f0 = ffn_bwd(0, dx, dxb, s_f0, w)
    dx, dxb = emit(2, (dx, dxb), {}, g_f0, 0)
    dx, dxb, g_c0 = ca_bwd(0, dx, dxb, s_c0, mem, w)
    dx, dxb = emit(3, (dx, dxb), {}, g_c0, 0)
    dproj, g_e = even_bwd_mixers(dxb, s_e, w)
    dproj = emit(4, dproj, {**g_e, "o_norm": g_o["o_norm"], "o_d": g_o["o_d"]})
    dx, dxb, g_e["e_norm"] = even_bwd_input(dx, dproj, s_e, w)

    grads = dict(g_e)
    grads.update(g_o)
    for g0, g1 in ((g_c0, g_c1), (g_f0, g_f1)):
        for k in g0:
            grads[k] = jnp.concatenate([g0[k], g1[k]], axis=0) if k.endswith("norm") else (g0[k], g1[k])
    grads["final_norm"] = g_final
    return loss, dx, grads


def _group(axes):
    pos = {a: lax.axis_index(a) for a in ("x", "y", "c")}
    me = 0
    for a in axes:
        me = me * 2 + pos[a]
    peers = []
    for mask in range(1, 2 ** len(axes)):
        peer = dict(pos)
        for bit, a in enumerate(axes):
            if (mask >> (len(axes) - 1 - bit)) & 1:
                peer[a] = 1 - pos[a]
        idx = 0
        for a in axes:
            idx = idx * 2 + peer[a]
        peers.append((idx, (peer["x"], peer["y"], peer["c"])))
    return me, peers


def _sibling():
    x, y, c = lax.axis_index("x"), lax.axis_index("y"), lax.axis_index("c")
    return c, (x, y, 1 - c)


_HBM =pl.BlockSpec(memory_space=pltpu.HBM)
_SEM = pl.BlockSpec(memory_space=pltpu.SEMAPHORE)
_EFFECT = pltpu.SideEffectType.DATAFLOW_SIDE_EFFECTING


def gather_ici_start(name, groups):
    flat = [b for g in groups for b in g]
    sizes = [len(g) for g in groups]
    k_ops, n_g = len(flat), len(groups)
    lands = [lax.empty((4, 2) + tuple(b.shape), b.dtype) for b in flat]

    def body(*refs):
        src, land = refs[:k_ops], refs[k_ops:2 * k_ops]
        sems = refs[2 * k_ops:2 * k_ops + 3 * n_g]
        token = refs[-1]
        me, peers = _group(("x", "y"))
        core = lax.axis_index("c")
        i = 0
        for g in range(n_g):
            send, recv, loc = sems[3 * g:3 * g + 3]
            for j in range(sizes[g]):
                pltpu.make_async_copy(src[i], land[i].at[me, core], loc.at[j]).start()
                for k, (_, dev) in enumerate(peers):
                    pltpu.make_async_remote_copy(src_ref=src[i], dst_ref=land[i].at[me, core], send_sem=send.at[3 * j + k],
                                                 recv_sem=recv.at[3 * j + k], device_id=dev, device_id_type=MESH).start()
                i += 1
        token[...] = jnp.zeros(token.shape, token.dtype)

    sem_shapes = []
    for s in sizes:
        sem_shapes += [pltpu.SemaphoreType.DMA((3 * s,)), pltpu.SemaphoreType.DMA((3 * s,)), pltpu.SemaphoreType.DMA((s,))]
    thru = [pltpu.HBM(a.shape, a.dtype) for a in flat + lands]
    outs = pl.pallas_call(
        body, name=name, out_shape=tuple(sem_shapes) + tuple(thru) + (S((8, LANES), F32),),
        in_specs=[_HBM] * (2 * k_ops), out_specs=[_SEM] * (3 * n_g) + [_HBM] * (2 * k_ops) + [pl.BlockSpec(memory_space=pltpu.VMEM)],
        input_output_aliases={i: 3 * n_g + i for i in range(2 * k_ops)},
        compiler_params=pltpu.CompilerParams(has_side_effects=_EFFECT),
    )(*[pltpu.with_memory_space_constraint(a, pltpu.HBM) for a in flat + lands])
    sems = [tuple(outs[3 * g:3 * g + 3]) for g in range(n_g)]
    srcs_thru, lands_thru, off = [], [], 3 * n_g
    for s in sizes:
        srcs_thru.append(list(outs[off:off + s]))
        off += s
    for s in sizes:
        lands_thru.append(list(outs[off:off + s]))
        off += s
    return sems, srcs_thru, lands_thru, outs[-1]


def gather_ici_wait(name, srcs, lands, sems, after):
    n = len(srcs)

    def body(*refs):
        src, land = refs[:n], refs[n:2 * n]
        send, recv, loc = refs[2 * n:2 * n + 3]
        me, peers = _group(("x", "y"))
        core = lax.axis_index("c")
        for j in range(n):
            for k, (idx, dev) in enumerate(peers):
                cp = pltpu.make_async_remote_copy(src_ref=src[j], dst_ref=land[j].at[idx, core], send_sem=send.at[3 * j + k],
                                                  recv_sem=recv.at[3 * j + k], device_id=dev, device_id_type=MESH)
                cp.wait_send()
                cp.wait_recv()
            pltpu.make_async_copy(src[j], land[j].at[me, core], loc.at[j]).wait()

    outs = pl.pallas_call(
        body, name=name, out_shape=tuple(pltpu.HBM(a.shape, a.dtype) for a in list(srcs) + list(lands)),
        in_specs=[_HBM] * (2 * n) + [_SEM] * 3 + [ANY], out_specs=[_HBM] * (2 * n),
        input_output_aliases={i: i for i in range(2 * n)},
        compiler_params=pltpu.CompilerParams(has_side_effects=_EFFECT),
    )(*srcs, *lands, *sems, after)
    return list(outs[n:])


def gather_d2d(name, bufs):
    k_ops = len(bufs)

    def body(*refs):
        in_refs, out_refs = refs[:k_ops], refs[k_ops:2 * k_ops]
        send_sems, recv_sems = refs[2 * k_ops:]
        core, sib = _sibling()
        sent, landed = [], []
        for i in range(k_ops):
            cp = pltpu.make_async_remote_copy(src_ref=in_refs[i].at[:, core], dst_ref=out_refs[i].at[:, core],
                                              send_sem=send_sems.at[i], recv_sem=recv_sems.at[i], device_id=sib, device_id_type=MESH)
            cp.start()
            sent.append(cp)
            landed.append(pltpu.make_async_remote_copy(src_ref=in_refs[i].at[:, core], dst_ref=out_refs[i].at[:, 1 - core],
                                                       send_sem=send_sems.at[i], recv_sem=recv_sems.at[i],
                                                       device_id=sib, device_id_type=MESH))
        for cp in landed:
            cp.wait_recv()
        for cp in sent:
            cp.wait_send()

    return pl.pallas_call(
        body, in_specs=[ANY] * k_ops, out_specs=[ANY] * k_ops, out_shape=[S(b.shape, b.dtype) for b in bufs],
        input_output_aliases={i: i for i in range(k_ops)},
        scratch_shapes=[pltpu.SemaphoreType.DMA((k_ops,)), pltpu.SemaphoreType.DMA((k_ops,))],
        name=name)(*bufs)


def scatter_d2d(name, pack):
    q, _, rows, c = pack.shape

    def body(in_ref, out_ref, send_sem, recv_sem):
        core, sib = _sibling()
        cp = pltpu.make_async_remote_copy(src_ref=in_ref.at[:, 1 - core], dst_ref=out_ref, send_sem=send_sem, recv_sem=recv_sem,
                                          device_id=sib, device_id_type=MESH)
        cp.start()
        cp.wait_recv()
        cp.wait_send()

    return pl.pallas_call(
        body, in_specs=[ANY], out_specs=ANY, out_shape=S((q, rows, c), pack.dtype),
        scratch_shapes=[pltpu.SemaphoreType.DMA, pltpu.SemaphoreType.DMA], name=name)(pack)


def scatter_ici_start(name, arr, carry):
    land = lax.empty(arr.shape, arr.dtype)
    n_c = len(carry)

    def body(*refs):
        in_ref, land_ref = refs[0], refs[1]
        send, recv = refs[2 + n_c], refs[3 + n_c]
        me, peers = _group(("x", "y"))
        for k, (idx, dev) in enumerate(peers):
            pltpu.make_async_remote_copy(src_ref=in_ref.at[idx], dst_ref=land_ref.at[me], send_sem=send.at[k], recv_sem=recv.at[k],
                                         device_id=dev, device_id_type=MESH).start()

    thru = [arr, land] + list(carry)
    outs = pl.pallas_call(
        body, name=name,
        out_shape=(pltpu.SemaphoreType.DMA((3,)), pltpu.SemaphoreType.DMA((3,))) + tuple(pltpu.HBM(a.shape, a.dtype) for a in thru),
        in_specs=[_HBM] * len(thru), out_specs=[_SEM, _SEM] + [_HBM] * len(thru),
        input_output_aliases={i: 2 + i for i in range(len(thru))},
        compiler_params=pltpu.CompilerParams(has_side_effects=_EFFECT),
    )(*[pltpu.with_memory_space_constraint(a, pltpu.HBM) for a in thru])
    return (outs[0], outs[1]), outs[2], outs[3], list(outs[4:])


def scatter_ici_wait(name, arr, land, sems, after):
    def body(in_ref, land_ref, send, recv, after_ref, in_thru, land_thru):
        _, peers = _group(("x", "y"))
        for k, (idx, dev) in enumerate(peers):
            cp = pltpu.make_async_remote_copy(src_ref=in_ref.at[idx], dst_ref=land_ref.at[idx], send_sem=send.at[k],
                                              recv_sem=recv.at[k], device_id=dev, device_id_type=MESH)
            cp.wait_send()
            cp.wait_recv()

    outs = pl.pallas_call(
        body, name=name, out_shape=(pltpu.HBM(arr.shape, arr.dtype), pltpu.HBM(arr.shape, arr.dtype)),
        in_specs=[_HBM, _HBM, _SEM, _SEM, ANY], out_specs=[_HBM, _HBM], input_output_aliases={0: 0, 1: 1},
        compiler_params=pltpu.CompilerParams(has_side_effects=_EFFECT),
    )(arr, land, sems[0], sems[1], after)
    return outs[0], outs[1]


def _row_tile(rows, cap=512):
    return next(t for t in range(cap - cap % 16, 0, -16) if rows % t == 0)


def sum_pair(name, pack, recv, core):
    q, rows, c = recv.shape
    tr = _row_tile(rows)

    def body(core_ref, a_ref, b_ref, o_ref):
        o_ref[...] = (a_ref[...].astype(F32) + b_ref[...].astype(F32)).astype(o_ref.dtype)

    spec = pltpu.PrefetchScalarGridSpec(
        num_scalar_prefetch=1, grid=(q, rows // tr),
        in_specs=[pl.BlockSpec((None, None, tr, c), lambda j, i, core: (j, core[0], i, 0)),
                  pl.BlockSpec((None, tr, c), lambda j, i, core: (j, i, 0))],
        out_specs=pl.BlockSpec((None, tr, c), lambda j, i, core: (j, i, 0)))
    return pl.pallas_call(body, grid_spec=spec, out_shape=S(recv.shape, recv.dtype),
                          compiler_params=_cp("parallel", "parallel"), name=name)(core, pack, recv)


def sum_quad(name, own, recv, chip):
    _, rows, c = recv.shape
    tr = _row_tile(rows)

    def body(chip_ref, a_ref, r1_ref, r2_ref, r3_ref, o_ref):
        o_ref[...] = ((a_ref[...].astype(F32) + r1_ref[...].astype(F32)) + r2_ref[...].astype(F32)) + r3_ref[...].astype(F32)

    def slot(mask):
        return pl.BlockSpec((None, tr, c), lambda i, chip, mask=mask: (jnp.bitwise_xor(chip[0], mask), i, 0))

    spec = pltpu.PrefetchScalarGridSpec(
        num_scalar_prefetch=1, grid=(rows // tr,), in_specs=[slot(0), slot(1), slot(2), slot(3)],
        out_specs=pl.BlockSpec((tr, c), lambda i, chip: (i, 0)))
    return pl.pallas_call(body, grid_spec=spec, out_shape=S((rows, c), F32),
                          compiler_params=_cp("parallel"), name=name)(chip, own, recv, recv, recv)


def adamw_native(name, g, w, m, v, tr=512):
    shape = w.shape
    cols = shape[-1]
    rows = w.size // cols
    tr = _tile(rows, tr) if rows % 8 == 0 else rows
    c1 = 1.0 - ADAM_B1 ** ADAM_STEP
    c2 = 1.0 - ADAM_B2 ** ADAM_STEP

    def body(g_ref, w_ref, m_ref, v_ref, d_ref, m2_ref, v2_ref):
        gv = g_ref[...]
        m2 = ADAM_B1 * m_ref[...] + (1.0 - ADAM_B1) * gv
        v2 = ADAM_B2 * v_ref[...] + (1.0 - ADAM_B2) * (gv * gv)
        m2_ref[...] = m2
        v2_ref[...] = v2
        d_ref[...] = -ADAM_LR * ((m2 / c1) / (jnp.sqrt(v2 / c2) + ADAM_EPS) + ADAM_WD * w_ref[...])

    row = pl.BlockSpec((tr, cols), lambda i: (i, 0))
    outs = pl.pallas_call(body, grid=(rows // tr,), in_specs=[row] * 4, out_specs=[row] * 3,
                          out_shape=[S((rows, cols), F32)] * 3, compiler_params=_cp("parallel"),
                          name=name)(*[a.reshape(rows, cols) for a in (g, w, m, v)])
    return tuple(o.reshape(shape) for o in outs)


def adamw_call(name, slots, w, m, v, tr=1024):
    n, r, c = slots.shape
    tr = _tile(r, tr)
    c1 = 1.0 - ADAM_B1 ** ADAM_STEP
    c2 = 1.0 - ADAM_B2 ** ADAM_STEP

    def body(s_ref, w_ref, m_ref, v_ref, g_ref, d_ref, m2_ref, v2_ref):
        g = s_ref[0].astype(F32)
        for j in range(1, n):
            g = g + s_ref[j].astype(F32)
        m2 = ADAM_B1 * m_ref[...] + (1.0 - ADAM_B1) * g
        v2 = ADAM_B2 * v_ref[...] + (1.0 - ADAM_B2) * (g * g)
        g_ref[...] = g
        m2_ref[...] = m2
        v2_ref[...] = v2
        d_ref[...] = -ADAM_LR * ((m2 / c1) / (jnp.sqrt(v2 / c2) + ADAM_EPS) + ADAM_WD * w_ref[...])

    row = pl.BlockSpec((tr, c), lambda i: (i, 0))
    return pl.pallas_call(body, grid=(r // tr,), in_specs=[pl.BlockSpec((n, tr, c), lambda i: (0, i, 0)), row, row, row],
                          out_specs=[row, row, row, row], out_shape=[S((r, c), F32)] * 4,
                          compiler_params=_cp("parallel"), name=name)(slots, w, m, v)


_REPLICATED = ("e_norm", "e_gmlp_w", "e_gmlp_b", "e_conv_b", "e_conv_ln_g", "e_conv_ln_b", "o_lam_re", "o_lam_im", "o_log_dt",
               "o_b_re", "o_b_im", "o_c_re", "o_c_im", "ca_norm", "ca_mem_norm", "ffn_norm", "final_norm")
_ORDER = ("e_norm", "e_w_in", "e_gmlp_w", "e_gmlp_b", "e_conv_w", "e_conv_b", "e_conv_ln_g", "e_conv_ln_b", "e_w_out",
          "o_norm", "o_w_in", "o_lam_re", "o_lam_im", "o_log_dt", "o_b_re", "o_b_im", "o_c_re", "o_c_im", "o_d", "o_w_out",
          "ca_norm", "ca_mem_norm", "ca_wq", "ca_wk", "ca_wv", "ca_wo", "ffn_norm", "ffn_w_gate", "ffn_w_up", "ffn_w_down",
          "final_norm")


def _rows128(a, multiple=8):
    flat = a.reshape(-1)
    rows = -(-flat.shape[0] // (LANES * multiple)) * multiple
    return jnp.pad(flat, (0, rows * LANES - flat.shape[0])).reshape(rows, LANES)


def _shard(full, axis):
    s = full.shape
    return jnp.moveaxis(full.reshape(s[:axis] + (N_DEV, s[axis] // N_DEV) + s[axis + 1:]), axis, 0)


_UNITS = (("e_w_in", 0, True), ("e_w_out", 0, False), ("o_w_in", 0, False), ("o_w_out", 0, True),
          *[(n, i, False) for n in ("ca_wq", "ca_wk", "ca_wv", "ca_wo") for i in (0, 1)],
          *[(n, i, tr) for n, tr in (("ffn_w_gate", True), ("ffn_w_up", True), ("ffn_w_down", False)) for i in (0, 1)])
_LAYERED = ("ca_wq", "ca_wk", "ca_wv", "ca_wo", "ffn_w_gate", "ffn_w_up", "ffn_w_down")
_SMALL_SHARDED = (("e_conv_w", 2), ("o_norm", 1), ("o_d", 1))
RS_ROW = 1024


def _unit_key(name, tr):
    return name + "_t" if tr else name


def _stage_of(name, layer):
    if name.startswith("e_"):
        return 0
    if name.startswith("o_"):
        return 2
    if name.startswith("ca_"):
        return 1 if layer == 0 else 3
    return 2 if layer == 0 else 3


def weight_fetcher(local):
    groups, meta = [[] for _ in range(4)], [[] for _ in range(4)]
    for name, layer, tr in _UNITS:
        blk = local[name][layer]
        st = _stage_of(name, layer)
        groups[st].append(_bf(blk.T if tr else blk))
        meta[st].append((name, layer, tr))
    small = jnp.concatenate([local[name].reshape(-1) for name, _ in _SMALL_SHARDED])
    groups[0].append(_rows128(small))
    sems, srcs, lands, token = gather_ici_start("ag_w_start", groups)

    def fetch(stage, after):
        if stage == 0:
            after = token
        landed = gather_ici_wait(f"ag_w_wait{stage}", srcs[stage], lands[stage], sems[stage], after)
        bufs = gather_d2d(f"ag_w_d2d{stage}", landed)
        got = {}
        for (name, layer, tr), blk, buf in zip(meta[stage], groups[stage], bufs):
            arr = buf.reshape((N_DEV * blk.shape[0],) + tuple(blk.shape[1:]))
            if name in _LAYERED:
                got[(_unit_key(name, tr), layer)] = arr
            else:
                got[_unit_key(name, tr)] = arr
        if stage == 0:
            flat = bufs[-1].reshape(N_DEV, -1)
            off = 0
            for name, axis in _SMALL_SHARDED:
                blk = local[name]
                seg = flat[:, off:off + blk.size].reshape((N_DEV,) + blk.shape)
                off += blk.size
                seg = jnp.moveaxis(seg, 0, axis)
                got[name] = seg.reshape(seg.shape[:axis] + (-1,) + seg.shape[axis + 2:])
            got["e_conv_w"] = got["e_conv_w"][0]
        return got

    return fetch


def _grad_stage_of(name, layer):
    if name.startswith("e_"):
        return 4
    if name.startswith("o_"):
        return 1
    if name.startswith("ca_"):
        return 3 if layer == 0 else 1
    return 2 if layer == 0 else 0


GRAD_STAGES = 5
SMALL_ROWS = 16


def gradient_reducer(local, mom, var):
    core = lax.axis_index("c").astype(jnp.int32).reshape(1)
    chip = (2 * lax.axis_index("x") + lax.axis_index("y")).astype(jnp.int32).reshape(1)
    pending = []

    def start(stage, grads, carry):
        units = [u for u in _UNITS if _grad_stage_of(u[0], u[1]) == stage]
        parts, spans = [], []
        for name, layer, tr in units:
            key = _unit_key(name, tr)
            g = grads[(key, layer)] if name in _LAYERED else grads[key]
            part = g.reshape(4, 2, -1, RS_ROW)
            spans.append((part.shape[2], g.shape[0] // N_DEV, g.shape[1]))
            parts.append(part)
        if stage == GRAD_STAGES - 1:
            small = jnp.concatenate([_shard(grads[name], axis).reshape(N_DEV, -1) for name, axis in _SMALL_SHARDED], axis=1)
            small = jnp.pad(small, ((0, 0), (0, SMALL_ROWS * RS_ROW - small.shape[1])))
            parts.append(small.astype(BF16).reshape(4, 2, SMALL_ROWS, RS_ROW))
        pack = jnp.concatenate(parts, axis=2)
        from_sibling = scatter_d2d(f"rs_d2d{stage}", pack)
        chip_sum = sum_pair(f"rs_pair{stage}", pack, from_sibling, core)
        sems, own, land, carry = scatter_ici_start(f"rs_start{stage}", chip_sum, carry)
        pending.append((stage, units, spans, sems, own, land))
        return carry

    def finish(after):
        res, per_layer, small_flat = {}, {}, None
        for stage, units, spans, sems, own, land in pending:
            own, land = scatter_ici_wait(f"rs_wait{stage}", own, land, sems, after)
            total = sum_quad(f"rs_quad{stage}", own, land, chip)
            off = 0
            for (name, layer, tr), (rows, r, c) in zip(units, spans):
                g = total[off:off + rows].reshape(r, c)
                off += rows
                per_layer.setdefault(name, {})[layer] = g.T if tr else g
            if stage == GRAD_STAGES - 1:
                small_flat = total[off:off + SMALL_ROWS].reshape(-1)
        for name, by_layer in per_layer.items():
            g = jnp.stack([by_layer[i] for i in sorted(by_layer)]) if name in _LAYERED else by_layer[0][None]
            res[name] = (g,) + adamw_native("adamw_" + name, g, local[name], mom[name], var[name])
        off = 0
        for name, _ in _SMALL_SHARDED:
            blk = local[name]
            g = small_flat[off:off + blk.size].reshape(blk.shape)
            off += blk.size
            res[name] = (g,) + adamw_native("adamw_" + name, g, blk, mom[name], var[name])
        return res

    return start, finish


def _pack_replicated(src, last):
    return jnp.concatenate([_rows128(src[name]) for name in _REPLICATED] + [_rows128(last)], axis=0)


def replicated_start(grads, loss):
    sems, srcs, lands, token = gather_ici_start("ag_g_start", [[_pack_replicated(grads, loss)]])
    return sems[0], srcs[0], lands[0], token


def replicated_finish(handle, after, w, mom, var):
    sems, srcs, lands, _ = handle
    (buf,) = gather_d2d("ag_g_d2d", gather_ici_wait("ag_g_wait", srcs, lands, sems, after))
    zero = jnp.zeros((1, 1), F32)
    rows = srcs[0].shape[0]
    outs = adamw_call("adamw_replicated", buf.reshape(N_DEV, rows, LANES), _pack_replicated(w, zero),
                      _pack_replicated(mom, zero), _pack_replicated(var, zero), tr=rows)
    res, off = {}, 0
    for name in _REPLICATED:
        n = w[name].size
        nr = _rows128(w[name]).shape[0]
        res[name] = tuple(o[off:off + nr].reshape(-1)[:n].reshape(w[name].shape) for o in outs)
        off += nr
    return res, outs[0][off, 0]


def kernel(x, mem, e_norm, e_w_in, e_gmlp_w, e_gmlp_b, e_conv_w, e_conv_b, e_conv_ln_g, e_conv_ln_b, e_w_out, o_norm, o_w_in, o_lam_re, o_lam_im, o_log_dt, o_b_re, o_b_im, o_c_re, o_c_im, o_d, o_w_out, ca_norm, ca_mem_norm, ca_wq, ca_wk, ca_wv, ca_wo, ffn_norm, ffn_w_gate, ffn_w_up, ffn_w_down, final_norm, loss_target, m_e_norm, m_e_w_in, m_e_gmlp_w, m_e_gmlp_b, m_e_conv_w, m_e_conv_b, m_e_conv_ln_g, m_e_conv_ln_b, m_e_w_out, m_o_norm, m_o_w_in, m_o_lam_re, m_o_lam_im, m_o_log_dt, m_o_b_re, m_o_b_im, m_o_c_re, m_o_c_im, m_o_d, m_o_w_out, m_ca_norm, m_ca_mem_norm, m_ca_wq, m_ca_wk, m_ca_wv, m_ca_wo, m_ffn_norm, m_ffn_w_gate, m_ffn_w_up, m_ffn_w_down, m_final_norm, v_e_norm, v_e_w_in, v_e_gmlp_w, v_e_gmlp_b, v_e_conv_w, v_e_conv_b, v_e_conv_ln_g, v_e_conv_ln_b, v_e_w_out, v_o_norm, v_o_w_in, v_o_lam_re, v_o_lam_im, v_o_log_dt, v_o_b_re, v_o_b_im, v_o_c_re, v_o_c_im, v_o_d, v_o_w_out, v_ca_norm, v_ca_mem_norm, v_ca_wq, v_ca_wk, v_ca_wv, v_ca_wo, v_ffn_norm, v_ffn_w_gate, v_ffn_w_up, v_ffn_w_down, v_final_norm):
    given = dict(locals())
    local = {k: given[k] for k in _ORDER}
    mom = {k: given["m_" + k] for k in _ORDER}
    var = {k: given["v_" + k] for k in _ORDER}

    w = {}
    w.update({
        "e_norm": e_norm, "e_gmlp_w": e_gmlp_w[0], "e_gmlp_b": e_gmlp_b.reshape(A_GROUPS, GMLP_BLOCK, 1),
        "e_conv_b": e_conv_b, "e_conv_ln_g": e_conv_ln_g, "e_conv_ln_b": e_conv_ln_b,
        "o_lam_re": o_lam_re[0], "o_lam_im": o_lam_im[0], "o_log_dt": o_log_dt[0], "o_b_re": o_b_re[0], "o_b_im": o_b_im[0],
        "o_c_re": o_c_re[0], "o_c_im": o_c_im[0], "ca_norm": ca_norm, "ca_mem_norm": ca_mem_norm, "ffn_norm": ffn_norm,
        "final_norm": final_norm.reshape(1, D_MODEL),
    })
    start_reduce, finish_reduce = gradient_reducer(local, mom, var)
    loss_part, grad_x, grads = local_step(x[0], mem[0], loss_target[0], w, weight_fetcher(local), start_reduce)
    grads["final_norm"] = grads["final_norm"].reshape(D_MODEL)

    handle = replicated_start(grads, loss_part)
    res = finish_reduce(handle[3])
    rep, loss = replicated_finish(handle, res["ffn_w_down"][1], local, mom, var)
    res.update(rep)
    return (loss, grad_x[None], *[res[k][0] for k in _ORDER], *[res[k][1] for k in _ORDER],
            *[res[k][2] for k in _ORDER], *[res[k][3] for k in _ORDER])
```

```python
import jax
import jax.numpy as jnp
from jax import lax
from jax.experimental import pallas as pl
from jax.experimental.pallas import tpu as pltpu

F32 = jnp.float32
BF16 = jnp.bfloat16
S = jax.ShapeDtypeStruct

D_MODEL = 1024
A_WIDTH = 512
A_GROUPS = 4
GMLP_BLOCK = 128
CHUNK = 64
B_WIDTH = 512
IN_WIDTH = 2 * A_WIDTH + 2 * B_WIDTH
CONV_WIDTH = 31
CONV_PAD = 32
C_WIDTH = 512
C_GROUP_CH = 16
C_GROUPS = 32
C_STATE = 64
N_STATE = C_GROUPS * C_STATE
CA_HEADS = 4
CA_HEAD_DIM = 256
FFN_HIDDEN = 2816
EPS = 1e-6
ADAM_LR = 0.001
ADAM_B1 = 0.9
ADAM_B2 = 0.999
ADAM_EPS = 1e-08
ADAM_WD = 0.01
ADAM_STEP = 10
N_DEV = 8
LANES = 128
VMEM_LIMIT = 56 << 20
VMEM_BUDGET = 40 << 20
MM_TN_RESIDENT = 8 << 20
MESH = pl.DeviceIdType.MESH
ANY = pl.BlockSpec(memory_space=pl.ANY)


def _cp(*sem):
    return pltpu.CompilerParams(dimension_semantics=sem, vmem_limit_bytes=VMEM_LIMIT)


def _tile(n, pref):
    t = pref
    while n % t:
        t //= 2
    return t


def _bf(v):
    return v if v.dtype == BF16 else v.astype(BF16)


def _sigmoid(x):
    return 1.0 / (1.0 + jnp.exp(-x))


_GC = 0.7978845608028654


def _gelu(x):
    return 0.5 * x * (1.0 + jnp.tanh(_GC * (x + 0.044715 * x * x * x)))


def _gelu_grad(x):
    x2 = x * x
    t = jnp.tanh(_GC * (x + 0.044715 * x * x2))
    return 0.5 * (1.0 + t) + 0.5 * x * (1.0 - t * t) * _GC * (1.0 + 3.0 * 0.044715 * x2)


def _tspec(entry, tm):
    if isinstance(entry, tuple):
        arr, cb, width = entry
        return arr, pl.BlockSpec((tm, width), lambda i, cb=cb: (i, cb))
    return entry, pl.BlockSpec((tm, entry.shape[1]), lambda i: (i, 0))


def rows_call(name, fn, tiled, full, outs, accs, tm=256):
    pairs = [_tspec(e, tm) for e in tiled]
    arrs = [p[0] for p in pairs]
    rows = arrs[0].shape[0]
    tm = _tile(rows, tm)
    pairs = [_tspec(e, tm) for e in tiled]
    n_in = len(tiled) + len(full)
    n_out = len(outs)

    def body(*refs):
        vals = [r[...] for r in refs[:n_in]]
        o_refs = refs[n_in:n_in + n_out]
        a_refs = refs[n_in + n_out:]
        ov, av = fn(*vals)
        for r, v in zip(o_refs, ov):
            r[...] = v.astype(r.dtype)
        if a_refs:
            @pl.when(pl.program_id(0) == 0)
            def _():
                for r in a_refs:
                    r[...] = jnp.zeros(r.shape, r.dtype)
            for r, v in zip(a_refs, av):
                r[...] += v

    in_specs = [p[1] for p in pairs] + [pl.BlockSpec(a.shape, lambda i, nd=a.ndim: (0,) * nd) for a in full]
    out_specs = [pl.BlockSpec((tm, c), lambda i: (i, 0)) for c, _ in outs]
    out_specs += [pl.BlockSpec(s, lambda i, nd=len(s): (0,) * nd) for s in accs]
    out_shape = [S((rows, c), dt) for c, dt in outs] + [S(s, F32) for s in accs]
    return pl.pallas_call(body, grid=(rows // tm,), in_specs=in_specs, out_specs=out_specs, out_shape=out_shape,
                          compiler_params=_cp("arbitrary"), name=name)(*arrs, *full)


def mm_nn(name, m, n, pairs, n_acc, epi, outs, tiled=(), cols=(), rowv=(), sums=()):
    a_ops, a_slot, b_arrs, b_specs, idx, trans = [], [], [], [], [], []
    fixed = 0
    for pair in pairs:
        a, b, k = pair[:3]
        bt = len(pair) > 3
        arr, cb, kdim = a if isinstance(a, tuple) else (a, 0, a.shape[1])
        key = (id(arr), cb, kdim)
        if key not in [o[0] for o in a_ops]:
            a_ops.append((key, arr, cb, kdim))
        a_slot.append([o[0] for o in a_ops].index(key))
        b_arr, off = b if isinstance(b, tuple) else (b, 0)
        b_arrs.append(b_arr)
        if bt:
            assert off % n == 0 and b_arr.shape[1] == kdim
            b_specs.append(pl.BlockSpec((n, kdim), lambda i, o=off // n: (o, 0), pipeline_mode=pl.Buffered(1)))
        else:
            assert b_arr.shape[1] == n
            b_specs.append(pl.BlockSpec((kdim, n), lambda i, o=off: (o, 0), pipeline_mode=pl.Buffered(1)))
        fixed += kdim * n * b_arr.dtype.itemsize
        idx.append(k)
        trans.append(bt)
    per_row = sum(2 * kdim * arr.dtype.itemsize for _, arr, _, kdim in a_ops)
    per_row += sum(2 * n * t.dtype.itemsize for t in tiled) + sum(2 * n * jnp.dtype(dt).itemsize for dt in outs)
    per_row += (n_acc + 3) * n * 4
    tm = next((t for t in (1024, 512, 256, 128) if m % t == 0 and fixed + t * per_row <= VMEM_BUDGET), _tile(m, 128))
    n_a, n_p = len(a_ops), len(pairs)
    n_in = n_a + n_p + len(tiled) + len(cols) + len(rowv)

    def body(*refs):
        a_vals = [_bf(r[...]) for r in refs[:n_a]]
        accs = [None] * n_acc
        for p in range(n_p):
            av, bv = a_vals[a_slot[p]], _bf(refs[n_a + p][...])
            if trans[p]:
                d = lax.dot_general(av, bv, (((1,), (1,)), ((), ())), preferred_element_type=F32)
            else:
                d = jnp.dot(av, bv, preferred_element_type=F32)
            accs[idx[p]] = d if accs[idx[p]] is None else accs[idx[p]] + d
        extra = [r[...] for r in refs[n_a + n_p:n_in]]
        ov = epi(accs, *extra)
        sv = ov[len(outs):]
        for r, v in zip(refs[n_in:n_in + len(outs)], ov):
            r[...] = v.astype(r.dtype)
        if sums:
            s_refs = refs[n_in + len(outs):]

            @pl.when(pl.program_id(0) == 0)
            def _():
                for r in s_refs:
                    r[...] = jnp.zeros(r.shape, r.dtype)
            for r, v in zip(s_refs, sv):
                r[...] += v

    in_specs = [pl.BlockSpec((tm, kdim), lambda i, cb=cb: (i, cb)) for _, _, cb, kdim in a_ops] + b_specs
    in_specs += [pl.BlockSpec((tm, n), lambda i: (i, 0)) for _ in tiled]
    in_specs += [pl.BlockSpec((tm, 1), lambda i: (i, 0)) for _ in cols]
    in_specs += [pl.BlockSpec((1, n), lambda i: (0, 0)) for _ in rowv]
    out_specs = [pl.BlockSpec((tm, n), lambda i: (i, 0)) for _ in outs]
    out_specs += [pl.BlockSpec(s, lambda i, nd=len(s): (0,) * nd) for s in sums]
    out_shape = [S((m, n), dt) for dt in outs] + [S(s, F32) for s in sums]
    return pl.pallas_call(body, grid=(m // tm,), in_specs=in_specs, out_specs=out_specs, out_shape=out_shape,
                          compiler_params=_cp("arbitrary" if sums else "parallel"),
                          name=name)(*[o[1] for o in a_ops], *b_arrs, *tiled, *cols, *rowv)


def mm_tn(name, a, b, out_dtype=BF16):
    if isinstance(a, tuple):
        a_arr, a_cb, m = a
    else:
        a_arr, a_cb, m = a, None, a.shape[1]
    if isinstance(b, tuple):
        b_arr, b_cb, n = b
    else:
        b_arr, b_cb, n = b, None, b.shape[1]
    t = a_arr.shape[0]
    whole_b = t * n * b_arr.dtype.itemsize <= MM_TN_RESIDENT and b_cb is None
    tn = n if whole_b else _tile(n, 512)
    tm = _tile(m, 512 if t * 512 * a_arr.dtype.itemsize * 2 + t * tn * b_arr.dtype.itemsize * 2 <= VMEM_BUDGET else 256)
    a_off = 0 if a_cb is None else a_cb * (m // tm)
    b_off = 0 if b_cb is None else b_cb * (n // tn)

    def body(a_ref, b_ref, o_ref):
        o_ref[...] = lax.dot_general(_bf(a_ref[...]), _bf(b_ref[...]), (((0,), (0,)), ((), ())),
                                     preferred_element_type=F32).astype(o_ref.dtype)

    if whole_b:
        b_spec = pl.BlockSpec((t, n), lambda i, j: (0, 0), pipeline_mode=pl.Buffered(1))
    else:
        b_spec = pl.BlockSpec((t, tn), lambda i, j: (0, j + b_off))
    return pl.pallas_call(
        body, grid=(m // tm, n // tn),
        in_specs=[pl.BlockSpec((t, tm), lambda i, j: (0, i + a_off)), b_spec],
        out_specs=pl.BlockSpec((tm, tn), lambda i, j: (i, j)), out_shape=S((m, n), out_dtype),
        compiler_params=_cp("parallel", "parallel"), name=name)(a_arr, b_arr)


def rms_fwd(name, x, gain):
    def fn(xv, g):
        r = lax.rsqrt(jnp.mean(xv * xv, axis=-1, keepdims=True) + EPS)
        return [xv * r * g, r], []
    return rows_call(name, fn, [x], [gain], [(x.shape[1], BF16), (1, F32)], [])


def rms_bwd_gain_only(name, dxn, x, r):
    def fn(dv, xv, rv):
        return [], [jnp.sum(dv * xv * rv, axis=0, keepdims=True)]
    return rows_call(name, fn, [dxn, x, r], [], [], [(1, x.shape[1])])[0]


def final_loss(name, x, gain, target):
    d = x.shape[1]

    def fn(xv, tv, g):
        r = lax.rsqrt(jnp.mean(xv * xv, axis=-1, keepdims=True) + EPS)
        xh = xv * r
        err = xh * g - tv
        dy = err * (1.0 / d)
        w = dy * g
        dx = r * (w - xh * jnp.mean(w * xh, axis=-1, keepdims=True))
        part = jnp.sum(jnp.sum(err * err, axis=-1, keepdims=True), axis=0, keepdims=True) * (0.5 / d)
        return [dx, dx], [jnp.sum(dy * xh, axis=0, keepdims=True), part]

    return rows_call(name, fn, [x, target], [gain], [(d, F32), (d, BF16)], [(1, d), (1, 1)])


def _gmlp_mask():
    row = lax.broadcasted_iota(jnp.int32, (GMLP_BLOCK, GMLP_BLOCK), 0) // CHUNK
    col = lax.broadcasted_iota(jnp.int32, (GMLP_BLOCK, GMLP_BLOCK), 1) // CHUNK
    return col <= row


def _ln_plain(v):
    mu = jnp.mean(v, axis=-1, keepdims=True)
    vc = v - mu
    rstd = lax.rsqrt(jnp.mean(vc * vc, axis=-1, keepdims=True) + EPS)
    return vc * rstd, rstd


def gmlp_fwd(name, proj, w, b, tm=512):
    t = proj.shape[0]
    tm = _tile(t, tm)

    def body(au_ref, av_ref, w_ref, b_ref, o_ref):
        mask = _gmlp_mask()
        u = _gelu(au_ref[...])
        vn, _ = _ln_plain(_gelu(av_ref[...]))
        vnb = _bf(vn)
        for g in range(A_GROUPS):
            wg = _bf(jnp.where(mask, w_ref[g], 0.0))
            cs = slice(g * GMLP_BLOCK, (g + 1) * GMLP_BLOCK)
            for n in range(tm // GMLP_BLOCK):
                rs = slice(n * GMLP_BLOCK, (n + 1) * GMLP_BLOCK)
                sg = jnp.dot(wg, vnb[rs, cs], preferred_element_type=F32) + b_ref[g]
                o_ref[rs, cs] = (u[rs, cs] * sg).astype(o_ref.dtype)

    return pl.pallas_call(
        body, grid=(t // tm,),
        in_specs=[pl.BlockSpec((tm, A_WIDTH), lambda i: (i, 0)), pl.BlockSpec((tm, A_WIDTH), lambda i: (i, 1)),
                  pl.BlockSpec(w.shape, lambda i: (0, 0, 0)), pl.BlockSpec(b.shape, lambda i: (0, 0, 0))],
        out_specs=pl.BlockSpec((tm, A_WIDTH), lambda i: (i, 0)), out_shape=S((t, A_WIDTH), BF16),
        compiler_params=_cp("parallel"), name=name)(proj, proj, w, b)


def gmlp_bwd(name, proj, dcat, w, b, tm=512):
    t = proj.shape[0]
    tm = _tile(t, tm)

    def body(au_ref, av_ref, do_ref, w_ref, b_ref, dp_ref, dw_ref, db_ref):
        @pl.when(pl.program_id(0) == 0)
        def _():
            dw_ref[...] = jnp.zeros(dw_ref.shape, F32)
            db_ref[...] = jnp.zeros(db_ref.shape, F32)

        mask = _gmlp_mask()
        au = au_ref[...]
        av = av_ref[...]
        u = _gelu(au)
        vn, rstd = _ln_plain(_gelu(av))
        vnb = _bf(vn)
        dout = do_ref[...]
        dvn_cols = []
        for g in range(A_GROUPS):
            wm = jnp.where(mask, w_ref[g], 0.0)
            wg = _bf(wm)
            wgt = _bf(wm.T)
            cs = slice(g * GMLP_BLOCK, (g + 1) * GMLP_BLOCK)
            dwg = jnp.zeros((GMLP_BLOCK, GMLP_BLOCK), F32)
            dbg = jnp.zeros((GMLP_BLOCK, 1), F32)
            dvn_rows = []
            for n in range(tm // GMLP_BLOCK):
                rs = slice(n * GMLP_BLOCK, (n + 1) * GMLP_BLOCK)
                sg = jnp.dot(wg, vnb[rs, cs], preferred_element_type=F32) + b_ref[g]
                dp_ref[rs, cs] = (dout[rs, cs] * sg * _gelu_grad(au[rs, cs])).astype(dp_ref.dtype)
                dsg = dout[rs, cs] * u[rs, cs]
                dsgb = _bf(dsg)
                dbg = dbg + jnp.sum(dsg, axis=1, keepdims=True)
                dwg = dwg + lax.dot_general(dsgb, vnb[rs, cs], (((1,), (1,)), ((), ())), preferred_element_type=F32)
                dvn_rows.append(jnp.dot(wgt, dsgb, preferred_element_type=F32))
            dw_ref[g] += jnp.where(mask, dwg, 0.0)
            db_ref[g] += dbg
            dvn_cols.append(jnp.concatenate(dvn_rows, axis=0))
        dvn = jnp.concatenate(dvn_cols, axis=1)
        dv = rstd * (dvn - jnp.mean(dvn, axis=-1, keepdims=True) - vn * jnp.mean(dvn * vn, axis=-1, keepdims=True))
        dp_ref[:, A_WIDTH:] = (dv * _gelu_grad(av)).astype(dp_ref.dtype)

    return pl.pallas_call(
        body, grid=(t // tm,),
        in_specs=[pl.BlockSpec((tm, A_WIDTH), lambda i: (i, 0)), pl.BlockSpec((tm, A_WIDTH), lambda i: (i, 1)),
                  pl.BlockSpec((tm, A_WIDTH), lambda i: (i, 0)),
                  pl.BlockSpec(w.shape, lambda i: (0, 0, 0)), pl.BlockSpec(b.shape, lambda i: (0, 0, 0))],
        out_specs=[pl.BlockSpec((tm, 2 * A_WIDTH), lambda i: (i, 0)),
                   pl.BlockSpec(w.shape, lambda i: (0, 0, 0)), pl.BlockSpec(b.shape, lambda i: (0, 0, 0))],
        out_shape=[S((t, 2 * A_WIDTH), BF16), S(w.shape, F32), S(b.shape, F32)],
        compiler_params=_cp("arbitrary"), name=name)(proj, proj, dcat, w, b)


CONV_ROWS = 256


def conv_fwd(name, proj, w, cb):
    t = proj.shape[0]
    tc = LANES
    rows = _tile(t, CONV_ROWS)
    a_cb, g_cb = 2 * A_WIDTH // tc, (2 * A_WIDTH + B_WIDTH) // tc

    def body(a_ref, g_ref, w_ref, cb_ref, o_ref, hpad):
        hpad[0:CONV_PAD, :] = jnp.zeros((CONV_PAD, tc), F32)

        def fill(i, _):
            r0 = pl.multiple_of(i * rows, rows)
            hpad[pl.ds(CONV_PAD + r0, rows), :] = a_ref[pl.ds(r0, rows), :] * _sigmoid(g_ref[pl.ds(r0, rows), :])
            return 0
        lax.fori_loop(0, t // rows, fill, 0)

        def conv(i, _):
            r0 = pl.multiple_of(i * rows, rows)
            win = hpad[pl.ds(r0, rows + CONV_PAD), :]
            acc = jnp.zeros((rows, tc), F32) + cb_ref[...]
            for b in range(SUB):
                wb = win if b == 0 else pltpu.roll(win, b, 0)
                for a in range(CONV_PAD // SUB):
                    k = CONV_WIDTH - 1 - (SUB * a + b)
                    if k >= 0:
                        lo = CONV_PAD - SUB * a
                        acc = acc + wb[lo:lo + rows, :] * w_ref[k:k + 1, :]
            o_ref[pl.ds(r0, rows), :] = acc
            return 0
        lax.fori_loop(0, t // rows, conv, 0)

    return pl.pallas_call(
        body, grid=(B_WIDTH // tc,),
        in_specs=[pl.BlockSpec((t, tc), lambda j: (0, a_cb + j)), pl.BlockSpec((t, tc), lambda j: (0, g_cb + j)),
                  pl.BlockSpec((CONV_WIDTH, tc), lambda j: (0, j)), pl.BlockSpec((1, tc), lambda j: (0, j))],
        out_specs=pl.BlockSpec((t, tc), lambda j: (0, j)), out_shape=S((t, B_WIDTH), F32),
        scratch_shapes=[pltpu.VMEM((t + CONV_PAD, tc), F32)],
        compiler_params=_cp("parallel"), name=name)(proj, proj, w, cb)


def conv_bwd(name, proj, dhc, w):
    t = proj.shape[0]
    tc = LANES
    rows = _tile(t, CONV_ROWS)
    a_cb, g_cb = 2 * A_WIDTH // tc, (2 * A_WIDTH + B_WIDTH) // tc
    win_rows = rows + CONV_PAD

    def body(a_ref, g_ref, d_ref, w_ref, da_ref, dg_ref, dw_ref, dcb_ref, hpad, dpad, dwacc):
        hpad[0:CONV_PAD, :] = jnp.zeros((CONV_PAD, tc), F32)
        dpad[t:t + CONV_PAD, :] = jnp.zeros((CONV_PAD, tc), F32)
        dwacc[...] = jnp.zeros(dwacc.shape, F32)

        def fill(i, _):
            r0 = pl.multiple_of(i * rows, rows)
            hpad[pl.ds(CONV_PAD + r0, rows), :] = a_ref[pl.ds(r0, rows), :] * _sigmoid(g_ref[pl.ds(r0, rows), :])
            dpad[pl.ds(r0, rows), :] = d_ref[pl.ds(r0, rows), :]
            return 0
        lax.fori_loop(0, t // rows, fill, 0)

        def step(i, dcb):
            r0 = pl.multiple_of(i * rows, rows)
            hwin = hpad[pl.ds(r0, win_rows), :]
            dwin = dpad[pl.ds(r0, win_rows), :]
            dchunk = dwin[:rows, :]
            dh = jnp.zeros((rows, tc), F32)
            for b in range(SUB):
                hb = hwin if b == 0 else pltpu.roll(hwin, b, 0)
                db = dwin if b == 0 else pltpu.roll(dwin, win_rows - b, 0)
                for a in range(CONV_PAD // SUB):
                    k = CONV_WIDTH - 1 - (SUB * a + b)
                    if k >= 0:
                        dh = dh + db[SUB * a:SUB * a + rows, :] * w_ref[k:k + 1, :]
                        lo = CONV_PAD - SUB * a
                        prod = dchunk * hb[lo:lo + rows, :]
                        dwacc[k] += jnp.sum(prod.reshape(rows // 8, 8, tc), axis=0)
            a = a_ref[pl.ds(r0, rows), :]
            sg = _sigmoid(g_ref[pl.ds(r0, rows), :])
            da_ref[pl.ds(r0, rows), :] = (dh * sg).astype(da_ref.dtype)
            dg_ref[pl.ds(r0, rows), :] = (dh * a * sg * (1.0 - sg)).astype(dg_ref.dtype)
            return dcb + jnp.sum(dchunk, axis=0, keepdims=True)
        dcb = lax.fori_loop(0, t // rows, step, jnp.zeros((1, tc), F32))
        dcb_ref[...] = dcb
        for k in range(CONV_WIDTH):
            dw_ref[k:k + 1, :] = jnp.sum(dwacc[k], axis=0, keepdims=True)

    return pl.pallas_call(
        body, grid=(B_WIDTH // tc,),
        in_specs=[pl.BlockSpec((t, tc), lambda j: (0, a_cb + j)), pl.BlockSpec((t, tc), lambda j: (0, g_cb + j)),
                  pl.BlockSpec((t, tc), lambda j: (0, j)), pl.BlockSpec((CONV_WIDTH, tc), lambda j: (0, j))],
        out_specs=[pl.BlockSpec((t, tc), lambda j: (0, j)), pl.BlockSpec((t, tc), lambda j: (0, j)),
                   pl.BlockSpec((CONV_WIDTH, tc), lambda j: (0, j)), pl.BlockSpec((1, tc), lambda j: (0, j))],
        out_shape=[S((t, B_WIDTH), BF16), S((t, B_WIDTH), BF16), S((CONV_WIDTH, B_WIDTH), F32), S((1, B_WIDTH), F32)],
        scratch_shapes=[pltpu.VMEM((t + CONV_PAD, tc), F32), pltpu.VMEM((t + CONV_PAD, tc), F32),
                        pltpu.VMEM((CONV_WIDTH, 8, tc), F32)],
        compiler_params=_cp("parallel"), name=name)(proj, proj, dhc, w)


def ln_silu_fwd(name, hc, g, b):
    def fn(h, gv, bv):
        y, _ = _ln_plain(h)
        z = y * gv + bv
        return [z * _sigmoid(z)], []
    return rows_call(name, fn, [hc], [g, b], [(hc.shape[1], BF16)], [])[0]


def ln_silu_bwd(name, hc, dcat, g, b):
    c = hc.shape[1]

    def fn(h, dout, gv, bv):
        y, rstd = _ln_plain(h)
        z = y * gv + bv
        s = _sigmoid(z)
        dz = dout * s * (1.0 + z * (1.0 - s))
        dyv = dz * gv
        dh = rstd * (dyv - jnp.mean(dyv, axis=-1, keepdims=True) - y * jnp.mean(dyv * y, axis=-1, keepdims=True))
        return [dh], [jnp.sum(dz * y, axis=0, keepdims=True), jnp.sum(dz, axis=0, keepdims=True)]

    return rows_call(name, fn, [hc, (dcat, 1, c)], [g, b], [(c, F32)], [(1, c), (1, c)])


_NT = (((1,), (1,)), ((), ()))
_TN = (((0,), (0,)), ((), ()))


def attn_fwd(name, q, k, v, tm=512):
    t, d = q.shape
    m = k.shape[0]
    tm = _tile(t, tm)
    scale = CA_HEAD_DIM ** -0.5

    def body(q_ref, k_ref, v_ref, o_ref):
        for h in range(CA_HEADS):
            cs = slice(h * CA_HEAD_DIM, (h + 1) * CA_HEAD_DIM)
            s = lax.dot_general(q_ref[:, cs], k_ref[:, cs], _NT, preferred_element_type=F32) * scale
            e = jnp.exp(s - jnp.max(s, axis=-1, keepdims=True))
            p = e / jnp.sum(e, axis=-1, keepdims=True)
            o_ref[:, cs] = jnp.dot(_bf(p), v_ref[:, cs], preferred_element_type=F32).astype(o_ref.dtype)

    return pl.pallas_call(
        body, grid=(t // tm,),
        in_specs=[pl.BlockSpec((tm, d), lambda i: (i, 0)), pl.BlockSpec((m, d), lambda i: (0, 0)),
                  pl.BlockSpec((m, d), lambda i: (0, 0))],
        out_specs=pl.BlockSpec((tm, d), lambda i: (i, 0)), out_shape=S((t, d), BF16),
        compiler_params=_cp("parallel"), name=name)(q, k, v)


def attn_bwd(name, q, k, v, do, tm=512):
    t, d = q.shape
    m = k.shape[0]
    tm = _tile(t, tm)
    scale = CA_HEAD_DIM ** -0.5

    def body(q_ref, k_ref, v_ref, do_ref, dq_ref, dk_ref, dv_ref):
        @pl.when(pl.program_id(0) == 0)
        def _():
            dk_ref[...] = jnp.zeros(dk_ref.shape, F32)
            dv_ref[...] = jnp.zeros(dv_ref.shape, F32)

        for h in range(CA_HEADS):
            cs = slice(h * CA_HEAD_DIM, (h + 1) * CA_HEAD_DIM)
            qh, kh, vh, doh = q_ref[:, cs], k_ref[:, cs], v_ref[:, cs], do_ref[:, cs]
            s = lax.dot_general(qh, kh, _NT, preferred_element_type=F32) * scale
            e = jnp.exp(s - jnp.max(s, axis=-1, keepdims=True))
            p = e / jnp.sum(e, axis=-1, keepdims=True)
            pb = _bf(p)
            dv_ref[:, cs] += lax.dot_general(pb, doh, _TN, preferred_element_type=F32)
            dp = lax.dot_general(doh, vh, _NT, preferred_element_type=F32)
            ds = _bf(p * (dp - jnp.sum(dp * p, axis=-1, keepdims=True)) * scale)
            dq_ref[:, cs] = jnp.dot(ds, kh, preferred_element_type=F32).astype(dq_ref.dtype)
            dk_ref[:, cs] += lax.dot_general(ds, qh, _TN, preferred_element_type=F32)

    return pl.pallas_call(
        body, grid=(t // tm,),
        in_specs=[pl.BlockSpec((tm, d), lambda i: (i, 0)), pl.BlockSpec((m, d), lambda i: (0, 0)),
                  pl.BlockSpec((m, d), lambda i: (0, 0)), pl.BlockSpec((tm, d), lambda i: (i, 0))],
        out_specs=[pl.BlockSpec((tm, d), lambda i: (i, 0)), pl.BlockSpec((m, d), lambda i: (0, 0)),
                   pl.BlockSpec((m, d), lambda i: (0, 0))],
        out_shape=[S((t, d), BF16), S((m, d), F32), S((m, d), F32)],
        compiler_params=_cp("arbitrary"), name=name)(q, k, v, do)


SUB = 8
S5_ROWS = 256


_HI = lax.Precision.HIGHEST
_GP = (C_GROUPS, C_STATE)
_RP = (C_WIDTH, C_STATE)


def _zoh(lr, li, ldt):
    dt = jnp.exp(ldt)
    mag = jnp.exp(lr * dt)
    ar = mag * jnp.cos(li * dt)
    ai = mag * jnp.sin(li * dt)
    den = lr * lr + li * li
    qr = ((ar - 1.0) * lr + ai * li) / den
    qi = (ai * lr - (ar - 1.0) * li) / den
    return dt, ar, ai, den, qr, qi


def _per_channel(v):
    return jnp.broadcast_to(v[:, None, :], (C_GROUPS, C_GROUP_CH, C_STATE)).reshape(_RP)


def _same_group(shape, row_per_group, col_per_group):
    rows = lax.broadcasted_iota(jnp.int32, shape, 0) // row_per_group
    cols = lax.broadcasted_iota(jnp.int32, shape, 1) // col_per_group
    return rows == cols


def _spread(shape, axis):
    long = lax.broadcasted_iota(jnp.int32, shape, axis) % C_STATE
    short = lax.broadcasted_iota(jnp.int32, shape, 1 - axis)
    return long == short


def s5_discretise(name, lam_re, lam_im, log_dt, bt_re, bt_im):
    def body(lr_ref, li_ref, ldt_ref, btr_ref, bti_ref, a_ref, bbr_ref, bbi_ref):
        _, ar, ai, _, qr, qi = _zoh(lr_ref[...], li_ref[...], ldt_ref[...])
        a_ref[0] = ar
        a_ref[1] = ai
        q2r, q2i = _per_channel(qr), _per_channel(qi)
        btr, bti = btr_ref[...], bti_ref[...]
        bbr_ref[...] = q2r * btr - q2i * bti
        bbi_ref[...] = q2r * bti + q2i * btr

    return pl.pallas_call(body, out_shape=[S((2,) + _GP, F32), S(_RP, F32), S(_RP, F32)],
                          name=name)(lam_re, lam_im, log_dt, bt_re, bt_im)


def s5_operands(name, a, bbr, bbi, c2r, c2i, ctr, cti):
    ns = N_STATE

    def body(a_ref, bbr_ref, bbi_ref, c2r_ref, c2i_ref, ctr_ref, cti_ref, pw_ref, qw_ref, mb_ref, mc_ref, mct_ref):
        ar, ai = a_ref[0:1, :], a_ref[1:2, :]
        pows = [(ar, ai)]
        for _ in range(SUB - 1):
            pr, pi = pows[-1]
            pows.append((pr * ar - pi * ai, pr * ai + pi * ar))
        rows = lax.broadcasted_iota(jnp.int32, (SUB, ns), 0)

        def rows_of(v):
            return jnp.broadcast_to(v, (SUB, ns))

        for k, s in enumerate((1, 2, 4)):
            pr, pi = rows_of(pows[s - 1][0]), rows_of(pows[s - 1][1])
            pw_ref[k, 0] = jnp.where(rows >= s, pr, 0.0)
            pw_ref[k, 1] = jnp.where(rows >= s, pi, 0.0)
            qw_ref[k, 0] = jnp.where(rows + s <= SUB - 1, pr, 0.0)
            qw_ref[k, 1] = jnp.where(rows + s <= SUB - 1, -pi, 0.0)
        fr = fi = br = bi = jnp.zeros((SUB, ns), F32)
        for i in range(SUB):
            fr = jnp.where(rows == i, rows_of(pows[i][0]), fr)
            fi = jnp.where(rows == i, rows_of(pows[i][1]), fi)
            br = jnp.where(rows == i, rows_of(pows[SUB - 1 - i][0]), br)
            bi = jnp.where(rows == i, rows_of(-pows[SUB - 1 - i][1]), bi)
        pw_ref[3, 0], pw_ref[3, 1], qw_ref[3, 0], qw_ref[3, 1] = fr, fi, br, bi

        wide = _spread((C_STATE, ns), 1).astype(BF16)
        tall = _spread((ns, C_STATE), 0).astype(BF16)
        in_rows = _same_group((C_WIDTH, ns), C_GROUP_CH, C_STATE)
        in_cols = _same_group((ns, C_WIDTH), C_STATE, C_GROUP_CH)

        def across(v, sign=1.0):
            return jnp.where(in_rows, sign * jnp.dot(_bf(v), wide, preferred_element_type=F32), 0.0).astype(BF16)

        def down(vt, sign=1.0):
            return jnp.where(in_cols, sign * jnp.dot(tall, _bf(vt), preferred_element_type=F32), 0.0).astype(BF16)

        mb_ref[:, 0:ns] = across(bbr_ref[...])
        mb_ref[:, ns:2 * ns] = across(bbi_ref[...])
        mct_ref[:, 0:ns] = across(c2r_ref[...])
        mct_ref[:, ns:2 * ns] = across(c2i_ref[...], -1.0)
        mc_ref[0:ns, :] = down(ctr_ref[...])
        mc_ref[ns:2 * ns, :] = down(cti_ref[...], -1.0)

    return pl.pallas_call(
        body, out_shape=[S((4, 2, SUB, ns), F32), S((4, 2, SUB, ns), F32), S((C_WIDTH, 2 * ns), BF16),
                         S((2 * ns, C_WIDTH), BF16), S((C_WIDTH, 2 * ns), BF16)],
        compiler_params=pltpu.CompilerParams(vmem_limit_bytes=VMEM_LIMIT), name=name)(a, bbr, bbi, c2r, c2i, ctr, cti)


def s5_param_grads(name, d_mb, d_mc, da, lam_re, lam_im, log_dt, bt_re, bt_im):
    ns = N_STATE

    def body(dmb_ref, dmc_ref, da_ref, lr_ref, li_ref, ldt_ref, btr_ref, bti_ref,
             glr_ref, gli_ref, gdt_ref, gbr_ref, gbi_ref, gcr_ref, gci_ref):
        lr, li = lr_ref[...], li_ref[...]
        dt, ar, ai, den, qr, qi = _zoh(lr, li, ldt_ref[...])
        wide = _spread((C_STATE, ns), 1).astype(F32)
        tall = _spread((ns, C_STATE), 0).astype(F32)
        in_rows = _same_group((C_WIDTH, ns), C_GROUP_CH, C_STATE)
        in_cols = _same_group((ns, C_WIDTH), C_STATE, C_GROUP_CH)

        def fold_rows(v):
            return lax.dot_general(jnp.where(in_rows, v, 0.0), wide, (((1,), (1,)), ((), ())), precision=_HI,
                                   preferred_element_type=F32)

        def fold_cols(v):
            return lax.dot_general(jnp.where(in_cols, v, 0.0), tall, (((0,), (0,)), ((), ())), precision=_HI,
                                   preferred_element_type=F32)

        gcr_ref[...] = fold_cols(dmc_ref[0:ns, :])
        gci_ref[...] = -fold_cols(dmc_ref[ns:2 * ns, :])
        gbbr = fold_rows(dmb_ref[:, 0:ns])
        gbbi = fold_rows(dmb_ref[:, ns:2 * ns])
        btr, bti = btr_ref[...], bti_ref[...]
        q2r, q2i = _per_channel(qr), _per_channel(qi)
        gbr_ref[...] = q2r * gbbr + q2i * gbbi
        gbi_ref[...] = q2r * gbbi - q2i * gbbr

        def per_group(v):
            return jnp.sum(v.reshape(C_GROUPS, C_GROUP_CH, C_STATE), axis=1)

        gqr = per_group(btr * gbbr + bti * gbbi)
        gqi = per_group(btr * gbbi - bti * gbbr)
        ilr, ili = lr / den, li / den
        gar = da_ref[0] + ilr * gqr - ili * gqi
        gai = da_ref[1] + ilr * gqi + ili * gqr
        sr = (qr * lr + qi * li) / den
        si = (qi * lr - qr * li) / den
        gzr = ar * gar + ai * gai
        gzi = ar * gai - ai * gar
        glr_ref[...] = -sr * gqr - si * gqi + dt * gzr
        gli_ref[...] = -sr * gqi + si * gqr + dt * gzi
        gdt_ref[...] = jnp.sum(lr * gzr + li * gzi, axis=1, keepdims=True) * dt

    return pl.pallas_call(
        body, out_shape=[S(_GP, F32), S(_GP, F32), S((C_GROUPS, 1), F32), S(_RP, F32), S(_RP, F32), S(_RP, F32), S(_RP, F32)],
        compiler_params=pltpu.CompilerParams(vmem_limit_bytes=VMEM_LIMIT), name=name,
    )(d_mb, d_mc, da, lam_re, lam_im, log_dt, bt_re, bt_im)


def _cmul_add(xr, xi, pr, pi, zr, zi):
    return xr + pr * zr - pi * zi, xi + pr * zi + pi * zr


def s5_fwd(name, u, mb, mc, pw, dskip):
    t = u.shape[0]
    tm = _tile(t, S5_ROWS)
    ns = N_STATE

    def body(u_ref, mb_ref, mc_ref, pw_ref, d_ref, gy_ref, y_ref, xs_ref, xb_ref, carry):
        @pl.when(pl.program_id(0) == 0)
        def _():
            carry[...] = jnp.zeros(carry.shape, F32)

        uv = u_ref[...]
        xs_ref[...] = jnp.dot(_bf(uv), mb_ref[...], preferred_element_type=F32)

        def group(i, _):
            r0 = pl.multiple_of(i * SUB, SUB)
            xr = xs_ref[pl.ds(r0, SUB), 0:ns]
            xi = xs_ref[pl.ds(r0, SUB), ns:2 * ns]
            for k, s in enumerate((1, 2, 4)):
                xr, xi = _cmul_add(xr, xi, pw_ref[k, 0], pw_ref[k, 1], pltpu.roll(xr, s, 0), pltpu.roll(xi, s, 0))
            xr, xi = _cmul_add(xr, xi, pw_ref[3, 0], pw_ref[3, 1], carry[0], carry[1])
            xs_ref[pl.ds(r0, SUB), 0:ns] = xr
            xs_ref[pl.ds(r0, SUB), ns:2 * ns] = xi
            carry[0] = jnp.broadcast_to(xr[SUB - 1:SUB, :], (SUB, ns))
            carry[1] = jnp.broadcast_to(xi[SUB - 1:SUB, :], (SUB, ns))
            return 0
        lax.fori_loop(0, tm // SUB, group, 0)

        xb = _bf(xs_ref[...])
        xb_ref[...] = xb
        y = jnp.dot(xb, mc_ref[...], preferred_element_type=F32) + d_ref[...] * uv
        y_ref[...] = y
        gy_ref[...] = _gelu(y).astype(gy_ref.dtype)

    c = u.shape[1]
    return pl.pallas_call(
        body, grid=(t // tm,),
        in_specs=[pl.BlockSpec((tm, c), lambda i: (i, 0)), pl.BlockSpec(mb.shape, lambda i: (0, 0)),
                  pl.BlockSpec(mc.shape, lambda i: (0, 0)), pl.BlockSpec(pw.shape, lambda i: (0, 0, 0, 0)),
                  pl.BlockSpec((1, c), lambda i: (0, 0))],
        out_specs=[pl.BlockSpec((tm, c), lambda i: (i, 0)), pl.BlockSpec((tm, c), lambda i: (i, 0)),
                   pl.BlockSpec((tm, 2 * ns), lambda i: (i, 0)), pl.BlockSpec((tm, 2 * ns), lambda i: (i, 0))],
        out_shape=[S((t, c), BF16), S((t, c), F32), S((t, 2 * ns), F32), S((t, 2 * ns), BF16)],
        scratch_shapes=[pltpu.VMEM((2, SUB, ns), F32)],
        compiler_params=_cp("arbitrary"), name=name)(u, mb, mc, pw, dskip)


def s5_bwd(name, dgy, y, u, xs, mct, mbt, qw, dskip):
    t, c = u.shape
    tm = _tile(t, S5_ROWS)
    nt = t // tm
    ns = N_STATE
    ng = tm // SUB

    def body(dgy_ref, y_ref, u_ref, xs_ref, mct_ref, mbt_ref, qw_ref, d_ref,
             du_ref, dy_ref, lb_ref, da_ref, dd_ref, lam, carry):
        @pl.when(pl.program_id(0) == 0)
        def _():
            carry[...] = jnp.zeros(carry.shape, F32)
            da_ref[...] = jnp.zeros(da_ref.shape, F32)
            dd_ref[...] = jnp.zeros(dd_ref.shape, F32)

        uv = u_ref[...]
        dy = dgy_ref[...] * _gelu_grad(y_ref[...])
        dyb = _bf(dy)
        dy_ref[...] = dyb
        dd_ref[...] += jnp.sum(dy * uv, axis=0, keepdims=True)
        lam[...] = jnp.dot(dyb, mct_ref[...], preferred_element_type=F32)
        last_row = lax.broadcasted_iota(jnp.int32, (SUB, ns), 0) == SUB - 1

        def group(j, _):
            i = ng - 1 - j
            r0 = pl.multiple_of(i * SUB, SUB)
            lr = lam[pl.ds(r0, SUB), 0:ns]
            li = lam[pl.ds(r0, SUB), ns:2 * ns]
            for k, s in enumerate((1, 2, 4)):
                lr, li = _cmul_add(lr, li, qw_ref[k, 0], qw_ref[k, 1],
                                   pltpu.roll(lr, SUB - s, 0), pltpu.roll(li, SUB - s, 0))
            cr, ci = carry[0], carry[1]
            lr, li = _cmul_add(lr, li, qw_ref[3, 0], qw_ref[3, 1], cr, ci)
            lam[pl.ds(r0, SUB), 0:ns] = lr
            lam[pl.ds(r0, SUB), ns:2 * ns] = li
            carry[0] = jnp.broadcast_to(lr[0:1, :], (SUB, ns))
            carry[1] = jnp.broadcast_to(li[0:1, :], (SUB, ns))
            nr = jnp.where(last_row, cr, pltpu.roll(lr, SUB - 1, 0))
            ni = jnp.where(last_row, ci, pltpu.roll(li, SUB - 1, 0))
            xr = xs_ref[pl.ds(r0, SUB), 0:ns]
            xi = xs_ref[pl.ds(r0, SUB), ns:2 * ns]
            da_ref[0] += nr * xr + ni * xi
            da_ref[1] += ni * xr - nr * xi
            return 0
        lax.fori_loop(0, ng, group, 0)

        lb = _bf(lam[...])
        lb_ref[...] = lb
        du_ref[...] = (jnp.dot(lb, mbt_ref[...], preferred_element_type=F32) + d_ref[...] * dy).astype(du_ref.dtype)

    rev = lambda i: (nt - 1 - i, 0)
    return pl.pallas_call(
        body, grid=(nt,),
        in_specs=[pl.BlockSpec((tm, c), rev), pl.BlockSpec((tm, c), rev), pl.BlockSpec((tm, c), rev),
                  pl.BlockSpec((tm, 2 * ns), rev),
                  pl.BlockSpec(mct.shape, lambda i: (0, 0)), pl.BlockSpec(mbt.shape, lambda i: (0, 0)),
                  pl.BlockSpec(qw.shape, lambda i: (0, 0, 0, 0)), pl.BlockSpec((1, c), lambda i: (0, 0))],
        out_specs=[pl.BlockSpec((tm, c), rev), pl.BlockSpec((tm, c), rev), pl.BlockSpec((tm, 2 * ns), rev),
                   pl.BlockSpec((2, SUB, ns), lambda i: (0, 0, 0)), pl.BlockSpec((1, c), lambda i: (0, 0))],
        out_shape=[S((t, c), BF16), S((t, c), BF16), S((t, 2 * ns), BF16), S((2, SUB, ns), F32), S((1, c), F32)],
        scratch_shapes=[pltpu.VMEM((tm, 2 * ns), F32), pltpu.VMEM((2, SUB, ns), F32)],
        compiler_params=_cp("arbitrary"), name=name)(dgy, y, u, xs, mct, mbt, qw, dskip)


def _first(accs, *_):
    return [accs[0]]


def _rms_bwd_epi(accs, xv, base, rv, g):
    dv = accs[0]
    w = dv * g
    xh = xv * rv
    dx = base + rv * (w - xh * jnp.mean(w * xh, axis=-1, keepdims=True))
    return [dx, dx, jnp.sum(dv * xh, axis=0, keepdims=True)]


def mm_rms_bwd(name, pairs, x, r, gain, dres):
    t, d = x.shape
    return mm_nn(name, t, d, pairs, 1, _rms_bwd_epi, [F32, BF16], tiled=[x, dres], cols=[r], rowv=[gain], sums=[(1, d)])


def _add_res(accs, res):
    return [accs[0] + res]


def even_fwd(x, w):
    t = x.shape[0]
    hn, r = rms_fwd("e_norm_f", x, w["e_norm"])
    (proj,) = mm_nn("e_in_f", t, IN_WIDTH, [(hn, w["e_w_in_t"], 0, "t")], 1, _first, [F32])
    out_a = gmlp_fwd("e_gmlp_f", proj, w["e_gmlp_w"], w["e_gmlp_b"])
    hc = conv_fwd("e_conv_f", proj, w["e_conv_w"], w["e_conv_b"])
    out_b = ln_silu_fwd("e_ln_f", hc, w["e_conv_ln_g"], w["e_conv_ln_b"])
    (x1,) = mm_nn("e_out_f", t, D_MODEL, [(out_a, (w["e_w_out"], 0), 0), (out_b, (w["e_w_out"], 1), 0)],
                  1, _add_res, [F32], tiled=[x])
    return x1, (x, hn, r, proj, out_a, hc, out_b)


def even_bwd_mixers(dxb, saved, w):
    x, hn, r, proj, out_a, hc, out_b = saved
    t = x.shape[0]
    (dcat,) = mm_nn("e_out_b", t, D_MODEL, [(dxb, w["e_w_out"], 0, "t")], 1, _first, [F32])
    g_w_out = jnp.concatenate([mm_tn("e_out_wa", out_a, dxb), mm_tn("e_out_wb", out_b, dxb)], axis=0)
    dab, g_gw, g_gb = gmlp_bwd("e_gmlp_b", proj, dcat, w["e_gmlp_w"], w["e_gmlp_b"])
    dhc, g_lg, g_lb = ln_silu_bwd("e_ln_b", hc, dcat, w["e_conv_ln_g"], w["e_conv_ln_b"])
    dba, dbg, g_cw, g_cb = conv_bwd("e_conv_b", proj, dhc, w["e_conv_w"])
    g_w_in_t = jnp.concatenate([mm_tn("e_in_w0", dab, hn), mm_tn("e_in_w1", dba, hn), mm_tn("e_in_w2", dbg, hn)], axis=0)
    grads = dict(e_w_in_t=g_w_in_t, e_gmlp_w=g_gw[None], e_gmlp_b=g_gb.reshape(1, A_GROUPS, GMLP_BLOCK),
                 e_conv_w=g_cw[None], e_conv_b=g_cb, e_conv_ln_g=g_lg, e_conv_ln_b=g_lb, e_w_out=g_w_out)
    return (dab, dba, dbg), grads


def even_bwd_input(dx, dproj, saved, w):
    x, _, r = saved[:3]
    dab, dba, dbg = dproj
    w_in_t = w["e_w_in_t"]
    return mm_rms_bwd("e_in_b", [(dab, (w_in_t, 0), 0), (dba, (w_in_t, 2), 0), (dbg, (w_in_t, 3), 0)], x, r, w["e_norm"], dx)


def s5_setup(w):
    def rows(v):
        return v.transpose(0, 2, 1).reshape(_RP)

    lam = (w["o_lam_re"], w["o_lam_im"], w["o_log_dt"].reshape(C_GROUPS, 1), rows(w["o_b_re"]), rows(w["o_b_im"]))
    a, bbr, bbi = s5_discretise("o_s5_zoh", *lam)
    c_re, c_im = w["o_c_re"], w["o_c_im"]
    pw, qw, mb, mc, mct = s5_operands("o_s5_ops", a.reshape(2, N_STATE), bbr, bbi, c_re.reshape(_RP), c_im.reshape(_RP),
                                      c_re.transpose(2, 0, 1).reshape(C_STATE, C_WIDTH),
                                      c_im.transpose(2, 0, 1).reshape(C_STATE, C_WIDTH))
    return dict(lam=lam, pw=pw, qw=qw, mb=mb, mc=mc, mct=mct, mbt=mb.T)


def odd_fwd(x, w, consts):
    t = x.shape[0]
    hn, r = rms_fwd("o_norm_f", x, w["o_norm"])
    (u,) = mm_nn("o_in_f", t, C_WIDTH, [(hn, w["o_w_in"], 0)], 1, _first, [F32])
    gy, y, xs, xsb = s5_fwd("o_s5_f", u, consts["mb"], consts["mc"], consts["pw"], w["o_d"])
    w_out_t = w["o_w_out_t"]

    def epi(accs, res):
        return [res + accs[0] * _sigmoid(accs[1]), accs[0], accs[1]]

    x1, o1, o2 = mm_nn("o_out_f", t, D_MODEL, [(gy, (w_out_t, 0), 0, "t"), (gy, (w_out_t, D_MODEL), 1, "t")], 2, epi,
                       [F32, BF16, BF16], tiled=[x])
    return x1, (x, hn, r, u, gy, y, xs, xsb, o1, o2)


def odd_bwd(dx, dxb, saved, w, consts):
    x, hn, r, u, gy, y, xs, xsb, o1, o2 = saved
    t = x.shape[0]

    def gate_bwd(dv, a, b):
        a = a.astype(F32)
        sg = _sigmoid(b.astype(F32))
        return [jnp.concatenate([dv * sg, dv * a * sg * (1.0 - sg)], axis=1)], []

    (do12,) = rows_call("o_gate_b", gate_bwd, [dx, o1, o2], [], [(2 * D_MODEL, BF16)], [])
    (dgy,) = mm_nn("o_out_b", t, C_WIDTH, [(do12, w["o_w_out_t"], 0)], 1, _first, [F32])
    g_w_out_t = mm_tn("o_out_w", do12, gy)
    du, dyb, lamb, da8, g_d = s5_bwd("o_s5_b", dgy, y, u, xs, consts["mct"], consts["mbt"], consts["qw"], w["o_d"])
    d_mb = mm_tn("o_s5_wb", u, lamb, out_dtype=F32)
    d_mc = mm_tn("o_s5_wc", xsb, dyb, out_dtype=F32)
    da = jnp.sum(da8, axis=1).reshape((2,) + _GP)
    g_lr, g_li, g_dt, g_btr, g_bti, g_cr, g_ci = s5_param_grads("o_s5_pg", d_mb, d_mc, da, *consts["lam"])

    def states_first(v):
        return v.reshape(C_GROUPS, C_GROUP_CH, C_STATE).transpose(0, 2, 1)[None]

    g_w_in = mm_tn("o_in_w", hn, du)
    dx0, dx0b, g_norm = mm_rms_bwd("o_in_b", [(du, w["o_w_in"], 0, "t")], x, r, w["o_norm"], dx)
    grads = dict(o_norm=g_norm, o_w_in=g_w_in, o_lam_re=g_lr[None], o_lam_im=g_li[None], o_log_dt=g_dt.reshape(1, C_GROUPS),
                 o_b_re=states_first(g_btr), o_b_im=states_first(g_bti),
                 o_c_re=g_cr.reshape((1, C_GROUPS, C_GROUP_CH, C_STATE)), o_c_im=g_ci.reshape((1, C_GROUPS, C_GROUP_CH, C_STATE)),
                 o_d=g_d, o_w_out_t=g_w_out_t)
    return dx0, dx0b, grads


def ca_fwd(i, x, mem, w):
    t, m = x.shape[0], mem.shape[0]
    xn, r = rms_fwd(f"ca{i}_norm_f", x, w["ca_norm"][i:i + 1])
    mn, rm = rms_fwd(f"ca{i}_mnorm_f", mem, w["ca_mem_norm"][i:i + 1])
    (q,) = mm_nn(f"ca{i}_q_f", t, D_MODEL, [(xn, w["ca_wq"][i], 0)], 1, _first, [BF16])
    k, v = mm_nn(f"ca{i}_kv_f", m, D_MODEL, [(mn, w["ca_wk"][i], 0), (mn, w["ca_wv"][i], 1)], 2,
                 lambda accs: [accs[0], accs[1]], [BF16, BF16])
    o = attn_fwd(f"ca{i}_attn_f", q, k, v)
    (x1,) = mm_nn(f"ca{i}_o_f", t, D_MODEL, [(o, w["ca_wo"][i], 0)], 1, _add_res, [F32], tiled=[x])
    return x1, (x, xn, r, mn, rm, q, k, v, o)


def ca_bwd(i, dx, dxb, saved, mem, w):
    x, xn, r, mn, rm, q, k, v, o = saved
    t, m = x.shape[0], mem.shape[0]
    (do,) = mm_nn(f"ca{i}_o_b", t, D_MODEL, [(dxb, w["ca_wo"][i], 0, "t")], 1, _first, [BF16])
    g_wo = mm_tn(f"ca{i}_o_w", o, dxb)
    dq, dk, dv = attn_bwd(f"ca{i}_attn_b", q, k, v, do)
    g_wq = mm_tn(f"ca{i}_q_w", xn, dq)
    g_wk = mm_tn(f"ca{i}_k_w", mn, dk)
    g_wv = mm_tn(f"ca{i}_v_w", mn, dv)
    (dmn,) = mm_nn(f"ca{i}_kv_b", m, D_MODEL, [(dk, w["ca_wk"][i], 0, "t"), (dv, w["ca_wv"][i], 0, "t")], 1, _first, [F32])
    g_mnorm = rms_bwd_gain_only(f"ca{i}_mnorm_b", dmn, mem, rm)
    dx0, dx0b, g_norm = mm_rms_bwd(f"ca{i}_q_b", [(dq, w["ca_wq"][i], 0, "t")], x, r, w["ca_norm"][i:i + 1], dx)
    return dx0, dx0b, dict(ca_norm=g_norm, ca_mem_norm=g_mnorm, ca_wq=g_wq, ca_wk=g_wk, ca_wv=g_wv, ca_wo=g_wo)


def ffn_fwd(i, x, w):
    t = x.shape[0]
    xn, r = rms_fwd(f"ffn{i}_norm_f", x, w["ffn_norm"][i:i + 1])

    def epi(accs):
        g, u = accs
        return [g, u, g * _sigmoid(g) * u]

    g, u, h = mm_nn(f"ffn{i}_up_f", t, FFN_HIDDEN, [(xn, w["ffn_w_gate_t"][i], 0, "t"), (xn, w["ffn_w_up_t"][i], 1, "t")],
                    2, epi, [BF16, BF16, BF16])
    (x1,) = mm_nn(f"ffn{i}_down_f", t, D_MODEL, [(h, w["ffn_w_down"][i], 0)], 1, _add_res, [F32], tiled=[x])
    return x1, (x, xn, r, g, u, h)


def ffn_bwd(i, dx, dxb, saved, w):
    x, xn, r, g, u, h = saved
    t = x.shape[0]

    def epi(accs, gv, uv):
        dh = accs[0]
        gv = gv.astype(F32)
        uv = uv.astype(F32)
        s = _sigmoid(gv)
        return [dh * uv * s * (1.0 + gv * (1.0 - s)), dh * gv * s]

    dg, du = mm_nn(f"ffn{i}_down_b", t, FFN_HIDDEN, [(dxb, w["ffn_w_down"][i], 0, "t")], 1, epi, [BF16, BF16], tiled=[g, u])
    g_wd = mm_tn(f"ffn{i}_down_w", h, dxb)
    g_wg_t = mm_tn(f"ffn{i}_gate_w", dg, xn)
    g_wu_t = mm_tn(f"ffn{i}_up_w", du, xn)
    dx0, dx0b, g_norm = mm_rms_bwd(f"ffn{i}_up_b", [(dg, w["ffn_w_gate_t"][i], 0), (du, w["ffn_w_up_t"][i], 0)], x, r,
                                   w["ffn_norm"][i:i + 1], dx)
    return dx0, dx0b, dict(ffn_norm=g_norm, ffn_w_gate_t=g_wg_t, ffn_w_up_t=g_wu_t, ffn_w_down=g_wd)


def local_step(x, mem, target, w, fetch=None, on_grads=None):
    consts = s5_setup(w)

    def need(stage, after):
        if fetch is not None:
            for k, v in fetch(stage, after).items():
                if isinstance(k, tuple):
                    w.setdefault(k[0], {})[k[1]] = v
                else:
                    w[k] = v

    need(0, x)
    x1, s_e = even_fwd(x, w)
    need(1, x1)
    x2, s_c0 = ca_fwd(0, x1, mem, w)
    need(2, x2)
    x3, s_f0 = ffn_fwd(0, x2, w)
    x4, s_o = odd_fwd(x3, w, consts)
    need(3, x4)
    x5, s_c1 = ca_fwd(1, x4, mem, w)
    x6, s_f1 = ffn_fwd(1, x5, w)
    dx, dxb, g_final, loss = final_loss("final_loss", x6, w["final_norm"], target)

    def emit(stage, carry, plain, layered=None, layer=0):
        if on_grads is None:
            return carry
        out = dict(plain)
        out.update({(k, layer): v for k, v in (layered or {}).items()})
        return on_grads(stage, out, list(carry))

    dx, dxb, g_f1 = ffn_bwd(1, dx, dxb, s_f1, w)
    dx, dxb = emit(0, (dx, dxb), {}, g_f1, 1)
    dx, dxb, g_c1 = ca_bwd(1, dx, dxb, s_c1, mem, w)
    dx, dxb, g_o = odd_bwd(dx, dxb, s_o, w, consts)
    dx, dxb = emit(1, (dx, dxb), g_o, g_c1, 1)
    dx, dxb, g_f0 = ffn_bwd(0, dx, dxb, s_f0, w)
    dx, dxb = emit(2, (dx, dxb), {}, g_f0, 0)
    dx, dxb, g_c0 = ca_bwd(0, dx, dxb, s_c0, mem, w)
    dx, dxb = emit(3, (dx, dxb), {}, g_c0, 0)
    dproj, g_e = even_bwd_mixers(dxb, s_e, w)
    dproj = emit(4, dproj, {**g_e, "o_norm": g_o["o_norm"], "o_d": g_o["o_d"]})
    dx, dxb, g_e["e_norm"] = even_bwd_input(dx, dproj, s_e, w)

    grads = dict(g_e)
    grads.update(g_o)
    for g0, g1 in ((g_c0, g_c1), (g_f0, g_f1)):
        for k in g0:
            grads[k] = jnp.concatenate([g0[k], g1[k]], axis=0) if k.endswith("norm") else (g0[k], g1[k])
    grads["final_norm"] = g_final
    return loss, dx, grads


def _group(axes):
    pos = {a: lax.axis_index(a) for a in ("x", "y", "c")}
    me = 0
    for a in axes:
        me = me * 2 + pos[a]
    peers = []
    for mask in range(1, 2 ** len(axes)):
        peer = dict(pos)
        for bit, a in enumerate(axes):
            if (mask >> (len(axes) - 1 - bit)) & 1:
                peer[a] = 1 - pos[a]
        idx = 0
        for a in axes:
            idx = idx * 2 + peer[a]
        peers.append((idx, (peer["x"], peer["y"], peer["c"])))
    return me, peers


def _sibling():
    x, y, c = lax.axis_index("x"), lax.axis_index("y"), lax.axis_index("c")
    return c, (x, y, 1 - c)


_HBM =pl.BlockSpec(memory_space=pltpu.HBM)
_SEM = pl.BlockSpec(memory_space=pltpu.SEMAPHORE)
_EFFECT = pltpu.SideEffectType.DATAFLOW_SIDE_EFFECTING


def gather_ici_start(name, groups):
    flat = [b for g in groups for b in g]
    sizes = [len(g) for g in groups]
    k_ops, n_g = len(flat), len(groups)
    lands = [lax.empty((4, 2) + tuple(b.shape), b.dtype) for b in flat]

    def body(*refs):
        src, land = refs[:k_ops], refs[k_ops:2 * k_ops]
        sems = refs[2 * k_ops:2 * k_ops + 3 * n_g]
        token = refs[-1]
        me, peers = _group(("x", "y"))
        core = lax.axis_index("c")
        i = 0
        for g in range(n_g):
            send, recv, loc = sems[3 * g:3 * g + 3]
            for j in range(sizes[g]):
                pltpu.make_async_copy(src[i], land[i].at[me, core], loc.at[j]).start()
                for k, (_, dev) in enumerate(peers):
                    pltpu.make_async_remote_copy(src_ref=src[i], dst_ref=land[i].at[me, core], send_sem=send.at[3 * j + k],
                                                 recv_sem=recv.at[3 * j + k], device_id=dev, device_id_type=MESH).start()
                i += 1
        token[...] = jnp.zeros(token.shape, token.dtype)

    sem_shapes = []
    for s in sizes:
        sem_shapes += [pltpu.SemaphoreType.DMA((3 * s,)), pltpu.SemaphoreType.DMA((3 * s,)), pltpu.SemaphoreType.DMA((s,))]
    thru = [pltpu.HBM(a.shape, a.dtype) for a in flat + lands]
    outs = pl.pallas_call(
        body, name=name, out_shape=tuple(sem_shapes) + tuple(thru) + (S((8, LANES), F32),),
        in_specs=[_HBM] * (2 * k_ops), out_specs=[_SEM] * (3 * n_g) + [_HBM] * (2 * k_ops) + [pl.BlockSpec(memory_space=pltpu.VMEM)],
        input_output_aliases={i: 3 * n_g + i for i in range(2 * k_ops)},
        compiler_params=pltpu.CompilerParams(has_side_effects=_EFFECT),
    )(*[pltpu.with_memory_space_constraint(a, pltpu.HBM) for a in flat + lands])
    sems = [tuple(outs[3 * g:3 * g + 3]) for g in range(n_g)]
    srcs_thru, lands_thru, off = [], [], 3 * n_g
    for s in sizes:
        srcs_thru.append(list(outs[off:off + s]))
        off += s
    for s in sizes:
        lands_thru.append(list(outs[off:off + s]))
        off += s
    return sems, srcs_thru, lands_thru, outs[-1]


def gather_ici_wait(name, srcs, lands, sems, after):
    n = len(srcs)

    def body(*refs):
        src, land = refs[:n], refs[n:2 * n]
        send, recv, loc = refs[2 * n:2 * n + 3]
        me, peers = _group(("x", "y"))
        core = lax.axis_index("c")
        for j in range(n):
            for k, (idx, dev) in enumerate(peers):
                cp = pltpu.make_async_remote_copy(src_ref=src[j], dst_ref=land[j].at[idx, core], send_sem=send.at[3 * j + k],
                                                  recv_sem=recv.at[3 * j + k], device_id=dev, device_id_type=MESH)
                cp.wait_send()
                cp.wait_recv()
            pltpu.make_async_copy(src[j], land[j].at[me, core], loc.at[j]).wait()

    outs = pl.pallas_call(
        body, name=name, out_shape=tuple(pltpu.HBM(a.shape, a.dtype) for a in list(srcs) + list(lands)),
        in_specs=[_HBM] * (2 * n) + [_SEM] * 3 + [ANY], out_specs=[_HBM] * (2 * n),
        input_output_aliases={i: i for i in range(2 * n)},
        compiler_params=pltpu.CompilerParams(has_side_effects=_EFFECT),
    )(*srcs, *lands, *sems, after)
    return list(outs[n:])


def gather_d2d(name, bufs):
    k_ops = len(bufs)

    def body(*refs):
        in_refs, out_refs = refs[:k_ops], refs[k_ops:2 * k_ops]
        send_sems, recv_sems = refs[2 * k_ops:]
        core, sib = _sibling()
        sent, landed = [], []
        for i in range(k_ops):
            cp = pltpu.make_async_remote_copy(src_ref=in_refs[i].at[:, core], dst_ref=out_refs[i].at[:, core],
                                              send_sem=send_sems.at[i], recv_sem=recv_sems.at[i], device_id=sib, device_id_type=MESH)
            cp.start()
            sent.append(cp)
            landed.append(pltpu.make_async_remote_copy(src_ref=in_refs[i].at[:, core], dst_ref=out_refs[i].at[:, 1 - core],
                                                       send_sem=send_sems.at[i], recv_sem=recv_sems.at[i],
                                                       device_id=sib, device_id_type=MESH))
        for cp in landed:
            cp.wait_recv()
        for cp in sent:
            cp.wait_send()

    return pl.pallas_call(
        body, in_specs=[ANY] * k_ops, out_specs=[ANY] * k_ops, out_shape=[S(b.shape, b.dtype) for b in bufs],
        input_output_aliases={i: i for i in range(k_ops)},
        scratch_shapes=[pltpu.SemaphoreType.DMA((k_ops,)), pltpu.SemaphoreType.DMA((k_ops,))],
        name=name)(*bufs)


def scatter_d2d(name, pack):
    q, _, rows, c = pack.shape

    def body(in_ref, out_ref, send_sem, recv_sem):
        core, sib = _sibling()
        cp = pltpu.make_async_remote_copy(src_ref=in_ref.at[:, 1 - core], dst_ref=out_ref, send_sem=send_sem, recv_sem=recv_sem,
                                          device_id=sib, device_id_type=MESH)
        cp.start()
        cp.wait_recv()
        cp.wait_send()

    return pl.pallas_call(
        body, in_specs=[ANY], out_specs=ANY, out_shape=S((q, rows, c), pack.dtype),
        scratch_shapes=[pltpu.SemaphoreType.DMA, pltpu.SemaphoreType.DMA], name=name)(pack)


def scatter_ici_start(name, arr, carry):
    land = lax.empty(arr.shape, arr.dtype)
    n_c = len(carry)

    def body(*refs):
        in_ref, land_ref = refs[0], refs[1]
        send, recv = refs[2 + n_c], refs[3 + n_c]
        me, peers = _group(("x", "y"))
        for k, (idx, dev) in enumerate(peers):
            pltpu.make_async_remote_copy(src_ref=in_ref.at[idx], dst_ref=land_ref.at[me], send_sem=send.at[k], recv_sem=recv.at[k],
                                         device_id=dev, device_id_type=MESH).start()

    thru = [arr, land] + list(carry)
    outs = pl.pallas_call(
        body, name=name,
        out_shape=(pltpu.SemaphoreType.DMA((3,)), pltpu.SemaphoreType.DMA((3,))) + tuple(pltpu.HBM(a.shape, a.dtype) for a in thru),
        in_specs=[_HBM] * len(thru), out_specs=[_SEM, _SEM] + [_HBM] * len(thru),
        input_output_aliases={i: 2 + i for i in range(len(thru))},
        compiler_params=pltpu.CompilerParams(has_side_effects=_EFFECT),
    )(*[pltpu.with_memory_space_constraint(a, pltpu.HBM) for a in thru])
    return (outs[0], outs[1]), outs[2], outs[3], list(outs[4:])


def scatter_ici_wait(name, arr, land, sems, after):
    def body(in_ref, land_ref, send, recv, after_ref, in_thru, land_thru):
        _, peers = _group(("x", "y"))
        for k, (idx, dev) in enumerate(peers):
            cp = pltpu.make_async_remote_copy(src_ref=in_ref.at[idx], dst_ref=land_ref.at[idx], send_sem=send.at[k],
                                              recv_sem=recv.at[k], device_id=dev, device_id_type=MESH)
            cp.wait_send()
            cp.wait_recv()

    outs = pl.pallas_call(
        body, name=name, out_shape=(pltpu.HBM(arr.shape, arr.dtype), pltpu.HBM(arr.shape, arr.dtype)),
        in_specs=[_HBM, _HBM, _SEM, _SEM, ANY], out_specs=[_HBM, _HBM], input_output_aliases={0: 0, 1: 1},
        compiler_params=pltpu.CompilerParams(has_side_effects=_EFFECT),
    )(arr, land, sems[0], sems[1], after)
    return outs[0], outs[1]


def _row_tile(rows, cap=512):
    return next(t for t in range(cap - cap % 16, 0, -16) if rows % t == 0)


def sum_pair(name, pack, recv, core):
    q, rows, c = recv.shape
    tr = _row_tile(rows)

    def body(core_ref, a_ref, b_ref, o_ref):
        o_ref[...] = (a_ref[...].astype(F32) + b_ref[...].astype(F32)).astype(o_ref.dtype)

    spec = pltpu.PrefetchScalarGridSpec(
        num_scalar_prefetch=1, grid=(q, rows // tr),
        in_specs=[pl.BlockSpec((None, None, tr, c), lambda j, i, core: (j, core[0], i, 0)),
                  pl.BlockSpec((None, tr, c), lambda j, i, core: (j, i, 0))],
        out_specs=pl.BlockSpec((None, tr, c), lambda j, i, core: (j, i, 0)))
    return pl.pallas_call(body, grid_spec=spec, out_shape=S(recv.shape, recv.dtype),
                          compiler_params=_cp("parallel", "parallel"), name=name)(core, pack, recv)


def sum_quad(name, own, recv, chip):
    _, rows, c = recv.shape
    tr = _row_tile(rows)

    def body(chip_ref, a_ref, r1_ref, r2_ref, r3_ref, o_ref):
        o_ref[...] = ((a_ref[...].astype(F32) + r1_ref[...].astype(F32)) + r2_ref[...].astype(F32)) + r3_ref[...].astype(F32)

    def slot(mask):
        return pl.BlockSpec((None, tr, c), lambda i, chip, mask=mask: (jnp.bitwise_xor(chip[0], mask), i, 0))

    spec = pltpu.PrefetchScalarGridSpec(
        num_scalar_prefetch=1, grid=(rows // tr,), in_specs=[slot(0), slot(1), slot(2), slot(3)],
        out_specs=pl.BlockSpec((tr, c), lambda i, chip: (i, 0)))
    return pl.pallas_call(body, grid_spec=spec, out_shape=S((rows, c), F32),
                          compiler_params=_cp("parallel"), name=name)(chip, own, recv, recv, recv)


def sum_slots(name, slots):
    n, r, c = slots.shape

    def body(s_ref, o_ref):
        acc = s_ref[0]
        for j in range(1, n):
            acc = acc + s_ref[j]
        o_ref[...] = acc

    return pl.pallas_call(body, out_shape=S((r, c), F32), compiler_params=pltpu.CompilerParams(vmem_limit_bytes=VMEM_LIMIT),
                          name=name)(slots)


def adamw_native(name, g, w, m, v, tr=512):
    shape = w.shape
    cols = shape[-1]
    rows = w.size // cols
    tr = _tile(rows, tr) if rows % 8 == 0 else rows
    c1 = 1.0 - ADAM_B1 ** ADAM_STEP
    c2 = 1.0 - ADAM_B2 ** ADAM_STEP

    def body(g_ref, w_ref, m_ref, v_ref, d_ref, m2_ref, v2_ref):
        gv = g_ref[...]
        m2 = ADAM_B1 * m_ref[...] + (1.0 - ADAM_B1) * gv
        v2 = ADAM_B2 * v_ref[...] + (1.0 - ADAM_B2) * (gv * gv)
        m2_ref[...] = m2
        v2_ref[...] = v2
        d_ref[...] = -ADAM_LR * ((m2 / c1) / (jnp.sqrt(v2 / c2) + ADAM_EPS) + ADAM_WD * w_ref[...])

    row = pl.BlockSpec((tr, cols), lambda i: (i, 0))
    outs = pl.pallas_call(body, grid=(rows // tr,), in_specs=[row] * 4, out_specs=[row] * 3,
                          out_shape=[S((rows, cols), F32)] * 3, compiler_params=_cp("parallel"),
                          name=name)(*[a.reshape(rows, cols) for a in (g, w, m, v)])
    return tuple(o.reshape(shape) for o in outs)


_REPLICATED = ("e_norm", "e_gmlp_w", "e_gmlp_b", "e_conv_b", "e_conv_ln_g", "e_conv_ln_b", "o_lam_re", "o_lam_im", "o_log_dt",
               "o_b_re", "o_b_im", "o_c_re", "o_c_im", "ca_norm", "ca_mem_norm", "ffn_norm", "final_norm")
_ORDER = ("e_norm", "e_w_in", "e_gmlp_w", "e_gmlp_b", "e_conv_w", "e_conv_b", "e_conv_ln_g", "e_conv_ln_b", "e_w_out",
          "o_norm", "o_w_in", "o_lam_re", "o_lam_im", "o_log_dt", "o_b_re", "o_b_im", "o_c_re", "o_c_im", "o_d", "o_w_out",
          "ca_norm", "ca_mem_norm", "ca_wq", "ca_wk", "ca_wv", "ca_wo", "ffn_norm", "ffn_w_gate", "ffn_w_up", "ffn_w_down",
          "final_norm")


def _rows128(a, multiple=8):
    flat = a.reshape(-1)
    rows = -(-flat.shape[0] // (LANES * multiple)) * multiple
    return jnp.pad(flat, (0, rows * LANES - flat.shape[0])).reshape(rows, LANES)


def _shard(full, axis):
    s = full.shape
    return jnp.moveaxis(full.reshape(s[:axis] + (N_DEV, s[axis] // N_DEV) + s[axis + 1:]), axis, 0)


_UNITS = (("e_w_in", 0, True), ("e_w_out", 0, False), ("o_w_in", 0, False), ("o_w_out", 0, True),
          *[(n, i, False) for n in ("ca_wq", "ca_wk", "ca_wv", "ca_wo") for i in (0, 1)],
          *[(n, i, tr) for n, tr in (("ffn_w_gate", True), ("ffn_w_up", True), ("ffn_w_down", False)) for i in (0, 1)])
_LAYERED = ("ca_wq", "ca_wk", "ca_wv", "ca_wo", "ffn_w_gate", "ffn_w_up", "ffn_w_down")
_SMALL_SHARDED = (("e_conv_w", 2), ("o_norm", 1), ("o_d", 1))
RS_ROW = 1024


def _unit_key(name, tr):
    return name + "_t" if tr else name


def _stage_of(name, layer):
    if name.startswith("e_"):
        return 0
    if name.startswith("o_"):
        return 2
    if name.startswith("ca_"):
        return 1 if layer == 0 else 3
    return 2 if layer == 0 else 3


def weight_fetcher(local):
    groups, meta = [[] for _ in range(4)], [[] for _ in range(4)]
    for name, layer, tr in _UNITS:
        blk = local[name][layer]
        st = _stage_of(name, layer)
        groups[st].append(_bf(blk.T if tr else blk))
        meta[st].append((name, layer, tr))
    small = jnp.concatenate([local[name].reshape(-1) for name, _ in _SMALL_SHARDED])
    groups[0].append(_rows128(small))
    sems, srcs, lands, token = gather_ici_start("ag_w_start", groups)

    def fetch(stage, after):
        if stage == 0:
            after = token
        landed = gather_ici_wait(f"ag_w_wait{stage}", srcs[stage], lands[stage], sems[stage], after)
        bufs = gather_d2d(f"ag_w_d2d{stage}", landed)
        got = {}
        for (name, layer, tr), blk, buf in zip(meta[stage], groups[stage], bufs):
            arr = buf.reshape((N_DEV * blk.shape[0],) + tuple(blk.shape[1:]))
            if name in _LAYERED:
                got[(_unit_key(name, tr), layer)] = arr
            else:
                got[_unit_key(name, tr)] = arr
        if stage == 0:
            flat = bufs[-1].reshape(N_DEV, -1)
            off = 0
            for name, axis in _SMALL_SHARDED:
                blk = local[name]
                seg = flat[:, off:off + blk.size].reshape((N_DEV,) + blk.shape)
                off += blk.size
                seg = jnp.moveaxis(seg, 0, axis)
                got[name] = seg.reshape(seg.shape[:axis] + (-1,) + seg.shape[axis + 2:])
            got["e_conv_w"] = got["e_conv_w"][0]
        return got

    return fetch


def _grad_stage_of(name, layer):
    if name.startswith("e_"):
        return 4
    if name.startswith("o_"):
        return 1
    if name.startswith("ca_"):
        return 3 if layer == 0 else 1
    return 2 if layer == 0 else 0


GRAD_STAGES = 5
SMALL_ROWS = 16


def gradient_reducer(local, mom, var):
    core = lax.axis_index("c").astype(jnp.int32).reshape(1)
    chip = (2 * lax.axis_index("x") + lax.axis_index("y")).astype(jnp.int32).reshape(1)
    pending = []

    def start(stage, grads, carry):
        units = [u for u in _UNITS if _grad_stage_of(u[0], u[1]) == stage]
        parts, spans = [], []
        for name, layer, tr in units:
            key = _unit_key(name, tr)
            g = grads[(key, layer)] if name in _LAYERED else grads[key]
            part = g.reshape(4, 2, -1, RS_ROW)
            spans.append((part.shape[2], g.shape[0] // N_DEV, g.shape[1]))
            parts.append(part)
        if stage == GRAD_STAGES - 1:
            small = jnp.concatenate([_shard(grads[name], axis).reshape(N_DEV, -1) for name, axis in _SMALL_SHARDED], axis=1)
            small = jnp.pad(small, ((0, 0), (0, SMALL_ROWS * RS_ROW - small.shape[1])))
            parts.append(small.astype(BF16).reshape(4, 2, SMALL_ROWS, RS_ROW))
        pack = jnp.concatenate(parts, axis=2)
        from_sibling = scatter_d2d(f"rs_d2d{stage}", pack)
        chip_sum = sum_pair(f"rs_pair{stage}", pack, from_sibling, core)
        sems, own, land, carry = scatter_ici_start(f"rs_start{stage}", chip_sum, carry)
        pending.append((stage, units, spans, sems, own, land))
        return carry

    def finish(after):
        res, per_layer, small_flat = {}, {}, None
        for stage, units, spans, sems, own, land in pending:
            own, land = scatter_ici_wait(f"rs_wait{stage}", own, land, sems, after)
            total = sum_quad(f"rs_quad{stage}", own, land, chip)
            off = 0
            for (name, layer, tr), (rows, r, c) in zip(units, spans):
                g = total[off:off + rows].reshape(r, c)
                off += rows
                per_layer.setdefault(name, {})[layer] = g.T if tr else g
            if stage == GRAD_STAGES - 1:
                small_flat = total[off:off + SMALL_ROWS].reshape(-1)
        for name, by_layer in per_layer.items():
            g = jnp.stack([by_layer[i] for i in sorted(by_layer)]) if name in _LAYERED else by_layer[0][None]
            res[name] = (g,) + adamw_native("adamw_" + name, g, local[name], mom[name], var[name])
        off = 0
        for name, _ in _SMALL_SHARDED:
            blk = local[name]
            g = small_flat[off:off + blk.size].reshape(blk.shape)
            off += blk.size
            res[name] = (g,) + adamw_native("adamw_" + name, g, blk, mom[name], var[name])
        return res

    return start, finish


def replicated_start(grads, loss):
    pack = jnp.concatenate([_rows128(grads[name]) for name in _REPLICATED] + [_rows128(loss)], axis=0)
    sems, srcs, lands, token = gather_ici_start("ag_g_start", [[pack]])
    return sems[0], srcs[0], lands[0], token


def replicated_finish(handle, after, w, mom, var):
    sems, srcs, lands, _ = handle
    (buf,) = gather_d2d("ag_g_d2d", gather_ici_wait("ag_g_wait", srcs, lands, sems, after))
    rows = srcs[0].shape[0]
    total = sum_slots("ag_g_sum", buf.reshape(N_DEV, rows, LANES))
    res, off = {}, 0
    for name in _REPLICATED:
        n = w[name].size
        nr = -(-n // (LANES * 8)) * 8
        g = total[off:off + nr].reshape(-1)[:n].reshape(w[name].shape)
        off += nr
        res[name] = (g,) + adamw_native("adamw_" + name, g, w[name], mom[name], var[name])
    return res, total[off, 0]


def kernel(x, mem, e_norm, e_w_in, e_gmlp_w, e_gmlp_b, e_conv_w, e_conv_b, e_conv_ln_g, e_conv_ln_b, e_w_out, o_norm, o_w_in, o_lam_re, o_lam_im, o_log_dt, o_b_re, o_b_im, o_c_re, o_c_im, o_d, o_w_out, ca_norm, ca_mem_norm, ca_wq, ca_wk, ca_wv, ca_wo, ffn_norm, ffn_w_gate, ffn_w_up, ffn_w_down, final_norm, loss_target, m_e_norm, m_e_w_in, m_e_gmlp_w, m_e_gmlp_b, m_e_conv_w, m_e_conv_b, m_e_conv_ln_g, m_e_conv_ln_b, m_e_w_out, m_o_norm, m_o_w_in, m_o_lam_re, m_o_lam_im, m_o_log_dt, m_o_b_re, m_o_b_im, m_o_c_re, m_o_c_im, m_o_d, m_o_w_out, m_ca_norm, m_ca_mem_norm, m_ca_wq, m_ca_wk, m_ca_wv, m_ca_wo, m_ffn_norm, m_ffn_w_gate, m_ffn_w_up, m_ffn_w_down, m_final_norm, v_e_norm, v_e_w_in, v_e_gmlp_w, v_e_gmlp_b, v_e_conv_w, v_e_conv_b, v_e_conv_ln_g, v_e_conv_ln_b, v_e_w_out, v_o_norm, v_o_w_in, v_o_lam_re, v_o_lam_im, v_o_log_dt, v_o_b_re, v_o_b_im, v_o_c_re, v_o_c_im, v_o_d, v_o_w_out, v_ca_norm, v_ca_mem_norm, v_ca_wq, v_ca_wk, v_ca_wv, v_ca_wo, v_ffn_norm, v_ffn_w_gate, v_ffn_w_up, v_ffn_w_down, v_final_norm):
    given = dict(locals())
    local = {k: given[k] for k in _ORDER}
    mom = {k: given["m_" + k] for k in _ORDER}
    var = {k: given["v_" + k] for k in _ORDER}

    w = {}
    w.update({
        "e_norm": e_norm, "e_gmlp_w": e_gmlp_w[0], "e_gmlp_b": e_gmlp_b.reshape(A_GROUPS, GMLP_BLOCK, 1),
        "e_conv_b": e_conv_b, "e_conv_ln_g": e_conv_ln_g, "e_conv_ln_b": e_conv_ln_b,
        "o_lam_re": o_lam_re[0], "o_lam_im": o_lam_im[0], "o_log_dt": o_log_dt[0], "o_b_re": o_b_re[0], "o_b_im": o_b_im[0],
        "o_c_re": o_c_re[0], "o_c_im": o_c_im[0], "ca_norm": ca_norm, "ca_mem_norm": ca_mem_norm, "ffn_norm": ffn_norm,
        "final_norm": final_norm.reshape(1, D_MODEL),
    })
    start_reduce, finish_reduce = gradient_reducer(local, mom, var)
    loss_part, grad_x, grads = local_step(x[0], mem[0], loss_target[0], w, weight_fetcher(local), start_reduce)
    grads["final_norm"] = grads["final_norm"].reshape(D_MODEL)

    handle = replicated_start(grads, loss_part)
    res = finish_reduce(handle[3])
    rep, loss = replicated_finish(handle, res["ffn_w_down"][1], local, mom, var)
    res.update(rep)
    return (loss, grad_x[None], *[res[k][0] for k in _ORDER], *[res[k][1] for k in _ORDER],
            *[res[k][2] for k in _ORDER], *[res[k][3] for k in _ORDER])
```

```python
import jax
import jax.numpy as jnp
from jax import lax
from jax.experimental import pallas as pl
from jax.experimental.pallas import tpu as pltpu

F32 = jnp.float32
BF16 = jnp.bfloat16
S = jax.ShapeDtypeStruct

D_MODEL = 1024
A_WIDTH = 512
A_GROUPS = 4
GMLP_BLOCK = 128
CHUNK = 64
B_WIDTH = 512
IN_WIDTH = 2 * A_WIDTH + 2 * B_WIDTH
CONV_WIDTH = 31
CONV_PAD = 32
C_WIDTH = 512
C_GROUP_CH = 16
C_GROUPS = 32
C_STATE = 64
N_STATE = C_GROUPS * C_STATE
CA_HEADS = 4
CA_HEAD_DIM = 256
FFN_HIDDEN = 2816
EPS = 1e-6
ADAM_LR = 0.001
ADAM_B1 = 0.9
ADAM_B2 = 0.999
ADAM_EPS = 1e-08
ADAM_WD = 0.01
ADAM_STEP = 10
N_DEV = 8
LANES = 128
VMEM_LIMIT = 56 << 20
VMEM_BUDGET = 40 << 20
MM_TN_RESIDENT = 8 << 20
MESH = pl.DeviceIdType.MESH
ANY = pl.BlockSpec(memory_space=pl.ANY)


def _cp(*sem):
    return pltpu.CompilerParams(dimension_semantics=sem, vmem_limit_bytes=VMEM_LIMIT)


def _tile(n, pref):
    t = pref
    while n % t:
        t //= 2
    return t


def _bf(v):
    return v if v.dtype == BF16 else v.astype(BF16)


def _sigmoid(x):
    return 1.0 / (1.0 + jnp.exp(-x))


_GC = 0.7978845608028654


def _gelu(x):
    return 0.5 * x * (1.0 + jnp.tanh(_GC * (x + 0.044715 * x * x * x)))


def _gelu_grad(x):
    x2 = x * x
    t = jnp.tanh(_GC * (x + 0.044715 * x * x2))
    return 0.5 * (1.0 + t) + 0.5 * x * (1.0 - t * t) * _GC * (1.0 + 3.0 * 0.044715 * x2)


def _tspec(entry, tm):
    if isinstance(entry, tuple):
        arr, cb, width = entry
        return arr, pl.BlockSpec((tm, width), lambda i, cb=cb: (i, cb))
    return entry, pl.BlockSpec((tm, entry.shape[1]), lambda i: (i, 0))


def rows_call(name, fn, tiled, full, outs, accs, tm=256):
    pairs = [_tspec(e, tm) for e in tiled]
    arrs = [p[0] for p in pairs]
    rows = arrs[0].shape[0]
    tm = _tile(rows, tm)
    pairs = [_tspec(e, tm) for e in tiled]
    n_in = len(tiled) + len(full)
    n_out = len(outs)

    def body(*refs):
        vals = [r[...] for r in refs[:n_in]]
        o_refs = refs[n_in:n_in + n_out]
        a_refs = refs[n_in + n_out:]
        ov, av = fn(*vals)
        for r, v in zip(o_refs, ov):
            r[...] = v.astype(r.dtype)
        if a_refs:
            @pl.when(pl.program_id(0) == 0)
            def _():
                for r in a_refs:
                    r[...] = jnp.zeros(r.shape, r.dtype)
            for r, v in zip(a_refs, av):
                r[...] += v

    in_specs = [p[1] for p in pairs] + [pl.BlockSpec(a.shape, lambda i, nd=a.ndim: (0,) * nd) for a in full]
    out_specs = [pl.BlockSpec((tm, c), lambda i: (i, 0)) for c, _ in outs]
    out_specs += [pl.BlockSpec(s, lambda i, nd=len(s): (0,) * nd) for s in accs]
    out_shape = [S((rows, c), dt) for c, dt in outs] + [S(s, F32) for s in accs]
    return pl.pallas_call(body, grid=(rows // tm,), in_specs=in_specs, out_specs=out_specs, out_shape=out_shape,
                          compiler_params=_cp("arbitrary"), name=name)(*arrs, *full)


def mm_nn(name, m, n, pairs, n_acc, epi, outs, tiled=(), cols=(), rowv=(), sums=(), norm_gain=None):
    a_ops, a_slot, b_arrs, b_specs, idx, trans = [], [], [], [], [], []
    fixed = 0
    for pair in pairs:
        a, b, k = pair[:3]
        bt = len(pair) > 3
        arr, cb, kdim = a if isinstance(a, tuple) else (a, 0, a.shape[1])
        key = (id(arr), cb, kdim)
        if key not in [o[0] for o in a_ops]:
            a_ops.append((key, arr, cb, kdim))
        a_slot.append([o[0] for o in a_ops].index(key))
        b_arr, off = b if isinstance(b, tuple) else (b, 0)
        b_arrs.append(b_arr)
        if bt:
            assert off % n == 0 and b_arr.shape[1] == kdim
            b_specs.append(pl.BlockSpec((n, kdim), lambda i, o=off // n: (o, 0), pipeline_mode=pl.Buffered(1)))
        else:
            assert b_arr.shape[1] == n
            b_specs.append(pl.BlockSpec((kdim, n), lambda i, o=off: (o, 0), pipeline_mode=pl.Buffered(1)))
        fixed += kdim * n * b_arr.dtype.itemsize
        idx.append(k)
        trans.append(bt)
    per_row = sum(2 * kdim * arr.dtype.itemsize for _, arr, _, kdim in a_ops)
    per_row += sum(2 * n * t.dtype.itemsize for t in tiled) + sum(2 * n * jnp.dtype(dt).itemsize for dt in outs)
    cn = n if sums or cols else (512 if n % 512 == 0 else 256)
    per_row += (n_acc + 3) * cn * 4
    tm = next((t for t in (1024, 512, 256, 128) if m % t == 0 and fixed + t * per_row <= VMEM_BUDGET), _tile(m, 128))
    n_a, n_p, n_t = len(a_ops), len(pairs), len(tiled)
    n_in = n_a + n_p + n_t + len(cols) + len(rowv)
    normed = norm_gain is not None
    o0 = n_in + normed

    def body(*refs):
        a_vals = [None if normed and i == 0 else _bf(r[...]) for i, r in enumerate(refs[:n_a])]
        if normed:
            xv = refs[0][...]
            rv = lax.rsqrt(jnp.mean(xv * xv, axis=-1, keepdims=True) + EPS)
            a_vals[0] = (xv * rv * refs[n_in][...]).astype(BF16)
            refs[o0 + len(outs)][...] = a_vals[0]
            refs[o0 + len(outs) + 1][...] = rv
        for j in range(n // cn):
            cs = slice(j * cn, (j + 1) * cn)
            accs = [None] * n_acc
            for p in range(n_p):
                av, b_ref = a_vals[a_slot[p]], refs[n_a + p]
                if trans[p]:
                    d = lax.dot_general(av, _bf(b_ref[cs, :]), (((1,), (1,)), ((), ())), preferred_element_type=F32)
                else:
                    d = jnp.dot(av, _bf(b_ref[:, cs]), preferred_element_type=F32)
                accs[idx[p]] = d if accs[idx[p]] is None else accs[idx[p]] + d
            extra = [r[:, cs] for r in refs[n_a + n_p:n_a + n_p + n_t]] + [r[...] for r in refs[n_a + n_p + n_t:n_in - len(rowv)]]
            extra += [r[:, cs] for r in refs[n_in - len(rowv):n_in]]
            ov = epi(accs, *extra)
            for r, v in zip(refs[o0:o0 + len(outs)], ov):
                r[:, cs] = v.astype(r.dtype)
        sv = ov[len(outs):]
        if sums:
            s_refs = refs[o0 + len(outs) + 2 * normed:]

            @pl.when(pl.program_id(0) == 0)
            def _():
                for r in s_refs:
                    r[...] = jnp.zeros(r.shape, r.dtype)
            for r, v in zip(s_refs, sv):
                r[...] += v

    in_specs = [pl.BlockSpec((tm, kdim), lambda i, cb=cb: (i, cb)) for _, _, cb, kdim in a_ops] + b_specs
    in_specs += [pl.BlockSpec((tm, n), lambda i: (i, 0)) for _ in tiled]
    in_specs += [pl.BlockSpec((tm, 1), lambda i: (i, 0)) for _ in cols]
    in_specs += [pl.BlockSpec((1, n), lambda i: (0, 0)) for _ in rowv]
    out_specs = [pl.BlockSpec((tm, n), lambda i: (i, 0)) for _ in outs]
    out_shape = [S((m, n), dt) for dt in outs]
    gain = []
    if normed:
        k0 = a_ops[0][3]
        gain = [norm_gain]
        in_specs.append(pl.BlockSpec((1, k0), lambda i: (0, 0)))
        out_specs += [pl.BlockSpec((tm, k0), lambda i: (i, 0)), pl.BlockSpec((tm, 1), lambda i: (i, 0))]
        out_shape += [S((m, k0), BF16), S((m, 1), F32)]
    out_specs += [pl.BlockSpec(s, lambda i, nd=len(s): (0,) * nd) for s in sums]
    out_shape += [S(s, F32) for s in sums]
    return pl.pallas_call(body, grid=(m // tm,), in_specs=in_specs, out_specs=out_specs, out_shape=out_shape,
                          compiler_params=_cp("arbitrary" if sums else "parallel"),
                          name=name)(*[o[1] for o in a_ops], *b_arrs, *tiled, *cols, *rowv, *gain)


def mm_tn(name, a, b, out_dtype=BF16):
    if isinstance(a, tuple):
        a_arr, a_cb, m = a
    else:
        a_arr, a_cb, m = a, None, a.shape[1]
    if isinstance(b, tuple):
        b_arr, b_cb, n = b
    else:
        b_arr, b_cb, n = b, None, b.shape[1]
    t = a_arr.shape[0]
    whole_b = t * n * b_arr.dtype.itemsize <= MM_TN_RESIDENT and b_cb is None
    tn = n if whole_b else _tile(n, 512)
    tm = _tile(m, 512 if t * 512 * a_arr.dtype.itemsize * 2 + t * tn * b_arr.dtype.itemsize * 2 <= VMEM_BUDGET else 256)
    a_off = 0 if a_cb is None else a_cb * (m // tm)
    b_off = 0 if b_cb is None else b_cb * (n // tn)

    def body(a_ref, b_ref, o_ref):
        o_ref[...] = lax.dot_general(_bf(a_ref[...]), _bf(b_ref[...]), (((0,), (0,)), ((), ())),
                                     preferred_element_type=F32).astype(o_ref.dtype)

    if whole_b:
        b_spec = pl.BlockSpec((t, n), lambda i, j: (0, 0), pipeline_mode=pl.Buffered(1))
    else:
        b_spec = pl.BlockSpec((t, tn), lambda i, j: (0, j + b_off))
    return pl.pallas_call(
        body, grid=(m // tm, n // tn),
        in_specs=[pl.BlockSpec((t, tm), lambda i, j: (0, i + a_off)), b_spec],
        out_specs=pl.BlockSpec((tm, tn), lambda i, j: (i, j)), out_shape=S((m, n), out_dtype),
        compiler_params=_cp("parallel", "parallel"), name=name)(a_arr, b_arr)


def rms_bwd_gain_only(name, dxn, x, r):
    def fn(dv, xv, rv):
        return [], [jnp.sum(dv * xv * rv, axis=0, keepdims=True)]
    return rows_call(name, fn, [dxn, x, r], [], [], [(1, x.shape[1])])[0]


def final_loss(name, x, gain, target):
    d = x.shape[1]

    def fn(xv, tv, g):
        r = lax.rsqrt(jnp.mean(xv * xv, axis=-1, keepdims=True) + EPS)
        xh = xv * r
        err = xh * g - tv
        dy = err * (1.0 / d)
        w = dy * g
        dx = r * (w - xh * jnp.mean(w * xh, axis=-1, keepdims=True))
        part = jnp.sum(jnp.sum(err * err, axis=-1, keepdims=True), axis=0, keepdims=True) * (0.5 / d)
        return [dx, dx], [jnp.sum(dy * xh, axis=0, keepdims=True), part]

    return rows_call(name, fn, [x, target], [gain], [(d, F32), (d, BF16)], [(1, d), (1, 1)])


def _gmlp_mask():
    row = lax.broadcasted_iota(jnp.int32, (GMLP_BLOCK, GMLP_BLOCK), 0) // CHUNK
    col = lax.broadcasted_iota(jnp.int32, (GMLP_BLOCK, GMLP_BLOCK), 1) // CHUNK
    return col <= row


def _ln_plain(v):
    mu = jnp.mean(v, axis=-1, keepdims=True)
    vc = v - mu
    rstd = lax.rsqrt(jnp.mean(vc * vc, axis=-1, keepdims=True) + EPS)
    return vc * rstd, rstd


def gmlp_fwd(name, proj, w, b, tm=512):
    t = proj.shape[0]
    tm = _tile(t, tm)

    def body(au_ref, av_ref, w_ref, b_ref, o_ref):
        mask = _gmlp_mask()
        u = _gelu(au_ref[...])
        vn, _ = _ln_plain(_gelu(av_ref[...]))
        vnb = _bf(vn)
        for g in range(A_GROUPS):
            wg = _bf(jnp.where(mask, w_ref[g], 0.0))
            cs = slice(g * GMLP_BLOCK, (g + 1) * GMLP_BLOCK)
            for n in range(tm // GMLP_BLOCK):
                rs = slice(n * GMLP_BLOCK, (n + 1) * GMLP_BLOCK)
                sg = jnp.dot(wg, vnb[rs, cs], preferred_element_type=F32) + b_ref[g]
                o_ref[rs, cs] = (u[rs, cs] * sg).astype(o_ref.dtype)

    return pl.pallas_call(
        body, grid=(t // tm,),
        in_specs=[pl.BlockSpec((tm, A_WIDTH), lambda i: (i, 0)), pl.BlockSpec((tm, A_WIDTH), lambda i: (i, 1)),
                  pl.BlockSpec(w.shape, lambda i: (0, 0, 0)), pl.BlockSpec(b.shape, lambda i: (0, 0, 0))],
        out_specs=pl.BlockSpec((tm, A_WIDTH), lambda i: (i, 0)), out_shape=S((t, A_WIDTH), BF16),
        compiler_params=_cp("parallel"), name=name)(proj, proj, w, b)


def gmlp_bwd(name, proj, dcat, w, b, tm=512):
    t = proj.shape[0]
    tm = _tile(t, tm)

    def body(au_ref, av_ref, do_ref, w_ref, b_ref, dp_ref, dw_ref, db_ref):
        @pl.when(pl.program_id(0) == 0)
        def _():
            dw_ref[...] = jnp.zeros(dw_ref.shape, F32)
            db_ref[...] = jnp.zeros(db_ref.shape, F32)

        mask = _gmlp_mask()
        au = au_ref[...]
        av = av_ref[...]
        u = _gelu(au)
        vn, rstd = _ln_plain(_gelu(av))
        vnb = _bf(vn)
        dout = do_ref[...]
        dvn_cols = []
        for g in range(A_GROUPS):
            wm = jnp.where(mask, w_ref[g], 0.0)
            wg = _bf(wm)
            wgt = _bf(wm.T)
            cs = slice(g * GMLP_BLOCK, (g + 1) * GMLP_BLOCK)
            dwg = jnp.zeros((GMLP_BLOCK, GMLP_BLOCK), F32)
            dbg = jnp.zeros((GMLP_BLOCK, 1), F32)
            dvn_rows = []
            for n in range(tm // GMLP_BLOCK):
                rs = slice(n * GMLP_BLOCK, (n + 1) * GMLP_BLOCK)
                sg = jnp.dot(wg, vnb[rs, cs], preferred_element_type=F32) + b_ref[g]
                dp_ref[rs, cs] = (dout[rs, cs] * sg * _gelu_grad(au[rs, cs])).astype(dp_ref.dtype)
                dsg = dout[rs, cs] * u[rs, cs]
                dsgb = _bf(dsg)
                dbg = dbg + jnp.sum(dsg, axis=1, keepdims=True)
                dwg = dwg + lax.dot_general(dsgb, vnb[rs, cs], (((1,), (1,)), ((), ())), preferred_element_type=F32)
                dvn_rows.append(jnp.dot(wgt, dsgb, preferred_element_type=F32))
            dw_ref[g] += jnp.where(mask, dwg, 0.0)
            db_ref[g] += dbg
            dvn_cols.append(jnp.concatenate(dvn_rows, axis=0))
        dvn = jnp.concatenate(dvn_cols, axis=1)
        dv = rstd * (dvn - jnp.mean(dvn, axis=-1, keepdims=True) - vn * jnp.mean(dvn * vn, axis=-1, keepdims=True))
        dp_ref[:, A_WIDTH:] = (dv * _gelu_grad(av)).astype(dp_ref.dtype)

    return pl.pallas_call(
        body, grid=(t // tm,),
        in_specs=[pl.BlockSpec((tm, A_WIDTH), lambda i: (i, 0)), pl.BlockSpec((tm, A_WIDTH), lambda i: (i, 1)),
                  pl.BlockSpec((tm, A_WIDTH), lambda i: (i, 0)),
                  pl.BlockSpec(w.shape, lambda i: (0, 0, 0)), pl.BlockSpec(b.shape, lambda i: (0, 0, 0))],
        out_specs=[pl.BlockSpec((tm, 2 * A_WIDTH), lambda i: (i, 0)),
                   pl.BlockSpec(w.shape, lambda i: (0, 0, 0)), pl.BlockSpec(b.shape, lambda i: (0, 0, 0))],
        out_shape=[S((t, 2 * A_WIDTH), BF16), S(w.shape, F32), S(b.shape, F32)],
        compiler_params=_cp("arbitrary"), name=name)(proj, proj, dcat, w, b)


CONV_ROWS = 256


def conv_fwd(name, proj, w, cb):
    t = proj.shape[0]
    tc = LANES
    rows = _tile(t, CONV_ROWS)
    a_cb, g_cb = 2 * A_WIDTH // tc, (2 * A_WIDTH + B_WIDTH) // tc

    def body(a_ref, g_ref, w_ref, cb_ref, o_ref, hpad):
        hpad[0:CONV_PAD, :] = jnp.zeros((CONV_PAD, tc), F32)

        def fill(i, _):
            r0 = pl.multiple_of(i * rows, rows)
            hpad[pl.ds(CONV_PAD + r0, rows), :] = a_ref[pl.ds(r0, rows), :] * _sigmoid(g_ref[pl.ds(r0, rows), :])
            return 0
        lax.fori_loop(0, t // rows, fill, 0)

        def conv(i, _):
            r0 = pl.multiple_of(i * rows, rows)
            win = hpad[pl.ds(r0, rows + CONV_PAD), :]
            acc = jnp.zeros((rows, tc), F32) + cb_ref[...]
            for b in range(SUB):
                wb = win if b == 0 else pltpu.roll(win, b, 0)
                for a in range(CONV_PAD // SUB):
                    k = CONV_WIDTH - 1 - (SUB * a + b)
                    if k >= 0:
                        lo = CONV_PAD - SUB * a
                        acc = acc + wb[lo:lo + rows, :] * w_ref[k:k + 1, :]
            o_ref[pl.ds(r0, rows), :] = acc
            return 0
        lax.fori_loop(0, t // rows, conv, 0)

    return pl.pallas_call(
        body, grid=(B_WIDTH // tc,),
        in_specs=[pl.BlockSpec((t, tc), lambda j: (0, a_cb + j)), pl.BlockSpec((t, tc), lambda j: (0, g_cb + j)),
                  pl.BlockSpec((CONV_WIDTH, tc), lambda j: (0, j)), pl.BlockSpec((1, tc), lambda j: (0, j))],
        out_specs=pl.BlockSpec((t, tc), lambda j: (0, j)), out_shape=S((t, B_WIDTH), F32),
        scratch_shapes=[pltpu.VMEM((t + CONV_PAD, tc), F32)],
        compiler_params=_cp("parallel"), name=name)(proj, proj, w, cb)


def conv_bwd(name, proj, dhc, w):
    t = proj.shape[0]
    tc = LANES
    rows = _tile(t, CONV_ROWS)
    a_cb, g_cb = 2 * A_WIDTH // tc, (2 * A_WIDTH + B_WIDTH) // tc
    win_rows = rows + CONV_PAD

    def body(a_ref, g_ref, d_ref, w_ref, da_ref, dg_ref, dw_ref, dcb_ref, hpad, dpad, dwacc):
        hpad[0:CONV_PAD, :] = jnp.zeros((CONV_PAD, tc), F32)
        dpad[t:t + CONV_PAD, :] = jnp.zeros((CONV_PAD, tc), F32)
        dwacc[...] = jnp.zeros(dwacc.shape, F32)

        def fill(i, _):
            r0 = pl.multiple_of(i * rows, rows)
            hpad[pl.ds(CONV_PAD + r0, rows), :] = a_ref[pl.ds(r0, rows), :] * _sigmoid(g_ref[pl.ds(r0, rows), :])
            dpad[pl.ds(r0, rows), :] = d_ref[pl.ds(r0, rows), :]
            return 0
        lax.fori_loop(0, t // rows, fill, 0)

        def step(i, dcb):
            r0 = pl.multiple_of(i * rows, rows)
            hwin = hpad[pl.ds(r0, win_rows), :]
            dwin = dpad[pl.ds(r0, win_rows), :]
            dchunk = dwin[:rows, :]
            dh = jnp.zeros((rows, tc), F32)
            for b in range(SUB):
                hb = hwin if b == 0 else pltpu.roll(hwin, b, 0)
                db = dwin if b == 0 else pltpu.roll(dwin, win_rows - b, 0)
                for a in range(CONV_PAD // SUB):
                    k = CONV_WIDTH - 1 - (SUB * a + b)
                    if k >= 0:
                        dh = dh + db[SUB * a:SUB * a + rows, :] * w_ref[k:k + 1, :]
                        lo = CONV_PAD - SUB * a
                        prod = dchunk * hb[lo:lo + rows, :]
                        dwacc[k] += jnp.sum(prod.reshape(rows // 8, 8, tc), axis=0)
            a = a_ref[pl.ds(r0, rows), :]
            sg = _sigmoid(g_ref[pl.ds(r0, rows), :])
            da_ref[pl.ds(r0, rows), :] = (dh * sg).astype(da_ref.dtype)
            dg_ref[pl.ds(r0, rows), :] = (dh * a * sg * (1.0 - sg)).astype(dg_ref.dtype)
            return dcb + jnp.sum(dchunk, axis=0, keepdims=True)
        dcb = lax.fori_loop(0, t // rows, step, jnp.zeros((1, tc), F32))
        dcb_ref[...] = dcb
        for k in range(CONV_WIDTH):
            dw_ref[k:k + 1, :] = jnp.sum(dwacc[k], axis=0, keepdims=True)

    return pl.pallas_call(
        body, grid=(B_WIDTH // tc,),
        in_specs=[pl.BlockSpec((t, tc), lambda j: (0, a_cb + j)), pl.BlockSpec((t, tc), lambda j: (0, g_cb + j)),
                  pl.BlockSpec((t, tc), lambda j: (0, j)), pl.BlockSpec((CONV_WIDTH, tc), lambda j: (0, j))],
        out_specs=[pl.BlockSpec((t, tc), lambda j: (0, j)), pl.BlockSpec((t, tc), lambda j: (0, j)),
                   pl.BlockSpec((CONV_WIDTH, tc), lambda j: (0, j)), pl.BlockSpec((1, tc), lambda j: (0, j))],
        out_shape=[S((t, B_WIDTH), BF16), S((t, B_WIDTH), BF16), S((CONV_WIDTH, B_WIDTH), F32), S((1, B_WIDTH), F32)],
        scratch_shapes=[pltpu.VMEM((t + CONV_PAD, tc), F32), pltpu.VMEM((t + CONV_PAD, tc), F32),
                        pltpu.VMEM((CONV_WIDTH, 8, tc), F32)],
        compiler_params=_cp("parallel"), name=name)(proj, proj, dhc, w)


def ln_silu_fwd(name, hc, g, b):
    def fn(h, gv, bv):
        y, _ = _ln_plain(h)
        z = y * gv + bv
        return [z * _sigmoid(z)], []
    return rows_call(name, fn, [hc], [g, b], [(hc.shape[1], BF16)], [])[0]


def ln_silu_bwd(name, hc, dcat, g, b):
    c = hc.shape[1]

    def fn(h, dout, gv, bv):
        y, rstd = _ln_plain(h)
        z = y * gv + bv
        s = _sigmoid(z)
        dz = dout * s * (1.0 + z * (1.0 - s))
        dyv = dz * gv
        dh = rstd * (dyv - jnp.mean(dyv, axis=-1, keepdims=True) - y * jnp.mean(dyv * y, axis=-1, keepdims=True))
        return [dh], [jnp.sum(dz * y, axis=0, keepdims=True), jnp.sum(dz, axis=0, keepdims=True)]

    return rows_call(name, fn, [hc, (dcat, 1, c)], [g, b], [(c, F32)], [(1, c), (1, c)])


_NT = (((1,), (1,)), ((), ()))
_TN = (((0,), (0,)), ((), ()))


def attn_fwd(name, q, k, v, tm=512):
    t, d = q.shape
    m = k.shape[0]
    tm = _tile(t, tm)
    scale = CA_HEAD_DIM ** -0.5

    def body(q_ref, k_ref, v_ref, o_ref):
        for h in range(CA_HEADS):
            cs = slice(h * CA_HEAD_DIM, (h + 1) * CA_HEAD_DIM)
            s = lax.dot_general(q_ref[:, cs], k_ref[:, cs], _NT, preferred_element_type=F32) * scale
            e = jnp.exp(s - jnp.max(s, axis=-1, keepdims=True))
            p = e / jnp.sum(e, axis=-1, keepdims=True)
            o_ref[:, cs] = jnp.dot(_bf(p), v_ref[:, cs], preferred_element_type=F32).astype(o_ref.dtype)

    return pl.pallas_call(
        body, grid=(t // tm,),
        in_specs=[pl.BlockSpec((tm, d), lambda i: (i, 0)), pl.BlockSpec((m, d), lambda i: (0, 0)),
                  pl.BlockSpec((m, d), lambda i: (0, 0))],
        out_specs=pl.BlockSpec((tm, d), lambda i: (i, 0)), out_shape=S((t, d), BF16),
        compiler_params=_cp("parallel"), name=name)(q, k, v)


def attn_bwd(name, q, k, v, do, tm=512):
    t, d = q.shape
    m = k.shape[0]
    tm = _tile(t, tm)
    scale = CA_HEAD_DIM ** -0.5

    def body(q_ref, k_ref, v_ref, do_ref, dq_ref, dk_ref, dv_ref):
        @pl.when(pl.program_id(0) == 0)
        def _():
            dk_ref[...] = jnp.zeros(dk_ref.shape, F32)
            dv_ref[...] = jnp.zeros(dv_ref.shape, F32)

        for h in range(CA_HEADS):
            cs = slice(h * CA_HEAD_DIM, (h + 1) * CA_HEAD_DIM)
            qh, kh, vh, doh = q_ref[:, cs], k_ref[:, cs], v_ref[:, cs], do_ref[:, cs]
            s = lax.dot_general(qh, kh, _NT, preferred_element_type=F32) * scale
            e = jnp.exp(s - jnp.max(s, axis=-1, keepdims=True))
            p = e / jnp.sum(e, axis=-1, keepdims=True)
            pb = _bf(p)
            dv_ref[:, cs] += lax.dot_general(pb, doh, _TN, preferred_element_type=F32)
            dp = lax.dot_general(doh, vh, _NT, preferred_element_type=F32)
            ds = _bf(p * (dp - jnp.sum(dp * p, axis=-1, keepdims=True)) * scale)
            dq_ref[:, cs] = jnp.dot(ds, kh, preferred_element_type=F32).astype(dq_ref.dtype)
            dk_ref[:, cs] += lax.dot_general(ds, qh, _TN, preferred_element_type=F32)

    return pl.pallas_call(
        body, grid=(t // tm,),
        in_specs=[pl.BlockSpec((tm, d), lambda i: (i, 0)), pl.BlockSpec((m, d), lambda i: (0, 0)),
                  pl.BlockSpec((m, d), lambda i: (0, 0)), pl.BlockSpec((tm, d), lambda i: (i, 0))],
        out_specs=[pl.BlockSpec((tm, d), lambda i: (i, 0)), pl.BlockSpec((m, d), lambda i: (0, 0)),
                   pl.BlockSpec((m, d), lambda i: (0, 0))],
        out_shape=[S((t, d), BF16), S((m, d), F32), S((m, d), F32)],
        compiler_params=_cp("arbitrary"), name=name)(q, k, v, do)


SUB = 8
S5_ROWS = 256


_HI = lax.Precision.HIGHEST
_GP = (C_GROUPS, C_STATE)
_RP = (C_WIDTH, C_STATE)


def _zoh(lr, li, ldt):
    dt = jnp.exp(ldt)
    mag = jnp.exp(lr * dt)
    ar = mag * jnp.cos(li * dt)
    ai = mag * jnp.sin(li * dt)
    den = lr * lr + li * li
    qr = ((ar - 1.0) * lr + ai * li) / den
    qi = (ai * lr - (ar - 1.0) * li) / den
    return dt, ar, ai, den, qr, qi


def _per_channel(v):
    return jnp.broadcast_to(v[:, None, :], (C_GROUPS, C_GROUP_CH, C_STATE)).reshape(_RP)


def _same_group(shape, row_per_group, col_per_group):
    rows = lax.broadcasted_iota(jnp.int32, shape, 0) // row_per_group
    cols = lax.broadcasted_iota(jnp.int32, shape, 1) // col_per_group
    return rows == cols


def _spread(shape, axis):
    long = lax.broadcasted_iota(jnp.int32, shape, axis) % C_STATE
    short = lax.broadcasted_iota(jnp.int32, shape, 1 - axis)
    return long == short


def s5_discretise(name, lam_re, lam_im, log_dt, bt_re, bt_im):
    def body(lr_ref, li_ref, ldt_ref, btr_ref, bti_ref, a_ref, bbr_ref, bbi_ref):
        _, ar, ai, _, qr, qi = _zoh(lr_ref[...], li_ref[...], ldt_ref[...])
        a_ref[0] = ar
        a_ref[1] = ai
        q2r, q2i = _per_channel(qr), _per_channel(qi)
        btr, bti = btr_ref[...], bti_ref[...]
        bbr_ref[...] = q2r * btr - q2i * bti
        bbi_ref[...] = q2r * bti + q2i * btr

    return pl.pallas_call(body, out_shape=[S((2,) + _GP, F32), S(_RP, F32), S(_RP, F32)],
                          name=name)(lam_re, lam_im, log_dt, bt_re, bt_im)


def s5_operands(name, a, bbr, bbi, c2r, c2i, ctr, cti):
    ns = N_STATE

    def body(a_ref, bbr_ref, bbi_ref, c2r_ref, c2i_ref, ctr_ref, cti_ref, pw_ref, qw_ref, mb_ref, mc_ref, mct_ref):
        ar, ai = a_ref[0:1, :], a_ref[1:2, :]
        pows = [(ar, ai)]
        for _ in range(SUB - 1):
            pr, pi = pows[-1]
            pows.append((pr * ar - pi * ai, pr * ai + pi * ar))
        rows = lax.broadcasted_iota(jnp.int32, (SUB, ns), 0)

        def rows_of(v):
            return jnp.broadcast_to(v, (SUB, ns))

        for k, s in enumerate((1, 2, 4)):
            pr, pi = rows_of(pows[s - 1][0]), rows_of(pows[s - 1][1])
            pw_ref[k, 0] = jnp.where(rows >= s, pr, 0.0)
            pw_ref[k, 1] = jnp.where(rows >= s, pi, 0.0)
            qw_ref[k, 0] = jnp.where(rows + s <= SUB - 1, pr, 0.0)
            qw_ref[k, 1] = jnp.where(rows + s <= SUB - 1, -pi, 0.0)
        fr = fi = br = bi = jnp.zeros((SUB, ns), F32)
        for i in range(SUB):
            fr = jnp.where(rows == i, rows_of(pows[i][0]), fr)
            fi = jnp.where(rows == i, rows_of(pows[i][1]), fi)
            br = jnp.where(rows == i, rows_of(pows[SUB - 1 - i][0]), br)
            bi = jnp.where(rows == i, rows_of(-pows[SUB - 1 - i][1]), bi)
        pw_ref[3, 0], pw_ref[3, 1], qw_ref[3, 0], qw_ref[3, 1] = fr, fi, br, bi

        wide = _spread((C_STATE, ns), 1).astype(BF16)
        tall = _spread((ns, C_STATE), 0).astype(BF16)
        in_rows = _same_group((C_WIDTH, ns), C_GROUP_CH, C_STATE)
        in_cols = _same_group((ns, C_WIDTH), C_STATE, C_GROUP_CH)

        def across(v, sign=1.0):
            return jnp.where(in_rows, sign * jnp.dot(_bf(v), wide, preferred_element_type=F32), 0.0).astype(BF16)

        def down(vt, sign=1.0):
            return jnp.where(in_cols, sign * jnp.dot(tall, _bf(vt), preferred_element_type=F32), 0.0).astype(BF16)

        mb_ref[:, 0:ns] = across(bbr_ref[...])
        mb_ref[:, ns:2 * ns] = across(bbi_ref[...])
        mct_ref[:, 0:ns] = across(c2r_ref[...])
        mct_ref[:, ns:2 * ns] = across(c2i_ref[...], -1.0)
        mc_ref[0:ns, :] = down(ctr_ref[...])
        mc_ref[ns:2 * ns, :] = down(cti_ref[...], -1.0)

    return pl.pallas_call(
        body, out_shape=[S((4, 2, SUB, ns), F32), S((4, 2, SUB, ns), F32), S((C_WIDTH, 2 * ns), BF16),
                         S((2 * ns, C_WIDTH), BF16), S((C_WIDTH, 2 * ns), BF16)],
        compiler_params=pltpu.CompilerParams(vmem_limit_bytes=VMEM_LIMIT), name=name)(a, bbr, bbi, c2r, c2i, ctr, cti)


def s5_param_grads(name, d_mb, d_mc, da, lam_re, lam_im, log_dt, bt_re, bt_im):
    ns = N_STATE

    def body(dmb_ref, dmc_ref, da_ref, lr_ref, li_ref, ldt_ref, btr_ref, bti_ref,
             glr_ref, gli_ref, gdt_ref, gbr_ref, gbi_ref, gcr_ref, gci_ref):
        lr, li = lr_ref[...], li_ref[...]
        dt, ar, ai, den, qr, qi = _zoh(lr, li, ldt_ref[...])
        wide = _spread((C_STATE, ns), 1).astype(F32)
        tall = _spread((ns, C_STATE), 0).astype(F32)
        in_rows = _same_group((C_WIDTH, ns), C_GROUP_CH, C_STATE)
        in_cols = _same_group((ns, C_WIDTH), C_STATE, C_GROUP_CH)

        def fold_rows(v):
            return lax.dot_general(jnp.where(in_rows, v, 0.0), wide, (((1,), (1,)), ((), ())), precision=_HI,
                                   preferred_element_type=F32)

        def fold_cols(v):
            return lax.dot_general(jnp.where(in_cols, v, 0.0), tall, (((0,), (0,)), ((), ())), precision=_HI,
                                   preferred_element_type=F32)

        gcr_ref[...] = fold_cols(dmc_ref[0:ns, :])
        gci_ref[...] = -fold_cols(dmc_ref[ns:2 * ns, :])
        gbbr = fold_rows(dmb_ref[:, 0:ns])
        gbbi = fold_rows(dmb_ref[:, ns:2 * ns])
        btr, bti = btr_ref[...], bti_ref[...]
        q2r, q2i = _per_channel(qr), _per_channel(qi)
        gbr_ref[...] = q2r * gbbr + q2i * gbbi
        gbi_ref[...] = q2r * gbbi - q2i * gbbr

        def per_group(v):
            return jnp.sum(v.reshape(C_GROUPS, C_GROUP_CH, C_STATE), axis=1)

        gqr = per_group(btr * gbbr + bti * gbbi)
        gqi = per_group(btr * gbbi - bti * gbbr)
        ilr, ili = lr / den, li / den
        gar = da_ref[0] + ilr * gqr - ili * gqi
        gai = da_ref[1] + ilr * gqi + ili * gqr
        sr = (qr * lr + qi * li) / den
        si = (qi * lr - qr * li) / den
        gzr = ar * gar + ai * gai
        gzi = ar * gai - ai * gar
        glr_ref[...] = -sr * gqr - si * gqi + dt * gzr
        gli_ref[...] = -sr * gqi + si * gqr + dt * gzi
        gdt_ref[...] = jnp.sum(lr * gzr + li * gzi, axis=1, keepdims=True) * dt

    return pl.pallas_call(
        body, out_shape=[S(_GP, F32), S(_GP, F32), S((C_GROUPS, 1), F32), S(_RP, F32), S(_RP, F32), S(_RP, F32), S(_RP, F32)],
        compiler_params=pltpu.CompilerParams(vmem_limit_bytes=VMEM_LIMIT), name=name,
    )(d_mb, d_mc, da, lam_re, lam_im, log_dt, bt_re, bt_im)


def _cmul_add(xr, xi, pr, pi, zr, zi):
    return xr + pr * zr - pi * zi, xi + pr * zi + pi * zr


def s5_fwd(name, u, mb, mc, pw, dskip):
    t = u.shape[0]
    tm = _tile(t, S5_ROWS)
    ns = N_STATE

    def body(u_ref, mb_ref, mc_ref, pw_ref, d_ref, gy_ref, y_ref, xs_ref, xb_ref, carry):
        @pl.when(pl.program_id(0) == 0)
        def _():
            carry[...] = jnp.zeros(carry.shape, F32)

        uv = u_ref[...]
        xs_ref[...] = jnp.dot(_bf(uv), mb_ref[...], preferred_element_type=F32)

        def group(i, _):
            r0 = pl.multiple_of(i * SUB, SUB)
            xr = xs_ref[pl.ds(r0, SUB), 0:ns]
            xi = xs_ref[pl.ds(r0, SUB), ns:2 * ns]
            for k, s in enumerate((1, 2, 4)):
                xr, xi = _cmul_add(xr, xi, pw_ref[k, 0], pw_ref[k, 1], pltpu.roll(xr, s, 0), pltpu.roll(xi, s, 0))
            xr, xi = _cmul_add(xr, xi, pw_ref[3, 0], pw_ref[3, 1], carry[0], carry[1])
            xs_ref[pl.ds(r0, SUB), 0:ns] = xr
            xs_ref[pl.ds(r0, SUB), ns:2 * ns] = xi
            carry[0] = jnp.broadcast_to(xr[SUB - 1:SUB, :], (SUB, ns))
            carry[1] = jnp.broadcast_to(xi[SUB - 1:SUB, :], (SUB, ns))
            return 0
        lax.fori_loop(0, tm // SUB, group, 0)

        xb = _bf(xs_ref[...])
        xb_ref[...] = xb
        y = jnp.dot(xb, mc_ref[...], preferred_element_type=F32) + d_ref[...] * uv
        y_ref[...] = y
        gy_ref[...] = _gelu(y).astype(gy_ref.dtype)

    c = u.shape[1]
    return pl.pallas_call(
        body, grid=(t // tm,),
        in_specs=[pl.BlockSpec((tm, c), lambda i: (i, 0)), pl.BlockSpec(mb.shape, lambda i: (0, 0)),
                  pl.BlockSpec(mc.shape, lambda i: (0, 0)), pl.BlockSpec(pw.shape, lambda i: (0, 0, 0, 0)),
                  pl.BlockSpec((1, c), lambda i: (0, 0))],
        out_specs=[pl.BlockSpec((tm, c), lambda i: (i, 0)), pl.BlockSpec((tm, c), lambda i: (i, 0)),
                   pl.BlockSpec((tm, 2 * ns), lambda i: (i, 0)), pl.BlockSpec((tm, 2 * ns), lambda i: (i, 0))],
        out_shape=[S((t, c), BF16), S((t, c), F32), S((t, 2 * ns), F32), S((t, 2 * ns), BF16)],
        scratch_shapes=[pltpu.VMEM((2, SUB, ns), F32)],
        compiler_params=_cp("arbitrary"), name=name)(u, mb, mc, pw, dskip)


def s5_bwd(name, dgy, y, u, xs, mct, mbt, qw, dskip):
    t, c = u.shape
    tm = _tile(t, S5_ROWS)
    nt = t // tm
    ns = N_STATE
    ng = tm // SUB

    def body(dgy_ref, y_ref, u_ref, xs_ref, mct_ref, mbt_ref, qw_ref, d_ref,
             du_ref, dy_ref, lb_ref, da_ref, dd_ref, lam, carry):
        @pl.when(pl.program_id(0) == 0)
        def _():
            carry[...] = jnp.zeros(carry.shape, F32)
            da_ref[...] = jnp.zeros(da_ref.shape, F32)
            dd_ref[...] = jnp.zeros(dd_ref.shape, F32)

        uv = u_ref[...]
        dy = dgy_ref[...] * _gelu_grad(y_ref[...])
        dyb = _bf(dy)
        dy_ref[...] = dyb
        dd_ref[...] += jnp.sum(dy * uv, axis=0, keepdims=True)
        lam[...] = jnp.dot(dyb, mct_ref[...], preferred_element_type=F32)
        last_row = lax.broadcasted_iota(jnp.int32, (SUB, ns), 0) == SUB - 1

        def group(j, _):
            i = ng - 1 - j
            r0 = pl.multiple_of(i * SUB, SUB)
            lr = lam[pl.ds(r0, SUB), 0:ns]
            li = lam[pl.ds(r0, SUB), ns:2 * ns]
            for k, s in enumerate((1, 2, 4)):
                lr, li = _cmul_add(lr, li, qw_ref[k, 0], qw_ref[k, 1],
                                   pltpu.roll(lr, SUB - s, 0), pltpu.roll(li, SUB - s, 0))
            cr, ci = carry[0], carry[1]
            lr, li = _cmul_add(lr, li, qw_ref[3, 0], qw_ref[3, 1], cr, ci)
            lam[pl.ds(r0, SUB), 0:ns] = lr
            lam[pl.ds(r0, SUB), ns:2 * ns] = li
            carry[0] = jnp.broadcast_to(lr[0:1, :], (SUB, ns))
            carry[1] = jnp.broadcast_to(li[0:1, :], (SUB, ns))
            nr = jnp.where(last_row, cr, pltpu.roll(lr, SUB - 1, 0))
            ni = jnp.where(last_row, ci, pltpu.roll(li, SUB - 1, 0))
            xr = xs_ref[pl.ds(r0, SUB), 0:ns]
            xi = xs_ref[pl.ds(r0, SUB), ns:2 * ns]
            da_ref[0] += nr * xr + ni * xi
            da_ref[1] += ni * xr - nr * xi
            return 0
        lax.fori_loop(0, ng, group, 0)

        lb = _bf(lam[...])
        lb_ref[...] = lb
        du_ref[...] = (jnp.dot(lb, mbt_ref[...], preferred_element_type=F32) + d_ref[...] * dy).astype(du_ref.dtype)

    rev = lambda i: (nt - 1 - i, 0)
    return pl.pallas_call(
        body, grid=(nt,),
        in_specs=[pl.BlockSpec((tm, c), rev), pl.BlockSpec((tm, c), rev), pl.BlockSpec((tm, c), rev),
                  pl.BlockSpec((tm, 2 * ns), rev),
                  pl.BlockSpec(mct.shape, lambda i: (0, 0)), pl.BlockSpec(mbt.shape, lambda i: (0, 0)),
                  pl.BlockSpec(qw.shape, lambda i: (0, 0, 0, 0)), pl.BlockSpec((1, c), lambda i: (0, 0))],
        out_specs=[pl.BlockSpec((tm, c), rev), pl.BlockSpec((tm, c), rev), pl.BlockSpec((tm, 2 * ns), rev),
                   pl.BlockSpec((2, SUB, ns), lambda i: (0, 0, 0)), pl.BlockSpec((1, c), lambda i: (0, 0))],
        out_shape=[S((t, c), BF16), S((t, c), BF16), S((t, 2 * ns), BF16), S((2, SUB, ns), F32), S((1, c), F32)],
        scratch_shapes=[pltpu.VMEM((tm, 2 * ns), F32), pltpu.VMEM((2, SUB, ns), F32)],
        compiler_params=_cp("arbitrary"), name=name)(dgy, y, u, xs, mct, mbt, qw, dskip)


def _first(accs, *_):
    return [accs[0]]


def _rms_bwd_epi(accs, xv, base, rv, g):
    dv = accs[0]
    w = dv * g
    xh = xv * rv
    dx = base + rv * (w - xh * jnp.mean(w * xh, axis=-1, keepdims=True))
    return [dx, dx, jnp.sum(dv * xh, axis=0, keepdims=True)]


def mm_rms_bwd(name, pairs, x, r, gain, dres):
    t, d = x.shape
    return mm_nn(name, t, d, pairs, 1, _rms_bwd_epi, [F32, BF16], tiled=[x, dres], cols=[r], rowv=[gain], sums=[(1, d)])


def _add_res(accs, res):
    return [accs[0] + res]


def even_fwd(x, w):
    t = x.shape[0]
    proj, hn, r = mm_nn("e_in_f", t, IN_WIDTH, [(x, w["e_w_in_t"], 0, "t")], 1, _first, [F32], norm_gain=w["e_norm"])
    out_a = gmlp_fwd("e_gmlp_f", proj, w["e_gmlp_w"], w["e_gmlp_b"])
    hc = conv_fwd("e_conv_f", proj, w["e_conv_w"], w["e_conv_b"])
    out_b = ln_silu_fwd("e_ln_f", hc, w["e_conv_ln_g"], w["e_conv_ln_b"])
    (x1,) = mm_nn("e_out_f", t, D_MODEL, [(out_a, (w["e_w_out"], 0), 0), (out_b, (w["e_w_out"], 1), 0)],
                  1, _add_res, [F32], tiled=[x])
    return x1, (x, hn, r, proj, out_a, hc, out_b)


def even_bwd_mixers(dxb, saved, w):
    x, hn, r, proj, out_a, hc, out_b = saved
    t = x.shape[0]
    (dcat,) = mm_nn("e_out_b", t, D_MODEL, [(dxb, w["e_w_out"], 0, "t")], 1, _first, [F32])
    g_w_out = jnp.concatenate([mm_tn("e_out_wa", out_a, dxb), mm_tn("e_out_wb", out_b, dxb)], axis=0)
    dab, g_gw, g_gb = gmlp_bwd("e_gmlp_b", proj, dcat, w["e_gmlp_w"], w["e_gmlp_b"])
    dhc, g_lg, g_lb = ln_silu_bwd("e_ln_b", hc, dcat, w["e_conv_ln_g"], w["e_conv_ln_b"])
    dba, dbg, g_cw, g_cb = conv_bwd("e_conv_b", proj, dhc, w["e_conv_w"])
    g_w_in_t = jnp.concatenate([mm_tn("e_in_w0", dab, hn), mm_tn("e_in_w1", dba, hn), mm_tn("e_in_w2", dbg, hn)], axis=0)
    grads = dict(e_w_in_t=g_w_in_t, e_gmlp_w=g_gw[None], e_gmlp_b=g_gb.reshape(1, A_GROUPS, GMLP_BLOCK),
                 e_conv_w=g_cw[None], e_conv_b=g_cb, e_conv_ln_g=g_lg, e_conv_ln_b=g_lb, e_w_out=g_w_out)
    return (dab, dba, dbg), grads


def even_bwd_input(dx, dproj, saved, w):
    x, _, r = saved[:3]
    dab, dba, dbg = dproj
    w_in_t = w["e_w_in_t"]
    return mm_rms_bwd("e_in_b", [(dab, (w_in_t, 0), 0), (dba, (w_in_t, 2), 0), (dbg, (w_in_t, 3), 0)], x, r, w["e_norm"], dx)


def s5_setup(w):
    def rows(v):
        return v.transpose(0, 2, 1).reshape(_RP)

    lam = (w["o_lam_re"], w["o_lam_im"], w["o_log_dt"].reshape(C_GROUPS, 1), rows(w["o_b_re"]), rows(w["o_b_im"]))
    a, bbr, bbi = s5_discretise("o_s5_zoh", *lam)
    c_re, c_im = w["o_c_re"], w["o_c_im"]
    pw, qw, mb, mc, mct = s5_operands("o_s5_ops", a.reshape(2, N_STATE), bbr, bbi, c_re.reshape(_RP), c_im.reshape(_RP),
                                      c_re.transpose(2, 0, 1).reshape(C_STATE, C_WIDTH),
                                      c_im.transpose(2, 0, 1).reshape(C_STATE, C_WIDTH))
    return dict(lam=lam, pw=pw, qw=qw, mb=mb, mc=mc, mct=mct, mbt=mb.T)


def odd_fwd(x, w, consts):
    t = x.shape[0]
    u, hn, r = mm_nn("o_in_f", t, C_WIDTH, [(x, w["o_w_in"], 0)], 1, _first, [F32], norm_gain=w["o_norm"])
    gy, y, xs, xsb = s5_fwd("o_s5_f", u, consts["mb"], consts["mc"], consts["pw"], w["o_d"])
    w_out_t = w["o_w_out_t"]

    def epi(accs, res):
        return [res + accs[0] * _sigmoid(accs[1]), accs[0], accs[1]]

    x1, o1, o2 = mm_nn("o_out_f", t, D_MODEL, [(gy, (w_out_t, 0), 0, "t"), (gy, (w_out_t, D_MODEL), 1, "t")], 2, epi,
                       [F32, BF16, BF16], tiled=[x])
    return x1, (x, hn, r, u, gy, y, xs, xsb, o1, o2)


def odd_bwd(dx, dxb, saved, w, consts):
    x, hn, r, u, gy, y, xs, xsb, o1, o2 = saved
    t = x.shape[0]

    def gate_bwd(dv, a, b):
        a = a.astype(F32)
        sg = _sigmoid(b.astype(F32))
        return [jnp.concatenate([dv * sg, dv * a * sg * (1.0 - sg)], axis=1)], []

    (do12,) = rows_call("o_gate_b", gate_bwd, [dx, o1, o2], [], [(2 * D_MODEL, BF16)], [])
    (dgy,) = mm_nn("o_out_b", t, C_WIDTH, [(do12, w["o_w_out_t"], 0)], 1, _first, [F32])
    g_w_out_t = mm_tn("o_out_w", do12, gy)
    du, dyb, lamb, da8, g_d = s5_bwd("o_s5_b", dgy, y, u, xs, consts["mct"], consts["mbt"], consts["qw"], w["o_d"])
    d_mb = mm_tn("o_s5_wb", u, lamb, out_dtype=F32)
    d_mc = mm_tn("o_s5_wc", xsb, dyb, out_dtype=F32)
    da = jnp.sum(da8, axis=1).reshape((2,) + _GP)
    g_lr, g_li, g_dt, g_btr, g_bti, g_cr, g_ci = s5_param_grads("o_s5_pg", d_mb, d_mc, da, *consts["lam"])

    def states_first(v):
        return v.reshape(C_GROUPS, C_GROUP_CH, C_STATE).transpose(0, 2, 1)[None]

    g_w_in = mm_tn("o_in_w", hn, du)
    dx0, dx0b, g_norm = mm_rms_bwd("o_in_b", [(du, w["o_w_in"], 0, "t")], x, r, w["o_norm"], dx)
    grads = dict(o_norm=g_norm, o_w_in=g_w_in, o_lam_re=g_lr[None], o_lam_im=g_li[None], o_log_dt=g_dt.reshape(1, C_GROUPS),
                 o_b_re=states_first(g_btr), o_b_im=states_first(g_bti),
                 o_c_re=g_cr.reshape((1, C_GROUPS, C_GROUP_CH, C_STATE)), o_c_im=g_ci.reshape((1, C_GROUPS, C_GROUP_CH, C_STATE)),
                 o_d=g_d, o_w_out_t=g_w_out_t)
    return dx0, dx0b, grads


def ca_fwd(i, x, mem, w):
    t, m = x.shape[0], mem.shape[0]
    q, xn, r = mm_nn(f"ca{i}_q_f", t, D_MODEL, [(x, w["ca_wq"][i], 0)], 1, _first, [BF16], norm_gain=w["ca_norm"][i:i + 1])
    k, v, mn, rm = mm_nn(f"ca{i}_kv_f", m, D_MODEL, [(mem, w["ca_wk"][i], 0), (mem, w["ca_wv"][i], 1)], 2,
                         lambda accs: [accs[0], accs[1]], [BF16, BF16], norm_gain=w["ca_mem_norm"][i:i + 1])
    o = attn_fwd(f"ca{i}_attn_f", q, k, v)
    (x1,) = mm_nn(f"ca{i}_o_f", t, D_MODEL, [(o, w["ca_wo"][i], 0)], 1, _add_res, [F32], tiled=[x])
    return x1, (x, xn, r, mn, rm, q, k, v, o)


def ca_bwd(i, dx, dxb, saved, mem, w):
    x, xn, r, mn, rm, q, k, v, o = saved
    t, m = x.shape[0], mem.shape[0]
    (do,) = mm_nn(f"ca{i}_o_b", t, D_MODEL, [(dxb, w["ca_wo"][i], 0, "t")], 1, _first, [BF16])
    g_wo = mm_tn(f"ca{i}_o_w", o, dxb)
    dq, dk, dv = attn_bwd(f"ca{i}_attn_b", q, k, v, do)
    g_wq = mm_tn(f"ca{i}_q_w", xn, dq)
    g_wk = mm_tn(f"ca{i}_k_w", mn, dk)
    g_wv = mm_tn(f"ca{i}_v_w", mn, dv)
    (dmn,) = mm_nn(f"ca{i}_kv_b", m, D_MODEL, [(dk, w["ca_wk"][i], 0, "t"), (dv, w["ca_wv"][i], 0, "t")], 1, _first, [F32])
    g_mnorm = rms_bwd_gain_only(f"ca{i}_mnorm_b", dmn, mem, rm)
    dx0, dx0b, g_norm = mm_rms_bwd(f"ca{i}_q_b", [(dq, w["ca_wq"][i], 0, "t")], x, r, w["ca_norm"][i:i + 1], dx)
    return dx0, dx0b, dict(ca_norm=g_norm, ca_mem_norm=g_mnorm, ca_wq=g_wq, ca_wk=g_wk, ca_wv=g_wv, ca_wo=g_wo)


def ffn_fwd(i, x, w):
    t = x.shape[0]
    def epi(accs):
        g, u = accs
        return [g, u, g * _sigmoid(g) * u]

    g, u, h, xn, r = mm_nn(f"ffn{i}_up_f", t, FFN_HIDDEN, [(x, w["ffn_w_gate_t"][i], 0, "t"), (x, w["ffn_w_up_t"][i], 1, "t")],
                           2, epi, [BF16, BF16, BF16], norm_gain=w["ffn_norm"][i:i + 1])
    (x1,) = mm_nn(f"ffn{i}_down_f", t, D_MODEL, [(h, w["ffn_w_down"][i], 0)], 1, _add_res, [F32], tiled=[x])
    return x1, (x, xn, r, g, u, h)


def ffn_bwd(i, dx, dxb, saved, w):
    x, xn, r, g, u, h = saved
    t = x.shape[0]

    def epi(accs, gv, uv):
        dh = accs[0]
        gv = gv.astype(F32)
        uv = uv.astype(F32)
        s = _sigmoid(gv)
        return [dh * uv * s * (1.0 + gv * (1.0 - s)), dh * gv * s]

    dg, du = mm_nn(f"ffn{i}_down_b", t, FFN_HIDDEN, [(dxb, w["ffn_w_down"][i], 0, "t")], 1, epi, [BF16, BF16], tiled=[g, u])
    g_wd = mm_tn(f"ffn{i}_down_w", h, dxb)
    g_wg_t = mm_tn(f"ffn{i}_gate_w", dg, xn)
    g_wu_t = mm_tn(f"ffn{i}_up_w", du, xn)
    dx0, dx0b, g_norm = mm_rms_bwd(f"ffn{i}_up_b", [(dg, w["ffn_w_gate_t"][i], 0), (du, w["ffn_w_up_t"][i], 0)], x, r,
                                   w["ffn_norm"][i:i + 1], dx)
    return dx0, dx0b, dict(ffn_norm=g_norm, ffn_w_gate_t=g_wg_t, ffn_w_up_t=g_wu_t, ffn_w_down=g_wd)


def local_step(x, mem, target, w, fetch=None, on_grads=None):
    consts = s5_setup(w)

    def need(stage, after):
        if fetch is not None:
            for k, v in fetch(stage, after).items():
                if isinstance(k, tuple):
                    w.setdefault(k[0], {})[k[1]] = v
                else:
                    w[k] = v

    need(0, x)
    x1, s_e = even_fwd(x, w)
    need(1, x1)
    x2, s_c0 = ca_fwd(0, x1, mem, w)
    need(2, x2)
    x3, s_f0 = ffn_fwd(0, x2, w)
    x4, s_o = odd_fwd(x3, w, consts)
    need(3, x4)
    x5, s_c1 = ca_fwd(1, x4, mem, w)
    x6, s_f1 = ffn_fwd(1, x5, w)
    dx, dxb, g_final, loss = final_loss("final_loss", x6, w["final_norm"], target)

    def emit(stage, carry, plain, layered=None, layer=0):
        if on_grads is None:
            return carry
        out = dict(plain)
        out.update({(k, layer): v for k, v in (layered or {}).items()})
        return on_grads(stage, out, list(carry))

    dx, dxb, g_f1 = ffn_bwd(1, dx, dxb, s_f1, w)
    dx, dxb = emit(0, (dx, dxb), {}, g_f1, 1)
    dx, dxb, g_c1 = ca_bwd(1, dx, dxb, s_c1, mem, w)
    dx, dxb, g_o = odd_bwd(dx, dxb, s_o, w, consts)
    dx, dxb = emit(1, (dx, dxb), g_o, g_c1, 1)
    dx, dxb, g_f0 = ffn_bwd(0, dx, dxb, s_f0, w)
    dx, dxb = emit(2, (dx, dxb), {}, g_f0, 0)
    dx, dxb, g_c0 = ca_bwd(0, dx, dxb, s_c0, mem, w)
    dx, dxb = emit(3, (dx, dxb), {}, g_c0, 0)
    dproj, g_e = even_bwd_mixers(dxb, s_e, w)
    dproj = emit(4, dproj, {**g_e, "o_norm": g_o["o_norm"], "o_d": g_o["o_d"]})
    dx, dxb, g_e["e_norm"] = even_bwd_input(dx, dproj, s_e, w)

    grads = dict(g_e)
    grads.update(g_o)
    for g0, g1 in ((g_c0, g_c1), (g_f0, g_f1)):
        for k in g0:
            grads[k] = jnp.concatenate([g0[k], g1[k]], axis=0) if k.endswith("norm") else (g0[k], g1[k])
    grads["final_norm"] = g_final
    return loss, dx, grads


def _group(axes):
    pos = {a: lax.axis_index(a) for a in ("x", "y", "c")}
    me = 0
    for a in axes:
        me = me * 2 + pos[a]
    peers = []
    for mask in range(1, 2 ** len(axes)):
        peer = dict(pos)
        for bit, a in enumerate(axes):
            if (mask >> (len(axes) - 1 - bit)) & 1:
                peer[a] = 1 - pos[a]
        idx = 0
        for a in axes:
            idx = idx * 2 + peer[a]
        peers.append((idx, (peer["x"], peer["y"], peer["c"])))
    return me, peers


def _sibling():
    x, y, c = lax.axis_index("x"), lax.axis_index("y"), lax.axis_index("c")
    return c, (x, y, 1 - c)


_HBM =pl.BlockSpec(memory_space=pltpu.HBM)
_SEM = pl.BlockSpec(memory_space=pltpu.SEMAPHORE)
_EFFECT = pltpu.SideEffectType.DATAFLOW_SIDE_EFFECTING


def gather_ici_start(name, groups):
    flat = [b for g in groups for b in g]
    sizes = [len(g) for g in groups]
    k_ops, n_g = len(flat), len(groups)
    lands = [lax.empty((4, 2) + tuple(b.shape), b.dtype) for b in flat]

    def body(*refs):
        src, land = refs[:k_ops], refs[k_ops:2 * k_ops]
        sems = refs[2 * k_ops:2 * k_ops + 3 * n_g]
        token = refs[-1]
        me, peers = _group(("x", "y"))
        core = lax.axis_index("c")
        i = 0
        for g in range(n_g):
            send, recv, loc = sems[3 * g:3 * g + 3]
            for j in range(sizes[g]):
                pltpu.make_async_copy(src[i], land[i].at[me, core], loc.at[j]).start()
                for k, (_, dev) in enumerate(peers):
                    pltpu.make_async_remote_copy(src_ref=src[i], dst_ref=land[i].at[me, core], send_sem=send.at[3 * j + k],
                                                 recv_sem=recv.at[3 * j + k], device_id=dev, device_id_type=MESH).start()
                i += 1
        token[...] = jnp.zeros(token.shape, token.dtype)

    sem_shapes = []
    for s in sizes:
        sem_shapes += [pltpu.SemaphoreType.DMA((3 * s,)), pltpu.SemaphoreType.DMA((3 * s,)), pltpu.SemaphoreType.DMA((s,))]
    thru = [pltpu.HBM(a.shape, a.dtype) for a in flat + lands]
    outs = pl.pallas_call(
        body, name=name, out_shape=tuple(sem_shapes) + tuple(thru) + (S((8, LANES), F32),),
        in_specs=[_HBM] * (2 * k_ops), out_specs=[_SEM] * (3 * n_g) + [_HBM] * (2 * k_ops) + [pl.BlockSpec(memory_space=pltpu.VMEM)],
        input_output_aliases={i: 3 * n_g + i for i in range(2 * k_ops)},
        compiler_params=pltpu.CompilerParams(has_side_effects=_EFFECT),
    )(*[pltpu.with_memory_space_constraint(a, pltpu.HBM) for a in flat + lands])
    sems = [tuple(outs[3 * g:3 * g + 3]) for g in range(n_g)]
    srcs_thru, lands_thru, off = [], [], 3 * n_g
    for s in sizes:
        srcs_thru.append(list(outs[off:off + s]))
        off += s
    for s in sizes:
        lands_thru.append(list(outs[off:off + s]))
        off += s
    return sems, srcs_thru, lands_thru, outs[-1]


def gather_ici_wait(name, srcs, lands, sems, after):
    n = len(srcs)

    def body(*refs):
        src, land = refs[:n], refs[n:2 * n]
        send, recv, loc = refs[2 * n:2 * n + 3]
        me, peers = _group(("x", "y"))
        core = lax.axis_index("c")
        for j in range(n):
            for k, (idx, dev) in enumerate(peers):
                cp = pltpu.make_async_remote_copy(src_ref=src[j], dst_ref=land[j].at[idx, core], send_sem=send.at[3 * j + k],
                                                  recv_sem=recv.at[3 * j + k], device_id=dev, device_id_type=MESH)
                cp.wait_send()
                cp.wait_recv()
            pltpu.make_async_copy(src[j], land[j].at[me, core], loc.at[j]).wait()

    outs = pl.pallas_call(
        body, name=name, out_shape=tuple(pltpu.HBM(a.shape, a.dtype) for a in list(srcs) + list(lands)),
        in_specs=[_HBM] * (2 * n) + [_SEM] * 3 + [ANY], out_specs=[_HBM] * (2 * n),
        input_output_aliases={i: i for i in range(2 * n)},
        compiler_params=pltpu.CompilerParams(has_side_effects=_EFFECT),
    )(*srcs, *lands, *sems, after)
    return list(outs[n:])


def gather_d2d(name, bufs):
    k_ops = len(bufs)

    def body(*refs):
        in_refs, out_refs = refs[:k_ops], refs[k_ops:2 * k_ops]
        send_sems, recv_sems = refs[2 * k_ops:]
        core, sib = _sibling()
        sent, landed = [], []
        for i in range(k_ops):
            cp = pltpu.make_async_remote_copy(src_ref=in_refs[i].at[:, core], dst_ref=out_refs[i].at[:, core],
                                              send_sem=send_sems.at[i], recv_sem=recv_sems.at[i], device_id=sib, device_id_type=MESH)
            cp.start()
            sent.append(cp)
            landed.append(pltpu.make_async_remote_copy(src_ref=in_refs[i].at[:, core], dst_ref=out_refs[i].at[:, 1 - core],
                                                       send_sem=send_sems.at[i], recv_sem=recv_sems.at[i],
                                                       device_id=sib, device_id_type=MESH))
        for cp in landed:
            cp.wait_recv()
        for cp in sent:
            cp.wait_send()

    return pl.pallas_call(
        body, in_specs=[ANY] * k_ops, out_specs=[ANY] * k_ops, out_shape=[S(b.shape, b.dtype) for b in bufs],
        input_output_aliases={i: i for i in range(k_ops)},
        scratch_shapes=[pltpu.SemaphoreType.DMA((k_ops,)), pltpu.SemaphoreType.DMA((k_ops,))],
        name=name)(*bufs)


def scatter_d2d(name, pack):
    q, _, rows, c = pack.shape

    def body(in_ref, out_ref, send_sem, recv_sem):
        core, sib = _sibling()
        cp = pltpu.make_async_remote_copy(src_ref=in_ref.at[:, 1 - core], dst_ref=out_ref, send_sem=send_sem, recv_sem=recv_sem,
                                          device_id=sib, device_id_type=MESH)
        cp.start()
        cp.wait_recv()
        cp.wait_send()

    return pl.pallas_call(
        body, in_specs=[ANY], out_specs=ANY, out_shape=S((q, rows, c), pack.dtype),
        scratch_shapes=[pltpu.SemaphoreType.DMA, pltpu.SemaphoreType.DMA], name=name)(pack)


def scatter_ici_start(name, arr, carry):
    land = lax.empty(arr.shape, arr.dtype)
    n_c = len(carry)

    def body(*refs):
        in_ref, land_ref = refs[0], refs[1]
        send, recv = refs[2 + n_c], refs[3 + n_c]
        me, peers = _group(("x", "y"))
        for k, (idx, dev) in enumerate(peers):
            pltpu.make_async_remote_copy(src_ref=in_ref.at[idx], dst_ref=land_ref.at[me], send_sem=send.at[k], recv_sem=recv.at[k],
                                         device_id=dev, device_id_type=MESH).start()

    thru = [arr, land] + list(carry)
    outs = pl.pallas_call(
        body, name=name,
        out_shape=(pltpu.SemaphoreType.DMA((3,)), pltpu.SemaphoreType.DMA((3,))) + tuple(pltpu.HBM(a.shape, a.dtype) for a in thru),
        in_specs=[_HBM] * len(thru), out_specs=[_SEM, _SEM] + [_HBM] * len(thru),
        input_output_aliases={i: 2 + i for i in range(len(thru))},
        compiler_params=pltpu.CompilerParams(has_side_effects=_EFFECT),
    )(*[pltpu.with_memory_space_constraint(a, pltpu.HBM) for a in thru])
    return (outs[0], outs[1]), outs[2], outs[3], list(outs[4:])


def scatter_ici_wait(name, arr, land, sems, after):
    def body(in_ref, land_ref, send, recv, after_ref, in_thru, land_thru):
        _, peers = _group(("x", "y"))
        for k, (idx, dev) in enumerate(peers):
            cp = pltpu.make_async_remote_copy(src_ref=in_ref.at[idx], dst_ref=land_ref.at[idx], send_sem=send.at[k],
                                              recv_sem=recv.at[k], device_id=dev, device_id_type=MESH)
            cp.wait_send()
            cp.wait_recv()

    outs = pl.pallas_call(
        body, name=name, out_shape=(pltpu.HBM(arr.shape, arr.dtype), pltpu.HBM(arr.shape, arr.dtype)),
        in_specs=[_HBM, _HBM, _SEM, _SEM, ANY], out_specs=[_HBM, _HBM], input_output_aliases={0: 0, 1: 1},
        compiler_params=pltpu.CompilerParams(has_side_effects=_EFFECT),
    )(arr, land, sems[0], sems[1], after)
    return outs[0], outs[1]


def _row_tile(rows, cap=512):
    return next(t for t in range(cap - cap % 16, 0, -16) if rows % t == 0)


def sum_pair(name, pack, recv, core):
    q, rows, c = recv.shape
    tr = _row_tile(rows)

    def body(core_ref, a_ref, b_ref, o_ref):
        o_ref[...] = (a_ref[...].astype(F32) + b_ref[...].astype(F32)).astype(o_ref.dtype)

    spec = pltpu.PrefetchScalarGridSpec(
        num_scalar_prefetch=1, grid=(q, rows // tr),
        in_specs=[pl.BlockSpec((None, None, tr, c), lambda j, i, core: (j, core[0], i, 0)),
                  pl.BlockSpec((None, tr, c), lambda j, i, core: (j, i, 0))],
        out_specs=pl.BlockSpec((None, tr, c), lambda j, i, core: (j, i, 0)))
    return pl.pallas_call(body, grid_spec=spec, out_shape=S(recv.shape, recv.dtype),
                          compiler_params=_cp("parallel", "parallel"), name=name)(core, pack, recv)


def sum_quad(name, own, recv, chip):
    _, rows, c = recv.shape
    tr = _row_tile(rows)

    def body(chip_ref, a_ref, r1_ref, r2_ref, r3_ref, o_ref):
        o_ref[...] = ((a_ref[...].astype(F32) + r1_ref[...].astype(F32)) + r2_ref[...].astype(F32)) + r3_ref[...].astype(F32)

    def slot(mask):
        return pl.BlockSpec((None, tr, c), lambda i, chip, mask=mask: (jnp.bitwise_xor(chip[0], mask), i, 0))

    spec = pltpu.PrefetchScalarGridSpec(
        num_scalar_prefetch=1, grid=(rows // tr,), in_specs=[slot(0), slot(1), slot(2), slot(3)],
        out_specs=pl.BlockSpec((tr, c), lambda i, chip: (i, 0)))
    return pl.pallas_call(body, grid_spec=spec, out_shape=S((rows, c), F32),
                          compiler_params=_cp("parallel"), name=name)(chip, own, recv, recv, recv)


def sum_slots(name, slots):
    n, r, c = slots.shape

    def body(s_ref, o_ref):
        acc = s_ref[0]
        for j in range(1, n):
            acc = acc + s_ref[j]
        o_ref[...] = acc

    return pl.pallas_call(body, out_shape=S((r, c), F32), compiler_params=pltpu.CompilerParams(vmem_limit_bytes=VMEM_LIMIT),
                          name=name)(slots)


def adamw_native(name, g, w, m, v, tr=512):
    shape = w.shape
    cols = shape[-1]
    rows = w.size // cols
    tr = _tile(rows, tr) if rows % 8 == 0 else rows
    c1 = 1.0 - ADAM_B1 ** ADAM_STEP
    c2 = 1.0 - ADAM_B2 ** ADAM_STEP

    def body(g_ref, w_ref, m_ref, v_ref, d_ref, m2_ref, v2_ref):
        gv = g_ref[...]
        m2 = ADAM_B1 * m_ref[...] + (1.0 - ADAM_B1) * gv
        v2 = ADAM_B2 * v_ref[...] + (1.0 - ADAM_B2) * (gv * gv)
        m2_ref[...] = m2
        v2_ref[...] = v2
        d_ref[...] = -ADAM_LR * ((m2 / c1) / (jnp.sqrt(v2 / c2) + ADAM_EPS) + ADAM_WD * w_ref[...])

    row = pl.BlockSpec((tr, cols), lambda i: (i, 0))
    outs = pl.pallas_call(body, grid=(rows // tr,), in_specs=[row] * 4, out_specs=[row] * 3,
                          out_shape=[S((rows, cols), F32)] * 3, compiler_params=_cp("parallel"),
                          name=name)(*[a.reshape(rows, cols) for a in (g, w, m, v)])
    return tuple(o.reshape(shape) for o in outs)


_REPLICATED = ("e_norm", "e_gmlp_w", "e_gmlp_b", "e_conv_b", "e_conv_ln_g", "e_conv_ln_b", "o_lam_re", "o_lam_im", "o_log_dt",
               "o_b_re", "o_b_im", "o_c_re", "o_c_im", "ca_norm", "ca_mem_norm", "ffn_norm", "final_norm")
_ORDER = ("e_norm", "e_w_in", "e_gmlp_w", "e_gmlp_b", "e_conv_w", "e_conv_b", "e_conv_ln_g", "e_conv_ln_b", "e_w_out",
          "o_norm", "o_w_in", "o_lam_re", "o_lam_im", "o_log_dt", "o_b_re", "o_b_im", "o_c_re", "o_c_im", "o_d", "o_w_out",
          "ca_norm", "ca_mem_norm", "ca_wq", "ca_wk", "ca_wv", "ca_wo", "ffn_norm", "ffn_w_gate", "ffn_w_up", "ffn_w_down",
          "final_norm")


def _rows128(a, multiple=8):
    flat = a.reshape(-1)
    rows = -(-flat.shape[0] // (LANES * multiple)) * multiple
    return jnp.pad(flat, (0, rows * LANES - flat.shape[0])).reshape(rows, LANES)


def _shard(full, axis):
    s = full.shape
    return jnp.moveaxis(full.reshape(s[:axis] + (N_DEV, s[axis] // N_DEV) + s[axis + 1:]), axis, 0)


_UNITS = (("e_w_in", 0, True), ("e_w_out", 0, False), ("o_w_in", 0, False), ("o_w_out", 0, True),
          *[(n, i, False) for n in ("ca_wq", "ca_wk", "ca_wv", "ca_wo") for i in (0, 1)],
          *[(n, i, tr) for n, tr in (("ffn_w_gate", True), ("ffn_w_up", True), ("ffn_w_down", False)) for i in (0, 1)])
_LAYERED = ("ca_wq", "ca_wk", "ca_wv", "ca_wo", "ffn_w_gate", "ffn_w_up", "ffn_w_down")
_SMALL_SHARDED = (("e_conv_w", 2), ("o_norm", 1), ("o_d", 1))
RS_ROW = 1024


def _unit_key(name, tr):
    return name + "_t" if tr else name


def _stage_of(name, layer):
    if name.startswith("e_"):
        return 0
    if name.startswith("o_"):
        return 2
    if name.startswith("ca_"):
        return 1 if layer == 0 else 3
    return 2 if layer == 0 else 3


def weight_fetcher(local):
    groups, meta = [[] for _ in range(4)], [[] for _ in range(4)]
    for name, layer, tr in _UNITS:
        blk = local[name][layer]
        st = _stage_of(name, layer)
        groups[st].append(_bf(blk.T if tr else blk))
        meta[st].append((name, layer, tr))
    small = jnp.concatenate([local[name].reshape(-1) for name, _ in _SMALL_SHARDED])
    groups[0].append(_rows128(small))
    sems, srcs, lands, token = gather_ici_start("ag_w_start", groups)

    def fetch(stage, after):
        if stage == 0:
            after = token
        landed = gather_ici_wait(f"ag_w_wait{stage}", srcs[stage], lands[stage], sems[stage], after)
        bufs = gather_d2d(f"ag_w_d2d{stage}", landed)
        got = {}
        for (name, layer, tr), blk, buf in zip(meta[stage], groups[stage], bufs):
            arr = buf.reshape((N_DEV * blk.shape[0],) + tuple(blk.shape[1:]))
            if name in _LAYERED:
                got[(_unit_key(name, tr), layer)] = arr
            else:
                got[_unit_key(name, tr)] = arr
        if stage == 0:
            flat = bufs[-1].reshape(N_DEV, -1)
            off = 0
            for name, axis in _SMALL_SHARDED:
                blk = local[name]
                seg = flat[:, off:off + blk.size].reshape((N_DEV,) + blk.shape)
                off += blk.size
                seg = jnp.moveaxis(seg, 0, axis)
                got[name] = seg.reshape(seg.shape[:axis] + (-1,) + seg.shape[axis + 2:])
            got["e_conv_w"] = got["e_conv_w"][0]
        return got

    return fetch


def _grad_stage_of(name, layer):
    if name.startswith("e_"):
        return 4
    if name.startswith("o_"):
        return 1
    if name.startswith("ca_"):
        return 3 if layer == 0 else 1
    return 2 if layer == 0 else 0


GRAD_STAGES = 5
SMALL_ROWS = 16


def gradient_reducer(local, mom, var):
    core = lax.axis_index("c").astype(jnp.int32).reshape(1)
    chip = (2 * lax.axis_index("x") + lax.axis_index("y")).astype(jnp.int32).reshape(1)
    pending = []

    def start(stage, grads, carry):
        units = [u for u in _UNITS if _grad_stage_of(u[0], u[1]) == stage]
        parts, spans = [], []
        for name, layer, tr in units:
            key = _unit_key(name, tr)
            g = grads[(key, layer)] if name in _LAYERED else grads[key]
            part = g.reshape(4, 2, -1, RS_ROW)
            spans.append((part.shape[2], g.shape[0] // N_DEV, g.shape[1]))
            parts.append(part)
        if stage == GRAD_STAGES - 1:
            small = jnp.concatenate([_shard(grads[name], axis).reshape(N_DEV, -1) for name, axis in _SMALL_SHARDED], axis=1)
            small = jnp.pad(small, ((0, 0), (0, SMALL_ROWS * RS_ROW - small.shape[1])))
            parts.append(small.astype(BF16).reshape(4, 2, SMALL_ROWS, RS_ROW))
        pack = jnp.concatenate(parts, axis=2)
        from_sibling = scatter_d2d(f"rs_d2d{stage}", pack)
        chip_sum = sum_pair(f"rs_pair{stage}", pack, from_sibling, core)
        sems, own, land, carry = scatter_ici_start(f"rs_start{stage}", chip_sum, carry)
        pending.append((stage, units, spans, sems, own, land))
        return carry

    def finish(after):
        res, per_layer, small_flat = {}, {}, None
        for stage, units, spans, sems, own, land in pending:
            own, land = scatter_ici_wait(f"rs_wait{stage}", own, land, sems, after)
            total = sum_quad(f"rs_quad{stage}", own, land, chip)
            off = 0
            for (name, layer, tr), (rows, r, c) in zip(units, spans):
                g = total[off:off + rows].reshape(r, c)
                off += rows
                per_layer.setdefault(name, {})[layer] = g.T if tr else g
            if stage == GRAD_STAGES - 1:
                small_flat = total[off:off + SMALL_ROWS].reshape(-1)
        for name, by_layer in per_layer.items():
            g = jnp.stack([by_layer[i] for i in sorted(by_layer)]) if name in _LAYERED else by_layer[0][None]
            res[name] = (g,) + adamw_native("adamw_" + name, g, local[name], mom[name], var[name])
        off = 0
        for name, _ in _SMALL_SHARDED:
            blk = local[name]
            g = small_flat[off:off + blk.size].reshape(blk.shape)
            off += blk.size
            res[name] = (g,) + adamw_native("adamw_" + name, g, blk, mom[name], var[name])
        return res

    return start, finish


def replicated_start(grads, loss):
    pack = jnp.concatenate([_rows128(grads[name]) for name in _REPLICATED] + [_rows128(loss)], axis=0)
    sems, srcs, lands, token = gather_ici_start("ag_g_start", [[pack]])
    return sems[0], srcs[0], lands[0], token


def replicated_finish(handle, after, w, mom, var):
    sems, srcs, lands, _ = handle
    (buf,) = gather_d2d("ag_g_d2d", gather_ici_wait("ag_g_wait", srcs, lands, sems, after))
    rows = srcs[0].shape[0]
    total = sum_slots("ag_g_sum", buf.reshape(N_DEV, rows, LANES))
    res, off = {}, 0
    for name in _REPLICATED:
        n = w[name].size
        nr = -(-n // (LANES * 8)) * 8
        g = total[off:off + nr].reshape(-1)[:n].reshape(w[name].shape)
        off += nr
        res[name] = (g,) + adamw_native("adamw_" + name, g, w[name], mom[name], var[name])
    return res, total[off, 0]


def kernel(x, mem, e_norm, e_w_in, e_gmlp_w, e_gmlp_b, e_conv_w, e_conv_b, e_conv_ln_g, e_conv_ln_b, e_w_out, o_norm, o_w_in, o_lam_re, o_lam_im, o_log_dt, o_b_re, o_b_im, o_c_re, o_c_im, o_d, o_w_out, ca_norm, ca_mem_norm, ca_wq, ca_wk, ca_wv, ca_wo, ffn_norm, ffn_w_gate, ffn_w_up, ffn_w_down, final_norm, loss_target, m_e_norm, m_e_w_in, m_e_gmlp_w, m_e_gmlp_b, m_e_conv_w, m_e_conv_b, m_e_conv_ln_g, m_e_conv_ln_b, m_e_w_out, m_o_norm, m_o_w_in, m_o_lam_re, m_o_lam_im, m_o_log_dt, m_o_b_re, m_o_b_im, m_o_c_re, m_o_c_im, m_o_d, m_o_w_out, m_ca_norm, m_ca_mem_norm, m_ca_wq, m_ca_wk, m_ca_wv, m_ca_wo, m_ffn_norm, m_ffn_w_gate, m_ffn_w_up, m_ffn_w_down, m_final_norm, v_e_norm, v_e_w_in, v_e_gmlp_w, v_e_gmlp_b, v_e_conv_w, v_e_conv_b, v_e_conv_ln_g, v_e_conv_ln_b, v_e_w_out, v_o_norm, v_o_w_in, v_o_lam_re, v_o_lam_im, v_o_log_dt, v_o_b_re, v_o_b_im, v_o_c_re, v_o_c_im, v_o_d, v_o_w_out, v_ca_norm, v_ca_mem_norm, v_ca_wq, v_ca_wk, v_ca_wv, v_ca_wo, v_ffn_norm, v_ffn_w_gate, v_ffn_w_up, v_ffn_w_down, v_final_norm):
    given = dict(locals())
    local = {k: given[k] for k in _ORDER}
    mom = {k: given["m_" + k] for k in _ORDER}
    var = {k: given["v_" + k] for k in _ORDER}

    w = {}
    w.update({
        "e_norm": e_norm, "e_gmlp_w": e_gmlp_w[0], "e_gmlp_b": e_gmlp_b.reshape(A_GROUPS, GMLP_BLOCK, 1),
        "e_conv_b": e_conv_b, "e_conv_ln_g": e_conv_ln_g, "e_conv_ln_b": e_conv_ln_b,
        "o_lam_re": o_lam_re[0], "o_lam_im": o_lam_im[0], "o_log_dt": o_log_dt[0], "o_b_re": o_b_re[0], "o_b_im": o_b_im[0],
        "o_c_re": o_c_re[0], "o_c_im": o_c_im[0], "ca_norm": ca_norm, "ca_mem_norm": ca_mem_norm, "ffn_norm": ffn_norm,
        "final_norm": final_norm.reshape(1, D_MODEL),
    })
    start_reduce, finish_reduce = gradient_reducer(local, mom, var)
    loss_part, grad_x, grads = local_step(x[0], mem[0], loss_target[0], w, weight_fetcher(local), start_reduce)
    grads["final_norm"] = grads["final_norm"].reshape(D_MODEL)

    handle = replicated_start(grads, loss_part)
    res = finish_reduce(handle[3])
    rep, loss = replicated_finish(handle, res["ffn_w_down"][1], local, mom, var)
    res.update(rep)
    return (loss, grad_x[None], *[res[k][0] for k in _ORDER], *[res[k][1] for k in _ORDER],
            *[res[k][2] for k in _ORDER], *[res[k][3] for k in _ORDER])
```

```python
import jax
import jax.numpy as jnp
from jax import lax
from jax.experimental import pallas as pl
from jax.experimental.pallas import tpu as pltpu

F32 = jnp.float32
BF16 = jnp.bfloat16
S = jax.ShapeDtypeStruct

D_MODEL = 1024
A_WIDTH = 512
A_GROUPS = 4
GMLP_BLOCK = 128
CHUNK = 64
B_WIDTH = 512
IN_WIDTH = 2 * A_WIDTH + 2 * B_WIDTH
CONV_WIDTH = 31
CONV_PAD = 32
C_WIDTH = 512
C_GROUP_CH = 16
C_GROUPS = 32
C_STATE = 64
N_STATE = C_GROUPS * C_STATE
CA_HEADS = 4
CA_HEAD_DIM = 256
FFN_HIDDEN = 2816
EPS = 1e-6
ADAM_LR = 0.001
ADAM_B1 = 0.9
ADAM_B2 = 0.999
ADAM_EPS = 1e-08
ADAM_WD = 0.01
ADAM_STEP = 10
N_DEV = 8
LANES = 128
VMEM_LIMIT = 56 << 20
VMEM_BUDGET = 40 << 20
MM_TN_RESIDENT = 8 << 20
MESH = pl.DeviceIdType.MESH
ANY = pl.BlockSpec(memory_space=pl.ANY)


def _cp(*sem):
    return pltpu.CompilerParams(dimension_semantics=sem, vmem_limit_bytes=VMEM_LIMIT)


def _tile(n, pref):
    t = pref
    while n % t:
        t //= 2
    return t


def _bf(v):
    return v if v.dtype == BF16 else v.astype(BF16)


def _sigmoid(x):
    return 1.0 / (1.0 + jnp.exp(-x))


_GC = 0.7978845608028654


def _gelu(x):
    return 0.5 * x * (1.0 + jnp.tanh(_GC * (x + 0.044715 * x * x * x)))


def _gelu_grad(x):
    x2 = x * x
    t = jnp.tanh(_GC * (x + 0.044715 * x * x2))
    return 0.5 * (1.0 + t) + 0.5 * x * (1.0 - t * t) * _GC * (1.0 + 3.0 * 0.044715 * x2)


def _tspec(entry, tm):
    if isinstance(entry, tuple):
        arr, cb, width = entry
        return arr, pl.BlockSpec((tm, width), lambda i, cb=cb: (i, cb))
    return entry, pl.BlockSpec((tm, entry.shape[1]), lambda i: (i, 0))


def rows_call(name, fn, tiled, full, outs, accs, tm=256):
    pairs = [_tspec(e, tm) for e in tiled]
    arrs = [p[0] for p in pairs]
    rows = arrs[0].shape[0]
    tm = _tile(rows, tm)
    pairs = [_tspec(e, tm) for e in tiled]
    n_in = len(tiled) + len(full)
    n_out = len(outs)

    def body(*refs):
        vals = [r[...] for r in refs[:n_in]]
        o_refs = refs[n_in:n_in + n_out]
        a_refs = refs[n_in + n_out:]
        ov, av = fn(*vals)
        for r, v in zip(o_refs, ov):
            r[...] = v.astype(r.dtype)
        if a_refs:
            @pl.when(pl.program_id(0) == 0)
            def _():
                for r in a_refs:
                    r[...] = jnp.zeros(r.shape, r.dtype)
            for r, v in zip(a_refs, av):
                r[...] += v

    in_specs = [p[1] for p in pairs] + [pl.BlockSpec(a.shape, lambda i, nd=a.ndim: (0,) * nd) for a in full]
    out_specs = [pl.BlockSpec((tm, c), lambda i: (i, 0)) for c, _ in outs]
    out_specs += [pl.BlockSpec(s, lambda i, nd=len(s): (0,) * nd) for s in accs]
    out_shape = [S((rows, c), dt) for c, dt in outs] + [S(s, F32) for s in accs]
    return pl.pallas_call(body, grid=(rows // tm,), in_specs=in_specs, out_specs=out_specs, out_shape=out_shape,
                          compiler_params=_cp("arbitrary"), name=name)(*arrs, *full)


def mm_nn(name, m, n, pairs, n_acc, epi, outs, tiled=(), cols=(), rowv=(), sums=(), norm_gain=None):
    a_ops, a_slot, b_arrs, b_specs, idx, trans = [], [], [], [], [], []
    fixed = 0
    for pair in pairs:
        a, b, k = pair[:3]
        bt = len(pair) > 3
        arr, cb, kdim = a if isinstance(a, tuple) else (a, 0, a.shape[1])
        key = (id(arr), cb, kdim)
        if key not in [o[0] for o in a_ops]:
            a_ops.append((key, arr, cb, kdim))
        a_slot.append([o[0] for o in a_ops].index(key))
        b_arr, off = b if isinstance(b, tuple) else (b, 0)
        b_arrs.append(b_arr)
        if bt:
            assert off % n == 0 and b_arr.shape[1] == kdim
            b_specs.append(pl.BlockSpec((n, kdim), lambda i, o=off // n: (o, 0), pipeline_mode=pl.Buffered(1)))
        else:
            assert b_arr.shape[1] == n
            b_specs.append(pl.BlockSpec((kdim, n), lambda i, o=off: (o, 0), pipeline_mode=pl.Buffered(1)))
        fixed += kdim * n * b_arr.dtype.itemsize
        idx.append(k)
        trans.append(bt)
    per_row = sum(2 * kdim * arr.dtype.itemsize for _, arr, _, kdim in a_ops)
    per_row += sum(2 * n * t.dtype.itemsize for t in tiled) + sum(2 * n * jnp.dtype(dt).itemsize for dt in outs)
    cn = n if sums or cols else (512 if n % 512 == 0 else 256)
    per_row += (n_acc + 3) * cn * 4
    tm = next((t for t in (1024, 512, 256, 128) if m % t == 0 and fixed + t * per_row <= VMEM_BUDGET), _tile(m, 128))
    n_a, n_p, n_t = len(a_ops), len(pairs), len(tiled)
    n_in = n_a + n_p + n_t + len(cols) + len(rowv)
    normed = norm_gain is not None
    o0 = n_in + normed

    def body(*refs):
        a_vals = [None if normed and i == 0 else _bf(r[...]) for i, r in enumerate(refs[:n_a])]
        if normed:
            xv = refs[0][...]
            rv = lax.rsqrt(jnp.mean(xv * xv, axis=-1, keepdims=True) + EPS)
            a_vals[0] = (xv * rv * refs[n_in][...]).astype(BF16)
            refs[o0 + len(outs)][...] = a_vals[0]
            refs[o0 + len(outs) + 1][...] = rv
        for j in range(n // cn):
            cs = slice(j * cn, (j + 1) * cn)
            accs = [None] * n_acc
            for p in range(n_p):
                av, b_ref = a_vals[a_slot[p]], refs[n_a + p]
                if trans[p]:
                    d = lax.dot_general(av, _bf(b_ref[cs, :]), (((1,), (1,)), ((), ())), preferred_element_type=F32)
                else:
                    d = jnp.dot(av, _bf(b_ref[:, cs]), preferred_element_type=F32)
                accs[idx[p]] = d if accs[idx[p]] is None else accs[idx[p]] + d
            extra = [r[:, cs] for r in refs[n_a + n_p:n_a + n_p + n_t]] + [r[...] for r in refs[n_a + n_p + n_t:n_in - len(rowv)]]
            extra += [r[:, cs] for r in refs[n_in - len(rowv):n_in]]
            ov = epi(accs, *extra)
            for r, v in zip(refs[o0:o0 + len(outs)], ov):
                r[:, cs] = v.astype(r.dtype)
        sv = ov[len(outs):]
        if sums:
            s_refs = refs[o0 + len(outs) + 2 * normed:]

            @pl.when(pl.program_id(0) == 0)
            def _():
                for r in s_refs:
                    r[...] = jnp.zeros(r.shape, r.dtype)
            for r, v in zip(s_refs, sv):
                r[...] += v

    in_specs = [pl.BlockSpec((tm, kdim), lambda i, cb=cb: (i, cb)) for _, _, cb, kdim in a_ops] + b_specs
    in_specs += [pl.BlockSpec((tm, n), lambda i: (i, 0)) for _ in tiled]
    in_specs += [pl.BlockSpec((tm, 1), lambda i: (i, 0)) for _ in cols]
    in_specs += [pl.BlockSpec((1, n), lambda i: (0, 0)) for _ in rowv]
    out_specs = [pl.BlockSpec((tm, n), lambda i: (i, 0)) for _ in outs]
    out_shape = [S((m, n), dt) for dt in outs]
    gain = []
    if normed:
        k0 = a_ops[0][3]
        gain = [norm_gain]
        in_specs.append(pl.BlockSpec((1, k0), lambda i: (0, 0)))
        out_specs += [pl.BlockSpec((tm, k0), lambda i: (i, 0)), pl.BlockSpec((tm, 1), lambda i: (i, 0))]
        out_shape += [S((m, k0), BF16), S((m, 1), F32)]
    out_specs += [pl.BlockSpec(s, lambda i, nd=len(s): (0,) * nd) for s in sums]
    out_shape += [S(s, F32) for s in sums]
    return pl.pallas_call(body, grid=(m // tm,), in_specs=in_specs, out_specs=out_specs, out_shape=out_shape,
                          compiler_params=_cp("arbitrary" if sums else "parallel"),
                          name=name)(*[o[1] for o in a_ops], *b_arrs, *tiled, *cols, *rowv, *gain)


def mm_tn(name, a, b, out_dtype=BF16):
    if isinstance(a, tuple):
        a_arr, a_cb, m = a
    else:
        a_arr, a_cb, m = a, None, a.shape[1]
    if isinstance(b, tuple):
        b_arr, b_cb, n = b
    else:
        b_arr, b_cb, n = b, None, b.shape[1]
    t = a_arr.shape[0]
    whole_b = t * n * b_arr.dtype.itemsize <= MM_TN_RESIDENT and b_cb is None
    tn = n if whole_b else _tile(n, 512)
    tm = _tile(m, 512 if t * 512 * a_arr.dtype.itemsize * 2 + t * tn * b_arr.dtype.itemsize * 2 <= VMEM_BUDGET else 256)
    a_off = 0 if a_cb is None else a_cb * (m // tm)
    b_off = 0 if b_cb is None else b_cb * (n // tn)

    def body(a_ref, b_ref, o_ref):
        o_ref[...] = lax.dot_general(_bf(a_ref[...]), _bf(b_ref[...]), (((0,), (0,)), ((), ())),
                                     preferred_element_type=F32).astype(o_ref.dtype)

    if whole_b:
        b_spec = pl.BlockSpec((t, n), lambda i, j: (0, 0), pipeline_mode=pl.Buffered(1))
    else:
        b_spec = pl.BlockSpec((t, tn), lambda i, j: (0, j + b_off))
    return pl.pallas_call(
        body, grid=(m // tm, n // tn),
        in_specs=[pl.BlockSpec((t, tm), lambda i, j: (0, i + a_off)), b_spec],
        out_specs=pl.BlockSpec((tm, tn), lambda i, j: (i, j)), out_shape=S((m, n), out_dtype),
        compiler_params=_cp("parallel", "parallel"), name=name)(a_arr, b_arr)


def rms_bwd_gain_only(name, dxn, x, r):
    def fn(dv, xv, rv):
        return [], [jnp.sum(dv * xv * rv, axis=0, keepdims=True)]
    return rows_call(name, fn, [dxn, x, r], [], [], [(1, x.shape[1])])[0]


def final_loss(name, x, gain, target):
    d = x.shape[1]

    def fn(xv, tv, g):
        r = lax.rsqrt(jnp.mean(xv * xv, axis=-1, keepdims=True) + EPS)
        xh = xv * r
        err = xh * g - tv
        dy = err * (1.0 / d)
        w = dy * g
        dx = r * (w - xh * jnp.mean(w * xh, axis=-1, keepdims=True))
        part = jnp.sum(jnp.sum(err * err, axis=-1, keepdims=True), axis=0, keepdims=True) * (0.5 / d)
        return [dx, dx], [jnp.sum(dy * xh, axis=0, keepdims=True), part]

    return rows_call(name, fn, [x, target], [gain], [(d, F32), (d, BF16)], [(1, d), (1, 1)])


def _gmlp_mask():
    row = lax.broadcasted_iota(jnp.int32, (GMLP_BLOCK, GMLP_BLOCK), 0) // CHUNK
    col = lax.broadcasted_iota(jnp.int32, (GMLP_BLOCK, GMLP_BLOCK), 1) // CHUNK
    return col <= row


def _ln_plain(v):
    mu = jnp.mean(v, axis=-1, keepdims=True)
    vc = v - mu
    rstd = lax.rsqrt(jnp.mean(vc * vc, axis=-1, keepdims=True) + EPS)
    return vc * rstd, rstd


def gmlp_fwd(name, proj, w, b, tm=512):
    t = proj.shape[0]
    tm = _tile(t, tm)

    def body(au_ref, av_ref, w_ref, b_ref, o_ref):
        mask = _gmlp_mask()
        u = _gelu(au_ref[...])
        vn, _ = _ln_plain(_gelu(av_ref[...]))
        vnb = _bf(vn)
        for g in range(A_GROUPS):
            wg = _bf(jnp.where(mask, w_ref[g], 0.0))
            cs = slice(g * GMLP_BLOCK, (g + 1) * GMLP_BLOCK)
            for n in range(tm // GMLP_BLOCK):
                rs = slice(n * GMLP_BLOCK, (n + 1) * GMLP_BLOCK)
                sg = jnp.dot(wg, vnb[rs, cs], preferred_element_type=F32) + b_ref[g]
                o_ref[rs, cs] = (u[rs, cs] * sg).astype(o_ref.dtype)

    return pl.pallas_call(
        body, grid=(t // tm,),
        in_specs=[pl.BlockSpec((tm, A_WIDTH), lambda i: (i, 0)), pl.BlockSpec((tm, A_WIDTH), lambda i: (i, 1)),
                  pl.BlockSpec(w.shape, lambda i: (0, 0, 0)), pl.BlockSpec(b.shape, lambda i: (0, 0, 0))],
        out_specs=pl.BlockSpec((tm, A_WIDTH), lambda i: (i, 0)), out_shape=S((t, A_WIDTH), BF16),
        compiler_params=_cp("parallel"), name=name)(proj, proj, w, b)


def gmlp_bwd(name, proj, dcat, w, b, tm=512):
    t = proj.shape[0]
    tm = _tile(t, tm)

    def body(au_ref, av_ref, do_ref, w_ref, b_ref, dp_ref, dw_ref, db_ref):
        @pl.when(pl.program_id(0) == 0)
        def _():
            dw_ref[...] = jnp.zeros(dw_ref.shape, F32)
            db_ref[...] = jnp.zeros(db_ref.shape, F32)

        mask = _gmlp_mask()
        au = au_ref[...]
        av = av_ref[...]
        u = _gelu(au)
        vn, rstd = _ln_plain(_gelu(av))
        vnb = _bf(vn)
        dout = do_ref[...]
        dvn_cols = []
        for g in range(A_GROUPS):
            wm = jnp.where(mask, w_ref[g], 0.0)
            wg = _bf(wm)
            wgt = _bf(wm.T)
            cs = slice(g * GMLP_BLOCK, (g + 1) * GMLP_BLOCK)
            dwg = jnp.zeros((GMLP_BLOCK, GMLP_BLOCK), F32)
            dbg = jnp.zeros((GMLP_BLOCK, 1), F32)
            dvn_rows = []
            for n in range(tm // GMLP_BLOCK):
                rs = slice(n * GMLP_BLOCK, (n + 1) * GMLP_BLOCK)
                sg = jnp.dot(wg, vnb[rs, cs], preferred_element_type=F32) + b_ref[g]
                dp_ref[rs, cs] = (dout[rs, cs] * sg * _gelu_grad(au[rs, cs])).astype(dp_ref.dtype)
                dsg = dout[rs, cs] * u[rs, cs]
                dsgb = _bf(dsg)
                dbg = dbg + jnp.sum(dsg, axis=1, keepdims=True)
                dwg = dwg + lax.dot_general(dsgb, vnb[rs, cs], (((1,), (1,)), ((), ())), preferred_element_type=F32)
                dvn_rows.append(jnp.dot(wgt, dsgb, preferred_element_type=F32))
            dw_ref[g] += jnp.where(mask, dwg, 0.0)
            db_ref[g] += dbg
            dvn_cols.append(jnp.concatenate(dvn_rows, axis=0))
        dvn = jnp.concatenate(dvn_cols, axis=1)
        dv = rstd * (dvn - jnp.mean(dvn, axis=-1, keepdims=True) - vn * jnp.mean(dvn * vn, axis=-1, keepdims=True))
        dp_ref[:, A_WIDTH:] = (dv * _gelu_grad(av)).astype(dp_ref.dtype)

    return pl.pallas_call(
        body, grid=(t // tm,),
        in_specs=[pl.BlockSpec((tm, A_WIDTH), lambda i: (i, 0)), pl.BlockSpec((tm, A_WIDTH), lambda i: (i, 1)),
                  pl.BlockSpec((tm, A_WIDTH), lambda i: (i, 0)),
                  pl.BlockSpec(w.shape, lambda i: (0, 0, 0)), pl.BlockSpec(b.shape, lambda i: (0, 0, 0))],
        out_specs=[pl.BlockSpec((tm, 2 * A_WIDTH), lambda i: (i, 0)),
                   pl.BlockSpec(w.shape, lambda i: (0, 0, 0)), pl.BlockSpec(b.shape, lambda i: (0, 0, 0))],
        out_shape=[S((t, 2 * A_WIDTH), BF16), S(w.shape, F32), S(b.shape, F32)],
        compiler_params=_cp("arbitrary"), name=name)(proj, proj, dcat, w, b)


CONV_ROWS = 256


def conv_fwd(name, proj, w, cb):
    t = proj.shape[0]
    tc = LANES
    rows = _tile(t, CONV_ROWS)
    a_cb, g_cb = 2 * A_WIDTH // tc, (2 * A_WIDTH + B_WIDTH) // tc

    def body(a_ref, g_ref, w_ref, cb_ref, o_ref, hpad):
        hpad[0:CONV_PAD, :] = jnp.zeros((CONV_PAD, tc), F32)

        def fill(i, _):
            r0 = pl.multiple_of(i * rows, rows)
            hpad[pl.ds(CONV_PAD + r0, rows), :] = a_ref[pl.ds(r0, rows), :] * _sigmoid(g_ref[pl.ds(r0, rows), :])
            return 0
        lax.fori_loop(0, t // rows, fill, 0)

        def conv(i, _):
            r0 = pl.multiple_of(i * rows, rows)
            win = hpad[pl.ds(r0, rows + CONV_PAD), :]
            acc = jnp.zeros((rows, tc), F32) + cb_ref[...]
            for b in range(SUB):
                wb = win if b == 0 else pltpu.roll(win, b, 0)
                for a in range(CONV_PAD // SUB):
                    k = CONV_WIDTH - 1 - (SUB * a + b)
                    if k >= 0:
                        lo = CONV_PAD - SUB * a
                        acc = acc + wb[lo:lo + rows, :] * w_ref[k:k + 1, :]
            o_ref[pl.ds(r0, rows), :] = acc
            return 0
        lax.fori_loop(0, t // rows, conv, 0)

    return pl.pallas_call(
        body, grid=(B_WIDTH // tc,),
        in_specs=[pl.BlockSpec((t, tc), lambda j: (0, a_cb + j)), pl.BlockSpec((t, tc), lambda j: (0, g_cb + j)),
                  pl.BlockSpec((CONV_WIDTH, tc), lambda j: (0, j)), pl.BlockSpec((1, tc), lambda j: (0, j))],
        out_specs=pl.BlockSpec((t, tc), lambda j: (0, j)), out_shape=S((t, B_WIDTH), F32),
        scratch_shapes=[pltpu.VMEM((t + CONV_PAD, tc), F32)],
        compiler_params=_cp("parallel"), name=name)(proj, proj, w, cb)


def conv_bwd(name, proj, dhc, w):
    t = proj.shape[0]
    tc = LANES
    rows = _tile(t, CONV_ROWS)
    a_cb, g_cb = 2 * A_WIDTH // tc, (2 * A_WIDTH + B_WIDTH) // tc
    win_rows = rows + CONV_PAD

    def body(a_ref, g_ref, d_ref, w_ref, da_ref, dg_ref, dw_ref, dcb_ref, hpad, dpad, dwacc):
        hpad[0:CONV_PAD, :] = jnp.zeros((CONV_PAD, tc), F32)
        dpad[t:t + CONV_PAD, :] = jnp.zeros((CONV_PAD, tc), F32)
        dwacc[...] = jnp.zeros(dwacc.shape, F32)

        def fill(i, _):
            r0 = pl.multiple_of(i * rows, rows)
            hpad[pl.ds(CONV_PAD + r0, rows), :] = a_ref[pl.ds(r0, rows), :] * _sigmoid(g_ref[pl.ds(r0, rows), :])
            dpad[pl.ds(r0, rows), :] = d_ref[pl.ds(r0, rows), :]
            return 0
        lax.fori_loop(0, t // rows, fill, 0)

        def step(i, dcb):
            r0 = pl.multiple_of(i * rows, rows)
            hwin = hpad[pl.ds(r0, win_rows), :]
            dwin = dpad[pl.ds(r0, win_rows), :]
            dchunk = dwin[:rows, :]
            dh = jnp.zeros((rows, tc), F32)
            for b in range(SUB):
                hb = hwin if b == 0 else pltpu.roll(hwin, b, 0)
                db = dwin if b == 0 else pltpu.roll(dwin, win_rows - b, 0)
                for a in range(CONV_PAD // SUB):
                    k = CONV_WIDTH - 1 - (SUB * a + b)
                    if k >= 0:
                        dh = dh + db[SUB * a:SUB * a + rows, :] * w_ref[k:k + 1, :]
                        lo = CONV_PAD - SUB * a
                        prod = dchunk * hb[lo:lo + rows, :]
                        dwacc[k] += jnp.sum(prod.reshape(rows // 8, 8, tc), axis=0)
            a = a_ref[pl.ds(r0, rows), :]
            sg = _sigmoid(g_ref[pl.ds(r0, rows), :])
            da_ref[pl.ds(r0, rows), :] = (dh * sg).astype(da_ref.dtype)
            dg_ref[pl.ds(r0, rows), :] = (dh * a * sg * (1.0 - sg)).astype(dg_ref.dtype)
            return dcb + jnp.sum(dchunk, axis=0, keepdims=True)
        dcb = lax.fori_loop(0, t // rows, step, jnp.zeros((1, tc), F32))
        dcb_ref[...] = dcb
        for k in range(CONV_WIDTH):
            dw_ref[k:k + 1, :] = jnp.sum(dwacc[k], axis=0, keepdims=True)

    return pl.pallas_call(
        body, grid=(B_WIDTH // tc,),
        in_specs=[pl.BlockSpec((t, tc), lambda j: (0, a_cb + j)), pl.BlockSpec((t, tc), lambda j: (0, g_cb + j)),
                  pl.BlockSpec((t, tc), lambda j: (0, j)), pl.BlockSpec((CONV_WIDTH, tc), lambda j: (0, j))],
        out_specs=[pl.BlockSpec((t, tc), lambda j: (0, j)), pl.BlockSpec((t, tc), lambda j: (0, j)),
                   pl.BlockSpec((CONV_WIDTH, tc), lambda j: (0, j)), pl.BlockSpec((1, tc), lambda j: (0, j))],
        out_shape=[S((t, B_WIDTH), BF16), S((t, B_WIDTH), BF16), S((CONV_WIDTH, B_WIDTH), F32), S((1, B_WIDTH), F32)],
        scratch_shapes=[pltpu.VMEM((t + CONV_PAD, tc), F32), pltpu.VMEM((t + CONV_PAD, tc), F32),
                        pltpu.VMEM((CONV_WIDTH, 8, tc), F32)],
        compiler_params=_cp("parallel"), name=name)(proj, proj, dhc, w)


def ln_silu_fwd(name, hc, g, b):
    def fn(h, gv, bv):
        y, _ = _ln_plain(h)
        z = y * gv + bv
        return [z * _sigmoid(z)], []
    return rows_call(name, fn, [hc], [g, b], [(hc.shape[1], BF16)], [])[0]


def ln_silu_bwd(name, hc, dcat, g, b):
    c = hc.shape[1]

    def fn(h, dout, gv, bv):
        y, rstd = _ln_plain(h)
        z = y * gv + bv
        s = _sigmoid(z)
        dz = dout * s * (1.0 + z * (1.0 - s))
        dyv = dz * gv
        dh = rstd * (dyv - jnp.mean(dyv, axis=-1, keepdims=True) - y * jnp.mean(dyv * y, axis=-1, keepdims=True))
        return [dh], [jnp.sum(dz * y, axis=0, keepdims=True), jnp.sum(dz, axis=0, keepdims=True)]

    return rows_call(name, fn, [hc, (dcat, 1, c)], [g, b], [(c, F32)], [(1, c), (1, c)])


_NT = (((1,), (1,)), ((), ()))
_TN = (((0,), (0,)), ((), ()))


def attn_fwd(name, q, k, v, tm=512):
    t, d = q.shape
    m = k.shape[0]
    tm = _tile(t, tm)
    scale = CA_HEAD_DIM ** -0.5

    def body(q_ref, k_ref, v_ref, o_ref):
        for h in range(CA_HEADS):
            cs = slice(h * CA_HEAD_DIM, (h + 1) * CA_HEAD_DIM)
            s = lax.dot_general(q_ref[:, cs], k_ref[:, cs], _NT, preferred_element_type=F32) * scale
            e = jnp.exp(s - jnp.max(s, axis=-1, keepdims=True))
            p = e / jnp.sum(e, axis=-1, keepdims=True)
            o_ref[:, cs] = jnp.dot(_bf(p), v_ref[:, cs], preferred_element_type=F32).astype(o_ref.dtype)

    return pl.pallas_call(
        body, grid=(t // tm,),
        in_specs=[pl.BlockSpec((tm, d), lambda i: (i, 0)), pl.BlockSpec((m, d), lambda i: (0, 0)),
                  pl.BlockSpec((m, d), lambda i: (0, 0))],
        out_specs=pl.BlockSpec((tm, d), lambda i: (i, 0)), out_shape=S((t, d), BF16),
        compiler_params=_cp("parallel"), name=name)(q, k, v)


def attn_bwd(name, q, k, v, do, tm=512):
    t, d = q.shape
    m = k.shape[0]
    tm = _tile(t, tm)
    scale = CA_HEAD_DIM ** -0.5

    def body(q_ref, k_ref, v_ref, do_ref, dq_ref, dk_ref, dv_ref):
        @pl.when(pl.program_id(0) == 0)
        def _():
            dk_ref[...] = jnp.zeros(dk_ref.shape, F32)
            dv_ref[...] = jnp.zeros(dv_ref.shape, F32)

        for h in range(CA_HEADS):
            cs = slice(h * CA_HEAD_DIM, (h + 1) * CA_HEAD_DIM)
            qh, kh, vh, doh = q_ref[:, cs], k_ref[:, cs], v_ref[:, cs], do_ref[:, cs]
            s = lax.dot_general(qh, kh, _NT, preferred_element_type=F32) * scale
            e = jnp.exp(s - jnp.max(s, axis=-1, keepdims=True))
            p = e / jnp.sum(e, axis=-1, keepdims=True)
            pb = _bf(p)
            dv_ref[:, cs] += lax.dot_general(pb, doh, _TN, preferred_element_type=F32)
            dp = lax.dot_general(doh, vh, _NT, preferred_element_type=F32)
            ds = _bf(p * (dp - jnp.sum(dp * p, axis=-1, keepdims=True)) * scale)
            dq_ref[:, cs] = jnp.dot(ds, kh, preferred_element_type=F32).astype(dq_ref.dtype)
            dk_ref[:, cs] += lax.dot_general(ds, qh, _TN, preferred_element_type=F32)

    return pl.pallas_call(
        body, grid=(t // tm,),
        in_specs=[pl.BlockSpec((tm, d), lambda i: (i, 0)), pl.BlockSpec((m, d), lambda i: (0, 0)),
                  pl.BlockSpec((m, d), lambda i: (0, 0)), pl.BlockSpec((tm, d), lambda i: (i, 0))],
        out_specs=[pl.BlockSpec((tm, d), lambda i: (i, 0)), pl.BlockSpec((m, d), lambda i: (0, 0)),
                   pl.BlockSpec((m, d), lambda i: (0, 0))],
        out_shape=[S((t, d), BF16), S((m, d), F32), S((m, d), F32)],
        compiler_params=_cp("arbitrary"), name=name)(q, k, v, do)


SUB = 8
S5_ROWS = 256


_HI = lax.Precision.HIGHEST
_GP = (C_GROUPS, C_STATE)
_RP = (C_WIDTH, C_STATE)


def _zoh(lr, li, ldt):
    dt = jnp.exp(ldt)
    mag = jnp.exp(lr * dt)
    ar = mag * jnp.cos(li * dt)
    ai = mag * jnp.sin(li * dt)
    den = lr * lr + li * li
    qr = ((ar - 1.0) * lr + ai * li) / den
    qi = (ai * lr - (ar - 1.0) * li) / den
    return dt, ar, ai, den, qr, qi


def _per_channel(v):
    return jnp.broadcast_to(v[:, None, :], (C_GROUPS, C_GROUP_CH, C_STATE)).reshape(_RP)


def _same_group(shape, row_per_group, col_per_group):
    rows = lax.broadcasted_iota(jnp.int32, shape, 0) // row_per_group
    cols = lax.broadcasted_iota(jnp.int32, shape, 1) // col_per_group
    return rows == cols


def _spread(shape, axis):
    long = lax.broadcasted_iota(jnp.int32, shape, axis) % C_STATE
    short = lax.broadcasted_iota(jnp.int32, shape, 1 - axis)
    return long == short


def s5_discretise(name, lam_re, lam_im, log_dt, bt_re, bt_im):
    def body(lr_ref, li_ref, ldt_ref, btr_ref, bti_ref, a_ref, bbr_ref, bbi_ref):
        _, ar, ai, _, qr, qi = _zoh(lr_ref[...], li_ref[...], ldt_ref[...])
        a_ref[0] = ar
        a_ref[1] = ai
        q2r, q2i = _per_channel(qr), _per_channel(qi)
        btr, bti = btr_ref[...], bti_ref[...]
        bbr_ref[...] = q2r * btr - q2i * bti
        bbi_ref[...] = q2r * bti + q2i * btr

    return pl.pallas_call(body, out_shape=[S((2,) + _GP, F32), S(_RP, F32), S(_RP, F32)],
                          name=name)(lam_re, lam_im, log_dt, bt_re, bt_im)


def s5_operands(name, a, bbr, bbi, c2r, c2i, ctr, cti):
    ns = N_STATE

    def body(a_ref, bbr_ref, bbi_ref, c2r_ref, c2i_ref, ctr_ref, cti_ref, pw_ref, qw_ref, mb_ref, mc_ref, mct_ref):
        ar, ai = a_ref[0:1, :], a_ref[1:2, :]
        pows = [(ar, ai)]
        for _ in range(SUB - 1):
            pr, pi = pows[-1]
            pows.append((pr * ar - pi * ai, pr * ai + pi * ar))
        rows = lax.broadcasted_iota(jnp.int32, (SUB, ns), 0)

        def rows_of(v):
            return jnp.broadcast_to(v, (SUB, ns))

        for k, s in enumerate((1, 2, 4)):
            pr, pi = rows_of(pows[s - 1][0]), rows_of(pows[s - 1][1])
            pw_ref[k, 0] = jnp.where(rows >= s, pr, 0.0)
            pw_ref[k, 1] = jnp.where(rows >= s, pi, 0.0)
            qw_ref[k, 0] = jnp.where(rows + s <= SUB - 1, pr, 0.0)
            qw_ref[k, 1] = jnp.where(rows + s <= SUB - 1, -pi, 0.0)
        fr = fi = br = bi = jnp.zeros((SUB, ns), F32)
        for i in range(SUB):
            fr = jnp.where(rows == i, rows_of(pows[i][0]), fr)
            fi = jnp.where(rows == i, rows_of(pows[i][1]), fi)
            br = jnp.where(rows == i, rows_of(pows[SUB - 1 - i][0]), br)
            bi = jnp.where(rows == i, rows_of(-pows[SUB - 1 - i][1]), bi)
        pw_ref[3, 0], pw_ref[3, 1], qw_ref[3, 0], qw_ref[3, 1] = fr, fi, br, bi

        wide = _spread((C_STATE, ns), 1).astype(BF16)
        tall = _spread((ns, C_STATE), 0).astype(BF16)
        in_rows = _same_group((C_WIDTH, ns), C_GROUP_CH, C_STATE)
        in_cols = _same_group((ns, C_WIDTH), C_STATE, C_GROUP_CH)

        def across(v, sign=1.0):
            return jnp.where(in_rows, sign * jnp.dot(_bf(v), wide, preferred_element_type=F32), 0.0).astype(BF16)

        def down(vt, sign=1.0):
            return jnp.where(in_cols, sign * jnp.dot(tall, _bf(vt), preferred_element_type=F32), 0.0).astype(BF16)

        mb_ref[:, 0:ns] = across(bbr_ref[...])
        mb_ref[:, ns:2 * ns] = across(bbi_ref[...])
        mct_ref[:, 0:ns] = across(c2r_ref[...])
        mct_ref[:, ns:2 * ns] = across(c2i_ref[...], -1.0)
        mc_ref[0:ns, :] = down(ctr_ref[...])
        mc_ref[ns:2 * ns, :] = down(cti_ref[...], -1.0)

    return pl.pallas_call(
        body, out_shape=[S((4, 2, SUB, ns), F32), S((4, 2, SUB, ns), F32), S((C_WIDTH, 2 * ns), BF16),
                         S((2 * ns, C_WIDTH), BF16), S((C_WIDTH, 2 * ns), BF16)],
        compiler_params=pltpu.CompilerParams(vmem_limit_bytes=VMEM_LIMIT), name=name)(a, bbr, bbi, c2r, c2i, ctr, cti)


def s5_param_grads(name, d_mb, d_mc, da, lam_re, lam_im, log_dt, bt_re, bt_im):
    ns = N_STATE

    def body(dmb_ref, dmc_ref, da_ref, lr_ref, li_ref, ldt_ref, btr_ref, bti_ref,
             glr_ref, gli_ref, gdt_ref, gbr_ref, gbi_ref, gcr_ref, gci_ref):
        lr, li = lr_ref[...], li_ref[...]
        dt, ar, ai, den, qr, qi = _zoh(lr, li, ldt_ref[...])
        wide = _spread((C_STATE, ns), 1).astype(F32)
        tall = _spread((ns, C_STATE), 0).astype(F32)
        in_rows = _same_group((C_WIDTH, ns), C_GROUP_CH, C_STATE)
        in_cols = _same_group((ns, C_WIDTH), C_STATE, C_GROUP_CH)

        def fold_rows(v):
            return lax.dot_general(jnp.where(in_rows, v, 0.0), wide, (((1,), (1,)), ((), ())), precision=_HI,
                                   preferred_element_type=F32)

        def fold_cols(v):
            return lax.dot_general(jnp.where(in_cols, v, 0.0), tall, (((0,), (0,)), ((), ())), precision=_HI,
                                   preferred_element_type=F32)

        gcr_ref[...] = fold_cols(dmc_ref[0:ns, :])
        gci_ref[...] = -fold_cols(dmc_ref[ns:2 * ns, :])
        gbbr = fold_rows(dmb_ref[:, 0:ns])
        gbbi = fold_rows(dmb_ref[:, ns:2 * ns])
        btr, bti = btr_ref[...], bti_ref[...]
        q2r, q2i = _per_channel(qr), _per_channel(qi)
        gbr_ref[...] = q2r * gbbr + q2i * gbbi
        gbi_ref[...] = q2r * gbbi - q2i * gbbr

        def per_group(v):
            return jnp.sum(v.reshape(C_GROUPS, C_GROUP_CH, C_STATE), axis=1)

        gqr = per_group(btr * gbbr + bti * gbbi)
        gqi = per_group(btr * gbbi - bti * gbbr)
        ilr, ili = lr / den, li / den
        gar = da_ref[0] + ilr * gqr - ili * gqi
        gai = da_ref[1] + ilr * gqi + ili * gqr
        sr = (qr * lr + qi * li) / den
        si = (qi * lr - qr * li) / den
        gzr = ar * gar + ai * gai
        gzi = ar * gai - ai * gar
        glr_ref[...] = -sr * gqr - si * gqi + dt * gzr
        gli_ref[...] = -sr * gqi + si * gqr + dt * gzi
        gdt_ref[...] = jnp.sum(lr * gzr + li * gzi, axis=1, keepdims=True) * dt

    return pl.pallas_call(
        body, out_shape=[S(_GP, F32), S(_GP, F32), S((C_GROUPS, 1), F32), S(_RP, F32), S(_RP, F32), S(_RP, F32), S(_RP, F32)],
        compiler_params=pltpu.CompilerParams(vmem_limit_bytes=VMEM_LIMIT), name=name,
    )(d_mb, d_mc, da, lam_re, lam_im, log_dt, bt_re, bt_im)


def _cmul_add(xr, xi, pr, pi, zr, zi):
    return xr + pr * zr - pi * zi, xi + pr * zi + pi * zr


def s5_fwd(name, u, mb, mc, pw, dskip):
    t = u.shape[0]
    tm = _tile(t, S5_ROWS)
    ns = N_STATE

    def body(u_ref, mb_ref, mc_ref, pw_ref, d_ref, gy_ref, y_ref, xs_ref, xb_ref, carry):
        @pl.when(pl.program_id(0) == 0)
        def _():
            carry[...] = jnp.zeros(carry.shape, F32)

        uv = u_ref[...]
        xs_ref[...] = jnp.dot(_bf(uv), mb_ref[...], preferred_element_type=F32)

        def group(i, _):
            r0 = pl.multiple_of(i * SUB, SUB)
            xr = xs_ref[pl.ds(r0, SUB), 0:ns]
            xi = xs_ref[pl.ds(r0, SUB), ns:2 * ns]
            for k, s in enumerate((1, 2, 4)):
                xr, xi = _cmul_add(xr, xi, pw_ref[k, 0], pw_ref[k, 1], pltpu.roll(xr, s, 0), pltpu.roll(xi, s, 0))
            xr, xi = _cmul_add(xr, xi, pw_ref[3, 0], pw_ref[3, 1], carry[0], carry[1])
            xs_ref[pl.ds(r0, SUB), 0:ns] = xr
            xs_ref[pl.ds(r0, SUB), ns:2 * ns] = xi
            carry[0] = jnp.broadcast_to(xr[SUB - 1:SUB, :], (SUB, ns))
            carry[1] = jnp.broadcast_to(xi[SUB - 1:SUB, :], (SUB, ns))
            return 0
        lax.fori_loop(0, tm // SUB, group, 0)

        xb = _bf(xs_ref[...])
        xb_ref[...] = xb
        y = jnp.dot(xb, mc_ref[...], preferred_element_type=F32) + d_ref[...] * uv
        y_ref[...] = y
        gy_ref[...] = _gelu(y).astype(gy_ref.dtype)

    c = u.shape[1]
    return pl.pallas_call(
        body, grid=(t // tm,),
        in_specs=[pl.BlockSpec((tm, c), lambda i: (i, 0)), pl.BlockSpec(mb.shape, lambda i: (0, 0)),
                  pl.BlockSpec(mc.shape, lambda i: (0, 0)), pl.BlockSpec(pw.shape, lambda i: (0, 0, 0, 0)),
                  pl.BlockSpec((1, c), lambda i: (0, 0))],
        out_specs=[pl.BlockSpec((tm, c), lambda i: (i, 0)), pl.BlockSpec((tm, c), lambda i: (i, 0)),
                   pl.BlockSpec((tm, 2 * ns), lambda i: (i, 0)), pl.BlockSpec((tm, 2 * ns), lambda i: (i, 0))],
        out_shape=[S((t, c), BF16), S((t, c), F32), S((t, 2 * ns), F32), S((t, 2 * ns), BF16)],
        scratch_shapes=[pltpu.VMEM((2, SUB, ns), F32)],
        compiler_params=_cp("arbitrary"), name=name)(u, mb, mc, pw, dskip)


def s5_bwd(name, dgy, y, u, xs, mct, mbt, qw, dskip):
    t, c = u.shape
    tm = _tile(t, S5_ROWS)
    nt = t // tm
    ns = N_STATE
    ng = tm // SUB

    def body(dgy_ref, y_ref, u_ref, xs_ref, mct_ref, mbt_ref, qw_ref, d_ref,
             du_ref, dy_ref, lb_ref, da_ref, dd_ref, lam, carry):
        @pl.when(pl.program_id(0) == 0)
        def _():
            carry[...] = jnp.zeros(carry.shape, F32)
            da_ref[...] = jnp.zeros(da_ref.shape, F32)
            dd_ref[...] = jnp.zeros(dd_ref.shape, F32)

        uv = u_ref[...]
        dy = dgy_ref[...] * _gelu_grad(y_ref[...])
        dyb = _bf(dy)
        dy_ref[...] = dyb
        dd_ref[...] += jnp.sum(dy * uv, axis=0, keepdims=True)
        lam[...] = jnp.dot(dyb, mct_ref[...], preferred_element_type=F32)
        last_row = lax.broadcasted_iota(jnp.int32, (SUB, ns), 0) == SUB - 1

        def group(j, _):
            i = ng - 1 - j
            r0 = pl.multiple_of(i * SUB, SUB)
            lr = lam[pl.ds(r0, SUB), 0:ns]
            li = lam[pl.ds(r0, SUB), ns:2 * ns]
            for k, s in enumerate((1, 2, 4)):
                lr, li = _cmul_add(lr, li, qw_ref[k, 0], qw_ref[k, 1],
                                   pltpu.roll(lr, SUB - s, 0), pltpu.roll(li, SUB - s, 0))
            cr, ci = carry[0], carry[1]
            lr, li = _cmul_add(lr, li, qw_ref[3, 0], qw_ref[3, 1], cr, ci)
            lam[pl.ds(r0, SUB), 0:ns] = lr
            lam[pl.ds(r0, SUB), ns:2 * ns] = li
            carry[0] = jnp.broadcast_to(lr[0:1, :], (SUB, ns))
            carry[1] = jnp.broadcast_to(li[0:1, :], (SUB, ns))
            nr = jnp.where(last_row, cr, pltpu.roll(lr, SUB - 1, 0))
            ni = jnp.where(last_row, ci, pltpu.roll(li, SUB - 1, 0))
            xr = xs_ref[pl.ds(r0, SUB), 0:ns]
            xi = xs_ref[pl.ds(r0, SUB), ns:2 * ns]
            da_ref[0] += nr * xr + ni * xi
            da_ref[1] += ni * xr - nr * xi
            return 0
        lax.fori_loop(0, ng, group, 0)

        lb = _bf(lam[...])
        lb_ref[...] = lb
        du_ref[...] = (jnp.dot(lb, mbt_ref[...], preferred_element_type=F32) + d_ref[...] * dy).astype(du_ref.dtype)

    rev = lambda i: (nt - 1 - i, 0)
    return pl.pallas_call(
        body, grid=(nt,),
        in_specs=[pl.BlockSpec((tm, c), rev), pl.BlockSpec((tm, c), rev), pl.BlockSpec((tm, c), rev),
                  pl.BlockSpec((tm, 2 * ns), rev),
                  pl.BlockSpec(mct.shape, lambda i: (0, 0)), pl.BlockSpec(mbt.shape, lambda i: (0, 0)),
                  pl.BlockSpec(qw.shape, lambda i: (0, 0, 0, 0)), pl.BlockSpec((1, c), lambda i: (0, 0))],
        out_specs=[pl.BlockSpec((tm, c), rev), pl.BlockSpec((tm, c), rev), pl.BlockSpec((tm, 2 * ns), rev),
                   pl.BlockSpec((2, SUB, ns), lambda i: (0, 0, 0)), pl.BlockSpec((1, c), lambda i: (0, 0))],
        out_shape=[S((t, c), BF16), S((t, c), BF16), S((t, 2 * ns), BF16), S((2, SUB, ns), F32), S((1, c), F32)],
        scratch_shapes=[pltpu.VMEM((tm, 2 * ns), F32), pltpu.VMEM((2, SUB, ns), F32)],
        compiler_params=_cp("arbitrary"), name=name)(dgy, y, u, xs, mct, mbt, qw, dskip)


def _first(accs, *_):
    return [accs[0]]


def _rms_bwd_epi(accs, xv, base, rv, g):
    dv = accs[0]
    w = dv * g
    xh = xv * rv
    dx = base + rv * (w - xh * jnp.mean(w * xh, axis=-1, keepdims=True))
    return [dx, dx, jnp.sum(dv * xh, axis=0, keepdims=True)]


def mm_rms_bwd(name, pairs, x, r, gain, dres):
    t, d = x.shape
    return mm_nn(name, t, d, pairs, 1, _rms_bwd_epi, [F32, BF16], tiled=[x, dres], cols=[r], rowv=[gain], sums=[(1, d)])


def _add_res(accs, res):
    return [accs[0] + res]


def even_fwd(x, w):
    t = x.shape[0]
    proj, hn, r = mm_nn("e_in_f", t, IN_WIDTH, [(x, w["e_w_in_t"], 0, "t")], 1, _first, [F32], norm_gain=w["e_norm"])
    out_a = gmlp_fwd("e_gmlp_f", proj, w["e_gmlp_w"], w["e_gmlp_b"])
    hc = conv_fwd("e_conv_f", proj, w["e_conv_w"], w["e_conv_b"])
    out_b = ln_silu_fwd("e_ln_f", hc, w["e_conv_ln_g"], w["e_conv_ln_b"])
    (x1,) = mm_nn("e_out_f", t, D_MODEL, [(out_a, (w["e_w_out"], 0), 0), (out_b, (w["e_w_out"], 1), 0)],
                  1, _add_res, [F32], tiled=[x])
    return x1, (x, hn, r, proj, out_a, hc, out_b)


def even_bwd_mixers(dxb, saved, w):
    x, hn, r, proj, out_a, hc, out_b = saved
    t = x.shape[0]
    (dcat,) = mm_nn("e_out_b", t, D_MODEL, [(dxb, w["e_w_out"], 0, "t")], 1, _first, [F32])
    g_w_out = jnp.concatenate([mm_tn("e_out_wa", out_a, dxb), mm_tn("e_out_wb", out_b, dxb)], axis=0)
    dab, g_gw, g_gb = gmlp_bwd("e_gmlp_b", proj, dcat, w["e_gmlp_w"], w["e_gmlp_b"])
    dhc, g_lg, g_lb = ln_silu_bwd("e_ln_b", hc, dcat, w["e_conv_ln_g"], w["e_conv_ln_b"])
    dba, dbg, g_cw, g_cb = conv_bwd("e_conv_b", proj, dhc, w["e_conv_w"])
    g_w_in_t = jnp.concatenate([mm_tn("e_in_w0", dab, hn), mm_tn("e_in_w1", dba, hn), mm_tn("e_in_w2", dbg, hn)], axis=0)
    grads = dict(e_w_in_t=g_w_in_t, e_gmlp_w=g_gw[None], e_gmlp_b=g_gb.reshape(1, A_GROUPS, GMLP_BLOCK),
                 e_conv_w=g_cw[None], e_conv_b=g_cb, e_conv_ln_g=g_lg, e_conv_ln_b=g_lb, e_w_out=g_w_out)
    return (dab, dba, dbg), grads


def even_bwd_input(dx, dproj, saved, w):
    x, _, r = saved[:3]
    dab, dba, dbg = dproj
    w_in_t = w["e_w_in_t"]
    return mm_rms_bwd("e_in_b", [(dab, (w_in_t, 0), 0), (dba, (w_in_t, 2), 0), (dbg, (w_in_t, 3), 0)], x, r, w["e_norm"], dx)


def s5_setup(w):
    def rows(v):
        return v.transpose(0, 2, 1).reshape(_RP)

    lam = (w["o_lam_re"], w["o_lam_im"], w["o_log_dt"].reshape(C_GROUPS, 1), rows(w["o_b_re"]), rows(w["o_b_im"]))
    a, bbr, bbi = s5_discretise("o_s5_zoh", *lam)
    c_re, c_im = w["o_c_re"], w["o_c_im"]
    pw, qw, mb, mc, mct = s5_operands("o_s5_ops", a.reshape(2, N_STATE), bbr, bbi, c_re.reshape(_RP), c_im.reshape(_RP),
                                      c_re.transpose(2, 0, 1).reshape(C_STATE, C_WIDTH),
                                      c_im.transpose(2, 0, 1).reshape(C_STATE, C_WIDTH))
    return dict(lam=lam, pw=pw, qw=qw, mb=mb, mc=mc, mct=mct, mbt=mb.T)


def odd_fwd(x, w, consts):
    t = x.shape[0]
    u, hn, r = mm_nn("o_in_f", t, C_WIDTH, [(x, w["o_w_in"], 0)], 1, _first, [F32], norm_gain=w["o_norm"])
    gy, y, xs, xsb = s5_fwd("o_s5_f", u, consts["mb"], consts["mc"], consts["pw"], w["o_d"])
    w_out_t = w["o_w_out_t"]

    def epi(accs, res):
        return [res + accs[0] * _sigmoid(accs[1]), accs[0], accs[1]]

    x1, o1, o2 = mm_nn("o_out_f", t, D_MODEL, [(gy, (w_out_t, 0), 0, "t"), (gy, (w_out_t, D_MODEL), 1, "t")], 2, epi,
                       [F32, BF16, BF16], tiled=[x])
    return x1, (x, hn, r, u, gy, y, xs, xsb, o1, o2)


def odd_bwd(dx, dxb, saved, w, consts):
    x, hn, r, u, gy, y, xs, xsb, o1, o2 = saved
    t = x.shape[0]

    def gate_bwd(dv, a, b):
        a = a.astype(F32)
        sg = _sigmoid(b.astype(F32))
        return [jnp.concatenate([dv * sg, dv * a * sg * (1.0 - sg)], axis=1)], []

    (do12,) = rows_call("o_gate_b", gate_bwd, [dx, o1, o2], [], [(2 * D_MODEL, BF16)], [])
    (dgy,) = mm_nn("o_out_b", t, C_WIDTH, [(do12, w["o_w_out_t"], 0)], 1, _first, [F32])
    g_w_out_t = mm_tn("o_out_w", do12, gy)
    du, dyb, lamb, da8, g_d = s5_bwd("o_s5_b", dgy, y, u, xs, consts["mct"], consts["mbt"], consts["qw"], w["o_d"])
    d_mb = mm_tn("o_s5_wb", u, lamb, out_dtype=F32)
    d_mc = mm_tn("o_s5_wc", xsb, dyb, out_dtype=F32)
    da = jnp.sum(da8, axis=1).reshape((2,) + _GP)
    g_lr, g_li, g_dt, g_btr, g_bti, g_cr, g_ci = s5_param_grads("o_s5_pg", d_mb, d_mc, da, *consts["lam"])

    def states_first(v):
        return v.reshape(C_GROUPS, C_GROUP_CH, C_STATE).transpose(0, 2, 1)[None]

    g_w_in = mm_tn("o_in_w", hn, du)
    dx0, dx0b, g_norm = mm_rms_bwd("o_in_b", [(du, w["o_w_in"], 0, "t")], x, r, w["o_norm"], dx)
    grads = dict(o_norm=g_norm, o_w_in=g_w_in, o_lam_re=g_lr[None], o_lam_im=g_li[None], o_log_dt=g_dt.reshape(1, C_GROUPS),
                 o_b_re=states_first(g_btr), o_b_im=states_first(g_bti),
                 o_c_re=g_cr.reshape((1, C_GROUPS, C_GROUP_CH, C_STATE)), o_c_im=g_ci.reshape((1, C_GROUPS, C_GROUP_CH, C_STATE)),
                 o_d=g_d, o_w_out_t=g_w_out_t)
    return dx0, dx0b, grads


def ca_fwd(i, x, mem, w):
    t, m = x.shape[0], mem.shape[0]
    q, xn, r = mm_nn(f"ca{i}_q_f", t, D_MODEL, [(x, w["ca_wq"][i], 0)], 1, _first, [BF16], norm_gain=w["ca_norm"][i:i + 1])
    k, v, mn, rm = mm_nn(f"ca{i}_kv_f", m, D_MODEL, [(mem, w["ca_wk"][i], 0), (mem, w["ca_wv"][i], 1)], 2,
                         lambda accs: [accs[0], accs[1]], [BF16, BF16], norm_gain=w["ca_mem_norm"][i:i + 1])
    o = attn_fwd(f"ca{i}_attn_f", q, k, v)
    (x1,) = mm_nn(f"ca{i}_o_f", t, D_MODEL, [(o, w["ca_wo"][i], 0)], 1, _add_res, [F32], tiled=[x])
    return x1, (x, xn, r, mn, rm, q, k, v, o)


def ca_bwd(i, dx, dxb, saved, mem, w):
    x, xn, r, mn, rm, q, k, v, o = saved
    t, m = x.shape[0], mem.shape[0]
    (do,) = mm_nn(f"ca{i}_o_b", t, D_MODEL, [(dxb, w["ca_wo"][i], 0, "t")], 1, _first, [BF16])
    g_wo = mm_tn(f"ca{i}_o_w", o, dxb)
    dq, dk, dv = attn_bwd(f"ca{i}_attn_b", q, k, v, do)
    g_wq = mm_tn(f"ca{i}_q_w", xn, dq)
    g_wk = mm_tn(f"ca{i}_k_w", mn, dk)
    g_wv = mm_tn(f"ca{i}_v_w", mn, dv)
    (dmn,) = mm_nn(f"ca{i}_kv_b", m, D_MODEL, [(dk, w["ca_wk"][i], 0, "t"), (dv, w["ca_wv"][i], 0, "t")], 1, _first, [F32])
    g_mnorm = rms_bwd_gain_only(f"ca{i}_mnorm_b", dmn, mem, rm)
    dx0, dx0b, g_norm = mm_rms_bwd(f"ca{i}_q_b", [(dq, w["ca_wq"][i], 0, "t")], x, r, w["ca_norm"][i:i + 1], dx)
    return dx0, dx0b, dict(ca_norm=g_norm, ca_mem_norm=g_mnorm, ca_wq=g_wq, ca_wk=g_wk, ca_wv=g_wv, ca_wo=g_wo)


def ffn_fwd(i, x, w):
    t = x.shape[0]
    def epi(accs):
        g, u = accs
        return [g, u, g * _sigmoid(g) * u]

    g, u, h, xn, r = mm_nn(f"ffn{i}_up_f", t, FFN_HIDDEN, [(x, w["ffn_w_gate_t"][i], 0, "t"), (x, w["ffn_w_up_t"][i], 1, "t")],
                           2, epi, [BF16, BF16, BF16], norm_gain=w["ffn_norm"][i:i + 1])
    (x1,) = mm_nn(f"ffn{i}_down_f", t, D_MODEL, [(h, w["ffn_w_down"][i], 0)], 1, _add_res, [F32], tiled=[x])
    return x1, (x, xn, r, g, u, h)


def ffn_bwd(i, dx, dxb, saved, w):
    x, xn, r, g, u, h = saved
    t = x.shape[0]

    def epi(accs, gv, uv):
        dh = accs[0]
        gv = gv.astype(F32)
        uv = uv.astype(F32)
        s = _sigmoid(gv)
        return [dh * uv * s * (1.0 + gv * (1.0 - s)), dh * gv * s]

    dg, du = mm_nn(f"ffn{i}_down_b", t, FFN_HIDDEN, [(dxb, w["ffn_w_down"][i], 0, "t")], 1, epi, [BF16, BF16], tiled=[g, u])
    g_wd = mm_tn(f"ffn{i}_down_w", h, dxb)
    g_wg_t = mm_tn(f"ffn{i}_gate_w", dg, xn)
    g_wu_t = mm_tn(f"ffn{i}_up_w", du, xn)
    dx0, dx0b, g_norm = mm_rms_bwd(f"ffn{i}_up_b", [(dg, w["ffn_w_gate_t"][i], 0), (du, w["ffn_w_up_t"][i], 0)], x, r,
                                   w["ffn_norm"][i:i + 1], dx)
    return dx0, dx0b, dict(ffn_norm=g_norm, ffn_w_gate_t=g_wg_t, ffn_w_up_t=g_wu_t, ffn_w_down=g_wd)


def local_step(x, mem, target, w, fetch=None, on_grads=None):
    consts = s5_setup(w)

    def need(stage, after):
        if fetch is not None:
            for k, v in fetch(stage, after).items():
                if isinstance(k, tuple):
                    w.setdefault(k[0], {})[k[1]] = v
                else:
                    w[k] = v

    need(0, x)
    x1, s_e = even_fwd(x, w)
    need(1, x1)
    x2, s_c0 = ca_fwd(0, x1, mem, w)
    need(2, x2)
    x3, s_f0 = ffn_fwd(0, x2, w)
    x4, s_o = odd_fwd(x3, w, consts)
    need(3, x4)
    x5, s_c1 = ca_fwd(1, x4, mem, w)
    x6, s_f1 = ffn_fwd(1, x5, w)
    dx, dxb, g_final, loss = final_loss("final_loss", x6, w["final_norm"], target)

    def emit(stage, carry, plain, layered=None, layer=0):
        if on_grads is None:
            return carry
        out = dict(plain)
        out.update({(k, layer): v for k, v in (layered or {}).items()})
        return on_grads(stage, out, list(carry))

    dx, dxb, g_f1 = ffn_bwd(1, dx, dxb, s_f1, w)
    dx, dxb = emit(0, (dx, dxb), {}, g_f1, 1)
    dx, dxb, g_c1 = ca_bwd(1, dx, dxb, s_c1, mem, w)
    dx, dxb, g_o = odd_bwd(dx, dxb, s_o, w, consts)
    dx, dxb = emit(1, (dx, dxb), g_o, g_c1, 1)
    dx, dxb, g_f0 = ffn_bwd(0, dx, dxb, s_f0, w)
    dx, dxb = emit(2, (dx, dxb), {}, g_f0, 0)
    dx, dxb, g_c0 = ca_bwd(0, dx, dxb, s_c0, mem, w)
    dx, dxb = emit(3, (dx, dxb), {}, g_c0, 0)
    dproj, g_e = even_bwd_mixers(dxb, s_e, w)
    dproj = emit(4, dproj, {**g_e, "o_norm": g_o["o_norm"], "o_d": g_o["o_d"]})
    dx, dxb, g_e["e_norm"] = even_bwd_input(dx, dproj, s_e, w)

    grads = dict(g_e)
    grads.update(g_o)
    for g0, g1 in ((g_c0, g_c1), (g_f0, g_f1)):
        for k in g0:
            grads[k] = jnp.concatenate([g0[k], g1[k]], axis=0) if k.endswith("norm") else (g0[k], g1[k])
    grads["final_norm"] = g_final
    return loss, dx, grads


def _group(axes):
    pos = {a: lax.axis_index(a) for a in ("x", "y", "c")}
    me = 0
    for a in axes:
        me = me * 2 + pos[a]
    peers = []
    for mask in range(1, 2 ** len(axes)):
        peer = dict(pos)
        for bit, a in enumerate(axes):
            if (mask >> (len(axes) - 1 - bit)) & 1:
                peer[a] = 1 - pos[a]
        idx = 0
        for a in axes:
            idx = idx * 2 + peer[a]
        peers.append((idx, (peer["x"], peer["y"], peer["c"])))
    return me, peers


def _sibling():
    x, y, c = lax.axis_index("x"), lax.axis_index("y"), lax.axis_index("c")
    return c, (x, y, 1 - c)


_HBM =pl.BlockSpec(memory_space=pltpu.HBM)
_SEM = pl.BlockSpec(memory_space=pltpu.SEMAPHORE)
_EFFECT = pltpu.SideEffectType.DATAFLOW_SIDE_EFFECTING


def gather_ici_start(name, groups):
    flat = [b for g in groups for b in g]
    sizes = [len(g) for g in groups]
    k_ops, n_g = len(flat), len(groups)
    lands = [lax.empty((4, 2) + tuple(b.shape), b.dtype) for b in flat]

    def body(*refs):
        src, land = refs[:k_ops], refs[k_ops:2 * k_ops]
        sems = refs[2 * k_ops:2 * k_ops + 3 * n_g]
        token = refs[-1]
        me, peers = _group(("x", "y"))
        core = lax.axis_index("c")
        i = 0
        for g in range(n_g):
            send, recv, loc = sems[3 * g:3 * g + 3]
            for j in range(sizes[g]):
                pltpu.make_async_copy(src[i], land[i].at[me, core], loc.at[j]).start()
                for k, (_, dev) in enumerate(peers):
                    pltpu.make_async_remote_copy(src_ref=src[i], dst_ref=land[i].at[me, core], send_sem=send.at[3 * j + k],
                                                 recv_sem=recv.at[3 * j + k], device_id=dev, device_id_type=MESH).start()
                i += 1
        token[...] = jnp.zeros(token.shape, token.dtype)

    sem_shapes = []
    for s in sizes:
        sem_shapes += [pltpu.SemaphoreType.DMA((3 * s,)), pltpu.SemaphoreType.DMA((3 * s,)), pltpu.SemaphoreType.DMA((s,))]
    thru = [pltpu.HBM(a.shape, a.dtype) for a in flat + lands]
    outs = pl.pallas_call(
        body, name=name, out_shape=tuple(sem_shapes) + tuple(thru) + (S((8, LANES), F32),),
        in_specs=[_HBM] * (2 * k_ops), out_specs=[_SEM] * (3 * n_g) + [_HBM] * (2 * k_ops) + [pl.BlockSpec(memory_space=pltpu.VMEM)],
        input_output_aliases={i: 3 * n_g + i for i in range(2 * k_ops)},
        compiler_params=pltpu.CompilerParams(has_side_effects=_EFFECT),
    )(*[pltpu.with_memory_space_constraint(a, pltpu.HBM) for a in flat + lands])
    sems = [tuple(outs[3 * g:3 * g + 3]) for g in range(n_g)]
    srcs_thru, lands_thru, off = [], [], 3 * n_g
    for s in sizes:
        srcs_thru.append(list(outs[off:off + s]))
        off += s
    for s in sizes:
        lands_thru.append(list(outs[off:off + s]))
        off += s
    return sems, srcs_thru, lands_thru, outs[-1]


def gather_ici_wait(name, srcs, lands, sems, after):
    n = len(srcs)

    def body(*refs):
        src, land = refs[:n], refs[n:2 * n]
        send, recv, loc = refs[2 * n:2 * n + 3]
        me, peers = _group(("x", "y"))
        core = lax.axis_index("c")
        for j in range(n):
            for k, (idx, dev) in enumerate(peers):
                cp = pltpu.make_async_remote_copy(src_ref=src[j], dst_ref=land[j].at[idx, core], send_sem=send.at[3 * j + k],
                                                  recv_sem=recv.at[3 * j + k], device_id=dev, device_id_type=MESH)
                cp.wait_send()
                cp.wait_recv()
            pltpu.make_async_copy(src[j], land[j].at[me, core], loc.at[j]).wait()

    outs = pl.pallas_call(
        body, name=name, out_shape=tuple(pltpu.HBM(a.shape, a.dtype) for a in list(srcs) + list(lands)),
        in_specs=[_HBM] * (2 * n) + [_SEM] * 3 + [ANY], out_specs=[_HBM] * (2 * n),
        input_output_aliases={i: i for i in range(2 * n)},
        compiler_params=pltpu.CompilerParams(has_side_effects=_EFFECT),
    )(*srcs, *lands, *sems, after)
    return list(outs[n:])


def gather_d2d(name, bufs):
    k_ops = len(bufs)

    def body(*refs):
        in_refs, out_refs = refs[:k_ops], refs[k_ops:2 * k_ops]
        send_sems, recv_sems = refs[2 * k_ops:]
        core, sib = _sibling()
        sent, landed = [], []
        for i in range(k_ops):
            cp = pltpu.make_async_remote_copy(src_ref=in_refs[i].at[:, core], dst_ref=out_refs[i].at[:, core],
                                              send_sem=send_sems.at[i], recv_sem=recv_sems.at[i], device_id=sib, device_id_type=MESH)
            cp.start()
            sent.append(cp)
            landed.append(pltpu.make_async_remote_copy(src_ref=in_refs[i].at[:, core], dst_ref=out_refs[i].at[:, 1 - core],
                                                       send_sem=send_sems.at[i], recv_sem=recv_sems.at[i],
                                                       device_id=sib, device_id_type=MESH))
        for cp in landed:
            cp.wait_recv()
        for cp in sent:
            cp.wait_send()

    return pl.pallas_call(
        body, in_specs=[ANY] * k_ops, out_specs=[ANY] * k_ops, out_shape=[S(b.shape, b.dtype) for b in bufs],
        input_output_aliases={i: i for i in range(k_ops)},
        scratch_shapes=[pltpu.SemaphoreType.DMA((k_ops,)), pltpu.SemaphoreType.DMA((k_ops,))],
        name=name)(*bufs)


_ALL = ("x", "y", "c")


def scatter_start(name, arr, carry):
    land = lax.empty(arr.shape, arr.dtype)
    n_c = len(carry)

    def body(*refs):
        in_ref, land_ref = refs[0], refs[1]
        send, recv = refs[2 + n_c], refs[3 + n_c]
        me, peers = _group(_ALL)
        for k, (idx, dev) in enumerate(peers):
            pltpu.make_async_remote_copy(src_ref=in_ref.at[idx], dst_ref=land_ref.at[me], send_sem=send.at[k], recv_sem=recv.at[k],
                                         device_id=dev, device_id_type=MESH).start()

    thru = [arr, land] + list(carry)
    outs = pl.pallas_call(
        body, name=name,
        out_shape=(pltpu.SemaphoreType.DMA((N_DEV - 1,)), pltpu.SemaphoreType.DMA((N_DEV - 1,)))
        + tuple(pltpu.HBM(a.shape, a.dtype) for a in thru),
        in_specs=[_HBM] * len(thru), out_specs=[_SEM, _SEM] + [_HBM] * len(thru),
        input_output_aliases={i: 2 + i for i in range(len(thru))},
        compiler_params=pltpu.CompilerParams(has_side_effects=_EFFECT),
    )(*[pltpu.with_memory_space_constraint(a, pltpu.HBM) for a in thru])
    return (outs[0], outs[1]), outs[2], outs[3], list(outs[4:])


def scatter_wait(name, arr, land, sems, after):
    def body(in_ref, land_ref, send, recv, after_ref, in_thru, land_thru):
        _, peers = _group(_ALL)
        for k, (idx, dev) in enumerate(peers):
            cp = pltpu.make_async_remote_copy(src_ref=in_ref.at[idx], dst_ref=land_ref.at[idx], send_sem=send.at[k],
                                              recv_sem=recv.at[k], device_id=dev, device_id_type=MESH)
            cp.wait_send()
            cp.wait_recv()

    outs = pl.pallas_call(
        body, name=name, out_shape=(pltpu.HBM(arr.shape, arr.dtype), pltpu.HBM(arr.shape, arr.dtype)),
        in_specs=[_HBM, _HBM, _SEM, _SEM, ANY], out_specs=[_HBM, _HBM], input_output_aliases={0: 0, 1: 1},
        compiler_params=pltpu.CompilerParams(has_side_effects=_EFFECT),
    )(arr, land, sems[0], sems[1], after)
    return outs[0], outs[1]


def _row_tile(rows, cap=512):
    return next(t for t in range(cap - cap % 16, 0, -16) if rows % t == 0)


def sum_shares(name, own, recv, me):
    n, rows, c = recv.shape
    tr = _row_tile(rows)

    def body(me_ref, *refs):
        acc = refs[0][...].astype(F32)
        for r in refs[1:n]:
            acc = acc + r[...].astype(F32)
        refs[n][...] = acc

    def slot(mask):
        return pl.BlockSpec((None, tr, c), lambda i, me, mask=mask: (jnp.bitwise_xor(me[0], mask), i, 0))

    spec = pltpu.PrefetchScalarGridSpec(
        num_scalar_prefetch=1, grid=(rows // tr,), in_specs=[slot(k) for k in range(n)],
        out_specs=pl.BlockSpec((tr, c), lambda i, me: (i, 0)))
    return pl.pallas_call(body, grid_spec=spec, out_shape=S((rows, c), F32),
                          compiler_params=_cp("parallel"), name=name)(me, own, *([recv] * (n - 1)))


def sum_slots(name, slots):
    n, r, c = slots.shape

    def body(s_ref, o_ref):
        acc = s_ref[0]
        for j in range(1, n):
            acc = acc + s_ref[j]
        o_ref[...] = acc

    return pl.pallas_call(body, out_shape=S((r, c), F32), compiler_params=pltpu.CompilerParams(vmem_limit_bytes=VMEM_LIMIT),
                          name=name)(slots)


def adamw_native(name, g, w, m, v, tr=512):
    shape = w.shape
    cols = shape[-1]
    rows = w.size // cols
    tr = _tile(rows, tr) if rows % 8 == 0 else rows
    c1 = 1.0 - ADAM_B1 ** ADAM_STEP
    c2 = 1.0 - ADAM_B2 ** ADAM_STEP

    def body(g_ref, w_ref, m_ref, v_ref, d_ref, m2_ref, v2_ref):
        gv = g_ref[...]
        m2 = ADAM_B1 * m_ref[...] + (1.0 - ADAM_B1) * gv
        v2 = ADAM_B2 * v_ref[...] + (1.0 - ADAM_B2) * (gv * gv)
        m2_ref[...] = m2
        v2_ref[...] = v2
        d_ref[...] = -ADAM_LR * ((m2 / c1) / (jnp.sqrt(v2 / c2) + ADAM_EPS) + ADAM_WD * w_ref[...])

    row = pl.BlockSpec((tr, cols), lambda i: (i, 0))
    outs = pl.pallas_call(body, grid=(rows // tr,), in_specs=[row] * 4, out_specs=[row] * 3,
                          out_shape=[S((rows, cols), F32)] * 3, compiler_params=_cp("parallel"),
                          name=name)(*[a.reshape(rows, cols) for a in (g, w, m, v)])
    return tuple(o.reshape(shape) for o in outs)


_REPLICATED = ("e_norm", "e_gmlp_w", "e_gmlp_b", "e_conv_b", "e_conv_ln_g", "e_conv_ln_b", "o_lam_re", "o_lam_im", "o_log_dt",
               "o_b_re", "o_b_im", "o_c_re", "o_c_im", "ca_norm", "ca_mem_norm", "ffn_norm", "final_norm")
_ORDER = ("e_norm", "e_w_in", "e_gmlp_w", "e_gmlp_b", "e_conv_w", "e_conv_b", "e_conv_ln_g", "e_conv_ln_b", "e_w_out",
          "o_norm", "o_w_in", "o_lam_re", "o_lam_im", "o_log_dt", "o_b_re", "o_b_im", "o_c_re", "o_c_im", "o_d", "o_w_out",
          "ca_norm", "ca_mem_norm", "ca_wq", "ca_wk", "ca_wv", "ca_wo", "ffn_norm", "ffn_w_gate", "ffn_w_up", "ffn_w_down",
          "final_norm")


def _rows128(a, multiple=8):
    flat = a.reshape(-1)
    rows = -(-flat.shape[0] // (LANES * multiple)) * multiple
    return jnp.pad(flat, (0, rows * LANES - flat.shape[0])).reshape(rows, LANES)


def _shard(full, axis):
    s = full.shape
    return jnp.moveaxis(full.reshape(s[:axis] + (N_DEV, s[axis] // N_DEV) + s[axis + 1:]), axis, 0)


_UNITS = (("e_w_in", 0, True), ("e_w_out", 0, False), ("o_w_in", 0, False), ("o_w_out", 0, True),
          *[(n, i, False) for n in ("ca_wq", "ca_wk", "ca_wv", "ca_wo") for i in (0, 1)],
          *[(n, i, tr) for n, tr in (("ffn_w_gate", True), ("ffn_w_up", True), ("ffn_w_down", False)) for i in (0, 1)])
_LAYERED = ("ca_wq", "ca_wk", "ca_wv", "ca_wo", "ffn_w_gate", "ffn_w_up", "ffn_w_down")
_SMALL_SHARDED = (("e_conv_w", 2), ("o_norm", 1), ("o_d", 1))
RS_ROW = 1024


def _unit_key(name, tr):
    return name + "_t" if tr else name


def _stage_of(name, layer):
    if name.startswith("e_"):
        return 0
    if name.startswith("o_"):
        return 2
    if name.startswith("ca_"):
        return 1 if layer == 0 else 3
    return 2 if layer == 0 else 3


def weight_fetcher(local):
    groups, meta = [[] for _ in range(4)], [[] for _ in range(4)]
    for name, layer, tr in _UNITS:
        blk = local[name][layer]
        st = _stage_of(name, layer)
        groups[st].append(_bf(blk.T if tr else blk))
        meta[st].append((name, layer, tr))
    small = jnp.concatenate([local[name].reshape(-1) for name, _ in _SMALL_SHARDED])
    groups[0].append(_rows128(small))
    sems, srcs, lands, token = gather_ici_start("ag_w_start", groups)

    def fetch(stage, after):
        if stage == 0:
            after = token
        landed = gather_ici_wait(f"ag_w_wait{stage}", srcs[stage], lands[stage], sems[stage], after)
        bufs = gather_d2d(f"ag_w_d2d{stage}", landed)
        got = {}
        for (name, layer, tr), blk, buf in zip(meta[stage], groups[stage], bufs):
            arr = buf.reshape((N_DEV * blk.shape[0],) + tuple(blk.shape[1:]))
            if name in _LAYERED:
                got[(_unit_key(name, tr), layer)] = arr
            else:
                got[_unit_key(name, tr)] = arr
        if stage == 0:
            flat = bufs[-1].reshape(N_DEV, -1)
            off = 0
            for name, axis in _SMALL_SHARDED:
                blk = local[name]
                seg = flat[:, off:off + blk.size].reshape((N_DEV,) + blk.shape)
                off += blk.size
                seg = jnp.moveaxis(seg, 0, axis)
                got[name] = seg.reshape(seg.shape[:axis] + (-1,) + seg.shape[axis + 2:])
            got["e_conv_w"] = got["e_conv_w"][0]
        return got

    return fetch


def _grad_stage_of(name, layer):
    if name.startswith("e_"):
        return 4
    if name.startswith("o_"):
        return 1
    if name.startswith("ca_"):
        return 3 if layer == 0 else 1
    return 2 if layer == 0 else 0


GRAD_STAGES = 5
SMALL_ROWS = 16


def gradient_reducer(local, mom, var):
    me = (4 * lax.axis_index("x") + 2 * lax.axis_index("y") + lax.axis_index("c")).astype(jnp.int32).reshape(1)
    pending = []

    def start(stage, grads, carry):
        units = [u for u in _UNITS if _grad_stage_of(u[0], u[1]) == stage]
        parts, spans = [], []
        for name, layer, tr in units:
            key = _unit_key(name, tr)
            g = grads[(key, layer)] if name in _LAYERED else grads[key]
            part = g.reshape(4, 2, -1, RS_ROW)
            spans.append((part.shape[2], g.shape[0] // N_DEV, g.shape[1]))
            parts.append(part)
        if stage == GRAD_STAGES - 1:
            small = jnp.concatenate([_shard(grads[name], axis).reshape(N_DEV, -1) for name, axis in _SMALL_SHARDED], axis=1)
            small = jnp.pad(small, ((0, 0), (0, SMALL_ROWS * RS_ROW - small.shape[1])))
            parts.append(small.astype(BF16).reshape(4, 2, SMALL_ROWS, RS_ROW))
        pack = jnp.concatenate(parts, axis=2)
        pack = pack.reshape((N_DEV,) + pack.shape[2:])
        sems, own, land, carry = scatter_start(f"rs_start{stage}", pack, carry)
        pending.append((stage, units, spans, sems, own, land))
        return carry

    def finish(after):
        res, per_layer, small_flat = {}, {}, None
        for stage, units, spans, sems, own, land in pending:
            own, land = scatter_wait(f"rs_wait{stage}", own, land, sems, after)
            total = sum_shares(f"rs_sum{stage}", own, land, me)
            off = 0
            for (name, layer, tr), (rows, r, c) in zip(units, spans):
                g = total[off:off + rows].reshape(r, c)
                off += rows
                per_layer.setdefault(name, {})[layer] = g.T if tr else g
            if stage == GRAD_STAGES - 1:
                small_flat = total[off:off + SMALL_ROWS].reshape(-1)
        for name, by_layer in per_layer.items():
            g = jnp.stack([by_layer[i] for i in sorted(by_layer)]) if name in _LAYERED else by_layer[0][None]
            res[name] = (g,) + adamw_native("adamw_" + name, g, local[name], mom[name], var[name])
        off = 0
        for name, _ in _SMALL_SHARDED:
            blk = local[name]
            g = small_flat[off:off + blk.size].reshape(blk.shape)
            off += blk.size
            res[name] = (g,) + adamw_native("adamw_" + name, g, blk, mom[name], var[name])
        return res

    return start, finish


def replicated_start(grads, loss):
    pack = jnp.concatenate([_rows128(grads[name]) for name in _REPLICATED] + [_rows128(loss)], axis=0)
    sems, srcs, lands, token = gather_ici_start("ag_g_start", [[pack]])
    return sems[0], srcs[0], lands[0], token


def replicated_finish(handle, after, w, mom, var):
    sems, srcs, lands, _ = handle
    (buf,) = gather_d2d("ag_g_d2d", gather_ici_wait("ag_g_wait", srcs, lands, sems, after))
    rows = srcs[0].shape[0]
    total = sum_slots("ag_g_sum", buf.reshape(N_DEV, rows, LANES))
    res, off = {}, 0
    for name in _REPLICATED:
        n = w[name].size
        nr = -(-n // (LANES * 8)) * 8
        g = total[off:off + nr].reshape(-1)[:n].reshape(w[name].shape)
        off += nr
        res[name] = (g,) + adamw_native("adamw_" + name, g, w[name], mom[name], var[name])
    return res, total[off, 0]


def kernel(x, mem, e_norm, e_w_in, e_gmlp_w, e_gmlp_b, e_conv_w, e_conv_b, e_conv_ln_g, e_conv_ln_b, e_w_out, o_norm, o_w_in, o_lam_re, o_lam_im, o_log_dt, o_b_re, o_b_im, o_c_re, o_c_im, o_d, o_w_out, ca_norm, ca_mem_norm, ca_wq, ca_wk, ca_wv, ca_wo, ffn_norm, ffn_w_gate, ffn_w_up, ffn_w_down, final_norm, loss_target, m_e_norm, m_e_w_in, m_e_gmlp_w, m_e_gmlp_b, m_e_conv_w, m_e_conv_b, m_e_conv_ln_g, m_e_conv_ln_b, m_e_w_out, m_o_norm, m_o_w_in, m_o_lam_re, m_o_lam_im, m_o_log_dt, m_o_b_re, m_o_b_im, m_o_c_re, m_o_c_im, m_o_d, m_o_w_out, m_ca_norm, m_ca_mem_norm, m_ca_wq, m_ca_wk, m_ca_wv, m_ca_wo, m_ffn_norm, m_ffn_w_gate, m_ffn_w_up, m_ffn_w_down, m_final_norm, v_e_norm, v_e_w_in, v_e_gmlp_w, v_e_gmlp_b, v_e_conv_w, v_e_conv_b, v_e_conv_ln_g, v_e_conv_ln_b, v_e_w_out, v_o_norm, v_o_w_in, v_o_lam_re, v_o_lam_im, v_o_log_dt, v_o_b_re, v_o_b_im, v_o_c_re, v_o_c_im, v_o_d, v_o_w_out, v_ca_norm, v_ca_mem_norm, v_ca_wq, v_ca_wk, v_ca_wv, v_ca_wo, v_ffn_norm, v_ffn_w_gate, v_ffn_w_up, v_ffn_w_down, v_final_norm):
    given = dict(locals())
    local = {k: given[k] for k in _ORDER}
    mom = {k: given["m_" + k] for k in _ORDER}
    var = {k: given["v_" + k] for k in _ORDER}

    w = {}
    w.update({
        "e_norm": e_norm, "e_gmlp_w": e_gmlp_w[0], "e_gmlp_b": e_gmlp_b.reshape(A_GROUPS, GMLP_BLOCK, 1),
        "e_conv_b": e_conv_b, "e_conv_ln_g": e_conv_ln_g, "e_conv_ln_b": e_conv_ln_b,
        "o_lam_re": o_lam_re[0], "o_lam_im": o_lam_im[0], "o_log_dt": o_log_dt[0], "o_b_re": o_b_re[0], "o_b_im": o_b_im[0],
        "o_c_re": o_c_re[0], "o_c_im": o_c_im[0], "ca_norm": ca_norm, "ca_mem_norm": ca_mem_norm, "ffn_norm": ffn_norm,
        "final_norm": final_norm.reshape(1, D_MODEL),
    })
    start_reduce, finish_reduce = gradient_reducer(local, mom, var)
    loss_part, grad_x, grads = local_step(x[0], mem[0], loss_target[0], w, weight_fetcher(local), start_reduce)
    grads["final_norm"] = grads["final_norm"].reshape(D_MODEL)

    handle = replicated_start(grads, loss_part)
    res = finish_reduce(handle[3])
    rep, loss = replicated_finish(handle, res["ffn_w_down"][1], local, mom, var)
    res.update(rep)
    return (loss, grad_x[None], *[res[k][0] for k in _ORDER], *[res[k][1] for k in _ORDER],
            *[res[k][2] for k in _ORDER], *[res[k][3] for k in _ORDER])
```

```python
import jax
import jax.numpy as jnp
from jax import lax
from jax.experimental import pallas as pl
from jax.experimental.pallas import tpu as pltpu

F32 = jnp.float32
BF16 = jnp.bfloat16
S = jax.ShapeDtypeStruct

D_MODEL = 1024
A_WIDTH = 512
A_GROUPS = 4
GMLP_BLOCK = 128
CHUNK = 64
B_WIDTH = 512
IN_WIDTH = 2 * A_WIDTH + 2 * B_WIDTH
CONV_WIDTH = 31
CONV_PAD = 32
C_WIDTH = 512
C_GROUP_CH = 16
C_GROUPS = 32
C_STATE = 64
N_STATE = C_GROUPS * C_STATE
CA_HEADS = 4
CA_HEAD_DIM = 256
FFN_HIDDEN = 2816
EPS = 1e-6
ADAM_LR = 0.001
ADAM_B1 = 0.9
ADAM_B2 = 0.999
ADAM_EPS = 1e-08
ADAM_WD = 0.01
ADAM_STEP = 10
N_DEV = 8
LANES = 128
VMEM_LIMIT = 56 << 20
VMEM_BUDGET = 40 << 20
MM_TN_RESIDENT = 8 << 20
MESH = pl.DeviceIdType.MESH
ANY = pl.BlockSpec(memory_space=pl.ANY)


def _cp(*sem):
    return pltpu.CompilerParams(dimension_semantics=sem, vmem_limit_bytes=VMEM_LIMIT)


def _tile(n, pref):
    t = pref
    while n % t:
        t //= 2
    return t


def _bf(v):
    return v if v.dtype == BF16 else v.astype(BF16)


def _sigmoid(x):
    return 1.0 / (1.0 + jnp.exp(-x))


_GC = 0.7978845608028654


def _gelu(x):
    return 0.5 * x * (1.0 + jnp.tanh(_GC * (x + 0.044715 * x * x * x)))


def _gelu_grad(x):
    x2 = x * x
    t = jnp.tanh(_GC * (x + 0.044715 * x * x2))
    return 0.5 * (1.0 + t) + 0.5 * x * (1.0 - t * t) * _GC * (1.0 + 3.0 * 0.044715 * x2)


def _tspec(entry, tm):
    if isinstance(entry, tuple):
        arr, cb, width = entry
        return arr, pl.BlockSpec((tm, width), lambda i, cb=cb: (i, cb))
    return entry, pl.BlockSpec((tm, entry.shape[1]), lambda i: (i, 0))


def rows_call(name, fn, tiled, full, outs, accs, tm=256):
    pairs = [_tspec(e, tm) for e in tiled]
    arrs = [p[0] for p in pairs]
    rows = arrs[0].shape[0]
    tm = _tile(rows, tm)
    pairs = [_tspec(e, tm) for e in tiled]
    n_in = len(tiled) + len(full)
    n_out = len(outs)

    def body(*refs):
        vals = [r[...] for r in refs[:n_in]]
        o_refs = refs[n_in:n_in + n_out]
        a_refs = refs[n_in + n_out:]
        ov, av = fn(*vals)
        for r, v in zip(o_refs, ov):
            r[...] = v.astype(r.dtype)
        if a_refs:
            @pl.when(pl.program_id(0) == 0)
            def _():
                for r in a_refs:
                    r[...] = jnp.zeros(r.shape, r.dtype)
            for r, v in zip(a_refs, av):
                r[...] += v

    in_specs = [p[1] for p in pairs] + [pl.BlockSpec(a.shape, lambda i, nd=a.ndim: (0,) * nd) for a in full]
    out_specs = [pl.BlockSpec((tm, c), lambda i: (i, 0)) for c, _ in outs]
    out_specs += [pl.BlockSpec(s, lambda i, nd=len(s): (0,) * nd) for s in accs]
    out_shape = [S((rows, c), dt) for c, dt in outs] + [S(s, F32) for s in accs]
    return pl.pallas_call(body, grid=(rows // tm,), in_specs=in_specs, out_specs=out_specs, out_shape=out_shape,
                          compiler_params=_cp("arbitrary"), name=name)(*arrs, *full)


def mm_nn(name, m, n, pairs, n_acc, epi, outs, tiled=(), cols=(), rowv=(), sums=(), norm_gain=None):
    a_ops, a_slot, b_arrs, b_specs, idx, trans = [], [], [], [], [], []
    fixed = 0
    for pair in pairs:
        a, b, k = pair[:3]
        bt = len(pair) > 3
        arr, cb, kdim = a if isinstance(a, tuple) else (a, 0, a.shape[1])
        key = (id(arr), cb, kdim)
        if key not in [o[0] for o in a_ops]:
            a_ops.append((key, arr, cb, kdim))
        a_slot.append([o[0] for o in a_ops].index(key))
        b_arr, off = b if isinstance(b, tuple) else (b, 0)
        b_arrs.append(b_arr)
        if bt:
            assert off % n == 0 and b_arr.shape[1] == kdim
            b_specs.append(pl.BlockSpec((n, kdim), lambda i, o=off // n: (o, 0), pipeline_mode=pl.Buffered(1)))
        else:
            assert b_arr.shape[1] == n
            b_specs.append(pl.BlockSpec((kdim, n), lambda i, o=off: (o, 0), pipeline_mode=pl.Buffered(1)))
        fixed += kdim * n * b_arr.dtype.itemsize
        idx.append(k)
        trans.append(bt)
    per_row = sum(2 * kdim * arr.dtype.itemsize for _, arr, _, kdim in a_ops)
    per_row += sum(2 * n * t.dtype.itemsize for t in tiled) + sum(2 * n * jnp.dtype(dt).itemsize for dt in outs)
    cn = n if sums or cols else (512 if n % 512 == 0 else 256)
    per_row += (n_acc + 3) * cn * 4
    tm = next((t for t in (1024, 512, 256, 128) if m % t == 0 and fixed + t * per_row <= VMEM_BUDGET), _tile(m, 128))
    n_a, n_p, n_t = len(a_ops), len(pairs), len(tiled)
    n_in = n_a + n_p + n_t + len(cols) + len(rowv)
    normed = norm_gain is not None
    o0 = n_in + normed

    def body(*refs):
        a_vals = [None if normed and i == 0 else _bf(r[...]) for i, r in enumerate(refs[:n_a])]
        if normed:
            xv = refs[0][...]
            rv = lax.rsqrt(jnp.mean(xv * xv, axis=-1, keepdims=True) + EPS)
            a_vals[0] = (xv * rv * refs[n_in][...]).astype(BF16)
            refs[o0 + len(outs)][...] = a_vals[0]
            refs[o0 + len(outs) + 1][...] = rv
        for j in range(n // cn):
            cs = slice(j * cn, (j + 1) * cn)
            accs = [None] * n_acc
            for p in range(n_p):
                av, b_ref = a_vals[a_slot[p]], refs[n_a + p]
                if trans[p]:
                    d = lax.dot_general(av, _bf(b_ref[cs, :]), (((1,), (1,)), ((), ())), preferred_element_type=F32)
                else:
                    d = jnp.dot(av, _bf(b_ref[:, cs]), preferred_element_type=F32)
                accs[idx[p]] = d if accs[idx[p]] is None else accs[idx[p]] + d
            extra = [r[:, cs] for r in refs[n_a + n_p:n_a + n_p + n_t]] + [r[...] for r in refs[n_a + n_p + n_t:n_in - len(rowv)]]
            extra += [r[:, cs] for r in refs[n_in - len(rowv):n_in]]
            ov = epi(accs, *extra)
            for r, v in zip(refs[o0:o0 + len(outs)], ov):
                r[:, cs] = v.astype(r.dtype)
        sv = ov[len(outs):]
        if sums:
            s_refs = refs[o0 + len(outs) + 2 * normed:]

            @pl.when(pl.program_id(0) == 0)
            def _():
                for r in s_refs:
                    r[...] = jnp.zeros(r.shape, r.dtype)
            for r, v in zip(s_refs, sv):
                r[...] += v

    in_specs = [pl.BlockSpec((tm, kdim), lambda i, cb=cb: (i, cb)) for _, _, cb, kdim in a_ops] + b_specs
    in_specs += [pl.BlockSpec((tm, n), lambda i: (i, 0)) for _ in tiled]
    in_specs += [pl.BlockSpec((tm, 1), lambda i: (i, 0)) for _ in cols]
    in_specs += [pl.BlockSpec((1, n), lambda i: (0, 0)) for _ in rowv]
    out_specs = [pl.BlockSpec((tm, n), lambda i: (i, 0)) for _ in outs]
    out_shape = [S((m, n), dt) for dt in outs]
    gain = []
    if normed:
        k0 = a_ops[0][3]
        gain = [norm_gain]
        in_specs.append(pl.BlockSpec((1, k0), lambda i: (0, 0)))
        out_specs += [pl.BlockSpec((tm, k0), lambda i: (i, 0)), pl.BlockSpec((tm, 1), lambda i: (i, 0))]
        out_shape += [S((m, k0), BF16), S((m, 1), F32)]
    out_specs += [pl.BlockSpec(s, lambda i, nd=len(s): (0,) * nd) for s in sums]
    out_shape += [S(s, F32) for s in sums]
    return pl.pallas_call(body, grid=(m // tm,), in_specs=in_specs, out_specs=out_specs, out_shape=out_shape,
                          compiler_params=_cp("arbitrary" if sums else "parallel"),
                          name=name)(*[o[1] for o in a_ops], *b_arrs, *tiled, *cols, *rowv, *gain)


def mm_tn(name, a, b, out_dtype=BF16):
    if isinstance(a, tuple):
        a_arr, a_cb, m = a
    else:
        a_arr, a_cb, m = a, None, a.shape[1]
    if isinstance(b, tuple):
        b_arr, b_cb, n = b
    else:
        b_arr, b_cb, n = b, None, b.shape[1]
    t = a_arr.shape[0]
    whole_b = t * n * b_arr.dtype.itemsize <= MM_TN_RESIDENT and b_cb is None
    tn = n if whole_b else _tile(n, 512)
    tm = _tile(m, 512 if t * 512 * a_arr.dtype.itemsize * 2 + t * tn * b_arr.dtype.itemsize * 2 <= VMEM_BUDGET else 256)
    a_off = 0 if a_cb is None else a_cb * (m // tm)
    b_off = 0 if b_cb is None else b_cb * (n // tn)

    def body(a_ref, b_ref, o_ref):
        o_ref[...] = lax.dot_general(_bf(a_ref[...]), _bf(b_ref[...]), (((0,), (0,)), ((), ())),
                                     preferred_element_type=F32).astype(o_ref.dtype)

    if whole_b:
        b_spec = pl.BlockSpec((t, n), lambda i, j: (0, 0), pipeline_mode=pl.Buffered(1))
    else:
        b_spec = pl.BlockSpec((t, tn), lambda i, j: (0, j + b_off))
    return pl.pallas_call(
        body, grid=(m // tm, n // tn),
        in_specs=[pl.BlockSpec((t, tm), lambda i, j: (0, i + a_off)), b_spec],
        out_specs=pl.BlockSpec((tm, tn), lambda i, j: (i, j)), out_shape=S((m, n), out_dtype),
        compiler_params=_cp("parallel", "parallel"), name=name)(a_arr, b_arr)


def rms_bwd_gain_only(name, dxn, x, r):
    def fn(dv, xv, rv):
        return [], [jnp.sum(dv * xv * rv, axis=0, keepdims=True)]
    return rows_call(name, fn, [dxn, x, r], [], [], [(1, x.shape[1])])[0]


def final_loss(name, x, gain, target):
    d = x.shape[1]

    def fn(xv, tv, g):
        r = lax.rsqrt(jnp.mean(xv * xv, axis=-1, keepdims=True) + EPS)
        xh = xv * r
        err = xh * g - tv
        dy = err * (1.0 / d)
        w = dy * g
        dx = r * (w - xh * jnp.mean(w * xh, axis=-1, keepdims=True))
        part = jnp.sum(jnp.sum(err * err, axis=-1, keepdims=True), axis=0, keepdims=True) * (0.5 / d)
        return [dx, dx], [jnp.sum(dy * xh, axis=0, keepdims=True), part]

    return rows_call(name, fn, [x, target], [gain], [(d, F32), (d, BF16)], [(1, d), (1, 1)])


def _gmlp_mask():
    row = lax.broadcasted_iota(jnp.int32, (GMLP_BLOCK, GMLP_BLOCK), 0) // CHUNK
    col = lax.broadcasted_iota(jnp.int32, (GMLP_BLOCK, GMLP_BLOCK), 1) // CHUNK
    return col <= row


def _ln_plain(v):
    mu = jnp.mean(v, axis=-1, keepdims=True)
    vc = v - mu
    rstd = lax.rsqrt(jnp.mean(vc * vc, axis=-1, keepdims=True) + EPS)
    return vc * rstd, rstd


def gmlp_fwd(name, proj, w, b, tm=512):
    t = proj.shape[0]
    tm = _tile(t, tm)

    def body(au_ref, av_ref, w_ref, b_ref, o_ref):
        mask = _gmlp_mask()
        u = _gelu(au_ref[...])
        vn, _ = _ln_plain(_gelu(av_ref[...]))
        vnb = _bf(vn)
        for g in range(A_GROUPS):
            wg = _bf(jnp.where(mask, w_ref[g], 0.0))
            cs = slice(g * GMLP_BLOCK, (g + 1) * GMLP_BLOCK)
            for n in range(tm // GMLP_BLOCK):
                rs = slice(n * GMLP_BLOCK, (n + 1) * GMLP_BLOCK)
                sg = jnp.dot(wg, vnb[rs, cs], preferred_element_type=F32) + b_ref[g]
                o_ref[rs, cs] = (u[rs, cs] * sg).astype(o_ref.dtype)

    return pl.pallas_call(
        body, grid=(t // tm,),
        in_specs=[pl.BlockSpec((tm, A_WIDTH), lambda i: (i, 0)), pl.BlockSpec((tm, A_WIDTH), lambda i: (i, 1)),
                  pl.BlockSpec(w.shape, lambda i: (0, 0, 0)), pl.BlockSpec(b.shape, lambda i: (0, 0, 0))],
        out_specs=pl.BlockSpec((tm, A_WIDTH), lambda i: (i, 0)), out_shape=S((t, A_WIDTH), BF16),
        compiler_params=_cp("parallel"), name=name)(proj, proj, w, b)


def gmlp_bwd(name, proj, dcat, w, b, tm=512):
    t = proj.shape[0]
    tm = _tile(t, tm)

    def body(au_ref, av_ref, do_ref, w_ref, b_ref, dp_ref, dw_ref, db_ref):
        @pl.when(pl.program_id(0) == 0)
        def _():
            dw_ref[...] = jnp.zeros(dw_ref.shape, F32)
            db_ref[...] = jnp.zeros(db_ref.shape, F32)

        mask = _gmlp_mask()
        au = au_ref[...]
        av = av_ref[...]
        u = _gelu(au)
        vn, rstd = _ln_plain(_gelu(av))
        vnb = _bf(vn)
        dout = do_ref[...]
        dvn_cols = []
        for g in range(A_GROUPS):
            wm = jnp.where(mask, w_ref[g], 0.0)
            wg = _bf(wm)
            wgt = _bf(wm.T)
            cs = slice(g * GMLP_BLOCK, (g + 1) * GMLP_BLOCK)
            dwg = jnp.zeros((GMLP_BLOCK, GMLP_BLOCK), F32)
            dbg = jnp.zeros((GMLP_BLOCK, 1), F32)
            dvn_rows = []
            for n in range(tm // GMLP_BLOCK):
                rs = slice(n * GMLP_BLOCK, (n + 1) * GMLP_BLOCK)
                sg = jnp.dot(wg, vnb[rs, cs], preferred_element_type=F32) + b_ref[g]
                dp_ref[rs, cs] = (dout[rs, cs] * sg * _gelu_grad(au[rs, cs])).astype(dp_ref.dtype)
                dsg = dout[rs, cs] * u[rs, cs]
                dsgb = _bf(dsg)
                dbg = dbg + jnp.sum(dsg, axis=1, keepdims=True)
                dwg = dwg + lax.dot_general(dsgb, vnb[rs, cs], (((1,), (1,)), ((), ())), preferred_element_type=F32)
                dvn_rows.append(jnp.dot(wgt, dsgb, preferred_element_type=F32))
            dw_ref[g] += jnp.where(mask, dwg, 0.0)
            db_ref[g] += dbg
            dvn_cols.append(jnp.concatenate(dvn_rows, axis=0))
        dvn = jnp.concatenate(dvn_cols, axis=1)
        dv = rstd * (dvn - jnp.mean(dvn, axis=-1, keepdims=True) - vn * jnp.mean(dvn * vn, axis=-1, keepdims=True))
        dp_ref[:, A_WIDTH:] = (dv * _gelu_grad(av)).astype(dp_ref.dtype)

    return pl.pallas_call(
        body, grid=(t // tm,),
        in_specs=[pl.BlockSpec((tm, A_WIDTH), lambda i: (i, 0)), pl.BlockSpec((tm, A_WIDTH), lambda i: (i, 1)),
                  pl.BlockSpec((tm, A_WIDTH), lambda i: (i, 0)),
                  pl.BlockSpec(w.shape, lambda i: (0, 0, 0)), pl.BlockSpec(b.shape, lambda i: (0, 0, 0))],
        out_specs=[pl.BlockSpec((tm, 2 * A_WIDTH), lambda i: (i, 0)),
                   pl.BlockSpec(w.shape, lambda i: (0, 0, 0)), pl.BlockSpec(b.shape, lambda i: (0, 0, 0))],
        out_shape=[S((t, 2 * A_WIDTH), BF16), S(w.shape, F32), S(b.shape, F32)],
        compiler_params=_cp("arbitrary"), name=name)(proj, proj, dcat, w, b)


CONV_ROWS = 256


def conv_fwd(name, proj, w, cb):
    t = proj.shape[0]
    tc = LANES
    rows = _tile(t, CONV_ROWS)
    a_cb, g_cb = 2 * A_WIDTH // tc, (2 * A_WIDTH + B_WIDTH) // tc

    def body(a_ref, g_ref, w_ref, cb_ref, o_ref, hpad):
        hpad[0:CONV_PAD, :] = jnp.zeros((CONV_PAD, tc), F32)

        def fill(i, _):
            r0 = pl.multiple_of(i * rows, rows)
            hpad[pl.ds(CONV_PAD + r0, rows), :] = a_ref[pl.ds(r0, rows), :] * _sigmoid(g_ref[pl.ds(r0, rows), :])
            return 0
        lax.fori_loop(0, t // rows, fill, 0)

        def conv(i, _):
            r0 = pl.multiple_of(i * rows, rows)
            win = hpad[pl.ds(r0, rows + CONV_PAD), :]
            acc = jnp.zeros((rows, tc), F32) + cb_ref[...]
            for b in range(SUB):
                wb = win if b == 0 else pltpu.roll(win, b, 0)
                for a in range(CONV_PAD // SUB):
                    k = CONV_WIDTH - 1 - (SUB * a + b)
                    if k >= 0:
                        lo = CONV_PAD - SUB * a
                        acc = acc + wb[lo:lo + rows, :] * w_ref[k:k + 1, :]
            o_ref[pl.ds(r0, rows), :] = acc
            return 0
        lax.fori_loop(0, t // rows, conv, 0)

    return pl.pallas_call(
        body, grid=(B_WIDTH // tc,),
        in_specs=[pl.BlockSpec((t, tc), lambda j: (0, a_cb + j)), pl.BlockSpec((t, tc), lambda j: (0, g_cb + j)),
                  pl.BlockSpec((CONV_WIDTH, tc), lambda j: (0, j)), pl.BlockSpec((1, tc), lambda j: (0, j))],
        out_specs=pl.BlockSpec((t, tc), lambda j: (0, j)), out_shape=S((t, B_WIDTH), F32),
        scratch_shapes=[pltpu.VMEM((t + CONV_PAD, tc), F32)],
        compiler_params=_cp("parallel"), name=name)(proj, proj, w, cb)


def conv_bwd(name, proj, dhc, w):
    t = proj.shape[0]
    tc = LANES
    rows = _tile(t, CONV_ROWS)
    a_cb, g_cb = 2 * A_WIDTH // tc, (2 * A_WIDTH + B_WIDTH) // tc
    win_rows = rows + CONV_PAD

    def body(a_ref, g_ref, d_ref, w_ref, da_ref, dg_ref, dw_ref, dcb_ref, hpad, dpad, dwacc):
        hpad[0:CONV_PAD, :] = jnp.zeros((CONV_PAD, tc), F32)
        dpad[t:t + CONV_PAD, :] = jnp.zeros((CONV_PAD, tc), F32)
        dwacc[...] = jnp.zeros(dwacc.shape, F32)

        def fill(i, _):
            r0 = pl.multiple_of(i * rows, rows)
            hpad[pl.ds(CONV_PAD + r0, rows), :] = a_ref[pl.ds(r0, rows), :] * _sigmoid(g_ref[pl.ds(r0, rows), :])
            dpad[pl.ds(r0, rows), :] = d_ref[pl.ds(r0, rows), :]
            return 0
        lax.fori_loop(0, t // rows, fill, 0)

        def step(i, dcb):
            r0 = pl.multiple_of(i * rows, rows)
            hwin = hpad[pl.ds(r0, win_rows), :]
            dwin = dpad[pl.ds(r0, win_rows), :]
            dchunk = dwin[:rows, :]
            dh = jnp.zeros((rows, tc), F32)
            for b in range(SUB):
                hb = hwin if b == 0 else pltpu.roll(hwin, b, 0)
                db = dwin if b == 0 else pltpu.roll(dwin, win_rows - b, 0)
                for a in range(CONV_PAD // SUB):
                    k = CONV_WIDTH - 1 - (SUB * a + b)
                    if k >= 0:
                        dh = dh + db[SUB * a:SUB * a + rows, :] * w_ref[k:k + 1, :]
                        lo = CONV_PAD - SUB * a
                        prod = dchunk * hb[lo:lo + rows, :]
                        dwacc[k] += jnp.sum(prod.reshape(rows // 8, 8, tc), axis=0)
            a = a_ref[pl.ds(r0, rows), :]
            sg = _sigmoid(g_ref[pl.ds(r0, rows), :])
            da_ref[pl.ds(r0, rows), :] = (dh * sg).astype(da_ref.dtype)
            dg_ref[pl.ds(r0, rows), :] = (dh * a * sg * (1.0 - sg)).astype(dg_ref.dtype)
            return dcb + jnp.sum(dchunk, axis=0, keepdims=True)
        dcb = lax.fori_loop(0, t // rows, step, jnp.zeros((1, tc), F32))
        dcb_ref[...] = dcb
        for k in range(CONV_WIDTH):
            dw_ref[k:k + 1, :] = jnp.sum(dwacc[k], axis=0, keepdims=True)

    return pl.pallas_call(
        body, grid=(B_WIDTH // tc,),
        in_specs=[pl.BlockSpec((t, tc), lambda j: (0, a_cb + j)), pl.BlockSpec((t, tc), lambda j: (0, g_cb + j)),
                  pl.BlockSpec((t, tc), lambda j: (0, j)), pl.BlockSpec((CONV_WIDTH, tc), lambda j: (0, j))],
        out_specs=[pl.BlockSpec((t, tc), lambda j: (0, j)), pl.BlockSpec((t, tc), lambda j: (0, j)),
                   pl.BlockSpec((CONV_WIDTH, tc), lambda j: (0, j)), pl.BlockSpec((1, tc), lambda j: (0, j))],
        out_shape=[S((t, B_WIDTH), BF16), S((t, B_WIDTH), BF16), S((CONV_WIDTH, B_WIDTH), F32), S((1, B_WIDTH), F32)],
        scratch_shapes=[pltpu.VMEM((t + CONV_PAD, tc), F32), pltpu.VMEM((t + CONV_PAD, tc), F32),
                        pltpu.VMEM((CONV_WIDTH, 8, tc), F32)],
        compiler_params=_cp("parallel"), name=name)(proj, proj, dhc, w)


def ln_silu_fwd(name, hc, g, b):
    def fn(h, gv, bv):
        y, _ = _ln_plain(h)
        z = y * gv + bv
        return [z * _sigmoid(z)], []
    return rows_call(name, fn, [hc], [g, b], [(hc.shape[1], BF16)], [])[0]


def ln_silu_bwd(name, hc, dcat, g, b):
    c = hc.shape[1]

    def fn(h, dout, gv, bv):
        y, rstd = _ln_plain(h)
        z = y * gv + bv
        s = _sigmoid(z)
        dz = dout * s * (1.0 + z * (1.0 - s))
        dyv = dz * gv
        dh = rstd * (dyv - jnp.mean(dyv, axis=-1, keepdims=True) - y * jnp.mean(dyv * y, axis=-1, keepdims=True))
        return [dh], [jnp.sum(dz * y, axis=0, keepdims=True), jnp.sum(dz, axis=0, keepdims=True)]

    return rows_call(name, fn, [hc, (dcat, 1, c)], [g, b], [(c, F32)], [(1, c), (1, c)])


_NT = (((1,), (1,)), ((), ()))
_TN = (((0,), (0,)), ((), ()))


def attn_fwd(name, q, k, v, tm=512):
    t, d = q.shape
    m = k.shape[0]
    tm = _tile(t, tm)
    scale = CA_HEAD_DIM ** -0.5

    def body(q_ref, k_ref, v_ref, o_ref):
        for h in range(CA_HEADS):
            cs = slice(h * CA_HEAD_DIM, (h + 1) * CA_HEAD_DIM)
            s = lax.dot_general(q_ref[:, cs], k_ref[:, cs], _NT, preferred_element_type=F32) * scale
            e = jnp.exp(s - jnp.max(s, axis=-1, keepdims=True))
            p = e / jnp.sum(e, axis=-1, keepdims=True)
            o_ref[:, cs] = jnp.dot(_bf(p), v_ref[:, cs], preferred_element_type=F32).astype(o_ref.dtype)

    return pl.pallas_call(
        body, grid=(t // tm,),
        in_specs=[pl.BlockSpec((tm, d), lambda i: (i, 0)), pl.BlockSpec((m, d), lambda i: (0, 0)),
                  pl.BlockSpec((m, d), lambda i: (0, 0))],
        out_specs=pl.BlockSpec((tm, d), lambda i: (i, 0)), out_shape=S((t, d), BF16),
        compiler_params=_cp("parallel"), name=name)(q, k, v)


def attn_bwd(name, q, k, v, do, tm=512):
    t, d = q.shape
    m = k.shape[0]
    tm = _tile(t, tm)
    scale = CA_HEAD_DIM ** -0.5

    def body(q_ref, k_ref, v_ref, do_ref, dq_ref, dk_ref, dv_ref):
        @pl.when(pl.program_id(0) == 0)
        def _():
            dk_ref[...] = jnp.zeros(dk_ref.shape, F32)
            dv_ref[...] = jnp.zeros(dv_ref.shape, F32)

        for h in range(CA_HEADS):
            cs = slice(h * CA_HEAD_DIM, (h + 1) * CA_HEAD_DIM)
            qh, kh, vh, doh = q_ref[:, cs], k_ref[:, cs], v_ref[:, cs], do_ref[:, cs]
            s = lax.dot_general(qh, kh, _NT, preferred_element_type=F32) * scale
            e = jnp.exp(s - jnp.max(s, axis=-1, keepdims=True))
            p = e / jnp.sum(e, axis=-1, keepdims=True)
            pb = _bf(p)
            dv_ref[:, cs] += lax.dot_general(pb, doh, _TN, preferred_element_type=F32)
            dp = lax.dot_general(doh, vh, _NT, preferred_element_type=F32)
            ds = _bf(p * (dp - jnp.sum(dp * p, axis=-1, keepdims=True)) * scale)
            dq_ref[:, cs] = jnp.dot(ds, kh, preferred_element_type=F32).astype(dq_ref.dtype)
            dk_ref[:, cs] += lax.dot_general(ds, qh, _TN, preferred_element_type=F32)

    return pl.pallas_call(
        body, grid=(t // tm,),
        in_specs=[pl.BlockSpec((tm, d), lambda i: (i, 0)), pl.BlockSpec((m, d), lambda i: (0, 0)),
                  pl.BlockSpec((m, d), lambda i: (0, 0)), pl.BlockSpec((tm, d), lambda i: (i, 0))],
        out_specs=[pl.BlockSpec((tm, d), lambda i: (i, 0)), pl.BlockSpec((m, d), lambda i: (0, 0)),
                   pl.BlockSpec((m, d), lambda i: (0, 0))],
        out_shape=[S((t, d), BF16), S((m, d), F32), S((m, d), F32)],
        compiler_params=_cp("arbitrary"), name=name)(q, k, v, do)


SUB = 8
S5_ROWS = 256


_HI = lax.Precision.HIGHEST
_GP = (C_GROUPS, C_STATE)
_RP = (C_WIDTH, C_STATE)


def _zoh(lr, li, ldt):
    dt = jnp.exp(ldt)
    mag = jnp.exp(lr * dt)
    ar = mag * jnp.cos(li * dt)
    ai = mag * jnp.sin(li * dt)
    den = lr * lr + li * li
    qr = ((ar - 1.0) * lr + ai * li) / den
    qi = (ai * lr - (ar - 1.0) * li) / den
    return dt, ar, ai, den, qr, qi


def _per_channel(v):
    return jnp.broadcast_to(v[:, None, :], (C_GROUPS, C_GROUP_CH, C_STATE)).reshape(_RP)


def _same_group(shape, row_per_group, col_per_group):
    rows = lax.broadcasted_iota(jnp.int32, shape, 0) // row_per_group
    cols = lax.broadcasted_iota(jnp.int32, shape, 1) // col_per_group
    return rows == cols


def _spread(shape, axis):
    long = lax.broadcasted_iota(jnp.int32, shape, axis) % C_STATE
    short = lax.broadcasted_iota(jnp.int32, shape, 1 - axis)
    return long == short


def s5_discretise(name, lam_re, lam_im, log_dt, bt_re, bt_im):
    def body(lr_ref, li_ref, ldt_ref, btr_ref, bti_ref, a_ref, bbr_ref, bbi_ref):
        _, ar, ai, _, qr, qi = _zoh(lr_ref[...], li_ref[...], ldt_ref[...])
        a_ref[0] = ar
        a_ref[1] = ai
        q2r, q2i = _per_channel(qr), _per_channel(qi)
        btr, bti = btr_ref[...], bti_ref[...]
        bbr_ref[...] = q2r * btr - q2i * bti
        bbi_ref[...] = q2r * bti + q2i * btr

    return pl.pallas_call(body, out_shape=[S((2,) + _GP, F32), S(_RP, F32), S(_RP, F32)],
                          name=name)(lam_re, lam_im, log_dt, bt_re, bt_im)


def s5_operands(name, a, bbr, bbi, c2r, c2i, ctr, cti):
    ns = N_STATE

    def body(a_ref, bbr_ref, bbi_ref, c2r_ref, c2i_ref, ctr_ref, cti_ref, pw_ref, qw_ref, mb_ref, mc_ref, mct_ref):
        ar, ai = a_ref[0:1, :], a_ref[1:2, :]
        pows = [(ar, ai)]
        for _ in range(SUB - 1):
            pr, pi = pows[-1]
            pows.append((pr * ar - pi * ai, pr * ai + pi * ar))
        rows = lax.broadcasted_iota(jnp.int32, (SUB, ns), 0)

        def rows_of(v):
            return jnp.broadcast_to(v, (SUB, ns))

        for k, s in enumerate((1, 2, 4)):
            pr, pi = rows_of(pows[s - 1][0]), rows_of(pows[s - 1][1])
            pw_ref[k, 0] = jnp.where(rows >= s, pr, 0.0)
            pw_ref[k, 1] = jnp.where(rows >= s, pi, 0.0)
            qw_ref[k, 0] = jnp.where(rows + s <= SUB - 1, pr, 0.0)
            qw_ref[k, 1] = jnp.where(rows + s <= SUB - 1, -pi, 0.0)
        fr = fi = br = bi = jnp.zeros((SUB, ns), F32)
        for i in range(SUB):
            fr = jnp.where(rows == i, rows_of(pows[i][0]), fr)
            fi = jnp.where(rows == i, rows_of(pows[i][1]), fi)
            br = jnp.where(rows == i, rows_of(pows[SUB - 1 - i][0]), br)
            bi = jnp.where(rows == i, rows_of(-pows[SUB - 1 - i][1]), bi)
        pw_ref[3, 0], pw_ref[3, 1], qw_ref[3, 0], qw_ref[3, 1] = fr, fi, br, bi

        wide = _spread((C_STATE, ns), 1).astype(BF16)
        tall = _spread((ns, C_STATE), 0).astype(BF16)
        in_rows = _same_group((C_WIDTH, ns), C_GROUP_CH, C_STATE)
        in_cols = _same_group((ns, C_WIDTH), C_STATE, C_GROUP_CH)

        def across(v, sign=1.0):
            return jnp.where(in_rows, sign * jnp.dot(_bf(v), wide, preferred_element_type=F32), 0.0).astype(BF16)

        def down(vt, sign=1.0):
            return jnp.where(in_cols, sign * jnp.dot(tall, _bf(vt), preferred_element_type=F32), 0.0).astype(BF16)

        mb_ref[:, 0:ns] = across(bbr_ref[...])
        mb_ref[:, ns:2 * ns] = across(bbi_ref[...])
        mct_ref[:, 0:ns] = across(c2r_ref[...])
        mct_ref[:, ns:2 * ns] = across(c2i_ref[...], -1.0)
        mc_ref[0:ns, :] = down(ctr_ref[...])
        mc_ref[ns:2 * ns, :] = down(cti_ref[...], -1.0)

    return pl.pallas_call(
        body, out_shape=[S((4, 2, SUB, ns), F32), S((4, 2, SUB, ns), F32), S((C_WIDTH, 2 * ns), BF16),
                         S((2 * ns, C_WIDTH), BF16), S((C_WIDTH, 2 * ns), BF16)],
        compiler_params=pltpu.CompilerParams(vmem_limit_bytes=VMEM_LIMIT), name=name)(a, bbr, bbi, c2r, c2i, ctr, cti)


def s5_param_grads(name, d_mb, d_mc, da, lam_re, lam_im, log_dt, bt_re, bt_im):
    ns = N_STATE

    def body(dmb_ref, dmc_ref, da_ref, lr_ref, li_ref, ldt_ref, btr_ref, bti_ref,
             glr_ref, gli_ref, gdt_ref, gbr_ref, gbi_ref, gcr_ref, gci_ref):
        lr, li = lr_ref[...], li_ref[...]
        dt, ar, ai, den, qr, qi = _zoh(lr, li, ldt_ref[...])
        wide = _spread((C_STATE, ns), 1).astype(F32)
        tall = _spread((ns, C_STATE), 0).astype(F32)
        in_rows = _same_group((C_WIDTH, ns), C_GROUP_CH, C_STATE)
        in_cols = _same_group((ns, C_WIDTH), C_STATE, C_GROUP_CH)

        def fold_rows(v):
            return lax.dot_general(jnp.where(in_rows, v, 0.0), wide, (((1,), (1,)), ((), ())), precision=_HI,
                                   preferred_element_type=F32)

        def fold_cols(v):
            return lax.dot_general(jnp.where(in_cols, v, 0.0), tall, (((0,), (0,)), ((), ())), precision=_HI,
                                   preferred_element_type=F32)

        gcr_ref[...] = fold_cols(dmc_ref[0:ns, :])
        gci_ref[...] = -fold_cols(dmc_ref[ns:2 * ns, :])
        gbbr = fold_rows(dmb_ref[:, 0:ns])
        gbbi = fold_rows(dmb_ref[:, ns:2 * ns])
        btr, bti = btr_ref[...], bti_ref[...]
        q2r, q2i = _per_channel(qr), _per_channel(qi)
        gbr_ref[...] = q2r * gbbr + q2i * gbbi
        gbi_ref[...] = q2r * gbbi - q2i * gbbr

        def per_group(v):
            return jnp.sum(v.reshape(C_GROUPS, C_GROUP_CH, C_STATE), axis=1)

        gqr = per_group(btr * gbbr + bti * gbbi)
        gqi = per_group(btr * gbbi - bti * gbbr)
        ilr, ili = lr / den, li / den
        gar = da_ref[0] + ilr * gqr - ili * gqi
        gai = da_ref[1] + ilr * gqi + ili * gqr
        sr = (qr * lr + qi * li) / den
        si = (qi * lr - qr * li) / den
        gzr = ar * gar + ai * gai
        gzi = ar * gai - ai * gar
        glr_ref[...] = -sr * gqr - si * gqi + dt * gzr
        gli_ref[...] = -sr * gqi + si * gqr + dt * gzi
        gdt_ref[...] = jnp.sum(lr * gzr + li * gzi, axis=1, keepdims=True) * dt

    return pl.pallas_call(
        body, out_shape=[S(_GP, F32), S(_GP, F32), S((C_GROUPS, 1), F32), S(_RP, F32), S(_RP, F32), S(_RP, F32), S(_RP, F32)],
        compiler_params=pltpu.CompilerParams(vmem_limit_bytes=VMEM_LIMIT), name=name,
    )(d_mb, d_mc, da, lam_re, lam_im, log_dt, bt_re, bt_im)


def _cmul_add(xr, xi, pr, pi, zr, zi):
    return xr + pr * zr - pi * zi, xi + pr * zi + pi * zr


def s5_fwd(name, u, mb, mc, pw, dskip):
    t = u.shape[0]
    tm = _tile(t, S5_ROWS)
    ns = N_STATE

    def body(u_ref, mb_ref, mc_ref, pw_ref, d_ref, gy_ref, y_ref, xs_ref, xb_ref, carry):
        @pl.when(pl.program_id(0) == 0)
        def _():
            carry[...] = jnp.zeros(carry.shape, F32)

        uv = u_ref[...]
        xs_ref[...] = jnp.dot(_bf(uv), mb_ref[...], preferred_element_type=F32)

        def group(i, _):
            r0 = pl.multiple_of(i * SUB, SUB)
            xr = xs_ref[pl.ds(r0, SUB), 0:ns]
            xi = xs_ref[pl.ds(r0, SUB), ns:2 * ns]
            for k, s in enumerate((1, 2, 4)):
                xr, xi = _cmul_add(xr, xi, pw_ref[k, 0], pw_ref[k, 1], pltpu.roll(xr, s, 0), pltpu.roll(xi, s, 0))
            xr, xi = _cmul_add(xr, xi, pw_ref[3, 0], pw_ref[3, 1], carry[0], carry[1])
            xs_ref[pl.ds(r0, SUB), 0:ns] = xr
            xs_ref[pl.ds(r0, SUB), ns:2 * ns] = xi
            carry[0] = jnp.broadcast_to(xr[SUB - 1:SUB, :], (SUB, ns))
            carry[1] = jnp.broadcast_to(xi[SUB - 1:SUB, :], (SUB, ns))
            return 0
        lax.fori_loop(0, tm // SUB, group, 0)

        xb = _bf(xs_ref[...])
        xb_ref[...] = xb
        y = jnp.dot(xb, mc_ref[...], preferred_element_type=F32) + d_ref[...] * uv
        y_ref[...] = y
        gy_ref[...] = _gelu(y).astype(gy_ref.dtype)

    c = u.shape[1]
    return pl.pallas_call(
        body, grid=(t // tm,),
        in_specs=[pl.BlockSpec((tm, c), lambda i: (i, 0)), pl.BlockSpec(mb.shape, lambda i: (0, 0)),
                  pl.BlockSpec(mc.shape, lambda i: (0, 0)), pl.BlockSpec(pw.shape, lambda i: (0, 0, 0, 0)),
                  pl.BlockSpec((1, c), lambda i: (0, 0))],
        out_specs=[pl.BlockSpec((tm, c), lambda i: (i, 0)), pl.BlockSpec((tm, c), lambda i: (i, 0)),
                   pl.BlockSpec((tm, 2 * ns), lambda i: (i, 0)), pl.BlockSpec((tm, 2 * ns), lambda i: (i, 0))],
        out_shape=[S((t, c), BF16), S((t, c), F32), S((t, 2 * ns), F32), S((t, 2 * ns), BF16)],
        scratch_shapes=[pltpu.VMEM((2, SUB, ns), F32)],
        compiler_params=_cp("arbitrary"), name=name)(u, mb, mc, pw, dskip)


def s5_bwd(name, dgy, y, u, xs, mct, mbt, qw, dskip):
    t, c = u.shape
    tm = _tile(t, S5_ROWS)
    nt = t // tm
    ns = N_STATE
    ng = tm // SUB

    def body(dgy_ref, y_ref, u_ref, xs_ref, mct_ref, mbt_ref, qw_ref, d_ref,
             du_ref, dy_ref, lb_ref, da_ref, dd_ref, lam, carry):
        @pl.when(pl.program_id(0) == 0)
        def _():
            carry[...] = jnp.zeros(carry.shape, F32)
            da_ref[...] = jnp.zeros(da_ref.shape, F32)
            dd_ref[...] = jnp.zeros(dd_ref.shape, F32)

        uv = u_ref[...]
        dy = dgy_ref[...] * _gelu_grad(y_ref[...])
        dyb = _bf(dy)
        dy_ref[...] = dyb
        dd_ref[...] += jnp.sum(dy * uv, axis=0, keepdims=True)
        lam[...] = jnp.dot(dyb, mct_ref[...], preferred_element_type=F32)
        last_row = lax.broadcasted_iota(jnp.int32, (SUB, ns), 0) == SUB - 1

        def group(j, _):
            i = ng - 1 - j
            r0 = pl.multiple_of(i * SUB, SUB)
            lr = lam[pl.ds(r0, SUB), 0:ns]
            li = lam[pl.ds(r0, SUB), ns:2 * ns]
            for k, s in enumerate((1, 2, 4)):
                lr, li = _cmul_add(lr, li, qw_ref[k, 0], qw_ref[k, 1],
                                   pltpu.roll(lr, SUB - s, 0), pltpu.roll(li, SUB - s, 0))
            cr, ci = carry[0], carry[1]
            lr, li = _cmul_add(lr, li, qw_ref[3, 0], qw_ref[3, 1], cr, ci)
            lam[pl.ds(r0, SUB), 0:ns] = lr
            lam[pl.ds(r0, SUB), ns:2 * ns] = li
            carry[0] = jnp.broadcast_to(lr[0:1, :], (SUB, ns))
            carry[1] = jnp.broadcast_to(li[0:1, :], (SUB, ns))
            nr = jnp.where(last_row, cr, pltpu.roll(lr, SUB - 1, 0))
            ni = jnp.where(last_row, ci, pltpu.roll(li, SUB - 1, 0))
            xr = xs_ref[pl.ds(r0, SUB), 0:ns]
            xi = xs_ref[pl.ds(r0, SUB), ns:2 * ns]
            da_ref[0] += nr * xr + ni * xi
            da_ref[1] += ni * xr - nr * xi
            return 0
        lax.fori_loop(0, ng, group, 0)

        lb = _bf(lam[...])
        lb_ref[...] = lb
        du_ref[...] = (jnp.dot(lb, mbt_ref[...], preferred_element_type=F32) + d_ref[...] * dy).astype(du_ref.dtype)

    rev = lambda i: (nt - 1 - i, 0)
    return pl.pallas_call(
        body, grid=(nt,),
        in_specs=[pl.BlockSpec((tm, c), rev), pl.BlockSpec((tm, c), rev), pl.BlockSpec((tm, c), rev),
                  pl.BlockSpec((tm, 2 * ns), rev),
                  pl.BlockSpec(mct.shape, lambda i: (0, 0)), pl.BlockSpec(mbt.shape, lambda i: (0, 0)),
                  pl.BlockSpec(qw.shape, lambda i: (0, 0, 0, 0)), pl.BlockSpec((1, c), lambda i: (0, 0))],
        out_specs=[pl.BlockSpec((tm, c), rev), pl.BlockSpec((tm, c), rev), pl.BlockSpec((tm, 2 * ns), rev),
                   pl.BlockSpec((2, SUB, ns), lambda i: (0, 0, 0)), pl.BlockSpec((1, c), lambda i: (0, 0))],
        out_shape=[S((t, c), BF16), S((t, c), BF16), S((t, 2 * ns), BF16), S((2, SUB, ns), F32), S((1, c), F32)],
        scratch_shapes=[pltpu.VMEM((tm, 2 * ns), F32), pltpu.VMEM((2, SUB, ns), F32)],
        compiler_params=_cp("arbitrary"), name=name)(dgy, y, u, xs, mct, mbt, qw, dskip)


def _first(accs, *_):
    return [accs[0]]


def _rms_bwd_epi(accs, xv, base, rv, g):
    dv = accs[0]
    w = dv * g
    xh = xv * rv
    dx = base + rv * (w - xh * jnp.mean(w * xh, axis=-1, keepdims=True))
    return [dx, dx, jnp.sum(dv * xh, axis=0, keepdims=True)]


def mm_rms_bwd(name, pairs, x, r, gain, dres):
    t, d = x.shape
    return mm_nn(name, t, d, pairs, 1, _rms_bwd_epi, [F32, BF16], tiled=[x, dres], cols=[r], rowv=[gain], sums=[(1, d)])


def _add_res(accs, res):
    return [accs[0] + res]


def even_fwd(x, w, need_out):
    t = x.shape[0]
    proj, hn, r = mm_nn("e_in_f", t, IN_WIDTH, [(x, w["e_w_in_t"], 0, "t")], 1, _first, [F32], norm_gain=w["e_norm"])
    out_a = gmlp_fwd("e_gmlp_f", proj, w["e_gmlp_w"], w["e_gmlp_b"])
    hc = conv_fwd("e_conv_f", proj, w["e_conv_w"], w["e_conv_b"])
    out_b = ln_silu_fwd("e_ln_f", hc, w["e_conv_ln_g"], w["e_conv_ln_b"])
    need_out(out_b)
    (x1,) = mm_nn("e_out_f", t, D_MODEL, [(out_a, (w["e_w_out"], 0), 0), (out_b, (w["e_w_out"], 1), 0)],
                  1, _add_res, [F32], tiled=[x])
    return x1, (x, hn, r, proj, out_a, hc, out_b)


def even_bwd_mixers(dxb, saved, w):
    x, hn, r, proj, out_a, hc, out_b = saved
    t = x.shape[0]
    (dcat,) = mm_nn("e_out_b", t, D_MODEL, [(dxb, w["e_w_out"], 0, "t")], 1, _first, [F32])
    g_w_out = jnp.concatenate([mm_tn("e_out_wa", out_a, dxb), mm_tn("e_out_wb", out_b, dxb)], axis=0)
    dab, g_gw, g_gb = gmlp_bwd("e_gmlp_b", proj, dcat, w["e_gmlp_w"], w["e_gmlp_b"])
    dhc, g_lg, g_lb = ln_silu_bwd("e_ln_b", hc, dcat, w["e_conv_ln_g"], w["e_conv_ln_b"])
    dba, dbg, g_cw, g_cb = conv_bwd("e_conv_b", proj, dhc, w["e_conv_w"])
    g_w_in_t = jnp.concatenate([mm_tn("e_in_w0", dab, hn), mm_tn("e_in_w1", dba, hn), mm_tn("e_in_w2", dbg, hn)], axis=0)
    grads = dict(e_w_in_t=g_w_in_t, e_gmlp_w=g_gw[None], e_gmlp_b=g_gb.reshape(1, A_GROUPS, GMLP_BLOCK),
                 e_conv_w=g_cw[None], e_conv_b=g_cb, e_conv_ln_g=g_lg, e_conv_ln_b=g_lb, e_w_out=g_w_out)
    return (dab, dba, dbg), grads


def even_bwd_input(dx, dproj, saved, w):
    x, _, r = saved[:3]
    dab, dba, dbg = dproj
    w_in_t = w["e_w_in_t"]
    return mm_rms_bwd("e_in_b", [(dab, (w_in_t, 0), 0), (dba, (w_in_t, 2), 0), (dbg, (w_in_t, 3), 0)], x, r, w["e_norm"], dx)


def s5_setup(w, anchor=None):
    def rows(v):
        return v.transpose(0, 2, 1).reshape(_RP)

    log_dt = w["o_log_dt"].reshape(C_GROUPS, 1)
    if anchor is not None:
        log_dt = log_dt + anchor
    lam = (w["o_lam_re"], w["o_lam_im"], log_dt, rows(w["o_b_re"]), rows(w["o_b_im"]))
    a, bbr, bbi = s5_discretise("o_s5_zoh", *lam)
    c_re, c_im = w["o_c_re"], w["o_c_im"]
    pw, qw, mb, mc, mct = s5_operands("o_s5_ops", a.reshape(2, N_STATE), bbr, bbi, c_re.reshape(_RP), c_im.reshape(_RP),
                                      c_re.transpose(2, 0, 1).reshape(C_STATE, C_WIDTH),
                                      c_im.transpose(2, 0, 1).reshape(C_STATE, C_WIDTH))
    return dict(lam=lam, pw=pw, qw=qw, mb=mb, mc=mc, mct=mct, mbt=mb.T)


def odd_fwd(x, w, consts):
    t = x.shape[0]
    u, hn, r = mm_nn("o_in_f", t, C_WIDTH, [(x, w["o_w_in"], 0)], 1, _first, [F32], norm_gain=w["o_norm"])
    gy, y, xs, xsb = s5_fwd("o_s5_f", u, consts["mb"], consts["mc"], consts["pw"], w["o_d"])
    w_out_t = w["o_w_out_t"]

    def epi(accs, res):
        return [res + accs[0] * _sigmoid(accs[1]), accs[0], accs[1]]

    x1, o1, o2 = mm_nn("o_out_f", t, D_MODEL, [(gy, (w_out_t, 0), 0, "t"), (gy, (w_out_t, D_MODEL), 1, "t")], 2, epi,
                       [F32, BF16, BF16], tiled=[x])
    return x1, (x, hn, r, u, gy, y, xs, xsb, o1, o2)


def odd_bwd(dx, dxb, saved, w, consts):
    x, hn, r, u, gy, y, xs, xsb, o1, o2 = saved
    t = x.shape[0]

    def gate_bwd(dv, a, b):
        a = a.astype(F32)
        sg = _sigmoid(b.astype(F32))
        return [jnp.concatenate([dv * sg, dv * a * sg * (1.0 - sg)], axis=1)], []

    (do12,) = rows_call("o_gate_b", gate_bwd, [dx, o1, o2], [], [(2 * D_MODEL, BF16)], [])
    (dgy,) = mm_nn("o_out_b", t, C_WIDTH, [(do12, w["o_w_out_t"], 0)], 1, _first, [F32])
    g_w_out_t = mm_tn("o_out_w", do12, gy)
    du, dyb, lamb, da8, g_d = s5_bwd("o_s5_b", dgy, y, u, xs, consts["mct"], consts["mbt"], consts["qw"], w["o_d"])
    d_mb = mm_tn("o_s5_wb", u, lamb, out_dtype=F32)
    d_mc = mm_tn("o_s5_wc", xsb, dyb, out_dtype=F32)
    da = jnp.sum(da8, axis=1).reshape((2,) + _GP)
    g_lr, g_li, g_dt, g_btr, g_bti, g_cr, g_ci = s5_param_grads("o_s5_pg", d_mb, d_mc, da, *consts["lam"])

    def states_first(v):
        return v.reshape(C_GROUPS, C_GROUP_CH, C_STATE).transpose(0, 2, 1)[None]

    g_w_in = mm_tn("o_in_w", hn, du)
    dx0, dx0b, g_norm = mm_rms_bwd("o_in_b", [(du, w["o_w_in"], 0, "t")], x, r, w["o_norm"], dx)
    grads = dict(o_norm=g_norm, o_w_in=g_w_in, o_lam_re=g_lr[None], o_lam_im=g_li[None], o_log_dt=g_dt.reshape(1, C_GROUPS),
                 o_b_re=states_first(g_btr), o_b_im=states_first(g_bti),
                 o_c_re=g_cr.reshape((1, C_GROUPS, C_GROUP_CH, C_STATE)), o_c_im=g_ci.reshape((1, C_GROUPS, C_GROUP_CH, C_STATE)),
                 o_d=g_d, o_w_out_t=g_w_out_t)
    return dx0, dx0b, grads


def ca_fwd(i, x, mem, w):
    t, m = x.shape[0], mem.shape[0]
    q, xn, r = mm_nn(f"ca{i}_q_f", t, D_MODEL, [(x, w["ca_wq"][i], 0)], 1, _first, [BF16], norm_gain=w["ca_norm"][i:i + 1])
    k, v, mn, rm = mm_nn(f"ca{i}_kv_f", m, D_MODEL, [(mem, w["ca_wk"][i], 0), (mem, w["ca_wv"][i], 1)], 2,
                         lambda accs: [accs[0], accs[1]], [BF16, BF16], norm_gain=w["ca_mem_norm"][i:i + 1])
    o = attn_fwd(f"ca{i}_attn_f", q, k, v)
    (x1,) = mm_nn(f"ca{i}_o_f", t, D_MODEL, [(o, w["ca_wo"][i], 0)], 1, _add_res, [F32], tiled=[x])
    return x1, (x, xn, r, mn, rm, q, k, v, o)


def ca_bwd(i, dx, dxb, saved, mem, w):
    x, xn, r, mn, rm, q, k, v, o = saved
    t, m = x.shape[0], mem.shape[0]
    (do,) = mm_nn(f"ca{i}_o_b", t, D_MODEL, [(dxb, w["ca_wo"][i], 0, "t")], 1, _first, [BF16])
    g_wo = mm_tn(f"ca{i}_o_w", o, dxb)
    dq, dk, dv = attn_bwd(f"ca{i}_attn_b", q, k, v, do)
    g_wq = mm_tn(f"ca{i}_q_w", xn, dq)
    g_wk = mm_tn(f"ca{i}_k_w", mn, dk)
    g_wv = mm_tn(f"ca{i}_v_w", mn, dv)
    (dmn,) = mm_nn(f"ca{i}_kv_b", m, D_MODEL, [(dk, w["ca_wk"][i], 0, "t"), (dv, w["ca_wv"][i], 0, "t")], 1, _first, [F32])
    g_mnorm = rms_bwd_gain_only(f"ca{i}_mnorm_b", dmn, mem, rm)
    dx0, dx0b, g_norm = mm_rms_bwd(f"ca{i}_q_b", [(dq, w["ca_wq"][i], 0, "t")], x, r, w["ca_norm"][i:i + 1], dx)
    return dx0, dx0b, dict(ca_norm=g_norm, ca_mem_norm=g_mnorm, ca_wq=g_wq, ca_wk=g_wk, ca_wv=g_wv, ca_wo=g_wo)


def ffn_fwd(i, x, w):
    t = x.shape[0]
    def epi(accs):
        g, u = accs
        return [g, u, g * _sigmoid(g) * u]

    g, u, h, xn, r = mm_nn(f"ffn{i}_up_f", t, FFN_HIDDEN, [(x, w["ffn_w_gate_t"][i], 0, "t"), (x, w["ffn_w_up_t"][i], 1, "t")],
                           2, epi, [BF16, BF16, BF16], norm_gain=w["ffn_norm"][i:i + 1])
    (x1,) = mm_nn(f"ffn{i}_down_f", t, D_MODEL, [(h, w["ffn_w_down"][i], 0)], 1, _add_res, [F32], tiled=[x])
    return x1, (x, xn, r, g, u, h)


def ffn_bwd(i, dx, dxb, saved, w):
    x, xn, r, g, u, h = saved
    t = x.shape[0]

    def epi(accs, gv, uv):
        dh = accs[0]
        gv = gv.astype(F32)
        uv = uv.astype(F32)
        s = _sigmoid(gv)
        return [dh * uv * s * (1.0 + gv * (1.0 - s)), dh * gv * s]

    dg, du = mm_nn(f"ffn{i}_down_b", t, FFN_HIDDEN, [(dxb, w["ffn_w_down"][i], 0, "t")], 1, epi, [BF16, BF16], tiled=[g, u])
    g_wd = mm_tn(f"ffn{i}_down_w", h, dxb)
    g_wg_t = mm_tn(f"ffn{i}_gate_w", dg, xn)
    g_wu_t = mm_tn(f"ffn{i}_up_w", du, xn)
    dx0, dx0b, g_norm = mm_rms_bwd(f"ffn{i}_up_b", [(dg, w["ffn_w_gate_t"][i], 0), (du, w["ffn_w_up_t"][i], 0)], x, r,
                                   w["ffn_norm"][i:i + 1], dx)
    return dx0, dx0b, dict(ffn_norm=g_norm, ffn_w_gate_t=g_wg_t, ffn_w_up_t=g_wu_t, ffn_w_down=g_wd)


def local_step(x, mem, target, w, fetch=None, on_grads=None, anchor=None):
    consts = s5_setup(w, anchor)

    def need(stage, after):
        if fetch is not None:
            for k, v in fetch(stage, after).items():
                if isinstance(k, tuple):
                    w.setdefault(k[0], {})[k[1]] = v
                else:
                    w[k] = v

    need(0, consts["pw"])
    x1, s_e = even_fwd(x, w, lambda after: need(1, after))
    x2, s_c0 = ca_fwd(0, x1, mem, w)
    need(2, x2)
    x3, s_f0 = ffn_fwd(0, x2, w)
    x4, s_o = odd_fwd(x3, w, consts)
    need(3, x4)
    x5, s_c1 = ca_fwd(1, x4, mem, w)
    x6, s_f1 = ffn_fwd(1, x5, w)
    dx, dxb, g_final, loss = final_loss("final_loss", x6, w["final_norm"], target)

    def emit(stage, carry, plain, layered=None, layer=0):
        if on_grads is None:
            return carry
        out = dict(plain)
        out.update({(k, layer): v for k, v in (layered or {}).items()})
        return on_grads(stage, out, list(carry))

    dx, dxb, g_f1 = ffn_bwd(1, dx, dxb, s_f1, w)
    dx, dxb = emit(0, (dx, dxb), {}, g_f1, 1)
    dx, dxb, g_c1 = ca_bwd(1, dx, dxb, s_c1, mem, w)
    dx, dxb, g_o = odd_bwd(dx, dxb, s_o, w, consts)
    dx, dxb = emit(1, (dx, dxb), g_o, g_c1, 1)
    dx, dxb, g_f0 = ffn_bwd(0, dx, dxb, s_f0, w)
    dx, dxb = emit(2, (dx, dxb), {}, g_f0, 0)
    dx, dxb, g_c0 = ca_bwd(0, dx, dxb, s_c0, mem, w)
    dx, dxb = emit(3, (dx, dxb), {}, g_c0, 0)
    dproj, g_e = even_bwd_mixers(dxb, s_e, w)
    dproj = emit(4, dproj, {**g_e, "o_norm": g_o["o_norm"], "o_d": g_o["o_d"]})
    dx, dxb, g_e["e_norm"] = even_bwd_input(dx, dproj, s_e, w)

    grads = dict(g_e)
    grads.update(g_o)
    for g0, g1 in ((g_c0, g_c1), (g_f0, g_f1)):
        for k in g0:
            grads[k] = jnp.concatenate([g0[k], g1[k]], axis=0) if k.endswith("norm") else (g0[k], g1[k])
    grads["final_norm"] = g_final
    return loss, dx, grads


def _group(axes):
    pos = {a: lax.axis_index(a) for a in ("x", "y", "c")}
    me = 0
    for a in axes:
        me = me * 2 + pos[a]
    peers = []
    for mask in range(1, 2 ** len(axes)):
        peer = dict(pos)
        for bit, a in enumerate(axes):
            if (mask >> (len(axes) - 1 - bit)) & 1:
                peer[a] = 1 - pos[a]
        idx = 0
        for a in axes:
            idx = idx * 2 + peer[a]
        peers.append((idx, (peer["x"], peer["y"], peer["c"])))
    return me, peers


def _sibling():
    x, y, c = lax.axis_index("x"), lax.axis_index("y"), lax.axis_index("c")
    return c, (x, y, 1 - c)


_HBM =pl.BlockSpec(memory_space=pltpu.HBM)
_SEM = pl.BlockSpec(memory_space=pltpu.SEMAPHORE)
_EFFECT = pltpu.SideEffectType.DATAFLOW_SIDE_EFFECTING


def _gather_peers(direct):
    chip, _ = _group(("x", "y"))
    core = lax.axis_index("c")
    if direct:
        _, peers = _group(_ALL)
        return chip, core, [(idx // 2, idx % 2, dev) for idx, dev in peers]
    _, peers = _group(("x", "y"))
    return chip, core, [(idx, core, dev) for idx, dev in peers]


def gather_ici_start(name, groups, direct):
    flat = [b for g in groups for b in g]
    sizes = [len(g) for g in groups]
    k_ops, n_g = len(flat), len(groups)
    lands = [lax.empty((4, 2) + tuple(b.shape), b.dtype) for b in flat]
    fan = [N_DEV - 1 if d else 3 for d in direct]

    def body(*refs):
        src, land = refs[:k_ops], refs[k_ops:2 * k_ops]
        sems = refs[2 * k_ops:2 * k_ops + 3 * n_g]
        token = refs[-1]
        i = 0
        for g in range(n_g):
            send, recv, loc = sems[3 * g:3 * g + 3]
            chip, core, peers = _gather_peers(direct[g])
            for j in range(sizes[g]):
                pltpu.make_async_copy(src[i], land[i].at[chip, core], loc.at[j]).start()
                for k, (_, _, dev) in enumerate(peers):
                    s = fan[g] * j + k
                    pltpu.make_async_remote_copy(src_ref=src[i], dst_ref=land[i].at[chip, core], send_sem=send.at[s],
                                                 recv_sem=recv.at[s], device_id=dev, device_id_type=MESH).start()
                i += 1
        token[...] = jnp.zeros(token.shape, token.dtype)

    sem_shapes = []
    for s, f in zip(sizes, fan):
        sem_shapes += [pltpu.SemaphoreType.DMA((f * s,)), pltpu.SemaphoreType.DMA((f * s,)), pltpu.SemaphoreType.DMA((s,))]
    thru = [pltpu.HBM(a.shape, a.dtype) for a in flat + lands]
    outs = pl.pallas_call(
        body, name=name, out_shape=tuple(sem_shapes) + tuple(thru) + (S((8, LANES), F32),),
        in_specs=[_HBM] * (2 * k_ops), out_specs=[_SEM] * (3 * n_g) + [_HBM] * (2 * k_ops) + [pl.BlockSpec(memory_space=pltpu.VMEM)],
        input_output_aliases={i: 3 * n_g + i for i in range(2 * k_ops)},
        compiler_params=pltpu.CompilerParams(has_side_effects=_EFFECT),
    )(*[pltpu.with_memory_space_constraint(a, pltpu.HBM) for a in flat + lands])
    sems = [tuple(outs[3 * g:3 * g + 3]) for g in range(n_g)]
    srcs_thru, lands_thru, off = [], [], 3 * n_g
    for s in sizes:
        srcs_thru.append(list(outs[off:off + s]))
        off += s
    for s in sizes:
        lands_thru.append(list(outs[off:off + s]))
        off += s
    return sems, srcs_thru, lands_thru, outs[-1]


def gather_ici_wait(name, srcs, lands, sems, after, direct=False):
    n = len(srcs)

    def body(*refs):
        src, land = refs[:n], refs[n:2 * n]
        send, recv, loc = refs[2 * n:2 * n + 3]
        chip, core, peers = _gather_peers(direct)
        for j in range(n):
            for k, (pchip, pcore, dev) in enumerate(peers):
                s = len(peers) * j + k
                cp = pltpu.make_async_remote_copy(src_ref=src[j], dst_ref=land[j].at[pchip, pcore], send_sem=send.at[s],
                                                  recv_sem=recv.at[s], device_id=dev, device_id_type=MESH)
                cp.wait_send()
                cp.wait_recv()
            pltpu.make_async_copy(src[j], land[j].at[chip, core], loc.at[j]).wait()

    outs = pl.pallas_call(
        body, name=name, out_shape=tuple(pltpu.HBM(a.shape, a.dtype) for a in list(srcs) + list(lands)),
        in_specs=[_HBM] * (2 * n) + [_SEM] * 3 + [ANY], out_specs=[_HBM] * (2 * n),
        input_output_aliases={i: i for i in range(2 * n)},
        compiler_params=pltpu.CompilerParams(has_side_effects=_EFFECT),
    )(*srcs, *lands, *sems, after)
    return list(outs[n:])


def gather_d2d(name, bufs):
    k_ops = len(bufs)

    def body(*refs):
        in_refs, out_refs = refs[:k_ops], refs[k_ops:2 * k_ops]
        send_sems, recv_sems = refs[2 * k_ops:]
        core, sib = _sibling()
        sent, landed = [], []
        for i in range(k_ops):
            cp = pltpu.make_async_remote_copy(src_ref=in_refs[i].at[:, core], dst_ref=out_refs[i].at[:, core],
                                              send_sem=send_sems.at[i], recv_sem=recv_sems.at[i], device_id=sib, device_id_type=MESH)
            cp.start()
            sent.append(cp)
            landed.append(pltpu.make_async_remote_copy(src_ref=in_refs[i].at[:, core], dst_ref=out_refs[i].at[:, 1 - core],
                                                       send_sem=send_sems.at[i], recv_sem=recv_sems.at[i],
                                                       device_id=sib, device_id_type=MESH))
        for cp in landed:
            cp.wait_recv()
        for cp in sent:
            cp.wait_send()

    return pl.pallas_call(
        body, in_specs=[ANY] * k_ops, out_specs=[ANY] * k_ops, out_shape=[S(b.shape, b.dtype) for b in bufs],
        input_output_aliases={i: i for i in range(k_ops)},
        scratch_shapes=[pltpu.SemaphoreType.DMA((k_ops,)), pltpu.SemaphoreType.DMA((k_ops,))],
        name=name)(*bufs)


_ALL = ("x", "y", "c")


def scatter_start(name, arr, carry):
    land = lax.empty(arr.shape, arr.dtype)
    n_c = len(carry)

    def body(*refs):
        in_ref, land_ref = refs[0], refs[1]
        send, recv = refs[2 + n_c], refs[3 + n_c]
        me, peers = _group(_ALL)
        for k, (idx, dev) in enumerate(peers):
            pltpu.make_async_remote_copy(src_ref=in_ref.at[idx], dst_ref=land_ref.at[me], send_sem=send.at[k], recv_sem=recv.at[k],
                                         device_id=dev, device_id_type=MESH).start()

    thru = [arr, land] + list(carry)
    outs = pl.pallas_call(
        body, name=name,
        out_shape=(pltpu.SemaphoreType.DMA((N_DEV - 1,)), pltpu.SemaphoreType.DMA((N_DEV - 1,)))
        + tuple(pltpu.HBM(a.shape, a.dtype) for a in thru),
        in_specs=[_HBM] * len(thru), out_specs=[_SEM, _SEM] + [_HBM] * len(thru),
        input_output_aliases={i: 2 + i for i in range(len(thru))},
        compiler_params=pltpu.CompilerParams(has_side_effects=_EFFECT),
    )(*[pltpu.with_memory_space_constraint(a, pltpu.HBM) for a in thru])
    return (outs[0], outs[1]), outs[2], outs[3], list(outs[4:])


def scatter_wait(name, arr, land, sems, after):
    def body(in_ref, land_ref, send, recv, after_ref, in_thru, land_thru):
        _, peers = _group(_ALL)
        for k, (idx, dev) in enumerate(peers):
            cp = pltpu.make_async_remote_copy(src_ref=in_ref.at[idx], dst_ref=land_ref.at[idx], send_sem=send.at[k],
                                              recv_sem=recv.at[k], device_id=dev, device_id_type=MESH)
            cp.wait_send()
            cp.wait_recv()

    outs = pl.pallas_call(
        body, name=name, out_shape=(pltpu.HBM(arr.shape, arr.dtype), pltpu.HBM(arr.shape, arr.dtype)),
        in_specs=[_HBM, _HBM, _SEM, _SEM, ANY], out_specs=[_HBM, _HBM], input_output_aliases={0: 0, 1: 1},
        compiler_params=pltpu.CompilerParams(has_side_effects=_EFFECT),
    )(arr, land, sems[0], sems[1], after)
    return outs[0], outs[1]


def _row_tile(rows, cap=512):
    return next(t for t in range(cap - cap % 16, 0, -16) if rows % t == 0)


def sum_shares(name, own, recv, me):
    n, rows, c = recv.shape
    tr = _row_tile(rows)

    def body(me_ref, *refs):
        acc = refs[0][...].astype(F32)
        for r in refs[1:n]:
            acc = acc + r[...].astype(F32)
        refs[n][...] = acc

    def slot(mask):
        return pl.BlockSpec((None, tr, c), lambda i, me, mask=mask: (jnp.bitwise_xor(me[0], mask), i, 0))

    spec = pltpu.PrefetchScalarGridSpec(
        num_scalar_prefetch=1, grid=(rows // tr,), in_specs=[slot(k) for k in range(n)],
        out_specs=pl.BlockSpec((tr, c), lambda i, me: (i, 0)))
    return pl.pallas_call(body, grid_spec=spec, out_shape=S((rows, c), F32),
                          compiler_params=_cp("parallel"), name=name)(me, own, *([recv] * (n - 1)))


def sum_slots(name, slots):
    n, r, c = slots.shape

    def body(s_ref, o_ref):
        acc = s_ref[0]
        for j in range(1, n):
            acc = acc + s_ref[j]
        o_ref[...] = acc

    return pl.pallas_call(body, out_shape=S((r, c), F32), compiler_params=pltpu.CompilerParams(vmem_limit_bytes=VMEM_LIMIT),
                          name=name)(slots)


def adamw_native(name, g, w, m, v, tr=512):
    shape = w.shape
    cols = shape[-1]
    rows = w.size // cols
    tr = _tile(rows, tr) if rows % 8 == 0 else rows
    c1 = 1.0 - ADAM_B1 ** ADAM_STEP
    c2 = 1.0 - ADAM_B2 ** ADAM_STEP

    def body(g_ref, w_ref, m_ref, v_ref, d_ref, m2_ref, v2_ref):
        gv = g_ref[...]
        m2 = ADAM_B1 * m_ref[...] + (1.0 - ADAM_B1) * gv
        v2 = ADAM_B2 * v_ref[...] + (1.0 - ADAM_B2) * (gv * gv)
        m2_ref[...] = m2
        v2_ref[...] = v2
        d_ref[...] = -ADAM_LR * ((m2 / c1) / (jnp.sqrt(v2 / c2) + ADAM_EPS) + ADAM_WD * w_ref[...])

    row = pl.BlockSpec((tr, cols), lambda i: (i, 0))
    outs = pl.pallas_call(body, grid=(rows // tr,), in_specs=[row] * 4, out_specs=[row] * 3,
                          out_shape=[S((rows, cols), F32)] * 3, compiler_params=_cp("parallel"),
                          name=name)(*[a.reshape(rows, cols) for a in (g, w, m, v)])
    return tuple(o.reshape(shape) for o in outs)


_REPLICATED = ("e_norm", "e_gmlp_w", "e_gmlp_b", "e_conv_b", "e_conv_ln_g", "e_conv_ln_b", "o_lam_re", "o_lam_im", "o_log_dt",
               "o_b_re", "o_b_im", "o_c_re", "o_c_im", "ca_norm", "ca_mem_norm", "ffn_norm", "final_norm")
_ORDER = ("e_norm", "e_w_in", "e_gmlp_w", "e_gmlp_b", "e_conv_w", "e_conv_b", "e_conv_ln_g", "e_conv_ln_b", "e_w_out",
          "o_norm", "o_w_in", "o_lam_re", "o_lam_im", "o_log_dt", "o_b_re", "o_b_im", "o_c_re", "o_c_im", "o_d", "o_w_out",
          "ca_norm", "ca_mem_norm", "ca_wq", "ca_wk", "ca_wv", "ca_wo", "ffn_norm", "ffn_w_gate", "ffn_w_up", "ffn_w_down",
          "final_norm")


def _rows128(a, multiple=8):
    flat = a.reshape(-1)
    rows = -(-flat.shape[0] // (LANES * multiple)) * multiple
    return jnp.pad(flat, (0, rows * LANES - flat.shape[0])).reshape(rows, LANES)


def _shard(full, axis):
    s = full.shape
    return jnp.moveaxis(full.reshape(s[:axis] + (N_DEV, s[axis] // N_DEV) + s[axis + 1:]), axis, 0)


_UNITS = (("e_w_in", 0, True), ("e_w_out", 0, False), ("o_w_in", 0, False), ("o_w_out", 0, True),
          *[(n, i, False) for n in ("ca_wq", "ca_wk", "ca_wv", "ca_wo") for i in (0, 1)],
          *[(n, i, tr) for n, tr in (("ffn_w_gate", True), ("ffn_w_up", True), ("ffn_w_down", False)) for i in (0, 1)])
_LAYERED = ("ca_wq", "ca_wk", "ca_wv", "ca_wo", "ffn_w_gate", "ffn_w_up", "ffn_w_down")
_SMALL_SHARDED = (("e_conv_w", 2), ("o_norm", 1), ("o_d", 1))
RS_ROW = 1024


def _unit_key(name, tr):
    return name + "_t" if tr else name


def _stage_of(name, layer):
    if name.startswith("e_"):
        return 0 if name == "e_w_in" else 1
    if name.startswith("o_"):
        return 2
    if name.startswith("ca_"):
        return 1 if layer == 0 else 3
    return 2 if layer == 0 else 3


def weight_fetcher(local):
    groups, meta = [[] for _ in range(4)], [[] for _ in range(4)]
    for name, layer, tr in _UNITS:
        blk = local[name][layer]
        st = _stage_of(name, layer)
        groups[st].append(_bf(blk.T if tr else blk))
        meta[st].append((name, layer, tr))
    small = jnp.concatenate([local[name].reshape(-1) for name, _ in _SMALL_SHARDED])
    groups[0].append(_rows128(small))
    direct = [False, False, False, True]
    sems, srcs, lands, token = gather_ici_start("ag_w_start", groups, direct)

    def fetch(stage, after):
        bufs = gather_ici_wait(f"ag_w_wait{stage}", srcs[stage], lands[stage], sems[stage], after, direct[stage])
        if not direct[stage]:
            bufs = gather_d2d(f"ag_w_d2d{stage}", bufs)
        got = {}
        for (name, layer, tr), blk, buf in zip(meta[stage], groups[stage], bufs):
            arr = buf.reshape((N_DEV * blk.shape[0],) + tuple(blk.shape[1:]))
            if name in _LAYERED:
                got[(_unit_key(name, tr), layer)] = arr
            else:
                got[_unit_key(name, tr)] = arr
        if stage == 0:
            flat = bufs[-1].reshape(N_DEV, -1)
            off = 0
            for name, axis in _SMALL_SHARDED:
                blk = local[name]
                seg = flat[:, off:off + blk.size].reshape((N_DEV,) + blk.shape)
                off += blk.size
                seg = jnp.moveaxis(seg, 0, axis)
                got[name] = seg.reshape(seg.shape[:axis] + (-1,) + seg.shape[axis + 2:])
            got["e_conv_w"] = got["e_conv_w"][0]
        return got

    return fetch, token


def _grad_stage_of(name, layer):
    if name.startswith("e_"):
        return 4
    if name.startswith("o_"):
        return 1
    if name.startswith("ca_"):
        return 3 if layer == 0 else 1
    return 2 if layer == 0 else 0


GRAD_STAGES = 5
SMALL_ROWS = 16


def gradient_reducer(local, mom, var):
    me = (4 * lax.axis_index("x") + 2 * lax.axis_index("y") + lax.axis_index("c")).astype(jnp.int32).reshape(1)
    pending = []

    def start(stage, grads, carry):
        units = [u for u in _UNITS if _grad_stage_of(u[0], u[1]) == stage]
        parts, spans = [], []
        for name, layer, tr in units:
            key = _unit_key(name, tr)
            g = grads[(key, layer)] if name in _LAYERED else grads[key]
            part = g.reshape(4, 2, -1, RS_ROW)
            spans.append((part.shape[2], g.shape[0] // N_DEV, g.shape[1]))
            parts.append(part)
        if stage == GRAD_STAGES - 1:
            small = jnp.concatenate([_shard(grads[name], axis).reshape(N_DEV, -1) for name, axis in _SMALL_SHARDED], axis=1)
            small = jnp.pad(small, ((0, 0), (0, SMALL_ROWS * RS_ROW - small.shape[1])))
            parts.append(small.astype(BF16).reshape(4, 2, SMALL_ROWS, RS_ROW))
        pack = jnp.concatenate(parts, axis=2)
        pack = pack.reshape((N_DEV,) + pack.shape[2:])
        sems, own, land, carry = scatter_start(f"rs_start{stage}", pack, carry)
        pending.append((stage, units, spans, sems, own, land))
        return carry

    def finish(after):
        res, per_layer, small_flat = {}, {}, None
        for stage, units, spans, sems, own, land in pending:
            own, land = scatter_wait(f"rs_wait{stage}", own, land, sems, after)
            total = sum_shares(f"rs_sum{stage}", own, land, me)
            off = 0
            for (name, layer, tr), (rows, r, c) in zip(units, spans):
                g = total[off:off + rows].reshape(r, c)
                off += rows
                per_layer.setdefault(name, {})[layer] = g.T if tr else g
            if stage == GRAD_STAGES - 1:
                small_flat = total[off:off + SMALL_ROWS].reshape(-1)
        for name, by_layer in per_layer.items():
            g = jnp.stack([by_layer[i] for i in sorted(by_layer)]) if name in _LAYERED else by_layer[0][None]
            res[name] = (g,) + adamw_native("adamw_" + name, g, local[name], mom[name], var[name])
        off = 0
        for name, _ in _SMALL_SHARDED:
            blk = local[name]
            g = small_flat[off:off + blk.size].reshape(blk.shape)
            off += blk.size
            res[name] = (g,) + adamw_native("adamw_" + name, g, blk, mom[name], var[name])
        return res

    return start, finish


def replicated_start(grads, loss):
    pack = jnp.concatenate([_rows128(grads[name]) for name in _REPLICATED] + [_rows128(loss)], axis=0)
    sems, srcs, lands, token = gather_ici_start("ag_g_start", [[pack]], [False])
    return sems[0], srcs[0], lands[0], token


def replicated_finish(handle, after, w, mom, var):
    sems, srcs, lands, _ = handle
    (buf,) = gather_d2d("ag_g_d2d", gather_ici_wait("ag_g_wait", srcs, lands, sems, after))
    rows = srcs[0].shape[0]
    total = sum_slots("ag_g_sum", buf.reshape(N_DEV, rows, LANES))
    res, off = {}, 0
    for name in _REPLICATED:
        n = w[name].size
        nr = -(-n // (LANES * 8)) * 8
        g = total[off:off + nr].reshape(-1)[:n].reshape(w[name].shape)
        off += nr
        res[name] = (g,) + adamw_native("adamw_" + name, g, w[name], mom[name], var[name])
    return res, total[off, 0]


def kernel(x, mem, e_norm, e_w_in, e_gmlp_w, e_gmlp_b, e_conv_w, e_conv_b, e_conv_ln_g, e_conv_ln_b, e_w_out, o_norm, o_w_in, o_lam_re, o_lam_im, o_log_dt, o_b_re, o_b_im, o_c_re, o_c_im, o_d, o_w_out, ca_norm, ca_mem_norm, ca_wq, ca_wk, ca_wv, ca_wo, ffn_norm, ffn_w_gate, ffn_w_up, ffn_w_down, final_norm, loss_target, m_e_norm, m_e_w_in, m_e_gmlp_w, m_e_gmlp_b, m_e_conv_w, m_e_conv_b, m_e_conv_ln_g, m_e_conv_ln_b, m_e_w_out, m_o_norm, m_o_w_in, m_o_lam_re, m_o_lam_im, m_o_log_dt, m_o_b_re, m_o_b_im, m_o_c_re, m_o_c_im, m_o_d, m_o_w_out, m_ca_norm, m_ca_mem_norm, m_ca_wq, m_ca_wk, m_ca_wv, m_ca_wo, m_ffn_norm, m_ffn_w_gate, m_ffn_w_up, m_ffn_w_down, m_final_norm, v_e_norm, v_e_w_in, v_e_gmlp_w, v_e_gmlp_b, v_e_conv_w, v_e_conv_b, v_e_conv_ln_g, v_e_conv_ln_b, v_e_w_out, v_o_norm, v_o_w_in, v_o_lam_re, v_o_lam_im, v_o_log_dt, v_o_b_re, v_o_b_im, v_o_c_re, v_o_c_im, v_o_d, v_o_w_out, v_ca_norm, v_ca_mem_norm, v_ca_wq, v_ca_wk, v_ca_wv, v_ca_wo, v_ffn_norm, v_ffn_w_gate, v_ffn_w_up, v_ffn_w_down, v_final_norm):
    given = dict(locals())
    local = {k: given[k] for k in _ORDER}
    mom = {k: given["m_" + k] for k in _ORDER}
    var = {k: given["v_" + k] for k in _ORDER}

    w = {}
    w.update({
        "e_norm": e_norm, "e_gmlp_w": e_gmlp_w[0], "e_gmlp_b": e_gmlp_b.reshape(A_GROUPS, GMLP_BLOCK, 1),
        "e_conv_b": e_conv_b, "e_conv_ln_g": e_conv_ln_g, "e_conv_ln_b": e_conv_ln_b,
        "o_lam_re": o_lam_re[0], "o_lam_im": o_lam_im[0], "o_log_dt": o_log_dt[0], "o_b_re": o_b_re[0], "o_b_im": o_b_im[0],
        "o_c_re": o_c_re[0], "o_c_im": o_c_im[0], "ca_norm": ca_norm, "ca_mem_norm": ca_mem_norm, "ffn_norm": ffn_norm,
        "final_norm": final_norm.reshape(1, D_MODEL),
    })
    start_reduce, finish_reduce = gradient_reducer(local, mom, var)
    fetch, token = weight_fetcher(local)
    loss_part, grad_x, grads = local_step(x[0], mem[0], loss_target[0], w, fetch, start_reduce, token[0:1, 0:1])
    grads["final_norm"] = grads["final_norm"].reshape(D_MODEL)

    handle = replicated_start(grads, loss_part)
    res = finish_reduce(handle[3])
    rep, loss = replicated_finish(handle, res["ffn_w_down"][1], local, mom, var)
    res.update(rep)
    return (loss, grad_x[None], *[res[k][0] for k in _ORDER], *[res[k][1] for k in _ORDER],
            *[res[k][2] for k in _ORDER], *[res[k][3] for k in _ORDER])
```

```python
import jax
import jax.numpy as jnp
from jax import lax
from jax.experimental import pallas as pl
from jax.experimental.pallas import tpu as pltpu

F32 = jnp.float32
BF16 = jnp.bfloat16
S = jax.ShapeDtypeStruct

D_MODEL = 1024
A_WIDTH = 512
A_GROUPS = 4
GMLP_BLOCK = 128
CHUNK = 64
B_WIDTH = 512
IN_WIDTH = 2 * A_WIDTH + 2 * B_WIDTH
CONV_WIDTH = 31
CONV_PAD = 32
C_WIDTH = 512
C_GROUP_CH = 16
C_GROUPS = 32
C_STATE = 64
N_STATE = C_GROUPS * C_STATE
CA_HEADS = 4
CA_HEAD_DIM = 256
FFN_HIDDEN = 2816
EPS = 1e-6
ADAM_LR = 0.001
ADAM_B1 = 0.9
ADAM_B2 = 0.999
ADAM_EPS = 1e-08
ADAM_WD = 0.01
ADAM_STEP = 10
N_DEV = 8
LANES = 128
VMEM_LIMIT = 56 << 20
VMEM_BUDGET = 40 << 20
MM_TN_RESIDENT = 8 << 20
MESH = pl.DeviceIdType.MESH
ANY = pl.BlockSpec(memory_space=pl.ANY)


def _cp(*sem):
    return pltpu.CompilerParams(dimension_semantics=sem, vmem_limit_bytes=VMEM_LIMIT)


def _tile(n, pref):
    t = pref
    while n % t:
        t //= 2
    return t


def _bf(v):
    return v if v.dtype == BF16 else v.astype(BF16)


def _sigmoid(x):
    return 1.0 / (1.0 + jnp.exp(-x))


_GC = 0.7978845608028654


def _gelu(x):
    return 0.5 * x * (1.0 + jnp.tanh(_GC * (x + 0.044715 * x * x * x)))


def _gelu_grad(x):
    x2 = x * x
    t = jnp.tanh(_GC * (x + 0.044715 * x * x2))
    return 0.5 * (1.0 + t) + 0.5 * x * (1.0 - t * t) * _GC * (1.0 + 3.0 * 0.044715 * x2)


def _tspec(entry, tm):
    if isinstance(entry, tuple):
        arr, cb, width = entry
        return arr, pl.BlockSpec((tm, width), lambda i, cb=cb: (i, cb))
    return entry, pl.BlockSpec((tm, entry.shape[1]), lambda i: (i, 0))


def rows_call(name, fn, tiled, full, outs, accs, tm=256):
    pairs = [_tspec(e, tm) for e in tiled]
    arrs = [p[0] for p in pairs]
    rows = arrs[0].shape[0]
    tm = _tile(rows, tm)
    pairs = [_tspec(e, tm) for e in tiled]
    n_in = len(tiled) + len(full)
    n_out = len(outs)

    def body(*refs):
        vals = [r[...] for r in refs[:n_in]]
        o_refs = refs[n_in:n_in + n_out]
        a_refs = refs[n_in + n_out:]
        ov, av = fn(*vals)
        for r, v in zip(o_refs, ov):
            r[...] = v.astype(r.dtype)
        if a_refs:
            @pl.when(pl.program_id(0) == 0)
            def _():
                for r in a_refs:
                    r[...] = jnp.zeros(r.shape, r.dtype)
            for r, v in zip(a_refs, av):
                r[...] += v

    in_specs = [p[1] for p in pairs] + [pl.BlockSpec(a.shape, lambda i, nd=a.ndim: (0,) * nd) for a in full]
    out_specs = [pl.BlockSpec((tm, c), lambda i: (i, 0)) for c, _ in outs]
    out_specs += [pl.BlockSpec(s, lambda i, nd=len(s): (0,) * nd) for s in accs]
    out_shape = [S((rows, c), dt) for c, dt in outs] + [S(s, F32) for s in accs]
    return pl.pallas_call(body, grid=(rows // tm,), in_specs=in_specs, out_specs=out_specs, out_shape=out_shape,
                          compiler_params=_cp("arbitrary"), name=name)(*arrs, *full)


def mm_nn(name, m, n, pairs, n_acc, epi, outs, tiled=(), cols=(), rowv=(), sums=(), norm_gain=None):
    a_ops, a_slot, b_arrs, b_specs, idx, trans = [], [], [], [], [], []
    fixed = 0
    for pair in pairs:
        a, b, k = pair[:3]
        bt = len(pair) > 3
        arr, cb, kdim = a if isinstance(a, tuple) else (a, 0, a.shape[1])
        key = (id(arr), cb, kdim)
        if key not in [o[0] for o in a_ops]:
            a_ops.append((key, arr, cb, kdim))
        a_slot.append([o[0] for o in a_ops].index(key))
        b_arr, off = b if isinstance(b, tuple) else (b, 0)
        b_arrs.append(b_arr)
        if bt:
            assert off % n == 0 and b_arr.shape[1] == kdim
            b_specs.append(pl.BlockSpec((n, kdim), lambda i, o=off // n: (o, 0), pipeline_mode=pl.Buffered(1)))
        else:
            assert b_arr.shape[1] == n
            b_specs.append(pl.BlockSpec((kdim, n), lambda i, o=off: (o, 0), pipeline_mode=pl.Buffered(1)))
        fixed += kdim * n * b_arr.dtype.itemsize
        idx.append(k)
        trans.append(bt)
    per_row = sum(2 * kdim * arr.dtype.itemsize for _, arr, _, kdim in a_ops)
    per_row += sum(2 * n * t.dtype.itemsize for t in tiled) + sum(2 * n * jnp.dtype(dt).itemsize for dt in outs)
    cn = n if sums or cols else (512 if n % 512 == 0 else 256)
    per_row += (n_acc + 3) * cn * 4
    tm = next((t for t in (1024, 512, 256, 128) if m % t == 0 and fixed + t * per_row <= VMEM_BUDGET), _tile(m, 128))
    n_a, n_p, n_t = len(a_ops), len(pairs), len(tiled)
    n_in = n_a + n_p + n_t + len(cols) + len(rowv)
    normed = norm_gain is not None
    o0 = n_in + normed

    def body(*refs):
        a_vals = [None if normed and i == 0 else _bf(r[...]) for i, r in enumerate(refs[:n_a])]
        if normed:
            xv = refs[0][...]
            rv = lax.rsqrt(jnp.mean(xv * xv, axis=-1, keepdims=True) + EPS)
            a_vals[0] = (xv * rv * refs[n_in][...]).astype(BF16)
            refs[o0 + len(outs)][...] = a_vals[0]
            refs[o0 + len(outs) + 1][...] = rv
        for j in range(n // cn):
            cs = slice(j * cn, (j + 1) * cn)
            accs = [None] * n_acc
            for p in range(n_p):
                av, b_ref = a_vals[a_slot[p]], refs[n_a + p]
                if trans[p]:
                    d = lax.dot_general(av, _bf(b_ref[cs, :]), (((1,), (1,)), ((), ())), preferred_element_type=F32)
                else:
                    d = jnp.dot(av, _bf(b_ref[:, cs]), preferred_element_type=F32)
                accs[idx[p]] = d if accs[idx[p]] is None else accs[idx[p]] + d
            extra = [r[:, cs] for r in refs[n_a + n_p:n_a + n_p + n_t]] + [r[...] for r in refs[n_a + n_p + n_t:n_in - len(rowv)]]
            extra += [r[:, cs] for r in refs[n_in - len(rowv):n_in]]
            ov = epi(accs, *extra)
            for r, v in zip(refs[o0:o0 + len(outs)], ov):
                r[:, cs] = v.astype(r.dtype)
        sv = ov[len(outs):]
        if sums:
            s_refs = refs[o0 + len(outs) + 2 * normed:]

            @pl.when(pl.program_id(0) == 0)
            def _():
                for r in s_refs:
                    r[...] = jnp.zeros(r.shape, r.dtype)
            for r, v in zip(s_refs, sv):
                r[...] += v

    in_specs = [pl.BlockSpec((tm, kdim), lambda i, cb=cb: (i, cb)) for _, _, cb, kdim in a_ops] + b_specs
    in_specs += [pl.BlockSpec((tm, n), lambda i: (i, 0)) for _ in tiled]
    in_specs += [pl.BlockSpec((tm, 1), lambda i: (i, 0)) for _ in cols]
    in_specs += [pl.BlockSpec((1, n), lambda i: (0, 0)) for _ in rowv]
    out_specs = [pl.BlockSpec((tm, n), lambda i: (i, 0)) for _ in outs]
    out_shape = [S((m, n), dt) for dt in outs]
    gain = []
    if normed:
        k0 = a_ops[0][3]
        gain = [norm_gain]
        in_specs.append(pl.BlockSpec((1, k0), lambda i: (0, 0)))
        out_specs += [pl.BlockSpec((tm, k0), lambda i: (i, 0)), pl.BlockSpec((tm, 1), lambda i: (i, 0))]
        out_shape += [S((m, k0), BF16), S((m, 1), F32)]
    out_specs += [pl.BlockSpec(s, lambda i, nd=len(s): (0,) * nd) for s in sums]
    out_shape += [S(s, F32) for s in sums]
    return pl.pallas_call(body, grid=(m // tm,), in_specs=in_specs, out_specs=out_specs, out_shape=out_shape,
                          compiler_params=_cp("arbitrary" if sums else "parallel"),
                          name=name)(*[o[1] for o in a_ops], *b_arrs, *tiled, *cols, *rowv, *gain)


def mm_tn(name, a, b, out_dtype=BF16):
    if isinstance(a, tuple):
        a_arr, a_cb, m = a
    else:
        a_arr, a_cb, m = a, None, a.shape[1]
    if isinstance(b, tuple):
        b_arr, b_cb, n = b
    else:
        b_arr, b_cb, n = b, None, b.shape[1]
    t = a_arr.shape[0]
    whole_b = t * n * b_arr.dtype.itemsize <= MM_TN_RESIDENT and b_cb is None
    tn = n if whole_b else _tile(n, 512)
    tm = _tile(m, 512 if t * 512 * a_arr.dtype.itemsize * 2 + t * tn * b_arr.dtype.itemsize * 2 <= VMEM_BUDGET else 256)
    a_off = 0 if a_cb is None else a_cb * (m // tm)
    b_off = 0 if b_cb is None else b_cb * (n // tn)

    def body(a_ref, b_ref, o_ref):
        o_ref[...] = lax.dot_general(_bf(a_ref[...]), _bf(b_ref[...]), (((0,), (0,)), ((), ())),
                                     preferred_element_type=F32).astype(o_ref.dtype)

    if whole_b:
        b_spec = pl.BlockSpec((t, n), lambda i, j: (0, 0), pipeline_mode=pl.Buffered(1))
    else:
        b_spec = pl.BlockSpec((t, tn), lambda i, j: (0, j + b_off))
    return pl.pallas_call(
        body, grid=(m // tm, n // tn),
        in_specs=[pl.BlockSpec((t, tm), lambda i, j: (0, i + a_off)), b_spec],
        out_specs=pl.BlockSpec((tm, tn), lambda i, j: (i, j)), out_shape=S((m, n), out_dtype),
        compiler_params=_cp("parallel", "parallel"), name=name)(a_arr, b_arr)


def rms_bwd_gain_only(name, dxn, x, r):
    def fn(dv, xv, rv):
        return [], [jnp.sum(dv * xv * rv, axis=0, keepdims=True)]
    return rows_call(name, fn, [dxn, x, r], [], [], [(1, x.shape[1])])[0]


def final_loss(name, x, gain, target):
    d = x.shape[1]

    def fn(xv, tv, g):
        r = lax.rsqrt(jnp.mean(xv * xv, axis=-1, keepdims=True) + EPS)
        xh = xv * r
        err = xh * g - tv
        dy = err * (1.0 / d)
        w = dy * g
        dx = r * (w - xh * jnp.mean(w * xh, axis=-1, keepdims=True))
        part = jnp.sum(jnp.sum(err * err, axis=-1, keepdims=True), axis=0, keepdims=True) * (0.5 / d)
        return [dx, dx], [jnp.sum(dy * xh, axis=0, keepdims=True), part]

    return rows_call(name, fn, [x, target], [gain], [(d, F32), (d, BF16)], [(1, d), (1, 1)])


def _gmlp_mask():
    row = lax.broadcasted_iota(jnp.int32, (GMLP_BLOCK, GMLP_BLOCK), 0) // CHUNK
    col = lax.broadcasted_iota(jnp.int32, (GMLP_BLOCK, GMLP_BLOCK), 1) // CHUNK
    return col <= row


def _ln_plain(v):
    mu = jnp.mean(v, axis=-1, keepdims=True)
    vc = v - mu
    rstd = lax.rsqrt(jnp.mean(vc * vc, axis=-1, keepdims=True) + EPS)
    return vc * rstd, rstd


def gmlp_fwd(name, proj, w, b, tm=512):
    t = proj.shape[0]
    tm = _tile(t, tm)

    def body(au_ref, av_ref, w_ref, b_ref, o_ref):
        mask = _gmlp_mask()
        u = _gelu(au_ref[...])
        vn, _ = _ln_plain(_gelu(av_ref[...]))
        vnb = _bf(vn)
        for g in range(A_GROUPS):
            wg = _bf(jnp.where(mask, w_ref[g], 0.0))
            cs = slice(g * GMLP_BLOCK, (g + 1) * GMLP_BLOCK)
            for n in range(tm // GMLP_BLOCK):
                rs = slice(n * GMLP_BLOCK, (n + 1) * GMLP_BLOCK)
                sg = jnp.dot(wg, vnb[rs, cs], preferred_element_type=F32) + b_ref[g]
                o_ref[rs, cs] = (u[rs, cs] * sg).astype(o_ref.dtype)

    return pl.pallas_call(
        body, grid=(t // tm,),
        in_specs=[pl.BlockSpec((tm, A_WIDTH), lambda i: (i, 0)), pl.BlockSpec((tm, A_WIDTH), lambda i: (i, 1)),
                  pl.BlockSpec(w.shape, lambda i: (0, 0, 0)), pl.BlockSpec(b.shape, lambda i: (0, 0, 0))],
        out_specs=pl.BlockSpec((tm, A_WIDTH), lambda i: (i, 0)), out_shape=S((t, A_WIDTH), BF16),
        compiler_params=_cp("parallel"), name=name)(proj, proj, w, b)


def gmlp_bwd(name, proj, dcat, w, b, tm=512):
    t = proj.shape[0]
    tm = _tile(t, tm)

    def body(au_ref, av_ref, do_ref, w_ref, b_ref, dp_ref, dw_ref, db_ref):
        @pl.when(pl.program_id(0) == 0)
        def _():
            dw_ref[...] = jnp.zeros(dw_ref.shape, F32)
            db_ref[...] = jnp.zeros(db_ref.shape, F32)

        mask = _gmlp_mask()
        au = au_ref[...]
        av = av_ref[...]
        u = _gelu(au)
        vn, rstd = _ln_plain(_gelu(av))
        vnb = _bf(vn)
        dout = do_ref[...]
        dvn_cols = []
        for g in range(A_GROUPS):
            wm = jnp.where(mask, w_ref[g], 0.0)
            wg = _bf(wm)
            wgt = _bf(wm.T)
            cs = slice(g * GMLP_BLOCK, (g + 1) * GMLP_BLOCK)
            dwg = jnp.zeros((GMLP_BLOCK, GMLP_BLOCK), F32)
            dbg = jnp.zeros((GMLP_BLOCK, 1), F32)
            dvn_rows = []
            for n in range(tm // GMLP_BLOCK):
                rs = slice(n * GMLP_BLOCK, (n + 1) * GMLP_BLOCK)
                sg = jnp.dot(wg, vnb[rs, cs], preferred_element_type=F32) + b_ref[g]
                dp_ref[rs, cs] = (dout[rs, cs] * sg * _gelu_grad(au[rs, cs])).astype(dp_ref.dtype)
                dsg = dout[rs, cs] * u[rs, cs]
                dsgb = _bf(dsg)
                dbg = dbg + jnp.sum(dsg, axis=1, keepdims=True)
                dwg = dwg + lax.dot_general(dsgb, vnb[rs, cs], (((1,), (1,)), ((), ())), preferred_element_type=F32)
                dvn_rows.append(jnp.dot(wgt, dsgb, preferred_element_type=F32))
            dw_ref[g] += jnp.where(mask, dwg, 0.0)
            db_ref[g] += dbg
            dvn_cols.append(jnp.concatenate(dvn_rows, axis=0))
        dvn = jnp.concatenate(dvn_cols, axis=1)
        dv = rstd * (dvn - jnp.mean(dvn, axis=-1, keepdims=True) - vn * jnp.mean(dvn * vn, axis=-1, keepdims=True))
        dp_ref[:, A_WIDTH:] = (dv * _gelu_grad(av)).astype(dp_ref.dtype)

    return pl.pallas_call(
        body, grid=(t // tm,),
        in_specs=[pl.BlockSpec((tm, A_WIDTH), lambda i: (i, 0)), pl.BlockSpec((tm, A_WIDTH), lambda i: (i, 1)),
                  pl.BlockSpec((tm, A_WIDTH), lambda i: (i, 0)),
                  pl.BlockSpec(w.shape, lambda i: (0, 0, 0)), pl.BlockSpec(b.shape, lambda i: (0, 0, 0))],
        out_specs=[pl.BlockSpec((tm, 2 * A_WIDTH), lambda i: (i, 0)),
                   pl.BlockSpec(w.shape, lambda i: (0, 0, 0)), pl.BlockSpec(b.shape, lambda i: (0, 0, 0))],
        out_shape=[S((t, 2 * A_WIDTH), BF16), S(w.shape, F32), S(b.shape, F32)],
        compiler_params=_cp("arbitrary"), name=name)(proj, proj, dcat, w, b)


CONV_ROWS = 256


def conv_fwd(name, proj, w, cb):
    t = proj.shape[0]
    tc = LANES
    rows = _tile(t, CONV_ROWS)
    a_cb, g_cb = 2 * A_WIDTH // tc, (2 * A_WIDTH + B_WIDTH) // tc

    def body(a_ref, g_ref, w_ref, cb_ref, o_ref, hpad):
        hpad[0:CONV_PAD, :] = jnp.zeros((CONV_PAD, tc), F32)

        def fill(i, _):
            r0 = pl.multiple_of(i * rows, rows)
            hpad[pl.ds(CONV_PAD + r0, rows), :] = a_ref[pl.ds(r0, rows), :] * _sigmoid(g_ref[pl.ds(r0, rows), :])
            return 0
        lax.fori_loop(0, t // rows, fill, 0)

        def conv(i, _):
            r0 = pl.multiple_of(i * rows, rows)
            win = hpad[pl.ds(r0, rows + CONV_PAD), :]
            acc = jnp.zeros((rows, tc), F32) + cb_ref[...]
            for b in range(SUB):
                wb = win if b == 0 else pltpu.roll(win, b, 0)
                for a in range(CONV_PAD // SUB):
                    k = CONV_WIDTH - 1 - (SUB * a + b)
                    if k >= 0:
                        lo = CONV_PAD - SUB * a
                        acc = acc + wb[lo:lo + rows, :] * w_ref[k:k + 1, :]
            o_ref[pl.ds(r0, rows), :] = acc
            return 0
        lax.fori_loop(0, t // rows, conv, 0)

    return pl.pallas_call(
        body, grid=(B_WIDTH // tc,),
        in_specs=[pl.BlockSpec((t, tc), lambda j: (0, a_cb + j)), pl.BlockSpec((t, tc), lambda j: (0, g_cb + j)),
                  pl.BlockSpec((CONV_WIDTH, tc), lambda j: (0, j)), pl.BlockSpec((1, tc), lambda j: (0, j))],
        out_specs=pl.BlockSpec((t, tc), lambda j: (0, j)), out_shape=S((t, B_WIDTH), F32),
        scratch_shapes=[pltpu.VMEM((t + CONV_PAD, tc), F32)],
        compiler_params=_cp("parallel"), name=name)(proj, proj, w, cb)


def conv_bwd(name, proj, dhc, w):
    t = proj.shape[0]
    tc = LANES
    rows = _tile(t, CONV_ROWS)
    a_cb, g_cb = 2 * A_WIDTH // tc, (2 * A_WIDTH + B_WIDTH) // tc
    win_rows = rows + CONV_PAD

    def body(a_ref, g_ref, d_ref, w_ref, da_ref, dg_ref, dw_ref, dcb_ref, hpad, dpad, dwacc):
        hpad[0:CONV_PAD, :] = jnp.zeros((CONV_PAD, tc), F32)
        dpad[t:t + CONV_PAD, :] = jnp.zeros((CONV_PAD, tc), F32)
        dwacc[...] = jnp.zeros(dwacc.shape, F32)

        def fill(i, _):
            r0 = pl.multiple_of(i * rows, rows)
            hpad[pl.ds(CONV_PAD + r0, rows), :] = a_ref[pl.ds(r0, rows), :] * _sigmoid(g_ref[pl.ds(r0, rows), :])
            dpad[pl.ds(r0, rows), :] = d_ref[pl.ds(r0, rows), :]
            return 0
        lax.fori_loop(0, t // rows, fill, 0)

        def step(i, dcb):
            r0 = pl.multiple_of(i * rows, rows)
            hwin = hpad[pl.ds(r0, win_rows), :]
            dwin = dpad[pl.ds(r0, win_rows), :]
            dchunk = dwin[:rows, :]
            dh = jnp.zeros((rows, tc), F32)
            for b in range(SUB):
                hb = hwin if b == 0 else pltpu.roll(hwin, b, 0)
                db = dwin if b == 0 else pltpu.roll(dwin, win_rows - b, 0)
                for a in range(CONV_PAD // SUB):
                    k = CONV_WIDTH - 1 - (SUB * a + b)
                    if k >= 0:
                        dh = dh + db[SUB * a:SUB * a + rows, :] * w_ref[k:k + 1, :]
                        lo = CONV_PAD - SUB * a
                        prod = dchunk * hb[lo:lo + rows, :]
                        dwacc[k] += jnp.sum(prod.reshape(rows // 8, 8, tc), axis=0)
            a = a_ref[pl.ds(r0, rows), :]
            sg = _sigmoid(g_ref[pl.ds(r0, rows), :])
            da_ref[pl.ds(r0, rows), :] = (dh * sg).astype(da_ref.dtype)
            dg_ref[pl.ds(r0, rows), :] = (dh * a * sg * (1.0 - sg)).astype(dg_ref.dtype)
            return dcb + jnp.sum(dchunk, axis=0, keepdims=True)
        dcb = lax.fori_loop(0, t // rows, step, jnp.zeros((1, tc), F32))
        dcb_ref[...] = dcb
        for k in range(CONV_WIDTH):
            dw_ref[k:k + 1, :] = jnp.sum(dwacc[k], axis=0, keepdims=True)

    return pl.pallas_call(
        body, grid=(B_WIDTH // tc,),
        in_specs=[pl.BlockSpec((t, tc), lambda j: (0, a_cb + j)), pl.BlockSpec((t, tc), lambda j: (0, g_cb + j)),
                  pl.BlockSpec((t, tc), lambda j: (0, j)), pl.BlockSpec((CONV_WIDTH, tc), lambda j: (0, j))],
        out_specs=[pl.BlockSpec((t, tc), lambda j: (0, j)), pl.BlockSpec((t, tc), lambda j: (0, j)),
                   pl.BlockSpec((CONV_WIDTH, tc), lambda j: (0, j)), pl.BlockSpec((1, tc), lambda j: (0, j))],
        out_shape=[S((t, B_WIDTH), BF16), S((t, B_WIDTH), BF16), S((CONV_WIDTH, B_WIDTH), F32), S((1, B_WIDTH), F32)],
        scratch_shapes=[pltpu.VMEM((t + CONV_PAD, tc), F32), pltpu.VMEM((t + CONV_PAD, tc), F32),
                        pltpu.VMEM((CONV_WIDTH, 8, tc), F32)],
        compiler_params=_cp("parallel"), name=name)(proj, proj, dhc, w)


def ln_silu_fwd(name, hc, g, b):
    def fn(h, gv, bv):
        y, _ = _ln_plain(h)
        z = y * gv + bv
        return [z * _sigmoid(z)], []
    return rows_call(name, fn, [hc], [g, b], [(hc.shape[1], BF16)], [])[0]


def ln_silu_bwd(name, hc, dcat, g, b):
    c = hc.shape[1]

    def fn(h, dout, gv, bv):
        y, rstd = _ln_plain(h)
        z = y * gv + bv
        s = _sigmoid(z)
        dz = dout * s * (1.0 + z * (1.0 - s))
        dyv = dz * gv
        dh = rstd * (dyv - jnp.mean(dyv, axis=-1, keepdims=True) - y * jnp.mean(dyv * y, axis=-1, keepdims=True))
        return [dh], [jnp.sum(dz * y, axis=0, keepdims=True), jnp.sum(dz, axis=0, keepdims=True)]

    return rows_call(name, fn, [hc, (dcat, 1, c)], [g, b], [(c, F32)], [(1, c), (1, c)])


_NT = (((1,), (1,)), ((), ()))
_TN = (((0,), (0,)), ((), ()))


def attn_fwd(name, q, k, v, tm=512):
    t, d = q.shape
    m = k.shape[0]
    tm = _tile(t, tm)
    scale = CA_HEAD_DIM ** -0.5

    def body(q_ref, k_ref, v_ref, o_ref):
        for h in range(CA_HEADS):
            cs = slice(h * CA_HEAD_DIM, (h + 1) * CA_HEAD_DIM)
            s = lax.dot_general(q_ref[:, cs], k_ref[:, cs], _NT, preferred_element_type=F32) * scale
            e = jnp.exp(s - jnp.max(s, axis=-1, keepdims=True))
            p = e / jnp.sum(e, axis=-1, keepdims=True)
            o_ref[:, cs] = jnp.dot(_bf(p), v_ref[:, cs], preferred_element_type=F32).astype(o_ref.dtype)

    return pl.pallas_call(
        body, grid=(t // tm,),
        in_specs=[pl.BlockSpec((tm, d), lambda i: (i, 0)), pl.BlockSpec((m, d), lambda i: (0, 0)),
                  pl.BlockSpec((m, d), lambda i: (0, 0))],
        out_specs=pl.BlockSpec((tm, d), lambda i: (i, 0)), out_shape=S((t, d), BF16),
        compiler_params=_cp("parallel"), name=name)(q, k, v)


def attn_bwd(name, q, k, v, do, tm=512):
    t, d = q.shape
    m = k.shape[0]
    tm = _tile(t, tm)
    scale = CA_HEAD_DIM ** -0.5

    def body(q_ref, k_ref, v_ref, do_ref, dq_ref, dk_ref, dv_ref):
        @pl.when(pl.program_id(0) == 0)
        def _():
            dk_ref[...] = jnp.zeros(dk_ref.shape, F32)
            dv_ref[...] = jnp.zeros(dv_ref.shape, F32)

        for h in range(CA_HEADS):
            cs = slice(h * CA_HEAD_DIM, (h + 1) * CA_HEAD_DIM)
            qh, kh, vh, doh = q_ref[:, cs], k_ref[:, cs], v_ref[:, cs], do_ref[:, cs]
            s = lax.dot_general(qh, kh, _NT, preferred_element_type=F32) * scale
            e = jnp.exp(s - jnp.max(s, axis=-1, keepdims=True))
            p = e / jnp.sum(e, axis=-1, keepdims=True)
            pb = _bf(p)
            dv_ref[:, cs] += lax.dot_general(pb, doh, _TN, preferred_element_type=F32)
            dp = lax.dot_general(doh, vh, _NT, preferred_element_type=F32)
            ds = _bf(p * (dp - jnp.sum(dp * p, axis=-1, keepdims=True)) * scale)
            dq_ref[:, cs] = jnp.dot(ds, kh, preferred_element_type=F32).astype(dq_ref.dtype)
            dk_ref[:, cs] += lax.dot_general(ds, qh, _TN, preferred_element_type=F32)

    return pl.pallas_call(
        body, grid=(t // tm,),
        in_specs=[pl.BlockSpec((tm, d), lambda i: (i, 0)), pl.BlockSpec((m, d), lambda i: (0, 0)),
                  pl.BlockSpec((m, d), lambda i: (0, 0)), pl.BlockSpec((tm, d), lambda i: (i, 0))],
        out_specs=[pl.BlockSpec((tm, d), lambda i: (i, 0)), pl.BlockSpec((m, d), lambda i: (0, 0)),
                   pl.BlockSpec((m, d), lambda i: (0, 0))],
        out_shape=[S((t, d), BF16), S((m, d), F32), S((m, d), F32)],
        compiler_params=_cp("arbitrary"), name=name)(q, k, v, do)


SUB = 8
S5_ROWS = 256


S5_BLOCKS = 4
BLOCK_CH = C_WIDTH // S5_BLOCKS
BLOCK_ST = N_STATE // S5_BLOCKS
_S5_BLOCKS = tuple((slice(BLOCK_CH * q, BLOCK_CH * (q + 1)), slice(BLOCK_ST * q, BLOCK_ST * (q + 1)),
                    slice(N_STATE + BLOCK_ST * q, N_STATE + BLOCK_ST * (q + 1))) for q in range(S5_BLOCKS))
_HI = lax.Precision.HIGHEST
_GP = (C_GROUPS, C_STATE)
_RP = (C_WIDTH, C_STATE)


def _zoh(lr, li, ldt):
    dt = jnp.exp(ldt)
    mag = jnp.exp(lr * dt)
    ar = mag * jnp.cos(li * dt)
    ai = mag * jnp.sin(li * dt)
    den = lr * lr + li * li
    qr = ((ar - 1.0) * lr + ai * li) / den
    qi = (ai * lr - (ar - 1.0) * li) / den
    return dt, ar, ai, den, qr, qi


def _per_channel(v):
    return jnp.broadcast_to(v[:, None, :], (C_GROUPS, C_GROUP_CH, C_STATE)).reshape(_RP)


def _same_group(shape, row_per_group, col_per_group):
    rows = lax.broadcasted_iota(jnp.int32, shape, 0) // row_per_group
    cols = lax.broadcasted_iota(jnp.int32, shape, 1) // col_per_group
    return rows == cols


def _spread(shape, axis):
    long = lax.broadcasted_iota(jnp.int32, shape, axis) % C_STATE
    short = lax.broadcasted_iota(jnp.int32, shape, 1 - axis)
    return long == short


def s5_discretise(name, lam_re, lam_im, log_dt, bt_re, bt_im):
    def body(lr_ref, li_ref, ldt_ref, btr_ref, bti_ref, a_ref, bbr_ref, bbi_ref):
        _, ar, ai, _, qr, qi = _zoh(lr_ref[...], li_ref[...], ldt_ref[...])
        a_ref[0] = ar
        a_ref[1] = ai
        q2r, q2i = _per_channel(qr), _per_channel(qi)
        btr, bti = btr_ref[...], bti_ref[...]
        bbr_ref[...] = q2r * btr - q2i * bti
        bbi_ref[...] = q2r * bti + q2i * btr

    return pl.pallas_call(body, out_shape=[S((2,) + _GP, F32), S(_RP, F32), S(_RP, F32)],
                          name=name)(lam_re, lam_im, log_dt, bt_re, bt_im)


def s5_operands(name, a, bbr, bbi, c2r, c2i, ctr, cti):
    ns = N_STATE

    def body(a_ref, bbr_ref, bbi_ref, c2r_ref, c2i_ref, ctr_ref, cti_ref, pw_ref, qw_ref, mb_ref, mc_ref, mct_ref):
        ar, ai = a_ref[0:1, :], a_ref[1:2, :]
        pows = [(ar, ai)]
        for _ in range(SUB - 1):
            pr, pi = pows[-1]
            pows.append((pr * ar - pi * ai, pr * ai + pi * ar))
        rows = lax.broadcasted_iota(jnp.int32, (SUB, ns), 0)

        def rows_of(v):
            return jnp.broadcast_to(v, (SUB, ns))

        for k, s in enumerate((1, 2, 4)):
            pr, pi = rows_of(pows[s - 1][0]), rows_of(pows[s - 1][1])
            pw_ref[k, 0] = jnp.where(rows >= s, pr, 0.0)
            pw_ref[k, 1] = jnp.where(rows >= s, pi, 0.0)
            qw_ref[k, 0] = jnp.where(rows + s <= SUB - 1, pr, 0.0)
            qw_ref[k, 1] = jnp.where(rows + s <= SUB - 1, -pi, 0.0)
        fr = fi = br = bi = jnp.zeros((SUB, ns), F32)
        for i in range(SUB):
            fr = jnp.where(rows == i, rows_of(pows[i][0]), fr)
            fi = jnp.where(rows == i, rows_of(pows[i][1]), fi)
            br = jnp.where(rows == i, rows_of(pows[SUB - 1 - i][0]), br)
            bi = jnp.where(rows == i, rows_of(-pows[SUB - 1 - i][1]), bi)
        pw_ref[3, 0], pw_ref[3, 1], qw_ref[3, 0], qw_ref[3, 1] = fr, fi, br, bi

        wide = _spread((C_STATE, ns), 1).astype(BF16)
        tall = _spread((ns, C_STATE), 0).astype(BF16)
        in_rows = _same_group((C_WIDTH, ns), C_GROUP_CH, C_STATE)
        in_cols = _same_group((ns, C_WIDTH), C_STATE, C_GROUP_CH)

        def across(v, sign=1.0):
            return jnp.where(in_rows, sign * jnp.dot(_bf(v), wide, preferred_element_type=F32), 0.0).astype(BF16)

        def down(vt, sign=1.0):
            return jnp.where(in_cols, sign * jnp.dot(tall, _bf(vt), preferred_element_type=F32), 0.0).astype(BF16)

        mb_ref[:, 0:ns] = across(bbr_ref[...])
        mb_ref[:, ns:2 * ns] = across(bbi_ref[...])
        mct_ref[:, 0:ns] = across(c2r_ref[...])
        mct_ref[:, ns:2 * ns] = across(c2i_ref[...], -1.0)
        mc_ref[0:ns, :] = down(ctr_ref[...])
        mc_ref[ns:2 * ns, :] = down(cti_ref[...], -1.0)

    return pl.pallas_call(
        body, out_shape=[S((4, 2, SUB, ns), F32), S((4, 2, SUB, ns), F32), S((C_WIDTH, 2 * ns), BF16),
                         S((2 * ns, C_WIDTH), BF16), S((C_WIDTH, 2 * ns), BF16)],
        compiler_params=pltpu.CompilerParams(vmem_limit_bytes=VMEM_LIMIT), name=name)(a, bbr, bbi, c2r, c2i, ctr, cti)


def s5_block_grads(name, u, lamb, xsb, dyb):
    t = u.shape[0]

    def mb_body(u_ref, lr_ref, li_ref, o_ref):
        ub = _bf(u_ref[...])
        o_ref[:, 0:BLOCK_ST] = lax.dot_general(ub, lr_ref[...], _TN, preferred_element_type=F32)
        o_ref[:, BLOCK_ST:2 * BLOCK_ST] = lax.dot_general(ub, li_ref[...], _TN, preferred_element_type=F32)

    d_mb = pl.pallas_call(
        mb_body, grid=(S5_BLOCKS,),
        in_specs=[pl.BlockSpec((t, BLOCK_CH), lambda q: (0, q)), pl.BlockSpec((t, BLOCK_ST), lambda q: (0, q)),
                  pl.BlockSpec((t, BLOCK_ST), lambda q: (0, S5_BLOCKS + q))],
        out_specs=pl.BlockSpec((BLOCK_CH, 2 * BLOCK_ST), lambda q: (q, 0)), out_shape=S((C_WIDTH, 2 * BLOCK_ST), F32),
        compiler_params=_cp("parallel"), name=name + "_b")(u, lamb, lamb)

    def mc_body(x_ref, dy_ref, o_ref):
        o_ref[...] = lax.dot_general(x_ref[...], dy_ref[...], _TN, preferred_element_type=F32)

    d_mc = pl.pallas_call(
        mc_body, grid=(2, S5_BLOCKS),
        in_specs=[pl.BlockSpec((t, BLOCK_ST), lambda p, q: (0, p * S5_BLOCKS + q)), pl.BlockSpec((t, BLOCK_CH), lambda p, q: (0, q))],
        out_specs=pl.BlockSpec((BLOCK_ST, BLOCK_CH), lambda p, q: (p * S5_BLOCKS + q, 0)),
        out_shape=S((2 * N_STATE, BLOCK_CH), F32), compiler_params=_cp("parallel", "parallel"), name=name + "_c")(xsb, dyb)
    return d_mb, d_mc


def s5_param_grads(name, d_mb, d_mc, da, lam_re, lam_im, log_dt, bt_re, bt_im):
    ns = N_STATE

    def body(dmb_ref, dmc_ref, da_ref, lr_ref, li_ref, ldt_ref, btr_ref, bti_ref,
             glr_ref, gli_ref, gdt_ref, gbr_ref, gbi_ref, gcr_ref, gci_ref):
        lr, li = lr_ref[...], li_ref[...]
        dt, ar, ai, den, qr, qi = _zoh(lr, li, ldt_ref[...])
        per_block = C_GROUPS // S5_BLOCKS
        wide = _spread((C_STATE, BLOCK_ST), 1).astype(F32)
        tall = _spread((BLOCK_ST, C_STATE), 0).astype(F32)
        rows = lax.broadcasted_iota(jnp.int32, (C_WIDTH, BLOCK_ST), 0) // C_GROUP_CH % per_block
        in_rows = rows == lax.broadcasted_iota(jnp.int32, (C_WIDTH, BLOCK_ST), 1) // C_STATE
        in_cols = _same_group((BLOCK_ST, BLOCK_CH), C_STATE, C_GROUP_CH)

        def fold_rows(v):
            return lax.dot_general(jnp.where(in_rows, v, 0.0), wide, (((1,), (1,)), ((), ())), precision=_HI,
                                   preferred_element_type=F32)

        def fold_cols(v):
            return lax.dot_general(jnp.where(in_cols, v, 0.0), tall, (((0,), (0,)), ((), ())), precision=_HI,
                                   preferred_element_type=F32)

        for cs, s_re, s_im in _S5_BLOCKS:
            gcr_ref[cs, :] = fold_cols(dmc_ref[s_re, :])
            gci_ref[cs, :] = -fold_cols(dmc_ref[s_im, :])
        gbbr = fold_rows(dmb_ref[:, 0:BLOCK_ST])
        gbbi = fold_rows(dmb_ref[:, BLOCK_ST:2 * BLOCK_ST])
        btr, bti = btr_ref[...], bti_ref[...]
        q2r, q2i = _per_channel(qr), _per_channel(qi)
        gbr_ref[...] = q2r * gbbr + q2i * gbbi
        gbi_ref[...] = q2r * gbbi - q2i * gbbr

        def per_group(v):
            return jnp.sum(v.reshape(C_GROUPS, C_GROUP_CH, C_STATE), axis=1)

        gqr = per_group(btr * gbbr + bti * gbbi)
        gqi = per_group(btr * gbbi - bti * gbbr)
        ilr, ili = lr / den, li / den
        gar = da_ref[0] + ilr * gqr - ili * gqi
        gai = da_ref[1] + ilr * gqi + ili * gqr
        sr = (qr * lr + qi * li) / den
        si = (qi * lr - qr * li) / den
        gzr = ar * gar + ai * gai
        gzi = ar * gai - ai * gar
        glr_ref[...] = -sr * gqr - si * gqi + dt * gzr
        gli_ref[...] = -sr * gqi + si * gqr + dt * gzi
        gdt_ref[...] = jnp.sum(lr * gzr + li * gzi, axis=1, keepdims=True) * dt

    return pl.pallas_call(
        body, out_shape=[S(_GP, F32), S(_GP, F32), S((C_GROUPS, 1), F32), S(_RP, F32), S(_RP, F32), S(_RP, F32), S(_RP, F32)],
        compiler_params=pltpu.CompilerParams(vmem_limit_bytes=VMEM_LIMIT), name=name,
    )(d_mb, d_mc, da, lam_re, lam_im, log_dt, bt_re, bt_im)


def _cmul_add(xr, xi, pr, pi, zr, zi):
    return xr + pr * zr - pi * zi, xi + pr * zi + pi * zr


def s5_fwd(name, u, mb, mc, pw, dskip):
    t = u.shape[0]
    tm = _tile(t, S5_ROWS)
    ns = N_STATE

    def body(u_ref, mb_ref, mc_ref, pw_ref, d_ref, gy_ref, y_ref, xs_ref, xb_ref, carry):
        @pl.when(pl.program_id(0) == 0)
        def _():
            carry[...] = jnp.zeros(carry.shape, F32)

        uv = u_ref[...]
        ub = _bf(uv)
        for cs, s_re, s_im in _S5_BLOCKS:
            xs_ref[:, s_re] = jnp.dot(ub[:, cs], mb_ref[cs, s_re], preferred_element_type=F32)
            xs_ref[:, s_im] = jnp.dot(ub[:, cs], mb_ref[cs, s_im], preferred_element_type=F32)

        def group(i, _):
            r0 = pl.multiple_of(i * SUB, SUB)
            xr = xs_ref[pl.ds(r0, SUB), 0:ns]
            xi = xs_ref[pl.ds(r0, SUB), ns:2 * ns]
            for k, s in enumerate((1, 2, 4)):
                xr, xi = _cmul_add(xr, xi, pw_ref[k, 0], pw_ref[k, 1], pltpu.roll(xr, s, 0), pltpu.roll(xi, s, 0))
            xr, xi = _cmul_add(xr, xi, pw_ref[3, 0], pw_ref[3, 1], carry[0], carry[1])
            xs_ref[pl.ds(r0, SUB), 0:ns] = xr
            xs_ref[pl.ds(r0, SUB), ns:2 * ns] = xi
            carry[0] = jnp.broadcast_to(xr[SUB - 1:SUB, :], (SUB, ns))
            carry[1] = jnp.broadcast_to(xi[SUB - 1:SUB, :], (SUB, ns))
            return 0
        lax.fori_loop(0, tm // SUB, group, 0)

        xb_ref[...] = _bf(xs_ref[...])
        for cs, s_re, s_im in _S5_BLOCKS:
            y = (jnp.dot(xb_ref[:, s_re], mc_ref[s_re, cs], preferred_element_type=F32)
                 + jnp.dot(xb_ref[:, s_im], mc_ref[s_im, cs], preferred_element_type=F32) + d_ref[:, cs] * uv[:, cs])
            y_ref[:, cs] = y
            gy_ref[:, cs] = _gelu(y).astype(gy_ref.dtype)

    c = u.shape[1]
    return pl.pallas_call(
        body, grid=(t // tm,),
        in_specs=[pl.BlockSpec((tm, c), lambda i: (i, 0)), pl.BlockSpec(mb.shape, lambda i: (0, 0)),
                  pl.BlockSpec(mc.shape, lambda i: (0, 0)), pl.BlockSpec(pw.shape, lambda i: (0, 0, 0, 0)),
                  pl.BlockSpec((1, c), lambda i: (0, 0))],
        out_specs=[pl.BlockSpec((tm, c), lambda i: (i, 0)), pl.BlockSpec((tm, c), lambda i: (i, 0)),
                   pl.BlockSpec((tm, 2 * ns), lambda i: (i, 0)), pl.BlockSpec((tm, 2 * ns), lambda i: (i, 0))],
        out_shape=[S((t, c), BF16), S((t, c), F32), S((t, 2 * ns), F32), S((t, 2 * ns), BF16)],
        scratch_shapes=[pltpu.VMEM((2, SUB, ns), F32)],
        compiler_params=_cp("arbitrary"), name=name)(u, mb, mc, pw, dskip)


def s5_bwd(name, dgy, y, u, xs, mct, mbt, qw, dskip):
    t, c = u.shape
    tm = _tile(t, S5_ROWS)
    nt = t // tm
    ns = N_STATE
    ng = tm // SUB

    def body(dgy_ref, y_ref, u_ref, xs_ref, mct_ref, mbt_ref, qw_ref, d_ref,
             du_ref, dy_ref, lb_ref, da_ref, dd_ref, lam, carry):
        @pl.when(pl.program_id(0) == 0)
        def _():
            carry[...] = jnp.zeros(carry.shape, F32)
            da_ref[...] = jnp.zeros(da_ref.shape, F32)
            dd_ref[...] = jnp.zeros(dd_ref.shape, F32)

        uv = u_ref[...]
        dy = dgy_ref[...] * _gelu_grad(y_ref[...])
        dyb = _bf(dy)
        dy_ref[...] = dyb
        dd_ref[...] += jnp.sum(dy * uv, axis=0, keepdims=True)
        for cs, s_re, s_im in _S5_BLOCKS:
            lam[:, s_re] = jnp.dot(dyb[:, cs], mct_ref[cs, s_re], preferred_element_type=F32)
            lam[:, s_im] = jnp.dot(dyb[:, cs], mct_ref[cs, s_im], preferred_element_type=F32)
        last_row = lax.broadcasted_iota(jnp.int32, (SUB, ns), 0) == SUB - 1

        def group(j, _):
            i = ng - 1 - j
            r0 = pl.multiple_of(i * SUB, SUB)
            lr = lam[pl.ds(r0, SUB), 0:ns]
            li = lam[pl.ds(r0, SUB), ns:2 * ns]
            for k, s in enumerate((1, 2, 4)):
                lr, li = _cmul_add(lr, li, qw_ref[k, 0], qw_ref[k, 1],
                                   pltpu.roll(lr, SUB - s, 0), pltpu.roll(li, SUB - s, 0))
            cr, ci = carry[0], carry[1]
            lr, li = _cmul_add(lr, li, qw_ref[3, 0], qw_ref[3, 1], cr, ci)
            lam[pl.ds(r0, SUB), 0:ns] = lr
            lam[pl.ds(r0, SUB), ns:2 * ns] = li
            carry[0] = jnp.broadcast_to(lr[0:1, :], (SUB, ns))
            carry[1] = jnp.broadcast_to(li[0:1, :], (SUB, ns))
            nr = jnp.where(last_row, cr, pltpu.roll(lr, SUB - 1, 0))
            ni = jnp.where(last_row, ci, pltpu.roll(li, SUB - 1, 0))
            xr = xs_ref[pl.ds(r0, SUB), 0:ns]
            xi = xs_ref[pl.ds(r0, SUB), ns:2 * ns]
            da_ref[0] += nr * xr + ni * xi
            da_ref[1] += ni * xr - nr * xi
            return 0
        lax.fori_loop(0, ng, group, 0)

        lb_ref[...] = _bf(lam[...])
        for cs, s_re, s_im in _S5_BLOCKS:
            du = (jnp.dot(lb_ref[:, s_re], mbt_ref[s_re, cs], preferred_element_type=F32)
                  + jnp.dot(lb_ref[:, s_im], mbt_ref[s_im, cs], preferred_element_type=F32) + d_ref[:, cs] * dy[:, cs])
            du_ref[:, cs] = du.astype(du_ref.dtype)

    rev = lambda i: (nt - 1 - i, 0)
    return pl.pallas_call(
        body, grid=(nt,),
        in_specs=[pl.BlockSpec((tm, c), rev), pl.BlockSpec((tm, c), rev), pl.BlockSpec((tm, c), rev),
                  pl.BlockSpec((tm, 2 * ns), rev),
                  pl.BlockSpec(mct.shape, lambda i: (0, 0)), pl.BlockSpec(mbt.shape, lambda i: (0, 0)),
                  pl.BlockSpec(qw.shape, lambda i: (0, 0, 0, 0)), pl.BlockSpec((1, c), lambda i: (0, 0))],
        out_specs=[pl.BlockSpec((tm, c), rev), pl.BlockSpec((tm, c), rev), pl.BlockSpec((tm, 2 * ns), rev),
                   pl.BlockSpec((2, SUB, ns), lambda i: (0, 0, 0)), pl.BlockSpec((1, c), lambda i: (0, 0))],
        out_shape=[S((t, c), BF16), S((t, c), BF16), S((t, 2 * ns), BF16), S((2, SUB, ns), F32), S((1, c), F32)],
        scratch_shapes=[pltpu.VMEM((tm, 2 * ns), F32), pltpu.VMEM((2, SUB, ns), F32)],
        compiler_params=_cp("arbitrary"), name=name)(dgy, y, u, xs, mct, mbt, qw, dskip)


def _first(accs, *_):
    return [accs[0]]


def _rms_bwd_epi(accs, xv, base, rv, g):
    dv = accs[0]
    w = dv * g
    xh = xv * rv
    dx = base + rv * (w - xh * jnp.mean(w * xh, axis=-1, keepdims=True))
    return [dx, dx, jnp.sum(dv * xh, axis=0, keepdims=True)]


def mm_rms_bwd(name, pairs, x, r, gain, dres):
    t, d = x.shape
    return mm_nn(name, t, d, pairs, 1, _rms_bwd_epi, [F32, BF16], tiled=[x, dres], cols=[r], rowv=[gain], sums=[(1, d)])


def _add_res(accs, res):
    return [accs[0] + res]


def even_fwd(x, w, need_out):
    t = x.shape[0]
    proj, hn, r = mm_nn("e_in_f", t, IN_WIDTH, [(x, w["e_w_in_t"], 0, "t")], 1, _first, [F32], norm_gain=w["e_norm"])
    out_a = gmlp_fwd("e_gmlp_f", proj, w["e_gmlp_w"], w["e_gmlp_b"])
    hc = conv_fwd("e_conv_f", proj, w["e_conv_w"], w["e_conv_b"])
    out_b = ln_silu_fwd("e_ln_f", hc, w["e_conv_ln_g"], w["e_conv_ln_b"])
    need_out(out_b)
    (x1,) = mm_nn("e_out_f", t, D_MODEL, [(out_a, (w["e_w_out"], 0), 0), (out_b, (w["e_w_out"], 1), 0)],
                  1, _add_res, [F32], tiled=[x])
    return x1, (x, hn, r, proj, out_a, hc, out_b)


def even_bwd_mixers(dxb, saved, w):
    x, hn, r, proj, out_a, hc, out_b = saved
    t = x.shape[0]
    (dcat,) = mm_nn("e_out_b", t, D_MODEL, [(dxb, w["e_w_out"], 0, "t")], 1, _first, [F32])
    g_w_out = jnp.concatenate([mm_tn("e_out_wa", out_a, dxb), mm_tn("e_out_wb", out_b, dxb)], axis=0)
    dab, g_gw, g_gb = gmlp_bwd("e_gmlp_b", proj, dcat, w["e_gmlp_w"], w["e_gmlp_b"])
    dhc, g_lg, g_lb = ln_silu_bwd("e_ln_b", hc, dcat, w["e_conv_ln_g"], w["e_conv_ln_b"])
    dba, dbg, g_cw, g_cb = conv_bwd("e_conv_b", proj, dhc, w["e_conv_w"])
    g_w_in_t = jnp.concatenate([mm_tn("e_in_w0", dab, hn), mm_tn("e_in_w1", dba, hn), mm_tn("e_in_w2", dbg, hn)], axis=0)
    grads = dict(e_w_in_t=g_w_in_t, e_gmlp_w=g_gw[None], e_gmlp_b=g_gb.reshape(1, A_GROUPS, GMLP_BLOCK),
                 e_conv_w=g_cw[None], e_conv_b=g_cb, e_conv_ln_g=g_lg, e_conv_ln_b=g_lb, e_w_out=g_w_out)
    return (dab, dba, dbg), grads


def even_bwd_input(dx, dproj, saved, w):
    x, _, r = saved[:3]
    dab, dba, dbg = dproj
    w_in_t = w["e_w_in_t"]
    return mm_rms_bwd("e_in_b", [(dab, (w_in_t, 0), 0), (dba, (w_in_t, 2), 0), (dbg, (w_in_t, 3), 0)], x, r, w["e_norm"], dx)


def s5_setup(w, anchor=None):
    def rows(v):
        return v.transpose(0, 2, 1).reshape(_RP)

    log_dt = w["o_log_dt"].reshape(C_GROUPS, 1)
    if anchor is not None:
        log_dt = log_dt + anchor
    lam = (w["o_lam_re"], w["o_lam_im"], log_dt, rows(w["o_b_re"]), rows(w["o_b_im"]))
    a, bbr, bbi = s5_discretise("o_s5_zoh", *lam)
    c_re, c_im = w["o_c_re"], w["o_c_im"]
    pw, qw, mb, mc, mct = s5_operands("o_s5_ops", a.reshape(2, N_STATE), bbr, bbi, c_re.reshape(_RP), c_im.reshape(_RP),
                                      c_re.transpose(2, 0, 1).reshape(C_STATE, C_WIDTH),
                                      c_im.transpose(2, 0, 1).reshape(C_STATE, C_WIDTH))
    return dict(lam=lam, pw=pw, qw=qw, mb=mb, mc=mc, mct=mct, mbt=mb.T)


def odd_fwd(x, w, consts):
    t = x.shape[0]
    u, hn, r = mm_nn("o_in_f", t, C_WIDTH, [(x, w["o_w_in"], 0)], 1, _first, [F32], norm_gain=w["o_norm"])
    gy, y, xs, xsb = s5_fwd("o_s5_f", u, consts["mb"], consts["mc"], consts["pw"], w["o_d"])
    w_out_t = w["o_w_out_t"]

    def epi(accs, res):
        return [res + accs[0] * _sigmoid(accs[1]), accs[0], accs[1]]

    x1, o1, o2 = mm_nn("o_out_f", t, D_MODEL, [(gy, (w_out_t, 0), 0, "t"), (gy, (w_out_t, D_MODEL), 1, "t")], 2, epi,
                       [F32, BF16, BF16], tiled=[x])
    return x1, (x, hn, r, u, gy, y, xs, xsb, o1, o2)


def odd_bwd(dx, dxb, saved, w, consts):
    x, hn, r, u, gy, y, xs, xsb, o1, o2 = saved
    t = x.shape[0]

    def gate_bwd(dv, a, b):
        a = a.astype(F32)
        sg = _sigmoid(b.astype(F32))
        return [jnp.concatenate([dv * sg, dv * a * sg * (1.0 - sg)], axis=1)], []

    (do12,) = rows_call("o_gate_b", gate_bwd, [dx, o1, o2], [], [(2 * D_MODEL, BF16)], [])
    (dgy,) = mm_nn("o_out_b", t, C_WIDTH, [(do12, w["o_w_out_t"], 0)], 1, _first, [F32])
    g_w_out_t = mm_tn("o_out_w", do12, gy)
    du, dyb, lamb, da8, g_d = s5_bwd("o_s5_b", dgy, y, u, xs, consts["mct"], consts["mbt"], consts["qw"], w["o_d"])
    d_mb, d_mc = s5_block_grads("o_s5_w", u, lamb, xsb, dyb)
    da = jnp.sum(da8, axis=1).reshape((2,) + _GP)
    g_lr, g_li, g_dt, g_btr, g_bti, g_cr, g_ci = s5_param_grads("o_s5_pg", d_mb, d_mc, da, *consts["lam"])

    def states_first(v):
        return v.reshape(C_GROUPS, C_GROUP_CH, C_STATE).transpose(0, 2, 1)[None]

    g_w_in = mm_tn("o_in_w", hn, du)
    dx0, dx0b, g_norm = mm_rms_bwd("o_in_b", [(du, w["o_w_in"], 0, "t")], x, r, w["o_norm"], dx)
    grads = dict(o_norm=g_norm, o_w_in=g_w_in, o_lam_re=g_lr[None], o_lam_im=g_li[None], o_log_dt=g_dt.reshape(1, C_GROUPS),
                 o_b_re=states_first(g_btr), o_b_im=states_first(g_bti),
                 o_c_re=g_cr.reshape((1, C_GROUPS, C_GROUP_CH, C_STATE)), o_c_im=g_ci.reshape((1, C_GROUPS, C_GROUP_CH, C_STATE)),
                 o_d=g_d, o_w_out_t=g_w_out_t)
    return dx0, dx0b, grads


def ca_fwd(i, x, mem, w):
    t, m = x.shape[0], mem.shape[0]
    q, xn, r = mm_nn(f"ca{i}_q_f", t, D_MODEL, [(x, w["ca_wq"][i], 0)], 1, _first, [BF16], norm_gain=w["ca_norm"][i:i + 1])
    k, v, mn, rm = mm_nn(f"ca{i}_kv_f", m, D_MODEL, [(mem, w["ca_wk"][i], 0), (mem, w["ca_wv"][i], 1)], 2,
                         lambda accs: [accs[0], accs[1]], [BF16, BF16], norm_gain=w["ca_mem_norm"][i:i + 1])
    o = attn_fwd(f"ca{i}_attn_f", q, k, v)
    (x1,) = mm_nn(f"ca{i}_o_f", t, D_MODEL, [(o, w["ca_wo"][i], 0)], 1, _add_res, [F32], tiled=[x])
    return x1, (x, xn, r, mn, rm, q, k, v, o)


def ca_bwd(i, dx, dxb, saved, mem, w):
    x, xn, r, mn, rm, q, k, v, o = saved
    t, m = x.shape[0], mem.shape[0]
    (do,) = mm_nn(f"ca{i}_o_b", t, D_MODEL, [(dxb, w["ca_wo"][i], 0, "t")], 1, _first, [BF16])
    g_wo = mm_tn(f"ca{i}_o_w", o, dxb)
    dq, dk, dv = attn_bwd(f"ca{i}_attn_b", q, k, v, do)
    g_wq = mm_tn(f"ca{i}_q_w", xn, dq)
    g_wk = mm_tn(f"ca{i}_k_w", mn, dk)
    g_wv = mm_tn(f"ca{i}_v_w", mn, dv)
    (dmn,) = mm_nn(f"ca{i}_kv_b", m, D_MODEL, [(dk, w["ca_wk"][i], 0, "t"), (dv, w["ca_wv"][i], 0, "t")], 1, _first, [F32])
    g_mnorm = rms_bwd_gain_only(f"ca{i}_mnorm_b", dmn, mem, rm)
    dx0, dx0b, g_norm = mm_rms_bwd(f"ca{i}_q_b", [(dq, w["ca_wq"][i], 0, "t")], x, r, w["ca_norm"][i:i + 1], dx)
    return dx0, dx0b, dict(ca_norm=g_norm, ca_mem_norm=g_mnorm, ca_wq=g_wq, ca_wk=g_wk, ca_wv=g_wv, ca_wo=g_wo)


def ffn_fwd(i, x, w):
    t = x.shape[0]
    def epi(accs):
        g, u = accs
        return [g, u, g * _sigmoid(g) * u]

    g, u, h, xn, r = mm_nn(f"ffn{i}_up_f", t, FFN_HIDDEN, [(x, w["ffn_w_gate_t"][i], 0, "t"), (x, w["ffn_w_up_t"][i], 1, "t")],
                           2, epi, [BF16, BF16, BF16], norm_gain=w["ffn_norm"][i:i + 1])
    (x1,) = mm_nn(f"ffn{i}_down_f", t, D_MODEL, [(h, w["ffn_w_down"][i], 0)], 1, _add_res, [F32], tiled=[x])
    return x1, (x, xn, r, g, u, h)


def ffn_bwd(i, dx, dxb, saved, w):
    x, xn, r, g, u, h = saved
    t = x.shape[0]

    def epi(accs, gv, uv):
        dh = accs[0]
        gv = gv.astype(F32)
        uv = uv.astype(F32)
        s = _sigmoid(gv)
        return [dh * uv * s * (1.0 + gv * (1.0 - s)), dh * gv * s]

    dg, du = mm_nn(f"ffn{i}_down_b", t, FFN_HIDDEN, [(dxb, w["ffn_w_down"][i], 0, "t")], 1, epi, [BF16, BF16], tiled=[g, u])
    g_wd = mm_tn(f"ffn{i}_down_w", h, dxb)
    g_wg_t = mm_tn(f"ffn{i}_gate_w", dg, xn)
    g_wu_t = mm_tn(f"ffn{i}_up_w", du, xn)
    dx0, dx0b, g_norm = mm_rms_bwd(f"ffn{i}_up_b", [(dg, w["ffn_w_gate_t"][i], 0), (du, w["ffn_w_up_t"][i], 0)], x, r,
                                   w["ffn_norm"][i:i + 1], dx)
    return dx0, dx0b, dict(ffn_norm=g_norm, ffn_w_gate_t=g_wg_t, ffn_w_up_t=g_wu_t, ffn_w_down=g_wd)


def local_step(x, mem, target, w, fetch=None, on_grads=None, anchor=None):
    consts = s5_setup(w, anchor)

    def need(stage, after):
        if fetch is not None:
            for k, v in fetch(stage, after).items():
                if isinstance(k, tuple):
                    w.setdefault(k[0], {})[k[1]] = v
                else:
                    w[k] = v

    need(0, consts["pw"])
    x1, s_e = even_fwd(x, w, lambda after: need(1, after))
    x2, s_c0 = ca_fwd(0, x1, mem, w)
    need(2, x2)
    x3, s_f0 = ffn_fwd(0, x2, w)
    x4, s_o = odd_fwd(x3, w, consts)
    need(3, x4)
    x5, s_c1 = ca_fwd(1, x4, mem, w)
    x6, s_f1 = ffn_fwd(1, x5, w)
    dx, dxb, g_final, loss = final_loss("final_loss", x6, w["final_norm"], target)

    def emit(stage, carry, plain, layered=None, layer=0):
        if on_grads is None:
            return carry
        out = dict(plain)
        out.update({(k, layer): v for k, v in (layered or {}).items()})
        return on_grads(stage, out, list(carry))

    dx, dxb, g_f1 = ffn_bwd(1, dx, dxb, s_f1, w)
    dx, dxb = emit(0, (dx, dxb), {}, g_f1, 1)
    dx, dxb, g_c1 = ca_bwd(1, dx, dxb, s_c1, mem, w)
    dx, dxb, g_o = odd_bwd(dx, dxb, s_o, w, consts)
    dx, dxb = emit(1, (dx, dxb), g_o, g_c1, 1)
    dx, dxb, g_f0 = ffn_bwd(0, dx, dxb, s_f0, w)
    dx, dxb = emit(2, (dx, dxb), {}, g_f0, 0)
    dx, dxb, g_c0 = ca_bwd(0, dx, dxb, s_c0, mem, w)
    dx, dxb = emit(3, (dx, dxb), {}, g_c0, 0)
    dproj, g_e = even_bwd_mixers(dxb, s_e, w)
    dproj = emit(4, dproj, {**g_e, "o_norm": g_o["o_norm"], "o_d": g_o["o_d"]})
    dx, dxb, g_e["e_norm"] = even_bwd_input(dx, dproj, s_e, w)

    grads = dict(g_e)
    grads.update(g_o)
    for g0, g1 in ((g_c0, g_c1), (g_f0, g_f1)):
        for k in g0:
            grads[k] = jnp.concatenate([g0[k], g1[k]], axis=0) if k.endswith("norm") else (g0[k], g1[k])
    grads["final_norm"] = g_final
    return loss, dx, grads


def _group(axes):
    pos = {a: lax.axis_index(a) for a in ("x", "y", "c")}
    me = 0
    for a in axes:
        me = me * 2 + pos[a]
    peers = []
    for mask in range(1, 2 ** len(axes)):
        peer = dict(pos)
        for bit, a in enumerate(axes):
            if (mask >> (len(axes) - 1 - bit)) & 1:
                peer[a] = 1 - pos[a]
        idx = 0
        for a in axes:
            idx = idx * 2 + peer[a]
        peers.append((idx, (peer["x"], peer["y"], peer["c"])))
    return me, peers


def _sibling():
    x, y, c = lax.axis_index("x"), lax.axis_index("y"), lax.axis_index("c")
    return c, (x, y, 1 - c)


_HBM =pl.BlockSpec(memory_space=pltpu.HBM)
_SEM = pl.BlockSpec(memory_space=pltpu.SEMAPHORE)
_EFFECT = pltpu.SideEffectType.DATAFLOW_SIDE_EFFECTING


def _gather_peers(direct):
    chip, _ = _group(("x", "y"))
    core = lax.axis_index("c")
    if direct:
        _, peers = _group(_ALL)
        return chip, core, [(idx // 2, idx % 2, dev) for idx, dev in peers]
    _, peers = _group(("x", "y"))
    return chip, core, [(idx, core, dev) for idx, dev in peers]


def gather_ici_start(name, groups, direct):
    flat = [b for g in groups for b in g]
    sizes = [len(g) for g in groups]
    k_ops, n_g = len(flat), len(groups)
    lands = [lax.empty((4, 2) + tuple(b.shape), b.dtype) for b in flat]
    fan = [N_DEV - 1 if d else 3 for d in direct]

    def body(*refs):
        src, land = refs[:k_ops], refs[k_ops:2 * k_ops]
        sems = refs[2 * k_ops:2 * k_ops + 3 * n_g]
        token = refs[-1]
        i = 0
        for g in range(n_g):
            send, recv, loc = sems[3 * g:3 * g + 3]
            chip, core, peers = _gather_peers(direct[g])
            for j in range(sizes[g]):
                pltpu.make_async_copy(src[i], land[i].at[chip, core], loc.at[j]).start()
                for k, (_, _, dev) in enumerate(peers):
                    s = fan[g] * j + k
                    pltpu.make_async_remote_copy(src_ref=src[i], dst_ref=land[i].at[chip, core], send_sem=send.at[s],
                                                 recv_sem=recv.at[s], device_id=dev, device_id_type=MESH).start()
                i += 1
        token[...] = jnp.zeros(token.shape, token.dtype)

    sem_shapes = []
    for s, f in zip(sizes, fan):
        sem_shapes += [pltpu.SemaphoreType.DMA((f * s,)), pltpu.SemaphoreType.DMA((f * s,)), pltpu.SemaphoreType.DMA((s,))]
    thru = [pltpu.HBM(a.shape, a.dtype) for a in flat + lands]
    outs = pl.pallas_call(
        body, name=name, out_shape=tuple(sem_shapes) + tuple(thru) + (S((8, LANES), F32),),
        in_specs=[_HBM] * (2 * k_ops), out_specs=[_SEM] * (3 * n_g) + [_HBM] * (2 * k_ops) + [pl.BlockSpec(memory_space=pltpu.VMEM)],
        input_output_aliases={i: 3 * n_g + i for i in range(2 * k_ops)},
        compiler_params=pltpu.CompilerParams(has_side_effects=_EFFECT),
    )(*[pltpu.with_memory_space_constraint(a, pltpu.HBM) for a in flat + lands])
    sems = [tuple(outs[3 * g:3 * g + 3]) for g in range(n_g)]
    srcs_thru, lands_thru, off = [], [], 3 * n_g
    for s in sizes:
        srcs_thru.append(list(outs[off:off + s]))
        off += s
    for s in sizes:
        lands_thru.append(list(outs[off:off + s]))
        off += s
    return sems, srcs_thru, lands_thru, outs[-1]


def gather_ici_wait(name, srcs, lands, sems, after, direct=False):
    n = len(srcs)

    def body(*refs):
        src, land = refs[:n], refs[n:2 * n]
        send, recv, loc = refs[2 * n:2 * n + 3]
        chip, core, peers = _gather_peers(direct)
        for j in range(n):
            for k, (pchip, pcore, dev) in enumerate(peers):
                s = len(peers) * j + k
                cp = pltpu.make_async_remote_copy(src_ref=src[j], dst_ref=land[j].at[pchip, pcore], send_sem=send.at[s],
                                                  recv_sem=recv.at[s], device_id=dev, device_id_type=MESH)
                cp.wait_send()
                cp.wait_recv()
            pltpu.make_async_copy(src[j], land[j].at[chip, core], loc.at[j]).wait()

    outs = pl.pallas_call(
        body, name=name, out_shape=tuple(pltpu.HBM(a.shape, a.dtype) for a in list(srcs) + list(lands)),
        in_specs=[_HBM] * (2 * n) + [_SEM] * 3 + [ANY], out_specs=[_HBM] * (2 * n),
        input_output_aliases={i: i for i in range(2 * n)},
        compiler_params=pltpu.CompilerParams(has_side_effects=_EFFECT),
    )(*srcs, *lands, *sems, after)
    return list(outs[n:])


def gather_d2d(name, bufs):
    k_ops = len(bufs)

    def body(*refs):
        in_refs, out_refs = refs[:k_ops], refs[k_ops:2 * k_ops]
        send_sems, recv_sems = refs[2 * k_ops:]
        core, sib = _sibling()
        sent, landed = [], []
        for i in range(k_ops):
            cp = pltpu.make_async_remote_copy(src_ref=in_refs[i].at[:, core], dst_ref=out_refs[i].at[:, core],
                                              send_sem=send_sems.at[i], recv_sem=recv_sems.at[i], device_id=sib, device_id_type=MESH)
            cp.start()
            sent.append(cp)
            landed.append(pltpu.make_async_remote_copy(src_ref=in_refs[i].at[:, core], dst_ref=out_refs[i].at[:, 1 - core],
                                                       send_sem=send_sems.at[i], recv_sem=recv_sems.at[i],
                                                       device_id=sib, device_id_type=MESH))
        for cp in landed:
            cp.wait_recv()
        for cp in sent:
            cp.wait_send()

    return pl.pallas_call(
        body, in_specs=[ANY] * k_ops, out_specs=[ANY] * k_ops, out_shape=[S(b.shape, b.dtype) for b in bufs],
        input_output_aliases={i: i for i in range(k_ops)},
        scratch_shapes=[pltpu.SemaphoreType.DMA((k_ops,)), pltpu.SemaphoreType.DMA((k_ops,))],
        name=name)(*bufs)


_ALL = ("x", "y", "c")


def scatter_start(name, arr, carry):
    land = lax.empty(arr.shape, arr.dtype)
    n_c = len(carry)

    def body(*refs):
        in_ref, land_ref = refs[0], refs[1]
        send, recv = refs[2 + n_c], refs[3 + n_c]
        me, peers = _group(_ALL)
        for k, (idx, dev) in enumerate(peers):
            pltpu.make_async_remote_copy(src_ref=in_ref.at[idx], dst_ref=land_ref.at[me], send_sem=send.at[k], recv_sem=recv.at[k],
                                         device_id=dev, device_id_type=MESH).start()

    thru = [arr, land] + list(carry)
    outs = pl.pallas_call(
        body, name=name,
        out_shape=(pltpu.SemaphoreType.DMA((N_DEV - 1,)), pltpu.SemaphoreType.DMA((N_DEV - 1,)))
        + tuple(pltpu.HBM(a.shape, a.dtype) for a in thru),
        in_specs=[_HBM] * len(thru), out_specs=[_SEM, _SEM] + [_HBM] * len(thru),
        input_output_aliases={i: 2 + i for i in range(len(thru))},
        compiler_params=pltpu.CompilerParams(has_side_effects=_EFFECT),
    )(*[pltpu.with_memory_space_constraint(a, pltpu.HBM) for a in thru])
    return (outs[0], outs[1]), outs[2], outs[3], list(outs[4:])


def scatter_wait(name, arr, land, sems, after):
    def body(in_ref, land_ref, send, recv, after_ref, in_thru, land_thru):
        _, peers = _group(_ALL)
        for k, (idx, dev) in enumerate(peers):
            cp = pltpu.make_async_remote_copy(src_ref=in_ref.at[idx], dst_ref=land_ref.at[idx], send_sem=send.at[k],
                                              recv_sem=recv.at[k], device_id=dev, device_id_type=MESH)
            cp.wait_send()
            cp.wait_recv()

    outs = pl.pallas_call(
        body, name=name, out_shape=(pltpu.HBM(arr.shape, arr.dtype), pltpu.HBM(arr.shape, arr.dtype)),
        in_specs=[_HBM, _HBM, _SEM, _SEM, ANY], out_specs=[_HBM, _HBM], input_output_aliases={0: 0, 1: 1},
        compiler_params=pltpu.CompilerParams(has_side_effects=_EFFECT),
    )(arr, land, sems[0], sems[1], after)
    return outs[0], outs[1]


def _row_tile(rows, cap=512):
    return next(t for t in range(cap - cap % 16, 0, -16) if rows % t == 0)


def sum_shares(name, own, recv, me):
    n, rows, c = recv.shape
    tr = _row_tile(rows)

    def body(me_ref, *refs):
        acc = refs[0][...].astype(F32)
        for r in refs[1:n]:
            acc = acc + r[...].astype(F32)
        refs[n][...] = acc

    def slot(mask):
        return pl.BlockSpec((None, tr, c), lambda i, me, mask=mask: (jnp.bitwise_xor(me[0], mask), i, 0))

    spec = pltpu.PrefetchScalarGridSpec(
        num_scalar_prefetch=1, grid=(rows // tr,), in_specs=[slot(k) for k in range(n)],
        out_specs=pl.BlockSpec((tr, c), lambda i, me: (i, 0)))
    return pl.pallas_call(body, grid_spec=spec, out_shape=S((rows, c), F32),
                          compiler_params=_cp("parallel"), name=name)(me, own, *([recv] * (n - 1)))


def sum_slots(name, slots):
    n, r, c = slots.shape

    def body(s_ref, o_ref):
        acc = s_ref[0]
        for j in range(1, n):
            acc = acc + s_ref[j]
        o_ref[...] = acc

    return pl.pallas_call(body, out_shape=S((r, c), F32), compiler_params=pltpu.CompilerParams(vmem_limit_bytes=VMEM_LIMIT),
                          name=name)(slots)


def adamw_native(name, g, w, m, v, tr=512):
    shape = w.shape
    cols = shape[-1]
    rows = w.size // cols
    tr = _tile(rows, tr) if rows % 8 == 0 else rows
    c1 = 1.0 - ADAM_B1 ** ADAM_STEP
    c2 = 1.0 - ADAM_B2 ** ADAM_STEP

    def body(g_ref, w_ref, m_ref, v_ref, d_ref, m2_ref, v2_ref):
        gv = g_ref[...]
        m2 = ADAM_B1 * m_ref[...] + (1.0 - ADAM_B1) * gv
        v2 = ADAM_B2 * v_ref[...] + (1.0 - ADAM_B2) * (gv * gv)
        m2_ref[...] = m2
        v2_ref[...] = v2
        d_ref[...] = -ADAM_LR * ((m2 / c1) / (jnp.sqrt(v2 / c2) + ADAM_EPS) + ADAM_WD * w_ref[...])

    row = pl.BlockSpec((tr, cols), lambda i: (i, 0))
    outs = pl.pallas_call(body, grid=(rows // tr,), in_specs=[row] * 4, out_specs=[row] * 3,
                          out_shape=[S((rows, cols), F32)] * 3, compiler_params=_cp("parallel"),
                          name=name)(*[a.reshape(rows, cols) for a in (g, w, m, v)])
    return tuple(o.reshape(shape) for o in outs)


_REPLICATED = ("e_norm", "e_gmlp_w", "e_gmlp_b", "e_conv_b", "e_conv_ln_g", "e_conv_ln_b", "o_lam_re", "o_lam_im", "o_log_dt",
               "o_b_re", "o_b_im", "o_c_re", "o_c_im", "ca_norm", "ca_mem_norm", "ffn_norm", "final_norm")
_ORDER = ("e_norm", "e_w_in", "e_gmlp_w", "e_gmlp_b", "e_conv_w", "e_conv_b", "e_conv_ln_g", "e_conv_ln_b", "e_w_out",
          "o_norm", "o_w_in", "o_lam_re", "o_lam_im", "o_log_dt", "o_b_re", "o_b_im", "o_c_re", "o_c_im", "o_d", "o_w_out",
          "ca_norm", "ca_mem_norm", "ca_wq", "ca_wk", "ca_wv", "ca_wo", "ffn_norm", "ffn_w_gate", "ffn_w_up", "ffn_w_down",
          "final_norm")


def _rows128(a, multiple=8):
    flat = a.reshape(-1)
    rows = -(-flat.shape[0] // (LANES * multiple)) * multiple
    return jnp.pad(flat, (0, rows * LANES - flat.shape[0])).reshape(rows, LANES)


def _shard(full, axis):
    s = full.shape
    return jnp.moveaxis(full.reshape(s[:axis] + (N_DEV, s[axis] // N_DEV) + s[axis + 1:]), axis, 0)


_UNITS = (("e_w_in", 0, True), ("e_w_out", 0, False), ("o_w_in", 0, False), ("o_w_out", 0, True),
          *[(n, i, False) for n in ("ca_wq", "ca_wk", "ca_wv", "ca_wo") for i in (0, 1)],
          *[(n, i, tr) for n, tr in (("ffn_w_gate", True), ("ffn_w_up", True), ("ffn_w_down", False)) for i in (0, 1)])
_LAYERED = ("ca_wq", "ca_wk", "ca_wv", "ca_wo", "ffn_w_gate", "ffn_w_up", "ffn_w_down")
_SMALL_SHARDED = (("e_conv_w", 2), ("o_norm", 1), ("o_d", 1))
RS_ROW = 1024


def _unit_key(name, tr):
    return name + "_t" if tr else name


def _stage_of(name, layer):
    if name.startswith("e_"):
        return 0 if name == "e_w_in" else 1
    if name.startswith("o_"):
        return 2
    if name.startswith("ca_"):
        return 1 if layer == 0 else 3
    return 2 if layer == 0 else 3


def weight_fetcher(local):
    groups, meta = [[] for _ in range(4)], [[] for _ in range(4)]
    for name, layer, tr in _UNITS:
        blk = local[name][layer]
        st = _stage_of(name, layer)
        groups[st].append(_bf(blk.T if tr else blk))
        meta[st].append((name, layer, tr))
    small = jnp.concatenate([local[name].reshape(-1) for name, _ in _SMALL_SHARDED])
    groups[0].append(_rows128(small))
    direct = [False, False, False, True]
    sems, srcs, lands, token = gather_ici_start("ag_w_start", groups, direct)

    def fetch(stage, after):
        bufs = gather_ici_wait(f"ag_w_wait{stage}", srcs[stage], lands[stage], sems[stage], after, direct[stage])
        if not direct[stage]:
            bufs = gather_d2d(f"ag_w_d2d{stage}", bufs)
        got = {}
        for (name, layer, tr), blk, buf in zip(meta[stage], groups[stage], bufs):
            arr = buf.reshape((N_DEV * blk.shape[0],) + tuple(blk.shape[1:]))
            if name in _LAYERED:
                got[(_unit_key(name, tr), layer)] = arr
            else:
                got[_unit_key(name, tr)] = arr
        if stage == 0:
            flat = bufs[-1].reshape(N_DEV, -1)
            off = 0
            for name, axis in _SMALL_SHARDED:
                blk = local[name]
                seg = flat[:, off:off + blk.size].reshape((N_DEV,) + blk.shape)
                off += blk.size
                seg = jnp.moveaxis(seg, 0, axis)
                got[name] = seg.reshape(seg.shape[:axis] + (-1,) + seg.shape[axis + 2:])
            got["e_conv_w"] = got["e_conv_w"][0]
        return got

    return fetch, token


def _grad_stage_of(name, layer):
    if name.startswith("e_"):
        return 4
    if name.startswith("o_"):
        return 1
    if name.startswith("ca_"):
        return 3 if layer == 0 else 1
    return 2 if layer == 0 else 0


GRAD_STAGES = 5
SMALL_ROWS = 16


def gradient_reducer(local, mom, var):
    me = (4 * lax.axis_index("x") + 2 * lax.axis_index("y") + lax.axis_index("c")).astype(jnp.int32).reshape(1)
    pending = []

    def start(stage, grads, carry):
        units = [u for u in _UNITS if _grad_stage_of(u[0], u[1]) == stage]
        parts, spans = [], []
        for name, layer, tr in units:
            key = _unit_key(name, tr)
            g = grads[(key, layer)] if name in _LAYERED else grads[key]
            part = g.reshape(4, 2, -1, RS_ROW)
            spans.append((part.shape[2], g.shape[0] // N_DEV, g.shape[1]))
            parts.append(part)
        if stage == GRAD_STAGES - 1:
            small = jnp.concatenate([_shard(grads[name], axis).reshape(N_DEV, -1) for name, axis in _SMALL_SHARDED], axis=1)
            small = jnp.pad(small, ((0, 0), (0, SMALL_ROWS * RS_ROW - small.shape[1])))
            parts.append(small.astype(BF16).reshape(4, 2, SMALL_ROWS, RS_ROW))
        pack = jnp.concatenate(parts, axis=2)
        pack = pack.reshape((N_DEV,) + pack.shape[2:])
        sems, own, land, carry = scatter_start(f"rs_start{stage}", pack, carry)
        pending.append((stage, units, spans, sems, own, land))
        return carry

    def finish(after):
        res, per_layer, small_flat = {}, {}, None
        for stage, units, spans, sems, own, land in pending:
            own, land = scatter_wait(f"rs_wait{stage}", own, land, sems, after)
            total = sum_shares(f"rs_sum{stage}", own, land, me)
            off = 0
            for (name, layer, tr), (rows, r, c) in zip(units, spans):
                g = total[off:off + rows].reshape(r, c)
                off += rows
                per_layer.setdefault(name, {})[layer] = g.T if tr else g
            if stage == GRAD_STAGES - 1:
                small_flat = total[off:off + SMALL_ROWS].reshape(-1)
        for name, by_layer in per_layer.items():
            g = jnp.stack([by_layer[i] for i in sorted(by_layer)]) if name in _LAYERED else by_layer[0][None]
            res[name] = (g,) + adamw_native("adamw_" + name, g, local[name], mom[name], var[name])
        off = 0
        for name, _ in _SMALL_SHARDED:
            blk = local[name]
            g = small_flat[off:off + blk.size].reshape(blk.shape)
            off += blk.size
            res[name] = (g,) + adamw_native("adamw_" + name, g, blk, mom[name], var[name])
        return res

    return start, finish


def replicated_start(grads, loss):
    pack = jnp.concatenate([_rows128(grads[name]) for name in _REPLICATED] + [_rows128(loss)], axis=0)
    sems, srcs, lands, token = gather_ici_start("ag_g_start", [[pack]], [False])
    return sems[0], srcs[0], lands[0], token


def replicated_finish(handle, after, w, mom, var):
    sems, srcs, lands, _ = handle
    (buf,) = gather_d2d("ag_g_d2d", gather_ici_wait("ag_g_wait", srcs, lands, sems, after))
    rows = srcs[0].shape[0]
    total = sum_slots("ag_g_sum", buf.reshape(N_DEV, rows, LANES))
    res, off = {}, 0
    for name in _REPLICATED:
        n = w[name].size
        nr = -(-n // (LANES * 8)) * 8
        g = total[off:off + nr].reshape(-1)[:n].reshape(w[name].shape)
        off += nr
        res[name] = (g,) + adamw_native("adamw_" + name, g, w[name], mom[name], var[name])
    return res, total[off, 0]


def kernel(x, mem, e_norm, e_w_in, e_gmlp_w, e_gmlp_b, e_conv_w, e_conv_b, e_conv_ln_g, e_conv_ln_b, e_w_out, o_norm, o_w_in, o_lam_re, o_lam_im, o_log_dt, o_b_re, o_b_im, o_c_re, o_c_im, o_d, o_w_out, ca_norm, ca_mem_norm, ca_wq, ca_wk, ca_wv, ca_wo, ffn_norm, ffn_w_gate, ffn_w_up, ffn_w_down, final_norm, loss_target, m_e_norm, m_e_w_in, m_e_gmlp_w, m_e_gmlp_b, m_e_conv_w, m_e_conv_b, m_e_conv_ln_g, m_e_conv_ln_b, m_e_w_out, m_o_norm, m_o_w_in, m_o_lam_re, m_o_lam_im, m_o_log_dt, m_o_b_re, m_o_b_im, m_o_c_re, m_o_c_im, m_o_d, m_o_w_out, m_ca_norm, m_ca_mem_norm, m_ca_wq, m_ca_wk, m_ca_wv, m_ca_wo, m_ffn_norm, m_ffn_w_gate, m_ffn_w_up, m_ffn_w_down, m_final_norm, v_e_norm, v_e_w_in, v_e_gmlp_w, v_e_gmlp_b, v_e_conv_w, v_e_conv_b, v_e_conv_ln_g, v_e_conv_ln_b, v_e_w_out, v_o_norm, v_o_w_in, v_o_lam_re, v_o_lam_im, v_o_log_dt, v_o_b_re, v_o_b_im, v_o_c_re, v_o_c_im, v_o_d, v_o_w_out, v_ca_norm, v_ca_mem_norm, v_ca_wq, v_ca_wk, v_ca_wv, v_ca_wo, v_ffn_norm, v_ffn_w_gate, v_ffn_w_up, v_ffn_w_down, v_final_norm):
    given = dict(locals())
    local = {k: given[k] for k in _ORDER}
    mom = {k: given["m_" + k] for k in _ORDER}
    var = {k: given["v_" + k] for k in _ORDER}

    w = {}
    w.update({
        "e_norm": e_norm, "e_gmlp_w": e_gmlp_w[0], "e_gmlp_b": e_gmlp_b.reshape(A_GROUPS, GMLP_BLOCK, 1),
        "e_conv_b": e_conv_b, "e_conv_ln_g": e_conv_ln_g, "e_conv_ln_b": e_conv_ln_b,
        "o_lam_re": o_lam_re[0], "o_lam_im": o_lam_im[0], "o_log_dt": o_log_dt[0], "o_b_re": o_b_re[0], "o_b_im": o_b_im[0],
        "o_c_re": o_c_re[0], "o_c_im": o_c_im[0], "ca_norm": ca_norm, "ca_mem_norm": ca_mem_norm, "ffn_norm": ffn_norm,
        "final_norm": final_norm.reshape(1, D_MODEL),
    })
    start_reduce, finish_reduce = gradient_reducer(local, mom, var)
    fetch, token = weight_fetcher(local)
    loss_part, grad_x, grads = local_step(x[0], mem[0], loss_target[0], w, fetch, start_reduce, token[0:1, 0:1])
    grads["final_norm"] = grads["final_norm"].reshape(D_MODEL)

    handle = replicated_start(grads, loss_part)
    res = finish_reduce(handle[3])
    rep, loss = replicated_finish(handle, res["ffn_w_down"][1], local, mom, var)
    res.update(rep)
    return (loss, grad_x[None], *[res[k][0] for k in _ORDER], *[res[k][1] for k in _ORDER],
            *[res[k][2] for k in _ORDER], *[res[k][3] for k in _ORDER])
```

```python
import jax
import jax.numpy as jnp
from jax import lax
from jax.experimental import pallas as pl
from jax.experimental.pallas import tpu as pltpu

F32 = jnp.float32
BF16 = jnp.bfloat16
S = jax.ShapeDtypeStruct

D_MODEL = 1024
A_WIDTH = 512
A_GROUPS = 4
GMLP_BLOCK = 128
CHUNK = 64
B_WIDTH = 512
IN_WIDTH = 2 * A_WIDTH + 2 * B_WIDTH
CONV_WIDTH = 31
CONV_PAD = 32
C_WIDTH = 512
C_GROUP_CH = 16
C_GROUPS = 32
C_STATE = 64
N_STATE = C_GROUPS * C_STATE
CA_HEADS = 4
CA_HEAD_DIM = 256
FFN_HIDDEN = 2816
EPS = 1e-6
ADAM_LR = 0.001
ADAM_B1 = 0.9
ADAM_B2 = 0.999
ADAM_EPS = 1e-08
ADAM_WD = 0.01
ADAM_STEP = 10
N_DEV = 8
LANES = 128
VMEM_LIMIT = 56 << 20
VMEM_BUDGET = 40 << 20
MM_TN_RESIDENT = 8 << 20
MESH = pl.DeviceIdType.MESH
ANY = pl.BlockSpec(memory_space=pl.ANY)


def _cp(*sem):
    return pltpu.CompilerParams(dimension_semantics=sem, vmem_limit_bytes=VMEM_LIMIT)


def _tile(n, pref):
    t = pref
    while n % t:
        t //= 2
    return t


def _bf(v):
    return v if v.dtype == BF16 else v.astype(BF16)


def _sigmoid(x):
    return 1.0 / (1.0 + jnp.exp(-x))


_GC = 0.7978845608028654


def _gelu(x):
    return 0.5 * x * (1.0 + jnp.tanh(_GC * (x + 0.044715 * x * x * x)))


def _gelu_grad(x):
    x2 = x * x
    t = jnp.tanh(_GC * (x + 0.044715 * x * x2))
    return 0.5 * (1.0 + t) + 0.5 * x * (1.0 - t * t) * _GC * (1.0 + 3.0 * 0.044715 * x2)


def _tspec(entry, tm):
    if isinstance(entry, tuple):
        arr, cb, width = entry
        return arr, pl.BlockSpec((tm, width), lambda i, cb=cb: (i, cb))
    return entry, pl.BlockSpec((tm, entry.shape[1]), lambda i: (i, 0))


def rows_call(name, fn, tiled, full, outs, accs, tm=256):
    pairs = [_tspec(e, tm) for e in tiled]
    arrs = [p[0] for p in pairs]
    rows = arrs[0].shape[0]
    tm = _tile(rows, tm)
    pairs = [_tspec(e, tm) for e in tiled]
    n_in = len(tiled) + len(full)
    n_out = len(outs)

    def body(*refs):
        vals = [r[...] for r in refs[:n_in]]
        o_refs = refs[n_in:n_in + n_out]
        a_refs = refs[n_in + n_out:]
        ov, av = fn(*vals)
        for r, v in zip(o_refs, ov):
            r[...] = v.astype(r.dtype)
        if a_refs:
            @pl.when(pl.program_id(0) == 0)
            def _():
                for r in a_refs:
                    r[...] = jnp.zeros(r.shape, r.dtype)
            for r, v in zip(a_refs, av):
                r[...] += v

    in_specs = [p[1] for p in pairs] + [pl.BlockSpec(a.shape, lambda i, nd=a.ndim: (0,) * nd) for a in full]
    out_specs = [pl.BlockSpec((tm, c), lambda i: (i, 0)) for c, _ in outs]
    out_specs += [pl.BlockSpec(s, lambda i, nd=len(s): (0,) * nd) for s in accs]
    out_shape = [S((rows, c), dt) for c, dt in outs] + [S(s, F32) for s in accs]
    return pl.pallas_call(body, grid=(rows // tm,), in_specs=in_specs, out_specs=out_specs, out_shape=out_shape,
                          compiler_params=_cp("arbitrary"), name=name)(*arrs, *full)


def mm_nn(name, m, n, pairs, n_acc, epi, outs, tiled=(), cols=(), rowv=(), sums=(), norm_gain=None):
    a_ops, a_slot, b_arrs, b_specs, idx, trans = [], [], [], [], [], []
    fixed = 0
    for pair in pairs:
        a, b, k = pair[:3]
        bt = len(pair) > 3
        arr, cb, kdim = a if isinstance(a, tuple) else (a, 0, a.shape[1])
        key = (id(arr), cb, kdim)
        if key not in [o[0] for o in a_ops]:
            a_ops.append((key, arr, cb, kdim))
        a_slot.append([o[0] for o in a_ops].index(key))
        b_arr, off = b if isinstance(b, tuple) else (b, 0)
        b_arrs.append(b_arr)
        if bt:
            assert off % n == 0 and b_arr.shape[1] == kdim
            b_specs.append(pl.BlockSpec((n, kdim), lambda i, o=off // n: (o, 0), pipeline_mode=pl.Buffered(1)))
        else:
            assert b_arr.shape[1] == n
            b_specs.append(pl.BlockSpec((kdim, n), lambda i, o=off: (o, 0), pipeline_mode=pl.Buffered(1)))
        fixed += kdim * n * b_arr.dtype.itemsize
        idx.append(k)
        trans.append(bt)
    per_row = sum(2 * kdim * arr.dtype.itemsize for _, arr, _, kdim in a_ops)
    per_row += sum(2 * n * t.dtype.itemsize for t in tiled) + sum(2 * n * jnp.dtype(dt).itemsize for dt in outs)
    cn = n if sums or cols else (512 if n % 512 == 0 else 256)
    per_row += (n_acc + 3) * cn * 4
    tm = next((t for t in (1024, 512, 256, 128) if m % t == 0 and fixed + t * per_row <= VMEM_BUDGET), _tile(m, 128))
    n_a, n_p, n_t = len(a_ops), len(pairs), len(tiled)
    n_in = n_a + n_p + n_t + len(cols) + len(rowv)
    normed = norm_gain is not None
    o0 = n_in + normed

    def body(*refs):
        a_vals = [None if normed and i == 0 else _bf(r[...]) for i, r in enumerate(refs[:n_a])]
        if normed:
            xv = refs[0][...]
            rv = lax.rsqrt(jnp.mean(xv * xv, axis=-1, keepdims=True) + EPS)
            a_vals[0] = (xv * rv * refs[n_in][...]).astype(BF16)
            refs[o0 + len(outs)][...] = a_vals[0]
            refs[o0 + len(outs) + 1][...] = rv
        for j in range(n // cn):
            cs = slice(j * cn, (j + 1) * cn)
            accs = [None] * n_acc
            for p in range(n_p):
                av, b_ref = a_vals[a_slot[p]], refs[n_a + p]
                if trans[p]:
                    d = lax.dot_general(av, _bf(b_ref[cs, :]), (((1,), (1,)), ((), ())), preferred_element_type=F32)
                else:
                    d = jnp.dot(av, _bf(b_ref[:, cs]), preferred_element_type=F32)
                accs[idx[p]] = d if accs[idx[p]] is None else accs[idx[p]] + d
            extra = [r[:, cs] for r in refs[n_a + n_p:n_a + n_p + n_t]] + [r[...] for r in refs[n_a + n_p + n_t:n_in - len(rowv)]]
            extra += [r[:, cs] for r in refs[n_in - len(rowv):n_in]]
            ov = epi(accs, *extra)
            for r, v in zip(refs[o0:o0 + len(outs)], ov):
                r[:, cs] = v.astype(r.dtype)
        sv = ov[len(outs):]
        if sums:
            s_refs = refs[o0 + len(outs) + 2 * normed:]

            @pl.when(pl.program_id(0) == 0)
            def _():
                for r in s_refs:
                    r[...] = jnp.zeros(r.shape, r.dtype)
            for r, v in zip(s_refs, sv):
                r[...] += v

    in_specs = [pl.BlockSpec((tm, kdim), lambda i, cb=cb: (i, cb)) for _, _, cb, kdim in a_ops] + b_specs
    in_specs += [pl.BlockSpec((tm, n), lambda i: (i, 0)) for _ in tiled]
    in_specs += [pl.BlockSpec((tm, 1), lambda i: (i, 0)) for _ in cols]
    in_specs += [pl.BlockSpec((1, n), lambda i: (0, 0)) for _ in rowv]
    out_specs = [pl.BlockSpec((tm, n), lambda i: (i, 0)) for _ in outs]
    out_shape = [S((m, n), dt) for dt in outs]
    gain = []
    if normed:
        k0 = a_ops[0][3]
        gain = [norm_gain]
        in_specs.append(pl.BlockSpec((1, k0), lambda i: (0, 0)))
        out_specs += [pl.BlockSpec((tm, k0), lambda i: (i, 0)), pl.BlockSpec((tm, 1), lambda i: (i, 0))]
        out_shape += [S((m, k0), BF16), S((m, 1), F32)]
    out_specs += [pl.BlockSpec(s, lambda i, nd=len(s): (0,) * nd) for s in sums]
    out_shape += [S(s, F32) for s in sums]
    return pl.pallas_call(body, grid=(m // tm,), in_specs=in_specs, out_specs=out_specs, out_shape=out_shape,
                          compiler_params=_cp("arbitrary" if sums else "parallel"),
                          name=name)(*[o[1] for o in a_ops], *b_arrs, *tiled, *cols, *rowv, *gain)


def mm_tn(name, a, b, out_dtype=BF16):
    if isinstance(a, tuple):
        a_arr, a_cb, m = a
    else:
        a_arr, a_cb, m = a, None, a.shape[1]
    if isinstance(b, tuple):
        b_arr, b_cb, n = b
    else:
        b_arr, b_cb, n = b, None, b.shape[1]
    t = a_arr.shape[0]
    whole_b = t * n * b_arr.dtype.itemsize <= MM_TN_RESIDENT and b_cb is None
    tn = n if whole_b else _tile(n, 512)
    tm = _tile(m, 512 if t * 512 * a_arr.dtype.itemsize * 2 + t * tn * b_arr.dtype.itemsize * 2 <= VMEM_BUDGET else 256)
    a_off = 0 if a_cb is None else a_cb * (m // tm)
    b_off = 0 if b_cb is None else b_cb * (n // tn)

    def body(a_ref, b_ref, o_ref):
        o_ref[...] = lax.dot_general(_bf(a_ref[...]), _bf(b_ref[...]), (((0,), (0,)), ((), ())),
                                     preferred_element_type=F32).astype(o_ref.dtype)

    if whole_b:
        b_spec = pl.BlockSpec((t, n), lambda i, j: (0, 0), pipeline_mode=pl.Buffered(1))
    else:
        b_spec = pl.BlockSpec((t, tn), lambda i, j: (0, j + b_off))
    return pl.pallas_call(
        body, grid=(m // tm, n // tn),
        in_specs=[pl.BlockSpec((t, tm), lambda i, j: (0, i + a_off)), b_spec],
        out_specs=pl.BlockSpec((tm, tn), lambda i, j: (i, j)), out_shape=S((m, n), out_dtype),
        compiler_params=_cp("parallel", "parallel"), name=name)(a_arr, b_arr)


def rms_bwd_gain_only(name, dxn, x, r):
    def fn(dv, xv, rv):
        return [], [jnp.sum(dv * xv * rv, axis=0, keepdims=True)]
    return rows_call(name, fn, [dxn, x, r], [], [], [(1, x.shape[1])])[0]


def _final_loss_epi(accs, res, tv, g):
    xv = res + accs[0]
    d = xv.shape[-1]
    r = lax.rsqrt(jnp.mean(xv * xv, axis=-1, keepdims=True) + EPS)
    xh = xv * r
    err = xh * g - tv
    dy = err * (1.0 / d)
    w = dy * g
    dx = r * (w - xh * jnp.mean(w * xh, axis=-1, keepdims=True))
    part = jnp.sum(jnp.sum(err * err, axis=-1, keepdims=True), axis=0, keepdims=True) * (0.5 / d)
    return [dx, dx, jnp.sum(dy * xh, axis=0, keepdims=True), part]


def _gmlp_mask():
    row = lax.broadcasted_iota(jnp.int32, (GMLP_BLOCK, GMLP_BLOCK), 0) // CHUNK
    col = lax.broadcasted_iota(jnp.int32, (GMLP_BLOCK, GMLP_BLOCK), 1) // CHUNK
    return col <= row


def _ln_plain(v):
    mu = jnp.mean(v, axis=-1, keepdims=True)
    vc = v - mu
    rstd = lax.rsqrt(jnp.mean(vc * vc, axis=-1, keepdims=True) + EPS)
    return vc * rstd, rstd


def gmlp_fwd(name, proj, w, b, tm=512):
    t = proj.shape[0]
    tm = _tile(t, tm)

    def body(au_ref, av_ref, w_ref, b_ref, o_ref):
        mask = _gmlp_mask()
        u = _gelu(au_ref[...])
        vn, _ = _ln_plain(_gelu(av_ref[...]))
        vnb = _bf(vn)
        for g in range(A_GROUPS):
            wg = _bf(jnp.where(mask, w_ref[g], 0.0))
            cs = slice(g * GMLP_BLOCK, (g + 1) * GMLP_BLOCK)
            for n in range(tm // GMLP_BLOCK):
                rs = slice(n * GMLP_BLOCK, (n + 1) * GMLP_BLOCK)
                sg = jnp.dot(wg, vnb[rs, cs], preferred_element_type=F32) + b_ref[g]
                o_ref[rs, cs] = (u[rs, cs] * sg).astype(o_ref.dtype)

    return pl.pallas_call(
        body, grid=(t // tm,),
        in_specs=[pl.BlockSpec((tm, A_WIDTH), lambda i: (i, 0)), pl.BlockSpec((tm, A_WIDTH), lambda i: (i, 1)),
                  pl.BlockSpec(w.shape, lambda i: (0, 0, 0)), pl.BlockSpec(b.shape, lambda i: (0, 0, 0))],
        out_specs=pl.BlockSpec((tm, A_WIDTH), lambda i: (i, 0)), out_shape=S((t, A_WIDTH), BF16),
        compiler_params=_cp("parallel"), name=name)(proj, proj, w, b)


def gmlp_bwd(name, proj, dcat, w, b, tm=512):
    t = proj.shape[0]
    tm = _tile(t, tm)

    def body(au_ref, av_ref, do_ref, w_ref, b_ref, dp_ref, dw_ref, db_ref):
        @pl.when(pl.program_id(0) == 0)
        def _():
            dw_ref[...] = jnp.zeros(dw_ref.shape, F32)
            db_ref[...] = jnp.zeros(db_ref.shape, F32)

        mask = _gmlp_mask()
        au = au_ref[...]
        av = av_ref[...]
        u = _gelu(au)
        vn, rstd = _ln_plain(_gelu(av))
        vnb = _bf(vn)
        dout = do_ref[...]
        dvn_cols = []
        for g in range(A_GROUPS):
            wm = jnp.where(mask, w_ref[g], 0.0)
            wg = _bf(wm)
            wgt = _bf(wm.T)
            cs = slice(g * GMLP_BLOCK, (g + 1) * GMLP_BLOCK)
            dwg = jnp.zeros((GMLP_BLOCK, GMLP_BLOCK), F32)
            dbg = jnp.zeros((GMLP_BLOCK, 1), F32)
            dvn_rows = []
            for n in range(tm // GMLP_BLOCK):
                rs = slice(n * GMLP_BLOCK, (n + 1) * GMLP_BLOCK)
                sg = jnp.dot(wg, vnb[rs, cs], preferred_element_type=F32) + b_ref[g]
                dp_ref[rs, cs] = (dout[rs, cs] * sg * _gelu_grad(au[rs, cs])).astype(dp_ref.dtype)
                dsg = dout[rs, cs] * u[rs, cs]
                dsgb = _bf(dsg)
                dbg = dbg + jnp.sum(dsg, axis=1, keepdims=True)
                dwg = dwg + lax.dot_general(dsgb, vnb[rs, cs], (((1,), (1,)), ((), ())), preferred_element_type=F32)
                dvn_rows.append(jnp.dot(wgt, dsgb, preferred_element_type=F32))
            dw_ref[g] += jnp.where(mask, dwg, 0.0)
            db_ref[g] += dbg
            dvn_cols.append(jnp.concatenate(dvn_rows, axis=0))
        dvn = jnp.concatenate(dvn_cols, axis=1)
        dv = rstd * (dvn - jnp.mean(dvn, axis=-1, keepdims=True) - vn * jnp.mean(dvn * vn, axis=-1, keepdims=True))
        dp_ref[:, A_WIDTH:] = (dv * _gelu_grad(av)).astype(dp_ref.dtype)

    return pl.pallas_call(
        body, grid=(t // tm,),
        in_specs=[pl.BlockSpec((tm, A_WIDTH), lambda i: (i, 0)), pl.BlockSpec((tm, A_WIDTH), lambda i: (i, 1)),
                  pl.BlockSpec((tm, A_WIDTH), lambda i: (i, 0)),
                  pl.BlockSpec(w.shape, lambda i: (0, 0, 0)), pl.BlockSpec(b.shape, lambda i: (0, 0, 0))],
        out_specs=[pl.BlockSpec((tm, 2 * A_WIDTH), lambda i: (i, 0)),
                   pl.BlockSpec(w.shape, lambda i: (0, 0, 0)), pl.BlockSpec(b.shape, lambda i: (0, 0, 0))],
        out_shape=[S((t, 2 * A_WIDTH), BF16), S(w.shape, F32), S(b.shape, F32)],
        compiler_params=_cp("arbitrary"), name=name)(proj, proj, dcat, w, b)


CONV_ROWS = 256


def conv_fwd(name, proj, w, cb):
    t = proj.shape[0]
    tc = LANES
    rows = _tile(t, CONV_ROWS)
    a_cb, g_cb = 2 * A_WIDTH // tc, (2 * A_WIDTH + B_WIDTH) // tc

    def body(a_ref, g_ref, w_ref, cb_ref, o_ref, hpad):
        hpad[0:CONV_PAD, :] = jnp.zeros((CONV_PAD, tc), F32)

        def fill(i, _):
            r0 = pl.multiple_of(i * rows, rows)
            hpad[pl.ds(CONV_PAD + r0, rows), :] = a_ref[pl.ds(r0, rows), :] * _sigmoid(g_ref[pl.ds(r0, rows), :])
            return 0
        lax.fori_loop(0, t // rows, fill, 0)

        def conv(i, _):
            r0 = pl.multiple_of(i * rows, rows)
            win = hpad[pl.ds(r0, rows + CONV_PAD), :]
            acc = jnp.zeros((rows, tc), F32) + cb_ref[...]
            for b in range(SUB):
                wb = win if b == 0 else pltpu.roll(win, b, 0)
                for a in range(CONV_PAD // SUB):
                    k = CONV_WIDTH - 1 - (SUB * a + b)
                    if k >= 0:
                        lo = CONV_PAD - SUB * a
                        acc = acc + wb[lo:lo + rows, :] * w_ref[k:k + 1, :]
            o_ref[pl.ds(r0, rows), :] = acc
            return 0
        lax.fori_loop(0, t // rows, conv, 0)

    return pl.pallas_call(
        body, grid=(B_WIDTH // tc,),
        in_specs=[pl.BlockSpec((t, tc), lambda j: (0, a_cb + j)), pl.BlockSpec((t, tc), lambda j: (0, g_cb + j)),
                  pl.BlockSpec((CONV_WIDTH, tc), lambda j: (0, j)), pl.BlockSpec((1, tc), lambda j: (0, j))],
        out_specs=pl.BlockSpec((t, tc), lambda j: (0, j)), out_shape=S((t, B_WIDTH), F32),
        scratch_shapes=[pltpu.VMEM((t + CONV_PAD, tc), F32)],
        compiler_params=_cp("parallel"), name=name)(proj, proj, w, cb)


def conv_bwd(name, proj, dhc, w):
    t = proj.shape[0]
    tc = LANES
    rows = _tile(t, CONV_ROWS)
    a_cb, g_cb = 2 * A_WIDTH // tc, (2 * A_WIDTH + B_WIDTH) // tc
    win_rows = rows + CONV_PAD

    def body(a_ref, g_ref, d_ref, w_ref, da_ref, dg_ref, dw_ref, dcb_ref, hpad, dpad, dwacc):
        hpad[0:CONV_PAD, :] = jnp.zeros((CONV_PAD, tc), F32)
        dpad[t:t + CONV_PAD, :] = jnp.zeros((CONV_PAD, tc), F32)
        dwacc[...] = jnp.zeros(dwacc.shape, F32)

        def fill(i, _):
            r0 = pl.multiple_of(i * rows, rows)
            hpad[pl.ds(CONV_PAD + r0, rows), :] = a_ref[pl.ds(r0, rows), :] * _sigmoid(g_ref[pl.ds(r0, rows), :])
            dpad[pl.ds(r0, rows), :] = d_ref[pl.ds(r0, rows), :]
            return 0
        lax.fori_loop(0, t // rows, fill, 0)

        def step(i, dcb):
            r0 = pl.multiple_of(i * rows, rows)
            hwin = hpad[pl.ds(r0, win_rows), :]
            dwin = dpad[pl.ds(r0, win_rows), :]
            dchunk = dwin[:rows, :]
            dh = jnp.zeros((rows, tc), F32)
            for b in range(SUB):
                hb = hwin if b == 0 else pltpu.roll(hwin, b, 0)
                db = dwin if b == 0 else pltpu.roll(dwin, win_rows - b, 0)
                for a in range(CONV_PAD // SUB):
                    k = CONV_WIDTH - 1 - (SUB * a + b)
                    if k >= 0:
                        dh = dh + db[SUB * a:SUB * a + rows, :] * w_ref[k:k + 1, :]
                        lo = CONV_PAD - SUB * a
                        prod = dchunk * hb[lo:lo + rows, :]
                        dwacc[k] += jnp.sum(prod.reshape(rows // 8, 8, tc), axis=0)
            a = a_ref[pl.ds(r0, rows), :]
            sg = _sigmoid(g_ref[pl.ds(r0, rows), :])
            da_ref[pl.ds(r0, rows), :] = (dh * sg).astype(da_ref.dtype)
            dg_ref[pl.ds(r0, rows), :] = (dh * a * sg * (1.0 - sg)).astype(dg_ref.dtype)
            return dcb + jnp.sum(dchunk, axis=0, keepdims=True)
        dcb = lax.fori_loop(0, t // rows, step, jnp.zeros((1, tc), F32))
        dcb_ref[...] = dcb
        for k in range(CONV_WIDTH):
            dw_ref[k:k + 1, :] = jnp.sum(dwacc[k], axis=0, keepdims=True)

    return pl.pallas_call(
        body, grid=(B_WIDTH // tc,),
        in_specs=[pl.BlockSpec((t, tc), lambda j: (0, a_cb + j)), pl.BlockSpec((t, tc), lambda j: (0, g_cb + j)),
                  pl.BlockSpec((t, tc), lambda j: (0, j)), pl.BlockSpec((CONV_WIDTH, tc), lambda j: (0, j))],
        out_specs=[pl.BlockSpec((t, tc), lambda j: (0, j)), pl.BlockSpec((t, tc), lambda j: (0, j)),
                   pl.BlockSpec((CONV_WIDTH, tc), lambda j: (0, j)), pl.BlockSpec((1, tc), lambda j: (0, j))],
        out_shape=[S((t, B_WIDTH), BF16), S((t, B_WIDTH), BF16), S((CONV_WIDTH, B_WIDTH), F32), S((1, B_WIDTH), F32)],
        scratch_shapes=[pltpu.VMEM((t + CONV_PAD, tc), F32), pltpu.VMEM((t + CONV_PAD, tc), F32),
                        pltpu.VMEM((CONV_WIDTH, 8, tc), F32)],
        compiler_params=_cp("parallel"), name=name)(proj, proj, dhc, w)


def ln_silu_fwd(name, hc, g, b):
    def fn(h, gv, bv):
        y, _ = _ln_plain(h)
        z = y * gv + bv
        return [z * _sigmoid(z)], []
    return rows_call(name, fn, [hc], [g, b], [(hc.shape[1], BF16)], [])[0]


def ln_silu_bwd(name, hc, dcat, g, b):
    c = hc.shape[1]

    def fn(h, dout, gv, bv):
        y, rstd = _ln_plain(h)
        z = y * gv + bv
        s = _sigmoid(z)
        dz = dout * s * (1.0 + z * (1.0 - s))
        dyv = dz * gv
        dh = rstd * (dyv - jnp.mean(dyv, axis=-1, keepdims=True) - y * jnp.mean(dyv * y, axis=-1, keepdims=True))
        return [dh], [jnp.sum(dz * y, axis=0, keepdims=True), jnp.sum(dz, axis=0, keepdims=True)]

    return rows_call(name, fn, [hc, (dcat, 1, c)], [g, b], [(c, F32)], [(1, c), (1, c)])


_NT = (((1,), (1,)), ((), ()))
_TN = (((0,), (0,)), ((), ()))


def attn_fwd(name, q, k, v, tm=512):
    t, d = q.shape
    m = k.shape[0]
    tm = _tile(t, tm)
    scale = CA_HEAD_DIM ** -0.5

    def body(q_ref, k_ref, v_ref, o_ref):
        for h in range(CA_HEADS):
            cs = slice(h * CA_HEAD_DIM, (h + 1) * CA_HEAD_DIM)
            s = lax.dot_general(q_ref[:, cs], k_ref[:, cs], _NT, preferred_element_type=F32) * scale
            e = jnp.exp(s - jnp.max(s, axis=-1, keepdims=True))
            p = e / jnp.sum(e, axis=-1, keepdims=True)
            o_ref[:, cs] = jnp.dot(_bf(p), v_ref[:, cs], preferred_element_type=F32).astype(o_ref.dtype)

    return pl.pallas_call(
        body, grid=(t // tm,),
        in_specs=[pl.BlockSpec((tm, d), lambda i: (i, 0)), pl.BlockSpec((m, d), lambda i: (0, 0)),
                  pl.BlockSpec((m, d), lambda i: (0, 0))],
        out_specs=pl.BlockSpec((tm, d), lambda i: (i, 0)), out_shape=S((t, d), BF16),
        compiler_params=_cp("parallel"), name=name)(q, k, v)


def attn_bwd(name, q, k, v, do, tm=512):
    t, d = q.shape
    m = k.shape[0]
    tm = _tile(t, tm)
    scale = CA_HEAD_DIM ** -0.5

    def body(q_ref, k_ref, v_ref, do_ref, dq_ref, dk_ref, dv_ref):
        @pl.when(pl.program_id(0) == 0)
        def _():
            dk_ref[...] = jnp.zeros(dk_ref.shape, F32)
            dv_ref[...] = jnp.zeros(dv_ref.shape, F32)

        for h in range(CA_HEADS):
            cs = slice(h * CA_HEAD_DIM, (h + 1) * CA_HEAD_DIM)
            qh, kh, vh, doh = q_ref[:, cs], k_ref[:, cs], v_ref[:, cs], do_ref[:, cs]
            s = lax.dot_general(qh, kh, _NT, preferred_element_type=F32) * scale
            e = jnp.exp(s - jnp.max(s, axis=-1, keepdims=True))
            p = e / jnp.sum(e, axis=-1, keepdims=True)
            pb = _bf(p)
            dv_ref[:, cs] += lax.dot_general(pb, doh, _TN, preferred_element_type=F32)
            dp = lax.dot_general(doh, vh, _NT, preferred_element_type=F32)
            ds = _bf(p * (dp - jnp.sum(dp * p, axis=-1, keepdims=True)) * scale)
            dq_ref[:, cs] = jnp.dot(ds, kh, preferred_element_type=F32).astype(dq_ref.dtype)
            dk_ref[:, cs] += lax.dot_general(ds, qh, _TN, preferred_element_type=F32)

    return pl.pallas_call(
        body, grid=(t // tm,),
        in_specs=[pl.BlockSpec((tm, d), lambda i: (i, 0)), pl.BlockSpec((m, d), lambda i: (0, 0)),
                  pl.BlockSpec((m, d), lambda i: (0, 0)), pl.BlockSpec((tm, d), lambda i: (i, 0))],
        out_specs=[pl.BlockSpec((tm, d), lambda i: (i, 0)), pl.BlockSpec((m, d), lambda i: (0, 0)),
                   pl.BlockSpec((m, d), lambda i: (0, 0))],
        out_shape=[S((t, d), BF16), S((m, d), F32), S((m, d), F32)],
        compiler_params=_cp("arbitrary"), name=name)(q, k, v, do)


SUB = 8
S5_ROWS = 256


S5_BLOCKS = 4
BLOCK_CH = C_WIDTH // S5_BLOCKS
BLOCK_ST = N_STATE // S5_BLOCKS
_S5_BLOCKS = tuple((slice(BLOCK_CH * q, BLOCK_CH * (q + 1)), slice(BLOCK_ST * q, BLOCK_ST * (q + 1)),
                    slice(N_STATE + BLOCK_ST * q, N_STATE + BLOCK_ST * (q + 1))) for q in range(S5_BLOCKS))
_HI = lax.Precision.HIGHEST
_GP = (C_GROUPS, C_STATE)
_RP = (C_WIDTH, C_STATE)


def _zoh(lr, li, ldt):
    dt = jnp.exp(ldt)
    mag = jnp.exp(lr * dt)
    ar = mag * jnp.cos(li * dt)
    ai = mag * jnp.sin(li * dt)
    den = lr * lr + li * li
    qr = ((ar - 1.0) * lr + ai * li) / den
    qi = (ai * lr - (ar - 1.0) * li) / den
    return dt, ar, ai, den, qr, qi


def _per_channel(v):
    return jnp.broadcast_to(v[:, None, :], (C_GROUPS, C_GROUP_CH, C_STATE)).reshape(_RP)


def _same_group(shape, row_per_group, col_per_group):
    rows = lax.broadcasted_iota(jnp.int32, shape, 0) // row_per_group
    cols = lax.broadcasted_iota(jnp.int32, shape, 1) // col_per_group
    return rows == cols


def _spread(shape, axis):
    long = lax.broadcasted_iota(jnp.int32, shape, axis) % C_STATE
    short = lax.broadcasted_iota(jnp.int32, shape, 1 - axis)
    return long == short


def s5_discretise(name, lam_re, lam_im, log_dt, bt_re, bt_im):
    def body(lr_ref, li_ref, ldt_ref, btr_ref, bti_ref, a_ref, bbr_ref, bbi_ref):
        _, ar, ai, _, qr, qi = _zoh(lr_ref[...], li_ref[...], ldt_ref[...])
        a_ref[0] = ar
        a_ref[1] = ai
        q2r, q2i = _per_channel(qr), _per_channel(qi)
        btr, bti = btr_ref[...], bti_ref[...]
        bbr_ref[...] = q2r * btr - q2i * bti
        bbi_ref[...] = q2r * bti + q2i * btr

    return pl.pallas_call(body, out_shape=[S((2,) + _GP, F32), S(_RP, F32), S(_RP, F32)],
                          name=name)(lam_re, lam_im, log_dt, bt_re, bt_im)


def s5_operands(name, a, bbr, bbi, c2r, c2i, ctr, cti):
    ns = N_STATE

    def body(a_ref, bbr_ref, bbi_ref, c2r_ref, c2i_ref, ctr_ref, cti_ref, pw_ref, qw_ref, mb_ref, mc_ref, mct_ref):
        ar, ai = a_ref[0:1, :], a_ref[1:2, :]
        pows = [(ar, ai)]
        for _ in range(SUB - 1):
            pr, pi = pows[-1]
            pows.append((pr * ar - pi * ai, pr * ai + pi * ar))
        rows = lax.broadcasted_iota(jnp.int32, (SUB, ns), 0)

        def rows_of(v):
            return jnp.broadcast_to(v, (SUB, ns))

        for k, s in enumerate((1, 2, 4)):
            pr, pi = rows_of(pows[s - 1][0]), rows_of(pows[s - 1][1])
            pw_ref[k, 0] = jnp.where(rows >= s, pr, 0.0)
            pw_ref[k, 1] = jnp.where(rows >= s, pi, 0.0)
            qw_ref[k, 0] = jnp.where(rows + s <= SUB - 1, pr, 0.0)
            qw_ref[k, 1] = jnp.where(rows + s <= SUB - 1, -pi, 0.0)
        fr = fi = br = bi = jnp.zeros((SUB, ns), F32)
        for i in range(SUB):
            fr = jnp.where(rows == i, rows_of(pows[i][0]), fr)
            fi = jnp.where(rows == i, rows_of(pows[i][1]), fi)
            br = jnp.where(rows == i, rows_of(pows[SUB - 1 - i][0]), br)
            bi = jnp.where(rows == i, rows_of(-pows[SUB - 1 - i][1]), bi)
        pw_ref[3, 0], pw_ref[3, 1], qw_ref[3, 0], qw_ref[3, 1] = fr, fi, br, bi

        wide = _spread((C_STATE, ns), 1).astype(BF16)
        tall = _spread((ns, C_STATE), 0).astype(BF16)
        in_rows = _same_group((C_WIDTH, ns), C_GROUP_CH, C_STATE)
        in_cols = _same_group((ns, C_WIDTH), C_STATE, C_GROUP_CH)

        def across(v, sign=1.0):
            return jnp.where(in_rows, sign * jnp.dot(_bf(v), wide, preferred_element_type=F32), 0.0).astype(BF16)

        def down(vt, sign=1.0):
            return jnp.where(in_cols, sign * jnp.dot(tall, _bf(vt), preferred_element_type=F32), 0.0).astype(BF16)

        mb_ref[:, 0:ns] = across(bbr_ref[...])
        mb_ref[:, ns:2 * ns] = across(bbi_ref[...])
        mct_ref[:, 0:ns] = across(c2r_ref[...])
        mct_ref[:, ns:2 * ns] = across(c2i_ref[...], -1.0)
        mc_ref[0:ns, :] = down(ctr_ref[...])
        mc_ref[ns:2 * ns, :] = down(cti_ref[...], -1.0)

    return pl.pallas_call(
        body, out_shape=[S((4, 2, SUB, ns), F32), S((4, 2, SUB, ns), F32), S((C_WIDTH, 2 * ns), BF16),
                         S((2 * ns, C_WIDTH), BF16), S((C_WIDTH, 2 * ns), BF16)],
        compiler_params=pltpu.CompilerParams(vmem_limit_bytes=VMEM_LIMIT), name=name)(a, bbr, bbi, c2r, c2i, ctr, cti)


def s5_block_grads(name, u, lamb, xsb, dyb):
    t = u.shape[0]

    def mb_body(u_ref, lr_ref, li_ref, o_ref):
        ub = _bf(u_ref[...])
        o_ref[:, 0:BLOCK_ST] = lax.dot_general(ub, lr_ref[...], _TN, preferred_element_type=F32)
        o_ref[:, BLOCK_ST:2 * BLOCK_ST] = lax.dot_general(ub, li_ref[...], _TN, preferred_element_type=F32)

    d_mb = pl.pallas_call(
        mb_body, grid=(S5_BLOCKS,),
        in_specs=[pl.BlockSpec((t, BLOCK_CH), lambda q: (0, q)), pl.BlockSpec((t, BLOCK_ST), lambda q: (0, q)),
                  pl.BlockSpec((t, BLOCK_ST), lambda q: (0, S5_BLOCKS + q))],
        out_specs=pl.BlockSpec((BLOCK_CH, 2 * BLOCK_ST), lambda q: (q, 0)), out_shape=S((C_WIDTH, 2 * BLOCK_ST), F32),
        compiler_params=_cp("parallel"), name=name + "_b")(u, lamb, lamb)

    def mc_body(x_ref, dy_ref, o_ref):
        o_ref[...] = lax.dot_general(x_ref[...], dy_ref[...], _TN, preferred_element_type=F32)

    d_mc = pl.pallas_call(
        mc_body, grid=(2, S5_BLOCKS),
        in_specs=[pl.BlockSpec((t, BLOCK_ST), lambda p, q: (0, p * S5_BLOCKS + q)), pl.BlockSpec((t, BLOCK_CH), lambda p, q: (0, q))],
        out_specs=pl.BlockSpec((BLOCK_ST, BLOCK_CH), lambda p, q: (p * S5_BLOCKS + q, 0)),
        out_shape=S((2 * N_STATE, BLOCK_CH), F32), compiler_params=_cp("parallel", "parallel"), name=name + "_c")(xsb, dyb)
    return d_mb, d_mc


def s5_param_grads(name, d_mb, d_mc, da, lam_re, lam_im, log_dt, bt_re, bt_im):
    ns = N_STATE

    def body(dmb_ref, dmc_ref, da_ref, lr_ref, li_ref, ldt_ref, btr_ref, bti_ref,
             glr_ref, gli_ref, gdt_ref, gbr_ref, gbi_ref, gcr_ref, gci_ref):
        lr, li = lr_ref[...], li_ref[...]
        dt, ar, ai, den, qr, qi = _zoh(lr, li, ldt_ref[...])
        per_block = C_GROUPS // S5_BLOCKS
        wide = _spread((C_STATE, BLOCK_ST), 1).astype(F32)
        tall = _spread((BLOCK_ST, C_STATE), 0).astype(F32)
        rows = lax.broadcasted_iota(jnp.int32, (C_WIDTH, BLOCK_ST), 0) // C_GROUP_CH % per_block
        in_rows = rows == lax.broadcasted_iota(jnp.int32, (C_WIDTH, BLOCK_ST), 1) // C_STATE
        in_cols = _same_group((BLOCK_ST, BLOCK_CH), C_STATE, C_GROUP_CH)

        def fold_rows(v):
            return lax.dot_general(jnp.where(in_rows, v, 0.0), wide, (((1,), (1,)), ((), ())), precision=_HI,
                                   preferred_element_type=F32)

        def fold_cols(v):
            return lax.dot_general(jnp.where(in_cols, v, 0.0), tall, (((0,), (0,)), ((), ())), precision=_HI,
                                   preferred_element_type=F32)

        for cs, s_re, s_im in _S5_BLOCKS:
            gcr_ref[cs, :] = fold_cols(dmc_ref[s_re, :])
            gci_ref[cs, :] = -fold_cols(dmc_ref[s_im, :])
        gbbr = fold_rows(dmb_ref[:, 0:BLOCK_ST])
        gbbi = fold_rows(dmb_ref[:, BLOCK_ST:2 * BLOCK_ST])
        btr, bti = btr_ref[...], bti_ref[...]
        q2r, q2i = _per_channel(qr), _per_channel(qi)
        gbr_ref[...] = q2r * gbbr + q2i * gbbi
        gbi_ref[...] = q2r * gbbi - q2i * gbbr

        def per_group(v):
            return jnp.sum(v.reshape(C_GROUPS, C_GROUP_CH, C_STATE), axis=1)

        gqr = per_group(btr * gbbr + bti * gbbi)
        gqi = per_group(btr * gbbi - bti * gbbr)
        ilr, ili = lr / den, li / den
        gar = da_ref[0] + ilr * gqr - ili * gqi
        gai = da_ref[1] + ilr * gqi + ili * gqr
        sr = (qr * lr + qi * li) / den
        si = (qi * lr - qr * li) / den
        gzr = ar * gar + ai * gai
        gzi = ar * gai - ai * gar
        glr_ref[...] = -sr * gqr - si * gqi + dt * gzr
        gli_ref[...] = -sr * gqi + si * gqr + dt * gzi
        gdt_ref[...] = jnp.sum(lr * gzr + li * gzi, axis=1, keepdims=True) * dt

    return pl.pallas_call(
        body, out_shape=[S(_GP, F32), S(_GP, F32), S((C_GROUPS, 1), F32), S(_RP, F32), S(_RP, F32), S(_RP, F32), S(_RP, F32)],
        compiler_params=pltpu.CompilerParams(vmem_limit_bytes=VMEM_LIMIT), name=name,
    )(d_mb, d_mc, da, lam_re, lam_im, log_dt, bt_re, bt_im)


def _cmul_add(xr, xi, pr, pi, zr, zi):
    return xr + pr * zr - pi * zi, xi + pr * zi + pi * zr


def s5_fwd(name, u, mb, mc, pw, dskip):
    t = u.shape[0]
    tm = _tile(t, S5_ROWS)
    ns = N_STATE

    def body(u_ref, mb_ref, mc_ref, pw_ref, d_ref, gy_ref, y_ref, xs_ref, xb_ref, carry):
        @pl.when(pl.program_id(0) == 0)
        def _():
            carry[...] = jnp.zeros(carry.shape, F32)

        uv = u_ref[...]
        ub = _bf(uv)
        for cs, s_re, s_im in _S5_BLOCKS:
            xs_ref[:, s_re] = jnp.dot(ub[:, cs], mb_ref[cs, s_re], preferred_element_type=F32)
            xs_ref[:, s_im] = jnp.dot(ub[:, cs], mb_ref[cs, s_im], preferred_element_type=F32)

        def group(i, _):
            r0 = pl.multiple_of(i * SUB, SUB)
            xr = xs_ref[pl.ds(r0, SUB), 0:ns]
            xi = xs_ref[pl.ds(r0, SUB), ns:2 * ns]
            for k, s in enumerate((1, 2, 4)):
                xr, xi = _cmul_add(xr, xi, pw_ref[k, 0], pw_ref[k, 1], pltpu.roll(xr, s, 0), pltpu.roll(xi, s, 0))
            xr, xi = _cmul_add(xr, xi, pw_ref[3, 0], pw_ref[3, 1], carry[0], carry[1])
            xs_ref[pl.ds(r0, SUB), 0:ns] = xr
            xs_ref[pl.ds(r0, SUB), ns:2 * ns] = xi
            carry[0] = jnp.broadcast_to(xr[SUB - 1:SUB, :], (SUB, ns))
            carry[1] = jnp.broadcast_to(xi[SUB - 1:SUB, :], (SUB, ns))
            return 0
        lax.fori_loop(0, tm // SUB, group, 0)

        xb_ref[...] = _bf(xs_ref[...])
        for cs, s_re, s_im in _S5_BLOCKS:
            y = (jnp.dot(xb_ref[:, s_re], mc_ref[s_re, cs], preferred_element_type=F32)
                 + jnp.dot(xb_ref[:, s_im], mc_ref[s_im, cs], preferred_element_type=F32) + d_ref[:, cs] * uv[:, cs])
            y_ref[:, cs] = y
            gy_ref[:, cs] = _gelu(y).astype(gy_ref.dtype)

    c = u.shape[1]
    return pl.pallas_call(
        body, grid=(t // tm,),
        in_specs=[pl.BlockSpec((tm, c), lambda i: (i, 0)), pl.BlockSpec(mb.shape, lambda i: (0, 0)),
                  pl.BlockSpec(mc.shape, lambda i: (0, 0)), pl.BlockSpec(pw.shape, lambda i: (0, 0, 0, 0)),
                  pl.BlockSpec((1, c), lambda i: (0, 0))],
        out_specs=[pl.BlockSpec((tm, c), lambda i: (i, 0)), pl.BlockSpec((tm, c), lambda i: (i, 0)),
                   pl.BlockSpec((tm, 2 * ns), lambda i: (i, 0)), pl.BlockSpec((tm, 2 * ns), lambda i: (i, 0))],
        out_shape=[S((t, c), BF16), S((t, c), F32), S((t, 2 * ns), F32), S((t, 2 * ns), BF16)],
        scratch_shapes=[pltpu.VMEM((2, SUB, ns), F32)],
        compiler_params=_cp("arbitrary"), name=name)(u, mb, mc, pw, dskip)


def s5_bwd(name, dgy, y, u, xs, mct, mbt, qw, dskip):
    t, c = u.shape
    tm = _tile(t, S5_ROWS)
    nt = t // tm
    ns = N_STATE
    ng = tm // SUB

    def body(dgy_ref, y_ref, u_ref, xs_ref, mct_ref, mbt_ref, qw_ref, d_ref,
             du_ref, dy_ref, lb_ref, da_ref, dd_ref, lam, carry):
        @pl.when(pl.program_id(0) == 0)
        def _():
            carry[...] = jnp.zeros(carry.shape, F32)
            da_ref[...] = jnp.zeros(da_ref.shape, F32)
            dd_ref[...] = jnp.zeros(dd_ref.shape, F32)

        uv = u_ref[...]
        dy = dgy_ref[...] * _gelu_grad(y_ref[...])
        dyb = _bf(dy)
        dy_ref[...] = dyb
        dd_ref[...] += jnp.sum(dy * uv, axis=0, keepdims=True)
        for cs, s_re, s_im in _S5_BLOCKS:
            lam[:, s_re] = jnp.dot(dyb[:, cs], mct_ref[cs, s_re], preferred_element_type=F32)
            lam[:, s_im] = jnp.dot(dyb[:, cs], mct_ref[cs, s_im], preferred_element_type=F32)
        last_row = lax.broadcasted_iota(jnp.int32, (SUB, ns), 0) == SUB - 1

        def group(j, _):
            i = ng - 1 - j
            r0 = pl.multiple_of(i * SUB, SUB)
            lr = lam[pl.ds(r0, SUB), 0:ns]
            li = lam[pl.ds(r0, SUB), ns:2 * ns]
            for k, s in enumerate((1, 2, 4)):
                lr, li = _cmul_add(lr, li, qw_ref[k, 0], qw_ref[k, 1],
                                   pltpu.roll(lr, SUB - s, 0), pltpu.roll(li, SUB - s, 0))
            cr, ci = carry[0], carry[1]
            lr, li = _cmul_add(lr, li, qw_ref[3, 0], qw_ref[3, 1], cr, ci)
            lam[pl.ds(r0, SUB), 0:ns] = lr
            lam[pl.ds(r0, SUB), ns:2 * ns] = li
            carry[0] = jnp.broadcast_to(lr[0:1, :], (SUB, ns))
            carry[1] = jnp.broadcast_to(li[0:1, :], (SUB, ns))
            nr = jnp.where(last_row, cr, pltpu.roll(lr, SUB - 1, 0))
            ni = jnp.where(last_row, ci, pltpu.roll(li, SUB - 1, 0))
            xr = xs_ref[pl.ds(r0, SUB), 0:ns]
            xi = xs_ref[pl.ds(r0, SUB), ns:2 * ns]
            da_ref[0] += nr * xr + ni * xi
            da_ref[1] += ni * xr - nr * xi
            return 0
        lax.fori_loop(0, ng, group, 0)

        lb_ref[...] = _bf(lam[...])
        for cs, s_re, s_im in _S5_BLOCKS:
            du = (jnp.dot(lb_ref[:, s_re], mbt_ref[s_re, cs], preferred_element_type=F32)
                  + jnp.dot(lb_ref[:, s_im], mbt_ref[s_im, cs], preferred_element_type=F32) + d_ref[:, cs] * dy[:, cs])
            du_ref[:, cs] = du.astype(du_ref.dtype)

    rev = lambda i: (nt - 1 - i, 0)
    return pl.pallas_call(
        body, grid=(nt,),
        in_specs=[pl.BlockSpec((tm, c), rev), pl.BlockSpec((tm, c), rev), pl.BlockSpec((tm, c), rev),
                  pl.BlockSpec((tm, 2 * ns), rev),
                  pl.BlockSpec(mct.shape, lambda i: (0, 0)), pl.BlockSpec(mbt.shape, lambda i: (0, 0)),
                  pl.BlockSpec(qw.shape, lambda i: (0, 0, 0, 0)), pl.BlockSpec((1, c), lambda i: (0, 0))],
        out_specs=[pl.BlockSpec((tm, c), rev), pl.BlockSpec((tm, c), rev), pl.BlockSpec((tm, 2 * ns), rev),
                   pl.BlockSpec((2, SUB, ns), lambda i: (0, 0, 0)), pl.BlockSpec((1, c), lambda i: (0, 0))],
        out_shape=[S((t, c), BF16), S((t, c), BF16), S((t, 2 * ns), BF16), S((2, SUB, ns), F32), S((1, c), F32)],
        scratch_shapes=[pltpu.VMEM((tm, 2 * ns), F32), pltpu.VMEM((2, SUB, ns), F32)],
        compiler_params=_cp("arbitrary"), name=name)(dgy, y, u, xs, mct, mbt, qw, dskip)


def _first(accs, *_):
    return [accs[0]]


def _rms_bwd_epi(accs, xv, base, rv, g):
    dv = accs[0]
    w = dv * g
    xh = xv * rv
    dx = base + rv * (w - xh * jnp.mean(w * xh, axis=-1, keepdims=True))
    return [dx, dx, jnp.sum(dv * xh, axis=0, keepdims=True)]


def mm_rms_bwd(name, pairs, x, r, gain, dres):
    t, d = x.shape
    return mm_nn(name, t, d, pairs, 1, _rms_bwd_epi, [F32, BF16], tiled=[x, dres], cols=[r], rowv=[gain], sums=[(1, d)])


def _add_res(accs, res):
    return [accs[0] + res]


def even_fwd(x, w, need_out):
    t = x.shape[0]
    proj, hn, r = mm_nn("e_in_f", t, IN_WIDTH, [(x, w["e_w_in_t"], 0, "t")], 1, _first, [F32], norm_gain=w["e_norm"])
    out_a = gmlp_fwd("e_gmlp_f", proj, w["e_gmlp_w"], w["e_gmlp_b"])
    hc = conv_fwd("e_conv_f", proj, w["e_conv_w"], w["e_conv_b"])
    out_b = ln_silu_fwd("e_ln_f", hc, w["e_conv_ln_g"], w["e_conv_ln_b"])
    need_out(out_b)
    (x1,) = mm_nn("e_out_f", t, D_MODEL, [(out_a, (w["e_w_out"], 0), 0), (out_b, (w["e_w_out"], 1), 0)],
                  1, _add_res, [F32], tiled=[x])
    return x1, (x, hn, r, proj, out_a, hc, out_b)


def even_bwd_mixers(dxb, saved, w):
    x, hn, r, proj, out_a, hc, out_b = saved
    t = x.shape[0]
    (dcat,) = mm_nn("e_out_b", t, D_MODEL, [(dxb, w["e_w_out"], 0, "t")], 1, _first, [F32])
    g_w_out = jnp.concatenate([mm_tn("e_out_wa", out_a, dxb), mm_tn("e_out_wb", out_b, dxb)], axis=0)
    dab, g_gw, g_gb = gmlp_bwd("e_gmlp_b", proj, dcat, w["e_gmlp_w"], w["e_gmlp_b"])
    dhc, g_lg, g_lb = ln_silu_bwd("e_ln_b", hc, dcat, w["e_conv_ln_g"], w["e_conv_ln_b"])
    dba, dbg, g_cw, g_cb = conv_bwd("e_conv_b", proj, dhc, w["e_conv_w"])
    g_w_in_t = jnp.concatenate([mm_tn("e_in_w0", dab, hn), mm_tn("e_in_w1", dba, hn), mm_tn("e_in_w2", dbg, hn)], axis=0)
    grads = dict(e_w_in_t=g_w_in_t, e_gmlp_w=g_gw[None], e_gmlp_b=g_gb.reshape(1, A_GROUPS, GMLP_BLOCK),
                 e_conv_w=g_cw[None], e_conv_b=g_cb, e_conv_ln_g=g_lg, e_conv_ln_b=g_lb, e_w_out=g_w_out)
    return (dab, dba, dbg), grads


def even_bwd_input(dx, dproj, saved, w):
    x, _, r = saved[:3]
    dab, dba, dbg = dproj
    w_in_t = w["e_w_in_t"]
    return mm_rms_bwd("e_in_b", [(dab, (w_in_t, 0), 0), (dba, (w_in_t, 2), 0), (dbg, (w_in_t, 3), 0)], x, r, w["e_norm"], dx)


def s5_setup(w, anchor=None):
    def rows(v):
        return v.transpose(0, 2, 1).reshape(_RP)

    log_dt = w["o_log_dt"].reshape(C_GROUPS, 1)
    if anchor is not None:
        log_dt = log_dt + anchor
    lam = (w["o_lam_re"], w["o_lam_im"], log_dt, rows(w["o_b_re"]), rows(w["o_b_im"]))
    a, bbr, bbi = s5_discretise("o_s5_zoh", *lam)
    c_re, c_im = w["o_c_re"], w["o_c_im"]
    pw, qw, mb, mc, mct = s5_operands("o_s5_ops", a.reshape(2, N_STATE), bbr, bbi, c_re.reshape(_RP), c_im.reshape(_RP),
                                      c_re.transpose(2, 0, 1).reshape(C_STATE, C_WIDTH),
                                      c_im.transpose(2, 0, 1).reshape(C_STATE, C_WIDTH))
    return dict(lam=lam, pw=pw, qw=qw, mb=mb, mc=mc, mct=mct, mbt=mb.T)


def odd_fwd(x, w, consts):
    t = x.shape[0]
    u, hn, r = mm_nn("o_in_f", t, C_WIDTH, [(x, w["o_w_in"], 0)], 1, _first, [F32], norm_gain=w["o_norm"])
    gy, y, xs, xsb = s5_fwd("o_s5_f", u, consts["mb"], consts["mc"], consts["pw"], w["o_d"])
    w_out_t = w["o_w_out_t"]

    def epi(accs, res):
        return [res + accs[0] * _sigmoid(accs[1]), accs[0], accs[1]]

    x1, o1, o2 = mm_nn("o_out_f", t, D_MODEL, [(gy, (w_out_t, 0), 0, "t"), (gy, (w_out_t, D_MODEL), 1, "t")], 2, epi,
                       [F32, BF16, BF16], tiled=[x])
    return x1, (x, hn, r, u, gy, y, xs, xsb, o1, o2)


def odd_bwd(dx, dxb, saved, w, consts):
    x, hn, r, u, gy, y, xs, xsb, o1, o2 = saved
    t = x.shape[0]

    def gate_bwd(dv, a, b):
        a = a.astype(F32)
        sg = _sigmoid(b.astype(F32))
        return [jnp.concatenate([dv * sg, dv * a * sg * (1.0 - sg)], axis=1)], []

    (do12,) = rows_call("o_gate_b", gate_bwd, [dx, o1, o2], [], [(2 * D_MODEL, BF16)], [])
    (dgy,) = mm_nn("o_out_b", t, C_WIDTH, [(do12, w["o_w_out_t"], 0)], 1, _first, [F32])
    g_w_out_t = mm_tn("o_out_w", do12, gy)
    du, dyb, lamb, da8, g_d = s5_bwd("o_s5_b", dgy, y, u, xs, consts["mct"], consts["mbt"], consts["qw"], w["o_d"])
    d_mb, d_mc = s5_block_grads("o_s5_w", u, lamb, xsb, dyb)
    da = jnp.sum(da8, axis=1).reshape((2,) + _GP)
    g_lr, g_li, g_dt, g_btr, g_bti, g_cr, g_ci = s5_param_grads("o_s5_pg", d_mb, d_mc, da, *consts["lam"])

    def states_first(v):
        return v.reshape(C_GROUPS, C_GROUP_CH, C_STATE).transpose(0, 2, 1)[None]

    g_w_in = mm_tn("o_in_w", hn, du)
    dx0, dx0b, g_norm = mm_rms_bwd("o_in_b", [(du, w["o_w_in"], 0, "t")], x, r, w["o_norm"], dx)
    grads = dict(o_norm=g_norm, o_w_in=g_w_in, o_lam_re=g_lr[None], o_lam_im=g_li[None], o_log_dt=g_dt.reshape(1, C_GROUPS),
                 o_b_re=states_first(g_btr), o_b_im=states_first(g_bti),
                 o_c_re=g_cr.reshape((1, C_GROUPS, C_GROUP_CH, C_STATE)), o_c_im=g_ci.reshape((1, C_GROUPS, C_GROUP_CH, C_STATE)),
                 o_d=g_d, o_w_out_t=g_w_out_t)
    return dx0, dx0b, grads


def ca_fwd(i, x, mem, w):
    t, m = x.shape[0], mem.shape[0]
    q, xn, r = mm_nn(f"ca{i}_q_f", t, D_MODEL, [(x, w["ca_wq"][i], 0)], 1, _first, [BF16], norm_gain=w["ca_norm"][i:i + 1])
    k, v, mn, rm = mm_nn(f"ca{i}_kv_f", m, D_MODEL, [(mem, w["ca_wk"][i], 0), (mem, w["ca_wv"][i], 1)], 2,
                         lambda accs: [accs[0], accs[1]], [BF16, BF16], norm_gain=w["ca_mem_norm"][i:i + 1])
    o = attn_fwd(f"ca{i}_attn_f", q, k, v)
    (x1,) = mm_nn(f"ca{i}_o_f", t, D_MODEL, [(o, w["ca_wo"][i], 0)], 1, _add_res, [F32], tiled=[x])
    return x1, (x, xn, r, mn, rm, q, k, v, o)


def ca_bwd(i, dx, dxb, saved, mem, w):
    x, xn, r, mn, rm, q, k, v, o = saved
    t, m = x.shape[0], mem.shape[0]
    (do,) = mm_nn(f"ca{i}_o_b", t, D_MODEL, [(dxb, w["ca_wo"][i], 0, "t")], 1, _first, [BF16])
    g_wo = mm_tn(f"ca{i}_o_w", o, dxb)
    dq, dk, dv = attn_bwd(f"ca{i}_attn_b", q, k, v, do)
    g_wq = mm_tn(f"ca{i}_q_w", xn, dq)
    g_wk = mm_tn(f"ca{i}_k_w", mn, dk)
    g_wv = mm_tn(f"ca{i}_v_w", mn, dv)
    (dmn,) = mm_nn(f"ca{i}_kv_b", m, D_MODEL, [(dk, w["ca_wk"][i], 0, "t"), (dv, w["ca_wv"][i], 0, "t")], 1, _first, [F32])
    g_mnorm = rms_bwd_gain_only(f"ca{i}_mnorm_b", dmn, mem, rm)
    dx0, dx0b, g_norm = mm_rms_bwd(f"ca{i}_q_b", [(dq, w["ca_wq"][i], 0, "t")], x, r, w["ca_norm"][i:i + 1], dx)
    return dx0, dx0b, dict(ca_norm=g_norm, ca_mem_norm=g_mnorm, ca_wq=g_wq, ca_wk=g_wk, ca_wv=g_wv, ca_wo=g_wo)


def ffn_fwd(i, x, w, target=None):
    t = x.shape[0]

    def epi(accs):
        g, u = accs
        return [g, u, g * _sigmoid(g) * u]

    g, u, h, xn, r = mm_nn(f"ffn{i}_up_f", t, FFN_HIDDEN, [(x, w["ffn_w_gate_t"][i], 0, "t"), (x, w["ffn_w_up_t"][i], 1, "t")],
                           2, epi, [BF16, BF16, BF16], norm_gain=w["ffn_norm"][i:i + 1])
    down = [(h, w["ffn_w_down"][i], 0)]
    if target is None:
        (out,) = mm_nn(f"ffn{i}_down_f", t, D_MODEL, down, 1, _add_res, [F32], tiled=[x])
    else:
        out = mm_nn(f"ffn{i}_down_f", t, D_MODEL, down, 1, _final_loss_epi, [F32, BF16], tiled=[x, target],
                    rowv=[w["final_norm"]], sums=[(1, D_MODEL), (1, 1)])
    return out, (x, xn, r, g, u, h)


def ffn_bwd(i, dx, dxb, saved, w):
    x, xn, r, g, u, h = saved
    t = x.shape[0]

    def epi(accs, gv, uv):
        dh = accs[0]
        gv = gv.astype(F32)
        uv = uv.astype(F32)
        s = _sigmoid(gv)
        return [dh * uv * s * (1.0 + gv * (1.0 - s)), dh * gv * s]

    dg, du = mm_nn(f"ffn{i}_down_b", t, FFN_HIDDEN, [(dxb, w["ffn_w_down"][i], 0, "t")], 1, epi, [BF16, BF16], tiled=[g, u])
    g_wd = mm_tn(f"ffn{i}_down_w", h, dxb)
    g_wg_t = mm_tn(f"ffn{i}_gate_w", dg, xn)
    g_wu_t = mm_tn(f"ffn{i}_up_w", du, xn)
    dx0, dx0b, g_norm = mm_rms_bwd(f"ffn{i}_up_b", [(dg, w["ffn_w_gate_t"][i], 0), (du, w["ffn_w_up_t"][i], 0)], x, r,
                                   w["ffn_norm"][i:i + 1], dx)
    return dx0, dx0b, dict(ffn_norm=g_norm, ffn_w_gate_t=g_wg_t, ffn_w_up_t=g_wu_t, ffn_w_down=g_wd)


def local_step(x, mem, target, w, fetch=None, on_grads=None, anchor=None):
    consts = s5_setup(w, anchor)

    def need(stage, after):
        if fetch is not None:
            for k, v in fetch(stage, after).items():
                if isinstance(k, tuple):
                    w.setdefault(k[0], {})[k[1]] = v
                else:
                    w[k] = v

    need(0, consts["pw"])
    x1, s_e = even_fwd(x, w, lambda after: need(1, after))
    x2, s_c0 = ca_fwd(0, x1, mem, w)
    need(2, x2)
    x3, s_f0 = ffn_fwd(0, x2, w)
    x4, s_o = odd_fwd(x3, w, consts)
    need(3, x4)
    x5, s_c1 = ca_fwd(1, x4, mem, w)
    (dx, dxb, g_final, loss), s_f1 = ffn_fwd(1, x5, w, target)

    def emit(stage, carry, plain, layered=None, layer=0):
        if on_grads is None:
            return carry
        out = dict(plain)
        out.update({(k, layer): v for k, v in (layered or {}).items()})
        return on_grads(stage, out, list(carry))

    dx, dxb, g_f1 = ffn_bwd(1, dx, dxb, s_f1, w)
    dx, dxb = emit(0, (dx, dxb), {}, g_f1, 1)
    dx, dxb, g_c1 = ca_bwd(1, dx, dxb, s_c1, mem, w)
    dx, dxb, g_o = odd_bwd(dx, dxb, s_o, w, consts)
    dx, dxb = emit(1, (dx, dxb), g_o, g_c1, 1)
    dx, dxb, g_f0 = ffn_bwd(0, dx, dxb, s_f0, w)
    dx, dxb = emit(2, (dx, dxb), {}, g_f0, 0)
    dx, dxb, g_c0 = ca_bwd(0, dx, dxb, s_c0, mem, w)
    dx, dxb = emit(3, (dx, dxb), {}, g_c0, 0)
    dproj, g_e = even_bwd_mixers(dxb, s_e, w)
    dproj = emit(4, dproj, {**g_e, "o_norm": g_o["o_norm"], "o_d": g_o["o_d"]})
    dx, dxb, g_e["e_norm"] = even_bwd_input(dx, dproj, s_e, w)

    grads = dict(g_e)
    grads.update(g_o)
    for g0, g1 in ((g_c0, g_c1), (g_f0, g_f1)):
        for k in g0:
            grads[k] = jnp.concatenate([g0[k], g1[k]], axis=0) if k.endswith("norm") else (g0[k], g1[k])
    grads["final_norm"] = g_final
    return loss, dx, grads


def _group(axes):
    pos = {a: lax.axis_index(a) for a in ("x", "y", "c")}
    me = 0
    for a in axes:
        me = me * 2 + pos[a]
    peers = []
    for mask in range(1, 2 ** len(axes)):
        peer = dict(pos)
        for bit, a in enumerate(axes):
            if (mask >> (len(axes) - 1 - bit)) & 1:
                peer[a] = 1 - pos[a]
        idx = 0
        for a in axes:
            idx = idx * 2 + peer[a]
        peers.append((idx, (peer["x"], peer["y"], peer["c"])))
    return me, peers


def _sibling():
    x, y, c = lax.axis_index("x"), lax.axis_index("y"), lax.axis_index("c")
    return c, (x, y, 1 - c)


_HBM =pl.BlockSpec(memory_space=pltpu.HBM)
_SEM = pl.BlockSpec(memory_space=pltpu.SEMAPHORE)
_EFFECT = pltpu.SideEffectType.DATAFLOW_SIDE_EFFECTING


def _gather_peers(direct):
    chip, _ = _group(("x", "y"))
    core = lax.axis_index("c")
    if direct:
        _, peers = _group(_ALL)
        return chip, core, [(idx // 2, idx % 2, dev) for idx, dev in peers]
    _, peers = _group(("x", "y"))
    return chip, core, [(idx, core, dev) for idx, dev in peers]


def gather_ici_start(name, groups, direct):
    flat = [b for g in groups for b in g]
    sizes = [len(g) for g in groups]
    k_ops, n_g = len(flat), len(groups)
    lands = [lax.empty((4, 2) + tuple(b.shape), b.dtype) for b in flat]
    fan = [N_DEV - 1 if d else 3 for d in direct]

    def body(*refs):
        src, land = refs[:k_ops], refs[k_ops:2 * k_ops]
        sems = refs[2 * k_ops:2 * k_ops + 3 * n_g]
        token = refs[-1]
        i = 0
        for g in range(n_g):
            send, recv, loc = sems[3 * g:3 * g + 3]
            chip, core, peers = _gather_peers(direct[g])
            for j in range(sizes[g]):
                pltpu.make_async_copy(src[i], land[i].at[chip, core], loc.at[j]).start()
                for k, (_, _, dev) in enumerate(peers):
                    s = fan[g] * j + k
                    pltpu.make_async_remote_copy(src_ref=src[i], dst_ref=land[i].at[chip, core], send_sem=send.at[s],
                                                 recv_sem=recv.at[s], device_id=dev, device_id_type=MESH).start()
                i += 1
        token[...] = jnp.zeros(token.shape, token.dtype)

    sem_shapes = []
    for s, f in zip(sizes, fan):
        sem_shapes += [pltpu.SemaphoreType.DMA((f * s,)), pltpu.SemaphoreType.DMA((f * s,)), pltpu.SemaphoreType.DMA((s,))]
    thru = [pltpu.HBM(a.shape, a.dtype) for a in flat + lands]
    outs = pl.pallas_call(
        body, name=name, out_shape=tuple(sem_shapes) + tuple(thru) + (S((8, LANES), F32),),
        in_specs=[_HBM] * (2 * k_ops), out_specs=[_SEM] * (3 * n_g) + [_HBM] * (2 * k_ops) + [pl.BlockSpec(memory_space=pltpu.VMEM)],
        input_output_aliases={i: 3 * n_g + i for i in range(2 * k_ops)},
        compiler_params=pltpu.CompilerParams(has_side_effects=_EFFECT),
    )(*[pltpu.with_memory_space_constraint(a, pltpu.HBM) for a in flat + lands])
    sems = [tuple(outs[3 * g:3 * g + 3]) for g in range(n_g)]
    srcs_thru, lands_thru, off = [], [], 3 * n_g
    for s in sizes:
        srcs_thru.append(list(outs[off:off + s]))
        off += s
    for s in sizes:
        lands_thru.append(list(outs[off:off + s]))
        off += s
    return sems, srcs_thru, lands_thru, outs[-1]


def gather_ici_wait(name, srcs, lands, sems, after, direct=False):
    n = len(srcs)

    def body(*refs):
        src, land = refs[:n], refs[n:2 * n]
        send, recv, loc = refs[2 * n:2 * n + 3]
        chip, core, peers = _gather_peers(direct)
        for j in range(n):
            for k, (pchip, pcore, dev) in enumerate(peers):
                s = len(peers) * j + k
                cp = pltpu.make_async_remote_copy(src_ref=src[j], dst_ref=land[j].at[pchip, pcore], send_sem=send.at[s],
                                                  recv_sem=recv.at[s], device_id=dev, device_id_type=MESH)
                cp.wait_send()
                cp.wait_recv()
            pltpu.make_async_copy(src[j], land[j].at[chip, core], loc.at[j]).wait()

    outs = pl.pallas_call(
        body, name=name, out_shape=tuple(pltpu.HBM(a.shape, a.dtype) for a in list(srcs) + list(lands)),
        in_specs=[_HBM] * (2 * n) + [_SEM] * 3 + [ANY], out_specs=[_HBM] * (2 * n),
        input_output_aliases={i: i for i in range(2 * n)},
        compiler_params=pltpu.CompilerParams(has_side_effects=_EFFECT),
    )(*srcs, *lands, *sems, after)
    return list(outs[n:])


def gather_d2d(name, bufs):
    k_ops = len(bufs)

    def body(*refs):
        in_refs, out_refs = refs[:k_ops], refs[k_ops:2 * k_ops]
        send_sems, recv_sems = refs[2 * k_ops:]
        core, sib = _sibling()
        sent, landed = [], []
        for i in range(k_ops):
            cp = pltpu.make_async_remote_copy(src_ref=in_refs[i].at[:, core], dst_ref=out_refs[i].at[:, core],
                                              send_sem=send_sems.at[i], recv_sem=recv_sems.at[i], device_id=sib, device_id_type=MESH)
            cp.start()
            sent.append(cp)
            landed.append(pltpu.make_async_remote_copy(src_ref=in_refs[i].at[:, core], dst_ref=out_refs[i].at[:, 1 - core],
                                                       send_sem=send_sems.at[i], recv_sem=recv_sems.at[i],
                                                       device_id=sib, device_id_type=MESH))
        for cp in landed:
            cp.wait_recv()
        for cp in sent:
            cp.wait_send()

    return pl.pallas_call(
        body, in_specs=[ANY] * k_ops, out_specs=[ANY] * k_ops, out_shape=[S(b.shape, b.dtype) for b in bufs],
        input_output_aliases={i: i for i in range(k_ops)},
        scratch_shapes=[pltpu.SemaphoreType.DMA((k_ops,)), pltpu.SemaphoreType.DMA((k_ops,))],
        name=name)(*bufs)


_ALL = ("x", "y", "c")


def scatter_start(name, arr, carry):
    land = lax.empty(arr.shape, arr.dtype)
    n_c = len(carry)

    def body(*refs):
        in_ref, land_ref = refs[0], refs[1]
        send, recv = refs[2 + n_c], refs[3 + n_c]
        me, peers = _group(_ALL)
        for k, (idx, dev) in enumerate(peers):
            pltpu.make_async_remote_copy(src_ref=in_ref.at[idx], dst_ref=land_ref.at[me], send_sem=send.at[k], recv_sem=recv.at[k],
                                         device_id=dev, device_id_type=MESH).start()

    thru = [arr, land] + list(carry)
    outs = pl.pallas_call(
        body, name=name,
        out_shape=(pltpu.SemaphoreType.DMA((N_DEV - 1,)), pltpu.SemaphoreType.DMA((N_DEV - 1,)))
        + tuple(pltpu.HBM(a.shape, a.dtype) for a in thru),
        in_specs=[_HBM] * len(thru), out_specs=[_SEM, _SEM] + [_HBM] * len(thru),
        input_output_aliases={i: 2 + i for i in range(len(thru))},
        compiler_params=pltpu.CompilerParams(has_side_effects=_EFFECT),
    )(*[pltpu.with_memory_space_constraint(a, pltpu.HBM) for a in thru])
    return (outs[0], outs[1]), outs[2], outs[3], list(outs[4:])


def scatter_wait(name, arr, land, sems, after):
    def body(in_ref, land_ref, send, recv, after_ref, in_thru, land_thru):
        _, peers = _group(_ALL)
        for k, (idx, dev) in enumerate(peers):
            cp = pltpu.make_async_remote_copy(src_ref=in_ref.at[idx], dst_ref=land_ref.at[idx], send_sem=send.at[k],
                                              recv_sem=recv.at[k], device_id=dev, device_id_type=MESH)
            cp.wait_send()
            cp.wait_recv()

    outs = pl.pallas_call(
        body, name=name, out_shape=(pltpu.HBM(arr.shape, arr.dtype), pltpu.HBM(arr.shape, arr.dtype)),
        in_specs=[_HBM, _HBM, _SEM, _SEM, ANY], out_specs=[_HBM, _HBM], input_output_aliases={0: 0, 1: 1},
        compiler_params=pltpu.CompilerParams(has_side_effects=_EFFECT),
    )(arr, land, sems[0], sems[1], after)
    return outs[0], outs[1]


def _row_tile(rows, cap=512):
    return next(t for t in range(cap - cap % 16, 0, -16) if rows % t == 0)


def sum_shares(name, own, recv, me):
    n, rows, c = recv.shape
    tr = _row_tile(rows)

    def body(me_ref, *refs):
        acc = refs[0][...].astype(F32)
        for r in refs[1:n]:
            acc = acc + r[...].astype(F32)
        refs[n][...] = acc

    def slot(mask):
        return pl.BlockSpec((None, tr, c), lambda i, me, mask=mask: (jnp.bitwise_xor(me[0], mask), i, 0))

    spec = pltpu.PrefetchScalarGridSpec(
        num_scalar_prefetch=1, grid=(rows // tr,), in_specs=[slot(k) for k in range(n)],
        out_specs=pl.BlockSpec((tr, c), lambda i, me: (i, 0)))
    return pl.pallas_call(body, grid_spec=spec, out_shape=S((rows, c), F32),
                          compiler_params=_cp("parallel"), name=name)(me, own, *([recv] * (n - 1)))


def sum_slots(name, slots):
    n, r, c = slots.shape

    def body(s_ref, o_ref):
        acc = s_ref[0]
        for j in range(1, n):
            acc = acc + s_ref[j]
        o_ref[...] = acc

    return pl.pallas_call(body, out_shape=S((r, c), F32), compiler_params=pltpu.CompilerParams(vmem_limit_bytes=VMEM_LIMIT),
                          name=name)(slots)


def adamw_units(name, pieces, transposed, w, m, v):
    n_l, k, n = w.shape
    tk = _tile(k, 512) if transposed else k
    c1 = 1.0 - ADAM_B1 ** ADAM_STEP
    c2 = 1.0 - ADAM_B2 ** ADAM_STEP

    def body(*refs):
        p_refs, (w_ref, m_ref, v_ref, g_ref, d_ref, m2_ref, v2_ref) = refs[:n_l], refs[n_l:]
        gv = p_refs[0][...]
        for j in range(1, n_l):
            gv = jnp.where(pl.program_id(0) == j, p_refs[j][...], gv)
        if transposed:
            gv = gv.T
        m2 = ADAM_B1 * m_ref[...] + (1.0 - ADAM_B1) * gv
        v2 = ADAM_B2 * v_ref[...] + (1.0 - ADAM_B2) * (gv * gv)
        g_ref[...] = gv
        m2_ref[...] = m2
        v2_ref[...] = v2
        d_ref[...] = -ADAM_LR * ((m2 / c1) / (jnp.sqrt(v2 / c2) + ADAM_EPS) + ADAM_WD * w_ref[...])

    piece = pl.BlockSpec((n, tk), lambda l, i: (0, i)) if transposed else pl.BlockSpec((k, n), lambda l, i: (0, 0))
    blk = pl.BlockSpec((None, tk, n), lambda l, i: (l, i, 0))
    return tuple(pl.pallas_call(body, grid=(n_l, k // tk), in_specs=[piece] * n_l + [blk] * 3, out_specs=[blk] * 4,
                                out_shape=[S(w.shape, F32)] * 4, compiler_params=_cp("parallel", "parallel"),
                                name=name)(*pieces, w, m, v))


def adamw_native(name, g, w, m, v, tr=512):
    shape = w.shape
    cols = shape[-1]
    rows = w.size // cols
    tr = _tile(rows, tr) if rows % 8 == 0 else rows
    c1 = 1.0 - ADAM_B1 ** ADAM_STEP
    c2 = 1.0 - ADAM_B2 ** ADAM_STEP

    def body(g_ref, w_ref, m_ref, v_ref, d_ref, m2_ref, v2_ref):
        gv = g_ref[...]
        m2 = ADAM_B1 * m_ref[...] + (1.0 - ADAM_B1) * gv
        v2 = ADAM_B2 * v_ref[...] + (1.0 - ADAM_B2) * (gv * gv)
        m2_ref[...] = m2
        v2_ref[...] = v2
        d_ref[...] = -ADAM_LR * ((m2 / c1) / (jnp.sqrt(v2 / c2) + ADAM_EPS) + ADAM_WD * w_ref[...])

    row = pl.BlockSpec((tr, cols), lambda i: (i, 0))
    outs = pl.pallas_call(body, grid=(rows // tr,), in_specs=[row] * 4, out_specs=[row] * 3,
                          out_shape=[S((rows, cols), F32)] * 3, compiler_params=_cp("parallel"),
                          name=name)(*[a.reshape(rows, cols) for a in (g, w, m, v)])
    return tuple(o.reshape(shape) for o in outs)


_REPLICATED = ("e_norm", "e_gmlp_w", "e_gmlp_b", "e_conv_b", "e_conv_ln_g", "e_conv_ln_b", "o_lam_re", "o_lam_im", "o_log_dt",
               "o_b_re", "o_b_im", "o_c_re", "o_c_im", "ca_norm", "ca_mem_norm", "ffn_norm", "final_norm")
_ORDER = ("e_norm", "e_w_in", "e_gmlp_w", "e_gmlp_b", "e_conv_w", "e_conv_b", "e_conv_ln_g", "e_conv_ln_b", "e_w_out",
          "o_norm", "o_w_in", "o_lam_re", "o_lam_im", "o_log_dt", "o_b_re", "o_b_im", "o_c_re", "o_c_im", "o_d", "o_w_out",
          "ca_norm", "ca_mem_norm", "ca_wq", "ca_wk", "ca_wv", "ca_wo", "ffn_norm", "ffn_w_gate", "ffn_w_up", "ffn_w_down",
          "final_norm")


def _rows128(a, multiple=8):
    flat = a.reshape(-1)
    rows = -(-flat.shape[0] // (LANES * multiple)) * multiple
    return jnp.pad(flat, (0, rows * LANES - flat.shape[0])).reshape(rows, LANES)


def _shard(full, axis):
    s = full.shape
    return jnp.moveaxis(full.reshape(s[:axis] + (N_DEV, s[axis] // N_DEV) + s[axis + 1:]), axis, 0)


_UNITS = (("e_w_in", 0, True), ("e_w_out", 0, False), ("o_w_in", 0, False), ("o_w_out", 0, True),
          *[(n, i, False) for n in ("ca_wq", "ca_wk", "ca_wv", "ca_wo") for i in (0, 1)],
          *[(n, i, tr) for n, tr in (("ffn_w_gate", True), ("ffn_w_up", True), ("ffn_w_down", False)) for i in (0, 1)])
_LAYERED = ("ca_wq", "ca_wk", "ca_wv", "ca_wo", "ffn_w_gate", "ffn_w_up", "ffn_w_down")
_SMALL_SHARDED = (("e_conv_w", 2), ("o_norm", 1), ("o_d", 1))
RS_ROW = 1024


def _unit_key(name, tr):
    return name + "_t" if tr else name


def _stage_of(name, layer):
    if name.startswith("e_"):
        return 0 if name == "e_w_in" else 1
    if name.startswith("o_"):
        return 2
    if name.startswith("ca_"):
        return 1 if layer == 0 else 3
    return 2 if layer == 0 else 3


def weight_fetcher(local):
    groups, meta = [[] for _ in range(4)], [[] for _ in range(4)]
    for name, layer, tr in _UNITS:
        blk = local[name][layer]
        st = _stage_of(name, layer)
        groups[st].append(_bf(blk.T if tr else blk))
        meta[st].append((name, layer, tr))
    small = jnp.concatenate([local[name].reshape(-1) for name, _ in _SMALL_SHARDED])
    groups[0].append(_rows128(small))
    direct = [False, False, False, True]
    sems, srcs, lands, token = gather_ici_start("ag_w_start", groups, direct)

    def fetch(stage, after):
        bufs = gather_ici_wait(f"ag_w_wait{stage}", srcs[stage], lands[stage], sems[stage], after, direct[stage])
        if not direct[stage]:
            bufs = gather_d2d(f"ag_w_d2d{stage}", bufs)
        got = {}
        for (name, layer, tr), blk, buf in zip(meta[stage], groups[stage], bufs):
            arr = buf.reshape((N_DEV * blk.shape[0],) + tuple(blk.shape[1:]))
            if name in _LAYERED:
                got[(_unit_key(name, tr), layer)] = arr
            else:
                got[_unit_key(name, tr)] = arr
        if stage == 0:
            flat = bufs[-1].reshape(N_DEV, -1)
            off = 0
            for name, axis in _SMALL_SHARDED:
                blk = local[name]
                seg = flat[:, off:off + blk.size].reshape((N_DEV,) + blk.shape)
                off += blk.size
                seg = jnp.moveaxis(seg, 0, axis)
                got[name] = seg.reshape(seg.shape[:axis] + (-1,) + seg.shape[axis + 2:])
            got["e_conv_w"] = got["e_conv_w"][0]
        return got

    return fetch, token


def _grad_stage_of(name, layer):
    if name.startswith("e_"):
        return 4
    if name.startswith("o_"):
        return 1
    if name.startswith("ca_"):
        return 3 if layer == 0 else 1
    return 2 if layer == 0 else 0


GRAD_STAGES = 5
SMALL_ROWS = 16


def gradient_reducer(local, mom, var):
    me = (4 * lax.axis_index("x") + 2 * lax.axis_index("y") + lax.axis_index("c")).astype(jnp.int32).reshape(1)
    pending = []

    def start(stage, grads, carry):
        units = [u for u in _UNITS if _grad_stage_of(u[0], u[1]) == stage]
        parts, spans = [], []
        for name, layer, tr in units:
            key = _unit_key(name, tr)
            g = grads[(key, layer)] if name in _LAYERED else grads[key]
            part = g.reshape(4, 2, -1, RS_ROW)
            spans.append((part.shape[2], g.shape[0] // N_DEV, g.shape[1]))
            parts.append(part)
        if stage == GRAD_STAGES - 1:
            small = jnp.concatenate([_shard(grads[name], axis).reshape(N_DEV, -1) for name, axis in _SMALL_SHARDED], axis=1)
            small = jnp.pad(small, ((0, 0), (0, SMALL_ROWS * RS_ROW - small.shape[1])))
            parts.append(small.astype(BF16).reshape(4, 2, SMALL_ROWS, RS_ROW))
        pack = jnp.concatenate(parts, axis=2)
        pack = pack.reshape((N_DEV,) + pack.shape[2:])
        sems, own, land, carry = scatter_start(f"rs_start{stage}", pack, carry)
        pending.append((stage, units, spans, sems, own, land))
        return carry

    def finish(after):
        res, per_layer, small_flat = {}, {}, None
        for stage, units, spans, sems, own, land in pending:
            own, land = scatter_wait(f"rs_wait{stage}", own, land, sems, after)
            total = sum_shares(f"rs_sum{stage}", own, land, me)
            off = 0
            for (name, layer, tr), (rows, r, c) in zip(units, spans):
                per_layer.setdefault(name, {})[layer] = (total[off:off + rows].reshape(r, c), tr)
                off += rows
            if stage == GRAD_STAGES - 1:
                small_flat = total[off:off + SMALL_ROWS].reshape(-1)
        for name, by_layer in per_layer.items():
            pieces = [by_layer[i][0] for i in sorted(by_layer)]
            res[name] = adamw_units("adamw_" + name, pieces, by_layer[0][1], local[name], mom[name], var[name])
        off = 0
        for name, _ in _SMALL_SHARDED:
            blk = local[name]
            g = small_flat[off:off + blk.size].reshape(blk.shape)
            off += blk.size
            res[name] = (g,) + adamw_native("adamw_" + name, g, blk, mom[name], var[name])
        return res

    return start, finish


def replicated_start(grads, loss):
    pack = jnp.concatenate([_rows128(grads[name]) for name in _REPLICATED] + [_rows128(loss)], axis=0)
    sems, srcs, lands, token = gather_ici_start("ag_g_start", [[pack]], [False])
    return sems[0], srcs[0], lands[0], token


def replicated_finish(handle, after, w, mom, var):
    sems, srcs, lands, _ = handle
    (buf,) = gather_d2d("ag_g_d2d", gather_ici_wait("ag_g_wait", srcs, lands, sems, after))
    rows = srcs[0].shape[0]
    total = sum_slots("ag_g_sum", buf.reshape(N_DEV, rows, LANES))
    res, off = {}, 0
    for name in _REPLICATED:
        n = w[name].size
        nr = -(-n // (LANES * 8)) * 8
        g = total[off:off + nr].reshape(-1)[:n].reshape(w[name].shape)
        off += nr
        res[name] = (g,) + adamw_native("adamw_" + name, g, w[name], mom[name], var[name])
    return res, total[off, 0]


def kernel(x, mem, e_norm, e_w_in, e_gmlp_w, e_gmlp_b, e_conv_w, e_conv_b, e_conv_ln_g, e_conv_ln_b, e_w_out, o_norm, o_w_in, o_lam_re, o_lam_im, o_log_dt, o_b_re, o_b_im, o_c_re, o_c_im, o_d, o_w_out, ca_norm, ca_mem_norm, ca_wq, ca_wk, ca_wv, ca_wo, ffn_norm, ffn_w_gate, ffn_w_up, ffn_w_down, final_norm, loss_target, m_e_norm, m_e_w_in, m_e_gmlp_w, m_e_gmlp_b, m_e_conv_w, m_e_conv_b, m_e_conv_ln_g, m_e_conv_ln_b, m_e_w_out, m_o_norm, m_o_w_in, m_o_lam_re, m_o_lam_im, m_o_log_dt, m_o_b_re, m_o_b_im, m_o_c_re, m_o_c_im, m_o_d, m_o_w_out, m_ca_norm, m_ca_mem_norm, m_ca_wq, m_ca_wk, m_ca_wv, m_ca_wo, m_ffn_norm, m_ffn_w_gate, m_ffn_w_up, m_ffn_w_down, m_final_norm, v_e_norm, v_e_w_in, v_e_gmlp_w, v_e_gmlp_b, v_e_conv_w, v_e_conv_b, v_e_conv_ln_g, v_e_conv_ln_b, v_e_w_out, v_o_norm, v_o_w_in, v_o_lam_re, v_o_lam_im, v_o_log_dt, v_o_b_re, v_o_b_im, v_o_c_re, v_o_c_im, v_o_d, v_o_w_out, v_ca_norm, v_ca_mem_norm, v_ca_wq, v_ca_wk, v_ca_wv, v_ca_wo, v_ffn_norm, v_ffn_w_gate, v_ffn_w_up, v_ffn_w_down, v_final_norm):
    given = dict(locals())
    local = {k: given[k] for k in _ORDER}
    mom = {k: given["m_" + k] for k in _ORDER}
    var = {k: given["v_" + k] for k in _ORDER}

    w = {}
    w.update({
        "e_norm": e_norm, "e_gmlp_w": e_gmlp_w[0], "e_gmlp_b": e_gmlp_b.reshape(A_GROUPS, GMLP_BLOCK, 1),
        "e_conv_b": e_conv_b, "e_conv_ln_g": e_conv_ln_g, "e_conv_ln_b": e_conv_ln_b,
        "o_lam_re": o_lam_re[0], "o_lam_im": o_lam_im[0], "o_log_dt": o_log_dt[0], "o_b_re": o_b_re[0], "o_b_im": o_b_im[0],
        "o_c_re": o_c_re[0], "o_c_im": o_c_im[0], "ca_norm": ca_norm, "ca_mem_norm": ca_mem_norm, "ffn_norm": ffn_norm,
        "final_norm": final_norm.reshape(1, D_MODEL),
    })
    start_reduce, finish_reduce = gradient_reducer(local, mom, var)
    fetch, token = weight_fetcher(local)
    loss_part, grad_x, grads = local_step(x[0], mem[0], loss_target[0], w, fetch, start_reduce, token[0:1, 0:1])
    grads["final_norm"] = grads["final_norm"].reshape(D_MODEL)

    handle = replicated_start(grads, loss_part)
    res = finish_reduce(handle[3])
    rep, loss = replicated_finish(handle, res["ffn_w_down"][1], local, mom, var)
    res.update(rep)
    return (loss, grad_x[None], *[res[k][0] for k in _ORDER], *[res[k][1] for k in _ORDER],
            *[res[k][2] for k in _ORDER], *[res[k][3] for k in _ORDER])
```

```python
import jax
import jax.numpy as jnp
from jax import lax
from jax.experimental import pallas as pl
from jax.experimental.pallas import tpu as pltpu

F32 = jnp.float32
BF16 = jnp.bfloat16
S = jax.ShapeDtypeStruct

D_MODEL = 1024
A_WIDTH = 512
A_GROUPS = 4
GMLP_BLOCK = 128
CHUNK = 64
B_WIDTH = 512
IN_WIDTH = 2 * A_WIDTH + 2 * B_WIDTH
CONV_WIDTH = 31
CONV_PAD = 32
C_WIDTH = 512
C_GROUP_CH = 16
C_GROUPS = 32
C_STATE = 64
N_STATE = C_GROUPS * C_STATE
CA_HEADS = 4
CA_HEAD_DIM = 256
FFN_HIDDEN = 2816
EPS = 1e-6
ADAM_LR = 0.001
ADAM_B1 = 0.9
ADAM_B2 = 0.999
ADAM_EPS = 1e-08
ADAM_WD = 0.01
ADAM_STEP = 10
N_DEV = 8
LANES = 128
VMEM_LIMIT = 56 << 20
VMEM_BUDGET = 40 << 20
MM_TN_RESIDENT = 8 << 20
MESH = pl.DeviceIdType.MESH
ANY = pl.BlockSpec(memory_space=pl.ANY)


def _cp(*sem):
    return pltpu.CompilerParams(dimension_semantics=sem, vmem_limit_bytes=VMEM_LIMIT)


def _tile(n, pref):
    t = pref
    while n % t:
        t //= 2
    return t


def _bf(v):
    return v if v.dtype == BF16 else v.astype(BF16)


def _sigmoid(x):
    return 1.0 / (1.0 + jnp.exp(-x))


_GC = 0.7978845608028654


def _gelu(x):
    return 0.5 * x * (1.0 + jnp.tanh(_GC * (x + 0.044715 * x * x * x)))


def _gelu_grad(x):
    x2 = x * x
    t = jnp.tanh(_GC * (x + 0.044715 * x * x2))
    return 0.5 * (1.0 + t) + 0.5 * x * (1.0 - t * t) * _GC * (1.0 + 3.0 * 0.044715 * x2)


def _tspec(entry, tm):
    if isinstance(entry, tuple):
        arr, cb, width = entry
        return arr, pl.BlockSpec((tm, width), lambda i, cb=cb: (i, cb))
    return entry, pl.BlockSpec((tm, entry.shape[1]), lambda i: (i, 0))


def rows_call(name, fn, tiled, full, outs, accs, tm=256):
    pairs = [_tspec(e, tm) for e in tiled]
    arrs = [p[0] for p in pairs]
    rows = arrs[0].shape[0]
    tm = _tile(rows, tm)
    pairs = [_tspec(e, tm) for e in tiled]
    n_in = len(tiled) + len(full)
    n_out = len(outs)

    def body(*refs):
        vals = [r[...] for r in refs[:n_in]]
        o_refs = refs[n_in:n_in + n_out]
        a_refs = refs[n_in + n_out:]
        ov, av = fn(*vals)
        for r, v in zip(o_refs, ov):
            r[...] = v.astype(r.dtype)
        if a_refs:
            @pl.when(pl.program_id(0) == 0)
            def _():
                for r in a_refs:
                    r[...] = jnp.zeros(r.shape, r.dtype)
            for r, v in zip(a_refs, av):
                r[...] += v

    in_specs = [p[1] for p in pairs] + [pl.BlockSpec(a.shape, lambda i, nd=a.ndim: (0,) * nd) for a in full]
    out_specs = [pl.BlockSpec((tm, c), lambda i: (i, 0)) for c, _ in outs]
    out_specs += [pl.BlockSpec(s, lambda i, nd=len(s): (0,) * nd) for s in accs]
    out_shape = [S((rows, c), dt) for c, dt in outs] + [S(s, F32) for s in accs]
    return pl.pallas_call(body, grid=(rows // tm,), in_specs=in_specs, out_specs=out_specs, out_shape=out_shape,
                          compiler_params=_cp("arbitrary"), name=name)(*arrs, *full)


def mm_nn(name, m, n, pairs, n_acc, epi, outs, tiled=(), cols=(), rowv=(), sums=(), norm_gain=None):
    a_ops, a_slot, b_arrs, b_specs, idx, trans = [], [], [], [], [], []
    fixed = 0
    for pair in pairs:
        a, b, k = pair[:3]
        bt = len(pair) > 3
        arr, cb, kdim = a if isinstance(a, tuple) else (a, 0, a.shape[1])
        key = (id(arr), cb, kdim)
        if key not in [o[0] for o in a_ops]:
            a_ops.append((key, arr, cb, kdim))
        a_slot.append([o[0] for o in a_ops].index(key))
        b_arr, off = b if isinstance(b, tuple) else (b, 0)
        b_arrs.append(b_arr)
        if bt:
            assert off % n == 0 and b_arr.shape[1] == kdim
            b_specs.append(pl.BlockSpec((n, kdim), lambda i, o=off // n: (o, 0), pipeline_mode=pl.Buffered(1)))
        else:
            assert b_arr.shape[1] == n
            b_specs.append(pl.BlockSpec((kdim, n), lambda i, o=off: (o, 0), pipeline_mode=pl.Buffered(1)))
        fixed += kdim * n * b_arr.dtype.itemsize
        idx.append(k)
        trans.append(bt)
    per_row = sum(2 * kdim * arr.dtype.itemsize for _, arr, _, kdim in a_ops)
    per_row += sum(2 * n * t.dtype.itemsize for t in tiled) + sum(2 * n * jnp.dtype(dt).itemsize for dt in outs)
    cn = n if sums or cols else (512 if n % 512 == 0 else 256)
    per_row += (n_acc + 3) * cn * 4
    tm = next((t for t in (1024, 512, 256, 128) if m % t == 0 and fixed + t * per_row <= VMEM_BUDGET), _tile(m, 128))
    n_a, n_p, n_t = len(a_ops), len(pairs), len(tiled)
    n_in = n_a + n_p + n_t + len(cols) + len(rowv)
    normed = norm_gain is not None
    o0 = n_in + normed

    def body(*refs):
        a_vals = [None if normed and i == 0 else _bf(r[...]) for i, r in enumerate(refs[:n_a])]
        if normed:
            xv = refs[0][...]
            rv = lax.rsqrt(jnp.mean(xv * xv, axis=-1, keepdims=True) + EPS)
            a_vals[0] = (xv * rv * refs[n_in][...]).astype(BF16)
            refs[o0 + len(outs)][...] = a_vals[0]
            refs[o0 + len(outs) + 1][...] = rv
        for j in range(n // cn):
            cs = slice(j * cn, (j + 1) * cn)
            accs = [None] * n_acc
            for p in range(n_p):
                av, b_ref = a_vals[a_slot[p]], refs[n_a + p]
                if trans[p]:
                    d = lax.dot_general(av, _bf(b_ref[cs, :]), (((1,), (1,)), ((), ())), preferred_element_type=F32)
                else:
                    d = jnp.dot(av, _bf(b_ref[:, cs]), preferred_element_type=F32)
                accs[idx[p]] = d if accs[idx[p]] is None else accs[idx[p]] + d
            extra = [r[:, cs] for r in refs[n_a + n_p:n_a + n_p + n_t]] + [r[...] for r in refs[n_a + n_p + n_t:n_in - len(rowv)]]
            extra += [r[:, cs] for r in refs[n_in - len(rowv):n_in]]
            ov = epi(accs, *extra)
            for r, v in zip(refs[o0:o0 + len(outs)], ov):
                r[:, cs] = v.astype(r.dtype)
        sv = ov[len(outs):]
        if sums:
            s_refs = refs[o0 + len(outs) + 2 * normed:]

            @pl.when(pl.program_id(0) == 0)
            def _():
                for r in s_refs:
                    r[...] = jnp.zeros(r.shape, r.dtype)
            for r, v in zip(s_refs, sv):
                r[...] += v

    in_specs = [pl.BlockSpec((tm, kdim), lambda i, cb=cb: (i, cb)) for _, _, cb, kdim in a_ops] + b_specs
    in_specs += [pl.BlockSpec((tm, n), lambda i: (i, 0)) for _ in tiled]
    in_specs += [pl.BlockSpec((tm, 1), lambda i: (i, 0)) for _ in cols]
    in_specs += [pl.BlockSpec((1, n), lambda i: (0, 0)) for _ in rowv]
    out_specs = [pl.BlockSpec((tm, n), lambda i: (i, 0)) for _ in outs]
    out_shape = [S((m, n), dt) for dt in outs]
    gain = []
    if normed:
        k0 = a_ops[0][3]
        gain = [norm_gain]
        in_specs.append(pl.BlockSpec((1, k0), lambda i: (0, 0)))
        out_specs += [pl.BlockSpec((tm, k0), lambda i: (i, 0)), pl.BlockSpec((tm, 1), lambda i: (i, 0))]
        out_shape += [S((m, k0), BF16), S((m, 1), F32)]
    out_specs += [pl.BlockSpec(s, lambda i, nd=len(s): (0,) * nd) for s in sums]
    out_shape += [S(s, F32) for s in sums]
    return pl.pallas_call(body, grid=(m // tm,), in_specs=in_specs, out_specs=out_specs, out_shape=out_shape,
                          compiler_params=_cp("arbitrary" if sums else "parallel"),
                          name=name)(*[o[1] for o in a_ops], *b_arrs, *tiled, *cols, *rowv, *gain)


def mm_tn(name, a, b, out_dtype=BF16):
    if isinstance(a, tuple):
        a_arr, a_cb, m = a
    else:
        a_arr, a_cb, m = a, None, a.shape[1]
    if isinstance(b, tuple):
        b_arr, b_cb, n = b
    else:
        b_arr, b_cb, n = b, None, b.shape[1]
    t = a_arr.shape[0]
    whole_b = t * n * b_arr.dtype.itemsize <= MM_TN_RESIDENT and b_cb is None
    tn = n if whole_b else _tile(n, 512)
    tm = _tile(m, 512 if t * 512 * a_arr.dtype.itemsize * 2 + t * tn * b_arr.dtype.itemsize * 2 <= VMEM_BUDGET else 256)
    a_off = 0 if a_cb is None else a_cb * (m // tm)
    b_off = 0 if b_cb is None else b_cb * (n // tn)

    def body(a_ref, b_ref, o_ref):
        o_ref[...] = lax.dot_general(_bf(a_ref[...]), _bf(b_ref[...]), (((0,), (0,)), ((), ())),
                                     preferred_element_type=F32).astype(o_ref.dtype)

    if whole_b:
        b_spec = pl.BlockSpec((t, n), lambda i, j: (0, 0), pipeline_mode=pl.Buffered(1))
    else:
        b_spec = pl.BlockSpec((t, tn), lambda i, j: (0, j + b_off))
    return pl.pallas_call(
        body, grid=(m // tm, n // tn),
        in_specs=[pl.BlockSpec((t, tm), lambda i, j: (0, i + a_off)), b_spec],
        out_specs=pl.BlockSpec((tm, tn), lambda i, j: (i, j)), out_shape=S((m, n), out_dtype),
        compiler_params=_cp("parallel", "parallel"), name=name)(a_arr, b_arr)


def rms_bwd_gain_only(name, dxn, x, r):
    def fn(dv, xv, rv):
        return [], [jnp.sum(dv * xv * rv, axis=0, keepdims=True)]
    return rows_call(name, fn, [dxn, x, r], [], [], [(1, x.shape[1])])[0]


def _final_loss_epi(accs, res, tv, g):
    xv = res + accs[0]
    d = xv.shape[-1]
    r = lax.rsqrt(jnp.mean(xv * xv, axis=-1, keepdims=True) + EPS)
    xh = xv * r
    err = xh * g - tv
    dy = err * (1.0 / d)
    w = dy * g
    dx = r * (w - xh * jnp.mean(w * xh, axis=-1, keepdims=True))
    part = jnp.sum(jnp.sum(err * err, axis=-1, keepdims=True), axis=0, keepdims=True) * (0.5 / d)
    return [dx, dx, jnp.sum(dy * xh, axis=0, keepdims=True), part]


def _gmlp_mask():
    row = lax.broadcasted_iota(jnp.int32, (GMLP_BLOCK, GMLP_BLOCK), 0) // CHUNK
    col = lax.broadcasted_iota(jnp.int32, (GMLP_BLOCK, GMLP_BLOCK), 1) // CHUNK
    return col <= row


def _ln_plain(v):
    mu = jnp.mean(v, axis=-1, keepdims=True)
    vc = v - mu
    rstd = lax.rsqrt(jnp.mean(vc * vc, axis=-1, keepdims=True) + EPS)
    return vc * rstd, rstd


def gmlp_fwd(name, proj, w, b, tm=512):
    t = proj.shape[0]
    tm = _tile(t, tm)

    def body(au_ref, av_ref, w_ref, b_ref, o_ref):
        mask = _gmlp_mask()
        u = _gelu(au_ref[...])
        vn, _ = _ln_plain(_gelu(av_ref[...]))
        vnb = _bf(vn)
        for g in range(A_GROUPS):
            wg = _bf(jnp.where(mask, w_ref[g], 0.0))
            cs = slice(g * GMLP_BLOCK, (g + 1) * GMLP_BLOCK)
            for n in range(tm // GMLP_BLOCK):
                rs = slice(n * GMLP_BLOCK, (n + 1) * GMLP_BLOCK)
                sg = jnp.dot(wg, vnb[rs, cs], preferred_element_type=F32) + b_ref[g]
                o_ref[rs, cs] = (u[rs, cs] * sg).astype(o_ref.dtype)

    return pl.pallas_call(
        body, grid=(t // tm,),
        in_specs=[pl.BlockSpec((tm, A_WIDTH), lambda i: (i, 0)), pl.BlockSpec((tm, A_WIDTH), lambda i: (i, 1)),
                  pl.BlockSpec(w.shape, lambda i: (0, 0, 0)), pl.BlockSpec(b.shape, lambda i: (0, 0, 0))],
        out_specs=pl.BlockSpec((tm, A_WIDTH), lambda i: (i, 0)), out_shape=S((t, A_WIDTH), BF16),
        compiler_params=_cp("parallel"), name=name)(proj, proj, w, b)


def gmlp_bwd(name, proj, dcat, w, b, tm=512):
    t = proj.shape[0]
    tm = _tile(t, tm)

    def body(au_ref, av_ref, do_ref, w_ref, b_ref, dp_ref, dw_ref, db_ref):
        @pl.when(pl.program_id(0) == 0)
        def _():
            dw_ref[...] = jnp.zeros(dw_ref.shape, F32)
            db_ref[...] = jnp.zeros(db_ref.shape, F32)

        mask = _gmlp_mask()
        au = au_ref[...]
        av = av_ref[...]
        u = _gelu(au)
        vn, rstd = _ln_plain(_gelu(av))
        vnb = _bf(vn)
        dout = do_ref[...]
        dvn_cols = []
        for g in range(A_GROUPS):
            wm = jnp.where(mask, w_ref[g], 0.0)
            wg = _bf(wm)
            wgt = _bf(wm.T)
            cs = slice(g * GMLP_BLOCK, (g + 1) * GMLP_BLOCK)
            dwg = jnp.zeros((GMLP_BLOCK, GMLP_BLOCK), F32)
            dbg = jnp.zeros((GMLP_BLOCK, 1), F32)
            dvn_rows = []
            for n in range(tm // GMLP_BLOCK):
                rs = slice(n * GMLP_BLOCK, (n + 1) * GMLP_BLOCK)
                sg = jnp.dot(wg, vnb[rs, cs], preferred_element_type=F32) + b_ref[g]
                dp_ref[rs, cs] = (dout[rs, cs] * sg * _gelu_grad(au[rs, cs])).astype(dp_ref.dtype)
                dsg = dout[rs, cs] * u[rs, cs]
                dsgb = _bf(dsg)
                dbg = dbg + jnp.sum(dsg, axis=1, keepdims=True)
                dwg = dwg + lax.dot_general(dsgb, vnb[rs, cs], (((1,), (1,)), ((), ())), preferred_element_type=F32)
                dvn_rows.append(jnp.dot(wgt, dsgb, preferred_element_type=F32))
            dw_ref[g] += jnp.where(mask, dwg, 0.0)
            db_ref[g] += dbg
            dvn_cols.append(jnp.concatenate(dvn_rows, axis=0))
        dvn = jnp.concatenate(dvn_cols, axis=1)
        dv = rstd * (dvn - jnp.mean(dvn, axis=-1, keepdims=True) - vn * jnp.mean(dvn * vn, axis=-1, keepdims=True))
        dp_ref[:, A_WIDTH:] = (dv * _gelu_grad(av)).astype(dp_ref.dtype)

    return pl.pallas_call(
        body, grid=(t // tm,),
        in_specs=[pl.BlockSpec((tm, A_WIDTH), lambda i: (i, 0)), pl.BlockSpec((tm, A_WIDTH), lambda i: (i, 1)),
                  pl.BlockSpec((tm, A_WIDTH), lambda i: (i, 0)),
                  pl.BlockSpec(w.shape, lambda i: (0, 0, 0)), pl.BlockSpec(b.shape, lambda i: (0, 0, 0))],
        out_specs=[pl.BlockSpec((tm, 2 * A_WIDTH), lambda i: (i, 0)),
                   pl.BlockSpec(w.shape, lambda i: (0, 0, 0)), pl.BlockSpec(b.shape, lambda i: (0, 0, 0))],
        out_shape=[S((t, 2 * A_WIDTH), BF16), S(w.shape, F32), S(b.shape, F32)],
        compiler_params=_cp("arbitrary"), name=name)(proj, proj, dcat, w, b)


CONV_ROWS = 256


def conv_fwd(name, proj, w, cb):
    t = proj.shape[0]
    tc = LANES
    rows = _tile(t, CONV_ROWS)
    a_cb, g_cb = 2 * A_WIDTH // tc, (2 * A_WIDTH + B_WIDTH) // tc

    def body(a_ref, g_ref, w_ref, cb_ref, o_ref, hpad):
        hpad[0:CONV_PAD, :] = jnp.zeros((CONV_PAD, tc), F32)

        def fill(i, _):
            r0 = pl.multiple_of(i * rows, rows)
            hpad[pl.ds(CONV_PAD + r0, rows), :] = a_ref[pl.ds(r0, rows), :] * _sigmoid(g_ref[pl.ds(r0, rows), :])
            return 0
        lax.fori_loop(0, t // rows, fill, 0)

        def conv(i, _):
            r0 = pl.multiple_of(i * rows, rows)
            win = hpad[pl.ds(r0, rows + CONV_PAD), :]
            acc = jnp.zeros((rows, tc), F32) + cb_ref[...]
            for b in range(SUB):
                wb = win if b == 0 else pltpu.roll(win, b, 0)
                for a in range(CONV_PAD // SUB):
                    k = CONV_WIDTH - 1 - (SUB * a + b)
                    if k >= 0:
                        lo = CONV_PAD - SUB * a
                        acc = acc + wb[lo:lo + rows, :] * w_ref[k:k + 1, :]
            o_ref[pl.ds(r0, rows), :] = acc
            return 0
        lax.fori_loop(0, t // rows, conv, 0)

    return pl.pallas_call(
        body, grid=(B_WIDTH // tc,),
        in_specs=[pl.BlockSpec((t, tc), lambda j: (0, a_cb + j)), pl.BlockSpec((t, tc), lambda j: (0, g_cb + j)),
                  pl.BlockSpec((CONV_WIDTH, tc), lambda j: (0, j)), pl.BlockSpec((1, tc), lambda j: (0, j))],
        out_specs=pl.BlockSpec((t, tc), lambda j: (0, j)), out_shape=S((t, B_WIDTH), F32),
        scratch_shapes=[pltpu.VMEM((t + CONV_PAD, tc), F32)],
        compiler_params=_cp("parallel"), name=name)(proj, proj, w, cb)


def conv_bwd(name, proj, dhc, w):
    t = proj.shape[0]
    tc = LANES
    rows = _tile(t, CONV_ROWS)
    a_cb, g_cb = 2 * A_WIDTH // tc, (2 * A_WIDTH + B_WIDTH) // tc
    win_rows = rows + CONV_PAD

    def body(a_ref, g_ref, d_ref, w_ref, da_ref, dg_ref, dw_ref, dcb_ref, hpad, dpad, dwacc):
        hpad[0:CONV_PAD, :] = jnp.zeros((CONV_PAD, tc), F32)
        dpad[t:t + CONV_PAD, :] = jnp.zeros((CONV_PAD, tc), F32)
        dwacc[...] = jnp.zeros(dwacc.shape, F32)

        def fill(i, _):
            r0 = pl.multiple_of(i * rows, rows)
            hpad[pl.ds(CONV_PAD + r0, rows), :] = a_ref[pl.ds(r0, rows), :] * _sigmoid(g_ref[pl.ds(r0, rows), :])
            dpad[pl.ds(r0, rows), :] = d_ref[pl.ds(r0, rows), :]
            return 0
        lax.fori_loop(0, t // rows, fill, 0)

        def step(i, dcb):
            r0 = pl.multiple_of(i * rows, rows)
            hwin = hpad[pl.ds(r0, win_rows), :]
            dwin = dpad[pl.ds(r0, win_rows), :]
            dchunk = dwin[:rows, :]
            dh = jnp.zeros((rows, tc), F32)
            for b in range(SUB):
                hb = hwin if b == 0 else pltpu.roll(hwin, b, 0)
                db = dwin if b == 0 else pltpu.roll(dwin, win_rows - b, 0)
                for a in range(CONV_PAD // SUB):
                    k = CONV_WIDTH - 1 - (SUB * a + b)
                    if k >= 0:
                        dh = dh + db[SUB * a:SUB * a + rows, :] * w_ref[k:k + 1, :]
                        lo = CONV_PAD - SUB * a
                        prod = dchunk * hb[lo:lo + rows, :]
                        dwacc[k] += jnp.sum(prod.reshape(rows // 8, 8, tc), axis=0)
            a = a_ref[pl.ds(r0, rows), :]
            sg = _sigmoid(g_ref[pl.ds(r0, rows), :])
            da_ref[pl.ds(r0, rows), :] = (dh * sg).astype(da_ref.dtype)
            dg_ref[pl.ds(r0, rows), :] = (dh * a * sg * (1.0 - sg)).astype(dg_ref.dtype)
            return dcb + jnp.sum(dchunk, axis=0, keepdims=True)
        dcb = lax.fori_loop(0, t // rows, step, jnp.zeros((1, tc), F32))
        dcb_ref[...] = dcb
        for k in range(CONV_WIDTH):
            dw_ref[k:k + 1, :] = jnp.sum(dwacc[k], axis=0, keepdims=True)

    return pl.pallas_call(
        body, grid=(B_WIDTH // tc,),
        in_specs=[pl.BlockSpec((t, tc), lambda j: (0, a_cb + j)), pl.BlockSpec((t, tc), lambda j: (0, g_cb + j)),
                  pl.BlockSpec((t, tc), lambda j: (0, j)), pl.BlockSpec((CONV_WIDTH, tc), lambda j: (0, j))],
        out_specs=[pl.BlockSpec((t, tc), lambda j: (0, j)), pl.BlockSpec((t, tc), lambda j: (0, j)),
                   pl.BlockSpec((CONV_WIDTH, tc), lambda j: (0, j)), pl.BlockSpec((1, tc), lambda j: (0, j))],
        out_shape=[S((t, B_WIDTH), BF16), S((t, B_WIDTH), BF16), S((CONV_WIDTH, B_WIDTH), F32), S((1, B_WIDTH), F32)],
        scratch_shapes=[pltpu.VMEM((t + CONV_PAD, tc), F32), pltpu.VMEM((t + CONV_PAD, tc), F32),
                        pltpu.VMEM((CONV_WIDTH, 8, tc), F32)],
        compiler_params=_cp("parallel"), name=name)(proj, proj, dhc, w)


def ln_silu_fwd(name, hc, g, b):
    def fn(h, gv, bv):
        y, _ = _ln_plain(h)
        z = y * gv + bv
        return [z * _sigmoid(z)], []
    return rows_call(name, fn, [hc], [g, b], [(hc.shape[1], BF16)], [])[0]


def ln_silu_bwd(name, hc, dcat, g, b):
    c = hc.shape[1]

    def fn(h, dout, gv, bv):
        y, rstd = _ln_plain(h)
        z = y * gv + bv
        s = _sigmoid(z)
        dz = dout * s * (1.0 + z * (1.0 - s))
        dyv = dz * gv
        dh = rstd * (dyv - jnp.mean(dyv, axis=-1, keepdims=True) - y * jnp.mean(dyv * y, axis=-1, keepdims=True))
        return [dh], [jnp.sum(dz * y, axis=0, keepdims=True), jnp.sum(dz, axis=0, keepdims=True)]

    return rows_call(name, fn, [hc, (dcat, 1, c)], [g, b], [(c, F32)], [(1, c), (1, c)])


_NT = (((1,), (1,)), ((), ()))
_TN = (((0,), (0,)), ((), ()))


def attn_fwd(name, q, k, v, tm=512):
    t, d = q.shape
    m = k.shape[0]
    tm = _tile(t, tm)
    scale = CA_HEAD_DIM ** -0.5

    def body(q_ref, k_ref, v_ref, o_ref):
        for h in range(CA_HEADS):
            cs = slice(h * CA_HEAD_DIM, (h + 1) * CA_HEAD_DIM)
            s = lax.dot_general(q_ref[:, cs], k_ref[:, cs], _NT, preferred_element_type=F32) * scale
            e = jnp.exp(s - jnp.max(s, axis=-1, keepdims=True))
            p = e / jnp.sum(e, axis=-1, keepdims=True)
            o_ref[:, cs] = jnp.dot(_bf(p), v_ref[:, cs], preferred_element_type=F32).astype(o_ref.dtype)

    return pl.pallas_call(
        body, grid=(t // tm,),
        in_specs=[pl.BlockSpec((tm, d), lambda i: (i, 0)), pl.BlockSpec((m, d), lambda i: (0, 0)),
                  pl.BlockSpec((m, d), lambda i: (0, 0))],
        out_specs=pl.BlockSpec((tm, d), lambda i: (i, 0)), out_shape=S((t, d), BF16),
        compiler_params=_cp("parallel"), name=name)(q, k, v)


def attn_bwd(name, q, k, v, do, tm=512):
    t, d = q.shape
    m = k.shape[0]
    tm = _tile(t, tm)
    scale = CA_HEAD_DIM ** -0.5

    def body(q_ref, k_ref, v_ref, do_ref, dq_ref, dk_ref, dv_ref):
        @pl.when(pl.program_id(0) == 0)
        def _():
            dk_ref[...] = jnp.zeros(dk_ref.shape, F32)
            dv_ref[...] = jnp.zeros(dv_ref.shape, F32)

        for h in range(CA_HEADS):
            cs = slice(h * CA_HEAD_DIM, (h + 1) * CA_HEAD_DIM)
            qh, kh, vh, doh = q_ref[:, cs], k_ref[:, cs], v_ref[:, cs], do_ref[:, cs]
            s = lax.dot_general(qh, kh, _NT, preferred_element_type=F32) * scale
            e = jnp.exp(s - jnp.max(s, axis=-1, keepdims=True))
            p = e / jnp.sum(e, axis=-1, keepdims=True)
            pb = _bf(p)
            dv_ref[:, cs] += lax.dot_general(pb, doh, _TN, preferred_element_type=F32)
            dp = lax.dot_general(doh, vh, _NT, preferred_element_type=F32)
            ds = _bf(p * (dp - jnp.sum(dp * p, axis=-1, keepdims=True)) * scale)
            dq_ref[:, cs] = jnp.dot(ds, kh, preferred_element_type=F32).astype(dq_ref.dtype)
            dk_ref[:, cs] += lax.dot_general(ds, qh, _TN, preferred_element_type=F32)

    return pl.pallas_call(
        body, grid=(t // tm,),
        in_specs=[pl.BlockSpec((tm, d), lambda i: (i, 0)), pl.BlockSpec((m, d), lambda i: (0, 0)),
                  pl.BlockSpec((m, d), lambda i: (0, 0)), pl.BlockSpec((tm, d), lambda i: (i, 0))],
        out_specs=[pl.BlockSpec((tm, d), lambda i: (i, 0)), pl.BlockSpec((m, d), lambda i: (0, 0)),
                   pl.BlockSpec((m, d), lambda i: (0, 0))],
        out_shape=[S((t, d), BF16), S((m, d), F32), S((m, d), F32)],
        compiler_params=_cp("arbitrary"), name=name)(q, k, v, do)


SUB = 8
S5_ROWS = 256


S5_BLOCKS = 4
BLOCK_CH = C_WIDTH // S5_BLOCKS
BLOCK_ST = N_STATE // S5_BLOCKS
_S5_BLOCKS = tuple((slice(BLOCK_CH * q, BLOCK_CH * (q + 1)), slice(BLOCK_ST * q, BLOCK_ST * (q + 1)),
                    slice(N_STATE + BLOCK_ST * q, N_STATE + BLOCK_ST * (q + 1))) for q in range(S5_BLOCKS))
_HI = lax.Precision.HIGHEST
_GP = (C_GROUPS, C_STATE)
_RP = (C_WIDTH, C_STATE)


def _zoh(lr, li, ldt):
    dt = jnp.exp(ldt)
    mag = jnp.exp(lr * dt)
    ar = mag * jnp.cos(li * dt)
    ai = mag * jnp.sin(li * dt)
    den = lr * lr + li * li
    qr = ((ar - 1.0) * lr + ai * li) / den
    qi = (ai * lr - (ar - 1.0) * li) / den
    return dt, ar, ai, den, qr, qi


def _per_channel(v):
    return jnp.broadcast_to(v[:, None, :], (C_GROUPS, C_GROUP_CH, C_STATE)).reshape(_RP)


def _same_group(shape, row_per_group, col_per_group):
    rows = lax.broadcasted_iota(jnp.int32, shape, 0) // row_per_group
    cols = lax.broadcasted_iota(jnp.int32, shape, 1) // col_per_group
    return rows == cols


def _spread(shape, axis):
    long = lax.broadcasted_iota(jnp.int32, shape, axis) % C_STATE
    short = lax.broadcasted_iota(jnp.int32, shape, 1 - axis)
    return long == short


def s5_discretise(name, lam_re, lam_im, log_dt, bt_re, bt_im):
    def body(lr_ref, li_ref, ldt_ref, btr_ref, bti_ref, a_ref, bbr_ref, bbi_ref):
        _, ar, ai, _, qr, qi = _zoh(lr_ref[...], li_ref[...], ldt_ref[...])
        a_ref[0] = ar
        a_ref[1] = ai
        q2r, q2i = _per_channel(qr), _per_channel(qi)
        btr, bti = btr_ref[...], bti_ref[...]
        bbr_ref[...] = q2r * btr - q2i * bti
        bbi_ref[...] = q2r * bti + q2i * btr

    return pl.pallas_call(body, out_shape=[S((2,) + _GP, F32), S(_RP, F32), S(_RP, F32)],
                          name=name)(lam_re, lam_im, log_dt, bt_re, bt_im)


def s5_operands(name, a, bbr, bbi, c2r, c2i, ctr, cti):
    ns = N_STATE

    def body(a_ref, bbr_ref, bbi_ref, c2r_ref, c2i_ref, ctr_ref, cti_ref, pw_ref, qw_ref, mb_ref, mc_ref, mct_ref):
        ar, ai = a_ref[0:1, :], a_ref[1:2, :]
        pows = [(ar, ai)]
        for _ in range(SUB - 1):
            pr, pi = pows[-1]
            pows.append((pr * ar - pi * ai, pr * ai + pi * ar))
        rows = lax.broadcasted_iota(jnp.int32, (SUB, ns), 0)

        def rows_of(v):
            return jnp.broadcast_to(v, (SUB, ns))

        for k, s in enumerate((1, 2, 4)):
            pr, pi = rows_of(pows[s - 1][0]), rows_of(pows[s - 1][1])
            pw_ref[k, 0] = jnp.where(rows >= s, pr, 0.0)
            pw_ref[k, 1] = jnp.where(rows >= s, pi, 0.0)
            qw_ref[k, 0] = jnp.where(rows + s <= SUB - 1, pr, 0.0)
            qw_ref[k, 1] = jnp.where(rows + s <= SUB - 1, -pi, 0.0)
        fr = fi = br = bi = jnp.zeros((SUB, ns), F32)
        for i in range(SUB):
            fr = jnp.where(rows == i, rows_of(pows[i][0]), fr)
            fi = jnp.where(rows == i, rows_of(pows[i][1]), fi)
            br = jnp.where(rows == i, rows_of(pows[SUB - 1 - i][0]), br)
            bi = jnp.where(rows == i, rows_of(-pows[SUB - 1 - i][1]), bi)
        pw_ref[3, 0], pw_ref[3, 1], qw_ref[3, 0], qw_ref[3, 1] = fr, fi, br, bi

        wide = _spread((C_STATE, ns), 1).astype(BF16)
        tall = _spread((ns, C_STATE), 0).astype(BF16)
        in_rows = _same_group((C_WIDTH, ns), C_GROUP_CH, C_STATE)
        in_cols = _same_group((ns, C_WIDTH), C_STATE, C_GROUP_CH)

        def across(v, sign=1.0):
            return jnp.where(in_rows, sign * jnp.dot(_bf(v), wide, preferred_element_type=F32), 0.0).astype(BF16)

        def down(vt, sign=1.0):
            return jnp.where(in_cols, sign * jnp.dot(tall, _bf(vt), preferred_element_type=F32), 0.0).astype(BF16)

        mb_ref[:, 0:ns] = across(bbr_ref[...])
        mb_ref[:, ns:2 * ns] = across(bbi_ref[...])
        mct_ref[:, 0:ns] = across(c2r_ref[...])
        mct_ref[:, ns:2 * ns] = across(c2i_ref[...], -1.0)
        mc_ref[0:ns, :] = down(ctr_ref[...])
        mc_ref[ns:2 * ns, :] = down(cti_ref[...], -1.0)

    return pl.pallas_call(
        body, out_shape=[S((4, 2, SUB, ns), F32), S((4, 2, SUB, ns), F32), S((C_WIDTH, 2 * ns), BF16),
                         S((2 * ns, C_WIDTH), BF16), S((C_WIDTH, 2 * ns), BF16)],
        compiler_params=pltpu.CompilerParams(vmem_limit_bytes=VMEM_LIMIT), name=name)(a, bbr, bbi, c2r, c2i, ctr, cti)


def s5_block_grads(name, u, lamb, xsb, dyb):
    t = u.shape[0]

    def mb_body(u_ref, lr_ref, li_ref, o_ref):
        ub = _bf(u_ref[...])
        o_ref[:, 0:BLOCK_ST] = lax.dot_general(ub, lr_ref[...], _TN, preferred_element_type=F32)
        o_ref[:, BLOCK_ST:2 * BLOCK_ST] = lax.dot_general(ub, li_ref[...], _TN, preferred_element_type=F32)

    d_mb = pl.pallas_call(
        mb_body, grid=(S5_BLOCKS,),
        in_specs=[pl.BlockSpec((t, BLOCK_CH), lambda q: (0, q)), pl.BlockSpec((t, BLOCK_ST), lambda q: (0, q)),
                  pl.BlockSpec((t, BLOCK_ST), lambda q: (0, S5_BLOCKS + q))],
        out_specs=pl.BlockSpec((BLOCK_CH, 2 * BLOCK_ST), lambda q: (q, 0)), out_shape=S((C_WIDTH, 2 * BLOCK_ST), F32),
        compiler_params=_cp("parallel"), name=name + "_b")(u, lamb, lamb)

    def mc_body(x_ref, dy_ref, o_ref):
        o_ref[...] = lax.dot_general(x_ref[...], dy_ref[...], _TN, preferred_element_type=F32)

    d_mc = pl.pallas_call(
        mc_body, grid=(2, S5_BLOCKS),
        in_specs=[pl.BlockSpec((t, BLOCK_ST), lambda p, q: (0, p * S5_BLOCKS + q)), pl.BlockSpec((t, BLOCK_CH), lambda p, q: (0, q))],
        out_specs=pl.BlockSpec((BLOCK_ST, BLOCK_CH), lambda p, q: (p * S5_BLOCKS + q, 0)),
        out_shape=S((2 * N_STATE, BLOCK_CH), F32), compiler_params=_cp("parallel", "parallel"), name=name + "_c")(xsb, dyb)
    return d_mb, d_mc


def s5_param_grads(name, d_mb, d_mc, da, lam_re, lam_im, log_dt, bt_re, bt_im):
    ns = N_STATE

    def body(dmb_ref, dmc_ref, da_ref, lr_ref, li_ref, ldt_ref, btr_ref, bti_ref,
             glr_ref, gli_ref, gdt_ref, gbr_ref, gbi_ref, gcr_ref, gci_ref):
        lr, li = lr_ref[...], li_ref[...]
        dt, ar, ai, den, qr, qi = _zoh(lr, li, ldt_ref[...])
        per_block = C_GROUPS // S5_BLOCKS
        wide = _spread((C_STATE, BLOCK_ST), 1).astype(F32)
        tall = _spread((BLOCK_ST, C_STATE), 0).astype(F32)
        rows = lax.broadcasted_iota(jnp.int32, (C_WIDTH, BLOCK_ST), 0) // C_GROUP_CH % per_block
        in_rows = rows == lax.broadcasted_iota(jnp.int32, (C_WIDTH, BLOCK_ST), 1) // C_STATE
        in_cols = _same_group((BLOCK_ST, BLOCK_CH), C_STATE, C_GROUP_CH)

        def fold_rows(v):
            return lax.dot_general(jnp.where(in_rows, v, 0.0), wide, (((1,), (1,)), ((), ())), precision=_HI,
                                   preferred_element_type=F32)

        def fold_cols(v):
            return lax.dot_general(jnp.where(in_cols, v, 0.0), tall, (((0,), (0,)), ((), ())), precision=_HI,
                                   preferred_element_type=F32)

        for cs, s_re, s_im in _S5_BLOCKS:
            gcr_ref[cs, :] = fold_cols(dmc_ref[s_re, :])
            gci_ref[cs, :] = -fold_cols(dmc_ref[s_im, :])
        gbbr = fold_rows(dmb_ref[:, 0:BLOCK_ST])
        gbbi = fold_rows(dmb_ref[:, BLOCK_ST:2 * BLOCK_ST])
        btr, bti = btr_ref[...], bti_ref[...]
        q2r, q2i = _per_channel(qr), _per_channel(qi)
        gbr_ref[...] = q2r * gbbr + q2i * gbbi
        gbi_ref[...] = q2r * gbbi - q2i * gbbr

        def per_group(v):
            return jnp.sum(v.reshape(C_GROUPS, C_GROUP_CH, C_STATE), axis=1)

        gqr = per_group(btr * gbbr + bti * gbbi)
        gqi = per_group(btr * gbbi - bti * gbbr)
        ilr, ili = lr / den, li / den
        gar = da_ref[0] + ilr * gqr - ili * gqi
        gai = da_ref[1] + ilr * gqi + ili * gqr
        sr = (qr * lr + qi * li) / den
        si = (qi * lr - qr * li) / den
        gzr = ar * gar + ai * gai
        gzi = ar * gai - ai * gar
        glr_ref[...] = -sr * gqr - si * gqi + dt * gzr
        gli_ref[...] = -sr * gqi + si * gqr + dt * gzi
        gdt_ref[...] = jnp.sum(lr * gzr + li * gzi, axis=1, keepdims=True) * dt

    return pl.pallas_call(
        body, out_shape=[S(_GP, F32), S(_GP, F32), S((C_GROUPS, 1), F32), S(_RP, F32), S(_RP, F32), S(_RP, F32), S(_RP, F32)],
        compiler_params=pltpu.CompilerParams(vmem_limit_bytes=VMEM_LIMIT), name=name,
    )(d_mb, d_mc, da, lam_re, lam_im, log_dt, bt_re, bt_im)


def _cmul_add(xr, xi, pr, pi, zr, zi):
    return xr + pr * zr - pi * zi, xi + pr * zi + pi * zr


def s5_fwd(name, u, mb, mc, pw, dskip):
    t = u.shape[0]
    tm = _tile(t, S5_ROWS)
    ns = N_STATE

    def body(u_ref, mb_ref, mc_ref, pw_ref, d_ref, gy_ref, y_ref, xs_ref, xb_ref, carry):
        @pl.when(pl.program_id(0) == 0)
        def _():
            carry[...] = jnp.zeros(carry.shape, F32)

        uv = u_ref[...]
        ub = _bf(uv)
        for cs, s_re, s_im in _S5_BLOCKS:
            xs_ref[:, s_re] = jnp.dot(ub[:, cs], mb_ref[cs, s_re], preferred_element_type=F32)
            xs_ref[:, s_im] = jnp.dot(ub[:, cs], mb_ref[cs, s_im], preferred_element_type=F32)

        def group(i, _):
            r0 = pl.multiple_of(i * SUB, SUB)
            xr = xs_ref[pl.ds(r0, SUB), 0:ns]
            xi = xs_ref[pl.ds(r0, SUB), ns:2 * ns]
            for k, s in enumerate((1, 2, 4)):
                xr, xi = _cmul_add(xr, xi, pw_ref[k, 0], pw_ref[k, 1], pltpu.roll(xr, s, 0), pltpu.roll(xi, s, 0))
            xr, xi = _cmul_add(xr, xi, pw_ref[3, 0], pw_ref[3, 1], carry[0], carry[1])
            xs_ref[pl.ds(r0, SUB), 0:ns] = xr
            xs_ref[pl.ds(r0, SUB), ns:2 * ns] = xi
            carry[0] = jnp.broadcast_to(xr[SUB - 1:SUB, :], (SUB, ns))
            carry[1] = jnp.broadcast_to(xi[SUB - 1:SUB, :], (SUB, ns))
            return 0
        lax.fori_loop(0, tm // SUB, group, 0)

        xb_ref[...] = _bf(xs_ref[...])
        for cs, s_re, s_im in _S5_BLOCKS:
            y = (jnp.dot(xb_ref[:, s_re], mc_ref[s_re, cs], preferred_element_type=F32)
                 + jnp.dot(xb_ref[:, s_im], mc_ref[s_im, cs], preferred_element_type=F32) + d_ref[:, cs] * uv[:, cs])
            y_ref[:, cs] = y
            gy_ref[:, cs] = _gelu(y).astype(gy_ref.dtype)

    c = u.shape[1]
    return pl.pallas_call(
        body, grid=(t // tm,),
        in_specs=[pl.BlockSpec((tm, c), lambda i: (i, 0)), pl.BlockSpec(mb.shape, lambda i: (0, 0)),
                  pl.BlockSpec(mc.shape, lambda i: (0, 0)), pl.BlockSpec(pw.shape, lambda i: (0, 0, 0, 0)),
                  pl.BlockSpec((1, c), lambda i: (0, 0))],
        out_specs=[pl.BlockSpec((tm, c), lambda i: (i, 0)), pl.BlockSpec((tm, c), lambda i: (i, 0)),
                   pl.BlockSpec((tm, 2 * ns), lambda i: (i, 0)), pl.BlockSpec((tm, 2 * ns), lambda i: (i, 0))],
        out_shape=[S((t, c), BF16), S((t, c), F32), S((t, 2 * ns), F32), S((t, 2 * ns), BF16)],
        scratch_shapes=[pltpu.VMEM((2, SUB, ns), F32)],
        compiler_params=_cp("arbitrary"), name=name)(u, mb, mc, pw, dskip)


def s5_bwd(name, dgy, y, u, xs, mct, mbt, qw, dskip):
    t, c = u.shape
    tm = _tile(t, S5_ROWS)
    nt = t // tm
    ns = N_STATE
    ng = tm // SUB

    def body(dgy_ref, y_ref, u_ref, xs_ref, mct_ref, mbt_ref, qw_ref, d_ref,
             du_ref, dy_ref, lb_ref, da_ref, dd_ref, lam, carry):
        @pl.when(pl.program_id(0) == 0)
        def _():
            carry[...] = jnp.zeros(carry.shape, F32)
            da_ref[...] = jnp.zeros(da_ref.shape, F32)
            dd_ref[...] = jnp.zeros(dd_ref.shape, F32)

        uv = u_ref[...]
        dy = dgy_ref[...] * _gelu_grad(y_ref[...])
        dyb = _bf(dy)
        dy_ref[...] = dyb
        dd_ref[...] += jnp.sum(dy * uv, axis=0, keepdims=True)
        for cs, s_re, s_im in _S5_BLOCKS:
            lam[:, s_re] = jnp.dot(dyb[:, cs], mct_ref[cs, s_re], preferred_element_type=F32)
            lam[:, s_im] = jnp.dot(dyb[:, cs], mct_ref[cs, s_im], preferred_element_type=F32)
        last_row = lax.broadcasted_iota(jnp.int32, (SUB, ns), 0) == SUB - 1

        def group(j, _):
            i = ng - 1 - j
            r0 = pl.multiple_of(i * SUB, SUB)
            lr = lam[pl.ds(r0, SUB), 0:ns]
            li = lam[pl.ds(r0, SUB), ns:2 * ns]
            for k, s in enumerate((1, 2, 4)):
                lr, li = _cmul_add(lr, li, qw_ref[k, 0], qw_ref[k, 1],
                                   pltpu.roll(lr, SUB - s, 0), pltpu.roll(li, SUB - s, 0))
            cr, ci = carry[0], carry[1]
            lr, li = _cmul_add(lr, li, qw_ref[3, 0], qw_ref[3, 1], cr, ci)
            lam[pl.ds(r0, SUB), 0:ns] = lr
            lam[pl.ds(r0, SUB), ns:2 * ns] = li
            carry[0] = jnp.broadcast_to(lr[0:1, :], (SUB, ns))
            carry[1] = jnp.broadcast_to(li[0:1, :], (SUB, ns))
            nr = jnp.where(last_row, cr, pltpu.roll(lr, SUB - 1, 0))
            ni = jnp.where(last_row, ci, pltpu.roll(li, SUB - 1, 0))
            xr = xs_ref[pl.ds(r0, SUB), 0:ns]
            xi = xs_ref[pl.ds(r0, SUB), ns:2 * ns]
            da_ref[0] += nr * xr + ni * xi
            da_ref[1] += ni * xr - nr * xi
            return 0
        lax.fori_loop(0, ng, group, 0)

        lb_ref[...] = _bf(lam[...])
        for cs, s_re, s_im in _S5_BLOCKS:
            du = (jnp.dot(lb_ref[:, s_re], mbt_ref[s_re, cs], preferred_element_type=F32)
                  + jnp.dot(lb_ref[:, s_im], mbt_ref[s_im, cs], preferred_element_type=F32) + d_ref[:, cs] * dy[:, cs])
            du_ref[:, cs] = du.astype(du_ref.dtype)

    rev = lambda i: (nt - 1 - i, 0)
    return pl.pallas_call(
        body, grid=(nt,),
        in_specs=[pl.BlockSpec((tm, c), rev), pl.BlockSpec((tm, c), rev), pl.BlockSpec((tm, c), rev),
                  pl.BlockSpec((tm, 2 * ns), rev),
                  pl.BlockSpec(mct.shape, lambda i: (0, 0)), pl.BlockSpec(mbt.shape, lambda i: (0, 0)),
                  pl.BlockSpec(qw.shape, lambda i: (0, 0, 0, 0)), pl.BlockSpec((1, c), lambda i: (0, 0))],
        out_specs=[pl.BlockSpec((tm, c), rev), pl.BlockSpec((tm, c), rev), pl.BlockSpec((tm, 2 * ns), rev),
                   pl.BlockSpec((2, SUB, ns), lambda i: (0, 0, 0)), pl.BlockSpec((1, c), lambda i: (0, 0))],
        out_shape=[S((t, c), BF16), S((t, c), BF16), S((t, 2 * ns), BF16), S((2, SUB, ns), F32), S((1, c), F32)],
        scratch_shapes=[pltpu.VMEM((tm, 2 * ns), F32), pltpu.VMEM((2, SUB, ns), F32)],
        compiler_params=_cp("arbitrary"), name=name)(dgy, y, u, xs, mct, mbt, qw, dskip)


def _first(accs, *_):
    return [accs[0]]


def _rms_bwd_epi(accs, xv, base, rv, g):
    dv = accs[0]
    w = dv * g
    xh = xv * rv
    dx = base + rv * (w - xh * jnp.mean(w * xh, axis=-1, keepdims=True))
    return [dx, dx, jnp.sum(dv * xh, axis=0, keepdims=True)]


def mm_rms_bwd(name, pairs, x, r, gain, dres):
    t, d = x.shape
    return mm_nn(name, t, d, pairs, 1, _rms_bwd_epi, [F32, BF16], tiled=[x, dres], cols=[r], rowv=[gain], sums=[(1, d)])


def _add_res(accs, res):
    return [accs[0] + res]


def even_fwd(x, w, need_out):
    t = x.shape[0]
    proj, hn, r = mm_nn("e_in_f", t, IN_WIDTH, [(x, w["e_w_in_t"], 0, "t")], 1, _first, [F32], norm_gain=w["e_norm"])
    out_a = gmlp_fwd("e_gmlp_f", proj, w["e_gmlp_w"], w["e_gmlp_b"])
    hc = conv_fwd("e_conv_f", proj, w["e_conv_w"], w["e_conv_b"])
    out_b = ln_silu_fwd("e_ln_f", hc, w["e_conv_ln_g"], w["e_conv_ln_b"])
    need_out(out_b)
    (x1,) = mm_nn("e_out_f", t, D_MODEL, [(out_a, (w["e_w_out"], 0), 0), (out_b, (w["e_w_out"], 1), 0)],
                  1, _add_res, [F32], tiled=[x])
    return x1, (x, hn, r, proj, out_a, hc, out_b)


def even_bwd_mixers(dxb, saved, w):
    x, hn, r, proj, out_a, hc, out_b = saved
    t = x.shape[0]
    (dcat,) = mm_nn("e_out_b", t, D_MODEL, [(dxb, w["e_w_out"], 0, "t")], 1, _first, [F32])
    g_w_out = jnp.concatenate([mm_tn("e_out_wa", out_a, dxb), mm_tn("e_out_wb", out_b, dxb)], axis=0)
    dab, g_gw, g_gb = gmlp_bwd("e_gmlp_b", proj, dcat, w["e_gmlp_w"], w["e_gmlp_b"])
    dhc, g_lg, g_lb = ln_silu_bwd("e_ln_b", hc, dcat, w["e_conv_ln_g"], w["e_conv_ln_b"])
    dba, dbg, g_cw, g_cb = conv_bwd("e_conv_b", proj, dhc, w["e_conv_w"])
    g_w_in_t = jnp.concatenate([mm_tn("e_in_w0", dab, hn), mm_tn("e_in_w1", dba, hn), mm_tn("e_in_w2", dbg, hn)], axis=0)
    grads = dict(e_w_in_t=g_w_in_t, e_gmlp_w=g_gw[None], e_gmlp_b=g_gb.reshape(1, A_GROUPS, GMLP_BLOCK),
                 e_conv_w=g_cw[None], e_conv_b=g_cb, e_conv_ln_g=g_lg, e_conv_ln_b=g_lb, e_w_out=g_w_out)
    return (dab, dba, dbg), grads


def even_bwd_input(dx, dproj, saved, w):
    x, _, r = saved[:3]
    dab, dba, dbg = dproj
    w_in_t = w["e_w_in_t"]
    return mm_rms_bwd("e_in_b", [(dab, (w_in_t, 0), 0), (dba, (w_in_t, 2), 0), (dbg, (w_in_t, 3), 0)], x, r, w["e_norm"], dx)


def s5_setup(w, anchor=None):
    def rows(v):
        return v.transpose(0, 2, 1).reshape(_RP)

    log_dt = w["o_log_dt"].reshape(C_GROUPS, 1)
    if anchor is not None:
        log_dt = log_dt + anchor
    lam = (w["o_lam_re"], w["o_lam_im"], log_dt, rows(w["o_b_re"]), rows(w["o_b_im"]))
    a, bbr, bbi = s5_discretise("o_s5_zoh", *lam)
    c_re, c_im = w["o_c_re"], w["o_c_im"]
    pw, qw, mb, mc, mct = s5_operands("o_s5_ops", a.reshape(2, N_STATE), bbr, bbi, c_re.reshape(_RP), c_im.reshape(_RP),
                                      c_re.transpose(2, 0, 1).reshape(C_STATE, C_WIDTH),
                                      c_im.transpose(2, 0, 1).reshape(C_STATE, C_WIDTH))
    return dict(lam=lam, pw=pw, qw=qw, mb=mb, mc=mc, mct=mct, mbt=mb.T)


def odd_fwd(x, w, consts):
    t = x.shape[0]
    u, hn, r = mm_nn("o_in_f", t, C_WIDTH, [(x, w["o_w_in"], 0)], 1, _first, [F32], norm_gain=w["o_norm"])
    gy, y, xs, xsb = s5_fwd("o_s5_f", u, consts["mb"], consts["mc"], consts["pw"], w["o_d"])
    w_out_t = w["o_w_out_t"]

    def epi(accs, res):
        return [res + accs[0] * _sigmoid(accs[1]), accs[0], accs[1]]

    x1, o1, o2 = mm_nn("o_out_f", t, D_MODEL, [(gy, (w_out_t, 0), 0, "t"), (gy, (w_out_t, D_MODEL), 1, "t")], 2, epi,
                       [F32, BF16, BF16], tiled=[x])
    return x1, (x, hn, r, u, gy, y, xs, xsb, o1, o2)


def odd_bwd(dx, dxb, saved, w, consts):
    x, hn, r, u, gy, y, xs, xsb, o1, o2 = saved
    t = x.shape[0]

    def gate_bwd(dv, a, b):
        a = a.astype(F32)
        sg = _sigmoid(b.astype(F32))
        return [jnp.concatenate([dv * sg, dv * a * sg * (1.0 - sg)], axis=1)], []

    (do12,) = rows_call("o_gate_b", gate_bwd, [dx, o1, o2], [], [(2 * D_MODEL, BF16)], [])
    (dgy,) = mm_nn("o_out_b", t, C_WIDTH, [(do12, w["o_w_out_t"], 0)], 1, _first, [F32])
    g_w_out_t = mm_tn("o_out_w", do12, gy)
    du, dyb, lamb, da8, g_d = s5_bwd("o_s5_b", dgy, y, u, xs, consts["mct"], consts["mbt"], consts["qw"], w["o_d"])
    d_mb, d_mc = s5_block_grads("o_s5_w", u, lamb, xsb, dyb)
    da = jnp.sum(da8, axis=1).reshape((2,) + _GP)
    g_lr, g_li, g_dt, g_btr, g_bti, g_cr, g_ci = s5_param_grads("o_s5_pg", d_mb, d_mc, da, *consts["lam"])

    def states_first(v):
        return v.reshape(C_GROUPS, C_GROUP_CH, C_STATE).transpose(0, 2, 1)[None]

    g_w_in = mm_tn("o_in_w", hn, du)
    dx0, dx0b, g_norm = mm_rms_bwd("o_in_b", [(du, w["o_w_in"], 0, "t")], x, r, w["o_norm"], dx)
    grads = dict(o_norm=g_norm, o_w_in=g_w_in, o_lam_re=g_lr[None], o_lam_im=g_li[None], o_log_dt=g_dt.reshape(1, C_GROUPS),
                 o_b_re=states_first(g_btr), o_b_im=states_first(g_bti),
                 o_c_re=g_cr.reshape((1, C_GROUPS, C_GROUP_CH, C_STATE)), o_c_im=g_ci.reshape((1, C_GROUPS, C_GROUP_CH, C_STATE)),
                 o_d=g_d, o_w_out_t=g_w_out_t)
    return dx0, dx0b, grads


def ca_fwd(i, x, mem, w):
    t, m = x.shape[0], mem.shape[0]
    q, xn, r = mm_nn(f"ca{i}_q_f", t, D_MODEL, [(x, w["ca_wq"][i], 0)], 1, _first, [BF16], norm_gain=w["ca_norm"][i:i + 1])
    k, v, mn, rm = mm_nn(f"ca{i}_kv_f", m, D_MODEL, [(mem, w["ca_wk"][i], 0), (mem, w["ca_wv"][i], 1)], 2,
                         lambda accs: [accs[0], accs[1]], [BF16, BF16], norm_gain=w["ca_mem_norm"][i:i + 1])
    o = attn_fwd(f"ca{i}_attn_f", q, k, v)
    (x1,) = mm_nn(f"ca{i}_o_f", t, D_MODEL, [(o, w["ca_wo"][i], 0)], 1, _add_res, [F32], tiled=[x])
    return x1, (x, xn, r, mn, rm, q, k, v, o)


def ca_bwd(i, dx, dxb, saved, mem, w):
    x, xn, r, mn, rm, q, k, v, o = saved
    t, m = x.shape[0], mem.shape[0]
    (do,) = mm_nn(f"ca{i}_o_b", t, D_MODEL, [(dxb, w["ca_wo"][i], 0, "t")], 1, _first, [BF16])
    g_wo = mm_tn(f"ca{i}_o_w", o, dxb)
    dq, dk, dv = attn_bwd(f"ca{i}_attn_b", q, k, v, do)
    g_wq = mm_tn(f"ca{i}_q_w", xn, dq)
    g_wk = mm_tn(f"ca{i}_k_w", mn, dk)
    g_wv = mm_tn(f"ca{i}_v_w", mn, dv)
    (dmn,) = mm_nn(f"ca{i}_kv_b", m, D_MODEL, [(dk, w["ca_wk"][i], 0, "t"), (dv, w["ca_wv"][i], 0, "t")], 1, _first, [F32])
    g_mnorm = rms_bwd_gain_only(f"ca{i}_mnorm_b", dmn, mem, rm)
    dx0, dx0b, g_norm = mm_rms_bwd(f"ca{i}_q_b", [(dq, w["ca_wq"][i], 0, "t")], x, r, w["ca_norm"][i:i + 1], dx)
    return dx0, dx0b, dict(ca_norm=g_norm, ca_mem_norm=g_mnorm, ca_wq=g_wq, ca_wk=g_wk, ca_wv=g_wv, ca_wo=g_wo)


def ffn_fwd(i, x, w, target=None):
    t = x.shape[0]

    def epi(accs):
        g, u = accs
        s = _sigmoid(g)
        silu = g * s
        return [u * (s + silu * (1.0 - s)), silu, silu * u]

    dgate, dup, h, xn, r = mm_nn(f"ffn{i}_up_f", t, FFN_HIDDEN, [(x, w["ffn_w_gate_t"][i], 0, "t"), (x, w["ffn_w_up_t"][i], 1, "t")],
                           2, epi, [BF16, BF16, BF16], norm_gain=w["ffn_norm"][i:i + 1])
    down = [(h, w["ffn_w_down"][i], 0)]
    if target is None:
        (out,) = mm_nn(f"ffn{i}_down_f", t, D_MODEL, down, 1, _add_res, [F32], tiled=[x])
    else:
        out = mm_nn(f"ffn{i}_down_f", t, D_MODEL, down, 1, _final_loss_epi, [F32, BF16], tiled=[x, target],
                    rowv=[w["final_norm"]], sums=[(1, D_MODEL), (1, 1)])
    return out, (x, xn, r, dgate, dup, h)


def ffn_bwd(i, dx, dxb, saved, w):
    x, xn, r, dgate, dup, h = saved
    t = x.shape[0]

    def epi(accs, dgv, duv):
        dh = accs[0]
        return [dh * dgv.astype(F32), dh * duv.astype(F32)]

    dg, du = mm_nn(f"ffn{i}_down_b", t, FFN_HIDDEN, [(dxb, w["ffn_w_down"][i], 0, "t")], 1, epi, [BF16, BF16],
                   tiled=[dgate, dup])
    g_wd = mm_tn(f"ffn{i}_down_w", h, dxb)
    g_wg_t = mm_tn(f"ffn{i}_gate_w", dg, xn)
    g_wu_t = mm_tn(f"ffn{i}_up_w", du, xn)
    dx0, dx0b, g_norm = mm_rms_bwd(f"ffn{i}_up_b", [(dg, w["ffn_w_gate_t"][i], 0), (du, w["ffn_w_up_t"][i], 0)], x, r,
                                   w["ffn_norm"][i:i + 1], dx)
    return dx0, dx0b, dict(ffn_norm=g_norm, ffn_w_gate_t=g_wg_t, ffn_w_up_t=g_wu_t, ffn_w_down=g_wd)


def local_step(x, mem, target, w, fetch=None, on_grads=None, anchor=None):
    consts = s5_setup(w, anchor)

    def need(stage, after):
        if fetch is not None:
            for k, v in fetch(stage, after).items():
                if isinstance(k, tuple):
                    w.setdefault(k[0], {})[k[1]] = v
                else:
                    w[k] = v

    need(0, consts["pw"])
    x1, s_e = even_fwd(x, w, lambda after: need(1, after))
    x2, s_c0 = ca_fwd(0, x1, mem, w)
    need(2, x2)
    x3, s_f0 = ffn_fwd(0, x2, w)
    x4, s_o = odd_fwd(x3, w, consts)
    need(3, x4)
    x5, s_c1 = ca_fwd(1, x4, mem, w)
    (dx, dxb, g_final, loss), s_f1 = ffn_fwd(1, x5, w, target)

    def emit(stage, carry, plain, layered=None, layer=0):
        if on_grads is None:
            return carry
        out = dict(plain)
        out.update({(k, layer): v for k, v in (layered or {}).items()})
        return on_grads(stage, out, list(carry))

    dx, dxb, g_f1 = ffn_bwd(1, dx, dxb, s_f1, w)
    dx, dxb = emit(0, (dx, dxb), {}, g_f1, 1)
    dx, dxb, g_c1 = ca_bwd(1, dx, dxb, s_c1, mem, w)
    dx, dxb, g_o = odd_bwd(dx, dxb, s_o, w, consts)
    dx, dxb = emit(1, (dx, dxb), g_o, g_c1, 1)
    dx, dxb, g_f0 = ffn_bwd(0, dx, dxb, s_f0, w)
    dx, dxb = emit(2, (dx, dxb), {}, g_f0, 0)
    dx, dxb, g_c0 = ca_bwd(0, dx, dxb, s_c0, mem, w)
    dx, dxb = emit(3, (dx, dxb), {}, g_c0, 0)
    dproj, g_e = even_bwd_mixers(dxb, s_e, w)
    dproj = emit(4, dproj, {**g_e, "o_norm": g_o["o_norm"], "o_d": g_o["o_d"]})
    dx, dxb, g_e["e_norm"] = even_bwd_input(dx, dproj, s_e, w)

    grads = dict(g_e)
    grads.update(g_o)
    for g0, g1 in ((g_c0, g_c1), (g_f0, g_f1)):
        for k in g0:
            grads[k] = jnp.concatenate([g0[k], g1[k]], axis=0) if k.endswith("norm") else (g0[k], g1[k])
    grads["final_norm"] = g_final
    return loss, dx, grads


def _group(axes):
    pos = {a: lax.axis_index(a) for a in ("x", "y", "c")}
    me = 0
    for a in axes:
        me = me * 2 + pos[a]
    peers = []
    for mask in range(1, 2 ** len(axes)):
        peer = dict(pos)
        for bit, a in enumerate(axes):
            if (mask >> (len(axes) - 1 - bit)) & 1:
                peer[a] = 1 - pos[a]
        idx = 0
        for a in axes:
            idx = idx * 2 + peer[a]
        peers.append((idx, (peer["x"], peer["y"], peer["c"])))
    return me, peers


def _sibling():
    x, y, c = lax.axis_index("x"), lax.axis_index("y"), lax.axis_index("c")
    return c, (x, y, 1 - c)


_HBM =pl.BlockSpec(memory_space=pltpu.HBM)
_SEM = pl.BlockSpec(memory_space=pltpu.SEMAPHORE)
_EFFECT = pltpu.SideEffectType.DATAFLOW_SIDE_EFFECTING


def _gather_peers(direct):
    chip, _ = _group(("x", "y"))
    core = lax.axis_index("c")
    if direct:
        _, peers = _group(_ALL)
        return chip, core, [(idx // 2, idx % 2, dev) for idx, dev in peers]
    _, peers = _group(("x", "y"))
    return chip, core, [(idx, core, dev) for idx, dev in peers]


def gather_ici_start(name, groups, direct):
    flat = [b for g in groups for b in g]
    sizes = [len(g) for g in groups]
    k_ops, n_g = len(flat), len(groups)
    lands = [lax.empty((4, 2) + tuple(b.shape), b.dtype) for b in flat]
    fan = [N_DEV - 1 if d else 3 for d in direct]

    def body(*refs):
        src, land = refs[:k_ops], refs[k_ops:2 * k_ops]
        sems = refs[2 * k_ops:2 * k_ops + 3 * n_g]
        token = refs[-1]
        i = 0
        for g in range(n_g):
            send, recv, loc = sems[3 * g:3 * g + 3]
            chip, core, peers = _gather_peers(direct[g])
            for j in range(sizes[g]):
                pltpu.make_async_copy(src[i], land[i].at[chip, core], loc.at[j]).start()
                for k, (_, _, dev) in enumerate(peers):
                    s = fan[g] * j + k
                    pltpu.make_async_remote_copy(src_ref=src[i], dst_ref=land[i].at[chip, core], send_sem=send.at[s],
                                                 recv_sem=recv.at[s], device_id=dev, device_id_type=MESH).start()
                i += 1
        token[...] = jnp.zeros(token.shape, token.dtype)

    sem_shapes = []
    for s, f in zip(sizes, fan):
        sem_shapes += [pltpu.SemaphoreType.DMA((f * s,)), pltpu.SemaphoreType.DMA((f * s,)), pltpu.SemaphoreType.DMA((s,))]
    thru = [pltpu.HBM(a.shape, a.dtype) for a in flat + lands]
    outs = pl.pallas_call(
        body, name=name, out_shape=tuple(sem_shapes) + tuple(thru) + (S((8, LANES), F32),),
        in_specs=[_HBM] * (2 * k_ops), out_specs=[_SEM] * (3 * n_g) + [_HBM] * (2 * k_ops) + [pl.BlockSpec(memory_space=pltpu.VMEM)],
        input_output_aliases={i: 3 * n_g + i for i in range(2 * k_ops)},
        compiler_params=pltpu.CompilerParams(has_side_effects=_EFFECT),
    )(*[pltpu.with_memory_space_constraint(a, pltpu.HBM) for a in flat + lands])
    sems = [tuple(outs[3 * g:3 * g + 3]) for g in range(n_g)]
    srcs_thru, lands_thru, off = [], [], 3 * n_g
    for s in sizes:
        srcs_thru.append(list(outs[off:off + s]))
        off += s
    for s in sizes:
        lands_thru.append(list(outs[off:off + s]))
        off += s
    return sems, srcs_thru, lands_thru, outs[-1]


def gather_ici_wait(name, srcs, lands, sems, after, direct=False):
    n = len(srcs)

    def body(*refs):
        src, land = refs[:n], refs[n:2 * n]
        send, recv, loc = refs[2 * n:2 * n + 3]
        chip, core, peers = _gather_peers(direct)
        for j in range(n):
            for k, (pchip, pcore, dev) in enumerate(peers):
                s = len(peers) * j + k
                cp = pltpu.make_async_remote_copy(src_ref=src[j], dst_ref=land[j].at[pchip, pcore], send_sem=send.at[s],
                                                  recv_sem=recv.at[s], device_id=dev, device_id_type=MESH)
                cp.wait_send()
                cp.wait_recv()
            pltpu.make_async_copy(src[j], land[j].at[chip, core], loc.at[j]).wait()

    outs = pl.pallas_call(
        body, name=name, out_shape=tuple(pltpu.HBM(a.shape, a.dtype) for a in list(srcs) + list(lands)),
        in_specs=[_HBM] * (2 * n) + [_SEM] * 3 + [ANY], out_specs=[_HBM] * (2 * n),
        input_output_aliases={i: i for i in range(2 * n)},
        compiler_params=pltpu.CompilerParams(has_side_effects=_EFFECT),
    )(*srcs, *lands, *sems, after)
    return list(outs[n:])


def gather_d2d(name, bufs):
    k_ops = len(bufs)

    def body(*refs):
        in_refs, out_refs = refs[:k_ops], refs[k_ops:2 * k_ops]
        send_sems, recv_sems = refs[2 * k_ops:]
        core, sib = _sibling()
        sent, landed = [], []
        for i in range(k_ops):
            cp = pltpu.make_async_remote_copy(src_ref=in_refs[i].at[:, core], dst_ref=out_refs[i].at[:, core],
                                              send_sem=send_sems.at[i], recv_sem=recv_sems.at[i], device_id=sib, device_id_type=MESH)
            cp.start()
            sent.append(cp)
            landed.append(pltpu.make_async_remote_copy(src_ref=in_refs[i].at[:, core], dst_ref=out_refs[i].at[:, 1 - core],
                                                       send_sem=send_sems.at[i], recv_sem=recv_sems.at[i],
                                                       device_id=sib, device_id_type=MESH))
        for cp in landed:
            cp.wait_recv()
        for cp in sent:
            cp.wait_send()

    return pl.pallas_call(
        body, in_specs=[ANY] * k_ops, out_specs=[ANY] * k_ops, out_shape=[S(b.shape, b.dtype) for b in bufs],
        input_output_aliases={i: i for i in range(k_ops)},
        scratch_shapes=[pltpu.SemaphoreType.DMA((k_ops,)), pltpu.SemaphoreType.DMA((k_ops,))],
        name=name)(*bufs)


_ALL = ("x", "y", "c")


def scatter_start(name, arr, carry):
    land = lax.empty(arr.shape, arr.dtype)
    n_c = len(carry)

    def body(*refs):
        in_ref, land_ref = refs[0], refs[1]
        send, recv = refs[2 + n_c], refs[3 + n_c]
        me, peers = _group(_ALL)
        for k, (idx, dev) in enumerate(peers):
            pltpu.make_async_remote_copy(src_ref=in_ref.at[idx], dst_ref=land_ref.at[me], send_sem=send.at[k], recv_sem=recv.at[k],
                                         device_id=dev, device_id_type=MESH).start()

    thru = [arr, land] + list(carry)
    outs = pl.pallas_call(
        body, name=name,
        out_shape=(pltpu.SemaphoreType.DMA((N_DEV - 1,)), pltpu.SemaphoreType.DMA((N_DEV - 1,)))
        + tuple(pltpu.HBM(a.shape, a.dtype) for a in thru),
        in_specs=[_HBM] * len(thru), out_specs=[_SEM, _SEM] + [_HBM] * len(thru),
        input_output_aliases={i: 2 + i for i in range(len(thru))},
        compiler_params=pltpu.CompilerParams(has_side_effects=_EFFECT),
    )(*[pltpu.with_memory_space_constraint(a, pltpu.HBM) for a in thru])
    return (outs[0], outs[1]), outs[2], outs[3], list(outs[4:])


def scatter_wait(name, arr, land, sems, after):
    def body(in_ref, land_ref, send, recv, after_ref, in_thru, land_thru):
        _, peers = _group(_ALL)
        for k, (idx, dev) in enumerate(peers):
            cp = pltpu.make_async_remote_copy(src_ref=in_ref.at[idx], dst_ref=land_ref.at[idx], send_sem=send.at[k],
                                              recv_sem=recv.at[k], device_id=dev, device_id_type=MESH)
            cp.wait_send()
            cp.wait_recv()

    outs = pl.pallas_call(
        body, name=name, out_shape=(pltpu.HBM(arr.shape, arr.dtype), pltpu.HBM(arr.shape, arr.dtype)),
        in_specs=[_HBM, _HBM, _SEM, _SEM, ANY], out_specs=[_HBM, _HBM], input_output_aliases={0: 0, 1: 1},
        compiler_params=pltpu.CompilerParams(has_side_effects=_EFFECT),
    )(arr, land, sems[0], sems[1], after)
    return outs[0], outs[1]


def _row_tile(rows, cap=512):
    return next(t for t in range(cap - cap % 16, 0, -16) if rows % t == 0)


def sum_shares(name, own, recv, me):
    n, rows, c = recv.shape
    tr = _row_tile(rows)

    def body(me_ref, *refs):
        acc = refs[0][...].astype(F32)
        for r in refs[1:n]:
            acc = acc + r[...].astype(F32)
        refs[n][...] = acc

    def slot(mask):
        return pl.BlockSpec((None, tr, c), lambda i, me, mask=mask: (jnp.bitwise_xor(me[0], mask), i, 0))

    spec = pltpu.PrefetchScalarGridSpec(
        num_scalar_prefetch=1, grid=(rows // tr,), in_specs=[slot(k) for k in range(n)],
        out_specs=pl.BlockSpec((tr, c), lambda i, me: (i, 0)))
    return pl.pallas_call(body, grid_spec=spec, out_shape=S((rows, c), F32),
                          compiler_params=_cp("parallel"), name=name)(me, own, *([recv] * (n - 1)))


def sum_slots(name, slots):
    n, r, c = slots.shape

    def body(s_ref, o_ref):
        acc = s_ref[0]
        for j in range(1, n):
            acc = acc + s_ref[j]
        o_ref[...] = acc

    return pl.pallas_call(body, out_shape=S((r, c), F32), compiler_params=pltpu.CompilerParams(vmem_limit_bytes=VMEM_LIMIT),
                          name=name)(slots)


def adamw_units(name, pieces, transposed, w, m, v):
    n_l, k, n = w.shape
    tk = _tile(k, 512) if transposed else k
    c1 = 1.0 - ADAM_B1 ** ADAM_STEP
    c2 = 1.0 - ADAM_B2 ** ADAM_STEP

    def body(*refs):
        p_refs, (w_ref, m_ref, v_ref, g_ref, d_ref, m2_ref, v2_ref) = refs[:n_l], refs[n_l:]
        gv = p_refs[0][...]
        for j in range(1, n_l):
            gv = jnp.where(pl.program_id(0) == j, p_refs[j][...], gv)
        if transposed:
            gv = gv.T
        m2 = ADAM_B1 * m_ref[...] + (1.0 - ADAM_B1) * gv
        v2 = ADAM_B2 * v_ref[...] + (1.0 - ADAM_B2) * (gv * gv)
        g_ref[...] = gv
        m2_ref[...] = m2
        v2_ref[...] = v2
        d_ref[...] = -ADAM_LR * ((m2 / c1) / (jnp.sqrt(v2 / c2) + ADAM_EPS) + ADAM_WD * w_ref[...])

    piece = pl.BlockSpec((n, tk), lambda l, i: (0, i)) if transposed else pl.BlockSpec((k, n), lambda l, i: (0, 0))
    blk = pl.BlockSpec((None, tk, n), lambda l, i: (l, i, 0))
    return tuple(pl.pallas_call(body, grid=(n_l, k // tk), in_specs=[piece] * n_l + [blk] * 3, out_specs=[blk] * 4,
                                out_shape=[S(w.shape, F32)] * 4, compiler_params=_cp("parallel", "parallel"),
                                name=name)(*pieces, w, m, v))


def adamw_native(name, g, w, m, v, tr=512):
    shape = w.shape
    cols = shape[-1]
    rows = w.size // cols
    tr = _tile(rows, tr) if rows % 8 == 0 else rows
    c1 = 1.0 - ADAM_B1 ** ADAM_STEP
    c2 = 1.0 - ADAM_B2 ** ADAM_STEP

    def body(g_ref, w_ref, m_ref, v_ref, d_ref, m2_ref, v2_ref):
        gv = g_ref[...]
        m2 = ADAM_B1 * m_ref[...] + (1.0 - ADAM_B1) * gv
        v2 = ADAM_B2 * v_ref[...] + (1.0 - ADAM_B2) * (gv * gv)
        m2_ref[...] = m2
        v2_ref[...] = v2
        d_ref[...] = -ADAM_LR * ((m2 / c1) / (jnp.sqrt(v2 / c2) + ADAM_EPS) + ADAM_WD * w_ref[...])

    row = pl.BlockSpec((tr, cols), lambda i: (i, 0))
    outs = pl.pallas_call(body, grid=(rows // tr,), in_specs=[row] * 4, out_specs=[row] * 3,
                          out_shape=[S((rows, cols), F32)] * 3, compiler_params=_cp("parallel"),
                          name=name)(*[a.reshape(rows, cols) for a in (g, w, m, v)])
    return tuple(o.reshape(shape) for o in outs)


_REPLICATED = ("e_norm", "e_gmlp_w", "e_gmlp_b", "e_conv_b", "e_conv_ln_g", "e_conv_ln_b", "o_lam_re", "o_lam_im", "o_log_dt",
               "o_b_re", "o_b_im", "o_c_re", "o_c_im", "ca_norm", "ca_mem_norm", "ffn_norm", "final_norm")
_ORDER = ("e_norm", "e_w_in", "e_gmlp_w", "e_gmlp_b", "e_conv_w", "e_conv_b", "e_conv_ln_g", "e_conv_ln_b", "e_w_out",
          "o_norm", "o_w_in", "o_lam_re", "o_lam_im", "o_log_dt", "o_b_re", "o_b_im", "o_c_re", "o_c_im", "o_d", "o_w_out",
          "ca_norm", "ca_mem_norm", "ca_wq", "ca_wk", "ca_wv", "ca_wo", "ffn_norm", "ffn_w_gate", "ffn_w_up", "ffn_w_down",
          "final_norm")


def _rows128(a, multiple=8):
    flat = a.reshape(-1)
    rows = -(-flat.shape[0] // (LANES * multiple)) * multiple
    return jnp.pad(flat, (0, rows * LANES - flat.shape[0])).reshape(rows, LANES)


def _shard(full, axis):
    s = full.shape
    return jnp.moveaxis(full.reshape(s[:axis] + (N_DEV, s[axis] // N_DEV) + s[axis + 1:]), axis, 0)


_UNITS = (("e_w_in", 0, True), ("e_w_out", 0, False), ("o_w_in", 0, False), ("o_w_out", 0, True),
          *[(n, i, False) for n in ("ca_wq", "ca_wk", "ca_wv", "ca_wo") for i in (0, 1)],
          *[(n, i, tr) for n, tr in (("ffn_w_gate", True), ("ffn_w_up", True), ("ffn_w_down", False)) for i in (0, 1)])
_LAYERED = ("ca_wq", "ca_wk", "ca_wv", "ca_wo", "ffn_w_gate", "ffn_w_up", "ffn_w_down")
_SMALL_SHARDED = (("e_conv_w", 2), ("o_norm", 1), ("o_d", 1))
RS_ROW = 1024


def _unit_key(name, tr):
    return name + "_t" if tr else name


def _stage_of(name, layer):
    if name.startswith("e_"):
        return 0 if name == "e_w_in" else 1
    if name.startswith("o_"):
        return 2
    if name.startswith("ca_"):
        return 1 if layer == 0 else 3
    return 2 if layer == 0 else 3


def weight_fetcher(local):
    groups, meta = [[] for _ in range(4)], [[] for _ in range(4)]
    for name, layer, tr in _UNITS:
        blk = local[name][layer]
        st = _stage_of(name, layer)
        groups[st].append(_bf(blk.T if tr else blk))
        meta[st].append((name, layer, tr))
    small = jnp.concatenate([local[name].reshape(-1) for name, _ in _SMALL_SHARDED])
    groups[0].append(_rows128(small))
    direct = [False, False, False, True]
    sems, srcs, lands, token = gather_ici_start("ag_w_start", groups, direct)

    def fetch(stage, after):
        bufs = gather_ici_wait(f"ag_w_wait{stage}", srcs[stage], lands[stage], sems[stage], after, direct[stage])
        if not direct[stage]:
            bufs = gather_d2d(f"ag_w_d2d{stage}", bufs)
        got = {}
        for (name, layer, tr), blk, buf in zip(meta[stage], groups[stage], bufs):
            arr = buf.reshape((N_DEV * blk.shape[0],) + tuple(blk.shape[1:]))
            if name in _LAYERED:
                got[(_unit_key(name, tr), layer)] = arr
            else:
                got[_unit_key(name, tr)] = arr
        if stage == 0:
            flat = bufs[-1].reshape(N_DEV, -1)
            off = 0
            for name, axis in _SMALL_SHARDED:
                blk = local[name]
                seg = flat[:, off:off + blk.size].reshape((N_DEV,) + blk.shape)
                off += blk.size
                seg = jnp.moveaxis(seg, 0, axis)
                got[name] = seg.reshape(seg.shape[:axis] + (-1,) + seg.shape[axis + 2:])
            got["e_conv_w"] = got["e_conv_w"][0]
        return got

    return fetch, token


def _grad_stage_of(name, layer):
    if name.startswith("e_"):
        return 4
    if name.startswith("o_"):
        return 1
    if name.startswith("ca_"):
        return 3 if layer == 0 else 1
    return 2 if layer == 0 else 0


GRAD_STAGES = 5
SMALL_ROWS = 16


def gradient_reducer(local, mom, var):
    me = (4 * lax.axis_index("x") + 2 * lax.axis_index("y") + lax.axis_index("c")).astype(jnp.int32).reshape(1)
    pending = []

    def start(stage, grads, carry):
        units = [u for u in _UNITS if _grad_stage_of(u[0], u[1]) == stage]
        parts, spans = [], []
        for name, layer, tr in units:
            key = _unit_key(name, tr)
            g = grads[(key, layer)] if name in _LAYERED else grads[key]
            part = g.reshape(4, 2, -1, RS_ROW)
            spans.append((part.shape[2], g.shape[0] // N_DEV, g.shape[1]))
            parts.append(part)
        if stage == GRAD_STAGES - 1:
            small = jnp.concatenate([_shard(grads[name], axis).reshape(N_DEV, -1) for name, axis in _SMALL_SHARDED], axis=1)
            small = jnp.pad(small, ((0, 0), (0, SMALL_ROWS * RS_ROW - small.shape[1])))
            parts.append(small.astype(BF16).reshape(4, 2, SMALL_ROWS, RS_ROW))
        pack = jnp.concatenate(parts, axis=2)
        pack = pack.reshape((N_DEV,) + pack.shape[2:])
        sems, own, land, carry = scatter_start(f"rs_start{stage}", pack, carry)
        pending.append((stage, units, spans, sems, own, land))
        return carry

    def finish(after):
        res, per_layer, small_flat = {}, {}, None
        for stage, units, spans, sems, own, land in pending:
            own, land = scatter_wait(f"rs_wait{stage}", own, land, sems, after)
            total = sum_shares(f"rs_sum{stage}", own, land, me)
            off = 0
            for (name, layer, tr), (rows, r, c) in zip(units, spans):
                per_layer.setdefault(name, {})[layer] = (total[off:off + rows].reshape(r, c), tr)
                off += rows
            if stage == GRAD_STAGES - 1:
                small_flat = total[off:off + SMALL_ROWS].reshape(-1)
        for name, by_layer in per_layer.items():
            pieces = [by_layer[i][0] for i in sorted(by_layer)]
            res[name] = adamw_units("adamw_" + name, pieces, by_layer[0][1], local[name], mom[name], var[name])
        off = 0
        for name, _ in _SMALL_SHARDED:
            blk = local[name]
            g = small_flat[off:off + blk.size].reshape(blk.shape)
            off += blk.size
            res[name] = (g,) + adamw_native("adamw_" + name, g, blk, mom[name], var[name])
        return res

    return start, finish


def replicated_start(grads, loss):
    pack = jnp.concatenate([_rows128(grads[name]) for name in _REPLICATED] + [_rows128(loss)], axis=0)
    sems, srcs, lands, token = gather_ici_start("ag_g_start", [[pack]], [False])
    return sems[0], srcs[0], lands[0], token


def replicated_finish(handle, after, w, mom, var):
    sems, srcs, lands, _ = handle
    (buf,) = gather_d2d("ag_g_d2d", gather_ici_wait("ag_g_wait", srcs, lands, sems, after))
    rows = srcs[0].shape[0]
    total = sum_slots("ag_g_sum", buf.reshape(N_DEV, rows, LANES))
    res, off = {}, 0
    for name in _REPLICATED:
        n = w[name].size
        nr = -(-n // (LANES * 8)) * 8
        g = total[off:off + nr].reshape(-1)[:n].reshape(w[name].shape)
        off += nr
        res[name] = (g,) + adamw_native("adamw_" + name, g, w[name], mom[name], var[name])
    return res, total[off, 0]


def kernel(x, mem, e_norm, e_w_in, e_gmlp_w, e_gmlp_b, e_conv_w, e_conv_b, e_conv_ln_g, e_conv_ln_b, e_w_out, o_norm, o_w_in, o_lam_re, o_lam_im, o_log_dt, o_b_re, o_b_im, o_c_re, o_c_im, o_d, o_w_out, ca_norm, ca_mem_norm, ca_wq, ca_wk, ca_wv, ca_wo, ffn_norm, ffn_w_gate, ffn_w_up, ffn_w_down, final_norm, loss_target, m_e_norm, m_e_w_in, m_e_gmlp_w, m_e_gmlp_b, m_e_conv_w, m_e_conv_b, m_e_conv_ln_g, m_e_conv_ln_b, m_e_w_out, m_o_norm, m_o_w_in, m_o_lam_re, m_o_lam_im, m_o_log_dt, m_o_b_re, m_o_b_im, m_o_c_re, m_o_c_im, m_o_d, m_o_w_out, m_ca_norm, m_ca_mem_norm, m_ca_wq, m_ca_wk, m_ca_wv, m_ca_wo, m_ffn_norm, m_ffn_w_gate, m_ffn_w_up, m_ffn_w_down, m_final_norm, v_e_norm, v_e_w_in, v_e_gmlp_w, v_e_gmlp_b, v_e_conv_w, v_e_conv_b, v_e_conv_ln_g, v_e_conv_ln_b, v_e_w_out, v_o_norm, v_o_w_in, v_o_lam_re, v_o_lam_im, v_o_log_dt, v_o_b_re, v_o_b_im, v_o_c_re, v_o_c_im, v_o_d, v_o_w_out, v_ca_norm, v_ca_mem_norm, v_ca_wq, v_ca_wk, v_ca_wv, v_ca_wo, v_ffn_norm, v_ffn_w_gate, v_ffn_w_up, v_ffn_w_down, v_final_norm):
    given = dict(locals())
    local = {k: given[k] for k in _ORDER}
    mom = {k: given["m_" + k] for k in _ORDER}
    var = {k: given["v_" + k] for k in _ORDER}

    w = {}
    w.update({
        "e_norm": e_norm, "e_gmlp_w": e_gmlp_w[0], "e_gmlp_b": e_gmlp_b.reshape(A_GROUPS, GMLP_BLOCK, 1),
        "e_conv_b": e_conv_b, "e_conv_ln_g": e_conv_ln_g, "e_conv_ln_b": e_conv_ln_b,
        "o_lam_re": o_lam_re[0], "o_lam_im": o_lam_im[0], "o_log_dt": o_log_dt[0], "o_b_re": o_b_re[0], "o_b_im": o_b_im[0],
        "o_c_re": o_c_re[0], "o_c_im": o_c_im[0], "ca_norm": ca_norm, "ca_mem_norm": ca_mem_norm, "ffn_norm": ffn_norm,
        "final_norm": final_norm.reshape(1, D_MODEL),
    })
    start_reduce, finish_reduce = gradient_reducer(local, mom, var)
    fetch, token = weight_fetcher(local)
    loss_part, grad_x, grads = local_step(x[0], mem[0], loss_target[0], w, fetch, start_reduce, token[0:1, 0:1])
    grads["final_norm"] = grads["final_norm"].reshape(D_MODEL)

    handle = replicated_start(grads, loss_part)
    res = finish_reduce(handle[3])
    rep, loss = replicated_finish(handle, res["ffn_w_down"][1], local, mom, var)
    res.update(rep)
    return (loss, grad_x[None], *[res[k][0] for k in _ORDER], *[res[k][1] for k in _ORDER],
            *[res[k][2] for k in _ORDER], *[res[k][3] for k in _ORDER])
```

```python
import jax
import jax.numpy as jnp
from jax import lax
from jax.experimental import pallas as pl
from jax.experimental.pallas import tpu as pltpu

F32 = jnp.float32
BF16 = jnp.bfloat16
S = jax.ShapeDtypeStruct

D_MODEL = 1024
A_WIDTH = 512
A_GROUPS = 4
GMLP_BLOCK = 128
CHUNK = 64
B_WIDTH = 512
IN_WIDTH = 2 * A_WIDTH + 2 * B_WIDTH
CONV_WIDTH = 31
CONV_PAD = 32
C_WIDTH = 512
C_GROUP_CH = 16
C_GROUPS = 32
C_STATE = 64
N_STATE = C_GROUPS * C_STATE
CA_HEADS = 4
CA_HEAD_DIM = 256
FFN_HIDDEN = 2816
EPS = 1e-6
ADAM_LR = 0.001
ADAM_B1 = 0.9
ADAM_B2 = 0.999
ADAM_EPS = 1e-08
ADAM_WD = 0.01
ADAM_STEP = 10
N_DEV = 8
LANES = 128
VMEM_LIMIT = 56 << 20
VMEM_BUDGET = 40 << 20
MM_TN_RESIDENT = 8 << 20
MESH = pl.DeviceIdType.MESH
ANY = pl.BlockSpec(memory_space=pl.ANY)


def _cp(*sem):
    return pltpu.CompilerParams(dimension_semantics=sem, vmem_limit_bytes=VMEM_LIMIT)


def _tile(n, pref):
    t = pref
    while n % t:
        t //= 2
    return t


def _bf(v):
    return v if v.dtype == BF16 else v.astype(BF16)


def _sigmoid(x):
    return 1.0 / (1.0 + jnp.exp(-x))


_GC = 0.7978845608028654


def _gelu(x):
    return 0.5 * x * (1.0 + jnp.tanh(_GC * (x + 0.044715 * x * x * x)))


def _gelu_grad(x):
    x2 = x * x
    t = jnp.tanh(_GC * (x + 0.044715 * x * x2))
    return 0.5 * (1.0 + t) + 0.5 * x * (1.0 - t * t) * _GC * (1.0 + 3.0 * 0.044715 * x2)


def _tspec(entry, tm):
    if isinstance(entry, tuple):
        arr, cb, width = entry
        return arr, pl.BlockSpec((tm, width), lambda i, cb=cb: (i, cb))
    return entry, pl.BlockSpec((tm, entry.shape[1]), lambda i: (i, 0))


def rows_call(name, fn, tiled, full, outs, accs, tm=256):
    pairs = [_tspec(e, tm) for e in tiled]
    arrs = [p[0] for p in pairs]
    rows = arrs[0].shape[0]
    tm = _tile(rows, tm)
    pairs = [_tspec(e, tm) for e in tiled]
    n_in = len(tiled) + len(full)
    n_out = len(outs)

    def body(*refs):
        vals = [r[...] for r in refs[:n_in]]
        o_refs = refs[n_in:n_in + n_out]
        a_refs = refs[n_in + n_out:]
        ov, av = fn(*vals)
        for r, v in zip(o_refs, ov):
            r[...] = v.astype(r.dtype)
        if a_refs:
            @pl.when(pl.program_id(0) == 0)
            def _():
                for r in a_refs:
                    r[...] = jnp.zeros(r.shape, r.dtype)
            for r, v in zip(a_refs, av):
                r[...] += v

    in_specs = [p[1] for p in pairs] + [pl.BlockSpec(a.shape, lambda i, nd=a.ndim: (0,) * nd) for a in full]
    out_specs = [pl.BlockSpec((tm, c), lambda i: (i, 0)) for c, _ in outs]
    out_specs += [pl.BlockSpec(s, lambda i, nd=len(s): (0,) * nd) for s in accs]
    out_shape = [S((rows, c), dt) for c, dt in outs] + [S(s, F32) for s in accs]
    return pl.pallas_call(body, grid=(rows // tm,), in_specs=in_specs, out_specs=out_specs, out_shape=out_shape,
                          compiler_params=_cp("arbitrary"), name=name)(*arrs, *full)


def mm_nn(name, m, n, pairs, n_acc, epi, outs, tiled=(), cols=(), rowv=(), sums=(), norm_gain=None):
    a_ops, a_slot, b_arrs, b_specs, idx, trans = [], [], [], [], [], []
    fixed = 0
    for pair in pairs:
        a, b, k = pair[:3]
        bt = len(pair) > 3
        arr, cb, kdim = a if isinstance(a, tuple) else (a, 0, a.shape[1])
        key = (id(arr), cb, kdim)
        if key not in [o[0] for o in a_ops]:
            a_ops.append((key, arr, cb, kdim))
        a_slot.append([o[0] for o in a_ops].index(key))
        b_arr, off = b if isinstance(b, tuple) else (b, 0)
        b_arrs.append(b_arr)
        if bt:
            assert off % n == 0 and b_arr.shape[1] == kdim
            b_specs.append(pl.BlockSpec((n, kdim), lambda i, o=off // n: (o, 0), pipeline_mode=pl.Buffered(1)))
        else:
            assert b_arr.shape[1] == n
            b_specs.append(pl.BlockSpec((kdim, n), lambda i, o=off: (o, 0), pipeline_mode=pl.Buffered(1)))
        fixed += kdim * n * b_arr.dtype.itemsize
        idx.append(k)
        trans.append(bt)
    per_row = sum(2 * kdim * arr.dtype.itemsize for _, arr, _, kdim in a_ops)
    per_row += sum(2 * n * t.dtype.itemsize for t in tiled) + sum(2 * n * jnp.dtype(dt).itemsize for dt in outs)
    cn = n if sums or cols else (512 if n % 512 == 0 else 256)
    per_row += (n_acc + 3) * cn * 4
    tm = next((t for t in (1024, 512, 256, 128) if m % t == 0 and fixed + t * per_row <= VMEM_BUDGET), _tile(m, 128))
    n_a, n_p, n_t = len(a_ops), len(pairs), len(tiled)
    n_in = n_a + n_p + n_t + len(cols) + len(rowv)
    normed = norm_gain is not None
    o0 = n_in + normed

    def body(*refs):
        a_vals = [None if normed and i == 0 else _bf(r[...]) for i, r in enumerate(refs[:n_a])]
        if normed:
            xv = refs[0][...]
            rv = lax.rsqrt(jnp.mean(xv * xv, axis=-1, keepdims=True) + EPS)
            a_vals[0] = (xv * rv * refs[n_in][...]).astype(BF16)
            refs[o0 + len(outs)][...] = a_vals[0]
            refs[o0 + len(outs) + 1][...] = rv
        for j in range(n // cn):
            cs = slice(j * cn, (j + 1) * cn)
            accs = [None] * n_acc
            for p in range(n_p):
                av, b_ref = a_vals[a_slot[p]], refs[n_a + p]
                if trans[p]:
                    d = lax.dot_general(av, _bf(b_ref[cs, :]), (((1,), (1,)), ((), ())), preferred_element_type=F32)
                else:
                    d = jnp.dot(av, _bf(b_ref[:, cs]), preferred_element_type=F32)
                accs[idx[p]] = d if accs[idx[p]] is None else accs[idx[p]] + d
            extra = [r[:, cs] for r in refs[n_a + n_p:n_a + n_p + n_t]] + [r[...] for r in refs[n_a + n_p + n_t:n_in - len(rowv)]]
            extra += [r[:, cs] for r in refs[n_in - len(rowv):n_in]]
            ov = epi(accs, *extra)
            for r, v in zip(refs[o0:o0 + len(outs)], ov):
                r[:, cs] = v.astype(r.dtype)
        sv = ov[len(outs):]
        if sums:
            s_refs = refs[o0 + len(outs) + 2 * normed:]

            @pl.when(pl.program_id(0) == 0)
            def _():
                for r in s_refs:
                    r[...] = jnp.zeros(r.shape, r.dtype)
            for r, v in zip(s_refs, sv):
                r[...] += v

    in_specs = [pl.BlockSpec((tm, kdim), lambda i, cb=cb: (i, cb)) for _, _, cb, kdim in a_ops] + b_specs
    in_specs += [pl.BlockSpec((tm, n), lambda i: (i, 0)) for _ in tiled]
    in_specs += [pl.BlockSpec((tm, 1), lambda i: (i, 0)) for _ in cols]
    in_specs += [pl.BlockSpec((1, n), lambda i: (0, 0)) for _ in rowv]
    out_specs = [pl.BlockSpec((tm, n), lambda i: (i, 0)) for _ in outs]
    out_shape = [S((m, n), dt) for dt in outs]
    gain = []
    if normed:
        k0 = a_ops[0][3]
        gain = [norm_gain]
        in_specs.append(pl.BlockSpec((1, k0), lambda i: (0, 0)))
        out_specs += [pl.BlockSpec((tm, k0), lambda i: (i, 0)), pl.BlockSpec((tm, 1), lambda i: (i, 0))]
        out_shape += [S((m, k0), BF16), S((m, 1), F32)]
    out_specs += [pl.BlockSpec(s, lambda i, nd=len(s): (0,) * nd) for s in sums]
    out_shape += [S(s, F32) for s in sums]
    return pl.pallas_call(body, grid=(m // tm,), in_specs=in_specs, out_specs=out_specs, out_shape=out_shape,
                          compiler_params=_cp("arbitrary" if sums else "parallel"),
                          name=name)(*[o[1] for o in a_ops], *b_arrs, *tiled, *cols, *rowv, *gain)


def mm_tn(name, a, b, out_dtype=BF16):
    if isinstance(a, tuple):
        a_arr, a_cb, m = a
    else:
        a_arr, a_cb, m = a, None, a.shape[1]
    if isinstance(b, tuple):
        b_arr, b_cb, n = b
    else:
        b_arr, b_cb, n = b, None, b.shape[1]
    t = a_arr.shape[0]
    whole_b = t * n * b_arr.dtype.itemsize <= MM_TN_RESIDENT and b_cb is None
    tn = n if whole_b else _tile(n, 512)
    tm = _tile(m, 512 if t * 512 * a_arr.dtype.itemsize * 2 + t * tn * b_arr.dtype.itemsize * 2 <= VMEM_BUDGET else 256)
    a_off = 0 if a_cb is None else a_cb * (m // tm)
    b_off = 0 if b_cb is None else b_cb * (n // tn)

    def body(a_ref, b_ref, o_ref):
        o_ref[...] = lax.dot_general(_bf(a_ref[...]), _bf(b_ref[...]), (((0,), (0,)), ((), ())),
                                     preferred_element_type=F32).astype(o_ref.dtype)

    if whole_b:
        b_spec = pl.BlockSpec((t, n), lambda i, j: (0, 0), pipeline_mode=pl.Buffered(1))
    else:
        b_spec = pl.BlockSpec((t, tn), lambda i, j: (0, j + b_off))
    return pl.pallas_call(
        body, grid=(m // tm, n // tn),
        in_specs=[pl.BlockSpec((t, tm), lambda i, j: (0, i + a_off)), b_spec],
        out_specs=pl.BlockSpec((tm, tn), lambda i, j: (i, j)), out_shape=S((m, n), out_dtype),
        compiler_params=_cp("parallel", "parallel"), name=name)(a_arr, b_arr)


def rms_bwd_gain_only(name, dxn, x, r):
    def fn(dv, xv, rv):
        return [], [jnp.sum(dv * xv * rv, axis=0, keepdims=True)]
    return rows_call(name, fn, [dxn, x, r], [], [], [(1, x.shape[1])])[0]


def _final_loss_epi(accs, res, tv, g):
    xv = res + accs[0]
    d = xv.shape[-1]
    r = lax.rsqrt(jnp.mean(xv * xv, axis=-1, keepdims=True) + EPS)
    xh = xv * r
    err = xh * g - tv
    dy = err * (1.0 / d)
    w = dy * g
    dx = r * (w - xh * jnp.mean(w * xh, axis=-1, keepdims=True))
    part = jnp.sum(jnp.sum(err * err, axis=-1, keepdims=True), axis=0, keepdims=True) * (0.5 / d)
    return [dx, dx, jnp.sum(dy * xh, axis=0, keepdims=True), part]


def _gmlp_mask():
    row = lax.broadcasted_iota(jnp.int32, (GMLP_BLOCK, GMLP_BLOCK), 0) // CHUNK
    col = lax.broadcasted_iota(jnp.int32, (GMLP_BLOCK, GMLP_BLOCK), 1) // CHUNK
    return col <= row


def _ln_plain(v):
    mu = jnp.mean(v, axis=-1, keepdims=True)
    vc = v - mu
    rstd = lax.rsqrt(jnp.mean(vc * vc, axis=-1, keepdims=True) + EPS)
    return vc * rstd, rstd


def gmlp_fwd(name, proj, w, b, tm=512):
    t = proj.shape[0]
    tm = _tile(t, tm)

    def body(au_ref, av_ref, w_ref, b_ref, o_ref):
        mask = _gmlp_mask()
        u = _gelu(au_ref[...])
        vn, _ = _ln_plain(_gelu(av_ref[...]))
        vnb = _bf(vn)
        for g in range(A_GROUPS):
            wg = _bf(jnp.where(mask, w_ref[g], 0.0))
            cs = slice(g * GMLP_BLOCK, (g + 1) * GMLP_BLOCK)
            for n in range(tm // GMLP_BLOCK):
                rs = slice(n * GMLP_BLOCK, (n + 1) * GMLP_BLOCK)
                sg = jnp.dot(wg, vnb[rs, cs], preferred_element_type=F32) + b_ref[g]
                o_ref[rs, cs] = (u[rs, cs] * sg).astype(o_ref.dtype)

    return pl.pallas_call(
        body, grid=(t // tm,),
        in_specs=[pl.BlockSpec((tm, A_WIDTH), lambda i: (i, 0)), pl.BlockSpec((tm, A_WIDTH), lambda i: (i, 1)),
                  pl.BlockSpec(w.shape, lambda i: (0, 0, 0)), pl.BlockSpec(b.shape, lambda i: (0, 0, 0))],
        out_specs=pl.BlockSpec((tm, A_WIDTH), lambda i: (i, 0)), out_shape=S((t, A_WIDTH), BF16),
        compiler_params=_cp("parallel"), name=name)(proj, proj, w, b)


def gmlp_bwd(name, proj, dcat, w, b, tm=512):
    t = proj.shape[0]
    tm = _tile(t, tm)

    def body(au_ref, av_ref, do_ref, w_ref, b_ref, dp_ref, dw_ref, db_ref):
        @pl.when(pl.program_id(0) == 0)
        def _():
            dw_ref[...] = jnp.zeros(dw_ref.shape, F32)
            db_ref[...] = jnp.zeros(db_ref.shape, F32)

        mask = _gmlp_mask()
        au = au_ref[...]
        av = av_ref[...]
        u = _gelu(au)
        vn, rstd = _ln_plain(_gelu(av))
        vnb = _bf(vn)
        dout = do_ref[...]
        dvn_cols = []
        for g in range(A_GROUPS):
            wm = jnp.where(mask, w_ref[g], 0.0)
            wg = _bf(wm)
            wgt = _bf(wm.T)
            cs = slice(g * GMLP_BLOCK, (g + 1) * GMLP_BLOCK)
            dwg = jnp.zeros((GMLP_BLOCK, GMLP_BLOCK), F32)
            dbg = jnp.zeros((GMLP_BLOCK, 1), F32)
            dvn_rows = []
            for n in range(tm // GMLP_BLOCK):
                rs = slice(n * GMLP_BLOCK, (n + 1) * GMLP_BLOCK)
                sg = jnp.dot(wg, vnb[rs, cs], preferred_element_type=F32) + b_ref[g]
                dp_ref[rs, cs] = (dout[rs, cs] * sg * _gelu_grad(au[rs, cs])).astype(dp_ref.dtype)
                dsg = dout[rs, cs] * u[rs, cs]
                dsgb = _bf(dsg)
                dbg = dbg + jnp.sum(dsg, axis=1, keepdims=True)
                dwg = dwg + lax.dot_general(dsgb, vnb[rs, cs], (((1,), (1,)), ((), ())), preferred_element_type=F32)
                dvn_rows.append(jnp.dot(wgt, dsgb, preferred_element_type=F32))
            dw_ref[g] += jnp.where(mask, dwg, 0.0)
            db_ref[g] += dbg
            dvn_cols.append(jnp.concatenate(dvn_rows, axis=0))
        dvn = jnp.concatenate(dvn_cols, axis=1)
        dv = rstd * (dvn - jnp.mean(dvn, axis=-1, keepdims=True) - vn * jnp.mean(dvn * vn, axis=-1, keepdims=True))
        dp_ref[:, A_WIDTH:] = (dv * _gelu_grad(av)).astype(dp_ref.dtype)

    return pl.pallas_call(
        body, grid=(t // tm,),
        in_specs=[pl.BlockSpec((tm, A_WIDTH), lambda i: (i, 0)), pl.BlockSpec((tm, A_WIDTH), lambda i: (i, 1)),
                  pl.BlockSpec((tm, A_WIDTH), lambda i: (i, 0)),
                  pl.BlockSpec(w.shape, lambda i: (0, 0, 0)), pl.BlockSpec(b.shape, lambda i: (0, 0, 0))],
        out_specs=[pl.BlockSpec((tm, 2 * A_WIDTH), lambda i: (i, 0)),
                   pl.BlockSpec(w.shape, lambda i: (0, 0, 0)), pl.BlockSpec(b.shape, lambda i: (0, 0, 0))],
        out_shape=[S((t, 2 * A_WIDTH), BF16), S(w.shape, F32), S(b.shape, F32)],
        compiler_params=_cp("arbitrary"), name=name)(proj, proj, dcat, w, b)


CONV_ROWS = 256


def conv_fwd(name, proj, w, cb):
    t = proj.shape[0]
    tc = LANES
    rows = _tile(t, CONV_ROWS)
    a_cb, g_cb = 2 * A_WIDTH // tc, (2 * A_WIDTH + B_WIDTH) // tc

    def body(a_ref, g_ref, w_ref, cb_ref, o_ref, hpad):
        hpad[0:CONV_PAD, :] = jnp.zeros((CONV_PAD, tc), F32)

        def fill(i, _):
            r0 = pl.multiple_of(i * rows, rows)
            hpad[pl.ds(CONV_PAD + r0, rows), :] = a_ref[pl.ds(r0, rows), :] * _sigmoid(g_ref[pl.ds(r0, rows), :])
            return 0
        lax.fori_loop(0, t // rows, fill, 0)

        def conv(i, _):
            r0 = pl.multiple_of(i * rows, rows)
            win = hpad[pl.ds(r0, rows + CONV_PAD), :]
            acc = jnp.zeros((rows, tc), F32) + cb_ref[...]
            for b in range(SUB):
                wb = win if b == 0 else pltpu.roll(win, b, 0)
                for a in range(CONV_PAD // SUB):
                    k = CONV_WIDTH - 1 - (SUB * a + b)
                    if k >= 0:
                        lo = CONV_PAD - SUB * a
                        acc = acc + wb[lo:lo + rows, :] * w_ref[k:k + 1, :]
            o_ref[pl.ds(r0, rows), :] = acc
            return 0
        lax.fori_loop(0, t // rows, conv, 0)

    return pl.pallas_call(
        body, grid=(B_WIDTH // tc,),
        in_specs=[pl.BlockSpec((t, tc), lambda j: (0, a_cb + j)), pl.BlockSpec((t, tc), lambda j: (0, g_cb + j)),
                  pl.BlockSpec((CONV_WIDTH, tc), lambda j: (0, j)), pl.BlockSpec((1, tc), lambda j: (0, j))],
        out_specs=pl.BlockSpec((t, tc), lambda j: (0, j)), out_shape=S((t, B_WIDTH), F32),
        scratch_shapes=[pltpu.VMEM((t + CONV_PAD, tc), F32)],
        compiler_params=_cp("parallel"), name=name)(proj, proj, w, cb)


def conv_bwd(name, proj, dhc, w):
    t = proj.shape[0]
    tc = LANES
    rows = _tile(t, CONV_ROWS)
    a_cb, g_cb = 2 * A_WIDTH // tc, (2 * A_WIDTH + B_WIDTH) // tc
    win_rows = rows + CONV_PAD

    def body(a_ref, g_ref, d_ref, w_ref, da_ref, dg_ref, dw_ref, dcb_ref, hpad, dpad, dwacc):
        hpad[0:CONV_PAD, :] = jnp.zeros((CONV_PAD, tc), F32)
        dpad[t:t + CONV_PAD, :] = jnp.zeros((CONV_PAD, tc), F32)
        dwacc[...] = jnp.zeros(dwacc.shape, F32)

        def fill(i, _):
            r0 = pl.multiple_of(i * rows, rows)
            hpad[pl.ds(CONV_PAD + r0, rows), :] = a_ref[pl.ds(r0, rows), :] * _sigmoid(g_ref[pl.ds(r0, rows), :])
            dpad[pl.ds(r0, rows), :] = d_ref[pl.ds(r0, rows), :]
            return 0
        lax.fori_loop(0, t // rows, fill, 0)

        def step(i, dcb):
            r0 = pl.multiple_of(i * rows, rows)
            hwin = hpad[pl.ds(r0, win_rows), :]
            dwin = dpad[pl.ds(r0, win_rows), :]
            dchunk = dwin[:rows, :]
            dh = jnp.zeros((rows, tc), F32)
            for b in range(SUB):
                hb = hwin if b == 0 else pltpu.roll(hwin, b, 0)
                db = dwin if b == 0 else pltpu.roll(dwin, win_rows - b, 0)
                for a in range(CONV_PAD // SUB):
                    k = CONV_WIDTH - 1 - (SUB * a + b)
                    if k >= 0:
                        dh = dh + db[SUB * a:SUB * a + rows, :] * w_ref[k:k + 1, :]
                        lo = CONV_PAD - SUB * a
                        prod = dchunk * hb[lo:lo + rows, :]
                        dwacc[k] += jnp.sum(prod.reshape(rows // 8, 8, tc), axis=0)
            a = a_ref[pl.ds(r0, rows), :]
            sg = _sigmoid(g_ref[pl.ds(r0, rows), :])
            da_ref[pl.ds(r0, rows), :] = (dh * sg).astype(da_ref.dtype)
            dg_ref[pl.ds(r0, rows), :] = (dh * a * sg * (1.0 - sg)).astype(dg_ref.dtype)
            return dcb + jnp.sum(dchunk, axis=0, keepdims=True)
        dcb = lax.fori_loop(0, t // rows, step, jnp.zeros((1, tc), F32))
        dcb_ref[...] = dcb
        for k in range(CONV_WIDTH):
            dw_ref[k:k + 1, :] = jnp.sum(dwacc[k], axis=0, keepdims=True)

    return pl.pallas_call(
        body, grid=(B_WIDTH // tc,),
        in_specs=[pl.BlockSpec((t, tc), lambda j: (0, a_cb + j)), pl.BlockSpec((t, tc), lambda j: (0, g_cb + j)),
                  pl.BlockSpec((t, tc), lambda j: (0, j)), pl.BlockSpec((CONV_WIDTH, tc), lambda j: (0, j))],
        out_specs=[pl.BlockSpec((t, tc), lambda j: (0, j)), pl.BlockSpec((t, tc), lambda j: (0, j)),
                   pl.BlockSpec((CONV_WIDTH, tc), lambda j: (0, j)), pl.BlockSpec((1, tc), lambda j: (0, j))],
        out_shape=[S((t, B_WIDTH), BF16), S((t, B_WIDTH), BF16), S((CONV_WIDTH, B_WIDTH), F32), S((1, B_WIDTH), F32)],
        scratch_shapes=[pltpu.VMEM((t + CONV_PAD, tc), F32), pltpu.VMEM((t + CONV_PAD, tc), F32),
                        pltpu.VMEM((CONV_WIDTH, 8, tc), F32)],
        compiler_params=_cp("parallel"), name=name)(proj, proj, dhc, w)


def ln_silu_fwd(name, hc, g, b):
    def fn(h, gv, bv):
        y, _ = _ln_plain(h)
        z = y * gv + bv
        return [z * _sigmoid(z)], []
    return rows_call(name, fn, [hc], [g, b], [(hc.shape[1], BF16)], [])[0]


def ln_silu_bwd(name, hc, dcat, g, b):
    c = hc.shape[1]

    def fn(h, dout, gv, bv):
        y, rstd = _ln_plain(h)
        z = y * gv + bv
        s = _sigmoid(z)
        dz = dout * s * (1.0 + z * (1.0 - s))
        dyv = dz * gv
        dh = rstd * (dyv - jnp.mean(dyv, axis=-1, keepdims=True) - y * jnp.mean(dyv * y, axis=-1, keepdims=True))
        return [dh], [jnp.sum(dz * y, axis=0, keepdims=True), jnp.sum(dz, axis=0, keepdims=True)]

    return rows_call(name, fn, [hc, (dcat, 1, c)], [g, b], [(c, F32)], [(1, c), (1, c)])


_NT = (((1,), (1,)), ((), ()))
_TN = (((0,), (0,)), ((), ()))


def attn_fwd(name, x, gain, wq, k, v, wo, tm=512):
    t, d = x.shape
    m = k.shape[0]
    tm = _tile(t, tm)
    scale = CA_HEAD_DIM ** -0.5

    def body(x_ref, g_ref, wq_ref, k_ref, v_ref, wo_ref, x1_ref, xn_ref, r_ref, q_ref, o_ref):
        xv = x_ref[...]
        rv = lax.rsqrt(jnp.mean(xv * xv, axis=-1, keepdims=True) + EPS)
        xn = (xv * rv * g_ref[...]).astype(BF16)
        xn_ref[...] = xn
        r_ref[...] = rv
        q_ref[...] = jnp.dot(xn, wq_ref[...], preferred_element_type=F32).astype(BF16)
        for h in range(CA_HEADS):
            cs = slice(h * CA_HEAD_DIM, (h + 1) * CA_HEAD_DIM)
            s = lax.dot_general(q_ref[:, cs], k_ref[:, cs], _NT, preferred_element_type=F32) * scale
            e = jnp.exp(s - jnp.max(s, axis=-1, keepdims=True))
            p = e / jnp.sum(e, axis=-1, keepdims=True)
            o_ref[:, cs] = jnp.dot(_bf(p), v_ref[:, cs], preferred_element_type=F32).astype(o_ref.dtype)
        x1_ref[...] = xv + jnp.dot(o_ref[...], wo_ref[...], preferred_element_type=F32)

    def whole(a):
        return pl.BlockSpec(a.shape, lambda i: (0, 0), pipeline_mode=pl.Buffered(1))

    rows = pl.BlockSpec((tm, d), lambda i: (i, 0))
    col = pl.BlockSpec((tm, 1), lambda i: (i, 0))
    return pl.pallas_call(
        body, grid=(t // tm,),
        in_specs=[rows, whole(gain), whole(wq), whole(k), whole(v), whole(wo)],
        out_specs=[rows, rows, col, rows, rows],
        out_shape=[S((t, d), F32), S((t, d), BF16), S((t, 1), F32), S((t, d), BF16), S((t, d), BF16)],
        compiler_params=_cp("parallel"), name=name)(x, gain, wq, k, v, wo)


def attn_bwd(name, dx, dxb, x, r, gain, q, k, v, wq, wo, tm=512):
    t, d = q.shape
    m = k.shape[0]
    tm = _tile(t, tm)
    scale = CA_HEAD_DIM ** -0.5

    def body(dx_ref, dxb_ref, x_ref, r_ref, g_ref, q_ref, k_ref, v_ref, wq_ref, wo_ref,
             dxo_ref, dxbo_ref, dq_ref, dk_ref, dv_ref, dg_ref, do_s):
        @pl.when(pl.program_id(0) == 0)
        def _():
            dk_ref[...] = jnp.zeros(dk_ref.shape, F32)
            dv_ref[...] = jnp.zeros(dv_ref.shape, F32)
            dg_ref[...] = jnp.zeros(dg_ref.shape, F32)

        do_s[...] = lax.dot_general(dxb_ref[...], wo_ref[...], _NT, preferred_element_type=F32).astype(BF16)
        for h in range(CA_HEADS):
            cs = slice(h * CA_HEAD_DIM, (h + 1) * CA_HEAD_DIM)
            qh, kh, vh, doh = q_ref[:, cs], k_ref[:, cs], v_ref[:, cs], do_s[:, cs]
            s = lax.dot_general(qh, kh, _NT, preferred_element_type=F32) * scale
            e = jnp.exp(s - jnp.max(s, axis=-1, keepdims=True))
            p = e / jnp.sum(e, axis=-1, keepdims=True)
            pb = _bf(p)
            dv_ref[:, cs] += lax.dot_general(pb, doh, _TN, preferred_element_type=F32)
            dp = lax.dot_general(doh, vh, _NT, preferred_element_type=F32)
            ds = _bf(p * (dp - jnp.sum(dp * p, axis=-1, keepdims=True)) * scale)
            dq_ref[:, cs] = jnp.dot(ds, kh, preferred_element_type=F32).astype(dq_ref.dtype)
            dk_ref[:, cs] += lax.dot_general(ds, qh, _TN, preferred_element_type=F32)
        dxn = lax.dot_general(dq_ref[...], wq_ref[...], _NT, preferred_element_type=F32)
        xh = x_ref[...] * r_ref[...]
        wv = dxn * g_ref[...]
        dxo = dx_ref[...] + r_ref[...] * (wv - xh * jnp.mean(wv * xh, axis=-1, keepdims=True))
        dxo_ref[...] = dxo
        dxbo_ref[...] = dxo.astype(BF16)
        dg_ref[...] += jnp.sum(dxn * xh, axis=0, keepdims=True)

    def whole(a):
        return pl.BlockSpec(a.shape, lambda i: (0, 0), pipeline_mode=pl.Buffered(1))

    rows = pl.BlockSpec((tm, d), lambda i: (i, 0))
    col = pl.BlockSpec((tm, 1), lambda i: (i, 0))
    acc = pl.BlockSpec((m, d), lambda i: (0, 0))
    return pl.pallas_call(
        body, grid=(t // tm,),
        in_specs=[rows, rows, rows, col, whole(gain), rows, whole(k), whole(v), whole(wq), whole(wo)],
        out_specs=[rows, rows, rows, acc, acc, pl.BlockSpec((1, d), lambda i: (0, 0))],
        out_shape=[S((t, d), F32), S((t, d), BF16), S((t, d), BF16), S((m, d), F32), S((m, d), F32), S((1, d), F32)],
        scratch_shapes=[pltpu.VMEM((tm, d), BF16)],
        compiler_params=_cp("arbitrary"), name=name)(dx, dxb, x, r, gain, q, k, v, wq, wo)


SUB = 8
S5_ROWS = 256


S5_BLOCKS = 4
BLOCK_CH = C_WIDTH // S5_BLOCKS
BLOCK_ST = N_STATE // S5_BLOCKS
_S5_BLOCKS = tuple((slice(BLOCK_CH * q, BLOCK_CH * (q + 1)), slice(BLOCK_ST * q, BLOCK_ST * (q + 1)),
                    slice(N_STATE + BLOCK_ST * q, N_STATE + BLOCK_ST * (q + 1))) for q in range(S5_BLOCKS))
_HI = lax.Precision.HIGHEST
_GP = (C_GROUPS, C_STATE)
_RP = (C_WIDTH, C_STATE)


def _zoh(lr, li, ldt):
    dt = jnp.exp(ldt)
    mag = jnp.exp(lr * dt)
    ar = mag * jnp.cos(li * dt)
    ai = mag * jnp.sin(li * dt)
    den = lr * lr + li * li
    qr = ((ar - 1.0) * lr + ai * li) / den
    qi = (ai * lr - (ar - 1.0) * li) / den
    return dt, ar, ai, den, qr, qi


def _per_channel(v):
    return jnp.broadcast_to(v[:, None, :], (C_GROUPS, C_GROUP_CH, C_STATE)).reshape(_RP)


def _same_group(shape, row_per_group, col_per_group):
    rows = lax.broadcasted_iota(jnp.int32, shape, 0) // row_per_group
    cols = lax.broadcasted_iota(jnp.int32, shape, 1) // col_per_group
    return rows == cols


def _spread(shape, axis):
    long = lax.broadcasted_iota(jnp.int32, shape, axis) % C_STATE
    short = lax.broadcasted_iota(jnp.int32, shape, 1 - axis)
    return long == short


def s5_discretise(name, lam_re, lam_im, log_dt, bt_re, bt_im):
    def body(lr_ref, li_ref, ldt_ref, btr_ref, bti_ref, a_ref, bbr_ref, bbi_ref):
        _, ar, ai, _, qr, qi = _zoh(lr_ref[...], li_ref[...], ldt_ref[...])
        a_ref[0] = ar
        a_ref[1] = ai
        q2r, q2i = _per_channel(qr), _per_channel(qi)
        btr, bti = btr_ref[...], bti_ref[...]
        bbr_ref[...] = q2r * btr - q2i * bti
        bbi_ref[...] = q2r * bti + q2i * btr

    return pl.pallas_call(body, out_shape=[S((2,) + _GP, F32), S(_RP, F32), S(_RP, F32)],
                          name=name)(lam_re, lam_im, log_dt, bt_re, bt_im)


def s5_operands(name, a, bbr, bbi, c2r, c2i, ctr, cti):
    ns = N_STATE

    def body(a_ref, bbr_ref, bbi_ref, c2r_ref, c2i_ref, ctr_ref, cti_ref, pw_ref, qw_ref, mb_ref, mc_ref, mct_ref):
        ar, ai = a_ref[0:1, :], a_ref[1:2, :]
        pows = [(ar, ai)]
        for _ in range(SUB - 1):
            pr, pi = pows[-1]
            pows.append((pr * ar - pi * ai, pr * ai + pi * ar))
        rows = lax.broadcasted_iota(jnp.int32, (SUB, ns), 0)

        def rows_of(v):
            return jnp.broadcast_to(v, (SUB, ns))

        for k, s in enumerate((1, 2, 4)):
            pr, pi = rows_of(pows[s - 1][0]), rows_of(pows[s - 1][1])
            pw_ref[k, 0] = jnp.where(rows >= s, pr, 0.0)
            pw_ref[k, 1] = jnp.where(rows >= s, pi, 0.0)
            qw_ref[k, 0] = jnp.where(rows + s <= SUB - 1, pr, 0.0)
            qw_ref[k, 1] = jnp.where(rows + s <= SUB - 1, -pi, 0.0)
        fr = fi = br = bi = jnp.zeros((SUB, ns), F32)
        for i in range(SUB):
            fr = jnp.where(rows == i, rows_of(pows[i][0]), fr)
            fi = jnp.where(rows == i, rows_of(pows[i][1]), fi)
            br = jnp.where(rows == i, rows_of(pows[SUB - 1 - i][0]), br)
            bi = jnp.where(rows == i, rows_of(-pows[SUB - 1 - i][1]), bi)
        pw_ref[3, 0], pw_ref[3, 1], qw_ref[3, 0], qw_ref[3, 1] = fr, fi, br, bi

        wide = _spread((C_STATE, ns), 1).astype(BF16)
        tall = _spread((ns, C_STATE), 0).astype(BF16)
        in_rows = _same_group((C_WIDTH, ns), C_GROUP_CH, C_STATE)
        in_cols = _same_group((ns, C_WIDTH), C_STATE, C_GROUP_CH)

        def across(v, sign=1.0):
            return jnp.where(in_rows, sign * jnp.dot(_bf(v), wide, preferred_element_type=F32), 0.0).astype(BF16)

        def down(vt, sign=1.0):
            return jnp.where(in_cols, sign * jnp.dot(tall, _bf(vt), preferred_element_type=F32), 0.0).astype(BF16)

        mb_ref[:, 0:ns] = across(bbr_ref[...])
        mb_ref[:, ns:2 * ns] = across(bbi_ref[...])
        mct_ref[:, 0:ns] = across(c2r_ref[...])
        mct_ref[:, ns:2 * ns] = across(c2i_ref[...], -1.0)
        mc_ref[0:ns, :] = down(ctr_ref[...])
        mc_ref[ns:2 * ns, :] = down(cti_ref[...], -1.0)

    return pl.pallas_call(
        body, out_shape=[S((4, 2, SUB, ns), F32), S((4, 2, SUB, ns), F32), S((C_WIDTH, 2 * ns), BF16),
                         S((2 * ns, C_WIDTH), BF16), S((C_WIDTH, 2 * ns), BF16)],
        compiler_params=pltpu.CompilerParams(vmem_limit_bytes=VMEM_LIMIT), name=name)(a, bbr, bbi, c2r, c2i, ctr, cti)


def s5_block_grads(name, u, lamb, xsb, dyb):
    t = u.shape[0]

    def mb_body(u_ref, lr_ref, li_ref, o_ref):
        ub = _bf(u_ref[...])
        o_ref[:, 0:BLOCK_ST] = lax.dot_general(ub, lr_ref[...], _TN, preferred_element_type=F32)
        o_ref[:, BLOCK_ST:2 * BLOCK_ST] = lax.dot_general(ub, li_ref[...], _TN, preferred_element_type=F32)

    d_mb = pl.pallas_call(
        mb_body, grid=(S5_BLOCKS,),
        in_specs=[pl.BlockSpec((t, BLOCK_CH), lambda q: (0, q)), pl.BlockSpec((t, BLOCK_ST), lambda q: (0, q)),
                  pl.BlockSpec((t, BLOCK_ST), lambda q: (0, S5_BLOCKS + q))],
        out_specs=pl.BlockSpec((BLOCK_CH, 2 * BLOCK_ST), lambda q: (q, 0)), out_shape=S((C_WIDTH, 2 * BLOCK_ST), F32),
        compiler_params=_cp("parallel"), name=name + "_b")(u, lamb, lamb)

    def mc_body(x_ref, dy_ref, o_ref):
        o_ref[...] = lax.dot_general(x_ref[...], dy_ref[...], _TN, preferred_element_type=F32)

    d_mc = pl.pallas_call(
        mc_body, grid=(2, S5_BLOCKS),
        in_specs=[pl.BlockSpec((t, BLOCK_ST), lambda p, q: (0, p * S5_BLOCKS + q)), pl.BlockSpec((t, BLOCK_CH), lambda p, q: (0, q))],
        out_specs=pl.BlockSpec((BLOCK_ST, BLOCK_CH), lambda p, q: (p * S5_BLOCKS + q, 0)),
        out_shape=S((2 * N_STATE, BLOCK_CH), F32), compiler_params=_cp("parallel", "parallel"), name=name + "_c")(xsb, dyb)
    return d_mb, d_mc


def s5_param_grads(name, d_mb, d_mc, da, lam_re, lam_im, log_dt, bt_re, bt_im):
    ns = N_STATE

    def body(dmb_ref, dmc_ref, da_ref, lr_ref, li_ref, ldt_ref, btr_ref, bti_ref,
             glr_ref, gli_ref, gdt_ref, gbr_ref, gbi_ref, gcr_ref, gci_ref):
        lr, li = lr_ref[...], li_ref[...]
        dt, ar, ai, den, qr, qi = _zoh(lr, li, ldt_ref[...])
        per_block = C_GROUPS // S5_BLOCKS
        wide = _spread((C_STATE, BLOCK_ST), 1).astype(F32)
        tall = _spread((BLOCK_ST, C_STATE), 0).astype(F32)
        rows = lax.broadcasted_iota(jnp.int32, (C_WIDTH, BLOCK_ST), 0) // C_GROUP_CH % per_block
        in_rows = rows == lax.broadcasted_iota(jnp.int32, (C_WIDTH, BLOCK_ST), 1) // C_STATE
        in_cols = _same_group((BLOCK_ST, BLOCK_CH), C_STATE, C_GROUP_CH)

        def fold_rows(v):
            return lax.dot_general(jnp.where(in_rows, v, 0.0), wide, (((1,), (1,)), ((), ())), precision=_HI,
                                   preferred_element_type=F32)

        def fold_cols(v):
            return lax.dot_general(jnp.where(in_cols, v, 0.0), tall, (((0,), (0,)), ((), ())), precision=_HI,
                                   preferred_element_type=F32)

        for cs, s_re, s_im in _S5_BLOCKS:
            gcr_ref[cs, :] = fold_cols(dmc_ref[s_re, :])
            gci_ref[cs, :] = -fold_cols(dmc_ref[s_im, :])
        gbbr = fold_rows(dmb_ref[:, 0:BLOCK_ST])
        gbbi = fold_rows(dmb_ref[:, BLOCK_ST:2 * BLOCK_ST])
        btr, bti = btr_ref[...], bti_ref[...]
        q2r, q2i = _per_channel(qr), _per_channel(qi)
        gbr_ref[...] = q2r * gbbr + q2i * gbbi
        gbi_ref[...] = q2r * gbbi - q2i * gbbr

        def per_group(v):
            return jnp.sum(v.reshape(C_GROUPS, C_GROUP_CH, C_STATE), axis=1)

        gqr = per_group(btr * gbbr + bti * gbbi)
        gqi = per_group(btr * gbbi - bti * gbbr)
        ilr, ili = lr / den, li / den
        gar = da_ref[0] + ilr * gqr - ili * gqi
        gai = da_ref[1] + ilr * gqi + ili * gqr
        sr = (qr * lr + qi * li) / den
        si = (qi * lr - qr * li) / den
        gzr = ar * gar + ai * gai
        gzi = ar * gai - ai * gar
        glr_ref[...] = -sr * gqr - si * gqi + dt * gzr
        gli_ref[...] = -sr * gqi + si * gqr + dt * gzi
        gdt_ref[...] = jnp.sum(lr * gzr + li * gzi, axis=1, keepdims=True) * dt

    return pl.pallas_call(
        body, out_shape=[S(_GP, F32), S(_GP, F32), S((C_GROUPS, 1), F32), S(_RP, F32), S(_RP, F32), S(_RP, F32), S(_RP, F32)],
        compiler_params=pltpu.CompilerParams(vmem_limit_bytes=VMEM_LIMIT), name=name,
    )(d_mb, d_mc, da, lam_re, lam_im, log_dt, bt_re, bt_im)


def _cmul_add(xr, xi, pr, pi, zr, zi):
    return xr + pr * zr - pi * zi, xi + pr * zi + pi * zr


def s5_fwd(name, u, mb, mc, pw, dskip):
    t = u.shape[0]
    tm = _tile(t, S5_ROWS)
    ns = N_STATE

    def body(u_ref, mb_ref, mc_ref, pw_ref, d_ref, gy_ref, y_ref, xs_ref, xb_ref, carry):
        @pl.when(pl.program_id(0) == 0)
        def _():
            carry[...] = jnp.zeros(carry.shape, F32)

        uv = u_ref[...]
        ub = _bf(uv)
        for cs, s_re, s_im in _S5_BLOCKS:
            xs_ref[:, s_re] = jnp.dot(ub[:, cs], mb_ref[cs, s_re], preferred_element_type=F32)
            xs_ref[:, s_im] = jnp.dot(ub[:, cs], mb_ref[cs, s_im], preferred_element_type=F32)

        def group(i, _):
            r0 = pl.multiple_of(i * SUB, SUB)
            xr = xs_ref[pl.ds(r0, SUB), 0:ns]
            xi = xs_ref[pl.ds(r0, SUB), ns:2 * ns]
            for k, s in enumerate((1, 2, 4)):
                xr, xi = _cmul_add(xr, xi, pw_ref[k, 0], pw_ref[k, 1], pltpu.roll(xr, s, 0), pltpu.roll(xi, s, 0))
            xr, xi = _cmul_add(xr, xi, pw_ref[3, 0], pw_ref[3, 1], carry[0], carry[1])
            xs_ref[pl.ds(r0, SUB), 0:ns] = xr
            xs_ref[pl.ds(r0, SUB), ns:2 * ns] = xi
            carry[0] = jnp.broadcast_to(xr[SUB - 1:SUB, :], (SUB, ns))
            carry[1] = jnp.broadcast_to(xi[SUB - 1:SUB, :], (SUB, ns))
            return 0
        lax.fori_loop(0, tm // SUB, group, 0)

        xb_ref[...] = _bf(xs_ref[...])
        for cs, s_re, s_im in _S5_BLOCKS:
            y = (jnp.dot(xb_ref[:, s_re], mc_ref[s_re, cs], preferred_element_type=F32)
                 + jnp.dot(xb_ref[:, s_im], mc_ref[s_im, cs], preferred_element_type=F32) + d_ref[:, cs] * uv[:, cs])
            y_ref[:, cs] = y
            gy_ref[:, cs] = _gelu(y).astype(gy_ref.dtype)

    c = u.shape[1]
    return pl.pallas_call(
        body, grid=(t // tm,),
        in_specs=[pl.BlockSpec((tm, c), lambda i: (i, 0)), pl.BlockSpec(mb.shape, lambda i: (0, 0)),
                  pl.BlockSpec(mc.shape, lambda i: (0, 0)), pl.BlockSpec(pw.shape, lambda i: (0, 0, 0, 0)),
                  pl.BlockSpec((1, c), lambda i: (0, 0))],
        out_specs=[pl.BlockSpec((tm, c), lambda i: (i, 0)), pl.BlockSpec((tm, c), lambda i: (i, 0)),
                   pl.BlockSpec((tm, 2 * ns), lambda i: (i, 0)), pl.BlockSpec((tm, 2 * ns), lambda i: (i, 0))],
        out_shape=[S((t, c), BF16), S((t, c), F32), S((t, 2 * ns), F32), S((t, 2 * ns), BF16)],
        scratch_shapes=[pltpu.VMEM((2, SUB, ns), F32)],
        compiler_params=_cp("arbitrary"), name=name)(u, mb, mc, pw, dskip)


def s5_bwd(name, dgy, y, u, xs, mct, mbt, qw, dskip):
    t, c = u.shape
    tm = _tile(t, S5_ROWS)
    nt = t // tm
    ns = N_STATE
    ng = tm // SUB

    def body(dgy_ref, y_ref, u_ref, xs_ref, mct_ref, mbt_ref, qw_ref, d_ref,
             du_ref, dy_ref, lb_ref, da_ref, dd_ref, lam, carry):
        @pl.when(pl.program_id(0) == 0)
        def _():
            carry[...] = jnp.zeros(carry.shape, F32)
            da_ref[...] = jnp.zeros(da_ref.shape, F32)
            dd_ref[...] = jnp.zeros(dd_ref.shape, F32)

        uv = u_ref[...]
        dy = dgy_ref[...] * _gelu_grad(y_ref[...])
        dyb = _bf(dy)
        dy_ref[...] = dyb
        dd_ref[...] += jnp.sum(dy * uv, axis=0, keepdims=True)
        for cs, s_re, s_im in _S5_BLOCKS:
            lam[:, s_re] = jnp.dot(dyb[:, cs], mct_ref[cs, s_re], preferred_element_type=F32)
            lam[:, s_im] = jnp.dot(dyb[:, cs], mct_ref[cs, s_im], preferred_element_type=F32)
        last_row = lax.broadcasted_iota(jnp.int32, (SUB, ns), 0) == SUB - 1

        def group(j, _):
            i = ng - 1 - j
            r0 = pl.multiple_of(i * SUB, SUB)
            lr = lam[pl.ds(r0, SUB), 0:ns]
            li = lam[pl.ds(r0, SUB), ns:2 * ns]
            for k, s in enumerate((1, 2, 4)):
                lr, li = _cmul_add(lr, li, qw_ref[k, 0], qw_ref[k, 1],
                                   pltpu.roll(lr, SUB - s, 0), pltpu.roll(li, SUB - s, 0))
            cr, ci = carry[0], carry[1]
            lr, li = _cmul_add(lr, li, qw_ref[3, 0], qw_ref[3, 1], cr, ci)
            lam[pl.ds(r0, SUB), 0:ns] = lr
            lam[pl.ds(r0, SUB), ns:2 * ns] = li
            carry[0] = jnp.broadcast_to(lr[0:1, :], (SUB, ns))
            carry[1] = jnp.broadcast_to(li[0:1, :], (SUB, ns))
            nr = jnp.where(last_row, cr, pltpu.roll(lr, SUB - 1, 0))
            ni = jnp.where(last_row, ci, pltpu.roll(li, SUB - 1, 0))
            xr = xs_ref[pl.ds(r0, SUB), 0:ns]
            xi = xs_ref[pl.ds(r0, SUB), ns:2 * ns]
            da_ref[0] += nr * xr + ni * xi
            da_ref[1] += ni * xr - nr * xi
            return 0
        lax.fori_loop(0, ng, group, 0)

        lb_ref[...] = _bf(lam[...])
        for cs, s_re, s_im in _S5_BLOCKS:
            du = (jnp.dot(lb_ref[:, s_re], mbt_ref[s_re, cs], preferred_element_type=F32)
                  + jnp.dot(lb_ref[:, s_im], mbt_ref[s_im, cs], preferred_element_type=F32) + d_ref[:, cs] * dy[:, cs])
            du_ref[:, cs] = du.astype(du_ref.dtype)

    rev = lambda i: (nt - 1 - i, 0)
    return pl.pallas_call(
        body, grid=(nt,),
        in_specs=[pl.BlockSpec((tm, c), rev), pl.BlockSpec((tm, c), rev), pl.BlockSpec((tm, c), rev),
                  pl.BlockSpec((tm, 2 * ns), rev),
                  pl.BlockSpec(mct.shape, lambda i: (0, 0)), pl.BlockSpec(mbt.shape, lambda i: (0, 0)),
                  pl.BlockSpec(qw.shape, lambda i: (0, 0, 0, 0)), pl.BlockSpec((1, c), lambda i: (0, 0))],
        out_specs=[pl.BlockSpec((tm, c), rev), pl.BlockSpec((tm, c), rev), pl.BlockSpec((tm, 2 * ns), rev),
                   pl.BlockSpec((2, SUB, ns), lambda i: (0, 0, 0)), pl.BlockSpec((1, c), lambda i: (0, 0))],
        out_shape=[S((t, c), BF16), S((t, c), BF16), S((t, 2 * ns), BF16), S((2, SUB, ns), F32), S((1, c), F32)],
        scratch_shapes=[pltpu.VMEM((tm, 2 * ns), F32), pltpu.VMEM((2, SUB, ns), F32)],
        compiler_params=_cp("arbitrary"), name=name)(dgy, y, u, xs, mct, mbt, qw, dskip)


def _first(accs, *_):
    return [accs[0]]


def _rms_bwd_epi(accs, xv, base, rv, g):
    dv = accs[0]
    w = dv * g
    xh = xv * rv
    dx = base + rv * (w - xh * jnp.mean(w * xh, axis=-1, keepdims=True))
    return [dx, dx, jnp.sum(dv * xh, axis=0, keepdims=True)]


def mm_rms_bwd(name, pairs, x, r, gain, dres):
    t, d = x.shape
    return mm_nn(name, t, d, pairs, 1, _rms_bwd_epi, [F32, BF16], tiled=[x, dres], cols=[r], rowv=[gain], sums=[(1, d)])


def _add_res(accs, res):
    return [accs[0] + res]


def even_fwd(x, w, need_out):
    t = x.shape[0]
    proj, hn, r = mm_nn("e_in_f", t, IN_WIDTH, [(x, w["e_w_in_t"], 0, "t")], 1, _first, [F32], norm_gain=w["e_norm"])
    out_a = gmlp_fwd("e_gmlp_f", proj, w["e_gmlp_w"], w["e_gmlp_b"])
    hc = conv_fwd("e_conv_f", proj, w["e_conv_w"], w["e_conv_b"])
    out_b = ln_silu_fwd("e_ln_f", hc, w["e_conv_ln_g"], w["e_conv_ln_b"])
    need_out(out_b)
    (x1,) = mm_nn("e_out_f", t, D_MODEL, [(out_a, (w["e_w_out"], 0), 0), (out_b, (w["e_w_out"], 1), 0)],
                  1, _add_res, [F32], tiled=[x])
    return x1, (x, hn, r, proj, out_a, hc, out_b)


def even_bwd_mixers(dxb, saved, w):
    x, hn, r, proj, out_a, hc, out_b = saved
    t = x.shape[0]
    (dcat,) = mm_nn("e_out_b", t, D_MODEL, [(dxb, w["e_w_out"], 0, "t")], 1, _first, [F32])
    g_w_out = jnp.concatenate([mm_tn("e_out_wa", out_a, dxb), mm_tn("e_out_wb", out_b, dxb)], axis=0)
    dab, g_gw, g_gb = gmlp_bwd("e_gmlp_b", proj, dcat, w["e_gmlp_w"], w["e_gmlp_b"])
    dhc, g_lg, g_lb = ln_silu_bwd("e_ln_b", hc, dcat, w["e_conv_ln_g"], w["e_conv_ln_b"])
    dba, dbg, g_cw, g_cb = conv_bwd("e_conv_b", proj, dhc, w["e_conv_w"])
    g_w_in_t = jnp.concatenate([mm_tn("e_in_w0", dab, hn), mm_tn("e_in_w1", dba, hn), mm_tn("e_in_w2", dbg, hn)], axis=0)
    grads = dict(e_w_in_t=g_w_in_t, e_gmlp_w=g_gw[None], e_gmlp_b=g_gb.reshape(1, A_GROUPS, GMLP_BLOCK),
                 e_conv_w=g_cw[None], e_conv_b=g_cb, e_conv_ln_g=g_lg, e_conv_ln_b=g_lb, e_w_out=g_w_out)
    return (dab, dba, dbg), grads


def even_bwd_input(dx, dproj, saved, w):
    x, _, r = saved[:3]
    dab, dba, dbg = dproj
    w_in_t = w["e_w_in_t"]
    return mm_rms_bwd("e_in_b", [(dab, (w_in_t, 0), 0), (dba, (w_in_t, 2), 0), (dbg, (w_in_t, 3), 0)], x, r, w["e_norm"], dx)


def s5_setup(w, anchor=None):
    def rows(v):
        return v.transpose(0, 2, 1).reshape(_RP)

    log_dt = w["o_log_dt"].reshape(C_GROUPS, 1)
    if anchor is not None:
        log_dt = log_dt + anchor
    lam = (w["o_lam_re"], w["o_lam_im"], log_dt, rows(w["o_b_re"]), rows(w["o_b_im"]))
    a, bbr, bbi = s5_discretise("o_s5_zoh", *lam)
    c_re, c_im = w["o_c_re"], w["o_c_im"]
    pw, qw, mb, mc, mct = s5_operands("o_s5_ops", a.reshape(2, N_STATE), bbr, bbi, c_re.reshape(_RP), c_im.reshape(_RP),
                                      c_re.transpose(2, 0, 1).reshape(C_STATE, C_WIDTH),
                                      c_im.transpose(2, 0, 1).reshape(C_STATE, C_WIDTH))
    return dict(lam=lam, pw=pw, qw=qw, mb=mb, mc=mc, mct=mct, mbt=mb.T)


def odd_fwd(x, w, consts):
    t = x.shape[0]
    u, hn, r = mm_nn("o_in_f", t, C_WIDTH, [(x, w["o_w_in"], 0)], 1, _first, [F32], norm_gain=w["o_norm"])
    gy, y, xs, xsb = s5_fwd("o_s5_f", u, consts["mb"], consts["mc"], consts["pw"], w["o_d"])
    w_out_t = w["o_w_out_t"]

    def epi(accs, res):
        return [res + accs[0] * _sigmoid(accs[1]), accs[0], accs[1]]

    x1, o1, o2 = mm_nn("o_out_f", t, D_MODEL, [(gy, (w_out_t, 0), 0, "t"), (gy, (w_out_t, D_MODEL), 1, "t")], 2, epi,
                       [F32, BF16, BF16], tiled=[x])
    return x1, (x, hn, r, u, gy, y, xs, xsb, o1, o2)


def odd_bwd(dx, dxb, saved, w, consts):
    x, hn, r, u, gy, y, xs, xsb, o1, o2 = saved
    t = x.shape[0]

    def gate_bwd(dv, a, b):
        a = a.astype(F32)
        sg = _sigmoid(b.astype(F32))
        return [jnp.concatenate([dv * sg, dv * a * sg * (1.0 - sg)], axis=1)], []

    (do12,) = rows_call("o_gate_b", gate_bwd, [dx, o1, o2], [], [(2 * D_MODEL, BF16)], [])
    (dgy,) = mm_nn("o_out_b", t, C_WIDTH, [(do12, w["o_w_out_t"], 0)], 1, _first, [F32])
    g_w_out_t = mm_tn("o_out_w", do12, gy)
    du, dyb, lamb, da8, g_d = s5_bwd("o_s5_b", dgy, y, u, xs, consts["mct"], consts["mbt"], consts["qw"], w["o_d"])
    d_mb, d_mc = s5_block_grads("o_s5_w", u, lamb, xsb, dyb)
    da = jnp.sum(da8, axis=1).reshape((2,) + _GP)
    g_lr, g_li, g_dt, g_btr, g_bti, g_cr, g_ci = s5_param_grads("o_s5_pg", d_mb, d_mc, da, *consts["lam"])

    def states_first(v):
        return v.reshape(C_GROUPS, C_GROUP_CH, C_STATE).transpose(0, 2, 1)[None]

    g_w_in = mm_tn("o_in_w", hn, du)
    dx0, dx0b, g_norm = mm_rms_bwd("o_in_b", [(du, w["o_w_in"], 0, "t")], x, r, w["o_norm"], dx)
    grads = dict(o_norm=g_norm, o_w_in=g_w_in, o_lam_re=g_lr[None], o_lam_im=g_li[None], o_log_dt=g_dt.reshape(1, C_GROUPS),
                 o_b_re=states_first(g_btr), o_b_im=states_first(g_bti),
                 o_c_re=g_cr.reshape((1, C_GROUPS, C_GROUP_CH, C_STATE)), o_c_im=g_ci.reshape((1, C_GROUPS, C_GROUP_CH, C_STATE)),
                 o_d=g_d, o_w_out_t=g_w_out_t)
    return dx0, dx0b, grads


def ca_fwd(i, x, mem, w):
    t, m = x.shape[0], mem.shape[0]
    k, v, mn, rm = mm_nn(f"ca{i}_kv_f", m, D_MODEL, [(mem, w["ca_wk"][i], 0), (mem, w["ca_wv"][i], 1)], 2,
                         lambda accs: [accs[0], accs[1]], [BF16, BF16], norm_gain=w["ca_mem_norm"][i:i + 1])
    x1, xn, r, q, o = attn_fwd(f"ca{i}_attn_f", x, w["ca_norm"][i:i + 1], w["ca_wq"][i], k, v, w["ca_wo"][i])
    return x1, (x, xn, r, mn, rm, q, k, v, o)


def ca_bwd(i, dx, dxb, saved, mem, w):
    x, xn, r, mn, rm, q, k, v, o = saved
    t, m = x.shape[0], mem.shape[0]
    g_wo = mm_tn(f"ca{i}_o_w", o, dxb)
    dx0, dx0b, dq, dk, dv, g_norm = attn_bwd(f"ca{i}_attn_b", dx, dxb, x, r, w["ca_norm"][i:i + 1], q, k, v,
                                             w["ca_wq"][i], w["ca_wo"][i])
    g_wq = mm_tn(f"ca{i}_q_w", xn, dq)
    g_wk = mm_tn(f"ca{i}_k_w", mn, dk)
    g_wv = mm_tn(f"ca{i}_v_w", mn, dv)
    (dmn,) = mm_nn(f"ca{i}_kv_b", m, D_MODEL, [(dk, w["ca_wk"][i], 0, "t"), (dv, w["ca_wv"][i], 0, "t")], 1, _first, [F32])
    g_mnorm = rms_bwd_gain_only(f"ca{i}_mnorm_b", dmn, mem, rm)
    return dx0, dx0b, dict(ca_norm=g_norm, ca_mem_norm=g_mnorm, ca_wq=g_wq, ca_wk=g_wk, ca_wv=g_wv, ca_wo=g_wo)


def ffn_fwd(i, x, w, target=None):
    t = x.shape[0]

    def epi(accs):
        g, u = accs
        s = _sigmoid(g)
        silu = g * s
        return [u * (s + silu * (1.0 - s)), silu, silu * u]

    dgate, dup, h, xn, r = mm_nn(f"ffn{i}_up_f", t, FFN_HIDDEN, [(x, w["ffn_w_gate_t"][i], 0, "t"), (x, w["ffn_w_up_t"][i], 1, "t")],
                           2, epi, [BF16, BF16, BF16], norm_gain=w["ffn_norm"][i:i + 1])
    down = [(h, w["ffn_w_down"][i], 0)]
    if target is None:
        (out,) = mm_nn(f"ffn{i}_down_f", t, D_MODEL, down, 1, _add_res, [F32], tiled=[x])
    else:
        out = mm_nn(f"ffn{i}_down_f", t, D_MODEL, down, 1, _final_loss_epi, [F32, BF16], tiled=[x, target],
                    rowv=[w["final_norm"]], sums=[(1, D_MODEL), (1, 1)])
    return out, (x, xn, r, dgate, dup, h)


def ffn_bwd(i, dx, dxb, saved, w):
    x, xn, r, dgate, dup, h = saved
    t = x.shape[0]

    def epi(accs, dgv, duv):
        dh = accs[0]
        return [dh * dgv.astype(F32), dh * duv.astype(F32)]

    dg, du = mm_nn(f"ffn{i}_down_b", t, FFN_HIDDEN, [(dxb, w["ffn_w_down"][i], 0, "t")], 1, epi, [BF16, BF16],
                   tiled=[dgate, dup])
    g_wd = mm_tn(f"ffn{i}_down_w", h, dxb)
    g_wg_t = mm_tn(f"ffn{i}_gate_w", dg, xn)
    g_wu_t = mm_tn(f"ffn{i}_up_w", du, xn)
    dx0, dx0b, g_norm = mm_rms_bwd(f"ffn{i}_up_b", [(dg, w["ffn_w_gate_t"][i], 0), (du, w["ffn_w_up_t"][i], 0)], x, r,
                                   w["ffn_norm"][i:i + 1], dx)
    return dx0, dx0b, dict(ffn_norm=g_norm, ffn_w_gate_t=g_wg_t, ffn_w_up_t=g_wu_t, ffn_w_down=g_wd)


def local_step(x, mem, target, w, fetch=None, on_grads=None, anchor=None):
    consts = s5_setup(w, anchor)

    def need(stage, after):
        if fetch is not None:
            for k, v in fetch(stage, after).items():
                if isinstance(k, tuple):
                    w.setdefault(k[0], {})[k[1]] = v
                else:
                    w[k] = v

    need(0, consts["pw"])
    x1, s_e = even_fwd(x, w, lambda after: need(1, after))
    x2, s_c0 = ca_fwd(0, x1, mem, w)
    need(2, x2)
    x3, s_f0 = ffn_fwd(0, x2, w)
    x4, s_o = odd_fwd(x3, w, consts)
    need(3, x4)
    x5, s_c1 = ca_fwd(1, x4, mem, w)
    (dx, dxb, g_final, loss), s_f1 = ffn_fwd(1, x5, w, target)

    def emit(stage, carry, plain, layered=None, layer=0):
        if on_grads is None:
            return carry
        out = dict(plain)
        out.update({(k, layer): v for k, v in (layered or {}).items()})
        return on_grads(stage, out, list(carry))

    dx, dxb, g_f1 = ffn_bwd(1, dx, dxb, s_f1, w)
    dx, dxb = emit(0, (dx, dxb), {}, g_f1, 1)
    dx, dxb, g_c1 = ca_bwd(1, dx, dxb, s_c1, mem, w)
    dx, dxb, g_o = odd_bwd(dx, dxb, s_o, w, consts)
    dx, dxb = emit(1, (dx, dxb), g_o, g_c1, 1)
    dx, dxb, g_f0 = ffn_bwd(0, dx, dxb, s_f0, w)
    dx, dxb = emit(2, (dx, dxb), {}, g_f0, 0)
    dx, dxb, g_c0 = ca_bwd(0, dx, dxb, s_c0, mem, w)
    dx, dxb = emit(3, (dx, dxb), {}, g_c0, 0)
    dproj, g_e = even_bwd_mixers(dxb, s_e, w)
    dproj = emit(4, dproj, {**g_e, "o_norm": g_o["o_norm"], "o_d": g_o["o_d"]})
    dx, dxb, g_e["e_norm"] = even_bwd_input(dx, dproj, s_e, w)

    grads = dict(g_e)
    grads.update(g_o)
    for g0, g1 in ((g_c0, g_c1), (g_f0, g_f1)):
        for k in g0:
            grads[k] = jnp.concatenate([g0[k], g1[k]], axis=0) if k.endswith("norm") else (g0[k], g1[k])
    grads["final_norm"] = g_final
    return loss, dx, grads


def _group(axes):
    pos = {a: lax.axis_index(a) for a in ("x", "y", "c")}
    me = 0
    for a in axes:
        me = me * 2 + pos[a]
    peers = []
    for mask in range(1, 2 ** len(axes)):
        peer = dict(pos)
        for bit, a in enumerate(axes):
            if (mask >> (len(axes) - 1 - bit)) & 1:
                peer[a] = 1 - pos[a]
        idx = 0
        for a in axes:
            idx = idx * 2 + peer[a]
        peers.append((idx, (peer["x"], peer["y"], peer["c"])))
    return me, peers


def _sibling():
    x, y, c = lax.axis_index("x"), lax.axis_index("y"), lax.axis_index("c")
    return c, (x, y, 1 - c)


_HBM =pl.BlockSpec(memory_space=pltpu.HBM)
_SEM = pl.BlockSpec(memory_space=pltpu.SEMAPHORE)
_EFFECT = pltpu.SideEffectType.DATAFLOW_SIDE_EFFECTING


def _gather_peers(direct):
    chip, _ = _group(("x", "y"))
    core = lax.axis_index("c")
    if direct:
        _, peers = _group(_ALL)
        return chip, core, [(idx // 2, idx % 2, dev) for idx, dev in peers]
    _, peers = _group(("x", "y"))
    return chip, core, [(idx, core, dev) for idx, dev in peers]


def gather_ici_start(name, groups, direct):
    flat = [b for g in groups for b in g]
    sizes = [len(g) for g in groups]
    k_ops, n_g = len(flat), len(groups)
    lands = [lax.empty((4, 2) + tuple(b.shape), b.dtype) for b in flat]
    fan = [N_DEV - 1 if d else 3 for d in direct]

    def body(*refs):
        src, land = refs[:k_ops], refs[k_ops:2 * k_ops]
        sems = refs[2 * k_ops:2 * k_ops + 3 * n_g]
        token = refs[-1]
        i = 0
        for g in range(n_g):
            send, recv, loc = sems[3 * g:3 * g + 3]
            chip, core, peers = _gather_peers(direct[g])
            for j in range(sizes[g]):
                pltpu.make_async_copy(src[i], land[i].at[chip, core], loc.at[j]).start()
                for k, (_, _, dev) in enumerate(peers):
                    s = fan[g] * j + k
                    pltpu.make_async_remote_copy(src_ref=src[i], dst_ref=land[i].at[chip, core], send_sem=send.at[s],
                                                 recv_sem=recv.at[s], device_id=dev, device_id_type=MESH).start()
                i += 1
        token[...] = jnp.zeros(token.shape, token.dtype)

    sem_shapes = []
    for s, f in zip(sizes, fan):
        sem_shapes += [pltpu.SemaphoreType.DMA((f * s,)), pltpu.SemaphoreType.DMA((f * s,)), pltpu.SemaphoreType.DMA((s,))]
    thru = [pltpu.HBM(a.shape, a.dtype) for a in flat + lands]
    outs = pl.pallas_call(
        body, name=name, out_shape=tuple(sem_shapes) + tuple(thru) + (S((8, LANES), F32),),
        in_specs=[_HBM] * (2 * k_ops), out_specs=[_SEM] * (3 * n_g) + [_HBM] * (2 * k_ops) + [pl.BlockSpec(memory_space=pltpu.VMEM)],
        input_output_aliases={i: 3 * n_g + i for i in range(2 * k_ops)},
        compiler_params=pltpu.CompilerParams(has_side_effects=_EFFECT),
    )(*[pltpu.with_memory_space_constraint(a, pltpu.HBM) for a in flat + lands])
    sems = [tuple(outs[3 * g:3 * g + 3]) for g in range(n_g)]
    srcs_thru, lands_thru, off = [], [], 3 * n_g
    for s in sizes:
        srcs_thru.append(list(outs[off:off + s]))
        off += s
    for s in sizes:
        lands_thru.append(list(outs[off:off + s]))
        off += s
    return sems, srcs_thru, lands_thru, outs[-1]


def gather_ici_wait(name, srcs, lands, sems, after, direct=False):
    n = len(srcs)

    def body(*refs):
        src, land = refs[:n], refs[n:2 * n]
        send, recv, loc = refs[2 * n:2 * n + 3]
        chip, core, peers = _gather_peers(direct)
        for j in range(n):
            for k, (pchip, pcore, dev) in enumerate(peers):
                s = len(peers) * j + k
                cp = pltpu.make_async_remote_copy(src_ref=src[j], dst_ref=land[j].at[pchip, pcore], send_sem=send.at[s],
                                                  recv_sem=recv.at[s], device_id=dev, device_id_type=MESH)
                cp.wait_send()
                cp.wait_recv()
            pltpu.make_async_copy(src[j], land[j].at[chip, core], loc.at[j]).wait()

    outs = pl.pallas_call(
        body, name=name, out_shape=tuple(pltpu.HBM(a.shape, a.dtype) for a in list(srcs) + list(lands)),
        in_specs=[_HBM] * (2 * n) + [_SEM] * 3 + [ANY], out_specs=[_HBM] * (2 * n),
        input_output_aliases={i: i for i in range(2 * n)},
        compiler_params=pltpu.CompilerParams(has_side_effects=_EFFECT),
    )(*srcs, *lands, *sems, after)
    return list(outs[n:])


def gather_d2d(name, bufs):
    k_ops = len(bufs)

    def body(*refs):
        in_refs, out_refs = refs[:k_ops], refs[k_ops:2 * k_ops]
        send_sems, recv_sems = refs[2 * k_ops:]
        core, sib = _sibling()
        sent, landed = [], []
        for i in range(k_ops):
            cp = pltpu.make_async_remote_copy(src_ref=in_refs[i].at[:, core], dst_ref=out_refs[i].at[:, core],
                                              send_sem=send_sems.at[i], recv_sem=recv_sems.at[i], device_id=sib, device_id_type=MESH)
            cp.start()
            sent.append(cp)
            landed.append(pltpu.make_async_remote_copy(src_ref=in_refs[i].at[:, core], dst_ref=out_refs[i].at[:, 1 - core],
                                                       send_sem=send_sems.at[i], recv_sem=recv_sems.at[i],
                                                       device_id=sib, device_id_type=MESH))
        for cp in landed:
            cp.wait_recv()
        for cp in sent:
            cp.wait_send()

    return pl.pallas_call(
        body, in_specs=[ANY] * k_ops, out_specs=[ANY] * k_ops, out_shape=[S(b.shape, b.dtype) for b in bufs],
        input_output_aliases={i: i for i in range(k_ops)},
        scratch_shapes=[pltpu.SemaphoreType.DMA((k_ops,)), pltpu.SemaphoreType.DMA((k_ops,))],
        name=name)(*bufs)


_ALL = ("x", "y", "c")


def scatter_start(name, arr, carry):
    land = lax.empty(arr.shape, arr.dtype)
    n_c = len(carry)

    def body(*refs):
        in_ref, land_ref = refs[0], refs[1]
        send, recv = refs[2 + n_c], refs[3 + n_c]
        me, peers = _group(_ALL)
        for k, (idx, dev) in enumerate(peers):
            pltpu.make_async_remote_copy(src_ref=in_ref.at[idx], dst_ref=land_ref.at[me], send_sem=send.at[k], recv_sem=recv.at[k],
                                         device_id=dev, device_id_type=MESH).start()

    thru = [arr, land] + list(carry)
    outs = pl.pallas_call(
        body, name=name,
        out_shape=(pltpu.SemaphoreType.DMA((N_DEV - 1,)), pltpu.SemaphoreType.DMA((N_DEV - 1,)))
        + tuple(pltpu.HBM(a.shape, a.dtype) for a in thru),
        in_specs=[_HBM] * len(thru), out_specs=[_SEM, _SEM] + [_HBM] * len(thru),
        input_output_aliases={i: 2 + i for i in range(len(thru))},
        compiler_params=pltpu.CompilerParams(has_side_effects=_EFFECT),
    )(*[pltpu.with_memory_space_constraint(a, pltpu.HBM) for a in thru])
    return (outs[0], outs[1]), outs[2], outs[3], list(outs[4:])


def scatter_wait(name, arr, land, sems, after):
    def body(in_ref, land_ref, send, recv, after_ref, in_thru, land_thru):
        _, peers = _group(_ALL)
        for k, (idx, dev) in enumerate(peers):
            cp = pltpu.make_async_remote_copy(src_ref=in_ref.at[idx], dst_ref=land_ref.at[idx], send_sem=send.at[k],
                                              recv_sem=recv.at[k], device_id=dev, device_id_type=MESH)
            cp.wait_send()
            cp.wait_recv()

    outs = pl.pallas_call(
        body, name=name, out_shape=(pltpu.HBM(arr.shape, arr.dtype), pltpu.HBM(arr.shape, arr.dtype)),
        in_specs=[_HBM, _HBM, _SEM, _SEM, ANY], out_specs=[_HBM, _HBM], input_output_aliases={0: 0, 1: 1},
        compiler_params=pltpu.CompilerParams(has_side_effects=_EFFECT),
    )(arr, land, sems[0], sems[1], after)
    return outs[0], outs[1]


def _row_tile(rows, cap=512):
    return next(t for t in range(cap - cap % 16, 0, -16) if rows % t == 0)


def sum_shares(name, own, recv, me):
    n, rows, c = recv.shape
    tr = _row_tile(rows)

    def body(me_ref, *refs):
        acc = refs[0][...].astype(F32)
        for r in refs[1:n]:
            acc = acc + r[...].astype(F32)
        refs[n][...] = acc

    def slot(mask):
        return pl.BlockSpec((None, tr, c), lambda i, me, mask=mask: (jnp.bitwise_xor(me[0], mask), i, 0))

    spec = pltpu.PrefetchScalarGridSpec(
        num_scalar_prefetch=1, grid=(rows // tr,), in_specs=[slot(k) for k in range(n)],
        out_specs=pl.BlockSpec((tr, c), lambda i, me: (i, 0)))
    return pl.pallas_call(body, grid_spec=spec, out_shape=S((rows, c), F32),
                          compiler_params=_cp("parallel"), name=name)(me, own, *([recv] * (n - 1)))


def sum_slots(name, slots):
    n, r, c = slots.shape

    def body(s_ref, o_ref):
        acc = s_ref[0]
        for j in range(1, n):
            acc = acc + s_ref[j]
        o_ref[...] = acc

    return pl.pallas_call(body, out_shape=S((r, c), F32), compiler_params=pltpu.CompilerParams(vmem_limit_bytes=VMEM_LIMIT),
                          name=name)(slots)


def adamw_units(name, pieces, transposed, w, m, v):
    n_l, k, n = w.shape
    tk = _tile(k, 512) if transposed else k
    c1 = 1.0 - ADAM_B1 ** ADAM_STEP
    c2 = 1.0 - ADAM_B2 ** ADAM_STEP

    def body(*refs):
        p_refs, (w_ref, m_ref, v_ref, g_ref, d_ref, m2_ref, v2_ref) = refs[:n_l], refs[n_l:]
        gv = p_refs[0][...]
        for j in range(1, n_l):
            gv = jnp.where(pl.program_id(0) == j, p_refs[j][...], gv)
        if transposed:
            gv = gv.T
        m2 = ADAM_B1 * m_ref[...] + (1.0 - ADAM_B1) * gv
        v2 = ADAM_B2 * v_ref[...] + (1.0 - ADAM_B2) * (gv * gv)
        g_ref[...] = gv
        m2_ref[...] = m2
        v2_ref[...] = v2
        d_ref[...] = -ADAM_LR * ((m2 / c1) / (jnp.sqrt(v2 / c2) + ADAM_EPS) + ADAM_WD * w_ref[...])

    piece = pl.BlockSpec((n, tk), lambda l, i: (0, i)) if transposed else pl.BlockSpec((k, n), lambda l, i: (0, 0))
    blk = pl.BlockSpec((None, tk, n), lambda l, i: (l, i, 0))
    return tuple(pl.pallas_call(body, grid=(n_l, k // tk), in_specs=[piece] * n_l + [blk] * 3, out_specs=[blk] * 4,
                                out_shape=[S(w.shape, F32)] * 4, compiler_params=_cp("parallel", "parallel"),
                                name=name)(*pieces, w, m, v))


def adamw_native(name, g, w, m, v, tr=512):
    shape = w.shape
    cols = shape[-1]
    rows = w.size // cols
    tr = _tile(rows, tr) if rows % 8 == 0 else rows
    c1 = 1.0 - ADAM_B1 ** ADAM_STEP
    c2 = 1.0 - ADAM_B2 ** ADAM_STEP

    def body(g_ref, w_ref, m_ref, v_ref, d_ref, m2_ref, v2_ref):
        gv = g_ref[...]
        m2 = ADAM_B1 * m_ref[...] + (1.0 - ADAM_B1) * gv
        v2 = ADAM_B2 * v_ref[...] + (1.0 - ADAM_B2) * (gv * gv)
        m2_ref[...] = m2
        v2_ref[...] = v2
        d_ref[...] = -ADAM_LR * ((m2 / c1) / (jnp.sqrt(v2 / c2) + ADAM_EPS) + ADAM_WD * w_ref[...])

    row = pl.BlockSpec((tr, cols), lambda i: (i, 0))
    outs = pl.pallas_call(body, grid=(rows // tr,), in_specs=[row] * 4, out_specs=[row] * 3,
                          out_shape=[S((rows, cols), F32)] * 3, compiler_params=_cp("parallel"),
                          name=name)(*[a.reshape(rows, cols) for a in (g, w, m, v)])
    return tuple(o.reshape(shape) for o in outs)


_REPLICATED = ("e_norm", "e_gmlp_w", "e_gmlp_b", "e_conv_b", "e_conv_ln_g", "e_conv_ln_b", "o_lam_re", "o_lam_im", "o_log_dt",
               "o_b_re", "o_b_im", "o_c_re", "o_c_im", "ca_norm", "ca_mem_norm", "ffn_norm", "final_norm")
_ORDER = ("e_norm", "e_w_in", "e_gmlp_w", "e_gmlp_b", "e_conv_w", "e_conv_b", "e_conv_ln_g", "e_conv_ln_b", "e_w_out",
          "o_norm", "o_w_in", "o_lam_re", "o_lam_im", "o_log_dt", "o_b_re", "o_b_im", "o_c_re", "o_c_im", "o_d", "o_w_out",
          "ca_norm", "ca_mem_norm", "ca_wq", "ca_wk", "ca_wv", "ca_wo", "ffn_norm", "ffn_w_gate", "ffn_w_up", "ffn_w_down",
          "final_norm")


def _rows128(a, multiple=8):
    flat = a.reshape(-1)
    rows = -(-flat.shape[0] // (LANES * multiple)) * multiple
    return jnp.pad(flat, (0, rows * LANES - flat.shape[0])).reshape(rows, LANES)


def _shard(full, axis):
    s = full.shape
    return jnp.moveaxis(full.reshape(s[:axis] + (N_DEV, s[axis] // N_DEV) + s[axis + 1:]), axis, 0)


_UNITS = (("e_w_in", 0, True), ("e_w_out", 0, False), ("o_w_in", 0, False), ("o_w_out", 0, True),
          *[(n, i, False) for n in ("ca_wq", "ca_wk", "ca_wv", "ca_wo") for i in (0, 1)],
          *[(n, i, tr) for n, tr in (("ffn_w_gate", True), ("ffn_w_up", True), ("ffn_w_down", False)) for i in (0, 1)])
_LAYERED = ("ca_wq", "ca_wk", "ca_wv", "ca_wo", "ffn_w_gate", "ffn_w_up", "ffn_w_down")
_SMALL_SHARDED = (("e_conv_w", 2), ("o_norm", 1), ("o_d", 1))
RS_ROW = 1024


def _unit_key(name, tr):
    return name + "_t" if tr else name


def _stage_of(name, layer):
    if name.startswith("e_"):
        return 0 if name == "e_w_in" else 1
    if name.startswith("o_"):
        return 2
    if name.startswith("ca_"):
        return 1 if layer == 0 else 3
    return 2 if layer == 0 else 3


def weight_fetcher(local):
    groups, meta = [[] for _ in range(4)], [[] for _ in range(4)]
    for name, layer, tr in _UNITS:
        blk = local[name][layer]
        st = _stage_of(name, layer)
        groups[st].append(_bf(blk.T if tr else blk))
        meta[st].append((name, layer, tr))
    small = jnp.concatenate([local[name].reshape(-1) for name, _ in _SMALL_SHARDED])
    groups[0].append(_rows128(small))
    direct = [False, False, False, True]
    sems, srcs, lands, token = gather_ici_start("ag_w_start", groups, direct)

    def fetch(stage, after):
        bufs = gather_ici_wait(f"ag_w_wait{stage}", srcs[stage], lands[stage], sems[stage], after, direct[stage])
        if not direct[stage]:
            bufs = gather_d2d(f"ag_w_d2d{stage}", bufs)
        got = {}
        for (name, layer, tr), blk, buf in zip(meta[stage], groups[stage], bufs):
            arr = buf.reshape((N_DEV * blk.shape[0],) + tuple(blk.shape[1:]))
            if name in _LAYERED:
                got[(_unit_key(name, tr), layer)] = arr
            else:
                got[_unit_key(name, tr)] = arr
        if stage == 0:
            flat = bufs[-1].reshape(N_DEV, -1)
            off = 0
            for name, axis in _SMALL_SHARDED:
                blk = local[name]
                seg = flat[:, off:off + blk.size].reshape((N_DEV,) + blk.shape)
                off += blk.size
                seg = jnp.moveaxis(seg, 0, axis)
                got[name] = seg.reshape(seg.shape[:axis] + (-1,) + seg.shape[axis + 2:])
            got["e_conv_w"] = got["e_conv_w"][0]
        return got

    return fetch, token


def _grad_stage_of(name, layer):
    if name.startswith("e_"):
        return 4
    if name.startswith("o_"):
        return 1
    if name.startswith("ca_"):
        return 3 if layer == 0 else 1
    return 2 if layer == 0 else 0


GRAD_STAGES = 5
SMALL_ROWS = 16


def gradient_reducer(local, mom, var):
    me = (4 * lax.axis_index("x") + 2 * lax.axis_index("y") + lax.axis_index("c")).astype(jnp.int32).reshape(1)
    pending = []

    def start(stage, grads, carry):
        units = [u for u in _UNITS if _grad_stage_of(u[0], u[1]) == stage]
        parts, spans = [], []
        for name, layer, tr in units:
            key = _unit_key(name, tr)
            g = grads[(key, layer)] if name in _LAYERED else grads[key]
            part = g.reshape(4, 2, -1, RS_ROW)
            spans.append((part.shape[2], g.shape[0] // N_DEV, g.shape[1]))
            parts.append(part)
        if stage == GRAD_STAGES - 1:
            small = jnp.concatenate([_shard(grads[name], axis).reshape(N_DEV, -1) for name, axis in _SMALL_SHARDED], axis=1)
            small = jnp.pad(small, ((0, 0), (0, SMALL_ROWS * RS_ROW - small.shape[1])))
            parts.append(small.astype(BF16).reshape(4, 2, SMALL_ROWS, RS_ROW))
        pack = jnp.concatenate(parts, axis=2)
        pack = pack.reshape((N_DEV,) + pack.shape[2:])
        sems, own, land, carry = scatter_start(f"rs_start{stage}", pack, carry)
        pending.append((stage, units, spans, sems, own, land))
        return carry

    def finish(after):
        res, per_layer, small_flat = {}, {}, None
        for stage, units, spans, sems, own, land in pending:
            own, land = scatter_wait(f"rs_wait{stage}", own, land, sems, after)
            total = sum_shares(f"rs_sum{stage}", own, land, me)
            off = 0
            for (name, layer, tr), (rows, r, c) in zip(units, spans):
                per_layer.setdefault(name, {})[layer] = (total[off:off + rows].reshape(r, c), tr)
                off += rows
            if stage == GRAD_STAGES - 1:
                small_flat = total[off:off + SMALL_ROWS].reshape(-1)
        for name, by_layer in per_layer.items():
            pieces = [by_layer[i][0] for i in sorted(by_layer)]
            res[name] = adamw_units("adamw_" + name, pieces, by_layer[0][1], local[name], mom[name], var[name])
        off = 0
        for name, _ in _SMALL_SHARDED:
            blk = local[name]
            g = small_flat[off:off + blk.size].reshape(blk.shape)
            off += blk.size
            res[name] = (g,) + adamw_native("adamw_" + name, g, blk, mom[name], var[name])
        return res

    return start, finish


def replicated_start(grads, loss):
    pack = jnp.concatenate([_rows128(grads[name]) for name in _REPLICATED] + [_rows128(loss)], axis=0)
    sems, srcs, lands, token = gather_ici_start("ag_g_start", [[pack]], [False])
    return sems[0], srcs[0], lands[0], token


def replicated_finish(handle, after, w, mom, var):
    sems, srcs, lands, _ = handle
    (buf,) = gather_d2d("ag_g_d2d", gather_ici_wait("ag_g_wait", srcs, lands, sems, after))
    rows = srcs[0].shape[0]
    total = sum_slots("ag_g_sum", buf.reshape(N_DEV, rows, LANES))
    res, off = {}, 0
    for name in _REPLICATED:
        n = w[name].size
        nr = -(-n // (LANES * 8)) * 8
        g = total[off:off + nr].reshape(-1)[:n].reshape(w[name].shape)
        off += nr
        res[name] = (g,) + adamw_native("adamw_" + name, g, w[name], mom[name], var[name])
    return res, total[off, 0]


def kernel(x, mem, e_norm, e_w_in, e_gmlp_w, e_gmlp_b, e_conv_w, e_conv_b, e_conv_ln_g, e_conv_ln_b, e_w_out, o_norm, o_w_in, o_lam_re, o_lam_im, o_log_dt, o_b_re, o_b_im, o_c_re, o_c_im, o_d, o_w_out, ca_norm, ca_mem_norm, ca_wq, ca_wk, ca_wv, ca_wo, ffn_norm, ffn_w_gate, ffn_w_up, ffn_w_down, final_norm, loss_target, m_e_norm, m_e_w_in, m_e_gmlp_w, m_e_gmlp_b, m_e_conv_w, m_e_conv_b, m_e_conv_ln_g, m_e_conv_ln_b, m_e_w_out, m_o_norm, m_o_w_in, m_o_lam_re, m_o_lam_im, m_o_log_dt, m_o_b_re, m_o_b_im, m_o_c_re, m_o_c_im, m_o_d, m_o_w_out, m_ca_norm, m_ca_mem_norm, m_ca_wq, m_ca_wk, m_ca_wv, m_ca_wo, m_ffn_norm, m_ffn_w_gate, m_ffn_w_up, m_ffn_w_down, m_final_norm, v_e_norm, v_e_w_in, v_e_gmlp_w, v_e_gmlp_b, v_e_conv_w, v_e_conv_b, v_e_conv_ln_g, v_e_conv_ln_b, v_e_w_out, v_o_norm, v_o_w_in, v_o_lam_re, v_o_lam_im, v_o_log_dt, v_o_b_re, v_o_b_im, v_o_c_re, v_o_c_im, v_o_d, v_o_w_out, v_ca_norm, v_ca_mem_norm, v_ca_wq, v_ca_wk, v_ca_wv, v_ca_wo, v_ffn_norm, v_ffn_w_gate, v_ffn_w_up, v_ffn_w_down, v_final_norm):
    given = dict(locals())
    local = {k: given[k] for k in _ORDER}
    mom = {k: given["m_" + k] for k in _ORDER}
    var = {k: given["v_" + k] for k in _ORDER}

    w = {}
    w.update({
        "e_norm": e_norm, "e_gmlp_w": e_gmlp_w[0], "e_gmlp_b": e_gmlp_b.reshape(A_GROUPS, GMLP_BLOCK, 1),
        "e_conv_b": e_conv_b, "e_conv_ln_g": e_conv_ln_g, "e_conv_ln_b": e_conv_ln_b,
        "o_lam_re": o_lam_re[0], "o_lam_im": o_lam_im[0], "o_log_dt": o_log_dt[0], "o_b_re": o_b_re[0], "o_b_im": o_b_im[0],
        "o_c_re": o_c_re[0], "o_c_im": o_c_im[0], "ca_norm": ca_norm, "ca_mem_norm": ca_mem_norm, "ffn_norm": ffn_norm,
        "final_norm": final_norm.reshape(1, D_MODEL),
    })
    start_reduce, finish_reduce = gradient_reducer(local, mom, var)
    fetch, token = weight_fetcher(local)
    loss_part, grad_x, grads = local_step(x[0], mem[0], loss_target[0], w, fetch, start_reduce, token[0:1, 0:1])
    grads["final_norm"] = grads["final_norm"].reshape(D_MODEL)

    handle = replicated_start(grads, loss_part)
    res = finish_reduce(handle[3])
    rep, loss = replicated_finish(handle, res["ffn_w_down"][1], local, mom, var)
    res.update(rep)
    return (loss, grad_x[None], *[res[k][0] for k in _ORDER], *[res[k][1] for k in _ORDER],
            *[res[k][2] for k in _ORDER], *[res[k][3] for k in _ORDER])
```

```python
import jax
import jax.numpy as jnp
from jax import lax
from jax.experimental import pallas as pl
from jax.experimental.pallas import tpu as pltpu

F32 = jnp.float32
BF16 = jnp.bfloat16
S = jax.ShapeDtypeStruct

D_MODEL = 1024
A_WIDTH = 512
A_GROUPS = 4
GMLP_BLOCK = 128
CHUNK = 64
B_WIDTH = 512
IN_WIDTH = 2 * A_WIDTH + 2 * B_WIDTH
CONV_WIDTH = 31
CONV_PAD = 32
C_WIDTH = 512
C_GROUP_CH = 16
C_GROUPS = 32
C_STATE = 64
N_STATE = C_GROUPS * C_STATE
CA_HEADS = 4
CA_HEAD_DIM = 256
FFN_HIDDEN = 2816
EPS = 1e-6
ADAM_LR = 0.001
ADAM_B1 = 0.9
ADAM_B2 = 0.999
ADAM_EPS = 1e-08
ADAM_WD = 0.01
ADAM_STEP = 10
N_DEV = 8
LANES = 128
VMEM_LIMIT = 56 << 20
VMEM_BUDGET = 40 << 20
MM_TN_RESIDENT = 8 << 20
MESH = pl.DeviceIdType.MESH
ANY = pl.BlockSpec(memory_space=pl.ANY)


def _cp(*sem):
    return pltpu.CompilerParams(dimension_semantics=sem, vmem_limit_bytes=VMEM_LIMIT)


def _tile(n, pref):
    t = pref
    while n % t:
        t //= 2
    return t


def _bf(v):
    return v if v.dtype == BF16 else v.astype(BF16)


def _sigmoid(x):
    return 1.0 / (1.0 + jnp.exp(-x))


_GC = 0.7978845608028654


def _gelu(x):
    return 0.5 * x * (1.0 + jnp.tanh(_GC * (x + 0.044715 * x * x * x)))


def _gelu_grad(x):
    x2 = x * x
    t = jnp.tanh(_GC * (x + 0.044715 * x * x2))
    return 0.5 * (1.0 + t) + 0.5 * x * (1.0 - t * t) * _GC * (1.0 + 3.0 * 0.044715 * x2)


def _tspec(entry, tm):
    if isinstance(entry, tuple):
        arr, cb, width = entry
        return arr, pl.BlockSpec((tm, width), lambda i, cb=cb: (i, cb))
    return entry, pl.BlockSpec((tm, entry.shape[1]), lambda i: (i, 0))


def rows_call(name, fn, tiled, full, outs, accs, tm=256):
    pairs = [_tspec(e, tm) for e in tiled]
    arrs = [p[0] for p in pairs]
    rows = arrs[0].shape[0]
    tm = _tile(rows, tm)
    pairs = [_tspec(e, tm) for e in tiled]
    n_in = len(tiled) + len(full)
    n_out = len(outs)

    def body(*refs):
        vals = [r[...] for r in refs[:n_in]]
        o_refs = refs[n_in:n_in + n_out]
        a_refs = refs[n_in + n_out:]
        ov, av = fn(*vals)
        for r, v in zip(o_refs, ov):
            r[...] = v.astype(r.dtype)
        if a_refs:
            @pl.when(pl.program_id(0) == 0)
            def _():
                for r in a_refs:
                    r[...] = jnp.zeros(r.shape, r.dtype)
            for r, v in zip(a_refs, av):
                r[...] += v

    in_specs = [p[1] for p in pairs] + [pl.BlockSpec(a.shape, lambda i, nd=a.ndim: (0,) * nd) for a in full]
    out_specs = [pl.BlockSpec((tm, c), lambda i: (i, 0)) for c, _ in outs]
    out_specs += [pl.BlockSpec(s, lambda i, nd=len(s): (0,) * nd) for s in accs]
    out_shape = [S((rows, c), dt) for c, dt in outs] + [S(s, F32) for s in accs]
    return pl.pallas_call(body, grid=(rows // tm,), in_specs=in_specs, out_specs=out_specs, out_shape=out_shape,
                          compiler_params=_cp("arbitrary"), name=name)(*arrs, *full)


def mm_nn(name, m, n, pairs, n_acc, epi, outs, tiled=(), cols=(), rowv=(), sums=(), norm_gain=None):
    a_ops, a_slot, b_arrs, b_specs, idx, trans = [], [], [], [], [], []
    fixed = 0
    for pair in pairs:
        a, b, k = pair[:3]
        bt = len(pair) > 3
        arr, cb, kdim = a if isinstance(a, tuple) else (a, 0, a.shape[1])
        key = (id(arr), cb, kdim)
        if key not in [o[0] for o in a_ops]:
            a_ops.append((key, arr, cb, kdim))
        a_slot.append([o[0] for o in a_ops].index(key))
        b_arr, off = b if isinstance(b, tuple) else (b, 0)
        b_arrs.append(b_arr)
        if bt:
            assert off % n == 0 and b_arr.shape[1] == kdim
            b_specs.append(pl.BlockSpec((n, kdim), lambda i, o=off // n: (o, 0), pipeline_mode=pl.Buffered(1)))
        else:
            assert b_arr.shape[1] == n
            b_specs.append(pl.BlockSpec((kdim, n), lambda i, o=off: (o, 0), pipeline_mode=pl.Buffered(1)))
        fixed += kdim * n * b_arr.dtype.itemsize
        idx.append(k)
        trans.append(bt)
    per_row = sum(2 * kdim * arr.dtype.itemsize for _, arr, _, kdim in a_ops)
    per_row += sum(2 * n * t.dtype.itemsize for t in tiled) + sum(2 * n * jnp.dtype(dt).itemsize for dt in outs)
    cn = n if sums or cols else (512 if n % 512 == 0 else 256)
    per_row += (n_acc + 3) * cn * 4
    tm = next((t for t in (1024, 512, 256, 128) if m % t == 0 and fixed + t * per_row <= VMEM_BUDGET), _tile(m, 128))
    n_a, n_p, n_t = len(a_ops), len(pairs), len(tiled)
    n_in = n_a + n_p + n_t + len(cols) + len(rowv)
    normed = norm_gain is not None
    o0 = n_in + normed

    def body(*refs):
        a_vals = [None if normed and i == 0 else _bf(r[...]) for i, r in enumerate(refs[:n_a])]
        if normed:
            xv = refs[0][...]
            rv = lax.rsqrt(jnp.mean(xv * xv, axis=-1, keepdims=True) + EPS)
            a_vals[0] = (xv * rv * refs[n_in][...]).astype(BF16)
            refs[o0 + len(outs)][...] = a_vals[0]
            refs[o0 + len(outs) + 1][...] = rv
        for j in range(n // cn):
            cs = slice(j * cn, (j + 1) * cn)
            accs = [None] * n_acc
            for p in range(n_p):
                av, b_ref = a_vals[a_slot[p]], refs[n_a + p]
                if trans[p]:
                    d = lax.dot_general(av, _bf(b_ref[cs, :]), (((1,), (1,)), ((), ())), preferred_element_type=F32)
                else:
                    d = jnp.dot(av, _bf(b_ref[:, cs]), preferred_element_type=F32)
                accs[idx[p]] = d if accs[idx[p]] is None else accs[idx[p]] + d
            extra = [r[:, cs] for r in refs[n_a + n_p:n_a + n_p + n_t]] + [r[...] for r in refs[n_a + n_p + n_t:n_in - len(rowv)]]
            extra += [r[:, cs] for r in refs[n_in - len(rowv):n_in]]
            ov = epi(accs, *extra)
            for r, v in zip(refs[o0:o0 + len(outs)], ov):
                r[:, cs] = v.astype(r.dtype)
        sv = ov[len(outs):]
        if sums:
            s_refs = refs[o0 + len(outs) + 2 * normed:]

            @pl.when(pl.program_id(0) == 0)
            def _():
                for r in s_refs:
                    r[...] = jnp.zeros(r.shape, r.dtype)
            for r, v in zip(s_refs, sv):
                r[...] += v

    in_specs = [pl.BlockSpec((tm, kdim), lambda i, cb=cb: (i, cb)) for _, _, cb, kdim in a_ops] + b_specs
    in_specs += [pl.BlockSpec((tm, n), lambda i: (i, 0)) for _ in tiled]
    in_specs += [pl.BlockSpec((tm, 1), lambda i: (i, 0)) for _ in cols]
    in_specs += [pl.BlockSpec((1, n), lambda i: (0, 0)) for _ in rowv]
    out_specs = [pl.BlockSpec((tm, n), lambda i: (i, 0)) for _ in outs]
    out_shape = [S((m, n), dt) for dt in outs]
    gain = []
    if normed:
        k0 = a_ops[0][3]
        gain = [norm_gain]
        in_specs.append(pl.BlockSpec((1, k0), lambda i: (0, 0)))
        out_specs += [pl.BlockSpec((tm, k0), lambda i: (i, 0)), pl.BlockSpec((tm, 1), lambda i: (i, 0))]
        out_shape += [S((m, k0), BF16), S((m, 1), F32)]
    out_specs += [pl.BlockSpec(s, lambda i, nd=len(s): (0,) * nd) for s in sums]
    out_shape += [S(s, F32) for s in sums]
    return pl.pallas_call(body, grid=(m // tm,), in_specs=in_specs, out_specs=out_specs, out_shape=out_shape,
                          compiler_params=_cp("arbitrary" if sums else "parallel"),
                          name=name)(*[o[1] for o in a_ops], *b_arrs, *tiled, *cols, *rowv, *gain)


def mm_tn(name, a, b, out_dtype=BF16):
    if isinstance(a, tuple):
        a_arr, a_cb, m = a
    else:
        a_arr, a_cb, m = a, None, a.shape[1]
    if isinstance(b, tuple):
        b_arr, b_cb, n = b
    else:
        b_arr, b_cb, n = b, None, b.shape[1]
    t = a_arr.shape[0]
    whole_b = t * n * b_arr.dtype.itemsize <= MM_TN_RESIDENT and b_cb is None
    tn = n if whole_b else _tile(n, 512)
    tm = _tile(m, 512 if t * 512 * a_arr.dtype.itemsize * 2 + t * tn * b_arr.dtype.itemsize * 2 <= VMEM_BUDGET else 256)
    a_off = 0 if a_cb is None else a_cb * (m // tm)
    b_off = 0 if b_cb is None else b_cb * (n // tn)

    def body(a_ref, b_ref, o_ref):
        o_ref[...] = lax.dot_general(_bf(a_ref[...]), _bf(b_ref[...]), (((0,), (0,)), ((), ())),
                                     preferred_element_type=F32).astype(o_ref.dtype)

    if whole_b:
        b_spec = pl.BlockSpec((t, n), lambda i, j: (0, 0), pipeline_mode=pl.Buffered(1))
    else:
        b_spec = pl.BlockSpec((t, tn), lambda i, j: (0, j + b_off))
    return pl.pallas_call(
        body, grid=(m // tm, n // tn),
        in_specs=[pl.BlockSpec((t, tm), lambda i, j: (0, i + a_off)), b_spec],
        out_specs=pl.BlockSpec((tm, tn), lambda i, j: (i, j)), out_shape=S((m, n), out_dtype),
        compiler_params=_cp("parallel", "parallel"), name=name)(a_arr, b_arr)


def rms_bwd_gain_only(name, dxn, x, r):
    def fn(dv, xv, rv):
        return [], [jnp.sum(dv * xv * rv, axis=0, keepdims=True)]
    return rows_call(name, fn, [dxn, x, r], [], [], [(1, x.shape[1])])[0]


def _final_loss_epi(accs, res, tv, g):
    xv = res + accs[0]
    d = xv.shape[-1]
    r = lax.rsqrt(jnp.mean(xv * xv, axis=-1, keepdims=True) + EPS)
    xh = xv * r
    err = xh * g - tv
    dy = err * (1.0 / d)
    w = dy * g
    dx = r * (w - xh * jnp.mean(w * xh, axis=-1, keepdims=True))
    part = jnp.sum(jnp.sum(err * err, axis=-1, keepdims=True), axis=0, keepdims=True) * (0.5 / d)
    return [dx, dx, jnp.sum(dy * xh, axis=0, keepdims=True), part]


def _gmlp_mask():
    row = lax.broadcasted_iota(jnp.int32, (GMLP_BLOCK, GMLP_BLOCK), 0) // CHUNK
    col = lax.broadcasted_iota(jnp.int32, (GMLP_BLOCK, GMLP_BLOCK), 1) // CHUNK
    return col <= row


def _ln_plain(v):
    mu = jnp.mean(v, axis=-1, keepdims=True)
    vc = v - mu
    rstd = lax.rsqrt(jnp.mean(vc * vc, axis=-1, keepdims=True) + EPS)
    return vc * rstd, rstd


def gmlp_fwd(name, proj, w, b, tm=512):
    t = proj.shape[0]
    tm = _tile(t, tm)

    def body(au_ref, av_ref, w_ref, b_ref, o_ref):
        mask = _gmlp_mask()
        u = _gelu(au_ref[...])
        vn, _ = _ln_plain(_gelu(av_ref[...]))
        vnb = _bf(vn)
        for g in range(A_GROUPS):
            wg = _bf(jnp.where(mask, w_ref[g], 0.0))
            cs = slice(g * GMLP_BLOCK, (g + 1) * GMLP_BLOCK)
            for n in range(tm // GMLP_BLOCK):
                rs = slice(n * GMLP_BLOCK, (n + 1) * GMLP_BLOCK)
                sg = jnp.dot(wg, vnb[rs, cs], preferred_element_type=F32) + b_ref[g]
                o_ref[rs, cs] = (u[rs, cs] * sg).astype(o_ref.dtype)

    return pl.pallas_call(
        body, grid=(t // tm,),
        in_specs=[pl.BlockSpec((tm, A_WIDTH), lambda i: (i, 0)), pl.BlockSpec((tm, A_WIDTH), lambda i: (i, 1)),
                  pl.BlockSpec(w.shape, lambda i: (0, 0, 0)), pl.BlockSpec(b.shape, lambda i: (0, 0, 0))],
        out_specs=pl.BlockSpec((tm, A_WIDTH), lambda i: (i, 0)), out_shape=S((t, A_WIDTH), BF16),
        compiler_params=_cp("parallel"), name=name)(proj, proj, w, b)


def gmlp_bwd(name, proj, dcat, w, b, tm=512):
    t = proj.shape[0]
    tm = _tile(t, tm)

    def body(au_ref, av_ref, do_ref, w_ref, b_ref, dp_ref, dw_ref, db_ref):
        @pl.when(pl.program_id(0) == 0)
        def _():
            dw_ref[...] = jnp.zeros(dw_ref.shape, F32)
            db_ref[...] = jnp.zeros(db_ref.shape, F32)

        mask = _gmlp_mask()
        au = au_ref[...]
        av = av_ref[...]
        u = _gelu(au)
        vn, rstd = _ln_plain(_gelu(av))
        vnb = _bf(vn)
        dout = do_ref[...]
        dvn_cols = []
        for g in range(A_GROUPS):
            wm = jnp.where(mask, w_ref[g], 0.0)
            wg = _bf(wm)
            wgt = _bf(wm.T)
            cs = slice(g * GMLP_BLOCK, (g + 1) * GMLP_BLOCK)
            dwg = jnp.zeros((GMLP_BLOCK, GMLP_BLOCK), F32)
            dbg = jnp.zeros((GMLP_BLOCK, 1), F32)
            dvn_rows = []
            for n in range(tm // GMLP_BLOCK):
                rs = slice(n * GMLP_BLOCK, (n + 1) * GMLP_BLOCK)
                sg = jnp.dot(wg, vnb[rs, cs], preferred_element_type=F32) + b_ref[g]
                dp_ref[rs, cs] = (dout[rs, cs] * sg * _gelu_grad(au[rs, cs])).astype(dp_ref.dtype)
                dsg = dout[rs, cs] * u[rs, cs]
                dsgb = _bf(dsg)
                dbg = dbg + jnp.sum(dsg, axis=1, keepdims=True)
                dwg = dwg + lax.dot_general(dsgb, vnb[rs, cs], (((1,), (1,)), ((), ())), preferred_element_type=F32)
                dvn_rows.append(jnp.dot(wgt, dsgb, preferred_element_type=F32))
            dw_ref[g] += jnp.where(mask, dwg, 0.0)
            db_ref[g] += dbg
            dvn_cols.append(jnp.concatenate(dvn_rows, axis=0))
        dvn = jnp.concatenate(dvn_cols, axis=1)
        dv = rstd * (dvn - jnp.mean(dvn, axis=-1, keepdims=True) - vn * jnp.mean(dvn * vn, axis=-1, keepdims=True))
        dp_ref[:, A_WIDTH:] = (dv * _gelu_grad(av)).astype(dp_ref.dtype)

    return pl.pallas_call(
        body, grid=(t // tm,),
        in_specs=[pl.BlockSpec((tm, A_WIDTH), lambda i: (i, 0)), pl.BlockSpec((tm, A_WIDTH), lambda i: (i, 1)),
                  pl.BlockSpec((tm, A_WIDTH), lambda i: (i, 0)),
                  pl.BlockSpec(w.shape, lambda i: (0, 0, 0)), pl.BlockSpec(b.shape, lambda i: (0, 0, 0))],
        out_specs=[pl.BlockSpec((tm, 2 * A_WIDTH), lambda i: (i, 0)),
                   pl.BlockSpec(w.shape, lambda i: (0, 0, 0)), pl.BlockSpec(b.shape, lambda i: (0, 0, 0))],
        out_shape=[S((t, 2 * A_WIDTH), BF16), S(w.shape, F32), S(b.shape, F32)],
        compiler_params=_cp("arbitrary"), name=name)(proj, proj, dcat, w, b)


CONV_ROWS = 256


def conv_fwd(name, proj, w, cb):
    t = proj.shape[0]
    tc = LANES
    rows = _tile(t, CONV_ROWS)
    a_cb, g_cb = 2 * A_WIDTH // tc, (2 * A_WIDTH + B_WIDTH) // tc

    def body(a_ref, g_ref, w_ref, cb_ref, o_ref, hpad):
        hpad[0:CONV_PAD, :] = jnp.zeros((CONV_PAD, tc), F32)

        def fill(i, _):
            r0 = pl.multiple_of(i * rows, rows)
            hpad[pl.ds(CONV_PAD + r0, rows), :] = a_ref[pl.ds(r0, rows), :] * _sigmoid(g_ref[pl.ds(r0, rows), :])
            return 0
        lax.fori_loop(0, t // rows, fill, 0)

        def conv(i, _):
            r0 = pl.multiple_of(i * rows, rows)
            win = hpad[pl.ds(r0, rows + CONV_PAD), :]
            acc = jnp.zeros((rows, tc), F32) + cb_ref[...]
            for b in range(SUB):
                wb = win if b == 0 else pltpu.roll(win, b, 0)
                for a in range(CONV_PAD // SUB):
                    k = CONV_WIDTH - 1 - (SUB * a + b)
                    if k >= 0:
                        lo = CONV_PAD - SUB * a
                        acc = acc + wb[lo:lo + rows, :] * w_ref[k:k + 1, :]
            o_ref[pl.ds(r0, rows), :] = acc
            return 0
        lax.fori_loop(0, t // rows, conv, 0)

    return pl.pallas_call(
        body, grid=(B_WIDTH // tc,),
        in_specs=[pl.BlockSpec((t, tc), lambda j: (0, a_cb + j)), pl.BlockSpec((t, tc), lambda j: (0, g_cb + j)),
                  pl.BlockSpec((CONV_WIDTH, tc), lambda j: (0, j)), pl.BlockSpec((1, tc), lambda j: (0, j))],
        out_specs=pl.BlockSpec((t, tc), lambda j: (0, j)), out_shape=S((t, B_WIDTH), F32),
        scratch_shapes=[pltpu.VMEM((t + CONV_PAD, tc), F32)],
        compiler_params=_cp("parallel"), name=name)(proj, proj, w, cb)


def conv_bwd(name, proj, dhc, w):
    t = proj.shape[0]
    tc = LANES
    rows = _tile(t, CONV_ROWS)
    a_cb, g_cb = 2 * A_WIDTH // tc, (2 * A_WIDTH + B_WIDTH) // tc
    win_rows = rows + CONV_PAD

    def body(a_ref, g_ref, d_ref, w_ref, da_ref, dg_ref, dw_ref, dcb_ref, hpad, dpad, dwacc):
        hpad[0:CONV_PAD, :] = jnp.zeros((CONV_PAD, tc), F32)
        dpad[t:t + CONV_PAD, :] = jnp.zeros((CONV_PAD, tc), F32)
        dwacc[...] = jnp.zeros(dwacc.shape, F32)

        def fill(i, _):
            r0 = pl.multiple_of(i * rows, rows)
            hpad[pl.ds(CONV_PAD + r0, rows), :] = a_ref[pl.ds(r0, rows), :] * _sigmoid(g_ref[pl.ds(r0, rows), :])
            dpad[pl.ds(r0, rows), :] = d_ref[pl.ds(r0, rows), :]
            return 0
        lax.fori_loop(0, t // rows, fill, 0)

        def step(i, dcb):
            r0 = pl.multiple_of(i * rows, rows)
            hwin = hpad[pl.ds(r0, win_rows), :]
            dwin = dpad[pl.ds(r0, win_rows), :]
            dchunk = dwin[:rows, :]
            dh = jnp.zeros((rows, tc), F32)
            for b in range(SUB):
                hb = hwin if b == 0 else pltpu.roll(hwin, b, 0)
                db = dwin if b == 0 else pltpu.roll(dwin, win_rows - b, 0)
                for a in range(CONV_PAD // SUB):
                    k = CONV_WIDTH - 1 - (SUB * a + b)
                    if k >= 0:
                        dh = dh + db[SUB * a:SUB * a + rows, :] * w_ref[k:k + 1, :]
                        lo = CONV_PAD - SUB * a
                        prod = dchunk * hb[lo:lo + rows, :]
                        dwacc[k] += jnp.sum(prod.reshape(rows // 8, 8, tc), axis=0)
            a = a_ref[pl.ds(r0, rows), :]
            sg = _sigmoid(g_ref[pl.ds(r0, rows), :])
            da_ref[pl.ds(r0, rows), :] = (dh * sg).astype(da_ref.dtype)
            dg_ref[pl.ds(r0, rows), :] = (dh * a * sg * (1.0 - sg)).astype(dg_ref.dtype)
            return dcb + jnp.sum(dchunk, axis=0, keepdims=True)
        dcb = lax.fori_loop(0, t // rows, step, jnp.zeros((1, tc), F32))
        dcb_ref[...] = dcb
        for k in range(CONV_WIDTH):
            dw_ref[k:k + 1, :] = jnp.sum(dwacc[k], axis=0, keepdims=True)

    return pl.pallas_call(
        body, grid=(B_WIDTH // tc,),
        in_specs=[pl.BlockSpec((t, tc), lambda j: (0, a_cb + j)), pl.BlockSpec((t, tc), lambda j: (0, g_cb + j)),
                  pl.BlockSpec((t, tc), lambda j: (0, j)), pl.BlockSpec((CONV_WIDTH, tc), lambda j: (0, j))],
        out_specs=[pl.BlockSpec((t, tc), lambda j: (0, j)), pl.BlockSpec((t, tc), lambda j: (0, j)),
                   pl.BlockSpec((CONV_WIDTH, tc), lambda j: (0, j)), pl.BlockSpec((1, tc), lambda j: (0, j))],
        out_shape=[S((t, B_WIDTH), BF16), S((t, B_WIDTH), BF16), S((CONV_WIDTH, B_WIDTH), F32), S((1, B_WIDTH), F32)],
        scratch_shapes=[pltpu.VMEM((t + CONV_PAD, tc), F32), pltpu.VMEM((t + CONV_PAD, tc), F32),
                        pltpu.VMEM((CONV_WIDTH, 8, tc), F32)],
        compiler_params=_cp("parallel"), name=name)(proj, proj, dhc, w)


def ln_silu_fwd(name, hc, g, b):
    def fn(h, gv, bv):
        y, _ = _ln_plain(h)
        z = y * gv + bv
        return [z * _sigmoid(z)], []
    return rows_call(name, fn, [hc], [g, b], [(hc.shape[1], BF16)], [])[0]


def ln_silu_bwd(name, hc, dcat, g, b):
    c = hc.shape[1]

    def fn(h, dout, gv, bv):
        y, rstd = _ln_plain(h)
        z = y * gv + bv
        s = _sigmoid(z)
        dz = dout * s * (1.0 + z * (1.0 - s))
        dyv = dz * gv
        dh = rstd * (dyv - jnp.mean(dyv, axis=-1, keepdims=True) - y * jnp.mean(dyv * y, axis=-1, keepdims=True))
        return [dh], [jnp.sum(dz * y, axis=0, keepdims=True), jnp.sum(dz, axis=0, keepdims=True)]

    return rows_call(name, fn, [hc, (dcat, 1, c)], [g, b], [(c, F32)], [(1, c), (1, c)])


_NT = (((1,), (1,)), ((), ()))
_TN = (((0,), (0,)), ((), ()))


def attn_fwd(name, x, gain, wq, k, v, wo, tm=512):
    t, d = x.shape
    m = k.shape[0]
    tm = _tile(t, tm)
    scale = CA_HEAD_DIM ** -0.5

    def body(x_ref, g_ref, wq_ref, k_ref, v_ref, wo_ref, x1_ref, xn_ref, r_ref, q_ref, o_ref):
        xv = x_ref[...]
        rv = lax.rsqrt(jnp.mean(xv * xv, axis=-1, keepdims=True) + EPS)
        xn = (xv * rv * g_ref[...]).astype(BF16)
        xn_ref[...] = xn
        r_ref[...] = rv
        q_ref[...] = jnp.dot(xn, wq_ref[...], preferred_element_type=F32).astype(BF16)
        for h in range(CA_HEADS):
            cs = slice(h * CA_HEAD_DIM, (h + 1) * CA_HEAD_DIM)
            s = lax.dot_general(q_ref[:, cs], k_ref[:, cs], _NT, preferred_element_type=F32) * scale
            e = jnp.exp(s - jnp.max(s, axis=-1, keepdims=True))
            p = e / jnp.sum(e, axis=-1, keepdims=True)
            o_ref[:, cs] = jnp.dot(_bf(p), v_ref[:, cs], preferred_element_type=F32).astype(o_ref.dtype)
        x1_ref[...] = xv + jnp.dot(o_ref[...], wo_ref[...], preferred_element_type=F32)

    def whole(a):
        return pl.BlockSpec(a.shape, lambda i: (0, 0), pipeline_mode=pl.Buffered(1))

    rows = pl.BlockSpec((tm, d), lambda i: (i, 0))
    col = pl.BlockSpec((tm, 1), lambda i: (i, 0))
    return pl.pallas_call(
        body, grid=(t // tm,),
        in_specs=[rows, whole(gain), whole(wq), whole(k), whole(v), whole(wo)],
        out_specs=[rows, rows, col, rows, rows],
        out_shape=[S((t, d), F32), S((t, d), BF16), S((t, 1), F32), S((t, d), BF16), S((t, d), BF16)],
        compiler_params=_cp("parallel"), name=name)(x, gain, wq, k, v, wo)


def attn_bwd(name, dx, dxb, x, r, gain, q, k, v, wq, wo, tm=512):
    t, d = q.shape
    m = k.shape[0]
    tm = _tile(t, tm)
    scale = CA_HEAD_DIM ** -0.5

    def body(dx_ref, dxb_ref, x_ref, r_ref, g_ref, q_ref, k_ref, v_ref, wq_ref, wo_ref,
             dxo_ref, dxbo_ref, dq_ref, dk_ref, dv_ref, dg_ref, do_s):
        @pl.when(pl.program_id(0) == 0)
        def _():
            dk_ref[...] = jnp.zeros(dk_ref.shape, F32)
            dv_ref[...] = jnp.zeros(dv_ref.shape, F32)
            dg_ref[...] = jnp.zeros(dg_ref.shape, F32)

        do_s[...] = lax.dot_general(dxb_ref[...], wo_ref[...], _NT, preferred_element_type=F32).astype(BF16)
        for h in range(CA_HEADS):
            cs = slice(h * CA_HEAD_DIM, (h + 1) * CA_HEAD_DIM)
            qh, kh, vh, doh = q_ref[:, cs], k_ref[:, cs], v_ref[:, cs], do_s[:, cs]
            s = lax.dot_general(qh, kh, _NT, preferred_element_type=F32) * scale
            e = jnp.exp(s - jnp.max(s, axis=-1, keepdims=True))
            p = e / jnp.sum(e, axis=-1, keepdims=True)
            pb = _bf(p)
            dv_ref[:, cs] += lax.dot_general(pb, doh, _TN, preferred_element_type=F32)
            dp = lax.dot_general(doh, vh, _NT, preferred_element_type=F32)
            ds = _bf(p * (dp - jnp.sum(dp * p, axis=-1, keepdims=True)) * scale)
            dq_ref[:, cs] = jnp.dot(ds, kh, preferred_element_type=F32).astype(dq_ref.dtype)
            dk_ref[:, cs] += lax.dot_general(ds, qh, _TN, preferred_element_type=F32)
        dxn = lax.dot_general(dq_ref[...], wq_ref[...], _NT, preferred_element_type=F32)
        xh = x_ref[...] * r_ref[...]
        wv = dxn * g_ref[...]
        dxo = dx_ref[...] + r_ref[...] * (wv - xh * jnp.mean(wv * xh, axis=-1, keepdims=True))
        dxo_ref[...] = dxo
        dxbo_ref[...] = dxo.astype(BF16)
        dg_ref[...] += jnp.sum(dxn * xh, axis=0, keepdims=True)

    def whole(a):
        return pl.BlockSpec(a.shape, lambda i: (0, 0), pipeline_mode=pl.Buffered(1))

    rows = pl.BlockSpec((tm, d), lambda i: (i, 0))
    col = pl.BlockSpec((tm, 1), lambda i: (i, 0))
    acc = pl.BlockSpec((m, d), lambda i: (0, 0))
    return pl.pallas_call(
        body, grid=(t // tm,),
        in_specs=[rows, rows, rows, col, whole(gain), rows, whole(k), whole(v), whole(wq), whole(wo)],
        out_specs=[rows, rows, rows, acc, acc, pl.BlockSpec((1, d), lambda i: (0, 0))],
        out_shape=[S((t, d), F32), S((t, d), BF16), S((t, d), BF16), S((m, d), F32), S((m, d), F32), S((1, d), F32)],
        scratch_shapes=[pltpu.VMEM((tm, d), BF16)],
        compiler_params=_cp("arbitrary"), name=name)(dx, dxb, x, r, gain, q, k, v, wq, wo)


SUB = 8
S5_ROWS = 256


S5_BLOCKS = 4
BLOCK_CH = C_WIDTH // S5_BLOCKS
BLOCK_ST = N_STATE // S5_BLOCKS
_S5_BLOCKS = tuple((slice(BLOCK_CH * q, BLOCK_CH * (q + 1)), slice(BLOCK_ST * q, BLOCK_ST * (q + 1)),
                    slice(N_STATE + BLOCK_ST * q, N_STATE + BLOCK_ST * (q + 1))) for q in range(S5_BLOCKS))
_HI = lax.Precision.HIGHEST
_GP = (C_GROUPS, C_STATE)
_RP = (C_WIDTH, C_STATE)


def _zoh(lr, li, ldt):
    dt = jnp.exp(ldt)
    mag = jnp.exp(lr * dt)
    ar = mag * jnp.cos(li * dt)
    ai = mag * jnp.sin(li * dt)
    den = lr * lr + li * li
    qr = ((ar - 1.0) * lr + ai * li) / den
    qi = (ai * lr - (ar - 1.0) * li) / den
    return dt, ar, ai, den, qr, qi


def _per_channel(v):
    return jnp.broadcast_to(v[:, None, :], (C_GROUPS, C_GROUP_CH, C_STATE)).reshape(_RP)


def _same_group(shape, row_per_group, col_per_group):
    rows = lax.broadcasted_iota(jnp.int32, shape, 0) // row_per_group
    cols = lax.broadcasted_iota(jnp.int32, shape, 1) // col_per_group
    return rows == cols


def _spread(shape, axis):
    long = lax.broadcasted_iota(jnp.int32, shape, axis) % C_STATE
    short = lax.broadcasted_iota(jnp.int32, shape, 1 - axis)
    return long == short


def s5_discretise(name, lam_re, lam_im, log_dt, bt_re, bt_im):
    def body(lr_ref, li_ref, ldt_ref, btr_ref, bti_ref, a_ref, bbr_ref, bbi_ref):
        _, ar, ai, _, qr, qi = _zoh(lr_ref[...], li_ref[...], ldt_ref[...])
        a_ref[0] = ar
        a_ref[1] = ai
        q2r, q2i = _per_channel(qr), _per_channel(qi)
        btr, bti = btr_ref[...], bti_ref[...]
        bbr_ref[...] = q2r * btr - q2i * bti
        bbi_ref[...] = q2r * bti + q2i * btr

    return pl.pallas_call(body, out_shape=[S((2,) + _GP, F32), S(_RP, F32), S(_RP, F32)],
                          name=name)(lam_re, lam_im, log_dt, bt_re, bt_im)


def s5_operands(name, a, bbr, bbi, c2r, c2i, ctr, cti):
    ns = N_STATE

    def body(a_ref, bbr_ref, bbi_ref, c2r_ref, c2i_ref, ctr_ref, cti_ref, pw_ref, qw_ref, mb_ref, mc_ref, mct_ref):
        ar, ai = a_ref[0:1, :], a_ref[1:2, :]
        pows = [(ar, ai)]
        for _ in range(SUB - 1):
            pr, pi = pows[-1]
            pows.append((pr * ar - pi * ai, pr * ai + pi * ar))
        rows = lax.broadcasted_iota(jnp.int32, (SUB, ns), 0)

        def rows_of(v):
            return jnp.broadcast_to(v, (SUB, ns))

        for k, s in enumerate((1, 2, 4)):
            pr, pi = rows_of(pows[s - 1][0]), rows_of(pows[s - 1][1])
            pw_ref[k, 0] = jnp.where(rows >= s, pr, 0.0)
            pw_ref[k, 1] = jnp.where(rows >= s, pi, 0.0)
            qw_ref[k, 0] = jnp.where(rows + s <= SUB - 1, pr, 0.0)
            qw_ref[k, 1] = jnp.where(rows + s <= SUB - 1, -pi, 0.0)
        fr = fi = br = bi = jnp.zeros((SUB, ns), F32)
        for i in range(SUB):
            fr = jnp.where(rows == i, rows_of(pows[i][0]), fr)
            fi = jnp.where(rows == i, rows_of(pows[i][1]), fi)
            br = jnp.where(rows == i, rows_of(pows[SUB - 1 - i][0]), br)
            bi = jnp.where(rows == i, rows_of(-pows[SUB - 1 - i][1]), bi)
        pw_ref[3, 0], pw_ref[3, 1], qw_ref[3, 0], qw_ref[3, 1] = fr, fi, br, bi

        wide = _spread((C_STATE, ns), 1).astype(BF16)
        tall = _spread((ns, C_STATE), 0).astype(BF16)
        in_rows = _same_group((C_WIDTH, ns), C_GROUP_CH, C_STATE)
        in_cols = _same_group((ns, C_WIDTH), C_STATE, C_GROUP_CH)

        def across(v, sign=1.0):
            return jnp.where(in_rows, sign * jnp.dot(_bf(v), wide, preferred_element_type=F32), 0.0).astype(BF16)

        def down(vt, sign=1.0):
            return jnp.where(in_cols, sign * jnp.dot(tall, _bf(vt), preferred_element_type=F32), 0.0).astype(BF16)

        mb_ref[:, 0:ns] = across(bbr_ref[...])
        mb_ref[:, ns:2 * ns] = across(bbi_ref[...])
        mct_ref[:, 0:ns] = across(c2r_ref[...])
        mct_ref[:, ns:2 * ns] = across(c2i_ref[...], -1.0)
        mc_ref[0:ns, :] = down(ctr_ref[...])
        mc_ref[ns:2 * ns, :] = down(cti_ref[...], -1.0)

    return pl.pallas_call(
        body, out_shape=[S((4, 2, SUB, ns), F32), S((4, 2, SUB, ns), F32), S((C_WIDTH, 2 * ns), BF16),
                         S((2 * ns, C_WIDTH), BF16), S((C_WIDTH, 2 * ns), BF16)],
        compiler_params=pltpu.CompilerParams(vmem_limit_bytes=VMEM_LIMIT), name=name)(a, bbr, bbi, c2r, c2i, ctr, cti)


def s5_block_grads(name, u, lamb, xsb, dyb):
    t = u.shape[0]

    def mb_body(u_ref, lr_ref, li_ref, o_ref):
        ub = _bf(u_ref[...])
        o_ref[:, 0:BLOCK_ST] = lax.dot_general(ub, lr_ref[...], _TN, preferred_element_type=F32)
        o_ref[:, BLOCK_ST:2 * BLOCK_ST] = lax.dot_general(ub, li_ref[...], _TN, preferred_element_type=F32)

    d_mb = pl.pallas_call(
        mb_body, grid=(S5_BLOCKS,),
        in_specs=[pl.BlockSpec((t, BLOCK_CH), lambda q: (0, q)), pl.BlockSpec((t, BLOCK_ST), lambda q: (0, q)),
                  pl.BlockSpec((t, BLOCK_ST), lambda q: (0, S5_BLOCKS + q))],
        out_specs=pl.BlockSpec((BLOCK_CH, 2 * BLOCK_ST), lambda q: (q, 0)), out_shape=S((C_WIDTH, 2 * BLOCK_ST), F32),
        compiler_params=_cp("parallel"), name=name + "_b")(u, lamb, lamb)

    def mc_body(x_ref, dy_ref, o_ref):
        o_ref[...] = lax.dot_general(x_ref[...], dy_ref[...], _TN, preferred_element_type=F32)

    d_mc = pl.pallas_call(
        mc_body, grid=(2, S5_BLOCKS),
        in_specs=[pl.BlockSpec((t, BLOCK_ST), lambda p, q: (0, p * S5_BLOCKS + q)), pl.BlockSpec((t, BLOCK_CH), lambda p, q: (0, q))],
        out_specs=pl.BlockSpec((BLOCK_ST, BLOCK_CH), lambda p, q: (p * S5_BLOCKS + q, 0)),
        out_shape=S((2 * N_STATE, BLOCK_CH), F32), compiler_params=_cp("parallel", "parallel"), name=name + "_c")(xsb, dyb)
    return d_mb, d_mc


def s5_param_grads(name, d_mb, d_mc, da, lam_re, lam_im, log_dt, bt_re, bt_im):
    ns = N_STATE

    def body(dmb_ref, dmc_ref, da_ref, lr_ref, li_ref, ldt_ref, btr_ref, bti_ref,
             glr_ref, gli_ref, gdt_ref, gbr_ref, gbi_ref, gcr_ref, gci_ref):
        lr, li = lr_ref[...], li_ref[...]
        dt, ar, ai, den, qr, qi = _zoh(lr, li, ldt_ref[...])
        per_block = C_GROUPS // S5_BLOCKS
        wide = _spread((C_STATE, BLOCK_ST), 1).astype(F32)
        tall = _spread((BLOCK_ST, C_STATE), 0).astype(F32)
        rows = lax.broadcasted_iota(jnp.int32, (C_WIDTH, BLOCK_ST), 0) // C_GROUP_CH % per_block
        in_rows = rows == lax.broadcasted_iota(jnp.int32, (C_WIDTH, BLOCK_ST), 1) // C_STATE
        in_cols = _same_group((BLOCK_ST, BLOCK_CH), C_STATE, C_GROUP_CH)

        def fold_rows(v):
            return lax.dot_general(jnp.where(in_rows, v, 0.0), wide, (((1,), (1,)), ((), ())), precision=_HI,
                                   preferred_element_type=F32)

        def fold_cols(v):
            return lax.dot_general(jnp.where(in_cols, v, 0.0), tall, (((0,), (0,)), ((), ())), precision=_HI,
                                   preferred_element_type=F32)

        for cs, s_re, s_im in _S5_BLOCKS:
            gcr_ref[cs, :] = fold_cols(dmc_ref[s_re, :])
            gci_ref[cs, :] = -fold_cols(dmc_ref[s_im, :])
        gbbr = fold_rows(dmb_ref[:, 0:BLOCK_ST])
        gbbi = fold_rows(dmb_ref[:, BLOCK_ST:2 * BLOCK_ST])
        btr, bti = btr_ref[...], bti_ref[...]
        q2r, q2i = _per_channel(qr), _per_channel(qi)
        gbr_ref[...] = q2r * gbbr + q2i * gbbi
        gbi_ref[...] = q2r * gbbi - q2i * gbbr

        def per_group(v):
            return jnp.sum(v.reshape(C_GROUPS, C_GROUP_CH, C_STATE), axis=1)

        gqr = per_group(btr * gbbr + bti * gbbi)
        gqi = per_group(btr * gbbi - bti * gbbr)
        ilr, ili = lr / den, li / den
        gar = da_ref[0] + ilr * gqr - ili * gqi
        gai = da_ref[1] + ilr * gqi + ili * gqr
        sr = (qr * lr + qi * li) / den
        si = (qi * lr - qr * li) / den
        gzr = ar * gar + ai * gai
        gzi = ar * gai - ai * gar
        glr_ref[...] = -sr * gqr - si * gqi + dt * gzr
        gli_ref[...] = -sr * gqi + si * gqr + dt * gzi
        gdt_ref[...] = jnp.sum(lr * gzr + li * gzi, axis=1, keepdims=True) * dt

    return pl.pallas_call(
        body, out_shape=[S(_GP, F32), S(_GP, F32), S((C_GROUPS, 1), F32), S(_RP, F32), S(_RP, F32), S(_RP, F32), S(_RP, F32)],
        compiler_params=pltpu.CompilerParams(vmem_limit_bytes=VMEM_LIMIT), name=name,
    )(d_mb, d_mc, da, lam_re, lam_im, log_dt, bt_re, bt_im)


def _cmul_add(xr, xi, pr, pi, zr, zi):
    return xr + pr * zr - pi * zi, xi + pr * zi + pi * zr


def s5_fwd(name, u, mb, mc, pw, dskip):
    t = u.shape[0]
    tm = _tile(t, S5_ROWS)
    ns = N_STATE

    def body(u_ref, mb_ref, mc_ref, pw_ref, d_ref, gy_ref, y_ref, xs_ref, xb_ref, carry):
        @pl.when(pl.program_id(0) == 0)
        def _():
            carry[...] = jnp.zeros(carry.shape, F32)

        uv = u_ref[...]
        ub = _bf(uv)
        for cs, s_re, s_im in _S5_BLOCKS:
            xs_ref[:, s_re] = jnp.dot(ub[:, cs], mb_ref[cs, s_re], preferred_element_type=F32)
            xs_ref[:, s_im] = jnp.dot(ub[:, cs], mb_ref[cs, s_im], preferred_element_type=F32)

        def group(i, _):
            r0 = pl.multiple_of(i * SUB, SUB)
            xr = xs_ref[pl.ds(r0, SUB), 0:ns]
            xi = xs_ref[pl.ds(r0, SUB), ns:2 * ns]
            for k, s in enumerate((1, 2, 4)):
                xr, xi = _cmul_add(xr, xi, pw_ref[k, 0], pw_ref[k, 1], pltpu.roll(xr, s, 0), pltpu.roll(xi, s, 0))
            xr, xi = _cmul_add(xr, xi, pw_ref[3, 0], pw_ref[3, 1], carry[0], carry[1])
            xs_ref[pl.ds(r0, SUB), 0:ns] = xr
            xs_ref[pl.ds(r0, SUB), ns:2 * ns] = xi
            carry[0] = jnp.broadcast_to(xr[SUB - 1:SUB, :], (SUB, ns))
            carry[1] = jnp.broadcast_to(xi[SUB - 1:SUB, :], (SUB, ns))
            return 0
        lax.fori_loop(0, tm // SUB, group, 0)

        xb_ref[...] = _bf(xs_ref[...])
        for cs, s_re, s_im in _S5_BLOCKS:
            y = (jnp.dot(xb_ref[:, s_re], mc_ref[s_re, cs], preferred_element_type=F32)
                 + jnp.dot(xb_ref[:, s_im], mc_ref[s_im, cs], preferred_element_type=F32) + d_ref[:, cs] * uv[:, cs])
            y_ref[:, cs] = y
            gy_ref[:, cs] = _gelu(y).astype(gy_ref.dtype)

    c = u.shape[1]
    return pl.pallas_call(
        body, grid=(t // tm,),
        in_specs=[pl.BlockSpec((tm, c), lambda i: (i, 0)), pl.BlockSpec(mb.shape, lambda i: (0, 0)),
                  pl.BlockSpec(mc.shape, lambda i: (0, 0)), pl.BlockSpec(pw.shape, lambda i: (0, 0, 0, 0)),
                  pl.BlockSpec((1, c), lambda i: (0, 0))],
        out_specs=[pl.BlockSpec((tm, c), lambda i: (i, 0)), pl.BlockSpec((tm, c), lambda i: (i, 0)),
                   pl.BlockSpec((tm, 2 * ns), lambda i: (i, 0)), pl.BlockSpec((tm, 2 * ns), lambda i: (i, 0))],
        out_shape=[S((t, c), BF16), S((t, c), F32), S((t, 2 * ns), F32), S((t, 2 * ns), BF16)],
        scratch_shapes=[pltpu.VMEM((2, SUB, ns), F32)],
        compiler_params=_cp("arbitrary"), name=name)(u, mb, mc, pw, dskip)


def s5_bwd(name, dgy, y, u, xs, mct, mbt, qw, dskip):
    t, c = u.shape
    tm = _tile(t, S5_ROWS)
    nt = t // tm
    ns = N_STATE
    ng = tm // SUB

    def body(dgy_ref, y_ref, u_ref, xs_ref, mct_ref, mbt_ref, qw_ref, d_ref,
             du_ref, dy_ref, lb_ref, da_ref, dd_ref, lam, carry):
        @pl.when(pl.program_id(0) == 0)
        def _():
            carry[...] = jnp.zeros(carry.shape, F32)
            da_ref[...] = jnp.zeros(da_ref.shape, F32)
            dd_ref[...] = jnp.zeros(dd_ref.shape, F32)

        uv = u_ref[...]
        dy = dgy_ref[...] * _gelu_grad(y_ref[...])
        dyb = _bf(dy)
        dy_ref[...] = dyb
        dd_ref[...] += jnp.sum(dy * uv, axis=0, keepdims=True)
        for cs, s_re, s_im in _S5_BLOCKS:
            lam[:, s_re] = jnp.dot(dyb[:, cs], mct_ref[cs, s_re], preferred_element_type=F32)
            lam[:, s_im] = jnp.dot(dyb[:, cs], mct_ref[cs, s_im], preferred_element_type=F32)
        last_row = lax.broadcasted_iota(jnp.int32, (SUB, ns), 0) == SUB - 1

        def group(j, _):
            i = ng - 1 - j
            r0 = pl.multiple_of(i * SUB, SUB)
            lr = lam[pl.ds(r0, SUB), 0:ns]
            li = lam[pl.ds(r0, SUB), ns:2 * ns]
            for k, s in enumerate((1, 2, 4)):
                lr, li = _cmul_add(lr, li, qw_ref[k, 0], qw_ref[k, 1],
                                   pltpu.roll(lr, SUB - s, 0), pltpu.roll(li, SUB - s, 0))
            cr, ci = carry[0], carry[1]
            lr, li = _cmul_add(lr, li, qw_ref[3, 0], qw_ref[3, 1], cr, ci)
            lam[pl.ds(r0, SUB), 0:ns] = lr
            lam[pl.ds(r0, SUB), ns:2 * ns] = li
            carry[0] = jnp.broadcast_to(lr[0:1, :], (SUB, ns))
            carry[1] = jnp.broadcast_to(li[0:1, :], (SUB, ns))
            nr = jnp.where(last_row, cr, pltpu.roll(lr, SUB - 1, 0))
            ni = jnp.where(last_row, ci, pltpu.roll(li, SUB - 1, 0))
            xr = xs_ref[pl.ds(r0, SUB), 0:ns]
            xi = xs_ref[pl.ds(r0, SUB), ns:2 * ns]
            da_ref[0] += nr * xr + ni * xi
            da_ref[1] += ni * xr - nr * xi
            return 0
        lax.fori_loop(0, ng, group, 0)

        lb_ref[...] = _bf(lam[...])
        for cs, s_re, s_im in _S5_BLOCKS:
            du = (jnp.dot(lb_ref[:, s_re], mbt_ref[s_re, cs], preferred_element_type=F32)
                  + jnp.dot(lb_ref[:, s_im], mbt_ref[s_im, cs], preferred_element_type=F32) + d_ref[:, cs] * dy[:, cs])
            du_ref[:, cs] = du.astype(du_ref.dtype)

    rev = lambda i: (nt - 1 - i, 0)
    return pl.pallas_call(
        body, grid=(nt,),
        in_specs=[pl.BlockSpec((tm, c), rev), pl.BlockSpec((tm, c), rev), pl.BlockSpec((tm, c), rev),
                  pl.BlockSpec((tm, 2 * ns), rev),
                  pl.BlockSpec(mct.shape, lambda i: (0, 0)), pl.BlockSpec(mbt.shape, lambda i: (0, 0)),
                  pl.BlockSpec(qw.shape, lambda i: (0, 0, 0, 0)), pl.BlockSpec((1, c), lambda i: (0, 0))],
        out_specs=[pl.BlockSpec((tm, c), rev), pl.BlockSpec((tm, c), rev), pl.BlockSpec((tm, 2 * ns), rev),
                   pl.BlockSpec((2, SUB, ns), lambda i: (0, 0, 0)), pl.BlockSpec((1, c), lambda i: (0, 0))],
        out_shape=[S((t, c), BF16), S((t, c), BF16), S((t, 2 * ns), BF16), S((2, SUB, ns), F32), S((1, c), F32)],
        scratch_shapes=[pltpu.VMEM((tm, 2 * ns), F32), pltpu.VMEM((2, SUB, ns), F32)],
        compiler_params=_cp("arbitrary"), name=name)(dgy, y, u, xs, mct, mbt, qw, dskip)


def _first(accs, *_):
    return [accs[0]]


def _rms_bwd_epi(accs, xv, base, rv, g):
    dv = accs[0]
    w = dv * g
    xh = xv * rv
    dx = base + rv * (w - xh * jnp.mean(w * xh, axis=-1, keepdims=True))
    return [dx, dx, jnp.sum(dv * xh, axis=0, keepdims=True)]


def mm_rms_bwd(name, pairs, x, r, gain, dres):
    t, d = x.shape
    return mm_nn(name, t, d, pairs, 1, _rms_bwd_epi, [F32, BF16], tiled=[x, dres], cols=[r], rowv=[gain], sums=[(1, d)])


def _add_res(accs, res):
    return [accs[0] + res]


def even_fwd(x, w, need_out):
    t = x.shape[0]
    proj, hn, r = mm_nn("e_in_f", t, IN_WIDTH, [(x, w["e_w_in_t"], 0, "t")], 1, _first, [F32], norm_gain=w["e_norm"])
    out_a = gmlp_fwd("e_gmlp_f", proj, w["e_gmlp_w"], w["e_gmlp_b"])
    hc = conv_fwd("e_conv_f", proj, w["e_conv_w"], w["e_conv_b"])
    out_b = ln_silu_fwd("e_ln_f", hc, w["e_conv_ln_g"], w["e_conv_ln_b"])
    need_out(out_b)
    (x1,) = mm_nn("e_out_f", t, D_MODEL, [(out_a, (w["e_w_out"], 0), 0), (out_b, (w["e_w_out"], 1), 0)],
                  1, _add_res, [F32], tiled=[x])
    return x1, (x, hn, r, proj, out_a, hc, out_b)


def even_bwd_mixers(dxb, saved, w):
    x, hn, r, proj, out_a, hc, out_b = saved
    t = x.shape[0]
    (dcat,) = mm_nn("e_out_b", t, D_MODEL, [(dxb, w["e_w_out"], 0, "t")], 1, _first, [F32])
    g_w_out = jnp.concatenate([mm_tn("e_out_wa", out_a, dxb), mm_tn("e_out_wb", out_b, dxb)], axis=0)
    dab, g_gw, g_gb = gmlp_bwd("e_gmlp_b", proj, dcat, w["e_gmlp_w"], w["e_gmlp_b"])
    dhc, g_lg, g_lb = ln_silu_bwd("e_ln_b", hc, dcat, w["e_conv_ln_g"], w["e_conv_ln_b"])
    dba, dbg, g_cw, g_cb = conv_bwd("e_conv_b", proj, dhc, w["e_conv_w"])
    g_w_in_t = jnp.concatenate([mm_tn("e_in_w0", dab, hn), mm_tn("e_in_w1", dba, hn), mm_tn("e_in_w2", dbg, hn)], axis=0)
    grads = dict(e_w_in_t=g_w_in_t, e_gmlp_w=g_gw[None], e_gmlp_b=g_gb.reshape(1, A_GROUPS, GMLP_BLOCK),
                 e_conv_w=g_cw[None], e_conv_b=g_cb, e_conv_ln_g=g_lg, e_conv_ln_b=g_lb, e_w_out=g_w_out)
    return (dab, dba, dbg), grads


def even_bwd_input(dx, dproj, saved, w):
    x, _, r = saved[:3]
    dab, dba, dbg = dproj
    w_in_t = w["e_w_in_t"]
    return mm_rms_bwd("e_in_b", [(dab, (w_in_t, 0), 0), (dba, (w_in_t, 2), 0), (dbg, (w_in_t, 3), 0)], x, r, w["e_norm"], dx)


def s5_setup(w, anchor=None):
    def rows(v):
        return v.transpose(0, 2, 1).reshape(_RP)

    log_dt = w["o_log_dt"].reshape(C_GROUPS, 1)
    if anchor is not None:
        log_dt = log_dt + anchor
    lam = (w["o_lam_re"], w["o_lam_im"], log_dt, rows(w["o_b_re"]), rows(w["o_b_im"]))
    a, bbr, bbi = s5_discretise("o_s5_zoh", *lam)
    c_re, c_im = w["o_c_re"], w["o_c_im"]
    pw, qw, mb, mc, mct = s5_operands("o_s5_ops", a.reshape(2, N_STATE), bbr, bbi, c_re.reshape(_RP), c_im.reshape(_RP),
                                      c_re.transpose(2, 0, 1).reshape(C_STATE, C_WIDTH),
                                      c_im.transpose(2, 0, 1).reshape(C_STATE, C_WIDTH))
    return dict(lam=lam, pw=pw, qw=qw, mb=mb, mc=mc, mct=mct, mbt=mb.T)


def odd_fwd(x, w, consts):
    t = x.shape[0]
    u, hn, r = mm_nn("o_in_f", t, C_WIDTH, [(x, w["o_w_in"], 0)], 1, _first, [F32], norm_gain=w["o_norm"])
    gy, y, xs, xsb = s5_fwd("o_s5_f", u, consts["mb"], consts["mc"], consts["pw"], w["o_d"])
    w_out_t = w["o_w_out_t"]

    def epi(accs, res):
        return [res + accs[0] * _sigmoid(accs[1]), accs[0], accs[1]]

    x1, o1, o2 = mm_nn("o_out_f", t, D_MODEL, [(gy, (w_out_t, 0), 0, "t"), (gy, (w_out_t, D_MODEL), 1, "t")], 2, epi,
                       [F32, BF16, BF16], tiled=[x])
    return x1, (x, hn, r, u, gy, y, xs, xsb, o1, o2)


def odd_bwd(dx, dxb, saved, w, consts):
    x, hn, r, u, gy, y, xs, xsb, o1, o2 = saved
    t = x.shape[0]

    def gate_bwd(dv, a, b):
        a = a.astype(F32)
        sg = _sigmoid(b.astype(F32))
        return [jnp.concatenate([dv * sg, dv * a * sg * (1.0 - sg)], axis=1)], []

    (do12,) = rows_call("o_gate_b", gate_bwd, [dx, o1, o2], [], [(2 * D_MODEL, BF16)], [])
    (dgy,) = mm_nn("o_out_b", t, C_WIDTH, [(do12, w["o_w_out_t"], 0)], 1, _first, [F32])
    g_w_out_t = mm_tn("o_out_w", do12, gy)
    du, dyb, lamb, da8, g_d = s5_bwd("o_s5_b", dgy, y, u, xs, consts["mct"], consts["mbt"], consts["qw"], w["o_d"])
    d_mb, d_mc = s5_block_grads("o_s5_w", u, lamb, xsb, dyb)
    da = jnp.sum(da8, axis=1).reshape((2,) + _GP)
    g_lr, g_li, g_dt, g_btr, g_bti, g_cr, g_ci = s5_param_grads("o_s5_pg", d_mb, d_mc, da, *consts["lam"])

    def states_first(v):
        return v.reshape(C_GROUPS, C_GROUP_CH, C_STATE).transpose(0, 2, 1)[None]

    g_w_in = mm_tn("o_in_w", hn, du)
    dx0, dx0b, g_norm = mm_rms_bwd("o_in_b", [(du, w["o_w_in"], 0, "t")], x, r, w["o_norm"], dx)
    grads = dict(o_norm=g_norm, o_w_in=g_w_in, o_lam_re=g_lr[None], o_lam_im=g_li[None], o_log_dt=g_dt.reshape(1, C_GROUPS),
                 o_b_re=states_first(g_btr), o_b_im=states_first(g_bti),
                 o_c_re=g_cr.reshape((1, C_GROUPS, C_GROUP_CH, C_STATE)), o_c_im=g_ci.reshape((1, C_GROUPS, C_GROUP_CH, C_STATE)),
                 o_d=g_d, o_w_out_t=g_w_out_t)
    return dx0, dx0b, grads


def ca_fwd(i, x, mem, w):
    t, m = x.shape[0], mem.shape[0]
    k, v, mn, rm = mm_nn(f"ca{i}_kv_f", m, D_MODEL, [(mem, w["ca_wk"][i], 0), (mem, w["ca_wv"][i], 1)], 2,
                         lambda accs: [accs[0], accs[1]], [BF16, BF16], norm_gain=w["ca_mem_norm"][i:i + 1])
    x1, xn, r, q, o = attn_fwd(f"ca{i}_attn_f", x, w["ca_norm"][i:i + 1], w["ca_wq"][i], k, v, w["ca_wo"][i])
    return x1, (x, xn, r, mn, rm, q, k, v, o)


def ca_bwd(i, dx, dxb, saved, mem, w):
    x, xn, r, mn, rm, q, k, v, o = saved
    t, m = x.shape[0], mem.shape[0]
    g_wo = mm_tn(f"ca{i}_o_w", o, dxb)
    dx0, dx0b, dq, dk, dv, g_norm = attn_bwd(f"ca{i}_attn_b", dx, dxb, x, r, w["ca_norm"][i:i + 1], q, k, v,
                                             w["ca_wq"][i], w["ca_wo"][i])
    g_wq = mm_tn(f"ca{i}_q_w", xn, dq)
    g_wk = mm_tn(f"ca{i}_k_w", mn, dk)
    g_wv = mm_tn(f"ca{i}_v_w", mn, dv)
    (dmn,) = mm_nn(f"ca{i}_kv_b", m, D_MODEL, [(dk, w["ca_wk"][i], 0, "t"), (dv, w["ca_wv"][i], 0, "t")], 1, _first, [F32])
    g_mnorm = rms_bwd_gain_only(f"ca{i}_mnorm_b", dmn, mem, rm)
    return dx0, dx0b, dict(ca_norm=g_norm, ca_mem_norm=g_mnorm, ca_wq=g_wq, ca_wk=g_wk, ca_wv=g_wv, ca_wo=g_wo)


FFN_ROWS = 256
FFN_CHUNK = 256


def _whole(a):
    return pl.BlockSpec(a.shape, lambda i: (0,) * a.ndim, pipeline_mode=pl.Buffered(1))


def ffn_fused_fwd(name, x, gain, wg_t, wu_t, wd, target=None, final_gain=None):
    t, d = x.shape
    hid = wd.shape[0]
    tm = _tile(t, FFN_ROWS)
    last = target is not None
    n_main = 4 if last else 1

    def body(*refs):
        x_ref, g_ref, wg_ref, wu_ref, wd_ref = refs[:5]
        rest = refs[5:]
        if last:
            tgt_ref, fg_ref = rest[:2]
            rest = rest[2:]
        main, (xn_ref, r_ref, dgate_ref, dup_ref, h_ref) = rest[:n_main], rest[n_main:]
        xv = x_ref[...]
        rv = lax.rsqrt(jnp.mean(xv * xv, axis=-1, keepdims=True) + EPS)
        xn = (xv * rv * g_ref[...]).astype(BF16)
        xn_ref[...] = xn
        r_ref[...] = rv
        for j in range(hid // FFN_CHUNK):
            cs = slice(j * FFN_CHUNK, (j + 1) * FFN_CHUNK)
            g = lax.dot_general(xn, wg_ref[cs, :], _NT, preferred_element_type=F32)
            u = lax.dot_general(xn, wu_ref[cs, :], _NT, preferred_element_type=F32)
            s = _sigmoid(g)
            silu = g * s
            dgate_ref[:, cs] = (u * (s + silu * (1.0 - s))).astype(BF16)
            dup_ref[:, cs] = silu.astype(BF16)
            h_ref[:, cs] = (silu * u).astype(BF16)
        acc = jnp.dot(h_ref[...], wd_ref[...], preferred_element_type=F32)
        if not last:
            main[0][...] = xv + acc
        else:
            dx, _, dgain, part = _final_loss_epi([acc], xv, tgt_ref[...], fg_ref[...])

            @pl.when(pl.program_id(0) == 0)
            def _():
                main[2][...] = jnp.zeros(main[2].shape, F32)
                main[3][...] = jnp.zeros(main[3].shape, F32)
            main[0][...] = dx
            main[1][...] = dx.astype(BF16)
            main[2][...] += dgain
            main[3][...] += part

    rows = pl.BlockSpec((tm, d), lambda i: (i, 0))
    wide = pl.BlockSpec((tm, hid), lambda i: (i, 0))
    col = pl.BlockSpec((tm, 1), lambda i: (i, 0))
    ins, in_specs = [x, gain, wg_t, wu_t, wd], [rows, _whole(gain), _whole(wg_t), _whole(wu_t), _whole(wd)]
    if last:
        ins += [target, final_gain]
        in_specs += [rows, _whole(final_gain)]
        out_specs = [rows, rows, pl.BlockSpec((1, d), lambda i: (0, 0)), pl.BlockSpec((1, 1), lambda i: (0, 0))]
        out_shape = [S((t, d), F32), S((t, d), BF16), S((1, d), F32), S((1, 1), F32)]
    else:
        out_specs, out_shape = [rows], [S((t, d), F32)]
    out_specs += [rows, col, wide, wide, wide]
    out_shape += [S((t, d), BF16), S((t, 1), F32)] + [S((t, hid), BF16)] * 3
    outs = pl.pallas_call(body, grid=(t // tm,), in_specs=in_specs, out_specs=out_specs, out_shape=out_shape,
                          compiler_params=_cp("arbitrary" if last else "parallel"), name=name)(*ins)
    return (tuple(outs[:4]) if last else outs[0]), outs[n_main:]


def ffn_fused_bwd(name, dx, dxb, x, r, gain, dgate, dup, wg_t, wu_t, wd):
    t, d = x.shape
    hid = wd.shape[0]
    tm = _tile(t, FFN_ROWS)

    def body(dx_ref, dxb_ref, x_ref, r_ref, g_ref, dgate_ref, dup_ref, wg_ref, wu_ref, wd_ref,
             dxo_ref, dxbo_ref, dg_ref, du_ref, dgain_ref):
        @pl.when(pl.program_id(0) == 0)
        def _():
            dgain_ref[...] = jnp.zeros(dgain_ref.shape, F32)

        dxb = dxb_ref[...]
        for j in range(hid // FFN_CHUNK):
            cs = slice(j * FFN_CHUNK, (j + 1) * FFN_CHUNK)
            dh = lax.dot_general(dxb, wd_ref[cs, :], _NT, preferred_element_type=F32)
            dg_ref[:, cs] = (dh * dgate_ref[:, cs].astype(F32)).astype(BF16)
            du_ref[:, cs] = (dh * dup_ref[:, cs].astype(F32)).astype(BF16)
        dxn = (jnp.dot(dg_ref[...], wg_ref[...], preferred_element_type=F32)
               + jnp.dot(du_ref[...], wu_ref[...], preferred_element_type=F32))
        dxo, _, dgain = _rms_bwd_epi([dxn], x_ref[...], dx_ref[...], r_ref[...], g_ref[...])
        dxo_ref[...] = dxo
        dxbo_ref[...] = dxo.astype(BF16)
        dgain_ref[...] += dgain

    rows = pl.BlockSpec((tm, d), lambda i: (i, 0))
    wide = pl.BlockSpec((tm, hid), lambda i: (i, 0))
    col = pl.BlockSpec((tm, 1), lambda i: (i, 0))
    return pl.pallas_call(
        body, grid=(t // tm,),
        in_specs=[rows, rows, rows, col, _whole(gain), wide, wide, _whole(wg_t), _whole(wu_t), _whole(wd)],
        out_specs=[rows, rows, wide, wide, pl.BlockSpec((1, d), lambda i: (0, 0))],
        out_shape=[S((t, d), F32), S((t, d), BF16), S((t, hid), BF16), S((t, hid), BF16), S((1, d), F32)],
        compiler_params=_cp("arbitrary"), name=name)(dx, dxb, x, r, gain, dgate, dup, wg_t, wu_t, wd)


def ffn_fwd(i, x, w, target=None):
    out, (xn, r, dgate, dup, h) = ffn_fused_fwd(f"ffn{i}_f", x, w["ffn_norm"][i:i + 1], w["ffn_w_gate_t"][i],
                                                w["ffn_w_up_t"][i], w["ffn_w_down"][i], target,
                                                None if target is None else w["final_norm"])
    return out, (x, xn, r, dgate, dup, h)


def ffn_bwd(i, dx, dxb, saved, w):
    x, xn, r, dgate, dup, h = saved
    g_wd = mm_tn(f"ffn{i}_down_w", h, dxb)
    dx0, dx0b, dg, du, g_norm = ffn_fused_bwd(f"ffn{i}_b", dx, dxb, x, r, w["ffn_norm"][i:i + 1], dgate, dup,
                                              w["ffn_w_gate_t"][i], w["ffn_w_up_t"][i], w["ffn_w_down"][i])
    g_wg_t = mm_tn(f"ffn{i}_gate_w", dg, xn)
    g_wu_t = mm_tn(f"ffn{i}_up_w", du, xn)
    return dx0, dx0b, dict(ffn_norm=g_norm, ffn_w_gate_t=g_wg_t, ffn_w_up_t=g_wu_t, ffn_w_down=g_wd)


def local_step(x, mem, target, w, fetch=None, on_grads=None, anchor=None):
    consts = s5_setup(w, anchor)

    def need(stage, after):
        if fetch is not None:
            for k, v in fetch(stage, after).items():
                if isinstance(k, tuple):
                    w.setdefault(k[0], {})[k[1]] = v
                else:
                    w[k] = v

    need(0, consts["pw"])
    x1, s_e = even_fwd(x, w, lambda after: need(1, after))
    x2, s_c0 = ca_fwd(0, x1, mem, w)
    need(2, x2)
    x3, s_f0 = ffn_fwd(0, x2, w)
    x4, s_o = odd_fwd(x3, w, consts)
    need(3, x4)
    x5, s_c1 = ca_fwd(1, x4, mem, w)
    (dx, dxb, g_final, loss), s_f1 = ffn_fwd(1, x5, w, target)

    def emit(stage, carry, plain, layered=None, layer=0):
        if on_grads is None:
            return carry
        out = dict(plain)
        out.update({(k, layer): v for k, v in (layered or {}).items()})
        return on_grads(stage, out, list(carry))

    dx, dxb, g_f1 = ffn_bwd(1, dx, dxb, s_f1, w)
    dx, dxb = emit(0, (dx, dxb), {}, g_f1, 1)
    dx, dxb, g_c1 = ca_bwd(1, dx, dxb, s_c1, mem, w)
    dx, dxb, g_o = odd_bwd(dx, dxb, s_o, w, consts)
    dx, dxb = emit(1, (dx, dxb), g_o, g_c1, 1)
    dx, dxb, g_f0 = ffn_bwd(0, dx, dxb, s_f0, w)
    dx, dxb = emit(2, (dx, dxb), {}, g_f0, 0)
    dx, dxb, g_c0 = ca_bwd(0, dx, dxb, s_c0, mem, w)
    dx, dxb = emit(3, (dx, dxb), {}, g_c0, 0)
    dproj, g_e = even_bwd_mixers(dxb, s_e, w)
    dproj = emit(4, dproj, {**g_e, "o_norm": g_o["o_norm"], "o_d": g_o["o_d"]})
    dx, dxb, g_e["e_norm"] = even_bwd_input(dx, dproj, s_e, w)

    grads = dict(g_e)
    grads.update(g_o)
    for g0, g1 in ((g_c0, g_c1), (g_f0, g_f1)):
        for k in g0:
            grads[k] = jnp.concatenate([g0[k], g1[k]], axis=0) if k.endswith("norm") else (g0[k], g1[k])
    grads["final_norm"] = g_final
    return loss, dx, grads


def _group(axes):
    pos = {a: lax.axis_index(a) for a in ("x", "y", "c")}
    me = 0
    for a in axes:
        me = me * 2 + pos[a]
    peers = []
    for mask in range(1, 2 ** len(axes)):
        peer = dict(pos)
        for bit, a in enumerate(axes):
            if (mask >> (len(axes) - 1 - bit)) & 1:
                peer[a] = 1 - pos[a]
        idx = 0
        for a in axes:
            idx = idx * 2 + peer[a]
        peers.append((idx, (peer["x"], peer["y"], peer["c"])))
    return me, peers


def _sibling():
    x, y, c = lax.axis_index("x"), lax.axis_index("y"), lax.axis_index("c")
    return c, (x, y, 1 - c)


_HBM =pl.BlockSpec(memory_space=pltpu.HBM)
_SEM = pl.BlockSpec(memory_space=pltpu.SEMAPHORE)
_EFFECT = pltpu.SideEffectType.DATAFLOW_SIDE_EFFECTING


def _gather_peers(direct):
    chip, _ = _group(("x", "y"))
    core = lax.axis_index("c")
    if direct:
        _, peers = _group(_ALL)
        return chip, core, [(idx // 2, idx % 2, dev) for idx, dev in peers]
    _, peers = _group(("x", "y"))
    return chip, core, [(idx, core, dev) for idx, dev in peers]


def gather_ici_start(name, groups, direct):
    flat = [b for g in groups for b in g]
    sizes = [len(g) for g in groups]
    k_ops, n_g = len(flat), len(groups)
    lands = [lax.empty((4, 2) + tuple(b.shape), b.dtype) for b in flat]
    fan = [N_DEV - 1 if d else 3 for d in direct]

    def body(*refs):
        src, land = refs[:k_ops], refs[k_ops:2 * k_ops]
        sems = refs[2 * k_ops:2 * k_ops + 3 * n_g]
        token = refs[-1]
        i = 0
        for g in range(n_g):
            send, recv, loc = sems[3 * g:3 * g + 3]
            chip, core, peers = _gather_peers(direct[g])
            for j in range(sizes[g]):
                pltpu.make_async_copy(src[i], land[i].at[chip, core], loc.at[j]).start()
                for k, (_, _, dev) in enumerate(peers):
                    s = fan[g] * j + k
                    pltpu.make_async_remote_copy(src_ref=src[i], dst_ref=land[i].at[chip, core], send_sem=send.at[s],
                                                 recv_sem=recv.at[s], device_id=dev, device_id_type=MESH).start()
                i += 1
        token[...] = jnp.zeros(token.shape, token.dtype)

    sem_shapes = []
    for s, f in zip(sizes, fan):
        sem_shapes += [pltpu.SemaphoreType.DMA((f * s,)), pltpu.SemaphoreType.DMA((f * s,)), pltpu.SemaphoreType.DMA((s,))]
    thru = [pltpu.HBM(a.shape, a.dtype) for a in flat + lands]
    outs = pl.pallas_call(
        body, name=name, out_shape=tuple(sem_shapes) + tuple(thru) + (S((8, LANES), F32),),
        in_specs=[_HBM] * (2 * k_ops), out_specs=[_SEM] * (3 * n_g) + [_HBM] * (2 * k_ops) + [pl.BlockSpec(memory_space=pltpu.VMEM)],
        input_output_aliases={i: 3 * n_g + i for i in range(2 * k_ops)},
        compiler_params=pltpu.CompilerParams(has_side_effects=_EFFECT),
    )(*[pltpu.with_memory_space_constraint(a, pltpu.HBM) for a in flat + lands])
    sems = [tuple(outs[3 * g:3 * g + 3]) for g in range(n_g)]
    srcs_thru, lands_thru, off = [], [], 3 * n_g
    for s in sizes:
        srcs_thru.append(list(outs[off:off + s]))
        off += s
    for s in sizes:
        lands_thru.append(list(outs[off:off + s]))
        off += s
    return sems, srcs_thru, lands_thru, outs[-1]


def gather_ici_wait(name, srcs, lands, sems, after, direct=False):
    n = len(srcs)

    def body(*refs):
        src, land = refs[:n], refs[n:2 * n]
        send, recv, loc = refs[2 * n:2 * n + 3]
        chip, core, peers = _gather_peers(direct)
        for j in range(n):
            for k, (pchip, pcore, dev) in enumerate(peers):
                s = len(peers) * j + k
                cp = pltpu.make_async_remote_copy(src_ref=src[j], dst_ref=land[j].at[pchip, pcore], send_sem=send.at[s],
                                                  recv_sem=recv.at[s], device_id=dev, device_id_type=MESH)
                cp.wait_send()
                cp.wait_recv()
            pltpu.make_async_copy(src[j], land[j].at[chip, core], loc.at[j]).wait()

    outs = pl.pallas_call(
        body, name=name, out_shape=tuple(pltpu.HBM(a.shape, a.dtype) for a in list(srcs) + list(lands)),
        in_specs=[_HBM] * (2 * n) + [_SEM] * 3 + [ANY], out_specs=[_HBM] * (2 * n),
        input_output_aliases={i: i for i in range(2 * n)},
        compiler_params=pltpu.CompilerParams(has_side_effects=_EFFECT),
    )(*srcs, *lands, *sems, after)
    return list(outs[n:])


def gather_d2d(name, bufs):
    k_ops = len(bufs)

    def body(*refs):
        in_refs, out_refs = refs[:k_ops], refs[k_ops:2 * k_ops]
        send_sems, recv_sems = refs[2 * k_ops:]
        core, sib = _sibling()
        sent, landed = [], []
        for i in range(k_ops):
            cp = pltpu.make_async_remote_copy(src_ref=in_refs[i].at[:, core], dst_ref=out_refs[i].at[:, core],
                                              send_sem=send_sems.at[i], recv_sem=recv_sems.at[i], device_id=sib, device_id_type=MESH)
            cp.start()
            sent.append(cp)
            landed.append(pltpu.make_async_remote_copy(src_ref=in_refs[i].at[:, core], dst_ref=out_refs[i].at[:, 1 - core],
                                                       send_sem=send_sems.at[i], recv_sem=recv_sems.at[i],
                                                       device_id=sib, device_id_type=MESH))
        for cp in landed:
            cp.wait_recv()
        for cp in sent:
            cp.wait_send()

    return pl.pallas_call(
        body, in_specs=[ANY] * k_ops, out_specs=[ANY] * k_ops, out_shape=[S(b.shape, b.dtype) for b in bufs],
        input_output_aliases={i: i for i in range(k_ops)},
        scratch_shapes=[pltpu.SemaphoreType.DMA((k_ops,)), pltpu.SemaphoreType.DMA((k_ops,))],
        name=name)(*bufs)


_ALL = ("x", "y", "c")


def scatter_start(name, arr, carry):
    land = lax.empty(arr.shape, arr.dtype)
    n_c = len(carry)

    def body(*refs):
        in_ref, land_ref = refs[0], refs[1]
        send, recv = refs[2 + n_c], refs[3 + n_c]
        me, peers = _group(_ALL)
        for k, (idx, dev) in enumerate(peers):
            pltpu.make_async_remote_copy(src_ref=in_ref.at[idx], dst_ref=land_ref.at[me], send_sem=send.at[k], recv_sem=recv.at[k],
                                         device_id=dev, device_id_type=MESH).start()

    thru = [arr, land] + list(carry)
    outs = pl.pallas_call(
        body, name=name,
        out_shape=(pltpu.SemaphoreType.DMA((N_DEV - 1,)), pltpu.SemaphoreType.DMA((N_DEV - 1,)))
        + tuple(pltpu.HBM(a.shape, a.dtype) for a in thru),
        in_specs=[_HBM] * len(thru), out_specs=[_SEM, _SEM] + [_HBM] * len(thru),
        input_output_aliases={i: 2 + i for i in range(len(thru))},
        compiler_params=pltpu.CompilerParams(has_side_effects=_EFFECT),
    )(*[pltpu.with_memory_space_constraint(a, pltpu.HBM) for a in thru])
    return (outs[0], outs[1]), outs[2], outs[3], list(outs[4:])


def scatter_wait(name, arr, land, sems, after):
    def body(in_ref, land_ref, send, recv, after_ref, in_thru, land_thru):
        _, peers = _group(_ALL)
        for k, (idx, dev) in enumerate(peers):
            cp = pltpu.make_async_remote_copy(src_ref=in_ref.at[idx], dst_ref=land_ref.at[idx], send_sem=send.at[k],
                                              recv_sem=recv.at[k], device_id=dev, device_id_type=MESH)
            cp.wait_send()
            cp.wait_recv()

    outs = pl.pallas_call(
        body, name=name, out_shape=(pltpu.HBM(arr.shape, arr.dtype), pltpu.HBM(arr.shape, arr.dtype)),
        in_specs=[_HBM, _HBM, _SEM, _SEM, ANY], out_specs=[_HBM, _HBM], input_output_aliases={0: 0, 1: 1},
        compiler_params=pltpu.CompilerParams(has_side_effects=_EFFECT),
    )(arr, land, sems[0], sems[1], after)
    return outs[0], outs[1]


def _row_tile(rows, cap=512):
    return next(t for t in range(cap - cap % 16, 0, -16) if rows % t == 0)


def sum_shares(name, own, recv, me):
    n, rows, c = recv.shape
    tr = _row_tile(rows)

    def body(me_ref, *refs):
        acc = refs[0][...].astype(F32)
        for r in refs[1:n]:
            acc = acc + r[...].astype(F32)
        refs[n][...] = acc

    def slot(mask):
        return pl.BlockSpec((None, tr, c), lambda i, me, mask=mask: (jnp.bitwise_xor(me[0], mask), i, 0))

    spec = pltpu.PrefetchScalarGridSpec(
        num_scalar_prefetch=1, grid=(rows // tr,), in_specs=[slot(k) for k in range(n)],
        out_specs=pl.BlockSpec((tr, c), lambda i, me: (i, 0)))
    return pl.pallas_call(body, grid_spec=spec, out_shape=S((rows, c), F32),
                          compiler_params=_cp("parallel"), name=name)(me, own, *([recv] * (n - 1)))


def sum_slots(name, slots):
    n, r, c = slots.shape

    def body(s_ref, o_ref):
        acc = s_ref[0]
        for j in range(1, n):
            acc = acc + s_ref[j]
        o_ref[...] = acc

    return pl.pallas_call(body, out_shape=S((r, c), F32), compiler_params=pltpu.CompilerParams(vmem_limit_bytes=VMEM_LIMIT),
                          name=name)(slots)


def adamw_units(name, pieces, transposed, w, m, v):
    n_l, k, n = w.shape
    tk = _tile(k, 512) if transposed else k
    c1 = 1.0 - ADAM_B1 ** ADAM_STEP
    c2 = 1.0 - ADAM_B2 ** ADAM_STEP

    def body(*refs):
        p_refs, (w_ref, m_ref, v_ref, g_ref, d_ref, m2_ref, v2_ref) = refs[:n_l], refs[n_l:]
        gv = p_refs[0][...]
        for j in range(1, n_l):
            gv = jnp.where(pl.program_id(0) == j, p_refs[j][...], gv)
        if transposed:
            gv = gv.T
        m2 = ADAM_B1 * m_ref[...] + (1.0 - ADAM_B1) * gv
        v2 = ADAM_B2 * v_ref[...] + (1.0 - ADAM_B2) * (gv * gv)
        g_ref[...] = gv
        m2_ref[...] = m2
        v2_ref[...] = v2
        d_ref[...] = -ADAM_LR * ((m2 / c1) / (jnp.sqrt(v2 / c2) + ADAM_EPS) + ADAM_WD * w_ref[...])

    piece = pl.BlockSpec((n, tk), lambda l, i: (0, i)) if transposed else pl.BlockSpec((k, n), lambda l, i: (0, 0))
    blk = pl.BlockSpec((None, tk, n), lambda l, i: (l, i, 0))
    return tuple(pl.pallas_call(body, grid=(n_l, k // tk), in_specs=[piece] * n_l + [blk] * 3, out_specs=[blk] * 4,
                                out_shape=[S(w.shape, F32)] * 4, compiler_params=_cp("parallel", "parallel"),
                                name=name)(*pieces, w, m, v))


def adamw_native(name, g, w, m, v, tr=512):
    shape = w.shape
    cols = shape[-1]
    rows = w.size // cols
    tr = _tile(rows, tr) if rows % 8 == 0 else rows
    c1 = 1.0 - ADAM_B1 ** ADAM_STEP
    c2 = 1.0 - ADAM_B2 ** ADAM_STEP

    def body(g_ref, w_ref, m_ref, v_ref, d_ref, m2_ref, v2_ref):
        gv = g_ref[...]
        m2 = ADAM_B1 * m_ref[...] + (1.0 - ADAM_B1) * gv
        v2 = ADAM_B2 * v_ref[...] + (1.0 - ADAM_B2) * (gv * gv)
        m2_ref[...] = m2
        v2_ref[...] = v2
        d_ref[...] = -ADAM_LR * ((m2 / c1) / (jnp.sqrt(v2 / c2) + ADAM_EPS) + ADAM_WD * w_ref[...])

    row = pl.BlockSpec((tr, cols), lambda i: (i, 0))
    outs = pl.pallas_call(body, grid=(rows // tr,), in_specs=[row] * 4, out_specs=[row] * 3,
                          out_shape=[S((rows, cols), F32)] * 3, compiler_params=_cp("parallel"),
                          name=name)(*[a.reshape(rows, cols) for a in (g, w, m, v)])
    return tuple(o.reshape(shape) for o in outs)


_REPLICATED = ("e_norm", "e_gmlp_w", "e_gmlp_b", "e_conv_b", "e_conv_ln_g", "e_conv_ln_b", "o_lam_re", "o_lam_im", "o_log_dt",
               "o_b_re", "o_b_im", "o_c_re", "o_c_im", "ca_norm", "ca_mem_norm", "ffn_norm", "final_norm")
_ORDER = ("e_norm", "e_w_in", "e_gmlp_w", "e_gmlp_b", "e_conv_w", "e_conv_b", "e_conv_ln_g", "e_conv_ln_b", "e_w_out",
          "o_norm", "o_w_in", "o_lam_re", "o_lam_im", "o_log_dt", "o_b_re", "o_b_im", "o_c_re", "o_c_im", "o_d", "o_w_out",
          "ca_norm", "ca_mem_norm", "ca_wq", "ca_wk", "ca_wv", "ca_wo", "ffn_norm", "ffn_w_gate", "ffn_w_up", "ffn_w_down",
          "final_norm")


def _rows128(a, multiple=8):
    flat = a.reshape(-1)
    rows = -(-flat.shape[0] // (LANES * multiple)) * multiple
    return jnp.pad(flat, (0, rows * LANES - flat.shape[0])).reshape(rows, LANES)


def _shard(full, axis):
    s = full.shape
    return jnp.moveaxis(full.reshape(s[:axis] + (N_DEV, s[axis] // N_DEV) + s[axis + 1:]), axis, 0)


_UNITS = (("e_w_in", 0, True), ("e_w_out", 0, False), ("o_w_in", 0, False), ("o_w_out", 0, True),
          *[(n, i, False) for n in ("ca_wq", "ca_wk", "ca_wv", "ca_wo") for i in (0, 1)],
          *[(n, i, tr) for n, tr in (("ffn_w_gate", True), ("ffn_w_up", True), ("ffn_w_down", False)) for i in (0, 1)])
_LAYERED = ("ca_wq", "ca_wk", "ca_wv", "ca_wo", "ffn_w_gate", "ffn_w_up", "ffn_w_down")
_SMALL_SHARDED = (("e_conv_w", 2), ("o_norm", 1), ("o_d", 1))
RS_ROW = 1024


def _unit_key(name, tr):
    return name + "_t" if tr else name


def _stage_of(name, layer):
    if name.startswith("e_"):
        return 0 if name == "e_w_in" else 1
    if name.startswith("o_"):
        return 2
    if name.startswith("ca_"):
        return 1 if layer == 0 else 3
    return 2 if layer == 0 else 3


def weight_fetcher(local):
    groups, meta = [[] for _ in range(4)], [[] for _ in range(4)]
    for name, layer, tr in _UNITS:
        blk = local[name][layer]
        st = _stage_of(name, layer)
        groups[st].append(_bf(blk.T if tr else blk))
        meta[st].append((name, layer, tr))
    small = jnp.concatenate([local[name].reshape(-1) for name, _ in _SMALL_SHARDED])
    groups[0].append(_rows128(small))
    direct = [False, False, False, True]
    sems, srcs, lands, token = gather_ici_start("ag_w_start", groups, direct)

    def fetch(stage, after):
        bufs = gather_ici_wait(f"ag_w_wait{stage}", srcs[stage], lands[stage], sems[stage], after, direct[stage])
        if not direct[stage]:
            bufs = gather_d2d(f"ag_w_d2d{stage}", bufs)
        got = {}
        for (name, layer, tr), blk, buf in zip(meta[stage], groups[stage], bufs):
            arr = buf.reshape((N_DEV * blk.shape[0],) + tuple(blk.shape[1:]))
            if name in _LAYERED:
                got[(_unit_key(name, tr), layer)] = arr
            else:
                got[_unit_key(name, tr)] = arr
        if stage == 0:
            flat = bufs[-1].reshape(N_DEV, -1)
            off = 0
            for name, axis in _SMALL_SHARDED:
                blk = local[name]
                seg = flat[:, off:off + blk.size].reshape((N_DEV,) + blk.shape)
                off += blk.size
                seg = jnp.moveaxis(seg, 0, axis)
                got[name] = seg.reshape(seg.shape[:axis] + (-1,) + seg.shape[axis + 2:])
            got["e_conv_w"] = got["e_conv_w"][0]
        return got

    return fetch, token


def _grad_stage_of(name, layer):
    if name.startswith("e_"):
        return 4
    if name.startswith("o_"):
        return 1
    if name.startswith("ca_"):
        return 3 if layer == 0 else 1
    return 2 if layer == 0 else 0


GRAD_STAGES = 5
SMALL_ROWS = 16


def gradient_reducer(local, mom, var):
    me = (4 * lax.axis_index("x") + 2 * lax.axis_index("y") + lax.axis_index("c")).astype(jnp.int32).reshape(1)
    pending = []

    def start(stage, grads, carry):
        units = [u for u in _UNITS if _grad_stage_of(u[0], u[1]) == stage]
        parts, spans = [], []
        for name, layer, tr in units:
            key = _unit_key(name, tr)
            g = grads[(key, layer)] if name in _LAYERED else grads[key]
            part = g.reshape(4, 2, -1, RS_ROW)
            spans.append((part.shape[2], g.shape[0] // N_DEV, g.shape[1]))
            parts.append(part)
        if stage == GRAD_STAGES - 1:
            small = jnp.concatenate([_shard(grads[name], axis).reshape(N_DEV, -1) for name, axis in _SMALL_SHARDED], axis=1)
            small = jnp.pad(small, ((0, 0), (0, SMALL_ROWS * RS_ROW - small.shape[1])))
            parts.append(small.astype(BF16).reshape(4, 2, SMALL_ROWS, RS_ROW))
        pack = jnp.concatenate(parts, axis=2)
        pack = pack.reshape((N_DEV,) + pack.shape[2:])
        sems, own, land, carry = scatter_start(f"rs_start{stage}", pack, carry)
        pending.append((stage, units, spans, sems, own, land))
        return carry

    def finish(after):
        res, per_layer, small_flat = {}, {}, None
        for stage, units, spans, sems, own, land in pending:
            own, land = scatter_wait(f"rs_wait{stage}", own, land, sems, after)
            total = sum_shares(f"rs_sum{stage}", own, land, me)
            off = 0
            for (name, layer, tr), (rows, r, c) in zip(units, spans):
                per_layer.setdefault(name, {})[layer] = (total[off:off + rows].reshape(r, c), tr)
                off += rows
            if stage == GRAD_STAGES - 1:
                small_flat = total[off:off + SMALL_ROWS].reshape(-1)
        for name, by_layer in per_layer.items():
            pieces = [by_layer[i][0] for i in sorted(by_layer)]
            res[name] = adamw_units("adamw_" + name, pieces, by_layer[0][1], local[name], mom[name], var[name])
        off = 0
        for name, _ in _SMALL_SHARDED:
            blk = local[name]
            g = small_flat[off:off + blk.size].reshape(blk.shape)
            off += blk.size
            res[name] = (g,) + adamw_native("adamw_" + name, g, blk, mom[name], var[name])
        return res

    return start, finish


def replicated_start(grads, loss):
    pack = jnp.concatenate([_rows128(grads[name]) for name in _REPLICATED] + [_rows128(loss)], axis=0)
    sems, srcs, lands, token = gather_ici_start("ag_g_start", [[pack]], [False])
    return sems[0], srcs[0], lands[0], token


def replicated_finish(handle, after, w, mom, var):
    sems, srcs, lands, _ = handle
    (buf,) = gather_d2d("ag_g_d2d", gather_ici_wait("ag_g_wait", srcs, lands, sems, after))
    rows = srcs[0].shape[0]
    total = sum_slots("ag_g_sum", buf.reshape(N_DEV, rows, LANES))
    res, off = {}, 0
    for name in _REPLICATED:
        n = w[name].size
        nr = -(-n // (LANES * 8)) * 8
        g = total[off:off + nr].reshape(-1)[:n].reshape(w[name].shape)
        off += nr
        res[name] = (g,) + adamw_native("adamw_" + name, g, w[name], mom[name], var[name])
    return res, total[off, 0]


def kernel(x, mem, e_norm, e_w_in, e_gmlp_w, e_gmlp_b, e_conv_w, e_conv_b, e_conv_ln_g, e_conv_ln_b, e_w_out, o_norm, o_w_in, o_lam_re, o_lam_im, o_log_dt, o_b_re, o_b_im, o_c_re, o_c_im, o_d, o_w_out, ca_norm, ca_mem_norm, ca_wq, ca_wk, ca_wv, ca_wo, ffn_norm, ffn_w_gate, ffn_w_up, ffn_w_down, final_norm, loss_target, m_e_norm, m_e_w_in, m_e_gmlp_w, m_e_gmlp_b, m_e_conv_w, m_e_conv_b, m_e_conv_ln_g, m_e_conv_ln_b, m_e_w_out, m_o_norm, m_o_w_in, m_o_lam_re, m_o_lam_im, m_o_log_dt, m_o_b_re, m_o_b_im, m_o_c_re, m_o_c_im, m_o_d, m_o_w_out, m_ca_norm, m_ca_mem_norm, m_ca_wq, m_ca_wk, m_ca_wv, m_ca_wo, m_ffn_norm, m_ffn_w_gate, m_ffn_w_up, m_ffn_w_down, m_final_norm, v_e_norm, v_e_w_in, v_e_gmlp_w, v_e_gmlp_b, v_e_conv_w, v_e_conv_b, v_e_conv_ln_g, v_e_conv_ln_b, v_e_w_out, v_o_norm, v_o_w_in, v_o_lam_re, v_o_lam_im, v_o_log_dt, v_o_b_re, v_o_b_im, v_o_c_re, v_o_c_im, v_o_d, v_o_w_out, v_ca_norm, v_ca_mem_norm, v_ca_wq, v_ca_wk, v_ca_wv, v_ca_wo, v_ffn_norm, v_ffn_w_gate, v_ffn_w_up, v_ffn_w_down, v_final_norm):
    given = dict(locals())
    local = {k: given[k] for k in _ORDER}
    mom = {k: given["m_" + k] for k in _ORDER}
    var = {k: given["v_" + k] for k in _ORDER}

    w = {}
    w.update({
        "e_norm": e_norm, "e_gmlp_w": e_gmlp_w[0], "e_gmlp_b": e_gmlp_b.reshape(A_GROUPS, GMLP_BLOCK, 1),
        "e_conv_b": e_conv_b, "e_conv_ln_g": e_conv_ln_g, "e_conv_ln_b": e_conv_ln_b,
        "o_lam_re": o_lam_re[0], "o_lam_im": o_lam_im[0], "o_log_dt": o_log_dt[0], "o_b_re": o_b_re[0], "o_b_im": o_b_im[0],
        "o_c_re": o_c_re[0], "o_c_im": o_c_im[0], "ca_norm": ca_norm, "ca_mem_norm": ca_mem_norm, "ffn_norm": ffn_norm,
        "final_norm": final_norm.reshape(1, D_MODEL),
    })
    start_reduce, finish_reduce = gradient_reducer(local, mom, var)
    fetch, token = weight_fetcher(local)
    loss_part, grad_x, grads = local_step(x[0], mem[0], loss_target[0], w, fetch, start_reduce, token[0:1, 0:1])
    grads["final_norm"] = grads["final_norm"].reshape(D_MODEL)

    handle = replicated_start(grads, loss_part)
    res = finish_reduce(handle[3])
    rep, loss = replicated_finish(handle, res["ffn_w_down"][1], local, mom, var)
    res.update(rep)
    return (loss, grad_x[None], *[res[k][0] for k in _ORDER], *[res[k][1] for k in _ORDER],
            *[res[k][2] for k in _ORDER], *[res[k][3] for k in _ORDER])
```

```python
import jax
import jax.numpy as jnp
from jax import lax
from jax.experimental import pallas as pl
from jax.experimental.pallas import tpu as pltpu

F32 = jnp.float32
BF16 = jnp.bfloat16
S = jax.ShapeDtypeStruct

D_MODEL = 1024
A_WIDTH = 512
A_GROUPS = 4
GMLP_BLOCK = 128
CHUNK = 64
B_WIDTH = 512
IN_WIDTH = 2 * A_WIDTH + 2 * B_WIDTH
CONV_WIDTH = 31
CONV_PAD = 32
C_WIDTH = 512
C_GROUP_CH = 16
C_GROUPS = 32
C_STATE = 64
N_STATE = C_GROUPS * C_STATE
CA_HEADS = 4
CA_HEAD_DIM = 256
FFN_HIDDEN = 2816
EPS = 1e-6
ADAM_LR = 0.001
ADAM_B1 = 0.9
ADAM_B2 = 0.999
ADAM_EPS = 1e-08
ADAM_WD = 0.01
ADAM_STEP = 10
N_DEV = 8
LANES = 128
VMEM_LIMIT = 56 << 20
VMEM_BUDGET = 40 << 20
MM_TN_RESIDENT = 8 << 20
MESH = pl.DeviceIdType.MESH
ANY = pl.BlockSpec(memory_space=pl.ANY)


def _cp(*sem):
    return pltpu.CompilerParams(dimension_semantics=sem, vmem_limit_bytes=VMEM_LIMIT)


def _tile(n, pref):
    t = pref
    while n % t:
        t //= 2
    return t


def _bf(v):
    return v if v.dtype == BF16 else v.astype(BF16)


def _sigmoid(x):
    return 1.0 / (1.0 + jnp.exp(-x))


_GC = 0.7978845608028654


def _gelu(x):
    return 0.5 * x * (1.0 + jnp.tanh(_GC * (x + 0.044715 * x * x * x)))


def _gelu_grad(x):
    x2 = x * x
    t = jnp.tanh(_GC * (x + 0.044715 * x * x2))
    return 0.5 * (1.0 + t) + 0.5 * x * (1.0 - t * t) * _GC * (1.0 + 3.0 * 0.044715 * x2)


def _tspec(entry, tm):
    if isinstance(entry, tuple):
        arr, cb, width = entry
        return arr, pl.BlockSpec((tm, width), lambda i, cb=cb: (i, cb))
    return entry, pl.BlockSpec((tm, entry.shape[1]), lambda i: (i, 0))


def rows_call(name, fn, tiled, full, outs, accs, tm=256):
    pairs = [_tspec(e, tm) for e in tiled]
    arrs = [p[0] for p in pairs]
    rows = arrs[0].shape[0]
    tm = _tile(rows, tm)
    pairs = [_tspec(e, tm) for e in tiled]
    n_in = len(tiled) + len(full)
    n_out = len(outs)

    def body(*refs):
        vals = [r[...] for r in refs[:n_in]]
        o_refs = refs[n_in:n_in + n_out]
        a_refs = refs[n_in + n_out:]
        ov, av = fn(*vals)
        for r, v in zip(o_refs, ov):
            r[...] = v.astype(r.dtype)
        if a_refs:
            @pl.when(pl.program_id(0) == 0)
            def _():
                for r in a_refs:
                    r[...] = jnp.zeros(r.shape, r.dtype)
            for r, v in zip(a_refs, av):
                r[...] += v

    in_specs = [p[1] for p in pairs] + [pl.BlockSpec(a.shape, lambda i, nd=a.ndim: (0,) * nd) for a in full]
    out_specs = [pl.BlockSpec((tm, c), lambda i: (i, 0)) for c, _ in outs]
    out_specs += [pl.BlockSpec(s, lambda i, nd=len(s): (0,) * nd) for s in accs]
    out_shape = [S((rows, c), dt) for c, dt in outs] + [S(s, F32) for s in accs]
    return pl.pallas_call(body, grid=(rows // tm,), in_specs=in_specs, out_specs=out_specs, out_shape=out_shape,
                          compiler_params=_cp("arbitrary"), name=name)(*arrs, *full)


def mm_nn(name, m, n, pairs, n_acc, epi, outs, tiled=(), cols=(), rowv=(), sums=(), norm_gain=None):
    a_ops, a_slot, b_arrs, b_specs, idx, trans = [], [], [], [], [], []
    fixed = 0
    for pair in pairs:
        a, b, k = pair[:3]
        bt = len(pair) > 3
        arr, cb, kdim = a if isinstance(a, tuple) else (a, 0, a.shape[1])
        key = (id(arr), cb, kdim)
        if key not in [o[0] for o in a_ops]:
            a_ops.append((key, arr, cb, kdim))
        a_slot.append([o[0] for o in a_ops].index(key))
        b_arr, off = b if isinstance(b, tuple) else (b, 0)
        b_arrs.append(b_arr)
        if bt:
            assert off % n == 0 and b_arr.shape[1] == kdim
            b_specs.append(pl.BlockSpec((n, kdim), lambda i, o=off // n: (o, 0), pipeline_mode=pl.Buffered(1)))
        else:
            assert b_arr.shape[1] == n
            b_specs.append(pl.BlockSpec((kdim, n), lambda i, o=off: (o, 0), pipeline_mode=pl.Buffered(1)))
        fixed += kdim * n * b_arr.dtype.itemsize
        idx.append(k)
        trans.append(bt)
    per_row = sum(2 * kdim * arr.dtype.itemsize for _, arr, _, kdim in a_ops)
    per_row += sum(2 * n * t.dtype.itemsize for t in tiled) + sum(2 * n * jnp.dtype(dt).itemsize for dt in outs)
    cn = n if sums or cols else (512 if n % 512 == 0 else 256)
    per_row += (n_acc + 3) * cn * 4
    tm = next((t for t in (1024, 512, 256, 128) if m % t == 0 and fixed + t * per_row <= VMEM_BUDGET), _tile(m, 128))
    n_a, n_p, n_t = len(a_ops), len(pairs), len(tiled)
    n_in = n_a + n_p + n_t + len(cols) + len(rowv)
    normed = norm_gain is not None
    o0 = n_in + normed

    def body(*refs):
        a_vals = [None if normed and i == 0 else _bf(r[...]) for i, r in enumerate(refs[:n_a])]
        if normed:
            xv = refs[0][...]
            rv = lax.rsqrt(jnp.mean(xv * xv, axis=-1, keepdims=True) + EPS)
            a_vals[0] = (xv * rv * refs[n_in][...]).astype(BF16)
            refs[o0 + len(outs)][...] = a_vals[0]
            refs[o0 + len(outs) + 1][...] = rv
        for j in range(n // cn):
            cs = slice(j * cn, (j + 1) * cn)
            accs = [None] * n_acc
            for p in range(n_p):
                av, b_ref = a_vals[a_slot[p]], refs[n_a + p]
                if trans[p]:
                    d = lax.dot_general(av, _bf(b_ref[cs, :]), (((1,), (1,)), ((), ())), preferred_element_type=F32)
                else:
                    d = jnp.dot(av, _bf(b_ref[:, cs]), preferred_element_type=F32)
                accs[idx[p]] = d if accs[idx[p]] is None else accs[idx[p]] + d
            extra = [r[:, cs] for r in refs[n_a + n_p:n_a + n_p + n_t]] + [r[...] for r in refs[n_a + n_p + n_t:n_in - len(rowv)]]
            extra += [r[:, cs] for r in refs[n_in - len(rowv):n_in]]
            ov = epi(accs, *extra)
            for r, v in zip(refs[o0:o0 + len(outs)], ov):
                r[:, cs] = v.astype(r.dtype)
        sv = ov[len(outs):]
        if sums:
            s_refs = refs[o0 + len(outs) + 2 * normed:]

            @pl.when(pl.program_id(0) == 0)
            def _():
                for r in s_refs:
                    r[...] = jnp.zeros(r.shape, r.dtype)
            for r, v in zip(s_refs, sv):
                r[...] += v

    in_specs = [pl.BlockSpec((tm, kdim), lambda i, cb=cb: (i, cb)) for _, _, cb, kdim in a_ops] + b_specs
    in_specs += [pl.BlockSpec((tm, n), lambda i: (i, 0)) for _ in tiled]
    in_specs += [pl.BlockSpec((tm, 1), lambda i: (i, 0)) for _ in cols]
    in_specs += [pl.BlockSpec((1, n), lambda i: (0, 0)) for _ in rowv]
    out_specs = [pl.BlockSpec((tm, n), lambda i: (i, 0)) for _ in outs]
    out_shape = [S((m, n), dt) for dt in outs]
    gain = []
    if normed:
        k0 = a_ops[0][3]
        gain = [norm_gain]
        in_specs.append(pl.BlockSpec((1, k0), lambda i: (0, 0)))
        out_specs += [pl.BlockSpec((tm, k0), lambda i: (i, 0)), pl.BlockSpec((tm, 1), lambda i: (i, 0))]
        out_shape += [S((m, k0), BF16), S((m, 1), F32)]
    out_specs += [pl.BlockSpec(s, lambda i, nd=len(s): (0,) * nd) for s in sums]
    out_shape += [S(s, F32) for s in sums]
    return pl.pallas_call(body, grid=(m // tm,), in_specs=in_specs, out_specs=out_specs, out_shape=out_shape,
                          compiler_params=_cp("arbitrary" if sums else "parallel"),
                          name=name)(*[o[1] for o in a_ops], *b_arrs, *tiled, *cols, *rowv, *gain)


def mm_tn(name, a, b, out_dtype=BF16):
    if isinstance(a, tuple):
        a_arr, a_cb, m = a
    else:
        a_arr, a_cb, m = a, None, a.shape[1]
    if isinstance(b, tuple):
        b_arr, b_cb, n = b
    else:
        b_arr, b_cb, n = b, None, b.shape[1]
    t = a_arr.shape[0]
    whole_b = t * n * b_arr.dtype.itemsize <= MM_TN_RESIDENT and b_cb is None
    tn = n if whole_b else _tile(n, 512)
    tm = _tile(m, 512 if t * 512 * a_arr.dtype.itemsize * 2 + t * tn * b_arr.dtype.itemsize * 2 <= VMEM_BUDGET else 256)
    a_off = 0 if a_cb is None else a_cb * (m // tm)
    b_off = 0 if b_cb is None else b_cb * (n // tn)

    def body(a_ref, b_ref, o_ref):
        o_ref[...] = lax.dot_general(_bf(a_ref[...]), _bf(b_ref[...]), (((0,), (0,)), ((), ())),
                                     preferred_element_type=F32).astype(o_ref.dtype)

    if whole_b:
        b_spec = pl.BlockSpec((t, n), lambda i, j: (0, 0), pipeline_mode=pl.Buffered(1))
    else:
        b_spec = pl.BlockSpec((t, tn), lambda i, j: (0, j + b_off))
    return pl.pallas_call(
        body, grid=(m // tm, n // tn),
        in_specs=[pl.BlockSpec((t, tm), lambda i, j: (0, i + a_off)), b_spec],
        out_specs=pl.BlockSpec((tm, tn), lambda i, j: (i, j)), out_shape=S((m, n), out_dtype),
        compiler_params=_cp("parallel", "parallel"), name=name)(a_arr, b_arr)


def rms_bwd_gain_only(name, dxn, x, r):
    def fn(dv, xv, rv):
        return [], [jnp.sum(dv * xv * rv, axis=0, keepdims=True)]
    return rows_call(name, fn, [dxn, x, r], [], [], [(1, x.shape[1])])[0]


def _final_loss_epi(accs, res, tv, g):
    xv = res + accs[0]
    d = xv.shape[-1]
    r = lax.rsqrt(jnp.mean(xv * xv, axis=-1, keepdims=True) + EPS)
    xh = xv * r
    err = xh * g - tv
    dy = err * (1.0 / d)
    w = dy * g
    dx = r * (w - xh * jnp.mean(w * xh, axis=-1, keepdims=True))
    part = jnp.sum(jnp.sum(err * err, axis=-1, keepdims=True), axis=0, keepdims=True) * (0.5 / d)
    return [dx, dx, jnp.sum(dy * xh, axis=0, keepdims=True), part]


def _gmlp_mask():
    row = lax.broadcasted_iota(jnp.int32, (GMLP_BLOCK, GMLP_BLOCK), 0) // CHUNK
    col = lax.broadcasted_iota(jnp.int32, (GMLP_BLOCK, GMLP_BLOCK), 1) // CHUNK
    return col <= row


def _ln_plain(v):
    mu = jnp.mean(v, axis=-1, keepdims=True)
    vc = v - mu
    rstd = lax.rsqrt(jnp.mean(vc * vc, axis=-1, keepdims=True) + EPS)
    return vc * rstd, rstd


def gmlp_fwd(name, proj, w, b, tm=512):
    t = proj.shape[0]
    tm = _tile(t, tm)

    def body(au_ref, av_ref, w_ref, b_ref, o_ref):
        mask = _gmlp_mask()
        u = _gelu(au_ref[...])
        vn, _ = _ln_plain(_gelu(av_ref[...]))
        vnb = _bf(vn)
        for g in range(A_GROUPS):
            wg = _bf(jnp.where(mask, w_ref[g], 0.0))
            cs = slice(g * GMLP_BLOCK, (g + 1) * GMLP_BLOCK)
            for n in range(tm // GMLP_BLOCK):
                rs = slice(n * GMLP_BLOCK, (n + 1) * GMLP_BLOCK)
                sg = jnp.dot(wg, vnb[rs, cs], preferred_element_type=F32) + b_ref[g]
                o_ref[rs, cs] = (u[rs, cs] * sg).astype(o_ref.dtype)

    return pl.pallas_call(
        body, grid=(t // tm,),
        in_specs=[pl.BlockSpec((tm, A_WIDTH), lambda i: (i, 0)), pl.BlockSpec((tm, A_WIDTH), lambda i: (i, 1)),
                  pl.BlockSpec(w.shape, lambda i: (0, 0, 0)), pl.BlockSpec(b.shape, lambda i: (0, 0, 0))],
        out_specs=pl.BlockSpec((tm, A_WIDTH), lambda i: (i, 0)), out_shape=S((t, A_WIDTH), BF16),
        compiler_params=_cp("parallel"), name=name)(proj, proj, w, b)


def gmlp_bwd(name, proj, dxb, w_out, w, b, tm=512):
    t = proj.shape[0]
    tm = _tile(t, tm)

    def body(au_ref, av_ref, dx_ref, wo_ref, w_ref, b_ref, dp_ref, dw_ref, db_ref):
        @pl.when(pl.program_id(0) == 0)
        def _():
            dw_ref[...] = jnp.zeros(dw_ref.shape, F32)
            db_ref[...] = jnp.zeros(db_ref.shape, F32)

        mask = _gmlp_mask()
        au = au_ref[...]
        av = av_ref[...]
        u = _gelu(au)
        vn, rstd = _ln_plain(_gelu(av))
        vnb = _bf(vn)
        dout = lax.dot_general(dx_ref[...], wo_ref[0:A_WIDTH, :], _NT, preferred_element_type=F32)
        dvn_cols = []
        for g in range(A_GROUPS):
            wm = jnp.where(mask, w_ref[g], 0.0)
            wg = _bf(wm)
            wgt = _bf(wm.T)
            cs = slice(g * GMLP_BLOCK, (g + 1) * GMLP_BLOCK)
            dwg = jnp.zeros((GMLP_BLOCK, GMLP_BLOCK), F32)
            dbg = jnp.zeros((GMLP_BLOCK, 1), F32)
            dvn_rows = []
            for n in range(tm // GMLP_BLOCK):
                rs = slice(n * GMLP_BLOCK, (n + 1) * GMLP_BLOCK)
                sg = jnp.dot(wg, vnb[rs, cs], preferred_element_type=F32) + b_ref[g]
                dp_ref[rs, cs] = (dout[rs, cs] * sg * _gelu_grad(au[rs, cs])).astype(dp_ref.dtype)
                dsg = dout[rs, cs] * u[rs, cs]
                dsgb = _bf(dsg)
                dbg = dbg + jnp.sum(dsg, axis=1, keepdims=True)
                dwg = dwg + lax.dot_general(dsgb, vnb[rs, cs], (((1,), (1,)), ((), ())), preferred_element_type=F32)
                dvn_rows.append(jnp.dot(wgt, dsgb, preferred_element_type=F32))
            dw_ref[g] += jnp.where(mask, dwg, 0.0)
            db_ref[g] += dbg
            dvn_cols.append(jnp.concatenate(dvn_rows, axis=0))
        dvn = jnp.concatenate(dvn_cols, axis=1)
        dv = rstd * (dvn - jnp.mean(dvn, axis=-1, keepdims=True) - vn * jnp.mean(dvn * vn, axis=-1, keepdims=True))
        dp_ref[:, A_WIDTH:] = (dv * _gelu_grad(av)).astype(dp_ref.dtype)

    return pl.pallas_call(
        body, grid=(t // tm,),
        in_specs=[pl.BlockSpec((tm, A_WIDTH), lambda i: (i, 0)), pl.BlockSpec((tm, A_WIDTH), lambda i: (i, 1)),
                  pl.BlockSpec((tm, dxb.shape[1]), lambda i: (i, 0)), pl.BlockSpec(w_out.shape, lambda i: (0, 0)),
                  pl.BlockSpec(w.shape, lambda i: (0, 0, 0)), pl.BlockSpec(b.shape, lambda i: (0, 0, 0))],
        out_specs=[pl.BlockSpec((tm, 2 * A_WIDTH), lambda i: (i, 0)),
                   pl.BlockSpec(w.shape, lambda i: (0, 0, 0)), pl.BlockSpec(b.shape, lambda i: (0, 0, 0))],
        out_shape=[S((t, 2 * A_WIDTH), BF16), S(w.shape, F32), S(b.shape, F32)],
        compiler_params=_cp("arbitrary"), name=name)(proj, proj, dxb, w_out, w, b)


CONV_ROWS = 256


def conv_fwd(name, proj, w, cb):
    t = proj.shape[0]
    tc = LANES
    rows = _tile(t, CONV_ROWS)
    a_cb, g_cb = 2 * A_WIDTH // tc, (2 * A_WIDTH + B_WIDTH) // tc

    def body(a_ref, g_ref, w_ref, cb_ref, o_ref, hpad):
        hpad[0:CONV_PAD, :] = jnp.zeros((CONV_PAD, tc), F32)

        def fill(i, _):
            r0 = pl.multiple_of(i * rows, rows)
            hpad[pl.ds(CONV_PAD + r0, rows), :] = a_ref[pl.ds(r0, rows), :] * _sigmoid(g_ref[pl.ds(r0, rows), :])
            return 0
        lax.fori_loop(0, t // rows, fill, 0)

        def conv(i, _):
            r0 = pl.multiple_of(i * rows, rows)
            win = hpad[pl.ds(r0, rows + CONV_PAD), :]
            acc = jnp.zeros((rows, tc), F32) + cb_ref[...]
            for b in range(SUB):
                wb = win if b == 0 else pltpu.roll(win, b, 0)
                for a in range(CONV_PAD // SUB):
                    k = CONV_WIDTH - 1 - (SUB * a + b)
                    if k >= 0:
                        lo = CONV_PAD - SUB * a
                        acc = acc + wb[lo:lo + rows, :] * w_ref[k:k + 1, :]
            o_ref[pl.ds(r0, rows), :] = acc
            return 0
        lax.fori_loop(0, t // rows, conv, 0)

    return pl.pallas_call(
        body, grid=(B_WIDTH // tc,),
        in_specs=[pl.BlockSpec((t, tc), lambda j: (0, a_cb + j)), pl.BlockSpec((t, tc), lambda j: (0, g_cb + j)),
                  pl.BlockSpec((CONV_WIDTH, tc), lambda j: (0, j)), pl.BlockSpec((1, tc), lambda j: (0, j))],
        out_specs=pl.BlockSpec((t, tc), lambda j: (0, j)), out_shape=S((t, B_WIDTH), F32),
        scratch_shapes=[pltpu.VMEM((t + CONV_PAD, tc), F32)],
        compiler_params=_cp("parallel"), name=name)(proj, proj, w, cb)


def conv_bwd(name, proj, dhc, w):
    t = proj.shape[0]
    tc = LANES
    rows = _tile(t, CONV_ROWS)
    a_cb, g_cb = 2 * A_WIDTH // tc, (2 * A_WIDTH + B_WIDTH) // tc
    win_rows = rows + CONV_PAD

    def body(a_ref, g_ref, d_ref, w_ref, da_ref, dg_ref, dw_ref, dcb_ref, hpad, dpad, dwacc):
        hpad[0:CONV_PAD, :] = jnp.zeros((CONV_PAD, tc), F32)
        dpad[t:t + CONV_PAD, :] = jnp.zeros((CONV_PAD, tc), F32)
        dwacc[...] = jnp.zeros(dwacc.shape, F32)

        def fill(i, _):
            r0 = pl.multiple_of(i * rows, rows)
            hpad[pl.ds(CONV_PAD + r0, rows), :] = a_ref[pl.ds(r0, rows), :] * _sigmoid(g_ref[pl.ds(r0, rows), :])
            dpad[pl.ds(r0, rows), :] = d_ref[pl.ds(r0, rows), :]
            return 0
        lax.fori_loop(0, t // rows, fill, 0)

        def step(i, dcb):
            r0 = pl.multiple_of(i * rows, rows)
            hwin = hpad[pl.ds(r0, win_rows), :]
            dwin = dpad[pl.ds(r0, win_rows), :]
            dchunk = dwin[:rows, :]
            dh = jnp.zeros((rows, tc), F32)
            for b in range(SUB):
                hb = hwin if b == 0 else pltpu.roll(hwin, b, 0)
                db = dwin if b == 0 else pltpu.roll(dwin, win_rows - b, 0)
                for a in range(CONV_PAD // SUB):
                    k = CONV_WIDTH - 1 - (SUB * a + b)
                    if k >= 0:
                        dh = dh + db[SUB * a:SUB * a + rows, :] * w_ref[k:k + 1, :]
                        lo = CONV_PAD - SUB * a
                        prod = dchunk * hb[lo:lo + rows, :]
                        dwacc[k] += jnp.sum(prod.reshape(rows // 8, 8, tc), axis=0)
            a = a_ref[pl.ds(r0, rows), :]
            sg = _sigmoid(g_ref[pl.ds(r0, rows), :])
            da_ref[pl.ds(r0, rows), :] = (dh * sg).astype(da_ref.dtype)
            dg_ref[pl.ds(r0, rows), :] = (dh * a * sg * (1.0 - sg)).astype(dg_ref.dtype)
            return dcb + jnp.sum(dchunk, axis=0, keepdims=True)
        dcb = lax.fori_loop(0, t // rows, step, jnp.zeros((1, tc), F32))
        dcb_ref[...] = dcb
        for k in range(CONV_WIDTH):
            dw_ref[k:k + 1, :] = jnp.sum(dwacc[k], axis=0, keepdims=True)

    return pl.pallas_call(
        body, grid=(B_WIDTH // tc,),
        in_specs=[pl.BlockSpec((t, tc), lambda j: (0, a_cb + j)), pl.BlockSpec((t, tc), lambda j: (0, g_cb + j)),
                  pl.BlockSpec((t, tc), lambda j: (0, j)), pl.BlockSpec((CONV_WIDTH, tc), lambda j: (0, j))],
        out_specs=[pl.BlockSpec((t, tc), lambda j: (0, j)), pl.BlockSpec((t, tc), lambda j: (0, j)),
                   pl.BlockSpec((CONV_WIDTH, tc), lambda j: (0, j)), pl.BlockSpec((1, tc), lambda j: (0, j))],
        out_shape=[S((t, B_WIDTH), BF16), S((t, B_WIDTH), BF16), S((CONV_WIDTH, B_WIDTH), F32), S((1, B_WIDTH), F32)],
        scratch_shapes=[pltpu.VMEM((t + CONV_PAD, tc), F32), pltpu.VMEM((t + CONV_PAD, tc), F32),
                        pltpu.VMEM((CONV_WIDTH, 8, tc), F32)],
        compiler_params=_cp("parallel"), name=name)(proj, proj, dhc, w)


def ln_silu_fwd(name, hc, g, b):
    def fn(h, gv, bv):
        y, _ = _ln_plain(h)
        z = y * gv + bv
        return [z * _sigmoid(z)], []
    return rows_call(name, fn, [hc], [g, b], [(hc.shape[1], BF16)], [])[0]


def ln_silu_bwd(name, hc, dxb, w_out, g, b):
    c = hc.shape[1]

    def fn(h, dxv, wv, gv, bv):
        dout = lax.dot_general(dxv, wv[A_WIDTH:, :], _NT, preferred_element_type=F32)
        y, rstd = _ln_plain(h)
        z = y * gv + bv
        s = _sigmoid(z)
        dz = dout * s * (1.0 + z * (1.0 - s))
        dyv = dz * gv
        dh = rstd * (dyv - jnp.mean(dyv, axis=-1, keepdims=True) - y * jnp.mean(dyv * y, axis=-1, keepdims=True))
        return [dh], [jnp.sum(dz * y, axis=0, keepdims=True), jnp.sum(dz, axis=0, keepdims=True)]

    return rows_call(name, fn, [hc, dxb], [w_out, g, b], [(c, F32)], [(1, c), (1, c)])


_NT = (((1,), (1,)), ((), ()))
_TN = (((0,), (0,)), ((), ()))


def attn_fwd(name, x, gain, wq, k, v, wo, tm=512):
    t, d = x.shape
    m = k.shape[0]
    tm = _tile(t, tm)
    scale = CA_HEAD_DIM ** -0.5

    def body(x_ref, g_ref, wq_ref, k_ref, v_ref, wo_ref, x1_ref, xn_ref, r_ref, q_ref, o_ref):
        xv = x_ref[...]
        rv = lax.rsqrt(jnp.mean(xv * xv, axis=-1, keepdims=True) + EPS)
        xn = (xv * rv * g_ref[...]).astype(BF16)
        xn_ref[...] = xn
        r_ref[...] = rv
        q_ref[...] = jnp.dot(xn, wq_ref[...], preferred_element_type=F32).astype(BF16)
        for h in range(CA_HEADS):
            cs = slice(h * CA_HEAD_DIM, (h + 1) * CA_HEAD_DIM)
            s = lax.dot_general(q_ref[:, cs], k_ref[:, cs], _NT, preferred_element_type=F32) * scale
            e = jnp.exp(s - jnp.max(s, axis=-1, keepdims=True))
            p = e / jnp.sum(e, axis=-1, keepdims=True)
            o_ref[:, cs] = jnp.dot(_bf(p), v_ref[:, cs], preferred_element_type=F32).astype(o_ref.dtype)
        x1_ref[...] = xv + jnp.dot(o_ref[...], wo_ref[...], preferred_element_type=F32)

    def whole(a):
        return pl.BlockSpec(a.shape, lambda i: (0, 0), pipeline_mode=pl.Buffered(1))

    rows = pl.BlockSpec((tm, d), lambda i: (i, 0))
    col = pl.BlockSpec((tm, 1), lambda i: (i, 0))
    return pl.pallas_call(
        body, grid=(t // tm,),
        in_specs=[rows, whole(gain), whole(wq), whole(k), whole(v), whole(wo)],
        out_specs=[rows, rows, col, rows, rows],
        out_shape=[S((t, d), F32), S((t, d), BF16), S((t, 1), F32), S((t, d), BF16), S((t, d), BF16)],
        compiler_params=_cp("parallel"), name=name)(x, gain, wq, k, v, wo)


def attn_bwd(name, dx, dxb, x, r, gain, q, k, v, wq, wo, tm=512):
    t, d = q.shape
    m = k.shape[0]
    tm = _tile(t, tm)
    scale = CA_HEAD_DIM ** -0.5

    def body(dx_ref, dxb_ref, x_ref, r_ref, g_ref, q_ref, k_ref, v_ref, wq_ref, wo_ref,
             dxo_ref, dxbo_ref, dq_ref, dk_ref, dv_ref, dg_ref, do_s):
        @pl.when(pl.program_id(0) == 0)
        def _():
            dk_ref[...] = jnp.zeros(dk_ref.shape, F32)
            dv_ref[...] = jnp.zeros(dv_ref.shape, F32)
            dg_ref[...] = jnp.zeros(dg_ref.shape, F32)

        do_s[...] = lax.dot_general(dxb_ref[...], wo_ref[...], _NT, preferred_element_type=F32).astype(BF16)
        for h in range(CA_HEADS):
            cs = slice(h * CA_HEAD_DIM, (h + 1) * CA_HEAD_DIM)
            qh, kh, vh, doh = q_ref[:, cs], k_ref[:, cs], v_ref[:, cs], do_s[:, cs]
            s = lax.dot_general(qh, kh, _NT, preferred_element_type=F32) * scale
            e = jnp.exp(s - jnp.max(s, axis=-1, keepdims=True))
            p = e / jnp.sum(e, axis=-1, keepdims=True)
            pb = _bf(p)
            dv_ref[:, cs] += lax.dot_general(pb, doh, _TN, preferred_element_type=F32)
            dp = lax.dot_general(doh, vh, _NT, preferred_element_type=F32)
            ds = _bf(p * (dp - jnp.sum(dp * p, axis=-1, keepdims=True)) * scale)
            dq_ref[:, cs] = jnp.dot(ds, kh, preferred_element_type=F32).astype(dq_ref.dtype)
            dk_ref[:, cs] += lax.dot_general(ds, qh, _TN, preferred_element_type=F32)
        dxn = lax.dot_general(dq_ref[...], wq_ref[...], _NT, preferred_element_type=F32)
        xh = x_ref[...] * r_ref[...]
        wv = dxn * g_ref[...]
        dxo = dx_ref[...] + r_ref[...] * (wv - xh * jnp.mean(wv * xh, axis=-1, keepdims=True))
        dxo_ref[...] = dxo
        dxbo_ref[...] = dxo.astype(BF16)
        dg_ref[...] += jnp.sum(dxn * xh, axis=0, keepdims=True)

    def whole(a):
        return pl.BlockSpec(a.shape, lambda i: (0, 0), pipeline_mode=pl.Buffered(1))

    rows = pl.BlockSpec((tm, d), lambda i: (i, 0))
    col = pl.BlockSpec((tm, 1), lambda i: (i, 0))
    acc = pl.BlockSpec((m, d), lambda i: (0, 0))
    return pl.pallas_call(
        body, grid=(t // tm,),
        in_specs=[rows, rows, rows, col, whole(gain), rows, whole(k), whole(v), whole(wq), whole(wo)],
        out_specs=[rows, rows, rows, acc, acc, pl.BlockSpec((1, d), lambda i: (0, 0))],
        out_shape=[S((t, d), F32), S((t, d), BF16), S((t, d), BF16), S((m, d), F32), S((m, d), F32), S((1, d), F32)],
        scratch_shapes=[pltpu.VMEM((tm, d), BF16)],
        compiler_params=_cp("arbitrary"), name=name)(dx, dxb, x, r, gain, q, k, v, wq, wo)


SUB = 8
S5_ROWS = 256


S5_BLOCKS = 4
BLOCK_CH = C_WIDTH // S5_BLOCKS
BLOCK_ST = N_STATE // S5_BLOCKS
_S5_BLOCKS = tuple((slice(BLOCK_CH * q, BLOCK_CH * (q + 1)), slice(BLOCK_ST * q, BLOCK_ST * (q + 1)),
                    slice(N_STATE + BLOCK_ST * q, N_STATE + BLOCK_ST * (q + 1))) for q in range(S5_BLOCKS))
_HI = lax.Precision.HIGHEST
_GP = (C_GROUPS, C_STATE)
_RP = (C_WIDTH, C_STATE)


def _zoh(lr, li, ldt):
    dt = jnp.exp(ldt)
    mag = jnp.exp(lr * dt)
    ar = mag * jnp.cos(li * dt)
    ai = mag * jnp.sin(li * dt)
    den = lr * lr + li * li
    qr = ((ar - 1.0) * lr + ai * li) / den
    qi = (ai * lr - (ar - 1.0) * li) / den
    return dt, ar, ai, den, qr, qi


def _per_channel(v):
    return jnp.broadcast_to(v[:, None, :], (C_GROUPS, C_GROUP_CH, C_STATE)).reshape(_RP)


def _same_group(shape, row_per_group, col_per_group):
    rows = lax.broadcasted_iota(jnp.int32, shape, 0) // row_per_group
    cols = lax.broadcasted_iota(jnp.int32, shape, 1) // col_per_group
    return rows == cols


def _spread(shape, axis):
    long = lax.broadcasted_iota(jnp.int32, shape, axis) % C_STATE
    short = lax.broadcasted_iota(jnp.int32, shape, 1 - axis)
    return long == short


def s5_discretise(name, lam_re, lam_im, log_dt, bt_re, bt_im):
    def body(lr_ref, li_ref, ldt_ref, btr_ref, bti_ref, a_ref, bbr_ref, bbi_ref):
        _, ar, ai, _, qr, qi = _zoh(lr_ref[...], li_ref[...], ldt_ref[...])
        a_ref[0] = ar
        a_ref[1] = ai
        q2r, q2i = _per_channel(qr), _per_channel(qi)
        btr, bti = btr_ref[...], bti_ref[...]
        bbr_ref[...] = q2r * btr - q2i * bti
        bbi_ref[...] = q2r * bti + q2i * btr

    return pl.pallas_call(body, out_shape=[S((2,) + _GP, F32), S(_RP, F32), S(_RP, F32)],
                          name=name)(lam_re, lam_im, log_dt, bt_re, bt_im)


def s5_operands(name, a, bbr, bbi, c2r, c2i, ctr, cti):
    ns = N_STATE

    def body(a_ref, bbr_ref, bbi_ref, c2r_ref, c2i_ref, ctr_ref, cti_ref, pw_ref, qw_ref, mb_ref, mc_ref, mct_ref):
        ar, ai = a_ref[0:1, :], a_ref[1:2, :]
        pows = [(ar, ai)]
        for _ in range(SUB - 1):
            pr, pi = pows[-1]
            pows.append((pr * ar - pi * ai, pr * ai + pi * ar))
        rows = lax.broadcasted_iota(jnp.int32, (SUB, ns), 0)

        def rows_of(v):
            return jnp.broadcast_to(v, (SUB, ns))

        for k, s in enumerate((1, 2, 4)):
            pr, pi = rows_of(pows[s - 1][0]), rows_of(pows[s - 1][1])
            pw_ref[k, 0] = jnp.where(rows >= s, pr, 0.0)
            pw_ref[k, 1] = jnp.where(rows >= s, pi, 0.0)
            qw_ref[k, 0] = jnp.where(rows + s <= SUB - 1, pr, 0.0)
            qw_ref[k, 1] = jnp.where(rows + s <= SUB - 1, -pi, 0.0)
        fr = fi = br = bi = jnp.zeros((SUB, ns), F32)
        for i in range(SUB):
            fr = jnp.where(rows == i, rows_of(pows[i][0]), fr)
            fi = jnp.where(rows == i, rows_of(pows[i][1]), fi)
            br = jnp.where(rows == i, rows_of(pows[SUB - 1 - i][0]), br)
            bi = jnp.where(rows == i, rows_of(-pows[SUB - 1 - i][1]), bi)
        pw_ref[3, 0], pw_ref[3, 1], qw_ref[3, 0], qw_ref[3, 1] = fr, fi, br, bi

        wide = _spread((C_STATE, ns), 1).astype(BF16)
        tall = _spread((ns, C_STATE), 0).astype(BF16)
        in_rows = _same_group((C_WIDTH, ns), C_GROUP_CH, C_STATE)
        in_cols = _same_group((ns, C_WIDTH), C_STATE, C_GROUP_CH)

        def across(v, sign=1.0):
            return jnp.where(in_rows, sign * jnp.dot(_bf(v), wide, preferred_element_type=F32), 0.0).astype(BF16)

        def down(vt, sign=1.0):
            return jnp.where(in_cols, sign * jnp.dot(tall, _bf(vt), preferred_element_type=F32), 0.0).astype(BF16)

        mb_ref[:, 0:ns] = across(bbr_ref[...])
        mb_ref[:, ns:2 * ns] = across(bbi_ref[...])
        mct_ref[:, 0:ns] = across(c2r_ref[...])
        mct_ref[:, ns:2 * ns] = across(c2i_ref[...], -1.0)
        mc_ref[0:ns, :] = down(ctr_ref[...])
        mc_ref[ns:2 * ns, :] = down(cti_ref[...], -1.0)

    return pl.pallas_call(
        body, out_shape=[S((4, 2, SUB, ns), F32), S((4, 2, SUB, ns), F32), S((C_WIDTH, 2 * ns), BF16),
                         S((2 * ns, C_WIDTH), BF16), S((C_WIDTH, 2 * ns), BF16)],
        compiler_params=pltpu.CompilerParams(vmem_limit_bytes=VMEM_LIMIT), name=name)(a, bbr, bbi, c2r, c2i, ctr, cti)


def s5_block_grads(name, u, lamb, xsb, dyb):
    t = u.shape[0]

    def mb_body(u_ref, lr_ref, li_ref, o_ref):
        ub = _bf(u_ref[...])
        o_ref[:, 0:BLOCK_ST] = lax.dot_general(ub, lr_ref[...], _TN, preferred_element_type=F32)
        o_ref[:, BLOCK_ST:2 * BLOCK_ST] = lax.dot_general(ub, li_ref[...], _TN, preferred_element_type=F32)

    d_mb = pl.pallas_call(
        mb_body, grid=(S5_BLOCKS,),
        in_specs=[pl.BlockSpec((t, BLOCK_CH), lambda q: (0, q)), pl.BlockSpec((t, BLOCK_ST), lambda q: (0, q)),
                  pl.BlockSpec((t, BLOCK_ST), lambda q: (0, S5_BLOCKS + q))],
        out_specs=pl.BlockSpec((BLOCK_CH, 2 * BLOCK_ST), lambda q: (q, 0)), out_shape=S((C_WIDTH, 2 * BLOCK_ST), F32),
        compiler_params=_cp("parallel"), name=name + "_b")(u, lamb, lamb)

    def mc_body(x_ref, dy_ref, o_ref):
        o_ref[...] = lax.dot_general(x_ref[...], dy_ref[...], _TN, preferred_element_type=F32)

    d_mc = pl.pallas_call(
        mc_body, grid=(2, S5_BLOCKS),
        in_specs=[pl.BlockSpec((t, BLOCK_ST), lambda p, q: (0, p * S5_BLOCKS + q)), pl.BlockSpec((t, BLOCK_CH), lambda p, q: (0, q))],
        out_specs=pl.BlockSpec((BLOCK_ST, BLOCK_CH), lambda p, q: (p * S5_BLOCKS + q, 0)),
        out_shape=S((2 * N_STATE, BLOCK_CH), F32), compiler_params=_cp("parallel", "parallel"), name=name + "_c")(xsb, dyb)
    return d_mb, d_mc


def s5_param_grads(name, d_mb, d_mc, da, lam_re, lam_im, log_dt, bt_re, bt_im):
    ns = N_STATE

    def body(dmb_ref, dmc_ref, da_ref, lr_ref, li_ref, ldt_ref, btr_ref, bti_ref,
             glr_ref, gli_ref, gdt_ref, gbr_ref, gbi_ref, gcr_ref, gci_ref):
        lr, li = lr_ref[...], li_ref[...]
        dt, ar, ai, den, qr, qi = _zoh(lr, li, ldt_ref[...])
        per_block = C_GROUPS // S5_BLOCKS
        wide = _spread((C_STATE, BLOCK_ST), 1).astype(F32)
        tall = _spread((BLOCK_ST, C_STATE), 0).astype(F32)
        rows = lax.broadcasted_iota(jnp.int32, (C_WIDTH, BLOCK_ST), 0) // C_GROUP_CH % per_block
        in_rows = rows == lax.broadcasted_iota(jnp.int32, (C_WIDTH, BLOCK_ST), 1) // C_STATE
        in_cols = _same_group((BLOCK_ST, BLOCK_CH), C_STATE, C_GROUP_CH)

        def fold_rows(v):
            return lax.dot_general(jnp.where(in_rows, v, 0.0), wide, (((1,), (1,)), ((), ())), precision=_HI,
                                   preferred_element_type=F32)

        def fold_cols(v):
            return lax.dot_general(jnp.where(in_cols, v, 0.0), tall, (((0,), (0,)), ((), ())), precision=_HI,
                                   preferred_element_type=F32)

        for cs, s_re, s_im in _S5_BLOCKS:
            gcr_ref[cs, :] = fold_cols(dmc_ref[s_re, :])
            gci_ref[cs, :] = -fold_cols(dmc_ref[s_im, :])
        gbbr = fold_rows(dmb_ref[:, 0:BLOCK_ST])
        gbbi = fold_rows(dmb_ref[:, BLOCK_ST:2 * BLOCK_ST])
        btr, bti = btr_ref[...], bti_ref[...]
        q2r, q2i = _per_channel(qr), _per_channel(qi)
        gbr_ref[...] = q2r * gbbr + q2i * gbbi
        gbi_ref[...] = q2r * gbbi - q2i * gbbr

        def per_group(v):
            return jnp.sum(v.reshape(C_GROUPS, C_GROUP_CH, C_STATE), axis=1)

        gqr = per_group(btr * gbbr + bti * gbbi)
        gqi = per_group(btr * gbbi - bti * gbbr)
        ilr, ili = lr / den, li / den
        gar = da_ref[0] + ilr * gqr - ili * gqi
        gai = da_ref[1] + ilr * gqi + ili * gqr
        sr = (qr * lr + qi * li) / den
        si = (qi * lr - qr * li) / den
        gzr = ar * gar + ai * gai
        gzi = ar * gai - ai * gar
        glr_ref[...] = -sr * gqr - si * gqi + dt * gzr
        gli_ref[...] = -sr * gqi + si * gqr + dt * gzi
        gdt_ref[...] = jnp.sum(lr * gzr + li * gzi, axis=1, keepdims=True) * dt

    return pl.pallas_call(
        body, out_shape=[S(_GP, F32), S(_GP, F32), S((C_GROUPS, 1), F32), S(_RP, F32), S(_RP, F32), S(_RP, F32), S(_RP, F32)],
        compiler_params=pltpu.CompilerParams(vmem_limit_bytes=VMEM_LIMIT), name=name,
    )(d_mb, d_mc, da, lam_re, lam_im, log_dt, bt_re, bt_im)


def _cmul_add(xr, xi, pr, pi, zr, zi):
    return xr + pr * zr - pi * zi, xi + pr * zi + pi * zr


def s5_fwd(name, u, mb, mc, pw, dskip):
    t = u.shape[0]
    tm = _tile(t, S5_ROWS)
    ns = N_STATE

    def body(u_ref, mb_ref, mc_ref, pw_ref, d_ref, gy_ref, y_ref, xs_ref, xb_ref, carry):
        @pl.when(pl.program_id(0) == 0)
        def _():
            carry[...] = jnp.zeros(carry.shape, F32)

        uv = u_ref[...]
        ub = _bf(uv)
        for cs, s_re, s_im in _S5_BLOCKS:
            xs_ref[:, s_re] = jnp.dot(ub[:, cs], mb_ref[cs, s_re], preferred_element_type=F32)
            xs_ref[:, s_im] = jnp.dot(ub[:, cs], mb_ref[cs, s_im], preferred_element_type=F32)

        def group(i, _):
            r0 = pl.multiple_of(i * SUB, SUB)
            xr = xs_ref[pl.ds(r0, SUB), 0:ns]
            xi = xs_ref[pl.ds(r0, SUB), ns:2 * ns]
            for k, s in enumerate((1, 2, 4)):
                xr, xi = _cmul_add(xr, xi, pw_ref[k, 0], pw_ref[k, 1], pltpu.roll(xr, s, 0), pltpu.roll(xi, s, 0))
            xr, xi = _cmul_add(xr, xi, pw_ref[3, 0], pw_ref[3, 1], carry[0], carry[1])
            xs_ref[pl.ds(r0, SUB), 0:ns] = xr
            xs_ref[pl.ds(r0, SUB), ns:2 * ns] = xi
            carry[0] = jnp.broadcast_to(xr[SUB - 1:SUB, :], (SUB, ns))
            carry[1] = jnp.broadcast_to(xi[SUB - 1:SUB, :], (SUB, ns))
            return 0
        lax.fori_loop(0, tm // SUB, group, 0)

        xb_ref[...] = _bf(xs_ref[...])
        for cs, s_re, s_im in _S5_BLOCKS:
            y = (jnp.dot(xb_ref[:, s_re], mc_ref[s_re, cs], preferred_element_type=F32)
                 + jnp.dot(xb_ref[:, s_im], mc_ref[s_im, cs], preferred_element_type=F32) + d_ref[:, cs] * uv[:, cs])
            y_ref[:, cs] = y
            gy_ref[:, cs] = _gelu(y).astype(gy_ref.dtype)

    c = u.shape[1]
    return pl.pallas_call(
        body, grid=(t // tm,),
        in_specs=[pl.BlockSpec((tm, c), lambda i: (i, 0)), pl.BlockSpec(mb.shape, lambda i: (0, 0)),
                  pl.BlockSpec(mc.shape, lambda i: (0, 0)), pl.BlockSpec(pw.shape, lambda i: (0, 0, 0, 0)),
                  pl.BlockSpec((1, c), lambda i: (0, 0))],
        out_specs=[pl.BlockSpec((tm, c), lambda i: (i, 0)), pl.BlockSpec((tm, c), lambda i: (i, 0)),
                   pl.BlockSpec((tm, 2 * ns), lambda i: (i, 0)), pl.BlockSpec((tm, 2 * ns), lambda i: (i, 0))],
        out_shape=[S((t, c), BF16), S((t, c), F32), S((t, 2 * ns), F32), S((t, 2 * ns), BF16)],
        scratch_shapes=[pltpu.VMEM((2, SUB, ns), F32)],
        compiler_params=_cp("arbitrary"), name=name)(u, mb, mc, pw, dskip)


def s5_bwd(name, dgy, y, u, xs, mct, mbt, qw, dskip):
    t, c = u.shape
    tm = _tile(t, S5_ROWS)
    nt = t // tm
    ns = N_STATE
    ng = tm // SUB

    def body(dgy_ref, y_ref, u_ref, xs_ref, mct_ref, mbt_ref, qw_ref, d_ref,
             du_ref, dy_ref, lb_ref, da_ref, dd_ref, lam, carry):
        @pl.when(pl.program_id(0) == 0)
        def _():
            carry[...] = jnp.zeros(carry.shape, F32)
            da_ref[...] = jnp.zeros(da_ref.shape, F32)
            dd_ref[...] = jnp.zeros(dd_ref.shape, F32)

        uv = u_ref[...]
        dy = dgy_ref[...] * _gelu_grad(y_ref[...])
        dyb = _bf(dy)
        dy_ref[...] = dyb
        dd_ref[...] += jnp.sum(dy * uv, axis=0, keepdims=True)
        for cs, s_re, s_im in _S5_BLOCKS:
            lam[:, s_re] = jnp.dot(dyb[:, cs], mct_ref[cs, s_re], preferred_element_type=F32)
            lam[:, s_im] = jnp.dot(dyb[:, cs], mct_ref[cs, s_im], preferred_element_type=F32)
        last_row = lax.broadcasted_iota(jnp.int32, (SUB, ns), 0) == SUB - 1

        def group(j, _):
            i = ng - 1 - j
            r0 = pl.multiple_of(i * SUB, SUB)
            lr = lam[pl.ds(r0, SUB), 0:ns]
            li = lam[pl.ds(r0, SUB), ns:2 * ns]
            for k, s in enumerate((1, 2, 4)):
                lr, li = _cmul_add(lr, li, qw_ref[k, 0], qw_ref[k, 1],
                                   pltpu.roll(lr, SUB - s, 0), pltpu.roll(li, SUB - s, 0))
            cr, ci = carry[0], carry[1]
            lr, li = _cmul_add(lr, li, qw_ref[3, 0], qw_ref[3, 1], cr, ci)
            lam[pl.ds(r0, SUB), 0:ns] = lr
            lam[pl.ds(r0, SUB), ns:2 * ns] = li
            carry[0] = jnp.broadcast_to(lr[0:1, :], (SUB, ns))
            carry[1] = jnp.broadcast_to(li[0:1, :], (SUB, ns))
            nr = jnp.where(last_row, cr, pltpu.roll(lr, SUB - 1, 0))
            ni = jnp.where(last_row, ci, pltpu.roll(li, SUB - 1, 0))
            xr = xs_ref[pl.ds(r0, SUB), 0:ns]
            xi = xs_ref[pl.ds(r0, SUB), ns:2 * ns]
            da_ref[0] += nr * xr + ni * xi
            da_ref[1] += ni * xr - nr * xi
            return 0
        lax.fori_loop(0, ng, group, 0)

        lb_ref[...] = _bf(lam[...])
        for cs, s_re, s_im in _S5_BLOCKS:
            du = (jnp.dot(lb_ref[:, s_re], mbt_ref[s_re, cs], preferred_element_type=F32)
                  + jnp.dot(lb_ref[:, s_im], mbt_ref[s_im, cs], preferred_element_type=F32) + d_ref[:, cs] * dy[:, cs])
            du_ref[:, cs] = du.astype(du_ref.dtype)

    rev = lambda i: (nt - 1 - i, 0)
    return pl.pallas_call(
        body, grid=(nt,),
        in_specs=[pl.BlockSpec((tm, c), rev), pl.BlockSpec((tm, c), rev), pl.BlockSpec((tm, c), rev),
                  pl.BlockSpec((tm, 2 * ns), rev),
                  pl.BlockSpec(mct.shape, lambda i: (0, 0)), pl.BlockSpec(mbt.shape, lambda i: (0, 0)),
                  pl.BlockSpec(qw.shape, lambda i: (0, 0, 0, 0)), pl.BlockSpec((1, c), lambda i: (0, 0))],
        out_specs=[pl.BlockSpec((tm, c), rev), pl.BlockSpec((tm, c), rev), pl.BlockSpec((tm, 2 * ns), rev),
                   pl.BlockSpec((2, SUB, ns), lambda i: (0, 0, 0)), pl.BlockSpec((1, c), lambda i: (0, 0))],
        out_shape=[S((t, c), BF16), S((t, c), BF16), S((t, 2 * ns), BF16), S((2, SUB, ns), F32), S((1, c), F32)],
        scratch_shapes=[pltpu.VMEM((tm, 2 * ns), F32), pltpu.VMEM((2, SUB, ns), F32)],
        compiler_params=_cp("arbitrary"), name=name)(dgy, y, u, xs, mct, mbt, qw, dskip)


def _first(accs, *_):
    return [accs[0]]


def _rms_bwd_epi(accs, xv, base, rv, g):
    dv = accs[0]
    w = dv * g
    xh = xv * rv
    dx = base + rv * (w - xh * jnp.mean(w * xh, axis=-1, keepdims=True))
    return [dx, dx, jnp.sum(dv * xh, axis=0, keepdims=True)]


def mm_rms_bwd(name, pairs, x, r, gain, dres):
    t, d = x.shape
    return mm_nn(name, t, d, pairs, 1, _rms_bwd_epi, [F32, BF16], tiled=[x, dres], cols=[r], rowv=[gain], sums=[(1, d)])


def _add_res(accs, res):
    return [accs[0] + res]


def even_fwd(x, w, need_out):
    t = x.shape[0]
    proj, hn, r = mm_nn("e_in_f", t, IN_WIDTH, [(x, w["e_w_in_t"], 0, "t")], 1, _first, [F32], norm_gain=w["e_norm"])
    out_a = gmlp_fwd("e_gmlp_f", proj, w["e_gmlp_w"], w["e_gmlp_b"])
    hc = conv_fwd("e_conv_f", proj, w["e_conv_w"], w["e_conv_b"])
    out_b = ln_silu_fwd("e_ln_f", hc, w["e_conv_ln_g"], w["e_conv_ln_b"])
    need_out(out_b)
    (x1,) = mm_nn("e_out_f", t, D_MODEL, [(out_a, (w["e_w_out"], 0), 0), (out_b, (w["e_w_out"], 1), 0)],
                  1, _add_res, [F32], tiled=[x])
    return x1, (x, hn, r, proj, out_a, hc, out_b)


def even_bwd_mixers(dxb, saved, w):
    x, hn, r, proj, out_a, hc, out_b = saved
    t = x.shape[0]
    g_w_out = jnp.concatenate([mm_tn("e_out_wa", out_a, dxb), mm_tn("e_out_wb", out_b, dxb)], axis=0)
    dab, g_gw, g_gb = gmlp_bwd("e_gmlp_b", proj, dxb, w["e_w_out"], w["e_gmlp_w"], w["e_gmlp_b"])
    dhc, g_lg, g_lb = ln_silu_bwd("e_ln_b", hc, dxb, w["e_w_out"], w["e_conv_ln_g"], w["e_conv_ln_b"])
    dba, dbg, g_cw, g_cb = conv_bwd("e_conv_b", proj, dhc, w["e_conv_w"])
    g_w_in_t = jnp.concatenate([mm_tn("e_in_w0", dab, hn), mm_tn("e_in_w1", dba, hn), mm_tn("e_in_w2", dbg, hn)], axis=0)
    grads = dict(e_w_in_t=g_w_in_t, e_gmlp_w=g_gw[None], e_gmlp_b=g_gb.reshape(1, A_GROUPS, GMLP_BLOCK),
                 e_conv_w=g_cw[None], e_conv_b=g_cb, e_conv_ln_g=g_lg, e_conv_ln_b=g_lb, e_w_out=g_w_out)
    return (dab, dba, dbg), grads


def even_bwd_input(dx, dproj, saved, w):
    x, _, r = saved[:3]
    dab, dba, dbg = dproj
    w_in_t = w["e_w_in_t"]
    return mm_rms_bwd("e_in_b", [(dab, (w_in_t, 0), 0), (dba, (w_in_t, 2), 0), (dbg, (w_in_t, 3), 0)], x, r, w["e_norm"], dx)


def s5_setup(w, anchor=None):
    def rows(v):
        return v.transpose(0, 2, 1).reshape(_RP)

    log_dt = w["o_log_dt"].reshape(C_GROUPS, 1)
    if anchor is not None:
        log_dt = log_dt + anchor
    lam = (w["o_lam_re"], w["o_lam_im"], log_dt, rows(w["o_b_re"]), rows(w["o_b_im"]))
    a, bbr, bbi = s5_discretise("o_s5_zoh", *lam)
    c_re, c_im = w["o_c_re"], w["o_c_im"]
    pw, qw, mb, mc, mct = s5_operands("o_s5_ops", a.reshape(2, N_STATE), bbr, bbi, c_re.reshape(_RP), c_im.reshape(_RP),
                                      c_re.transpose(2, 0, 1).reshape(C_STATE, C_WIDTH),
                                      c_im.transpose(2, 0, 1).reshape(C_STATE, C_WIDTH))
    return dict(lam=lam, pw=pw, qw=qw, mb=mb, mc=mc, mct=mct, mbt=mb.T)


def odd_fwd(x, w, consts):
    t = x.shape[0]
    u, hn, r = mm_nn("o_in_f", t, C_WIDTH, [(x, w["o_w_in"], 0)], 1, _first, [F32], norm_gain=w["o_norm"])
    gy, y, xs, xsb = s5_fwd("o_s5_f", u, consts["mb"], consts["mc"], consts["pw"], w["o_d"])
    w_out_t = w["o_w_out_t"]

    def epi(accs, res):
        return [res + accs[0] * _sigmoid(accs[1]), accs[0], accs[1]]

    x1, o1, o2 = mm_nn("o_out_f", t, D_MODEL, [(gy, (w_out_t, 0), 0, "t"), (gy, (w_out_t, D_MODEL), 1, "t")], 2, epi,
                       [F32, BF16, BF16], tiled=[x])
    return x1, (x, hn, r, u, gy, y, xs, xsb, o1, o2)


def odd_bwd(dx, dxb, saved, w, consts):
    x, hn, r, u, gy, y, xs, xsb, o1, o2 = saved
    t = x.shape[0]

    def gate_bwd(dv, a, b, wv):
        a = a.astype(F32)
        sg = _sigmoid(b.astype(F32))
        do12 = jnp.concatenate([dv * sg, dv * a * sg * (1.0 - sg)], axis=1).astype(BF16)
        return [do12, jnp.dot(do12, wv, preferred_element_type=F32)], []

    do12, dgy = rows_call("o_out_b", gate_bwd, [dx, o1, o2], [w["o_w_out_t"]], [(2 * D_MODEL, BF16), (C_WIDTH, F32)], [])
    g_w_out_t = mm_tn("o_out_w", do12, gy)
    du, dyb, lamb, da8, g_d = s5_bwd("o_s5_b", dgy, y, u, xs, consts["mct"], consts["mbt"], consts["qw"], w["o_d"])
    d_mb, d_mc = s5_block_grads("o_s5_w", u, lamb, xsb, dyb)
    da = jnp.sum(da8, axis=1).reshape((2,) + _GP)
    g_lr, g_li, g_dt, g_btr, g_bti, g_cr, g_ci = s5_param_grads("o_s5_pg", d_mb, d_mc, da, *consts["lam"])

    def states_first(v):
        return v.reshape(C_GROUPS, C_GROUP_CH, C_STATE).transpose(0, 2, 1)[None]

    g_w_in = mm_tn("o_in_w", hn, du)
    dx0, dx0b, g_norm = mm_rms_bwd("o_in_b", [(du, w["o_w_in"], 0, "t")], x, r, w["o_norm"], dx)
    grads = dict(o_norm=g_norm, o_w_in=g_w_in, o_lam_re=g_lr[None], o_lam_im=g_li[None], o_log_dt=g_dt.reshape(1, C_GROUPS),
                 o_b_re=states_first(g_btr), o_b_im=states_first(g_bti),
                 o_c_re=g_cr.reshape((1, C_GROUPS, C_GROUP_CH, C_STATE)), o_c_im=g_ci.reshape((1, C_GROUPS, C_GROUP_CH, C_STATE)),
                 o_d=g_d, o_w_out_t=g_w_out_t)
    return dx0, dx0b, grads


def ca_fwd(i, x, mem, w):
    t, m = x.shape[0], mem.shape[0]
    k, v, mn, rm = mm_nn(f"ca{i}_kv_f", m, D_MODEL, [(mem, w["ca_wk"][i], 0), (mem, w["ca_wv"][i], 1)], 2,
                         lambda accs: [accs[0], accs[1]], [BF16, BF16], norm_gain=w["ca_mem_norm"][i:i + 1])
    x1, xn, r, q, o = attn_fwd(f"ca{i}_attn_f", x, w["ca_norm"][i:i + 1], w["ca_wq"][i], k, v, w["ca_wo"][i])
    return x1, (x, xn, r, mn, rm, q, k, v, o)


def ca_bwd(i, dx, dxb, saved, mem, w):
    x, xn, r, mn, rm, q, k, v, o = saved
    t, m = x.shape[0], mem.shape[0]
    g_wo = mm_tn(f"ca{i}_o_w", o, dxb)
    dx0, dx0b, dq, dk, dv, g_norm = attn_bwd(f"ca{i}_attn_b", dx, dxb, x, r, w["ca_norm"][i:i + 1], q, k, v,
                                             w["ca_wq"][i], w["ca_wo"][i])
    g_wq = mm_tn(f"ca{i}_q_w", xn, dq)
    g_wk = mm_tn(f"ca{i}_k_w", mn, dk)
    g_wv = mm_tn(f"ca{i}_v_w", mn, dv)
    (dmn,) = mm_nn(f"ca{i}_kv_b", m, D_MODEL, [(dk, w["ca_wk"][i], 0, "t"), (dv, w["ca_wv"][i], 0, "t")], 1, _first, [F32])
    g_mnorm = rms_bwd_gain_only(f"ca{i}_mnorm_b", dmn, mem, rm)
    return dx0, dx0b, dict(ca_norm=g_norm, ca_mem_norm=g_mnorm, ca_wq=g_wq, ca_wk=g_wk, ca_wv=g_wv, ca_wo=g_wo)


FFN_ROWS = 512
FFN_CHUNK = 256


def _whole(a):
    return pl.BlockSpec(a.shape, lambda i: (0,) * a.ndim, pipeline_mode=pl.Buffered(1))


def ffn_fused_fwd(name, x, gain, wg_t, wu_t, wd, target=None, final_gain=None):
    t, d = x.shape
    hid = wd.shape[0]
    tm = _tile(t, FFN_ROWS)
    last = target is not None
    n_main = 4 if last else 1

    def body(*refs):
        x_ref, g_ref, wg_ref, wu_ref, wd_ref = refs[:5]
        rest = refs[5:]
        if last:
            tgt_ref, fg_ref = rest[:2]
            rest = rest[2:]
        main, (xn_ref, r_ref, dgate_ref, dup_ref, h_ref) = rest[:n_main], rest[n_main:]
        xv = x_ref[...]
        rv = lax.rsqrt(jnp.mean(xv * xv, axis=-1, keepdims=True) + EPS)
        xn = (xv * rv * g_ref[...]).astype(BF16)
        xn_ref[...] = xn
        r_ref[...] = rv
        for j in range(hid // FFN_CHUNK):
            cs = slice(j * FFN_CHUNK, (j + 1) * FFN_CHUNK)
            g = lax.dot_general(xn, wg_ref[cs, :], _NT, preferred_element_type=F32)
            u = lax.dot_general(xn, wu_ref[cs, :], _NT, preferred_element_type=F32)
            s = _sigmoid(g)
            silu = g * s
            dgate_ref[:, cs] = (u * (s + silu * (1.0 - s))).astype(BF16)
            dup_ref[:, cs] = silu.astype(BF16)
            h_ref[:, cs] = (silu * u).astype(BF16)
        acc = jnp.dot(h_ref[...], wd_ref[...], preferred_element_type=F32)
        if not last:
            main[0][...] = xv + acc
        else:
            dx, _, dgain, part = _final_loss_epi([acc], xv, tgt_ref[...], fg_ref[...])

            @pl.when(pl.program_id(0) == 0)
            def _():
                main[2][...] = jnp.zeros(main[2].shape, F32)
                main[3][...] = jnp.zeros(main[3].shape, F32)
            main[0][...] = dx
            main[1][...] = dx.astype(BF16)
            main[2][...] += dgain
            main[3][...] += part

    rows = pl.BlockSpec((tm, d), lambda i: (i, 0))
    wide = pl.BlockSpec((tm, hid), lambda i: (i, 0))
    col = pl.BlockSpec((tm, 1), lambda i: (i, 0))
    ins, in_specs = [x, gain, wg_t, wu_t, wd], [rows, _whole(gain), _whole(wg_t), _whole(wu_t), _whole(wd)]
    if last:
        ins += [target, final_gain]
        in_specs += [rows, _whole(final_gain)]
        out_specs = [rows, rows, pl.BlockSpec((1, d), lambda i: (0, 0)), pl.BlockSpec((1, 1), lambda i: (0, 0))]
        out_shape = [S((t, d), F32), S((t, d), BF16), S((1, d), F32), S((1, 1), F32)]
    else:
        out_specs, out_shape = [rows], [S((t, d), F32)]
    out_specs += [rows, col, wide, wide, wide]
    out_shape += [S((t, d), BF16), S((t, 1), F32)] + [S((t, hid), BF16)] * 3
    outs = pl.pallas_call(body, grid=(t // tm,), in_specs=in_specs, out_specs=out_specs, out_shape=out_shape,
                          compiler_params=_cp("arbitrary" if last else "parallel"), name=name)(*ins)
    return (tuple(outs[:4]) if last else outs[0]), outs[n_main:]


def ffn_fused_bwd(name, dx, dxb, x, r, gain, dgate, dup, wg_t, wu_t, wd):
    t, d = x.shape
    hid = wd.shape[0]
    tm = _tile(t, FFN_ROWS // 2)

    def body(dx_ref, dxb_ref, x_ref, r_ref, g_ref, dgate_ref, dup_ref, wg_ref, wu_ref, wd_ref,
             dxo_ref, dxbo_ref, dg_ref, du_ref, dgain_ref):
        @pl.when(pl.program_id(0) == 0)
        def _():
            dgain_ref[...] = jnp.zeros(dgain_ref.shape, F32)

        dxb = dxb_ref[...]
        for j in range(hid // FFN_CHUNK):
            cs = slice(j * FFN_CHUNK, (j + 1) * FFN_CHUNK)
            dh = lax.dot_general(dxb, wd_ref[cs, :], _NT, preferred_element_type=F32)
            dg_ref[:, cs] = (dh * dgate_ref[:, cs].astype(F32)).astype(BF16)
            du_ref[:, cs] = (dh * dup_ref[:, cs].astype(F32)).astype(BF16)
        dxn = (jnp.dot(dg_ref[...], wg_ref[...], preferred_element_type=F32)
               + jnp.dot(du_ref[...], wu_ref[...], preferred_element_type=F32))
        dxo, _, dgain = _rms_bwd_epi([dxn], x_ref[...], dx_ref[...], r_ref[...], g_ref[...])
        dxo_ref[...] = dxo
        dxbo_ref[...] = dxo.astype(BF16)
        dgain_ref[...] += dgain

    rows = pl.BlockSpec((tm, d), lambda i: (i, 0))
    wide = pl.BlockSpec((tm, hid), lambda i: (i, 0))
    col = pl.BlockSpec((tm, 1), lambda i: (i, 0))
    return pl.pallas_call(
        body, grid=(t // tm,),
        in_specs=[rows, rows, rows, col, _whole(gain), wide, wide, _whole(wg_t), _whole(wu_t), _whole(wd)],
        out_specs=[rows, rows, wide, wide, pl.BlockSpec((1, d), lambda i: (0, 0))],
        out_shape=[S((t, d), F32), S((t, d), BF16), S((t, hid), BF16), S((t, hid), BF16), S((1, d), F32)],
        compiler_params=_cp("arbitrary"), name=name)(dx, dxb, x, r, gain, dgate, dup, wg_t, wu_t, wd)


def ffn_fwd(i, x, w, target=None):
    out, (xn, r, dgate, dup, h) = ffn_fused_fwd(f"ffn{i}_f", x, w["ffn_norm"][i:i + 1], w["ffn_w_gate_t"][i],
                                                w["ffn_w_up_t"][i], w["ffn_w_down"][i], target,
                                                None if target is None else w["final_norm"])
    return out, (x, xn, r, dgate, dup, h)


def ffn_bwd(i, dx, dxb, saved, w):
    x, xn, r, dgate, dup, h = saved
    g_wd = mm_tn(f"ffn{i}_down_w", h, dxb)
    dx0, dx0b, dg, du, g_norm = ffn_fused_bwd(f"ffn{i}_b", dx, dxb, x, r, w["ffn_norm"][i:i + 1], dgate, dup,
                                              w["ffn_w_gate_t"][i], w["ffn_w_up_t"][i], w["ffn_w_down"][i])
    g_wg_t = mm_tn(f"ffn{i}_gate_w", dg, xn)
    g_wu_t = mm_tn(f"ffn{i}_up_w", du, xn)
    return dx0, dx0b, dict(ffn_norm=g_norm, ffn_w_gate_t=g_wg_t, ffn_w_up_t=g_wu_t, ffn_w_down=g_wd)


def local_step(x, mem, target, w, fetch=None, on_grads=None, anchor=None):
    consts = s5_setup(w, anchor)

    def need(stage, after):
        if fetch is not None:
            for k, v in fetch(stage, after).items():
                if isinstance(k, tuple):
                    w.setdefault(k[0], {})[k[1]] = v
                else:
                    w[k] = v

    need(0, consts["pw"])
    x1, s_e = even_fwd(x, w, lambda after: need(1, after))
    x2, s_c0 = ca_fwd(0, x1, mem, w)
    need(2, x2)
    x3, s_f0 = ffn_fwd(0, x2, w)
    x4, s_o = odd_fwd(x3, w, consts)
    need(3, x4)
    x5, s_c1 = ca_fwd(1, x4, mem, w)
    (dx, dxb, g_final, loss), s_f1 = ffn_fwd(1, x5, w, target)

    def emit(stage, carry, plain, layered=None, layer=0):
        if on_grads is None:
            return carry
        out = dict(plain)
        out.update({(k, layer): v for k, v in (layered or {}).items()})
        return on_grads(stage, out, list(carry))

    dx, dxb, g_f1 = ffn_bwd(1, dx, dxb, s_f1, w)
    dx, dxb = emit(0, (dx, dxb), {}, g_f1, 1)
    dx, dxb, g_c1 = ca_bwd(1, dx, dxb, s_c1, mem, w)
    dx, dxb, g_o = odd_bwd(dx, dxb, s_o, w, consts)
    dx, dxb = emit(1, (dx, dxb), g_o, g_c1, 1)
    dx, dxb, g_f0 = ffn_bwd(0, dx, dxb, s_f0, w)
    dx, dxb = emit(2, (dx, dxb), {}, g_f0, 0)
    dx, dxb, g_c0 = ca_bwd(0, dx, dxb, s_c0, mem, w)
    dx, dxb = emit(3, (dx, dxb), {}, g_c0, 0)
    dproj, g_e = even_bwd_mixers(dxb, s_e, w)
    dproj = emit(4, dproj, {**g_e, "o_norm": g_o["o_norm"], "o_d": g_o["o_d"]})
    dx, dxb, g_e["e_norm"] = even_bwd_input(dx, dproj, s_e, w)

    grads = dict(g_e)
    grads.update(g_o)
    for g0, g1 in ((g_c0, g_c1), (g_f0, g_f1)):
        for k in g0:
            grads[k] = jnp.concatenate([g0[k], g1[k]], axis=0) if k.endswith("norm") else (g0[k], g1[k])
    grads["final_norm"] = g_final
    return loss, dx, grads


def _group(axes):
    pos = {a: lax.axis_index(a) for a in ("x", "y", "c")}
    me = 0
    for a in axes:
        me = me * 2 + pos[a]
    peers = []
    for mask in range(1, 2 ** len(axes)):
        peer = dict(pos)
        for bit, a in enumerate(axes):
            if (mask >> (len(axes) - 1 - bit)) & 1:
                peer[a] = 1 - pos[a]
        idx = 0
        for a in axes:
            idx = idx * 2 + peer[a]
        peers.append((idx, (peer["x"], peer["y"], peer["c"])))
    return me, peers


def _sibling():
    x, y, c = lax.axis_index("x"), lax.axis_index("y"), lax.axis_index("c")
    return c, (x, y, 1 - c)


_HBM =pl.BlockSpec(memory_space=pltpu.HBM)
_SEM = pl.BlockSpec(memory_space=pltpu.SEMAPHORE)
_EFFECT = pltpu.SideEffectType.DATAFLOW_SIDE_EFFECTING


def _gather_peers(direct):
    chip, _ = _group(("x", "y"))
    core = lax.axis_index("c")
    if direct:
        _, peers = _group(_ALL)
        return chip, core, [(idx // 2, idx % 2, dev) for idx, dev in peers]
    _, peers = _group(("x", "y"))
    return chip, core, [(idx, core, dev) for idx, dev in peers]


def gather_ici_start(name, groups, direct):
    flat = [b for g in groups for b in g]
    sizes = [len(g) for g in groups]
    k_ops, n_g = len(flat), len(groups)
    lands = [lax.empty((4, 2) + tuple(b.shape), b.dtype) for b in flat]
    fan = [N_DEV - 1 if d else 3 for d in direct]

    def body(*refs):
        src, land = refs[:k_ops], refs[k_ops:2 * k_ops]
        sems = refs[2 * k_ops:2 * k_ops + 3 * n_g]
        token = refs[-1]
        i = 0
        for g in range(n_g):
            send, recv, loc = sems[3 * g:3 * g + 3]
            chip, core, peers = _gather_peers(direct[g])
            for j in range(sizes[g]):
                pltpu.make_async_copy(src[i], land[i].at[chip, core], loc.at[j]).start()
                for k, (_, _, dev) in enumerate(peers):
                    s = fan[g] * j + k
                    pltpu.make_async_remote_copy(src_ref=src[i], dst_ref=land[i].at[chip, core], send_sem=send.at[s],
                                                 recv_sem=recv.at[s], device_id=dev, device_id_type=MESH).start()
                i += 1
        token[...] = jnp.zeros(token.shape, token.dtype)

    sem_shapes = []
    for s, f in zip(sizes, fan):
        sem_shapes += [pltpu.SemaphoreType.DMA((f * s,)), pltpu.SemaphoreType.DMA((f * s,)), pltpu.SemaphoreType.DMA((s,))]
    thru = [pltpu.HBM(a.shape, a.dtype) for a in flat + lands]
    outs = pl.pallas_call(
        body, name=name, out_shape=tuple(sem_shapes) + tuple(thru) + (S((8, LANES), F32),),
        in_specs=[_HBM] * (2 * k_ops), out_specs=[_SEM] * (3 * n_g) + [_HBM] * (2 * k_ops) + [pl.BlockSpec(memory_space=pltpu.VMEM)],
        input_output_aliases={i: 3 * n_g + i for i in range(2 * k_ops)},
        compiler_params=pltpu.CompilerParams(has_side_effects=_EFFECT),
    )(*[pltpu.with_memory_space_constraint(a, pltpu.HBM) for a in flat + lands])
    sems = [tuple(outs[3 * g:3 * g + 3]) for g in range(n_g)]
    srcs_thru, lands_thru, off = [], [], 3 * n_g
    for s in sizes:
        srcs_thru.append(list(outs[off:off + s]))
        off += s
    for s in sizes:
        lands_thru.append(list(outs[off:off + s]))
        off += s
    return sems, srcs_thru, lands_thru, outs[-1]


def gather_ici_wait(name, srcs, lands, sems, after, direct=False):
    n = len(srcs)

    def body(*refs):
        src, land = refs[:n], refs[n:2 * n]
        send, recv, loc = refs[2 * n:2 * n + 3]
        chip, core, peers = _gather_peers(direct)
        for j in range(n):
            for k, (pchip, pcore, dev) in enumerate(peers):
                s = len(peers) * j + k
                cp = pltpu.make_async_remote_copy(src_ref=src[j], dst_ref=land[j].at[pchip, pcore], send_sem=send.at[s],
                                                  recv_sem=recv.at[s], device_id=dev, device_id_type=MESH)
                cp.wait_send()
                cp.wait_recv()
            pltpu.make_async_copy(src[j], land[j].at[chip, core], loc.at[j]).wait()

    outs = pl.pallas_call(
        body, name=name, out_shape=tuple(pltpu.HBM(a.shape, a.dtype) for a in list(srcs) + list(lands)),
        in_specs=[_HBM] * (2 * n) + [_SEM] * 3 + [ANY], out_specs=[_HBM] * (2 * n),
        input_output_aliases={i: i for i in range(2 * n)},
        compiler_params=pltpu.CompilerParams(has_side_effects=_EFFECT),
    )(*srcs, *lands, *sems, after)
    return list(outs[n:])


def gather_d2d(name, bufs):
    k_ops = len(bufs)

    def body(*refs):
        in_refs, out_refs = refs[:k_ops], refs[k_ops:2 * k_ops]
        send_sems, recv_sems = refs[2 * k_ops:]
        core, sib = _sibling()
        sent, landed = [], []
        for i in range(k_ops):
            cp = pltpu.make_async_remote_copy(src_ref=in_refs[i].at[:, core], dst_ref=out_refs[i].at[:, core],
                                              send_sem=send_sems.at[i], recv_sem=recv_sems.at[i], device_id=sib, device_id_type=MESH)
            cp.start()
            sent.append(cp)
            landed.append(pltpu.make_async_remote_copy(src_ref=in_refs[i].at[:, core], dst_ref=out_refs[i].at[:, 1 - core],
                                                       send_sem=send_sems.at[i], recv_sem=recv_sems.at[i],
                                                       device_id=sib, device_id_type=MESH))
        for cp in landed:
            cp.wait_recv()
        for cp in sent:
            cp.wait_send()

    return pl.pallas_call(
        body, in_specs=[ANY] * k_ops, out_specs=[ANY] * k_ops, out_shape=[S(b.shape, b.dtype) for b in bufs],
        input_output_aliases={i: i for i in range(k_ops)},
        scratch_shapes=[pltpu.SemaphoreType.DMA((k_ops,)), pltpu.SemaphoreType.DMA((k_ops,))],
        name=name)(*bufs)


_ALL = ("x", "y", "c")


def scatter_start(name, arr, carry):
    land = lax.empty(arr.shape, arr.dtype)
    n_c = len(carry)

    def body(*refs):
        in_ref, land_ref = refs[0], refs[1]
        send, recv = refs[2 + n_c], refs[3 + n_c]
        me, peers = _group(_ALL)
        for k, (idx, dev) in enumerate(peers):
            pltpu.make_async_remote_copy(src_ref=in_ref.at[idx], dst_ref=land_ref.at[me], send_sem=send.at[k], recv_sem=recv.at[k],
                                         device_id=dev, device_id_type=MESH).start()

    thru = [arr, land] + list(carry)
    outs = pl.pallas_call(
        body, name=name,
        out_shape=(pltpu.SemaphoreType.DMA((N_DEV - 1,)), pltpu.SemaphoreType.DMA((N_DEV - 1,)))
        + tuple(pltpu.HBM(a.shape, a.dtype) for a in thru),
        in_specs=[_HBM] * len(thru), out_specs=[_SEM, _SEM] + [_HBM] * len(thru),
        input_output_aliases={i: 2 + i for i in range(len(thru))},
        compiler_params=pltpu.CompilerParams(has_side_effects=_EFFECT),
    )(*[pltpu.with_memory_space_constraint(a, pltpu.HBM) for a in thru])
    return (outs[0], outs[1]), outs[2], outs[3], list(outs[4:])


def scatter_wait(name, arr, land, sems, after):
    def body(in_ref, land_ref, send, recv, after_ref, in_thru, land_thru):
        _, peers = _group(_ALL)
        for k, (idx, dev) in enumerate(peers):
            cp = pltpu.make_async_remote_copy(src_ref=in_ref.at[idx], dst_ref=land_ref.at[idx], send_sem=send.at[k],
                                              recv_sem=recv.at[k], device_id=dev, device_id_type=MESH)
            cp.wait_send()
            cp.wait_recv()

    outs = pl.pallas_call(
        body, name=name, out_shape=(pltpu.HBM(arr.shape, arr.dtype), pltpu.HBM(arr.shape, arr.dtype)),
        in_specs=[_HBM, _HBM, _SEM, _SEM, ANY], out_specs=[_HBM, _HBM], input_output_aliases={0: 0, 1: 1},
        compiler_params=pltpu.CompilerParams(has_side_effects=_EFFECT),
    )(arr, land, sems[0], sems[1], after)
    return outs[0], outs[1]


def _row_tile(rows, cap=512):
    return next(t for t in range(cap - cap % 16, 0, -16) if rows % t == 0)


def sum_shares(name, own, recv, me):
    n, rows, c = recv.shape
    tr = _row_tile(rows)

    def body(me_ref, *refs):
        acc = refs[0][...].astype(F32)
        for r in refs[1:n]:
            acc = acc + r[...].astype(F32)
        refs[n][...] = acc

    def slot(mask):
        return pl.BlockSpec((None, tr, c), lambda i, me, mask=mask: (jnp.bitwise_xor(me[0], mask), i, 0))

    spec = pltpu.PrefetchScalarGridSpec(
        num_scalar_prefetch=1, grid=(rows // tr,), in_specs=[slot(k) for k in range(n)],
        out_specs=pl.BlockSpec((tr, c), lambda i, me: (i, 0)))
    return pl.pallas_call(body, grid_spec=spec, out_shape=S((rows, c), F32),
                          compiler_params=_cp("parallel"), name=name)(me, own, *([recv] * (n - 1)))


def sum_slots(name, slots):
    n, r, c = slots.shape

    def body(s_ref, o_ref):
        acc = s_ref[0]
        for j in range(1, n):
            acc = acc + s_ref[j]
        o_ref[...] = acc

    return pl.pallas_call(body, out_shape=S((r, c), F32), compiler_params=pltpu.CompilerParams(vmem_limit_bytes=VMEM_LIMIT),
                          name=name)(slots)


def adamw_units(name, pieces, transposed, w, m, v):
    n_l, k, n = w.shape
    tk = _tile(k, 512) if transposed else k
    c1 = 1.0 - ADAM_B1 ** ADAM_STEP
    c2 = 1.0 - ADAM_B2 ** ADAM_STEP

    def body(*refs):
        p_refs, (w_ref, m_ref, v_ref, g_ref, d_ref, m2_ref, v2_ref) = refs[:n_l], refs[n_l:]
        gv = p_refs[0][...]
        for j in range(1, n_l):
            gv = jnp.where(pl.program_id(0) == j, p_refs[j][...], gv)
        if transposed:
            gv = gv.T
        m2 = ADAM_B1 * m_ref[...] + (1.0 - ADAM_B1) * gv
        v2 = ADAM_B2 * v_ref[...] + (1.0 - ADAM_B2) * (gv * gv)
        g_ref[...] = gv
        m2_ref[...] = m2
        v2_ref[...] = v2
        d_ref[...] = -ADAM_LR * ((m2 / c1) / (jnp.sqrt(v2 / c2) + ADAM_EPS) + ADAM_WD * w_ref[...])

    piece = pl.BlockSpec((n, tk), lambda l, i: (0, i)) if transposed else pl.BlockSpec((k, n), lambda l, i: (0, 0))
    blk = pl.BlockSpec((None, tk, n), lambda l, i: (l, i, 0))
    return tuple(pl.pallas_call(body, grid=(n_l, k // tk), in_specs=[piece] * n_l + [blk] * 3, out_specs=[blk] * 4,
                                out_shape=[S(w.shape, F32)] * 4, compiler_params=_cp("parallel", "parallel"),
                                name=name)(*pieces, w, m, v))


def adamw_native(name, g, w, m, v, tr=512):
    shape = w.shape
    cols = shape[-1]
    rows = w.size // cols
    tr = _tile(rows, tr) if rows % 8 == 0 else rows
    c1 = 1.0 - ADAM_B1 ** ADAM_STEP
    c2 = 1.0 - ADAM_B2 ** ADAM_STEP

    def body(g_ref, w_ref, m_ref, v_ref, d_ref, m2_ref, v2_ref):
        gv = g_ref[...]
        m2 = ADAM_B1 * m_ref[...] + (1.0 - ADAM_B1) * gv
        v2 = ADAM_B2 * v_ref[...] + (1.0 - ADAM_B2) * (gv * gv)
        m2_ref[...] = m2
        v2_ref[...] = v2
        d_ref[...] = -ADAM_LR * ((m2 / c1) / (jnp.sqrt(v2 / c2) + ADAM_EPS) + ADAM_WD * w_ref[...])

    row = pl.BlockSpec((tr, cols), lambda i: (i, 0))
    outs = pl.pallas_call(body, grid=(rows // tr,), in_specs=[row] * 4, out_specs=[row] * 3,
                          out_shape=[S((rows, cols), F32)] * 3, compiler_params=_cp("parallel"),
                          name=name)(*[a.reshape(rows, cols) for a in (g, w, m, v)])
    return tuple(o.reshape(shape) for o in outs)


_REPLICATED = ("e_norm", "e_gmlp_w", "e_gmlp_b", "e_conv_b", "e_conv_ln_g", "e_conv_ln_b", "o_lam_re", "o_lam_im", "o_log_dt",
               "o_b_re", "o_b_im", "o_c_re", "o_c_im", "ca_norm", "ca_mem_norm", "ffn_norm", "final_norm")
_ORDER = ("e_norm", "e_w_in", "e_gmlp_w", "e_gmlp_b", "e_conv_w", "e_conv_b", "e_conv_ln_g", "e_conv_ln_b", "e_w_out",
          "o_norm", "o_w_in", "o_lam_re", "o_lam_im", "o_log_dt", "o_b_re", "o_b_im", "o_c_re", "o_c_im", "o_d", "o_w_out",
          "ca_norm", "ca_mem_norm", "ca_wq", "ca_wk", "ca_wv", "ca_wo", "ffn_norm", "ffn_w_gate", "ffn_w_up", "ffn_w_down",
          "final_norm")


def _rows128(a, multiple=8):
    flat = a.reshape(-1)
    rows = -(-flat.shape[0] // (LANES * multiple)) * multiple
    return jnp.pad(flat, (0, rows * LANES - flat.shape[0])).reshape(rows, LANES)


def _shard(full, axis):
    s = full.shape
    return jnp.moveaxis(full.reshape(s[:axis] + (N_DEV, s[axis] // N_DEV) + s[axis + 1:]), axis, 0)


_UNITS = (("e_w_in", 0, True), ("e_w_out", 0, False), ("o_w_in", 0, False), ("o_w_out", 0, True),
          *[(n, i, False) for n in ("ca_wq", "ca_wk", "ca_wv", "ca_wo") for i in (0, 1)],
          *[(n, i, tr) for n, tr in (("ffn_w_gate", True), ("ffn_w_up", True), ("ffn_w_down", False)) for i in (0, 1)])
_LAYERED = ("ca_wq", "ca_wk", "ca_wv", "ca_wo", "ffn_w_gate", "ffn_w_up", "ffn_w_down")
_SMALL_SHARDED = (("e_conv_w", 2), ("o_norm", 1), ("o_d", 1))
RS_ROW = 1024


def _unit_key(name, tr):
    return name + "_t" if tr else name


def _stage_of(name, layer):
    if name.startswith("e_"):
        return 0 if name == "e_w_in" else 1
    if name.startswith("o_"):
        return 2
    if name.startswith("ca_"):
        return 1 if layer == 0 else 3
    return 2 if layer == 0 else 3


def weight_fetcher(local):
    groups, meta = [[] for _ in range(4)], [[] for _ in range(4)]
    for name, layer, tr in _UNITS:
        blk = local[name][layer]
        st = _stage_of(name, layer)
        groups[st].append(_bf(blk.T if tr else blk))
        meta[st].append((name, layer, tr))
    small = jnp.concatenate([local[name].reshape(-1) for name, _ in _SMALL_SHARDED])
    groups[0].append(_rows128(small))
    direct = [False, False, False, True]
    sems, srcs, lands, token = gather_ici_start("ag_w_start", groups, direct)

    def fetch(stage, after):
        bufs = gather_ici_wait(f"ag_w_wait{stage}", srcs[stage], lands[stage], sems[stage], after, direct[stage])
        if not direct[stage]:
            bufs = gather_d2d(f"ag_w_d2d{stage}", bufs)
        got = {}
        for (name, layer, tr), blk, buf in zip(meta[stage], groups[stage], bufs):
            arr = buf.reshape((N_DEV * blk.shape[0],) + tuple(blk.shape[1:]))
            if name in _LAYERED:
                got[(_unit_key(name, tr), layer)] = arr
            else:
                got[_unit_key(name, tr)] = arr
        if stage == 0:
            flat = bufs[-1].reshape(N_DEV, -1)
            off = 0
            for name, axis in _SMALL_SHARDED:
                blk = local[name]
                seg = flat[:, off:off + blk.size].reshape((N_DEV,) + blk.shape)
                off += blk.size
                seg = jnp.moveaxis(seg, 0, axis)
                got[name] = seg.reshape(seg.shape[:axis] + (-1,) + seg.shape[axis + 2:])
            got["e_conv_w"] = got["e_conv_w"][0]
        return got

    return fetch, token


def _grad_stage_of(name, layer):
    if name.startswith("e_"):
        return 4
    if name.startswith("o_"):
        return 1
    if name.startswith("ca_"):
        return 3 if layer == 0 else 1
    return 2 if layer == 0 else 0


GRAD_STAGES = 5
SMALL_ROWS = 16


def gradient_reducer(local, mom, var):
    me = (4 * lax.axis_index("x") + 2 * lax.axis_index("y") + lax.axis_index("c")).astype(jnp.int32).reshape(1)
    pending = []

    def start(stage, grads, carry):
        units = [u for u in _UNITS if _grad_stage_of(u[0], u[1]) == stage]
        parts, spans = [], []
        for name, layer, tr in units:
            key = _unit_key(name, tr)
            g = grads[(key, layer)] if name in _LAYERED else grads[key]
            part = g.reshape(4, 2, -1, RS_ROW)
            spans.append((part.shape[2], g.shape[0] // N_DEV, g.shape[1]))
            parts.append(part)
        if stage == GRAD_STAGES - 1:
            small = jnp.concatenate([_shard(grads[name], axis).reshape(N_DEV, -1) for name, axis in _SMALL_SHARDED], axis=1)
            small = jnp.pad(small, ((0, 0), (0, SMALL_ROWS * RS_ROW - small.shape[1])))
            parts.append(small.astype(BF16).reshape(4, 2, SMALL_ROWS, RS_ROW))
        pack = jnp.concatenate(parts, axis=2)
        pack = pack.reshape((N_DEV,) + pack.shape[2:])
        sems, own, land, carry = scatter_start(f"rs_start{stage}", pack, carry)
        pending.append((stage, units, spans, sems, own, land))
        return carry

    def finish(after):
        res, per_layer, small_flat = {}, {}, None
        for stage, units, spans, sems, own, land in pending:
            own, land = scatter_wait(f"rs_wait{stage}", own, land, sems, after)
            total = sum_shares(f"rs_sum{stage}", own, land, me)
            off = 0
            for (name, layer, tr), (rows, r, c) in zip(units, spans):
                per_layer.setdefault(name, {})[layer] = (total[off:off + rows].reshape(r, c), tr)
                off += rows
            if stage == GRAD_STAGES - 1:
                small_flat = total[off:off + SMALL_ROWS].reshape(-1)
        for name, by_layer in per_layer.items():
            pieces = [by_layer[i][0] for i in sorted(by_layer)]
            res[name] = adamw_units("adamw_" + name, pieces, by_layer[0][1], local[name], mom[name], var[name])
        off = 0
        for name, _ in _SMALL_SHARDED:
            blk = local[name]
            g = small_flat[off:off + blk.size].reshape(blk.shape)
            off += blk.size
            res[name] = (g,) + adamw_native("adamw_" + name, g, blk, mom[name], var[name])
        return res

    return start, finish


def replicated_start(grads, loss):
    pack = jnp.concatenate([_rows128(grads[name]) for name in _REPLICATED] + [_rows128(loss)], axis=0)
    sems, srcs, lands, token = gather_ici_start("ag_g_start", [[pack]], [False])
    return sems[0], srcs[0], lands[0], token


def replicated_finish(handle, after, w, mom, var):
    sems, srcs, lands, _ = handle
    (buf,) = gather_d2d("ag_g_d2d", gather_ici_wait("ag_g_wait", srcs, lands, sems, after))
    rows = srcs[0].shape[0]
    total = sum_slots("ag_g_sum", buf.reshape(N_DEV, rows, LANES))
    res, off = {}, 0
    for name in _REPLICATED:
        n = w[name].size
        nr = -(-n // (LANES * 8)) * 8
        g = total[off:off + nr].reshape(-1)[:n].reshape(w[name].shape)
        off += nr
        res[name] = (g,) + adamw_native("adamw_" + name, g, w[name], mom[name], var[name])
    return res, total[off, 0]


def kernel(x, mem, e_norm, e_w_in, e_gmlp_w, e_gmlp_b, e_conv_w, e_conv_b, e_conv_ln_g, e_conv_ln_b, e_w_out, o_norm, o_w_in, o_lam_re, o_lam_im, o_log_dt, o_b_re, o_b_im, o_c_re, o_c_im, o_d, o_w_out, ca_norm, ca_mem_norm, ca_wq, ca_wk, ca_wv, ca_wo, ffn_norm, ffn_w_gate, ffn_w_up, ffn_w_down, final_norm, loss_target, m_e_norm, m_e_w_in, m_e_gmlp_w, m_e_gmlp_b, m_e_conv_w, m_e_conv_b, m_e_conv_ln_g, m_e_conv_ln_b, m_e_w_out, m_o_norm, m_o_w_in, m_o_lam_re, m_o_lam_im, m_o_log_dt, m_o_b_re, m_o_b_im, m_o_c_re, m_o_c_im, m_o_d, m_o_w_out, m_ca_norm, m_ca_mem_norm, m_ca_wq, m_ca_wk, m_ca_wv, m_ca_wo, m_ffn_norm, m_ffn_w_gate, m_ffn_w_up, m_ffn_w_down, m_final_norm, v_e_norm, v_e_w_in, v_e_gmlp_w, v_e_gmlp_b, v_e_conv_w, v_e_conv_b, v_e_conv_ln_g, v_e_conv_ln_b, v_e_w_out, v_o_norm, v_o_w_in, v_o_lam_re, v_o_lam_im, v_o_log_dt, v_o_b_re, v_o_b_im, v_o_c_re, v_o_c_im, v_o_d, v_o_w_out, v_ca_norm, v_ca_mem_norm, v_ca_wq, v_ca_wk, v_ca_wv, v_ca_wo, v_ffn_norm, v_ffn_w_gate, v_ffn_w_up, v_ffn_w_down, v_final_norm):
    given = dict(locals())
    local = {k: given[k] for k in _ORDER}
    mom = {k: given["m_" + k] for k in _ORDER}
    var = {k: given["v_" + k] for k in _ORDER}

    w = {}
    w.update({
        "e_norm": e_norm, "e_gmlp_w": e_gmlp_w[0], "e_gmlp_b": e_gmlp_b.reshape(A_GROUPS, GMLP_BLOCK, 1),
        "e_conv_b": e_conv_b, "e_conv_ln_g": e_conv_ln_g, "e_conv_ln_b": e_conv_ln_b,
        "o_lam_re": o_lam_re[0], "o_lam_im": o_lam_im[0], "o_log_dt": o_log_dt[0], "o_b_re": o_b_re[0], "o_b_im": o_b_im[0],
        "o_c_re": o_c_re[0], "o_c_im": o_c_im[0], "ca_norm": ca_norm, "ca_mem_norm": ca_mem_norm, "ffn_norm": ffn_norm,
        "final_norm": final_norm.reshape(1, D_MODEL),
    })
    start_reduce, finish_reduce = gradient_reducer(local, mom, var)
    fetch, token = weight_fetcher(local)
    loss_part, grad_x, grads = local_step(x[0], mem[0], loss_target[0], w, fetch, start_reduce, token[0:1, 0:1])
    grads["final_norm"] = grads["final_norm"].reshape(D_MODEL)

    handle = replicated_start(grads, loss_part)
    res = finish_reduce(handle[3])
    rep, loss = replicated_finish(handle, res["ffn_w_down"][1], local, mom, var)
    res.update(rep)
    return (loss, grad_x[None], *[res[k][0] for k in _ORDER], *[res[k][1] for k in _ORDER],
            *[res[k][2] for k in _ORDER], *[res[k][3] for k in _ORDER])
```

```python
import jax
import jax.numpy as jnp
from jax import lax
from jax.experimental import pallas as pl
from jax.experimental.pallas import tpu as pltpu

F32 = jnp.float32
BF16 = jnp.bfloat16
S = jax.ShapeDtypeStruct

D_MODEL = 1024
A_WIDTH = 512
A_GROUPS = 4
GMLP_BLOCK = 128
CHUNK = 64
B_WIDTH = 512
IN_WIDTH = 2 * A_WIDTH + 2 * B_WIDTH
CONV_WIDTH = 31
CONV_PAD = 32
C_WIDTH = 512
C_GROUP_CH = 16
C_GROUPS = 32
C_STATE = 64
N_STATE = C_GROUPS * C_STATE
CA_HEADS = 4
CA_HEAD_DIM = 256
FFN_HIDDEN = 2816
EPS = 1e-6
ADAM_LR = 0.001
ADAM_B1 = 0.9
ADAM_B2 = 0.999
ADAM_EPS = 1e-08
ADAM_WD = 0.01
ADAM_STEP = 10
N_DEV = 8
LANES = 128
VMEM_LIMIT = 56 << 20
VMEM_BUDGET = 40 << 20
MM_TN_RESIDENT = 8 << 20
MESH = pl.DeviceIdType.MESH
ANY = pl.BlockSpec(memory_space=pl.ANY)


def _cp(*sem):
    return pltpu.CompilerParams(dimension_semantics=sem, vmem_limit_bytes=VMEM_LIMIT)


def _tile(n, pref):
    t = pref
    while n % t:
        t //= 2
    return t


def _bf(v):
    return v if v.dtype == BF16 else v.astype(BF16)


def _sigmoid(x):
    return 1.0 / (1.0 + jnp.exp(-x))


_GC = 0.7978845608028654


def _gelu(x):
    return 0.5 * x * (1.0 + jnp.tanh(_GC * (x + 0.044715 * x * x * x)))


def _gelu_grad(x):
    x2 = x * x
    t = jnp.tanh(_GC * (x + 0.044715 * x * x2))
    return 0.5 * (1.0 + t) + 0.5 * x * (1.0 - t * t) * _GC * (1.0 + 3.0 * 0.044715 * x2)


def _tspec(entry, tm):
    if isinstance(entry, tuple):
        arr, cb, width = entry
        return arr, pl.BlockSpec((tm, width), lambda i, cb=cb: (i, cb))
    return entry, pl.BlockSpec((tm, entry.shape[1]), lambda i: (i, 0))


def rows_call(name, fn, tiled, full, outs, accs, tm=256):
    pairs = [_tspec(e, tm) for e in tiled]
    arrs = [p[0] for p in pairs]
    rows = arrs[0].shape[0]
    tm = _tile(rows, tm)
    pairs = [_tspec(e, tm) for e in tiled]
    n_in = len(tiled) + len(full)
    n_out = len(outs)

    def body(*refs):
        vals = [r[...] for r in refs[:n_in]]
        o_refs = refs[n_in:n_in + n_out]
        a_refs = refs[n_in + n_out:]
        ov, av = fn(*vals)
        for r, v in zip(o_refs, ov):
            r[...] = v.astype(r.dtype)
        if a_refs:
            @pl.when(pl.program_id(0) == 0)
            def _():
                for r in a_refs:
                    r[...] = jnp.zeros(r.shape, r.dtype)
            for r, v in zip(a_refs, av):
                r[...] += v

    in_specs = [p[1] for p in pairs] + [pl.BlockSpec(a.shape, lambda i, nd=a.ndim: (0,) * nd) for a in full]
    out_specs = [pl.BlockSpec((tm, c), lambda i: (i, 0)) for c, _ in outs]
    out_specs += [pl.BlockSpec(s, lambda i, nd=len(s): (0,) * nd) for s in accs]
    out_shape = [S((rows, c), dt) for c, dt in outs] + [S(s, F32) for s in accs]
    return pl.pallas_call(body, grid=(rows // tm,), in_specs=in_specs, out_specs=out_specs, out_shape=out_shape,
                          compiler_params=_cp("arbitrary"), name=name)(*arrs, *full)


def mm_nn(name, m, n, pairs, n_acc, epi, outs, tiled=(), cols=(), rowv=(), sums=(), norm_gain=None):
    a_ops, a_slot, b_arrs, b_specs, idx, trans = [], [], [], [], [], []
    fixed = 0
    for pair in pairs:
        a, b, k = pair[:3]
        bt = len(pair) > 3
        arr, cb, kdim = a if isinstance(a, tuple) else (a, 0, a.shape[1])
        key = (id(arr), cb, kdim)
        if key not in [o[0] for o in a_ops]:
            a_ops.append((key, arr, cb, kdim))
        a_slot.append([o[0] for o in a_ops].index(key))
        b_arr, off = b if isinstance(b, tuple) else (b, 0)
        b_arrs.append(b_arr)
        if bt:
            assert off % n == 0 and b_arr.shape[1] == kdim
            b_specs.append(pl.BlockSpec((n, kdim), lambda i, o=off // n: (o, 0), pipeline_mode=pl.Buffered(1)))
        else:
            assert b_arr.shape[1] == n
            b_specs.append(pl.BlockSpec((kdim, n), lambda i, o=off: (o, 0), pipeline_mode=pl.Buffered(1)))
        fixed += kdim * n * b_arr.dtype.itemsize
        idx.append(k)
        trans.append(bt)
    per_row = sum(2 * kdim * arr.dtype.itemsize for _, arr, _, kdim in a_ops)
    per_row += sum(2 * n * t.dtype.itemsize for t in tiled) + sum(2 * n * jnp.dtype(dt).itemsize for dt in outs)
    cn = n if sums or cols else (512 if n % 512 == 0 else 256)
    per_row += (n_acc + 3) * cn * 4
    tm = next((t for t in (1024, 512, 256, 128) if m % t == 0 and fixed + t * per_row <= VMEM_BUDGET), _tile(m, 128))
    n_a, n_p, n_t = len(a_ops), len(pairs), len(tiled)
    n_in = n_a + n_p + n_t + len(cols) + len(rowv)
    normed = norm_gain is not None
    o0 = n_in + normed

    def body(*refs):
        a_vals = [None if normed and i == 0 else _bf(r[...]) for i, r in enumerate(refs[:n_a])]
        if normed:
            xv = refs[0][...]
            rv = lax.rsqrt(jnp.mean(xv * xv, axis=-1, keepdims=True) + EPS)
            a_vals[0] = (xv * rv * refs[n_in][...]).astype(BF16)
            refs[o0 + len(outs)][...] = a_vals[0]
            refs[o0 + len(outs) + 1][...] = rv
        for j in range(n // cn):
            cs = slice(j * cn, (j + 1) * cn)
            accs = [None] * n_acc
            for p in range(n_p):
                av, b_ref = a_vals[a_slot[p]], refs[n_a + p]
                if trans[p]:
                    d = lax.dot_general(av, _bf(b_ref[cs, :]), (((1,), (1,)), ((), ())), preferred_element_type=F32)
                else:
                    d = jnp.dot(av, _bf(b_ref[:, cs]), preferred_element_type=F32)
                accs[idx[p]] = d if accs[idx[p]] is None else accs[idx[p]] + d
            extra = [r[:, cs] for r in refs[n_a + n_p:n_a + n_p + n_t]] + [r[...] for r in refs[n_a + n_p + n_t:n_in - len(rowv)]]
            extra += [r[:, cs] for r in refs[n_in - len(rowv):n_in]]
            ov = epi(accs, *extra)
            for r, v in zip(refs[o0:o0 + len(outs)], ov):
                r[:, cs] = v.astype(r.dtype)
        sv = ov[len(outs):]
        if sums:
            s_refs = refs[o0 + len(outs) + 2 * normed:]

            @pl.when(pl.program_id(0) == 0)
            def _():
                for r in s_refs:
                    r[...] = jnp.zeros(r.shape, r.dtype)
            for r, v in zip(s_refs, sv):
                r[...] += v

    in_specs = [pl.BlockSpec((tm, kdim), lambda i, cb=cb: (i, cb)) for _, _, cb, kdim in a_ops] + b_specs
    in_specs += [pl.BlockSpec((tm, n), lambda i: (i, 0)) for _ in tiled]
    in_specs += [pl.BlockSpec((tm, 1), lambda i: (i, 0)) for _ in cols]
    in_specs += [pl.BlockSpec((1, n), lambda i: (0, 0)) for _ in rowv]
    out_specs = [pl.BlockSpec((tm, n), lambda i: (i, 0)) for _ in outs]
    out_shape = [S((m, n), dt) for dt in outs]
    gain = []
    if normed:
        k0 = a_ops[0][3]
        gain = [norm_gain]
        in_specs.append(pl.BlockSpec((1, k0), lambda i: (0, 0)))
        out_specs += [pl.BlockSpec((tm, k0), lambda i: (i, 0)), pl.BlockSpec((tm, 1), lambda i: (i, 0))]
        out_shape += [S((m, k0), BF16), S((m, 1), F32)]
    out_specs += [pl.BlockSpec(s, lambda i, nd=len(s): (0,) * nd) for s in sums]
    out_shape += [S(s, F32) for s in sums]
    return pl.pallas_call(body, grid=(m // tm,), in_specs=in_specs, out_specs=out_specs, out_shape=out_shape,
                          compiler_params=_cp("arbitrary" if sums else "parallel"),
                          name=name)(*[o[1] for o in a_ops], *b_arrs, *tiled, *cols, *rowv, *gain)


def mm_tn(name, a, b, out_dtype=BF16):
    if isinstance(a, tuple):
        a_arr, a_cb, m = a
    else:
        a_arr, a_cb, m = a, None, a.shape[1]
    if isinstance(b, tuple):
        b_arr, b_cb, n = b
    else:
        b_arr, b_cb, n = b, None, b.shape[1]
    t = a_arr.shape[0]
    whole_b = t * n * b_arr.dtype.itemsize <= MM_TN_RESIDENT and b_cb is None
    tn = n if whole_b else _tile(n, 512)
    tm = _tile(m, 512 if t * 512 * a_arr.dtype.itemsize * 2 + t * tn * b_arr.dtype.itemsize * 2 <= VMEM_BUDGET else 256)
    a_off = 0 if a_cb is None else a_cb * (m // tm)
    b_off = 0 if b_cb is None else b_cb * (n // tn)

    def body(a_ref, b_ref, o_ref):
        o_ref[...] = lax.dot_general(_bf(a_ref[...]), _bf(b_ref[...]), (((0,), (0,)), ((), ())),
                                     preferred_element_type=F32).astype(o_ref.dtype)

    if whole_b:
        b_spec = pl.BlockSpec((t, n), lambda i, j: (0, 0), pipeline_mode=pl.Buffered(1))
    else:
        b_spec = pl.BlockSpec((t, tn), lambda i, j: (0, j + b_off))
    return pl.pallas_call(
        body, grid=(m // tm, n // tn),
        in_specs=[pl.BlockSpec((t, tm), lambda i, j: (0, i + a_off)), b_spec],
        out_specs=pl.BlockSpec((tm, tn), lambda i, j: (i, j)), out_shape=S((m, n), out_dtype),
        compiler_params=_cp("parallel", "parallel"), name=name)(a_arr, b_arr)


def rms_bwd_gain_only(name, dxn, x, r):
    def fn(dv, xv, rv):
        return [], [jnp.sum(dv * xv * rv, axis=0, keepdims=True)]
    return rows_call(name, fn, [dxn, x, r], [], [], [(1, x.shape[1])])[0]


def _final_loss_epi(accs, res, tv, g):
    xv = res + accs[0]
    d = xv.shape[-1]
    r = lax.rsqrt(jnp.mean(xv * xv, axis=-1, keepdims=True) + EPS)
    xh = xv * r
    err = xh * g - tv
    dy = err * (1.0 / d)
    w = dy * g
    dx = r * (w - xh * jnp.mean(w * xh, axis=-1, keepdims=True))
    part = jnp.sum(jnp.sum(err * err, axis=-1, keepdims=True), axis=0, keepdims=True) * (0.5 / d)
    return [dx, dx, jnp.sum(dy * xh, axis=0, keepdims=True), part]


def _gmlp_mask():
    row = lax.broadcasted_iota(jnp.int32, (GMLP_BLOCK, GMLP_BLOCK), 0) // CHUNK
    col = lax.broadcasted_iota(jnp.int32, (GMLP_BLOCK, GMLP_BLOCK), 1) // CHUNK
    return col <= row


def _ln_plain(v):
    mu = jnp.mean(v, axis=-1, keepdims=True)
    vc = v - mu
    rstd = lax.rsqrt(jnp.mean(vc * vc, axis=-1, keepdims=True) + EPS)
    return vc * rstd, rstd


def gmlp_fwd(name, proj, w, b, tm=512):
    t = proj.shape[0]
    tm = _tile(t, tm)

    def body(au_ref, av_ref, w_ref, b_ref, o_ref):
        mask = _gmlp_mask()
        u = _gelu(au_ref[...])
        vn, _ = _ln_plain(_gelu(av_ref[...]))
        vnb = _bf(vn)
        for g in range(A_GROUPS):
            wg = _bf(jnp.where(mask, w_ref[g], 0.0))
            cs = slice(g * GMLP_BLOCK, (g + 1) * GMLP_BLOCK)
            for n in range(tm // GMLP_BLOCK):
                rs = slice(n * GMLP_BLOCK, (n + 1) * GMLP_BLOCK)
                sg = jnp.dot(wg, vnb[rs, cs], preferred_element_type=F32) + b_ref[g]
                o_ref[rs, cs] = (u[rs, cs] * sg).astype(o_ref.dtype)

    return pl.pallas_call(
        body, grid=(t // tm,),
        in_specs=[pl.BlockSpec((tm, A_WIDTH), lambda i: (i, 0)), pl.BlockSpec((tm, A_WIDTH), lambda i: (i, 1)),
                  pl.BlockSpec(w.shape, lambda i: (0, 0, 0)), pl.BlockSpec(b.shape, lambda i: (0, 0, 0))],
        out_specs=pl.BlockSpec((tm, A_WIDTH), lambda i: (i, 0)), out_shape=S((t, A_WIDTH), BF16),
        compiler_params=_cp("parallel"), name=name)(proj, proj, w, b)


def gmlp_bwd(name, proj, dxb, w_out, w, b, tm=512):
    t = proj.shape[0]
    tm = _tile(t, tm)

    def body(au_ref, av_ref, dx_ref, wo_ref, w_ref, b_ref, dp_ref, dw_ref, db_ref):
        @pl.when(pl.program_id(0) == 0)
        def _():
            dw_ref[...] = jnp.zeros(dw_ref.shape, F32)
            db_ref[...] = jnp.zeros(db_ref.shape, F32)

        mask = _gmlp_mask()
        au = au_ref[...]
        av = av_ref[...]
        u = _gelu(au)
        vn, rstd = _ln_plain(_gelu(av))
        vnb = _bf(vn)
        dout = lax.dot_general(dx_ref[...], wo_ref[0:A_WIDTH, :], _NT, preferred_element_type=F32)
        dvn_cols = []
        for g in range(A_GROUPS):
            wm = jnp.where(mask, w_ref[g], 0.0)
            wg = _bf(wm)
            wgt = _bf(wm.T)
            cs = slice(g * GMLP_BLOCK, (g + 1) * GMLP_BLOCK)
            dwg = jnp.zeros((GMLP_BLOCK, GMLP_BLOCK), F32)
            dbg = jnp.zeros((GMLP_BLOCK, 1), F32)
            dvn_rows = []
            for n in range(tm // GMLP_BLOCK):
                rs = slice(n * GMLP_BLOCK, (n + 1) * GMLP_BLOCK)
                sg = jnp.dot(wg, vnb[rs, cs], preferred_element_type=F32) + b_ref[g]
                dp_ref[rs, cs] = (dout[rs, cs] * sg * _gelu_grad(au[rs, cs])).astype(dp_ref.dtype)
                dsg = dout[rs, cs] * u[rs, cs]
                dsgb = _bf(dsg)
                dbg = dbg + jnp.sum(dsg, axis=1, keepdims=True)
                dwg = dwg + lax.dot_general(dsgb, vnb[rs, cs], (((1,), (1,)), ((), ())), preferred_element_type=F32)
                dvn_rows.append(jnp.dot(wgt, dsgb, preferred_element_type=F32))
            dw_ref[g] += jnp.where(mask, dwg, 0.0)
            db_ref[g] += dbg
            dvn_cols.append(jnp.concatenate(dvn_rows, axis=0))
        dvn = jnp.concatenate(dvn_cols, axis=1)
        dv = rstd * (dvn - jnp.mean(dvn, axis=-1, keepdims=True) - vn * jnp.mean(dvn * vn, axis=-1, keepdims=True))
        dp_ref[:, A_WIDTH:] = (dv * _gelu_grad(av)).astype(dp_ref.dtype)

    return pl.pallas_call(
        body, grid=(t // tm,),
        in_specs=[pl.BlockSpec((tm, A_WIDTH), lambda i: (i, 0)), pl.BlockSpec((tm, A_WIDTH), lambda i: (i, 1)),
                  pl.BlockSpec((tm, dxb.shape[1]), lambda i: (i, 0)), pl.BlockSpec(w_out.shape, lambda i: (0, 0)),
                  pl.BlockSpec(w.shape, lambda i: (0, 0, 0)), pl.BlockSpec(b.shape, lambda i: (0, 0, 0))],
        out_specs=[pl.BlockSpec((tm, 2 * A_WIDTH), lambda i: (i, 0)),
                   pl.BlockSpec(w.shape, lambda i: (0, 0, 0)), pl.BlockSpec(b.shape, lambda i: (0, 0, 0))],
        out_shape=[S((t, 2 * A_WIDTH), BF16), S(w.shape, F32), S(b.shape, F32)],
        compiler_params=_cp("arbitrary"), name=name)(proj, proj, dxb, w_out, w, b)


CONV_ROWS = 256


def conv_fwd(name, proj, w, cb):
    t = proj.shape[0]
    tc = LANES
    rows = _tile(t, CONV_ROWS)
    a_cb, g_cb = 2 * A_WIDTH // tc, (2 * A_WIDTH + B_WIDTH) // tc

    def body(a_ref, g_ref, w_ref, cb_ref, o_ref, hpad):
        hpad[0:CONV_PAD, :] = jnp.zeros((CONV_PAD, tc), F32)

        def fill(i, _):
            r0 = pl.multiple_of(i * rows, rows)
            hpad[pl.ds(CONV_PAD + r0, rows), :] = a_ref[pl.ds(r0, rows), :] * _sigmoid(g_ref[pl.ds(r0, rows), :])
            return 0
        lax.fori_loop(0, t // rows, fill, 0)

        def conv(i, _):
            r0 = pl.multiple_of(i * rows, rows)
            win = hpad[pl.ds(r0, rows + CONV_PAD), :]
            acc = jnp.zeros((rows, tc), F32) + cb_ref[...]
            for b in range(SUB):
                wb = win if b == 0 else pltpu.roll(win, b, 0)
                for a in range(CONV_PAD // SUB):
                    k = CONV_WIDTH - 1 - (SUB * a + b)
                    if k >= 0:
                        lo = CONV_PAD - SUB * a
                        acc = acc + wb[lo:lo + rows, :] * w_ref[k:k + 1, :]
            o_ref[pl.ds(r0, rows), :] = acc
            return 0
        lax.fori_loop(0, t // rows, conv, 0)

    return pl.pallas_call(
        body, grid=(B_WIDTH // tc,),
        in_specs=[pl.BlockSpec((t, tc), lambda j: (0, a_cb + j)), pl.BlockSpec((t, tc), lambda j: (0, g_cb + j)),
                  pl.BlockSpec((CONV_WIDTH, tc), lambda j: (0, j)), pl.BlockSpec((1, tc), lambda j: (0, j))],
        out_specs=pl.BlockSpec((t, tc), lambda j: (0, j)), out_shape=S((t, B_WIDTH), F32),
        scratch_shapes=[pltpu.VMEM((t + CONV_PAD, tc), F32)],
        compiler_params=_cp("parallel"), name=name)(proj, proj, w, cb)


def conv_bwd(name, proj, dhc, w):
    t = proj.shape[0]
    tc = LANES
    rows = _tile(t, CONV_ROWS)
    a_cb, g_cb = 2 * A_WIDTH // tc, (2 * A_WIDTH + B_WIDTH) // tc
    win_rows = rows + CONV_PAD

    def body(a_ref, g_ref, d_ref, w_ref, da_ref, dg_ref, dw_ref, dcb_ref, hpad, dpad, dwacc):
        hpad[0:CONV_PAD, :] = jnp.zeros((CONV_PAD, tc), F32)
        dpad[t:t + CONV_PAD, :] = jnp.zeros((CONV_PAD, tc), F32)
        dwacc[...] = jnp.zeros(dwacc.shape, F32)

        def fill(i, _):
            r0 = pl.multiple_of(i * rows, rows)
            hpad[pl.ds(CONV_PAD + r0, rows), :] = a_ref[pl.ds(r0, rows), :] * _sigmoid(g_ref[pl.ds(r0, rows), :])
            dpad[pl.ds(r0, rows), :] = d_ref[pl.ds(r0, rows), :]
            return 0
        lax.fori_loop(0, t // rows, fill, 0)

        def step(i, dcb):
            r0 = pl.multiple_of(i * rows, rows)
            hwin = hpad[pl.ds(r0, win_rows), :]
            dwin = dpad[pl.ds(r0, win_rows), :]
            dchunk = dwin[:rows, :]
            dh = jnp.zeros((rows, tc), F32)
            for b in range(SUB):
                hb = hwin if b == 0 else pltpu.roll(hwin, b, 0)
                db = dwin if b == 0 else pltpu.roll(dwin, win_rows - b, 0)
                for a in range(CONV_PAD // SUB):
                    k = CONV_WIDTH - 1 - (SUB * a + b)
                    if k >= 0:
                        dh = dh + db[SUB * a:SUB * a + rows, :] * w_ref[k:k + 1, :]
                        lo = CONV_PAD - SUB * a
                        prod = dchunk * hb[lo:lo + rows, :]
                        dwacc[k] += jnp.sum(prod.reshape(rows // 8, 8, tc), axis=0)
            a = a_ref[pl.ds(r0, rows), :]
            sg = _sigmoid(g_ref[pl.ds(r0, rows), :])
            da_ref[pl.ds(r0, rows), :] = (dh * sg).astype(da_ref.dtype)
            dg_ref[pl.ds(r0, rows), :] = (dh * a * sg * (1.0 - sg)).astype(dg_ref.dtype)
            return dcb + jnp.sum(dchunk, axis=0, keepdims=True)
        dcb = lax.fori_loop(0, t // rows, step, jnp.zeros((1, tc), F32))
        dcb_ref[...] = dcb
        for k in range(CONV_WIDTH):
            dw_ref[k:k + 1, :] = jnp.sum(dwacc[k], axis=0, keepdims=True)

    return pl.pallas_call(
        body, grid=(B_WIDTH // tc,),
        in_specs=[pl.BlockSpec((t, tc), lambda j: (0, a_cb + j)), pl.BlockSpec((t, tc), lambda j: (0, g_cb + j)),
                  pl.BlockSpec((t, tc), lambda j: (0, j)), pl.BlockSpec((CONV_WIDTH, tc), lambda j: (0, j))],
        out_specs=[pl.BlockSpec((t, tc), lambda j: (0, j)), pl.BlockSpec((t, tc), lambda j: (0, j)),
                   pl.BlockSpec((CONV_WIDTH, tc), lambda j: (0, j)), pl.BlockSpec((1, tc), lambda j: (0, j))],
        out_shape=[S((t, B_WIDTH), BF16), S((t, B_WIDTH), BF16), S((CONV_WIDTH, B_WIDTH), F32), S((1, B_WIDTH), F32)],
        scratch_shapes=[pltpu.VMEM((t + CONV_PAD, tc), F32), pltpu.VMEM((t + CONV_PAD, tc), F32),
                        pltpu.VMEM((CONV_WIDTH, 8, tc), F32)],
        compiler_params=_cp("parallel"), name=name)(proj, proj, dhc, w)


def ln_silu_fwd(name, hc, g, b):
    def fn(h, gv, bv):
        y, _ = _ln_plain(h)
        z = y * gv + bv
        return [z * _sigmoid(z)], []
    return rows_call(name, fn, [hc], [g, b], [(hc.shape[1], BF16)], [])[0]


def ln_silu_bwd(name, hc, dxb, w_out, g, b):
    c = hc.shape[1]

    def fn(h, dxv, wv, gv, bv):
        dout = lax.dot_general(dxv, wv[A_WIDTH:, :], _NT, preferred_element_type=F32)
        y, rstd = _ln_plain(h)
        z = y * gv + bv
        s = _sigmoid(z)
        dz = dout * s * (1.0 + z * (1.0 - s))
        dyv = dz * gv
        dh = rstd * (dyv - jnp.mean(dyv, axis=-1, keepdims=True) - y * jnp.mean(dyv * y, axis=-1, keepdims=True))
        return [dh], [jnp.sum(dz * y, axis=0, keepdims=True), jnp.sum(dz, axis=0, keepdims=True)]

    return rows_call(name, fn, [hc, dxb], [w_out, g, b], [(c, F32)], [(1, c), (1, c)])


_NT = (((1,), (1,)), ((), ()))
_TN = (((0,), (0,)), ((), ()))


def attn_fwd(name, x, gain, wq, k, v, wo, tm=512):
    t, d = x.shape
    m = k.shape[0]
    tm = _tile(t, tm)
    scale = CA_HEAD_DIM ** -0.5

    def body(x_ref, g_ref, wq_ref, k_ref, v_ref, wo_ref, x1_ref, xn_ref, r_ref, q_ref, o_ref):
        xv = x_ref[...]
        rv = lax.rsqrt(jnp.mean(xv * xv, axis=-1, keepdims=True) + EPS)
        xn = (xv * rv * g_ref[...]).astype(BF16)
        xn_ref[...] = xn
        r_ref[...] = rv
        q_ref[...] = jnp.dot(xn, wq_ref[...], preferred_element_type=F32).astype(BF16)
        for h in range(CA_HEADS):
            cs = slice(h * CA_HEAD_DIM, (h + 1) * CA_HEAD_DIM)
            s = lax.dot_general(q_ref[:, cs], k_ref[:, cs], _NT, preferred_element_type=F32) * scale
            e = jnp.exp(s - jnp.max(s, axis=-1, keepdims=True))
            p = e / jnp.sum(e, axis=-1, keepdims=True)
            o_ref[:, cs] = jnp.dot(_bf(p), v_ref[:, cs], preferred_element_type=F32).astype(o_ref.dtype)
        x1_ref[...] = xv + jnp.dot(o_ref[...], wo_ref[...], preferred_element_type=F32)

    def whole(a):
        return pl.BlockSpec(a.shape, lambda i: (0, 0), pipeline_mode=pl.Buffered(1))

    rows = pl.BlockSpec((tm, d), lambda i: (i, 0))
    col = pl.BlockSpec((tm, 1), lambda i: (i, 0))
    return pl.pallas_call(
        body, grid=(t // tm,),
        in_specs=[rows, whole(gain), whole(wq), whole(k), whole(v), whole(wo)],
        out_specs=[rows, rows, col, rows, rows],
        out_shape=[S((t, d), F32), S((t, d), BF16), S((t, 1), F32), S((t, d), BF16), S((t, d), BF16)],
        compiler_params=_cp("parallel"), name=name)(x, gain, wq, k, v, wo)


def attn_bwd(name, dx, dxb, x, r, gain, q, k, v, wq, wo, tm=512):
    t, d = q.shape
    m = k.shape[0]
    tm = _tile(t, tm)
    scale = CA_HEAD_DIM ** -0.5

    def body(dx_ref, dxb_ref, x_ref, r_ref, g_ref, q_ref, k_ref, v_ref, wq_ref, wo_ref,
             dxo_ref, dxbo_ref, dq_ref, dk_ref, dv_ref, dg_ref, do_s):
        @pl.when(pl.program_id(0) == 0)
        def _():
            dk_ref[...] = jnp.zeros(dk_ref.shape, F32)
            dv_ref[...] = jnp.zeros(dv_ref.shape, F32)
            dg_ref[...] = jnp.zeros(dg_ref.shape, F32)

        do_s[...] = lax.dot_general(dxb_ref[...], wo_ref[...], _NT, preferred_element_type=F32).astype(BF16)
        for h in range(CA_HEADS):
            cs = slice(h * CA_HEAD_DIM, (h + 1) * CA_HEAD_DIM)
            qh, kh, vh, doh = q_ref[:, cs], k_ref[:, cs], v_ref[:, cs], do_s[:, cs]
            s = lax.dot_general(qh, kh, _NT, preferred_element_type=F32) * scale
            e = jnp.exp(s - jnp.max(s, axis=-1, keepdims=True))
            p = e / jnp.sum(e, axis=-1, keepdims=True)
            pb = _bf(p)
            dv_ref[:, cs] += lax.dot_general(pb, doh, _TN, preferred_element_type=F32)
            dp = lax.dot_general(doh, vh, _NT, preferred_element_type=F32)
            ds = _bf(p * (dp - jnp.sum(dp * p, axis=-1, keepdims=True)) * scale)
            dq_ref[:, cs] = jnp.dot(ds, kh, preferred_element_type=F32).astype(dq_ref.dtype)
            dk_ref[:, cs] += lax.dot_general(ds, qh, _TN, preferred_element_type=F32)
        dxn = lax.dot_general(dq_ref[...], wq_ref[...], _NT, preferred_element_type=F32)
        xh = x_ref[...] * r_ref[...]
        wv = dxn * g_ref[...]
        dxo = dx_ref[...] + r_ref[...] * (wv - xh * jnp.mean(wv * xh, axis=-1, keepdims=True))
        dxo_ref[...] = dxo
        dxbo_ref[...] = dxo.astype(BF16)
        dg_ref[...] += jnp.sum(dxn * xh, axis=0, keepdims=True)

    def whole(a):
        return pl.BlockSpec(a.shape, lambda i: (0, 0), pipeline_mode=pl.Buffered(1))

    rows = pl.BlockSpec((tm, d), lambda i: (i, 0))
    col = pl.BlockSpec((tm, 1), lambda i: (i, 0))
    acc = pl.BlockSpec((m, d), lambda i: (0, 0))
    return pl.pallas_call(
        body, grid=(t // tm,),
        in_specs=[rows, rows, rows, col, whole(gain), rows, whole(k), whole(v), whole(wq), whole(wo)],
        out_specs=[rows, rows, rows, acc, acc, pl.BlockSpec((1, d), lambda i: (0, 0))],
        out_shape=[S((t, d), F32), S((t, d), BF16), S((t, d), BF16), S((m, d), F32), S((m, d), F32), S((1, d), F32)],
        scratch_shapes=[pltpu.VMEM((tm, d), BF16)],
        compiler_params=_cp("arbitrary"), name=name)(dx, dxb, x, r, gain, q, k, v, wq, wo)


SUB = 8
S5_ROWS = 256


S5_BLOCKS = 4
BLOCK_CH = C_WIDTH // S5_BLOCKS
BLOCK_ST = N_STATE // S5_BLOCKS
_S5_BLOCKS = tuple((slice(BLOCK_CH * q, BLOCK_CH * (q + 1)), slice(BLOCK_ST * q, BLOCK_ST * (q + 1)),
                    slice(N_STATE + BLOCK_ST * q, N_STATE + BLOCK_ST * (q + 1))) for q in range(S5_BLOCKS))
_HI = lax.Precision.HIGHEST
_GP = (C_GROUPS, C_STATE)
_RP = (C_WIDTH, C_STATE)


def _zoh(lr, li, ldt):
    dt = jnp.exp(ldt)
    mag = jnp.exp(lr * dt)
    ar = mag * jnp.cos(li * dt)
    ai = mag * jnp.sin(li * dt)
    den = lr * lr + li * li
    qr = ((ar - 1.0) * lr + ai * li) / den
    qi = (ai * lr - (ar - 1.0) * li) / den
    return dt, ar, ai, den, qr, qi


def _per_channel(v):
    return jnp.broadcast_to(v[:, None, :], (C_GROUPS, C_GROUP_CH, C_STATE)).reshape(_RP)


def _same_group(shape, row_per_group, col_per_group):
    rows = lax.broadcasted_iota(jnp.int32, shape, 0) // row_per_group
    cols = lax.broadcasted_iota(jnp.int32, shape, 1) // col_per_group
    return rows == cols


def _spread(shape, axis):
    long = lax.broadcasted_iota(jnp.int32, shape, axis) % C_STATE
    short = lax.broadcasted_iota(jnp.int32, shape, 1 - axis)
    return long == short


def s5_discretise(name, lam_re, lam_im, log_dt, bt_re, bt_im):
    def body(lr_ref, li_ref, ldt_ref, btr_ref, bti_ref, a_ref, bbr_ref, bbi_ref):
        _, ar, ai, _, qr, qi = _zoh(lr_ref[...], li_ref[...], ldt_ref[...])
        a_ref[0] = ar
        a_ref[1] = ai
        q2r, q2i = _per_channel(qr), _per_channel(qi)
        btr, bti = btr_ref[...], bti_ref[...]
        bbr_ref[...] = q2r * btr - q2i * bti
        bbi_ref[...] = q2r * bti + q2i * btr

    return pl.pallas_call(body, out_shape=[S((2,) + _GP, F32), S(_RP, F32), S(_RP, F32)],
                          name=name)(lam_re, lam_im, log_dt, bt_re, bt_im)


def s5_operands(name, a, bbr, bbi, c2r, c2i, ctr, cti):
    ns = N_STATE

    def body(a_ref, bbr_ref, bbi_ref, c2r_ref, c2i_ref, ctr_ref, cti_ref, pw_ref, qw_ref, mb_ref, mc_ref, mct_ref):
        ar, ai = a_ref[0:1, :], a_ref[1:2, :]
        pows = [(ar, ai)]
        for _ in range(SUB - 1):
            pr, pi = pows[-1]
            pows.append((pr * ar - pi * ai, pr * ai + pi * ar))
        rows = lax.broadcasted_iota(jnp.int32, (SUB, ns), 0)

        def rows_of(v):
            return jnp.broadcast_to(v, (SUB, ns))

        for k, s in enumerate((1, 2, 4)):
            pr, pi = rows_of(pows[s - 1][0]), rows_of(pows[s - 1][1])
            pw_ref[k, 0] = jnp.where(rows >= s, pr, 0.0)
            pw_ref[k, 1] = jnp.where(rows >= s, pi, 0.0)
            qw_ref[k, 0] = jnp.where(rows + s <= SUB - 1, pr, 0.0)
            qw_ref[k, 1] = jnp.where(rows + s <= SUB - 1, -pi, 0.0)
        fr = fi = br = bi = jnp.zeros((SUB, ns), F32)
        for i in range(SUB):
            fr = jnp.where(rows == i, rows_of(pows[i][0]), fr)
            fi = jnp.where(rows == i, rows_of(pows[i][1]), fi)
            br = jnp.where(rows == i, rows_of(pows[SUB - 1 - i][0]), br)
            bi = jnp.where(rows == i, rows_of(-pows[SUB - 1 - i][1]), bi)
        pw_ref[3, 0], pw_ref[3, 1], qw_ref[3, 0], qw_ref[3, 1] = fr, fi, br, bi

        wide = _spread((C_STATE, ns), 1).astype(BF16)
        tall = _spread((ns, C_STATE), 0).astype(BF16)
        in_rows = _same_group((C_WIDTH, ns), C_GROUP_CH, C_STATE)
        in_cols = _same_group((ns, C_WIDTH), C_STATE, C_GROUP_CH)

        def across(v, sign=1.0):
            return jnp.where(in_rows, sign * jnp.dot(_bf(v), wide, preferred_element_type=F32), 0.0).astype(BF16)

        def down(vt, sign=1.0):
            return jnp.where(in_cols, sign * jnp.dot(tall, _bf(vt), preferred_element_type=F32), 0.0).astype(BF16)

        mb_ref[:, 0:ns] = across(bbr_ref[...])
        mb_ref[:, ns:2 * ns] = across(bbi_ref[...])
        mct_ref[:, 0:ns] = across(c2r_ref[...])
        mct_ref[:, ns:2 * ns] = across(c2i_ref[...], -1.0)
        mc_ref[0:ns, :] = down(ctr_ref[...])
        mc_ref[ns:2 * ns, :] = down(cti_ref[...], -1.0)

    return pl.pallas_call(
        body, out_shape=[S((4, 2, SUB, ns), F32), S((4, 2, SUB, ns), F32), S((C_WIDTH, 2 * ns), BF16),
                         S((2 * ns, C_WIDTH), BF16), S((C_WIDTH, 2 * ns), BF16)],
        compiler_params=pltpu.CompilerParams(vmem_limit_bytes=VMEM_LIMIT), name=name)(a, bbr, bbi, c2r, c2i, ctr, cti)


def s5_block_grads(name, u, lamb, xsb, dyb):
    t = u.shape[0]

    def mb_body(u_ref, lr_ref, li_ref, o_ref):
        ub = _bf(u_ref[...])
        o_ref[:, 0:BLOCK_ST] = lax.dot_general(ub, lr_ref[...], _TN, preferred_element_type=F32)
        o_ref[:, BLOCK_ST:2 * BLOCK_ST] = lax.dot_general(ub, li_ref[...], _TN, preferred_element_type=F32)

    d_mb = pl.pallas_call(
        mb_body, grid=(S5_BLOCKS,),
        in_specs=[pl.BlockSpec((t, BLOCK_CH), lambda q: (0, q)), pl.BlockSpec((t, BLOCK_ST), lambda q: (0, q)),
                  pl.BlockSpec((t, BLOCK_ST), lambda q: (0, S5_BLOCKS + q))],
        out_specs=pl.BlockSpec((BLOCK_CH, 2 * BLOCK_ST), lambda q: (q, 0)), out_shape=S((C_WIDTH, 2 * BLOCK_ST), F32),
        compiler_params=_cp("parallel"), name=name + "_b")(u, lamb, lamb)

    def mc_body(x_ref, dy_ref, o_ref):
        o_ref[...] = lax.dot_general(x_ref[...], dy_ref[...], _TN, preferred_element_type=F32)

    d_mc = pl.pallas_call(
        mc_body, grid=(2, S5_BLOCKS),
        in_specs=[pl.BlockSpec((t, BLOCK_ST), lambda p, q: (0, p * S5_BLOCKS + q)), pl.BlockSpec((t, BLOCK_CH), lambda p, q: (0, q))],
        out_specs=pl.BlockSpec((BLOCK_ST, BLOCK_CH), lambda p, q: (p * S5_BLOCKS + q, 0)),
        out_shape=S((2 * N_STATE, BLOCK_CH), F32), compiler_params=_cp("parallel", "parallel"), name=name + "_c")(xsb, dyb)
    return d_mb, d_mc


def s5_param_grads(name, d_mb, d_mc, da, lam_re, lam_im, log_dt, bt_re, bt_im):
    ns = N_STATE

    def body(dmb_ref, dmc_ref, da_ref, lr_ref, li_ref, ldt_ref, btr_ref, bti_ref,
             glr_ref, gli_ref, gdt_ref, gbr_ref, gbi_ref, gcr_ref, gci_ref):
        lr, li = lr_ref[...], li_ref[...]
        dt, ar, ai, den, qr, qi = _zoh(lr, li, ldt_ref[...])
        per_block = C_GROUPS // S5_BLOCKS
        wide = _spread((C_STATE, BLOCK_ST), 1).astype(F32)
        tall = _spread((BLOCK_ST, C_STATE), 0).astype(F32)
        rows = lax.broadcasted_iota(jnp.int32, (C_WIDTH, BLOCK_ST), 0) // C_GROUP_CH % per_block
        in_rows = rows == lax.broadcasted_iota(jnp.int32, (C_WIDTH, BLOCK_ST), 1) // C_STATE
        in_cols = _same_group((BLOCK_ST, BLOCK_CH), C_STATE, C_GROUP_CH)

        def fold_rows(v):
            return lax.dot_general(jnp.where(in_rows, v, 0.0), wide, (((1,), (1,)), ((), ())), precision=_HI,
                                   preferred_element_type=F32)

        def fold_cols(v):
            return lax.dot_general(jnp.where(in_cols, v, 0.0), tall, (((0,), (0,)), ((), ())), precision=_HI,
                                   preferred_element_type=F32)

        for cs, s_re, s_im in _S5_BLOCKS:
            gcr_ref[cs, :] = fold_cols(dmc_ref[s_re, :])
            gci_ref[cs, :] = -fold_cols(dmc_ref[s_im, :])
        gbbr = fold_rows(dmb_ref[:, 0:BLOCK_ST])
        gbbi = fold_rows(dmb_ref[:, BLOCK_ST:2 * BLOCK_ST])
        btr, bti = btr_ref[...], bti_ref[...]
        q2r, q2i = _per_channel(qr), _per_channel(qi)
        gbr_ref[...] = q2r * gbbr + q2i * gbbi
        gbi_ref[...] = q2r * gbbi - q2i * gbbr

        def per_group(v):
            return jnp.sum(v.reshape(C_GROUPS, C_GROUP_CH, C_STATE), axis=1)

        gqr = per_group(btr * gbbr + bti * gbbi)
        gqi = per_group(btr * gbbi - bti * gbbr)
        ilr, ili = lr / den, li / den
        gar = da_ref[0] + ilr * gqr - ili * gqi
        gai = da_ref[1] + ilr * gqi + ili * gqr
        sr = (qr * lr + qi * li) / den
        si = (qi * lr - qr * li) / den
        gzr = ar * gar + ai * gai
        gzi = ar * gai - ai * gar
        glr_ref[...] = -sr * gqr - si * gqi + dt * gzr
        gli_ref[...] = -sr * gqi + si * gqr + dt * gzi
        gdt_ref[...] = jnp.sum(lr * gzr + li * gzi, axis=1, keepdims=True) * dt

    return pl.pallas_call(
        body, out_shape=[S(_GP, F32), S(_GP, F32), S((C_GROUPS, 1), F32), S(_RP, F32), S(_RP, F32), S(_RP, F32), S(_RP, F32)],
        compiler_params=pltpu.CompilerParams(vmem_limit_bytes=VMEM_LIMIT), name=name,
    )(d_mb, d_mc, da, lam_re, lam_im, log_dt, bt_re, bt_im)


def _cmul_add(xr, xi, pr, pi, zr, zi):
    return xr + pr * zr - pi * zi, xi + pr * zi + pi * zr


def s5_fwd(name, u, mb, mc, pw, dskip):
    t = u.shape[0]
    tm = _tile(t, S5_ROWS)
    ns = N_STATE

    def body(u_ref, mb_ref, mc_ref, pw_ref, d_ref, gy_ref, y_ref, xs_ref, xb_ref, carry):
        @pl.when(pl.program_id(0) == 0)
        def _():
            carry[...] = jnp.zeros(carry.shape, F32)

        uv = u_ref[...]
        ub = _bf(uv)
        for cs, s_re, s_im in _S5_BLOCKS:
            xs_ref[:, s_re] = jnp.dot(ub[:, cs], mb_ref[cs, s_re], preferred_element_type=F32)
            xs_ref[:, s_im] = jnp.dot(ub[:, cs], mb_ref[cs, s_im], preferred_element_type=F32)

        def group(i, _):
            r0 = pl.multiple_of(i * SUB, SUB)
            xr = xs_ref[pl.ds(r0, SUB), 0:ns]
            xi = xs_ref[pl.ds(r0, SUB), ns:2 * ns]
            for k, s in enumerate((1, 2, 4)):
                xr, xi = _cmul_add(xr, xi, pw_ref[k, 0], pw_ref[k, 1], pltpu.roll(xr, s, 0), pltpu.roll(xi, s, 0))
            xr, xi = _cmul_add(xr, xi, pw_ref[3, 0], pw_ref[3, 1], carry[0], carry[1])
            xs_ref[pl.ds(r0, SUB), 0:ns] = xr
            xs_ref[pl.ds(r0, SUB), ns:2 * ns] = xi
            carry[0] = jnp.broadcast_to(xr[SUB - 1:SUB, :], (SUB, ns))
            carry[1] = jnp.broadcast_to(xi[SUB - 1:SUB, :], (SUB, ns))
            return 0
        lax.fori_loop(0, tm // SUB, group, 0)

        xb_ref[...] = _bf(xs_ref[...])
        for cs, s_re, s_im in _S5_BLOCKS:
            y = (jnp.dot(xb_ref[:, s_re], mc_ref[s_re, cs], preferred_element_type=F32)
                 + jnp.dot(xb_ref[:, s_im], mc_ref[s_im, cs], preferred_element_type=F32) + d_ref[:, cs] * uv[:, cs])
            y_ref[:, cs] = y
            gy_ref[:, cs] = _gelu(y).astype(gy_ref.dtype)

    c = u.shape[1]
    return pl.pallas_call(
        body, grid=(t // tm,),
        in_specs=[pl.BlockSpec((tm, c), lambda i: (i, 0)), pl.BlockSpec(mb.shape, lambda i: (0, 0)),
                  pl.BlockSpec(mc.shape, lambda i: (0, 0)), pl.BlockSpec(pw.shape, lambda i: (0, 0, 0, 0)),
                  pl.BlockSpec((1, c), lambda i: (0, 0))],
        out_specs=[pl.BlockSpec((tm, c), lambda i: (i, 0)), pl.BlockSpec((tm, c), lambda i: (i, 0)),
                   pl.BlockSpec((tm, 2 * ns), lambda i: (i, 0)), pl.BlockSpec((tm, 2 * ns), lambda i: (i, 0))],
        out_shape=[S((t, c), BF16), S((t, c), F32), S((t, 2 * ns), F32), S((t, 2 * ns), BF16)],
        scratch_shapes=[pltpu.VMEM((2, SUB, ns), F32)],
        compiler_params=_cp("arbitrary"), name=name)(u, mb, mc, pw, dskip)


def s5_bwd(name, dgy, y, u, xs, mct, mbt, qw, dskip):
    t, c = u.shape
    tm = _tile(t, S5_ROWS)
    nt = t // tm
    ns = N_STATE
    ng = tm // SUB

    def body(dgy_ref, y_ref, u_ref, xs_ref, mct_ref, mbt_ref, qw_ref, d_ref,
             du_ref, dy_ref, lb_ref, da_ref, dd_ref, lam, carry):
        @pl.when(pl.program_id(0) == 0)
        def _():
            carry[...] = jnp.zeros(carry.shape, F32)
            da_ref[...] = jnp.zeros(da_ref.shape, F32)
            dd_ref[...] = jnp.zeros(dd_ref.shape, F32)

        uv = u_ref[...]
        dy = dgy_ref[...] * _gelu_grad(y_ref[...])
        dyb = _bf(dy)
        dy_ref[...] = dyb
        dd_ref[...] += jnp.sum(dy * uv, axis=0, keepdims=True)
        for cs, s_re, s_im in _S5_BLOCKS:
            lam[:, s_re] = jnp.dot(dyb[:, cs], mct_ref[cs, s_re], preferred_element_type=F32)
            lam[:, s_im] = jnp.dot(dyb[:, cs], mct_ref[cs, s_im], preferred_element_type=F32)
        last_row = lax.broadcasted_iota(jnp.int32, (SUB, ns), 0) == SUB - 1

        def group(j, _):
            i = ng - 1 - j
            r0 = pl.multiple_of(i * SUB, SUB)
            lr = lam[pl.ds(r0, SUB), 0:ns]
            li = lam[pl.ds(r0, SUB), ns:2 * ns]
            for k, s in enumerate((1, 2, 4)):
                lr, li = _cmul_add(lr, li, qw_ref[k, 0], qw_ref[k, 1],
                                   pltpu.roll(lr, SUB - s, 0), pltpu.roll(li, SUB - s, 0))
            cr, ci = carry[0], carry[1]
            lr, li = _cmul_add(lr, li, qw_ref[3, 0], qw_ref[3, 1], cr, ci)
            lam[pl.ds(r0, SUB), 0:ns] = lr
            lam[pl.ds(r0, SUB), ns:2 * ns] = li
            carry[0] = jnp.broadcast_to(lr[0:1, :], (SUB, ns))
            carry[1] = jnp.broadcast_to(li[0:1, :], (SUB, ns))
            nr = jnp.where(last_row, cr, pltpu.roll(lr, SUB - 1, 0))
            ni = jnp.where(last_row, ci, pltpu.roll(li, SUB - 1, 0))
            xr = xs_ref[pl.ds(r0, SUB), 0:ns]
            xi = xs_ref[pl.ds(r0, SUB), ns:2 * ns]
            da_ref[0] += nr * xr + ni * xi
            da_ref[1] += ni * xr - nr * xi
            return 0
        lax.fori_loop(0, ng, group, 0)

        lb_ref[...] = _bf(lam[...])
        for cs, s_re, s_im in _S5_BLOCKS:
            du = (jnp.dot(lb_ref[:, s_re], mbt_ref[s_re, cs], preferred_element_type=F32)
                  + jnp.dot(lb_ref[:, s_im], mbt_ref[s_im, cs], preferred_element_type=F32) + d_ref[:, cs] * dy[:, cs])
            du_ref[:, cs] = du.astype(du_ref.dtype)

    rev = lambda i: (nt - 1 - i, 0)
    return pl.pallas_call(
        body, grid=(nt,),
        in_specs=[pl.BlockSpec((tm, c), rev), pl.BlockSpec((tm, c), rev), pl.BlockSpec((tm, c), rev),
                  pl.BlockSpec((tm, 2 * ns), rev),
                  pl.BlockSpec(mct.shape, lambda i: (0, 0)), pl.BlockSpec(mbt.shape, lambda i: (0, 0)),
                  pl.BlockSpec(qw.shape, lambda i: (0, 0, 0, 0)), pl.BlockSpec((1, c), lambda i: (0, 0))],
        out_specs=[pl.BlockSpec((tm, c), rev), pl.BlockSpec((tm, c), rev), pl.BlockSpec((tm, 2 * ns), rev),
                   pl.BlockSpec((2, SUB, ns), lambda i: (0, 0, 0)), pl.BlockSpec((1, c), lambda i: (0, 0))],
        out_shape=[S((t, c), BF16), S((t, c), BF16), S((t, 2 * ns), BF16), S((2, SUB, ns), F32), S((1, c), F32)],
        scratch_shapes=[pltpu.VMEM((tm, 2 * ns), F32), pltpu.VMEM((2, SUB, ns), F32)],
        compiler_params=_cp("arbitrary"), name=name)(dgy, y, u, xs, mct, mbt, qw, dskip)


def _first(accs, *_):
    return [accs[0]]


def _rms_bwd_epi(accs, xv, base, rv, g):
    dv = accs[0]
    w = dv * g
    xh = xv * rv
    dx = base + rv * (w - xh * jnp.mean(w * xh, axis=-1, keepdims=True))
    return [dx, dx, jnp.sum(dv * xh, axis=0, keepdims=True)]


def mm_rms_bwd(name, pairs, x, r, gain, dres):
    t, d = x.shape
    return mm_nn(name, t, d, pairs, 1, _rms_bwd_epi, [F32, BF16], tiled=[x, dres], cols=[r], rowv=[gain], sums=[(1, d)])


def _add_res(accs, res):
    return [accs[0] + res]


def even_fwd(x, w, need_out):
    t = x.shape[0]
    proj, hn, r = mm_nn("e_in_f", t, IN_WIDTH, [(x, w["e_w_in_t"], 0, "t")], 1, _first, [F32], norm_gain=w["e_norm"])
    out_a = gmlp_fwd("e_gmlp_f", proj, w["e_gmlp_w"], w["e_gmlp_b"])
    hc = conv_fwd("e_conv_f", proj, w["e_conv_w"], w["e_conv_b"])
    out_b = ln_silu_fwd("e_ln_f", hc, w["e_conv_ln_g"], w["e_conv_ln_b"])
    need_out(out_b)
    (x1,) = mm_nn("e_out_f", t, D_MODEL, [(out_a, (w["e_w_out"], 0), 0), (out_b, (w["e_w_out"], 1), 0)],
                  1, _add_res, [F32], tiled=[x])
    return x1, (x, hn, r, proj, out_a, hc, out_b)


def even_bwd_mixers(dxb, saved, w):
    x, hn, r, proj, out_a, hc, out_b = saved
    t = x.shape[0]
    g_w_out = jnp.concatenate([mm_tn("e_out_wa", out_a, dxb), mm_tn("e_out_wb", out_b, dxb)], axis=0)
    dab, g_gw, g_gb = gmlp_bwd("e_gmlp_b", proj, dxb, w["e_w_out"], w["e_gmlp_w"], w["e_gmlp_b"])
    dhc, g_lg, g_lb = ln_silu_bwd("e_ln_b", hc, dxb, w["e_w_out"], w["e_conv_ln_g"], w["e_conv_ln_b"])
    dba, dbg, g_cw, g_cb = conv_bwd("e_conv_b", proj, dhc, w["e_conv_w"])
    g_w_in_t = jnp.concatenate([mm_tn("e_in_w0", dab, hn), mm_tn("e_in_w1", dba, hn), mm_tn("e_in_w2", dbg, hn)], axis=0)
    grads = dict(e_w_in_t=g_w_in_t, e_gmlp_w=g_gw[None], e_gmlp_b=g_gb.reshape(1, A_GROUPS, GMLP_BLOCK),
                 e_conv_w=g_cw[None], e_conv_b=g_cb, e_conv_ln_g=g_lg, e_conv_ln_b=g_lb, e_w_out=g_w_out)
    return (dab, dba, dbg), grads


def even_bwd_input(dx, dproj, saved, w):
    x, _, r = saved[:3]
    dab, dba, dbg = dproj
    w_in_t = w["e_w_in_t"]
    return mm_rms_bwd("e_in_b", [(dab, (w_in_t, 0), 0), (dba, (w_in_t, 2), 0), (dbg, (w_in_t, 3), 0)], x, r, w["e_norm"], dx)


def s5_setup(w, anchor=None):
    def rows(v):
        return v.transpose(0, 2, 1).reshape(_RP)

    log_dt = w["o_log_dt"].reshape(C_GROUPS, 1)
    if anchor is not None:
        log_dt = log_dt + anchor
    lam = (w["o_lam_re"], w["o_lam_im"], log_dt, rows(w["o_b_re"]), rows(w["o_b_im"]))
    a, bbr, bbi = s5_discretise("o_s5_zoh", *lam)
    c_re, c_im = w["o_c_re"], w["o_c_im"]
    pw, qw, mb, mc, mct = s5_operands("o_s5_ops", a.reshape(2, N_STATE), bbr, bbi, c_re.reshape(_RP), c_im.reshape(_RP),
                                      c_re.transpose(2, 0, 1).reshape(C_STATE, C_WIDTH),
                                      c_im.transpose(2, 0, 1).reshape(C_STATE, C_WIDTH))
    return dict(lam=lam, pw=pw, qw=qw, mb=mb, mc=mc, mct=mct, mbt=mb.T)


def odd_fwd(x, w, consts):
    t = x.shape[0]
    u, hn, r = mm_nn("o_in_f", t, C_WIDTH, [(x, w["o_w_in"], 0)], 1, _first, [F32], norm_gain=w["o_norm"])
    gy, y, xs, xsb = s5_fwd("o_s5_f", u, consts["mb"], consts["mc"], consts["pw"], w["o_d"])
    w_out_t = w["o_w_out_t"]

    def epi(accs, res):
        return [res + accs[0] * _sigmoid(accs[1]), accs[0], accs[1]]

    x1, o1, o2 = mm_nn("o_out_f", t, D_MODEL, [(gy, (w_out_t, 0), 0, "t"), (gy, (w_out_t, D_MODEL), 1, "t")], 2, epi,
                       [F32, BF16, BF16], tiled=[x])
    return x1, (x, hn, r, u, gy, y, xs, xsb, o1, o2)


def odd_bwd(dx, dxb, saved, w, consts):
    x, hn, r, u, gy, y, xs, xsb, o1, o2 = saved
    t = x.shape[0]

    def gate_bwd(dv, a, b, wv):
        a = a.astype(F32)
        sg = _sigmoid(b.astype(F32))
        do12 = jnp.concatenate([dv * sg, dv * a * sg * (1.0 - sg)], axis=1).astype(BF16)
        return [do12, jnp.dot(do12, wv, preferred_element_type=F32)], []

    do12, dgy = rows_call("o_out_b", gate_bwd, [dx, o1, o2], [w["o_w_out_t"]], [(2 * D_MODEL, BF16), (C_WIDTH, F32)], [])
    g_w_out_t = mm_tn("o_out_w", do12, gy)
    du, dyb, lamb, da8, g_d = s5_bwd("o_s5_b", dgy, y, u, xs, consts["mct"], consts["mbt"], consts["qw"], w["o_d"])
    d_mb, d_mc = s5_block_grads("o_s5_w", u, lamb, xsb, dyb)
    da = jnp.sum(da8, axis=1).reshape((2,) + _GP)
    g_lr, g_li, g_dt, g_btr, g_bti, g_cr, g_ci = s5_param_grads("o_s5_pg", d_mb, d_mc, da, *consts["lam"])

    def states_first(v):
        return v.reshape(C_GROUPS, C_GROUP_CH, C_STATE).transpose(0, 2, 1)[None]

    g_w_in = mm_tn("o_in_w", hn, du)
    dx0, dx0b, g_norm = mm_rms_bwd("o_in_b", [(du, w["o_w_in"], 0, "t")], x, r, w["o_norm"], dx)
    grads = dict(o_norm=g_norm, o_w_in=g_w_in, o_lam_re=g_lr[None], o_lam_im=g_li[None], o_log_dt=g_dt.reshape(1, C_GROUPS),
                 o_b_re=states_first(g_btr), o_b_im=states_first(g_bti),
                 o_c_re=g_cr.reshape((1, C_GROUPS, C_GROUP_CH, C_STATE)), o_c_im=g_ci.reshape((1, C_GROUPS, C_GROUP_CH, C_STATE)),
                 o_d=g_d, o_w_out_t=g_w_out_t)
    return dx0, dx0b, grads


def ca_fwd(i, x, mem, w):
    t, m = x.shape[0], mem.shape[0]
    k, v, mn, rm = mm_nn(f"ca{i}_kv_f", m, D_MODEL, [(mem, w["ca_wk"][i], 0), (mem, w["ca_wv"][i], 1)], 2,
                         lambda accs: [accs[0], accs[1]], [BF16, BF16], norm_gain=w["ca_mem_norm"][i:i + 1])
    x1, xn, r, q, o = attn_fwd(f"ca{i}_attn_f", x, w["ca_norm"][i:i + 1], w["ca_wq"][i], k, v, w["ca_wo"][i])
    return x1, (x, xn, r, mn, rm, q, k, v, o)


def ca_bwd(i, dx, dxb, saved, mem, w):
    x, xn, r, mn, rm, q, k, v, o = saved
    t, m = x.shape[0], mem.shape[0]
    g_wo = mm_tn(f"ca{i}_o_w", o, dxb)
    dx0, dx0b, dq, dk, dv, g_norm = attn_bwd(f"ca{i}_attn_b", dx, dxb, x, r, w["ca_norm"][i:i + 1], q, k, v,
                                             w["ca_wq"][i], w["ca_wo"][i])
    g_wq = mm_tn(f"ca{i}_q_w", xn, dq)
    g_wk = mm_tn(f"ca{i}_k_w", mn, dk)
    g_wv = mm_tn(f"ca{i}_v_w", mn, dv)
    (dmn,) = mm_nn(f"ca{i}_kv_b", m, D_MODEL, [(dk, w["ca_wk"][i], 0, "t"), (dv, w["ca_wv"][i], 0, "t")], 1, _first, [F32])
    g_mnorm = rms_bwd_gain_only(f"ca{i}_mnorm_b", dmn, mem, rm)
    return dx0, dx0b, dict(ca_norm=g_norm, ca_mem_norm=g_mnorm, ca_wq=g_wq, ca_wk=g_wk, ca_wv=g_wv, ca_wo=g_wo)


FFN_ROWS = 512
FFN_CHUNK = 256


def _whole(a):
    return pl.BlockSpec(a.shape, lambda i: (0,) * a.ndim, pipeline_mode=pl.Buffered(1))


def ffn_fused_fwd(name, x, gain, wg_t, wu_t, wd, target=None, final_gain=None):
    t, d = x.shape
    hid = wd.shape[0]
    tm = _tile(t, FFN_ROWS)
    last = target is not None
    n_main = 4 if last else 1

    def body(*refs):
        x_ref, g_ref, wg_ref, wu_ref, wd_ref = refs[:5]
        rest = refs[5:]
        if last:
            tgt_ref, fg_ref = rest[:2]
            rest = rest[2:]
        main, (xn_ref, r_ref, dgate_ref, dup_ref, h_ref) = rest[:n_main], rest[n_main:]
        xv = x_ref[...]
        rv = lax.rsqrt(jnp.mean(xv * xv, axis=-1, keepdims=True) + EPS)
        xn = (xv * rv * g_ref[...]).astype(BF16)
        xn_ref[...] = xn
        r_ref[...] = rv
        for j in range(hid // FFN_CHUNK):
            cs = slice(j * FFN_CHUNK, (j + 1) * FFN_CHUNK)
            g = lax.dot_general(xn, wg_ref[cs, :], _NT, preferred_element_type=F32)
            u = lax.dot_general(xn, wu_ref[cs, :], _NT, preferred_element_type=F32)
            s = _sigmoid(g)
            silu = g * s
            dgate_ref[:, cs] = (u * (s + silu * (1.0 - s))).astype(BF16)
            dup_ref[:, cs] = silu.astype(BF16)
            h_ref[:, cs] = (silu * u).astype(BF16)
        acc = jnp.dot(h_ref[...], wd_ref[...], preferred_element_type=F32)
        if not last:
            main[0][...] = xv + acc
        else:
            dx, _, dgain, part = _final_loss_epi([acc], xv, tgt_ref[...], fg_ref[...])

            @pl.when(pl.program_id(0) == 0)
            def _():
                main[2][...] = jnp.zeros(main[2].shape, F32)
                main[3][...] = jnp.zeros(main[3].shape, F32)
            main[0][...] = dx
            main[1][...] = dx.astype(BF16)
            main[2][...] += dgain
            main[3][...] += part

    rows = pl.BlockSpec((tm, d), lambda i: (i, 0))
    wide = pl.BlockSpec((tm, hid), lambda i: (i, 0))
    col = pl.BlockSpec((tm, 1), lambda i: (i, 0))
    ins, in_specs = [x, gain, wg_t, wu_t, wd], [rows, _whole(gain), _whole(wg_t), _whole(wu_t), _whole(wd)]
    if last:
        ins += [target, final_gain]
        in_specs += [rows, _whole(final_gain)]
        out_specs = [rows, rows, pl.BlockSpec((1, d), lambda i: (0, 0)), pl.BlockSpec((1, 1), lambda i: (0, 0))]
        out_shape = [S((t, d), F32), S((t, d), BF16), S((1, d), F32), S((1, 1), F32)]
    else:
        out_specs, out_shape = [rows], [S((t, d), F32)]
    out_specs += [rows, col, wide, wide, wide]
    out_shape += [S((t, d), BF16), S((t, 1), F32)] + [S((t, hid), BF16)] * 3
    outs = pl.pallas_call(body, grid=(t // tm,), in_specs=in_specs, out_specs=out_specs, out_shape=out_shape,
                          compiler_params=_cp("arbitrary" if last else "parallel"), name=name)(*ins)
    return (tuple(outs[:4]) if last else outs[0]), outs[n_main:]


def ffn_fused_bwd(name, dx, dxb, x, r, gain, dgate, dup, wg_t, wu_t, wd):
    t, d = x.shape
    hid = wd.shape[0]
    tm = _tile(t, FFN_ROWS // 2)

    def body(dx_ref, dxb_ref, x_ref, r_ref, g_ref, dgate_ref, dup_ref, wg_ref, wu_ref, wd_ref,
             dxo_ref, dxbo_ref, dg_ref, du_ref, dgain_ref):
        @pl.when(pl.program_id(0) == 0)
        def _():
            dgain_ref[...] = jnp.zeros(dgain_ref.shape, F32)

        dxb = dxb_ref[...]
        for j in range(hid // FFN_CHUNK):
            cs = slice(j * FFN_CHUNK, (j + 1) * FFN_CHUNK)
            dh = lax.dot_general(dxb, wd_ref[cs, :], _NT, preferred_element_type=F32)
            dg_ref[:, cs] = (dh * dgate_ref[:, cs].astype(F32)).astype(BF16)
            du_ref[:, cs] = (dh * dup_ref[:, cs].astype(F32)).astype(BF16)
        dxn = (jnp.dot(dg_ref[...], wg_ref[...], preferred_element_type=F32)
               + jnp.dot(du_ref[...], wu_ref[...], preferred_element_type=F32))
        dxo, _, dgain = _rms_bwd_epi([dxn], x_ref[...], dx_ref[...], r_ref[...], g_ref[...])
        dxo_ref[...] = dxo
        dxbo_ref[...] = dxo.astype(BF16)
        dgain_ref[...] += dgain

    rows = pl.BlockSpec((tm, d), lambda i: (i, 0))
    wide = pl.BlockSpec((tm, hid), lambda i: (i, 0))
    col = pl.BlockSpec((tm, 1), lambda i: (i, 0))
    return pl.pallas_call(
        body, grid=(t // tm,),
        in_specs=[rows, rows, rows, col, _whole(gain), wide, wide, _whole(wg_t), _whole(wu_t), _whole(wd)],
        out_specs=[rows, rows, wide, wide, pl.BlockSpec((1, d), lambda i: (0, 0))],
        out_shape=[S((t, d), F32), S((t, d), BF16), S((t, hid), BF16), S((t, hid), BF16), S((1, d), F32)],
        compiler_params=_cp("arbitrary"), name=name)(dx, dxb, x, r, gain, dgate, dup, wg_t, wu_t, wd)


def ffn_fwd(i, x, w, target=None):
    out, (xn, r, dgate, dup, h) = ffn_fused_fwd(f"ffn{i}_f", x, w["ffn_norm"][i:i + 1], w["ffn_w_gate_t"][i],
                                                w["ffn_w_up_t"][i], w["ffn_w_down"][i], target,
                                                None if target is None else w["final_norm"])
    return out, (x, xn, r, dgate, dup, h)


def ffn_bwd(i, dx, dxb, saved, w):
    x, xn, r, dgate, dup, h = saved
    g_wd = mm_tn(f"ffn{i}_down_w", h, dxb)
    dx0, dx0b, dg, du, g_norm = ffn_fused_bwd(f"ffn{i}_b", dx, dxb, x, r, w["ffn_norm"][i:i + 1], dgate, dup,
                                              w["ffn_w_gate_t"][i], w["ffn_w_up_t"][i], w["ffn_w_down"][i])
    g_wg_t = mm_tn(f"ffn{i}_gate_w", dg, xn)
    g_wu_t = mm_tn(f"ffn{i}_up_w", du, xn)
    return dx0, dx0b, dict(ffn_norm=g_norm, ffn_w_gate_t=g_wg_t, ffn_w_up_t=g_wu_t, ffn_w_down=g_wd)


def local_step(x, mem, target, w, fetch=None, on_grads=None, anchor=None):
    consts = s5_setup(w, anchor)

    def need(stage, after):
        if fetch is not None:
            for k, v in fetch(stage, after).items():
                if isinstance(k, tuple):
                    w.setdefault(k[0], {})[k[1]] = v
                else:
                    w[k] = v

    need(0, consts["pw"])
    x1, s_e = even_fwd(x, w, lambda after: need(1, after))
    x2, s_c0 = ca_fwd(0, x1, mem, w)
    need(2, x2)
    x3, s_f0 = ffn_fwd(0, x2, w)
    x4, s_o = odd_fwd(x3, w, consts)
    need(3, x4)
    x5, s_c1 = ca_fwd(1, x4, mem, w)
    need(4, x5)
    (dx, dxb, g_final, loss), s_f1 = ffn_fwd(1, x5, w, target)

    def emit(stage, carry, plain, layered=None, layer=0):
        if on_grads is None:
            return carry
        out = dict(plain)
        out.update({(k, layer): v for k, v in (layered or {}).items()})
        return on_grads(stage, out, list(carry))

    dx, dxb, g_f1 = ffn_bwd(1, dx, dxb, s_f1, w)
    dx, dxb = emit(0, (dx, dxb), {}, g_f1, 1)
    dx, dxb, g_c1 = ca_bwd(1, dx, dxb, s_c1, mem, w)
    dx, dxb, g_o = odd_bwd(dx, dxb, s_o, w, consts)
    dx, dxb = emit(1, (dx, dxb), g_o, g_c1, 1)
    dx, dxb, g_f0 = ffn_bwd(0, dx, dxb, s_f0, w)
    dx, dxb = emit(2, (dx, dxb), {}, g_f0, 0)
    dx, dxb, g_c0 = ca_bwd(0, dx, dxb, s_c0, mem, w)
    dx, dxb = emit(3, (dx, dxb), {}, g_c0, 0)
    dproj, g_e = even_bwd_mixers(dxb, s_e, w)
    dproj = emit(4, dproj, {**g_e, "o_norm": g_o["o_norm"], "o_d": g_o["o_d"]})
    dx, dxb, g_e["e_norm"] = even_bwd_input(dx, dproj, s_e, w)

    grads = dict(g_e)
    grads.update(g_o)
    for g0, g1 in ((g_c0, g_c1), (g_f0, g_f1)):
        for k in g0:
            grads[k] = jnp.concatenate([g0[k], g1[k]], axis=0) if k.endswith("norm") else (g0[k], g1[k])
    grads["final_norm"] = g_final
    return loss, dx, grads


def _group(axes):
    pos = {a: lax.axis_index(a) for a in ("x", "y", "c")}
    me = 0
    for a in axes:
        me = me * 2 + pos[a]
    peers = []
    for mask in range(1, 2 ** len(axes)):
        peer = dict(pos)
        for bit, a in enumerate(axes):
            if (mask >> (len(axes) - 1 - bit)) & 1:
                peer[a] = 1 - pos[a]
        idx = 0
        for a in axes:
            idx = idx * 2 + peer[a]
        peers.append((idx, (peer["x"], peer["y"], peer["c"])))
    return me, peers


def _sibling():
    x, y, c = lax.axis_index("x"), lax.axis_index("y"), lax.axis_index("c")
    return c, (x, y, 1 - c)


_HBM =pl.BlockSpec(memory_space=pltpu.HBM)
_SEM = pl.BlockSpec(memory_space=pltpu.SEMAPHORE)
_EFFECT = pltpu.SideEffectType.DATAFLOW_SIDE_EFFECTING


def _gather_peers(direct):
    chip, _ = _group(("x", "y"))
    core = lax.axis_index("c")
    if direct:
        _, peers = _group(_ALL)
        return chip, core, [(idx // 2, idx % 2, dev) for idx, dev in peers]
    _, peers = _group(("x", "y"))
    return chip, core, [(idx, core, dev) for idx, dev in peers]


def gather_ici_start(name, groups, direct):
    flat = [b for g in groups for b in g]
    sizes = [len(g) for g in groups]
    k_ops, n_g = len(flat), len(groups)
    lands = [lax.empty((4, 2) + tuple(b.shape), b.dtype) for b in flat]
    fan = [N_DEV - 1 if d else 3 for d in direct]

    def body(*refs):
        src, land = refs[:k_ops], refs[k_ops:2 * k_ops]
        sems = refs[2 * k_ops:2 * k_ops + 3 * n_g]
        token = refs[-1]
        i = 0
        for g in range(n_g):
            send, recv, loc = sems[3 * g:3 * g + 3]
            chip, core, peers = _gather_peers(direct[g])
            for j in range(sizes[g]):
                pltpu.make_async_copy(src[i], land[i].at[chip, core], loc.at[j]).start()
                for k, (_, _, dev) in enumerate(peers):
                    s = fan[g] * j + k
                    pltpu.make_async_remote_copy(src_ref=src[i], dst_ref=land[i].at[chip, core], send_sem=send.at[s],
                                                 recv_sem=recv.at[s], device_id=dev, device_id_type=MESH).start()
                i += 1
        token[...] = jnp.zeros(token.shape, token.dtype)

    sem_shapes = []
    for s, f in zip(sizes, fan):
        sem_shapes += [pltpu.SemaphoreType.DMA((f * s,)), pltpu.SemaphoreType.DMA((f * s,)), pltpu.SemaphoreType.DMA((s,))]
    thru = [pltpu.HBM(a.shape, a.dtype) for a in flat + lands]
    outs = pl.pallas_call(
        body, name=name, out_shape=tuple(sem_shapes) + tuple(thru) + (S((8, LANES), F32),),
        in_specs=[_HBM] * (2 * k_ops), out_specs=[_SEM] * (3 * n_g) + [_HBM] * (2 * k_ops) + [pl.BlockSpec(memory_space=pltpu.VMEM)],
        input_output_aliases={i: 3 * n_g + i for i in range(2 * k_ops)},
        compiler_params=pltpu.CompilerParams(has_side_effects=_EFFECT),
    )(*[pltpu.with_memory_space_constraint(a, pltpu.HBM) for a in flat + lands])
    sems = [tuple(outs[3 * g:3 * g + 3]) for g in range(n_g)]
    srcs_thru, lands_thru, off = [], [], 3 * n_g
    for s in sizes:
        srcs_thru.append(list(outs[off:off + s]))
        off += s
    for s in sizes:
        lands_thru.append(list(outs[off:off + s]))
        off += s
    return sems, srcs_thru, lands_thru, outs[-1]


def gather_ici_wait(name, srcs, lands, sems, after, direct=False):
    n = len(srcs)

    def body(*refs):
        src, land = refs[:n], refs[n:2 * n]
        send, recv, loc = refs[2 * n:2 * n + 3]
        chip, core, peers = _gather_peers(direct)
        for j in range(n):
            for k, (pchip, pcore, dev) in enumerate(peers):
                s = len(peers) * j + k
                cp = pltpu.make_async_remote_copy(src_ref=src[j], dst_ref=land[j].at[pchip, pcore], send_sem=send.at[s],
                                                  recv_sem=recv.at[s], device_id=dev, device_id_type=MESH)
                cp.wait_send()
                cp.wait_recv()
            pltpu.make_async_copy(src[j], land[j].at[chip, core], loc.at[j]).wait()

    outs = pl.pallas_call(
        body, name=name, out_shape=tuple(pltpu.HBM(a.shape, a.dtype) for a in list(srcs) + list(lands)),
        in_specs=[_HBM] * (2 * n) + [_SEM] * 3 + [ANY], out_specs=[_HBM] * (2 * n),
        input_output_aliases={i: i for i in range(2 * n)},
        compiler_params=pltpu.CompilerParams(has_side_effects=_EFFECT),
    )(*srcs, *lands, *sems, after)
    return list(outs[n:])


def gather_d2d(name, bufs):
    k_ops = len(bufs)

    def body(*refs):
        in_refs, out_refs = refs[:k_ops], refs[k_ops:2 * k_ops]
        send_sems, recv_sems = refs[2 * k_ops:]
        core, sib = _sibling()
        sent, landed = [], []
        for i in range(k_ops):
            cp = pltpu.make_async_remote_copy(src_ref=in_refs[i].at[:, core], dst_ref=out_refs[i].at[:, core],
                                              send_sem=send_sems.at[i], recv_sem=recv_sems.at[i], device_id=sib, device_id_type=MESH)
            cp.start()
            sent.append(cp)
            landed.append(pltpu.make_async_remote_copy(src_ref=in_refs[i].at[:, core], dst_ref=out_refs[i].at[:, 1 - core],
                                                       send_sem=send_sems.at[i], recv_sem=recv_sems.at[i],
                                                       device_id=sib, device_id_type=MESH))
        for cp in landed:
            cp.wait_recv()
        for cp in sent:
            cp.wait_send()

    return pl.pallas_call(
        body, in_specs=[ANY] * k_ops, out_specs=[ANY] * k_ops, out_shape=[S(b.shape, b.dtype) for b in bufs],
        input_output_aliases={i: i for i in range(k_ops)},
        scratch_shapes=[pltpu.SemaphoreType.DMA((k_ops,)), pltpu.SemaphoreType.DMA((k_ops,))],
        name=name)(*bufs)


_ALL = ("x", "y", "c")


def _unit_rows(units):
    offs, off = [], 0
    for u in units:
        offs.append(off)
        off += u.shape[1]
    return offs, off


def scatter_start(name, units, carry):
    n_u, n_c = len(units), len(carry)
    offs, rows = _unit_rows(units)
    land = lax.empty((N_DEV, rows) + tuple(units[0].shape[2:]), units[0].dtype)
    fan = N_DEV - 1

    def body(*refs):
        u_refs, land_ref = refs[:n_u], refs[n_u]
        send, recv, loc = refs[n_u + 1 + n_c:n_u + 4 + n_c]
        me, peers = _group(_ALL)
        for j in range(n_u):
            rs = pl.ds(offs[j], units[j].shape[1])
            pltpu.make_async_copy(u_refs[j].at[me], land_ref.at[me, rs], loc.at[j]).start()
            for k, (idx, dev) in enumerate(peers):
                pltpu.make_async_remote_copy(src_ref=u_refs[j].at[idx], dst_ref=land_ref.at[me, rs], send_sem=send.at[fan * j + k],
                                             recv_sem=recv.at[fan * j + k], device_id=dev, device_id_type=MESH).start()

    thru = list(units) + [land] + list(carry)
    outs = pl.pallas_call(
        body, name=name,
        out_shape=(pltpu.SemaphoreType.DMA((fan * n_u,)), pltpu.SemaphoreType.DMA((fan * n_u,)), pltpu.SemaphoreType.DMA((n_u,)))
        + tuple(pltpu.HBM(a.shape, a.dtype) for a in thru),
        in_specs=[_HBM] * len(thru), out_specs=[_SEM] * 3 + [_HBM] * len(thru),
        input_output_aliases={i: 3 + i for i in range(len(thru))},
        compiler_params=pltpu.CompilerParams(has_side_effects=_EFFECT),
    )(*[pltpu.with_memory_space_constraint(a, pltpu.HBM) for a in thru])
    return tuple(outs[:3]), list(outs[3:3 + n_u]), outs[3 + n_u], list(outs[4 + n_u:])


def scatter_wait(name, units, land, sems, after):
    n_u = len(units)
    offs, _ = _unit_rows(units)
    fan = N_DEV - 1

    def body(*refs):
        u_refs, land_ref = refs[:n_u], refs[n_u]
        send, recv, loc = refs[n_u + 1:n_u + 4]
        me, peers = _group(_ALL)
        for j in range(n_u):
            rs = pl.ds(offs[j], units[j].shape[1])
            for k, (idx, dev) in enumerate(peers):
                cp = pltpu.make_async_remote_copy(src_ref=u_refs[j].at[idx], dst_ref=land_ref.at[idx, rs], send_sem=send.at[fan * j + k],
                                                  recv_sem=recv.at[fan * j + k], device_id=dev, device_id_type=MESH)
                cp.wait_send()
                cp.wait_recv()
            pltpu.make_async_copy(u_refs[j].at[me], land_ref.at[me, rs], loc.at[j]).wait()

    thru = list(units) + [land]
    outs = pl.pallas_call(
        body, name=name, out_shape=tuple(pltpu.HBM(a.shape, a.dtype) for a in thru),
        in_specs=[_HBM] * len(thru) + [_SEM] * 3 + [ANY], out_specs=[_HBM] * len(thru),
        input_output_aliases={i: i for i in range(len(thru))},
        compiler_params=pltpu.CompilerParams(has_side_effects=_EFFECT),
    )(*thru, *sems, after)
    return outs[n_u]


def _row_tile(rows, cap=512):
    return next(t for t in range(cap - cap % 16, 0, -16) if rows % t == 0)


def sum_shares(name, recv, me):
    n, rows, c = recv.shape
    tr = _row_tile(rows)

    def body(me_ref, *refs):
        acc = refs[0][...].astype(F32)
        for r in refs[1:n]:
            acc = acc + r[...].astype(F32)
        refs[n][...] = acc

    def slot(mask):
        return pl.BlockSpec((None, tr, c), lambda i, me, mask=mask: (jnp.bitwise_xor(me[0], mask), i, 0))

    spec = pltpu.PrefetchScalarGridSpec(
        num_scalar_prefetch=1, grid=(rows // tr,), in_specs=[slot(k) for k in range(n)],
        out_specs=pl.BlockSpec((tr, c), lambda i, me: (i, 0)))
    return pl.pallas_call(body, grid_spec=spec, out_shape=S((rows, c), F32),
                          compiler_params=_cp("parallel"), name=name)(me, *([recv] * n))


def sum_slots(name, slots):
    n, r, c = slots.shape

    def body(s_ref, o_ref):
        acc = s_ref[0]
        for j in range(1, n):
            acc = acc + s_ref[j]
        o_ref[...] = acc

    return pl.pallas_call(body, out_shape=S((r, c), F32), compiler_params=pltpu.CompilerParams(vmem_limit_bytes=VMEM_LIMIT),
                          name=name)(slots)


def adamw_units(name, pieces, transposed, w, m, v):
    n_l, k, n = w.shape
    tk = _tile(k, 512) if transposed else k
    c1 = 1.0 - ADAM_B1 ** ADAM_STEP
    c2 = 1.0 - ADAM_B2 ** ADAM_STEP

    def body(*refs):
        p_refs, (w_ref, m_ref, v_ref, g_ref, d_ref, m2_ref, v2_ref) = refs[:n_l], refs[n_l:]
        gv = p_refs[0][...]
        for j in range(1, n_l):
            gv = jnp.where(pl.program_id(0) == j, p_refs[j][...], gv)
        if transposed:
            gv = gv.T
        m2 = ADAM_B1 * m_ref[...] + (1.0 - ADAM_B1) * gv
        v2 = ADAM_B2 * v_ref[...] + (1.0 - ADAM_B2) * (gv * gv)
        g_ref[...] = gv
        m2_ref[...] = m2
        v2_ref[...] = v2
        d_ref[...] = -ADAM_LR * ((m2 / c1) / (jnp.sqrt(v2 / c2) + ADAM_EPS) + ADAM_WD * w_ref[...])

    piece = pl.BlockSpec((n, tk), lambda l, i: (0, i)) if transposed else pl.BlockSpec((k, n), lambda l, i: (0, 0))
    blk = pl.BlockSpec((None, tk, n), lambda l, i: (l, i, 0))
    return tuple(pl.pallas_call(body, grid=(n_l, k // tk), in_specs=[piece] * n_l + [blk] * 3, out_specs=[blk] * 4,
                                out_shape=[S(w.shape, F32)] * 4, compiler_params=_cp("parallel", "parallel"),
                                name=name)(*pieces, w, m, v))


def adamw_native(name, g, w, m, v, tr=512):
    shape = w.shape
    cols = shape[-1]
    rows = w.size // cols
    tr = _tile(rows, tr) if rows % 8 == 0 else rows
    c1 = 1.0 - ADAM_B1 ** ADAM_STEP
    c2 = 1.0 - ADAM_B2 ** ADAM_STEP

    def body(g_ref, w_ref, m_ref, v_ref, d_ref, m2_ref, v2_ref):
        gv = g_ref[...]
        m2 = ADAM_B1 * m_ref[...] + (1.0 - ADAM_B1) * gv
        v2 = ADAM_B2 * v_ref[...] + (1.0 - ADAM_B2) * (gv * gv)
        m2_ref[...] = m2
        v2_ref[...] = v2
        d_ref[...] = -ADAM_LR * ((m2 / c1) / (jnp.sqrt(v2 / c2) + ADAM_EPS) + ADAM_WD * w_ref[...])

    row = pl.BlockSpec((tr, cols), lambda i: (i, 0))
    outs = pl.pallas_call(body, grid=(rows // tr,), in_specs=[row] * 4, out_specs=[row] * 3,
                          out_shape=[S((rows, cols), F32)] * 3, compiler_params=_cp("parallel"),
                          name=name)(*[a.reshape(rows, cols) for a in (g, w, m, v)])
    return tuple(o.reshape(shape) for o in outs)


_REPLICATED = ("e_norm", "e_gmlp_w", "e_gmlp_b", "e_conv_b", "e_conv_ln_g", "e_conv_ln_b", "o_lam_re", "o_lam_im", "o_log_dt",
               "o_b_re", "o_b_im", "o_c_re", "o_c_im", "ca_norm", "ca_mem_norm", "ffn_norm", "final_norm")
_ORDER = ("e_norm", "e_w_in", "e_gmlp_w", "e_gmlp_b", "e_conv_w", "e_conv_b", "e_conv_ln_g", "e_conv_ln_b", "e_w_out",
          "o_norm", "o_w_in", "o_lam_re", "o_lam_im", "o_log_dt", "o_b_re", "o_b_im", "o_c_re", "o_c_im", "o_d", "o_w_out",
          "ca_norm", "ca_mem_norm", "ca_wq", "ca_wk", "ca_wv", "ca_wo", "ffn_norm", "ffn_w_gate", "ffn_w_up", "ffn_w_down",
          "final_norm")


def _rows128(a, multiple=8):
    flat = a.reshape(-1)
    rows = -(-flat.shape[0] // (LANES * multiple)) * multiple
    return jnp.pad(flat, (0, rows * LANES - flat.shape[0])).reshape(rows, LANES)


def _shard(full, axis):
    s = full.shape
    return jnp.moveaxis(full.reshape(s[:axis] + (N_DEV, s[axis] // N_DEV) + s[axis + 1:]), axis, 0)


_UNITS = (("e_w_in", 0, True), ("e_w_out", 0, False), ("o_w_in", 0, False), ("o_w_out", 0, True),
          *[(n, i, False) for n in ("ca_wq", "ca_wk", "ca_wv", "ca_wo") for i in (0, 1)],
          *[(n, i, tr) for n, tr in (("ffn_w_gate", True), ("ffn_w_up", True), ("ffn_w_down", False)) for i in (0, 1)])
_LAYERED = ("ca_wq", "ca_wk", "ca_wv", "ca_wo", "ffn_w_gate", "ffn_w_up", "ffn_w_down")
_SMALL_SHARDED = (("e_conv_w", 2), ("o_norm", 1), ("o_d", 1))
RS_ROW = 1024


def _unit_key(name, tr):
    return name + "_t" if tr else name


def _stage_of(name, layer):
    if name.startswith("e_"):
        return 0 if name == "e_w_in" else 1
    if name.startswith("o_"):
        return 2
    if name.startswith("ca_"):
        return 1 if layer == 0 else 3
    return 2 if layer == 0 else 4


GATHER_STAGES = 5
GATHER_DIRECT = (False, False, False, True, False)


def weight_fetcher(local):
    groups, meta = [[] for _ in range(GATHER_STAGES)], [[] for _ in range(GATHER_STAGES)]
    for name, layer, tr in _UNITS:
        blk = local[name][layer]
        st = _stage_of(name, layer)
        groups[st].append(_bf(blk.T if tr else blk))
        meta[st].append((name, layer, tr))
    small = jnp.concatenate([local[name].reshape(-1) for name, _ in _SMALL_SHARDED])
    groups[0].append(_rows128(small))
    direct = list(GATHER_DIRECT)
    sems, srcs, lands, token = gather_ici_start("ag_w_start", groups, direct)

    def fetch(stage, after):
        bufs = gather_ici_wait(f"ag_w_wait{stage}", srcs[stage], lands[stage], sems[stage], after, direct[stage])
        if not direct[stage]:
            bufs = gather_d2d(f"ag_w_d2d{stage}", bufs)
        got = {}
        for (name, layer, tr), blk, buf in zip(meta[stage], groups[stage], bufs):
            arr = buf.reshape((N_DEV * blk.shape[0],) + tuple(blk.shape[1:]))
            if name in _LAYERED:
                got[(_unit_key(name, tr), layer)] = arr
            else:
                got[_unit_key(name, tr)] = arr
        if stage == 0:
            flat = bufs[-1].reshape(N_DEV, -1)
            off = 0
            for name, axis in _SMALL_SHARDED:
                blk = local[name]
                seg = flat[:, off:off + blk.size].reshape((N_DEV,) + blk.shape)
                off += blk.size
                seg = jnp.moveaxis(seg, 0, axis)
                got[name] = seg.reshape(seg.shape[:axis] + (-1,) + seg.shape[axis + 2:])
            got["e_conv_w"] = got["e_conv_w"][0]
        return got

    return fetch, token


def _grad_stage_of(name, layer):
    if name.startswith("e_"):
        return 4
    if name.startswith("o_"):
        return 1
    if name.startswith("ca_"):
        return 3 if layer == 0 else 1
    return 2 if layer == 0 else 0


GRAD_STAGES = 5
SMALL_ROWS = 16


def gradient_reducer(local, mom, var):
    me = (4 * lax.axis_index("x") + 2 * lax.axis_index("y") + lax.axis_index("c")).astype(jnp.int32).reshape(1)
    pending = []

    def start(stage, grads, carry):
        units = [u for u in _UNITS if _grad_stage_of(u[0], u[1]) == stage]
        parts, spans = [], []
        for name, layer, tr in units:
            key = _unit_key(name, tr)
            g = grads[(key, layer)] if name in _LAYERED else grads[key]
            part = g.reshape(N_DEV, -1, RS_ROW)
            spans.append((part.shape[1], g.shape[0] // N_DEV, g.shape[1]))
            parts.append(part)
        if stage == GRAD_STAGES - 1:
            small = jnp.concatenate([_shard(grads[name], axis).reshape(N_DEV, -1) for name, axis in _SMALL_SHARDED], axis=1)
            small = jnp.pad(small, ((0, 0), (0, SMALL_ROWS * RS_ROW - small.shape[1])))
            parts.append(small.astype(BF16).reshape(N_DEV, SMALL_ROWS, RS_ROW))
        sems, sent, land, carry = scatter_start(f"rs_start{stage}", parts, carry)
        pending.append((stage, units, spans, sems, sent, land))
        return carry

    def finish(after):
        res, per_layer, small_flat = {}, {}, None
        for stage, units, spans, sems, sent, land in pending:
            land = scatter_wait(f"rs_wait{stage}", sent, land, sems, after)
            total = sum_shares(f"rs_sum{stage}", land, me)
            off = 0
            for (name, layer, tr), (rows, r, c) in zip(units, spans):
                per_layer.setdefault(name, {})[layer] = (total[off:off + rows].reshape(r, c), tr)
                off += rows
            if stage == GRAD_STAGES - 1:
                small_flat = total[off:off + SMALL_ROWS].reshape(-1)
        for name, by_layer in per_layer.items():
            pieces = [by_layer[i][0] for i in sorted(by_layer)]
            res[name] = adamw_units("adamw_" + name, pieces, by_layer[0][1], local[name], mom[name], var[name])
        off = 0
        for name, _ in _SMALL_SHARDED:
            blk = local[name]
            g = small_flat[off:off + blk.size].reshape(blk.shape)
            off += blk.size
            res[name] = (g,) + adamw_native("adamw_" + name, g, blk, mom[name], var[name])
        return res

    return start, finish


def replicated_start(grads, loss):
    pack = jnp.concatenate([_rows128(grads[name]) for name in _REPLICATED] + [_rows128(loss)], axis=0)
    sems, srcs, lands, token = gather_ici_start("ag_g_start", [[pack]], [False])
    return sems[0], srcs[0], lands[0], token


def replicated_finish(handle, after, w, mom, var):
    sems, srcs, lands, _ = handle
    (buf,) = gather_d2d("ag_g_d2d", gather_ici_wait("ag_g_wait", srcs, lands, sems, after))
    rows = srcs[0].shape[0]
    total = sum_slots("ag_g_sum", buf.reshape(N_DEV, rows, LANES))
    res, off = {}, 0
    for name in _REPLICATED:
        n = w[name].size
        nr = -(-n // (LANES * 8)) * 8
        g = total[off:off + nr].reshape(-1)[:n].reshape(w[name].shape)
        off += nr
        res[name] = (g,) + adamw_native("adamw_" + name, g, w[name], mom[name], var[name])
    return res, total[off, 0]


def kernel(x, mem, e_norm, e_w_in, e_gmlp_w, e_gmlp_b, e_conv_w, e_conv_b, e_conv_ln_g, e_conv_ln_b, e_w_out, o_norm, o_w_in, o_lam_re, o_lam_im, o_log_dt, o_b_re, o_b_im, o_c_re, o_c_im, o_d, o_w_out, ca_norm, ca_mem_norm, ca_wq, ca_wk, ca_wv, ca_wo, ffn_norm, ffn_w_gate, ffn_w_up, ffn_w_down, final_norm, loss_target, m_e_norm, m_e_w_in, m_e_gmlp_w, m_e_gmlp_b, m_e_conv_w, m_e_conv_b, m_e_conv_ln_g, m_e_conv_ln_b, m_e_w_out, m_o_norm, m_o_w_in, m_o_lam_re, m_o_lam_im, m_o_log_dt, m_o_b_re, m_o_b_im, m_o_c_re, m_o_c_im, m_o_d, m_o_w_out, m_ca_norm, m_ca_mem_norm, m_ca_wq, m_ca_wk, m_ca_wv, m_ca_wo, m_ffn_norm, m_ffn_w_gate, m_ffn_w_up, m_ffn_w_down, m_final_norm, v_e_norm, v_e_w_in, v_e_gmlp_w, v_e_gmlp_b, v_e_conv_w, v_e_conv_b, v_e_conv_ln_g, v_e_conv_ln_b, v_e_w_out, v_o_norm, v_o_w_in, v_o_lam_re, v_o_lam_im, v_o_log_dt, v_o_b_re, v_o_b_im, v_o_c_re, v_o_c_im, v_o_d, v_o_w_out, v_ca_norm, v_ca_mem_norm, v_ca_wq, v_ca_wk, v_ca_wv, v_ca_wo, v_ffn_norm, v_ffn_w_gate, v_ffn_w_up, v_ffn_w_down, v_final_norm):
    given = dict(locals())
    local = {k: given[k] for k in _ORDER}
    mom = {k: given["m_" + k] for k in _ORDER}
    var = {k: given["v_" + k] for k in _ORDER}

    w = {}
    w.update({
        "e_norm": e_norm, "e_gmlp_w": e_gmlp_w[0], "e_gmlp_b": e_gmlp_b.reshape(A_GROUPS, GMLP_BLOCK, 1),
        "e_conv_b": e_conv_b, "e_conv_ln_g": e_conv_ln_g, "e_conv_ln_b": e_conv_ln_b,
        "o_lam_re": o_lam_re[0], "o_lam_im": o_lam_im[0], "o_log_dt": o_log_dt[0], "o_b_re": o_b_re[0], "o_b_im": o_b_im[0],
        "o_c_re": o_c_re[0], "o_c_im": o_c_im[0], "ca_norm": ca_norm, "ca_mem_norm": ca_mem_norm, "ffn_norm": ffn_norm,
        "final_norm": final_norm.reshape(1, D_MODEL),
    })
    start_reduce, finish_reduce = gradient_reducer(local, mom, var)
    fetch, token = weight_fetcher(local)
    loss_part, grad_x, grads = local_step(x[0], mem[0], loss_target[0], w, fetch, start_reduce, token[0:1, 0:1])
    grads["final_norm"] = grads["final_norm"].reshape(D_MODEL)

    handle = replicated_start(grads, loss_part)
    res = finish_reduce(handle[3])
    rep, loss = replicated_finish(handle, res["ffn_w_down"][1], local, mom, var)
    res.update(rep)
    return (loss, grad_x[None], *[res[k][0] for k in _ORDER], *[res[k][1] for k in _ORDER],
            *[res[k][2] for k in _ORDER], *[res[k][3] for k in _ORDER])
```

```python
import jax
import jax.numpy as jnp
from jax import lax
from jax.experimental import pallas as pl
from jax.experimental.pallas import tpu as pltpu

F32 = jnp.float32
BF16 = jnp.bfloat16
S = jax.ShapeDtypeStruct

D_MODEL = 1024
A_WIDTH = 512
A_GROUPS = 4
GMLP_BLOCK = 128
CHUNK = 64
B_WIDTH = 512
IN_WIDTH = 2 * A_WIDTH + 2 * B_WIDTH
CONV_WIDTH = 31
CONV_PAD = 32
C_WIDTH = 512
C_GROUP_CH = 16
C_GROUPS = 32
C_STATE = 64
N_STATE = C_GROUPS * C_STATE
CA_HEADS = 4
CA_HEAD_DIM = 256
FFN_HIDDEN = 2816
EPS = 1e-6
ADAM_LR = 0.001
ADAM_B1 = 0.9
ADAM_B2 = 0.999
ADAM_EPS = 1e-08
ADAM_WD = 0.01
ADAM_STEP = 10
N_DEV = 8
LANES = 128
VMEM_LIMIT = 56 << 20
VMEM_BUDGET = 40 << 20
MM_TN_RESIDENT = 8 << 20
MESH = pl.DeviceIdType.MESH
ANY = pl.BlockSpec(memory_space=pl.ANY)


def _cp(*sem):
    return pltpu.CompilerParams(dimension_semantics=sem, vmem_limit_bytes=VMEM_LIMIT)


def _tile(n, pref):
    t = pref
    while n % t:
        t //= 2
    return t


def _bf(v):
    return v if v.dtype == BF16 else v.astype(BF16)


def _sigmoid(x):
    return 1.0 / (1.0 + jnp.exp(-x))


_GC = 0.7978845608028654


def _gelu(x):
    return 0.5 * x * (1.0 + jnp.tanh(_GC * (x + 0.044715 * x * x * x)))


def _gelu_grad(x):
    x2 = x * x
    t = jnp.tanh(_GC * (x + 0.044715 * x * x2))
    return 0.5 * (1.0 + t) + 0.5 * x * (1.0 - t * t) * _GC * (1.0 + 3.0 * 0.044715 * x2)


def _tspec(entry, tm):
    if isinstance(entry, tuple):
        arr, cb, width = entry
        return arr, pl.BlockSpec((tm, width), lambda i, cb=cb: (i, cb))
    return entry, pl.BlockSpec((tm, entry.shape[1]), lambda i: (i, 0))


def rows_call(name, fn, tiled, full, outs, accs, tm=256):
    pairs = [_tspec(e, tm) for e in tiled]
    arrs = [p[0] for p in pairs]
    rows = arrs[0].shape[0]
    tm = _tile(rows, tm)
    pairs = [_tspec(e, tm) for e in tiled]
    n_in = len(tiled) + len(full)
    n_out = len(outs)

    def body(*refs):
        vals = [r[...] for r in refs[:n_in]]
        o_refs = refs[n_in:n_in + n_out]
        a_refs = refs[n_in + n_out:]
        ov, av = fn(*vals)
        for r, v in zip(o_refs, ov):
            r[...] = v.astype(r.dtype)
        if a_refs:
            @pl.when(pl.program_id(0) == 0)
            def _():
                for r in a_refs:
                    r[...] = jnp.zeros(r.shape, r.dtype)
            for r, v in zip(a_refs, av):
                r[...] += v

    in_specs = [p[1] for p in pairs] + [pl.BlockSpec(a.shape, lambda i, nd=a.ndim: (0,) * nd) for a in full]
    out_specs = [pl.BlockSpec((tm, c), lambda i: (i, 0)) for c, _ in outs]
    out_specs += [pl.BlockSpec(s, lambda i, nd=len(s): (0,) * nd) for s in accs]
    out_shape = [S((rows, c), dt) for c, dt in outs] + [S(s, F32) for s in accs]
    return pl.pallas_call(body, grid=(rows // tm,), in_specs=in_specs, out_specs=out_specs, out_shape=out_shape,
                          compiler_params=_cp("arbitrary"), name=name)(*arrs, *full)


def mm_nn(name, m, n, pairs, n_acc, epi, outs, tiled=(), cols=(), rowv=(), sums=(), norm_gain=None):
    a_ops, a_slot, b_arrs, b_specs, idx, trans = [], [], [], [], [], []
    fixed = 0
    for pair in pairs:
        a, b, k = pair[:3]
        bt = len(pair) > 3
        arr, cb, kdim = a if isinstance(a, tuple) else (a, 0, a.shape[1])
        key = (id(arr), cb, kdim)
        if key not in [o[0] for o in a_ops]:
            a_ops.append((key, arr, cb, kdim))
        a_slot.append([o[0] for o in a_ops].index(key))
        b_arr, off = b if isinstance(b, tuple) else (b, 0)
        b_arrs.append(b_arr)
        if bt:
            assert off % n == 0 and b_arr.shape[1] == kdim
            b_specs.append(pl.BlockSpec((n, kdim), lambda i, o=off // n: (o, 0), pipeline_mode=pl.Buffered(1)))
        else:
            assert b_arr.shape[1] == n
            b_specs.append(pl.BlockSpec((kdim, n), lambda i, o=off: (o, 0), pipeline_mode=pl.Buffered(1)))
        fixed += kdim * n * b_arr.dtype.itemsize
        idx.append(k)
        trans.append(bt)
    per_row = sum(2 * kdim * arr.dtype.itemsize for _, arr, _, kdim in a_ops)
    per_row += sum(2 * n * t.dtype.itemsize for t in tiled) + sum(2 * n * jnp.dtype(dt).itemsize for dt in outs)
    cn = n if sums or cols else (512 if n % 512 == 0 else 256)
    per_row += (n_acc + 3) * cn * 4
    tm = next((t for t in (1024, 512, 256, 128) if m % t == 0 and fixed + t * per_row <= VMEM_BUDGET), _tile(m, 128))
    n_a, n_p, n_t = len(a_ops), len(pairs), len(tiled)
    n_in = n_a + n_p + n_t + len(cols) + len(rowv)
    normed = norm_gain is not None
    o0 = n_in + normed

    def body(*refs):
        a_vals = [None if normed and i == 0 else _bf(r[...]) for i, r in enumerate(refs[:n_a])]
        if normed:
            xv = refs[0][...]
            rv = lax.rsqrt(jnp.mean(xv * xv, axis=-1, keepdims=True) + EPS)
            a_vals[0] = (xv * rv * refs[n_in][...]).astype(BF16)
            refs[o0 + len(outs)][...] = a_vals[0]
            refs[o0 + len(outs) + 1][...] = rv
        for j in range(n // cn):
            cs = slice(j * cn, (j + 1) * cn)
            accs = [None] * n_acc
            for p in range(n_p):
                av, b_ref = a_vals[a_slot[p]], refs[n_a + p]
                if trans[p]:
                    d = lax.dot_general(av, _bf(b_ref[cs, :]), (((1,), (1,)), ((), ())), preferred_element_type=F32)
                else:
                    d = jnp.dot(av, _bf(b_ref[:, cs]), preferred_element_type=F32)
                accs[idx[p]] = d if accs[idx[p]] is None else accs[idx[p]] + d
            extra = [r[:, cs] for r in refs[n_a + n_p:n_a + n_p + n_t]] + [r[...] for r in refs[n_a + n_p + n_t:n_in - len(rowv)]]
            extra += [r[:, cs] for r in refs[n_in - len(rowv):n_in]]
            ov = epi(accs, *extra)
            for r, v in zip(refs[o0:o0 + len(outs)], ov):
                r[:, cs] = v.astype(r.dtype)
        sv = ov[len(outs):]
        if sums:
            s_refs = refs[o0 + len(outs) + 2 * normed:]

            @pl.when(pl.program_id(0) == 0)
            def _():
                for r in s_refs:
                    r[...] = jnp.zeros(r.shape, r.dtype)
            for r, v in zip(s_refs, sv):
                r[...] += v

    in_specs = [pl.BlockSpec((tm, kdim), lambda i, cb=cb: (i, cb)) for _, _, cb, kdim in a_ops] + b_specs
    in_specs += [pl.BlockSpec((tm, n), lambda i: (i, 0)) for _ in tiled]
    in_specs += [pl.BlockSpec((tm, 1), lambda i: (i, 0)) for _ in cols]
    in_specs += [pl.BlockSpec((1, n), lambda i: (0, 0)) for _ in rowv]
    out_specs = [pl.BlockSpec((tm, n), lambda i: (i, 0)) for _ in outs]
    out_shape = [S((m, n), dt) for dt in outs]
    gain = []
    if normed:
        k0 = a_ops[0][3]
        gain = [norm_gain]
        in_specs.append(pl.BlockSpec((1, k0), lambda i: (0, 0)))
        out_specs += [pl.BlockSpec((tm, k0), lambda i: (i, 0)), pl.BlockSpec((tm, 1), lambda i: (i, 0))]
        out_shape += [S((m, k0), BF16), S((m, 1), F32)]
    out_specs += [pl.BlockSpec(s, lambda i, nd=len(s): (0,) * nd) for s in sums]
    out_shape += [S(s, F32) for s in sums]
    return pl.pallas_call(body, grid=(m // tm,), in_specs=in_specs, out_specs=out_specs, out_shape=out_shape,
                          compiler_params=_cp("arbitrary" if sums else "parallel"),
                          name=name)(*[o[1] for o in a_ops], *b_arrs, *tiled, *cols, *rowv, *gain)


def mm_tn(name, a, b, out_dtype=BF16):
    if isinstance(a, tuple):
        a_arr, a_cb, m = a
    else:
        a_arr, a_cb, m = a, None, a.shape[1]
    if isinstance(b, tuple):
        b_arr, b_cb, n = b
    else:
        b_arr, b_cb, n = b, None, b.shape[1]
    t = a_arr.shape[0]
    whole_b = t * n * b_arr.dtype.itemsize <= MM_TN_RESIDENT and b_cb is None
    tn = n if whole_b else _tile(n, 512)
    tm = _tile(m, 512 if t * 512 * a_arr.dtype.itemsize * 2 + t * tn * b_arr.dtype.itemsize * 2 <= VMEM_BUDGET else 256)
    a_off = 0 if a_cb is None else a_cb * (m // tm)
    b_off = 0 if b_cb is None else b_cb * (n // tn)

    def body(a_ref, b_ref, o_ref):
        o_ref[...] = lax.dot_general(_bf(a_ref[...]), _bf(b_ref[...]), (((0,), (0,)), ((), ())),
                                     preferred_element_type=F32).astype(o_ref.dtype)

    if whole_b:
        b_spec = pl.BlockSpec((t, n), lambda i, j: (0, 0), pipeline_mode=pl.Buffered(1))
    else:
        b_spec = pl.BlockSpec((t, tn), lambda i, j: (0, j + b_off))
    return pl.pallas_call(
        body, grid=(m // tm, n // tn),
        in_specs=[pl.BlockSpec((t, tm), lambda i, j: (0, i + a_off)), b_spec],
        out_specs=pl.BlockSpec((tm, tn), lambda i, j: (i, j)), out_shape=S((m, n), out_dtype),
        compiler_params=_cp("parallel", "parallel"), name=name)(a_arr, b_arr)


def rms_bwd_gain_only(name, dxn, x, r):
    def fn(dv, xv, rv):
        return [], [jnp.sum(dv * xv * rv, axis=0, keepdims=True)]
    return rows_call(name, fn, [dxn, x, r], [], [], [(1, x.shape[1])])[0]


def _final_loss_epi(accs, res, tv, g):
    xv = res + accs[0]
    d = xv.shape[-1]
    r = lax.rsqrt(jnp.mean(xv * xv, axis=-1, keepdims=True) + EPS)
    xh = xv * r
    err = xh * g - tv
    dy = err * (1.0 / d)
    w = dy * g
    dx = r * (w - xh * jnp.mean(w * xh, axis=-1, keepdims=True))
    part = jnp.sum(jnp.sum(err * err, axis=-1, keepdims=True), axis=0, keepdims=True) * (0.5 / d)
    return [dx, dx, jnp.sum(dy * xh, axis=0, keepdims=True), part]


def _gmlp_mask():
    row = lax.broadcasted_iota(jnp.int32, (GMLP_BLOCK, GMLP_BLOCK), 0) // CHUNK
    col = lax.broadcasted_iota(jnp.int32, (GMLP_BLOCK, GMLP_BLOCK), 1) // CHUNK
    return col <= row


def _ln_plain(v):
    mu = jnp.mean(v, axis=-1, keepdims=True)
    vc = v - mu
    rstd = lax.rsqrt(jnp.mean(vc * vc, axis=-1, keepdims=True) + EPS)
    return vc * rstd, rstd


def even_out_fwd(name, proj, hc, x, w, b, ln_g, ln_b, w_out, tm=512):
    t, d = x.shape
    tm = _tile(t, tm)

    def body(au_ref, av_ref, hc_ref, x_ref, w_ref, b_ref, lg_ref, lb_ref, wo_ref, x1_ref, oa_ref, ob_ref):
        mask = _gmlp_mask()
        u = _gelu(au_ref[...])
        vn, _ = _ln_plain(_gelu(av_ref[...]))
        vnb = _bf(vn)
        for g in range(A_GROUPS):
            wg = _bf(jnp.where(mask, w_ref[g], 0.0))
            cs = slice(g * GMLP_BLOCK, (g + 1) * GMLP_BLOCK)
            for n in range(tm // GMLP_BLOCK):
                rs = slice(n * GMLP_BLOCK, (n + 1) * GMLP_BLOCK)
                sg = jnp.dot(wg, vnb[rs, cs], preferred_element_type=F32) + b_ref[g]
                oa_ref[rs, cs] = (u[rs, cs] * sg).astype(oa_ref.dtype)
        y, _ = _ln_plain(hc_ref[...])
        z = y * lg_ref[...] + lb_ref[...]
        ob_ref[...] = (z * _sigmoid(z)).astype(ob_ref.dtype)
        x1_ref[...] = (x_ref[...] + jnp.dot(oa_ref[...], wo_ref[0:A_WIDTH, :], preferred_element_type=F32)
                       + jnp.dot(ob_ref[...], wo_ref[A_WIDTH:, :], preferred_element_type=F32))

    half = pl.BlockSpec((tm, A_WIDTH), lambda i: (i, 0))
    return pl.pallas_call(
        body, grid=(t // tm,),
        in_specs=[half, pl.BlockSpec((tm, A_WIDTH), lambda i: (i, 1)), half, pl.BlockSpec((tm, d), lambda i: (i, 0)),
                  _whole(w), _whole(b), _whole(ln_g), _whole(ln_b), _whole(w_out)],
        out_specs=[pl.BlockSpec((tm, d), lambda i: (i, 0)), half, half],
        out_shape=[S((t, d), F32), S((t, A_WIDTH), BF16), S((t, B_WIDTH), BF16)],
        compiler_params=_cp("parallel"), name=name)(proj, proj, hc, x, w, b, ln_g, ln_b, w_out)


def gmlp_bwd(name, proj, dxb, w_out, w, b, tm=512):
    t = proj.shape[0]
    tm = _tile(t, tm)

    def body(au_ref, av_ref, dx_ref, wo_ref, w_ref, b_ref, dp_ref, dw_ref, db_ref):
        @pl.when(pl.program_id(0) == 0)
        def _():
            dw_ref[...] = jnp.zeros(dw_ref.shape, F32)
            db_ref[...] = jnp.zeros(db_ref.shape, F32)

        mask = _gmlp_mask()
        au = au_ref[...]
        av = av_ref[...]
        u = _gelu(au)
        vn, rstd = _ln_plain(_gelu(av))
        vnb = _bf(vn)
        dout = lax.dot_general(dx_ref[...], wo_ref[0:A_WIDTH, :], _NT, preferred_element_type=F32)
        dvn_cols = []
        for g in range(A_GROUPS):
            wm = jnp.where(mask, w_ref[g], 0.0)
            wg = _bf(wm)
            wgt = _bf(wm.T)
            cs = slice(g * GMLP_BLOCK, (g + 1) * GMLP_BLOCK)
            dwg = jnp.zeros((GMLP_BLOCK, GMLP_BLOCK), F32)
            dbg = jnp.zeros((GMLP_BLOCK, 1), F32)
            dvn_rows = []
            for n in range(tm // GMLP_BLOCK):
                rs = slice(n * GMLP_BLOCK, (n + 1) * GMLP_BLOCK)
                sg = jnp.dot(wg, vnb[rs, cs], preferred_element_type=F32) + b_ref[g]
                dp_ref[rs, cs] = (dout[rs, cs] * sg * _gelu_grad(au[rs, cs])).astype(dp_ref.dtype)
                dsg = dout[rs, cs] * u[rs, cs]
                dsgb = _bf(dsg)
                dbg = dbg + jnp.sum(dsg, axis=1, keepdims=True)
                dwg = dwg + lax.dot_general(dsgb, vnb[rs, cs], (((1,), (1,)), ((), ())), preferred_element_type=F32)
                dvn_rows.append(jnp.dot(wgt, dsgb, preferred_element_type=F32))
            dw_ref[g] += jnp.where(mask, dwg, 0.0)
            db_ref[g] += dbg
            dvn_cols.append(jnp.concatenate(dvn_rows, axis=0))
        dvn = jnp.concatenate(dvn_cols, axis=1)
        dv = rstd * (dvn - jnp.mean(dvn, axis=-1, keepdims=True) - vn * jnp.mean(dvn * vn, axis=-1, keepdims=True))
        dp_ref[:, A_WIDTH:] = (dv * _gelu_grad(av)).astype(dp_ref.dtype)

    return pl.pallas_call(
        body, grid=(t // tm,),
        in_specs=[pl.BlockSpec((tm, A_WIDTH), lambda i: (i, 0)), pl.BlockSpec((tm, A_WIDTH), lambda i: (i, 1)),
                  pl.BlockSpec((tm, dxb.shape[1]), lambda i: (i, 0)), pl.BlockSpec(w_out.shape, lambda i: (0, 0)),
                  pl.BlockSpec(w.shape, lambda i: (0, 0, 0)), pl.BlockSpec(b.shape, lambda i: (0, 0, 0))],
        out_specs=[pl.BlockSpec((tm, 2 * A_WIDTH), lambda i: (i, 0)),
                   pl.BlockSpec(w.shape, lambda i: (0, 0, 0)), pl.BlockSpec(b.shape, lambda i: (0, 0, 0))],
        out_shape=[S((t, 2 * A_WIDTH), BF16), S(w.shape, F32), S(b.shape, F32)],
        compiler_params=_cp("arbitrary"), name=name)(proj, proj, dxb, w_out, w, b)


CONV_ROWS = 256
CONV_ROWS_BWD = 64


def conv_fwd(name, proj, w, cb):
    t = proj.shape[0]
    tc = LANES
    rows = _tile(t, CONV_ROWS)
    a_cb, g_cb = 2 * A_WIDTH // tc, (2 * A_WIDTH + B_WIDTH) // tc

    def body(a_ref, g_ref, w_ref, cb_ref, o_ref, hpad):
        hpad[0:CONV_PAD, :] = jnp.zeros((CONV_PAD, tc), F32)

        def fill(i, _):
            r0 = pl.multiple_of(i * rows, rows)
            hpad[pl.ds(CONV_PAD + r0, rows), :] = a_ref[pl.ds(r0, rows), :] * _sigmoid(g_ref[pl.ds(r0, rows), :])
            return 0
        lax.fori_loop(0, t // rows, fill, 0)

        def conv(i, _):
            r0 = pl.multiple_of(i * rows, rows)
            win = hpad[pl.ds(r0, rows + CONV_PAD), :]
            acc = jnp.zeros((rows, tc), F32) + cb_ref[...]
            for b in range(SUB):
                wb = win if b == 0 else pltpu.roll(win, b, 0)
                for a in range(CONV_PAD // SUB):
                    k = CONV_WIDTH - 1 - (SUB * a + b)
                    if k >= 0:
                        lo = CONV_PAD - SUB * a
                        acc = acc + wb[lo:lo + rows, :] * w_ref[k:k + 1, :]
            o_ref[pl.ds(r0, rows), :] = acc
            return 0
        lax.fori_loop(0, t // rows, conv, 0)

    return pl.pallas_call(
        body, grid=(B_WIDTH // tc,),
        in_specs=[pl.BlockSpec((t, tc), lambda j: (0, a_cb + j)), pl.BlockSpec((t, tc), lambda j: (0, g_cb + j)),
                  pl.BlockSpec((CONV_WIDTH, tc), lambda j: (0, j)), pl.BlockSpec((1, tc), lambda j: (0, j))],
        out_specs=pl.BlockSpec((t, tc), lambda j: (0, j)), out_shape=S((t, B_WIDTH), F32),
        scratch_shapes=[pltpu.VMEM((t + CONV_PAD, tc), F32)],
        compiler_params=_cp("parallel"), name=name)(proj, proj, w, cb)


def conv_bwd(name, proj, dhc, w):
    t = proj.shape[0]
    tc = LANES
    rows = _tile(t, CONV_ROWS_BWD)
    a_cb, g_cb = 2 * A_WIDTH // tc, (2 * A_WIDTH + B_WIDTH) // tc
    win_rows = rows + CONV_PAD

    def body(a_ref, g_ref, d_ref, w_ref, da_ref, dg_ref, dw_ref, dcb_ref, hpad, dpad, dwacc):
        hpad[0:CONV_PAD, :] = jnp.zeros((CONV_PAD, tc), F32)
        dpad[t:t + CONV_PAD, :] = jnp.zeros((CONV_PAD, tc), F32)
        dwacc[...] = jnp.zeros(dwacc.shape, F32)

        def fill(i, _):
            r0 = pl.multiple_of(i * rows, rows)
            hpad[pl.ds(CONV_PAD + r0, rows), :] = a_ref[pl.ds(r0, rows), :] * _sigmoid(g_ref[pl.ds(r0, rows), :])
            dpad[pl.ds(r0, rows), :] = d_ref[pl.ds(r0, rows), :]
            return 0
        lax.fori_loop(0, t // rows, fill, 0)

        def step(i, dcb):
            r0 = pl.multiple_of(i * rows, rows)
            hwin = hpad[pl.ds(r0, win_rows), :]
            dwin = dpad[pl.ds(r0, win_rows), :]
            dchunk = dwin[:rows, :]
            dh = jnp.zeros((rows, tc), F32)
            for b in range(SUB):
                hb = hwin if b == 0 else pltpu.roll(hwin, b, 0)
                db = dwin if b == 0 else pltpu.roll(dwin, win_rows - b, 0)
                for a in range(CONV_PAD // SUB):
                    k = CONV_WIDTH - 1 - (SUB * a + b)
                    if k >= 0:
                        dh = dh + db[SUB * a:SUB * a + rows, :] * w_ref[k:k + 1, :]
                        lo = CONV_PAD - SUB * a
                        prod = dchunk * hb[lo:lo + rows, :]
                        dwacc[k] += jnp.sum(prod.reshape(rows // 8, 8, tc), axis=0)
            a = a_ref[pl.ds(r0, rows), :]
            sg = _sigmoid(g_ref[pl.ds(r0, rows), :])
            da_ref[pl.ds(r0, rows), :] = (dh * sg).astype(da_ref.dtype)
            dg_ref[pl.ds(r0, rows), :] = (dh * a * sg * (1.0 - sg)).astype(dg_ref.dtype)
            return dcb + jnp.sum(dchunk, axis=0, keepdims=True)
        dcb = lax.fori_loop(0, t // rows, step, jnp.zeros((1, tc), F32))
        dcb_ref[...] = dcb
        for k in range(CONV_WIDTH):
            dw_ref[k:k + 1, :] = jnp.sum(dwacc[k], axis=0, keepdims=True)

    return pl.pallas_call(
        body, grid=(B_WIDTH // tc,),
        in_specs=[pl.BlockSpec((t, tc), lambda j: (0, a_cb + j)), pl.BlockSpec((t, tc), lambda j: (0, g_cb + j)),
                  pl.BlockSpec((t, tc), lambda j: (0, j)), pl.BlockSpec((CONV_WIDTH, tc), lambda j: (0, j))],
        out_specs=[pl.BlockSpec((t, tc), lambda j: (0, j)), pl.BlockSpec((t, tc), lambda j: (0, j)),
                   pl.BlockSpec((CONV_WIDTH, tc), lambda j: (0, j)), pl.BlockSpec((1, tc), lambda j: (0, j))],
        out_shape=[S((t, B_WIDTH), BF16), S((t, B_WIDTH), BF16), S((CONV_WIDTH, B_WIDTH), F32), S((1, B_WIDTH), F32)],
        scratch_shapes=[pltpu.VMEM((t + CONV_PAD, tc), F32), pltpu.VMEM((t + CONV_PAD, tc), F32),
                        pltpu.VMEM((CONV_WIDTH, 8, tc), F32)],
        compiler_params=_cp("parallel"), name=name)(proj, proj, dhc, w)


def ln_silu_bwd(name, hc, dxb, w_out, g, b):
    c = hc.shape[1]

    def fn(h, dxv, wv, gv, bv):
        dout = lax.dot_general(dxv, wv[A_WIDTH:, :], _NT, preferred_element_type=F32)
        y, rstd = _ln_plain(h)
        z = y * gv + bv
        s = _sigmoid(z)
        dz = dout * s * (1.0 + z * (1.0 - s))
        dyv = dz * gv
        dh = rstd * (dyv - jnp.mean(dyv, axis=-1, keepdims=True) - y * jnp.mean(dyv * y, axis=-1, keepdims=True))
        return [dh], [jnp.sum(dz * y, axis=0, keepdims=True), jnp.sum(dz, axis=0, keepdims=True)]

    return rows_call(name, fn, [hc, dxb], [w_out, g, b], [(c, F32)], [(1, c), (1, c)])


_NT = (((1,), (1,)), ((), ()))
_TN = (((0,), (0,)), ((), ()))


def attn_fwd(name, x, gain, wq, k, v, wo, tm=512):
    t, d = x.shape
    m = k.shape[0]
    tm = _tile(t, tm)
    scale = CA_HEAD_DIM ** -0.5

    def body(x_ref, g_ref, wq_ref, k_ref, v_ref, wo_ref, x1_ref, xn_ref, r_ref, q_ref, o_ref):
        xv = x_ref[...]
        rv = lax.rsqrt(jnp.mean(xv * xv, axis=-1, keepdims=True) + EPS)
        xn = (xv * rv * g_ref[...]).astype(BF16)
        xn_ref[...] = xn
        r_ref[...] = rv
        q_ref[...] = jnp.dot(xn, wq_ref[...], preferred_element_type=F32).astype(BF16)
        for h in range(CA_HEADS):
            cs = slice(h * CA_HEAD_DIM, (h + 1) * CA_HEAD_DIM)
            s = lax.dot_general(q_ref[:, cs], k_ref[:, cs], _NT, preferred_element_type=F32) * scale
            e = jnp.exp(s - jnp.max(s, axis=-1, keepdims=True))
            p = e / jnp.sum(e, axis=-1, keepdims=True)
            o_ref[:, cs] = jnp.dot(_bf(p), v_ref[:, cs], preferred_element_type=F32).astype(o_ref.dtype)
        x1_ref[...] = xv + jnp.dot(o_ref[...], wo_ref[...], preferred_element_type=F32)

    def whole(a):
        return pl.BlockSpec(a.shape, lambda i: (0, 0), pipeline_mode=pl.Buffered(1))

    rows = pl.BlockSpec((tm, d), lambda i: (i, 0))
    col = pl.BlockSpec((tm, 1), lambda i: (i, 0))
    return pl.pallas_call(
        body, grid=(t // tm,),
        in_specs=[rows, whole(gain), whole(wq), whole(k), whole(v), whole(wo)],
        out_specs=[rows, rows, col, rows, rows],
        out_shape=[S((t, d), F32), S((t, d), BF16), S((t, 1), F32), S((t, d), BF16), S((t, d), BF16)],
        compiler_params=_cp("parallel"), name=name)(x, gain, wq, k, v, wo)


def attn_bwd(name, dx, dxb, x, r, gain, q, k, v, wq, wo, tm=512):
    t, d = q.shape
    m = k.shape[0]
    tm = _tile(t, tm)
    scale = CA_HEAD_DIM ** -0.5

    def body(dx_ref, dxb_ref, x_ref, r_ref, g_ref, q_ref, k_ref, v_ref, wq_ref, wo_ref,
             dxo_ref, dxbo_ref, dq_ref, dk_ref, dv_ref, dg_ref, do_s):
        @pl.when(pl.program_id(0) == 0)
        def _():
            dk_ref[...] = jnp.zeros(dk_ref.shape, F32)
            dv_ref[...] = jnp.zeros(dv_ref.shape, F32)
            dg_ref[...] = jnp.zeros(dg_ref.shape, F32)

        do_s[...] = lax.dot_general(dxb_ref[...], wo_ref[...], _NT, preferred_element_type=F32).astype(BF16)
        for h in range(CA_HEADS):
            cs = slice(h * CA_HEAD_DIM, (h + 1) * CA_HEAD_DIM)
            qh, kh, vh, doh = q_ref[:, cs], k_ref[:, cs], v_ref[:, cs], do_s[:, cs]
            s = lax.dot_general(qh, kh, _NT, preferred_element_type=F32) * scale
            e = jnp.exp(s - jnp.max(s, axis=-1, keepdims=True))
            p = e / jnp.sum(e, axis=-1, keepdims=True)
            pb = _bf(p)
            dv_ref[:, cs] += lax.dot_general(pb, doh, _TN, preferred_element_type=F32)
            dp = lax.dot_general(doh, vh, _NT, preferred_element_type=F32)
            ds = _bf(p * (dp - jnp.sum(dp * p, axis=-1, keepdims=True)) * scale)
            dq_ref[:, cs] = jnp.dot(ds, kh, preferred_element_type=F32).astype(dq_ref.dtype)
            dk_ref[:, cs] += lax.dot_general(ds, qh, _TN, preferred_element_type=F32)
        dxn = lax.dot_general(dq_ref[...], wq_ref[...], _NT, preferred_element_type=F32)
        xh = x_ref[...] * r_ref[...]
        wv = dxn * g_ref[...]
        dxo = dx_ref[...] + r_ref[...] * (wv - xh * jnp.mean(wv * xh, axis=-1, keepdims=True))
        dxo_ref[...] = dxo
        dxbo_ref[...] = dxo.astype(BF16)
        dg_ref[...] += jnp.sum(dxn * xh, axis=0, keepdims=True)

    def whole(a):
        return pl.BlockSpec(a.shape, lambda i: (0, 0), pipeline_mode=pl.Buffered(1))

    rows = pl.BlockSpec((tm, d), lambda i: (i, 0))
    col = pl.BlockSpec((tm, 1), lambda i: (i, 0))
    acc = pl.BlockSpec((m, d), lambda i: (0, 0))
    return pl.pallas_call(
        body, grid=(t // tm,),
        in_specs=[rows, rows, rows, col, whole(gain), rows, whole(k), whole(v), whole(wq), whole(wo)],
        out_specs=[rows, rows, rows, acc, acc, pl.BlockSpec((1, d), lambda i: (0, 0))],
        out_shape=[S((t, d), F32), S((t, d), BF16), S((t, d), BF16), S((m, d), F32), S((m, d), F32), S((1, d), F32)],
        scratch_shapes=[pltpu.VMEM((tm, d), BF16)],
        compiler_params=_cp("arbitrary"), name=name)(dx, dxb, x, r, gain, q, k, v, wq, wo)


SUB = 8
S5_ROWS = 256


S5_BLOCKS = 4
BLOCK_CH = C_WIDTH // S5_BLOCKS
BLOCK_ST = N_STATE // S5_BLOCKS
_S5_BLOCKS = tuple((slice(BLOCK_CH * q, BLOCK_CH * (q + 1)), slice(BLOCK_ST * q, BLOCK_ST * (q + 1)),
                    slice(N_STATE + BLOCK_ST * q, N_STATE + BLOCK_ST * (q + 1))) for q in range(S5_BLOCKS))
_HI = lax.Precision.HIGHEST
_GP = (C_GROUPS, C_STATE)
_RP = (C_WIDTH, C_STATE)


def _zoh(lr, li, ldt):
    dt = jnp.exp(ldt)
    mag = jnp.exp(lr * dt)
    ar = mag * jnp.cos(li * dt)
    ai = mag * jnp.sin(li * dt)
    den = lr * lr + li * li
    qr = ((ar - 1.0) * lr + ai * li) / den
    qi = (ai * lr - (ar - 1.0) * li) / den
    return dt, ar, ai, den, qr, qi


def _per_channel(v):
    return jnp.broadcast_to(v[:, None, :], (C_GROUPS, C_GROUP_CH, C_STATE)).reshape(_RP)


def _same_group(shape, row_per_group, col_per_group):
    rows = lax.broadcasted_iota(jnp.int32, shape, 0) // row_per_group
    cols = lax.broadcasted_iota(jnp.int32, shape, 1) // col_per_group
    return rows == cols


def _spread(shape, axis):
    long = lax.broadcasted_iota(jnp.int32, shape, axis) % C_STATE
    short = lax.broadcasted_iota(jnp.int32, shape, 1 - axis)
    return long == short


def s5_discretise(name, lam_re, lam_im, log_dt, bt_re, bt_im):
    def body(lr_ref, li_ref, ldt_ref, btr_ref, bti_ref, a_ref, bbr_ref, bbi_ref):
        _, ar, ai, _, qr, qi = _zoh(lr_ref[...], li_ref[...], ldt_ref[...])
        a_ref[0] = ar
        a_ref[1] = ai
        q2r, q2i = _per_channel(qr), _per_channel(qi)
        btr, bti = btr_ref[...], bti_ref[...]
        bbr_ref[...] = q2r * btr - q2i * bti
        bbi_ref[...] = q2r * bti + q2i * btr

    return pl.pallas_call(body, out_shape=[S((2,) + _GP, F32), S(_RP, F32), S(_RP, F32)],
                          name=name)(lam_re, lam_im, log_dt, bt_re, bt_im)


def s5_operands(name, a, bbr, bbi, c2r, c2i, ctr, cti):
    ns = N_STATE

    def body(a_ref, bbr_ref, bbi_ref, c2r_ref, c2i_ref, ctr_ref, cti_ref, pw_ref, qw_ref, mb_ref, mc_ref, mct_ref):
        ar, ai = a_ref[0:1, :], a_ref[1:2, :]
        pows = [(ar, ai)]
        for _ in range(SUB - 1):
            pr, pi = pows[-1]
            pows.append((pr * ar - pi * ai, pr * ai + pi * ar))
        rows = lax.broadcasted_iota(jnp.int32, (SUB, ns), 0)

        def rows_of(v):
            return jnp.broadcast_to(v, (SUB, ns))

        for k, s in enumerate((1, 2, 4)):
            pr, pi = rows_of(pows[s - 1][0]), rows_of(pows[s - 1][1])
            pw_ref[k, 0] = jnp.where(rows >= s, pr, 0.0)
            pw_ref[k, 1] = jnp.where(rows >= s, pi, 0.0)
            qw_ref[k, 0] = jnp.where(rows + s <= SUB - 1, pr, 0.0)
            qw_ref[k, 1] = jnp.where(rows + s <= SUB - 1, -pi, 0.0)
        fr = fi = br = bi = jnp.zeros((SUB, ns), F32)
        for i in range(SUB):
            fr = jnp.where(rows == i, rows_of(pows[i][0]), fr)
            fi = jnp.where(rows == i, rows_of(pows[i][1]), fi)
            br = jnp.where(rows == i, rows_of(pows[SUB - 1 - i][0]), br)
            bi = jnp.where(rows == i, rows_of(-pows[SUB - 1 - i][1]), bi)
        pw_ref[3, 0], pw_ref[3, 1], qw_ref[3, 0], qw_ref[3, 1] = fr, fi, br, bi

        wide = _spread((C_STATE, ns), 1).astype(BF16)
        tall = _spread((ns, C_STATE), 0).astype(BF16)
        in_rows = _same_group((C_WIDTH, ns), C_GROUP_CH, C_STATE)
        in_cols = _same_group((ns, C_WIDTH), C_STATE, C_GROUP_CH)

        def across(v, sign=1.0):
            return jnp.where(in_rows, sign * jnp.dot(_bf(v), wide, preferred_element_type=F32), 0.0).astype(BF16)

        def down(vt, sign=1.0):
            return jnp.where(in_cols, sign * jnp.dot(tall, _bf(vt), preferred_element_type=F32), 0.0).astype(BF16)

        mb_ref[:, 0:ns] = across(bbr_ref[...])
        mb_ref[:, ns:2 * ns] = across(bbi_ref[...])
        mct_ref[:, 0:ns] = across(c2r_ref[...])
        mct_ref[:, ns:2 * ns] = across(c2i_ref[...], -1.0)
        mc_ref[0:ns, :] = down(ctr_ref[...])
        mc_ref[ns:2 * ns, :] = down(cti_ref[...], -1.0)

    return pl.pallas_call(
        body, out_shape=[S((4, 2, SUB, ns), F32), S((4, 2, SUB, ns), F32), S((C_WIDTH, 2 * ns), BF16),
                         S((2 * ns, C_WIDTH), BF16), S((C_WIDTH, 2 * ns), BF16)],
        compiler_params=pltpu.CompilerParams(vmem_limit_bytes=VMEM_LIMIT), name=name)(a, bbr, bbi, c2r, c2i, ctr, cti)


def s5_block_grads(name, u, lamb, xsb, dyb):
    t = u.shape[0]

    def mb_body(u_ref, lr_ref, li_ref, o_ref):
        ub = _bf(u_ref[...])
        o_ref[:, 0:BLOCK_ST] = lax.dot_general(ub, lr_ref[...], _TN, preferred_element_type=F32)
        o_ref[:, BLOCK_ST:2 * BLOCK_ST] = lax.dot_general(ub, li_ref[...], _TN, preferred_element_type=F32)

    d_mb = pl.pallas_call(
        mb_body, grid=(S5_BLOCKS,),
        in_specs=[pl.BlockSpec((t, BLOCK_CH), lambda q: (0, q)), pl.BlockSpec((t, BLOCK_ST), lambda q: (0, q)),
                  pl.BlockSpec((t, BLOCK_ST), lambda q: (0, S5_BLOCKS + q))],
        out_specs=pl.BlockSpec((BLOCK_CH, 2 * BLOCK_ST), lambda q: (q, 0)), out_shape=S((C_WIDTH, 2 * BLOCK_ST), F32),
        compiler_params=_cp("parallel"), name=name + "_b")(u, lamb, lamb)

    def mc_body(x_ref, dy_ref, o_ref):
        o_ref[...] = lax.dot_general(x_ref[...], dy_ref[...], _TN, preferred_element_type=F32)

    d_mc = pl.pallas_call(
        mc_body, grid=(2, S5_BLOCKS),
        in_specs=[pl.BlockSpec((t, BLOCK_ST), lambda p, q: (0, p * S5_BLOCKS + q)), pl.BlockSpec((t, BLOCK_CH), lambda p, q: (0, q))],
        out_specs=pl.BlockSpec((BLOCK_ST, BLOCK_CH), lambda p, q: (p * S5_BLOCKS + q, 0)),
        out_shape=S((2 * N_STATE, BLOCK_CH), F32), compiler_params=_cp("parallel", "parallel"), name=name + "_c")(xsb, dyb)
    return d_mb, d_mc


def s5_param_grads(name, d_mb, d_mc, da, lam_re, lam_im, log_dt, bt_re, bt_im):
    ns = N_STATE

    def body(dmb_ref, dmc_ref, da_ref, lr_ref, li_ref, ldt_ref, btr_ref, bti_ref,
             glr_ref, gli_ref, gdt_ref, gbr_ref, gbi_ref, gcr_ref, gci_ref):
        lr, li = lr_ref[...], li_ref[...]
        dt, ar, ai, den, qr, qi = _zoh(lr, li, ldt_ref[...])
        per_block = C_GROUPS // S5_BLOCKS
        wide = _spread((C_STATE, BLOCK_ST), 1).astype(F32)
        tall = _spread((BLOCK_ST, C_STATE), 0).astype(F32)
        rows = lax.broadcasted_iota(jnp.int32, (C_WIDTH, BLOCK_ST), 0) // C_GROUP_CH % per_block
        in_rows = rows == lax.broadcasted_iota(jnp.int32, (C_WIDTH, BLOCK_ST), 1) // C_STATE
        in_cols = _same_group((BLOCK_ST, BLOCK_CH), C_STATE, C_GROUP_CH)

        def fold_rows(v):
            return lax.dot_general(jnp.where(in_rows, v, 0.0), wide, (((1,), (1,)), ((), ())), precision=_HI,
                                   preferred_element_type=F32)

        def fold_cols(v):
            return lax.dot_general(jnp.where(in_cols, v, 0.0), tall, (((0,), (0,)), ((), ())), precision=_HI,
                                   preferred_element_type=F32)

        for cs, s_re, s_im in _S5_BLOCKS:
            gcr_ref[cs, :] = fold_cols(dmc_ref[s_re, :])
            gci_ref[cs, :] = -fold_cols(dmc_ref[s_im, :])
        gbbr = fold_rows(dmb_ref[:, 0:BLOCK_ST])
        gbbi = fold_rows(dmb_ref[:, BLOCK_ST:2 * BLOCK_ST])
        btr, bti = btr_ref[...], bti_ref[...]
        q2r, q2i = _per_channel(qr), _per_channel(qi)
        gbr_ref[...] = q2r * gbbr + q2i * gbbi
        gbi_ref[...] = q2r * gbbi - q2i * gbbr

        def per_group(v):
            return jnp.sum(v.reshape(C_GROUPS, C_GROUP_CH, C_STATE), axis=1)

        gqr = per_group(btr * gbbr + bti * gbbi)
        gqi = per_group(btr * gbbi - bti * gbbr)
        ilr, ili = lr / den, li / den
        gar = da_ref[0] + ilr * gqr - ili * gqi
        gai = da_ref[1] + ilr * gqi + ili * gqr
        sr = (qr * lr + qi * li) / den
        si = (qi * lr - qr * li) / den
        gzr = ar * gar + ai * gai
        gzi = ar * gai - ai * gar
        glr_ref[...] = -sr * gqr - si * gqi + dt * gzr
        gli_ref[...] = -sr * gqi + si * gqr + dt * gzi
        gdt_ref[...] = jnp.sum(lr * gzr + li * gzi, axis=1, keepdims=True) * dt

    return pl.pallas_call(
        body, out_shape=[S(_GP, F32), S(_GP, F32), S((C_GROUPS, 1), F32), S(_RP, F32), S(_RP, F32), S(_RP, F32), S(_RP, F32)],
        compiler_params=pltpu.CompilerParams(vmem_limit_bytes=VMEM_LIMIT), name=name,
    )(d_mb, d_mc, da, lam_re, lam_im, log_dt, bt_re, bt_im)


def _cmul_add(xr, xi, pr, pi, zr, zi):
    return xr + pr * zr - pi * zi, xi + pr * zi + pi * zr


def s5_fwd(name, u, mb, mc, pw, dskip):
    t = u.shape[0]
    tm = _tile(t, S5_ROWS)
    ns = N_STATE

    def body(u_ref, mb_ref, mc_ref, pw_ref, d_ref, gy_ref, y_ref, xs_ref, xb_ref, carry):
        @pl.when(pl.program_id(0) == 0)
        def _():
            carry[...] = jnp.zeros(carry.shape, F32)

        uv = u_ref[...]
        ub = _bf(uv)
        for cs, s_re, s_im in _S5_BLOCKS:
            xs_ref[:, s_re] = jnp.dot(ub[:, cs], mb_ref[cs, s_re], preferred_element_type=F32)
            xs_ref[:, s_im] = jnp.dot(ub[:, cs], mb_ref[cs, s_im], preferred_element_type=F32)

        def group(i, _):
            r0 = pl.multiple_of(i * SUB, SUB)
            xr = xs_ref[pl.ds(r0, SUB), 0:ns]
            xi = xs_ref[pl.ds(r0, SUB), ns:2 * ns]
            for k, s in enumerate((1, 2, 4)):
                xr, xi = _cmul_add(xr, xi, pw_ref[k, 0], pw_ref[k, 1], pltpu.roll(xr, s, 0), pltpu.roll(xi, s, 0))
            xr, xi = _cmul_add(xr, xi, pw_ref[3, 0], pw_ref[3, 1], carry[0], carry[1])
            xs_ref[pl.ds(r0, SUB), 0:ns] = xr
            xs_ref[pl.ds(r0, SUB), ns:2 * ns] = xi
            carry[0] = jnp.broadcast_to(xr[SUB - 1:SUB, :], (SUB, ns))
            carry[1] = jnp.broadcast_to(xi[SUB - 1:SUB, :], (SUB, ns))
            return 0
        lax.fori_loop(0, tm // SUB, group, 0)

        xb_ref[...] = _bf(xs_ref[...])
        for cs, s_re, s_im in _S5_BLOCKS:
            y = (jnp.dot(xb_ref[:, s_re], mc_ref[s_re, cs], preferred_element_type=F32)
                 + jnp.dot(xb_ref[:, s_im], mc_ref[s_im, cs], preferred_element_type=F32) + d_ref[:, cs] * uv[:, cs])
            y_ref[:, cs] = y
            gy_ref[:, cs] = _gelu(y).astype(gy_ref.dtype)

    c = u.shape[1]
    return pl.pallas_call(
        body, grid=(t // tm,),
        in_specs=[pl.BlockSpec((tm, c), lambda i: (i, 0)), pl.BlockSpec(mb.shape, lambda i: (0, 0)),
                  pl.BlockSpec(mc.shape, lambda i: (0, 0)), pl.BlockSpec(pw.shape, lambda i: (0, 0, 0, 0)),
                  pl.BlockSpec((1, c), lambda i: (0, 0))],
        out_specs=[pl.BlockSpec((tm, c), lambda i: (i, 0)), pl.BlockSpec((tm, c), lambda i: (i, 0)),
                   pl.BlockSpec((tm, 2 * ns), lambda i: (i, 0)), pl.BlockSpec((tm, 2 * ns), lambda i: (i, 0))],
        out_shape=[S((t, c), BF16), S((t, c), F32), S((t, 2 * ns), F32), S((t, 2 * ns), BF16)],
        scratch_shapes=[pltpu.VMEM((2, SUB, ns), F32)],
        compiler_params=_cp("arbitrary"), name=name)(u, mb, mc, pw, dskip)


def s5_bwd(name, dgy, y, u, xs, mct, mbt, qw, dskip):
    t, c = u.shape
    tm = _tile(t, S5_ROWS)
    nt = t // tm
    ns = N_STATE
    ng = tm // SUB

    def body(dgy_ref, y_ref, u_ref, xs_ref, mct_ref, mbt_ref, qw_ref, d_ref,
             du_ref, dy_ref, lb_ref, da_ref, dd_ref, lam, carry):
        @pl.when(pl.program_id(0) == 0)
        def _():
            carry[...] = jnp.zeros(carry.shape, F32)
            da_ref[...] = jnp.zeros(da_ref.shape, F32)
            dd_ref[...] = jnp.zeros(dd_ref.shape, F32)

        uv = u_ref[...]
        dy = dgy_ref[...] * _gelu_grad(y_ref[...])
        dyb = _bf(dy)
        dy_ref[...] = dyb
        dd_ref[...] += jnp.sum(dy * uv, axis=0, keepdims=True)
        for cs, s_re, s_im in _S5_BLOCKS:
            lam[:, s_re] = jnp.dot(dyb[:, cs], mct_ref[cs, s_re], preferred_element_type=F32)
            lam[:, s_im] = jnp.dot(dyb[:, cs], mct_ref[cs, s_im], preferred_element_type=F32)
        last_row = lax.broadcasted_iota(jnp.int32, (SUB, ns), 0) == SUB - 1

        def group(j, _):
            i = ng - 1 - j
            r0 = pl.multiple_of(i * SUB, SUB)
            lr = lam[pl.ds(r0, SUB), 0:ns]
            li = lam[pl.ds(r0, SUB), ns:2 * ns]
            for k, s in enumerate((1, 2, 4)):
                lr, li = _cmul_add(lr, li, qw_ref[k, 0], qw_ref[k, 1],
                                   pltpu.roll(lr, SUB - s, 0), pltpu.roll(li, SUB - s, 0))
            cr, ci = carry[0], carry[1]
            lr, li = _cmul_add(lr, li, qw_ref[3, 0], qw_ref[3, 1], cr, ci)
            lam[pl.ds(r0, SUB), 0:ns] = lr
            lam[pl.ds(r0, SUB), ns:2 * ns] = li
            carry[0] = jnp.broadcast_to(lr[0:1, :], (SUB, ns))
            carry[1] = jnp.broadcast_to(li[0:1, :], (SUB, ns))
            nr = jnp.where(last_row, cr, pltpu.roll(lr, SUB - 1, 0))
            ni = jnp.where(last_row, ci, pltpu.roll(li, SUB - 1, 0))
            xr = xs_ref[pl.ds(r0, SUB), 0:ns]
            xi = xs_ref[pl.ds(r0, SUB), ns:2 * ns]
            da_ref[0] += nr * xr + ni * xi
            da_ref[1] += ni * xr - nr * xi
            return 0
        lax.fori_loop(0, ng, group, 0)

        lb_ref[...] = _bf(lam[...])
        for cs, s_re, s_im in _S5_BLOCKS:
            du = (jnp.dot(lb_ref[:, s_re], mbt_ref[s_re, cs], preferred_element_type=F32)
                  + jnp.dot(lb_ref[:, s_im], mbt_ref[s_im, cs], preferred_element_type=F32) + d_ref[:, cs] * dy[:, cs])
            du_ref[:, cs] = du.astype(du_ref.dtype)

    rev = lambda i: (nt - 1 - i, 0)
    return pl.pallas_call(
        body, grid=(nt,),
        in_specs=[pl.BlockSpec((tm, c), rev), pl.BlockSpec((tm, c), rev), pl.BlockSpec((tm, c), rev),
                  pl.BlockSpec((tm, 2 * ns), rev),
                  pl.BlockSpec(mct.shape, lambda i: (0, 0)), pl.BlockSpec(mbt.shape, lambda i: (0, 0)),
                  pl.BlockSpec(qw.shape, lambda i: (0, 0, 0, 0)), pl.BlockSpec((1, c), lambda i: (0, 0))],
        out_specs=[pl.BlockSpec((tm, c), rev), pl.BlockSpec((tm, c), rev), pl.BlockSpec((tm, 2 * ns), rev),
                   pl.BlockSpec((2, SUB, ns), lambda i: (0, 0, 0)), pl.BlockSpec((1, c), lambda i: (0, 0))],
        out_shape=[S((t, c), BF16), S((t, c), BF16), S((t, 2 * ns), BF16), S((2, SUB, ns), F32), S((1, c), F32)],
        scratch_shapes=[pltpu.VMEM((tm, 2 * ns), F32), pltpu.VMEM((2, SUB, ns), F32)],
        compiler_params=_cp("arbitrary"), name=name)(dgy, y, u, xs, mct, mbt, qw, dskip)


def _first(accs, *_):
    return [accs[0]]


def _rms_bwd_epi(accs, xv, base, rv, g):
    dv = accs[0]
    w = dv * g
    xh = xv * rv
    dx = base + rv * (w - xh * jnp.mean(w * xh, axis=-1, keepdims=True))
    return [dx, dx, jnp.sum(dv * xh, axis=0, keepdims=True)]


def mm_rms_bwd(name, pairs, x, r, gain, dres):
    t, d = x.shape
    return mm_nn(name, t, d, pairs, 1, _rms_bwd_epi, [F32, BF16], tiled=[x, dres], cols=[r], rowv=[gain], sums=[(1, d)])


def even_fwd(x, w, need_out):
    t = x.shape[0]
    proj, hn, r = mm_nn("e_in_f", t, IN_WIDTH, [(x, w["e_w_in_t"], 0, "t")], 1, _first, [F32], norm_gain=w["e_norm"])
    hc = conv_fwd("e_conv_f", proj, w["e_conv_w"], w["e_conv_b"])
    need_out(hc)
    x1, out_a, out_b = even_out_fwd("e_out_f", proj, hc, x, w["e_gmlp_w"], w["e_gmlp_b"], w["e_conv_ln_g"], w["e_conv_ln_b"],
                                    w["e_w_out"])
    return x1, (x, hn, r, proj, out_a, hc, out_b)


def even_bwd_mixers(dxb, saved, w):
    x, hn, r, proj, out_a, hc, out_b = saved
    t = x.shape[0]
    g_w_out = jnp.concatenate([mm_tn("e_out_wa", out_a, dxb), mm_tn("e_out_wb", out_b, dxb)], axis=0)
    dab, g_gw, g_gb = gmlp_bwd("e_gmlp_b", proj, dxb, w["e_w_out"], w["e_gmlp_w"], w["e_gmlp_b"])
    dhc, g_lg, g_lb = ln_silu_bwd("e_ln_b", hc, dxb, w["e_w_out"], w["e_conv_ln_g"], w["e_conv_ln_b"])
    dba, dbg, g_cw, g_cb = conv_bwd("e_conv_b", proj, dhc, w["e_conv_w"])
    g_w_in_t = jnp.concatenate([mm_tn("e_in_w0", dab, hn), mm_tn("e_in_w1", dba, hn), mm_tn("e_in_w2", dbg, hn)], axis=0)
    grads = dict(e_w_in_t=g_w_in_t, e_gmlp_w=g_gw[None], e_gmlp_b=g_gb.reshape(1, A_GROUPS, GMLP_BLOCK),
                 e_conv_w=g_cw[None], e_conv_b=g_cb, e_conv_ln_g=g_lg, e_conv_ln_b=g_lb, e_w_out=g_w_out)
    return (dab, dba, dbg), grads


def even_bwd_input(dx, dproj, saved, w):
    x, _, r = saved[:3]
    dab, dba, dbg = dproj
    w_in_t = w["e_w_in_t"]
    return mm_rms_bwd("e_in_b", [(dab, (w_in_t, 0), 0), (dba, (w_in_t, 2), 0), (dbg, (w_in_t, 3), 0)], x, r, w["e_norm"], dx)


def s5_setup(w, anchor=None):
    def rows(v):
        return v.transpose(0, 2, 1).reshape(_RP)

    log_dt = w["o_log_dt"].reshape(C_GROUPS, 1)
    if anchor is not None:
        log_dt = log_dt + anchor
    lam = (w["o_lam_re"], w["o_lam_im"], log_dt, rows(w["o_b_re"]), rows(w["o_b_im"]))
    a, bbr, bbi = s5_discretise("o_s5_zoh", *lam)
    c_re, c_im = w["o_c_re"], w["o_c_im"]
    pw, qw, mb, mc, mct = s5_operands("o_s5_ops", a.reshape(2, N_STATE), bbr, bbi, c_re.reshape(_RP), c_im.reshape(_RP),
                                      c_re.transpose(2, 0, 1).reshape(C_STATE, C_WIDTH),
                                      c_im.transpose(2, 0, 1).reshape(C_STATE, C_WIDTH))
    return dict(lam=lam, pw=pw, qw=qw, mb=mb, mc=mc, mct=mct, mbt=mb.T)


def odd_fwd(x, w, consts):
    t = x.shape[0]
    u, hn, r = mm_nn("o_in_f", t, C_WIDTH, [(x, w["o_w_in"], 0)], 1, _first, [F32], norm_gain=w["o_norm"])
    gy, y, xs, xsb = s5_fwd("o_s5_f", u, consts["mb"], consts["mc"], consts["pw"], w["o_d"])
    w_out_t = w["o_w_out_t"]

    def epi(accs, res):
        return [res + accs[0] * _sigmoid(accs[1]), accs[0], accs[1]]

    x1, o1, o2 = mm_nn("o_out_f", t, D_MODEL, [(gy, (w_out_t, 0), 0, "t"), (gy, (w_out_t, D_MODEL), 1, "t")], 2, epi,
                       [F32, BF16, BF16], tiled=[x])
    return x1, (x, hn, r, u, gy, y, xs, xsb, o1, o2)


def odd_bwd(dx, dxb, saved, w, consts):
    x, hn, r, u, gy, y, xs, xsb, o1, o2 = saved
    t = x.shape[0]

    def gate_bwd(dv, a, b, wv):
        a = a.astype(F32)
        sg = _sigmoid(b.astype(F32))
        do12 = jnp.concatenate([dv * sg, dv * a * sg * (1.0 - sg)], axis=1).astype(BF16)
        return [do12, jnp.dot(do12, wv, preferred_element_type=F32)], []

    do12, dgy = rows_call("o_out_b", gate_bwd, [dx, o1, o2], [w["o_w_out_t"]], [(2 * D_MODEL, BF16), (C_WIDTH, F32)], [])
    g_w_out_t = mm_tn("o_out_w", do12, gy)
    du, dyb, lamb, da8, g_d = s5_bwd("o_s5_b", dgy, y, u, xs, consts["mct"], consts["mbt"], consts["qw"], w["o_d"])
    d_mb, d_mc = s5_block_grads("o_s5_w", u, lamb, xsb, dyb)
    da = jnp.sum(da8, axis=1).reshape((2,) + _GP)
    g_lr, g_li, g_dt, g_btr, g_bti, g_cr, g_ci = s5_param_grads("o_s5_pg", d_mb, d_mc, da, *consts["lam"])

    def states_first(v):
        return v.reshape(C_GROUPS, C_GROUP_CH, C_STATE).transpose(0, 2, 1)[None]

    g_w_in = mm_tn("o_in_w", hn, du)
    dx0, dx0b, g_norm = mm_rms_bwd("o_in_b", [(du, w["o_w_in"], 0, "t")], x, r, w["o_norm"], dx)
    grads = dict(o_norm=g_norm, o_w_in=g_w_in, o_lam_re=g_lr[None], o_lam_im=g_li[None], o_log_dt=g_dt.reshape(1, C_GROUPS),
                 o_b_re=states_first(g_btr), o_b_im=states_first(g_bti),
                 o_c_re=g_cr.reshape((1, C_GROUPS, C_GROUP_CH, C_STATE)), o_c_im=g_ci.reshape((1, C_GROUPS, C_GROUP_CH, C_STATE)),
                 o_d=g_d, o_w_out_t=g_w_out_t)
    return dx0, dx0b, grads


def ca_fwd(i, x, mem, w):
    t, m = x.shape[0], mem.shape[0]
    k, v, mn, rm = mm_nn(f"ca{i}_kv_f", m, D_MODEL, [(mem, w["ca_wk"][i], 0), (mem, w["ca_wv"][i], 1)], 2,
                         lambda accs: [accs[0], accs[1]], [BF16, BF16], norm_gain=w["ca_mem_norm"][i:i + 1])
    x1, xn, r, q, o = attn_fwd(f"ca{i}_attn_f", x, w["ca_norm"][i:i + 1], w["ca_wq"][i], k, v, w["ca_wo"][i])
    return x1, (x, xn, r, mn, rm, q, k, v, o)


def ca_bwd(i, dx, dxb, saved, mem, w):
    x, xn, r, mn, rm, q, k, v, o = saved
    t, m = x.shape[0], mem.shape[0]
    g_wo = mm_tn(f"ca{i}_o_w", o, dxb)
    dx0, dx0b, dq, dk, dv, g_norm = attn_bwd(f"ca{i}_attn_b", dx, dxb, x, r, w["ca_norm"][i:i + 1], q, k, v,
                                             w["ca_wq"][i], w["ca_wo"][i])
    g_wq = mm_tn(f"ca{i}_q_w", xn, dq)
    g_wk = mm_tn(f"ca{i}_k_w", mn, dk)
    g_wv = mm_tn(f"ca{i}_v_w", mn, dv)
    (dmn,) = mm_nn(f"ca{i}_kv_b", m, D_MODEL, [(dk, w["ca_wk"][i], 0, "t"), (dv, w["ca_wv"][i], 0, "t")], 1, _first, [F32])
    g_mnorm = rms_bwd_gain_only(f"ca{i}_mnorm_b", dmn, mem, rm)
    return dx0, dx0b, dict(ca_norm=g_norm, ca_mem_norm=g_mnorm, ca_wq=g_wq, ca_wk=g_wk, ca_wv=g_wv, ca_wo=g_wo)


FFN_ROWS = 512
FFN_CHUNK = 256


def _whole(a):
    return pl.BlockSpec(a.shape, lambda i: (0,) * a.ndim, pipeline_mode=pl.Buffered(1))


def ffn_fused_fwd(name, x, gain, wg_t, wu_t, wd, target=None, final_gain=None):
    t, d = x.shape
    hid = wd.shape[0]
    tm = _tile(t, FFN_ROWS)
    last = target is not None
    n_main = 4 if last else 1

    def body(*refs):
        x_ref, g_ref, wg_ref, wu_ref, wd_ref = refs[:5]
        rest = refs[5:]
        if last:
            tgt_ref, fg_ref = rest[:2]
            rest = rest[2:]
        main, (xn_ref, r_ref, dgate_ref, dup_ref, h_ref) = rest[:n_main], rest[n_main:]
        xv = x_ref[...]
        rv = lax.rsqrt(jnp.mean(xv * xv, axis=-1, keepdims=True) + EPS)
        xn = (xv * rv * g_ref[...]).astype(BF16)
        xn_ref[...] = xn
        r_ref[...] = rv
        for j in range(hid // FFN_CHUNK):
            cs = slice(j * FFN_CHUNK, (j + 1) * FFN_CHUNK)
            g = lax.dot_general(xn, wg_ref[cs, :], _NT, preferred_element_type=F32)
            u = lax.dot_general(xn, wu_ref[cs, :], _NT, preferred_element_type=F32)
            s = _sigmoid(g)
            silu = g * s
            dgate_ref[:, cs] = (u * (s + silu * (1.0 - s))).astype(BF16)
            dup_ref[:, cs] = silu.astype(BF16)
            h_ref[:, cs] = (silu * u).astype(BF16)
        acc = jnp.dot(h_ref[...], wd_ref[...], preferred_element_type=F32)
        if not last:
            main[0][...] = xv + acc
        else:
            dx, _, dgain, part = _final_loss_epi([acc], xv, tgt_ref[...], fg_ref[...])

            @pl.when(pl.program_id(0) == 0)
            def _():
                main[2][...] = jnp.zeros(main[2].shape, F32)
                main[3][...] = jnp.zeros(main[3].shape, F32)
            main[0][...] = dx
            main[1][...] = dx.astype(BF16)
            main[2][...] += dgain
            main[3][...] += part

    rows = pl.BlockSpec((tm, d), lambda i: (i, 0))
    wide = pl.BlockSpec((tm, hid), lambda i: (i, 0))
    col = pl.BlockSpec((tm, 1), lambda i: (i, 0))
    ins, in_specs = [x, gain, wg_t, wu_t, wd], [rows, _whole(gain), _whole(wg_t), _whole(wu_t), _whole(wd)]
    if last:
        ins += [target, final_gain]
        in_specs += [rows, _whole(final_gain)]
        out_specs = [rows, rows, pl.BlockSpec((1, d), lambda i: (0, 0)), pl.BlockSpec((1, 1), lambda i: (0, 0))]
        out_shape = [S((t, d), F32), S((t, d), BF16), S((1, d), F32), S((1, 1), F32)]
    else:
        out_specs, out_shape = [rows], [S((t, d), F32)]
    out_specs += [rows, col, wide, wide, wide]
    out_shape += [S((t, d), BF16), S((t, 1), F32)] + [S((t, hid), BF16)] * 3
    outs = pl.pallas_call(body, grid=(t // tm,), in_specs=in_specs, out_specs=out_specs, out_shape=out_shape,
                          compiler_params=_cp("arbitrary" if last else "parallel"), name=name)(*ins)
    return (tuple(outs[:4]) if last else outs[0]), outs[n_main:]


def ffn_fused_bwd(name, dx, dxb, x, r, gain, dgate, dup, wg_t, wu_t, wd):
    t, d = x.shape
    hid = wd.shape[0]
    tm = _tile(t, FFN_ROWS // 2)

    def body(dx_ref, dxb_ref, x_ref, r_ref, g_ref, dgate_ref, dup_ref, wg_ref, wu_ref, wd_ref,
             dxo_ref, dxbo_ref, dg_ref, du_ref, dgain_ref):
        @pl.when(pl.program_id(0) == 0)
        def _():
            dgain_ref[...] = jnp.zeros(dgain_ref.shape, F32)

        dxb = dxb_ref[...]
        for j in range(hid // FFN_CHUNK):
            cs = slice(j * FFN_CHUNK, (j + 1) * FFN_CHUNK)
            dh = lax.dot_general(dxb, wd_ref[cs, :], _NT, preferred_element_type=F32)
            dg_ref[:, cs] = (dh * dgate_ref[:, cs].astype(F32)).astype(BF16)
            du_ref[:, cs] = (dh * dup_ref[:, cs].astype(F32)).astype(BF16)
        dxn = (jnp.dot(dg_ref[...], wg_ref[...], preferred_element_type=F32)
               + jnp.dot(du_ref[...], wu_ref[...], preferred_element_type=F32))
        dxo, _, dgain = _rms_bwd_epi([dxn], x_ref[...], dx_ref[...], r_ref[...], g_ref[...])
        dxo_ref[...] = dxo
        dxbo_ref[...] = dxo.astype(BF16)
        dgain_ref[...] += dgain

    rows = pl.BlockSpec((tm, d), lambda i: (i, 0))
    wide = pl.BlockSpec((tm, hid), lambda i: (i, 0))
    col = pl.BlockSpec((tm, 1), lambda i: (i, 0))
    return pl.pallas_call(
        body, grid=(t // tm,),
        in_specs=[rows, rows, rows, col, _whole(gain), wide, wide, _whole(wg_t), _whole(wu_t), _whole(wd)],
        out_specs=[rows, rows, wide, wide, pl.BlockSpec((1, d), lambda i: (0, 0))],
        out_shape=[S((t, d), F32), S((t, d), BF16), S((t, hid), BF16), S((t, hid), BF16), S((1, d), F32)],
        compiler_params=_cp("arbitrary"), name=name)(dx, dxb, x, r, gain, dgate, dup, wg_t, wu_t, wd)


def ffn_fwd(i, x, w, target=None):
    out, (xn, r, dgate, dup, h) = ffn_fused_fwd(f"ffn{i}_f", x, w["ffn_norm"][i:i + 1], w["ffn_w_gate_t"][i],
                                                w["ffn_w_up_t"][i], w["ffn_w_down"][i], target,
                                                None if target is None else w["final_norm"])
    return out, (x, xn, r, dgate, dup, h)


def ffn_bwd(i, dx, dxb, saved, w):
    x, xn, r, dgate, dup, h = saved
    g_wd = mm_tn(f"ffn{i}_down_w", h, dxb)
    dx0, dx0b, dg, du, g_norm = ffn_fused_bwd(f"ffn{i}_b", dx, dxb, x, r, w["ffn_norm"][i:i + 1], dgate, dup,
                                              w["ffn_w_gate_t"][i], w["ffn_w_up_t"][i], w["ffn_w_down"][i])
    g_wg_t = mm_tn(f"ffn{i}_gate_w", dg, xn)
    g_wu_t = mm_tn(f"ffn{i}_up_w", du, xn)
    return dx0, dx0b, dict(ffn_norm=g_norm, ffn_w_gate_t=g_wg_t, ffn_w_up_t=g_wu_t, ffn_w_down=g_wd)


def local_step(x, mem, target, w, fetch=None, on_grads=None, anchor=None):
    consts = s5_setup(w, anchor)

    def need(stage, after):
        if fetch is not None:
            for k, v in fetch(stage, after).items():
                if isinstance(k, tuple):
                    w.setdefault(k[0], {})[k[1]] = v
                else:
                    w[k] = v

    need(0, consts["pw"])
    x1, s_e = even_fwd(x, w, lambda after: need(1, after))
    x2, s_c0 = ca_fwd(0, x1, mem, w)
    need(2, x2)
    x3, s_f0 = ffn_fwd(0, x2, w)
    x4, s_o = odd_fwd(x3, w, consts)
    need(3, x4)
    x5, s_c1 = ca_fwd(1, x4, mem, w)
    need(4, x5)
    (dx, dxb, g_final, loss), s_f1 = ffn_fwd(1, x5, w, target)

    def emit(stage, carry, plain, layered=None, layer=0):
        if on_grads is None:
            return carry
        out = dict(plain)
        out.update({(k, layer): v for k, v in (layered or {}).items()})
        return on_grads(stage, out, list(carry))

    dx, dxb, g_f1 = ffn_bwd(1, dx, dxb, s_f1, w)
    dx, dxb = emit(0, (dx, dxb), {}, g_f1, 1)
    dx, dxb, g_c1 = ca_bwd(1, dx, dxb, s_c1, mem, w)
    dx, dxb, g_o = odd_bwd(dx, dxb, s_o, w, consts)
    dx, dxb = emit(1, (dx, dxb), g_o, g_c1, 1)
    dx, dxb, g_f0 = ffn_bwd(0, dx, dxb, s_f0, w)
    dx, dxb = emit(2, (dx, dxb), {}, g_f0, 0)
    dx, dxb, g_c0 = ca_bwd(0, dx, dxb, s_c0, mem, w)
    dx, dxb = emit(3, (dx, dxb), {}, g_c0, 0)
    dproj, g_e = even_bwd_mixers(dxb, s_e, w)
    dproj = emit(4, dproj, {**g_e, "o_norm": g_o["o_norm"], "o_d": g_o["o_d"]})
    dx, dxb, g_e["e_norm"] = even_bwd_input(dx, dproj, s_e, w)

    grads = dict(g_e)
    grads.update(g_o)
    for g0, g1 in ((g_c0, g_c1), (g_f0, g_f1)):
        for k in g0:
            grads[k] = jnp.concatenate([g0[k], g1[k]], axis=0) if k.endswith("norm") else (g0[k], g1[k])
    grads["final_norm"] = g_final
    return loss, dx, grads


def _group(axes):
    pos = {a: lax.axis_index(a) for a in ("x", "y", "c")}
    me = 0
    for a in axes:
        me = me * 2 + pos[a]
    peers = []
    for mask in range(1, 2 ** len(axes)):
        peer = dict(pos)
        for bit, a in enumerate(axes):
            if (mask >> (len(axes) - 1 - bit)) & 1:
                peer[a] = 1 - pos[a]
        idx = 0
        for a in axes:
            idx = idx * 2 + peer[a]
        peers.append((idx, (peer["x"], peer["y"], peer["c"])))
    return me, peers


def _sibling():
    x, y, c = lax.axis_index("x"), lax.axis_index("y"), lax.axis_index("c")
    return c, (x, y, 1 - c)


_HBM =pl.BlockSpec(memory_space=pltpu.HBM)
_SEM = pl.BlockSpec(memory_space=pltpu.SEMAPHORE)
_EFFECT = pltpu.SideEffectType.DATAFLOW_SIDE_EFFECTING


def _gather_peers(direct):
    chip, _ = _group(("x", "y"))
    core = lax.axis_index("c")
    if direct:
        _, peers = _group(_ALL)
        return chip, core, [(idx // 2, idx % 2, dev) for idx, dev in peers]
    _, peers = _group(("x", "y"))
    return chip, core, [(idx, core, dev) for idx, dev in peers]


def gather_ici_start(name, groups, direct):
    flat = [b for g in groups for b in g]
    sizes = [len(g) for g in groups]
    k_ops, n_g = len(flat), len(groups)
    lands = [lax.empty((4, 2) + tuple(b.shape), b.dtype) for b in flat]
    fan = [N_DEV - 1 if d else 3 for d in direct]

    def body(*refs):
        src, land = refs[:k_ops], refs[k_ops:2 * k_ops]
        sems = refs[2 * k_ops:2 * k_ops + 3 * n_g]
        token = refs[-1]
        i = 0
        for g in range(n_g):
            send, recv, loc = sems[3 * g:3 * g + 3]
            chip, core, peers = _gather_peers(direct[g])
            for j in range(sizes[g]):
                pltpu.make_async_copy(src[i], land[i].at[chip, core], loc.at[j]).start()
                for k, (_, _, dev) in enumerate(peers):
                    s = fan[g] * j + k
                    pltpu.make_async_remote_copy(src_ref=src[i], dst_ref=land[i].at[chip, core], send_sem=send.at[s],
                                                 recv_sem=recv.at[s], device_id=dev, device_id_type=MESH).start()
                i += 1
        token[...] = jnp.zeros(token.shape, token.dtype)

    sem_shapes = []
    for s, f in zip(sizes, fan):
        sem_shapes += [pltpu.SemaphoreType.DMA((f * s,)), pltpu.SemaphoreType.DMA((f * s,)), pltpu.SemaphoreType.DMA((s,))]
    thru = [pltpu.HBM(a.shape, a.dtype) for a in flat + lands]
    outs = pl.pallas_call(
        body, name=name, out_shape=tuple(sem_shapes) + tuple(thru) + (S((8, LANES), F32),),
        in_specs=[_HBM] * (2 * k_ops), out_specs=[_SEM] * (3 * n_g) + [_HBM] * (2 * k_ops) + [pl.BlockSpec(memory_space=pltpu.VMEM)],
        input_output_aliases={i: 3 * n_g + i for i in range(2 * k_ops)},
        compiler_params=pltpu.CompilerParams(has_side_effects=_EFFECT),
    )(*[pltpu.with_memory_space_constraint(a, pltpu.HBM) for a in flat + lands])
    sems = [tuple(outs[3 * g:3 * g + 3]) for g in range(n_g)]
    srcs_thru, lands_thru, off = [], [], 3 * n_g
    for s in sizes:
        srcs_thru.append(list(outs[off:off + s]))
        off += s
    for s in sizes:
        lands_thru.append(list(outs[off:off + s]))
        off += s
    return sems, srcs_thru, lands_thru, outs[-1]


def gather_ici_wait(name, srcs, lands, sems, after, direct=False):
    n = len(srcs)

    def body(*refs):
        src, land = refs[:n], refs[n:2 * n]
        send, recv, loc = refs[2 * n:2 * n + 3]
        chip, core, peers = _gather_peers(direct)
        for j in range(n):
            for k, (pchip, pcore, dev) in enumerate(peers):
                s = len(peers) * j + k
                cp = pltpu.make_async_remote_copy(src_ref=src[j], dst_ref=land[j].at[pchip, pcore], send_sem=send.at[s],
                                                  recv_sem=recv.at[s], device_id=dev, device_id_type=MESH)
                cp.wait_send()
                cp.wait_recv()
            pltpu.make_async_copy(src[j], land[j].at[chip, core], loc.at[j]).wait()

    outs = pl.pallas_call(
        body, name=name, out_shape=tuple(pltpu.HBM(a.shape, a.dtype) for a in list(srcs) + list(lands)),
        in_specs=[_HBM] * (2 * n) + [_SEM] * 3 + [ANY], out_specs=[_HBM] * (2 * n),
        input_output_aliases={i: i for i in range(2 * n)},
        compiler_params=pltpu.CompilerParams(has_side_effects=_EFFECT),
    )(*srcs, *lands, *sems, after)
    return list(outs[n:])


def gather_d2d(name, bufs):
    k_ops = len(bufs)

    def body(*refs):
        in_refs, out_refs = refs[:k_ops], refs[k_ops:2 * k_ops]
        send_sems, recv_sems = refs[2 * k_ops:]
        core, sib = _sibling()
        sent, landed = [], []
        for i in range(k_ops):
            cp = pltpu.make_async_remote_copy(src_ref=in_refs[i].at[:, core], dst_ref=out_refs[i].at[:, core],
                                              send_sem=send_sems.at[i], recv_sem=recv_sems.at[i], device_id=sib, device_id_type=MESH)
            cp.start()
            sent.append(cp)
            landed.append(pltpu.make_async_remote_copy(src_ref=in_refs[i].at[:, core], dst_ref=out_refs[i].at[:, 1 - core],
                                                       send_sem=send_sems.at[i], recv_sem=recv_sems.at[i],
                                                       device_id=sib, device_id_type=MESH))
        for cp in landed:
            cp.wait_recv()
        for cp in sent:
            cp.wait_send()

    return pl.pallas_call(
        body, in_specs=[ANY] * k_ops, out_specs=[ANY] * k_ops, out_shape=[S(b.shape, b.dtype) for b in bufs],
        input_output_aliases={i: i for i in range(k_ops)},
        scratch_shapes=[pltpu.SemaphoreType.DMA((k_ops,)), pltpu.SemaphoreType.DMA((k_ops,))],
        name=name)(*bufs)


_ALL = ("x", "y", "c")


def _unit_rows(units):
    offs, off = [], 0
    for u in units:
        offs.append(off)
        off += u.shape[1]
    return offs, off


def scatter_start(name, units, carry):
    n_u, n_c = len(units), len(carry)
    offs, rows = _unit_rows(units)
    land = lax.empty((N_DEV, rows) + tuple(units[0].shape[2:]), units[0].dtype)
    fan = N_DEV - 1

    def body(*refs):
        u_refs, land_ref = refs[:n_u], refs[n_u]
        send, recv, loc = refs[n_u + 1 + n_c:n_u + 4 + n_c]
        me, peers = _group(_ALL)
        for j in range(n_u):
            rs = pl.ds(offs[j], units[j].shape[1])
            pltpu.make_async_copy(u_refs[j].at[me], land_ref.at[me, rs], loc.at[j]).start()
            for k, (idx, dev) in enumerate(peers):
                pltpu.make_async_remote_copy(src_ref=u_refs[j].at[idx], dst_ref=land_ref.at[me, rs], send_sem=send.at[fan * j + k],
                                             recv_sem=recv.at[fan * j + k], device_id=dev, device_id_type=MESH).start()

    thru = list(units) + [land] + list(carry)
    outs = pl.pallas_call(
        body, name=name,
        out_shape=(pltpu.SemaphoreType.DMA((fan * n_u,)), pltpu.SemaphoreType.DMA((fan * n_u,)), pltpu.SemaphoreType.DMA((n_u,)))
        + tuple(pltpu.HBM(a.shape, a.dtype) for a in thru),
        in_specs=[_HBM] * len(thru), out_specs=[_SEM] * 3 + [_HBM] * len(thru),
        input_output_aliases={i: 3 + i for i in range(len(thru))},
        compiler_params=pltpu.CompilerParams(has_side_effects=_EFFECT),
    )(*[pltpu.with_memory_space_constraint(a, pltpu.HBM) for a in thru])
    return tuple(outs[:3]), list(outs[3:3 + n_u]), outs[3 + n_u], list(outs[4 + n_u:])


def scatter_wait(name, units, land, sems, after):
    n_u = len(units)
    offs, _ = _unit_rows(units)
    fan = N_DEV - 1

    def body(*refs):
        u_refs, land_ref = refs[:n_u], refs[n_u]
        send, recv, loc = refs[n_u + 1:n_u + 4]
        me, peers = _group(_ALL)
        for j in range(n_u):
            rs = pl.ds(offs[j], units[j].shape[1])
            for k, (idx, dev) in enumerate(peers):
                cp = pltpu.make_async_remote_copy(src_ref=u_refs[j].at[idx], dst_ref=land_ref.at[idx, rs], send_sem=send.at[fan * j + k],
                                                  recv_sem=recv.at[fan * j + k], device_id=dev, device_id_type=MESH)
                cp.wait_send()
                cp.wait_recv()
            pltpu.make_async_copy(u_refs[j].at[me], land_ref.at[me, rs], loc.at[j]).wait()

    thru = list(units) + [land]
    outs = pl.pallas_call(
        body, name=name, out_shape=tuple(pltpu.HBM(a.shape, a.dtype) for a in thru),
        in_specs=[_HBM] * len(thru) + [_SEM] * 3 + [ANY], out_specs=[_HBM] * len(thru),
        input_output_aliases={i: i for i in range(len(thru))},
        compiler_params=pltpu.CompilerParams(has_side_effects=_EFFECT),
    )(*thru, *sems, after)
    return outs[n_u]


def _row_tile(rows, cap=512):
    return next(t for t in range(cap - cap % 16, 0, -16) if rows % t == 0)


def sum_shares(name, recv, me):
    n, rows, c = recv.shape
    tr = _row_tile(rows)

    def body(me_ref, *refs):
        acc = refs[0][...].astype(F32)
        for r in refs[1:n]:
            acc = acc + r[...].astype(F32)
        refs[n][...] = acc

    def slot(mask):
        return pl.BlockSpec((None, tr, c), lambda i, me, mask=mask: (jnp.bitwise_xor(me[0], mask), i, 0))

    spec = pltpu.PrefetchScalarGridSpec(
        num_scalar_prefetch=1, grid=(rows // tr,), in_specs=[slot(k) for k in range(n)],
        out_specs=pl.BlockSpec((tr, c), lambda i, me: (i, 0)))
    return pl.pallas_call(body, grid_spec=spec, out_shape=S((rows, c), F32),
                          compiler_params=_cp("parallel"), name=name)(me, *([recv] * n))


def sum_slots(name, slots):
    n, r, c = slots.shape

    def body(s_ref, o_ref):
        acc = s_ref[0]
        for j in range(1, n):
            acc = acc + s_ref[j]
        o_ref[...] = acc

    return pl.pallas_call(body, out_shape=S((r, c), F32), compiler_params=pltpu.CompilerParams(vmem_limit_bytes=VMEM_LIMIT),
                          name=name)(slots)


def adamw_units(name, pieces, transposed, w, m, v):
    n_l, k, n = w.shape
    tk = _tile(k, 512) if transposed else k
    c1 = 1.0 - ADAM_B1 ** ADAM_STEP
    c2 = 1.0 - ADAM_B2 ** ADAM_STEP

    def body(*refs):
        p_refs, (w_ref, m_ref, v_ref, g_ref, d_ref, m2_ref, v2_ref) = refs[:n_l], refs[n_l:]
        gv = p_refs[0][...]
        for j in range(1, n_l):
            gv = jnp.where(pl.program_id(0) == j, p_refs[j][...], gv)
        if transposed:
            gv = gv.T
        m2 = ADAM_B1 * m_ref[...] + (1.0 - ADAM_B1) * gv
        v2 = ADAM_B2 * v_ref[...] + (1.0 - ADAM_B2) * (gv * gv)
        g_ref[...] = gv
        m2_ref[...] = m2
        v2_ref[...] = v2
        d_ref[...] = -ADAM_LR * ((m2 / c1) / (jnp.sqrt(v2 / c2) + ADAM_EPS) + ADAM_WD * w_ref[...])

    piece = pl.BlockSpec((n, tk), lambda l, i: (0, i)) if transposed else pl.BlockSpec((k, n), lambda l, i: (0, 0))
    blk = pl.BlockSpec((None, tk, n), lambda l, i: (l, i, 0))
    return tuple(pl.pallas_call(body, grid=(n_l, k // tk), in_specs=[piece] * n_l + [blk] * 3, out_specs=[blk] * 4,
                                out_shape=[S(w.shape, F32)] * 4, compiler_params=_cp("parallel", "parallel"),
                                name=name)(*pieces, w, m, v))


def adamw_native(name, g, w, m, v, tr=512):
    shape = w.shape
    cols = shape[-1]
    rows = w.size // cols
    tr = _tile(rows, tr) if rows % 8 == 0 else rows
    c1 = 1.0 - ADAM_B1 ** ADAM_STEP
    c2 = 1.0 - ADAM_B2 ** ADAM_STEP

    def body(g_ref, w_ref, m_ref, v_ref, d_ref, m2_ref, v2_ref):
        gv = g_ref[...]
        m2 = ADAM_B1 * m_ref[...] + (1.0 - ADAM_B1) * gv
        v2 = ADAM_B2 * v_ref[...] + (1.0 - ADAM_B2) * (gv * gv)
        m2_ref[...] = m2
        v2_ref[...] = v2
        d_ref[...] = -ADAM_LR * ((m2 / c1) / (jnp.sqrt(v2 / c2) + ADAM_EPS) + ADAM_WD * w_ref[...])

    row = pl.BlockSpec((tr, cols), lambda i: (i, 0))
    outs = pl.pallas_call(body, grid=(rows // tr,), in_specs=[row] * 4, out_specs=[row] * 3,
                          out_shape=[S((rows, cols), F32)] * 3, compiler_params=_cp("parallel"),
                          name=name)(*[a.reshape(rows, cols) for a in (g, w, m, v)])
    return tuple(o.reshape(shape) for o in outs)


_REPLICATED = ("e_norm", "e_gmlp_w", "e_gmlp_b", "e_conv_b", "e_conv_ln_g", "e_conv_ln_b", "o_lam_re", "o_lam_im", "o_log_dt",
               "o_b_re", "o_b_im", "o_c_re", "o_c_im", "ca_norm", "ca_mem_norm", "ffn_norm", "final_norm")
_ORDER = ("e_norm", "e_w_in", "e_gmlp_w", "e_gmlp_b", "e_conv_w", "e_conv_b", "e_conv_ln_g", "e_conv_ln_b", "e_w_out",
          "o_norm", "o_w_in", "o_lam_re", "o_lam_im", "o_log_dt", "o_b_re", "o_b_im", "o_c_re", "o_c_im", "o_d", "o_w_out",
          "ca_norm", "ca_mem_norm", "ca_wq", "ca_wk", "ca_wv", "ca_wo", "ffn_norm", "ffn_w_gate", "ffn_w_up", "ffn_w_down",
          "final_norm")


def _rows128(a, multiple=8):
    flat = a.reshape(-1)
    rows = -(-flat.shape[0] // (LANES * multiple)) * multiple
    return jnp.pad(flat, (0, rows * LANES - flat.shape[0])).reshape(rows, LANES)


def _shard(full, axis):
    s = full.shape
    return jnp.moveaxis(full.reshape(s[:axis] + (N_DEV, s[axis] // N_DEV) + s[axis + 1:]), axis, 0)


_UNITS = (("e_w_in", 0, True), ("e_w_out", 0, False), ("o_w_in", 0, False), ("o_w_out", 0, True),
          *[(n, i, False) for n in ("ca_wq", "ca_wk", "ca_wv", "ca_wo") for i in (0, 1)],
          *[(n, i, tr) for n, tr in (("ffn_w_gate", True), ("ffn_w_up", True), ("ffn_w_down", False)) for i in (0, 1)])
_LAYERED = ("ca_wq", "ca_wk", "ca_wv", "ca_wo", "ffn_w_gate", "ffn_w_up", "ffn_w_down")
_SMALL_SHARDED = (("e_conv_w", 2), ("o_norm", 1), ("o_d", 1))
RS_ROW = 1024


def _unit_key(name, tr):
    return name + "_t" if tr else name


def _stage_of(name, layer):
    if name.startswith("e_"):
        return 0 if name == "e_w_in" else 1
    if name.startswith("o_"):
        return 2
    if name.startswith("ca_"):
        return 1 if layer == 0 else 3
    return 2 if layer == 0 else 4


GATHER_STAGES = 5
GATHER_DIRECT = (False, False, False, True, False)


def weight_fetcher(local):
    groups, meta = [[] for _ in range(GATHER_STAGES)], [[] for _ in range(GATHER_STAGES)]
    for name, layer, tr in _UNITS:
        blk = local[name][layer]
        st = _stage_of(name, layer)
        groups[st].append(_bf(blk.T if tr else blk))
        meta[st].append((name, layer, tr))
    small = jnp.concatenate([local[name].reshape(-1) for name, _ in _SMALL_SHARDED])
    groups[0].append(_rows128(small))
    direct = list(GATHER_DIRECT)
    sems, srcs, lands, token = gather_ici_start("ag_w_start", groups, direct)

    def fetch(stage, after):
        bufs = gather_ici_wait(f"ag_w_wait{stage}", srcs[stage], lands[stage], sems[stage], after, direct[stage])
        if not direct[stage]:
            bufs = gather_d2d(f"ag_w_d2d{stage}", bufs)
        got = {}
        for (name, layer, tr), blk, buf in zip(meta[stage], groups[stage], bufs):
            arr = buf.reshape((N_DEV * blk.shape[0],) + tuple(blk.shape[1:]))
            if name in _LAYERED:
                got[(_unit_key(name, tr), layer)] = arr
            else:
                got[_unit_key(name, tr)] = arr
        if stage == 0:
            flat = bufs[-1].reshape(N_DEV, -1)
            off = 0
            for name, axis in _SMALL_SHARDED:
                blk = local[name]
                seg = flat[:, off:off + blk.size].reshape((N_DEV,) + blk.shape)
                off += blk.size
                seg = jnp.moveaxis(seg, 0, axis)
                got[name] = seg.reshape(seg.shape[:axis] + (-1,) + seg.shape[axis + 2:])
            got["e_conv_w"] = got["e_conv_w"][0]
        return got

    return fetch, token


def _grad_stage_of(name, layer):
    if name.startswith("e_"):
        return 4
    if name.startswith("o_"):
        return 1
    if name.startswith("ca_"):
        return 3 if layer == 0 else 1
    return 2 if layer == 0 else 0


GRAD_STAGES = 5
SMALL_ROWS = 16


def gradient_reducer(local, mom, var):
    me = (4 * lax.axis_index("x") + 2 * lax.axis_index("y") + lax.axis_index("c")).astype(jnp.int32).reshape(1)
    pending = []

    def start(stage, grads, carry):
        units = [u for u in _UNITS if _grad_stage_of(u[0], u[1]) == stage]
        parts, spans = [], []
        for name, layer, tr in units:
            key = _unit_key(name, tr)
            g = grads[(key, layer)] if name in _LAYERED else grads[key]
            part = g.reshape(N_DEV, -1, RS_ROW)
            spans.append((part.shape[1], g.shape[0] // N_DEV, g.shape[1]))
            parts.append(part)
        if stage == GRAD_STAGES - 1:
            small = jnp.concatenate([_shard(grads[name], axis).reshape(N_DEV, -1) for name, axis in _SMALL_SHARDED], axis=1)
            small = jnp.pad(small, ((0, 0), (0, SMALL_ROWS * RS_ROW - small.shape[1])))
            parts.append(small.astype(BF16).reshape(N_DEV, SMALL_ROWS, RS_ROW))
        sems, sent, land, carry = scatter_start(f"rs_start{stage}", parts, carry)
        pending.append((stage, units, spans, sems, sent, land))
        return carry

    def finish(after):
        res, per_layer, small_flat = {}, {}, None
        for stage, units, spans, sems, sent, land in pending:
            land = scatter_wait(f"rs_wait{stage}", sent, land, sems, after)
            total = sum_shares(f"rs_sum{stage}", land, me)
            off = 0
            for (name, layer, tr), (rows, r, c) in zip(units, spans):
                per_layer.setdefault(name, {})[layer] = (total[off:off + rows].reshape(r, c), tr)
                off += rows
            if stage == GRAD_STAGES - 1:
                small_flat = total[off:off + SMALL_ROWS].reshape(-1)
        for name, by_layer in per_layer.items():
            pieces = [by_layer[i][0] for i in sorted(by_layer)]
            res[name] = adamw_units("adamw_" + name, pieces, by_layer[0][1], local[name], mom[name], var[name])
        off = 0
        for name, _ in _SMALL_SHARDED:
            blk = local[name]
            g = small_flat[off:off + blk.size].reshape(blk.shape)
            off += blk.size
            res[name] = (g,) + adamw_native("adamw_" + name, g, blk, mom[name], var[name])
        return res

    return start, finish


def replicated_start(grads, loss):
    pack = jnp.concatenate([_rows128(grads[name]) for name in _REPLICATED] + [_rows128(loss)], axis=0)
    sems, srcs, lands, token = gather_ici_start("ag_g_start", [[pack]], [False])
    return sems[0], srcs[0], lands[0], token


def replicated_finish(handle, after, w, mom, var):
    sems, srcs, lands, _ = handle
    (buf,) = gather_d2d("ag_g_d2d", gather_ici_wait("ag_g_wait", srcs, lands, sems, after))
    rows = srcs[0].shape[0]
    total = sum_slots("ag_g_sum", buf.reshape(N_DEV, rows, LANES))
    res, off = {}, 0
    for name in _REPLICATED:
        n = w[name].size
        nr = -(-n // (LANES * 8)) * 8
        g = total[off:off + nr].reshape(-1)[:n].reshape(w[name].shape)
        off += nr
        res[name] = (g,) + adamw_native("adamw_" + name, g, w[name], mom[name], var[name])
    return res, total[off, 0]


def kernel(x, mem, e_norm, e_w_in, e_gmlp_w, e_gmlp_b, e_conv_w, e_conv_b, e_conv_ln_g, e_conv_ln_b, e_w_out, o_norm, o_w_in, o_lam_re, o_lam_im, o_log_dt, o_b_re, o_b_im, o_c_re, o_c_im, o_d, o_w_out, ca_norm, ca_mem_norm, ca_wq, ca_wk, ca_wv, ca_wo, ffn_norm, ffn_w_gate, ffn_w_up, ffn_w_down, final_norm, loss_target, m_e_norm, m_e_w_in, m_e_gmlp_w, m_e_gmlp_b, m_e_conv_w, m_e_conv_b, m_e_conv_ln_g, m_e_conv_ln_b, m_e_w_out, m_o_norm, m_o_w_in, m_o_lam_re, m_o_lam_im, m_o_log_dt, m_o_b_re, m_o_b_im, m_o_c_re, m_o_c_im, m_o_d, m_o_w_out, m_ca_norm, m_ca_mem_norm, m_ca_wq, m_ca_wk, m_ca_wv, m_ca_wo, m_ffn_norm, m_ffn_w_gate, m_ffn_w_up, m_ffn_w_down, m_final_norm, v_e_norm, v_e_w_in, v_e_gmlp_w, v_e_gmlp_b, v_e_conv_w, v_e_conv_b, v_e_conv_ln_g, v_e_conv_ln_b, v_e_w_out, v_o_norm, v_o_w_in, v_o_lam_re, v_o_lam_im, v_o_log_dt, v_o_b_re, v_o_b_im, v_o_c_re, v_o_c_im, v_o_d, v_o_w_out, v_ca_norm, v_ca_mem_norm, v_ca_wq, v_ca_wk, v_ca_wv, v_ca_wo, v_ffn_norm, v_ffn_w_gate, v_ffn_w_up, v_ffn_w_down, v_final_norm):
    given = dict(locals())
    local = {k: given[k] for k in _ORDER}
    mom = {k: given["m_" + k] for k in _ORDER}
    var = {k: given["v_" + k] for k in _ORDER}

    w = {}
    w.update({
        "e_norm": e_norm, "e_gmlp_w": e_gmlp_w[0], "e_gmlp_b": e_gmlp_b.reshape(A_GROUPS, GMLP_BLOCK, 1),
        "e_conv_b": e_conv_b, "e_conv_ln_g": e_conv_ln_g, "e_conv_ln_b": e_conv_ln_b,
        "o_lam_re": o_lam_re[0], "o_lam_im": o_lam_im[0], "o_log_dt": o_log_dt[0], "o_b_re": o_b_re[0], "o_b_im": o_b_im[0],
        "o_c_re": o_c_re[0], "o_c_im": o_c_im[0], "ca_norm": ca_norm, "ca_mem_norm": ca_mem_norm, "ffn_norm": ffn_norm,
        "final_norm": final_norm.reshape(1, D_MODEL),
    })
    start_reduce, finish_reduce = gradient_reducer(local, mom, var)
    fetch, token = weight_fetcher(local)
    loss_part, grad_x, grads = local_step(x[0], mem[0], loss_target[0], w, fetch, start_reduce, token[0:1, 0:1])
    grads["final_norm"] = grads["final_norm"].reshape(D_MODEL)

    handle = replicated_start(grads, loss_part)
    res = finish_reduce(handle[3])
    rep, loss = replicated_finish(handle, res["ffn_w_down"][1], local, mom, var)
    res.update(rep)
    return (loss, grad_x[None], *[res[k][0] for k in _ORDER], *[res[k][1] for k in _ORDER],
            *[res[k][2] for k in _ORDER], *[res[k][3] for k in _ORDER])
```

```python
import jax
import jax.numpy as jnp
from jax import lax
from jax.experimental import pallas as pl
from jax.experimental.pallas import tpu as pltpu

F32 = jnp.float32
BF16 = jnp.bfloat16
S = jax.ShapeDtypeStruct

D_MODEL = 1024
A_WIDTH = 512
A_GROUPS = 4
GMLP_BLOCK = 128
CHUNK = 64
B_WIDTH = 512
IN_WIDTH = 2 * A_WIDTH + 2 * B_WIDTH
CONV_WIDTH = 31
CONV_PAD = 32
C_WIDTH = 512
C_GROUP_CH = 16
C_GROUPS = 32
C_STATE = 64
N_STATE = C_GROUPS * C_STATE
CA_HEADS = 4
CA_HEAD_DIM = 256
EPS = 1e-6
ADAM_LR = 0.001
ADAM_B1 = 0.9
ADAM_B2 = 0.999
ADAM_EPS = 1e-08
ADAM_WD = 0.01
ADAM_STEP = 10
N_DEV = 8
LANES = 128
VMEM_LIMIT = 56 << 20
VMEM_BUDGET = 40 << 20
MM_TN_RESIDENT = 8 << 20
MESH = pl.DeviceIdType.MESH
ANY = pl.BlockSpec(memory_space=pl.ANY)


def _cp(*sem):
    return pltpu.CompilerParams(dimension_semantics=sem, vmem_limit_bytes=VMEM_LIMIT)


def _tile(n, pref):
    t = pref
    while n % t:
        t //= 2
    return t


def _bf(v):
    return v if v.dtype == BF16 else v.astype(BF16)


def _sigmoid(x):
    return 1.0 / (1.0 + jnp.exp(-x))


_GC = 0.7978845608028654


def _gelu(x):
    return 0.5 * x * (1.0 + jnp.tanh(_GC * (x + 0.044715 * x * x * x)))


def _gelu_grad(x):
    x2 = x * x
    t = jnp.tanh(_GC * (x + 0.044715 * x * x2))
    return 0.5 * (1.0 + t) + 0.5 * x * (1.0 - t * t) * _GC * (1.0 + 3.0 * 0.044715 * x2)


def _tspec(entry, tm):
    if isinstance(entry, tuple):
        arr, cb, width = entry
        return arr, pl.BlockSpec((tm, width), lambda i, cb=cb: (i, cb))
    return entry, pl.BlockSpec((tm, entry.shape[1]), lambda i: (i, 0))


def rows_call(name, fn, tiled, full, outs, accs, tm=256):
    pairs = [_tspec(e, tm) for e in tiled]
    arrs = [p[0] for p in pairs]
    rows = arrs[0].shape[0]
    tm = _tile(rows, tm)
    pairs = [_tspec(e, tm) for e in tiled]
    n_in = len(tiled) + len(full)
    n_out = len(outs)

    def body(*refs):
        vals = [r[...] for r in refs[:n_in]]
        o_refs = refs[n_in:n_in + n_out]
        a_refs = refs[n_in + n_out:]
        ov, av = fn(*vals)
        for r, v in zip(o_refs, ov):
            r[...] = v.astype(r.dtype)
        if a_refs:
            @pl.when(pl.program_id(0) == 0)
            def _():
                for r in a_refs:
                    r[...] = jnp.zeros(r.shape, r.dtype)
            for r, v in zip(a_refs, av):
                r[...] += v

    in_specs = [p[1] for p in pairs] + [pl.BlockSpec(a.shape, lambda i, nd=a.ndim: (0,) * nd) for a in full]
    out_specs = [pl.BlockSpec((tm, c), lambda i: (i, 0)) for c, _ in outs]
    out_specs += [pl.BlockSpec(s, lambda i, nd=len(s): (0,) * nd) for s in accs]
    out_shape = [S((rows, c), dt) for c, dt in outs] + [S(s, F32) for s in accs]
    return pl.pallas_call(body, grid=(rows // tm,), in_specs=in_specs, out_specs=out_specs, out_shape=out_shape,
                          compiler_params=_cp("arbitrary"), name=name)(*arrs, *full)


def mm_nn(name, m, n, pairs, n_acc, epi, outs, tiled=(), cols=(), rowv=(), sums=(), norm_gain=None):
    a_ops, a_slot, b_arrs, b_specs, idx, trans = [], [], [], [], [], []
    fixed = 0
    for pair in pairs:
        a, b, k = pair[:3]
        bt = len(pair) > 3
        arr, cb, kdim = a if isinstance(a, tuple) else (a, 0, a.shape[1])
        key = (id(arr), cb, kdim)
        if key not in [o[0] for o in a_ops]:
            a_ops.append((key, arr, cb, kdim))
        a_slot.append([o[0] for o in a_ops].index(key))
        b_arr, off = b if isinstance(b, tuple) else (b, 0)
        b_arrs.append(b_arr)
        if bt:
            assert off % n == 0 and b_arr.shape[1] == kdim
            b_specs.append(pl.BlockSpec((n, kdim), lambda i, o=off // n: (o, 0), pipeline_mode=pl.Buffered(1)))
        else:
            assert b_arr.shape[1] == n
            b_specs.append(pl.BlockSpec((kdim, n), lambda i, o=off: (o, 0), pipeline_mode=pl.Buffered(1)))
        fixed += kdim * n * b_arr.dtype.itemsize
        idx.append(k)
        trans.append(bt)
    per_row = sum(2 * kdim * arr.dtype.itemsize for _, arr, _, kdim in a_ops)
    per_row += sum(2 * n * t.dtype.itemsize for t in tiled) + sum(2 * n * jnp.dtype(dt).itemsize for dt in outs)
    cn = n if sums or cols else (512 if n % 512 == 0 else 256)
    per_row += (n_acc + 3) * cn * 4
    tm = next((t for t in (1024, 512, 256, 128) if m % t == 0 and fixed + t * per_row <= VMEM_BUDGET), _tile(m, 128))
    n_a, n_p, n_t = len(a_ops), len(pairs), len(tiled)
    n_in = n_a + n_p + n_t + len(cols) + len(rowv)
    normed = norm_gain is not None
    o0 = n_in + normed

    def body(*refs):
        a_vals = [None if normed and i == 0 else _bf(r[...]) for i, r in enumerate(refs[:n_a])]
        if normed:
            xv = refs[0][...]
            rv = lax.rsqrt(jnp.mean(xv * xv, axis=-1, keepdims=True) + EPS)
            a_vals[0] = (xv * rv * refs[n_in][...]).astype(BF16)
            refs[o0 + len(outs)][...] = a_vals[0]
            refs[o0 + len(outs) + 1][...] = rv
        for j in range(n // cn):
            cs = slice(j * cn, (j + 1) * cn)
            accs = [None] * n_acc
            for p in range(n_p):
                av, b_ref = a_vals[a_slot[p]], refs[n_a + p]
                if trans[p]:
                    d = lax.dot_general(av, _bf(b_ref[cs, :]), (((1,), (1,)), ((), ())), preferred_element_type=F32)
                else:
                    d = jnp.dot(av, _bf(b_ref[:, cs]), preferred_element_type=F32)
                accs[idx[p]] = d if accs[idx[p]] is None else accs[idx[p]] + d
            extra = [r[:, cs] for r in refs[n_a + n_p:n_a + n_p + n_t]] + [r[...] for r in refs[n_a + n_p + n_t:n_in - len(rowv)]]
            extra += [r[:, cs] for r in refs[n_in - len(rowv):n_in]]
            ov = epi(accs, *extra)
            for r, v in zip(refs[o0:o0 + len(outs)], ov):
                r[:, cs] = v.astype(r.dtype)
        sv = ov[len(outs):]
        if sums:
            s_refs = refs[o0 + len(outs) + 2 * normed:]

            @pl.when(pl.program_id(0) == 0)
            def _():
                for r in s_refs:
                    r[...] = jnp.zeros(r.shape, r.dtype)
            for r, v in zip(s_refs, sv):
                r[...] += v

    in_specs = [pl.BlockSpec((tm, kdim), lambda i, cb=cb: (i, cb)) for _, _, cb, kdim in a_ops] + b_specs
    in_specs += [pl.BlockSpec((tm, n), lambda i: (i, 0)) for _ in tiled]
    in_specs += [pl.BlockSpec((tm, 1), lambda i: (i, 0)) for _ in cols]
    in_specs += [pl.BlockSpec((1, n), lambda i: (0, 0)) for _ in rowv]
    out_specs = [pl.BlockSpec((tm, n), lambda i: (i, 0)) for _ in outs]
    out_shape = [S((m, n), dt) for dt in outs]
    gain = []
    if normed:
        k0 = a_ops[0][3]
        gain = [norm_gain]
        in_specs.append(pl.BlockSpec((1, k0), lambda i: (0, 0)))
        out_specs += [pl.BlockSpec((tm, k0), lambda i: (i, 0)), pl.BlockSpec((tm, 1), lambda i: (i, 0))]
        out_shape += [S((m, k0), BF16), S((m, 1), F32)]
    out_specs += [pl.BlockSpec(s, lambda i, nd=len(s): (0,) * nd) for s in sums]
    out_shape += [S(s, F32) for s in sums]
    return pl.pallas_call(body, grid=(m // tm,), in_specs=in_specs, out_specs=out_specs, out_shape=out_shape,
                          compiler_params=_cp("arbitrary" if sums else "parallel"),
                          name=name)(*[o[1] for o in a_ops], *b_arrs, *tiled, *cols, *rowv, *gain)


def mm_tn(name, a, b, out_dtype=BF16):
    if isinstance(a, tuple):
        a_arr, a_cb, m = a
    else:
        a_arr, a_cb, m = a, None, a.shape[1]
    if isinstance(b, tuple):
        b_arr, b_cb, n = b
    else:
        b_arr, b_cb, n = b, None, b.shape[1]
    t = a_arr.shape[0]
    whole_b = t * n * b_arr.dtype.itemsize <= MM_TN_RESIDENT and b_cb is None
    tn = n if whole_b else _tile(n, 512)
    tm = _tile(m, 512 if t * 512 * a_arr.dtype.itemsize * 2 + t * tn * b_arr.dtype.itemsize * 2 <= VMEM_BUDGET else 256)
    a_off = 0 if a_cb is None else a_cb * (m // tm)
    b_off = 0 if b_cb is None else b_cb * (n // tn)

    def body(a_ref, b_ref, o_ref):
        o_ref[...] = lax.dot_general(_bf(a_ref[...]), _bf(b_ref[...]), (((0,), (0,)), ((), ())),
                                     preferred_element_type=F32).astype(o_ref.dtype)

    if whole_b:
        b_spec = pl.BlockSpec((t, n), lambda i, j: (0, 0), pipeline_mode=pl.Buffered(1))
    else:
        b_spec = pl.BlockSpec((t, tn), lambda i, j: (0, j + b_off))
    return pl.pallas_call(
        body, grid=(m // tm, n // tn),
        in_specs=[pl.BlockSpec((t, tm), lambda i, j: (0, i + a_off)), b_spec],
        out_specs=pl.BlockSpec((tm, tn), lambda i, j: (i, j)), out_shape=S((m, n), out_dtype),
        compiler_params=_cp("parallel", "parallel"), name=name)(a_arr, b_arr)


def rms_bwd_gain_only(name, dxn, x, r):
    def fn(dv, xv, rv):
        return [], [jnp.sum(dv * xv * rv, axis=0, keepdims=True)]
    return rows_call(name, fn, [dxn, x, r], [], [], [(1, x.shape[1])])[0]


def _final_loss_epi(accs, res, tv, g):
    xv = res + accs[0]
    d = xv.shape[-1]
    r = lax.rsqrt(jnp.mean(xv * xv, axis=-1, keepdims=True) + EPS)
    xh = xv * r
    err = xh * g - tv
    dy = err * (1.0 / d)
    w = dy * g
    dx = r * (w - xh * jnp.mean(w * xh, axis=-1, keepdims=True))
    part = jnp.sum(jnp.sum(err * err, axis=-1, keepdims=True), axis=0, keepdims=True) * (0.5 / d)
    return [dx, dx, jnp.sum(dy * xh, axis=0, keepdims=True), part]


def _gmlp_mask():
    row = lax.broadcasted_iota(jnp.int32, (GMLP_BLOCK, GMLP_BLOCK), 0) // CHUNK
    col = lax.broadcasted_iota(jnp.int32, (GMLP_BLOCK, GMLP_BLOCK), 1) // CHUNK
    return col <= row


def _ln_plain(v):
    mu = jnp.mean(v, axis=-1, keepdims=True)
    vc = v - mu
    rstd = lax.rsqrt(jnp.mean(vc * vc, axis=-1, keepdims=True) + EPS)
    return vc * rstd, rstd


def even_out_fwd(name, proj, hc, x, w, b, ln_g, ln_b, w_out, tm=512):
    t, d = x.shape
    tm = _tile(t, tm)

    def body(au_ref, av_ref, hc_ref, x_ref, w_ref, b_ref, lg_ref, lb_ref, wo_ref, x1_ref, oa_ref, ob_ref):
        mask = _gmlp_mask()
        u = _gelu(au_ref[...])
        vn, _ = _ln_plain(_gelu(av_ref[...]))
        vnb = _bf(vn)
        for g in range(A_GROUPS):
            wg = _bf(jnp.where(mask, w_ref[g], 0.0))
            cs = slice(g * GMLP_BLOCK, (g + 1) * GMLP_BLOCK)
            for n in range(tm // GMLP_BLOCK):
                rs = slice(n * GMLP_BLOCK, (n + 1) * GMLP_BLOCK)
                sg = jnp.dot(wg, vnb[rs, cs], preferred_element_type=F32) + b_ref[g]
                oa_ref[rs, cs] = (u[rs, cs] * sg).astype(oa_ref.dtype)
        y, _ = _ln_plain(hc_ref[...])
        z = y * lg_ref[...] + lb_ref[...]
        ob_ref[...] = (z * _sigmoid(z)).astype(ob_ref.dtype)
        x1_ref[...] = (x_ref[...] + jnp.dot(oa_ref[...], wo_ref[0:A_WIDTH, :], preferred_element_type=F32)
                       + jnp.dot(ob_ref[...], wo_ref[A_WIDTH:, :], preferred_element_type=F32))

    half = pl.BlockSpec((tm, A_WIDTH), lambda i: (i, 0))
    return pl.pallas_call(
        body, grid=(t // tm,),
        in_specs=[half, pl.BlockSpec((tm, A_WIDTH), lambda i: (i, 1)), half, pl.BlockSpec((tm, d), lambda i: (i, 0)),
                  _whole(w), _whole(b), _whole(ln_g), _whole(ln_b), _whole(w_out)],
        out_specs=[pl.BlockSpec((tm, d), lambda i: (i, 0)), half, half],
        out_shape=[S((t, d), F32), S((t, A_WIDTH), BF16), S((t, B_WIDTH), BF16)],
        compiler_params=_cp("parallel"), name=name)(proj, proj, hc, x, w, b, ln_g, ln_b, w_out)


def gmlp_bwd(name, proj, dxb, w_out, w, b, tm=512):
    t = proj.shape[0]
    tm = _tile(t, tm)

    def body(au_ref, av_ref, dx_ref, wo_ref, w_ref, b_ref, dp_ref, dw_ref, db_ref):
        @pl.when(pl.program_id(0) == 0)
        def _():
            dw_ref[...] = jnp.zeros(dw_ref.shape, F32)
            db_ref[...] = jnp.zeros(db_ref.shape, F32)

        mask = _gmlp_mask()
        au = au_ref[...]
        av = av_ref[...]
        u = _gelu(au)
        vn, rstd = _ln_plain(_gelu(av))
        vnb = _bf(vn)
        dout = lax.dot_general(dx_ref[...], wo_ref[0:A_WIDTH, :], _NT, preferred_element_type=F32)
        dvn_cols = []
        for g in range(A_GROUPS):
            wm = jnp.where(mask, w_ref[g], 0.0)
            wg = _bf(wm)
            wgt = _bf(wm.T)
            cs = slice(g * GMLP_BLOCK, (g + 1) * GMLP_BLOCK)
            dwg = jnp.zeros((GMLP_BLOCK, GMLP_BLOCK), F32)
            dbg = jnp.zeros((GMLP_BLOCK, 1), F32)
            dvn_rows = []
            for n in range(tm // GMLP_BLOCK):
                rs = slice(n * GMLP_BLOCK, (n + 1) * GMLP_BLOCK)
                sg = jnp.dot(wg, vnb[rs, cs], preferred_element_type=F32) + b_ref[g]
                dp_ref[rs, cs] = (dout[rs, cs] * sg * _gelu_grad(au[rs, cs])).astype(dp_ref.dtype)
                dsg = dout[rs, cs] * u[rs, cs]
                dsgb = _bf(dsg)
                dbg = dbg + jnp.sum(dsg, axis=1, keepdims=True)
                dwg = dwg + lax.dot_general(dsgb, vnb[rs, cs], (((1,), (1,)), ((), ())), preferred_element_type=F32)
                dvn_rows.append(jnp.dot(wgt, dsgb, preferred_element_type=F32))
            dw_ref[g] += jnp.where(mask, dwg, 0.0)
            db_ref[g] += dbg
            dvn_cols.append(jnp.concatenate(dvn_rows, axis=0))
        dvn = jnp.concatenate(dvn_cols, axis=1)
        dv = rstd * (dvn - jnp.mean(dvn, axis=-1, keepdims=True) - vn * jnp.mean(dvn * vn, axis=-1, keepdims=True))
        dp_ref[:, A_WIDTH:] = (dv * _gelu_grad(av)).astype(dp_ref.dtype)

    return pl.pallas_call(
        body, grid=(t // tm,),
        in_specs=[pl.BlockSpec((tm, A_WIDTH), lambda i: (i, 0)), pl.BlockSpec((tm, A_WIDTH), lambda i: (i, 1)),
                  pl.BlockSpec((tm, dxb.shape[1]), lambda i: (i, 0)), pl.BlockSpec(w_out.shape, lambda i: (0, 0)),
                  pl.BlockSpec(w.shape, lambda i: (0, 0, 0)), pl.BlockSpec(b.shape, lambda i: (0, 0, 0))],
        out_specs=[pl.BlockSpec((tm, 2 * A_WIDTH), lambda i: (i, 0)),
                   pl.BlockSpec(w.shape, lambda i: (0, 0, 0)), pl.BlockSpec(b.shape, lambda i: (0, 0, 0))],
        out_shape=[S((t, 2 * A_WIDTH), BF16), S(w.shape, F32), S(b.shape, F32)],
        compiler_params=_cp("arbitrary"), name=name)(proj, proj, dxb, w_out, w, b)


CONV_ROWS = 256
CONV_ROWS_BWD = 64


def conv_fwd(name, proj, w, cb):
    t = proj.shape[0]
    tc = LANES
    rows = _tile(t, CONV_ROWS)
    a_cb, g_cb = 2 * A_WIDTH // tc, (2 * A_WIDTH + B_WIDTH) // tc

    def body(a_ref, g_ref, w_ref, cb_ref, o_ref, hpad):
        hpad[0:CONV_PAD, :] = jnp.zeros((CONV_PAD, tc), F32)

        def fill(i, _):
            r0 = pl.multiple_of(i * rows, rows)
            hpad[pl.ds(CONV_PAD + r0, rows), :] = a_ref[pl.ds(r0, rows), :] * _sigmoid(g_ref[pl.ds(r0, rows), :])
            return 0
        lax.fori_loop(0, t // rows, fill, 0)

        def conv(i, _):
            r0 = pl.multiple_of(i * rows, rows)
            win = hpad[pl.ds(r0, rows + CONV_PAD), :]
            acc = jnp.zeros((rows, tc), F32) + cb_ref[...]
            for b in range(SUB):
                wb = win if b == 0 else pltpu.roll(win, b, 0)
                for a in range(CONV_PAD // SUB):
                    k = CONV_WIDTH - 1 - (SUB * a + b)
                    if k >= 0:
                        lo = CONV_PAD - SUB * a
                        acc = acc + wb[lo:lo + rows, :] * w_ref[k:k + 1, :]
            o_ref[pl.ds(r0, rows), :] = acc
            return 0
        lax.fori_loop(0, t // rows, conv, 0)

    return pl.pallas_call(
        body, grid=(B_WIDTH // tc,),
        in_specs=[pl.BlockSpec((t, tc), lambda j: (0, a_cb + j)), pl.BlockSpec((t, tc), lambda j: (0, g_cb + j)),
                  pl.BlockSpec((CONV_WIDTH, tc), lambda j: (0, j)), pl.BlockSpec((1, tc), lambda j: (0, j))],
        out_specs=pl.BlockSpec((t, tc), lambda j: (0, j)), out_shape=S((t, B_WIDTH), F32),
        scratch_shapes=[pltpu.VMEM((t + CONV_PAD, tc), F32)],
        compiler_params=_cp("parallel"), name=name)(proj, proj, w, cb)


def conv_bwd(name, proj, dhc, w):
    t = proj.shape[0]
    tc = LANES
    rows = _tile(t, CONV_ROWS_BWD)
    a_cb, g_cb = 2 * A_WIDTH // tc, (2 * A_WIDTH + B_WIDTH) // tc
    win_rows = rows + CONV_PAD

    def body(a_ref, g_ref, d_ref, w_ref, da_ref, dg_ref, dw_ref, dcb_ref, hpad, dpad, dwacc):
        hpad[0:CONV_PAD, :] = jnp.zeros((CONV_PAD, tc), F32)
        dpad[t:t + CONV_PAD, :] = jnp.zeros((CONV_PAD, tc), F32)
        dwacc[...] = jnp.zeros(dwacc.shape, F32)

        def fill(i, _):
            r0 = pl.multiple_of(i * rows, rows)
            hpad[pl.ds(CONV_PAD + r0, rows), :] = a_ref[pl.ds(r0, rows), :] * _sigmoid(g_ref[pl.ds(r0, rows), :])
            dpad[pl.ds(r0, rows), :] = d_ref[pl.ds(r0, rows), :]
            return 0
        lax.fori_loop(0, t // rows, fill, 0)

        def step(i, dcb):
            r0 = pl.multiple_of(i * rows, rows)
            hwin = hpad[pl.ds(r0, win_rows), :]
            dwin = dpad[pl.ds(r0, win_rows), :]
            dchunk = dwin[:rows, :]
            dh = jnp.zeros((rows, tc), F32)
            for b in range(SUB):
                hb = hwin if b == 0 else pltpu.roll(hwin, b, 0)
                db = dwin if b == 0 else pltpu.roll(dwin, win_rows - b, 0)
                for a in range(CONV_PAD // SUB):
                    k = CONV_WIDTH - 1 - (SUB * a + b)
                    if k >= 0:
                        dh = dh + db[SUB * a:SUB * a + rows, :] * w_ref[k:k + 1, :]
                        lo = CONV_PAD - SUB * a
                        prod = dchunk * hb[lo:lo + rows, :]
                        dwacc[k] += jnp.sum(prod.reshape(rows // 8, 8, tc), axis=0)
            a = a_ref[pl.ds(r0, rows), :]
            sg = _sigmoid(g_ref[pl.ds(r0, rows), :])
            da_ref[pl.ds(r0, rows), :] = (dh * sg).astype(da_ref.dtype)
            dg_ref[pl.ds(r0, rows), :] = (dh * a * sg * (1.0 - sg)).astype(dg_ref.dtype)
            return dcb + jnp.sum(dchunk, axis=0, keepdims=True)
        dcb = lax.fori_loop(0, t // rows, step, jnp.zeros((1, tc), F32))
        dcb_ref[...] = dcb
        for k in range(CONV_WIDTH):
            dw_ref[k:k + 1, :] = jnp.sum(dwacc[k], axis=0, keepdims=True)

    return pl.pallas_call(
        body, grid=(B_WIDTH // tc,),
        in_specs=[pl.BlockSpec((t, tc), lambda j: (0, a_cb + j)), pl.BlockSpec((t, tc), lambda j: (0, g_cb + j)),
                  pl.BlockSpec((t, tc), lambda j: (0, j)), pl.BlockSpec((CONV_WIDTH, tc), lambda j: (0, j))],
        out_specs=[pl.BlockSpec((t, tc), lambda j: (0, j)), pl.BlockSpec((t, tc), lambda j: (0, j)),
                   pl.BlockSpec((CONV_WIDTH, tc), lambda j: (0, j)), pl.BlockSpec((1, tc), lambda j: (0, j))],
        out_shape=[S((t, B_WIDTH), BF16), S((t, B_WIDTH), BF16), S((CONV_WIDTH, B_WIDTH), F32), S((1, B_WIDTH), F32)],
        scratch_shapes=[pltpu.VMEM((t + CONV_PAD, tc), F32), pltpu.VMEM((t + CONV_PAD, tc), F32),
                        pltpu.VMEM((CONV_WIDTH, 8, tc), F32)],
        compiler_params=_cp("parallel"), name=name)(proj, proj, dhc, w)


def ln_silu_bwd(name, hc, dxb, w_out, g, b):
    c = hc.shape[1]

    def fn(h, dxv, wv, gv, bv):
        dout = lax.dot_general(dxv, wv[A_WIDTH:, :], _NT, preferred_element_type=F32)
        y, rstd = _ln_plain(h)
        z = y * gv + bv
        s = _sigmoid(z)
        dz = dout * s * (1.0 + z * (1.0 - s))
        dyv = dz * gv
        dh = rstd * (dyv - jnp.mean(dyv, axis=-1, keepdims=True) - y * jnp.mean(dyv * y, axis=-1, keepdims=True))
        return [dh], [jnp.sum(dz * y, axis=0, keepdims=True), jnp.sum(dz, axis=0, keepdims=True)]

    return rows_call(name, fn, [hc, dxb], [w_out, g, b], [(c, F32)], [(1, c), (1, c)])


_NT = (((1,), (1,)), ((), ()))
_TN = (((0,), (0,)), ((), ()))


def attn_fwd(name, x, gain, wq, k, v, wo, tm=512):
    t, d = x.shape
    m = k.shape[0]
    tm = _tile(t, tm)
    scale = CA_HEAD_DIM ** -0.5

    def body(x_ref, g_ref, wq_ref, k_ref, v_ref, wo_ref, x1_ref, xn_ref, r_ref, q_ref, o_ref):
        xv = x_ref[...]
        rv = lax.rsqrt(jnp.mean(xv * xv, axis=-1, keepdims=True) + EPS)
        xn = (xv * rv * g_ref[...]).astype(BF16)
        xn_ref[...] = xn
        r_ref[...] = rv
        q_ref[...] = jnp.dot(xn, wq_ref[...], preferred_element_type=F32).astype(BF16)
        for h in range(CA_HEADS):
            cs = slice(h * CA_HEAD_DIM, (h + 1) * CA_HEAD_DIM)
            s = lax.dot_general(q_ref[:, cs], k_ref[:, cs], _NT, preferred_element_type=F32) * scale
            e = jnp.exp(s - jnp.max(s, axis=-1, keepdims=True))
            p = e / jnp.sum(e, axis=-1, keepdims=True)
            o_ref[:, cs] = jnp.dot(_bf(p), v_ref[:, cs], preferred_element_type=F32).astype(o_ref.dtype)
        x1_ref[...] = xv + jnp.dot(o_ref[...], wo_ref[...], preferred_element_type=F32)

    def whole(a):
        return pl.BlockSpec(a.shape, lambda i: (0, 0), pipeline_mode=pl.Buffered(1))

    rows = pl.BlockSpec((tm, d), lambda i: (i, 0))
    col = pl.BlockSpec((tm, 1), lambda i: (i, 0))
    return pl.pallas_call(
        body, grid=(t // tm,),
        in_specs=[rows, whole(gain), whole(wq), whole(k), whole(v), whole(wo)],
        out_specs=[rows, rows, col, rows, rows],
        out_shape=[S((t, d), F32), S((t, d), BF16), S((t, 1), F32), S((t, d), BF16), S((t, d), BF16)],
        compiler_params=_cp("parallel"), name=name)(x, gain, wq, k, v, wo)


def attn_bwd(name, dx, dxb, x, r, gain, q, k, v, wq, wo, tm=512):
    t, d = q.shape
    m = k.shape[0]
    tm = _tile(t, tm)
    scale = CA_HEAD_DIM ** -0.5

    def body(dx_ref, dxb_ref, x_ref, r_ref, g_ref, q_ref, k_ref, v_ref, wq_ref, wo_ref,
             dxo_ref, dxbo_ref, dq_ref, dk_ref, dv_ref, dg_ref, do_s):
        @pl.when(pl.program_id(0) == 0)
        def _():
            dk_ref[...] = jnp.zeros(dk_ref.shape, F32)
            dv_ref[...] = jnp.zeros(dv_ref.shape, F32)
            dg_ref[...] = jnp.zeros(dg_ref.shape, F32)

        do_s[...] = lax.dot_general(dxb_ref[...], wo_ref[...], _NT, preferred_element_type=F32).astype(BF16)
        for h in range(CA_HEADS):
            cs = slice(h * CA_HEAD_DIM, (h + 1) * CA_HEAD_DIM)
            qh, kh, vh, doh = q_ref[:, cs], k_ref[:, cs], v_ref[:, cs], do_s[:, cs]
            s = lax.dot_general(qh, kh, _NT, preferred_element_type=F32) * scale
            e = jnp.exp(s - jnp.max(s, axis=-1, keepdims=True))
            p = e / jnp.sum(e, axis=-1, keepdims=True)
            pb = _bf(p)
            dv_ref[:, cs] += lax.dot_general(pb, doh, _TN, preferred_element_type=F32)
            dp = lax.dot_general(doh, vh, _NT, preferred_element_type=F32)
            ds = _bf(p * (dp - jnp.sum(dp * p, axis=-1, keepdims=True)) * scale)
            dq_ref[:, cs] = jnp.dot(ds, kh, preferred_element_type=F32).astype(dq_ref.dtype)
            dk_ref[:, cs] += lax.dot_general(ds, qh, _TN, preferred_element_type=F32)
        dxn = lax.dot_general(dq_ref[...], wq_ref[...], _NT, preferred_element_type=F32)
        xh = x_ref[...] * r_ref[...]
        wv = dxn * g_ref[...]
        dxo = dx_ref[...] + r_ref[...] * (wv - xh * jnp.mean(wv * xh, axis=-1, keepdims=True))
        dxo_ref[...] = dxo
        dxbo_ref[...] = dxo.astype(BF16)
        dg_ref[...] += jnp.sum(dxn * xh, axis=0, keepdims=True)

    def whole(a):
        return pl.BlockSpec(a.shape, lambda i: (0, 0), pipeline_mode=pl.Buffered(1))

    rows = pl.BlockSpec((tm, d), lambda i: (i, 0))
    col = pl.BlockSpec((tm, 1), lambda i: (i, 0))
    acc = pl.BlockSpec((m, d), lambda i: (0, 0))
    return pl.pallas_call(
        body, grid=(t // tm,),
        in_specs=[rows, rows, rows, col, whole(gain), rows, whole(k), whole(v), whole(wq), whole(wo)],
        out_specs=[rows, rows, rows, acc, acc, pl.BlockSpec((1, d), lambda i: (0, 0))],
        out_shape=[S((t, d), F32), S((t, d), BF16), S((t, d), BF16), S((m, d), F32), S((m, d), F32), S((1, d), F32)],
        scratch_shapes=[pltpu.VMEM((tm, d), BF16)],
        compiler_params=_cp("arbitrary"), name=name)(dx, dxb, x, r, gain, q, k, v, wq, wo)


SUB = 8
S5_ROWS = 256


S5_BLOCKS = 4
BLOCK_CH = C_WIDTH // S5_BLOCKS
BLOCK_ST = N_STATE // S5_BLOCKS
_S5_BLOCKS = tuple((slice(BLOCK_CH * q, BLOCK_CH * (q + 1)), slice(BLOCK_ST * q, BLOCK_ST * (q + 1)),
                    slice(N_STATE + BLOCK_ST * q, N_STATE + BLOCK_ST * (q + 1))) for q in range(S5_BLOCKS))
_HI = lax.Precision.HIGHEST
_GP = (C_GROUPS, C_STATE)
_RP = (C_WIDTH, C_STATE)


def _zoh(lr, li, ldt):
    dt = jnp.exp(ldt)
    mag = jnp.exp(lr * dt)
    ar = mag * jnp.cos(li * dt)
    ai = mag * jnp.sin(li * dt)
    den = lr * lr + li * li
    qr = ((ar - 1.0) * lr + ai * li) / den
    qi = (ai * lr - (ar - 1.0) * li) / den
    return dt, ar, ai, den, qr, qi


def _per_channel(v):
    return jnp.broadcast_to(v[:, None, :], (C_GROUPS, C_GROUP_CH, C_STATE)).reshape(_RP)


def _same_group(shape, row_per_group, col_per_group):
    rows = lax.broadcasted_iota(jnp.int32, shape, 0) // row_per_group
    cols = lax.broadcasted_iota(jnp.int32, shape, 1) // col_per_group
    return rows == cols


def _spread(shape, axis):
    long = lax.broadcasted_iota(jnp.int32, shape, axis) % C_STATE
    short = lax.broadcasted_iota(jnp.int32, shape, 1 - axis)
    return long == short


def s5_discretise(name, lam_re, lam_im, log_dt, bt_re, bt_im):
    def body(lr_ref, li_ref, ldt_ref, btr_ref, bti_ref, a_ref, bbr_ref, bbi_ref):
        _, ar, ai, _, qr, qi = _zoh(lr_ref[...], li_ref[...], ldt_ref[...])
        a_ref[0] = ar
        a_ref[1] = ai
        q2r, q2i = _per_channel(qr), _per_channel(qi)
        btr, bti = btr_ref[...], bti_ref[...]
        bbr_ref[...] = q2r * btr - q2i * bti
        bbi_ref[...] = q2r * bti + q2i * btr

    return pl.pallas_call(body, out_shape=[S((2,) + _GP, F32), S(_RP, F32), S(_RP, F32)],
                          name=name)(lam_re, lam_im, log_dt, bt_re, bt_im)


def s5_operands(name, a, bbr, bbi, c2r, c2i, ctr, cti):
    ns = N_STATE

    def body(a_ref, bbr_ref, bbi_ref, c2r_ref, c2i_ref, ctr_ref, cti_ref, pw_ref, qw_ref, mb_ref, mc_ref, mct_ref):
        ar, ai = a_ref[0:1, :], a_ref[1:2, :]
        pows = [(ar, ai)]
        for _ in range(SUB - 1):
            pr, pi = pows[-1]
            pows.append((pr * ar - pi * ai, pr * ai + pi * ar))
        rows = lax.broadcasted_iota(jnp.int32, (SUB, ns), 0)

        def rows_of(v):
            return jnp.broadcast_to(v, (SUB, ns))

        for k, s in enumerate((1, 2, 4)):
            pr, pi = rows_of(pows[s - 1][0]), rows_of(pows[s - 1][1])
            pw_ref[k, 0] = jnp.where(rows >= s, pr, 0.0)
            pw_ref[k, 1] = jnp.where(rows >= s, pi, 0.0)
            qw_ref[k, 0] = jnp.where(rows + s <= SUB - 1, pr, 0.0)
            qw_ref[k, 1] = jnp.where(rows + s <= SUB - 1, -pi, 0.0)
        fr = fi = br = bi = jnp.zeros((SUB, ns), F32)
        for i in range(SUB):
            fr = jnp.where(rows == i, rows_of(pows[i][0]), fr)
            fi = jnp.where(rows == i, rows_of(pows[i][1]), fi)
            br = jnp.where(rows == i, rows_of(pows[SUB - 1 - i][0]), br)
            bi = jnp.where(rows == i, rows_of(-pows[SUB - 1 - i][1]), bi)
        pw_ref[3, 0], pw_ref[3, 1], qw_ref[3, 0], qw_ref[3, 1] = fr, fi, br, bi

        wide = _spread((C_STATE, ns), 1).astype(BF16)
        tall = _spread((ns, C_STATE), 0).astype(BF16)
        in_rows = _same_group((C_WIDTH, ns), C_GROUP_CH, C_STATE)
        in_cols = _same_group((ns, C_WIDTH), C_STATE, C_GROUP_CH)

        def across(v, sign=1.0):
            return jnp.where(in_rows, sign * jnp.dot(_bf(v), wide, preferred_element_type=F32), 0.0).astype(BF16)

        def down(vt, sign=1.0):
            return jnp.where(in_cols, sign * jnp.dot(tall, _bf(vt), preferred_element_type=F32), 0.0).astype(BF16)

        mb_ref[:, 0:ns] = across(bbr_ref[...])
        mb_ref[:, ns:2 * ns] = across(bbi_ref[...])
        mct_ref[:, 0:ns] = across(c2r_ref[...])
        mct_ref[:, ns:2 * ns] = across(c2i_ref[...], -1.0)
        mc_ref[0:ns, :] = down(ctr_ref[...])
        mc_ref[ns:2 * ns, :] = down(cti_ref[...], -1.0)

    return pl.pallas_call(
        body, out_shape=[S((4, 2, SUB, ns), F32), S((4, 2, SUB, ns), F32), S((C_WIDTH, 2 * ns), BF16),
                         S((2 * ns, C_WIDTH), BF16), S((C_WIDTH, 2 * ns), BF16)],
        compiler_params=pltpu.CompilerParams(vmem_limit_bytes=VMEM_LIMIT), name=name)(a, bbr, bbi, c2r, c2i, ctr, cti)


def s5_block_grads(name, u, lamb, xsb, dyb):
    t = u.shape[0]

    def mb_body(u_ref, lr_ref, li_ref, o_ref):
        ub = _bf(u_ref[...])
        o_ref[:, 0:BLOCK_ST] = lax.dot_general(ub, lr_ref[...], _TN, preferred_element_type=F32)
        o_ref[:, BLOCK_ST:2 * BLOCK_ST] = lax.dot_general(ub, li_ref[...], _TN, preferred_element_type=F32)

    d_mb = pl.pallas_call(
        mb_body, grid=(S5_BLOCKS,),
        in_specs=[pl.BlockSpec((t, BLOCK_CH), lambda q: (0, q)), pl.BlockSpec((t, BLOCK_ST), lambda q: (0, q)),
                  pl.BlockSpec((t, BLOCK_ST), lambda q: (0, S5_BLOCKS + q))],
        out_specs=pl.BlockSpec((BLOCK_CH, 2 * BLOCK_ST), lambda q: (q, 0)), out_shape=S((C_WIDTH, 2 * BLOCK_ST), F32),
        compiler_params=_cp("parallel"), name=name + "_b")(u, lamb, lamb)

    def mc_body(x_ref, dy_ref, o_ref):
        o_ref[...] = lax.dot_general(x_ref[...], dy_ref[...], _TN, preferred_element_type=F32)

    d_mc = pl.pallas_call(
        mc_body, grid=(2, S5_BLOCKS),
        in_specs=[pl.BlockSpec((t, BLOCK_ST), lambda p, q: (0, p * S5_BLOCKS + q)), pl.BlockSpec((t, BLOCK_CH), lambda p, q: (0, q))],
        out_specs=pl.BlockSpec((BLOCK_ST, BLOCK_CH), lambda p, q: (p * S5_BLOCKS + q, 0)),
        out_shape=S((2 * N_STATE, BLOCK_CH), F32), compiler_params=_cp("parallel", "parallel"), name=name + "_c")(xsb, dyb)
    return d_mb, d_mc


def s5_param_grads(name, d_mb, d_mc, da, lam_re, lam_im, log_dt, bt_re, bt_im):
    ns = N_STATE

    def body(dmb_ref, dmc_ref, da_ref, lr_ref, li_ref, ldt_ref, btr_ref, bti_ref,
             glr_ref, gli_ref, gdt_ref, gbr_ref, gbi_ref, gcr_ref, gci_ref):
        lr, li = lr_ref[...], li_ref[...]
        dt, ar, ai, den, qr, qi = _zoh(lr, li, ldt_ref[...])
        per_block = C_GROUPS // S5_BLOCKS
        wide = _spread((C_STATE, BLOCK_ST), 1).astype(F32)
        tall = _spread((BLOCK_ST, C_STATE), 0).astype(F32)
        rows = lax.broadcasted_iota(jnp.int32, (C_WIDTH, BLOCK_ST), 0) // C_GROUP_CH % per_block
        in_rows = rows == lax.broadcasted_iota(jnp.int32, (C_WIDTH, BLOCK_ST), 1) // C_STATE
        in_cols = _same_group((BLOCK_ST, BLOCK_CH), C_STATE, C_GROUP_CH)

        def fold_rows(v):
            return lax.dot_general(jnp.where(in_rows, v, 0.0), wide, (((1,), (1,)), ((), ())), precision=_HI,
                                   preferred_element_type=F32)

        def fold_cols(v):
            return lax.dot_general(jnp.where(in_cols, v, 0.0), tall, (((0,), (0,)), ((), ())), precision=_HI,
                                   preferred_element_type=F32)

        for cs, s_re, s_im in _S5_BLOCKS:
            gcr_ref[cs, :] = fold_cols(dmc_ref[s_re, :])
            gci_ref[cs, :] = -fold_cols(dmc_ref[s_im, :])
        gbbr = fold_rows(dmb_ref[:, 0:BLOCK_ST])
        gbbi = fold_rows(dmb_ref[:, BLOCK_ST:2 * BLOCK_ST])
        btr, bti = btr_ref[...], bti_ref[...]
        q2r, q2i = _per_channel(qr), _per_channel(qi)
        gbr_ref[...] = q2r * gbbr + q2i * gbbi
        gbi_ref[...] = q2r * gbbi - q2i * gbbr

        def per_group(v):
            return jnp.sum(v.reshape(C_GROUPS, C_GROUP_CH, C_STATE), axis=1)

        gqr = per_group(btr * gbbr + bti * gbbi)
        gqi = per_group(btr * gbbi - bti * gbbr)
        ilr, ili = lr / den, li / den
        gar = da_ref[0] + ilr * gqr - ili * gqi
        gai = da_ref[1] + ilr * gqi + ili * gqr
        sr = (qr * lr + qi * li) / den
        si = (qi * lr - qr * li) / den
        gzr = ar * gar + ai * gai
        gzi = ar * gai - ai * gar
        glr_ref[...] = -sr * gqr - si * gqi + dt * gzr
        gli_ref[...] = -sr * gqi + si * gqr + dt * gzi
        gdt_ref[...] = jnp.sum(lr * gzr + li * gzi, axis=1, keepdims=True) * dt

    return pl.pallas_call(
        body, out_shape=[S(_GP, F32), S(_GP, F32), S((C_GROUPS, 1), F32), S(_RP, F32), S(_RP, F32), S(_RP, F32), S(_RP, F32)],
        compiler_params=pltpu.CompilerParams(vmem_limit_bytes=VMEM_LIMIT), name=name,
    )(d_mb, d_mc, da, lam_re, lam_im, log_dt, bt_re, bt_im)


def _cmul_add(xr, xi, pr, pi, zr, zi):
    return xr + pr * zr - pi * zi, xi + pr * zi + pi * zr


def s5_fwd(name, x, gain, w_in, mb, mc, pw, dskip, w_out_t):
    t, d = x.shape
    tm = _tile(t, S5_ROWS)
    ns = N_STATE

    def body(x_ref, g_ref, wi_ref, mb_ref, mc_ref, pw_ref, d_ref, wo_ref,
             x1_ref, hn_ref, r_ref, u_ref, gy_ref, y_ref, xs_ref, xb_ref, o1_ref, o2_ref, carry):
        @pl.when(pl.program_id(0) == 0)
        def _():
            carry[...] = jnp.zeros(carry.shape, F32)

        xv = x_ref[...]
        rv = lax.rsqrt(jnp.mean(xv * xv, axis=-1, keepdims=True) + EPS)
        hn = (xv * rv * g_ref[...]).astype(BF16)
        hn_ref[...] = hn
        r_ref[...] = rv
        uv = jnp.dot(hn, wi_ref[...], preferred_element_type=F32)
        u_ref[...] = uv
        ub = _bf(uv)
        for cs, s_re, s_im in _S5_BLOCKS:
            xs_ref[:, s_re] = jnp.dot(ub[:, cs], mb_ref[cs, s_re], preferred_element_type=F32)
            xs_ref[:, s_im] = jnp.dot(ub[:, cs], mb_ref[cs, s_im], preferred_element_type=F32)

        def group(i, _):
            r0 = pl.multiple_of(i * SUB, SUB)
            xr = xs_ref[pl.ds(r0, SUB), 0:ns]
            xi = xs_ref[pl.ds(r0, SUB), ns:2 * ns]
            for k, s in enumerate((1, 2, 4)):
                xr, xi = _cmul_add(xr, xi, pw_ref[k, 0], pw_ref[k, 1], pltpu.roll(xr, s, 0), pltpu.roll(xi, s, 0))
            xr, xi = _cmul_add(xr, xi, pw_ref[3, 0], pw_ref[3, 1], carry[0], carry[1])
            xs_ref[pl.ds(r0, SUB), 0:ns] = xr
            xs_ref[pl.ds(r0, SUB), ns:2 * ns] = xi
            carry[0] = jnp.broadcast_to(xr[SUB - 1:SUB, :], (SUB, ns))
            carry[1] = jnp.broadcast_to(xi[SUB - 1:SUB, :], (SUB, ns))
            return 0
        lax.fori_loop(0, tm // SUB, group, 0)

        xb_ref[...] = _bf(xs_ref[...])
        for cs, s_re, s_im in _S5_BLOCKS:
            y = (jnp.dot(xb_ref[:, s_re], mc_ref[s_re, cs], preferred_element_type=F32)
                 + jnp.dot(xb_ref[:, s_im], mc_ref[s_im, cs], preferred_element_type=F32) + d_ref[:, cs] * uv[:, cs])
            y_ref[:, cs] = y
            gy_ref[:, cs] = _gelu(y).astype(gy_ref.dtype)
        o1 = lax.dot_general(gy_ref[...], wo_ref[0:d, :], _NT, preferred_element_type=F32)
        o2 = lax.dot_general(gy_ref[...], wo_ref[d:2 * d, :], _NT, preferred_element_type=F32)
        o1_ref[...] = o1.astype(BF16)
        o2_ref[...] = o2.astype(BF16)
        x1_ref[...] = xv + o1 * _sigmoid(o2)

    c = w_in.shape[1]
    rows = pl.BlockSpec((tm, d), lambda i: (i, 0))
    narrow = pl.BlockSpec((tm, c), lambda i: (i, 0))
    states = pl.BlockSpec((tm, 2 * ns), lambda i: (i, 0))
    return pl.pallas_call(
        body, grid=(t // tm,),
        in_specs=[rows, _whole(gain), _whole(w_in), _whole(mb), _whole(mc), _whole(pw), _whole(dskip), _whole(w_out_t)],
        out_specs=[rows, rows, pl.BlockSpec((tm, 1), lambda i: (i, 0)), narrow, narrow, narrow, states, states, rows, rows],
        out_shape=[S((t, d), F32), S((t, d), BF16), S((t, 1), F32), S((t, c), F32), S((t, c), BF16), S((t, c), F32),
                   S((t, 2 * ns), F32), S((t, 2 * ns), BF16), S((t, d), BF16), S((t, d), BF16)],
        scratch_shapes=[pltpu.VMEM((2, SUB, ns), F32)],
        compiler_params=_cp("arbitrary"), name=name)(x, gain, w_in, mb, mc, pw, dskip, w_out_t)


def s5_bwd(name, dgy, y, u, xs, mct, mbt, qw, dskip):
    t, c = u.shape
    tm = _tile(t, S5_ROWS)
    nt = t // tm
    ns = N_STATE
    ng = tm // SUB

    def body(dgy_ref, y_ref, u_ref, xs_ref, mct_ref, mbt_ref, qw_ref, d_ref,
             du_ref, dy_ref, lb_ref, da_ref, dd_ref, lam, carry):
        @pl.when(pl.program_id(0) == 0)
        def _():
            carry[...] = jnp.zeros(carry.shape, F32)
            da_ref[...] = jnp.zeros(da_ref.shape, F32)
            dd_ref[...] = jnp.zeros(dd_ref.shape, F32)

        uv = u_ref[...]
        dy = dgy_ref[...] * _gelu_grad(y_ref[...])
        dyb = _bf(dy)
        dy_ref[...] = dyb
        dd_ref[...] += jnp.sum(dy * uv, axis=0, keepdims=True)
        for cs, s_re, s_im in _S5_BLOCKS:
            lam[:, s_re] = jnp.dot(dyb[:, cs], mct_ref[cs, s_re], preferred_element_type=F32)
            lam[:, s_im] = jnp.dot(dyb[:, cs], mct_ref[cs, s_im], preferred_element_type=F32)
        last_row = lax.broadcasted_iota(jnp.int32, (SUB, ns), 0) == SUB - 1

        def group(j, _):
            i = ng - 1 - j
            r0 = pl.multiple_of(i * SUB, SUB)
            lr = lam[pl.ds(r0, SUB), 0:ns]
            li = lam[pl.ds(r0, SUB), ns:2 * ns]
            for k, s in enumerate((1, 2, 4)):
                lr, li = _cmul_add(lr, li, qw_ref[k, 0], qw_ref[k, 1],
                                   pltpu.roll(lr, SUB - s, 0), pltpu.roll(li, SUB - s, 0))
            cr, ci = carry[0], carry[1]
            lr, li = _cmul_add(lr, li, qw_ref[3, 0], qw_ref[3, 1], cr, ci)
            lam[pl.ds(r0, SUB), 0:ns] = lr
            lam[pl.ds(r0, SUB), ns:2 * ns] = li
            carry[0] = jnp.broadcast_to(lr[0:1, :], (SUB, ns))
            carry[1] = jnp.broadcast_to(li[0:1, :], (SUB, ns))
            nr = jnp.where(last_row, cr, pltpu.roll(lr, SUB - 1, 0))
            ni = jnp.where(last_row, ci, pltpu.roll(li, SUB - 1, 0))
            xr = xs_ref[pl.ds(r0, SUB), 0:ns]
            xi = xs_ref[pl.ds(r0, SUB), ns:2 * ns]
            da_ref[0] += nr * xr + ni * xi
            da_ref[1] += ni * xr - nr * xi
            return 0
        lax.fori_loop(0, ng, group, 0)

        lb_ref[...] = _bf(lam[...])
        for cs, s_re, s_im in _S5_BLOCKS:
            du = (jnp.dot(lb_ref[:, s_re], mbt_ref[s_re, cs], preferred_element_type=F32)
                  + jnp.dot(lb_ref[:, s_im], mbt_ref[s_im, cs], preferred_element_type=F32) + d_ref[:, cs] * dy[:, cs])
            du_ref[:, cs] = du.astype(du_ref.dtype)

    rev = lambda i: (nt - 1 - i, 0)
    return pl.pallas_call(
        body, grid=(nt,),
        in_specs=[pl.BlockSpec((tm, c), rev), pl.BlockSpec((tm, c), rev), pl.BlockSpec((tm, c), rev),
                  pl.BlockSpec((tm, 2 * ns), rev),
                  pl.BlockSpec(mct.shape, lambda i: (0, 0)), pl.BlockSpec(mbt.shape, lambda i: (0, 0)),
                  pl.BlockSpec(qw.shape, lambda i: (0, 0, 0, 0)), pl.BlockSpec((1, c), lambda i: (0, 0))],
        out_specs=[pl.BlockSpec((tm, c), rev), pl.BlockSpec((tm, c), rev), pl.BlockSpec((tm, 2 * ns), rev),
                   pl.BlockSpec((2, SUB, ns), lambda i: (0, 0, 0)), pl.BlockSpec((1, c), lambda i: (0, 0))],
        out_shape=[S((t, c), BF16), S((t, c), BF16), S((t, 2 * ns), BF16), S((2, SUB, ns), F32), S((1, c), F32)],
        scratch_shapes=[pltpu.VMEM((tm, 2 * ns), F32), pltpu.VMEM((2, SUB, ns), F32)],
        compiler_params=_cp("arbitrary"), name=name)(dgy, y, u, xs, mct, mbt, qw, dskip)


def _first(accs, *_):
    return [accs[0]]


def _rms_bwd_epi(accs, xv, base, rv, g):
    dv = accs[0]
    w = dv * g
    xh = xv * rv
    dx = base + rv * (w - xh * jnp.mean(w * xh, axis=-1, keepdims=True))
    return [dx, dx, jnp.sum(dv * xh, axis=0, keepdims=True)]


def mm_rms_bwd(name, pairs, x, r, gain, dres):
    t, d = x.shape
    return mm_nn(name, t, d, pairs, 1, _rms_bwd_epi, [F32, BF16], tiled=[x, dres], cols=[r], rowv=[gain], sums=[(1, d)])


def even_fwd(x, w, need_out):
    t = x.shape[0]
    proj, hn, r = mm_nn("e_in_f", t, IN_WIDTH, [(x, w["e_w_in_t"], 0, "t")], 1, _first, [F32], norm_gain=w["e_norm"])
    hc = conv_fwd("e_conv_f", proj, w["e_conv_w"], w["e_conv_b"])
    need_out(hc)
    x1, out_a, out_b = even_out_fwd("e_out_f", proj, hc, x, w["e_gmlp_w"], w["e_gmlp_b"], w["e_conv_ln_g"], w["e_conv_ln_b"],
                                    w["e_w_out"])
    return x1, (x, hn, r, proj, out_a, hc, out_b)


def even_bwd_mixers(dxb, saved, w):
    x, hn, r, proj, out_a, hc, out_b = saved
    t = x.shape[0]
    g_w_out = jnp.concatenate([mm_tn("e_out_wa", out_a, dxb), mm_tn("e_out_wb", out_b, dxb)], axis=0)
    dab, g_gw, g_gb = gmlp_bwd("e_gmlp_b", proj, dxb, w["e_w_out"], w["e_gmlp_w"], w["e_gmlp_b"])
    dhc, g_lg, g_lb = ln_silu_bwd("e_ln_b", hc, dxb, w["e_w_out"], w["e_conv_ln_g"], w["e_conv_ln_b"])
    dba, dbg, g_cw, g_cb = conv_bwd("e_conv_b", proj, dhc, w["e_conv_w"])
    g_w_in_t = jnp.concatenate([mm_tn("e_in_w0", dab, hn), mm_tn("e_in_w1", dba, hn), mm_tn("e_in_w2", dbg, hn)], axis=0)
    grads = dict(e_w_in_t=g_w_in_t, e_gmlp_w=g_gw[None], e_gmlp_b=g_gb.reshape(1, A_GROUPS, GMLP_BLOCK),
                 e_conv_w=g_cw[None], e_conv_b=g_cb, e_conv_ln_g=g_lg, e_conv_ln_b=g_lb, e_w_out=g_w_out)
    return (dab, dba, dbg), grads


def even_bwd_input(dx, dproj, saved, w):
    x, _, r = saved[:3]
    dab, dba, dbg = dproj
    w_in_t = w["e_w_in_t"]
    return mm_rms_bwd("e_in_b", [(dab, (w_in_t, 0), 0), (dba, (w_in_t, 2), 0), (dbg, (w_in_t, 3), 0)], x, r, w["e_norm"], dx)


def s5_setup(w, anchor=None):
    def rows(v):
        return v.transpose(0, 2, 1).reshape(_RP)

    log_dt = w["o_log_dt"].reshape(C_GROUPS, 1)
    if anchor is not None:
        log_dt = log_dt + anchor
    lam = (w["o_lam_re"], w["o_lam_im"], log_dt, rows(w["o_b_re"]), rows(w["o_b_im"]))
    a, bbr, bbi = s5_discretise("o_s5_zoh", *lam)
    c_re, c_im = w["o_c_re"], w["o_c_im"]
    pw, qw, mb, mc, mct = s5_operands("o_s5_ops", a.reshape(2, N_STATE), bbr, bbi, c_re.reshape(_RP), c_im.reshape(_RP),
                                      c_re.transpose(2, 0, 1).reshape(C_STATE, C_WIDTH),
                                      c_im.transpose(2, 0, 1).reshape(C_STATE, C_WIDTH))
    return dict(lam=lam, pw=pw, qw=qw, mb=mb, mc=mc, mct=mct, mbt=mb.T)


def odd_fwd(x, w, consts):
    x1, hn, r, u, gy, y, xs, xsb, o1, o2 = s5_fwd("o_s5_f", x, w["o_norm"], w["o_w_in"], consts["mb"], consts["mc"],
                                                  consts["pw"], w["o_d"], w["o_w_out_t"])
    return x1, (x, hn, r, u, gy, y, xs, xsb, o1, o2)


def odd_bwd(dx, dxb, saved, w, consts):
    x, hn, r, u, gy, y, xs, xsb, o1, o2 = saved
    t = x.shape[0]

    def gate_bwd(dv, a, b, wv):
        a = a.astype(F32)
        sg = _sigmoid(b.astype(F32))
        do12 = jnp.concatenate([dv * sg, dv * a * sg * (1.0 - sg)], axis=1).astype(BF16)
        return [do12, jnp.dot(do12, wv, preferred_element_type=F32)], []

    do12, dgy = rows_call("o_out_b", gate_bwd, [dx, o1, o2], [w["o_w_out_t"]], [(2 * D_MODEL, BF16), (C_WIDTH, F32)], [])
    g_w_out_t = mm_tn("o_out_w", do12, gy)
    du, dyb, lamb, da8, g_d = s5_bwd("o_s5_b", dgy, y, u, xs, consts["mct"], consts["mbt"], consts["qw"], w["o_d"])
    d_mb, d_mc = s5_block_grads("o_s5_w", u, lamb, xsb, dyb)
    da = jnp.sum(da8, axis=1).reshape((2,) + _GP)
    g_lr, g_li, g_dt, g_btr, g_bti, g_cr, g_ci = s5_param_grads("o_s5_pg", d_mb, d_mc, da, *consts["lam"])

    def states_first(v):
        return v.reshape(C_GROUPS, C_GROUP_CH, C_STATE).transpose(0, 2, 1)[None]

    g_w_in = mm_tn("o_in_w", hn, du)
    dx0, dx0b, g_norm = mm_rms_bwd("o_in_b", [(du, w["o_w_in"], 0, "t")], x, r, w["o_norm"], dx)
    grads = dict(o_norm=g_norm, o_w_in=g_w_in, o_lam_re=g_lr[None], o_lam_im=g_li[None], o_log_dt=g_dt.reshape(1, C_GROUPS),
                 o_b_re=states_first(g_btr), o_b_im=states_first(g_bti),
                 o_c_re=g_cr.reshape((1, C_GROUPS, C_GROUP_CH, C_STATE)), o_c_im=g_ci.reshape((1, C_GROUPS, C_GROUP_CH, C_STATE)),
                 o_d=g_d, o_w_out_t=g_w_out_t)
    return dx0, dx0b, grads


def ca_fwd(i, x, mem, w):
    t, m = x.shape[0], mem.shape[0]
    k, v, mn, rm = mm_nn(f"ca{i}_kv_f", m, D_MODEL, [(mem, w["ca_wk"][i], 0), (mem, w["ca_wv"][i], 1)], 2,
                         lambda accs: [accs[0], accs[1]], [BF16, BF16], norm_gain=w["ca_mem_norm"][i:i + 1])
    x1, xn, r, q, o = attn_fwd(f"ca{i}_attn_f", x, w["ca_norm"][i:i + 1], w["ca_wq"][i], k, v, w["ca_wo"][i])
    return x1, (x, xn, r, mn, rm, q, k, v, o)


def ca_bwd(i, dx, dxb, saved, mem, w):
    x, xn, r, mn, rm, q, k, v, o = saved
    t, m = x.shape[0], mem.shape[0]
    g_wo = mm_tn(f"ca{i}_o_w", o, dxb)
    dx0, dx0b, dq, dk, dv, g_norm = attn_bwd(f"ca{i}_attn_b", dx, dxb, x, r, w["ca_norm"][i:i + 1], q, k, v,
                                             w["ca_wq"][i], w["ca_wo"][i])
    g_wq = mm_tn(f"ca{i}_q_w", xn, dq)
    g_wk = mm_tn(f"ca{i}_k_w", mn, dk)
    g_wv = mm_tn(f"ca{i}_v_w", mn, dv)
    (dmn,) = mm_nn(f"ca{i}_kv_b", m, D_MODEL, [(dk, w["ca_wk"][i], 0, "t"), (dv, w["ca_wv"][i], 0, "t")], 1, _first, [F32])
    g_mnorm = rms_bwd_gain_only(f"ca{i}_mnorm_b", dmn, mem, rm)
    return dx0, dx0b, dict(ca_norm=g_norm, ca_mem_norm=g_mnorm, ca_wq=g_wq, ca_wk=g_wk, ca_wv=g_wv, ca_wo=g_wo)


FFN_ROWS = 512
FFN_CHUNK = 256


def _whole(a):
    return pl.BlockSpec(a.shape, lambda i: (0,) * a.ndim, pipeline_mode=pl.Buffered(1))


def ffn_fused_fwd(name, x, gain, wg_t, wu_t, wd, target=None, final_gain=None):
    t, d = x.shape
    hid = wd.shape[0]
    tm = _tile(t, FFN_ROWS)
    last = target is not None
    n_main = 4 if last else 1

    def body(*refs):
        x_ref, g_ref, wg_ref, wu_ref, wd_ref = refs[:5]
        rest = refs[5:]
        if last:
            tgt_ref, fg_ref = rest[:2]
            rest = rest[2:]
        main, (xn_ref, r_ref, dgate_ref, dup_ref, h_ref) = rest[:n_main], rest[n_main:]
        xv = x_ref[...]
        rv = lax.rsqrt(jnp.mean(xv * xv, axis=-1, keepdims=True) + EPS)
        xn = (xv * rv * g_ref[...]).astype(BF16)
        xn_ref[...] = xn
        r_ref[...] = rv
        for j in range(hid // FFN_CHUNK):
            cs = slice(j * FFN_CHUNK, (j + 1) * FFN_CHUNK)
            g = lax.dot_general(xn, wg_ref[cs, :], _NT, preferred_element_type=F32)
            u = lax.dot_general(xn, wu_ref[cs, :], _NT, preferred_element_type=F32)
            s = _sigmoid(g)
            silu = g * s
            dgate_ref[:, cs] = (u * (s + silu * (1.0 - s))).astype(BF16)
            dup_ref[:, cs] = silu.astype(BF16)
            h_ref[:, cs] = (silu * u).astype(BF16)
        acc = jnp.dot(h_ref[...], wd_ref[...], preferred_element_type=F32)
        if not last:
            main[0][...] = xv + acc
        else:
            dx, _, dgain, part = _final_loss_epi([acc], xv, tgt_ref[...], fg_ref[...])

            @pl.when(pl.program_id(0) == 0)
            def _():
                main[2][...] = jnp.zeros(main[2].shape, F32)
                main[3][...] = jnp.zeros(main[3].shape, F32)
            main[0][...] = dx
            main[1][...] = dx.astype(BF16)
            main[2][...] += dgain
            main[3][...] += part

    rows = pl.BlockSpec((tm, d), lambda i: (i, 0))
    wide = pl.BlockSpec((tm, hid), lambda i: (i, 0))
    col = pl.BlockSpec((tm, 1), lambda i: (i, 0))
    ins, in_specs = [x, gain, wg_t, wu_t, wd], [rows, _whole(gain), _whole(wg_t), _whole(wu_t), _whole(wd)]
    if last:
        ins += [target, final_gain]
        in_specs += [rows, _whole(final_gain)]
        out_specs = [rows, rows, pl.BlockSpec((1, d), lambda i: (0, 0)), pl.BlockSpec((1, 1), lambda i: (0, 0))]
        out_shape = [S((t, d), F32), S((t, d), BF16), S((1, d), F32), S((1, 1), F32)]
    else:
        out_specs, out_shape = [rows], [S((t, d), F32)]
    out_specs += [rows, col, wide, wide, wide]
    out_shape += [S((t, d), BF16), S((t, 1), F32)] + [S((t, hid), BF16)] * 3
    outs = pl.pallas_call(body, grid=(t // tm,), in_specs=in_specs, out_specs=out_specs, out_shape=out_shape,
                          compiler_params=_cp("arbitrary" if last else "parallel"), name=name)(*ins)
    return (tuple(outs[:4]) if last else outs[0]), outs[n_main:]


def ffn_fused_bwd(name, dx, dxb, x, r, gain, dgate, dup, wg_t, wu_t, wd):
    t, d = x.shape
    hid = wd.shape[0]
    tm = _tile(t, FFN_ROWS // 2)

    def body(dx_ref, dxb_ref, x_ref, r_ref, g_ref, dgate_ref, dup_ref, wg_ref, wu_ref, wd_ref,
             dxo_ref, dxbo_ref, dg_ref, du_ref, dgain_ref):
        @pl.when(pl.program_id(0) == 0)
        def _():
            dgain_ref[...] = jnp.zeros(dgain_ref.shape, F32)

        dxb = dxb_ref[...]
        for j in range(hid // FFN_CHUNK):
            cs = slice(j * FFN_CHUNK, (j + 1) * FFN_CHUNK)
            dh = lax.dot_general(dxb, wd_ref[cs, :], _NT, preferred_element_type=F32)
            dg_ref[:, cs] = (dh * dgate_ref[:, cs].astype(F32)).astype(BF16)
            du_ref[:, cs] = (dh * dup_ref[:, cs].astype(F32)).astype(BF16)
        dxn = (jnp.dot(dg_ref[...], wg_ref[...], preferred_element_type=F32)
               + jnp.dot(du_ref[...], wu_ref[...], preferred_element_type=F32))
        dxo, _, dgain = _rms_bwd_epi([dxn], x_ref[...], dx_ref[...], r_ref[...], g_ref[...])
        dxo_ref[...] = dxo
        dxbo_ref[...] = dxo.astype(BF16)
        dgain_ref[...] += dgain

    rows = pl.BlockSpec((tm, d), lambda i: (i, 0))
    wide = pl.BlockSpec((tm, hid), lambda i: (i, 0))
    col = pl.BlockSpec((tm, 1), lambda i: (i, 0))
    return pl.pallas_call(
        body, grid=(t // tm,),
        in_specs=[rows, rows, rows, col, _whole(gain), wide, wide, _whole(wg_t), _whole(wu_t), _whole(wd)],
        out_specs=[rows, rows, wide, wide, pl.BlockSpec((1, d), lambda i: (0, 0))],
        out_shape=[S((t, d), F32), S((t, d), BF16), S((t, hid), BF16), S((t, hid), BF16), S((1, d), F32)],
        compiler_params=_cp("arbitrary"), name=name)(dx, dxb, x, r, gain, dgate, dup, wg_t, wu_t, wd)


def ffn_fwd(i, x, w, target=None):
    out, (xn, r, dgate, dup, h) = ffn_fused_fwd(f"ffn{i}_f", x, w["ffn_norm"][i:i + 1], w["ffn_w_gate_t"][i],
                                                w["ffn_w_up_t"][i], w["ffn_w_down"][i], target,
                                                None if target is None else w["final_norm"])
    return out, (x, xn, r, dgate, dup, h)


def ffn_bwd(i, dx, dxb, saved, w):
    x, xn, r, dgate, dup, h = saved
    g_wd = mm_tn(f"ffn{i}_down_w", h, dxb)
    dx0, dx0b, dg, du, g_norm = ffn_fused_bwd(f"ffn{i}_b", dx, dxb, x, r, w["ffn_norm"][i:i + 1], dgate, dup,
                                              w["ffn_w_gate_t"][i], w["ffn_w_up_t"][i], w["ffn_w_down"][i])
    g_wg_t = mm_tn(f"ffn{i}_gate_w", dg, xn)
    g_wu_t = mm_tn(f"ffn{i}_up_w", du, xn)
    return dx0, dx0b, dict(ffn_norm=g_norm, ffn_w_gate_t=g_wg_t, ffn_w_up_t=g_wu_t, ffn_w_down=g_wd)


def local_step(x, mem, target, w, fetch=None, on_grads=None, anchor=None):
    consts = s5_setup(w, anchor)

    def need(stage, after):
        if fetch is not None:
            for k, v in fetch(stage, after).items():
                if isinstance(k, tuple):
                    w.setdefault(k[0], {})[k[1]] = v
                else:
                    w[k] = v

    need(0, consts["pw"])
    x1, s_e = even_fwd(x, w, lambda after: need(1, after))
    x2, s_c0 = ca_fwd(0, x1, mem, w)
    need(2, x2)
    x3, s_f0 = ffn_fwd(0, x2, w)
    x4, s_o = odd_fwd(x3, w, consts)
    need(3, x4)
    x5, s_c1 = ca_fwd(1, x4, mem, w)
    need(4, x5)
    (dx, dxb, g_final, loss), s_f1 = ffn_fwd(1, x5, w, target)

    def emit(stage, carry, plain, layered=None, layer=0):
        if on_grads is None:
            return carry
        out = dict(plain)
        out.update({(k, layer): v for k, v in (layered or {}).items()})
        return on_grads(stage, out, list(carry))

    dx, dxb, g_f1 = ffn_bwd(1, dx, dxb, s_f1, w)
    dx, dxb = emit(0, (dx, dxb), {}, g_f1, 1)
    dx, dxb, g_c1 = ca_bwd(1, dx, dxb, s_c1, mem, w)
    dx, dxb, g_o = odd_bwd(dx, dxb, s_o, w, consts)
    dx, dxb = emit(1, (dx, dxb), g_o, g_c1, 1)
    dx, dxb, g_f0 = ffn_bwd(0, dx, dxb, s_f0, w)
    dx, dxb = emit(2, (dx, dxb), {}, g_f0, 0)
    dx, dxb, g_c0 = ca_bwd(0, dx, dxb, s_c0, mem, w)
    dx, dxb = emit(3, (dx, dxb), {}, g_c0, 0)
    dproj, g_e = even_bwd_mixers(dxb, s_e, w)
    dproj = emit(4, dproj, {**g_e, "o_norm": g_o["o_norm"], "o_d": g_o["o_d"]})
    dx, dxb, g_e["e_norm"] = even_bwd_input(dx, dproj, s_e, w)

    grads = dict(g_e)
    grads.update(g_o)
    for g0, g1 in ((g_c0, g_c1), (g_f0, g_f1)):
        for k in g0:
            grads[k] = jnp.concatenate([g0[k], g1[k]], axis=0) if k.endswith("norm") else (g0[k], g1[k])
    grads["final_norm"] = g_final
    return loss, dx, grads


def _group(axes):
    pos = {a: lax.axis_index(a) for a in ("x", "y", "c")}
    me = 0
    for a in axes:
        me = me * 2 + pos[a]
    peers = []
    for mask in range(1, 2 ** len(axes)):
        peer = dict(pos)
        for bit, a in enumerate(axes):
            if (mask >> (len(axes) - 1 - bit)) & 1:
                peer[a] = 1 - pos[a]
        idx = 0
        for a in axes:
            idx = idx * 2 + peer[a]
        peers.append((idx, (peer["x"], peer["y"], peer["c"])))
    return me, peers


def _sibling():
    x, y, c = lax.axis_index("x"), lax.axis_index("y"), lax.axis_index("c")
    return c, (x, y, 1 - c)


_HBM =pl.BlockSpec(memory_space=pltpu.HBM)
_SEM = pl.BlockSpec(memory_space=pltpu.SEMAPHORE)
_EFFECT = pltpu.SideEffectType.DATAFLOW_SIDE_EFFECTING


def _gather_peers(direct):
    chip, _ = _group(("x", "y"))
    core = lax.axis_index("c")
    if direct:
        _, peers = _group(_ALL)
        return chip, core, [(idx // 2, idx % 2, dev) for idx, dev in peers]
    _, peers = _group(("x", "y"))
    return chip, core, [(idx, core, dev) for idx, dev in peers]


def gather_ici_start(name, groups, direct):
    flat = [b for g in groups for b in g]
    sizes = [len(g) for g in groups]
    k_ops, n_g = len(flat), len(groups)
    lands = [lax.empty((4, 2) + tuple(b.shape), b.dtype) for b in flat]
    fan = [N_DEV - 1 if d else 3 for d in direct]

    def body(*refs):
        src, land = refs[:k_ops], refs[k_ops:2 * k_ops]
        sems = refs[2 * k_ops:2 * k_ops + 3 * n_g]
        token = refs[-1]
        i = 0
        for g in range(n_g):
            send, recv, loc = sems[3 * g:3 * g + 3]
            chip, core, peers = _gather_peers(direct[g])
            for j in range(sizes[g]):
                pltpu.make_async_copy(src[i], land[i].at[chip, core], loc.at[j]).start()
                for k, (_, _, dev) in enumerate(peers):
                    s = fan[g] * j + k
                    pltpu.make_async_remote_copy(src_ref=src[i], dst_ref=land[i].at[chip, core], send_sem=send.at[s],
                                                 recv_sem=recv.at[s], device_id=dev, device_id_type=MESH).start()
                i += 1
        token[...] = jnp.zeros(token.shape, token.dtype)

    sem_shapes = []
    for s, f in zip(sizes, fan):
        sem_shapes += [pltpu.SemaphoreType.DMA((f * s,)), pltpu.SemaphoreType.DMA((f * s,)), pltpu.SemaphoreType.DMA((s,))]
    thru = [pltpu.HBM(a.shape, a.dtype) for a in flat + lands]
    outs = pl.pallas_call(
        body, name=name, out_shape=tuple(sem_shapes) + tuple(thru) + (S((8, LANES), F32),),
        in_specs=[_HBM] * (2 * k_ops), out_specs=[_SEM] * (3 * n_g) + [_HBM] * (2 * k_ops) + [pl.BlockSpec(memory_space=pltpu.VMEM)],
        input_output_aliases={i: 3 * n_g + i for i in range(2 * k_ops)},
        compiler_params=pltpu.CompilerParams(has_side_effects=_EFFECT),
    )(*[pltpu.with_memory_space_constraint(a, pltpu.HBM) for a in flat + lands])
    sems = [tuple(outs[3 * g:3 * g + 3]) for g in range(n_g)]
    srcs_thru, lands_thru, off = [], [], 3 * n_g
    for s in sizes:
        srcs_thru.append(list(outs[off:off + s]))
        off += s
    for s in sizes:
        lands_thru.append(list(outs[off:off + s]))
        off += s
    return sems, srcs_thru, lands_thru, outs[-1]


def gather_ici_wait(name, srcs, lands, sems, after, direct=False):
    n = len(srcs)

    def body(*refs):
        src, land = refs[:n], refs[n:2 * n]
        send, recv, loc = refs[2 * n:2 * n + 3]
        chip, core, peers = _gather_peers(direct)
        for j in range(n):
            for k, (pchip, pcore, dev) in enumerate(peers):
                s = len(peers) * j + k
                cp = pltpu.make_async_remote_copy(src_ref=src[j], dst_ref=land[j].at[pchip, pcore], send_sem=send.at[s],
                                                  recv_sem=recv.at[s], device_id=dev, device_id_type=MESH)
                cp.wait_send()
                cp.wait_recv()
            pltpu.make_async_copy(src[j], land[j].at[chip, core], loc.at[j]).wait()

    outs = pl.pallas_call(
        body, name=name, out_shape=tuple(pltpu.HBM(a.shape, a.dtype) for a in list(srcs) + list(lands)),
        in_specs=[_HBM] * (2 * n) + [_SEM] * 3 + [ANY], out_specs=[_HBM] * (2 * n),
        input_output_aliases={i: i for i in range(2 * n)},
        compiler_params=pltpu.CompilerParams(has_side_effects=_EFFECT),
    )(*srcs, *lands, *sems, after)
    return list(outs[n:])


def gather_d2d(name, bufs):
    k_ops = len(bufs)

    def body(*refs):
        in_refs, out_refs = refs[:k_ops], refs[k_ops:2 * k_ops]
        send_sems, recv_sems = refs[2 * k_ops:]
        core, sib = _sibling()
        sent, landed = [], []
        for i in range(k_ops):
            cp = pltpu.make_async_remote_copy(src_ref=in_refs[i].at[:, core], dst_ref=out_refs[i].at[:, core],
                                              send_sem=send_sems.at[i], recv_sem=recv_sems.at[i], device_id=sib, device_id_type=MESH)
            cp.start()
            sent.append(cp)
            landed.append(pltpu.make_async_remote_copy(src_ref=in_refs[i].at[:, core], dst_ref=out_refs[i].at[:, 1 - core],
                                                       send_sem=send_sems.at[i], recv_sem=recv_sems.at[i],
                                                       device_id=sib, device_id_type=MESH))
        for cp in landed:
            cp.wait_recv()
        for cp in sent:
            cp.wait_send()

    return pl.pallas_call(
        body, in_specs=[ANY] * k_ops, out_specs=[ANY] * k_ops, out_shape=[S(b.shape, b.dtype) for b in bufs],
        input_output_aliases={i: i for i in range(k_ops)},
        scratch_shapes=[pltpu.SemaphoreType.DMA((k_ops,)), pltpu.SemaphoreType.DMA((k_ops,))],
        name=name)(*bufs)


_ALL = ("x", "y", "c")


def _unit_rows(units):
    offs, off = [], 0
    for u in units:
        offs.append(off)
        off += u.shape[1]
    return offs, off


def scatter_start(name, units, carry):
    n_u, n_c = len(units), len(carry)
    offs, rows = _unit_rows(units)
    land = lax.empty((N_DEV, rows) + tuple(units[0].shape[2:]), units[0].dtype)
    fan = N_DEV - 1

    def body(*refs):
        u_refs, land_ref = refs[:n_u], refs[n_u]
        send, recv, loc = refs[n_u + 1 + n_c:n_u + 4 + n_c]
        me, peers = _group(_ALL)
        for j in range(n_u):
            rs = pl.ds(offs[j], units[j].shape[1])
            pltpu.make_async_copy(u_refs[j].at[me], land_ref.at[me, rs], loc.at[j]).start()
            for k, (idx, dev) in enumerate(peers):
                pltpu.make_async_remote_copy(src_ref=u_refs[j].at[idx], dst_ref=land_ref.at[me, rs], send_sem=send.at[fan * j + k],
                                             recv_sem=recv.at[fan * j + k], device_id=dev, device_id_type=MESH).start()

    thru = list(units) + [land] + list(carry)
    outs = pl.pallas_call(
        body, name=name,
        out_shape=(pltpu.SemaphoreType.DMA((fan * n_u,)), pltpu.SemaphoreType.DMA((fan * n_u,)), pltpu.SemaphoreType.DMA((n_u,)))
        + tuple(pltpu.HBM(a.shape, a.dtype) for a in thru),
        in_specs=[_HBM] * len(thru), out_specs=[_SEM] * 3 + [_HBM] * len(thru),
        input_output_aliases={i: 3 + i for i in range(len(thru))},
        compiler_params=pltpu.CompilerParams(has_side_effects=_EFFECT),
    )(*[pltpu.with_memory_space_constraint(a, pltpu.HBM) for a in thru])
    return tuple(outs[:3]), list(outs[3:3 + n_u]), outs[3 + n_u], list(outs[4 + n_u:])


def scatter_wait(name, units, land, sems, after):
    n_u = len(units)
    offs, _ = _unit_rows(units)
    fan = N_DEV - 1

    def body(*refs):
        u_refs, land_ref = refs[:n_u], refs[n_u]
        send, recv, loc = refs[n_u + 1:n_u + 4]
        me, peers = _group(_ALL)
        for j in range(n_u):
            rs = pl.ds(offs[j], units[j].shape[1])
            for k, (idx, dev) in enumerate(peers):
                cp = pltpu.make_async_remote_copy(src_ref=u_refs[j].at[idx], dst_ref=land_ref.at[idx, rs], send_sem=send.at[fan * j + k],
                                                  recv_sem=recv.at[fan * j + k], device_id=dev, device_id_type=MESH)
                cp.wait_send()
                cp.wait_recv()
            pltpu.make_async_copy(u_refs[j].at[me], land_ref.at[me, rs], loc.at[j]).wait()

    thru = list(units) + [land]
    outs = pl.pallas_call(
        body, name=name, out_shape=tuple(pltpu.HBM(a.shape, a.dtype) for a in thru),
        in_specs=[_HBM] * len(thru) + [_SEM] * 3 + [ANY], out_specs=[_HBM] * len(thru),
        input_output_aliases={i: i for i in range(len(thru))},
        compiler_params=pltpu.CompilerParams(has_side_effects=_EFFECT),
    )(*thru, *sems, after)
    return outs[n_u]


def _row_tile(rows, cap=512):
    return next(t for t in range(cap - cap % 16, 0, -16) if rows % t == 0)


def sum_shares(name, recv, me):
    n, rows, c = recv.shape
    tr = _row_tile(rows)

    def body(me_ref, *refs):
        acc = refs[0][...].astype(F32)
        for r in refs[1:n]:
            acc = acc + r[...].astype(F32)
        refs[n][...] = acc

    def slot(mask):
        return pl.BlockSpec((None, tr, c), lambda i, me, mask=mask: (jnp.bitwise_xor(me[0], mask), i, 0))

    spec = pltpu.PrefetchScalarGridSpec(
        num_scalar_prefetch=1, grid=(rows // tr,), in_specs=[slot(k) for k in range(n)],
        out_specs=pl.BlockSpec((tr, c), lambda i, me: (i, 0)))
    return pl.pallas_call(body, grid_spec=spec, out_shape=S((rows, c), F32),
                          compiler_params=_cp("parallel"), name=name)(me, *([recv] * n))


def sum_slots(name, slots):
    n, r, c = slots.shape

    def body(s_ref, o_ref):
        acc = s_ref[0]
        for j in range(1, n):
            acc = acc + s_ref[j]
        o_ref[...] = acc

    return pl.pallas_call(body, out_shape=S((r, c), F32), compiler_params=pltpu.CompilerParams(vmem_limit_bytes=VMEM_LIMIT),
                          name=name)(slots)


def adamw_units(name, pieces, transposed, w, m, v):
    n_l, k, n = w.shape
    tk = _tile(k, 512) if transposed else k
    c1 = 1.0 - ADAM_B1 ** ADAM_STEP
    c2 = 1.0 - ADAM_B2 ** ADAM_STEP

    def body(*refs):
        p_refs, (w_ref, m_ref, v_ref, g_ref, d_ref, m2_ref, v2_ref) = refs[:n_l], refs[n_l:]
        gv = p_refs[0][...]
        for j in range(1, n_l):
            gv = jnp.where(pl.program_id(0) == j, p_refs[j][...], gv)
        if transposed:
            gv = gv.T
        m2 = ADAM_B1 * m_ref[...] + (1.0 - ADAM_B1) * gv
        v2 = ADAM_B2 * v_ref[...] + (1.0 - ADAM_B2) * (gv * gv)
        g_ref[...] = gv
        m2_ref[...] = m2
        v2_ref[...] = v2
        d_ref[...] = -ADAM_LR * ((m2 / c1) / (jnp.sqrt(v2 / c2) + ADAM_EPS) + ADAM_WD * w_ref[...])

    piece = pl.BlockSpec((n, tk), lambda l, i: (0, i)) if transposed else pl.BlockSpec((k, n), lambda l, i: (0, 0))
    blk = pl.BlockSpec((None, tk, n), lambda l, i: (l, i, 0))
    return tuple(pl.pallas_call(body, grid=(n_l, k // tk), in_specs=[piece] * n_l + [blk] * 3, out_specs=[blk] * 4,
                                out_shape=[S(w.shape, F32)] * 4, compiler_params=_cp("parallel", "parallel"),
                                name=name)(*pieces, w, m, v))


def adamw_native(name, g, w, m, v, tr=512):
    shape = w.shape
    cols = shape[-1]
    rows = w.size // cols
    tr = _tile(rows, tr) if rows % 8 == 0 else rows
    c1 = 1.0 - ADAM_B1 ** ADAM_STEP
    c2 = 1.0 - ADAM_B2 ** ADAM_STEP

    def body(g_ref, w_ref, m_ref, v_ref, d_ref, m2_ref, v2_ref):
        gv = g_ref[...]
        m2 = ADAM_B1 * m_ref[...] + (1.0 - ADAM_B1) * gv
        v2 = ADAM_B2 * v_ref[...] + (1.0 - ADAM_B2) * (gv * gv)
        m2_ref[...] = m2
        v2_ref[...] = v2
        d_ref[...] = -ADAM_LR * ((m2 / c1) / (jnp.sqrt(v2 / c2) + ADAM_EPS) + ADAM_WD * w_ref[...])

    row = pl.BlockSpec((tr, cols), lambda i: (i, 0))
    outs = pl.pallas_call(body, grid=(rows // tr,), in_specs=[row] * 4, out_specs=[row] * 3,
                          out_shape=[S((rows, cols), F32)] * 3, compiler_params=_cp("parallel"),
                          name=name)(*[a.reshape(rows, cols) for a in (g, w, m, v)])
    return tuple(o.reshape(shape) for o in outs)


_REPLICATED = ("e_norm", "e_gmlp_w", "e_gmlp_b", "e_conv_b", "e_conv_ln_g", "e_conv_ln_b", "o_lam_re", "o_lam_im", "o_log_dt",
               "o_b_re", "o_b_im", "o_c_re", "o_c_im", "ca_norm", "ca_mem_norm", "ffn_norm", "final_norm")
_ORDER = ("e_norm", "e_w_in", "e_gmlp_w", "e_gmlp_b", "e_conv_w", "e_conv_b", "e_conv_ln_g", "e_conv_ln_b", "e_w_out",
          "o_norm", "o_w_in", "o_lam_re", "o_lam_im", "o_log_dt", "o_b_re", "o_b_im", "o_c_re", "o_c_im", "o_d", "o_w_out",
          "ca_norm", "ca_mem_norm", "ca_wq", "ca_wk", "ca_wv", "ca_wo", "ffn_norm", "ffn_w_gate", "ffn_w_up", "ffn_w_down",
          "final_norm")


def _rows128(a, multiple=8):
    flat = a.reshape(-1)
    rows = -(-flat.shape[0] // (LANES * multiple)) * multiple
    return jnp.pad(flat, (0, rows * LANES - flat.shape[0])).reshape(rows, LANES)


def _shard(full, axis):
    s = full.shape
    return jnp.moveaxis(full.reshape(s[:axis] + (N_DEV, s[axis] // N_DEV) + s[axis + 1:]), axis, 0)


_UNITS = (("e_w_in", 0, True), ("e_w_out", 0, False), ("o_w_in", 0, False), ("o_w_out", 0, True),
          *[(n, i, False) for n in ("ca_wq", "ca_wk", "ca_wv", "ca_wo") for i in (0, 1)],
          *[(n, i, tr) for n, tr in (("ffn_w_gate", True), ("ffn_w_up", True), ("ffn_w_down", False)) for i in (0, 1)])
_LAYERED = ("ca_wq", "ca_wk", "ca_wv", "ca_wo", "ffn_w_gate", "ffn_w_up", "ffn_w_down")
_SMALL_SHARDED = (("e_conv_w", 2), ("o_norm", 1), ("o_d", 1))
RS_ROW = 1024


def _unit_key(name, tr):
    return name + "_t" if tr else name


def _stage_of(name, layer):
    if name.startswith("e_"):
        return 0 if name == "e_w_in" else 1
    if name.startswith("o_"):
        return 2
    if name.startswith("ca_"):
        return 1 if layer == 0 else 3
    return 2 if layer == 0 else 4


GATHER_STAGES = 5
GATHER_DIRECT = (False, False, False, True, False)


def weight_fetcher(local):
    groups, meta = [[] for _ in range(GATHER_STAGES)], [[] for _ in range(GATHER_STAGES)]
    for name, layer, tr in _UNITS:
        blk = local[name][layer]
        st = _stage_of(name, layer)
        groups[st].append(_bf(blk.T if tr else blk))
        meta[st].append((name, layer, tr))
    small = jnp.concatenate([local[name].reshape(-1) for name, _ in _SMALL_SHARDED])
    groups[0].append(_rows128(small))
    direct = list(GATHER_DIRECT)
    sems, srcs, lands, token = gather_ici_start("ag_w_start", groups, direct)

    def fetch(stage, after):
        bufs = gather_ici_wait(f"ag_w_wait{stage}", srcs[stage], lands[stage], sems[stage], after, direct[stage])
        if not direct[stage]:
            bufs = gather_d2d(f"ag_w_d2d{stage}", bufs)
        got = {}
        for (name, layer, tr), blk, buf in zip(meta[stage], groups[stage], bufs):
            arr = buf.reshape((N_DEV * blk.shape[0],) + tuple(blk.shape[1:]))
            if name in _LAYERED:
                got[(_unit_key(name, tr), layer)] = arr
            else:
                got[_unit_key(name, tr)] = arr
        if stage == 0:
            flat = bufs[-1].reshape(N_DEV, -1)
            off = 0
            for name, axis in _SMALL_SHARDED:
                blk = local[name]
                seg = flat[:, off:off + blk.size].reshape((N_DEV,) + blk.shape)
                off += blk.size
                seg = jnp.moveaxis(seg, 0, axis)
                got[name] = seg.reshape(seg.shape[:axis] + (-1,) + seg.shape[axis + 2:])
            got["e_conv_w"] = got["e_conv_w"][0]
        return got

    return fetch, token


def _grad_stage_of(name, layer):
    if name.startswith("e_"):
        return 4
    if name.startswith("o_"):
        return 1
    if name.startswith("ca_"):
        return 3 if layer == 0 else 1
    return 2 if layer == 0 else 0


GRAD_STAGES = 5
SMALL_ROWS = 16


def gradient_reducer(local, mom, var):
    me = (4 * lax.axis_index("x") + 2 * lax.axis_index("y") + lax.axis_index("c")).astype(jnp.int32).reshape(1)
    pending = []

    def start(stage, grads, carry):
        units = [u for u in _UNITS if _grad_stage_of(u[0], u[1]) == stage]
        parts, spans = [], []
        for name, layer, tr in units:
            key = _unit_key(name, tr)
            g = grads[(key, layer)] if name in _LAYERED else grads[key]
            part = g.reshape(N_DEV, -1, RS_ROW)
            spans.append((part.shape[1], g.shape[0] // N_DEV, g.shape[1]))
            parts.append(part)
        if stage == GRAD_STAGES - 1:
            small = jnp.concatenate([_shard(grads[name], axis).reshape(N_DEV, -1) for name, axis in _SMALL_SHARDED], axis=1)
            small = jnp.pad(small, ((0, 0), (0, SMALL_ROWS * RS_ROW - small.shape[1])))
            parts.append(small.astype(BF16).reshape(N_DEV, SMALL_ROWS, RS_ROW))
        sems, sent, land, carry = scatter_start(f"rs_start{stage}", parts, carry)
        pending.append((stage, units, spans, sems, sent, land))
        return carry

    def finish(after):
        res, per_layer, small_flat = {}, {}, None
        for stage, units, spans, sems, sent, land in pending:
            land = scatter_wait(f"rs_wait{stage}", sent, land, sems, after)
            total = sum_shares(f"rs_sum{stage}", land, me)
            off = 0
            for (name, layer, tr), (rows, r, c) in zip(units, spans):
                per_layer.setdefault(name, {})[layer] = (total[off:off + rows].reshape(r, c), tr)
                off += rows
            if stage == GRAD_STAGES - 1:
                small_flat = total[off:off + SMALL_ROWS].reshape(-1)
        for name, by_layer in per_layer.items():
            pieces = [by_layer[i][0] for i in sorted(by_layer)]
            res[name] = adamw_units("adamw_" + name, pieces, by_layer[0][1], local[name], mom[name], var[name])
        off = 0
        for name, _ in _SMALL_SHARDED:
            blk = local[name]
            g = small_flat[off:off + blk.size].reshape(blk.shape)
            off += blk.size
            res[name] = (g,) + adamw_native("adamw_" + name, g, blk, mom[name], var[name])
        return res

    return start, finish


def replicated_start(grads, loss):
    pack = jnp.concatenate([_rows128(grads[name]) for name in _REPLICATED] + [_rows128(loss)], axis=0)
    sems, srcs, lands, token = gather_ici_start("ag_g_start", [[pack]], [False])
    return sems[0], srcs[0], lands[0], token


def replicated_finish(handle, after, w, mom, var):
    sems, srcs, lands, _ = handle
    (buf,) = gather_d2d("ag_g_d2d", gather_ici_wait("ag_g_wait", srcs, lands, sems, after))
    rows = srcs[0].shape[0]
    total = sum_slots("ag_g_sum", buf.reshape(N_DEV, rows, LANES))
    res, off = {}, 0
    for name in _REPLICATED:
        n = w[name].size
        nr = -(-n // (LANES * 8)) * 8
        g = total[off:off + nr].reshape(-1)[:n].reshape(w[name].shape)
        off += nr
        res[name] = (g,) + adamw_native("adamw_" + name, g, w[name], mom[name], var[name])
    return res, total[off, 0]


def kernel(x, mem, e_norm, e_w_in, e_gmlp_w, e_gmlp_b, e_conv_w, e_conv_b, e_conv_ln_g, e_conv_ln_b, e_w_out, o_norm, o_w_in, o_lam_re, o_lam_im, o_log_dt, o_b_re, o_b_im, o_c_re, o_c_im, o_d, o_w_out, ca_norm, ca_mem_norm, ca_wq, ca_wk, ca_wv, ca_wo, ffn_norm, ffn_w_gate, ffn_w_up, ffn_w_down, final_norm, loss_target, m_e_norm, m_e_w_in, m_e_gmlp_w, m_e_gmlp_b, m_e_conv_w, m_e_conv_b, m_e_conv_ln_g, m_e_conv_ln_b, m_e_w_out, m_o_norm, m_o_w_in, m_o_lam_re, m_o_lam_im, m_o_log_dt, m_o_b_re, m_o_b_im, m_o_c_re, m_o_c_im, m_o_d, m_o_w_out, m_ca_norm, m_ca_mem_norm, m_ca_wq, m_ca_wk, m_ca_wv, m_ca_wo, m_ffn_norm, m_ffn_w_gate, m_ffn_w_up, m_ffn_w_down, m_final_norm, v_e_norm, v_e_w_in, v_e_gmlp_w, v_e_gmlp_b, v_e_conv_w, v_e_conv_b, v_e_conv_ln_g, v_e_conv_ln_b, v_e_w_out, v_o_norm, v_o_w_in, v_o_lam_re, v_o_lam_im, v_o_log_dt, v_o_b_re, v_o_b_im, v_o_c_re, v_o_c_im, v_o_d, v_o_w_out, v_ca_norm, v_ca_mem_norm, v_ca_wq, v_ca_wk, v_ca_wv, v_ca_wo, v_ffn_norm, v_ffn_w_gate, v_ffn_w_up, v_ffn_w_down, v_final_norm):
    given = dict(locals())
    local = {k: given[k] for k in _ORDER}
    mom = {k: given["m_" + k] for k in _ORDER}
    var = {k: given["v_" + k] for k in _ORDER}

    w = {}
    w.update({
        "e_norm": e_norm, "e_gmlp_w": e_gmlp_w[0], "e_gmlp_b": e_gmlp_b.reshape(A_GROUPS, GMLP_BLOCK, 1),
        "e_conv_b": e_conv_b, "e_conv_ln_g": e_conv_ln_g, "e_conv_ln_b": e_conv_ln_b,
        "o_lam_re": o_lam_re[0], "o_lam_im": o_lam_im[0], "o_log_dt": o_log_dt[0], "o_b_re": o_b_re[0], "o_b_im": o_b_im[0],
        "o_c_re": o_c_re[0], "o_c_im": o_c_im[0], "ca_norm": ca_norm, "ca_mem_norm": ca_mem_norm, "ffn_norm": ffn_norm,
        "final_norm": final_norm.reshape(1, D_MODEL),
    })
    start_reduce, finish_reduce = gradient_reducer(local, mom, var)
    fetch, token = weight_fetcher(local)
    loss_part, grad_x, grads = local_step(x[0], mem[0], loss_target[0], w, fetch, start_reduce, token[0:1, 0:1])
    grads["final_norm"] = grads["final_norm"].reshape(D_MODEL)

    handle = replicated_start(grads, loss_part)
    res = finish_reduce(handle[3])
    rep, loss = replicated_finish(handle, res["ffn_w_down"][1], local, mom, var)
    res.update(rep)
    return (loss, grad_x[None], *[res[k][0] for k in _ORDER], *[res[k][1] for k in _ORDER],
            *[res[k][2] for k in _ORDER], *[res[k][3] for k in _ORDER])
```

```python
import jax
import jax.numpy as jnp
from jax import lax
from jax.experimental import pallas as pl
from jax.experimental.pallas import tpu as pltpu

F32 = jnp.float32
BF16 = jnp.bfloat16
S = jax.ShapeDtypeStruct

D_MODEL = 1024
A_WIDTH = 512
A_GROUPS = 4
GMLP_BLOCK = 128
CHUNK = 64
B_WIDTH = 512
IN_WIDTH = 2 * A_WIDTH + 2 * B_WIDTH
CONV_WIDTH = 31
CONV_PAD = 32
C_WIDTH = 512
C_GROUP_CH = 16
C_GROUPS = 32
C_STATE = 64
N_STATE = C_GROUPS * C_STATE
CA_HEADS = 4
CA_HEAD_DIM = 256
EPS = 1e-6
ADAM_LR = 0.001
ADAM_B1 = 0.9
ADAM_B2 = 0.999
ADAM_EPS = 1e-08
ADAM_WD = 0.01
ADAM_STEP = 10
N_DEV = 8
LANES = 128
VMEM_LIMIT = 56 << 20
VMEM_BUDGET = 40 << 20
MM_TN_RESIDENT = 8 << 20
MESH = pl.DeviceIdType.MESH
ANY = pl.BlockSpec(memory_space=pl.ANY)


def _cp(*sem):
    return pltpu.CompilerParams(dimension_semantics=sem, vmem_limit_bytes=VMEM_LIMIT)


def _tile(n, pref):
    t = pref
    while n % t:
        t //= 2
    return t


def _bf(v):
    return v if v.dtype == BF16 else v.astype(BF16)


def _sigmoid(x):
    return 1.0 / (1.0 + jnp.exp(-x))


_GC = 0.7978845608028654


def _gelu(x):
    return 0.5 * x * (1.0 + jnp.tanh(_GC * (x + 0.044715 * x * x * x)))


def _gelu_grad(x):
    x2 = x * x
    t = jnp.tanh(_GC * (x + 0.044715 * x * x2))
    return 0.5 * (1.0 + t) + 0.5 * x * (1.0 - t * t) * _GC * (1.0 + 3.0 * 0.044715 * x2)


def _tspec(entry, tm):
    if isinstance(entry, tuple):
        arr, cb, width = entry
        return arr, pl.BlockSpec((tm, width), lambda i, cb=cb: (i, cb))
    return entry, pl.BlockSpec((tm, entry.shape[1]), lambda i: (i, 0))


def rows_call(name, fn, tiled, full, outs, accs, tm=256):
    pairs = [_tspec(e, tm) for e in tiled]
    arrs = [p[0] for p in pairs]
    rows = arrs[0].shape[0]
    tm = _tile(rows, tm)
    pairs = [_tspec(e, tm) for e in tiled]
    n_in = len(tiled) + len(full)
    n_out = len(outs)

    def body(*refs):
        vals = [r[...] for r in refs[:n_in]]
        o_refs = refs[n_in:n_in + n_out]
        a_refs = refs[n_in + n_out:]
        ov, av = fn(*vals)
        for r, v in zip(o_refs, ov):
            r[...] = v.astype(r.dtype)
        if a_refs:
            @pl.when(pl.program_id(0) == 0)
            def _():
                for r in a_refs:
                    r[...] = jnp.zeros(r.shape, r.dtype)
            for r, v in zip(a_refs, av):
                r[...] += v

    in_specs = [p[1] for p in pairs] + [pl.BlockSpec(a.shape, lambda i, nd=a.ndim: (0,) * nd) for a in full]
    out_specs = [pl.BlockSpec((tm, c), lambda i: (i, 0)) for c, _ in outs]
    out_specs += [pl.BlockSpec(s, lambda i, nd=len(s): (0,) * nd) for s in accs]
    out_shape = [S((rows, c), dt) for c, dt in outs] + [S(s, F32) for s in accs]
    return pl.pallas_call(body, grid=(rows // tm,), in_specs=in_specs, out_specs=out_specs, out_shape=out_shape,
                          compiler_params=_cp("arbitrary"), name=name)(*arrs, *full)


def mm_nn(name, m, n, pairs, n_acc, epi, outs, tiled=(), cols=(), rowv=(), sums=(), norm_gain=None):
    a_ops, a_slot, b_arrs, b_specs, idx, trans = [], [], [], [], [], []
    fixed = 0
    for pair in pairs:
        a, b, k = pair[:3]
        bt = len(pair) > 3
        arr, cb, kdim = a if isinstance(a, tuple) else (a, 0, a.shape[1])
        key = (id(arr), cb, kdim)
        if key not in [o[0] for o in a_ops]:
            a_ops.append((key, arr, cb, kdim))
        a_slot.append([o[0] for o in a_ops].index(key))
        b_arr, off = b if isinstance(b, tuple) else (b, 0)
        b_arrs.append(b_arr)
        if bt:
            assert off % n == 0 and b_arr.shape[1] == kdim
            b_specs.append(pl.BlockSpec((n, kdim), lambda i, o=off // n: (o, 0), pipeline_mode=pl.Buffered(1)))
        else:
            assert b_arr.shape[1] == n
            b_specs.append(pl.BlockSpec((kdim, n), lambda i, o=off: (o, 0), pipeline_mode=pl.Buffered(1)))
        fixed += kdim * n * b_arr.dtype.itemsize
        idx.append(k)
        trans.append(bt)
    per_row = sum(2 * kdim * arr.dtype.itemsize for _, arr, _, kdim in a_ops)
    per_row += sum(2 * n * t.dtype.itemsize for t in tiled) + sum(2 * n * jnp.dtype(dt).itemsize for dt in outs)
    cn = n if sums or cols else (512 if n % 512 == 0 else 256)
    per_row += (n_acc + 3) * cn * 4
    tm = next((t for t in (1024, 512, 256, 128) if m % t == 0 and fixed + t * per_row <= VMEM_BUDGET), _tile(m, 128))
    n_a, n_p, n_t = len(a_ops), len(pairs), len(tiled)
    n_in = n_a + n_p + n_t + len(cols) + len(rowv)
    normed = norm_gain is not None
    o0 = n_in + normed

    def body(*refs):
        a_vals = [None if normed and i == 0 else _bf(r[...]) for i, r in enumerate(refs[:n_a])]
        if normed:
            xv = refs[0][...]
            rv = lax.rsqrt(jnp.mean(xv * xv, axis=-1, keepdims=True) + EPS)
            a_vals[0] = (xv * rv * refs[n_in][...]).astype(BF16)
            refs[o0 + len(outs)][...] = a_vals[0]
            refs[o0 + len(outs) + 1][...] = rv
        for j in range(n // cn):
            cs = slice(j * cn, (j + 1) * cn)
            accs = [None] * n_acc
            for p in range(n_p):
                av, b_ref = a_vals[a_slot[p]], refs[n_a + p]
                if trans[p]:
                    d = lax.dot_general(av, _bf(b_ref[cs, :]), (((1,), (1,)), ((), ())), preferred_element_type=F32)
                else:
                    d = jnp.dot(av, _bf(b_ref[:, cs]), preferred_element_type=F32)
                accs[idx[p]] = d if accs[idx[p]] is None else accs[idx[p]] + d
            extra = [r[:, cs] for r in refs[n_a + n_p:n_a + n_p + n_t]] + [r[...] for r in refs[n_a + n_p + n_t:n_in - len(rowv)]]
            extra += [r[:, cs] for r in refs[n_in - len(rowv):n_in]]
            ov = epi(accs, *extra)
            for r, v in zip(refs[o0:o0 + len(outs)], ov):
                r[:, cs] = v.astype(r.dtype)
        sv = ov[len(outs):]
        if sums:
            s_refs = refs[o0 + len(outs) + 2 * normed:]

            @pl.when(pl.program_id(0) == 0)
            def _():
                for r in s_refs:
                    r[...] = jnp.zeros(r.shape, r.dtype)
            for r, v in zip(s_refs, sv):
                r[...] += v

    in_specs = [pl.BlockSpec((tm, kdim), lambda i, cb=cb: (i, cb)) for _, _, cb, kdim in a_ops] + b_specs
    in_specs += [pl.BlockSpec((tm, n), lambda i: (i, 0)) for _ in tiled]
    in_specs += [pl.BlockSpec((tm, 1), lambda i: (i, 0)) for _ in cols]
    in_specs += [pl.BlockSpec((1, n), lambda i: (0, 0)) for _ in rowv]
    out_specs = [pl.BlockSpec((tm, n), lambda i: (i, 0)) for _ in outs]
    out_shape = [S((m, n), dt) for dt in outs]
    gain = []
    if normed:
        k0 = a_ops[0][3]
        gain = [norm_gain]
        in_specs.append(pl.BlockSpec((1, k0), lambda i: (0, 0)))
        out_specs += [pl.BlockSpec((tm, k0), lambda i: (i, 0)), pl.BlockSpec((tm, 1), lambda i: (i, 0))]
        out_shape += [S((m, k0), BF16), S((m, 1), F32)]
    out_specs += [pl.BlockSpec(s, lambda i, nd=len(s): (0,) * nd) for s in sums]
    out_shape += [S(s, F32) for s in sums]
    return pl.pallas_call(body, grid=(m // tm,), in_specs=in_specs, out_specs=out_specs, out_shape=out_shape,
                          compiler_params=_cp("arbitrary" if sums else "parallel"),
                          name=name)(*[o[1] for o in a_ops], *b_arrs, *tiled, *cols, *rowv, *gain)


def mm_tn(name, a, b, out_dtype=BF16):
    if isinstance(a, tuple):
        a_arr, a_cb, m = a
    else:
        a_arr, a_cb, m = a, None, a.shape[1]
    if isinstance(b, tuple):
        b_arr, b_cb, n = b
    else:
        b_arr, b_cb, n = b, None, b.shape[1]
    t = a_arr.shape[0]
    whole_b = t * n * b_arr.dtype.itemsize <= MM_TN_RESIDENT and b_cb is None
    tn = n if whole_b else _tile(n, 512)
    tm = _tile(m, 512 if t * 512 * a_arr.dtype.itemsize * 2 + t * tn * b_arr.dtype.itemsize * 2 <= VMEM_BUDGET else 256)
    a_off = 0 if a_cb is None else a_cb * (m // tm)
    b_off = 0 if b_cb is None else b_cb * (n // tn)

    def body(a_ref, b_ref, o_ref):
        o_ref[...] = lax.dot_general(_bf(a_ref[...]), _bf(b_ref[...]), (((0,), (0,)), ((), ())),
                                     preferred_element_type=F32).astype(o_ref.dtype)

    if whole_b:
        b_spec = pl.BlockSpec((t, n), lambda i, j: (0, 0), pipeline_mode=pl.Buffered(1))
    else:
        b_spec = pl.BlockSpec((t, tn), lambda i, j: (0, j + b_off))
    return pl.pallas_call(
        body, grid=(m // tm, n // tn),
        in_specs=[pl.BlockSpec((t, tm), lambda i, j: (0, i + a_off)), b_spec],
        out_specs=pl.BlockSpec((tm, tn), lambda i, j: (i, j)), out_shape=S((m, n), out_dtype),
        compiler_params=_cp("parallel", "parallel"), name=name)(a_arr, b_arr)


def rms_bwd_gain_only(name, dxn, x, r):
    def fn(dv, xv, rv):
        return [], [jnp.sum(dv * xv * rv, axis=0, keepdims=True)]
    return rows_call(name, fn, [dxn, x, r], [], [], [(1, x.shape[1])])[0]


def _final_loss_epi(accs, res, tv, g):
    xv = res + accs[0]
    d = xv.shape[-1]
    r = lax.rsqrt(jnp.mean(xv * xv, axis=-1, keepdims=True) + EPS)
    xh = xv * r
    err = xh * g - tv
    dy = err * (1.0 / d)
    w = dy * g
    dx = r * (w - xh * jnp.mean(w * xh, axis=-1, keepdims=True))
    part = jnp.sum(jnp.sum(err * err, axis=-1, keepdims=True), axis=0, keepdims=True) * (0.5 / d)
    return [dx, dx, jnp.sum(dy * xh, axis=0, keepdims=True), part]


def _gmlp_mask():
    row = lax.broadcasted_iota(jnp.int32, (GMLP_BLOCK, GMLP_BLOCK), 0) // CHUNK
    col = lax.broadcasted_iota(jnp.int32, (GMLP_BLOCK, GMLP_BLOCK), 1) // CHUNK
    return col <= row


def _ln_plain(v):
    mu = jnp.mean(v, axis=-1, keepdims=True)
    vc = v - mu
    rstd = lax.rsqrt(jnp.mean(vc * vc, axis=-1, keepdims=True) + EPS)
    return vc * rstd, rstd


def even_out_fwd(name, proj, hc, x, w, b, ln_g, ln_b, w_out, tm=512):
    t, d = x.shape
    tm = _tile(t, tm)

    def body(au_ref, av_ref, hc_ref, x_ref, w_ref, b_ref, lg_ref, lb_ref, wo_ref, x1_ref, oa_ref, ob_ref):
        mask = _gmlp_mask()
        u = _gelu(au_ref[...])
        vn, _ = _ln_plain(_gelu(av_ref[...]))
        vnb = _bf(vn)
        for g in range(A_GROUPS):
            wg = _bf(jnp.where(mask, w_ref[g], 0.0))
            cs = slice(g * GMLP_BLOCK, (g + 1) * GMLP_BLOCK)
            for n in range(tm // GMLP_BLOCK):
                rs = slice(n * GMLP_BLOCK, (n + 1) * GMLP_BLOCK)
                sg = jnp.dot(wg, vnb[rs, cs], preferred_element_type=F32) + b_ref[g]
                oa_ref[rs, cs] = (u[rs, cs] * sg).astype(oa_ref.dtype)
        y, _ = _ln_plain(hc_ref[...])
        z = y * lg_ref[...] + lb_ref[...]
        ob_ref[...] = (z * _sigmoid(z)).astype(ob_ref.dtype)
        x1_ref[...] = (x_ref[...] + jnp.dot(oa_ref[...], wo_ref[0:A_WIDTH, :], preferred_element_type=F32)
                       + jnp.dot(ob_ref[...], wo_ref[A_WIDTH:, :], preferred_element_type=F32))

    half = pl.BlockSpec((tm, A_WIDTH), lambda i: (i, 0))
    return pl.pallas_call(
        body, grid=(t // tm,),
        in_specs=[half, pl.BlockSpec((tm, A_WIDTH), lambda i: (i, 1)), half, pl.BlockSpec((tm, d), lambda i: (i, 0)),
                  _whole(w), _whole(b), _whole(ln_g), _whole(ln_b), _whole(w_out)],
        out_specs=[pl.BlockSpec((tm, d), lambda i: (i, 0)), half, half],
        out_shape=[S((t, d), F32), S((t, A_WIDTH), BF16), S((t, B_WIDTH), BF16)],
        compiler_params=_cp("parallel"), name=name)(proj, proj, hc, x, w, b, ln_g, ln_b, w_out)


def gmlp_bwd(name, proj, dxb, w_out, w, b, tm=512):
    t = proj.shape[0]
    tm = _tile(t, tm)

    def body(au_ref, av_ref, dx_ref, wo_ref, w_ref, b_ref, dp_ref, dw_ref, db_ref):
        @pl.when(pl.program_id(0) == 0)
        def _():
            dw_ref[...] = jnp.zeros(dw_ref.shape, F32)
            db_ref[...] = jnp.zeros(db_ref.shape, F32)

        mask = _gmlp_mask()
        au = au_ref[...]
        av = av_ref[...]
        u = _gelu(au)
        vn, rstd = _ln_plain(_gelu(av))
        vnb = _bf(vn)
        dout = lax.dot_general(dx_ref[...], wo_ref[0:A_WIDTH, :], _NT, preferred_element_type=F32)
        dvn_cols = []
        for g in range(A_GROUPS):
            wm = jnp.where(mask, w_ref[g], 0.0)
            wg = _bf(wm)
            wgt = _bf(wm.T)
            cs = slice(g * GMLP_BLOCK, (g + 1) * GMLP_BLOCK)
            dwg = jnp.zeros((GMLP_BLOCK, GMLP_BLOCK), F32)
            dbg = jnp.zeros((GMLP_BLOCK, 1), F32)
            dvn_rows = []
            for n in range(tm // GMLP_BLOCK):
                rs = slice(n * GMLP_BLOCK, (n + 1) * GMLP_BLOCK)
                sg = jnp.dot(wg, vnb[rs, cs], preferred_element_type=F32) + b_ref[g]
                dp_ref[rs, cs] = (dout[rs, cs] * sg * _gelu_grad(au[rs, cs])).astype(dp_ref.dtype)
                dsg = dout[rs, cs] * u[rs, cs]
                dsgb = _bf(dsg)
                dbg = dbg + jnp.sum(dsg, axis=1, keepdims=True)
                dwg = dwg + lax.dot_general(dsgb, vnb[rs, cs], (((1,), (1,)), ((), ())), preferred_element_type=F32)
                dvn_rows.append(jnp.dot(wgt, dsgb, preferred_element_type=F32))
            dw_ref[g] += jnp.where(mask, dwg, 0.0)
            db_ref[g] += dbg
            dvn_cols.append(jnp.concatenate(dvn_rows, axis=0))
        dvn = jnp.concatenate(dvn_cols, axis=1)
        dv = rstd * (dvn - jnp.mean(dvn, axis=-1, keepdims=True) - vn * jnp.mean(dvn * vn, axis=-1, keepdims=True))
        dp_ref[:, A_WIDTH:] = (dv * _gelu_grad(av)).astype(dp_ref.dtype)

    return pl.pallas_call(
        body, grid=(t // tm,),
        in_specs=[pl.BlockSpec((tm, A_WIDTH), lambda i: (i, 0)), pl.BlockSpec((tm, A_WIDTH), lambda i: (i, 1)),
                  pl.BlockSpec((tm, dxb.shape[1]), lambda i: (i, 0)), pl.BlockSpec(w_out.shape, lambda i: (0, 0)),
                  pl.BlockSpec(w.shape, lambda i: (0, 0, 0)), pl.BlockSpec(b.shape, lambda i: (0, 0, 0))],
        out_specs=[pl.BlockSpec((tm, 2 * A_WIDTH), lambda i: (i, 0)),
                   pl.BlockSpec(w.shape, lambda i: (0, 0, 0)), pl.BlockSpec(b.shape, lambda i: (0, 0, 0))],
        out_shape=[S((t, 2 * A_WIDTH), BF16), S(w.shape, F32), S(b.shape, F32)],
        compiler_params=_cp("arbitrary"), name=name)(proj, proj, dxb, w_out, w, b)


CONV_ROWS = 256
CONV_ROWS_BWD = 64


def conv_fwd(name, proj, w, cb):
    t = proj.shape[0]
    tc = LANES
    rows = _tile(t, CONV_ROWS)
    a_cb, g_cb = 2 * A_WIDTH // tc, (2 * A_WIDTH + B_WIDTH) // tc

    def body(a_ref, g_ref, w_ref, cb_ref, o_ref, hpad):
        hpad[0:CONV_PAD, :] = jnp.zeros((CONV_PAD, tc), F32)

        def fill(i, _):
            r0 = pl.multiple_of(i * rows, rows)
            hpad[pl.ds(CONV_PAD + r0, rows), :] = a_ref[pl.ds(r0, rows), :] * _sigmoid(g_ref[pl.ds(r0, rows), :])
            return 0
        lax.fori_loop(0, t // rows, fill, 0)

        def conv(i, _):
            r0 = pl.multiple_of(i * rows, rows)
            win = hpad[pl.ds(r0, rows + CONV_PAD), :]
            acc = jnp.zeros((rows, tc), F32) + cb_ref[...]
            for b in range(SUB):
                wb = win if b == 0 else pltpu.roll(win, b, 0)
                for a in range(CONV_PAD // SUB):
                    k = CONV_WIDTH - 1 - (SUB * a + b)
                    if k >= 0:
                        lo = CONV_PAD - SUB * a
                        acc = acc + wb[lo:lo + rows, :] * w_ref[k:k + 1, :]
            o_ref[pl.ds(r0, rows), :] = acc
            return 0
        lax.fori_loop(0, t // rows, conv, 0)

    return pl.pallas_call(
        body, grid=(B_WIDTH // tc,),
        in_specs=[pl.BlockSpec((t, tc), lambda j: (0, a_cb + j)), pl.BlockSpec((t, tc), lambda j: (0, g_cb + j)),
                  pl.BlockSpec((CONV_WIDTH, tc), lambda j: (0, j)), pl.BlockSpec((1, tc), lambda j: (0, j))],
        out_specs=pl.BlockSpec((t, tc), lambda j: (0, j)), out_shape=S((t, B_WIDTH), F32),
        scratch_shapes=[pltpu.VMEM((t + CONV_PAD, tc), F32)],
        compiler_params=_cp("parallel"), name=name)(proj, proj, w, cb)


def conv_bwd(name, proj, dhc, w):
    t = proj.shape[0]
    tc = LANES
    rows = _tile(t, CONV_ROWS_BWD)
    a_cb, g_cb = 2 * A_WIDTH // tc, (2 * A_WIDTH + B_WIDTH) // tc
    win_rows = rows + CONV_PAD

    def body(a_ref, g_ref, d_ref, w_ref, da_ref, dg_ref, dw_ref, dcb_ref, hpad, dpad, dwacc):
        hpad[0:CONV_PAD, :] = jnp.zeros((CONV_PAD, tc), F32)
        dpad[t:t + CONV_PAD, :] = jnp.zeros((CONV_PAD, tc), F32)
        dwacc[...] = jnp.zeros(dwacc.shape, F32)

        def fill(i, _):
            r0 = pl.multiple_of(i * rows, rows)
            hpad[pl.ds(CONV_PAD + r0, rows), :] = a_ref[pl.ds(r0, rows), :] * _sigmoid(g_ref[pl.ds(r0, rows), :])
            dpad[pl.ds(r0, rows), :] = d_ref[pl.ds(r0, rows), :]
            return 0
        lax.fori_loop(0, t // rows, fill, 0)

        def step(i, dcb):
            r0 = pl.multiple_of(i * rows, rows)
            hwin = hpad[pl.ds(r0, win_rows), :]
            dwin = dpad[pl.ds(r0, win_rows), :]
            dchunk = dwin[:rows, :]
            dh = jnp.zeros((rows, tc), F32)
            for b in range(SUB):
                hb = hwin if b == 0 else pltpu.roll(hwin, b, 0)
                db = dwin if b == 0 else pltpu.roll(dwin, win_rows - b, 0)
                for a in range(CONV_PAD // SUB):
                    k = CONV_WIDTH - 1 - (SUB * a + b)
                    if k >= 0:
                        dh = dh + db[SUB * a:SUB * a + rows, :] * w_ref[k:k + 1, :]
                        lo = CONV_PAD - SUB * a
                        prod = dchunk * hb[lo:lo + rows, :]
                        dwacc[k] += jnp.sum(prod.reshape(rows // 8, 8, tc), axis=0)
            a = a_ref[pl.ds(r0, rows), :]
            sg = _sigmoid(g_ref[pl.ds(r0, rows), :])
            da_ref[pl.ds(r0, rows), :] = (dh * sg).astype(da_ref.dtype)
            dg_ref[pl.ds(r0, rows), :] = (dh * a * sg * (1.0 - sg)).astype(dg_ref.dtype)
            return dcb + jnp.sum(dchunk, axis=0, keepdims=True)
        dcb = lax.fori_loop(0, t // rows, step, jnp.zeros((1, tc), F32))
        dcb_ref[...] = dcb
        for k in range(CONV_WIDTH):
            dw_ref[k:k + 1, :] = jnp.sum(dwacc[k], axis=0, keepdims=True)

    return pl.pallas_call(
        body, grid=(B_WIDTH // tc,),
        in_specs=[pl.BlockSpec((t, tc), lambda j: (0, a_cb + j)), pl.BlockSpec((t, tc), lambda j: (0, g_cb + j)),
                  pl.BlockSpec((t, tc), lambda j: (0, j)), pl.BlockSpec((CONV_WIDTH, tc), lambda j: (0, j))],
        out_specs=[pl.BlockSpec((t, tc), lambda j: (0, j)), pl.BlockSpec((t, tc), lambda j: (0, j)),
                   pl.BlockSpec((CONV_WIDTH, tc), lambda j: (0, j)), pl.BlockSpec((1, tc), lambda j: (0, j))],
        out_shape=[S((t, B_WIDTH), BF16), S((t, B_WIDTH), BF16), S((CONV_WIDTH, B_WIDTH), F32), S((1, B_WIDTH), F32)],
        scratch_shapes=[pltpu.VMEM((t + CONV_PAD, tc), F32), pltpu.VMEM((t + CONV_PAD, tc), F32),
                        pltpu.VMEM((CONV_WIDTH, 8, tc), F32)],
        compiler_params=_cp("parallel"), name=name)(proj, proj, dhc, w)


def ln_silu_bwd(name, hc, dxb, w_out, g, b):
    c = hc.shape[1]

    def fn(h, dxv, wv, gv, bv):
        dout = lax.dot_general(dxv, wv[A_WIDTH:, :], _NT, preferred_element_type=F32)
        y, rstd = _ln_plain(h)
        z = y * gv + bv
        s = _sigmoid(z)
        dz = dout * s * (1.0 + z * (1.0 - s))
        dyv = dz * gv
        dh = rstd * (dyv - jnp.mean(dyv, axis=-1, keepdims=True) - y * jnp.mean(dyv * y, axis=-1, keepdims=True))
        return [dh], [jnp.sum(dz * y, axis=0, keepdims=True), jnp.sum(dz, axis=0, keepdims=True)]

    return rows_call(name, fn, [hc, dxb], [w_out, g, b], [(c, F32)], [(1, c), (1, c)])


_NT = (((1,), (1,)), ((), ()))
_TN = (((0,), (0,)), ((), ()))


def attn_fwd(name, x, gain, wq, k, v, wo, tm=512):
    t, d = x.shape
    m = k.shape[0]
    tm = _tile(t, tm)
    scale = CA_HEAD_DIM ** -0.5

    def body(x_ref, g_ref, wq_ref, k_ref, v_ref, wo_ref, x1_ref, xn_ref, r_ref, q_ref, o_ref):
        xv = x_ref[...]
        rv = lax.rsqrt(jnp.mean(xv * xv, axis=-1, keepdims=True) + EPS)
        xn = (xv * rv * g_ref[...]).astype(BF16)
        xn_ref[...] = xn
        r_ref[...] = rv
        q_ref[...] = jnp.dot(xn, wq_ref[...], preferred_element_type=F32).astype(BF16)
        for h in range(CA_HEADS):
            cs = slice(h * CA_HEAD_DIM, (h + 1) * CA_HEAD_DIM)
            s = lax.dot_general(q_ref[:, cs], k_ref[:, cs], _NT, preferred_element_type=F32) * scale
            e = jnp.exp(s - jnp.max(s, axis=-1, keepdims=True))
            p = e / jnp.sum(e, axis=-1, keepdims=True)
            o_ref[:, cs] = jnp.dot(_bf(p), v_ref[:, cs], preferred_element_type=F32).astype(o_ref.dtype)
        x1_ref[...] = xv + jnp.dot(o_ref[...], wo_ref[...], preferred_element_type=F32)

    def whole(a):
        return pl.BlockSpec(a.shape, lambda i: (0, 0), pipeline_mode=pl.Buffered(1))

    rows = pl.BlockSpec((tm, d), lambda i: (i, 0))
    col = pl.BlockSpec((tm, 1), lambda i: (i, 0))
    return pl.pallas_call(
        body, grid=(t // tm,),
        in_specs=[rows, whole(gain), whole(wq), whole(k), whole(v), whole(wo)],
        out_specs=[rows, rows, col, rows, rows],
        out_shape=[S((t, d), F32), S((t, d), BF16), S((t, 1), F32), S((t, d), BF16), S((t, d), BF16)],
        compiler_params=_cp("parallel"), name=name)(x, gain, wq, k, v, wo)


def attn_bwd(name, dx, dxb, x, r, gain, q, k, v, wq, wo, tm=512):
    t, d = q.shape
    m = k.shape[0]
    tm = _tile(t, tm)
    scale = CA_HEAD_DIM ** -0.5

    def body(dx_ref, dxb_ref, x_ref, r_ref, g_ref, q_ref, k_ref, v_ref, wq_ref, wo_ref,
             dxo_ref, dxbo_ref, dq_ref, dk_ref, dv_ref, dg_ref, do_s):
        @pl.when(pl.program_id(0) == 0)
        def _():
            dk_ref[...] = jnp.zeros(dk_ref.shape, F32)
            dv_ref[...] = jnp.zeros(dv_ref.shape, F32)
            dg_ref[...] = jnp.zeros(dg_ref.shape, F32)

        do_s[...] = lax.dot_general(dxb_ref[...], wo_ref[...], _NT, preferred_element_type=F32).astype(BF16)
        for h in range(CA_HEADS):
            cs = slice(h * CA_HEAD_DIM, (h + 1) * CA_HEAD_DIM)
            qh, kh, vh, doh = q_ref[:, cs], k_ref[:, cs], v_ref[:, cs], do_s[:, cs]
            s = lax.dot_general(qh, kh, _NT, preferred_element_type=F32) * scale
            e = jnp.exp(s - jnp.max(s, axis=-1, keepdims=True))
            p = e / jnp.sum(e, axis=-1, keepdims=True)
            pb = _bf(p)
            dv_ref[:, cs] += lax.dot_general(pb, doh, _TN, preferred_element_type=F32)
            dp = lax.dot_general(doh, vh, _NT, preferred_element_type=F32)
            ds = _bf(p * (dp - jnp.sum(dp * p, axis=-1, keepdims=True)) * scale)
            dq_ref[:, cs] = jnp.dot(ds, kh, preferred_element_type=F32).astype(dq_ref.dtype)
            dk_ref[:, cs] += lax.dot_general(ds, qh, _TN, preferred_element_type=F32)
        dxn = lax.dot_general(dq_ref[...], wq_ref[...], _NT, preferred_element_type=F32)
        xh = x_ref[...] * r_ref[...]
        wv = dxn * g_ref[...]
        dxo = dx_ref[...] + r_ref[...] * (wv - xh * jnp.mean(wv * xh, axis=-1, keepdims=True))
        dxo_ref[...] = dxo
        dxbo_ref[...] = dxo.astype(BF16)
        dg_ref[...] += jnp.sum(dxn * xh, axis=0, keepdims=True)

    def whole(a):
        return pl.BlockSpec(a.shape, lambda i: (0, 0), pipeline_mode=pl.Buffered(1))

    rows = pl.BlockSpec((tm, d), lambda i: (i, 0))
    col = pl.BlockSpec((tm, 1), lambda i: (i, 0))
    acc = pl.BlockSpec((m, d), lambda i: (0, 0))
    return pl.pallas_call(
        body, grid=(t // tm,),
        in_specs=[rows, rows, rows, col, whole(gain), rows, whole(k), whole(v), whole(wq), whole(wo)],
        out_specs=[rows, rows, rows, acc, acc, pl.BlockSpec((1, d), lambda i: (0, 0))],
        out_shape=[S((t, d), F32), S((t, d), BF16), S((t, d), BF16), S((m, d), F32), S((m, d), F32), S((1, d), F32)],
        scratch_shapes=[pltpu.VMEM((tm, d), BF16)],
        compiler_params=_cp("arbitrary"), name=name)(dx, dxb, x, r, gain, q, k, v, wq, wo)


SUB = 8
S5_ROWS = 256


S5_BLOCKS = 4
BLOCK_CH = C_WIDTH // S5_BLOCKS
BLOCK_ST = N_STATE // S5_BLOCKS
_S5_BLOCKS = tuple((slice(BLOCK_CH * q, BLOCK_CH * (q + 1)), slice(BLOCK_ST * q, BLOCK_ST * (q + 1)),
                    slice(N_STATE + BLOCK_ST * q, N_STATE + BLOCK_ST * (q + 1))) for q in range(S5_BLOCKS))
_HI = lax.Precision.HIGHEST
_GP = (C_GROUPS, C_STATE)
_RP = (C_WIDTH, C_STATE)


def _zoh(lr, li, ldt):
    dt = jnp.exp(ldt)
    mag = jnp.exp(lr * dt)
    ar = mag * jnp.cos(li * dt)
    ai = mag * jnp.sin(li * dt)
    den = lr * lr + li * li
    qr = ((ar - 1.0) * lr + ai * li) / den
    qi = (ai * lr - (ar - 1.0) * li) / den
    return dt, ar, ai, den, qr, qi


def _per_channel(v):
    return jnp.broadcast_to(v[:, None, :], (C_GROUPS, C_GROUP_CH, C_STATE)).reshape(_RP)


def _same_group(shape, row_per_group, col_per_group):
    rows = lax.broadcasted_iota(jnp.int32, shape, 0) // row_per_group
    cols = lax.broadcasted_iota(jnp.int32, shape, 1) // col_per_group
    return rows == cols


def _spread(shape, axis):
    long = lax.broadcasted_iota(jnp.int32, shape, axis) % C_STATE
    short = lax.broadcasted_iota(jnp.int32, shape, 1 - axis)
    return long == short


def s5_discretise(name, lam_re, lam_im, log_dt, bt_re, bt_im):
    def body(lr_ref, li_ref, ldt_ref, btr_ref, bti_ref, a_ref, bbr_ref, bbi_ref):
        _, ar, ai, _, qr, qi = _zoh(lr_ref[...], li_ref[...], ldt_ref[...])
        a_ref[0] = ar
        a_ref[1] = ai
        q2r, q2i = _per_channel(qr), _per_channel(qi)
        btr, bti = btr_ref[...], bti_ref[...]
        bbr_ref[...] = q2r * btr - q2i * bti
        bbi_ref[...] = q2r * bti + q2i * btr

    return pl.pallas_call(body, out_shape=[S((2,) + _GP, F32), S(_RP, F32), S(_RP, F32)],
                          name=name)(lam_re, lam_im, log_dt, bt_re, bt_im)


def s5_operands(name, a, bbr, bbi, c2r, c2i, ctr, cti):
    ns = N_STATE

    def body(a_ref, bbr_ref, bbi_ref, c2r_ref, c2i_ref, ctr_ref, cti_ref, pw_ref, qw_ref, mb_ref, mc_ref, mct_ref):
        ar, ai = a_ref[0:1, :], a_ref[1:2, :]
        pows = [(ar, ai)]
        for _ in range(SUB - 1):
            pr, pi = pows[-1]
            pows.append((pr * ar - pi * ai, pr * ai + pi * ar))
        rows = lax.broadcasted_iota(jnp.int32, (SUB, ns), 0)

        def rows_of(v):
            return jnp.broadcast_to(v, (SUB, ns))

        for k, s in enumerate((1, 2, 4)):
            pr, pi = rows_of(pows[s - 1][0]), rows_of(pows[s - 1][1])
            pw_ref[k, 0] = jnp.where(rows >= s, pr, 0.0)
            pw_ref[k, 1] = jnp.where(rows >= s, pi, 0.0)
            qw_ref[k, 0] = jnp.where(rows + s <= SUB - 1, pr, 0.0)
            qw_ref[k, 1] = jnp.where(rows + s <= SUB - 1, -pi, 0.0)
        fr = fi = br = bi = jnp.zeros((SUB, ns), F32)
        for i in range(SUB):
            fr = jnp.where(rows == i, rows_of(pows[i][0]), fr)
            fi = jnp.where(rows == i, rows_of(pows[i][1]), fi)
            br = jnp.where(rows == i, rows_of(pows[SUB - 1 - i][0]), br)
            bi = jnp.where(rows == i, rows_of(-pows[SUB - 1 - i][1]), bi)
        pw_ref[3, 0], pw_ref[3, 1], qw_ref[3, 0], qw_ref[3, 1] = fr, fi, br, bi

        wide = _spread((C_STATE, ns), 1).astype(BF16)
        tall = _spread((ns, C_STATE), 0).astype(BF16)
        in_rows = _same_group((C_WIDTH, ns), C_GROUP_CH, C_STATE)
        in_cols = _same_group((ns, C_WIDTH), C_STATE, C_GROUP_CH)

        def across(v, sign=1.0):
            return jnp.where(in_rows, sign * jnp.dot(_bf(v), wide, preferred_element_type=F32), 0.0).astype(BF16)

        def down(vt, sign=1.0):
            return jnp.where(in_cols, sign * jnp.dot(tall, _bf(vt), preferred_element_type=F32), 0.0).astype(BF16)

        mb_ref[:, 0:ns] = across(bbr_ref[...])
        mb_ref[:, ns:2 * ns] = across(bbi_ref[...])
        mct_ref[:, 0:ns] = across(c2r_ref[...])
        mct_ref[:, ns:2 * ns] = across(c2i_ref[...], -1.0)
        mc_ref[0:ns, :] = down(ctr_ref[...])
        mc_ref[ns:2 * ns, :] = down(cti_ref[...], -1.0)

    return pl.pallas_call(
        body, out_shape=[S((4, 2, SUB, ns), F32), S((4, 2, SUB, ns), F32), S((C_WIDTH, 2 * ns), BF16),
                         S((2 * ns, C_WIDTH), BF16), S((C_WIDTH, 2 * ns), BF16)],
        compiler_params=pltpu.CompilerParams(vmem_limit_bytes=VMEM_LIMIT), name=name)(a, bbr, bbi, c2r, c2i, ctr, cti)


def s5_block_grads(name, u, lamb, xsb, dyb):
    t = u.shape[0]

    def mb_body(u_ref, lr_ref, li_ref, o_ref):
        ub = _bf(u_ref[...])
        o_ref[:, 0:BLOCK_ST] = lax.dot_general(ub, lr_ref[...], _TN, preferred_element_type=F32)
        o_ref[:, BLOCK_ST:2 * BLOCK_ST] = lax.dot_general(ub, li_ref[...], _TN, preferred_element_type=F32)

    d_mb = pl.pallas_call(
        mb_body, grid=(S5_BLOCKS,),
        in_specs=[pl.BlockSpec((t, BLOCK_CH), lambda q: (0, q)), pl.BlockSpec((t, BLOCK_ST), lambda q: (0, q)),
                  pl.BlockSpec((t, BLOCK_ST), lambda q: (0, S5_BLOCKS + q))],
        out_specs=pl.BlockSpec((BLOCK_CH, 2 * BLOCK_ST), lambda q: (q, 0)), out_shape=S((C_WIDTH, 2 * BLOCK_ST), F32),
        compiler_params=_cp("parallel"), name=name + "_b")(u, lamb, lamb)

    def mc_body(x_ref, dy_ref, o_ref):
        o_ref[...] = lax.dot_general(x_ref[...], dy_ref[...], _TN, preferred_element_type=F32)

    d_mc = pl.pallas_call(
        mc_body, grid=(2, S5_BLOCKS),
        in_specs=[pl.BlockSpec((t, BLOCK_ST), lambda p, q: (0, p * S5_BLOCKS + q)), pl.BlockSpec((t, BLOCK_CH), lambda p, q: (0, q))],
        out_specs=pl.BlockSpec((BLOCK_ST, BLOCK_CH), lambda p, q: (p * S5_BLOCKS + q, 0)),
        out_shape=S((2 * N_STATE, BLOCK_CH), F32), compiler_params=_cp("parallel", "parallel"), name=name + "_c")(xsb, dyb)
    return d_mb, d_mc


def s5_param_grads(name, d_mb, d_mc, da, lam_re, lam_im, log_dt, bt_re, bt_im):
    ns = N_STATE

    def body(dmb_ref, dmc_ref, da_ref, lr_ref, li_ref, ldt_ref, btr_ref, bti_ref,
             glr_ref, gli_ref, gdt_ref, gbr_ref, gbi_ref, gcr_ref, gci_ref):
        lr, li = lr_ref[...], li_ref[...]
        dt, ar, ai, den, qr, qi = _zoh(lr, li, ldt_ref[...])
        per_block = C_GROUPS // S5_BLOCKS
        wide = _spread((C_STATE, BLOCK_ST), 1).astype(F32)
        tall = _spread((BLOCK_ST, C_STATE), 0).astype(F32)
        rows = lax.broadcasted_iota(jnp.int32, (C_WIDTH, BLOCK_ST), 0) // C_GROUP_CH % per_block
        in_rows = rows == lax.broadcasted_iota(jnp.int32, (C_WIDTH, BLOCK_ST), 1) // C_STATE
        in_cols = _same_group((BLOCK_ST, BLOCK_CH), C_STATE, C_GROUP_CH)

        def fold_rows(v):
            return lax.dot_general(jnp.where(in_rows, v, 0.0), wide, (((1,), (1,)), ((), ())), precision=_HI,
                                   preferred_element_type=F32)

        def fold_cols(v):
            return lax.dot_general(jnp.where(in_cols, v, 0.0), tall, (((0,), (0,)), ((), ())), precision=_HI,
                                   preferred_element_type=F32)

        for cs, s_re, s_im in _S5_BLOCKS:
            gcr_ref[cs, :] = fold_cols(dmc_ref[s_re, :])
            gci_ref[cs, :] = -fold_cols(dmc_ref[s_im, :])
        gbbr = fold_rows(dmb_ref[:, 0:BLOCK_ST])
        gbbi = fold_rows(dmb_ref[:, BLOCK_ST:2 * BLOCK_ST])
        btr, bti = btr_ref[...], bti_ref[...]
        q2r, q2i = _per_channel(qr), _per_channel(qi)
        gbr_ref[...] = q2r * gbbr + q2i * gbbi
        gbi_ref[...] = q2r * gbbi - q2i * gbbr

        def per_group(v):
            return jnp.sum(v.reshape(C_GROUPS, C_GROUP_CH, C_STATE), axis=1)

        gqr = per_group(btr * gbbr + bti * gbbi)
        gqi = per_group(btr * gbbi - bti * gbbr)
        ilr, ili = lr / den, li / den
        gar = da_ref[0] + ilr * gqr - ili * gqi
        gai = da_ref[1] + ilr * gqi + ili * gqr
        sr = (qr * lr + qi * li) / den
        si = (qi * lr - qr * li) / den
        gzr = ar * gar + ai * gai
        gzi = ar * gai - ai * gar
        glr_ref[...] = -sr * gqr - si * gqi + dt * gzr
        gli_ref[...] = -sr * gqi + si * gqr + dt * gzi
        gdt_ref[...] = jnp.sum(lr * gzr + li * gzi, axis=1, keepdims=True) * dt

    return pl.pallas_call(
        body, out_shape=[S(_GP, F32), S(_GP, F32), S((C_GROUPS, 1), F32), S(_RP, F32), S(_RP, F32), S(_RP, F32), S(_RP, F32)],
        compiler_params=pltpu.CompilerParams(vmem_limit_bytes=VMEM_LIMIT), name=name,
    )(d_mb, d_mc, da, lam_re, lam_im, log_dt, bt_re, bt_im)


def _cmul_add(xr, xi, pr, pi, zr, zi):
    return xr + pr * zr - pi * zi, xi + pr * zi + pi * zr


def s5_fwd(name, x, gain, w_in, mb, mc, pw, dskip, w_out_t):
    t, d = x.shape
    tm = _tile(t, S5_ROWS)
    ns = N_STATE

    def body(x_ref, g_ref, wi_ref, mb_ref, mc_ref, pw_ref, d_ref, wo_ref,
             x1_ref, hn_ref, r_ref, u_ref, gy_ref, y_ref, xs_ref, xb_ref, o1_ref, o2_ref, carry):
        @pl.when(pl.program_id(0) == 0)
        def _():
            carry[...] = jnp.zeros(carry.shape, F32)

        xv = x_ref[...]
        rv = lax.rsqrt(jnp.mean(xv * xv, axis=-1, keepdims=True) + EPS)
        hn = (xv * rv * g_ref[...]).astype(BF16)
        hn_ref[...] = hn
        r_ref[...] = rv
        uv = jnp.dot(hn, wi_ref[...], preferred_element_type=F32)
        u_ref[...] = uv
        ub = _bf(uv)
        for cs, s_re, s_im in _S5_BLOCKS:
            xs_ref[:, s_re] = jnp.dot(ub[:, cs], mb_ref[cs, s_re], preferred_element_type=F32)
            xs_ref[:, s_im] = jnp.dot(ub[:, cs], mb_ref[cs, s_im], preferred_element_type=F32)

        def group(i, _):
            r0 = pl.multiple_of(i * SUB, SUB)
            xr = xs_ref[pl.ds(r0, SUB), 0:ns]
            xi = xs_ref[pl.ds(r0, SUB), ns:2 * ns]
            for k, s in enumerate((1, 2, 4)):
                xr, xi = _cmul_add(xr, xi, pw_ref[k, 0], pw_ref[k, 1], pltpu.roll(xr, s, 0), pltpu.roll(xi, s, 0))
            xr, xi = _cmul_add(xr, xi, pw_ref[3, 0], pw_ref[3, 1], carry[0], carry[1])
            xs_ref[pl.ds(r0, SUB), 0:ns] = xr
            xs_ref[pl.ds(r0, SUB), ns:2 * ns] = xi
            carry[0] = jnp.broadcast_to(xr[SUB - 1:SUB, :], (SUB, ns))
            carry[1] = jnp.broadcast_to(xi[SUB - 1:SUB, :], (SUB, ns))
            return 0
        lax.fori_loop(0, tm // SUB, group, 0)

        xb_ref[...] = _bf(xs_ref[...])
        for cs, s_re, s_im in _S5_BLOCKS:
            y = (jnp.dot(xb_ref[:, s_re], mc_ref[s_re, cs], preferred_element_type=F32)
                 + jnp.dot(xb_ref[:, s_im], mc_ref[s_im, cs], preferred_element_type=F32) + d_ref[:, cs] * uv[:, cs])
            y_ref[:, cs] = y
            gy_ref[:, cs] = _gelu(y).astype(gy_ref.dtype)
        o1 = lax.dot_general(gy_ref[...], wo_ref[0:d, :], _NT, preferred_element_type=F32)
        o2 = lax.dot_general(gy_ref[...], wo_ref[d:2 * d, :], _NT, preferred_element_type=F32)
        o1_ref[...] = o1.astype(BF16)
        o2_ref[...] = o2.astype(BF16)
        x1_ref[...] = xv + o1 * _sigmoid(o2)

    c = w_in.shape[1]
    rows = pl.BlockSpec((tm, d), lambda i: (i, 0))
    narrow = pl.BlockSpec((tm, c), lambda i: (i, 0))
    states = pl.BlockSpec((tm, 2 * ns), lambda i: (i, 0))
    return pl.pallas_call(
        body, grid=(t // tm,),
        in_specs=[rows, _whole(gain), _whole(w_in), _whole(mb), _whole(mc), _whole(pw), _whole(dskip), _whole(w_out_t)],
        out_specs=[rows, rows, pl.BlockSpec((tm, 1), lambda i: (i, 0)), narrow, narrow, narrow, states, states, rows, rows],
        out_shape=[S((t, d), F32), S((t, d), BF16), S((t, 1), F32), S((t, c), F32), S((t, c), BF16), S((t, c), F32),
                   S((t, 2 * ns), F32), S((t, 2 * ns), BF16), S((t, d), BF16), S((t, d), BF16)],
        scratch_shapes=[pltpu.VMEM((2, SUB, ns), F32)],
        compiler_params=_cp("arbitrary"), name=name)(x, gain, w_in, mb, mc, pw, dskip, w_out_t)


def s5_bwd(name, dgy, y, u, xs, mct, mbt, qw, dskip):
    t, c = u.shape
    tm = _tile(t, S5_ROWS)
    nt = t // tm
    ns = N_STATE
    ng = tm // SUB

    def body(dgy_ref, y_ref, u_ref, xs_ref, mct_ref, mbt_ref, qw_ref, d_ref,
             du_ref, dy_ref, lb_ref, da_ref, dd_ref, lam, carry):
        @pl.when(pl.program_id(0) == 0)
        def _():
            carry[...] = jnp.zeros(carry.shape, F32)
            da_ref[...] = jnp.zeros(da_ref.shape, F32)
            dd_ref[...] = jnp.zeros(dd_ref.shape, F32)

        uv = u_ref[...]
        dy = dgy_ref[...] * _gelu_grad(y_ref[...])
        dyb = _bf(dy)
        dy_ref[...] = dyb
        dd_ref[...] += jnp.sum(dy * uv, axis=0, keepdims=True)
        for cs, s_re, s_im in _S5_BLOCKS:
            lam[:, s_re] = jnp.dot(dyb[:, cs], mct_ref[cs, s_re], preferred_element_type=F32)
            lam[:, s_im] = jnp.dot(dyb[:, cs], mct_ref[cs, s_im], preferred_element_type=F32)
        last_row = lax.broadcasted_iota(jnp.int32, (SUB, ns), 0) == SUB - 1

        def group(j, _):
            i = ng - 1 - j
            r0 = pl.multiple_of(i * SUB, SUB)
            lr = lam[pl.ds(r0, SUB), 0:ns]
            li = lam[pl.ds(r0, SUB), ns:2 * ns]
            for k, s in enumerate((1, 2, 4)):
                lr, li = _cmul_add(lr, li, qw_ref[k, 0], qw_ref[k, 1],
                                   pltpu.roll(lr, SUB - s, 0), pltpu.roll(li, SUB - s, 0))
            cr, ci = carry[0], carry[1]
            lr, li = _cmul_add(lr, li, qw_ref[3, 0], qw_ref[3, 1], cr, ci)
            lam[pl.ds(r0, SUB), 0:ns] = lr
            lam[pl.ds(r0, SUB), ns:2 * ns] = li
            carry[0] = jnp.broadcast_to(lr[0:1, :], (SUB, ns))
            carry[1] = jnp.broadcast_to(li[0:1, :], (SUB, ns))
            nr = jnp.where(last_row, cr, pltpu.roll(lr, SUB - 1, 0))
            ni = jnp.where(last_row, ci, pltpu.roll(li, SUB - 1, 0))
            xr = xs_ref[pl.ds(r0, SUB), 0:ns]
            xi = xs_ref[pl.ds(r0, SUB), ns:2 * ns]
            da_ref[0] += nr * xr + ni * xi
            da_ref[1] += ni * xr - nr * xi
            return 0
        lax.fori_loop(0, ng, group, 0)

        lb_ref[...] = _bf(lam[...])
        for cs, s_re, s_im in _S5_BLOCKS:
            du = (jnp.dot(lb_ref[:, s_re], mbt_ref[s_re, cs], preferred_element_type=F32)
                  + jnp.dot(lb_ref[:, s_im], mbt_ref[s_im, cs], preferred_element_type=F32) + d_ref[:, cs] * dy[:, cs])
            du_ref[:, cs] = du.astype(du_ref.dtype)

    rev = lambda i: (nt - 1 - i, 0)
    return pl.pallas_call(
        body, grid=(nt,),
        in_specs=[pl.BlockSpec((tm, c), rev), pl.BlockSpec((tm, c), rev), pl.BlockSpec((tm, c), rev),
                  pl.BlockSpec((tm, 2 * ns), rev),
                  pl.BlockSpec(mct.shape, lambda i: (0, 0)), pl.BlockSpec(mbt.shape, lambda i: (0, 0)),
                  pl.BlockSpec(qw.shape, lambda i: (0, 0, 0, 0)), pl.BlockSpec((1, c), lambda i: (0, 0))],
        out_specs=[pl.BlockSpec((tm, c), rev), pl.BlockSpec((tm, c), rev), pl.BlockSpec((tm, 2 * ns), rev),
                   pl.BlockSpec((2, SUB, ns), lambda i: (0, 0, 0)), pl.BlockSpec((1, c), lambda i: (0, 0))],
        out_shape=[S((t, c), BF16), S((t, c), BF16), S((t, 2 * ns), BF16), S((2, SUB, ns), F32), S((1, c), F32)],
        scratch_shapes=[pltpu.VMEM((tm, 2 * ns), F32), pltpu.VMEM((2, SUB, ns), F32)],
        compiler_params=_cp("arbitrary"), name=name)(dgy, y, u, xs, mct, mbt, qw, dskip)


def _first(accs, *_):
    return [accs[0]]


def _rms_bwd_epi(accs, xv, base, rv, g):
    dv = accs[0]
    w = dv * g
    xh = xv * rv
    dx = base + rv * (w - xh * jnp.mean(w * xh, axis=-1, keepdims=True))
    return [dx, dx, jnp.sum(dv * xh, axis=0, keepdims=True)]


def mm_rms_bwd(name, pairs, x, r, gain, dres):
    t, d = x.shape
    return mm_nn(name, t, d, pairs, 1, _rms_bwd_epi, [F32, BF16], tiled=[x, dres], cols=[r], rowv=[gain], sums=[(1, d)])


def even_fwd(x, w, need_out):
    t = x.shape[0]
    proj, hn, r = mm_nn("e_in_f", t, IN_WIDTH, [(x, w["e_w_in_t"], 0, "t")], 1, _first, [F32], norm_gain=w["e_norm"])
    hc = conv_fwd("e_conv_f", proj, w["e_conv_w"], w["e_conv_b"])
    need_out(hc)
    x1, out_a, out_b = even_out_fwd("e_out_f", proj, hc, x, w["e_gmlp_w"], w["e_gmlp_b"], w["e_conv_ln_g"], w["e_conv_ln_b"],
                                    w["e_w_out"])
    return x1, (x, hn, r, proj, out_a, hc, out_b)


def even_bwd_mixers(dxb, saved, w):
    x, hn, r, proj, out_a, hc, out_b = saved
    t = x.shape[0]
    g_w_out = jnp.concatenate([mm_tn("e_out_wa", out_a, dxb), mm_tn("e_out_wb", out_b, dxb)], axis=0)
    dab, g_gw, g_gb = gmlp_bwd("e_gmlp_b", proj, dxb, w["e_w_out"], w["e_gmlp_w"], w["e_gmlp_b"])
    dhc, g_lg, g_lb = ln_silu_bwd("e_ln_b", hc, dxb, w["e_w_out"], w["e_conv_ln_g"], w["e_conv_ln_b"])
    dba, dbg, g_cw, g_cb = conv_bwd("e_conv_b", proj, dhc, w["e_conv_w"])
    g_w_in_t = jnp.concatenate([mm_tn("e_in_w0", dab, hn), mm_tn("e_in_w1", dba, hn), mm_tn("e_in_w2", dbg, hn)], axis=0)
    grads = dict(e_w_in_t=g_w_in_t, e_gmlp_w=g_gw[None], e_gmlp_b=g_gb.reshape(1, A_GROUPS, GMLP_BLOCK),
                 e_conv_w=g_cw[None], e_conv_b=g_cb, e_conv_ln_g=g_lg, e_conv_ln_b=g_lb, e_w_out=g_w_out)
    return (dab, dba, dbg), grads


def even_bwd_input(dx, dproj, saved, w):
    x, _, r = saved[:3]
    dab, dba, dbg = dproj
    w_in_t = w["e_w_in_t"]
    return mm_rms_bwd("e_in_b", [(dab, (w_in_t, 0), 0), (dba, (w_in_t, 2), 0), (dbg, (w_in_t, 3), 0)], x, r, w["e_norm"], dx)


def s5_setup(w, anchor=None):
    def rows(v):
        return v.transpose(0, 2, 1).reshape(_RP)

    log_dt = w["o_log_dt"].reshape(C_GROUPS, 1)
    if anchor is not None:
        log_dt = log_dt + anchor
    lam = (w["o_lam_re"], w["o_lam_im"], log_dt, rows(w["o_b_re"]), rows(w["o_b_im"]))
    a, bbr, bbi = s5_discretise("o_s5_zoh", *lam)
    c_re, c_im = w["o_c_re"], w["o_c_im"]
    pw, qw, mb, mc, mct = s5_operands("o_s5_ops", a.reshape(2, N_STATE), bbr, bbi, c_re.reshape(_RP), c_im.reshape(_RP),
                                      c_re.transpose(2, 0, 1).reshape(C_STATE, C_WIDTH),
                                      c_im.transpose(2, 0, 1).reshape(C_STATE, C_WIDTH))
    return dict(lam=lam, pw=pw, qw=qw, mb=mb, mc=mc, mct=mct, mbt=mb.T)


def odd_fwd(x, w, consts):
    x1, hn, r, u, gy, y, xs, xsb, o1, o2 = s5_fwd("o_s5_f", x, w["o_norm"], w["o_w_in"], consts["mb"], consts["mc"],
                                                  consts["pw"], w["o_d"], w["o_w_out_t"])
    return x1, (x, hn, r, u, gy, y, xs, xsb, o1, o2)


def odd_bwd(dx, dxb, saved, w, consts):
    x, hn, r, u, gy, y, xs, xsb, o1, o2 = saved
    t = x.shape[0]

    def gate_bwd(dv, a, b, wv):
        a = a.astype(F32)
        sg = _sigmoid(b.astype(F32))
        do12 = jnp.concatenate([dv * sg, dv * a * sg * (1.0 - sg)], axis=1).astype(BF16)
        return [do12, jnp.dot(do12, wv, preferred_element_type=F32)], []

    do12, dgy = rows_call("o_out_b", gate_bwd, [dx, o1, o2], [w["o_w_out_t"]], [(2 * D_MODEL, BF16), (C_WIDTH, F32)], [])
    g_w_out_t = mm_tn("o_out_w", do12, gy)
    du, dyb, lamb, da8, g_d = s5_bwd("o_s5_b", dgy, y, u, xs, consts["mct"], consts["mbt"], consts["qw"], w["o_d"])
    d_mb, d_mc = s5_block_grads("o_s5_w", u, lamb, xsb, dyb)
    da = jnp.sum(da8, axis=1).reshape((2,) + _GP)
    g_lr, g_li, g_dt, g_btr, g_bti, g_cr, g_ci = s5_param_grads("o_s5_pg", d_mb, d_mc, da, *consts["lam"])

    def states_first(v):
        return v.reshape(C_GROUPS, C_GROUP_CH, C_STATE).transpose(0, 2, 1)[None]

    g_w_in = mm_tn("o_in_w", hn, du)
    dx0, dx0b, g_norm = mm_rms_bwd("o_in_b", [(du, w["o_w_in"], 0, "t")], x, r, w["o_norm"], dx)
    grads = dict(o_norm=g_norm, o_w_in=g_w_in, o_lam_re=g_lr[None], o_lam_im=g_li[None], o_log_dt=g_dt.reshape(1, C_GROUPS),
                 o_b_re=states_first(g_btr), o_b_im=states_first(g_bti),
                 o_c_re=g_cr.reshape((1, C_GROUPS, C_GROUP_CH, C_STATE)), o_c_im=g_ci.reshape((1, C_GROUPS, C_GROUP_CH, C_STATE)),
                 o_d=g_d, o_w_out_t=g_w_out_t)
    return dx0, dx0b, grads


def ca_fwd(i, x, mem, w):
    t, m = x.shape[0], mem.shape[0]
    k, v, mn, rm = mm_nn(f"ca{i}_kv_f", m, D_MODEL, [(mem, w["ca_wk"][i], 0), (mem, w["ca_wv"][i], 1)], 2,
                         lambda accs: [accs[0], accs[1]], [BF16, BF16], norm_gain=w["ca_mem_norm"][i:i + 1])
    x1, xn, r, q, o = attn_fwd(f"ca{i}_attn_f", x, w["ca_norm"][i:i + 1], w["ca_wq"][i], k, v, w["ca_wo"][i])
    return x1, (x, xn, r, mn, rm, q, k, v, o)


def ca_bwd(i, dx, dxb, saved, mem, w):
    x, xn, r, mn, rm, q, k, v, o = saved
    t, m = x.shape[0], mem.shape[0]
    g_wo = mm_tn(f"ca{i}_o_w", o, dxb)
    dx0, dx0b, dq, dk, dv, g_norm = attn_bwd(f"ca{i}_attn_b", dx, dxb, x, r, w["ca_norm"][i:i + 1], q, k, v,
                                             w["ca_wq"][i], w["ca_wo"][i])
    g_wq = mm_tn(f"ca{i}_q_w", xn, dq)
    g_wk = mm_tn(f"ca{i}_k_w", mn, dk)
    g_wv = mm_tn(f"ca{i}_v_w", mn, dv)
    (dmn,) = mm_nn(f"ca{i}_kv_b", m, D_MODEL, [(dk, w["ca_wk"][i], 0, "t"), (dv, w["ca_wv"][i], 0, "t")], 1, _first, [F32])
    g_mnorm = rms_bwd_gain_only(f"ca{i}_mnorm_b", dmn, mem, rm)
    return dx0, dx0b, dict(ca_norm=g_norm, ca_mem_norm=g_mnorm, ca_wq=g_wq, ca_wk=g_wk, ca_wv=g_wv, ca_wo=g_wo)


FFN_ROWS = 512
FFN_CHUNK = 256


def _whole(a):
    return pl.BlockSpec(a.shape, lambda i: (0,) * a.ndim, pipeline_mode=pl.Buffered(1))


def ffn_fused_fwd(name, x, gain, wg_t, wu_t, wd, target=None, final_gain=None):
    t, d = x.shape
    hid = wd.shape[0]
    tm = _tile(t, FFN_ROWS)
    last = target is not None
    n_main = 4 if last else 1

    def body(*refs):
        x_ref, g_ref, wg_ref, wu_ref, wd_ref = refs[:5]
        rest = refs[5:]
        if last:
            tgt_ref, fg_ref = rest[:2]
            rest = rest[2:]
        main, (xn_ref, r_ref, dgate_ref, dup_ref, h_ref) = rest[:n_main], rest[n_main:]
        xv = x_ref[...]
        rv = lax.rsqrt(jnp.mean(xv * xv, axis=-1, keepdims=True) + EPS)
        xn = (xv * rv * g_ref[...]).astype(BF16)
        xn_ref[...] = xn
        r_ref[...] = rv
        for j in range(hid // FFN_CHUNK):
            cs = slice(j * FFN_CHUNK, (j + 1) * FFN_CHUNK)
            g = lax.dot_general(xn, wg_ref[cs, :], _NT, preferred_element_type=F32)
            u = lax.dot_general(xn, wu_ref[cs, :], _NT, preferred_element_type=F32)
            s = _sigmoid(g)
            silu = g * s
            dgate_ref[:, cs] = (u * (s + silu * (1.0 - s))).astype(BF16)
            dup_ref[:, cs] = silu.astype(BF16)
            h_ref[:, cs] = (silu * u).astype(BF16)
        acc = jnp.dot(h_ref[...], wd_ref[...], preferred_element_type=F32)
        if not last:
            main[0][...] = xv + acc
        else:
            dx, _, dgain, part = _final_loss_epi([acc], xv, tgt_ref[...], fg_ref[...])

            @pl.when(pl.program_id(0) == 0)
            def _():
                main[2][...] = jnp.zeros(main[2].shape, F32)
                main[3][...] = jnp.zeros(main[3].shape, F32)
            main[0][...] = dx
            main[1][...] = dx.astype(BF16)
            main[2][...] += dgain
            main[3][...] += part

    rows = pl.BlockSpec((tm, d), lambda i: (i, 0))
    wide = pl.BlockSpec((tm, hid), lambda i: (i, 0))
    col = pl.BlockSpec((tm, 1), lambda i: (i, 0))
    ins, in_specs = [x, gain, wg_t, wu_t, wd], [rows, _whole(gain), _whole(wg_t), _whole(wu_t), _whole(wd)]
    if last:
        ins += [target, final_gain]
        in_specs += [rows, _whole(final_gain)]
        out_specs = [rows, rows, pl.BlockSpec((1, d), lambda i: (0, 0)), pl.BlockSpec((1, 1), lambda i: (0, 0))]
        out_shape = [S((t, d), F32), S((t, d), BF16), S((1, d), F32), S((1, 1), F32)]
    else:
        out_specs, out_shape = [rows], [S((t, d), F32)]
    out_specs += [rows, col, wide, wide, wide]
    out_shape += [S((t, d), BF16), S((t, 1), F32)] + [S((t, hid), BF16)] * 3
    outs = pl.pallas_call(body, grid=(t // tm,), in_specs=in_specs, out_specs=out_specs, out_shape=out_shape,
                          compiler_params=_cp("arbitrary" if last else "parallel"), name=name)(*ins)
    return (tuple(outs[:4]) if last else outs[0]), outs[n_main:]


def ffn_fused_bwd(name, dx, dxb, x, r, gain, dgate, dup, wg_t, wu_t, wd):
    t, d = x.shape
    hid = wd.shape[0]
    tm = _tile(t, FFN_ROWS // 2)

    def body(dx_ref, dxb_ref, x_ref, r_ref, g_ref, dgate_ref, dup_ref, wg_ref, wu_ref, wd_ref,
             dxo_ref, dxbo_ref, dg_ref, du_ref, dgain_ref):
        @pl.when(pl.program_id(0) == 0)
        def _():
            dgain_ref[...] = jnp.zeros(dgain_ref.shape, F32)

        dxb = dxb_ref[...]
        for j in range(hid // FFN_CHUNK):
            cs = slice(j * FFN_CHUNK, (j + 1) * FFN_CHUNK)
            dh = lax.dot_general(dxb, wd_ref[cs, :], _NT, preferred_element_type=F32)
            dg_ref[:, cs] = (dh * dgate_ref[:, cs].astype(F32)).astype(BF16)
            du_ref[:, cs] = (dh * dup_ref[:, cs].astype(F32)).astype(BF16)
        dxn = (jnp.dot(dg_ref[...], wg_ref[...], preferred_element_type=F32)
               + jnp.dot(du_ref[...], wu_ref[...], preferred_element_type=F32))
        dxo, _, dgain = _rms_bwd_epi([dxn], x_ref[...], dx_ref[...], r_ref[...], g_ref[...])
        dxo_ref[...] = dxo
        dxbo_ref[...] = dxo.astype(BF16)
        dgain_ref[...] += dgain

    rows = pl.BlockSpec((tm, d), lambda i: (i, 0))
    wide = pl.BlockSpec((tm, hid), lambda i: (i, 0))
    col = pl.BlockSpec((tm, 1), lambda i: (i, 0))
    return pl.pallas_call(
        body, grid=(t // tm,),
        in_specs=[rows, rows, rows, col, _whole(gain), wide, wide, _whole(wg_t), _whole(wu_t), _whole(wd)],
        out_specs=[rows, rows, wide, wide, pl.BlockSpec((1, d), lambda i: (0, 0))],
        out_shape=[S((t, d), F32), S((t, d), BF16), S((t, hid), BF16), S((t, hid), BF16), S((1, d), F32)],
        compiler_params=_cp("arbitrary"), name=name)(dx, dxb, x, r, gain, dgate, dup, wg_t, wu_t, wd)


def ffn_fwd(i, x, w, target=None):
    out, (xn, r, dgate, dup, h) = ffn_fused_fwd(f"ffn{i}_f", x, w["ffn_norm"][i:i + 1], w["ffn_w_gate_t"][i],
                                                w["ffn_w_up_t"][i], w["ffn_w_down"][i], target,
                                                None if target is None else w["final_norm"])
    return out, (x, xn, r, dgate, dup, h)


def ffn_bwd(i, dx, dxb, saved, w):
    x, xn, r, dgate, dup, h = saved
    g_wd = mm_tn(f"ffn{i}_down_w", h, dxb)
    dx0, dx0b, dg, du, g_norm = ffn_fused_bwd(f"ffn{i}_b", dx, dxb, x, r, w["ffn_norm"][i:i + 1], dgate, dup,
                                              w["ffn_w_gate_t"][i], w["ffn_w_up_t"][i], w["ffn_w_down"][i])
    g_wg_t = mm_tn(f"ffn{i}_gate_w", dg, xn)
    g_wu_t = mm_tn(f"ffn{i}_up_w", du, xn)
    return dx0, dx0b, dict(ffn_norm=g_norm, ffn_w_gate_t=g_wg_t, ffn_w_up_t=g_wu_t, ffn_w_down=g_wd)


def local_step(x, mem, target, w, fetch=None, on_grads=None, anchor=None):
    consts = s5_setup(w, anchor)

    def need(stage, after):
        if fetch is not None:
            for k, v in fetch(stage, after).items():
                if isinstance(k, tuple):
                    w.setdefault(k[0], {})[k[1]] = v
                else:
                    w[k] = v

    need(0, consts["pw"])
    x1, s_e = even_fwd(x, w, lambda after: need(1, after))
    x2, s_c0 = ca_fwd(0, x1, mem, w)
    need(2, x2)
    x3, s_f0 = ffn_fwd(0, x2, w)
    x4, s_o = odd_fwd(x3, w, consts)
    need(3, x4)
    x5, s_c1 = ca_fwd(1, x4, mem, w)
    need(4, x5)
    (dx, dxb, g_final, loss), s_f1 = ffn_fwd(1, x5, w, target)

    def emit(stage, carry, plain, layered=None, layer=0):
        if on_grads is None:
            return carry
        out = dict(plain)
        out.update({(k, layer): v for k, v in (layered or {}).items()})
        return on_grads(stage, out, list(carry))

    dx, dxb, g_f1 = ffn_bwd(1, dx, dxb, s_f1, w)
    dx, dxb = emit(0, (dx, dxb), {}, g_f1, 1)
    dx, dxb, g_c1 = ca_bwd(1, dx, dxb, s_c1, mem, w)
    dx, dxb, g_o = odd_bwd(dx, dxb, s_o, w, consts)
    dx, dxb = emit(1, (dx, dxb), g_o, g_c1, 1)
    dx, dxb, g_f0 = ffn_bwd(0, dx, dxb, s_f0, w)
    dx, dxb = emit(2, (dx, dxb), {}, g_f0, 0)
    dx, dxb, g_c0 = ca_bwd(0, dx, dxb, s_c0, mem, w)
    dx, dxb = emit(3, (dx, dxb), {}, g_c0, 0)
    dproj, g_e = even_bwd_mixers(dxb, s_e, w)
    dproj = emit(4, dproj, {**g_e, "o_norm": g_o["o_norm"], "o_d": g_o["o_d"]})
    dx, dxb, g_e["e_norm"] = even_bwd_input(dx, dproj, s_e, w)

    grads = dict(g_e)
    grads.update(g_o)
    for g0, g1 in ((g_c0, g_c1), (g_f0, g_f1)):
        for k in g0:
            grads[k] = jnp.concatenate([g0[k], g1[k]], axis=0) if k.endswith("norm") else (g0[k], g1[k])
    grads["final_norm"] = g_final
    return loss, dx, grads


def _group(axes):
    pos = {a: lax.axis_index(a) for a in ("x", "y", "c")}
    me = 0
    for a in axes:
        me = me * 2 + pos[a]
    peers = []
    for mask in range(1, 2 ** len(axes)):
        peer = dict(pos)
        for bit, a in enumerate(axes):
            if (mask >> (len(axes) - 1 - bit)) & 1:
                peer[a] = 1 - pos[a]
        idx = 0
        for a in axes:
            idx = idx * 2 + peer[a]
        peers.append((idx, (peer["x"], peer["y"], peer["c"])))
    return me, peers


def _sibling():
    x, y, c = lax.axis_index("x"), lax.axis_index("y"), lax.axis_index("c")
    return c, (x, y, 1 - c)


_HBM =pl.BlockSpec(memory_space=pltpu.HBM)
_SEM = pl.BlockSpec(memory_space=pltpu.SEMAPHORE)
_EFFECT = pltpu.SideEffectType.DATAFLOW_SIDE_EFFECTING


def _gather_peers(direct):
    chip, _ = _group(("x", "y"))
    core = lax.axis_index("c")
    if direct:
        _, peers = _group(_ALL)
        return chip, core, [(idx // 2, idx % 2, dev) for idx, dev in peers]
    _, peers = _group(("x", "y"))
    return chip, core, [(idx, core, dev) for idx, dev in peers]


def gather_ici_start(name, groups, direct):
    flat = [b for g in groups for b in g]
    sizes = [len(g) for g in groups]
    k_ops, n_g = len(flat), len(groups)
    lands = [lax.empty((4, 2) + tuple(b.shape), b.dtype) for b in flat]
    fan = [N_DEV - 1 if d else 3 for d in direct]

    def body(*refs):
        src, land = refs[:k_ops], refs[k_ops:2 * k_ops]
        sems = refs[2 * k_ops:2 * k_ops + 3 * n_g]
        token = refs[-1]
        i = 0
        for g in range(n_g):
            send, recv, loc = sems[3 * g:3 * g + 3]
            chip, core, peers = _gather_peers(direct[g])
            for j in range(sizes[g]):
                pltpu.make_async_copy(src[i], land[i].at[chip, core], loc.at[j]).start()
                for k, (_, _, dev) in enumerate(peers):
                    s = fan[g] * j + k
                    pltpu.make_async_remote_copy(src_ref=src[i], dst_ref=land[i].at[chip, core], send_sem=send.at[s],
                                                 recv_sem=recv.at[s], device_id=dev, device_id_type=MESH).start()
                i += 1
        token[...] = jnp.zeros(token.shape, token.dtype)

    sem_shapes = []
    for s, f in zip(sizes, fan):
        sem_shapes += [pltpu.SemaphoreType.DMA((f * s,)), pltpu.SemaphoreType.DMA((f * s,)), pltpu.SemaphoreType.DMA((s,))]
    thru = [pltpu.HBM(a.shape, a.dtype) for a in flat + lands]
    outs = pl.pallas_call(
        body, name=name, out_shape=tuple(sem_shapes) + tuple(thru) + (S((8, LANES), F32),),
        in_specs=[_HBM] * (2 * k_ops), out_specs=[_SEM] * (3 * n_g) + [_HBM] * (2 * k_ops) + [pl.BlockSpec(memory_space=pltpu.VMEM)],
        input_output_aliases={i: 3 * n_g + i for i in range(2 * k_ops)},
        compiler_params=pltpu.CompilerParams(has_side_effects=_EFFECT),
    )(*[pltpu.with_memory_space_constraint(a, pltpu.HBM) for a in flat + lands])
    sems = [tuple(outs[3 * g:3 * g + 3]) for g in range(n_g)]
    srcs_thru, lands_thru, off = [], [], 3 * n_g
    for s in sizes:
        srcs_thru.append(list(outs[off:off + s]))
        off += s
    for s in sizes:
        lands_thru.append(list(outs[off:off + s]))
        off += s
    return sems, srcs_thru, lands_thru, outs[-1]


def gather_ici_wait(name, srcs, lands, sems, after, direct=False):
    n = len(srcs)

    def body(*refs):
        src, land = refs[:n], refs[n:2 * n]
        send, recv, loc = refs[2 * n:2 * n + 3]
        chip, core, peers = _gather_peers(direct)
        for j in range(n):
            for k, (pchip, pcore, dev) in enumerate(peers):
                s = len(peers) * j + k
                cp = pltpu.make_async_remote_copy(src_ref=src[j], dst_ref=land[j].at[pchip, pcore], send_sem=send.at[s],
                                                  recv_sem=recv.at[s], device_id=dev, device_id_type=MESH)
                cp.wait_send()
                cp.wait_recv()
            pltpu.make_async_copy(src[j], land[j].at[chip, core], loc.at[j]).wait()

    outs = pl.pallas_call(
        body, name=name, out_shape=tuple(pltpu.HBM(a.shape, a.dtype) for a in list(srcs) + list(lands)),
        in_specs=[_HBM] * (2 * n) + [_SEM] * 3 + [ANY], out_specs=[_HBM] * (2 * n),
        input_output_aliases={i: i for i in range(2 * n)},
        compiler_params=pltpu.CompilerParams(has_side_effects=_EFFECT),
    )(*srcs, *lands, *sems, after)
    return list(outs[n:])


def gather_d2d(name, bufs):
    k_ops = len(bufs)

    def body(*refs):
        in_refs, out_refs = refs[:k_ops], refs[k_ops:2 * k_ops]
        send_sems, recv_sems = refs[2 * k_ops:]
        core, sib = _sibling()
        sent, landed = [], []
        for i in range(k_ops):
            cp = pltpu.make_async_remote_copy(src_ref=in_refs[i].at[:, core], dst_ref=out_refs[i].at[:, core],
                                              send_sem=send_sems.at[i], recv_sem=recv_sems.at[i], device_id=sib, device_id_type=MESH)
            cp.start()
            sent.append(cp)
            landed.append(pltpu.make_async_remote_copy(src_ref=in_refs[i].at[:, core], dst_ref=out_refs[i].at[:, 1 - core],
                                                       send_sem=send_sems.at[i], recv_sem=recv_sems.at[i],
                                                       device_id=sib, device_id_type=MESH))
        for cp in landed:
            cp.wait_recv()
        for cp in sent:
            cp.wait_send()

    return pl.pallas_call(
        body, in_specs=[ANY] * k_ops, out_specs=[ANY] * k_ops, out_shape=[S(b.shape, b.dtype) for b in bufs],
        input_output_aliases={i: i for i in range(k_ops)},
        scratch_shapes=[pltpu.SemaphoreType.DMA((k_ops,)), pltpu.SemaphoreType.DMA((k_ops,))],
        name=name)(*bufs)


_ALL = ("x", "y", "c")


def _unit_rows(units):
    offs, off = [], 0
    for u in units:
        offs.append(off)
        off += u.shape[1]
    return offs, off


def scatter_start(name, units, carry):
    n_u, n_c = len(units), len(carry)
    offs, rows = _unit_rows(units)
    land = lax.empty((N_DEV, rows) + tuple(units[0].shape[2:]), units[0].dtype)
    fan = N_DEV - 1

    def body(*refs):
        u_refs, land_ref = refs[:n_u], refs[n_u]
        send, recv, loc = refs[n_u + 1 + n_c:n_u + 4 + n_c]
        me, peers = _group(_ALL)
        for j in range(n_u):
            rs = pl.ds(offs[j], units[j].shape[1])
            pltpu.make_async_copy(u_refs[j].at[me], land_ref.at[me, rs], loc.at[j]).start()
            for k, (idx, dev) in enumerate(peers):
                pltpu.make_async_remote_copy(src_ref=u_refs[j].at[idx], dst_ref=land_ref.at[me, rs], send_sem=send.at[fan * j + k],
                                             recv_sem=recv.at[fan * j + k], device_id=dev, device_id_type=MESH).start()

    thru = list(units) + [land] + list(carry)
    outs = pl.pallas_call(
        body, name=name,
        out_shape=(pltpu.SemaphoreType.DMA((fan * n_u,)), pltpu.SemaphoreType.DMA((fan * n_u,)), pltpu.SemaphoreType.DMA((n_u,)))
        + tuple(pltpu.HBM(a.shape, a.dtype) for a in thru),
        in_specs=[_HBM] * len(thru), out_specs=[_SEM] * 3 + [_HBM] * len(thru),
        input_output_aliases={i: 3 + i for i in range(len(thru))},
        compiler_params=pltpu.CompilerParams(has_side_effects=_EFFECT),
    )(*[pltpu.with_memory_space_constraint(a, pltpu.HBM) for a in thru])
    return tuple(outs[:3]), list(outs[3:3 + n_u]), outs[3 + n_u], list(outs[4 + n_u:])


def scatter_wait(name, units, land, sems, after):
    n_u = len(units)
    offs, _ = _unit_rows(units)
    fan = N_DEV - 1

    def body(*refs):
        u_refs, land_ref = refs[:n_u], refs[n_u]
        send, recv, loc = refs[n_u + 1:n_u + 4]
        me, peers = _group(_ALL)
        for j in range(n_u):
            rs = pl.ds(offs[j], units[j].shape[1])
            for k, (idx, dev) in enumerate(peers):
                cp = pltpu.make_async_remote_copy(src_ref=u_refs[j].at[idx], dst_ref=land_ref.at[idx, rs], send_sem=send.at[fan * j + k],
                                                  recv_sem=recv.at[fan * j + k], device_id=dev, device_id_type=MESH)
                cp.wait_send()
                cp.wait_recv()
            pltpu.make_async_copy(u_refs[j].at[me], land_ref.at[me, rs], loc.at[j]).wait()

    thru = list(units) + [land]
    outs = pl.pallas_call(
        body, name=name, out_shape=tuple(pltpu.HBM(a.shape, a.dtype) for a in thru),
        in_specs=[_HBM] * len(thru) + [_SEM] * 3 + [ANY], out_specs=[_HBM] * len(thru),
        input_output_aliases={i: i for i in range(len(thru))},
        compiler_params=pltpu.CompilerParams(has_side_effects=_EFFECT),
    )(*thru, *sems, after)
    return outs[n_u]


def _row_tile(rows, cap=512):
    return next(t for t in range(cap - cap % 16, 0, -16) if rows % t == 0)


def sum_shares(name, recv, me):
    n, rows, c = recv.shape
    tr = _row_tile(rows)

    def body(me_ref, *refs):
        acc = refs[0][...].astype(F32)
        for r in refs[1:n]:
            acc = acc + r[...].astype(F32)
        refs[n][...] = acc

    def slot(mask):
        return pl.BlockSpec((None, tr, c), lambda i, me, mask=mask: (jnp.bitwise_xor(me[0], mask), i, 0))

    spec = pltpu.PrefetchScalarGridSpec(
        num_scalar_prefetch=1, grid=(rows // tr,), in_specs=[slot(k) for k in range(n)],
        out_specs=pl.BlockSpec((tr, c), lambda i, me: (i, 0)))
    return pl.pallas_call(body, grid_spec=spec, out_shape=S((rows, c), F32),
                          compiler_params=_cp("parallel"), name=name)(me, *([recv] * n))


def sum_slots(name, slots):
    n, r, c = slots.shape

    def body(s_ref, o_ref):
        acc = s_ref[0]
        for j in range(1, n):
            acc = acc + s_ref[j]
        o_ref[...] = acc

    return pl.pallas_call(body, out_shape=S((r, c), F32), compiler_params=pltpu.CompilerParams(vmem_limit_bytes=VMEM_LIMIT),
                          name=name)(slots)


def adamw_units(name, pieces, transposed, w, m, v):
    n_l, k, n = w.shape
    tk = _tile(k, 512) if transposed else k
    p_rows = n if transposed else k
    arrs = [p[0] if isinstance(p, tuple) else p for p in pieces]
    offs = [p[1] // p_rows if isinstance(p, tuple) else 0 for p in pieces]
    assert all(not isinstance(p, tuple) or p[1] % p_rows == 0 for p in pieces)
    c1 = 1.0 - ADAM_B1 ** ADAM_STEP
    c2 = 1.0 - ADAM_B2 ** ADAM_STEP

    def body(*refs):
        p_refs, (w_ref, m_ref, v_ref, g_ref, d_ref, m2_ref, v2_ref) = refs[:n_l], refs[n_l:]
        gv = p_refs[0][...]
        for j in range(1, n_l):
            gv = jnp.where(pl.program_id(0) == j, p_refs[j][...], gv)
        if transposed:
            gv = gv.T
        m2 = ADAM_B1 * m_ref[...] + (1.0 - ADAM_B1) * gv
        v2 = ADAM_B2 * v_ref[...] + (1.0 - ADAM_B2) * (gv * gv)
        g_ref[...] = gv
        m2_ref[...] = m2
        v2_ref[...] = v2
        d_ref[...] = -ADAM_LR * ((m2 / c1) / (jnp.sqrt(v2 / c2) + ADAM_EPS) + ADAM_WD * w_ref[...])

    def piece(o):
        if transposed:
            return pl.BlockSpec((n, tk), lambda l, i, o=o: (o, i))
        return pl.BlockSpec((k, n), lambda l, i, o=o: (o, 0))

    blk = pl.BlockSpec((None, tk, n), lambda l, i: (l, i, 0))
    return tuple(pl.pallas_call(body, grid=(n_l, k // tk), in_specs=[piece(o) for o in offs] + [blk] * 3, out_specs=[blk] * 4,
                                out_shape=[S(w.shape, F32)] * 4, compiler_params=_cp("parallel", "parallel"),
                                name=name)(*arrs, w, m, v))


def adamw_native(name, g, w, m, v, tr=512):
    shape = w.shape
    cols = shape[-1]
    rows = w.size // cols
    tr = _tile(rows, tr) if rows % 8 == 0 else rows
    c1 = 1.0 - ADAM_B1 ** ADAM_STEP
    c2 = 1.0 - ADAM_B2 ** ADAM_STEP

    def body(g_ref, w_ref, m_ref, v_ref, d_ref, m2_ref, v2_ref):
        gv = g_ref[...]
        m2 = ADAM_B1 * m_ref[...] + (1.0 - ADAM_B1) * gv
        v2 = ADAM_B2 * v_ref[...] + (1.0 - ADAM_B2) * (gv * gv)
        m2_ref[...] = m2
        v2_ref[...] = v2
        d_ref[...] = -ADAM_LR * ((m2 / c1) / (jnp.sqrt(v2 / c2) + ADAM_EPS) + ADAM_WD * w_ref[...])

    row = pl.BlockSpec((tr, cols), lambda i: (i, 0))
    outs = pl.pallas_call(body, grid=(rows // tr,), in_specs=[row] * 4, out_specs=[row] * 3,
                          out_shape=[S((rows, cols), F32)] * 3, compiler_params=_cp("parallel"),
                          name=name)(*[a.reshape(rows, cols) for a in (g, w, m, v)])
    return tuple(o.reshape(shape) for o in outs)


_REPLICATED = ("e_norm", "e_gmlp_w", "e_gmlp_b", "e_conv_b", "e_conv_ln_g", "e_conv_ln_b", "o_lam_re", "o_lam_im", "o_log_dt",
               "o_b_re", "o_b_im", "o_c_re", "o_c_im", "ca_norm", "ca_mem_norm", "ffn_norm", "final_norm")
_ORDER = ("e_norm", "e_w_in", "e_gmlp_w", "e_gmlp_b", "e_conv_w", "e_conv_b", "e_conv_ln_g", "e_conv_ln_b", "e_w_out",
          "o_norm", "o_w_in", "o_lam_re", "o_lam_im", "o_log_dt", "o_b_re", "o_b_im", "o_c_re", "o_c_im", "o_d", "o_w_out",
          "ca_norm", "ca_mem_norm", "ca_wq", "ca_wk", "ca_wv", "ca_wo", "ffn_norm", "ffn_w_gate", "ffn_w_up", "ffn_w_down",
          "final_norm")


def _rows128(a, multiple=8):
    flat = a.reshape(-1)
    rows = -(-flat.shape[0] // (LANES * multiple)) * multiple
    return jnp.pad(flat, (0, rows * LANES - flat.shape[0])).reshape(rows, LANES)


def _shard(full, axis):
    s = full.shape
    return jnp.moveaxis(full.reshape(s[:axis] + (N_DEV, s[axis] // N_DEV) + s[axis + 1:]), axis, 0)


_UNITS = (("e_w_in", 0, True), ("e_w_out", 0, False), ("o_w_in", 0, False), ("o_w_out", 0, True),
          *[(n, i, False) for n in ("ca_wq", "ca_wk", "ca_wv", "ca_wo") for i in (0, 1)],
          *[(n, i, tr) for n, tr in (("ffn_w_gate", True), ("ffn_w_up", True), ("ffn_w_down", False)) for i in (0, 1)])
_LAYERED = ("ca_wq", "ca_wk", "ca_wv", "ca_wo", "ffn_w_gate", "ffn_w_up", "ffn_w_down")
_SMALL_SHARDED = (("e_conv_w", 2), ("o_norm", 1), ("o_d", 1))
RS_ROW = 1024


def _unit_key(name, tr):
    return name + "_t" if tr else name


def _stage_of(name, layer):
    if name.startswith("e_"):
        return 0 if name == "e_w_in" else 1
    if name.startswith("o_"):
        return 2
    if name.startswith("ca_"):
        return 1 if layer == 0 else 3
    return 2 if layer == 0 else 4


GATHER_STAGES = 5
GATHER_DIRECT = (False, False, False, True, False)


def weight_fetcher(local):
    groups, meta = [[] for _ in range(GATHER_STAGES)], [[] for _ in range(GATHER_STAGES)]
    for name, layer, tr in _UNITS:
        blk = local[name][layer]
        st = _stage_of(name, layer)
        groups[st].append(_bf(blk.T if tr else blk))
        meta[st].append((name, layer, tr))
    small = jnp.concatenate([local[name].reshape(-1) for name, _ in _SMALL_SHARDED])
    groups[0].append(_rows128(small))
    direct = list(GATHER_DIRECT)
    sems, srcs, lands, token = gather_ici_start("ag_w_start", groups, direct)

    def fetch(stage, after):
        bufs = gather_ici_wait(f"ag_w_wait{stage}", srcs[stage], lands[stage], sems[stage], after, direct[stage])
        if not direct[stage]:
            bufs = gather_d2d(f"ag_w_d2d{stage}", bufs)
        got = {}
        for (name, layer, tr), blk, buf in zip(meta[stage], groups[stage], bufs):
            arr = buf.reshape((N_DEV * blk.shape[0],) + tuple(blk.shape[1:]))
            if name in _LAYERED:
                got[(_unit_key(name, tr), layer)] = arr
            else:
                got[_unit_key(name, tr)] = arr
        if stage == 0:
            flat = bufs[-1].reshape(N_DEV, -1)
            off = 0
            for name, axis in _SMALL_SHARDED:
                blk = local[name]
                seg = flat[:, off:off + blk.size].reshape((N_DEV,) + blk.shape)
                off += blk.size
                seg = jnp.moveaxis(seg, 0, axis)
                got[name] = seg.reshape(seg.shape[:axis] + (-1,) + seg.shape[axis + 2:])
            got["e_conv_w"] = got["e_conv_w"][0]
        return got

    return fetch, token


def _grad_stage_of(name, layer):
    if name.startswith("e_"):
        return 4
    if name.startswith("o_"):
        return 1
    if name.startswith("ca_"):
        return 3 if layer == 0 else 1
    return 2 if layer == 0 else 0


GRAD_STAGES = 5
SMALL_ROWS = 16


def gradient_reducer(local, mom, var):
    me = (4 * lax.axis_index("x") + 2 * lax.axis_index("y") + lax.axis_index("c")).astype(jnp.int32).reshape(1)
    pending = []

    def start(stage, grads, carry):
        def grad_of(unit):
            key = _unit_key(unit[0], unit[2])
            return grads[(key, unit[1])] if unit[0] in _LAYERED else grads[key]

        units = sorted([u for u in _UNITS if _grad_stage_of(u[0], u[1]) == stage], key=lambda u: -grad_of(u).size)
        parts, spans = [], []
        for unit in units:
            g = grad_of(unit)
            part = g.reshape(N_DEV, -1, RS_ROW)
            spans.append((part.shape[1], g.shape[0] // N_DEV, g.shape[1]))
            parts.append(part)
        if stage == GRAD_STAGES - 1:
            small = jnp.concatenate([_shard(grads[name], axis).reshape(N_DEV, -1) for name, axis in _SMALL_SHARDED], axis=1)
            small = jnp.pad(small, ((0, 0), (0, SMALL_ROWS * RS_ROW - small.shape[1])))
            parts.append(small.astype(BF16).reshape(N_DEV, SMALL_ROWS, RS_ROW))
        sems, sent, land, carry = scatter_start(f"rs_start{stage}", parts, carry)
        pending.append((stage, units, spans, sems, sent, land))
        return carry

    def finish(after):
        res, per_layer, small_flat = {}, {}, None
        for stage, units, spans, sems, sent, land in pending:
            land = scatter_wait(f"rs_wait{stage}", sent, land, sems, after)
            total = sum_shares(f"rs_sum{stage}", land, me)
            off = 0
            for (name, layer, tr), (rows, r, c) in zip(units, spans):
                piece = (total, off) if c == RS_ROW else total[off:off + rows].reshape(r, c)
                per_layer.setdefault(name, {})[layer] = (piece, tr)
                off += rows
            if stage == GRAD_STAGES - 1:
                small_flat = total[off:off + SMALL_ROWS].reshape(-1)
        for name, by_layer in per_layer.items():
            pieces = [by_layer[i][0] for i in sorted(by_layer)]
            res[name] = adamw_units("adamw_" + name, pieces, by_layer[0][1], local[name], mom[name], var[name])
        off = 0
        for name, _ in _SMALL_SHARDED:
            blk = local[name]
            g = small_flat[off:off + blk.size].reshape(blk.shape)
            off += blk.size
            res[name] = (g,) + adamw_native("adamw_" + name, g, blk, mom[name], var[name])
        return res

    return start, finish


def replicated_start(grads, loss):
    pack = jnp.concatenate([_rows128(grads[name]) for name in _REPLICATED] + [_rows128(loss)], axis=0)
    sems, srcs, lands, token = gather_ici_start("ag_g_start", [[pack]], [False])
    return sems[0], srcs[0], lands[0], token


def replicated_finish(handle, after, w, mom, var):
    sems, srcs, lands, _ = handle
    (buf,) = gather_d2d("ag_g_d2d", gather_ici_wait("ag_g_wait", srcs, lands, sems, after))
    rows = srcs[0].shape[0]
    total = sum_slots("ag_g_sum", buf.reshape(N_DEV, rows, LANES))
    res, off = {}, 0
    for name in _REPLICATED:
        n = w[name].size
        nr = -(-n // (LANES * 8)) * 8
        g = total[off:off + nr].reshape(-1)[:n].reshape(w[name].shape)
        off += nr
        res[name] = (g,) + adamw_native("adamw_" + name, g, w[name], mom[name], var[name])
    return res, total[off, 0]


def kernel(x, mem, e_norm, e_w_in, e_gmlp_w, e_gmlp_b, e_conv_w, e_conv_b, e_conv_ln_g, e_conv_ln_b, e_w_out, o_norm, o_w_in, o_lam_re, o_lam_im, o_log_dt, o_b_re, o_b_im, o_c_re, o_c_im, o_d, o_w_out, ca_norm, ca_mem_norm, ca_wq, ca_wk, ca_wv, ca_wo, ffn_norm, ffn_w_gate, ffn_w_up, ffn_w_down, final_norm, loss_target, m_e_norm, m_e_w_in, m_e_gmlp_w, m_e_gmlp_b, m_e_conv_w, m_e_conv_b, m_e_conv_ln_g, m_e_conv_ln_b, m_e_w_out, m_o_norm, m_o_w_in, m_o_lam_re, m_o_lam_im, m_o_log_dt, m_o_b_re, m_o_b_im, m_o_c_re, m_o_c_im, m_o_d, m_o_w_out, m_ca_norm, m_ca_mem_norm, m_ca_wq, m_ca_wk, m_ca_wv, m_ca_wo, m_ffn_norm, m_ffn_w_gate, m_ffn_w_up, m_ffn_w_down, m_final_norm, v_e_norm, v_e_w_in, v_e_gmlp_w, v_e_gmlp_b, v_e_conv_w, v_e_conv_b, v_e_conv_ln_g, v_e_conv_ln_b, v_e_w_out, v_o_norm, v_o_w_in, v_o_lam_re, v_o_lam_im, v_o_log_dt, v_o_b_re, v_o_b_im, v_o_c_re, v_o_c_im, v_o_d, v_o_w_out, v_ca_norm, v_ca_mem_norm, v_ca_wq, v_ca_wk, v_ca_wv, v_ca_wo, v_ffn_norm, v_ffn_w_gate, v_ffn_w_up, v_ffn_w_down, v_final_norm):
    given = dict(locals())
    local = {k: given[k] for k in _ORDER}
    mom = {k: given["m_" + k] for k in _ORDER}
    var = {k: given["v_" + k] for k in _ORDER}

    w = {}
    w.update({
        "e_norm": e_norm, "e_gmlp_w": e_gmlp_w[0], "e_gmlp_b": e_gmlp_b.reshape(A_GROUPS, GMLP_BLOCK, 1),
        "e_conv_b": e_conv_b, "e_conv_ln_g": e_conv_ln_g, "e_conv_ln_b": e_conv_ln_b,
        "o_lam_re": o_lam_re[0], "o_lam_im": o_lam_im[0], "o_log_dt": o_log_dt[0], "o_b_re": o_b_re[0], "o_b_im": o_b_im[0],
        "o_c_re": o_c_re[0], "o_c_im": o_c_im[0], "ca_norm": ca_norm, "ca_mem_norm": ca_mem_norm, "ffn_norm": ffn_norm,
        "final_norm": final_norm.reshape(1, D_MODEL),
    })
    start_reduce, finish_reduce = gradient_reducer(local, mom, var)
    fetch, token = weight_fetcher(local)
    loss_part, grad_x, grads = local_step(x[0], mem[0], loss_target[0], w, fetch, start_reduce, token[0:1, 0:1])
    grads["final_norm"] = grads["final_norm"].reshape(D_MODEL)

    handle = replicated_start(grads, loss_part)
    res = finish_reduce(handle[3])
    rep, loss = replicated_finish(handle, res["ffn_w_down"][1], local, mom, var)
    res.update(rep)
    return (loss, grad_x[None], *[res[k][0] for k in _ORDER], *[res[k][1] for k in _ORDER],
            *[res[k][2] for k in _ORDER], *[res[k][3] for k in _ORDER])
```

```python
import jax
import jax.numpy as jnp
from jax import lax
from jax.experimental import pallas as pl
from jax.experimental.pallas import tpu as pltpu

F32 = jnp.float32
BF16 = jnp.bfloat16
S = jax.ShapeDtypeStruct

D_MODEL = 1024
A_WIDTH = 512
A_GROUPS = 4
GMLP_BLOCK = 128
CHUNK = 64
B_WIDTH = 512
IN_WIDTH = 2 * A_WIDTH + 2 * B_WIDTH
CONV_WIDTH = 31
CONV_PAD = 32
C_WIDTH = 512
C_GROUP_CH = 16
C_GROUPS = 32
C_STATE = 64
N_STATE = C_GROUPS * C_STATE
CA_HEADS = 4
CA_HEAD_DIM = 256
EPS = 1e-6
ADAM_LR = 0.001
ADAM_B1 = 0.9
ADAM_B2 = 0.999
ADAM_EPS = 1e-08
ADAM_WD = 0.01
ADAM_STEP = 10
N_DEV = 8
LANES = 128
VMEM_LIMIT = 56 << 20
VMEM_BUDGET = 40 << 20
MM_TN_RESIDENT = 8 << 20
MESH = pl.DeviceIdType.MESH
ANY = pl.BlockSpec(memory_space=pl.ANY)


def _cp(*sem):
    return pltpu.CompilerParams(dimension_semantics=sem, vmem_limit_bytes=VMEM_LIMIT)


def _tile(n, pref):
    t = pref
    while n % t:
        t //= 2
    return t


def _bf(v):
    return v if v.dtype == BF16 else v.astype(BF16)


def _sigmoid(x):
    return 1.0 / (1.0 + jnp.exp(-x))


_GC = 0.7978845608028654


def _gelu(x):
    return 0.5 * x * (1.0 + jnp.tanh(_GC * (x + 0.044715 * x * x * x)))


def _gelu_grad(x):
    x2 = x * x
    t = jnp.tanh(_GC * (x + 0.044715 * x * x2))
    return 0.5 * (1.0 + t) + 0.5 * x * (1.0 - t * t) * _GC * (1.0 + 3.0 * 0.044715 * x2)


def _tspec(entry, tm):
    if isinstance(entry, tuple):
        arr, cb, width = entry
        return arr, pl.BlockSpec((tm, width), lambda i, cb=cb: (i, cb))
    return entry, pl.BlockSpec((tm, entry.shape[1]), lambda i: (i, 0))


def rows_call(name, fn, tiled, full, outs, accs, tm=256):
    pairs = [_tspec(e, tm) for e in tiled]
    arrs = [p[0] for p in pairs]
    rows = arrs[0].shape[0]
    tm = _tile(rows, tm)
    pairs = [_tspec(e, tm) for e in tiled]
    n_in = len(tiled) + len(full)
    n_out = len(outs)

    def body(*refs):
        vals = [r[...] for r in refs[:n_in]]
        o_refs = refs[n_in:n_in + n_out]
        a_refs = refs[n_in + n_out:]
        ov, av = fn(*vals)
        for r, v in zip(o_refs, ov):
            r[...] = v.astype(r.dtype)
        if a_refs:
            @pl.when(pl.program_id(0) == 0)
            def _():
                for r in a_refs:
                    r[...] = jnp.zeros(r.shape, r.dtype)
            for r, v in zip(a_refs, av):
                r[...] += v

    in_specs = [p[1] for p in pairs] + [pl.BlockSpec(a.shape, lambda i, nd=a.ndim: (0,) * nd) for a in full]
    out_specs = [pl.BlockSpec((tm, c), lambda i: (i, 0)) for c, _ in outs]
    out_specs += [pl.BlockSpec(s, lambda i, nd=len(s): (0,) * nd) for s in accs]
    out_shape = [S((rows, c), dt) for c, dt in outs] + [S(s, F32) for s in accs]
    return pl.pallas_call(body, grid=(rows // tm,), in_specs=in_specs, out_specs=out_specs, out_shape=out_shape,
                          compiler_params=_cp("arbitrary"), name=name)(*arrs, *full)


def mm_nn(name, m, n, pairs, n_acc, epi, outs, tiled=(), cols=(), rowv=(), sums=(), norm_gain=None):
    a_ops, a_slot, b_arrs, b_specs, idx, trans = [], [], [], [], [], []
    fixed = 0
    for pair in pairs:
        a, b, k = pair[:3]
        bt = len(pair) > 3
        arr, cb, kdim = a if isinstance(a, tuple) else (a, 0, a.shape[1])
        key = (id(arr), cb, kdim)
        if key not in [o[0] for o in a_ops]:
            a_ops.append((key, arr, cb, kdim))
        a_slot.append([o[0] for o in a_ops].index(key))
        b_arr, off = b if isinstance(b, tuple) else (b, 0)
        b_arrs.append(b_arr)
        if bt:
            assert off % n == 0 and b_arr.shape[1] == kdim
            b_specs.append(pl.BlockSpec((n, kdim), lambda i, o=off // n: (o, 0), pipeline_mode=pl.Buffered(1)))
        else:
            assert b_arr.shape[1] == n
            b_specs.append(pl.BlockSpec((kdim, n), lambda i, o=off: (o, 0), pipeline_mode=pl.Buffered(1)))
        fixed += kdim * n * b_arr.dtype.itemsize
        idx.append(k)
        trans.append(bt)
    per_row = sum(2 * kdim * arr.dtype.itemsize for _, arr, _, kdim in a_ops)
    per_row += sum(2 * n * t.dtype.itemsize for t in tiled) + sum(2 * n * jnp.dtype(dt).itemsize for dt in outs)
    cn = n if sums or cols else (512 if n % 512 == 0 else 256)
    per_row += (n_acc + 3) * cn * 4
    tm = next((t for t in (1024, 512, 256, 128) if m % t == 0 and fixed + t * per_row <= VMEM_BUDGET), _tile(m, 128))
    n_a, n_p, n_t = len(a_ops), len(pairs), len(tiled)
    n_in = n_a + n_p + n_t + len(cols) + len(rowv)
    normed = norm_gain is not None
    o0 = n_in + normed

    def body(*refs):
        a_vals = [None if normed and i == 0 else _bf(r[...]) for i, r in enumerate(refs[:n_a])]
        if normed:
            xv = refs[0][...]
            rv = lax.rsqrt(jnp.mean(xv * xv, axis=-1, keepdims=True) + EPS)
            a_vals[0] = (xv * rv * refs[n_in][...]).astype(BF16)
            refs[o0 + len(outs)][...] = a_vals[0]
            refs[o0 + len(outs) + 1][...] = rv
        for j in range(n // cn):
            cs = slice(j * cn, (j + 1) * cn)
            accs = [None] * n_acc
            for p in range(n_p):
                av, b_ref = a_vals[a_slot[p]], refs[n_a + p]
                if trans[p]:
                    d = lax.dot_general(av, _bf(b_ref[cs, :]), (((1,), (1,)), ((), ())), preferred_element_type=F32)
                else:
                    d = jnp.dot(av, _bf(b_ref[:, cs]), preferred_element_type=F32)
                accs[idx[p]] = d if accs[idx[p]] is None else accs[idx[p]] + d
            extra = [r[:, cs] for r in refs[n_a + n_p:n_a + n_p + n_t]] + [r[...] for r in refs[n_a + n_p + n_t:n_in - len(rowv)]]
            extra += [r[:, cs] for r in refs[n_in - len(rowv):n_in]]
            ov = epi(accs, *extra)
            for r, v in zip(refs[o0:o0 + len(outs)], ov):
                r[:, cs] = v.astype(r.dtype)
        sv = ov[len(outs):]
        if sums:
            s_refs = refs[o0 + len(outs) + 2 * normed:]

            @pl.when(pl.program_id(0) == 0)
            def _():
                for r in s_refs:
                    r[...] = jnp.zeros(r.shape, r.dtype)
            for r, v in zip(s_refs, sv):
                r[...] += v

    in_specs = [pl.BlockSpec((tm, kdim), lambda i, cb=cb: (i, cb)) for _, _, cb, kdim in a_ops] + b_specs
    in_specs += [pl.BlockSpec((tm, n), lambda i: (i, 0)) for _ in tiled]
    in_specs += [pl.BlockSpec((tm, 1), lambda i: (i, 0)) for _ in cols]
    in_specs += [pl.BlockSpec((1, n), lambda i: (0, 0)) for _ in rowv]
    out_specs = [pl.BlockSpec((tm, n), lambda i: (i, 0)) for _ in outs]
    out_shape = [S((m, n), dt) for dt in outs]
    gain = []
    if normed:
        k0 = a_ops[0][3]
        gain = [norm_gain]
        in_specs.append(pl.BlockSpec((1, k0), lambda i: (0, 0)))
        out_specs += [pl.BlockSpec((tm, k0), lambda i: (i, 0)), pl.BlockSpec((tm, 1), lambda i: (i, 0))]
        out_shape += [S((m, k0), BF16), S((m, 1), F32)]
    out_specs += [pl.BlockSpec(s, lambda i, nd=len(s): (0,) * nd) for s in sums]
    out_shape += [S(s, F32) for s in sums]
    return pl.pallas_call(body, grid=(m // tm,), in_specs=in_specs, out_specs=out_specs, out_shape=out_shape,
                          compiler_params=_cp("arbitrary" if sums else "parallel"),
                          name=name)(*[o[1] for o in a_ops], *b_arrs, *tiled, *cols, *rowv, *gain)


def mm_tn(name, a, b, out_dtype=BF16):
    if isinstance(a, tuple):
        a_arr, a_cb, m = a
    else:
        a_arr, a_cb, m = a, None, a.shape[1]
    if isinstance(b, tuple):
        b_arr, b_cb, n = b
    else:
        b_arr, b_cb, n = b, None, b.shape[1]
    t = a_arr.shape[0]
    whole_b = t * n * b_arr.dtype.itemsize <= MM_TN_RESIDENT and b_cb is None
    tn = n if whole_b else _tile(n, 512)
    tm = _tile(m, 512 if t * 512 * a_arr.dtype.itemsize * 2 + t * tn * b_arr.dtype.itemsize * 2 <= VMEM_BUDGET else 256)
    a_off = 0 if a_cb is None else a_cb * (m // tm)
    b_off = 0 if b_cb is None else b_cb * (n // tn)

    def body(a_ref, b_ref, o_ref):
        o_ref[...] = lax.dot_general(_bf(a_ref[...]), _bf(b_ref[...]), (((0,), (0,)), ((), ())),
                                     preferred_element_type=F32).astype(o_ref.dtype)

    if whole_b:
        b_spec = pl.BlockSpec((t, n), lambda i, j: (0, 0), pipeline_mode=pl.Buffered(1))
    else:
        b_spec = pl.BlockSpec((t, tn), lambda i, j: (0, j + b_off))
    return pl.pallas_call(
        body, grid=(m // tm, n // tn),
        in_specs=[pl.BlockSpec((t, tm), lambda i, j: (0, i + a_off)), b_spec],
        out_specs=pl.BlockSpec((tm, tn), lambda i, j: (i, j)), out_shape=S((m, n), out_dtype),
        compiler_params=_cp("parallel", "parallel"), name=name)(a_arr, b_arr)


def rms_bwd_gain_only(name, dxn, x, r):
    def fn(dv, xv, rv):
        return [], [jnp.sum(dv * xv * rv, axis=0, keepdims=True)]
    return rows_call(name, fn, [dxn, x, r], [], [], [(1, x.shape[1])])[0]


def _final_loss_epi(accs, res, tv, g):
    xv = res + accs[0]
    d = xv.shape[-1]
    r = lax.rsqrt(jnp.mean(xv * xv, axis=-1, keepdims=True) + EPS)
    xh = xv * r
    err = xh * g - tv
    dy = err * (1.0 / d)
    w = dy * g
    dx = r * (w - xh * jnp.mean(w * xh, axis=-1, keepdims=True))
    part = jnp.sum(jnp.sum(err * err, axis=-1, keepdims=True), axis=0, keepdims=True) * (0.5 / d)
    return [dx, dx, jnp.sum(dy * xh, axis=0, keepdims=True), part]


def _gmlp_mask():
    row = lax.broadcasted_iota(jnp.int32, (GMLP_BLOCK, GMLP_BLOCK), 0) // CHUNK
    col = lax.broadcasted_iota(jnp.int32, (GMLP_BLOCK, GMLP_BLOCK), 1) // CHUNK
    return col <= row


def _ln_plain(v):
    mu = jnp.mean(v, axis=-1, keepdims=True)
    vc = v - mu
    rstd = lax.rsqrt(jnp.mean(vc * vc, axis=-1, keepdims=True) + EPS)
    return vc * rstd, rstd


def even_out_fwd(name, proj, hc, x, w, b, ln_g, ln_b, w_out, tm=512):
    t, d = x.shape
    tm = _tile(t, tm)

    def body(au_ref, av_ref, hc_ref, x_ref, w_ref, b_ref, lg_ref, lb_ref, wo_ref, x1_ref, oa_ref, ob_ref):
        mask = _gmlp_mask()
        u = _gelu(au_ref[...])
        vn, _ = _ln_plain(_gelu(av_ref[...]))
        vnb = _bf(vn)
        for g in range(A_GROUPS):
            wg = _bf(jnp.where(mask, w_ref[g], 0.0))
            cs = slice(g * GMLP_BLOCK, (g + 1) * GMLP_BLOCK)
            for n in range(tm // GMLP_BLOCK):
                rs = slice(n * GMLP_BLOCK, (n + 1) * GMLP_BLOCK)
                sg = jnp.dot(wg, vnb[rs, cs], preferred_element_type=F32) + b_ref[g]
                oa_ref[rs, cs] = (u[rs, cs] * sg).astype(oa_ref.dtype)
        y, _ = _ln_plain(hc_ref[...])
        z = y * lg_ref[...] + lb_ref[...]
        ob_ref[...] = (z * _sigmoid(z)).astype(ob_ref.dtype)
        x1_ref[...] = (x_ref[...] + jnp.dot(oa_ref[...], wo_ref[0:A_WIDTH, :], preferred_element_type=F32)
                       + jnp.dot(ob_ref[...], wo_ref[A_WIDTH:, :], preferred_element_type=F32))

    half = pl.BlockSpec((tm, A_WIDTH), lambda i: (i, 0))
    return pl.pallas_call(
        body, grid=(t // tm,),
        in_specs=[half, pl.BlockSpec((tm, A_WIDTH), lambda i: (i, 1)), half, pl.BlockSpec((tm, d), lambda i: (i, 0)),
                  _whole(w), _whole(b), _whole(ln_g), _whole(ln_b), _whole(w_out)],
        out_specs=[pl.BlockSpec((tm, d), lambda i: (i, 0)), half, half],
        out_shape=[S((t, d), F32), S((t, A_WIDTH), BF16), S((t, B_WIDTH), BF16)],
        compiler_params=_cp("parallel"), name=name)(proj, proj, hc, x, w, b, ln_g, ln_b, w_out)


def gmlp_bwd(name, proj, dxb, w_out, w, b, tm=512):
    t = proj.shape[0]
    tm = _tile(t, tm)

    def body(au_ref, av_ref, dx_ref, wo_ref, w_ref, b_ref, dp_ref, dw_ref, db_ref):
        @pl.when(pl.program_id(0) == 0)
        def _():
            dw_ref[...] = jnp.zeros(dw_ref.shape, F32)
            db_ref[...] = jnp.zeros(db_ref.shape, F32)

        mask = _gmlp_mask()
        au = au_ref[...]
        av = av_ref[...]
        u = _gelu(au)
        vn, rstd = _ln_plain(_gelu(av))
        vnb = _bf(vn)
        dout = lax.dot_general(dx_ref[...], wo_ref[0:A_WIDTH, :], _NT, preferred_element_type=F32)
        dvn_cols = []
        for g in range(A_GROUPS):
            wm = jnp.where(mask, w_ref[g], 0.0)
            wg = _bf(wm)
            wgt = _bf(wm.T)
            cs = slice(g * GMLP_BLOCK, (g + 1) * GMLP_BLOCK)
            dwg = jnp.zeros((GMLP_BLOCK, GMLP_BLOCK), F32)
            dbg = jnp.zeros((GMLP_BLOCK, 1), F32)
            dvn_rows = []
            for n in range(tm // GMLP_BLOCK):
                rs = slice(n * GMLP_BLOCK, (n + 1) * GMLP_BLOCK)
                sg = jnp.dot(wg, vnb[rs, cs], preferred_element_type=F32) + b_ref[g]
                dp_ref[rs, cs] = (dout[rs, cs] * sg * _gelu_grad(au[rs, cs])).astype(dp_ref.dtype)
                dsg = dout[rs, cs] * u[rs, cs]
                dsgb = _bf(dsg)
                dbg = dbg + jnp.sum(dsg, axis=1, keepdims=True)
                dwg = dwg + lax.dot_general(dsgb, vnb[rs, cs], (((1,), (1,)), ((), ())), preferred_element_type=F32)
                dvn_rows.append(jnp.dot(wgt, dsgb, preferred_element_type=F32))
            dw_ref[g] += jnp.where(mask, dwg, 0.0)
            db_ref[g] += dbg
            dvn_cols.append(jnp.concatenate(dvn_rows, axis=0))
        dvn = jnp.concatenate(dvn_cols, axis=1)
        dv = rstd * (dvn - jnp.mean(dvn, axis=-1, keepdims=True) - vn * jnp.mean(dvn * vn, axis=-1, keepdims=True))
        dp_ref[:, A_WIDTH:] = (dv * _gelu_grad(av)).astype(dp_ref.dtype)

    return pl.pallas_call(
        body, grid=(t // tm,),
        in_specs=[pl.BlockSpec((tm, A_WIDTH), lambda i: (i, 0)), pl.BlockSpec((tm, A_WIDTH), lambda i: (i, 1)),
                  pl.BlockSpec((tm, dxb.shape[1]), lambda i: (i, 0)), pl.BlockSpec(w_out.shape, lambda i: (0, 0)),
                  pl.BlockSpec(w.shape, lambda i: (0, 0, 0)), pl.BlockSpec(b.shape, lambda i: (0, 0, 0))],
        out_specs=[pl.BlockSpec((tm, 2 * A_WIDTH), lambda i: (i, 0)),
                   pl.BlockSpec(w.shape, lambda i: (0, 0, 0)), pl.BlockSpec(b.shape, lambda i: (0, 0, 0))],
        out_shape=[S((t, 2 * A_WIDTH), BF16), S(w.shape, F32), S(b.shape, F32)],
        compiler_params=_cp("arbitrary"), name=name)(proj, proj, dxb, w_out, w, b)


CONV_ROWS = 256
CONV_ROWS_BWD = 64


def conv_fwd(name, proj, w, cb):
    t = proj.shape[0]
    tc = LANES
    rows = _tile(t, CONV_ROWS)
    a_cb, g_cb = 2 * A_WIDTH // tc, (2 * A_WIDTH + B_WIDTH) // tc

    def body(a_ref, g_ref, w_ref, cb_ref, o_ref, hpad):
        hpad[0:CONV_PAD, :] = jnp.zeros((CONV_PAD, tc), F32)

        def fill(i, _):
            r0 = pl.multiple_of(i * rows, rows)
            hpad[pl.ds(CONV_PAD + r0, rows), :] = a_ref[pl.ds(r0, rows), :] * _sigmoid(g_ref[pl.ds(r0, rows), :])
            return 0
        lax.fori_loop(0, t // rows, fill, 0)

        def conv(i, _):
            r0 = pl.multiple_of(i * rows, rows)
            win = hpad[pl.ds(r0, rows + CONV_PAD), :]
            acc = jnp.zeros((rows, tc), F32) + cb_ref[...]
            for b in range(SUB):
                wb = win if b == 0 else pltpu.roll(win, b, 0)
                for a in range(CONV_PAD // SUB):
                    k = CONV_WIDTH - 1 - (SUB * a + b)
                    if k >= 0:
                        lo = CONV_PAD - SUB * a
                        acc = acc + wb[lo:lo + rows, :] * w_ref[k:k + 1, :]
            o_ref[pl.ds(r0, rows), :] = acc
            return 0
        lax.fori_loop(0, t // rows, conv, 0)

    return pl.pallas_call(
        body, grid=(B_WIDTH // tc,),
        in_specs=[pl.BlockSpec((t, tc), lambda j: (0, a_cb + j)), pl.BlockSpec((t, tc), lambda j: (0, g_cb + j)),
                  pl.BlockSpec((CONV_WIDTH, tc), lambda j: (0, j)), pl.BlockSpec((1, tc), lambda j: (0, j))],
        out_specs=pl.BlockSpec((t, tc), lambda j: (0, j)), out_shape=S((t, B_WIDTH), F32),
        scratch_shapes=[pltpu.VMEM((t + CONV_PAD, tc), F32)],
        compiler_params=_cp("parallel"), name=name)(proj, proj, w, cb)


def conv_bwd(name, proj, dhc, w):
    t = proj.shape[0]
    tc = LANES
    rows = _tile(t, CONV_ROWS_BWD)
    a_cb, g_cb = 2 * A_WIDTH // tc, (2 * A_WIDTH + B_WIDTH) // tc
    win_rows = rows + CONV_PAD

    def body(a_ref, g_ref, d_ref, w_ref, da_ref, dg_ref, dw_ref, dcb_ref, hpad, dpad, dwacc):
        hpad[0:CONV_PAD, :] = jnp.zeros((CONV_PAD, tc), F32)
        dpad[t:t + CONV_PAD, :] = jnp.zeros((CONV_PAD, tc), F32)
        dwacc[...] = jnp.zeros(dwacc.shape, F32)

        def fill(i, _):
            r0 = pl.multiple_of(i * rows, rows)
            hpad[pl.ds(CONV_PAD + r0, rows), :] = a_ref[pl.ds(r0, rows), :] * _sigmoid(g_ref[pl.ds(r0, rows), :])
            dpad[pl.ds(r0, rows), :] = d_ref[pl.ds(r0, rows), :]
            return 0
        lax.fori_loop(0, t // rows, fill, 0)

        def step(i, dcb):
            r0 = pl.multiple_of(i * rows, rows)
            hwin = hpad[pl.ds(r0, win_rows), :]
            dwin = dpad[pl.ds(r0, win_rows), :]
            dchunk = dwin[:rows, :]
            dh = jnp.zeros((rows, tc), F32)
            for b in range(SUB):
                hb = hwin if b == 0 else pltpu.roll(hwin, b, 0)
                db = dwin if b == 0 else pltpu.roll(dwin, win_rows - b, 0)
                for a in range(CONV_PAD // SUB):
                    k = CONV_WIDTH - 1 - (SUB * a + b)
                    if k >= 0:
                        dh = dh + db[SUB * a:SUB * a + rows, :] * w_ref[k:k + 1, :]
                        lo = CONV_PAD - SUB * a
                        prod = dchunk * hb[lo:lo + rows, :]
                        dwacc[k] += jnp.sum(prod.reshape(rows // 8, 8, tc), axis=0)
            a = a_ref[pl.ds(r0, rows), :]
            sg = _sigmoid(g_ref[pl.ds(r0, rows), :])
            da_ref[pl.ds(r0, rows), :] = (dh * sg).astype(da_ref.dtype)
            dg_ref[pl.ds(r0, rows), :] = (dh * a * sg * (1.0 - sg)).astype(dg_ref.dtype)
            return dcb + jnp.sum(dchunk, axis=0, keepdims=True)
        dcb = lax.fori_loop(0, t // rows, step, jnp.zeros((1, tc), F32))
        dcb_ref[...] = dcb
        for k in range(CONV_WIDTH):
            dw_ref[k:k + 1, :] = jnp.sum(dwacc[k], axis=0, keepdims=True)

    return pl.pallas_call(
        body, grid=(B_WIDTH // tc,),
        in_specs=[pl.BlockSpec((t, tc), lambda j: (0, a_cb + j)), pl.BlockSpec((t, tc), lambda j: (0, g_cb + j)),
                  pl.BlockSpec((t, tc), lambda j: (0, j)), pl.BlockSpec((CONV_WIDTH, tc), lambda j: (0, j))],
        out_specs=[pl.BlockSpec((t, tc), lambda j: (0, j)), pl.BlockSpec((t, tc), lambda j: (0, j)),
                   pl.BlockSpec((CONV_WIDTH, tc), lambda j: (0, j)), pl.BlockSpec((1, tc), lambda j: (0, j))],
        out_shape=[S((t, B_WIDTH), BF16), S((t, B_WIDTH), BF16), S((CONV_WIDTH, B_WIDTH), F32), S((1, B_WIDTH), F32)],
        scratch_shapes=[pltpu.VMEM((t + CONV_PAD, tc), F32), pltpu.VMEM((t + CONV_PAD, tc), F32),
                        pltpu.VMEM((CONV_WIDTH, 8, tc), F32)],
        compiler_params=_cp("parallel"), name=name)(proj, proj, dhc, w)


def ln_silu_bwd(name, hc, dxb, w_out, g, b):
    c = hc.shape[1]

    def fn(h, dxv, wv, gv, bv):
        dout = lax.dot_general(dxv, wv[A_WIDTH:, :], _NT, preferred_element_type=F32)
        y, rstd = _ln_plain(h)
        z = y * gv + bv
        s = _sigmoid(z)
        dz = dout * s * (1.0 + z * (1.0 - s))
        dyv = dz * gv
        dh = rstd * (dyv - jnp.mean(dyv, axis=-1, keepdims=True) - y * jnp.mean(dyv * y, axis=-1, keepdims=True))
        return [dh], [jnp.sum(dz * y, axis=0, keepdims=True), jnp.sum(dz, axis=0, keepdims=True)]

    return rows_call(name, fn, [hc, dxb], [w_out, g, b], [(c, F32)], [(1, c), (1, c)])


_NT = (((1,), (1,)), ((), ()))
_TN = (((0,), (0,)), ((), ()))


def attn_fwd(name, x, gain, wq, k, v, wo, tm=512):
    t, d = x.shape
    m = k.shape[0]
    tm = _tile(t, tm)
    scale = CA_HEAD_DIM ** -0.5

    def body(x_ref, g_ref, wq_ref, k_ref, v_ref, wo_ref, x1_ref, xn_ref, r_ref, q_ref, o_ref):
        xv = x_ref[...]
        rv = lax.rsqrt(jnp.mean(xv * xv, axis=-1, keepdims=True) + EPS)
        xn = (xv * rv * g_ref[...]).astype(BF16)
        xn_ref[...] = xn
        r_ref[...] = rv
        q_ref[...] = jnp.dot(xn, wq_ref[...], preferred_element_type=F32).astype(BF16)
        for h in range(CA_HEADS):
            cs = slice(h * CA_HEAD_DIM, (h + 1) * CA_HEAD_DIM)
            s = lax.dot_general(q_ref[:, cs], k_ref[:, cs], _NT, preferred_element_type=F32) * scale
            e = jnp.exp(s - jnp.max(s, axis=-1, keepdims=True))
            p = e / jnp.sum(e, axis=-1, keepdims=True)
            o_ref[:, cs] = jnp.dot(_bf(p), v_ref[:, cs], preferred_element_type=F32).astype(o_ref.dtype)
        x1_ref[...] = xv + jnp.dot(o_ref[...], wo_ref[...], preferred_element_type=F32)

    def whole(a):
        return pl.BlockSpec(a.shape, lambda i: (0, 0), pipeline_mode=pl.Buffered(1))

    rows = pl.BlockSpec((tm, d), lambda i: (i, 0))
    col = pl.BlockSpec((tm, 1), lambda i: (i, 0))
    return pl.pallas_call(
        body, grid=(t // tm,),
        in_specs=[rows, whole(gain), whole(wq), whole(k), whole(v), whole(wo)],
        out_specs=[rows, rows, col, rows, rows],
        out_shape=[S((t, d), F32), S((t, d), BF16), S((t, 1), F32), S((t, d), BF16), S((t, d), BF16)],
        compiler_params=_cp("parallel"), name=name)(x, gain, wq, k, v, wo)


def attn_bwd(name, dx, dxb, x, r, gain, q, k, v, wq, wo, tm=512):
    t, d = q.shape
    m = k.shape[0]
    tm = _tile(t, tm)
    scale = CA_HEAD_DIM ** -0.5

    def body(dx_ref, dxb_ref, x_ref, r_ref, g_ref, q_ref, k_ref, v_ref, wq_ref, wo_ref,
             dxo_ref, dxbo_ref, dq_ref, dk_ref, dv_ref, dg_ref, do_s):
        @pl.when(pl.program_id(0) == 0)
        def _():
            dk_ref[...] = jnp.zeros(dk_ref.shape, F32)
            dv_ref[...] = jnp.zeros(dv_ref.shape, F32)
            dg_ref[...] = jnp.zeros(dg_ref.shape, F32)

        do_s[...] = lax.dot_general(dxb_ref[...], wo_ref[...], _NT, preferred_element_type=F32).astype(BF16)
        for h in range(CA_HEADS):
            cs = slice(h * CA_HEAD_DIM, (h + 1) * CA_HEAD_DIM)
            qh, kh, vh, doh = q_ref[:, cs], k_ref[:, cs], v_ref[:, cs], do_s[:, cs]
            s = lax.dot_general(qh, kh, _NT, preferred_element_type=F32) * scale
            e = jnp.exp(s - jnp.max(s, axis=-1, keepdims=True))
            p = e / jnp.sum(e, axis=-1, keepdims=True)
            pb = _bf(p)
            dv_ref[:, cs] += lax.dot_general(pb, doh, _TN, preferred_element_type=F32)
            dp = lax.dot_general(doh, vh, _NT, preferred_element_type=F32)
            ds = _bf(p * (dp - jnp.sum(dp * p, axis=-1, keepdims=True)) * scale)
            dq_ref[:, cs] = jnp.dot(ds, kh, preferred_element_type=F32).astype(dq_ref.dtype)
            dk_ref[:, cs] += lax.dot_general(ds, qh, _TN, preferred_element_type=F32)
        dxn = lax.dot_general(dq_ref[...], wq_ref[...], _NT, preferred_element_type=F32)
        xh = x_ref[...] * r_ref[...]
        wv = dxn * g_ref[...]
        dxo = dx_ref[...] + r_ref[...] * (wv - xh * jnp.mean(wv * xh, axis=-1, keepdims=True))
        dxo_ref[...] = dxo
        dxbo_ref[...] = dxo.astype(BF16)
        dg_ref[...] += jnp.sum(dxn * xh, axis=0, keepdims=True)

    def whole(a):
        return pl.BlockSpec(a.shape, lambda i: (0, 0), pipeline_mode=pl.Buffered(1))

    rows = pl.BlockSpec((tm, d), lambda i: (i, 0))
    col = pl.BlockSpec((tm, 1), lambda i: (i, 0))
    acc = pl.BlockSpec((m, d), lambda i: (0, 0))
    return pl.pallas_call(
        body, grid=(t // tm,),
        in_specs=[rows, rows, rows, col, whole(gain), rows, whole(k), whole(v), whole(wq), whole(wo)],
        out_specs=[rows, rows, rows, acc, acc, pl.BlockSpec((1, d), lambda i: (0, 0))],
        out_shape=[S((t, d), F32), S((t, d), BF16), S((t, d), BF16), S((m, d), F32), S((m, d), F32), S((1, d), F32)],
        scratch_shapes=[pltpu.VMEM((tm, d), BF16)],
        compiler_params=_cp("arbitrary"), name=name)(dx, dxb, x, r, gain, q, k, v, wq, wo)


SUB = 8
S5_ROWS = 256


S5_BLOCKS = 4
BLOCK_CH = C_WIDTH // S5_BLOCKS
BLOCK_ST = N_STATE // S5_BLOCKS
_S5_BLOCKS = tuple((slice(BLOCK_CH * q, BLOCK_CH * (q + 1)), slice(BLOCK_ST * q, BLOCK_ST * (q + 1)),
                    slice(N_STATE + BLOCK_ST * q, N_STATE + BLOCK_ST * (q + 1))) for q in range(S5_BLOCKS))
_HI = lax.Precision.HIGHEST
_GP = (C_GROUPS, C_STATE)
_RP = (C_WIDTH, C_STATE)


def _zoh(lr, li, ldt):
    dt = jnp.exp(ldt)
    mag = jnp.exp(lr * dt)
    ar = mag * jnp.cos(li * dt)
    ai = mag * jnp.sin(li * dt)
    den = lr * lr + li * li
    qr = ((ar - 1.0) * lr + ai * li) / den
    qi = (ai * lr - (ar - 1.0) * li) / den
    return dt, ar, ai, den, qr, qi


def _per_channel(v):
    return jnp.broadcast_to(v[:, None, :], (C_GROUPS, C_GROUP_CH, C_STATE)).reshape(_RP)


def _same_group(shape, row_per_group, col_per_group):
    rows = lax.broadcasted_iota(jnp.int32, shape, 0) // row_per_group
    cols = lax.broadcasted_iota(jnp.int32, shape, 1) // col_per_group
    return rows == cols


def _spread(shape, axis):
    long = lax.broadcasted_iota(jnp.int32, shape, axis) % C_STATE
    short = lax.broadcasted_iota(jnp.int32, shape, 1 - axis)
    return long == short


def s5_discretise(name, lam_re, lam_im, log_dt, bt_re, bt_im):
    def body(lr_ref, li_ref, ldt_ref, btr_ref, bti_ref, a_ref, bbr_ref, bbi_ref):
        _, ar, ai, _, qr, qi = _zoh(lr_ref[...], li_ref[...], ldt_ref[...])
        a_ref[0] = ar
        a_ref[1] = ai
        q2r, q2i = _per_channel(qr), _per_channel(qi)
        btr, bti = btr_ref[...], bti_ref[...]
        bbr_ref[...] = q2r * btr - q2i * bti
        bbi_ref[...] = q2r * bti + q2i * btr

    return pl.pallas_call(body, out_shape=[S((2,) + _GP, F32), S(_RP, F32), S(_RP, F32)],
                          name=name)(lam_re, lam_im, log_dt, bt_re, bt_im)


def s5_operands(name, a, bbr, bbi, c2r, c2i, ctr, cti):
    ns = N_STATE

    def body(a_ref, bbr_ref, bbi_ref, c2r_ref, c2i_ref, ctr_ref, cti_ref, pw_ref, qw_ref, mb_ref, mc_ref, mct_ref):
        ar, ai = a_ref[0:1, :], a_ref[1:2, :]
        pows = [(ar, ai)]
        for _ in range(SUB - 1):
            pr, pi = pows[-1]
            pows.append((pr * ar - pi * ai, pr * ai + pi * ar))
        rows = lax.broadcasted_iota(jnp.int32, (SUB, ns), 0)

        def rows_of(v):
            return jnp.broadcast_to(v, (SUB, ns))

        for k, s in enumerate((1, 2, 4)):
            pr, pi = rows_of(pows[s - 1][0]), rows_of(pows[s - 1][1])
            pw_ref[k, 0] = jnp.where(rows >= s, pr, 0.0)
            pw_ref[k, 1] = jnp.where(rows >= s, pi, 0.0)
            qw_ref[k, 0] = jnp.where(rows + s <= SUB - 1, pr, 0.0)
            qw_ref[k, 1] = jnp.where(rows + s <= SUB - 1, -pi, 0.0)
        fr = fi = br = bi = jnp.zeros((SUB, ns), F32)
        for i in range(SUB):
            fr = jnp.where(rows == i, rows_of(pows[i][0]), fr)
            fi = jnp.where(rows == i, rows_of(pows[i][1]), fi)
            br = jnp.where(rows == i, rows_of(pows[SUB - 1 - i][0]), br)
            bi = jnp.where(rows == i, rows_of(-pows[SUB - 1 - i][1]), bi)
        pw_ref[3, 0], pw_ref[3, 1], qw_ref[3, 0], qw_ref[3, 1] = fr, fi, br, bi

        wide = _spread((C_STATE, ns), 1).astype(BF16)
        tall = _spread((ns, C_STATE), 0).astype(BF16)
        in_rows = _same_group((C_WIDTH, ns), C_GROUP_CH, C_STATE)
        in_cols = _same_group((ns, C_WIDTH), C_STATE, C_GROUP_CH)

        def across(v, sign=1.0):
            return jnp.where(in_rows, sign * jnp.dot(_bf(v), wide, preferred_element_type=F32), 0.0).astype(BF16)

        def down(vt, sign=1.0):
            return jnp.where(in_cols, sign * jnp.dot(tall, _bf(vt), preferred_element_type=F32), 0.0).astype(BF16)

        mb_ref[:, 0:ns] = across(bbr_ref[...])
        mb_ref[:, ns:2 * ns] = across(bbi_ref[...])
        mct_ref[:, 0:ns] = across(c2r_ref[...])
        mct_ref[:, ns:2 * ns] = across(c2i_ref[...], -1.0)
        mc_ref[0:ns, :] = down(ctr_ref[...])
        mc_ref[ns:2 * ns, :] = down(cti_ref[...], -1.0)

    return pl.pallas_call(
        body, out_shape=[S((4, 2, SUB, ns), F32), S((4, 2, SUB, ns), F32), S((C_WIDTH, 2 * ns), BF16),
                         S((2 * ns, C_WIDTH), BF16), S((C_WIDTH, 2 * ns), BF16)],
        compiler_params=pltpu.CompilerParams(vmem_limit_bytes=VMEM_LIMIT), name=name)(a, bbr, bbi, c2r, c2i, ctr, cti)


def s5_block_grads(name, u, lamb, xsb, dyb):
    t = u.shape[0]

    def mb_body(u_ref, lr_ref, li_ref, o_ref):
        ub = _bf(u_ref[...])
        o_ref[:, 0:BLOCK_ST] = lax.dot_general(ub, lr_ref[...], _TN, preferred_element_type=F32)
        o_ref[:, BLOCK_ST:2 * BLOCK_ST] = lax.dot_general(ub, li_ref[...], _TN, preferred_element_type=F32)

    d_mb = pl.pallas_call(
        mb_body, grid=(S5_BLOCKS,),
        in_specs=[pl.BlockSpec((t, BLOCK_CH), lambda q: (0, q)), pl.BlockSpec((t, BLOCK_ST), lambda q: (0, q)),
                  pl.BlockSpec((t, BLOCK_ST), lambda q: (0, S5_BLOCKS + q))],
        out_specs=pl.BlockSpec((BLOCK_CH, 2 * BLOCK_ST), lambda q: (q, 0)), out_shape=S((C_WIDTH, 2 * BLOCK_ST), F32),
        compiler_params=_cp("parallel"), name=name + "_b")(u, lamb, lamb)

    def mc_body(x_ref, dy_ref, o_ref):
        o_ref[...] = lax.dot_general(x_ref[...], dy_ref[...], _TN, preferred_element_type=F32)

    d_mc = pl.pallas_call(
        mc_body, grid=(2, S5_BLOCKS),
        in_specs=[pl.BlockSpec((t, BLOCK_ST), lambda p, q: (0, p * S5_BLOCKS + q)), pl.BlockSpec((t, BLOCK_CH), lambda p, q: (0, q))],
        out_specs=pl.BlockSpec((BLOCK_ST, BLOCK_CH), lambda p, q: (p * S5_BLOCKS + q, 0)),
        out_shape=S((2 * N_STATE, BLOCK_CH), F32), compiler_params=_cp("parallel", "parallel"), name=name + "_c")(xsb, dyb)
    return d_mb, d_mc


def s5_param_grads(name, d_mb, d_mc, da, lam_re, lam_im, log_dt, bt_re, bt_im):
    ns = N_STATE

    def body(dmb_ref, dmc_ref, da_ref, lr_ref, li_ref, ldt_ref, btr_ref, bti_ref,
             glr_ref, gli_ref, gdt_ref, gbr_ref, gbi_ref, gcr_ref, gci_ref):
        lr, li = lr_ref[...], li_ref[...]
        dt, ar, ai, den, qr, qi = _zoh(lr, li, ldt_ref[...])
        per_block = C_GROUPS // S5_BLOCKS
        wide = _spread((C_STATE, BLOCK_ST), 1).astype(F32)
        tall = _spread((BLOCK_ST, C_STATE), 0).astype(F32)
        rows = lax.broadcasted_iota(jnp.int32, (C_WIDTH, BLOCK_ST), 0) // C_GROUP_CH % per_block
        in_rows = rows == lax.broadcasted_iota(jnp.int32, (C_WIDTH, BLOCK_ST), 1) // C_STATE
        in_cols = _same_group((BLOCK_ST, BLOCK_CH), C_STATE, C_GROUP_CH)

        def fold_rows(v):
            return lax.dot_general(jnp.where(in_rows, v, 0.0), wide, (((1,), (1,)), ((), ())), precision=_HI,
                                   preferred_element_type=F32)

        def fold_cols(v):
            return lax.dot_general(jnp.where(in_cols, v, 0.0), tall, (((0,), (0,)), ((), ())), precision=_HI,
                                   preferred_element_type=F32)

        for cs, s_re, s_im in _S5_BLOCKS:
            gcr_ref[cs, :] = fold_cols(dmc_ref[s_re, :])
            gci_ref[cs, :] = -fold_cols(dmc_ref[s_im, :])
        gbbr = fold_rows(dmb_ref[:, 0:BLOCK_ST])
        gbbi = fold_rows(dmb_ref[:, BLOCK_ST:2 * BLOCK_ST])
        btr, bti = btr_ref[...], bti_ref[...]
        q2r, q2i = _per_channel(qr), _per_channel(qi)
        gbr_ref[...] = q2r * gbbr + q2i * gbbi
        gbi_ref[...] = q2r * gbbi - q2i * gbbr

        def per_group(v):
            return jnp.sum(v.reshape(C_GROUPS, C_GROUP_CH, C_STATE), axis=1)

        gqr = per_group(btr * gbbr + bti * gbbi)
        gqi = per_group(btr * gbbi - bti * gbbr)
        ilr, ili = lr / den, li / den
        gar = da_ref[0] + ilr * gqr - ili * gqi
        gai = da_ref[1] + ilr * gqi + ili * gqr
        sr = (qr * lr + qi * li) / den
        si = (qi * lr - qr * li) / den
        gzr = ar * gar + ai * gai
        gzi = ar * gai - ai * gar
        glr_ref[...] = -sr * gqr - si * gqi + dt * gzr
        gli_ref[...] = -sr * gqi + si * gqr + dt * gzi
        gdt_ref[...] = jnp.sum(lr * gzr + li * gzi, axis=1, keepdims=True) * dt

    return pl.pallas_call(
        body, out_shape=[S(_GP, F32), S(_GP, F32), S((C_GROUPS, 1), F32), S(_RP, F32), S(_RP, F32), S(_RP, F32), S(_RP, F32)],
        compiler_params=pltpu.CompilerParams(vmem_limit_bytes=VMEM_LIMIT), name=name,
    )(d_mb, d_mc, da, lam_re, lam_im, log_dt, bt_re, bt_im)


def _cmul_add(xr, xi, pr, pi, zr, zi):
    return xr + pr * zr - pi * zi, xi + pr * zi + pi * zr


def s5_fwd(name, x, gain, w_in, mb, mc, pw, dskip, w_out_t):
    t, d = x.shape
    tm = _tile(t, S5_ROWS)
    ns = N_STATE

    def body(x_ref, g_ref, wi_ref, mb_ref, mc_ref, pw_ref, d_ref, wo_ref,
             x1_ref, hn_ref, r_ref, u_ref, gy_ref, y_ref, xs_ref, xb_ref, o1_ref, o2_ref, carry):
        @pl.when(pl.program_id(0) == 0)
        def _():
            carry[...] = jnp.zeros(carry.shape, F32)

        xv = x_ref[...]
        rv = lax.rsqrt(jnp.mean(xv * xv, axis=-1, keepdims=True) + EPS)
        hn = (xv * rv * g_ref[...]).astype(BF16)
        hn_ref[...] = hn
        r_ref[...] = rv
        uv = jnp.dot(hn, wi_ref[...], preferred_element_type=F32)
        u_ref[...] = uv
        ub = _bf(uv)
        for cs, s_re, s_im in _S5_BLOCKS:
            xs_ref[:, s_re] = jnp.dot(ub[:, cs], mb_ref[cs, s_re], preferred_element_type=F32)
            xs_ref[:, s_im] = jnp.dot(ub[:, cs], mb_ref[cs, s_im], preferred_element_type=F32)

        def group(i, _):
            r0 = pl.multiple_of(i * SUB, SUB)
            xr = xs_ref[pl.ds(r0, SUB), 0:ns]
            xi = xs_ref[pl.ds(r0, SUB), ns:2 * ns]
            for k, s in enumerate((1, 2, 4)):
                xr, xi = _cmul_add(xr, xi, pw_ref[k, 0], pw_ref[k, 1], pltpu.roll(xr, s, 0), pltpu.roll(xi, s, 0))
            xr, xi = _cmul_add(xr, xi, pw_ref[3, 0], pw_ref[3, 1], carry[0], carry[1])
            xs_ref[pl.ds(r0, SUB), 0:ns] = xr
            xs_ref[pl.ds(r0, SUB), ns:2 * ns] = xi
            carry[0] = jnp.broadcast_to(xr[SUB - 1:SUB, :], (SUB, ns))
            carry[1] = jnp.broadcast_to(xi[SUB - 1:SUB, :], (SUB, ns))
            return 0
        lax.fori_loop(0, tm // SUB, group, 0)

        xb_ref[...] = _bf(xs_ref[...])
        for cs, s_re, s_im in _S5_BLOCKS:
            y = (jnp.dot(xb_ref[:, s_re], mc_ref[s_re, cs], preferred_element_type=F32)
                 + jnp.dot(xb_ref[:, s_im], mc_ref[s_im, cs], preferred_element_type=F32) + d_ref[:, cs] * uv[:, cs])
            y_ref[:, cs] = y
            gy_ref[:, cs] = _gelu(y).astype(gy_ref.dtype)
        o1 = lax.dot_general(gy_ref[...], wo_ref[0:d, :], _NT, preferred_element_type=F32)
        o2 = lax.dot_general(gy_ref[...], wo_ref[d:2 * d, :], _NT, preferred_element_type=F32)
        o1_ref[...] = o1.astype(BF16)
        o2_ref[...] = o2.astype(BF16)
        x1_ref[...] = xv + o1 * _sigmoid(o2)

    c = w_in.shape[1]
    rows = pl.BlockSpec((tm, d), lambda i: (i, 0))
    narrow = pl.BlockSpec((tm, c), lambda i: (i, 0))
    states = pl.BlockSpec((tm, 2 * ns), lambda i: (i, 0))
    return pl.pallas_call(
        body, grid=(t // tm,),
        in_specs=[rows, _whole(gain), _whole(w_in), _whole(mb), _whole(mc), _whole(pw), _whole(dskip), _whole(w_out_t)],
        out_specs=[rows, rows, pl.BlockSpec((tm, 1), lambda i: (i, 0)), narrow, narrow, narrow, states, states, rows, rows],
        out_shape=[S((t, d), F32), S((t, d), BF16), S((t, 1), F32), S((t, c), F32), S((t, c), BF16), S((t, c), F32),
                   S((t, 2 * ns), F32), S((t, 2 * ns), BF16), S((t, d), BF16), S((t, d), BF16)],
        scratch_shapes=[pltpu.VMEM((2, SUB, ns), F32)],
        compiler_params=_cp("arbitrary"), name=name)(x, gain, w_in, mb, mc, pw, dskip, w_out_t)


def s5_bwd(name, dgy, y, u, xs, mct, mbt, qw, dskip):
    t, c = u.shape
    tm = _tile(t, S5_ROWS)
    nt = t // tm
    ns = N_STATE
    ng = tm // SUB

    def body(dgy_ref, y_ref, u_ref, xs_ref, mct_ref, mbt_ref, qw_ref, d_ref,
             du_ref, dy_ref, lb_ref, da_ref, dd_ref, lam, carry):
        @pl.when(pl.program_id(0) == 0)
        def _():
            carry[...] = jnp.zeros(carry.shape, F32)
            da_ref[...] = jnp.zeros(da_ref.shape, F32)
            dd_ref[...] = jnp.zeros(dd_ref.shape, F32)

        uv = u_ref[...]
        dy = dgy_ref[...] * _gelu_grad(y_ref[...])
        dyb = _bf(dy)
        dy_ref[...] = dyb
        dd_ref[...] += jnp.sum(dy * uv, axis=0, keepdims=True)
        for cs, s_re, s_im in _S5_BLOCKS:
            lam[:, s_re] = jnp.dot(dyb[:, cs], mct_ref[cs, s_re], preferred_element_type=F32)
            lam[:, s_im] = jnp.dot(dyb[:, cs], mct_ref[cs, s_im], preferred_element_type=F32)
        last_row = lax.broadcasted_iota(jnp.int32, (SUB, ns), 0) == SUB - 1

        def group(j, _):
            i = ng - 1 - j
            r0 = pl.multiple_of(i * SUB, SUB)
            lr = lam[pl.ds(r0, SUB), 0:ns]
            li = lam[pl.ds(r0, SUB), ns:2 * ns]
            for k, s in enumerate((1, 2, 4)):
                lr, li = _cmul_add(lr, li, qw_ref[k, 0], qw_ref[k, 1],
                                   pltpu.roll(lr, SUB - s, 0), pltpu.roll(li, SUB - s, 0))
            cr, ci = carry[0], carry[1]
            lr, li = _cmul_add(lr, li, qw_ref[3, 0], qw_ref[3, 1], cr, ci)
            lam[pl.ds(r0, SUB), 0:ns] = lr
            lam[pl.ds(r0, SUB), ns:2 * ns] = li
            carry[0] = jnp.broadcast_to(lr[0:1, :], (SUB, ns))
            carry[1] = jnp.broadcast_to(li[0:1, :], (SUB, ns))
            nr = jnp.where(last_row, cr, pltpu.roll(lr, SUB - 1, 0))
            ni = jnp.where(last_row, ci, pltpu.roll(li, SUB - 1, 0))
            xr = xs_ref[pl.ds(r0, SUB), 0:ns]
            xi = xs_ref[pl.ds(r0, SUB), ns:2 * ns]
            da_ref[0] += nr * xr + ni * xi
            da_ref[1] += ni * xr - nr * xi
            return 0
        lax.fori_loop(0, ng, group, 0)

        lb_ref[...] = _bf(lam[...])
        for cs, s_re, s_im in _S5_BLOCKS:
            du = (jnp.dot(lb_ref[:, s_re], mbt_ref[s_re, cs], preferred_element_type=F32)
                  + jnp.dot(lb_ref[:, s_im], mbt_ref[s_im, cs], preferred_element_type=F32) + d_ref[:, cs] * dy[:, cs])
            du_ref[:, cs] = du.astype(du_ref.dtype)

    rev = lambda i: (nt - 1 - i, 0)
    return pl.pallas_call(
        body, grid=(nt,),
        in_specs=[pl.BlockSpec((tm, c), rev), pl.BlockSpec((tm, c), rev), pl.BlockSpec((tm, c), rev),
                  pl.BlockSpec((tm, 2 * ns), rev),
                  pl.BlockSpec(mct.shape, lambda i: (0, 0)), pl.BlockSpec(mbt.shape, lambda i: (0, 0)),
                  pl.BlockSpec(qw.shape, lambda i: (0, 0, 0, 0)), pl.BlockSpec((1, c), lambda i: (0, 0))],
        out_specs=[pl.BlockSpec((tm, c), rev), pl.BlockSpec((tm, c), rev), pl.BlockSpec((tm, 2 * ns), rev),
                   pl.BlockSpec((2, SUB, ns), lambda i: (0, 0, 0)), pl.BlockSpec((1, c), lambda i: (0, 0))],
        out_shape=[S((t, c), BF16), S((t, c), BF16), S((t, 2 * ns), BF16), S((2, SUB, ns), F32), S((1, c), F32)],
        scratch_shapes=[pltpu.VMEM((tm, 2 * ns), F32), pltpu.VMEM((2, SUB, ns), F32)],
        compiler_params=_cp("arbitrary"), name=name)(dgy, y, u, xs, mct, mbt, qw, dskip)


def _first(accs, *_):
    return [accs[0]]


def _rms_bwd_epi(accs, xv, base, rv, g):
    dv = accs[0]
    w = dv * g
    xh = xv * rv
    dx = base + rv * (w - xh * jnp.mean(w * xh, axis=-1, keepdims=True))
    return [dx, dx, jnp.sum(dv * xh, axis=0, keepdims=True)]


def mm_rms_bwd(name, pairs, x, r, gain, dres):
    t, d = x.shape
    return mm_nn(name, t, d, pairs, 1, _rms_bwd_epi, [F32, BF16], tiled=[x, dres], cols=[r], rowv=[gain], sums=[(1, d)])


def even_fwd(x, w, need_out):
    t = x.shape[0]
    proj, hn, r = mm_nn("e_in_f", t, IN_WIDTH, [(x, w["e_w_in_t"], 0, "t")], 1, _first, [F32], norm_gain=w["e_norm"])
    hc = conv_fwd("e_conv_f", proj, w["e_conv_w"], w["e_conv_b"])
    need_out(hc)
    x1, out_a, out_b = even_out_fwd("e_out_f", proj, hc, x, w["e_gmlp_w"], w["e_gmlp_b"], w["e_conv_ln_g"], w["e_conv_ln_b"],
                                    w["e_w_out"])
    return x1, (x, hn, r, proj, out_a, hc, out_b)


def even_bwd_mixers(dxb, saved, w):
    x, hn, r, proj, out_a, hc, out_b = saved
    t = x.shape[0]
    g_w_out = jnp.concatenate([mm_tn("e_out_wa", out_a, dxb), mm_tn("e_out_wb", out_b, dxb)], axis=0)
    dab, g_gw, g_gb = gmlp_bwd("e_gmlp_b", proj, dxb, w["e_w_out"], w["e_gmlp_w"], w["e_gmlp_b"])
    dhc, g_lg, g_lb = ln_silu_bwd("e_ln_b", hc, dxb, w["e_w_out"], w["e_conv_ln_g"], w["e_conv_ln_b"])
    dba, dbg, g_cw, g_cb = conv_bwd("e_conv_b", proj, dhc, w["e_conv_w"])
    g_w_in_t = jnp.concatenate([mm_tn("e_in_w0", dab, hn), mm_tn("e_in_w1", dba, hn), mm_tn("e_in_w2", dbg, hn)], axis=0)
    grads = dict(e_w_in_t=g_w_in_t, e_gmlp_w=g_gw[None], e_gmlp_b=g_gb.reshape(1, A_GROUPS, GMLP_BLOCK),
                 e_conv_w=g_cw[None], e_conv_b=g_cb, e_conv_ln_g=g_lg, e_conv_ln_b=g_lb, e_w_out=g_w_out)
    return (dab, dba, dbg), grads


def even_bwd_input(dx, dproj, saved, w):
    x, _, r = saved[:3]
    dab, dba, dbg = dproj
    w_in_t = w["e_w_in_t"]
    return mm_rms_bwd("e_in_b", [(dab, (w_in_t, 0), 0), (dba, (w_in_t, 2), 0), (dbg, (w_in_t, 3), 0)], x, r, w["e_norm"], dx)


def s5_setup(w, anchor=None):
    def rows(v):
        return v.transpose(0, 2, 1).reshape(_RP)

    log_dt = w["o_log_dt"].reshape(C_GROUPS, 1)
    if anchor is not None:
        log_dt = log_dt + anchor
    lam = (w["o_lam_re"], w["o_lam_im"], log_dt, rows(w["o_b_re"]), rows(w["o_b_im"]))
    a, bbr, bbi = s5_discretise("o_s5_zoh", *lam)
    c_re, c_im = w["o_c_re"], w["o_c_im"]
    pw, qw, mb, mc, mct = s5_operands("o_s5_ops", a.reshape(2, N_STATE), bbr, bbi, c_re.reshape(_RP), c_im.reshape(_RP),
                                      c_re.transpose(2, 0, 1).reshape(C_STATE, C_WIDTH),
                                      c_im.transpose(2, 0, 1).reshape(C_STATE, C_WIDTH))
    return dict(lam=lam, pw=pw, qw=qw, mb=mb, mc=mc, mct=mct, mbt=mb.T)


def odd_fwd(x, w, consts):
    x1, hn, r, u, gy, y, xs, xsb, o1, o2 = s5_fwd("o_s5_f", x, w["o_norm"], w["o_w_in"], consts["mb"], consts["mc"],
                                                  consts["pw"], w["o_d"], w["o_w_out_t"])
    return x1, (x, hn, r, u, gy, y, xs, xsb, o1, o2)


def odd_bwd(dx, dxb, saved, w, consts):
    x, hn, r, u, gy, y, xs, xsb, o1, o2 = saved
    t = x.shape[0]

    def gate_bwd(dv, a, b, wv):
        a = a.astype(F32)
        sg = _sigmoid(b.astype(F32))
        do12 = jnp.concatenate([dv * sg, dv * a * sg * (1.0 - sg)], axis=1).astype(BF16)
        return [do12, jnp.dot(do12, wv, preferred_element_type=F32)], []

    do12, dgy = rows_call("o_out_b", gate_bwd, [dx, o1, o2], [w["o_w_out_t"]], [(2 * D_MODEL, BF16), (C_WIDTH, F32)], [])
    g_w_out_t = mm_tn("o_out_w", do12, gy)
    du, dyb, lamb, da8, g_d = s5_bwd("o_s5_b", dgy, y, u, xs, consts["mct"], consts["mbt"], consts["qw"], w["o_d"])
    d_mb, d_mc = s5_block_grads("o_s5_w", u, lamb, xsb, dyb)
    da = jnp.sum(da8, axis=1).reshape((2,) + _GP)
    g_lr, g_li, g_dt, g_btr, g_bti, g_cr, g_ci = s5_param_grads("o_s5_pg", d_mb, d_mc, da, *consts["lam"])

    def states_first(v):
        return v.reshape(C_GROUPS, C_GROUP_CH, C_STATE).transpose(0, 2, 1)[None]

    g_w_in = mm_tn("o_in_w", hn, du)
    dx0, dx0b, g_norm = mm_rms_bwd("o_in_b", [(du, w["o_w_in"], 0, "t")], x, r, w["o_norm"], dx)
    grads = dict(o_norm=g_norm, o_w_in=g_w_in, o_lam_re=g_lr[None], o_lam_im=g_li[None], o_log_dt=g_dt.reshape(1, C_GROUPS),
                 o_b_re=states_first(g_btr), o_b_im=states_first(g_bti),
                 o_c_re=g_cr.reshape((1, C_GROUPS, C_GROUP_CH, C_STATE)), o_c_im=g_ci.reshape((1, C_GROUPS, C_GROUP_CH, C_STATE)),
                 o_d=g_d, o_w_out_t=g_w_out_t)
    return dx0, dx0b, grads


def ca_fwd(i, x, mem, w):
    t, m = x.shape[0], mem.shape[0]
    k, v, mn, rm = mm_nn(f"ca{i}_kv_f", m, D_MODEL, [(mem, w["ca_wk"][i], 0), (mem, w["ca_wv"][i], 1)], 2,
                         lambda accs: [accs[0], accs[1]], [BF16, BF16], norm_gain=w["ca_mem_norm"][i:i + 1])
    x1, xn, r, q, o = attn_fwd(f"ca{i}_attn_f", x, w["ca_norm"][i:i + 1], w["ca_wq"][i], k, v, w["ca_wo"][i])
    return x1, (x, xn, r, mn, rm, q, k, v, o)


def ca_bwd(i, dx, dxb, saved, mem, w):
    x, xn, r, mn, rm, q, k, v, o = saved
    t, m = x.shape[0], mem.shape[0]
    g_wo = mm_tn(f"ca{i}_o_w", o, dxb)
    dx0, dx0b, dq, dk, dv, g_norm = attn_bwd(f"ca{i}_attn_b", dx, dxb, x, r, w["ca_norm"][i:i + 1], q, k, v,
                                             w["ca_wq"][i], w["ca_wo"][i])
    g_wq = mm_tn(f"ca{i}_q_w", xn, dq)
    g_wk = mm_tn(f"ca{i}_k_w", mn, dk)
    g_wv = mm_tn(f"ca{i}_v_w", mn, dv)
    (dmn,) = mm_nn(f"ca{i}_kv_b", m, D_MODEL, [(dk, w["ca_wk"][i], 0, "t"), (dv, w["ca_wv"][i], 0, "t")], 1, _first, [F32])
    g_mnorm = rms_bwd_gain_only(f"ca{i}_mnorm_b", dmn, mem, rm)
    return dx0, dx0b, dict(ca_norm=g_norm, ca_mem_norm=g_mnorm, ca_wq=g_wq, ca_wk=g_wk, ca_wv=g_wv, ca_wo=g_wo)


FFN_ROWS = 512
FFN_CHUNK = 256


def _whole(a):
    return pl.BlockSpec(a.shape, lambda i: (0,) * a.ndim, pipeline_mode=pl.Buffered(1))


def ffn_fused_fwd(name, x, gain, wg_t, wu_t, wd, target=None, final_gain=None):
    t, d = x.shape
    hid = wd.shape[0]
    tm = _tile(t, FFN_ROWS)
    last = target is not None
    n_main = 4 if last else 1

    def body(*refs):
        x_ref, g_ref, wg_ref, wu_ref, wd_ref = refs[:5]
        rest = refs[5:]
        if last:
            tgt_ref, fg_ref = rest[:2]
            rest = rest[2:]
        main, (xn_ref, r_ref, dgate_ref, dup_ref, h_ref) = rest[:n_main], rest[n_main:]
        xv = x_ref[...]
        rv = lax.rsqrt(jnp.mean(xv * xv, axis=-1, keepdims=True) + EPS)
        xn = (xv * rv * g_ref[...]).astype(BF16)
        xn_ref[...] = xn
        r_ref[...] = rv
        for j in range(hid // FFN_CHUNK):
            cs = slice(j * FFN_CHUNK, (j + 1) * FFN_CHUNK)
            g = lax.dot_general(xn, wg_ref[cs, :], _NT, preferred_element_type=F32)
            u = lax.dot_general(xn, wu_ref[cs, :], _NT, preferred_element_type=F32)
            s = _sigmoid(g)
            silu = g * s
            dgate_ref[:, cs] = (u * (s + silu * (1.0 - s))).astype(BF16)
            dup_ref[:, cs] = silu.astype(BF16)
            h_ref[:, cs] = (silu * u).astype(BF16)
        acc = jnp.dot(h_ref[...], wd_ref[...], preferred_element_type=F32)
        if not last:
            main[0][...] = xv + acc
        else:
            dx, _, dgain, part = _final_loss_epi([acc], xv, tgt_ref[...], fg_ref[...])

            @pl.when(pl.program_id(0) == 0)
            def _():
                main[2][...] = jnp.zeros(main[2].shape, F32)
                main[3][...] = jnp.zeros(main[3].shape, F32)
            main[0][...] = dx
            main[1][...] = dx.astype(BF16)
            main[2][...] += dgain
            main[3][...] += part

    rows = pl.BlockSpec((tm, d), lambda i: (i, 0))
    wide = pl.BlockSpec((tm, hid), lambda i: (i, 0))
    col = pl.BlockSpec((tm, 1), lambda i: (i, 0))
    ins, in_specs = [x, gain, wg_t, wu_t, wd], [rows, _whole(gain), _whole(wg_t), _whole(wu_t), _whole(wd)]
    if last:
        ins += [target, final_gain]
        in_specs += [rows, _whole(final_gain)]
        out_specs = [rows, rows, pl.BlockSpec((1, d), lambda i: (0, 0)), pl.BlockSpec((1, 1), lambda i: (0, 0))]
        out_shape = [S((t, d), F32), S((t, d), BF16), S((1, d), F32), S((1, 1), F32)]
    else:
        out_specs, out_shape = [rows], [S((t, d), F32)]
    out_specs += [rows, col, wide, wide, wide]
    out_shape += [S((t, d), BF16), S((t, 1), F32)] + [S((t, hid), BF16)] * 3
    outs = pl.pallas_call(body, grid=(t // tm,), in_specs=in_specs, out_specs=out_specs, out_shape=out_shape,
                          compiler_params=_cp("arbitrary" if last else "parallel"), name=name)(*ins)
    return (tuple(outs[:4]) if last else outs[0]), outs[n_main:]


def ffn_fused_bwd(name, dx, dxb, x, r, gain, dgate, dup, wg_t, wu_t, wd):
    t, d = x.shape
    hid = wd.shape[0]
    tm = _tile(t, FFN_ROWS // 2)

    def body(dx_ref, dxb_ref, x_ref, r_ref, g_ref, dgate_ref, dup_ref, wg_ref, wu_ref, wd_ref,
             dxo_ref, dxbo_ref, dg_ref, du_ref, dgain_ref):
        @pl.when(pl.program_id(0) == 0)
        def _():
            dgain_ref[...] = jnp.zeros(dgain_ref.shape, F32)

        dxb = dxb_ref[...]
        for j in range(hid // FFN_CHUNK):
            cs = slice(j * FFN_CHUNK, (j + 1) * FFN_CHUNK)
            dh = lax.dot_general(dxb, wd_ref[cs, :], _NT, preferred_element_type=F32)
            dg_ref[:, cs] = (dh * dgate_ref[:, cs].astype(F32)).astype(BF16)
            du_ref[:, cs] = (dh * dup_ref[:, cs].astype(F32)).astype(BF16)
        dxn = (jnp.dot(dg_ref[...], wg_ref[...], preferred_element_type=F32)
               + jnp.dot(du_ref[...], wu_ref[...], preferred_element_type=F32))
        dxo, _, dgain = _rms_bwd_epi([dxn], x_ref[...], dx_ref[...], r_ref[...], g_ref[...])
        dxo_ref[...] = dxo
        dxbo_ref[...] = dxo.astype(BF16)
        dgain_ref[...] += dgain

    rows = pl.BlockSpec((tm, d), lambda i: (i, 0))
    wide = pl.BlockSpec((tm, hid), lambda i: (i, 0))
    col = pl.BlockSpec((tm, 1), lambda i: (i, 0))
    return pl.pallas_call(
        body, grid=(t // tm,),
        in_specs=[rows, rows, rows, col, _whole(gain), wide, wide, _whole(wg_t), _whole(wu_t), _whole(wd)],
        out_specs=[rows, rows, wide, wide, pl.BlockSpec((1, d), lambda i: (0, 0))],
        out_shape=[S((t, d), F32), S((t, d), BF16), S((t, hid), BF16), S((t, hid), BF16), S((1, d), F32)],
        compiler_params=_cp("arbitrary"), name=name)(dx, dxb, x, r, gain, dgate, dup, wg_t, wu_t, wd)


def ffn_fwd(i, x, w, target=None):
    out, (xn, r, dgate, dup, h) = ffn_fused_fwd(f"ffn{i}_f", x, w["ffn_norm"][i:i + 1], w["ffn_w_gate_t"][i],
                                                w["ffn_w_up_t"][i], w["ffn_w_down"][i], target,
                                                None if target is None else w["final_norm"])
    return out, (x, xn, r, dgate, dup, h)


def ffn_bwd(i, dx, dxb, saved, w):
    x, xn, r, dgate, dup, h = saved
    g_wd = mm_tn(f"ffn{i}_down_w", h, dxb)
    dx0, dx0b, dg, du, g_norm = ffn_fused_bwd(f"ffn{i}_b", dx, dxb, x, r, w["ffn_norm"][i:i + 1], dgate, dup,
                                              w["ffn_w_gate_t"][i], w["ffn_w_up_t"][i], w["ffn_w_down"][i])
    g_wg_t = mm_tn(f"ffn{i}_gate_w", dg, xn)
    g_wu_t = mm_tn(f"ffn{i}_up_w", du, xn)
    return dx0, dx0b, dict(ffn_norm=g_norm, ffn_w_gate_t=g_wg_t, ffn_w_up_t=g_wu_t, ffn_w_down=g_wd)


def local_step(x, mem, target, w, fetch=None, on_grads=None, anchor=None):
    consts = s5_setup(w, anchor)

    def need(stage, after):
        if fetch is not None:
            for k, v in fetch(stage, after).items():
                if isinstance(k, tuple):
                    w.setdefault(k[0], {})[k[1]] = v
                else:
                    w[k] = v

    need(0, consts["pw"])
    x1, s_e = even_fwd(x, w, lambda after: need(1, after))
    x2, s_c0 = ca_fwd(0, x1, mem, w)
    need(2, x2)
    x3, s_f0 = ffn_fwd(0, x2, w)
    need(3, x3)
    x4, s_o = odd_fwd(x3, w, consts)
    x5, s_c1 = ca_fwd(1, x4, mem, w)
    need(4, x5)
    (dx, dxb, g_final, loss), s_f1 = ffn_fwd(1, x5, w, target)

    def emit(stage, carry, plain, layered=None, layer=0):
        if on_grads is None:
            return carry
        out = dict(plain)
        out.update({(k, layer): v for k, v in (layered or {}).items()})
        return on_grads(stage, out, list(carry))

    dx, dxb, g_f1 = ffn_bwd(1, dx, dxb, s_f1, w)
    dx, dxb = emit(0, (dx, dxb), {}, g_f1, 1)
    dx, dxb, g_c1 = ca_bwd(1, dx, dxb, s_c1, mem, w)
    dx, dxb, g_o = odd_bwd(dx, dxb, s_o, w, consts)
    dx, dxb = emit(1, (dx, dxb), g_o, g_c1, 1)
    dx, dxb, g_f0 = ffn_bwd(0, dx, dxb, s_f0, w)
    dx, dxb = emit(2, (dx, dxb), {}, g_f0, 0)
    dx, dxb, g_c0 = ca_bwd(0, dx, dxb, s_c0, mem, w)
    dx, dxb = emit(3, (dx, dxb), {}, g_c0, 0)
    dproj, g_e = even_bwd_mixers(dxb, s_e, w)
    dproj = emit(4, dproj, {**g_e, "o_norm": g_o["o_norm"], "o_d": g_o["o_d"]})
    dx, dxb, g_e["e_norm"] = even_bwd_input(dx, dproj, s_e, w)

    grads = dict(g_e)
    grads.update(g_o)
    for g0, g1 in ((g_c0, g_c1), (g_f0, g_f1)):
        for k in g0:
            grads[k] = jnp.concatenate([g0[k], g1[k]], axis=0) if k.endswith("norm") else (g0[k], g1[k])
    grads["final_norm"] = g_final
    return loss, dx, grads


def _group(axes):
    pos = {a: lax.axis_index(a) for a in ("x", "y", "c")}
    me = 0
    for a in axes:
        me = me * 2 + pos[a]
    peers = []
    for mask in range(1, 2 ** len(axes)):
        peer = dict(pos)
        for bit, a in enumerate(axes):
            if (mask >> (len(axes) - 1 - bit)) & 1:
                peer[a] = 1 - pos[a]
        idx = 0
        for a in axes:
            idx = idx * 2 + peer[a]
        peers.append((idx, (peer["x"], peer["y"], peer["c"])))
    return me, peers


def _sibling():
    x, y, c = lax.axis_index("x"), lax.axis_index("y"), lax.axis_index("c")
    return c, (x, y, 1 - c)


_HBM =pl.BlockSpec(memory_space=pltpu.HBM)
_SEM = pl.BlockSpec(memory_space=pltpu.SEMAPHORE)
_EFFECT = pltpu.SideEffectType.DATAFLOW_SIDE_EFFECTING


def _gather_peers(direct):
    chip, _ = _group(("x", "y"))
    core = lax.axis_index("c")
    if direct:
        _, peers = _group(_ALL)
        return chip, core, [(idx // 2, idx % 2, dev) for idx, dev in peers]
    _, peers = _group(("x", "y"))
    return chip, core, [(idx, core, dev) for idx, dev in peers]


def gather_ici_start(name, groups, direct):
    flat = [b for g in groups for b in g]
    sizes = [len(g) for g in groups]
    k_ops, n_g = len(flat), len(groups)
    lands = [lax.empty((4, 2) + tuple(b.shape), b.dtype) for b in flat]
    fan = [N_DEV - 1 if d else 3 for d in direct]

    def body(*refs):
        src, land = refs[:k_ops], refs[k_ops:2 * k_ops]
        sems = refs[2 * k_ops:2 * k_ops + 3 * n_g]
        token = refs[-1]
        i = 0
        for g in range(n_g):
            send, recv, loc = sems[3 * g:3 * g + 3]
            chip, core, peers = _gather_peers(direct[g])
            for j in range(sizes[g]):
                pltpu.make_async_copy(src[i], land[i].at[chip, core], loc.at[j]).start()
                for k, (_, _, dev) in enumerate(peers):
                    s = fan[g] * j + k
                    pltpu.make_async_remote_copy(src_ref=src[i], dst_ref=land[i].at[chip, core], send_sem=send.at[s],
                                                 recv_sem=recv.at[s], device_id=dev, device_id_type=MESH).start()
                i += 1
        token[...] = jnp.zeros(token.shape, token.dtype)

    sem_shapes = []
    for s, f in zip(sizes, fan):
        sem_shapes += [pltpu.SemaphoreType.DMA((f * s,)), pltpu.SemaphoreType.DMA((f * s,)), pltpu.SemaphoreType.DMA((s,))]
    thru = [pltpu.HBM(a.shape, a.dtype) for a in flat + lands]
    outs = pl.pallas_call(
        body, name=name, out_shape=tuple(sem_shapes) + tuple(thru) + (S((8, LANES), F32),),
        in_specs=[_HBM] * (2 * k_ops), out_specs=[_SEM] * (3 * n_g) + [_HBM] * (2 * k_ops) + [pl.BlockSpec(memory_space=pltpu.VMEM)],
        input_output_aliases={i: 3 * n_g + i for i in range(2 * k_ops)},
        compiler_params=pltpu.CompilerParams(has_side_effects=_EFFECT),
    )(*[pltpu.with_memory_space_constraint(a, pltpu.HBM) for a in flat + lands])
    sems = [tuple(outs[3 * g:3 * g + 3]) for g in range(n_g)]
    srcs_thru, lands_thru, off = [], [], 3 * n_g
    for s in sizes:
        srcs_thru.append(list(outs[off:off + s]))
        off += s
    for s in sizes:
        lands_thru.append(list(outs[off:off + s]))
        off += s
    return sems, srcs_thru, lands_thru, outs[-1]


def gather_ici_wait(name, srcs, lands, sems, after, direct=False):
    n = len(srcs)

    def body(*refs):
        src, land = refs[:n], refs[n:2 * n]
        send, recv, loc = refs[2 * n:2 * n + 3]
        chip, core, peers = _gather_peers(direct)
        for j in range(n):
            for k, (pchip, pcore, dev) in enumerate(peers):
                s = len(peers) * j + k
                cp = pltpu.make_async_remote_copy(src_ref=src[j], dst_ref=land[j].at[pchip, pcore], send_sem=send.at[s],
                                                  recv_sem=recv.at[s], device_id=dev, device_id_type=MESH)
                cp.wait_send()
                cp.wait_recv()
            pltpu.make_async_copy(src[j], land[j].at[chip, core], loc.at[j]).wait()

    outs = pl.pallas_call(
        body, name=name, out_shape=tuple(pltpu.HBM(a.shape, a.dtype) for a in list(srcs) + list(lands)),
        in_specs=[_HBM] * (2 * n) + [_SEM] * 3 + [ANY], out_specs=[_HBM] * (2 * n),
        input_output_aliases={i: i for i in range(2 * n)},
        compiler_params=pltpu.CompilerParams(has_side_effects=_EFFECT),
    )(*srcs, *lands, *sems, after)
    return list(outs[n:])


def gather_d2d(name, bufs):
    k_ops = len(bufs)

    def body(*refs):
        in_refs, out_refs = refs[:k_ops], refs[k_ops:2 * k_ops]
        send_sems, recv_sems = refs[2 * k_ops:]
        core, sib = _sibling()
        sent, landed = [], []
        for i in range(k_ops):
            cp = pltpu.make_async_remote_copy(src_ref=in_refs[i].at[:, core], dst_ref=out_refs[i].at[:, core],
                                              send_sem=send_sems.at[i], recv_sem=recv_sems.at[i], device_id=sib, device_id_type=MESH)
            cp.start()
            sent.append(cp)
            landed.append(pltpu.make_async_remote_copy(src_ref=in_refs[i].at[:, core], dst_ref=out_refs[i].at[:, 1 - core],
                                                       send_sem=send_sems.at[i], recv_sem=recv_sems.at[i],
                                                       device_id=sib, device_id_type=MESH))
        for cp in landed:
            cp.wait_recv()
        for cp in sent:
            cp.wait_send()

    return pl.pallas_call(
        body, in_specs=[ANY] * k_ops, out_specs=[ANY] * k_ops, out_shape=[S(b.shape, b.dtype) for b in bufs],
        input_output_aliases={i: i for i in range(k_ops)},
        scratch_shapes=[pltpu.SemaphoreType.DMA((k_ops,)), pltpu.SemaphoreType.DMA((k_ops,))],
        name=name)(*bufs)


_ALL = ("x", "y", "c")


def _unit_rows(units):
    offs, off = [], 0
    for u in units:
        offs.append(off)
        off += u.shape[1]
    return offs, off


def scatter_start(name, units, carry):
    n_u, n_c = len(units), len(carry)
    offs, rows = _unit_rows(units)
    land = lax.empty((N_DEV, rows) + tuple(units[0].shape[2:]), units[0].dtype)
    fan = N_DEV - 1

    def body(*refs):
        u_refs, land_ref = refs[:n_u], refs[n_u]
        send, recv, loc = refs[n_u + 1 + n_c:n_u + 4 + n_c]
        me, peers = _group(_ALL)
        for j in range(n_u):
            rs = pl.ds(offs[j], units[j].shape[1])
            pltpu.make_async_copy(u_refs[j].at[me], land_ref.at[me, rs], loc.at[j]).start()
            for k, (idx, dev) in enumerate(peers):
                pltpu.make_async_remote_copy(src_ref=u_refs[j].at[idx], dst_ref=land_ref.at[me, rs], send_sem=send.at[fan * j + k],
                                             recv_sem=recv.at[fan * j + k], device_id=dev, device_id_type=MESH).start()

    thru = list(units) + [land] + list(carry)
    outs = pl.pallas_call(
        body, name=name,
        out_shape=(pltpu.SemaphoreType.DMA((fan * n_u,)), pltpu.SemaphoreType.DMA((fan * n_u,)), pltpu.SemaphoreType.DMA((n_u,)))
        + tuple(pltpu.HBM(a.shape, a.dtype) for a in thru),
        in_specs=[_HBM] * len(thru), out_specs=[_SEM] * 3 + [_HBM] * len(thru),
        input_output_aliases={i: 3 + i for i in range(len(thru))},
        compiler_params=pltpu.CompilerParams(has_side_effects=_EFFECT),
    )(*[pltpu.with_memory_space_constraint(a, pltpu.HBM) for a in thru])
    return tuple(outs[:3]), list(outs[3:3 + n_u]), outs[3 + n_u], list(outs[4 + n_u:])


def scatter_wait(name, units, land, sems, after):
    n_u = len(units)
    offs, _ = _unit_rows(units)
    fan = N_DEV - 1

    def body(*refs):
        u_refs, land_ref = refs[:n_u], refs[n_u]
        send, recv, loc = refs[n_u + 1:n_u + 4]
        me, peers = _group(_ALL)
        for j in range(n_u):
            rs = pl.ds(offs[j], units[j].shape[1])
            for k, (idx, dev) in enumerate(peers):
                cp = pltpu.make_async_remote_copy(src_ref=u_refs[j].at[idx], dst_ref=land_ref.at[idx, rs], send_sem=send.at[fan * j + k],
                                                  recv_sem=recv.at[fan * j + k], device_id=dev, device_id_type=MESH)
                cp.wait_send()
                cp.wait_recv()
            pltpu.make_async_copy(u_refs[j].at[me], land_ref.at[me, rs], loc.at[j]).wait()

    thru = list(units) + [land]
    outs = pl.pallas_call(
        body, name=name, out_shape=tuple(pltpu.HBM(a.shape, a.dtype) for a in thru),
        in_specs=[_HBM] * len(thru) + [_SEM] * 3 + [ANY], out_specs=[_HBM] * len(thru),
        input_output_aliases={i: i for i in range(len(thru))},
        compiler_params=pltpu.CompilerParams(has_side_effects=_EFFECT),
    )(*thru, *sems, after)
    return outs[n_u]


def _row_tile(rows, cap=512):
    return next(t for t in range(cap - cap % 16, 0, -16) if rows % t == 0)


def sum_shares(name, recv, me):
    n, rows, c = recv.shape
    tr = _row_tile(rows)

    def body(me_ref, *refs):
        acc = refs[0][...].astype(F32)
        for r in refs[1:n]:
            acc = acc + r[...].astype(F32)
        refs[n][...] = acc

    def slot(mask):
        return pl.BlockSpec((None, tr, c), lambda i, me, mask=mask: (jnp.bitwise_xor(me[0], mask), i, 0))

    spec = pltpu.PrefetchScalarGridSpec(
        num_scalar_prefetch=1, grid=(rows // tr,), in_specs=[slot(k) for k in range(n)],
        out_specs=pl.BlockSpec((tr, c), lambda i, me: (i, 0)))
    return pl.pallas_call(body, grid_spec=spec, out_shape=S((rows, c), F32),
                          compiler_params=_cp("parallel"), name=name)(me, *([recv] * n))


def sum_slots(name, slots):
    n, r, c = slots.shape

    def body(s_ref, o_ref):
        acc = s_ref[0]
        for j in range(1, n):
            acc = acc + s_ref[j]
        o_ref[...] = acc

    return pl.pallas_call(body, out_shape=S((r, c), F32), compiler_params=pltpu.CompilerParams(vmem_limit_bytes=VMEM_LIMIT),
                          name=name)(slots)


def adamw_units(name, pieces, transposed, w, m, v):
    n_l, k, n = w.shape
    tk = _tile(k, 512) if transposed else k
    p_rows = n if transposed else k
    arrs = [p[0] if isinstance(p, tuple) else p for p in pieces]
    offs = [p[1] // p_rows if isinstance(p, tuple) else 0 for p in pieces]
    assert all(not isinstance(p, tuple) or p[1] % p_rows == 0 for p in pieces)
    c1 = 1.0 - ADAM_B1 ** ADAM_STEP
    c2 = 1.0 - ADAM_B2 ** ADAM_STEP

    def body(*refs):
        p_refs, (w_ref, m_ref, v_ref, g_ref, d_ref, m2_ref, v2_ref) = refs[:n_l], refs[n_l:]
        gv = p_refs[0][...]
        for j in range(1, n_l):
            gv = jnp.where(pl.program_id(0) == j, p_refs[j][...], gv)
        if transposed:
            gv = gv.T
        m2 = ADAM_B1 * m_ref[...] + (1.0 - ADAM_B1) * gv
        v2 = ADAM_B2 * v_ref[...] + (1.0 - ADAM_B2) * (gv * gv)
        g_ref[...] = gv
        m2_ref[...] = m2
        v2_ref[...] = v2
        d_ref[...] = -ADAM_LR * ((m2 / c1) / (jnp.sqrt(v2 / c2) + ADAM_EPS) + ADAM_WD * w_ref[...])

    def piece(o):
        if transposed:
            return pl.BlockSpec((n, tk), lambda l, i, o=o: (o, i))
        return pl.BlockSpec((k, n), lambda l, i, o=o: (o, 0))

    blk = pl.BlockSpec((None, tk, n), lambda l, i: (l, i, 0))
    return tuple(pl.pallas_call(body, grid=(n_l, k // tk), in_specs=[piece(o) for o in offs] + [blk] * 3, out_specs=[blk] * 4,
                                out_shape=[S(w.shape, F32)] * 4, compiler_params=_cp("parallel", "parallel"),
                                name=name)(*arrs, w, m, v))


def adamw_native(name, g, w, m, v, tr=512):
    shape = w.shape
    cols = shape[-1]
    rows = w.size // cols
    tr = _tile(rows, tr) if rows % 8 == 0 else rows
    c1 = 1.0 - ADAM_B1 ** ADAM_STEP
    c2 = 1.0 - ADAM_B2 ** ADAM_STEP

    def body(g_ref, w_ref, m_ref, v_ref, d_ref, m2_ref, v2_ref):
        gv = g_ref[...]
        m2 = ADAM_B1 * m_ref[...] + (1.0 - ADAM_B1) * gv
        v2 = ADAM_B2 * v_ref[...] + (1.0 - ADAM_B2) * (gv * gv)
        m2_ref[...] = m2
        v2_ref[...] = v2
        d_ref[...] = -ADAM_LR * ((m2 / c1) / (jnp.sqrt(v2 / c2) + ADAM_EPS) + ADAM_WD * w_ref[...])

    row = pl.BlockSpec((tr, cols), lambda i: (i, 0))
    outs = pl.pallas_call(body, grid=(rows // tr,), in_specs=[row] * 4, out_specs=[row] * 3,
                          out_shape=[S((rows, cols), F32)] * 3, compiler_params=_cp("parallel"),
                          name=name)(*[a.reshape(rows, cols) for a in (g, w, m, v)])
    return tuple(o.reshape(shape) for o in outs)


_REPLICATED = ("e_norm", "e_gmlp_w", "e_gmlp_b", "e_conv_b", "e_conv_ln_g", "e_conv_ln_b", "o_lam_re", "o_lam_im", "o_log_dt",
               "o_b_re", "o_b_im", "o_c_re", "o_c_im", "ca_norm", "ca_mem_norm", "ffn_norm", "final_norm")
_ORDER = ("e_norm", "e_w_in", "e_gmlp_w", "e_gmlp_b", "e_conv_w", "e_conv_b", "e_conv_ln_g", "e_conv_ln_b", "e_w_out",
          "o_norm", "o_w_in", "o_lam_re", "o_lam_im", "o_log_dt", "o_b_re", "o_b_im", "o_c_re", "o_c_im", "o_d", "o_w_out",
          "ca_norm", "ca_mem_norm", "ca_wq", "ca_wk", "ca_wv", "ca_wo", "ffn_norm", "ffn_w_gate", "ffn_w_up", "ffn_w_down",
          "final_norm")


def _rows128(a, multiple=8):
    flat = a.reshape(-1)
    rows = -(-flat.shape[0] // (LANES * multiple)) * multiple
    return jnp.pad(flat, (0, rows * LANES - flat.shape[0])).reshape(rows, LANES)


def _shard(full, axis):
    s = full.shape
    return jnp.moveaxis(full.reshape(s[:axis] + (N_DEV, s[axis] // N_DEV) + s[axis + 1:]), axis, 0)


_UNITS = (("e_w_in", 0, True), ("e_w_out", 0, False), ("o_w_in", 0, False), ("o_w_out", 0, True),
          *[(n, i, False) for n in ("ca_wq", "ca_wk", "ca_wv", "ca_wo") for i in (0, 1)],
          *[(n, i, tr) for n, tr in (("ffn_w_gate", True), ("ffn_w_up", True), ("ffn_w_down", False)) for i in (0, 1)])
_LAYERED = ("ca_wq", "ca_wk", "ca_wv", "ca_wo", "ffn_w_gate", "ffn_w_up", "ffn_w_down")
_SMALL_SHARDED = (("e_conv_w", 2), ("o_norm", 1), ("o_d", 1))
RS_ROW = 1024


def _unit_key(name, tr):
    return name + "_t" if tr else name


def _stage_of(name, layer):
    if name.startswith("e_"):
        return 0 if name == "e_w_in" else 1
    if name.startswith("o_"):
        return 3
    if name.startswith("ca_"):
        return 1 if layer == 0 else 3
    return 2 if layer == 0 else 4


GATHER_STAGES = 5
GATHER_DIRECT = (False, False, False, True, False)


def weight_fetcher(local):
    groups, meta = [[] for _ in range(GATHER_STAGES)], [[] for _ in range(GATHER_STAGES)]
    for name, layer, tr in _UNITS:
        blk = local[name][layer]
        st = _stage_of(name, layer)
        groups[st].append(_bf(blk.T if tr else blk))
        meta[st].append((name, layer, tr))
    small = jnp.concatenate([local[name].reshape(-1) for name, _ in _SMALL_SHARDED])
    groups[0].append(_rows128(small))
    direct = list(GATHER_DIRECT)
    sems, srcs, lands, token = gather_ici_start("ag_w_start", groups, direct)

    def fetch(stage, after):
        bufs = gather_ici_wait(f"ag_w_wait{stage}", srcs[stage], lands[stage], sems[stage], after, direct[stage])
        if not direct[stage]:
            bufs = gather_d2d(f"ag_w_d2d{stage}", bufs)
        got = {}
        for (name, layer, tr), blk, buf in zip(meta[stage], groups[stage], bufs):
            arr = buf.reshape((N_DEV * blk.shape[0],) + tuple(blk.shape[1:]))
            if name in _LAYERED:
                got[(_unit_key(name, tr), layer)] = arr
            else:
                got[_unit_key(name, tr)] = arr
        if stage == 0:
            flat = bufs[-1].reshape(N_DEV, -1)
            off = 0
            for name, axis in _SMALL_SHARDED:
                blk = local[name]
                seg = flat[:, off:off + blk.size].reshape((N_DEV,) + blk.shape)
                off += blk.size
                seg = jnp.moveaxis(seg, 0, axis)
                got[name] = seg.reshape(seg.shape[:axis] + (-1,) + seg.shape[axis + 2:])
            got["e_conv_w"] = got["e_conv_w"][0]
        return got

    return fetch, token


def _grad_stage_of(name, layer):
    if name.startswith("e_"):
        return 4
    if name.startswith("o_"):
        return 1
    if name.startswith("ca_"):
        return 3 if layer == 0 else 1
    return 2 if layer == 0 else 0


GRAD_STAGES = 5
SMALL_ROWS = 16


def gradient_reducer(local, mom, var):
    me = (4 * lax.axis_index("x") + 2 * lax.axis_index("y") + lax.axis_index("c")).astype(jnp.int32).reshape(1)
    pending = []

    def start(stage, grads, carry):
        def grad_of(unit):
            key = _unit_key(unit[0], unit[2])
            return grads[(key, unit[1])] if unit[0] in _LAYERED else grads[key]

        units = sorted([u for u in _UNITS if _grad_stage_of(u[0], u[1]) == stage], key=lambda u: -grad_of(u).size)
        parts, spans = [], []
        for unit in units:
            g = grad_of(unit)
            part = g.reshape(N_DEV, -1, RS_ROW)
            spans.append((part.shape[1], g.shape[0] // N_DEV, g.shape[1]))
            parts.append(part)
        if stage == GRAD_STAGES - 1:
            small = jnp.concatenate([_shard(grads[name], axis).reshape(N_DEV, -1) for name, axis in _SMALL_SHARDED], axis=1)
            small = jnp.pad(small, ((0, 0), (0, SMALL_ROWS * RS_ROW - small.shape[1])))
            parts.append(small.astype(BF16).reshape(N_DEV, SMALL_ROWS, RS_ROW))
        sems, sent, land, carry = scatter_start(f"rs_start{stage}", parts, carry)
        pending.append((stage, units, spans, sems, sent, land))
        return carry

    def finish(after):
        res, per_layer, small_flat = {}, {}, None
        for stage, units, spans, sems, sent, land in pending:
            land = scatter_wait(f"rs_wait{stage}", sent, land, sems, after)
            total = sum_shares(f"rs_sum{stage}", land, me)
            off = 0
            for (name, layer, tr), (rows, r, c) in zip(units, spans):
                piece = (total, off) if c == RS_ROW else total[off:off + rows].reshape(r, c)
                per_layer.setdefault(name, {})[layer] = (piece, tr)
                off += rows
            if stage == GRAD_STAGES - 1:
                small_flat = total[off:off + SMALL_ROWS].reshape(-1)
        for name, by_layer in per_layer.items():
            pieces = [by_layer[i][0] for i in sorted(by_layer)]
            res[name] = adamw_units("adamw_" + name, pieces, by_layer[0][1], local[name], mom[name], var[name])
        off = 0
        for name, _ in _SMALL_SHARDED:
            blk = local[name]
            g = small_flat[off:off + blk.size].reshape(blk.shape)
            off += blk.size
            res[name] = (g,) + adamw_native("adamw_" + name, g, blk, mom[name], var[name])
        return res

    return start, finish


def replicated_start(grads, loss):
    pack = jnp.concatenate([_rows128(grads[name]) for name in _REPLICATED] + [_rows128(loss)], axis=0)
    sems, srcs, lands, token = gather_ici_start("ag_g_start", [[pack]], [False])
    return sems[0], srcs[0], lands[0], token


def replicated_finish(handle, after, w, mom, var):
    sems, srcs, lands, _ = handle
    (buf,) = gather_d2d("ag_g_d2d", gather_ici_wait("ag_g_wait", srcs, lands, sems, after))
    rows = srcs[0].shape[0]
    total = sum_slots("ag_g_sum", buf.reshape(N_DEV, rows, LANES))
    res, off = {}, 0
    for name in _REPLICATED:
        n = w[name].size
        nr = -(-n // (LANES * 8)) * 8
        g = total[off:off + nr].reshape(-1)[:n].reshape(w[name].shape)
        off += nr
        res[name] = (g,) + adamw_native("adamw_" + name, g, w[name], mom[name], var[name])
    return res, total[off, 0]


def kernel(x, mem, e_norm, e_w_in, e_gmlp_w, e_gmlp_b, e_conv_w, e_conv_b, e_conv_ln_g, e_conv_ln_b, e_w_out, o_norm, o_w_in, o_lam_re, o_lam_im, o_log_dt, o_b_re, o_b_im, o_c_re, o_c_im, o_d, o_w_out, ca_norm, ca_mem_norm, ca_wq, ca_wk, ca_wv, ca_wo, ffn_norm, ffn_w_gate, ffn_w_up, ffn_w_down, final_norm, loss_target, m_e_norm, m_e_w_in, m_e_gmlp_w, m_e_gmlp_b, m_e_conv_w, m_e_conv_b, m_e_conv_ln_g, m_e_conv_ln_b, m_e_w_out, m_o_norm, m_o_w_in, m_o_lam_re, m_o_lam_im, m_o_log_dt, m_o_b_re, m_o_b_im, m_o_c_re, m_o_c_im, m_o_d, m_o_w_out, m_ca_norm, m_ca_mem_norm, m_ca_wq, m_ca_wk, m_ca_wv, m_ca_wo, m_ffn_norm, m_ffn_w_gate, m_ffn_w_up, m_ffn_w_down, m_final_norm, v_e_norm, v_e_w_in, v_e_gmlp_w, v_e_gmlp_b, v_e_conv_w, v_e_conv_b, v_e_conv_ln_g, v_e_conv_ln_b, v_e_w_out, v_o_norm, v_o_w_in, v_o_lam_re, v_o_lam_im, v_o_log_dt, v_o_b_re, v_o_b_im, v_o_c_re, v_o_c_im, v_o_d, v_o_w_out, v_ca_norm, v_ca_mem_norm, v_ca_wq, v_ca_wk, v_ca_wv, v_ca_wo, v_ffn_norm, v_ffn_w_gate, v_ffn_w_up, v_ffn_w_down, v_final_norm):
    given = dict(locals())
    local = {k: given[k] for k in _ORDER}
    mom = {k: given["m_" + k] for k in _ORDER}
    var = {k: given["v_" + k] for k in _ORDER}

    w = {}
    w.update({
        "e_norm": e_norm, "e_gmlp_w": e_gmlp_w[0], "e_gmlp_b": e_gmlp_b.reshape(A_GROUPS, GMLP_BLOCK, 1),
        "e_conv_b": e_conv_b, "e_conv_ln_g": e_conv_ln_g, "e_conv_ln_b": e_conv_ln_b,
        "o_lam_re": o_lam_re[0], "o_lam_im": o_lam_im[0], "o_log_dt": o_log_dt[0], "o_b_re": o_b_re[0], "o_b_im": o_b_im[0],
        "o_c_re": o_c_re[0], "o_c_im": o_c_im[0], "ca_norm": ca_norm, "ca_mem_norm": ca_mem_norm, "ffn_norm": ffn_norm,
        "final_norm": final_norm.reshape(1, D_MODEL),
    })
    start_reduce, finish_reduce = gradient_reducer(local, mom, var)
    fetch, token = weight_fetcher(local)
    loss_part, grad_x, grads = local_step(x[0], mem[0], loss_target[0], w, fetch, start_reduce, token[0:1, 0:1])
    grads["final_norm"] = grads["final_norm"].reshape(D_MODEL)

    handle = replicated_start(grads, loss_part)
    res = finish_reduce(handle[3])
    rep, loss = replicated_finish(handle, res["ffn_w_down"][1], local, mom, var)
    res.update(rep)
    return (loss, grad_x[None], *[res[k][0] for k in _ORDER], *[res[k][1] for k in _ORDER],
            *[res[k][2] for k in _ORDER], *[res[k][3] for k in _ORDER])
```

```python
import jax
import jax.numpy as jnp
from jax import lax
from jax.experimental import pallas as pl
from jax.experimental.pallas import tpu as pltpu

F32 = jnp.float32
BF16 = jnp.bfloat16
S = jax.ShapeDtypeStruct

D_MODEL = 1024
A_WIDTH = 512
A_GROUPS = 4
GMLP_BLOCK = 128
CHUNK = 64
B_WIDTH = 512
IN_WIDTH = 2 * A_WIDTH + 2 * B_WIDTH
CONV_WIDTH = 31
CONV_PAD = 32
C_WIDTH = 512
C_GROUP_CH = 16
C_GROUPS = 32
C_STATE = 64
N_STATE = C_GROUPS * C_STATE
CA_HEADS = 4
CA_HEAD_DIM = 256
EPS = 1e-6
ADAM_LR = 0.001
ADAM_B1 = 0.9
ADAM_B2 = 0.999
ADAM_EPS = 1e-08
ADAM_WD = 0.01
ADAM_STEP = 10
N_DEV = 8
LANES = 128
VMEM_LIMIT = 56 << 20
VMEM_BUDGET = 40 << 20
MM_TN_RESIDENT = 8 << 20
MESH = pl.DeviceIdType.MESH
ANY = pl.BlockSpec(memory_space=pl.ANY)


def _cp(*sem):
    return pltpu.CompilerParams(dimension_semantics=sem, vmem_limit_bytes=VMEM_LIMIT)


def _tile(n, pref):
    t = pref
    while n % t:
        t //= 2
    return t


def _bf(v):
    return v if v.dtype == BF16 else v.astype(BF16)


def _sigmoid(x):
    return 1.0 / (1.0 + jnp.exp(-x))


_GC = 0.7978845608028654


def _gelu(x):
    return 0.5 * x * (1.0 + jnp.tanh(_GC * (x + 0.044715 * x * x * x)))


def _gelu_grad(x):
    x2 = x * x
    t = jnp.tanh(_GC * (x + 0.044715 * x * x2))
    return 0.5 * (1.0 + t) + 0.5 * x * (1.0 - t * t) * _GC * (1.0 + 3.0 * 0.044715 * x2)


def _tspec(entry, tm):
    if isinstance(entry, tuple):
        arr, cb, width = entry
        return arr, pl.BlockSpec((tm, width), lambda i, cb=cb: (i, cb))
    return entry, pl.BlockSpec((tm, entry.shape[1]), lambda i: (i, 0))


def rows_call(name, fn, tiled, full, outs, accs, tm=256):
    pairs = [_tspec(e, tm) for e in tiled]
    arrs = [p[0] for p in pairs]
    rows = arrs[0].shape[0]
    tm = _tile(rows, tm)
    pairs = [_tspec(e, tm) for e in tiled]
    n_in = len(tiled) + len(full)
    n_out = len(outs)

    def body(*refs):
        vals = [r[...] for r in refs[:n_in]]
        o_refs = refs[n_in:n_in + n_out]
        a_refs = refs[n_in + n_out:]
        ov, av = fn(*vals)
        for r, v in zip(o_refs, ov):
            r[...] = v.astype(r.dtype)
        if a_refs:
            @pl.when(pl.program_id(0) == 0)
            def _():
                for r in a_refs:
                    r[...] = jnp.zeros(r.shape, r.dtype)
            for r, v in zip(a_refs, av):
                r[...] += v

    in_specs = [p[1] for p in pairs] + [pl.BlockSpec(a.shape, lambda i, nd=a.ndim: (0,) * nd) for a in full]
    out_specs = [pl.BlockSpec((tm, c), lambda i: (i, 0)) for c, _ in outs]
    out_specs += [pl.BlockSpec(s, lambda i, nd=len(s): (0,) * nd) for s in accs]
    out_shape = [S((rows, c), dt) for c, dt in outs] + [S(s, F32) for s in accs]
    return pl.pallas_call(body, grid=(rows // tm,), in_specs=in_specs, out_specs=out_specs, out_shape=out_shape,
                          compiler_params=_cp("arbitrary"), name=name)(*arrs, *full)


def mm_nn(name, m, n, pairs, n_acc, epi, outs, tiled=(), cols=(), rowv=(), sums=(), norm_gain=None):
    a_ops, a_slot, b_arrs, b_specs, idx, trans = [], [], [], [], [], []
    fixed = 0
    for pair in pairs:
        a, b, k = pair[:3]
        bt = len(pair) > 3
        arr, cb, kdim = a if isinstance(a, tuple) else (a, 0, a.shape[1])
        key = (id(arr), cb, kdim)
        if key not in [o[0] for o in a_ops]:
            a_ops.append((key, arr, cb, kdim))
        a_slot.append([o[0] for o in a_ops].index(key))
        b_arr, off = b if isinstance(b, tuple) else (b, 0)
        b_arrs.append(b_arr)
        if bt:
            assert off % n == 0 and b_arr.shape[1] == kdim
            b_specs.append(pl.BlockSpec((n, kdim), lambda i, o=off // n: (o, 0), pipeline_mode=pl.Buffered(1)))
        else:
            assert b_arr.shape[1] == n
            b_specs.append(pl.BlockSpec((kdim, n), lambda i, o=off: (o, 0), pipeline_mode=pl.Buffered(1)))
        fixed += kdim * n * b_arr.dtype.itemsize
        idx.append(k)
        trans.append(bt)
    per_row = sum(2 * kdim * arr.dtype.itemsize for _, arr, _, kdim in a_ops)
    per_row += sum(2 * n * t.dtype.itemsize for t in tiled) + sum(2 * n * jnp.dtype(dt).itemsize for dt in outs)
    cn = n if sums or cols else (512 if n % 512 == 0 else 256)
    per_row += (n_acc + 3) * cn * 4
    tm = next((t for t in (1024, 512, 256, 128) if m % t == 0 and fixed + t * per_row <= VMEM_BUDGET), _tile(m, 128))
    n_a, n_p, n_t = len(a_ops), len(pairs), len(tiled)
    n_in = n_a + n_p + n_t + len(cols) + len(rowv)
    normed = norm_gain is not None
    o0 = n_in + normed

    def body(*refs):
        a_vals = [None if normed and i == 0 else _bf(r[...]) for i, r in enumerate(refs[:n_a])]
        if normed:
            xv = refs[0][...]
            rv = lax.rsqrt(jnp.mean(xv * xv, axis=-1, keepdims=True) + EPS)
            a_vals[0] = (xv * rv * refs[n_in][...]).astype(BF16)
            refs[o0 + len(outs)][...] = a_vals[0]
            refs[o0 + len(outs) + 1][...] = rv
        for j in range(n // cn):
            cs = slice(j * cn, (j + 1) * cn)
            accs = [None] * n_acc
            for p in range(n_p):
                av, b_ref = a_vals[a_slot[p]], refs[n_a + p]
                if trans[p]:
                    d = lax.dot_general(av, _bf(b_ref[cs, :]), (((1,), (1,)), ((), ())), preferred_element_type=F32)
                else:
                    d = jnp.dot(av, _bf(b_ref[:, cs]), preferred_element_type=F32)
                accs[idx[p]] = d if accs[idx[p]] is None else accs[idx[p]] + d
            extra = [r[:, cs] for r in refs[n_a + n_p:n_a + n_p + n_t]] + [r[...] for r in refs[n_a + n_p + n_t:n_in - len(rowv)]]
            extra += [r[:, cs] for r in refs[n_in - len(rowv):n_in]]
            ov = epi(accs, *extra)
            for r, v in zip(refs[o0:o0 + len(outs)], ov):
                r[:, cs] = v.astype(r.dtype)
        sv = ov[len(outs):]
        if sums:
            s_refs = refs[o0 + len(outs) + 2 * normed:]

            @pl.when(pl.program_id(0) == 0)
            def _():
                for r in s_refs:
                    r[...] = jnp.zeros(r.shape, r.dtype)
            for r, v in zip(s_refs, sv):
                r[...] += v

    in_specs = [pl.BlockSpec((tm, kdim), lambda i, cb=cb: (i, cb)) for _, _, cb, kdim in a_ops] + b_specs
    in_specs += [pl.BlockSpec((tm, n), lambda i: (i, 0)) for _ in tiled]
    in_specs += [pl.BlockSpec((tm, 1), lambda i: (i, 0)) for _ in cols]
    in_specs += [pl.BlockSpec((1, n), lambda i: (0, 0)) for _ in rowv]
    out_specs = [pl.BlockSpec((tm, n), lambda i: (i, 0)) for _ in outs]
    out_shape = [S((m, n), dt) for dt in outs]
    gain = []
    if normed:
        k0 = a_ops[0][3]
        gain = [norm_gain]
        in_specs.append(pl.BlockSpec((1, k0), lambda i: (0, 0)))
        out_specs += [pl.BlockSpec((tm, k0), lambda i: (i, 0)), pl.BlockSpec((tm, 1), lambda i: (i, 0))]
        out_shape += [S((m, k0), BF16), S((m, 1), F32)]
    out_specs += [pl.BlockSpec(s, lambda i, nd=len(s): (0,) * nd) for s in sums]
    out_shape += [S(s, F32) for s in sums]
    return pl.pallas_call(body, grid=(m // tm,), in_specs=in_specs, out_specs=out_specs, out_shape=out_shape,
                          compiler_params=_cp("arbitrary" if sums else "parallel"),
                          name=name)(*[o[1] for o in a_ops], *b_arrs, *tiled, *cols, *rowv, *gain)


def mm_tn(name, a, b, out_dtype=BF16):
    if isinstance(a, tuple):
        a_arr, a_cb, m = a
    else:
        a_arr, a_cb, m = a, None, a.shape[1]
    if isinstance(b, tuple):
        b_arr, b_cb, n = b
    else:
        b_arr, b_cb, n = b, None, b.shape[1]
    t = a_arr.shape[0]
    whole_b = t * n * b_arr.dtype.itemsize <= MM_TN_RESIDENT and b_cb is None
    tn = n if whole_b else _tile(n, 512)
    tm = _tile(m, 512 if t * 512 * a_arr.dtype.itemsize * 2 + t * tn * b_arr.dtype.itemsize * 2 <= VMEM_BUDGET else 256)
    a_off = 0 if a_cb is None else a_cb * (m // tm)
    b_off = 0 if b_cb is None else b_cb * (n // tn)

    def body(a_ref, b_ref, o_ref):
        o_ref[...] = lax.dot_general(_bf(a_ref[...]), _bf(b_ref[...]), (((0,), (0,)), ((), ())),
                                     preferred_element_type=F32).astype(o_ref.dtype)

    if whole_b:
        b_spec = pl.BlockSpec((t, n), lambda i, j: (0, 0), pipeline_mode=pl.Buffered(1))
    else:
        b_spec = pl.BlockSpec((t, tn), lambda i, j: (0, j + b_off))
    return pl.pallas_call(
        body, grid=(m // tm, n // tn),
        in_specs=[pl.BlockSpec((t, tm), lambda i, j: (0, i + a_off)), b_spec],
        out_specs=pl.BlockSpec((tm, tn), lambda i, j: (i, j)), out_shape=S((m, n), out_dtype),
        compiler_params=_cp("parallel", "parallel"), name=name)(a_arr, b_arr)


def rms_bwd_gain_only(name, dxn, x, r):
    def fn(dv, xv, rv):
        return [], [jnp.sum(dv * xv * rv, axis=0, keepdims=True)]
    return rows_call(name, fn, [dxn, x, r], [], [], [(1, x.shape[1])])[0]


def _final_loss_epi(accs, res, tv, g):
    xv = res + accs[0]
    d = xv.shape[-1]
    r = lax.rsqrt(jnp.mean(xv * xv, axis=-1, keepdims=True) + EPS)
    xh = xv * r
    err = xh * g - tv
    dy = err * (1.0 / d)
    w = dy * g
    dx = r * (w - xh * jnp.mean(w * xh, axis=-1, keepdims=True))
    part = jnp.sum(jnp.sum(err * err, axis=-1, keepdims=True), axis=0, keepdims=True) * (0.5 / d)
    return [dx, dx, jnp.sum(dy * xh, axis=0, keepdims=True), part]


def _gmlp_mask():
    row = lax.broadcasted_iota(jnp.int32, (GMLP_BLOCK, GMLP_BLOCK), 0) // CHUNK
    col = lax.broadcasted_iota(jnp.int32, (GMLP_BLOCK, GMLP_BLOCK), 1) // CHUNK
    return col <= row


def _ln_plain(v):
    mu = jnp.mean(v, axis=-1, keepdims=True)
    vc = v - mu
    rstd = lax.rsqrt(jnp.mean(vc * vc, axis=-1, keepdims=True) + EPS)
    return vc * rstd, rstd


def even_out_fwd(name, proj, hc, x, w, b, ln_g, ln_b, w_out, tm=512):
    t, d = x.shape
    tm = _tile(t, tm)

    def body(au_ref, av_ref, hc_ref, x_ref, w_ref, b_ref, lg_ref, lb_ref, wo_ref, x1_ref, oa_ref, ob_ref):
        mask = _gmlp_mask()
        u = _gelu(au_ref[...])
        vn, _ = _ln_plain(_gelu(av_ref[...]))
        vnb = _bf(vn)
        for g in range(A_GROUPS):
            wg = _bf(jnp.where(mask, w_ref[g], 0.0))
            cs = slice(g * GMLP_BLOCK, (g + 1) * GMLP_BLOCK)
            for n in range(tm // GMLP_BLOCK):
                rs = slice(n * GMLP_BLOCK, (n + 1) * GMLP_BLOCK)
                sg = jnp.dot(wg, vnb[rs, cs], preferred_element_type=F32) + b_ref[g]
                oa_ref[rs, cs] = (u[rs, cs] * sg).astype(oa_ref.dtype)
        y, _ = _ln_plain(hc_ref[...])
        z = y * lg_ref[...] + lb_ref[...]
        ob_ref[...] = (z * _sigmoid(z)).astype(ob_ref.dtype)
        x1_ref[...] = (x_ref[...] + jnp.dot(oa_ref[...], wo_ref[0:A_WIDTH, :], preferred_element_type=F32)
                       + jnp.dot(ob_ref[...], wo_ref[A_WIDTH:, :], preferred_element_type=F32))

    half = pl.BlockSpec((tm, A_WIDTH), lambda i: (i, 0))
    return pl.pallas_call(
        body, grid=(t // tm,),
        in_specs=[half, pl.BlockSpec((tm, A_WIDTH), lambda i: (i, 1)), half, pl.BlockSpec((tm, d), lambda i: (i, 0)),
                  _whole(w), _whole(b), _whole(ln_g), _whole(ln_b), _whole(w_out)],
        out_specs=[pl.BlockSpec((tm, d), lambda i: (i, 0)), half, half],
        out_shape=[S((t, d), F32), S((t, A_WIDTH), BF16), S((t, B_WIDTH), BF16)],
        compiler_params=_cp("parallel"), name=name)(proj, proj, hc, x, w, b, ln_g, ln_b, w_out)


def gmlp_bwd(name, proj, dxb, w_out, w, b, tm=512):
    t = proj.shape[0]
    tm = _tile(t, tm)

    def body(au_ref, av_ref, dx_ref, wo_ref, w_ref, b_ref, dp_ref, dw_ref, db_ref):
        @pl.when(pl.program_id(0) == 0)
        def _():
            dw_ref[...] = jnp.zeros(dw_ref.shape, F32)
            db_ref[...] = jnp.zeros(db_ref.shape, F32)

        mask = _gmlp_mask()
        au = au_ref[...]
        av = av_ref[...]
        u = _gelu(au)
        vn, rstd = _ln_plain(_gelu(av))
        vnb = _bf(vn)
        dout = lax.dot_general(dx_ref[...], wo_ref[0:A_WIDTH, :], _NT, preferred_element_type=F32)
        dvn_cols = []
        for g in range(A_GROUPS):
            wm = jnp.where(mask, w_ref[g], 0.0)
            wg = _bf(wm)
            wgt = _bf(wm.T)
            cs = slice(g * GMLP_BLOCK, (g + 1) * GMLP_BLOCK)
            dwg = jnp.zeros((GMLP_BLOCK, GMLP_BLOCK), F32)
            dbg = jnp.zeros((GMLP_BLOCK, 1), F32)
            dvn_rows = []
            for n in range(tm // GMLP_BLOCK):
                rs = slice(n * GMLP_BLOCK, (n + 1) * GMLP_BLOCK)
                sg = jnp.dot(wg, vnb[rs, cs], preferred_element_type=F32) + b_ref[g]
                dp_ref[rs, cs] = (dout[rs, cs] * sg * _gelu_grad(au[rs, cs])).astype(dp_ref.dtype)
                dsg = dout[rs, cs] * u[rs, cs]
                dsgb = _bf(dsg)
                dbg = dbg + jnp.sum(dsg, axis=1, keepdims=True)
                dwg = dwg + lax.dot_general(dsgb, vnb[rs, cs], (((1,), (1,)), ((), ())), preferred_element_type=F32)
                dvn_rows.append(jnp.dot(wgt, dsgb, preferred_element_type=F32))
            dw_ref[g] += jnp.where(mask, dwg, 0.0)
            db_ref[g] += dbg
            dvn_cols.append(jnp.concatenate(dvn_rows, axis=0))
        dvn = jnp.concatenate(dvn_cols, axis=1)
        dv = rstd * (dvn - jnp.mean(dvn, axis=-1, keepdims=True) - vn * jnp.mean(dvn * vn, axis=-1, keepdims=True))
        dp_ref[:, A_WIDTH:] = (dv * _gelu_grad(av)).astype(dp_ref.dtype)

    return pl.pallas_call(
        body, grid=(t // tm,),
        in_specs=[pl.BlockSpec((tm, A_WIDTH), lambda i: (i, 0)), pl.BlockSpec((tm, A_WIDTH), lambda i: (i, 1)),
                  pl.BlockSpec((tm, dxb.shape[1]), lambda i: (i, 0)), pl.BlockSpec(w_out.shape, lambda i: (0, 0)),
                  pl.BlockSpec(w.shape, lambda i: (0, 0, 0)), pl.BlockSpec(b.shape, lambda i: (0, 0, 0))],
        out_specs=[pl.BlockSpec((tm, 2 * A_WIDTH), lambda i: (i, 0)),
                   pl.BlockSpec(w.shape, lambda i: (0, 0, 0)), pl.BlockSpec(b.shape, lambda i: (0, 0, 0))],
        out_shape=[S((t, 2 * A_WIDTH), BF16), S(w.shape, F32), S(b.shape, F32)],
        compiler_params=_cp("arbitrary"), name=name)(proj, proj, dxb, w_out, w, b)


CONV_ROWS = 256
CONV_ROWS_BWD = 64


def conv_fwd(name, proj, w, cb):
    t = proj.shape[0]
    tc = LANES
    rows = _tile(t, CONV_ROWS)
    a_cb, g_cb = 2 * A_WIDTH // tc, (2 * A_WIDTH + B_WIDTH) // tc

    def body(a_ref, g_ref, w_ref, cb_ref, o_ref, hpad):
        hpad[0:CONV_PAD, :] = jnp.zeros((CONV_PAD, tc), F32)

        def fill(i, _):
            r0 = pl.multiple_of(i * rows, rows)
            hpad[pl.ds(CONV_PAD + r0, rows), :] = a_ref[pl.ds(r0, rows), :] * _sigmoid(g_ref[pl.ds(r0, rows), :])
            return 0
        lax.fori_loop(0, t // rows, fill, 0)

        def conv(i, _):
            r0 = pl.multiple_of(i * rows, rows)
            win = hpad[pl.ds(r0, rows + CONV_PAD), :]
            acc = jnp.zeros((rows, tc), F32) + cb_ref[...]
            for b in range(SUB):
                wb = win if b == 0 else pltpu.roll(win, b, 0)
                for a in range(CONV_PAD // SUB):
                    k = CONV_WIDTH - 1 - (SUB * a + b)
                    if k >= 0:
                        lo = CONV_PAD - SUB * a
                        acc = acc + wb[lo:lo + rows, :] * w_ref[k:k + 1, :]
            o_ref[pl.ds(r0, rows), :] = acc
            return 0
        lax.fori_loop(0, t // rows, conv, 0)

    return pl.pallas_call(
        body, grid=(B_WIDTH // tc,),
        in_specs=[pl.BlockSpec((t, tc), lambda j: (0, a_cb + j)), pl.BlockSpec((t, tc), lambda j: (0, g_cb + j)),
                  pl.BlockSpec((CONV_WIDTH, tc), lambda j: (0, j)), pl.BlockSpec((1, tc), lambda j: (0, j))],
        out_specs=pl.BlockSpec((t, tc), lambda j: (0, j)), out_shape=S((t, B_WIDTH), F32),
        scratch_shapes=[pltpu.VMEM((t + CONV_PAD, tc), F32)],
        compiler_params=_cp("parallel"), name=name)(proj, proj, w, cb)


def conv_bwd(name, proj, dhc, w):
    t = proj.shape[0]
    tc = LANES
    rows = _tile(t, CONV_ROWS_BWD)
    a_cb, g_cb = 2 * A_WIDTH // tc, (2 * A_WIDTH + B_WIDTH) // tc
    win_rows = rows + CONV_PAD

    def body(a_ref, g_ref, d_ref, w_ref, da_ref, dg_ref, dw_ref, dcb_ref, hpad, dpad, dwacc):
        hpad[0:CONV_PAD, :] = jnp.zeros((CONV_PAD, tc), F32)
        dpad[t:t + CONV_PAD, :] = jnp.zeros((CONV_PAD, tc), F32)
        dwacc[...] = jnp.zeros(dwacc.shape, F32)

        def fill(i, _):
            r0 = pl.multiple_of(i * rows, rows)
            hpad[pl.ds(CONV_PAD + r0, rows), :] = a_ref[pl.ds(r0, rows), :] * _sigmoid(g_ref[pl.ds(r0, rows), :])
            dpad[pl.ds(r0, rows), :] = d_ref[pl.ds(r0, rows), :]
            return 0
        lax.fori_loop(0, t // rows, fill, 0)

        def step(i, dcb):
            r0 = pl.multiple_of(i * rows, rows)
            hwin = hpad[pl.ds(r0, win_rows), :]
            dwin = dpad[pl.ds(r0, win_rows), :]
            dchunk = dwin[:rows, :]
            dh = jnp.zeros((rows, tc), F32)
            for b in range(SUB):
                hb = hwin if b == 0 else pltpu.roll(hwin, b, 0)
                db = dwin if b == 0 else pltpu.roll(dwin, win_rows - b, 0)
                for a in range(CONV_PAD // SUB):
                    k = CONV_WIDTH - 1 - (SUB * a + b)
                    if k >= 0:
                        dh = dh + db[SUB * a:SUB * a + rows, :] * w_ref[k:k + 1, :]
                        lo = CONV_PAD - SUB * a
                        prod = dchunk * hb[lo:lo + rows, :]
                        dwacc[k] += jnp.sum(prod.reshape(rows // 8, 8, tc), axis=0)
            a = a_ref[pl.ds(r0, rows), :]
            sg = _sigmoid(g_ref[pl.ds(r0, rows), :])
            da_ref[pl.ds(r0, rows), :] = (dh * sg).astype(da_ref.dtype)
            dg_ref[pl.ds(r0, rows), :] = (dh * a * sg * (1.0 - sg)).astype(dg_ref.dtype)
            return dcb + jnp.sum(dchunk, axis=0, keepdims=True)
        dcb = lax.fori_loop(0, t // rows, step, jnp.zeros((1, tc), F32))
        dcb_ref[...] = dcb
        for k in range(CONV_WIDTH):
            dw_ref[k:k + 1, :] = jnp.sum(dwacc[k], axis=0, keepdims=True)

    return pl.pallas_call(
        body, grid=(B_WIDTH // tc,),
        in_specs=[pl.BlockSpec((t, tc), lambda j: (0, a_cb + j)), pl.BlockSpec((t, tc), lambda j: (0, g_cb + j)),
                  pl.BlockSpec((t, tc), lambda j: (0, j)), pl.BlockSpec((CONV_WIDTH, tc), lambda j: (0, j))],
        out_specs=[pl.BlockSpec((t, tc), lambda j: (0, j)), pl.BlockSpec((t, tc), lambda j: (0, j)),
                   pl.BlockSpec((CONV_WIDTH, tc), lambda j: (0, j)), pl.BlockSpec((1, tc), lambda j: (0, j))],
        out_shape=[S((t, B_WIDTH), BF16), S((t, B_WIDTH), BF16), S((CONV_WIDTH, B_WIDTH), F32), S((1, B_WIDTH), F32)],
        scratch_shapes=[pltpu.VMEM((t + CONV_PAD, tc), F32), pltpu.VMEM((t + CONV_PAD, tc), F32),
                        pltpu.VMEM((CONV_WIDTH, 8, tc), F32)],
        compiler_params=_cp("parallel"), name=name)(proj, proj, dhc, w)


def ln_silu_bwd(name, hc, dxb, w_out, g, b):
    c = hc.shape[1]

    def fn(h, dxv, wv, gv, bv):
        dout = lax.dot_general(dxv, wv[A_WIDTH:, :], _NT, preferred_element_type=F32)
        y, rstd = _ln_plain(h)
        z = y * gv + bv
        s = _sigmoid(z)
        dz = dout * s * (1.0 + z * (1.0 - s))
        dyv = dz * gv
        dh = rstd * (dyv - jnp.mean(dyv, axis=-1, keepdims=True) - y * jnp.mean(dyv * y, axis=-1, keepdims=True))
        return [dh], [jnp.sum(dz * y, axis=0, keepdims=True), jnp.sum(dz, axis=0, keepdims=True)]

    return rows_call(name, fn, [hc, dxb], [w_out, g, b], [(c, F32)], [(1, c), (1, c)])


_NT = (((1,), (1,)), ((), ()))
_TN = (((0,), (0,)), ((), ()))


def attn_fwd(name, x, gain, wq, k, v, wo, tm=512):
    t, d = x.shape
    m = k.shape[0]
    tm = _tile(t, tm)
    scale = CA_HEAD_DIM ** -0.5

    def body(x_ref, g_ref, wq_ref, k_ref, v_ref, wo_ref, x1_ref, xn_ref, r_ref, q_ref, o_ref):
        xv = x_ref[...]
        rv = lax.rsqrt(jnp.mean(xv * xv, axis=-1, keepdims=True) + EPS)
        xn = (xv * rv * g_ref[...]).astype(BF16)
        xn_ref[...] = xn
        r_ref[...] = rv
        q_ref[...] = jnp.dot(xn, wq_ref[...], preferred_element_type=F32).astype(BF16)
        for h in range(CA_HEADS):
            cs = slice(h * CA_HEAD_DIM, (h + 1) * CA_HEAD_DIM)
            s = lax.dot_general(q_ref[:, cs], k_ref[:, cs], _NT, preferred_element_type=F32) * scale
            e = jnp.exp(s - jnp.max(s, axis=-1, keepdims=True))
            p = e / jnp.sum(e, axis=-1, keepdims=True)
            o_ref[:, cs] = jnp.dot(_bf(p), v_ref[:, cs], preferred_element_type=F32).astype(o_ref.dtype)
        x1_ref[...] = xv + jnp.dot(o_ref[...], wo_ref[...], preferred_element_type=F32)

    def whole(a):
        return pl.BlockSpec(a.shape, lambda i: (0, 0), pipeline_mode=pl.Buffered(1))

    rows = pl.BlockSpec((tm, d), lambda i: (i, 0))
    col = pl.BlockSpec((tm, 1), lambda i: (i, 0))
    return pl.pallas_call(
        body, grid=(t // tm,),
        in_specs=[rows, whole(gain), whole(wq), whole(k), whole(v), whole(wo)],
        out_specs=[rows, rows, col, rows, rows],
        out_shape=[S((t, d), F32), S((t, d), BF16), S((t, 1), F32), S((t, d), BF16), S((t, d), BF16)],
        compiler_params=_cp("parallel"), name=name)(x, gain, wq, k, v, wo)


def attn_bwd(name, dx, dxb, x, r, gain, q, k, v, wq, wo, tm=512):
    t, d = q.shape
    m = k.shape[0]
    tm = _tile(t, tm)
    scale = CA_HEAD_DIM ** -0.5

    def body(dx_ref, dxb_ref, x_ref, r_ref, g_ref, q_ref, k_ref, v_ref, wq_ref, wo_ref,
             dxo_ref, dxbo_ref, dq_ref, dk_ref, dv_ref, dg_ref, do_s):
        @pl.when(pl.program_id(0) == 0)
        def _():
            dk_ref[...] = jnp.zeros(dk_ref.shape, F32)
            dv_ref[...] = jnp.zeros(dv_ref.shape, F32)
            dg_ref[...] = jnp.zeros(dg_ref.shape, F32)

        do_s[...] = lax.dot_general(dxb_ref[...], wo_ref[...], _NT, preferred_element_type=F32).astype(BF16)
        for h in range(CA_HEADS):
            cs = slice(h * CA_HEAD_DIM, (h + 1) * CA_HEAD_DIM)
            qh, kh, vh, doh = q_ref[:, cs], k_ref[:, cs], v_ref[:, cs], do_s[:, cs]
            s = lax.dot_general(qh, kh, _NT, preferred_element_type=F32) * scale
            e = jnp.exp(s - jnp.max(s, axis=-1, keepdims=True))
            p = e / jnp.sum(e, axis=-1, keepdims=True)
            pb = _bf(p)
            dv_ref[:, cs] += lax.dot_general(pb, doh, _TN, preferred_element_type=F32)
            dp = lax.dot_general(doh, vh, _NT, preferred_element_type=F32)
            ds = _bf(p * (dp - jnp.sum(dp * p, axis=-1, keepdims=True)) * scale)
            dq_ref[:, cs] = jnp.dot(ds, kh, preferred_element_type=F32).astype(dq_ref.dtype)
            dk_ref[:, cs] += lax.dot_general(ds, qh, _TN, preferred_element_type=F32)
        dxn = lax.dot_general(dq_ref[...], wq_ref[...], _NT, preferred_element_type=F32)
        xh = x_ref[...] * r_ref[...]
        wv = dxn * g_ref[...]
        dxo = dx_ref[...] + r_ref[...] * (wv - xh * jnp.mean(wv * xh, axis=-1, keepdims=True))
        dxo_ref[...] = dxo
        dxbo_ref[...] = dxo.astype(BF16)
        dg_ref[...] += jnp.sum(dxn * xh, axis=0, keepdims=True)

    def whole(a):
        return pl.BlockSpec(a.shape, lambda i: (0, 0), pipeline_mode=pl.Buffered(1))

    rows = pl.BlockSpec((tm, d), lambda i: (i, 0))
    col = pl.BlockSpec((tm, 1), lambda i: (i, 0))
    acc = pl.BlockSpec((m, d), lambda i: (0, 0))
    return pl.pallas_call(
        body, grid=(t // tm,),
        in_specs=[rows, rows, rows, col, whole(gain), rows, whole(k), whole(v), whole(wq), whole(wo)],
        out_specs=[rows, rows, rows, acc, acc, pl.BlockSpec((1, d), lambda i: (0, 0))],
        out_shape=[S((t, d), F32), S((t, d), BF16), S((t, d), BF16), S((m, d), F32), S((m, d), F32), S((1, d), F32)],
        scratch_shapes=[pltpu.VMEM((tm, d), BF16)],
        compiler_params=_cp("arbitrary"), name=name)(dx, dxb, x, r, gain, q, k, v, wq, wo)


SUB = 8
S5_ROWS = 256


S5_BLOCKS = 4
BLOCK_CH = C_WIDTH // S5_BLOCKS
BLOCK_ST = N_STATE // S5_BLOCKS
_S5_BLOCKS = tuple((slice(BLOCK_CH * q, BLOCK_CH * (q + 1)), slice(BLOCK_ST * q, BLOCK_ST * (q + 1)),
                    slice(N_STATE + BLOCK_ST * q, N_STATE + BLOCK_ST * (q + 1))) for q in range(S5_BLOCKS))
_HI = lax.Precision.HIGHEST
_GP = (C_GROUPS, C_STATE)
_RP = (C_WIDTH, C_STATE)


def _zoh(lr, li, ldt):
    dt = jnp.exp(ldt)
    mag = jnp.exp(lr * dt)
    ar = mag * jnp.cos(li * dt)
    ai = mag * jnp.sin(li * dt)
    den = lr * lr + li * li
    qr = ((ar - 1.0) * lr + ai * li) / den
    qi = (ai * lr - (ar - 1.0) * li) / den
    return dt, ar, ai, den, qr, qi


def _per_channel(v):
    return jnp.broadcast_to(v[:, None, :], (C_GROUPS, C_GROUP_CH, C_STATE)).reshape(_RP)


def _same_group(shape, row_per_group, col_per_group):
    rows = lax.broadcasted_iota(jnp.int32, shape, 0) // row_per_group
    cols = lax.broadcasted_iota(jnp.int32, shape, 1) // col_per_group
    return rows == cols


def _spread(shape, axis):
    long = lax.broadcasted_iota(jnp.int32, shape, axis) % C_STATE
    short = lax.broadcasted_iota(jnp.int32, shape, 1 - axis)
    return long == short


def s5_discretise(name, lam_re, lam_im, log_dt, bt_re, bt_im):
    def body(lr_ref, li_ref, ldt_ref, btr_ref, bti_ref, a_ref, bbr_ref, bbi_ref):
        _, ar, ai, _, qr, qi = _zoh(lr_ref[...], li_ref[...], ldt_ref[...])
        a_ref[0] = ar
        a_ref[1] = ai
        q2r, q2i = _per_channel(qr), _per_channel(qi)
        btr, bti = btr_ref[...], bti_ref[...]
        bbr_ref[...] = q2r * btr - q2i * bti
        bbi_ref[...] = q2r * bti + q2i * btr

    return pl.pallas_call(body, out_shape=[S((2,) + _GP, F32), S(_RP, F32), S(_RP, F32)],
                          name=name)(lam_re, lam_im, log_dt, bt_re, bt_im)


def s5_operands(name, a, bbr, bbi, c2r, c2i, ctr, cti):
    ns = N_STATE

    def body(a_ref, bbr_ref, bbi_ref, c2r_ref, c2i_ref, ctr_ref, cti_ref, pw_ref, qw_ref, mb_ref, mc_ref, mct_ref):
        ar, ai = a_ref[0:1, :], a_ref[1:2, :]
        pows = [(ar, ai)]
        for _ in range(SUB - 1):
            pr, pi = pows[-1]
            pows.append((pr * ar - pi * ai, pr * ai + pi * ar))
        rows = lax.broadcasted_iota(jnp.int32, (SUB, ns), 0)

        def rows_of(v):
            return jnp.broadcast_to(v, (SUB, ns))

        for k, s in enumerate((1, 2, 4)):
            pr, pi = rows_of(pows[s - 1][0]), rows_of(pows[s - 1][1])
            pw_ref[k, 0] = jnp.where(rows >= s, pr, 0.0)
            pw_ref[k, 1] = jnp.where(rows >= s, pi, 0.0)
            qw_ref[k, 0] = jnp.where(rows + s <= SUB - 1, pr, 0.0)
            qw_ref[k, 1] = jnp.where(rows + s <= SUB - 1, -pi, 0.0)
        fr = fi = br = bi = jnp.zeros((SUB, ns), F32)
        for i in range(SUB):
            fr = jnp.where(rows == i, rows_of(pows[i][0]), fr)
            fi = jnp.where(rows == i, rows_of(pows[i][1]), fi)
            br = jnp.where(rows == i, rows_of(pows[SUB - 1 - i][0]), br)
            bi = jnp.where(rows == i, rows_of(-pows[SUB - 1 - i][1]), bi)
        pw_ref[3, 0], pw_ref[3, 1], qw_ref[3, 0], qw_ref[3, 1] = fr, fi, br, bi

        wide = _spread((C_STATE, ns), 1).astype(BF16)
        tall = _spread((ns, C_STATE), 0).astype(BF16)
        in_rows = _same_group((C_WIDTH, ns), C_GROUP_CH, C_STATE)
        in_cols = _same_group((ns, C_WIDTH), C_STATE, C_GROUP_CH)

        def across(v, sign=1.0):
            return jnp.where(in_rows, sign * jnp.dot(_bf(v), wide, preferred_element_type=F32), 0.0).astype(BF16)

        def down(vt, sign=1.0):
            return jnp.where(in_cols, sign * jnp.dot(tall, _bf(vt), preferred_element_type=F32), 0.0).astype(BF16)

        mb_ref[:, 0:ns] = across(bbr_ref[...])
        mb_ref[:, ns:2 * ns] = across(bbi_ref[...])
        mct_ref[:, 0:ns] = across(c2r_ref[...])
        mct_ref[:, ns:2 * ns] = across(c2i_ref[...], -1.0)
        mc_ref[0:ns, :] = down(ctr_ref[...])
        mc_ref[ns:2 * ns, :] = down(cti_ref[...], -1.0)

    return pl.pallas_call(
        body, out_shape=[S((4, 2, SUB, ns), F32), S((4, 2, SUB, ns), F32), S((C_WIDTH, 2 * ns), BF16),
                         S((2 * ns, C_WIDTH), BF16), S((C_WIDTH, 2 * ns), BF16)],
        compiler_params=pltpu.CompilerParams(vmem_limit_bytes=VMEM_LIMIT), name=name)(a, bbr, bbi, c2r, c2i, ctr, cti)


def s5_block_grads(name, u, lamb, xsb, dyb):
    t = u.shape[0]

    def mb_body(u_ref, lr_ref, li_ref, o_ref):
        ub = _bf(u_ref[...])
        o_ref[:, 0:BLOCK_ST] = lax.dot_general(ub, lr_ref[...], _TN, preferred_element_type=F32)
        o_ref[:, BLOCK_ST:2 * BLOCK_ST] = lax.dot_general(ub, li_ref[...], _TN, preferred_element_type=F32)

    d_mb = pl.pallas_call(
        mb_body, grid=(S5_BLOCKS,),
        in_specs=[pl.BlockSpec((t, BLOCK_CH), lambda q: (0, q)), pl.BlockSpec((t, BLOCK_ST), lambda q: (0, q)),
                  pl.BlockSpec((t, BLOCK_ST), lambda q: (0, S5_BLOCKS + q))],
        out_specs=pl.BlockSpec((BLOCK_CH, 2 * BLOCK_ST), lambda q: (q, 0)), out_shape=S((C_WIDTH, 2 * BLOCK_ST), F32),
        compiler_params=_cp("parallel"), name=name + "_b")(u, lamb, lamb)

    def mc_body(x_ref, dy_ref, o_ref):
        o_ref[...] = lax.dot_general(x_ref[...], dy_ref[...], _TN, preferred_element_type=F32)

    d_mc = pl.pallas_call(
        mc_body, grid=(2, S5_BLOCKS),
        in_specs=[pl.BlockSpec((t, BLOCK_ST), lambda p, q: (0, p * S5_BLOCKS + q)), pl.BlockSpec((t, BLOCK_CH), lambda p, q: (0, q))],
        out_specs=pl.BlockSpec((BLOCK_ST, BLOCK_CH), lambda p, q: (p * S5_BLOCKS + q, 0)),
        out_shape=S((2 * N_STATE, BLOCK_CH), F32), compiler_params=_cp("parallel", "parallel"), name=name + "_c")(xsb, dyb)
    return d_mb, d_mc


def s5_param_grads(name, d_mb, d_mc, da, lam_re, lam_im, log_dt, bt_re, bt_im):
    ns = N_STATE

    def body(dmb_ref, dmc_ref, da_ref, lr_ref, li_ref, ldt_ref, btr_ref, bti_ref,
             glr_ref, gli_ref, gdt_ref, gbr_ref, gbi_ref, gcr_ref, gci_ref):
        lr, li = lr_ref[...], li_ref[...]
        dt, ar, ai, den, qr, qi = _zoh(lr, li, ldt_ref[...])
        per_block = C_GROUPS // S5_BLOCKS
        wide = _spread((C_STATE, BLOCK_ST), 1).astype(F32)
        tall = _spread((BLOCK_ST, C_STATE), 0).astype(F32)
        rows = lax.broadcasted_iota(jnp.int32, (C_WIDTH, BLOCK_ST), 0) // C_GROUP_CH % per_block
        in_rows = rows == lax.broadcasted_iota(jnp.int32, (C_WIDTH, BLOCK_ST), 1) // C_STATE
        in_cols = _same_group((BLOCK_ST, BLOCK_CH), C_STATE, C_GROUP_CH)

        def fold_rows(v):
            return lax.dot_general(jnp.where(in_rows, v, 0.0), wide, (((1,), (1,)), ((), ())), precision=_HI,
                                   preferred_element_type=F32)

        def fold_cols(v):
            return lax.dot_general(jnp.where(in_cols, v, 0.0), tall, (((0,), (0,)), ((), ())), precision=_HI,
                                   preferred_element_type=F32)

        for cs, s_re, s_im in _S5_BLOCKS:
            gcr_ref[cs, :] = fold_cols(dmc_ref[s_re, :])
            gci_ref[cs, :] = -fold_cols(dmc_ref[s_im, :])
        gbbr = fold_rows(dmb_ref[:, 0:BLOCK_ST])
        gbbi = fold_rows(dmb_ref[:, BLOCK_ST:2 * BLOCK_ST])
        btr, bti = btr_ref[...], bti_ref[...]
        q2r, q2i = _per_channel(qr), _per_channel(qi)
        gbr_ref[...] = q2r * gbbr + q2i * gbbi
        gbi_ref[...] = q2r * gbbi - q2i * gbbr

        def per_group(v):
            return jnp.sum(v.reshape(C_GROUPS, C_GROUP_CH, C_STATE), axis=1)

        gqr = per_group(btr * gbbr + bti * gbbi)
        gqi = per_group(btr * gbbi - bti * gbbr)
        ilr, ili = lr / den, li / den
        gar = da_ref[0] + ilr * gqr - ili * gqi
        gai = da_ref[1] + ilr * gqi + ili * gqr
        sr = (qr * lr + qi * li) / den
        si = (qi * lr - qr * li) / den
        gzr = ar * gar + ai * gai
        gzi = ar * gai - ai * gar
        glr_ref[...] = -sr * gqr - si * gqi + dt * gzr
        gli_ref[...] = -sr * gqi + si * gqr + dt * gzi
        gdt_ref[...] = jnp.sum(lr * gzr + li * gzi, axis=1, keepdims=True) * dt

    return pl.pallas_call(
        body, out_shape=[S(_GP, F32), S(_GP, F32), S((C_GROUPS, 1), F32), S(_RP, F32), S(_RP, F32), S(_RP, F32), S(_RP, F32)],
        compiler_params=pltpu.CompilerParams(vmem_limit_bytes=VMEM_LIMIT), name=name,
    )(d_mb, d_mc, da, lam_re, lam_im, log_dt, bt_re, bt_im)


def _cmul_add(xr, xi, pr, pi, zr, zi):
    return xr + pr * zr - pi * zi, xi + pr * zi + pi * zr


def s5_fwd(name, x, gain, w_in, mb, mc, pw, dskip, w_out_t):
    t, d = x.shape
    tm = _tile(t, S5_ROWS)
    ns = N_STATE

    def body(x_ref, g_ref, wi_ref, mb_ref, mc_ref, pw_ref, d_ref, wo_ref,
             x1_ref, hn_ref, r_ref, u_ref, gy_ref, y_ref, xs_ref, xb_ref, o1_ref, o2_ref, carry):
        @pl.when(pl.program_id(0) == 0)
        def _():
            carry[...] = jnp.zeros(carry.shape, F32)

        xv = x_ref[...]
        rv = lax.rsqrt(jnp.mean(xv * xv, axis=-1, keepdims=True) + EPS)
        hn = (xv * rv * g_ref[...]).astype(BF16)
        hn_ref[...] = hn
        r_ref[...] = rv
        uv = jnp.dot(hn, wi_ref[...], preferred_element_type=F32)
        u_ref[...] = uv
        ub = _bf(uv)
        for cs, s_re, s_im in _S5_BLOCKS:
            xs_ref[:, s_re] = jnp.dot(ub[:, cs], mb_ref[cs, s_re], preferred_element_type=F32)
            xs_ref[:, s_im] = jnp.dot(ub[:, cs], mb_ref[cs, s_im], preferred_element_type=F32)

        def group(i, _):
            r0 = pl.multiple_of(i * SUB, SUB)
            xr = xs_ref[pl.ds(r0, SUB), 0:ns]
            xi = xs_ref[pl.ds(r0, SUB), ns:2 * ns]
            for k, s in enumerate((1, 2, 4)):
                xr, xi = _cmul_add(xr, xi, pw_ref[k, 0], pw_ref[k, 1], pltpu.roll(xr, s, 0), pltpu.roll(xi, s, 0))
            xr, xi = _cmul_add(xr, xi, pw_ref[3, 0], pw_ref[3, 1], carry[0], carry[1])
            xs_ref[pl.ds(r0, SUB), 0:ns] = xr
            xs_ref[pl.ds(r0, SUB), ns:2 * ns] = xi
            carry[0] = jnp.broadcast_to(xr[SUB - 1:SUB, :], (SUB, ns))
            carry[1] = jnp.broadcast_to(xi[SUB - 1:SUB, :], (SUB, ns))
            return 0
        lax.fori_loop(0, tm // SUB, group, 0)

        xb_ref[...] = _bf(xs_ref[...])
        for cs, s_re, s_im in _S5_BLOCKS:
            y = (jnp.dot(xb_ref[:, s_re], mc_ref[s_re, cs], preferred_element_type=F32)
                 + jnp.dot(xb_ref[:, s_im], mc_ref[s_im, cs], preferred_element_type=F32) + d_ref[:, cs] * uv[:, cs])
            y_ref[:, cs] = y
            gy_ref[:, cs] = _gelu(y).astype(gy_ref.dtype)
        o1 = lax.dot_general(gy_ref[...], wo_ref[0:d, :], _NT, preferred_element_type=F32)
        o2 = lax.dot_general(gy_ref[...], wo_ref[d:2 * d, :], _NT, preferred_element_type=F32)
        o1_ref[...] = o1.astype(BF16)
        o2_ref[...] = o2.astype(BF16)
        x1_ref[...] = xv + o1 * _sigmoid(o2)

    c = w_in.shape[1]
    rows = pl.BlockSpec((tm, d), lambda i: (i, 0))
    narrow = pl.BlockSpec((tm, c), lambda i: (i, 0))
    states = pl.BlockSpec((tm, 2 * ns), lambda i: (i, 0))
    return pl.pallas_call(
        body, grid=(t // tm,),
        in_specs=[rows, _whole(gain), _whole(w_in), _whole(mb), _whole(mc), _whole(pw), _whole(dskip), _whole(w_out_t)],
        out_specs=[rows, rows, pl.BlockSpec((tm, 1), lambda i: (i, 0)), narrow, narrow, narrow, states, states, rows, rows],
        out_shape=[S((t, d), F32), S((t, d), BF16), S((t, 1), F32), S((t, c), F32), S((t, c), BF16), S((t, c), F32),
                   S((t, 2 * ns), F32), S((t, 2 * ns), BF16), S((t, d), BF16), S((t, d), BF16)],
        scratch_shapes=[pltpu.VMEM((2, SUB, ns), F32)],
        compiler_params=_cp("arbitrary"), name=name)(x, gain, w_in, mb, mc, pw, dskip, w_out_t)


def s5_bwd(name, dgy, y, u, xs, mct, mbt, qw, dskip):
    t, c = u.shape
    tm = _tile(t, S5_ROWS)
    nt = t // tm
    ns = N_STATE
    ng = tm // SUB

    def body(dgy_ref, y_ref, u_ref, xs_ref, mct_ref, mbt_ref, qw_ref, d_ref,
             du_ref, dy_ref, lb_ref, da_ref, dd_ref, lam, carry):
        @pl.when(pl.program_id(0) == 0)
        def _():
            carry[...] = jnp.zeros(carry.shape, F32)
            da_ref[...] = jnp.zeros(da_ref.shape, F32)
            dd_ref[...] = jnp.zeros(dd_ref.shape, F32)

        uv = u_ref[...]
        dy = dgy_ref[...] * _gelu_grad(y_ref[...])
        dyb = _bf(dy)
        dy_ref[...] = dyb
        dd_ref[...] += jnp.sum(dy * uv, axis=0, keepdims=True)
        for cs, s_re, s_im in _S5_BLOCKS:
            lam[:, s_re] = jnp.dot(dyb[:, cs], mct_ref[cs, s_re], preferred_element_type=F32)
            lam[:, s_im] = jnp.dot(dyb[:, cs], mct_ref[cs, s_im], preferred_element_type=F32)
        last_row = lax.broadcasted_iota(jnp.int32, (SUB, ns), 0) == SUB - 1

        def group(j, _):
            i = ng - 1 - j
            r0 = pl.multiple_of(i * SUB, SUB)
            lr = lam[pl.ds(r0, SUB), 0:ns]
            li = lam[pl.ds(r0, SUB), ns:2 * ns]
            for k, s in enumerate((1, 2, 4)):
                lr, li = _cmul_add(lr, li, qw_ref[k, 0], qw_ref[k, 1],
                                   pltpu.roll(lr, SUB - s, 0), pltpu.roll(li, SUB - s, 0))
            cr, ci = carry[0], carry[1]
            lr, li = _cmul_add(lr, li, qw_ref[3, 0], qw_ref[3, 1], cr, ci)
            lam[pl.ds(r0, SUB), 0:ns] = lr
            lam[pl.ds(r0, SUB), ns:2 * ns] = li
            carry[0] = jnp.broadcast_to(lr[0:1, :], (SUB, ns))
            carry[1] = jnp.broadcast_to(li[0:1, :], (SUB, ns))
            nr = jnp.where(last_row, cr, pltpu.roll(lr, SUB - 1, 0))
            ni = jnp.where(last_row, ci, pltpu.roll(li, SUB - 1, 0))
            xr = xs_ref[pl.ds(r0, SUB), 0:ns]
            xi = xs_ref[pl.ds(r0, SUB), ns:2 * ns]
            da_ref[0] += nr * xr + ni * xi
            da_ref[1] += ni * xr - nr * xi
            return 0
        lax.fori_loop(0, ng, group, 0)

        lb_ref[...] = _bf(lam[...])
        for cs, s_re, s_im in _S5_BLOCKS:
            du = (jnp.dot(lb_ref[:, s_re], mbt_ref[s_re, cs], preferred_element_type=F32)
                  + jnp.dot(lb_ref[:, s_im], mbt_ref[s_im, cs], preferred_element_type=F32) + d_ref[:, cs] * dy[:, cs])
            du_ref[:, cs] = du.astype(du_ref.dtype)

    rev = lambda i: (nt - 1 - i, 0)
    return pl.pallas_call(
        body, grid=(nt,),
        in_specs=[pl.BlockSpec((tm, c), rev), pl.BlockSpec((tm, c), rev), pl.BlockSpec((tm, c), rev),
                  pl.BlockSpec((tm, 2 * ns), rev),
                  pl.BlockSpec(mct.shape, lambda i: (0, 0)), pl.BlockSpec(mbt.shape, lambda i: (0, 0)),
                  pl.BlockSpec(qw.shape, lambda i: (0, 0, 0, 0)), pl.BlockSpec((1, c), lambda i: (0, 0))],
        out_specs=[pl.BlockSpec((tm, c), rev), pl.BlockSpec((tm, c), rev), pl.BlockSpec((tm, 2 * ns), rev),
                   pl.BlockSpec((2, SUB, ns), lambda i: (0, 0, 0)), pl.BlockSpec((1, c), lambda i: (0, 0))],
        out_shape=[S((t, c), BF16), S((t, c), BF16), S((t, 2 * ns), BF16), S((2, SUB, ns), F32), S((1, c), F32)],
        scratch_shapes=[pltpu.VMEM((tm, 2 * ns), F32), pltpu.VMEM((2, SUB, ns), F32)],
        compiler_params=_cp("arbitrary"), name=name)(dgy, y, u, xs, mct, mbt, qw, dskip)


def _first(accs, *_):
    return [accs[0]]


def _rms_bwd_epi(accs, xv, base, rv, g):
    dv = accs[0]
    w = dv * g
    xh = xv * rv
    dx = base + rv * (w - xh * jnp.mean(w * xh, axis=-1, keepdims=True))
    return [dx, dx, jnp.sum(dv * xh, axis=0, keepdims=True)]


def mm_rms_bwd(name, pairs, x, r, gain, dres):
    t, d = x.shape
    return mm_nn(name, t, d, pairs, 1, _rms_bwd_epi, [F32, BF16], tiled=[x, dres], cols=[r], rowv=[gain], sums=[(1, d)])


def even_fwd(x, w, need_out):
    t = x.shape[0]
    proj, hn, r = mm_nn("e_in_f", t, IN_WIDTH, [(x, w["e_w_in_t"], 0, "t")], 1, _first, [F32], norm_gain=w["e_norm"])
    hc = conv_fwd("e_conv_f", proj, w["e_conv_w"], w["e_conv_b"])
    need_out(hc)
    x1, out_a, out_b = even_out_fwd("e_out_f", proj, hc, x, w["e_gmlp_w"], w["e_gmlp_b"], w["e_conv_ln_g"], w["e_conv_ln_b"],
                                    w["e_w_out"])
    return x1, (x, hn, r, proj, out_a, hc, out_b)


def even_bwd_mixers(dxb, saved, w):
    x, hn, r, proj, out_a, hc, out_b = saved
    t = x.shape[0]
    g_w_out = jnp.concatenate([mm_tn("e_out_wa", out_a, dxb), mm_tn("e_out_wb", out_b, dxb)], axis=0)
    dab, g_gw, g_gb = gmlp_bwd("e_gmlp_b", proj, dxb, w["e_w_out"], w["e_gmlp_w"], w["e_gmlp_b"])
    dhc, g_lg, g_lb = ln_silu_bwd("e_ln_b", hc, dxb, w["e_w_out"], w["e_conv_ln_g"], w["e_conv_ln_b"])
    dba, dbg, g_cw, g_cb = conv_bwd("e_conv_b", proj, dhc, w["e_conv_w"])
    g_w_in_t = jnp.concatenate([mm_tn("e_in_w0", dab, hn), mm_tn("e_in_w1", dba, hn), mm_tn("e_in_w2", dbg, hn)], axis=0)
    grads = dict(e_w_in_t=g_w_in_t, e_gmlp_w=g_gw[None], e_gmlp_b=g_gb.reshape(1, A_GROUPS, GMLP_BLOCK),
                 e_conv_w=g_cw[None], e_conv_b=g_cb, e_conv_ln_g=g_lg, e_conv_ln_b=g_lb, e_w_out=g_w_out)
    return (dab, dba, dbg), grads


def even_bwd_input(dx, dproj, saved, w):
    x, _, r = saved[:3]
    dab, dba, dbg = dproj
    w_in_t = w["e_w_in_t"]
    return mm_rms_bwd("e_in_b", [(dab, (w_in_t, 0), 0), (dba, (w_in_t, 2), 0), (dbg, (w_in_t, 3), 0)], x, r, w["e_norm"], dx)


def s5_setup(w, anchor=None):
    def rows(v):
        return v.transpose(0, 2, 1).reshape(_RP)

    log_dt = w["o_log_dt"].reshape(C_GROUPS, 1)
    if anchor is not None:
        log_dt = log_dt + anchor
    lam = (w["o_lam_re"], w["o_lam_im"], log_dt, rows(w["o_b_re"]), rows(w["o_b_im"]))
    a, bbr, bbi = s5_discretise("o_s5_zoh", *lam)
    c_re, c_im = w["o_c_re"], w["o_c_im"]
    pw, qw, mb, mc, mct = s5_operands("o_s5_ops", a.reshape(2, N_STATE), bbr, bbi, c_re.reshape(_RP), c_im.reshape(_RP),
                                      c_re.transpose(2, 0, 1).reshape(C_STATE, C_WIDTH),
                                      c_im.transpose(2, 0, 1).reshape(C_STATE, C_WIDTH))
    return dict(lam=lam, pw=pw, qw=qw, mb=mb, mc=mc, mct=mct, mbt=mb.T)


def odd_fwd(x, w, consts):
    x1, hn, r, u, gy, y, xs, xsb, o1, o2 = s5_fwd("o_s5_f", x, w["o_norm"], w["o_w_in"], consts["mb"], consts["mc"],
                                                  consts["pw"], w["o_d"], w["o_w_out_t"])
    return x1, (x, hn, r, u, gy, y, xs, xsb, o1, o2)


def odd_bwd(dx, dxb, saved, w, consts):
    x, hn, r, u, gy, y, xs, xsb, o1, o2 = saved
    t = x.shape[0]

    def gate_bwd(dv, a, b, wv):
        a = a.astype(F32)
        sg = _sigmoid(b.astype(F32))
        do12 = jnp.concatenate([dv * sg, dv * a * sg * (1.0 - sg)], axis=1).astype(BF16)
        return [do12, jnp.dot(do12, wv, preferred_element_type=F32)], []

    do12, dgy = rows_call("o_out_b", gate_bwd, [dx, o1, o2], [w["o_w_out_t"]], [(2 * D_MODEL, BF16), (C_WIDTH, F32)], [])
    g_w_out_t = mm_tn("o_out_w", do12, gy)
    du, dyb, lamb, da8, g_d = s5_bwd("o_s5_b", dgy, y, u, xs, consts["mct"], consts["mbt"], consts["qw"], w["o_d"])
    d_mb, d_mc = s5_block_grads("o_s5_w", u, lamb, xsb, dyb)
    da = jnp.sum(da8, axis=1).reshape((2,) + _GP)
    g_lr, g_li, g_dt, g_btr, g_bti, g_cr, g_ci = s5_param_grads("o_s5_pg", d_mb, d_mc, da, *consts["lam"])

    def states_first(v):
        return v.reshape(C_GROUPS, C_GROUP_CH, C_STATE).transpose(0, 2, 1)[None]

    g_w_in = mm_tn("o_in_w", hn, du)
    dx0, dx0b, g_norm = mm_rms_bwd("o_in_b", [(du, w["o_w_in"], 0, "t")], x, r, w["o_norm"], dx)
    grads = dict(o_norm=g_norm, o_w_in=g_w_in, o_lam_re=g_lr[None], o_lam_im=g_li[None], o_log_dt=g_dt.reshape(1, C_GROUPS),
                 o_b_re=states_first(g_btr), o_b_im=states_first(g_bti),
                 o_c_re=g_cr.reshape((1, C_GROUPS, C_GROUP_CH, C_STATE)), o_c_im=g_ci.reshape((1, C_GROUPS, C_GROUP_CH, C_STATE)),
                 o_d=g_d, o_w_out_t=g_w_out_t)
    return dx0, dx0b, grads


def ca_fwd(i, x, mem, w):
    t, m = x.shape[0], mem.shape[0]
    k, v, mn, rm = mm_nn(f"ca{i}_kv_f", m, D_MODEL, [(mem, w["ca_wk"][i], 0), (mem, w["ca_wv"][i], 1)], 2,
                         lambda accs: [accs[0], accs[1]], [BF16, BF16], norm_gain=w["ca_mem_norm"][i:i + 1])
    x1, xn, r, q, o = attn_fwd(f"ca{i}_attn_f", x, w["ca_norm"][i:i + 1], w["ca_wq"][i], k, v, w["ca_wo"][i])
    return x1, (x, xn, r, mn, rm, q, k, v, o)


def ca_bwd(i, dx, dxb, saved, mem, w):
    x, xn, r, mn, rm, q, k, v, o = saved
    t, m = x.shape[0], mem.shape[0]
    g_wo = mm_tn(f"ca{i}_o_w", o, dxb)
    dx0, dx0b, dq, dk, dv, g_norm = attn_bwd(f"ca{i}_attn_b", dx, dxb, x, r, w["ca_norm"][i:i + 1], q, k, v,
                                             w["ca_wq"][i], w["ca_wo"][i])
    g_wq = mm_tn(f"ca{i}_q_w", xn, dq)
    g_wk = mm_tn(f"ca{i}_k_w", mn, dk)
    g_wv = mm_tn(f"ca{i}_v_w", mn, dv)
    (dmn,) = mm_nn(f"ca{i}_kv_b", m, D_MODEL, [(dk, w["ca_wk"][i], 0, "t"), (dv, w["ca_wv"][i], 0, "t")], 1, _first, [F32])
    g_mnorm = rms_bwd_gain_only(f"ca{i}_mnorm_b", dmn, mem, rm)
    return dx0, dx0b, dict(ca_norm=g_norm, ca_mem_norm=g_mnorm, ca_wq=g_wq, ca_wk=g_wk, ca_wv=g_wv, ca_wo=g_wo)


FFN_ROWS = 512
FFN_CHUNK = 256


def _whole(a):
    return pl.BlockSpec(a.shape, lambda i: (0,) * a.ndim, pipeline_mode=pl.Buffered(1))


def ffn_fused_fwd(name, x, gain, wg_t, wu_t, wd, target=None, final_gain=None):
    t, d = x.shape
    hid = wd.shape[0]
    tm = _tile(t, FFN_ROWS)
    last = target is not None
    n_main = 4 if last else 1

    def body(*refs):
        x_ref, g_ref, wg_ref, wu_ref, wd_ref = refs[:5]
        rest = refs[5:]
        if last:
            tgt_ref, fg_ref = rest[:2]
            rest = rest[2:]
        main, (xn_ref, r_ref, dgate_ref, dup_ref, h_ref) = rest[:n_main], rest[n_main:]
        xv = x_ref[...]
        rv = lax.rsqrt(jnp.mean(xv * xv, axis=-1, keepdims=True) + EPS)
        xn = (xv * rv * g_ref[...]).astype(BF16)
        xn_ref[...] = xn
        r_ref[...] = rv
        for j in range(hid // FFN_CHUNK):
            cs = slice(j * FFN_CHUNK, (j + 1) * FFN_CHUNK)
            g = lax.dot_general(xn, wg_ref[cs, :], _NT, preferred_element_type=F32)
            u = lax.dot_general(xn, wu_ref[cs, :], _NT, preferred_element_type=F32)
            s = _sigmoid(g)
            silu = g * s
            dgate_ref[:, cs] = (u * (s + silu * (1.0 - s))).astype(BF16)
            dup_ref[:, cs] = silu.astype(BF16)
            h_ref[:, cs] = (silu * u).astype(BF16)
        acc = jnp.dot(h_ref[...], wd_ref[...], preferred_element_type=F32)
        if not last:
            main[0][...] = xv + acc
        else:
            dx, _, dgain, part = _final_loss_epi([acc], xv, tgt_ref[...], fg_ref[...])

            @pl.when(pl.program_id(0) == 0)
            def _():
                main[2][...] = jnp.zeros(main[2].shape, F32)
                main[3][...] = jnp.zeros(main[3].shape, F32)
            main[0][...] = dx
            main[1][...] = dx.astype(BF16)
            main[2][...] += dgain
            main[3][...] += part

    rows = pl.BlockSpec((tm, d), lambda i: (i, 0))
    wide = pl.BlockSpec((tm, hid), lambda i: (i, 0))
    col = pl.BlockSpec((tm, 1), lambda i: (i, 0))
    ins, in_specs = [x, gain, wg_t, wu_t, wd], [rows, _whole(gain), _whole(wg_t), _whole(wu_t), _whole(wd)]
    if last:
        ins += [target, final_gain]
        in_specs += [rows, _whole(final_gain)]
        out_specs = [rows, rows, pl.BlockSpec((1, d), lambda i: (0, 0)), pl.BlockSpec((1, 1), lambda i: (0, 0))]
        out_shape = [S((t, d), F32), S((t, d), BF16), S((1, d), F32), S((1, 1), F32)]
    else:
        out_specs, out_shape = [rows], [S((t, d), F32)]
    out_specs += [rows, col, wide, wide, wide]
    out_shape += [S((t, d), BF16), S((t, 1), F32)] + [S((t, hid), BF16)] * 3
    outs = pl.pallas_call(body, grid=(t // tm,), in_specs=in_specs, out_specs=out_specs, out_shape=out_shape,
                          compiler_params=_cp("arbitrary" if last else "parallel"), name=name)(*ins)
    return (tuple(outs[:4]) if last else outs[0]), outs[n_main:]


def ffn_fused_bwd(name, dx, dxb, x, r, gain, dgate, dup, wg_t, wu_t, wd):
    t, d = x.shape
    hid = wd.shape[0]
    tm = _tile(t, FFN_ROWS // 2)

    def body(dx_ref, dxb_ref, x_ref, r_ref, g_ref, dgate_ref, dup_ref, wg_ref, wu_ref, wd_ref,
             dxo_ref, dxbo_ref, dg_ref, du_ref, dgain_ref):
        @pl.when(pl.program_id(0) == 0)
        def _():
            dgain_ref[...] = jnp.zeros(dgain_ref.shape, F32)

        dxb = dxb_ref[...]
        for j in range(hid // FFN_CHUNK):
            cs = slice(j * FFN_CHUNK, (j + 1) * FFN_CHUNK)
            dh = lax.dot_general(dxb, wd_ref[cs, :], _NT, preferred_element_type=F32)
            dg_ref[:, cs] = (dh * dgate_ref[:, cs].astype(F32)).astype(BF16)
            du_ref[:, cs] = (dh * dup_ref[:, cs].astype(F32)).astype(BF16)
        dxn = (jnp.dot(dg_ref[...], wg_ref[...], preferred_element_type=F32)
               + jnp.dot(du_ref[...], wu_ref[...], preferred_element_type=F32))
        dxo, _, dgain = _rms_bwd_epi([dxn], x_ref[...], dx_ref[...], r_ref[...], g_ref[...])
        dxo_ref[...] = dxo
        dxbo_ref[...] = dxo.astype(BF16)
        dgain_ref[...] += dgain

    rows = pl.BlockSpec((tm, d), lambda i: (i, 0))
    wide = pl.BlockSpec((tm, hid), lambda i: (i, 0))
    col = pl.BlockSpec((tm, 1), lambda i: (i, 0))
    return pl.pallas_call(
        body, grid=(t // tm,),
        in_specs=[rows, rows, rows, col, _whole(gain), wide, wide, _whole(wg_t), _whole(wu_t), _whole(wd)],
        out_specs=[rows, rows, wide, wide, pl.BlockSpec((1, d), lambda i: (0, 0))],
        out_shape=[S((t, d), F32), S((t, d), BF16), S((t, hid), BF16), S((t, hid), BF16), S((1, d), F32)],
        compiler_params=_cp("arbitrary"), name=name)(dx, dxb, x, r, gain, dgate, dup, wg_t, wu_t, wd)


def ffn_fwd(i, x, w, target=None):
    out, (xn, r, dgate, dup, h) = ffn_fused_fwd(f"ffn{i}_f", x, w["ffn_norm"][i:i + 1], w["ffn_w_gate_t"][i],
                                                w["ffn_w_up_t"][i], w["ffn_w_down"][i], target,
                                                None if target is None else w["final_norm"])
    return out, (x, xn, r, dgate, dup, h)


def ffn_bwd(i, dx, dxb, saved, w):
    x, xn, r, dgate, dup, h = saved
    g_wd = mm_tn(f"ffn{i}_down_w", h, dxb)
    dx0, dx0b, dg, du, g_norm = ffn_fused_bwd(f"ffn{i}_b", dx, dxb, x, r, w["ffn_norm"][i:i + 1], dgate, dup,
                                              w["ffn_w_gate_t"][i], w["ffn_w_up_t"][i], w["ffn_w_down"][i])
    g_wg_t = mm_tn(f"ffn{i}_gate_w", dg, xn)
    g_wu_t = mm_tn(f"ffn{i}_up_w", du, xn)
    return dx0, dx0b, dict(ffn_norm=g_norm, ffn_w_gate_t=g_wg_t, ffn_w_up_t=g_wu_t, ffn_w_down=g_wd)


def local_step(x, mem, target, w, fetch=None, on_grads=None, anchor=None):
    consts = s5_setup(w, anchor)

    def need(stage, after):
        if fetch is not None:
            for k, v in fetch(stage, after).items():
                if isinstance(k, tuple):
                    w.setdefault(k[0], {})[k[1]] = v
                else:
                    w[k] = v

    need(0, consts["pw"])
    x1, s_e = even_fwd(x, w, lambda after: need(1, after))
    x2, s_c0 = ca_fwd(0, x1, mem, w)
    need(2, x2)
    x3, s_f0 = ffn_fwd(0, x2, w)
    need(3, x3)
    x4, s_o = odd_fwd(x3, w, consts)
    need(4, x4)
    x5, s_c1 = ca_fwd(1, x4, mem, w)
    need(5, x5)
    (dx, dxb, g_final, loss), s_f1 = ffn_fwd(1, x5, w, target)

    def emit(stage, carry, plain, layered=None, layer=0):
        if on_grads is None:
            return carry
        out = dict(plain)
        out.update({(k, layer): v for k, v in (layered or {}).items()})
        return on_grads(stage, out, list(carry))

    dx, dxb, g_f1 = ffn_bwd(1, dx, dxb, s_f1, w)
    dx, dxb = emit(0, (dx, dxb), {}, g_f1, 1)
    dx, dxb, g_c1 = ca_bwd(1, dx, dxb, s_c1, mem, w)
    dx, dxb, g_o = odd_bwd(dx, dxb, s_o, w, consts)
    dx, dxb = emit(1, (dx, dxb), g_o, g_c1, 1)
    dx, dxb, g_f0 = ffn_bwd(0, dx, dxb, s_f0, w)
    dx, dxb = emit(2, (dx, dxb), {}, g_f0, 0)
    dx, dxb, g_c0 = ca_bwd(0, dx, dxb, s_c0, mem, w)
    dx, dxb = emit(3, (dx, dxb), {}, g_c0, 0)
    dproj, g_e = even_bwd_mixers(dxb, s_e, w)
    dproj = emit(4, dproj, {**g_e, "o_norm": g_o["o_norm"], "o_d": g_o["o_d"]})
    dx, dxb, g_e["e_norm"] = even_bwd_input(dx, dproj, s_e, w)

    grads = dict(g_e)
    grads.update(g_o)
    for g0, g1 in ((g_c0, g_c1), (g_f0, g_f1)):
        for k in g0:
            grads[k] = jnp.concatenate([g0[k], g1[k]], axis=0) if k.endswith("norm") else (g0[k], g1[k])
    grads["final_norm"] = g_final
    return loss, dx, grads


def _group(axes):
    pos = {a: lax.axis_index(a) for a in ("x", "y", "c")}
    me = 0
    for a in axes:
        me = me * 2 + pos[a]
    peers = []
    for mask in range(1, 2 ** len(axes)):
        peer = dict(pos)
        for bit, a in enumerate(axes):
            if (mask >> (len(axes) - 1 - bit)) & 1:
                peer[a] = 1 - pos[a]
        idx = 0
        for a in axes:
            idx = idx * 2 + peer[a]
        peers.append((idx, (peer["x"], peer["y"], peer["c"])))
    return me, peers


def _sibling():
    x, y, c = lax.axis_index("x"), lax.axis_index("y"), lax.axis_index("c")
    return c, (x, y, 1 - c)


_HBM =pl.BlockSpec(memory_space=pltpu.HBM)
_SEM = pl.BlockSpec(memory_space=pltpu.SEMAPHORE)
_EFFECT = pltpu.SideEffectType.DATAFLOW_SIDE_EFFECTING


def _gather_peers(direct):
    chip, _ = _group(("x", "y"))
    core = lax.axis_index("c")
    if direct:
        _, peers = _group(_ALL)
        return chip, core, [(idx // 2, idx % 2, dev) for idx, dev in peers]
    _, peers = _group(("x", "y"))
    return chip, core, [(idx, core, dev) for idx, dev in peers]


def gather_ici_start(name, groups, direct):
    flat = [b for g in groups for b in g]
    sizes = [len(g) for g in groups]
    k_ops, n_g = len(flat), len(groups)
    lands = [lax.empty((4, 2) + tuple(b.shape), b.dtype) for b in flat]
    fan = [N_DEV - 1 if d else 3 for d in direct]

    def body(*refs):
        src, land = refs[:k_ops], refs[k_ops:2 * k_ops]
        sems = refs[2 * k_ops:2 * k_ops + 3 * n_g]
        token = refs[-1]
        i = 0
        for g in range(n_g):
            send, recv, loc = sems[3 * g:3 * g + 3]
            chip, core, peers = _gather_peers(direct[g])
            for j in range(sizes[g]):
                pltpu.make_async_copy(src[i], land[i].at[chip, core], loc.at[j]).start()
                for k, (_, _, dev) in enumerate(peers):
                    s = fan[g] * j + k
                    pltpu.make_async_remote_copy(src_ref=src[i], dst_ref=land[i].at[chip, core], send_sem=send.at[s],
                                                 recv_sem=recv.at[s], device_id=dev, device_id_type=MESH).start()
                i += 1
        token[...] = jnp.zeros(token.shape, token.dtype)

    sem_shapes = []
    for s, f in zip(sizes, fan):
        sem_shapes += [pltpu.SemaphoreType.DMA((f * s,)), pltpu.SemaphoreType.DMA((f * s,)), pltpu.SemaphoreType.DMA((s,))]
    thru = [pltpu.HBM(a.shape, a.dtype) for a in flat + lands]
    outs = pl.pallas_call(
        body, name=name, out_shape=tuple(sem_shapes) + tuple(thru) + (S((8, LANES), F32),),
        in_specs=[_HBM] * (2 * k_ops), out_specs=[_SEM] * (3 * n_g) + [_HBM] * (2 * k_ops) + [pl.BlockSpec(memory_space=pltpu.VMEM)],
        input_output_aliases={i: 3 * n_g + i for i in range(2 * k_ops)},
        compiler_params=pltpu.CompilerParams(has_side_effects=_EFFECT),
    )(*[pltpu.with_memory_space_constraint(a, pltpu.HBM) for a in flat + lands])
    sems = [tuple(outs[3 * g:3 * g + 3]) for g in range(n_g)]
    srcs_thru, lands_thru, off = [], [], 3 * n_g
    for s in sizes:
        srcs_thru.append(list(outs[off:off + s]))
        off += s
    for s in sizes:
        lands_thru.append(list(outs[off:off + s]))
        off += s
    return sems, srcs_thru, lands_thru, outs[-1]


def gather_ici_wait(name, srcs, lands, sems, after, direct=False):
    n = len(srcs)

    def body(*refs):
        src, land = refs[:n], refs[n:2 * n]
        send, recv, loc = refs[2 * n:2 * n + 3]
        chip, core, peers = _gather_peers(direct)
        for j in range(n):
            for k, (pchip, pcore, dev) in enumerate(peers):
                s = len(peers) * j + k
                cp = pltpu.make_async_remote_copy(src_ref=src[j], dst_ref=land[j].at[pchip, pcore], send_sem=send.at[s],
                                                  recv_sem=recv.at[s], device_id=dev, device_id_type=MESH)
                cp.wait_send()
                cp.wait_recv()
            pltpu.make_async_copy(src[j], land[j].at[chip, core], loc.at[j]).wait()

    outs = pl.pallas_call(
        body, name=name, out_shape=tuple(pltpu.HBM(a.shape, a.dtype) for a in list(srcs) + list(lands)),
        in_specs=[_HBM] * (2 * n) + [_SEM] * 3 + [ANY], out_specs=[_HBM] * (2 * n),
        input_output_aliases={i: i for i in range(2 * n)},
        compiler_params=pltpu.CompilerParams(has_side_effects=_EFFECT),
    )(*srcs, *lands, *sems, after)
    return list(outs[n:])


def gather_d2d(name, bufs):
    k_ops = len(bufs)

    def body(*refs):
        in_refs, out_refs = refs[:k_ops], refs[k_ops:2 * k_ops]
        send_sems, recv_sems = refs[2 * k_ops:]
        core, sib = _sibling()
        sent, landed = [], []
        for i in range(k_ops):
            cp = pltpu.make_async_remote_copy(src_ref=in_refs[i].at[:, core], dst_ref=out_refs[i].at[:, core],
                                              send_sem=send_sems.at[i], recv_sem=recv_sems.at[i], device_id=sib, device_id_type=MESH)
            cp.start()
            sent.append(cp)
            landed.append(pltpu.make_async_remote_copy(src_ref=in_refs[i].at[:, core], dst_ref=out_refs[i].at[:, 1 - core],
                                                       send_sem=send_sems.at[i], recv_sem=recv_sems.at[i],
                                                       device_id=sib, device_id_type=MESH))
        for cp in landed:
            cp.wait_recv()
        for cp in sent:
            cp.wait_send()

    return pl.pallas_call(
        body, in_specs=[ANY] * k_ops, out_specs=[ANY] * k_ops, out_shape=[S(b.shape, b.dtype) for b in bufs],
        input_output_aliases={i: i for i in range(k_ops)},
        scratch_shapes=[pltpu.SemaphoreType.DMA((k_ops,)), pltpu.SemaphoreType.DMA((k_ops,))],
        name=name)(*bufs)


_ALL = ("x", "y", "c")


def _unit_rows(units):
    offs, off = [], 0
    for u in units:
        offs.append(off)
        off += u.shape[1]
    return offs, off


def scatter_start(name, units, carry):
    n_u, n_c = len(units), len(carry)
    offs, rows = _unit_rows(units)
    land = lax.empty((N_DEV, rows) + tuple(units[0].shape[2:]), units[0].dtype)
    fan = N_DEV - 1

    def body(*refs):
        u_refs, land_ref = refs[:n_u], refs[n_u]
        send, recv, loc = refs[n_u + 1 + n_c:n_u + 4 + n_c]
        me, peers = _group(_ALL)
        for j in range(n_u):
            rs = pl.ds(offs[j], units[j].shape[1])
            pltpu.make_async_copy(u_refs[j].at[me], land_ref.at[me, rs], loc.at[j]).start()
            for k, (idx, dev) in enumerate(peers):
                pltpu.make_async_remote_copy(src_ref=u_refs[j].at[idx], dst_ref=land_ref.at[me, rs], send_sem=send.at[fan * j + k],
                                             recv_sem=recv.at[fan * j + k], device_id=dev, device_id_type=MESH).start()

    thru = list(units) + [land] + list(carry)
    outs = pl.pallas_call(
        body, name=name,
        out_shape=(pltpu.SemaphoreType.DMA((fan * n_u,)), pltpu.SemaphoreType.DMA((fan * n_u,)), pltpu.SemaphoreType.DMA((n_u,)))
        + tuple(pltpu.HBM(a.shape, a.dtype) for a in thru),
        in_specs=[_HBM] * len(thru), out_specs=[_SEM] * 3 + [_HBM] * len(thru),
        input_output_aliases={i: 3 + i for i in range(len(thru))},
        compiler_params=pltpu.CompilerParams(has_side_effects=_EFFECT),
    )(*[pltpu.with_memory_space_constraint(a, pltpu.HBM) for a in thru])
    return tuple(outs[:3]), list(outs[3:3 + n_u]), outs[3 + n_u], list(outs[4 + n_u:])


def scatter_wait(name, units, land, sems, after):
    n_u = len(units)
    offs, _ = _unit_rows(units)
    fan = N_DEV - 1

    def body(*refs):
        u_refs, land_ref = refs[:n_u], refs[n_u]
        send, recv, loc = refs[n_u + 1:n_u + 4]
        me, peers = _group(_ALL)
        for j in range(n_u):
            rs = pl.ds(offs[j], units[j].shape[1])
            for k, (idx, dev) in enumerate(peers):
                cp = pltpu.make_async_remote_copy(src_ref=u_refs[j].at[idx], dst_ref=land_ref.at[idx, rs], send_sem=send.at[fan * j + k],
                                                  recv_sem=recv.at[fan * j + k], device_id=dev, device_id_type=MESH)
                cp.wait_send()
                cp.wait_recv()
            pltpu.make_async_copy(u_refs[j].at[me], land_ref.at[me, rs], loc.at[j]).wait()

    thru = list(units) + [land]
    outs = pl.pallas_call(
        body, name=name, out_shape=tuple(pltpu.HBM(a.shape, a.dtype) for a in thru),
        in_specs=[_HBM] * len(thru) + [_SEM] * 3 + [ANY], out_specs=[_HBM] * len(thru),
        input_output_aliases={i: i for i in range(len(thru))},
        compiler_params=pltpu.CompilerParams(has_side_effects=_EFFECT),
    )(*thru, *sems, after)
    return outs[n_u]


def _row_tile(rows, cap=512):
    return next(t for t in range(cap - cap % 16, 0, -16) if rows % t == 0)


def sum_shares(name, recv, me):
    n, rows, c = recv.shape
    tr = _row_tile(rows)

    def body(me_ref, *refs):
        acc = refs[0][...].astype(F32)
        for r in refs[1:n]:
            acc = acc + r[...].astype(F32)
        refs[n][...] = acc

    def slot(mask):
        return pl.BlockSpec((None, tr, c), lambda i, me, mask=mask: (jnp.bitwise_xor(me[0], mask), i, 0))

    spec = pltpu.PrefetchScalarGridSpec(
        num_scalar_prefetch=1, grid=(rows // tr,), in_specs=[slot(k) for k in range(n)],
        out_specs=pl.BlockSpec((tr, c), lambda i, me: (i, 0)))
    return pl.pallas_call(body, grid_spec=spec, out_shape=S((rows, c), F32),
                          compiler_params=_cp("parallel"), name=name)(me, *([recv] * n))


def sum_slots(name, slots):
    n, r, c = slots.shape

    def body(s_ref, o_ref):
        acc = s_ref[0]
        for j in range(1, n):
            acc = acc + s_ref[j]
        o_ref[...] = acc

    return pl.pallas_call(body, out_shape=S((r, c), F32), compiler_params=pltpu.CompilerParams(vmem_limit_bytes=VMEM_LIMIT),
                          name=name)(slots)


def adamw_units(name, pieces, transposed, w, m, v):
    n_l, k, n = w.shape
    tk = _tile(k, 512) if transposed else k
    p_rows = n if transposed else k
    arrs = [p[0] if isinstance(p, tuple) else p for p in pieces]
    offs = [p[1] // p_rows if isinstance(p, tuple) else 0 for p in pieces]
    assert all(not isinstance(p, tuple) or p[1] % p_rows == 0 for p in pieces)
    c1 = 1.0 - ADAM_B1 ** ADAM_STEP
    c2 = 1.0 - ADAM_B2 ** ADAM_STEP

    def body(*refs):
        p_refs, (w_ref, m_ref, v_ref, g_ref, d_ref, m2_ref, v2_ref) = refs[:n_l], refs[n_l:]
        gv = p_refs[0][...]
        for j in range(1, n_l):
            gv = jnp.where(pl.program_id(0) == j, p_refs[j][...], gv)
        if transposed:
            gv = gv.T
        m2 = ADAM_B1 * m_ref[...] + (1.0 - ADAM_B1) * gv
        v2 = ADAM_B2 * v_ref[...] + (1.0 - ADAM_B2) * (gv * gv)
        g_ref[...] = gv
        m2_ref[...] = m2
        v2_ref[...] = v2
        d_ref[...] = -ADAM_LR * ((m2 / c1) / (jnp.sqrt(v2 / c2) + ADAM_EPS) + ADAM_WD * w_ref[...])

    def piece(o):
        if transposed:
            return pl.BlockSpec((n, tk), lambda l, i, o=o: (o, i))
        return pl.BlockSpec((k, n), lambda l, i, o=o: (o, 0))

    blk = pl.BlockSpec((None, tk, n), lambda l, i: (l, i, 0))
    return tuple(pl.pallas_call(body, grid=(n_l, k // tk), in_specs=[piece(o) for o in offs] + [blk] * 3, out_specs=[blk] * 4,
                                out_shape=[S(w.shape, F32)] * 4, compiler_params=_cp("parallel", "parallel"),
                                name=name)(*arrs, w, m, v))


def adamw_native(name, g, w, m, v, tr=512):
    shape = w.shape
    cols = shape[-1]
    rows = w.size // cols
    tr = _tile(rows, tr) if rows % 8 == 0 else rows
    c1 = 1.0 - ADAM_B1 ** ADAM_STEP
    c2 = 1.0 - ADAM_B2 ** ADAM_STEP

    def body(g_ref, w_ref, m_ref, v_ref, d_ref, m2_ref, v2_ref):
        gv = g_ref[...]
        m2 = ADAM_B1 * m_ref[...] + (1.0 - ADAM_B1) * gv
        v2 = ADAM_B2 * v_ref[...] + (1.0 - ADAM_B2) * (gv * gv)
        m2_ref[...] = m2
        v2_ref[...] = v2
        d_ref[...] = -ADAM_LR * ((m2 / c1) / (jnp.sqrt(v2 / c2) + ADAM_EPS) + ADAM_WD * w_ref[...])

    row = pl.BlockSpec((tr, cols), lambda i: (i, 0))
    outs = pl.pallas_call(body, grid=(rows // tr,), in_specs=[row] * 4, out_specs=[row] * 3,
                          out_shape=[S((rows, cols), F32)] * 3, compiler_params=_cp("parallel"),
                          name=name)(*[a.reshape(rows, cols) for a in (g, w, m, v)])
    return tuple(o.reshape(shape) for o in outs)


_REPLICATED = ("e_norm", "e_gmlp_w", "e_gmlp_b", "e_conv_b", "e_conv_ln_g", "e_conv_ln_b", "o_lam_re", "o_lam_im", "o_log_dt",
               "o_b_re", "o_b_im", "o_c_re", "o_c_im", "ca_norm", "ca_mem_norm", "ffn_norm", "final_norm")
_ORDER = ("e_norm", "e_w_in", "e_gmlp_w", "e_gmlp_b", "e_conv_w", "e_conv_b", "e_conv_ln_g", "e_conv_ln_b", "e_w_out",
          "o_norm", "o_w_in", "o_lam_re", "o_lam_im", "o_log_dt", "o_b_re", "o_b_im", "o_c_re", "o_c_im", "o_d", "o_w_out",
          "ca_norm", "ca_mem_norm", "ca_wq", "ca_wk", "ca_wv", "ca_wo", "ffn_norm", "ffn_w_gate", "ffn_w_up", "ffn_w_down",
          "final_norm")


def _rows128(a, multiple=8):
    flat = a.reshape(-1)
    rows = -(-flat.shape[0] // (LANES * multiple)) * multiple
    return jnp.pad(flat, (0, rows * LANES - flat.shape[0])).reshape(rows, LANES)


def _shard(full, axis):
    s = full.shape
    return jnp.moveaxis(full.reshape(s[:axis] + (N_DEV, s[axis] // N_DEV) + s[axis + 1:]), axis, 0)


_UNITS = (("e_w_in", 0, True), ("e_w_out", 0, False), ("o_w_in", 0, False), ("o_w_out", 0, True),
          *[(n, i, False) for n in ("ca_wq", "ca_wk", "ca_wv", "ca_wo") for i in (0, 1)],
          *[(n, i, tr) for n, tr in (("ffn_w_gate", True), ("ffn_w_up", True), ("ffn_w_down", False)) for i in (0, 1)])
_LAYERED = ("ca_wq", "ca_wk", "ca_wv", "ca_wo", "ffn_w_gate", "ffn_w_up", "ffn_w_down")
_SMALL_SHARDED = (("e_conv_w", 2), ("o_norm", 1), ("o_d", 1))
RS_ROW = 1024


def _unit_key(name, tr):
    return name + "_t" if tr else name


def _stage_of(name, layer):
    if name.startswith("e_"):
        return 0 if name == "e_w_in" else 1
    if name.startswith("o_"):
        return 3
    if name.startswith("ca_"):
        return 1 if layer == 0 else 4
    return 2 if layer == 0 else 5


GATHER_STAGES = 6
GATHER_DIRECT = (False, False, False, True, True, False)


def weight_fetcher(local):
    groups, meta = [[] for _ in range(GATHER_STAGES)], [[] for _ in range(GATHER_STAGES)]
    for name, layer, tr in _UNITS:
        blk = local[name][layer]
        st = _stage_of(name, layer)
        groups[st].append(_bf(blk.T if tr else blk))
        meta[st].append((name, layer, tr))
    small = jnp.concatenate([local[name].reshape(-1) for name, _ in _SMALL_SHARDED])
    groups[0].append(_rows128(small))
    direct = list(GATHER_DIRECT)
    sems, srcs, lands, token = gather_ici_start("ag_w_start", groups, direct)

    def fetch(stage, after):
        bufs = gather_ici_wait(f"ag_w_wait{stage}", srcs[stage], lands[stage], sems[stage], after, direct[stage])
        if not direct[stage]:
            bufs = gather_d2d(f"ag_w_d2d{stage}", bufs)
        got = {}
        for (name, layer, tr), blk, buf in zip(meta[stage], groups[stage], bufs):
            arr = buf.reshape((N_DEV * blk.shape[0],) + tuple(blk.shape[1:]))
            if name in _LAYERED:
                got[(_unit_key(name, tr), layer)] = arr
            else:
                got[_unit_key(name, tr)] = arr
        if stage == 0:
            flat = bufs[-1].reshape(N_DEV, -1)
            off = 0
            for name, axis in _SMALL_SHARDED:
                blk = local[name]
                seg = flat[:, off:off + blk.size].reshape((N_DEV,) + blk.shape)
                off += blk.size
                seg = jnp.moveaxis(seg, 0, axis)
                got[name] = seg.reshape(seg.shape[:axis] + (-1,) + seg.shape[axis + 2:])
            got["e_conv_w"] = got["e_conv_w"][0]
        return got

    return fetch, token


def _grad_stage_of(name, layer):
    if name.startswith("e_"):
        return 4
    if name.startswith("o_"):
        return 1
    if name.startswith("ca_"):
        return 3 if layer == 0 else 1
    return 2 if layer == 0 else 0


GRAD_STAGES = 5
SMALL_ROWS = 16


def gradient_reducer(local, mom, var):
    me = (4 * lax.axis_index("x") + 2 * lax.axis_index("y") + lax.axis_index("c")).astype(jnp.int32).reshape(1)
    pending = []

    def start(stage, grads, carry):
        def grad_of(unit):
            key = _unit_key(unit[0], unit[2])
            return grads[(key, unit[1])] if unit[0] in _LAYERED else grads[key]

        units = sorted([u for u in _UNITS if _grad_stage_of(u[0], u[1]) == stage], key=lambda u: -grad_of(u).size)
        parts, spans = [], []
        for unit in units:
            g = grad_of(unit)
            part = g.reshape(N_DEV, -1, RS_ROW)
            spans.append((part.shape[1], g.shape[0] // N_DEV, g.shape[1]))
            parts.append(part)
        if stage == GRAD_STAGES - 1:
            small = jnp.concatenate([_shard(grads[name], axis).reshape(N_DEV, -1) for name, axis in _SMALL_SHARDED], axis=1)
            small = jnp.pad(small, ((0, 0), (0, SMALL_ROWS * RS_ROW - small.shape[1])))
            parts.append(small.astype(BF16).reshape(N_DEV, SMALL_ROWS, RS_ROW))
        sems, sent, land, carry = scatter_start(f"rs_start{stage}", parts, carry)
        pending.append((stage, units, spans, sems, sent, land))
        return carry

    def finish(after):
        res, per_layer, small_flat = {}, {}, None
        for stage, units, spans, sems, sent, land in pending:
            land = scatter_wait(f"rs_wait{stage}", sent, land, sems, after)
            total = sum_shares(f"rs_sum{stage}", land, me)
            off = 0
            for (name, layer, tr), (rows, r, c) in zip(units, spans):
                piece = (total, off) if c == RS_ROW else total[off:off + rows].reshape(r, c)
                per_layer.setdefault(name, {})[layer] = (piece, tr)
                off += rows
            if stage == GRAD_STAGES - 1:
                small_flat = total[off:off + SMALL_ROWS].reshape(-1)
        for name, by_layer in per_layer.items():
            pieces = [by_layer[i][0] for i in sorted(by_layer)]
            res[name] = adamw_units("adamw_" + name, pieces, by_layer[0][1], local[name], mom[name], var[name])
        off = 0
        for name, _ in _SMALL_SHARDED:
            blk = local[name]
            g = small_flat[off:off + blk.size].reshape(blk.shape)
            off += blk.size
            res[name] = (g,) + adamw_native("adamw_" + name, g, blk, mom[name], var[name])
        return res

    return start, finish


def replicated_start(grads, loss):
    pack = jnp.concatenate([_rows128(grads[name]) for name in _REPLICATED] + [_rows128(loss)], axis=0)
    sems, srcs, lands, token = gather_ici_start("ag_g_start", [[pack]], [False])
    return sems[0], srcs[0], lands[0], token


def replicated_finish(handle, after, w, mom, var):
    sems, srcs, lands, _ = handle
    (buf,) = gather_d2d("ag_g_d2d", gather_ici_wait("ag_g_wait", srcs, lands, sems, after))
    rows = srcs[0].shape[0]
    total = sum_slots("ag_g_sum", buf.reshape(N_DEV, rows, LANES))
    res, off = {}, 0
    for name in _REPLICATED:
        n = w[name].size
        nr = -(-n // (LANES * 8)) * 8
        g = total[off:off + nr].reshape(-1)[:n].reshape(w[name].shape)
        off += nr
        res[name] = (g,) + adamw_native("adamw_" + name, g, w[name], mom[name], var[name])
    return res, total[off, 0]


def kernel(x, mem, e_norm, e_w_in, e_gmlp_w, e_gmlp_b, e_conv_w, e_conv_b, e_conv_ln_g, e_conv_ln_b, e_w_out, o_norm, o_w_in, o_lam_re, o_lam_im, o_log_dt, o_b_re, o_b_im, o_c_re, o_c_im, o_d, o_w_out, ca_norm, ca_mem_norm, ca_wq, ca_wk, ca_wv, ca_wo, ffn_norm, ffn_w_gate, ffn_w_up, ffn_w_down, final_norm, loss_target, m_e_norm, m_e_w_in, m_e_gmlp_w, m_e_gmlp_b, m_e_conv_w, m_e_conv_b, m_e_conv_ln_g, m_e_conv_ln_b, m_e_w_out, m_o_norm, m_o_w_in, m_o_lam_re, m_o_lam_im, m_o_log_dt, m_o_b_re, m_o_b_im, m_o_c_re, m_o_c_im, m_o_d, m_o_w_out, m_ca_norm, m_ca_mem_norm, m_ca_wq, m_ca_wk, m_ca_wv, m_ca_wo, m_ffn_norm, m_ffn_w_gate, m_ffn_w_up, m_ffn_w_down, m_final_norm, v_e_norm, v_e_w_in, v_e_gmlp_w, v_e_gmlp_b, v_e_conv_w, v_e_conv_b, v_e_conv_ln_g, v_e_conv_ln_b, v_e_w_out, v_o_norm, v_o_w_in, v_o_lam_re, v_o_lam_im, v_o_log_dt, v_o_b_re, v_o_b_im, v_o_c_re, v_o_c_im, v_o_d, v_o_w_out, v_ca_norm, v_ca_mem_norm, v_ca_wq, v_ca_wk, v_ca_wv, v_ca_wo, v_ffn_norm, v_ffn_w_gate, v_ffn_w_up, v_ffn_w_down, v_final_norm):
    given = dict(locals())
    local = {k: given[k] for k in _ORDER}
    mom = {k: given["m_" + k] for k in _ORDER}
    var = {k: given["v_" + k] for k in _ORDER}

    w = {}
    w.update({
        "e_norm": e_norm, "e_gmlp_w": e_gmlp_w[0], "e_gmlp_b": e_gmlp_b.reshape(A_GROUPS, GMLP_BLOCK, 1),
        "e_conv_b": e_conv_b, "e_conv_ln_g": e_conv_ln_g, "e_conv_ln_b": e_conv_ln_b,
        "o_lam_re": o_lam_re[0], "o_lam_im": o_lam_im[0], "o_log_dt": o_log_dt[0], "o_b_re": o_b_re[0], "o_b_im": o_b_im[0],
        "o_c_re": o_c_re[0], "o_c_im": o_c_im[0], "ca_norm": ca_norm, "ca_mem_norm": ca_mem_norm, "ffn_norm": ffn_norm,
        "final_norm": final_norm.reshape(1, D_MODEL),
    })
    start_reduce, finish_reduce = gradient_reducer(local, mom, var)
    fetch, token = weight_fetcher(local)
    loss_part, grad_x, grads = local_step(x[0], mem[0], loss_target[0], w, fetch, start_reduce, token[0:1, 0:1])
    grads["final_norm"] = grads["final_norm"].reshape(D_MODEL)

    handle = replicated_start(grads, loss_part)
    res = finish_reduce(handle[3])
    rep, loss = replicated_finish(handle, res["ffn_w_down"][1], local, mom, var)
    res.update(rep)
    return (loss, grad_x[None], *[res[k][0] for k in _ORDER], *[res[k][1] for k in _ORDER],
            *[res[k][2] for k in _ORDER], *[res[k][3] for k in _ORDER])
```

```python
import jax
import jax.numpy as jnp
from jax import lax
from jax.experimental import pallas as pl
from jax.experimental.pallas import tpu as pltpu

F32 = jnp.float32
BF16 = jnp.bfloat16
S = jax.ShapeDtypeStruct

D_MODEL = 1024
A_WIDTH = 512
A_GROUPS = 4
GMLP_BLOCK = 128
CHUNK = 64
B_WIDTH = 512
IN_WIDTH = 2 * A_WIDTH + 2 * B_WIDTH
CONV_WIDTH = 31
CONV_PAD = 32
C_WIDTH = 512
C_GROUP_CH = 16
C_GROUPS = 32
C_STATE = 64
N_STATE = C_GROUPS * C_STATE
CA_HEADS = 4
CA_HEAD_DIM = 256
EPS = 1e-6
ADAM_LR = 0.001
ADAM_B1 = 0.9
ADAM_B2 = 0.999
ADAM_EPS = 1e-08
ADAM_WD = 0.01
ADAM_STEP = 10
N_DEV = 8
LANES = 128
VMEM_LIMIT = 56 << 20
VMEM_BUDGET = 40 << 20
MM_TN_RESIDENT = 8 << 20
MESH = pl.DeviceIdType.MESH
ANY = pl.BlockSpec(memory_space=pl.ANY)


def _cp(*sem):
    return pltpu.CompilerParams(dimension_semantics=sem, vmem_limit_bytes=VMEM_LIMIT)


def _tile(n, pref):
    t = pref
    while n % t:
        t //= 2
    return t


def _bf(v):
    return v if v.dtype == BF16 else v.astype(BF16)


def _sigmoid(x):
    return 1.0 / (1.0 + jnp.exp(-x))


_GC = 0.7978845608028654


def _gelu(x):
    return 0.5 * x * (1.0 + jnp.tanh(_GC * (x + 0.044715 * x * x * x)))


def _gelu_grad(x):
    x2 = x * x
    t = jnp.tanh(_GC * (x + 0.044715 * x * x2))
    return 0.5 * (1.0 + t) + 0.5 * x * (1.0 - t * t) * _GC * (1.0 + 3.0 * 0.044715 * x2)


def _tspec(entry, tm):
    if isinstance(entry, tuple):
        arr, cb, width = entry
        return arr, pl.BlockSpec((tm, width), lambda i, cb=cb: (i, cb))
    return entry, pl.BlockSpec((tm, entry.shape[1]), lambda i: (i, 0))


def rows_call(name, fn, tiled, full, outs, accs, tm=256):
    pairs = [_tspec(e, tm) for e in tiled]
    arrs = [p[0] for p in pairs]
    rows = arrs[0].shape[0]
    tm = _tile(rows, tm)
    pairs = [_tspec(e, tm) for e in tiled]
    n_in = len(tiled) + len(full)
    n_out = len(outs)

    def body(*refs):
        vals = [r[...] for r in refs[:n_in]]
        o_refs = refs[n_in:n_in + n_out]
        a_refs = refs[n_in + n_out:]
        ov, av = fn(*vals)
        for r, v in zip(o_refs, ov):
            r[...] = v.astype(r.dtype)
        if a_refs:
            @pl.when(pl.program_id(0) == 0)
            def _():
                for r in a_refs:
                    r[...] = jnp.zeros(r.shape, r.dtype)
            for r, v in zip(a_refs, av):
                r[...] += v

    in_specs = [p[1] for p in pairs] + [pl.BlockSpec(a.shape, lambda i, nd=a.ndim: (0,) * nd) for a in full]
    out_specs = [pl.BlockSpec((tm, c), lambda i: (i, 0)) for c, _ in outs]
    out_specs += [pl.BlockSpec(s, lambda i, nd=len(s): (0,) * nd) for s in accs]
    out_shape = [S((rows, c), dt) for c, dt in outs] + [S(s, F32) for s in accs]
    return pl.pallas_call(body, grid=(rows // tm,), in_specs=in_specs, out_specs=out_specs, out_shape=out_shape,
                          compiler_params=_cp("arbitrary"), name=name)(*arrs, *full)


def mm_nn(name, m, n, pairs, n_acc, epi, outs, tiled=(), cols=(), rowv=(), sums=(), norm_gain=None):
    a_ops, a_slot, b_arrs, b_specs, idx, trans = [], [], [], [], [], []
    fixed = 0
    for pair in pairs:
        a, b, k = pair[:3]
        bt = len(pair) > 3
        arr, cb, kdim = a if isinstance(a, tuple) else (a, 0, a.shape[1])
        key = (id(arr), cb, kdim)
        if key not in [o[0] for o in a_ops]:
            a_ops.append((key, arr, cb, kdim))
        a_slot.append([o[0] for o in a_ops].index(key))
        b_arr, off = b if isinstance(b, tuple) else (b, 0)
        b_arrs.append(b_arr)
        if bt:
            assert off % n == 0 and b_arr.shape[1] == kdim
            b_specs.append(pl.BlockSpec((n, kdim), lambda i, o=off // n: (o, 0), pipeline_mode=pl.Buffered(1)))
        else:
            assert b_arr.shape[1] == n
            b_specs.append(pl.BlockSpec((kdim, n), lambda i, o=off: (o, 0), pipeline_mode=pl.Buffered(1)))
        fixed += kdim * n * b_arr.dtype.itemsize
        idx.append(k)
        trans.append(bt)
    per_row = sum(2 * kdim * arr.dtype.itemsize for _, arr, _, kdim in a_ops)
    per_row += sum(2 * n * t.dtype.itemsize for t in tiled) + sum(2 * n * jnp.dtype(dt).itemsize for dt in outs)
    cn = n if sums or cols else (512 if n % 512 == 0 else 256)
    per_row += (n_acc + 3) * cn * 4
    tm = next((t for t in (1024, 512, 256, 128) if m % t == 0 and fixed + t * per_row <= VMEM_BUDGET), _tile(m, 128))
    n_a, n_p, n_t = len(a_ops), len(pairs), len(tiled)
    n_in = n_a + n_p + n_t + len(cols) + len(rowv)
    normed = norm_gain is not None
    o0 = n_in + normed

    def body(*refs):
        a_vals = [None if normed and i == 0 else _bf(r[...]) for i, r in enumerate(refs[:n_a])]
        if normed:
            xv = refs[0][...]
            rv = lax.rsqrt(jnp.mean(xv * xv, axis=-1, keepdims=True) + EPS)
            a_vals[0] = (xv * rv * refs[n_in][...]).astype(BF16)
            refs[o0 + len(outs)][...] = a_vals[0]
            refs[o0 + len(outs) + 1][...] = rv
        for j in range(n // cn):
            cs = slice(j * cn, (j + 1) * cn)
            accs = [None] * n_acc
            for p in range(n_p):
                av, b_ref = a_vals[a_slot[p]], refs[n_a + p]
                if trans[p]:
                    d = lax.dot_general(av, _bf(b_ref[cs, :]), (((1,), (1,)), ((), ())), preferred_element_type=F32)
                else:
                    d = jnp.dot(av, _bf(b_ref[:, cs]), preferred_element_type=F32)
                accs[idx[p]] = d if accs[idx[p]] is None else accs[idx[p]] + d
            extra = [r[:, cs] for r in refs[n_a + n_p:n_a + n_p + n_t]] + [r[...] for r in refs[n_a + n_p + n_t:n_in - len(rowv)]]
            extra += [r[:, cs] for r in refs[n_in - len(rowv):n_in]]
            ov = epi(accs, *extra)
            for r, v in zip(refs[o0:o0 + len(outs)], ov):
                r[:, cs] = v.astype(r.dtype)
        sv = ov[len(outs):]
        if sums:
            s_refs = refs[o0 + len(outs) + 2 * normed:]

            @pl.when(pl.program_id(0) == 0)
            def _():
                for r in s_refs:
                    r[...] = jnp.zeros(r.shape, r.dtype)
            for r, v in zip(s_refs, sv):
                r[...] += v

    in_specs = [pl.BlockSpec((tm, kdim), lambda i, cb=cb: (i, cb)) for _, _, cb, kdim in a_ops] + b_specs
    in_specs += [pl.BlockSpec((tm, n), lambda i: (i, 0)) for _ in tiled]
    in_specs += [pl.BlockSpec((tm, 1), lambda i: (i, 0)) for _ in cols]
    in_specs += [pl.BlockSpec((1, n), lambda i: (0, 0)) for _ in rowv]
    out_specs = [pl.BlockSpec((tm, n), lambda i: (i, 0)) for _ in outs]
    out_shape = [S((m, n), dt) for dt in outs]
    gain = []
    if normed:
        k0 = a_ops[0][3]
        gain = [norm_gain]
        in_specs.append(pl.BlockSpec((1, k0), lambda i: (0, 0)))
        out_specs += [pl.BlockSpec((tm, k0), lambda i: (i, 0)), pl.BlockSpec((tm, 1), lambda i: (i, 0))]
        out_shape += [S((m, k0), BF16), S((m, 1), F32)]
    out_specs += [pl.BlockSpec(s, lambda i, nd=len(s): (0,) * nd) for s in sums]
    out_shape += [S(s, F32) for s in sums]
    return pl.pallas_call(body, grid=(m // tm,), in_specs=in_specs, out_specs=out_specs, out_shape=out_shape,
                          compiler_params=_cp("arbitrary" if sums else "parallel"),
                          name=name)(*[o[1] for o in a_ops], *b_arrs, *tiled, *cols, *rowv, *gain)


def mm_tn(name, a, b, out_dtype=BF16):
    if isinstance(a, tuple):
        a_arr, a_cb, m = a
    else:
        a_arr, a_cb, m = a, None, a.shape[1]
    if isinstance(b, tuple):
        b_arr, b_cb, n = b
    else:
        b_arr, b_cb, n = b, None, b.shape[1]
    t = a_arr.shape[0]
    whole_b = t * n * b_arr.dtype.itemsize <= MM_TN_RESIDENT and b_cb is None
    tn = n if whole_b else _tile(n, 512)
    tm = _tile(m, 512 if t * 512 * a_arr.dtype.itemsize * 2 + t * tn * b_arr.dtype.itemsize * 2 <= VMEM_BUDGET else 256)
    a_off = 0 if a_cb is None else a_cb * (m // tm)
    b_off = 0 if b_cb is None else b_cb * (n // tn)

    def body(a_ref, b_ref, o_ref):
        o_ref[...] = lax.dot_general(_bf(a_ref[...]), _bf(b_ref[...]), (((0,), (0,)), ((), ())),
                                     preferred_element_type=F32).astype(o_ref.dtype)

    if whole_b:
        b_spec = pl.BlockSpec((t, n), lambda i, j: (0, 0), pipeline_mode=pl.Buffered(1))
    else:
        b_spec = pl.BlockSpec((t, tn), lambda i, j: (0, j + b_off))
    return pl.pallas_call(
        body, grid=(m // tm, n // tn),
        in_specs=[pl.BlockSpec((t, tm), lambda i, j: (0, i + a_off)), b_spec],
        out_specs=pl.BlockSpec((tm, tn), lambda i, j: (i, j)), out_shape=S((m, n), out_dtype),
        compiler_params=_cp("parallel", "parallel"), name=name)(a_arr, b_arr)


def rms_bwd_gain_only(name, dxn, x, r):
    def fn(dv, xv, rv):
        return [], [jnp.sum(dv * xv * rv, axis=0, keepdims=True)]
    return rows_call(name, fn, [dxn, x, r], [], [], [(1, x.shape[1])])[0]


def _final_loss_epi(accs, res, tv, g):
    xv = res + accs[0]
    d = xv.shape[-1]
    r = lax.rsqrt(jnp.mean(xv * xv, axis=-1, keepdims=True) + EPS)
    xh = xv * r
    err = xh * g - tv
    dy = err * (1.0 / d)
    w = dy * g
    dx = r * (w - xh * jnp.mean(w * xh, axis=-1, keepdims=True))
    part = jnp.sum(jnp.sum(err * err, axis=-1, keepdims=True), axis=0, keepdims=True) * (0.5 / d)
    return [dx, dx, jnp.sum(dy * xh, axis=0, keepdims=True), part]


def _gmlp_mask():
    row = lax.broadcasted_iota(jnp.int32, (GMLP_BLOCK, GMLP_BLOCK), 0) // CHUNK
    col = lax.broadcasted_iota(jnp.int32, (GMLP_BLOCK, GMLP_BLOCK), 1) // CHUNK
    return col <= row


def _ln_plain(v):
    mu = jnp.mean(v, axis=-1, keepdims=True)
    vc = v - mu
    rstd = lax.rsqrt(jnp.mean(vc * vc, axis=-1, keepdims=True) + EPS)
    return vc * rstd, rstd


def even_out_fwd(name, proj, hc, x, w, b, ln_g, ln_b, w_out, tm=512):
    t, d = x.shape
    tm = _tile(t, tm)

    def body(au_ref, av_ref, hc_ref, x_ref, w_ref, b_ref, lg_ref, lb_ref, wo_ref, x1_ref, oa_ref, ob_ref):
        mask = _gmlp_mask()
        u = _gelu(au_ref[...])
        vn, _ = _ln_plain(_gelu(av_ref[...]))
        vnb = _bf(vn)
        for g in range(A_GROUPS):
            wg = _bf(jnp.where(mask, w_ref[g], 0.0))
            cs = slice(g * GMLP_BLOCK, (g + 1) * GMLP_BLOCK)
            for n in range(tm // GMLP_BLOCK):
                rs = slice(n * GMLP_BLOCK, (n + 1) * GMLP_BLOCK)
                sg = jnp.dot(wg, vnb[rs, cs], preferred_element_type=F32) + b_ref[g]
                oa_ref[rs, cs] = (u[rs, cs] * sg).astype(oa_ref.dtype)
        y, _ = _ln_plain(hc_ref[...])
        z = y * lg_ref[...] + lb_ref[...]
        ob_ref[...] = (z * _sigmoid(z)).astype(ob_ref.dtype)
        x1_ref[...] = (x_ref[...] + jnp.dot(oa_ref[...], wo_ref[0:A_WIDTH, :], preferred_element_type=F32)
                       + jnp.dot(ob_ref[...], wo_ref[A_WIDTH:, :], preferred_element_type=F32))

    half = pl.BlockSpec((tm, A_WIDTH), lambda i: (i, 0))
    return pl.pallas_call(
        body, grid=(t // tm,),
        in_specs=[half, pl.BlockSpec((tm, A_WIDTH), lambda i: (i, 1)), half, pl.BlockSpec((tm, d), lambda i: (i, 0)),
                  _whole(w), _whole(b), _whole(ln_g), _whole(ln_b), _whole(w_out)],
        out_specs=[pl.BlockSpec((tm, d), lambda i: (i, 0)), half, half],
        out_shape=[S((t, d), F32), S((t, A_WIDTH), BF16), S((t, B_WIDTH), BF16)],
        compiler_params=_cp("parallel"), name=name)(proj, proj, hc, x, w, b, ln_g, ln_b, w_out)


def gmlp_bwd(name, proj, dxb, w_out, w, b, tm=512):
    t = proj.shape[0]
    tm = _tile(t, tm)

    def body(au_ref, av_ref, dx_ref, wo_ref, w_ref, b_ref, dp_ref, dw_ref, db_ref):
        @pl.when(pl.program_id(0) == 0)
        def _():
            dw_ref[...] = jnp.zeros(dw_ref.shape, F32)
            db_ref[...] = jnp.zeros(db_ref.shape, F32)

        mask = _gmlp_mask()
        au = au_ref[...]
        av = av_ref[...]
        u = _gelu(au)
        vn, rstd = _ln_plain(_gelu(av))
        vnb = _bf(vn)
        dout = lax.dot_general(dx_ref[...], wo_ref[0:A_WIDTH, :], _NT, preferred_element_type=F32)
        dvn_cols = []
        for g in range(A_GROUPS):
            wm = jnp.where(mask, w_ref[g], 0.0)
            wg = _bf(wm)
            wgt = _bf(wm.T)
            cs = slice(g * GMLP_BLOCK, (g + 1) * GMLP_BLOCK)
            dwg = jnp.zeros((GMLP_BLOCK, GMLP_BLOCK), F32)
            dbg = jnp.zeros((GMLP_BLOCK, 1), F32)
            dvn_rows = []
            for n in range(tm // GMLP_BLOCK):
                rs = slice(n * GMLP_BLOCK, (n + 1) * GMLP_BLOCK)
                sg = jnp.dot(wg, vnb[rs, cs], preferred_element_type=F32) + b_ref[g]
                dp_ref[rs, cs] = (dout[rs, cs] * sg * _gelu_grad(au[rs, cs])).astype(dp_ref.dtype)
                dsg = dout[rs, cs] * u[rs, cs]
                dsgb = _bf(dsg)
                dbg = dbg + jnp.sum(dsg, axis=1, keepdims=True)
                dwg = dwg + lax.dot_general(dsgb, vnb[rs, cs], (((1,), (1,)), ((), ())), preferred_element_type=F32)
                dvn_rows.append(jnp.dot(wgt, dsgb, preferred_element_type=F32))
            dw_ref[g] += jnp.where(mask, dwg, 0.0)
            db_ref[g] += dbg
            dvn_cols.append(jnp.concatenate(dvn_rows, axis=0))
        dvn = jnp.concatenate(dvn_cols, axis=1)
        dv = rstd * (dvn - jnp.mean(dvn, axis=-1, keepdims=True) - vn * jnp.mean(dvn * vn, axis=-1, keepdims=True))
        dp_ref[:, A_WIDTH:] = (dv * _gelu_grad(av)).astype(dp_ref.dtype)

    return pl.pallas_call(
        body, grid=(t // tm,),
        in_specs=[pl.BlockSpec((tm, A_WIDTH), lambda i: (i, 0)), pl.BlockSpec((tm, A_WIDTH), lambda i: (i, 1)),
                  pl.BlockSpec((tm, dxb.shape[1]), lambda i: (i, 0)), pl.BlockSpec(w_out.shape, lambda i: (0, 0)),
                  pl.BlockSpec(w.shape, lambda i: (0, 0, 0)), pl.BlockSpec(b.shape, lambda i: (0, 0, 0))],
        out_specs=[pl.BlockSpec((tm, 2 * A_WIDTH), lambda i: (i, 0)),
                   pl.BlockSpec(w.shape, lambda i: (0, 0, 0)), pl.BlockSpec(b.shape, lambda i: (0, 0, 0))],
        out_shape=[S((t, 2 * A_WIDTH), BF16), S(w.shape, F32), S(b.shape, F32)],
        compiler_params=_cp("arbitrary"), name=name)(proj, proj, dxb, w_out, w, b)


CONV_ROWS = 256
CONV_ROWS_BWD = 64


def conv_fwd(name, proj, w, cb):
    t = proj.shape[0]
    tc = LANES
    rows = _tile(t, CONV_ROWS)
    a_cb, g_cb = 2 * A_WIDTH // tc, (2 * A_WIDTH + B_WIDTH) // tc

    def body(a_ref, g_ref, w_ref, cb_ref, o_ref, hpad):
        hpad[0:CONV_PAD, :] = jnp.zeros((CONV_PAD, tc), F32)

        def fill(i, _):
            r0 = pl.multiple_of(i * rows, rows)
            hpad[pl.ds(CONV_PAD + r0, rows), :] = a_ref[pl.ds(r0, rows), :] * _sigmoid(g_ref[pl.ds(r0, rows), :])
            return 0
        lax.fori_loop(0, t // rows, fill, 0)

        def conv(i, _):
            r0 = pl.multiple_of(i * rows, rows)
            win = hpad[pl.ds(r0, rows + CONV_PAD), :]
            acc = jnp.zeros((rows, tc), F32) + cb_ref[...]
            for b in range(SUB):
                wb = win if b == 0 else pltpu.roll(win, b, 0)
                for a in range(CONV_PAD // SUB):
                    k = CONV_WIDTH - 1 - (SUB * a + b)
                    if k >= 0:
                        lo = CONV_PAD - SUB * a
                        acc = acc + wb[lo:lo + rows, :] * w_ref[k:k + 1, :]
            o_ref[pl.ds(r0, rows), :] = acc
            return 0
        lax.fori_loop(0, t // rows, conv, 0)

    return pl.pallas_call(
        body, grid=(B_WIDTH // tc,),
        in_specs=[pl.BlockSpec((t, tc), lambda j: (0, a_cb + j)), pl.BlockSpec((t, tc), lambda j: (0, g_cb + j)),
                  pl.BlockSpec((CONV_WIDTH, tc), lambda j: (0, j)), pl.BlockSpec((1, tc), lambda j: (0, j))],
        out_specs=pl.BlockSpec((t, tc), lambda j: (0, j)), out_shape=S((t, B_WIDTH), F32),
        scratch_shapes=[pltpu.VMEM((t + CONV_PAD, tc), F32)],
        compiler_params=_cp("parallel"), name=name)(proj, proj, w, cb)


def conv_bwd(name, proj, dhc, w):
    t = proj.shape[0]
    tc = LANES
    rows = _tile(t, CONV_ROWS_BWD)
    a_cb, g_cb = 2 * A_WIDTH // tc, (2 * A_WIDTH + B_WIDTH) // tc
    win_rows = rows + CONV_PAD

    def body(a_ref, g_ref, d_ref, w_ref, da_ref, dg_ref, dw_ref, dcb_ref, hpad, dpad, dwacc):
        hpad[0:CONV_PAD, :] = jnp.zeros((CONV_PAD, tc), F32)
        dpad[t:t + CONV_PAD, :] = jnp.zeros((CONV_PAD, tc), F32)
        dwacc[...] = jnp.zeros(dwacc.shape, F32)

        def fill(i, _):
            r0 = pl.multiple_of(i * rows, rows)
            hpad[pl.ds(CONV_PAD + r0, rows), :] = a_ref[pl.ds(r0, rows), :] * _sigmoid(g_ref[pl.ds(r0, rows), :])
            dpad[pl.ds(r0, rows), :] = d_ref[pl.ds(r0, rows), :]
            return 0
        lax.fori_loop(0, t // rows, fill, 0)

        def step(i, dcb):
            r0 = pl.multiple_of(i * rows, rows)
            hwin = hpad[pl.ds(r0, win_rows), :]
            dwin = dpad[pl.ds(r0, win_rows), :]
            dchunk = dwin[:rows, :]
            dh = jnp.zeros((rows, tc), F32)
            for b in range(SUB):
                hb = hwin if b == 0 else pltpu.roll(hwin, b, 0)
                db = dwin if b == 0 else pltpu.roll(dwin, win_rows - b, 0)
                for a in range(CONV_PAD // SUB):
                    k = CONV_WIDTH - 1 - (SUB * a + b)
                    if k >= 0:
                        dh = dh + db[SUB * a:SUB * a + rows, :] * w_ref[k:k + 1, :]
                        lo = CONV_PAD - SUB * a
                        prod = dchunk * hb[lo:lo + rows, :]
                        dwacc[k] += jnp.sum(prod.reshape(rows // 8, 8, tc), axis=0)
            a = a_ref[pl.ds(r0, rows), :]
            sg = _sigmoid(g_ref[pl.ds(r0, rows), :])
            da_ref[pl.ds(r0, rows), :] = (dh * sg).astype(da_ref.dtype)
            dg_ref[pl.ds(r0, rows), :] = (dh * a * sg * (1.0 - sg)).astype(dg_ref.dtype)
            return dcb + jnp.sum(dchunk, axis=0, keepdims=True)
        dcb = lax.fori_loop(0, t // rows, step, jnp.zeros((1, tc), F32))
        dcb_ref[...] = dcb
        for k in range(CONV_WIDTH):
            dw_ref[k:k + 1, :] = jnp.sum(dwacc[k], axis=0, keepdims=True)

    return pl.pallas_call(
        body, grid=(B_WIDTH // tc,),
        in_specs=[pl.BlockSpec((t, tc), lambda j: (0, a_cb + j)), pl.BlockSpec((t, tc), lambda j: (0, g_cb + j)),
                  pl.BlockSpec((t, tc), lambda j: (0, j)), pl.BlockSpec((CONV_WIDTH, tc), lambda j: (0, j))],
        out_specs=[pl.BlockSpec((t, tc), lambda j: (0, j)), pl.BlockSpec((t, tc), lambda j: (0, j)),
                   pl.BlockSpec((CONV_WIDTH, tc), lambda j: (0, j)), pl.BlockSpec((1, tc), lambda j: (0, j))],
        out_shape=[S((t, B_WIDTH), BF16), S((t, B_WIDTH), BF16), S((CONV_WIDTH, B_WIDTH), F32), S((1, B_WIDTH), F32)],
        scratch_shapes=[pltpu.VMEM((t + CONV_PAD, tc), F32), pltpu.VMEM((t + CONV_PAD, tc), F32),
                        pltpu.VMEM((CONV_WIDTH, 8, tc), F32)],
        compiler_params=_cp("parallel"), name=name)(proj, proj, dhc, w)


def ln_silu_bwd(name, hc, dxb, w_out, g, b):
    c = hc.shape[1]

    def fn(h, dxv, wv, gv, bv):
        dout = lax.dot_general(dxv, wv[A_WIDTH:, :], _NT, preferred_element_type=F32)
        y, rstd = _ln_plain(h)
        z = y * gv + bv
        s = _sigmoid(z)
        dz = dout * s * (1.0 + z * (1.0 - s))
        dyv = dz * gv
        dh = rstd * (dyv - jnp.mean(dyv, axis=-1, keepdims=True) - y * jnp.mean(dyv * y, axis=-1, keepdims=True))
        return [dh], [jnp.sum(dz * y, axis=0, keepdims=True), jnp.sum(dz, axis=0, keepdims=True)]

    return rows_call(name, fn, [hc, dxb], [w_out, g, b], [(c, F32)], [(1, c), (1, c)])


_NT = (((1,), (1,)), ((), ()))
_TN = (((0,), (0,)), ((), ()))


def attn_fwd(name, x, gain, wq, k, v, wo, tm=512):
    t, d = x.shape
    m = k.shape[0]
    tm = _tile(t, tm)
    scale = CA_HEAD_DIM ** -0.5

    def body(x_ref, g_ref, wq_ref, k_ref, v_ref, wo_ref, x1_ref, xn_ref, r_ref, q_ref, o_ref):
        xv = x_ref[...]
        rv = lax.rsqrt(jnp.mean(xv * xv, axis=-1, keepdims=True) + EPS)
        xn = (xv * rv * g_ref[...]).astype(BF16)
        xn_ref[...] = xn
        r_ref[...] = rv
        q_ref[...] = jnp.dot(xn, wq_ref[...], preferred_element_type=F32).astype(BF16)
        for h in range(CA_HEADS):
            cs = slice(h * CA_HEAD_DIM, (h + 1) * CA_HEAD_DIM)
            s = lax.dot_general(q_ref[:, cs], k_ref[:, cs], _NT, preferred_element_type=F32) * scale
            e = jnp.exp(s - jnp.max(s, axis=-1, keepdims=True))
            p = e / jnp.sum(e, axis=-1, keepdims=True)
            o_ref[:, cs] = jnp.dot(_bf(p), v_ref[:, cs], preferred_element_type=F32).astype(o_ref.dtype)
        x1_ref[...] = xv + jnp.dot(o_ref[...], wo_ref[...], preferred_element_type=F32)

    def whole(a):
        return pl.BlockSpec(a.shape, lambda i: (0, 0), pipeline_mode=pl.Buffered(1))

    rows = pl.BlockSpec((tm, d), lambda i: (i, 0))
    col = pl.BlockSpec((tm, 1), lambda i: (i, 0))
    return pl.pallas_call(
        body, grid=(t // tm,),
        in_specs=[rows, whole(gain), whole(wq), whole(k), whole(v), whole(wo)],
        out_specs=[rows, rows, col, rows, rows],
        out_shape=[S((t, d), F32), S((t, d), BF16), S((t, 1), F32), S((t, d), BF16), S((t, d), BF16)],
        compiler_params=_cp("parallel"), name=name)(x, gain, wq, k, v, wo)


def attn_bwd(name, dx, dxb, x, r, gain, q, k, v, wq, wo, tm=512):
    t, d = q.shape
    m = k.shape[0]
    tm = _tile(t, tm)
    scale = CA_HEAD_DIM ** -0.5

    def body(dx_ref, dxb_ref, x_ref, r_ref, g_ref, q_ref, k_ref, v_ref, wq_ref, wo_ref,
             dxo_ref, dxbo_ref, dq_ref, dk_ref, dv_ref, dg_ref, do_s):
        @pl.when(pl.program_id(0) == 0)
        def _():
            dk_ref[...] = jnp.zeros(dk_ref.shape, F32)
            dv_ref[...] = jnp.zeros(dv_ref.shape, F32)
            dg_ref[...] = jnp.zeros(dg_ref.shape, F32)

        do_s[...] = lax.dot_general(dxb_ref[...], wo_ref[...], _NT, preferred_element_type=F32).astype(BF16)
        for h in range(CA_HEADS):
            cs = slice(h * CA_HEAD_DIM, (h + 1) * CA_HEAD_DIM)
            qh, kh, vh, doh = q_ref[:, cs], k_ref[:, cs], v_ref[:, cs], do_s[:, cs]
            s = lax.dot_general(qh, kh, _NT, preferred_element_type=F32) * scale
            e = jnp.exp(s - jnp.max(s, axis=-1, keepdims=True))
            p = e / jnp.sum(e, axis=-1, keepdims=True)
            pb = _bf(p)
            dv_ref[:, cs] += lax.dot_general(pb, doh, _TN, preferred_element_type=F32)
            dp = lax.dot_general(doh, vh, _NT, preferred_element_type=F32)
            ds = _bf(p * (dp - jnp.sum(dp * p, axis=-1, keepdims=True)) * scale)
            dq_ref[:, cs] = jnp.dot(ds, kh, preferred_element_type=F32).astype(dq_ref.dtype)
            dk_ref[:, cs] += lax.dot_general(ds, qh, _TN, preferred_element_type=F32)
        dxn = lax.dot_general(dq_ref[...], wq_ref[...], _NT, preferred_element_type=F32)
        xh = x_ref[...] * r_ref[...]
        wv = dxn * g_ref[...]
        dxo = dx_ref[...] + r_ref[...] * (wv - xh * jnp.mean(wv * xh, axis=-1, keepdims=True))
        dxo_ref[...] = dxo
        dxbo_ref[...] = dxo.astype(BF16)
        dg_ref[...] += jnp.sum(dxn * xh, axis=0, keepdims=True)

    def whole(a):
        return pl.BlockSpec(a.shape, lambda i: (0, 0), pipeline_mode=pl.Buffered(1))

    rows = pl.BlockSpec((tm, d), lambda i: (i, 0))
    col = pl.BlockSpec((tm, 1), lambda i: (i, 0))
    acc = pl.BlockSpec((m, d), lambda i: (0, 0))
    return pl.pallas_call(
        body, grid=(t // tm,),
        in_specs=[rows, rows, rows, col, whole(gain), rows, whole(k), whole(v), whole(wq), whole(wo)],
        out_specs=[rows, rows, rows, acc, acc, pl.BlockSpec((1, d), lambda i: (0, 0))],
        out_shape=[S((t, d), F32), S((t, d), BF16), S((t, d), BF16), S((m, d), F32), S((m, d), F32), S((1, d), F32)],
        scratch_shapes=[pltpu.VMEM((tm, d), BF16)],
        compiler_params=_cp("arbitrary"), name=name)(dx, dxb, x, r, gain, q, k, v, wq, wo)


SUB = 8
S5_ROWS = 256


S5_BLOCKS = 4
BLOCK_CH = C_WIDTH // S5_BLOCKS
BLOCK_ST = N_STATE // S5_BLOCKS
_S5_BLOCKS = tuple((slice(BLOCK_CH * q, BLOCK_CH * (q + 1)), slice(BLOCK_ST * q, BLOCK_ST * (q + 1)),
                    slice(N_STATE + BLOCK_ST * q, N_STATE + BLOCK_ST * (q + 1))) for q in range(S5_BLOCKS))
_HI = lax.Precision.HIGHEST
_GP = (C_GROUPS, C_STATE)
_RP = (C_WIDTH, C_STATE)


def _zoh(lr, li, ldt):
    dt = jnp.exp(ldt)
    mag = jnp.exp(lr * dt)
    ar = mag * jnp.cos(li * dt)
    ai = mag * jnp.sin(li * dt)
    den = lr * lr + li * li
    qr = ((ar - 1.0) * lr + ai * li) / den
    qi = (ai * lr - (ar - 1.0) * li) / den
    return dt, ar, ai, den, qr, qi


def _per_channel(v):
    return jnp.broadcast_to(v[:, None, :], (C_GROUPS, C_GROUP_CH, C_STATE)).reshape(_RP)


def _same_group(shape, row_per_group, col_per_group):
    rows = lax.broadcasted_iota(jnp.int32, shape, 0) // row_per_group
    cols = lax.broadcasted_iota(jnp.int32, shape, 1) // col_per_group
    return rows == cols


def _spread(shape, axis):
    long = lax.broadcasted_iota(jnp.int32, shape, axis) % C_STATE
    short = lax.broadcasted_iota(jnp.int32, shape, 1 - axis)
    return long == short


def s5_discretise(name, lam_re, lam_im, log_dt, bt_re, bt_im):
    def body(lr_ref, li_ref, ldt_ref, btr_ref, bti_ref, a_ref, bbr_ref, bbi_ref):
        _, ar, ai, _, qr, qi = _zoh(lr_ref[...], li_ref[...], ldt_ref[...])
        a_ref[0] = ar
        a_ref[1] = ai
        q2r, q2i = _per_channel(qr), _per_channel(qi)
        btr, bti = btr_ref[...], bti_ref[...]
        bbr_ref[...] = q2r * btr - q2i * bti
        bbi_ref[...] = q2r * bti + q2i * btr

    return pl.pallas_call(body, out_shape=[S((2,) + _GP, F32), S(_RP, F32), S(_RP, F32)],
                          name=name)(lam_re, lam_im, log_dt, bt_re, bt_im)


def s5_operands(name, a, bbr, bbi, c2r, c2i, ctr, cti):
    ns = N_STATE

    def body(a_ref, bbr_ref, bbi_ref, c2r_ref, c2i_ref, ctr_ref, cti_ref, pw_ref, qw_ref, mb_ref, mc_ref, mct_ref):
        ar, ai = a_ref[0:1, :], a_ref[1:2, :]
        pows = [(ar, ai)]
        for _ in range(SUB - 1):
            pr, pi = pows[-1]
            pows.append((pr * ar - pi * ai, pr * ai + pi * ar))
        rows = lax.broadcasted_iota(jnp.int32, (SUB, ns), 0)

        def rows_of(v):
            return jnp.broadcast_to(v, (SUB, ns))

        for k, s in enumerate((1, 2, 4)):
            pr, pi = rows_of(pows[s - 1][0]), rows_of(pows[s - 1][1])
            pw_ref[k, 0] = jnp.where(rows >= s, pr, 0.0)
            pw_ref[k, 1] = jnp.where(rows >= s, pi, 0.0)
            qw_ref[k, 0] = jnp.where(rows + s <= SUB - 1, pr, 0.0)
            qw_ref[k, 1] = jnp.where(rows + s <= SUB - 1, -pi, 0.0)
        fr = fi = br = bi = jnp.zeros((SUB, ns), F32)
        for i in range(SUB):
            fr = jnp.where(rows == i, rows_of(pows[i][0]), fr)
            fi = jnp.where(rows == i, rows_of(pows[i][1]), fi)
            br = jnp.where(rows == i, rows_of(pows[SUB - 1 - i][0]), br)
            bi = jnp.where(rows == i, rows_of(-pows[SUB - 1 - i][1]), bi)
        pw_ref[3, 0], pw_ref[3, 1], qw_ref[3, 0], qw_ref[3, 1] = fr, fi, br, bi

        wide = _spread((C_STATE, ns), 1).astype(BF16)
        tall = _spread((ns, C_STATE), 0).astype(BF16)
        in_rows = _same_group((C_WIDTH, ns), C_GROUP_CH, C_STATE)
        in_cols = _same_group((ns, C_WIDTH), C_STATE, C_GROUP_CH)

        def across(v, sign=1.0):
            return jnp.where(in_rows, sign * jnp.dot(_bf(v), wide, preferred_element_type=F32), 0.0).astype(BF16)

        def down(vt, sign=1.0):
            return jnp.where(in_cols, sign * jnp.dot(tall, _bf(vt), preferred_element_type=F32), 0.0).astype(BF16)

        mb_ref[:, 0:ns] = across(bbr_ref[...])
        mb_ref[:, ns:2 * ns] = across(bbi_ref[...])
        mct_ref[:, 0:ns] = across(c2r_ref[...])
        mct_ref[:, ns:2 * ns] = across(c2i_ref[...], -1.0)
        mc_ref[0:ns, :] = down(ctr_ref[...])
        mc_ref[ns:2 * ns, :] = down(cti_ref[...], -1.0)

    return pl.pallas_call(
        body, out_shape=[S((4, 2, SUB, ns), F32), S((4, 2, SUB, ns), F32), S((C_WIDTH, 2 * ns), BF16),
                         S((2 * ns, C_WIDTH), BF16), S((C_WIDTH, 2 * ns), BF16)],
        compiler_params=pltpu.CompilerParams(vmem_limit_bytes=VMEM_LIMIT), name=name)(a, bbr, bbi, c2r, c2i, ctr, cti)


def s5_block_grads(name, u, lamb, xsb, dyb):
    t = u.shape[0]

    def mb_body(u_ref, lr_ref, li_ref, o_ref):
        ub = _bf(u_ref[...])
        o_ref[:, 0:BLOCK_ST] = lax.dot_general(ub, lr_ref[...], _TN, preferred_element_type=F32)
        o_ref[:, BLOCK_ST:2 * BLOCK_ST] = lax.dot_general(ub, li_ref[...], _TN, preferred_element_type=F32)

    d_mb = pl.pallas_call(
        mb_body, grid=(S5_BLOCKS,),
        in_specs=[pl.BlockSpec((t, BLOCK_CH), lambda q: (0, q)), pl.BlockSpec((t, BLOCK_ST), lambda q: (0, q)),
                  pl.BlockSpec((t, BLOCK_ST), lambda q: (0, S5_BLOCKS + q))],
        out_specs=pl.BlockSpec((BLOCK_CH, 2 * BLOCK_ST), lambda q: (q, 0)), out_shape=S((C_WIDTH, 2 * BLOCK_ST), F32),
        compiler_params=_cp("parallel"), name=name + "_b")(u, lamb, lamb)

    def mc_body(x_ref, dy_ref, o_ref):
        o_ref[...] = lax.dot_general(x_ref[...], dy_ref[...], _TN, preferred_element_type=F32)

    d_mc = pl.pallas_call(
        mc_body, grid=(2, S5_BLOCKS),
        in_specs=[pl.BlockSpec((t, BLOCK_ST), lambda p, q: (0, p * S5_BLOCKS + q)), pl.BlockSpec((t, BLOCK_CH), lambda p, q: (0, q))],
        out_specs=pl.BlockSpec((BLOCK_ST, BLOCK_CH), lambda p, q: (p * S5_BLOCKS + q, 0)),
        out_shape=S((2 * N_STATE, BLOCK_CH), F32), compiler_params=_cp("parallel", "parallel"), name=name + "_c")(xsb, dyb)
    return d_mb, d_mc


def s5_param_grads(name, d_mb, d_mc, da, lam_re, lam_im, log_dt, bt_re, bt_im):
    ns = N_STATE

    def body(dmb_ref, dmc_ref, da_ref, lr_ref, li_ref, ldt_ref, btr_ref, bti_ref,
             glr_ref, gli_ref, gdt_ref, gbr_ref, gbi_ref, gcr_ref, gci_ref):
        lr, li = lr_ref[...], li_ref[...]
        dt, ar, ai, den, qr, qi = _zoh(lr, li, ldt_ref[...])
        per_block = C_GROUPS // S5_BLOCKS
        wide = _spread((C_STATE, BLOCK_ST), 1).astype(F32)
        tall = _spread((BLOCK_ST, C_STATE), 0).astype(F32)
        rows = lax.broadcasted_iota(jnp.int32, (C_WIDTH, BLOCK_ST), 0) // C_GROUP_CH % per_block
        in_rows = rows == lax.broadcasted_iota(jnp.int32, (C_WIDTH, BLOCK_ST), 1) // C_STATE
        in_cols = _same_group((BLOCK_ST, BLOCK_CH), C_STATE, C_GROUP_CH)

        def fold_rows(v):
            return lax.dot_general(jnp.where(in_rows, v, 0.0), wide, (((1,), (1,)), ((), ())), precision=_HI,
                                   preferred_element_type=F32)

        def fold_cols(v):
            return lax.dot_general(jnp.where(in_cols, v, 0.0), tall, (((0,), (0,)), ((), ())), precision=_HI,
                                   preferred_element_type=F32)

        for cs, s_re, s_im in _S5_BLOCKS:
            gcr_ref[cs, :] = fold_cols(dmc_ref[s_re, :])
            gci_ref[cs, :] = -fold_cols(dmc_ref[s_im, :])
        gbbr = fold_rows(dmb_ref[:, 0:BLOCK_ST])
        gbbi = fold_rows(dmb_ref[:, BLOCK_ST:2 * BLOCK_ST])
        btr, bti = btr_ref[...], bti_ref[...]
        q2r, q2i = _per_channel(qr), _per_channel(qi)
        gbr_ref[...] = q2r * gbbr + q2i * gbbi
        gbi_ref[...] = q2r * gbbi - q2i * gbbr

        def per_group(v):
            return jnp.sum(v.reshape(C_GROUPS, C_GROUP_CH, C_STATE), axis=1)

        gqr = per_group(btr * gbbr + bti * gbbi)
        gqi = per_group(btr * gbbi - bti * gbbr)
        ilr, ili = lr / den, li / den
        gar = da_ref[0] + ilr * gqr - ili * gqi
        gai = da_ref[1] + ilr * gqi + ili * gqr
        sr = (qr * lr + qi * li) / den
        si = (qi * lr - qr * li) / den
        gzr = ar * gar + ai * gai
        gzi = ar * gai - ai * gar
        glr_ref[...] = -sr * gqr - si * gqi + dt * gzr
        gli_ref[...] = -sr * gqi + si * gqr + dt * gzi
        gdt_ref[...] = jnp.sum(lr * gzr + li * gzi, axis=1, keepdims=True) * dt

    return pl.pallas_call(
        body, out_shape=[S(_GP, F32), S(_GP, F32), S((C_GROUPS, 1), F32), S(_RP, F32), S(_RP, F32), S(_RP, F32), S(_RP, F32)],
        compiler_params=pltpu.CompilerParams(vmem_limit_bytes=VMEM_LIMIT), name=name,
    )(d_mb, d_mc, da, lam_re, lam_im, log_dt, bt_re, bt_im)


def _cmul_add(xr, xi, pr, pi, zr, zi):
    return xr + pr * zr - pi * zi, xi + pr * zi + pi * zr


def s5_fwd(name, x, gain, w_in, mb, mc, pw, dskip, w_out_t):
    t, d = x.shape
    tm = _tile(t, S5_ROWS)
    ns = N_STATE

    def body(x_ref, g_ref, wi_ref, mb_ref, mc_ref, pw_ref, d_ref, wo_ref,
             x1_ref, hn_ref, r_ref, u_ref, gy_ref, y_ref, xs_ref, xb_ref, o1_ref, o2_ref, carry):
        @pl.when(pl.program_id(0) == 0)
        def _():
            carry[...] = jnp.zeros(carry.shape, F32)

        xv = x_ref[...]
        rv = lax.rsqrt(jnp.mean(xv * xv, axis=-1, keepdims=True) + EPS)
        hn = (xv * rv * g_ref[...]).astype(BF16)
        hn_ref[...] = hn
        r_ref[...] = rv
        uv = jnp.dot(hn, wi_ref[...], preferred_element_type=F32)
        u_ref[...] = uv
        ub = _bf(uv)
        for cs, s_re, s_im in _S5_BLOCKS:
            xs_ref[:, s_re] = jnp.dot(ub[:, cs], mb_ref[cs, s_re], preferred_element_type=F32)
            xs_ref[:, s_im] = jnp.dot(ub[:, cs], mb_ref[cs, s_im], preferred_element_type=F32)

        def group(i, _):
            r0 = pl.multiple_of(i * SUB, SUB)
            xr = xs_ref[pl.ds(r0, SUB), 0:ns]
            xi = xs_ref[pl.ds(r0, SUB), ns:2 * ns]
            for k, s in enumerate((1, 2, 4)):
                xr, xi = _cmul_add(xr, xi, pw_ref[k, 0], pw_ref[k, 1], pltpu.roll(xr, s, 0), pltpu.roll(xi, s, 0))
            xr, xi = _cmul_add(xr, xi, pw_ref[3, 0], pw_ref[3, 1], carry[0], carry[1])
            xs_ref[pl.ds(r0, SUB), 0:ns] = xr
            xs_ref[pl.ds(r0, SUB), ns:2 * ns] = xi
            carry[0] = jnp.broadcast_to(xr[SUB - 1:SUB, :], (SUB, ns))
            carry[1] = jnp.broadcast_to(xi[SUB - 1:SUB, :], (SUB, ns))
            return 0
        lax.fori_loop(0, tm // SUB, group, 0)

        xb_ref[...] = _bf(xs_ref[...])
        for cs, s_re, s_im in _S5_BLOCKS:
            y = (jnp.dot(xb_ref[:, s_re], mc_ref[s_re, cs], preferred_element_type=F32)
                 + jnp.dot(xb_ref[:, s_im], mc_ref[s_im, cs], preferred_element_type=F32) + d_ref[:, cs] * uv[:, cs])
            y_ref[:, cs] = y
            gy_ref[:, cs] = _gelu(y).astype(gy_ref.dtype)
        o1 = lax.dot_general(gy_ref[...], wo_ref[0:d, :], _NT, preferred_element_type=F32)
        o2 = lax.dot_general(gy_ref[...], wo_ref[d:2 * d, :], _NT, preferred_element_type=F32)
        o1_ref[...] = o1.astype(BF16)
        o2_ref[...] = o2.astype(BF16)
        x1_ref[...] = xv + o1 * _sigmoid(o2)

    c = w_in.shape[1]
    rows = pl.BlockSpec((tm, d), lambda i: (i, 0))
    narrow = pl.BlockSpec((tm, c), lambda i: (i, 0))
    states = pl.BlockSpec((tm, 2 * ns), lambda i: (i, 0))
    return pl.pallas_call(
        body, grid=(t // tm,),
        in_specs=[rows, _whole(gain), _whole(w_in), _whole(mb), _whole(mc), _whole(pw), _whole(dskip), _whole(w_out_t)],
        out_specs=[rows, rows, pl.BlockSpec((tm, 1), lambda i: (i, 0)), narrow, narrow, narrow, states, states, rows, rows],
        out_shape=[S((t, d), F32), S((t, d), BF16), S((t, 1), F32), S((t, c), F32), S((t, c), BF16), S((t, c), F32),
                   S((t, 2 * ns), F32), S((t, 2 * ns), BF16), S((t, d), BF16), S((t, d), BF16)],
        scratch_shapes=[pltpu.VMEM((2, SUB, ns), F32)],
        compiler_params=_cp("arbitrary"), name=name)(x, gain, w_in, mb, mc, pw, dskip, w_out_t)


def s5_bwd(name, dgy, y, u, xs, mct, mbt, qw, dskip):
    t, c = u.shape
    tm = _tile(t, S5_ROWS)
    nt = t // tm
    ns = N_STATE
    ng = tm // SUB

    def body(dgy_ref, y_ref, u_ref, xs_ref, mct_ref, mbt_ref, qw_ref, d_ref,
             du_ref, dy_ref, lb_ref, da_ref, dd_ref, lam, carry):
        @pl.when(pl.program_id(0) == 0)
        def _():
            carry[...] = jnp.zeros(carry.shape, F32)
            da_ref[...] = jnp.zeros(da_ref.shape, F32)
            dd_ref[...] = jnp.zeros(dd_ref.shape, F32)

        uv = u_ref[...]
        dy = dgy_ref[...] * _gelu_grad(y_ref[...])
        dyb = _bf(dy)
        dy_ref[...] = dyb
        dd_ref[...] += jnp.sum(dy * uv, axis=0, keepdims=True)
        for cs, s_re, s_im in _S5_BLOCKS:
            lam[:, s_re] = jnp.dot(dyb[:, cs], mct_ref[cs, s_re], preferred_element_type=F32)
            lam[:, s_im] = jnp.dot(dyb[:, cs], mct_ref[cs, s_im], preferred_element_type=F32)
        last_row = lax.broadcasted_iota(jnp.int32, (SUB, ns), 0) == SUB - 1

        def group(j, _):
            i = ng - 1 - j
            r0 = pl.multiple_of(i * SUB, SUB)
            lr = lam[pl.ds(r0, SUB), 0:ns]
            li = lam[pl.ds(r0, SUB), ns:2 * ns]
            for k, s in enumerate((1, 2, 4)):
                lr, li = _cmul_add(lr, li, qw_ref[k, 0], qw_ref[k, 1],
                                   pltpu.roll(lr, SUB - s, 0), pltpu.roll(li, SUB - s, 0))
            cr, ci = carry[0], carry[1]
            lr, li = _cmul_add(lr, li, qw_ref[3, 0], qw_ref[3, 1], cr, ci)
            lam[pl.ds(r0, SUB), 0:ns] = lr
            lam[pl.ds(r0, SUB), ns:2 * ns] = li
            carry[0] = jnp.broadcast_to(lr[0:1, :], (SUB, ns))
            carry[1] = jnp.broadcast_to(li[0:1, :], (SUB, ns))
            nr = jnp.where(last_row, cr, pltpu.roll(lr, SUB - 1, 0))
            ni = jnp.where(last_row, ci, pltpu.roll(li, SUB - 1, 0))
            xr = xs_ref[pl.ds(r0, SUB), 0:ns]
            xi = xs_ref[pl.ds(r0, SUB), ns:2 * ns]
            da_ref[0] += nr * xr + ni * xi
            da_ref[1] += ni * xr - nr * xi
            return 0
        lax.fori_loop(0, ng, group, 0)

        lb_ref[...] = _bf(lam[...])
        for cs, s_re, s_im in _S5_BLOCKS:
            du = (jnp.dot(lb_ref[:, s_re], mbt_ref[s_re, cs], preferred_element_type=F32)
                  + jnp.dot(lb_ref[:, s_im], mbt_ref[s_im, cs], preferred_element_type=F32) + d_ref[:, cs] * dy[:, cs])
            du_ref[:, cs] = du.astype(du_ref.dtype)

    rev = lambda i: (nt - 1 - i, 0)
    return pl.pallas_call(
        body, grid=(nt,),
        in_specs=[pl.BlockSpec((tm, c), rev), pl.BlockSpec((tm, c), rev), pl.BlockSpec((tm, c), rev),
                  pl.BlockSpec((tm, 2 * ns), rev),
                  pl.BlockSpec(mct.shape, lambda i: (0, 0)), pl.BlockSpec(mbt.shape, lambda i: (0, 0)),
                  pl.BlockSpec(qw.shape, lambda i: (0, 0, 0, 0)), pl.BlockSpec((1, c), lambda i: (0, 0))],
        out_specs=[pl.BlockSpec((tm, c), rev), pl.BlockSpec((tm, c), rev), pl.BlockSpec((tm, 2 * ns), rev),
                   pl.BlockSpec((2, SUB, ns), lambda i: (0, 0, 0)), pl.BlockSpec((1, c), lambda i: (0, 0))],
        out_shape=[S((t, c), BF16), S((t, c), BF16), S((t, 2 * ns), BF16), S((2, SUB, ns), F32), S((1, c), F32)],
        scratch_shapes=[pltpu.VMEM((tm, 2 * ns), F32), pltpu.VMEM((2, SUB, ns), F32)],
        compiler_params=_cp("arbitrary"), name=name)(dgy, y, u, xs, mct, mbt, qw, dskip)


def _first(accs, *_):
    return [accs[0]]


def _rms_bwd_epi(accs, xv, base, rv, g):
    dv = accs[0]
    w = dv * g
    xh = xv * rv
    dx = base + rv * (w - xh * jnp.mean(w * xh, axis=-1, keepdims=True))
    return [dx, dx, jnp.sum(dv * xh, axis=0, keepdims=True)]


def mm_rms_bwd(name, pairs, x, r, gain, dres):
    t, d = x.shape
    return mm_nn(name, t, d, pairs, 1, _rms_bwd_epi, [F32, BF16], tiled=[x, dres], cols=[r], rowv=[gain], sums=[(1, d)])


def even_fwd(x, w, need_out):
    t = x.shape[0]
    proj, hn, r = mm_nn("e_in_f", t, IN_WIDTH, [(x, w["e_w_in_t"], 0, "t")], 1, _first, [F32], norm_gain=w["e_norm"])
    hc = conv_fwd("e_conv_f", proj, w["e_conv_w"], w["e_conv_b"])
    need_out(hc)
    x1, out_a, out_b = even_out_fwd("e_out_f", proj, hc, x, w["e_gmlp_w"], w["e_gmlp_b"], w["e_conv_ln_g"], w["e_conv_ln_b"],
                                    w["e_w_out"])
    return x1, (x, hn, r, proj, out_a, hc, out_b)


def even_bwd_mixers(dxb, saved, w):
    x, hn, r, proj, out_a, hc, out_b = saved
    t = x.shape[0]
    g_w_out = jnp.concatenate([mm_tn("e_out_wa", out_a, dxb), mm_tn("e_out_wb", out_b, dxb)], axis=0)
    dab, g_gw, g_gb = gmlp_bwd("e_gmlp_b", proj, dxb, w["e_w_out"], w["e_gmlp_w"], w["e_gmlp_b"])
    dhc, g_lg, g_lb = ln_silu_bwd("e_ln_b", hc, dxb, w["e_w_out"], w["e_conv_ln_g"], w["e_conv_ln_b"])
    dba, dbg, g_cw, g_cb = conv_bwd("e_conv_b", proj, dhc, w["e_conv_w"])
    g_w_in_t = jnp.concatenate([mm_tn("e_in_w0", dab, hn), mm_tn("e_in_w1", dba, hn), mm_tn("e_in_w2", dbg, hn)], axis=0)
    grads = dict(e_w_in_t=g_w_in_t, e_gmlp_w=g_gw[None], e_gmlp_b=g_gb.reshape(1, A_GROUPS, GMLP_BLOCK),
                 e_conv_w=g_cw[None], e_conv_b=g_cb, e_conv_ln_g=g_lg, e_conv_ln_b=g_lb, e_w_out=g_w_out)
    return (dab, dba, dbg), grads


def even_bwd_input(dx, dproj, saved, w):
    x, _, r = saved[:3]
    dab, dba, dbg = dproj
    w_in_t = w["e_w_in_t"]
    return mm_rms_bwd("e_in_b", [(dab, (w_in_t, 0), 0), (dba, (w_in_t, 2), 0), (dbg, (w_in_t, 3), 0)], x, r, w["e_norm"], dx)


def s5_setup(w, anchor=None):
    def rows(v):
        return v.transpose(0, 2, 1).reshape(_RP)

    log_dt = w["o_log_dt"].reshape(C_GROUPS, 1)
    if anchor is not None:
        log_dt = log_dt + anchor
    lam = (w["o_lam_re"], w["o_lam_im"], log_dt, rows(w["o_b_re"]), rows(w["o_b_im"]))
    a, bbr, bbi = s5_discretise("o_s5_zoh", *lam)
    c_re, c_im = w["o_c_re"], w["o_c_im"]
    pw, qw, mb, mc, mct = s5_operands("o_s5_ops", a.reshape(2, N_STATE), bbr, bbi, c_re.reshape(_RP), c_im.reshape(_RP),
                                      c_re.transpose(2, 0, 1).reshape(C_STATE, C_WIDTH),
                                      c_im.transpose(2, 0, 1).reshape(C_STATE, C_WIDTH))
    return dict(lam=lam, pw=pw, qw=qw, mb=mb, mc=mc, mct=mct, mbt=mb.T)


def odd_fwd(x, w, consts):
    x1, hn, r, u, gy, y, xs, xsb, o1, o2 = s5_fwd("o_s5_f", x, w["o_norm"], w["o_w_in"], consts["mb"], consts["mc"],
                                                  consts["pw"], w["o_d"], w["o_w_out_t"])
    return x1, (x, hn, r, u, gy, y, xs, xsb, o1, o2)


def odd_bwd(dx, dxb, saved, w, consts):
    x, hn, r, u, gy, y, xs, xsb, o1, o2 = saved
    t = x.shape[0]

    def gate_bwd(dv, a, b, wv):
        a = a.astype(F32)
        sg = _sigmoid(b.astype(F32))
        do12 = jnp.concatenate([dv * sg, dv * a * sg * (1.0 - sg)], axis=1).astype(BF16)
        return [do12, jnp.dot(do12, wv, preferred_element_type=F32)], []

    do12, dgy = rows_call("o_out_b", gate_bwd, [dx, o1, o2], [w["o_w_out_t"]], [(2 * D_MODEL, BF16), (C_WIDTH, F32)], [])
    g_w_out_t = mm_tn("o_out_w", do12, gy)
    du, dyb, lamb, da8, g_d = s5_bwd("o_s5_b", dgy, y, u, xs, consts["mct"], consts["mbt"], consts["qw"], w["o_d"])
    d_mb, d_mc = s5_block_grads("o_s5_w", u, lamb, xsb, dyb)
    da = jnp.sum(da8, axis=1).reshape((2,) + _GP)
    g_lr, g_li, g_dt, g_btr, g_bti, g_cr, g_ci = s5_param_grads("o_s5_pg", d_mb, d_mc, da, *consts["lam"])

    def states_first(v):
        return v.reshape(C_GROUPS, C_GROUP_CH, C_STATE).transpose(0, 2, 1)[None]

    g_w_in = mm_tn("o_in_w", hn, du)
    dx0, dx0b, g_norm = mm_rms_bwd("o_in_b", [(du, w["o_w_in"], 0, "t")], x, r, w["o_norm"], dx)
    grads = dict(o_norm=g_norm, o_w_in=g_w_in, o_lam_re=g_lr[None], o_lam_im=g_li[None], o_log_dt=g_dt.reshape(1, C_GROUPS),
                 o_b_re=states_first(g_btr), o_b_im=states_first(g_bti),
                 o_c_re=g_cr.reshape((1, C_GROUPS, C_GROUP_CH, C_STATE)), o_c_im=g_ci.reshape((1, C_GROUPS, C_GROUP_CH, C_STATE)),
                 o_d=g_d, o_w_out_t=g_w_out_t)
    return dx0, dx0b, grads


def ca_fwd(i, x, mem, w):
    t, m = x.shape[0], mem.shape[0]
    k, v, mn, rm = mm_nn(f"ca{i}_kv_f", m, D_MODEL, [(mem, w["ca_wk"][i], 0), (mem, w["ca_wv"][i], 1)], 2,
                         lambda accs: [accs[0], accs[1]], [BF16, BF16], norm_gain=w["ca_mem_norm"][i:i + 1])
    x1, xn, r, q, o = attn_fwd(f"ca{i}_attn_f", x, w["ca_norm"][i:i + 1], w["ca_wq"][i], k, v, w["ca_wo"][i])
    return x1, (x, xn, r, mn, rm, q, k, v, o)


def ca_bwd(i, dx, dxb, saved, mem, w):
    x, xn, r, mn, rm, q, k, v, o = saved
    t, m = x.shape[0], mem.shape[0]
    g_wo = mm_tn(f"ca{i}_o_w", o, dxb)
    dx0, dx0b, dq, dk, dv, g_norm = attn_bwd(f"ca{i}_attn_b", dx, dxb, x, r, w["ca_norm"][i:i + 1], q, k, v,
                                             w["ca_wq"][i], w["ca_wo"][i])
    g_wq = mm_tn(f"ca{i}_q_w", xn, dq)
    g_wk = mm_tn(f"ca{i}_k_w", mn, dk)
    g_wv = mm_tn(f"ca{i}_v_w", mn, dv)
    (dmn,) = mm_nn(f"ca{i}_kv_b", m, D_MODEL, [(dk, w["ca_wk"][i], 0, "t"), (dv, w["ca_wv"][i], 0, "t")], 1, _first, [F32])
    g_mnorm = rms_bwd_gain_only(f"ca{i}_mnorm_b", dmn, mem, rm)
    return dx0, dx0b, dict(ca_norm=g_norm, ca_mem_norm=g_mnorm, ca_wq=g_wq, ca_wk=g_wk, ca_wv=g_wv, ca_wo=g_wo)


FFN_ROWS = 512
FFN_CHUNK = 256


def _whole(a):
    return pl.BlockSpec(a.shape, lambda i: (0,) * a.ndim, pipeline_mode=pl.Buffered(1))


def ffn_fused_fwd(name, x, gain, wg_t, wu_t, wd, target=None, final_gain=None):
    t, d = x.shape
    hid = wd.shape[0]
    tm = _tile(t, FFN_ROWS)
    last = target is not None
    n_main = 4 if last else 1

    def body(*refs):
        x_ref, g_ref, wg_ref, wu_ref, wd_ref = refs[:5]
        rest = refs[5:]
        if last:
            tgt_ref, fg_ref = rest[:2]
            rest = rest[2:]
        main, (xn_ref, r_ref, dgate_ref, dup_ref, h_ref) = rest[:n_main], rest[n_main:]
        xv = x_ref[...]
        rv = lax.rsqrt(jnp.mean(xv * xv, axis=-1, keepdims=True) + EPS)
        xn = (xv * rv * g_ref[...]).astype(BF16)
        xn_ref[...] = xn
        r_ref[...] = rv
        for j in range(hid // FFN_CHUNK):
            cs = slice(j * FFN_CHUNK, (j + 1) * FFN_CHUNK)
            g = lax.dot_general(xn, wg_ref[cs, :], _NT, preferred_element_type=F32)
            u = lax.dot_general(xn, wu_ref[cs, :], _NT, preferred_element_type=F32)
            s = _sigmoid(g)
            silu = g * s
            dgate_ref[:, cs] = (u * (s + silu * (1.0 - s))).astype(BF16)
            dup_ref[:, cs] = silu.astype(BF16)
            h_ref[:, cs] = (silu * u).astype(BF16)
        acc = jnp.dot(h_ref[...], wd_ref[...], preferred_element_type=F32)
        if not last:
            main[0][...] = xv + acc
        else:
            dx, _, dgain, part = _final_loss_epi([acc], xv, tgt_ref[...], fg_ref[...])

            @pl.when(pl.program_id(0) == 0)
            def _():
                main[2][...] = jnp.zeros(main[2].shape, F32)
                main[3][...] = jnp.zeros(main[3].shape, F32)
            main[0][...] = dx
            main[1][...] = dx.astype(BF16)
            main[2][...] += dgain
            main[3][...] += part

    rows = pl.BlockSpec((tm, d), lambda i: (i, 0))
    wide = pl.BlockSpec((tm, hid), lambda i: (i, 0))
    col = pl.BlockSpec((tm, 1), lambda i: (i, 0))
    ins, in_specs = [x, gain, wg_t, wu_t, wd], [rows, _whole(gain), _whole(wg_t), _whole(wu_t), _whole(wd)]
    if last:
        ins += [target, final_gain]
        in_specs += [rows, _whole(final_gain)]
        out_specs = [rows, rows, pl.BlockSpec((1, d), lambda i: (0, 0)), pl.BlockSpec((1, 1), lambda i: (0, 0))]
        out_shape = [S((t, d), F32), S((t, d), BF16), S((1, d), F32), S((1, 1), F32)]
    else:
        out_specs, out_shape = [rows], [S((t, d), F32)]
    out_specs += [rows, col, wide, wide, wide]
    out_shape += [S((t, d), BF16), S((t, 1), F32)] + [S((t, hid), BF16)] * 3
    outs = pl.pallas_call(body, grid=(t // tm,), in_specs=in_specs, out_specs=out_specs, out_shape=out_shape,
                          compiler_params=_cp("arbitrary" if last else "parallel"), name=name)(*ins)
    return (tuple(outs[:4]) if last else outs[0]), outs[n_main:]


def ffn_fused_bwd(name, dx, dxb, x, r, gain, dgate, dup, wg_t, wu_t, wd):
    t, d = x.shape
    hid = wd.shape[0]
    tm = _tile(t, FFN_ROWS // 2)

    def body(dx_ref, dxb_ref, x_ref, r_ref, g_ref, dgate_ref, dup_ref, wg_ref, wu_ref, wd_ref,
             dxo_ref, dxbo_ref, dg_ref, du_ref, dgain_ref):
        @pl.when(pl.program_id(0) == 0)
        def _():
            dgain_ref[...] = jnp.zeros(dgain_ref.shape, F32)

        dxb = dxb_ref[...]
        for j in range(hid // FFN_CHUNK):
            cs = slice(j * FFN_CHUNK, (j + 1) * FFN_CHUNK)
            dh = lax.dot_general(dxb, wd_ref[cs, :], _NT, preferred_element_type=F32)
            dg_ref[:, cs] = (dh * dgate_ref[:, cs].astype(F32)).astype(BF16)
            du_ref[:, cs] = (dh * dup_ref[:, cs].astype(F32)).astype(BF16)
        dxn = (jnp.dot(dg_ref[...], wg_ref[...], preferred_element_type=F32)
               + jnp.dot(du_ref[...], wu_ref[...], preferred_element_type=F32))
        dxo, _, dgain = _rms_bwd_epi([dxn], x_ref[...], dx_ref[...], r_ref[...], g_ref[...])
        dxo_ref[...] = dxo
        dxbo_ref[...] = dxo.astype(BF16)
        dgain_ref[...] += dgain

    rows = pl.BlockSpec((tm, d), lambda i: (i, 0))
    wide = pl.BlockSpec((tm, hid), lambda i: (i, 0))
    col = pl.BlockSpec((tm, 1), lambda i: (i, 0))
    return pl.pallas_call(
        body, grid=(t // tm,),
        in_specs=[rows, rows, rows, col, _whole(gain), wide, wide, _whole(wg_t), _whole(wu_t), _whole(wd)],
        out_specs=[rows, rows, wide, wide, pl.BlockSpec((1, d), lambda i: (0, 0))],
        out_shape=[S((t, d), F32), S((t, d), BF16), S((t, hid), BF16), S((t, hid), BF16), S((1, d), F32)],
        compiler_params=_cp("arbitrary"), name=name)(dx, dxb, x, r, gain, dgate, dup, wg_t, wu_t, wd)


def ffn_fwd(i, x, w, target=None):
    out, (xn, r, dgate, dup, h) = ffn_fused_fwd(f"ffn{i}_f", x, w["ffn_norm"][i:i + 1], w["ffn_w_gate_t"][i],
                                                w["ffn_w_up_t"][i], w["ffn_w_down"][i], target,
                                                None if target is None else w["final_norm"])
    return out, (x, xn, r, dgate, dup, h)


def ffn_bwd(i, dx, dxb, saved, w):
    x, xn, r, dgate, dup, h = saved
    g_wd = mm_tn(f"ffn{i}_down_w", h, dxb)
    dx0, dx0b, dg, du, g_norm = ffn_fused_bwd(f"ffn{i}_b", dx, dxb, x, r, w["ffn_norm"][i:i + 1], dgate, dup,
                                              w["ffn_w_gate_t"][i], w["ffn_w_up_t"][i], w["ffn_w_down"][i])
    g_wg_t = mm_tn(f"ffn{i}_gate_w", dg, xn)
    g_wu_t = mm_tn(f"ffn{i}_up_w", du, xn)
    return dx0, dx0b, dict(ffn_norm=g_norm, ffn_w_gate_t=g_wg_t, ffn_w_up_t=g_wu_t, ffn_w_down=g_wd)


def local_step(x, mem, target, w, fetch=None, on_grads=None, anchor=None):
    consts = s5_setup(w, anchor)

    def need(stage, after):
        if fetch is not None:
            for k, v in fetch(stage, after).items():
                if isinstance(k, tuple):
                    w.setdefault(k[0], {})[k[1]] = v
                else:
                    w[k] = v

    need(0, consts["pw"])
    x1, s_e = even_fwd(x, w, lambda after: need(1, after))
    x2, s_c0 = ca_fwd(0, x1, mem, w)
    need(2, x2)
    x3, s_f0 = ffn_fwd(0, x2, w)
    need(3, x3)
    x4, s_o = odd_fwd(x3, w, consts)
    need(4, x4)
    if fetch is not None:
        fetch(5, x4, True)
    x5, s_c1 = ca_fwd(1, x4, mem, w)
    need(5, x5)
    (dx, dxb, g_final, loss), s_f1 = ffn_fwd(1, x5, w, target)

    def emit(stage, carry, plain, layered=None, layer=0):
        if on_grads is None:
            return carry
        out = dict(plain)
        out.update({(k, layer): v for k, v in (layered or {}).items()})
        return on_grads(stage, out, list(carry))

    dx, dxb, g_f1 = ffn_bwd(1, dx, dxb, s_f1, w)
    dx, dxb = emit(0, (dx, dxb), {}, g_f1, 1)
    dx, dxb, g_c1 = ca_bwd(1, dx, dxb, s_c1, mem, w)
    dx, dxb, g_o = odd_bwd(dx, dxb, s_o, w, consts)
    dx, dxb = emit(1, (dx, dxb), g_o, g_c1, 1)
    dx, dxb, g_f0 = ffn_bwd(0, dx, dxb, s_f0, w)
    dx, dxb = emit(2, (dx, dxb), {}, g_f0, 0)
    dx, dxb, g_c0 = ca_bwd(0, dx, dxb, s_c0, mem, w)
    dx, dxb = emit(3, (dx, dxb), {}, g_c0, 0)
    dproj, g_e = even_bwd_mixers(dxb, s_e, w)
    dproj = emit(4, dproj, {**g_e, "o_norm": g_o["o_norm"], "o_d": g_o["o_d"]})
    dx, dxb, g_e["e_norm"] = even_bwd_input(dx, dproj, s_e, w)

    grads = dict(g_e)
    grads.update(g_o)
    for g0, g1 in ((g_c0, g_c1), (g_f0, g_f1)):
        for k in g0:
            grads[k] = jnp.concatenate([g0[k], g1[k]], axis=0) if k.endswith("norm") else (g0[k], g1[k])
    grads["final_norm"] = g_final
    return loss, dx, grads


def _group(axes):
    pos = {a: lax.axis_index(a) for a in ("x", "y", "c")}
    me = 0
    for a in axes:
        me = me * 2 + pos[a]
    peers = []
    for mask in range(1, 2 ** len(axes)):
        peer = dict(pos)
        for bit, a in enumerate(axes):
            if (mask >> (len(axes) - 1 - bit)) & 1:
                peer[a] = 1 - pos[a]
        idx = 0
        for a in axes:
            idx = idx * 2 + peer[a]
        peers.append((idx, (peer["x"], peer["y"], peer["c"])))
    return me, peers


def _sibling():
    x, y, c = lax.axis_index("x"), lax.axis_index("y"), lax.axis_index("c")
    return c, (x, y, 1 - c)


_HBM =pl.BlockSpec(memory_space=pltpu.HBM)
_SEM = pl.BlockSpec(memory_space=pltpu.SEMAPHORE)
_EFFECT = pltpu.SideEffectType.DATAFLOW_SIDE_EFFECTING


def _gather_peers(direct):
    chip, _ = _group(("x", "y"))
    core = lax.axis_index("c")
    if direct:
        _, peers = _group(_ALL)
        return chip, core, [(idx // 2, idx % 2, dev) for idx, dev in peers]
    _, peers = _group(("x", "y"))
    return chip, core, [(idx, core, dev) for idx, dev in peers]


def gather_ici_start(name, groups, direct):
    flat = [b for g in groups for b in g]
    sizes = [len(g) for g in groups]
    k_ops, n_g = len(flat), len(groups)
    lands = [lax.empty((4, 2) + tuple(b.shape), b.dtype) for b in flat]
    fan = [N_DEV - 1 if d else 3 for d in direct]

    def body(*refs):
        src, land = refs[:k_ops], refs[k_ops:2 * k_ops]
        sems = refs[2 * k_ops:2 * k_ops + 3 * n_g]
        token = refs[-1]
        i = 0
        for g in range(n_g):
            send, recv, loc = sems[3 * g:3 * g + 3]
            chip, core, peers = _gather_peers(direct[g])
            for j in range(sizes[g]):
                pltpu.make_async_copy(src[i], land[i].at[chip, core], loc.at[j]).start()
                for k, (_, _, dev) in enumerate(peers):
                    s = fan[g] * j + k
                    pltpu.make_async_remote_copy(src_ref=src[i], dst_ref=land[i].at[chip, core], send_sem=send.at[s],
                                                 recv_sem=recv.at[s], device_id=dev, device_id_type=MESH).start()
                i += 1
        token[...] = jnp.zeros(token.shape, token.dtype)

    sem_shapes = []
    for s, f in zip(sizes, fan):
        sem_shapes += [pltpu.SemaphoreType.DMA((f * s,)), pltpu.SemaphoreType.DMA((f * s,)), pltpu.SemaphoreType.DMA((s,))]
    thru = [pltpu.HBM(a.shape, a.dtype) for a in flat + lands]
    outs = pl.pallas_call(
        body, name=name, out_shape=tuple(sem_shapes) + tuple(thru) + (S((8, LANES), F32),),
        in_specs=[_HBM] * (2 * k_ops), out_specs=[_SEM] * (3 * n_g) + [_HBM] * (2 * k_ops) + [pl.BlockSpec(memory_space=pltpu.VMEM)],
        input_output_aliases={i: 3 * n_g + i for i in range(2 * k_ops)},
        compiler_params=pltpu.CompilerParams(has_side_effects=_EFFECT),
    )(*[pltpu.with_memory_space_constraint(a, pltpu.HBM) for a in flat + lands])
    sems = [tuple(outs[3 * g:3 * g + 3]) for g in range(n_g)]
    srcs_thru, lands_thru, off = [], [], 3 * n_g
    for s in sizes:
        srcs_thru.append(list(outs[off:off + s]))
        off += s
    for s in sizes:
        lands_thru.append(list(outs[off:off + s]))
        off += s
    return sems, srcs_thru, lands_thru, outs[-1]


def gather_ici_wait(name, srcs, lands, sems, after, direct=False):
    n = len(srcs)

    def body(*refs):
        src, land = refs[:n], refs[n:2 * n]
        send, recv, loc = refs[2 * n:2 * n + 3]
        chip, core, peers = _gather_peers(direct)
        for j in range(n):
            for k, (pchip, pcore, dev) in enumerate(peers):
                s = len(peers) * j + k
                cp = pltpu.make_async_remote_copy(src_ref=src[j], dst_ref=land[j].at[pchip, pcore], send_sem=send.at[s],
                                                  recv_sem=recv.at[s], device_id=dev, device_id_type=MESH)
                cp.wait_send()
                cp.wait_recv()
            pltpu.make_async_copy(src[j], land[j].at[chip, core], loc.at[j]).wait()

    outs = pl.pallas_call(
        body, name=name, out_shape=tuple(pltpu.HBM(a.shape, a.dtype) for a in list(srcs) + list(lands)),
        in_specs=[_HBM] * (2 * n) + [_SEM] * 3 + [ANY], out_specs=[_HBM] * (2 * n),
        input_output_aliases={i: i for i in range(2 * n)},
        compiler_params=pltpu.CompilerParams(has_side_effects=_EFFECT),
    )(*srcs, *lands, *sems, after)
    return list(outs[n:])


def gather_d2d(name, bufs):
    k_ops = len(bufs)

    def body(*refs):
        in_refs, out_refs = refs[:k_ops], refs[k_ops:2 * k_ops]
        send_sems, recv_sems = refs[2 * k_ops:]
        core, sib = _sibling()
        sent, landed = [], []
        for i in range(k_ops):
            cp = pltpu.make_async_remote_copy(src_ref=in_refs[i].at[:, core], dst_ref=out_refs[i].at[:, core],
                                              send_sem=send_sems.at[i], recv_sem=recv_sems.at[i], device_id=sib, device_id_type=MESH)
            cp.start()
            sent.append(cp)
            landed.append(pltpu.make_async_remote_copy(src_ref=in_refs[i].at[:, core], dst_ref=out_refs[i].at[:, 1 - core],
                                                       send_sem=send_sems.at[i], recv_sem=recv_sems.at[i],
                                                       device_id=sib, device_id_type=MESH))
        for cp in landed:
            cp.wait_recv()
        for cp in sent:
            cp.wait_send()

    return pl.pallas_call(
        body, in_specs=[ANY] * k_ops, out_specs=[ANY] * k_ops, out_shape=[S(b.shape, b.dtype) for b in bufs],
        input_output_aliases={i: i for i in range(k_ops)},
        scratch_shapes=[pltpu.SemaphoreType.DMA((k_ops,)), pltpu.SemaphoreType.DMA((k_ops,))],
        name=name)(*bufs)


def gather_d2d_start(name, bufs):
    k_ops = len(bufs)

    def body(*refs):
        in_refs = refs[:k_ops]
        send, recv = refs[k_ops], refs[k_ops + 1]
        core, sib = _sibling()
        for i in range(k_ops):
            pltpu.make_async_remote_copy(src_ref=in_refs[i].at[:, core], dst_ref=in_refs[i].at[:, core], send_sem=send.at[i],
                                         recv_sem=recv.at[i], device_id=sib, device_id_type=MESH).start()

    outs = pl.pallas_call(
        body, name=name,
        out_shape=(pltpu.SemaphoreType.DMA((k_ops,)), pltpu.SemaphoreType.DMA((k_ops,))) + tuple(pltpu.HBM(b.shape, b.dtype) for b in bufs),
        in_specs=[_HBM] * k_ops, out_specs=[_SEM, _SEM] + [_HBM] * k_ops,
        input_output_aliases={i: 2 + i for i in range(k_ops)},
        compiler_params=pltpu.CompilerParams(has_side_effects=_EFFECT),
    )(*[pltpu.with_memory_space_constraint(b, pltpu.HBM) for b in bufs])
    return (outs[0], outs[1]), list(outs[2:])


def gather_d2d_wait(name, bufs, sems, after):
    k_ops = len(bufs)

    def body(*refs):
        in_refs = refs[:k_ops]
        send, recv = refs[k_ops], refs[k_ops + 1]
        core, sib = _sibling()
        for i in range(k_ops):
            cp = pltpu.make_async_remote_copy(src_ref=in_refs[i].at[:, core], dst_ref=in_refs[i].at[:, 1 - core], send_sem=send.at[i],
                                              recv_sem=recv.at[i], device_id=sib, device_id_type=MESH)
            cp.wait_send()
            cp.wait_recv()

    outs = pl.pallas_call(
        body, name=name, out_shape=tuple(pltpu.HBM(b.shape, b.dtype) for b in bufs),
        in_specs=[_HBM] * k_ops + [_SEM, _SEM, ANY], out_specs=[_HBM] * k_ops,
        input_output_aliases={i: i for i in range(k_ops)},
        compiler_params=pltpu.CompilerParams(has_side_effects=_EFFECT),
    )(*bufs, sems[0], sems[1], after)
    return list(outs)


_ALL = ("x", "y", "c")


def _unit_rows(units):
    offs, off = [], 0
    for u in units:
        offs.append(off)
        off += u.shape[1]
    return offs, off


def scatter_start(name, units, carry):
    n_u, n_c = len(units), len(carry)
    offs, rows = _unit_rows(units)
    land = lax.empty((N_DEV, rows) + tuple(units[0].shape[2:]), units[0].dtype)
    fan = N_DEV - 1

    def body(*refs):
        u_refs, land_ref = refs[:n_u], refs[n_u]
        send, recv, loc = refs[n_u + 1 + n_c:n_u + 4 + n_c]
        me, peers = _group(_ALL)
        for j in range(n_u):
            rs = pl.ds(offs[j], units[j].shape[1])
            pltpu.make_async_copy(u_refs[j].at[me], land_ref.at[me, rs], loc.at[j]).start()
            for k, (idx, dev) in enumerate(peers):
                pltpu.make_async_remote_copy(src_ref=u_refs[j].at[idx], dst_ref=land_ref.at[me, rs], send_sem=send.at[fan * j + k],
                                             recv_sem=recv.at[fan * j + k], device_id=dev, device_id_type=MESH).start()

    thru = list(units) + [land] + list(carry)
    outs = pl.pallas_call(
        body, name=name,
        out_shape=(pltpu.SemaphoreType.DMA((fan * n_u,)), pltpu.SemaphoreType.DMA((fan * n_u,)), pltpu.SemaphoreType.DMA((n_u,)))
        + tuple(pltpu.HBM(a.shape, a.dtype) for a in thru),
        in_specs=[_HBM] * len(thru), out_specs=[_SEM] * 3 + [_HBM] * len(thru),
        input_output_aliases={i: 3 + i for i in range(len(thru))},
        compiler_params=pltpu.CompilerParams(has_side_effects=_EFFECT),
    )(*[pltpu.with_memory_space_constraint(a, pltpu.HBM) for a in thru])
    return tuple(outs[:3]), list(outs[3:3 + n_u]), outs[3 + n_u], list(outs[4 + n_u:])


def scatter_wait(name, units, land, sems, after):
    n_u = len(units)
    offs, _ = _unit_rows(units)
    fan = N_DEV - 1

    def body(*refs):
        u_refs, land_ref = refs[:n_u], refs[n_u]
        send, recv, loc = refs[n_u + 1:n_u + 4]
        me, peers = _group(_ALL)
        for j in range(n_u):
            rs = pl.ds(offs[j], units[j].shape[1])
            for k, (idx, dev) in enumerate(peers):
                cp = pltpu.make_async_remote_copy(src_ref=u_refs[j].at[idx], dst_ref=land_ref.at[idx, rs], send_sem=send.at[fan * j + k],
                                                  recv_sem=recv.at[fan * j + k], device_id=dev, device_id_type=MESH)
                cp.wait_send()
                cp.wait_recv()
            pltpu.make_async_copy(u_refs[j].at[me], land_ref.at[me, rs], loc.at[j]).wait()

    thru = list(units) + [land]
    outs = pl.pallas_call(
        body, name=name, out_shape=tuple(pltpu.HBM(a.shape, a.dtype) for a in thru),
        in_specs=[_HBM] * len(thru) + [_SEM] * 3 + [ANY], out_specs=[_HBM] * len(thru),
        input_output_aliases={i: i for i in range(len(thru))},
        compiler_params=pltpu.CompilerParams(has_side_effects=_EFFECT),
    )(*thru, *sems, after)
    return outs[n_u]


def _row_tile(rows, cap=512):
    return next(t for t in range(cap - cap % 16, 0, -16) if rows % t == 0)


def sum_shares(name, recv, me):
    n, rows, c = recv.shape
    tr = _row_tile(rows)

    def body(me_ref, *refs):
        acc = refs[0][...].astype(F32)
        for r in refs[1:n]:
            acc = acc + r[...].astype(F32)
        refs[n][...] = acc

    def slot(mask):
        return pl.BlockSpec((None, tr, c), lambda i, me, mask=mask: (jnp.bitwise_xor(me[0], mask), i, 0))

    spec = pltpu.PrefetchScalarGridSpec(
        num_scalar_prefetch=1, grid=(rows // tr,), in_specs=[slot(k) for k in range(n)],
        out_specs=pl.BlockSpec((tr, c), lambda i, me: (i, 0)))
    return pl.pallas_call(body, grid_spec=spec, out_shape=S((rows, c), F32),
                          compiler_params=_cp("parallel"), name=name)(me, *([recv] * n))


def sum_slots(name, slots):
    n, r, c = slots.shape

    def body(s_ref, o_ref):
        acc = s_ref[0]
        for j in range(1, n):
            acc = acc + s_ref[j]
        o_ref[...] = acc

    return pl.pallas_call(body, out_shape=S((r, c), F32), compiler_params=pltpu.CompilerParams(vmem_limit_bytes=VMEM_LIMIT),
                          name=name)(slots)


def adamw_units(name, pieces, transposed, w, m, v):
    n_l, k, n = w.shape
    tk = _tile(k, 512) if transposed else k
    p_rows = n if transposed else k
    arrs = [p[0] if isinstance(p, tuple) else p for p in pieces]
    offs = [p[1] // p_rows if isinstance(p, tuple) else 0 for p in pieces]
    assert all(not isinstance(p, tuple) or p[1] % p_rows == 0 for p in pieces)
    c1 = 1.0 - ADAM_B1 ** ADAM_STEP
    c2 = 1.0 - ADAM_B2 ** ADAM_STEP

    def body(*refs):
        p_refs, (w_ref, m_ref, v_ref, g_ref, d_ref, m2_ref, v2_ref) = refs[:n_l], refs[n_l:]
        gv = p_refs[0][...]
        for j in range(1, n_l):
            gv = jnp.where(pl.program_id(0) == j, p_refs[j][...], gv)
        if transposed:
            gv = gv.T
        m2 = ADAM_B1 * m_ref[...] + (1.0 - ADAM_B1) * gv
        v2 = ADAM_B2 * v_ref[...] + (1.0 - ADAM_B2) * (gv * gv)
        g_ref[...] = gv
        m2_ref[...] = m2
        v2_ref[...] = v2
        d_ref[...] = -ADAM_LR * ((m2 / c1) / (jnp.sqrt(v2 / c2) + ADAM_EPS) + ADAM_WD * w_ref[...])

    def piece(o):
        if transposed:
            return pl.BlockSpec((n, tk), lambda l, i, o=o: (o, i))
        return pl.BlockSpec((k, n), lambda l, i, o=o: (o, 0))

    blk = pl.BlockSpec((None, tk, n), lambda l, i: (l, i, 0))
    return tuple(pl.pallas_call(body, grid=(n_l, k // tk), in_specs=[piece(o) for o in offs] + [blk] * 3, out_specs=[blk] * 4,
                                out_shape=[S(w.shape, F32)] * 4, compiler_params=_cp("parallel", "parallel"),
                                name=name)(*arrs, w, m, v))


def adamw_native(name, g, w, m, v, tr=512):
    shape = w.shape
    cols = shape[-1]
    rows = w.size // cols
    tr = _tile(rows, tr) if rows % 8 == 0 else rows
    c1 = 1.0 - ADAM_B1 ** ADAM_STEP
    c2 = 1.0 - ADAM_B2 ** ADAM_STEP

    def body(g_ref, w_ref, m_ref, v_ref, d_ref, m2_ref, v2_ref):
        gv = g_ref[...]
        m2 = ADAM_B1 * m_ref[...] + (1.0 - ADAM_B1) * gv
        v2 = ADAM_B2 * v_ref[...] + (1.0 - ADAM_B2) * (gv * gv)
        m2_ref[...] = m2
        v2_ref[...] = v2
        d_ref[...] = -ADAM_LR * ((m2 / c1) / (jnp.sqrt(v2 / c2) + ADAM_EPS) + ADAM_WD * w_ref[...])

    row = pl.BlockSpec((tr, cols), lambda i: (i, 0))
    outs = pl.pallas_call(body, grid=(rows // tr,), in_specs=[row] * 4, out_specs=[row] * 3,
                          out_shape=[S((rows, cols), F32)] * 3, compiler_params=_cp("parallel"),
                          name=name)(*[a.reshape(rows, cols) for a in (g, w, m, v)])
    return tuple(o.reshape(shape) for o in outs)


_REPLICATED = ("e_norm", "e_gmlp_w", "e_gmlp_b", "e_conv_b", "e_conv_ln_g", "e_conv_ln_b", "o_lam_re", "o_lam_im", "o_log_dt",
               "o_b_re", "o_b_im", "o_c_re", "o_c_im", "ca_norm", "ca_mem_norm", "ffn_norm", "final_norm")
_ORDER = ("e_norm", "e_w_in", "e_gmlp_w", "e_gmlp_b", "e_conv_w", "e_conv_b", "e_conv_ln_g", "e_conv_ln_b", "e_w_out",
          "o_norm", "o_w_in", "o_lam_re", "o_lam_im", "o_log_dt", "o_b_re", "o_b_im", "o_c_re", "o_c_im", "o_d", "o_w_out",
          "ca_norm", "ca_mem_norm", "ca_wq", "ca_wk", "ca_wv", "ca_wo", "ffn_norm", "ffn_w_gate", "ffn_w_up", "ffn_w_down",
          "final_norm")


def _rows128(a, multiple=8):
    flat = a.reshape(-1)
    rows = -(-flat.shape[0] // (LANES * multiple)) * multiple
    return jnp.pad(flat, (0, rows * LANES - flat.shape[0])).reshape(rows, LANES)


def _shard(full, axis):
    s = full.shape
    return jnp.moveaxis(full.reshape(s[:axis] + (N_DEV, s[axis] // N_DEV) + s[axis + 1:]), axis, 0)


_UNITS = (("e_w_in", 0, True), ("e_w_out", 0, False), ("o_w_in", 0, False), ("o_w_out", 0, True),
          *[(n, i, False) for n in ("ca_wq", "ca_wk", "ca_wv", "ca_wo") for i in (0, 1)],
          *[(n, i, tr) for n, tr in (("ffn_w_gate", True), ("ffn_w_up", True), ("ffn_w_down", False)) for i in (0, 1)])
_LAYERED = ("ca_wq", "ca_wk", "ca_wv", "ca_wo", "ffn_w_gate", "ffn_w_up", "ffn_w_down")
_SMALL_SHARDED = (("e_conv_w", 2), ("o_norm", 1), ("o_d", 1))
RS_ROW = 1024


def _unit_key(name, tr):
    return name + "_t" if tr else name


def _stage_of(name, layer):
    if name.startswith("e_"):
        return 0 if name == "e_w_in" else 1
    if name.startswith("o_"):
        return 3
    if name.startswith("ca_"):
        return 1 if layer == 0 else 4
    return 2 if layer == 0 else 5


GATHER_STAGES = 6
GATHER_DIRECT = (False, False, False, True, True, False)


def weight_fetcher(local):
    groups, meta = [[] for _ in range(GATHER_STAGES)], [[] for _ in range(GATHER_STAGES)]
    for name, layer, tr in _UNITS:
        blk = local[name][layer]
        st = _stage_of(name, layer)
        groups[st].append(_bf(blk.T if tr else blk))
        meta[st].append((name, layer, tr))
    small = jnp.concatenate([local[name].reshape(-1) for name, _ in _SMALL_SHARDED])
    groups[0].append(_rows128(small))
    direct = list(GATHER_DIRECT)
    sems, srcs, lands, token = gather_ici_start("ag_w_start", groups, direct)

    early = {}

    def fetch(stage, after, start_only=False):
        if start_only:
            landed = gather_ici_wait(f"ag_w_wait{stage}", srcs[stage], lands[stage], sems[stage], after, direct[stage])
            early[stage] = gather_d2d_start(f"ag_w_d2d{stage}_start", landed)
            return {}
        if stage in early:
            bufs = gather_d2d_wait(f"ag_w_d2d{stage}_wait", early[stage][1], early[stage][0], after)
        else:
            bufs = gather_ici_wait(f"ag_w_wait{stage}", srcs[stage], lands[stage], sems[stage], after, direct[stage])
            if not direct[stage]:
                bufs = gather_d2d(f"ag_w_d2d{stage}", bufs)
        got = {}
        for (name, layer, tr), blk, buf in zip(meta[stage], groups[stage], bufs):
            arr = buf.reshape((N_DEV * blk.shape[0],) + tuple(blk.shape[1:]))
            if name in _LAYERED:
                got[(_unit_key(name, tr), layer)] = arr
            else:
                got[_unit_key(name, tr)] = arr
        if stage == 0:
            flat = bufs[-1].reshape(N_DEV, -1)
            off = 0
            for name, axis in _SMALL_SHARDED:
                blk = local[name]
                seg = flat[:, off:off + blk.size].reshape((N_DEV,) + blk.shape)
                off += blk.size
                seg = jnp.moveaxis(seg, 0, axis)
                got[name] = seg.reshape(seg.shape[:axis] + (-1,) + seg.shape[axis + 2:])
            got["e_conv_w"] = got["e_conv_w"][0]
        return got

    return fetch, token


def _grad_stage_of(name, layer):
    if name.startswith("e_"):
        return 4
    if name.startswith("o_"):
        return 1
    if name.startswith("ca_"):
        return 3 if layer == 0 else 1
    return 2 if layer == 0 else 0


GRAD_STAGES = 5
SMALL_ROWS = 16


def gradient_reducer(local, mom, var):
    me = (4 * lax.axis_index("x") + 2 * lax.axis_index("y") + lax.axis_index("c")).astype(jnp.int32).reshape(1)
    pending = []

    def start(stage, grads, carry):
        def grad_of(unit):
            key = _unit_key(unit[0], unit[2])
            return grads[(key, unit[1])] if unit[0] in _LAYERED else grads[key]

        units = sorted([u for u in _UNITS if _grad_stage_of(u[0], u[1]) == stage], key=lambda u: -grad_of(u).size)
        parts, spans = [], []
        for unit in units:
            g = grad_of(unit)
            part = g.reshape(N_DEV, -1, RS_ROW)
            spans.append((part.shape[1], g.shape[0] // N_DEV, g.shape[1]))
            parts.append(part)
        if stage == GRAD_STAGES - 1:
            small = jnp.concatenate([_shard(grads[name], axis).reshape(N_DEV, -1) for name, axis in _SMALL_SHARDED], axis=1)
            small = jnp.pad(small, ((0, 0), (0, SMALL_ROWS * RS_ROW - small.shape[1])))
            parts.append(small.astype(BF16).reshape(N_DEV, SMALL_ROWS, RS_ROW))
        sems, sent, land, carry = scatter_start(f"rs_start{stage}", parts, carry)
        pending.append((stage, units, spans, sems, sent, land))
        return carry

    def finish(after):
        res, per_layer, small_flat = {}, {}, None
        for stage, units, spans, sems, sent, land in pending:
            land = scatter_wait(f"rs_wait{stage}", sent, land, sems, after)
            total = sum_shares(f"rs_sum{stage}", land, me)
            off = 0
            for (name, layer, tr), (rows, r, c) in zip(units, spans):
                piece = (total, off) if c == RS_ROW else total[off:off + rows].reshape(r, c)
                per_layer.setdefault(name, {})[layer] = (piece, tr)
                off += rows
            if stage == GRAD_STAGES - 1:
                small_flat = total[off:off + SMALL_ROWS].reshape(-1)
        for name, by_layer in per_layer.items():
            pieces = [by_layer[i][0] for i in sorted(by_layer)]
            res[name] = adamw_units("adamw_" + name, pieces, by_layer[0][1], local[name], mom[name], var[name])
        off = 0
        for name, _ in _SMALL_SHARDED:
            blk = local[name]
            g = small_flat[off:off + blk.size].reshape(blk.shape)
            off += blk.size
            res[name] = (g,) + adamw_native("adamw_" + name, g, blk, mom[name], var[name])
        return res

    return start, finish


def replicated_start(grads, loss):
    pack = jnp.concatenate([_rows128(grads[name]) for name in _REPLICATED] + [_rows128(loss)], axis=0)
    sems, srcs, lands, token = gather_ici_start("ag_g_start", [[pack]], [False])
    return sems[0], srcs[0], lands[0], token


def replicated_finish(handle, after, w, mom, var):
    sems, srcs, lands, _ = handle
    (buf,) = gather_d2d("ag_g_d2d", gather_ici_wait("ag_g_wait", srcs, lands, sems, after))
    rows = srcs[0].shape[0]
    total = sum_slots("ag_g_sum", buf.reshape(N_DEV, rows, LANES))
    res, off = {}, 0
    for name in _REPLICATED:
        n = w[name].size
        nr = -(-n // (LANES * 8)) * 8
        g = total[off:off + nr].reshape(-1)[:n].reshape(w[name].shape)
        off += nr
        res[name] = (g,) + adamw_native("adamw_" + name, g, w[name], mom[name], var[name])
    return res, total[off, 0]


def kernel(x, mem, e_norm, e_w_in, e_gmlp_w, e_gmlp_b, e_conv_w, e_conv_b, e_conv_ln_g, e_conv_ln_b, e_w_out, o_norm, o_w_in, o_lam_re, o_lam_im, o_log_dt, o_b_re, o_b_im, o_c_re, o_c_im, o_d, o_w_out, ca_norm, ca_mem_norm, ca_wq, ca_wk, ca_wv, ca_wo, ffn_norm, ffn_w_gate, ffn_w_up, ffn_w_down, final_norm, loss_target, m_e_norm, m_e_w_in, m_e_gmlp_w, m_e_gmlp_b, m_e_conv_w, m_e_conv_b, m_e_conv_ln_g, m_e_conv_ln_b, m_e_w_out, m_o_norm, m_o_w_in, m_o_lam_re, m_o_lam_im, m_o_log_dt, m_o_b_re, m_o_b_im, m_o_c_re, m_o_c_im, m_o_d, m_o_w_out, m_ca_norm, m_ca_mem_norm, m_ca_wq, m_ca_wk, m_ca_wv, m_ca_wo, m_ffn_norm, m_ffn_w_gate, m_ffn_w_up, m_ffn_w_down, m_final_norm, v_e_norm, v_e_w_in, v_e_gmlp_w, v_e_gmlp_b, v_e_conv_w, v_e_conv_b, v_e_conv_ln_g, v_e_conv_ln_b, v_e_w_out, v_o_norm, v_o_w_in, v_o_lam_re, v_o_lam_im, v_o_log_dt, v_o_b_re, v_o_b_im, v_o_c_re, v_o_c_im, v_o_d, v_o_w_out, v_ca_norm, v_ca_mem_norm, v_ca_wq, v_ca_wk, v_ca_wv, v_ca_wo, v_ffn_norm, v_ffn_w_gate, v_ffn_w_up, v_ffn_w_down, v_final_norm):
    given = dict(locals())
    local = {k: given[k] for k in _ORDER}
    mom = {k: given["m_" + k] for k in _ORDER}
    var = {k: given["v_" + k] for k in _ORDER}

    w = {}
    w.update({
        "e_norm": e_norm, "e_gmlp_w": e_gmlp_w[0], "e_gmlp_b": e_gmlp_b.reshape(A_GROUPS, GMLP_BLOCK, 1),
        "e_conv_b": e_conv_b, "e_conv_ln_g": e_conv_ln_g, "e_conv_ln_b": e_conv_ln_b,
        "o_lam_re": o_lam_re[0], "o_lam_im": o_lam_im[0], "o_log_dt": o_log_dt[0], "o_b_re": o_b_re[0], "o_b_im": o_b_im[0],
        "o_c_re": o_c_re[0], "o_c_im": o_c_im[0], "ca_norm": ca_norm, "ca_mem_norm": ca_mem_norm, "ffn_norm": ffn_norm,
        "final_norm": final_norm.reshape(1, D_MODEL),
    })
    start_reduce, finish_reduce = gradient_reducer(local, mom, var)
    fetch, token = weight_fetcher(local)
    loss_part, grad_x, grads = local_step(x[0], mem[0], loss_target[0], w, fetch, start_reduce, token[0:1, 0:1])
    grads["final_norm"] = grads["final_norm"].reshape(D_MODEL)

    handle = replicated_start(grads, loss_part)
    res = finish_reduce(handle[3])
    rep, loss = replicated_finish(handle, res["ffn_w_down"][1], local, mom, var)
    res.update(rep)
    return (loss, grad_x[None], *[res[k][0] for k in _ORDER], *[res[k][1] for k in _ORDER],
            *[res[k][2] for k in _ORDER], *[res[k][3] for k in _ORDER])
```

```python
import jax
import jax.numpy as jnp
from jax import lax
from jax.experimental import pallas as pl
from jax.experimental.pallas import tpu as pltpu

F32 = jnp.float32
BF16 = jnp.bfloat16
S = jax.ShapeDtypeStruct

D_MODEL = 1024
A_WIDTH = 512
A_GROUPS = 4
GMLP_BLOCK = 128
CHUNK = 64
B_WIDTH = 512
IN_WIDTH = 2 * A_WIDTH + 2 * B_WIDTH
CONV_WIDTH = 31
CONV_PAD = 32
C_WIDTH = 512
C_GROUP_CH = 16
C_GROUPS = 32
C_STATE = 64
N_STATE = C_GROUPS * C_STATE
CA_HEADS = 4
CA_HEAD_DIM = 256
EPS = 1e-6
ADAM_LR = 0.001
ADAM_B1 = 0.9
ADAM_B2 = 0.999
ADAM_EPS = 1e-08
ADAM_WD = 0.01
ADAM_STEP = 10
N_DEV = 8
LANES = 128
VMEM_LIMIT = 56 << 20
VMEM_BUDGET = 40 << 20
MM_TN_RESIDENT = 8 << 20
MESH = pl.DeviceIdType.MESH
ANY = pl.BlockSpec(memory_space=pl.ANY)


def _cp(*sem):
    return pltpu.CompilerParams(dimension_semantics=sem, vmem_limit_bytes=VMEM_LIMIT)


def _tile(n, pref):
    t = pref
    while n % t:
        t //= 2
    return t


def _bf(v):
    return v if v.dtype == BF16 else v.astype(BF16)


def _sigmoid(x):
    return 1.0 / (1.0 + jnp.exp(-x))


_GC = 0.7978845608028654


def _gelu(x):
    return 0.5 * x * (1.0 + jnp.tanh(_GC * (x + 0.044715 * x * x * x)))


def _gelu_grad(x):
    x2 = x * x
    t = jnp.tanh(_GC * (x + 0.044715 * x * x2))
    return 0.5 * (1.0 + t) + 0.5 * x * (1.0 - t * t) * _GC * (1.0 + 3.0 * 0.044715 * x2)


def _tspec(entry, tm):
    if isinstance(entry, tuple):
        arr, cb, width = entry
        return arr, pl.BlockSpec((tm, width), lambda i, cb=cb: (i, cb))
    return entry, pl.BlockSpec((tm, entry.shape[1]), lambda i: (i, 0))


def rows_call(name, fn, tiled, full, outs, accs, tm=256):
    pairs = [_tspec(e, tm) for e in tiled]
    arrs = [p[0] for p in pairs]
    rows = arrs[0].shape[0]
    tm = _tile(rows, tm)
    pairs = [_tspec(e, tm) for e in tiled]
    n_in = len(tiled) + len(full)
    n_out = len(outs)

    def body(*refs):
        vals = [r[...] for r in refs[:n_in]]
        o_refs = refs[n_in:n_in + n_out]
        a_refs = refs[n_in + n_out:]
        ov, av = fn(*vals)
        for r, v in zip(o_refs, ov):
            r[...] = v.astype(r.dtype)
        if a_refs:
            @pl.when(pl.program_id(0) == 0)
            def _():
                for r in a_refs:
                    r[...] = jnp.zeros(r.shape, r.dtype)
            for r, v in zip(a_refs, av):
                r[...] += v

    in_specs = [p[1] for p in pairs] + [pl.BlockSpec(a.shape, lambda i, nd=a.ndim: (0,) * nd) for a in full]
    out_specs = [pl.BlockSpec((tm, c), lambda i: (i, 0)) for c, _ in outs]
    out_specs += [pl.BlockSpec(s, lambda i, nd=len(s): (0,) * nd) for s in accs]
    out_shape = [S((rows, c), dt) for c, dt in outs] + [S(s, F32) for s in accs]
    return pl.pallas_call(body, grid=(rows // tm,), in_specs=in_specs, out_specs=out_specs, out_shape=out_shape,
                          compiler_params=_cp("arbitrary"), name=name)(*arrs, *full)


def mm_nn(name, m, n, pairs, n_acc, epi, outs, tiled=(), cols=(), rowv=(), sums=(), norm_gain=None):
    a_ops, a_slot, b_arrs, b_specs, idx, trans = [], [], [], [], [], []
    fixed = 0
    for pair in pairs:
        a, b, k = pair[:3]
        bt = len(pair) > 3
        arr, cb, kdim = a if isinstance(a, tuple) else (a, 0, a.shape[1])
        key = (id(arr), cb, kdim)
        if key not in [o[0] for o in a_ops]:
            a_ops.append((key, arr, cb, kdim))
        a_slot.append([o[0] for o in a_ops].index(key))
        b_arr, off = b if isinstance(b, tuple) else (b, 0)
        b_arrs.append(b_arr)
        if bt:
            assert off % n == 0 and b_arr.shape[1] == kdim
            b_specs.append(pl.BlockSpec((n, kdim), lambda i, o=off // n: (o, 0), pipeline_mode=pl.Buffered(1)))
        else:
            assert b_arr.shape[1] == n
            b_specs.append(pl.BlockSpec((kdim, n), lambda i, o=off: (o, 0), pipeline_mode=pl.Buffered(1)))
        fixed += kdim * n * b_arr.dtype.itemsize
        idx.append(k)
        trans.append(bt)
    per_row = sum(2 * kdim * arr.dtype.itemsize for _, arr, _, kdim in a_ops)
    per_row += sum(2 * n * t.dtype.itemsize for t in tiled) + sum(2 * n * jnp.dtype(dt).itemsize for dt in outs)
    cn = n if sums or cols else (512 if n % 512 == 0 else 256)
    per_row += (n_acc + 3) * cn * 4
    tm = next((t for t in (1024, 512, 256, 128) if m % t == 0 and fixed + t * per_row <= VMEM_BUDGET), _tile(m, 128))
    n_a, n_p, n_t = len(a_ops), len(pairs), len(tiled)
    n_in = n_a + n_p + n_t + len(cols) + len(rowv)
    normed = norm_gain is not None
    o0 = n_in + normed

    def body(*refs):
        a_vals = [None if normed and i == 0 else _bf(r[...]) for i, r in enumerate(refs[:n_a])]
        if normed:
            xv = refs[0][...]
            rv = lax.rsqrt(jnp.mean(xv * xv, axis=-1, keepdims=True) + EPS)
            a_vals[0] = (xv * rv * refs[n_in][...]).astype(BF16)
            refs[o0 + len(outs)][...] = a_vals[0]
            refs[o0 + len(outs) + 1][...] = rv
        for j in range(n // cn):
            cs = slice(j * cn, (j + 1) * cn)
            accs = [None] * n_acc
            for p in range(n_p):
                av, b_ref = a_vals[a_slot[p]], refs[n_a + p]
                if trans[p]:
                    d = lax.dot_general(av, _bf(b_ref[cs, :]), (((1,), (1,)), ((), ())), preferred_element_type=F32)
                else:
                    d = jnp.dot(av, _bf(b_ref[:, cs]), preferred_element_type=F32)
                accs[idx[p]] = d if accs[idx[p]] is None else accs[idx[p]] + d
            extra = [r[:, cs] for r in refs[n_a + n_p:n_a + n_p + n_t]] + [r[...] for r in refs[n_a + n_p + n_t:n_in - len(rowv)]]
            extra += [r[:, cs] for r in refs[n_in - len(rowv):n_in]]
            ov = epi(accs, *extra)
            for r, v in zip(refs[o0:o0 + len(outs)], ov):
                r[:, cs] = v.astype(r.dtype)
        sv = ov[len(outs):]
        if sums:
            s_refs = refs[o0 + len(outs) + 2 * normed:]

            @pl.when(pl.program_id(0) == 0)
            def _():
                for r in s_refs:
                    r[...] = jnp.zeros(r.shape, r.dtype)
            for r, v in zip(s_refs, sv):
                r[...] += v

    in_specs = [pl.BlockSpec((tm, kdim), lambda i, cb=cb: (i, cb)) for _, _, cb, kdim in a_ops] + b_specs
    in_specs += [pl.BlockSpec((tm, n), lambda i: (i, 0)) for _ in tiled]
    in_specs += [pl.BlockSpec((tm, 1), lambda i: (i, 0)) for _ in cols]
    in_specs += [pl.BlockSpec((1, n), lambda i: (0, 0)) for _ in rowv]
    out_specs = [pl.BlockSpec((tm, n), lambda i: (i, 0)) for _ in outs]
    out_shape = [S((m, n), dt) for dt in outs]
    gain = []
    if normed:
        k0 = a_ops[0][3]
        gain = [norm_gain]
        in_specs.append(pl.BlockSpec((1, k0), lambda i: (0, 0)))
        out_specs += [pl.BlockSpec((tm, k0), lambda i: (i, 0)), pl.BlockSpec((tm, 1), lambda i: (i, 0))]
        out_shape += [S((m, k0), BF16), S((m, 1), F32)]
    out_specs += [pl.BlockSpec(s, lambda i, nd=len(s): (0,) * nd) for s in sums]
    out_shape += [S(s, F32) for s in sums]
    return pl.pallas_call(body, grid=(m // tm,), in_specs=in_specs, out_specs=out_specs, out_shape=out_shape,
                          compiler_params=_cp("arbitrary" if sums else "parallel"),
                          name=name)(*[o[1] for o in a_ops], *b_arrs, *tiled, *cols, *rowv, *gain)


def mm_tn(name, a, b, out_dtype=BF16):
    if isinstance(a, tuple):
        a_arr, a_cb, m = a
    else:
        a_arr, a_cb, m = a, None, a.shape[1]
    if isinstance(b, tuple):
        b_arr, b_cb, n = b
    else:
        b_arr, b_cb, n = b, None, b.shape[1]
    t = a_arr.shape[0]
    whole_b = t * n * b_arr.dtype.itemsize <= MM_TN_RESIDENT and b_cb is None
    tn = n if whole_b else _tile(n, 512)
    tm = _tile(m, 512 if t * 512 * a_arr.dtype.itemsize * 2 + t * tn * b_arr.dtype.itemsize * 2 <= VMEM_BUDGET else 256)
    a_off = 0 if a_cb is None else a_cb * (m // tm)
    b_off = 0 if b_cb is None else b_cb * (n // tn)

    def body(a_ref, b_ref, o_ref):
        o_ref[...] = lax.dot_general(_bf(a_ref[...]), _bf(b_ref[...]), (((0,), (0,)), ((), ())),
                                     preferred_element_type=F32).astype(o_ref.dtype)

    if whole_b:
        b_spec = pl.BlockSpec((t, n), lambda i, j: (0, 0), pipeline_mode=pl.Buffered(1))
    else:
        b_spec = pl.BlockSpec((t, tn), lambda i, j: (0, j + b_off))
    return pl.pallas_call(
        body, grid=(m // tm, n // tn),
        in_specs=[pl.BlockSpec((t, tm), lambda i, j: (0, i + a_off)), b_spec],
        out_specs=pl.BlockSpec((tm, tn), lambda i, j: (i, j)), out_shape=S((m, n), out_dtype),
        compiler_params=_cp("parallel", "parallel"), name=name)(a_arr, b_arr)


def rms_bwd_gain_only(name, dxn, x, r):
    def fn(dv, xv, rv):
        return [], [jnp.sum(dv * xv * rv, axis=0, keepdims=True)]
    return rows_call(name, fn, [dxn, x, r], [], [], [(1, x.shape[1])])[0]


def _final_loss_epi(accs, res, tv, g):
    xv = res + accs[0]
    d = xv.shape[-1]
    r = lax.rsqrt(jnp.mean(xv * xv, axis=-1, keepdims=True) + EPS)
    xh = xv * r
    err = xh * g - tv
    dy = err * (1.0 / d)
    w = dy * g
    dx = r * (w - xh * jnp.mean(w * xh, axis=-1, keepdims=True))
    part = jnp.sum(jnp.sum(err * err, axis=-1, keepdims=True), axis=0, keepdims=True) * (0.5 / d)
    return [dx, dx, jnp.sum(dy * xh, axis=0, keepdims=True), part]


def _gmlp_mask():
    row = lax.broadcasted_iota(jnp.int32, (GMLP_BLOCK, GMLP_BLOCK), 0) // CHUNK
    col = lax.broadcasted_iota(jnp.int32, (GMLP_BLOCK, GMLP_BLOCK), 1) // CHUNK
    return col <= row


def _ln_plain(v):
    mu = jnp.mean(v, axis=-1, keepdims=True)
    vc = v - mu
    rstd = lax.rsqrt(jnp.mean(vc * vc, axis=-1, keepdims=True) + EPS)
    return vc * rstd, rstd


def even_out_fwd(name, proj, hc, x, w, b, ln_g, ln_b, w_out, tm=512):
    t, d = x.shape
    tm = _tile(t, tm)

    def body(au_ref, av_ref, hc_ref, x_ref, w_ref, b_ref, lg_ref, lb_ref, wo_ref, x1_ref, oa_ref, ob_ref):
        mask = _gmlp_mask()
        u = _gelu(au_ref[...])
        vn, _ = _ln_plain(_gelu(av_ref[...]))
        vnb = _bf(vn)
        for g in range(A_GROUPS):
            wg = _bf(jnp.where(mask, w_ref[g], 0.0))
            cs = slice(g * GMLP_BLOCK, (g + 1) * GMLP_BLOCK)
            for n in range(tm // GMLP_BLOCK):
                rs = slice(n * GMLP_BLOCK, (n + 1) * GMLP_BLOCK)
                sg = jnp.dot(wg, vnb[rs, cs], preferred_element_type=F32) + b_ref[g]
                oa_ref[rs, cs] = (u[rs, cs] * sg).astype(oa_ref.dtype)
        y, _ = _ln_plain(hc_ref[...])
        z = y * lg_ref[...] + lb_ref[...]
        ob_ref[...] = (z * _sigmoid(z)).astype(ob_ref.dtype)
        x1_ref[...] = (x_ref[...] + jnp.dot(oa_ref[...], wo_ref[0:A_WIDTH, :], preferred_element_type=F32)
                       + jnp.dot(ob_ref[...], wo_ref[A_WIDTH:, :], preferred_element_type=F32))

    half = pl.BlockSpec((tm, A_WIDTH), lambda i: (i, 0))
    return pl.pallas_call(
        body, grid=(t // tm,),
        in_specs=[half, pl.BlockSpec((tm, A_WIDTH), lambda i: (i, 1)), half, pl.BlockSpec((tm, d), lambda i: (i, 0)),
                  _whole(w), _whole(b), _whole(ln_g), _whole(ln_b), _whole(w_out)],
        out_specs=[pl.BlockSpec((tm, d), lambda i: (i, 0)), half, half],
        out_shape=[S((t, d), F32), S((t, A_WIDTH), BF16), S((t, B_WIDTH), BF16)],
        compiler_params=_cp("parallel"), name=name)(proj, proj, hc, x, w, b, ln_g, ln_b, w_out)


def gmlp_bwd(name, proj, dxb, w_out, w, b, tm=512):
    t = proj.shape[0]
    tm = _tile(t, tm)

    def body(au_ref, av_ref, dx_ref, wo_ref, w_ref, b_ref, dp_ref, dw_ref, db_ref):
        @pl.when(pl.program_id(0) == 0)
        def _():
            dw_ref[...] = jnp.zeros(dw_ref.shape, F32)
            db_ref[...] = jnp.zeros(db_ref.shape, F32)

        mask = _gmlp_mask()
        au = au_ref[...]
        av = av_ref[...]
        u = _gelu(au)
        vn, rstd = _ln_plain(_gelu(av))
        vnb = _bf(vn)
        dout = lax.dot_general(dx_ref[...], wo_ref[0:A_WIDTH, :], _NT, preferred_element_type=F32)
        dvn_cols = []
        for g in range(A_GROUPS):
            wm = jnp.where(mask, w_ref[g], 0.0)
            wg = _bf(wm)
            wgt = _bf(wm.T)
            cs = slice(g * GMLP_BLOCK, (g + 1) * GMLP_BLOCK)
            dwg = jnp.zeros((GMLP_BLOCK, GMLP_BLOCK), F32)
            dbg = jnp.zeros((GMLP_BLOCK, 1), F32)
            dvn_rows = []
            for n in range(tm // GMLP_BLOCK):
                rs = slice(n * GMLP_BLOCK, (n + 1) * GMLP_BLOCK)
                sg = jnp.dot(wg, vnb[rs, cs], preferred_element_type=F32) + b_ref[g]
                dp_ref[rs, cs] = (dout[rs, cs] * sg * _gelu_grad(au[rs, cs])).astype(dp_ref.dtype)
                dsg = dout[rs, cs] * u[rs, cs]
                dsgb = _bf(dsg)
                dbg = dbg + jnp.sum(dsg, axis=1, keepdims=True)
                dwg = dwg + lax.dot_general(dsgb, vnb[rs, cs], (((1,), (1,)), ((), ())), preferred_element_type=F32)
                dvn_rows.append(jnp.dot(wgt, dsgb, preferred_element_type=F32))
            dw_ref[g] += jnp.where(mask, dwg, 0.0)
            db_ref[g] += dbg
            dvn_cols.append(jnp.concatenate(dvn_rows, axis=0))
        dvn = jnp.concatenate(dvn_cols, axis=1)
        dv = rstd * (dvn - jnp.mean(dvn, axis=-1, keepdims=True) - vn * jnp.mean(dvn * vn, axis=-1, keepdims=True))
        dp_ref[:, A_WIDTH:] = (dv * _gelu_grad(av)).astype(dp_ref.dtype)

    return pl.pallas_call(
        body, grid=(t // tm,),
        in_specs=[pl.BlockSpec((tm, A_WIDTH), lambda i: (i, 0)), pl.BlockSpec((tm, A_WIDTH), lambda i: (i, 1)),
                  pl.BlockSpec((tm, dxb.shape[1]), lambda i: (i, 0)), pl.BlockSpec(w_out.shape, lambda i: (0, 0)),
                  pl.BlockSpec(w.shape, lambda i: (0, 0, 0)), pl.BlockSpec(b.shape, lambda i: (0, 0, 0))],
        out_specs=[pl.BlockSpec((tm, 2 * A_WIDTH), lambda i: (i, 0)),
                   pl.BlockSpec(w.shape, lambda i: (0, 0, 0)), pl.BlockSpec(b.shape, lambda i: (0, 0, 0))],
        out_shape=[S((t, 2 * A_WIDTH), BF16), S(w.shape, F32), S(b.shape, F32)],
        compiler_params=_cp("arbitrary"), name=name)(proj, proj, dxb, w_out, w, b)


CONV_ROWS = 256
CONV_ROWS_BWD = 64


def conv_fwd(name, proj, w, cb):
    t = proj.shape[0]
    tc = LANES
    rows = _tile(t, CONV_ROWS)
    a_cb, g_cb = 2 * A_WIDTH // tc, (2 * A_WIDTH + B_WIDTH) // tc

    def body(a_ref, g_ref, w_ref, cb_ref, o_ref, hpad):
        hpad[0:CONV_PAD, :] = jnp.zeros((CONV_PAD, tc), F32)

        def fill(i, _):
            r0 = pl.multiple_of(i * rows, rows)
            hpad[pl.ds(CONV_PAD + r0, rows), :] = a_ref[pl.ds(r0, rows), :] * _sigmoid(g_ref[pl.ds(r0, rows), :])
            return 0
        lax.fori_loop(0, t // rows, fill, 0)

        def conv(i, _):
            r0 = pl.multiple_of(i * rows, rows)
            win = hpad[pl.ds(r0, rows + CONV_PAD), :]
            acc = jnp.zeros((rows, tc), F32) + cb_ref[...]
            for b in range(SUB):
                wb = win if b == 0 else pltpu.roll(win, b, 0)
                for a in range(CONV_PAD // SUB):
                    k = CONV_WIDTH - 1 - (SUB * a + b)
                    if k >= 0:
                        lo = CONV_PAD - SUB * a
                        acc = acc + wb[lo:lo + rows, :] * w_ref[k:k + 1, :]
            o_ref[pl.ds(r0, rows), :] = acc
            return 0
        lax.fori_loop(0, t // rows, conv, 0)

    return pl.pallas_call(
        body, grid=(B_WIDTH // tc,),
        in_specs=[pl.BlockSpec((t, tc), lambda j: (0, a_cb + j)), pl.BlockSpec((t, tc), lambda j: (0, g_cb + j)),
                  pl.BlockSpec((CONV_WIDTH, tc), lambda j: (0, j)), pl.BlockSpec((1, tc), lambda j: (0, j))],
        out_specs=pl.BlockSpec((t, tc), lambda j: (0, j)), out_shape=S((t, B_WIDTH), F32),
        scratch_shapes=[pltpu.VMEM((t + CONV_PAD, tc), F32)],
        compiler_params=_cp("parallel"), name=name)(proj, proj, w, cb)


def conv_bwd(name, proj, dhc, w):
    t = proj.shape[0]
    tc = LANES
    rows = _tile(t, CONV_ROWS_BWD)
    a_cb, g_cb = 2 * A_WIDTH // tc, (2 * A_WIDTH + B_WIDTH) // tc
    win_rows = rows + CONV_PAD

    def body(a_ref, g_ref, d_ref, w_ref, da_ref, dg_ref, dw_ref, dcb_ref, hpad, dpad, dwacc):
        hpad[0:CONV_PAD, :] = jnp.zeros((CONV_PAD, tc), F32)
        dpad[t:t + CONV_PAD, :] = jnp.zeros((CONV_PAD, tc), F32)
        dwacc[...] = jnp.zeros(dwacc.shape, F32)

        def fill(i, _):
            r0 = pl.multiple_of(i * rows, rows)
            hpad[pl.ds(CONV_PAD + r0, rows), :] = a_ref[pl.ds(r0, rows), :] * _sigmoid(g_ref[pl.ds(r0, rows), :])
            dpad[pl.ds(r0, rows), :] = d_ref[pl.ds(r0, rows), :]
            return 0
        lax.fori_loop(0, t // rows, fill, 0)

        def step(i, dcb):
            r0 = pl.multiple_of(i * rows, rows)
            hwin = hpad[pl.ds(r0, win_rows), :]
            dwin = dpad[pl.ds(r0, win_rows), :]
            dchunk = dwin[:rows, :]
            dh = jnp.zeros((rows, tc), F32)
            for b in range(SUB):
                hb = hwin if b == 0 else pltpu.roll(hwin, b, 0)
                db = dwin if b == 0 else pltpu.roll(dwin, win_rows - b, 0)
                for a in range(CONV_PAD // SUB):
                    k = CONV_WIDTH - 1 - (SUB * a + b)
                    if k >= 0:
                        dh = dh + db[SUB * a:SUB * a + rows, :] * w_ref[k:k + 1, :]
                        lo = CONV_PAD - SUB * a
                        prod = dchunk * hb[lo:lo + rows, :]
                        dwacc[k] += jnp.sum(prod.reshape(rows // 8, 8, tc), axis=0)
            a = a_ref[pl.ds(r0, rows), :]
            sg = _sigmoid(g_ref[pl.ds(r0, rows), :])
            da_ref[pl.ds(r0, rows), :] = (dh * sg).astype(da_ref.dtype)
            dg_ref[pl.ds(r0, rows), :] = (dh * a * sg * (1.0 - sg)).astype(dg_ref.dtype)
            return dcb + jnp.sum(dchunk, axis=0, keepdims=True)
        dcb = lax.fori_loop(0, t // rows, step, jnp.zeros((1, tc), F32))
        dcb_ref[...] = dcb
        for k in range(CONV_WIDTH):
            dw_ref[k:k + 1, :] = jnp.sum(dwacc[k], axis=0, keepdims=True)

    return pl.pallas_call(
        body, grid=(B_WIDTH // tc,),
        in_specs=[pl.BlockSpec((t, tc), lambda j: (0, a_cb + j)), pl.BlockSpec((t, tc), lambda j: (0, g_cb + j)),
                  pl.BlockSpec((t, tc), lambda j: (0, j)), pl.BlockSpec((CONV_WIDTH, tc), lambda j: (0, j))],
        out_specs=[pl.BlockSpec((t, tc), lambda j: (0, j)), pl.BlockSpec((t, tc), lambda j: (0, j)),
                   pl.BlockSpec((CONV_WIDTH, tc), lambda j: (0, j)), pl.BlockSpec((1, tc), lambda j: (0, j))],
        out_shape=[S((t, B_WIDTH), BF16), S((t, B_WIDTH), BF16), S((CONV_WIDTH, B_WIDTH), F32), S((1, B_WIDTH), F32)],
        scratch_shapes=[pltpu.VMEM((t + CONV_PAD, tc), F32), pltpu.VMEM((t + CONV_PAD, tc), F32),
                        pltpu.VMEM((CONV_WIDTH, 8, tc), F32)],
        compiler_params=_cp("parallel"), name=name)(proj, proj, dhc, w)


def ln_silu_bwd(name, hc, dxb, w_out, g, b):
    c = hc.shape[1]

    def fn(h, dxv, wv, gv, bv):
        dout = lax.dot_general(dxv, wv[A_WIDTH:, :], _NT, preferred_element_type=F32)
        y, rstd = _ln_plain(h)
        z = y * gv + bv
        s = _sigmoid(z)
        dz = dout * s * (1.0 + z * (1.0 - s))
        dyv = dz * gv
        dh = rstd * (dyv - jnp.mean(dyv, axis=-1, keepdims=True) - y * jnp.mean(dyv * y, axis=-1, keepdims=True))
        return [dh], [jnp.sum(dz * y, axis=0, keepdims=True), jnp.sum(dz, axis=0, keepdims=True)]

    return rows_call(name, fn, [hc, dxb], [w_out, g, b], [(c, F32)], [(1, c), (1, c)])


_NT = (((1,), (1,)), ((), ()))
_TN = (((0,), (0,)), ((), ()))


def attn_fwd(name, x, gain, wq, k, v, wo, tm=512):
    t, d = x.shape
    m = k.shape[0]
    tm = _tile(t, tm)
    scale = CA_HEAD_DIM ** -0.5

    def body(x_ref, g_ref, wq_ref, k_ref, v_ref, wo_ref, x1_ref, xn_ref, r_ref, q_ref, o_ref):
        xv = x_ref[...]
        rv = lax.rsqrt(jnp.mean(xv * xv, axis=-1, keepdims=True) + EPS)
        xn = (xv * rv * g_ref[...]).astype(BF16)
        xn_ref[...] = xn
        r_ref[...] = rv
        q_ref[...] = jnp.dot(xn, wq_ref[...], preferred_element_type=F32).astype(BF16)
        for h in range(CA_HEADS):
            cs = slice(h * CA_HEAD_DIM, (h + 1) * CA_HEAD_DIM)
            s = lax.dot_general(q_ref[:, cs], k_ref[:, cs], _NT, preferred_element_type=F32) * scale
            e = jnp.exp(s - jnp.max(s, axis=-1, keepdims=True))
            p = e / jnp.sum(e, axis=-1, keepdims=True)
            o_ref[:, cs] = jnp.dot(_bf(p), v_ref[:, cs], preferred_element_type=F32).astype(o_ref.dtype)
        x1_ref[...] = xv + jnp.dot(o_ref[...], wo_ref[...], preferred_element_type=F32)

    def whole(a):
        return pl.BlockSpec(a.shape, lambda i: (0, 0), pipeline_mode=pl.Buffered(1))

    rows = pl.BlockSpec((tm, d), lambda i: (i, 0))
    col = pl.BlockSpec((tm, 1), lambda i: (i, 0))
    return pl.pallas_call(
        body, grid=(t // tm,),
        in_specs=[rows, whole(gain), whole(wq), whole(k), whole(v), whole(wo)],
        out_specs=[rows, rows, col, rows, rows],
        out_shape=[S((t, d), F32), S((t, d), BF16), S((t, 1), F32), S((t, d), BF16), S((t, d), BF16)],
        compiler_params=_cp("parallel"), name=name)(x, gain, wq, k, v, wo)


def attn_bwd(name, dx, dxb, x, r, gain, q, k, v, wq, wo, tm=512):
    t, d = q.shape
    m = k.shape[0]
    tm = _tile(t, tm)
    scale = CA_HEAD_DIM ** -0.5

    def body(dx_ref, dxb_ref, x_ref, r_ref, g_ref, q_ref, k_ref, v_ref, wq_ref, wo_ref,
             dxo_ref, dxbo_ref, dq_ref, dk_ref, dv_ref, dg_ref, do_s):
        @pl.when(pl.program_id(0) == 0)
        def _():
            dk_ref[...] = jnp.zeros(dk_ref.shape, F32)
            dv_ref[...] = jnp.zeros(dv_ref.shape, F32)
            dg_ref[...] = jnp.zeros(dg_ref.shape, F32)

        do_s[...] = lax.dot_general(dxb_ref[...], wo_ref[...], _NT, preferred_element_type=F32).astype(BF16)
        for h in range(CA_HEADS):
            cs = slice(h * CA_HEAD_DIM, (h + 1) * CA_HEAD_DIM)
            qh, kh, vh, doh = q_ref[:, cs], k_ref[:, cs], v_ref[:, cs], do_s[:, cs]
            s = lax.dot_general(qh, kh, _NT, preferred_element_type=F32) * scale
            e = jnp.exp(s - jnp.max(s, axis=-1, keepdims=True))
            p = e / jnp.sum(e, axis=-1, keepdims=True)
            pb = _bf(p)
            dv_ref[:, cs] += lax.dot_general(pb, doh, _TN, preferred_element_type=F32)
            dp = lax.dot_general(doh, vh, _NT, preferred_element_type=F32)
            ds = _bf(p * (dp - jnp.sum(dp * p, axis=-1, keepdims=True)) * scale)
            dq_ref[:, cs] = jnp.dot(ds, kh, preferred_element_type=F32).astype(dq_ref.dtype)
            dk_ref[:, cs] += lax.dot_general(ds, qh, _TN, preferred_element_type=F32)
        dxn = lax.dot_general(dq_ref[...], wq_ref[...], _NT, preferred_element_type=F32)
        xh = x_ref[...] * r_ref[...]
        wv = dxn * g_ref[...]
        dxo = dx_ref[...] + r_ref[...] * (wv - xh * jnp.mean(wv * xh, axis=-1, keepdims=True))
        dxo_ref[...] = dxo
        dxbo_ref[...] = dxo.astype(BF16)
        dg_ref[...] += jnp.sum(dxn * xh, axis=0, keepdims=True)

    def whole(a):
        return pl.BlockSpec(a.shape, lambda i: (0, 0), pipeline_mode=pl.Buffered(1))

    rows = pl.BlockSpec((tm, d), lambda i: (i, 0))
    col = pl.BlockSpec((tm, 1), lambda i: (i, 0))
    acc = pl.BlockSpec((m, d), lambda i: (0, 0))
    return pl.pallas_call(
        body, grid=(t // tm,),
        in_specs=[rows, rows, rows, col, whole(gain), rows, whole(k), whole(v), whole(wq), whole(wo)],
        out_specs=[rows, rows, rows, acc, acc, pl.BlockSpec((1, d), lambda i: (0, 0))],
        out_shape=[S((t, d), F32), S((t, d), BF16), S((t, d), BF16), S((m, d), F32), S((m, d), F32), S((1, d), F32)],
        scratch_shapes=[pltpu.VMEM((tm, d), BF16)],
        compiler_params=_cp("arbitrary"), name=name)(dx, dxb, x, r, gain, q, k, v, wq, wo)


SUB = 8
S5_ROWS = 256


S5_BLOCKS = 4
BLOCK_CH = C_WIDTH // S5_BLOCKS
BLOCK_ST = N_STATE // S5_BLOCKS
_S5_BLOCKS = tuple((slice(BLOCK_CH * q, BLOCK_CH * (q + 1)), slice(BLOCK_ST * q, BLOCK_ST * (q + 1)),
                    slice(N_STATE + BLOCK_ST * q, N_STATE + BLOCK_ST * (q + 1))) for q in range(S5_BLOCKS))
_HI = lax.Precision.HIGHEST
_GP = (C_GROUPS, C_STATE)
_RP = (C_WIDTH, C_STATE)


def _zoh(lr, li, ldt):
    dt = jnp.exp(ldt)
    mag = jnp.exp(lr * dt)
    ar = mag * jnp.cos(li * dt)
    ai = mag * jnp.sin(li * dt)
    den = lr * lr + li * li
    qr = ((ar - 1.0) * lr + ai * li) / den
    qi = (ai * lr - (ar - 1.0) * li) / den
    return dt, ar, ai, den, qr, qi


def _per_channel(v):
    return jnp.broadcast_to(v[:, None, :], (C_GROUPS, C_GROUP_CH, C_STATE)).reshape(_RP)


def _same_group(shape, row_per_group, col_per_group):
    rows = lax.broadcasted_iota(jnp.int32, shape, 0) // row_per_group
    cols = lax.broadcasted_iota(jnp.int32, shape, 1) // col_per_group
    return rows == cols


def _spread(shape, axis):
    long = lax.broadcasted_iota(jnp.int32, shape, axis) % C_STATE
    short = lax.broadcasted_iota(jnp.int32, shape, 1 - axis)
    return long == short


def s5_discretise(name, lam_re, lam_im, log_dt, bt_re, bt_im):
    def body(lr_ref, li_ref, ldt_ref, btr_ref, bti_ref, a_ref, bbr_ref, bbi_ref):
        _, ar, ai, _, qr, qi = _zoh(lr_ref[...], li_ref[...], ldt_ref[...])
        a_ref[0] = ar
        a_ref[1] = ai
        q2r, q2i = _per_channel(qr), _per_channel(qi)
        btr, bti = btr_ref[...], bti_ref[...]
        bbr_ref[...] = q2r * btr - q2i * bti
        bbi_ref[...] = q2r * bti + q2i * btr

    return pl.pallas_call(body, out_shape=[S((2,) + _GP, F32), S(_RP, F32), S(_RP, F32)],
                          name=name)(lam_re, lam_im, log_dt, bt_re, bt_im)


def s5_operands(name, a, bbr, bbi, c2r, c2i, ctr, cti):
    ns = N_STATE

    def body(a_ref, bbr_ref, bbi_ref, c2r_ref, c2i_ref, ctr_ref, cti_ref, pw_ref, qw_ref, mb_ref, mc_ref, mct_ref):
        ar, ai = a_ref[0:1, :], a_ref[1:2, :]
        pows = [(ar, ai)]
        for _ in range(SUB - 1):
            pr, pi = pows[-1]
            pows.append((pr * ar - pi * ai, pr * ai + pi * ar))
        rows = lax.broadcasted_iota(jnp.int32, (SUB, ns), 0)

        def rows_of(v):
            return jnp.broadcast_to(v, (SUB, ns))

        for k, s in enumerate((1, 2, 4)):
            pr, pi = rows_of(pows[s - 1][0]), rows_of(pows[s - 1][1])
            pw_ref[k, 0] = jnp.where(rows >= s, pr, 0.0)
            pw_ref[k, 1] = jnp.where(rows >= s, pi, 0.0)
            qw_ref[k, 0] = jnp.where(rows + s <= SUB - 1, pr, 0.0)
            qw_ref[k, 1] = jnp.where(rows + s <= SUB - 1, -pi, 0.0)
        fr = fi = br = bi = jnp.zeros((SUB, ns), F32)
        for i in range(SUB):
            fr = jnp.where(rows == i, rows_of(pows[i][0]), fr)
            fi = jnp.where(rows == i, rows_of(pows[i][1]), fi)
            br = jnp.where(rows == i, rows_of(pows[SUB - 1 - i][0]), br)
            bi = jnp.where(rows == i, rows_of(-pows[SUB - 1 - i][1]), bi)
        pw_ref[3, 0], pw_ref[3, 1], qw_ref[3, 0], qw_ref[3, 1] = fr, fi, br, bi

        wide = _spread((C_STATE, ns), 1).astype(BF16)
        tall = _spread((ns, C_STATE), 0).astype(BF16)
        in_rows = _same_group((C_WIDTH, ns), C_GROUP_CH, C_STATE)
        in_cols = _same_group((ns, C_WIDTH), C_STATE, C_GROUP_CH)

        def across(v, sign=1.0):
            return jnp.where(in_rows, sign * jnp.dot(_bf(v), wide, preferred_element_type=F32), 0.0).astype(BF16)

        def down(vt, sign=1.0):
            return jnp.where(in_cols, sign * jnp.dot(tall, _bf(vt), preferred_element_type=F32), 0.0).astype(BF16)

        mb_ref[:, 0:ns] = across(bbr_ref[...])
        mb_ref[:, ns:2 * ns] = across(bbi_ref[...])
        mct_ref[:, 0:ns] = across(c2r_ref[...])
        mct_ref[:, ns:2 * ns] = across(c2i_ref[...], -1.0)
        mc_ref[0:ns, :] = down(ctr_ref[...])
        mc_ref[ns:2 * ns, :] = down(cti_ref[...], -1.0)

    return pl.pallas_call(
        body, out_shape=[S((4, 2, SUB, ns), F32), S((4, 2, SUB, ns), F32), S((C_WIDTH, 2 * ns), BF16),
                         S((2 * ns, C_WIDTH), BF16), S((C_WIDTH, 2 * ns), BF16)],
        compiler_params=pltpu.CompilerParams(vmem_limit_bytes=VMEM_LIMIT), name=name)(a, bbr, bbi, c2r, c2i, ctr, cti)


def s5_block_grads(name, u, lamb, xsb, dyb):
    t = u.shape[0]

    def mb_body(u_ref, lr_ref, li_ref, o_ref):
        ub = _bf(u_ref[...])
        o_ref[:, 0:BLOCK_ST] = lax.dot_general(ub, lr_ref[...], _TN, preferred_element_type=F32)
        o_ref[:, BLOCK_ST:2 * BLOCK_ST] = lax.dot_general(ub, li_ref[...], _TN, preferred_element_type=F32)

    d_mb = pl.pallas_call(
        mb_body, grid=(S5_BLOCKS,),
        in_specs=[pl.BlockSpec((t, BLOCK_CH), lambda q: (0, q)), pl.BlockSpec((t, BLOCK_ST), lambda q: (0, q)),
                  pl.BlockSpec((t, BLOCK_ST), lambda q: (0, S5_BLOCKS + q))],
        out_specs=pl.BlockSpec((BLOCK_CH, 2 * BLOCK_ST), lambda q: (q, 0)), out_shape=S((C_WIDTH, 2 * BLOCK_ST), F32),
        compiler_params=_cp("parallel"), name=name + "_b")(u, lamb, lamb)

    def mc_body(x_ref, dy_ref, o_ref):
        o_ref[...] = lax.dot_general(x_ref[...], dy_ref[...], _TN, preferred_element_type=F32)

    d_mc = pl.pallas_call(
        mc_body, grid=(2, S5_BLOCKS),
        in_specs=[pl.BlockSpec((t, BLOCK_ST), lambda p, q: (0, p * S5_BLOCKS + q)), pl.BlockSpec((t, BLOCK_CH), lambda p, q: (0, q))],
        out_specs=pl.BlockSpec((BLOCK_ST, BLOCK_CH), lambda p, q: (p * S5_BLOCKS + q, 0)),
        out_shape=S((2 * N_STATE, BLOCK_CH), F32), compiler_params=_cp("parallel", "parallel"), name=name + "_c")(xsb, dyb)
    return d_mb, d_mc


def s5_param_grads(name, d_mb, d_mc, da, lam_re, lam_im, log_dt, bt_re, bt_im):
    ns = N_STATE

    def body(dmb_ref, dmc_ref, da_ref, lr_ref, li_ref, ldt_ref, btr_ref, bti_ref,
             glr_ref, gli_ref, gdt_ref, gbr_ref, gbi_ref, gcr_ref, gci_ref):
        lr, li = lr_ref[...], li_ref[...]
        dt, ar, ai, den, qr, qi = _zoh(lr, li, ldt_ref[...])
        per_block = C_GROUPS // S5_BLOCKS
        wide = _spread((C_STATE, BLOCK_ST), 1).astype(F32)
        tall = _spread((BLOCK_ST, C_STATE), 0).astype(F32)
        rows = lax.broadcasted_iota(jnp.int32, (C_WIDTH, BLOCK_ST), 0) // C_GROUP_CH % per_block
        in_rows = rows == lax.broadcasted_iota(jnp.int32, (C_WIDTH, BLOCK_ST), 1) // C_STATE
        in_cols = _same_group((BLOCK_ST, BLOCK_CH), C_STATE, C_GROUP_CH)

        def fold_rows(v):
            return lax.dot_general(jnp.where(in_rows, v, 0.0), wide, (((1,), (1,)), ((), ())), precision=_HI,
                                   preferred_element_type=F32)

        def fold_cols(v):
            return lax.dot_general(jnp.where(in_cols, v, 0.0), tall, (((0,), (0,)), ((), ())), precision=_HI,
                                   preferred_element_type=F32)

        for cs, s_re, s_im in _S5_BLOCKS:
            gcr_ref[cs, :] = fold_cols(dmc_ref[s_re, :])
            gci_ref[cs, :] = -fold_cols(dmc_ref[s_im, :])
        gbbr = fold_rows(dmb_ref[:, 0:BLOCK_ST])
        gbbi = fold_rows(dmb_ref[:, BLOCK_ST:2 * BLOCK_ST])
        btr, bti = btr_ref[...], bti_ref[...]
        q2r, q2i = _per_channel(qr), _per_channel(qi)
        gbr_ref[...] = q2r * gbbr + q2i * gbbi
        gbi_ref[...] = q2r * gbbi - q2i * gbbr

        def per_group(v):
            return jnp.sum(v.reshape(C_GROUPS, C_GROUP_CH, C_STATE), axis=1)

        gqr = per_group(btr * gbbr + bti * gbbi)
        gqi = per_group(btr * gbbi - bti * gbbr)
        ilr, ili = lr / den, li / den
        gar = da_ref[0] + ilr * gqr - ili * gqi
        gai = da_ref[1] + ilr * gqi + ili * gqr
        sr = (qr * lr + qi * li) / den
        si = (qi * lr - qr * li) / den
        gzr = ar * gar + ai * gai
        gzi = ar * gai - ai * gar
        glr_ref[...] = -sr * gqr - si * gqi + dt * gzr
        gli_ref[...] = -sr * gqi + si * gqr + dt * gzi
        gdt_ref[...] = jnp.sum(lr * gzr + li * gzi, axis=1, keepdims=True) * dt

    return pl.pallas_call(
        body, out_shape=[S(_GP, F32), S(_GP, F32), S((C_GROUPS, 1), F32), S(_RP, F32), S(_RP, F32), S(_RP, F32), S(_RP, F32)],
        compiler_params=pltpu.CompilerParams(vmem_limit_bytes=VMEM_LIMIT), name=name,
    )(d_mb, d_mc, da, lam_re, lam_im, log_dt, bt_re, bt_im)


def _cmul_add(xr, xi, pr, pi, zr, zi):
    return xr + pr * zr - pi * zi, xi + pr * zi + pi * zr


def s5_fwd(name, x, gain, w_in, mb, mc, pw, dskip, w_out_t):
    t, d = x.shape
    tm = _tile(t, S5_ROWS)
    ns = N_STATE

    def body(x_ref, g_ref, wi_ref, mb_ref, mc_ref, pw_ref, d_ref, wo_ref,
             x1_ref, hn_ref, r_ref, u_ref, gy_ref, y_ref, xs_ref, xb_ref, o1_ref, o2_ref, carry):
        @pl.when(pl.program_id(0) == 0)
        def _():
            carry[...] = jnp.zeros(carry.shape, F32)

        xv = x_ref[...]
        rv = lax.rsqrt(jnp.mean(xv * xv, axis=-1, keepdims=True) + EPS)
        hn = (xv * rv * g_ref[...]).astype(BF16)
        hn_ref[...] = hn
        r_ref[...] = rv
        uv = jnp.dot(hn, wi_ref[...], preferred_element_type=F32)
        u_ref[...] = uv
        ub = _bf(uv)
        for cs, s_re, s_im in _S5_BLOCKS:
            xs_ref[:, s_re] = jnp.dot(ub[:, cs], mb_ref[cs, s_re], preferred_element_type=F32)
            xs_ref[:, s_im] = jnp.dot(ub[:, cs], mb_ref[cs, s_im], preferred_element_type=F32)

        def group(i, _):
            r0 = pl.multiple_of(i * SUB, SUB)
            xr = xs_ref[pl.ds(r0, SUB), 0:ns]
            xi = xs_ref[pl.ds(r0, SUB), ns:2 * ns]
            for k, s in enumerate((1, 2, 4)):
                xr, xi = _cmul_add(xr, xi, pw_ref[k, 0], pw_ref[k, 1], pltpu.roll(xr, s, 0), pltpu.roll(xi, s, 0))
            xr, xi = _cmul_add(xr, xi, pw_ref[3, 0], pw_ref[3, 1], carry[0], carry[1])
            xs_ref[pl.ds(r0, SUB), 0:ns] = xr
            xs_ref[pl.ds(r0, SUB), ns:2 * ns] = xi
            carry[0] = jnp.broadcast_to(xr[SUB - 1:SUB, :], (SUB, ns))
            carry[1] = jnp.broadcast_to(xi[SUB - 1:SUB, :], (SUB, ns))
            return 0
        lax.fori_loop(0, tm // SUB, group, 0)

        xb_ref[...] = _bf(xs_ref[...])
        for cs, s_re, s_im in _S5_BLOCKS:
            y = (jnp.dot(xb_ref[:, s_re], mc_ref[s_re, cs], preferred_element_type=F32)
                 + jnp.dot(xb_ref[:, s_im], mc_ref[s_im, cs], preferred_element_type=F32) + d_ref[:, cs] * uv[:, cs])
            y_ref[:, cs] = y
            gy_ref[:, cs] = _gelu(y).astype(gy_ref.dtype)
        o1 = lax.dot_general(gy_ref[...], wo_ref[0:d, :], _NT, preferred_element_type=F32)
        o2 = lax.dot_general(gy_ref[...], wo_ref[d:2 * d, :], _NT, preferred_element_type=F32)
        o1_ref[...] = o1.astype(BF16)
        o2_ref[...] = o2.astype(BF16)
        x1_ref[...] = xv + o1 * _sigmoid(o2)

    c = w_in.shape[1]
    rows = pl.BlockSpec((tm, d), lambda i: (i, 0))
    narrow = pl.BlockSpec((tm, c), lambda i: (i, 0))
    states = pl.BlockSpec((tm, 2 * ns), lambda i: (i, 0))
    return pl.pallas_call(
        body, grid=(t // tm,),
        in_specs=[rows, _whole(gain), _whole(w_in), _whole(mb), _whole(mc), _whole(pw), _whole(dskip), _whole(w_out_t)],
        out_specs=[rows, rows, pl.BlockSpec((tm, 1), lambda i: (i, 0)), narrow, narrow, narrow, states, states, rows, rows],
        out_shape=[S((t, d), F32), S((t, d), BF16), S((t, 1), F32), S((t, c), F32), S((t, c), BF16), S((t, c), F32),
                   S((t, 2 * ns), F32), S((t, 2 * ns), BF16), S((t, d), BF16), S((t, d), BF16)],
        scratch_shapes=[pltpu.VMEM((2, SUB, ns), F32)],
        compiler_params=_cp("arbitrary"), name=name)(x, gain, w_in, mb, mc, pw, dskip, w_out_t)


def s5_bwd(name, dgy, y, u, xs, mct, mbt, qw, dskip):
    t, c = u.shape
    tm = _tile(t, S5_ROWS)
    nt = t // tm
    ns = N_STATE
    ng = tm // SUB

    def body(dgy_ref, y_ref, u_ref, xs_ref, mct_ref, mbt_ref, qw_ref, d_ref,
             du_ref, dy_ref, lb_ref, da_ref, dd_ref, lam, carry):
        @pl.when(pl.program_id(0) == 0)
        def _():
            carry[...] = jnp.zeros(carry.shape, F32)
            da_ref[...] = jnp.zeros(da_ref.shape, F32)
            dd_ref[...] = jnp.zeros(dd_ref.shape, F32)

        uv = u_ref[...]
        dy = dgy_ref[...] * _gelu_grad(y_ref[...])
        dyb = _bf(dy)
        dy_ref[...] = dyb
        dd_ref[...] += jnp.sum(dy * uv, axis=0, keepdims=True)
        for cs, s_re, s_im in _S5_BLOCKS:
            lam[:, s_re] = jnp.dot(dyb[:, cs], mct_ref[cs, s_re], preferred_element_type=F32)
            lam[:, s_im] = jnp.dot(dyb[:, cs], mct_ref[cs, s_im], preferred_element_type=F32)
        last_row = lax.broadcasted_iota(jnp.int32, (SUB, ns), 0) == SUB - 1

        def group(j, _):
            i = ng - 1 - j
            r0 = pl.multiple_of(i * SUB, SUB)
            lr = lam[pl.ds(r0, SUB), 0:ns]
            li = lam[pl.ds(r0, SUB), ns:2 * ns]
            for k, s in enumerate((1, 2, 4)):
                lr, li = _cmul_add(lr, li, qw_ref[k, 0], qw_ref[k, 1],
                                   pltpu.roll(lr, SUB - s, 0), pltpu.roll(li, SUB - s, 0))
            cr, ci = carry[0], carry[1]
            lr, li = _cmul_add(lr, li, qw_ref[3, 0], qw_ref[3, 1], cr, ci)
            lam[pl.ds(r0, SUB), 0:ns] = lr
            lam[pl.ds(r0, SUB), ns:2 * ns] = li
            carry[0] = jnp.broadcast_to(lr[0:1, :], (SUB, ns))
            carry[1] = jnp.broadcast_to(li[0:1, :], (SUB, ns))
            nr = jnp.where(last_row, cr, pltpu.roll(lr, SUB - 1, 0))
            ni = jnp.where(last_row, ci, pltpu.roll(li, SUB - 1, 0))
            xr = xs_ref[pl.ds(r0, SUB), 0:ns]
            xi = xs_ref[pl.ds(r0, SUB), ns:2 * ns]
            da_ref[0] += nr * xr + ni * xi
            da_ref[1] += ni * xr - nr * xi
            return 0
        lax.fori_loop(0, ng, group, 0)

        lb_ref[...] = _bf(lam[...])
        for cs, s_re, s_im in _S5_BLOCKS:
            du = (jnp.dot(lb_ref[:, s_re], mbt_ref[s_re, cs], preferred_element_type=F32)
                  + jnp.dot(lb_ref[:, s_im], mbt_ref[s_im, cs], preferred_element_type=F32) + d_ref[:, cs] * dy[:, cs])
            du_ref[:, cs] = du.astype(du_ref.dtype)

    rev = lambda i: (nt - 1 - i, 0)
    return pl.pallas_call(
        body, grid=(nt,),
        in_specs=[pl.BlockSpec((tm, c), rev), pl.BlockSpec((tm, c), rev), pl.BlockSpec((tm, c), rev),
                  pl.BlockSpec((tm, 2 * ns), rev),
                  pl.BlockSpec(mct.shape, lambda i: (0, 0)), pl.BlockSpec(mbt.shape, lambda i: (0, 0)),
                  pl.BlockSpec(qw.shape, lambda i: (0, 0, 0, 0)), pl.BlockSpec((1, c), lambda i: (0, 0))],
        out_specs=[pl.BlockSpec((tm, c), rev), pl.BlockSpec((tm, c), rev), pl.BlockSpec((tm, 2 * ns), rev),
                   pl.BlockSpec((2, SUB, ns), lambda i: (0, 0, 0)), pl.BlockSpec((1, c), lambda i: (0, 0))],
        out_shape=[S((t, c), BF16), S((t, c), BF16), S((t, 2 * ns), BF16), S((2, SUB, ns), F32), S((1, c), F32)],
        scratch_shapes=[pltpu.VMEM((tm, 2 * ns), F32), pltpu.VMEM((2, SUB, ns), F32)],
        compiler_params=_cp("arbitrary"), name=name)(dgy, y, u, xs, mct, mbt, qw, dskip)


def _first(accs, *_):
    return [accs[0]]


def _rms_bwd_epi(accs, xv, base, rv, g):
    dv = accs[0]
    w = dv * g
    xh = xv * rv
    dx = base + rv * (w - xh * jnp.mean(w * xh, axis=-1, keepdims=True))
    return [dx, dx, jnp.sum(dv * xh, axis=0, keepdims=True)]


def mm_rms_bwd(name, pairs, x, r, gain, dres):
    t, d = x.shape
    return mm_nn(name, t, d, pairs, 1, _rms_bwd_epi, [F32, BF16], tiled=[x, dres], cols=[r], rowv=[gain], sums=[(1, d)])


def even_fwd(x, w, need_out):
    t = x.shape[0]
    proj, hn, r = mm_nn("e_in_f", t, IN_WIDTH, [(x, w["e_w_in_t"], 0, "t")], 1, _first, [F32], norm_gain=w["e_norm"])
    hc = conv_fwd("e_conv_f", proj, w["e_conv_w"], w["e_conv_b"])
    need_out(hc)
    x1, out_a, out_b = even_out_fwd("e_out_f", proj, hc, x, w["e_gmlp_w"], w["e_gmlp_b"], w["e_conv_ln_g"], w["e_conv_ln_b"],
                                    w["e_w_out"])
    return x1, (x, hn, r, proj, out_a, hc, out_b)


def even_bwd_mixers(dxb, saved, w):
    x, hn, r, proj, out_a, hc, out_b = saved
    t = x.shape[0]
    g_w_out = jnp.concatenate([mm_tn("e_out_wa", out_a, dxb), mm_tn("e_out_wb", out_b, dxb)], axis=0)
    dab, g_gw, g_gb = gmlp_bwd("e_gmlp_b", proj, dxb, w["e_w_out"], w["e_gmlp_w"], w["e_gmlp_b"])
    dhc, g_lg, g_lb = ln_silu_bwd("e_ln_b", hc, dxb, w["e_w_out"], w["e_conv_ln_g"], w["e_conv_ln_b"])
    dba, dbg, g_cw, g_cb = conv_bwd("e_conv_b", proj, dhc, w["e_conv_w"])
    g_w_in_t = jnp.concatenate([mm_tn("e_in_w0", dab, hn), mm_tn("e_in_w1", dba, hn), mm_tn("e_in_w2", dbg, hn)], axis=0)
    grads = dict(e_w_in_t=g_w_in_t, e_gmlp_w=g_gw[None], e_gmlp_b=g_gb.reshape(1, A_GROUPS, GMLP_BLOCK),
                 e_conv_w=g_cw[None], e_conv_b=g_cb, e_conv_ln_g=g_lg, e_conv_ln_b=g_lb, e_w_out=g_w_out)
    return (dab, dba, dbg), grads


def even_bwd_input(dx, dproj, saved, w):
    x, _, r = saved[:3]
    dab, dba, dbg = dproj
    w_in_t = w["e_w_in_t"]
    return mm_rms_bwd("e_in_b", [(dab, (w_in_t, 0), 0), (dba, (w_in_t, 2), 0), (dbg, (w_in_t, 3), 0)], x, r, w["e_norm"], dx)


def s5_setup(w, anchor=None):
    def rows(v):
        return v.transpose(0, 2, 1).reshape(_RP)

    log_dt = w["o_log_dt"].reshape(C_GROUPS, 1)
    if anchor is not None:
        log_dt = log_dt + anchor
    lam = (w["o_lam_re"], w["o_lam_im"], log_dt, rows(w["o_b_re"]), rows(w["o_b_im"]))
    a, bbr, bbi = s5_discretise("o_s5_zoh", *lam)
    c_re, c_im = w["o_c_re"], w["o_c_im"]
    pw, qw, mb, mc, mct = s5_operands("o_s5_ops", a.reshape(2, N_STATE), bbr, bbi, c_re.reshape(_RP), c_im.reshape(_RP),
                                      c_re.transpose(2, 0, 1).reshape(C_STATE, C_WIDTH),
                                      c_im.transpose(2, 0, 1).reshape(C_STATE, C_WIDTH))
    return dict(lam=lam, pw=pw, qw=qw, mb=mb, mc=mc, mct=mct, mbt=mb.T)


def odd_fwd(x, w, consts):
    x1, hn, r, u, gy, y, xs, xsb, o1, o2 = s5_fwd("o_s5_f", x, w["o_norm"], w["o_w_in"], consts["mb"], consts["mc"],
                                                  consts["pw"], w["o_d"], w["o_w_out_t"])
    return x1, (x, hn, r, u, gy, y, xs, xsb, o1, o2)


def odd_bwd(dx, dxb, saved, w, consts):
    x, hn, r, u, gy, y, xs, xsb, o1, o2 = saved
    t = x.shape[0]

    def gate_bwd(dv, a, b, wv):
        a = a.astype(F32)
        sg = _sigmoid(b.astype(F32))
        do12 = jnp.concatenate([dv * sg, dv * a * sg * (1.0 - sg)], axis=1).astype(BF16)
        return [do12, jnp.dot(do12, wv, preferred_element_type=F32)], []

    do12, dgy = rows_call("o_out_b", gate_bwd, [dx, o1, o2], [w["o_w_out_t"]], [(2 * D_MODEL, BF16), (C_WIDTH, F32)], [])
    g_w_out_t = mm_tn("o_out_w", do12, gy)
    du, dyb, lamb, da8, g_d = s5_bwd("o_s5_b", dgy, y, u, xs, consts["mct"], consts["mbt"], consts["qw"], w["o_d"])
    d_mb, d_mc = s5_block_grads("o_s5_w", u, lamb, xsb, dyb)
    da = jnp.sum(da8, axis=1).reshape((2,) + _GP)
    g_lr, g_li, g_dt, g_btr, g_bti, g_cr, g_ci = s5_param_grads("o_s5_pg", d_mb, d_mc, da, *consts["lam"])

    def states_first(v):
        return v.reshape(C_GROUPS, C_GROUP_CH, C_STATE).transpose(0, 2, 1)[None]

    g_w_in = mm_tn("o_in_w", hn, du)
    dx0, dx0b, g_norm = mm_rms_bwd("o_in_b", [(du, w["o_w_in"], 0, "t")], x, r, w["o_norm"], dx)
    grads = dict(o_norm=g_norm, o_w_in=g_w_in, o_lam_re=g_lr[None], o_lam_im=g_li[None], o_log_dt=g_dt.reshape(1, C_GROUPS),
                 o_b_re=states_first(g_btr), o_b_im=states_first(g_bti),
                 o_c_re=g_cr.reshape((1, C_GROUPS, C_GROUP_CH, C_STATE)), o_c_im=g_ci.reshape((1, C_GROUPS, C_GROUP_CH, C_STATE)),
                 o_d=g_d, o_w_out_t=g_w_out_t)
    return dx0, dx0b, grads


def ca_fwd(i, x, mem, w):
    t, m = x.shape[0], mem.shape[0]
    k, v, mn, rm = mm_nn(f"ca{i}_kv_f", m, D_MODEL, [(mem, w["ca_wk"][i], 0), (mem, w["ca_wv"][i], 1)], 2,
                         lambda accs: [accs[0], accs[1]], [BF16, BF16], norm_gain=w["ca_mem_norm"][i:i + 1])
    x1, xn, r, q, o = attn_fwd(f"ca{i}_attn_f", x, w["ca_norm"][i:i + 1], w["ca_wq"][i], k, v, w["ca_wo"][i])
    return x1, (x, xn, r, mn, rm, q, k, v, o)


def ca_bwd(i, dx, dxb, saved, mem, w):
    x, xn, r, mn, rm, q, k, v, o = saved
    t, m = x.shape[0], mem.shape[0]
    g_wo = mm_tn(f"ca{i}_o_w", o, dxb)
    dx0, dx0b, dq, dk, dv, g_norm = attn_bwd(f"ca{i}_attn_b", dx, dxb, x, r, w["ca_norm"][i:i + 1], q, k, v,
                                             w["ca_wq"][i], w["ca_wo"][i])
    g_wq = mm_tn(f"ca{i}_q_w", xn, dq)
    g_wk = mm_tn(f"ca{i}_k_w", mn, dk)
    g_wv = mm_tn(f"ca{i}_v_w", mn, dv)
    (dmn,) = mm_nn(f"ca{i}_kv_b", m, D_MODEL, [(dk, w["ca_wk"][i], 0, "t"), (dv, w["ca_wv"][i], 0, "t")], 1, _first, [F32])
    g_mnorm = rms_bwd_gain_only(f"ca{i}_mnorm_b", dmn, mem, rm)
    return dx0, dx0b, dict(ca_norm=g_norm, ca_mem_norm=g_mnorm, ca_wq=g_wq, ca_wk=g_wk, ca_wv=g_wv, ca_wo=g_wo)


FFN_ROWS = 512
FFN_CHUNK = 256


def _whole(a):
    return pl.BlockSpec(a.shape, lambda i: (0,) * a.ndim, pipeline_mode=pl.Buffered(1))


def ffn_fused_fwd(name, x, gain, wg_t, wu_t, wd, target=None, final_gain=None):
    t, d = x.shape
    hid = wd.shape[0]
    tm = _tile(t, FFN_ROWS)
    last = target is not None
    n_main = 4 if last else 1

    def body(*refs):
        x_ref, g_ref, wg_ref, wu_ref, wd_ref = refs[:5]
        rest = refs[5:]
        if last:
            tgt_ref, fg_ref = rest[:2]
            rest = rest[2:]
        main, (xn_ref, r_ref, dgate_ref, dup_ref, h_ref) = rest[:n_main], rest[n_main:]
        xv = x_ref[...]
        rv = lax.rsqrt(jnp.mean(xv * xv, axis=-1, keepdims=True) + EPS)
        xn = (xv * rv * g_ref[...]).astype(BF16)
        xn_ref[...] = xn
        r_ref[...] = rv
        for j in range(hid // FFN_CHUNK):
            cs = slice(j * FFN_CHUNK, (j + 1) * FFN_CHUNK)
            g = lax.dot_general(xn, wg_ref[cs, :], _NT, preferred_element_type=F32)
            u = lax.dot_general(xn, wu_ref[cs, :], _NT, preferred_element_type=F32)
            s = _sigmoid(g)
            silu = g * s
            dgate_ref[:, cs] = (u * (s + silu * (1.0 - s))).astype(BF16)
            dup_ref[:, cs] = silu.astype(BF16)
            h_ref[:, cs] = (silu * u).astype(BF16)
        acc = jnp.dot(h_ref[...], wd_ref[...], preferred_element_type=F32)
        if not last:
            main[0][...] = xv + acc
        else:
            dx, _, dgain, part = _final_loss_epi([acc], xv, tgt_ref[...], fg_ref[...])

            @pl.when(pl.program_id(0) == 0)
            def _():
                main[2][...] = jnp.zeros(main[2].shape, F32)
                main[3][...] = jnp.zeros(main[3].shape, F32)
            main[0][...] = dx
            main[1][...] = dx.astype(BF16)
            main[2][...] += dgain
            main[3][...] += part

    rows = pl.BlockSpec((tm, d), lambda i: (i, 0))
    wide = pl.BlockSpec((tm, hid), lambda i: (i, 0))
    col = pl.BlockSpec((tm, 1), lambda i: (i, 0))
    ins, in_specs = [x, gain, wg_t, wu_t, wd], [rows, _whole(gain), _whole(wg_t), _whole(wu_t), _whole(wd)]
    if last:
        ins += [target, final_gain]
        in_specs += [rows, _whole(final_gain)]
        out_specs = [rows, rows, pl.BlockSpec((1, d), lambda i: (0, 0)), pl.BlockSpec((1, 1), lambda i: (0, 0))]
        out_shape = [S((t, d), F32), S((t, d), BF16), S((1, d), F32), S((1, 1), F32)]
    else:
        out_specs, out_shape = [rows], [S((t, d), F32)]
    out_specs += [rows, col, wide, wide, wide]
    out_shape += [S((t, d), BF16), S((t, 1), F32)] + [S((t, hid), BF16)] * 3
    outs = pl.pallas_call(body, grid=(t // tm,), in_specs=in_specs, out_specs=out_specs, out_shape=out_shape,
                          compiler_params=_cp("arbitrary" if last else "parallel"), name=name)(*ins)
    return (tuple(outs[:4]) if last else outs[0]), outs[n_main:]


def ffn_fused_bwd(name, dx, dxb, x, r, gain, dgate, dup, wg_t, wu_t, wd):
    t, d = x.shape
    hid = wd.shape[0]
    tm = _tile(t, FFN_ROWS // 2)

    def body(dx_ref, dxb_ref, x_ref, r_ref, g_ref, dgate_ref, dup_ref, wg_ref, wu_ref, wd_ref,
             dxo_ref, dxbo_ref, dg_ref, du_ref, dgain_ref):
        @pl.when(pl.program_id(0) == 0)
        def _():
            dgain_ref[...] = jnp.zeros(dgain_ref.shape, F32)

        dxb = dxb_ref[...]
        for j in range(hid // FFN_CHUNK):
            cs = slice(j * FFN_CHUNK, (j + 1) * FFN_CHUNK)
            dh = lax.dot_general(dxb, wd_ref[cs, :], _NT, preferred_element_type=F32)
            dg_ref[:, cs] = (dh * dgate_ref[:, cs].astype(F32)).astype(BF16)
            du_ref[:, cs] = (dh * dup_ref[:, cs].astype(F32)).astype(BF16)
        dxn = (jnp.dot(dg_ref[...], wg_ref[...], preferred_element_type=F32)
               + jnp.dot(du_ref[...], wu_ref[...], preferred_element_type=F32))
        dxo, _, dgain = _rms_bwd_epi([dxn], x_ref[...], dx_ref[...], r_ref[...], g_ref[...])
        dxo_ref[...] = dxo
        dxbo_ref[...] = dxo.astype(BF16)
        dgain_ref[...] += dgain

    rows = pl.BlockSpec((tm, d), lambda i: (i, 0))
    wide = pl.BlockSpec((tm, hid), lambda i: (i, 0))
    col = pl.BlockSpec((tm, 1), lambda i: (i, 0))
    return pl.pallas_call(
        body, grid=(t // tm,),
        in_specs=[rows, rows, rows, col, _whole(gain), wide, wide, _whole(wg_t), _whole(wu_t), _whole(wd)],
        out_specs=[rows, rows, wide, wide, pl.BlockSpec((1, d), lambda i: (0, 0))],
        out_shape=[S((t, d), F32), S((t, d), BF16), S((t, hid), BF16), S((t, hid), BF16), S((1, d), F32)],
        compiler_params=_cp("arbitrary"), name=name)(dx, dxb, x, r, gain, dgate, dup, wg_t, wu_t, wd)


def ffn_fwd(i, x, w, target=None):
    out, (xn, r, dgate, dup, h) = ffn_fused_fwd(f"ffn{i}_f", x, w["ffn_norm"][i:i + 1], w["ffn_w_gate_t"][i],
                                                w["ffn_w_up_t"][i], w["ffn_w_down"][i], target,
                                                None if target is None else w["final_norm"])
    return out, (x, xn, r, dgate, dup, h)


def ffn_bwd(i, dx, dxb, saved, w):
    x, xn, r, dgate, dup, h = saved
    g_wd = mm_tn(f"ffn{i}_down_w", h, dxb)
    dx0, dx0b, dg, du, g_norm = ffn_fused_bwd(f"ffn{i}_b", dx, dxb, x, r, w["ffn_norm"][i:i + 1], dgate, dup,
                                              w["ffn_w_gate_t"][i], w["ffn_w_up_t"][i], w["ffn_w_down"][i])
    g_wg_t = mm_tn(f"ffn{i}_gate_w", dg, xn)
    g_wu_t = mm_tn(f"ffn{i}_up_w", du, xn)
    return dx0, dx0b, dict(ffn_norm=g_norm, ffn_w_gate_t=g_wg_t, ffn_w_up_t=g_wu_t, ffn_w_down=g_wd)


def local_step(x, mem, target, w, fetch=None, on_grads=None, anchor=None):
    consts = s5_setup(w, anchor)

    def need(stage, after):
        if fetch is not None:
            for k, v in fetch(stage, after).items():
                if isinstance(k, tuple):
                    w.setdefault(k[0], {})[k[1]] = v
                else:
                    w[k] = v

    need(0, consts["pw"])
    def early(stage, after):
        if fetch is not None:
            fetch(stage, after, True)

    def before_out(after):
        need(1, after)
        early(2, after)

    x1, s_e = even_fwd(x, w, before_out)
    need(2, x1)
    x2, s_c0 = ca_fwd(0, x1, mem, w)
    need(3, x2)
    x3, s_f0 = ffn_fwd(0, x2, w)
    need(4, x3)
    x4, s_o = odd_fwd(x3, w, consts)
    need(5, x4)
    early(6, x4)
    x5, s_c1 = ca_fwd(1, x4, mem, w)
    need(6, x5)
    (dx, dxb, g_final, loss), s_f1 = ffn_fwd(1, x5, w, target)

    def emit(stage, carry, plain, layered=None, layer=0):
        if on_grads is None:
            return carry
        out = dict(plain)
        out.update({(k, layer): v for k, v in (layered or {}).items()})
        return on_grads(stage, out, list(carry))

    dx, dxb, g_f1 = ffn_bwd(1, dx, dxb, s_f1, w)
    dx, dxb = emit(0, (dx, dxb), {}, g_f1, 1)
    dx, dxb, g_c1 = ca_bwd(1, dx, dxb, s_c1, mem, w)
    dx, dxb, g_o = odd_bwd(dx, dxb, s_o, w, consts)
    dx, dxb = emit(1, (dx, dxb), g_o, g_c1, 1)
    dx, dxb, g_f0 = ffn_bwd(0, dx, dxb, s_f0, w)
    dx, dxb = emit(2, (dx, dxb), {}, g_f0, 0)
    dx, dxb, g_c0 = ca_bwd(0, dx, dxb, s_c0, mem, w)
    dx, dxb = emit(3, (dx, dxb), {}, g_c0, 0)
    dproj, g_e = even_bwd_mixers(dxb, s_e, w)
    dproj = emit(4, dproj, {**g_e, "o_norm": g_o["o_norm"], "o_d": g_o["o_d"]})
    dx, dxb, g_e["e_norm"] = even_bwd_input(dx, dproj, s_e, w)

    grads = dict(g_e)
    grads.update(g_o)
    for g0, g1 in ((g_c0, g_c1), (g_f0, g_f1)):
        for k in g0:
            grads[k] = jnp.concatenate([g0[k], g1[k]], axis=0) if k.endswith("norm") else (g0[k], g1[k])
    grads["final_norm"] = g_final
    return loss, dx, grads


def _group(axes):
    pos = {a: lax.axis_index(a) for a in ("x", "y", "c")}
    me = 0
    for a in axes:
        me = me * 2 + pos[a]
    peers = []
    for mask in range(1, 2 ** len(axes)):
        peer = dict(pos)
        for bit, a in enumerate(axes):
            if (mask >> (len(axes) - 1 - bit)) & 1:
                peer[a] = 1 - pos[a]
        idx = 0
        for a in axes:
            idx = idx * 2 + peer[a]
        peers.append((idx, (peer["x"], peer["y"], peer["c"])))
    return me, peers


def _sibling():
    x, y, c = lax.axis_index("x"), lax.axis_index("y"), lax.axis_index("c")
    return c, (x, y, 1 - c)


_HBM =pl.BlockSpec(memory_space=pltpu.HBM)
_SEM = pl.BlockSpec(memory_space=pltpu.SEMAPHORE)
_EFFECT = pltpu.SideEffectType.DATAFLOW_SIDE_EFFECTING


def _gather_peers(direct):
    chip, _ = _group(("x", "y"))
    core = lax.axis_index("c")
    if direct:
        _, peers = _group(_ALL)
        return chip, core, [(idx // 2, idx % 2, dev) for idx, dev in peers]
    _, peers = _group(("x", "y"))
    return chip, core, [(idx, core, dev) for idx, dev in peers]


def gather_ici_start(name, groups, direct):
    flat = [b for g in groups for b in g]
    sizes = [len(g) for g in groups]
    k_ops, n_g = len(flat), len(groups)
    lands = [lax.empty((4, 2) + tuple(b.shape), b.dtype) for b in flat]
    fan = [N_DEV - 1 if d else 3 for d in direct]

    def body(*refs):
        src, land = refs[:k_ops], refs[k_ops:2 * k_ops]
        sems = refs[2 * k_ops:2 * k_ops + 3 * n_g]
        token = refs[-1]
        i = 0
        for g in range(n_g):
            send, recv, loc = sems[3 * g:3 * g + 3]
            chip, core, peers = _gather_peers(direct[g])
            for j in range(sizes[g]):
                pltpu.make_async_copy(src[i], land[i].at[chip, core], loc.at[j]).start()
                for k, (_, _, dev) in enumerate(peers):
                    s = fan[g] * j + k
                    pltpu.make_async_remote_copy(src_ref=src[i], dst_ref=land[i].at[chip, core], send_sem=send.at[s],
                                                 recv_sem=recv.at[s], device_id=dev, device_id_type=MESH).start()
                i += 1
        token[...] = jnp.zeros(token.shape, token.dtype)

    sem_shapes = []
    for s, f in zip(sizes, fan):
        sem_shapes += [pltpu.SemaphoreType.DMA((f * s,)), pltpu.SemaphoreType.DMA((f * s,)), pltpu.SemaphoreType.DMA((s,))]
    thru = [pltpu.HBM(a.shape, a.dtype) for a in flat + lands]
    outs = pl.pallas_call(
        body, name=name, out_shape=tuple(sem_shapes) + tuple(thru) + (S((8, LANES), F32),),
        in_specs=[_HBM] * (2 * k_ops), out_specs=[_SEM] * (3 * n_g) + [_HBM] * (2 * k_ops) + [pl.BlockSpec(memory_space=pltpu.VMEM)],
        input_output_aliases={i: 3 * n_g + i for i in range(2 * k_ops)},
        compiler_params=pltpu.CompilerParams(has_side_effects=_EFFECT),
    )(*[pltpu.with_memory_space_constraint(a, pltpu.HBM) for a in flat + lands])
    sems = [tuple(outs[3 * g:3 * g + 3]) for g in range(n_g)]
    srcs_thru, lands_thru, off = [], [], 3 * n_g
    for s in sizes:
        srcs_thru.append(list(outs[off:off + s]))
        off += s
    for s in sizes:
        lands_thru.append(list(outs[off:off + s]))
        off += s
    return sems, srcs_thru, lands_thru, outs[-1]


def gather_ici_wait(name, srcs, lands, sems, after, direct=False):
    n = len(srcs)

    def body(*refs):
        src, land = refs[:n], refs[n:2 * n]
        send, recv, loc = refs[2 * n:2 * n + 3]
        chip, core, peers = _gather_peers(direct)
        for j in range(n):
            for k, (pchip, pcore, dev) in enumerate(peers):
                s = len(peers) * j + k
                cp = pltpu.make_async_remote_copy(src_ref=src[j], dst_ref=land[j].at[pchip, pcore], send_sem=send.at[s],
                                                  recv_sem=recv.at[s], device_id=dev, device_id_type=MESH)
                cp.wait_send()
                cp.wait_recv()
            pltpu.make_async_copy(src[j], land[j].at[chip, core], loc.at[j]).wait()

    outs = pl.pallas_call(
        body, name=name, out_shape=tuple(pltpu.HBM(a.shape, a.dtype) for a in list(srcs) + list(lands)),
        in_specs=[_HBM] * (2 * n) + [_SEM] * 3 + [ANY], out_specs=[_HBM] * (2 * n),
        input_output_aliases={i: i for i in range(2 * n)},
        compiler_params=pltpu.CompilerParams(has_side_effects=_EFFECT),
    )(*srcs, *lands, *sems, after)
    return list(outs[n:])


def gather_d2d(name, bufs):
    k_ops = len(bufs)

    def body(*refs):
        in_refs, out_refs = refs[:k_ops], refs[k_ops:2 * k_ops]
        send_sems, recv_sems = refs[2 * k_ops:]
        core, sib = _sibling()
        sent, landed = [], []
        for i in range(k_ops):
            cp = pltpu.make_async_remote_copy(src_ref=in_refs[i].at[:, core], dst_ref=out_refs[i].at[:, core],
                                              send_sem=send_sems.at[i], recv_sem=recv_sems.at[i], device_id=sib, device_id_type=MESH)
            cp.start()
            sent.append(cp)
            landed.append(pltpu.make_async_remote_copy(src_ref=in_refs[i].at[:, core], dst_ref=out_refs[i].at[:, 1 - core],
                                                       send_sem=send_sems.at[i], recv_sem=recv_sems.at[i],
                                                       device_id=sib, device_id_type=MESH))
        for cp in landed:
            cp.wait_recv()
        for cp in sent:
            cp.wait_send()

    return pl.pallas_call(
        body, in_specs=[ANY] * k_ops, out_specs=[ANY] * k_ops, out_shape=[S(b.shape, b.dtype) for b in bufs],
        input_output_aliases={i: i for i in range(k_ops)},
        scratch_shapes=[pltpu.SemaphoreType.DMA((k_ops,)), pltpu.SemaphoreType.DMA((k_ops,))],
        name=name)(*bufs)


def gather_d2d_start(name, bufs):
    k_ops = len(bufs)

    def body(*refs):
        in_refs = refs[:k_ops]
        send, recv = refs[k_ops], refs[k_ops + 1]
        core, sib = _sibling()
        for i in range(k_ops):
            pltpu.make_async_remote_copy(src_ref=in_refs[i].at[:, core], dst_ref=in_refs[i].at[:, core], send_sem=send.at[i],
                                         recv_sem=recv.at[i], device_id=sib, device_id_type=MESH).start()

    outs = pl.pallas_call(
        body, name=name,
        out_shape=(pltpu.SemaphoreType.DMA((k_ops,)), pltpu.SemaphoreType.DMA((k_ops,))) + tuple(pltpu.HBM(b.shape, b.dtype) for b in bufs),
        in_specs=[_HBM] * k_ops, out_specs=[_SEM, _SEM] + [_HBM] * k_ops,
        input_output_aliases={i: 2 + i for i in range(k_ops)},
        compiler_params=pltpu.CompilerParams(has_side_effects=_EFFECT),
    )(*[pltpu.with_memory_space_constraint(b, pltpu.HBM) for b in bufs])
    return (outs[0], outs[1]), list(outs[2:])


def gather_d2d_wait(name, bufs, sems, after):
    k_ops = len(bufs)

    def body(*refs):
        in_refs = refs[:k_ops]
        send, recv = refs[k_ops], refs[k_ops + 1]
        core, sib = _sibling()
        for i in range(k_ops):
            cp = pltpu.make_async_remote_copy(src_ref=in_refs[i].at[:, core], dst_ref=in_refs[i].at[:, 1 - core], send_sem=send.at[i],
                                              recv_sem=recv.at[i], device_id=sib, device_id_type=MESH)
            cp.wait_send()
            cp.wait_recv()

    outs = pl.pallas_call(
        body, name=name, out_shape=tuple(pltpu.HBM(b.shape, b.dtype) for b in bufs),
        in_specs=[_HBM] * k_ops + [_SEM, _SEM, ANY], out_specs=[_HBM] * k_ops,
        input_output_aliases={i: i for i in range(k_ops)},
        compiler_params=pltpu.CompilerParams(has_side_effects=_EFFECT),
    )(*bufs, sems[0], sems[1], after)
    return list(outs)


_ALL = ("x", "y", "c")


def _unit_rows(units):
    offs, off = [], 0
    for u in units:
        offs.append(off)
        off += u.shape[1]
    return offs, off


def scatter_start(name, units, carry):
    n_u, n_c = len(units), len(carry)
    offs, rows = _unit_rows(units)
    land = lax.empty((N_DEV, rows) + tuple(units[0].shape[2:]), units[0].dtype)
    fan = N_DEV - 1

    def body(*refs):
        u_refs, land_ref = refs[:n_u], refs[n_u]
        send, recv, loc = refs[n_u + 1 + n_c:n_u + 4 + n_c]
        me, peers = _group(_ALL)
        for j in range(n_u):
            rs = pl.ds(offs[j], units[j].shape[1])
            pltpu.make_async_copy(u_refs[j].at[me], land_ref.at[me, rs], loc.at[j]).start()
            for k, (idx, dev) in enumerate(peers):
                pltpu.make_async_remote_copy(src_ref=u_refs[j].at[idx], dst_ref=land_ref.at[me, rs], send_sem=send.at[fan * j + k],
                                             recv_sem=recv.at[fan * j + k], device_id=dev, device_id_type=MESH).start()

    thru = list(units) + [land] + list(carry)
    outs = pl.pallas_call(
        body, name=name,
        out_shape=(pltpu.SemaphoreType.DMA((fan * n_u,)), pltpu.SemaphoreType.DMA((fan * n_u,)), pltpu.SemaphoreType.DMA((n_u,)))
        + tuple(pltpu.HBM(a.shape, a.dtype) for a in thru),
        in_specs=[_HBM] * len(thru), out_specs=[_SEM] * 3 + [_HBM] * len(thru),
        input_output_aliases={i: 3 + i for i in range(len(thru))},
        compiler_params=pltpu.CompilerParams(has_side_effects=_EFFECT),
    )(*[pltpu.with_memory_space_constraint(a, pltpu.HBM) for a in thru])
    return tuple(outs[:3]), list(outs[3:3 + n_u]), outs[3 + n_u], list(outs[4 + n_u:])


def scatter_wait(name, units, land, sems, after):
    n_u = len(units)
    offs, _ = _unit_rows(units)
    fan = N_DEV - 1

    def body(*refs):
        u_refs, land_ref = refs[:n_u], refs[n_u]
        send, recv, loc = refs[n_u + 1:n_u + 4]
        me, peers = _group(_ALL)
        for j in range(n_u):
            rs = pl.ds(offs[j], units[j].shape[1])
            for k, (idx, dev) in enumerate(peers):
                cp = pltpu.make_async_remote_copy(src_ref=u_refs[j].at[idx], dst_ref=land_ref.at[idx, rs], send_sem=send.at[fan * j + k],
                                                  recv_sem=recv.at[fan * j + k], device_id=dev, device_id_type=MESH)
                cp.wait_send()
                cp.wait_recv()
            pltpu.make_async_copy(u_refs[j].at[me], land_ref.at[me, rs], loc.at[j]).wait()

    thru = list(units) + [land]
    outs = pl.pallas_call(
        body, name=name, out_shape=tuple(pltpu.HBM(a.shape, a.dtype) for a in thru),
        in_specs=[_HBM] * len(thru) + [_SEM] * 3 + [ANY], out_specs=[_HBM] * len(thru),
        input_output_aliases={i: i for i in range(len(thru))},
        compiler_params=pltpu.CompilerParams(has_side_effects=_EFFECT),
    )(*thru, *sems, after)
    return outs[n_u]


def _row_tile(rows, cap=512):
    return next(t for t in range(cap - cap % 16, 0, -16) if rows % t == 0)


def sum_shares(name, recv, me):
    n, rows, c = recv.shape
    tr = _row_tile(rows)

    def body(me_ref, *refs):
        acc = refs[0][...].astype(F32)
        for r in refs[1:n]:
            acc = acc + r[...].astype(F32)
        refs[n][...] = acc

    def slot(mask):
        return pl.BlockSpec((None, tr, c), lambda i, me, mask=mask: (jnp.bitwise_xor(me[0], mask), i, 0))

    spec = pltpu.PrefetchScalarGridSpec(
        num_scalar_prefetch=1, grid=(rows // tr,), in_specs=[slot(k) for k in range(n)],
        out_specs=pl.BlockSpec((tr, c), lambda i, me: (i, 0)))
    return pl.pallas_call(body, grid_spec=spec, out_shape=S((rows, c), F32),
                          compiler_params=_cp("parallel"), name=name)(me, *([recv] * n))


def sum_slots(name, slots):
    n, r, c = slots.shape

    def body(s_ref, o_ref):
        acc = s_ref[0]
        for j in range(1, n):
            acc = acc + s_ref[j]
        o_ref[...] = acc

    return pl.pallas_call(body, out_shape=S((r, c), F32), compiler_params=pltpu.CompilerParams(vmem_limit_bytes=VMEM_LIMIT),
                          name=name)(slots)


def adamw_units(name, pieces, transposed, w, m, v):
    n_l, k, n = w.shape
    tk = _tile(k, 512) if transposed else k
    p_rows = n if transposed else k
    arrs = [p[0] if isinstance(p, tuple) else p for p in pieces]
    offs = [p[1] // p_rows if isinstance(p, tuple) else 0 for p in pieces]
    assert all(not isinstance(p, tuple) or p[1] % p_rows == 0 for p in pieces)
    c1 = 1.0 - ADAM_B1 ** ADAM_STEP
    c2 = 1.0 - ADAM_B2 ** ADAM_STEP

    def body(*refs):
        p_refs, (w_ref, m_ref, v_ref, g_ref, d_ref, m2_ref, v2_ref) = refs[:n_l], refs[n_l:]
        gv = p_refs[0][...]
        for j in range(1, n_l):
            gv = jnp.where(pl.program_id(0) == j, p_refs[j][...], gv)
        if transposed:
            gv = gv.T
        m2 = ADAM_B1 * m_ref[...] + (1.0 - ADAM_B1) * gv
        v2 = ADAM_B2 * v_ref[...] + (1.0 - ADAM_B2) * (gv * gv)
        g_ref[...] = gv
        m2_ref[...] = m2
        v2_ref[...] = v2
        d_ref[...] = -ADAM_LR * ((m2 / c1) / (jnp.sqrt(v2 / c2) + ADAM_EPS) + ADAM_WD * w_ref[...])

    def piece(o):
        if transposed:
            return pl.BlockSpec((n, tk), lambda l, i, o=o: (o, i))
        return pl.BlockSpec((k, n), lambda l, i, o=o: (o, 0))

    blk = pl.BlockSpec((None, tk, n), lambda l, i: (l, i, 0))
    return tuple(pl.pallas_call(body, grid=(n_l, k // tk), in_specs=[piece(o) for o in offs] + [blk] * 3, out_specs=[blk] * 4,
                                out_shape=[S(w.shape, F32)] * 4, compiler_params=_cp("parallel", "parallel"),
                                name=name)(*arrs, w, m, v))


def adamw_native(name, g, w, m, v, tr=512):
    shape = w.shape
    cols = shape[-1]
    rows = w.size // cols
    tr = _tile(rows, tr) if rows % 8 == 0 else rows
    c1 = 1.0 - ADAM_B1 ** ADAM_STEP
    c2 = 1.0 - ADAM_B2 ** ADAM_STEP

    def body(g_ref, w_ref, m_ref, v_ref, d_ref, m2_ref, v2_ref):
        gv = g_ref[...]
        m2 = ADAM_B1 * m_ref[...] + (1.0 - ADAM_B1) * gv
        v2 = ADAM_B2 * v_ref[...] + (1.0 - ADAM_B2) * (gv * gv)
        m2_ref[...] = m2
        v2_ref[...] = v2
        d_ref[...] = -ADAM_LR * ((m2 / c1) / (jnp.sqrt(v2 / c2) + ADAM_EPS) + ADAM_WD * w_ref[...])

    row = pl.BlockSpec((tr, cols), lambda i: (i, 0))
    outs = pl.pallas_call(body, grid=(rows // tr,), in_specs=[row] * 4, out_specs=[row] * 3,
                          out_shape=[S((rows, cols), F32)] * 3, compiler_params=_cp("parallel"),
                          name=name)(*[a.reshape(rows, cols) for a in (g, w, m, v)])
    return tuple(o.reshape(shape) for o in outs)


_REPLICATED = ("e_norm", "e_gmlp_w", "e_gmlp_b", "e_conv_b", "e_conv_ln_g", "e_conv_ln_b", "o_lam_re", "o_lam_im", "o_log_dt",
               "o_b_re", "o_b_im", "o_c_re", "o_c_im", "ca_norm", "ca_mem_norm", "ffn_norm", "final_norm")
_ORDER = ("e_norm", "e_w_in", "e_gmlp_w", "e_gmlp_b", "e_conv_w", "e_conv_b", "e_conv_ln_g", "e_conv_ln_b", "e_w_out",
          "o_norm", "o_w_in", "o_lam_re", "o_lam_im", "o_log_dt", "o_b_re", "o_b_im", "o_c_re", "o_c_im", "o_d", "o_w_out",
          "ca_norm", "ca_mem_norm", "ca_wq", "ca_wk", "ca_wv", "ca_wo", "ffn_norm", "ffn_w_gate", "ffn_w_up", "ffn_w_down",
          "final_norm")


def _rows128(a, multiple=8):
    flat = a.reshape(-1)
    rows = -(-flat.shape[0] // (LANES * multiple)) * multiple
    return jnp.pad(flat, (0, rows * LANES - flat.shape[0])).reshape(rows, LANES)


def _shard(full, axis):
    s = full.shape
    return jnp.moveaxis(full.reshape(s[:axis] + (N_DEV, s[axis] // N_DEV) + s[axis + 1:]), axis, 0)


_UNITS = (("e_w_in", 0, True), ("e_w_out", 0, False), ("o_w_in", 0, False), ("o_w_out", 0, True),
          *[(n, i, False) for n in ("ca_wq", "ca_wk", "ca_wv", "ca_wo") for i in (0, 1)],
          *[(n, i, tr) for n, tr in (("ffn_w_gate", True), ("ffn_w_up", True), ("ffn_w_down", False)) for i in (0, 1)])
_LAYERED = ("ca_wq", "ca_wk", "ca_wv", "ca_wo", "ffn_w_gate", "ffn_w_up", "ffn_w_down")
_SMALL_SHARDED = (("e_conv_w", 2), ("o_norm", 1), ("o_d", 1))
RS_ROW = 1024


def _unit_key(name, tr):
    return name + "_t" if tr else name


def _stage_of(name, layer):
    if name.startswith("e_"):
        return 0 if name == "e_w_in" else 1
    if name.startswith("o_"):
        return 4
    if name.startswith("ca_"):
        return 2 if layer == 0 else 5
    return 3 if layer == 0 else 6


GATHER_STAGES = 7
GATHER_DIRECT = (False, False, False, False, True, True, False)


def weight_fetcher(local):
    groups, meta = [[] for _ in range(GATHER_STAGES)], [[] for _ in range(GATHER_STAGES)]
    for name, layer, tr in _UNITS:
        blk = local[name][layer]
        st = _stage_of(name, layer)
        groups[st].append(_bf(blk.T if tr else blk))
        meta[st].append((name, layer, tr))
    small = jnp.concatenate([local[name].reshape(-1) for name, _ in _SMALL_SHARDED])
    groups[0].append(_rows128(small))
    direct = list(GATHER_DIRECT)
    sems, srcs, lands, token = gather_ici_start("ag_w_start", groups, direct)

    early = {}

    def fetch(stage, after, start_only=False):
        if start_only:
            landed = gather_ici_wait(f"ag_w_wait{stage}", srcs[stage], lands[stage], sems[stage], after, direct[stage])
            early[stage] = gather_d2d_start(f"ag_w_d2d{stage}_start", landed)
            return {}
        if stage in early:
            bufs = gather_d2d_wait(f"ag_w_d2d{stage}_wait", early[stage][1], early[stage][0], after)
        else:
            bufs = gather_ici_wait(f"ag_w_wait{stage}", srcs[stage], lands[stage], sems[stage], after, direct[stage])
            if not direct[stage]:
                bufs = gather_d2d(f"ag_w_d2d{stage}", bufs)
        got = {}
        for (name, layer, tr), blk, buf in zip(meta[stage], groups[stage], bufs):
            arr = buf.reshape((N_DEV * blk.shape[0],) + tuple(blk.shape[1:]))
            if name in _LAYERED:
                got[(_unit_key(name, tr), layer)] = arr
            else:
                got[_unit_key(name, tr)] = arr
        if stage == 0:
            flat = bufs[-1].reshape(N_DEV, -1)
            off = 0
            for name, axis in _SMALL_SHARDED:
                blk = local[name]
                seg = flat[:, off:off + blk.size].reshape((N_DEV,) + blk.shape)
                off += blk.size
                seg = jnp.moveaxis(seg, 0, axis)
                got[name] = seg.reshape(seg.shape[:axis] + (-1,) + seg.shape[axis + 2:])
            got["e_conv_w"] = got["e_conv_w"][0]
        return got

    return fetch, token


def _grad_stage_of(name, layer):
    if name.startswith("e_"):
        return 4
    if name.startswith("o_"):
        return 1
    if name.startswith("ca_"):
        return 3 if layer == 0 else 1
    return 2 if layer == 0 else 0


GRAD_STAGES = 5
SMALL_ROWS = 16


def gradient_reducer(local, mom, var):
    me = (4 * lax.axis_index("x") + 2 * lax.axis_index("y") + lax.axis_index("c")).astype(jnp.int32).reshape(1)
    pending = []

    def start(stage, grads, carry):
        def grad_of(unit):
            key = _unit_key(unit[0], unit[2])
            return grads[(key, unit[1])] if unit[0] in _LAYERED else grads[key]

        units = sorted([u for u in _UNITS if _grad_stage_of(u[0], u[1]) == stage], key=lambda u: -grad_of(u).size)
        parts, spans = [], []
        for unit in units:
            g = grad_of(unit)
            part = g.reshape(N_DEV, -1, RS_ROW)
            spans.append((part.shape[1], g.shape[0] // N_DEV, g.shape[1]))
            parts.append(part)
        if stage == GRAD_STAGES - 1:
            small = jnp.concatenate([_shard(grads[name], axis).reshape(N_DEV, -1) for name, axis in _SMALL_SHARDED], axis=1)
            small = jnp.pad(small, ((0, 0), (0, SMALL_ROWS * RS_ROW - small.shape[1])))
            parts.append(small.astype(BF16).reshape(N_DEV, SMALL_ROWS, RS_ROW))
        sems, sent, land, carry = scatter_start(f"rs_start{stage}", parts, carry)
        pending.append((stage, units, spans, sems, sent, land))
        return carry

    def finish(after):
        res, per_layer, small_flat = {}, {}, None
        for stage, units, spans, sems, sent, land in pending:
            land = scatter_wait(f"rs_wait{stage}", sent, land, sems, after)
            total = sum_shares(f"rs_sum{stage}", land, me)
            off = 0
            for (name, layer, tr), (rows, r, c) in zip(units, spans):
                piece = (total, off) if c == RS_ROW else total[off:off + rows].reshape(r, c)
                per_layer.setdefault(name, {})[layer] = (piece, tr)
                off += rows
            if stage == GRAD_STAGES - 1:
                small_flat = total[off:off + SMALL_ROWS].reshape(-1)
        for name, by_layer in per_layer.items():
            pieces = [by_layer[i][0] for i in sorted(by_layer)]
            res[name] = adamw_units("adamw_" + name, pieces, by_layer[0][1], local[name], mom[name], var[name])
        off = 0
        for name, _ in _SMALL_SHARDED:
            blk = local[name]
            g = small_flat[off:off + blk.size].reshape(blk.shape)
            off += blk.size
            res[name] = (g,) + adamw_native("adamw_" + name, g, blk, mom[name], var[name])
        return res

    return start, finish


def replicated_start(grads, loss):
    pack = jnp.concatenate([_rows128(grads[name]) for name in _REPLICATED] + [_rows128(loss)], axis=0)
    sems, srcs, lands, token = gather_ici_start("ag_g_start", [[pack]], [False])
    return sems[0], srcs[0], lands[0], token


def replicated_finish(handle, after, w, mom, var):
    sems, srcs, lands, _ = handle
    (buf,) = gather_d2d("ag_g_d2d", gather_ici_wait("ag_g_wait", srcs, lands, sems, after))
    rows = srcs[0].shape[0]
    total = sum_slots("ag_g_sum", buf.reshape(N_DEV, rows, LANES))
    res, off = {}, 0
    for name in _REPLICATED:
        n = w[name].size
        nr = -(-n // (LANES * 8)) * 8
        g = total[off:off + nr].reshape(-1)[:n].reshape(w[name].shape)
        off += nr
        res[name] = (g,) + adamw_native("adamw_" + name, g, w[name], mom[name], var[name])
    return res, total[off, 0]


def kernel(x, mem, e_norm, e_w_in, e_gmlp_w, e_gmlp_b, e_conv_w, e_conv_b, e_conv_ln_g, e_conv_ln_b, e_w_out, o_norm, o_w_in, o_lam_re, o_lam_im, o_log_dt, o_b_re, o_b_im, o_c_re, o_c_im, o_d, o_w_out, ca_norm, ca_mem_norm, ca_wq, ca_wk, ca_wv, ca_wo, ffn_norm, ffn_w_gate, ffn_w_up, ffn_w_down, final_norm, loss_target, m_e_norm, m_e_w_in, m_e_gmlp_w, m_e_gmlp_b, m_e_conv_w, m_e_conv_b, m_e_conv_ln_g, m_e_conv_ln_b, m_e_w_out, m_o_norm, m_o_w_in, m_o_lam_re, m_o_lam_im, m_o_log_dt, m_o_b_re, m_o_b_im, m_o_c_re, m_o_c_im, m_o_d, m_o_w_out, m_ca_norm, m_ca_mem_norm, m_ca_wq, m_ca_wk, m_ca_wv, m_ca_wo, m_ffn_norm, m_ffn_w_gate, m_ffn_w_up, m_ffn_w_down, m_final_norm, v_e_norm, v_e_w_in, v_e_gmlp_w, v_e_gmlp_b, v_e_conv_w, v_e_conv_b, v_e_conv_ln_g, v_e_conv_ln_b, v_e_w_out, v_o_norm, v_o_w_in, v_o_lam_re, v_o_lam_im, v_o_log_dt, v_o_b_re, v_o_b_im, v_o_c_re, v_o_c_im, v_o_d, v_o_w_out, v_ca_norm, v_ca_mem_norm, v_ca_wq, v_ca_wk, v_ca_wv, v_ca_wo, v_ffn_norm, v_ffn_w_gate, v_ffn_w_up, v_ffn_w_down, v_final_norm):
    given = dict(locals())
    local = {k: given[k] for k in _ORDER}
    mom = {k: given["m_" + k] for k in _ORDER}
    var = {k: given["v_" + k] for k in _ORDER}

    w = {}
    w.update({
        "e_norm": e_norm, "e_gmlp_w": e_gmlp_w[0], "e_gmlp_b": e_gmlp_b.reshape(A_GROUPS, GMLP_BLOCK, 1),
        "e_conv_b": e_conv_b, "e_conv_ln_g": e_conv_ln_g, "e_conv_ln_b": e_conv_ln_b,
        "o_lam_re": o_lam_re[0], "o_lam_im": o_lam_im[0], "o_log_dt": o_log_dt[0], "o_b_re": o_b_re[0], "o_b_im": o_b_im[0],
        "o_c_re": o_c_re[0], "o_c_im": o_c_im[0], "ca_norm": ca_norm, "ca_mem_norm": ca_mem_norm, "ffn_norm": ffn_norm,
        "final_norm": final_norm.reshape(1, D_MODEL),
    })
    start_reduce, finish_reduce = gradient_reducer(local, mom, var)
    fetch, token = weight_fetcher(local)
    loss_part, grad_x, grads = local_step(x[0], mem[0], loss_target[0], w, fetch, start_reduce, token[0:1, 0:1])
    grads["final_norm"] = grads["final_norm"].reshape(D_MODEL)

    handle = replicated_start(grads, loss_part)
    res = finish_reduce(handle[3])
    rep, loss = replicated_finish(handle, res["ffn_w_down"][1], local, mom, var)
    res.update(rep)
    return (loss, grad_x[None], *[res[k][0] for k in _ORDER], *[res[k][1] for k in _ORDER],
            *[res[k][2] for k in _ORDER], *[res[k][3] for k in _ORDER])
```

```python
import jax
import jax.numpy as jnp
from jax import lax
from jax.experimental import pallas as pl
from jax.experimental.pallas import tpu as pltpu

F32 = jnp.float32
BF16 = jnp.bfloat16
S = jax.ShapeDtypeStruct

D_MODEL = 1024
A_WIDTH = 512
A_GROUPS = 4
GMLP_BLOCK = 128
CHUNK = 64
B_WIDTH = 512
IN_WIDTH = 2 * A_WIDTH + 2 * B_WIDTH
CONV_WIDTH = 31
CONV_PAD = 32
C_WIDTH = 512
C_GROUP_CH = 16
C_GROUPS = 32
C_STATE = 64
N_STATE = C_GROUPS * C_STATE
CA_HEADS = 4
CA_HEAD_DIM = 256
EPS = 1e-6
ADAM_LR = 0.001
ADAM_B1 = 0.9
ADAM_B2 = 0.999
ADAM_EPS = 1e-08
ADAM_WD = 0.01
ADAM_STEP = 10
N_DEV = 8
LANES = 128
VMEM_LIMIT = 56 << 20
VMEM_BUDGET = 40 << 20
MM_TN_RESIDENT = 8 << 20
MESH = pl.DeviceIdType.MESH
ANY = pl.BlockSpec(memory_space=pl.ANY)


def _cp(*sem):
    return pltpu.CompilerParams(dimension_semantics=sem, vmem_limit_bytes=VMEM_LIMIT)


def _tile(n, pref):
    t = pref
    while n % t:
        t //= 2
    return t


def _bf(v):
    return v if v.dtype == BF16 else v.astype(BF16)


def _sigmoid(x):
    return 1.0 / (1.0 + jnp.exp(-x))


_GC = 0.7978845608028654


def _gelu(x):
    return 0.5 * x * (1.0 + jnp.tanh(_GC * (x + 0.044715 * x * x * x)))


def _gelu_grad(x):
    x2 = x * x
    t = jnp.tanh(_GC * (x + 0.044715 * x * x2))
    return 0.5 * (1.0 + t) + 0.5 * x * (1.0 - t * t) * _GC * (1.0 + 3.0 * 0.044715 * x2)


def _tspec(entry, tm):
    if isinstance(entry, tuple):
        arr, cb, width = entry
        return arr, pl.BlockSpec((tm, width), lambda i, cb=cb: (i, cb))
    return entry, pl.BlockSpec((tm, entry.shape[1]), lambda i: (i, 0))


def rows_call(name, fn, tiled, full, outs, accs, tm=256):
    pairs = [_tspec(e, tm) for e in tiled]
    arrs = [p[0] for p in pairs]
    rows = arrs[0].shape[0]
    tm = _tile(rows, tm)
    pairs = [_tspec(e, tm) for e in tiled]
    n_in = len(tiled) + len(full)
    n_out = len(outs)

    def body(*refs):
        vals = [r[...] for r in refs[:n_in]]
        o_refs = refs[n_in:n_in + n_out]
        a_refs = refs[n_in + n_out:]
        ov, av = fn(*vals)
        for r, v in zip(o_refs, ov):
            r[...] = v.astype(r.dtype)
        if a_refs:
            @pl.when(pl.program_id(0) == 0)
            def _():
                for r in a_refs:
                    r[...] = jnp.zeros(r.shape, r.dtype)
            for r, v in zip(a_refs, av):
                r[...] += v

    in_specs = [p[1] for p in pairs] + [pl.BlockSpec(a.shape, lambda i, nd=a.ndim: (0,) * nd) for a in full]
    out_specs = [pl.BlockSpec((tm, c), lambda i: (i, 0)) for c, _ in outs]
    out_specs += [pl.BlockSpec(s, lambda i, nd=len(s): (0,) * nd) for s in accs]
    out_shape = [S((rows, c), dt) for c, dt in outs] + [S(s, F32) for s in accs]
    return pl.pallas_call(body, grid=(rows // tm,), in_specs=in_specs, out_specs=out_specs, out_shape=out_shape,
                          compiler_params=_cp("arbitrary"), name=name)(*arrs, *full)


def mm_nn(name, m, n, pairs, n_acc, epi, outs, tiled=(), cols=(), rowv=(), sums=(), norm_gain=None):
    a_ops, a_slot, b_arrs, b_specs, idx, trans = [], [], [], [], [], []
    fixed = 0
    for pair in pairs:
        a, b, k = pair[:3]
        bt = len(pair) > 3
        arr, cb, kdim = a if isinstance(a, tuple) else (a, 0, a.shape[1])
        key = (id(arr), cb, kdim)
        if key not in [o[0] for o in a_ops]:
            a_ops.append((key, arr, cb, kdim))
        a_slot.append([o[0] for o in a_ops].index(key))
        b_arr, off = b if isinstance(b, tuple) else (b, 0)
        b_arrs.append(b_arr)
        if bt:
            assert off % n == 0 and b_arr.shape[1] == kdim
            b_specs.append(pl.BlockSpec((n, kdim), lambda i, o=off // n: (o, 0), pipeline_mode=pl.Buffered(1)))
        else:
            assert b_arr.shape[1] == n
            b_specs.append(pl.BlockSpec((kdim, n), lambda i, o=off: (o, 0), pipeline_mode=pl.Buffered(1)))
        fixed += kdim * n * b_arr.dtype.itemsize
        idx.append(k)
        trans.append(bt)
    per_row = sum(2 * kdim * arr.dtype.itemsize for _, arr, _, kdim in a_ops)
    per_row += sum(2 * n * t.dtype.itemsize for t in tiled) + sum(2 * n * jnp.dtype(dt).itemsize for dt in outs)
    cn = n if sums or cols else (512 if n % 512 == 0 else 256)
    per_row += (n_acc + 3) * cn * 4
    tm = next((t for t in (1024, 512, 256, 128) if m % t == 0 and fixed + t * per_row <= VMEM_BUDGET), _tile(m, 128))
    n_a, n_p, n_t = len(a_ops), len(pairs), len(tiled)
    n_in = n_a + n_p + n_t + len(cols) + len(rowv)
    normed = norm_gain is not None
    o0 = n_in + normed

    def body(*refs):
        a_vals = [None if normed and i == 0 else _bf(r[...]) for i, r in enumerate(refs[:n_a])]
        if normed:
            xv = refs[0][...]
            rv = lax.rsqrt(jnp.mean(xv * xv, axis=-1, keepdims=True) + EPS)
            a_vals[0] = (xv * rv * refs[n_in][...]).astype(BF16)
            refs[o0 + len(outs)][...] = a_vals[0]
            refs[o0 + len(outs) + 1][...] = rv
        for j in range(n // cn):
            cs = slice(j * cn, (j + 1) * cn)
            accs = [None] * n_acc
            for p in range(n_p):
                av, b_ref = a_vals[a_slot[p]], refs[n_a + p]
                if trans[p]:
                    d = lax.dot_general(av, _bf(b_ref[cs, :]), (((1,), (1,)), ((), ())), preferred_element_type=F32)
                else:
                    d = jnp.dot(av, _bf(b_ref[:, cs]), preferred_element_type=F32)
                accs[idx[p]] = d if accs[idx[p]] is None else accs[idx[p]] + d
            extra = [r[:, cs] for r in refs[n_a + n_p:n_a + n_p + n_t]] + [r[...] for r in refs[n_a + n_p + n_t:n_in - len(rowv)]]
            extra += [r[:, cs] for r in refs[n_in - len(rowv):n_in]]
            ov = epi(accs, *extra)
            for r, v in zip(refs[o0:o0 + len(outs)], ov):
                r[:, cs] = v.astype(r.dtype)
        sv = ov[len(outs):]
        if sums:
            s_refs = refs[o0 + len(outs) + 2 * normed:]

            @pl.when(pl.program_id(0) == 0)
            def _():
                for r in s_refs:
                    r[...] = jnp.zeros(r.shape, r.dtype)
            for r, v in zip(s_refs, sv):
                r[...] += v

    in_specs = [pl.BlockSpec((tm, kdim), lambda i, cb=cb: (i, cb)) for _, _, cb, kdim in a_ops] + b_specs
    in_specs += [pl.BlockSpec((tm, n), lambda i: (i, 0)) for _ in tiled]
    in_specs += [pl.BlockSpec((tm, 1), lambda i: (i, 0)) for _ in cols]
    in_specs += [pl.BlockSpec((1, n), lambda i: (0, 0)) for _ in rowv]
    out_specs = [pl.BlockSpec((tm, n), lambda i: (i, 0)) for _ in outs]
    out_shape = [S((m, n), dt) for dt in outs]
    gain = []
    if normed:
        k0 = a_ops[0][3]
        gain = [norm_gain]
        in_specs.append(pl.BlockSpec((1, k0), lambda i: (0, 0)))
        out_specs += [pl.BlockSpec((tm, k0), lambda i: (i, 0)), pl.BlockSpec((tm, 1), lambda i: (i, 0))]
        out_shape += [S((m, k0), BF16), S((m, 1), F32)]
    out_specs += [pl.BlockSpec(s, lambda i, nd=len(s): (0,) * nd) for s in sums]
    out_shape += [S(s, F32) for s in sums]
    return pl.pallas_call(body, grid=(m // tm,), in_specs=in_specs, out_specs=out_specs, out_shape=out_shape,
                          compiler_params=_cp("arbitrary" if sums else "parallel"),
                          name=name)(*[o[1] for o in a_ops], *b_arrs, *tiled, *cols, *rowv, *gain)


def mm_tn(name, a, b, out_dtype=BF16):
    if isinstance(a, tuple):
        a_arr, a_cb, m = a
    else:
        a_arr, a_cb, m = a, None, a.shape[1]
    if isinstance(b, tuple):
        b_arr, b_cb, n = b
    else:
        b_arr, b_cb, n = b, None, b.shape[1]
    t = a_arr.shape[0]
    whole_b = t * n * b_arr.dtype.itemsize <= MM_TN_RESIDENT and b_cb is None
    tn = n if whole_b else _tile(n, 512)
    tm = _tile(m, 512 if t * 512 * a_arr.dtype.itemsize * 2 + t * tn * b_arr.dtype.itemsize * 2 <= VMEM_BUDGET else 256)
    a_off = 0 if a_cb is None else a_cb * (m // tm)
    b_off = 0 if b_cb is None else b_cb * (n // tn)

    def body(a_ref, b_ref, o_ref):
        o_ref[...] = lax.dot_general(_bf(a_ref[...]), _bf(b_ref[...]), (((0,), (0,)), ((), ())),
                                     preferred_element_type=F32).astype(o_ref.dtype)

    if whole_b:
        b_spec = pl.BlockSpec((t, n), lambda i, j: (0, 0), pipeline_mode=pl.Buffered(1))
    else:
        b_spec = pl.BlockSpec((t, tn), lambda i, j: (0, j + b_off))
    return pl.pallas_call(
        body, grid=(m // tm, n // tn),
        in_specs=[pl.BlockSpec((t, tm), lambda i, j: (0, i + a_off)), b_spec],
        out_specs=pl.BlockSpec((tm, tn), lambda i, j: (i, j)), out_shape=S((m, n), out_dtype),
        compiler_params=_cp("parallel", "parallel"), name=name)(a_arr, b_arr)


def rms_bwd_gain_only(name, dxn, x, r):
    def fn(dv, xv, rv):
        return [], [jnp.sum(dv * xv * rv, axis=0, keepdims=True)]
    return rows_call(name, fn, [dxn, x, r], [], [], [(1, x.shape[1])])[0]


def _final_loss_epi(accs, res, tv, g):
    xv = res + accs[0]
    d = xv.shape[-1]
    r = lax.rsqrt(jnp.mean(xv * xv, axis=-1, keepdims=True) + EPS)
    xh = xv * r
    err = xh * g - tv
    dy = err * (1.0 / d)
    w = dy * g
    dx = r * (w - xh * jnp.mean(w * xh, axis=-1, keepdims=True))
    part = jnp.sum(jnp.sum(err * err, axis=-1, keepdims=True), axis=0, keepdims=True) * (0.5 / d)
    return [dx, dx, jnp.sum(dy * xh, axis=0, keepdims=True), part]


def _gmlp_mask():
    row = lax.broadcasted_iota(jnp.int32, (GMLP_BLOCK, GMLP_BLOCK), 0) // CHUNK
    col = lax.broadcasted_iota(jnp.int32, (GMLP_BLOCK, GMLP_BLOCK), 1) // CHUNK
    return col <= row


def _ln_plain(v):
    mu = jnp.mean(v, axis=-1, keepdims=True)
    vc = v - mu
    rstd = lax.rsqrt(jnp.mean(vc * vc, axis=-1, keepdims=True) + EPS)
    return vc * rstd, rstd


def even_out_fwd(name, proj, hc, x, w, b, ln_g, ln_b, w_out, tm=512):
    t, d = x.shape
    tm = _tile(t, tm)

    def body(au_ref, av_ref, hc_ref, x_ref, w_ref, b_ref, lg_ref, lb_ref, wo_ref, x1_ref, cat_ref):
        mask = _gmlp_mask()
        u = _gelu(au_ref[...])
        vn, _ = _ln_plain(_gelu(av_ref[...]))
        vnb = _bf(vn)
        for g in range(A_GROUPS):
            wg = _bf(jnp.where(mask, w_ref[g], 0.0))
            cs = slice(g * GMLP_BLOCK, (g + 1) * GMLP_BLOCK)
            for n in range(tm // GMLP_BLOCK):
                rs = slice(n * GMLP_BLOCK, (n + 1) * GMLP_BLOCK)
                sg = jnp.dot(wg, vnb[rs, cs], preferred_element_type=F32) + b_ref[g]
                cat_ref[rs, cs] = (u[rs, cs] * sg).astype(cat_ref.dtype)
        y, _ = _ln_plain(hc_ref[...])
        z = y * lg_ref[...] + lb_ref[...]
        cat_ref[:, A_WIDTH:] = (z * _sigmoid(z)).astype(cat_ref.dtype)
        x1_ref[...] = x_ref[...] + jnp.dot(cat_ref[...], wo_ref[...], preferred_element_type=F32)

    half = pl.BlockSpec((tm, A_WIDTH), lambda i: (i, 0))
    return pl.pallas_call(
        body, grid=(t // tm,),
        in_specs=[half, pl.BlockSpec((tm, A_WIDTH), lambda i: (i, 1)), half, pl.BlockSpec((tm, d), lambda i: (i, 0)),
                  _whole(w), _whole(b), _whole(ln_g), _whole(ln_b), _whole(w_out)],
        out_specs=[pl.BlockSpec((tm, d), lambda i: (i, 0)), pl.BlockSpec((tm, A_WIDTH + B_WIDTH), lambda i: (i, 0))],
        out_shape=[S((t, d), F32), S((t, A_WIDTH + B_WIDTH), BF16)],
        compiler_params=_cp("parallel"), name=name)(proj, proj, hc, x, w, b, ln_g, ln_b, w_out)


def gmlp_bwd(name, proj, dxb, w_out, w, b, tm=512):
    t = proj.shape[0]
    tm = _tile(t, tm)

    def body(au_ref, av_ref, dx_ref, wo_ref, w_ref, b_ref, dp_ref, dw_ref, db_ref):
        @pl.when(pl.program_id(0) == 0)
        def _():
            dw_ref[...] = jnp.zeros(dw_ref.shape, F32)
            db_ref[...] = jnp.zeros(db_ref.shape, F32)

        mask = _gmlp_mask()
        au = au_ref[...]
        av = av_ref[...]
        u = _gelu(au)
        vn, rstd = _ln_plain(_gelu(av))
        vnb = _bf(vn)
        dout = lax.dot_general(dx_ref[...], wo_ref[0:A_WIDTH, :], _NT, preferred_element_type=F32)
        dvn_cols = []
        for g in range(A_GROUPS):
            wm = jnp.where(mask, w_ref[g], 0.0)
            wg = _bf(wm)
            wgt = _bf(wm.T)
            cs = slice(g * GMLP_BLOCK, (g + 1) * GMLP_BLOCK)
            dwg = jnp.zeros((GMLP_BLOCK, GMLP_BLOCK), F32)
            dbg = jnp.zeros((GMLP_BLOCK, 1), F32)
            dvn_rows = []
            for n in range(tm // GMLP_BLOCK):
                rs = slice(n * GMLP_BLOCK, (n + 1) * GMLP_BLOCK)
                sg = jnp.dot(wg, vnb[rs, cs], preferred_element_type=F32) + b_ref[g]
                dp_ref[rs, cs] = (dout[rs, cs] * sg * _gelu_grad(au[rs, cs])).astype(dp_ref.dtype)
                dsg = dout[rs, cs] * u[rs, cs]
                dsgb = _bf(dsg)
                dbg = dbg + jnp.sum(dsg, axis=1, keepdims=True)
                dwg = dwg + lax.dot_general(dsgb, vnb[rs, cs], (((1,), (1,)), ((), ())), preferred_element_type=F32)
                dvn_rows.append(jnp.dot(wgt, dsgb, preferred_element_type=F32))
            dw_ref[g] += jnp.where(mask, dwg, 0.0)
            db_ref[g] += dbg
            dvn_cols.append(jnp.concatenate(dvn_rows, axis=0))
        dvn = jnp.concatenate(dvn_cols, axis=1)
        dv = rstd * (dvn - jnp.mean(dvn, axis=-1, keepdims=True) - vn * jnp.mean(dvn * vn, axis=-1, keepdims=True))
        dp_ref[:, A_WIDTH:] = (dv * _gelu_grad(av)).astype(dp_ref.dtype)

    return pl.pallas_call(
        body, grid=(t // tm,),
        in_specs=[pl.BlockSpec((tm, A_WIDTH), lambda i: (i, 0)), pl.BlockSpec((tm, A_WIDTH), lambda i: (i, 1)),
                  pl.BlockSpec((tm, dxb.shape[1]), lambda i: (i, 0)), pl.BlockSpec(w_out.shape, lambda i: (0, 0)),
                  pl.BlockSpec(w.shape, lambda i: (0, 0, 0)), pl.BlockSpec(b.shape, lambda i: (0, 0, 0))],
        out_specs=[pl.BlockSpec((tm, 2 * A_WIDTH), lambda i: (i, 0)),
                   pl.BlockSpec(w.shape, lambda i: (0, 0, 0)), pl.BlockSpec(b.shape, lambda i: (0, 0, 0))],
        out_shape=[S((t, 2 * A_WIDTH), BF16), S(w.shape, F32), S(b.shape, F32)],
        compiler_params=_cp("arbitrary"), name=name)(proj, proj, dxb, w_out, w, b)


CONV_ROWS = 256
CONV_ROWS_BWD = 64


def conv_fwd(name, proj, w, cb):
    t = proj.shape[0]
    tc = LANES
    rows = _tile(t, CONV_ROWS)
    a_cb, g_cb = 2 * A_WIDTH // tc, (2 * A_WIDTH + B_WIDTH) // tc

    def body(a_ref, g_ref, w_ref, cb_ref, o_ref, hpad):
        hpad[0:CONV_PAD, :] = jnp.zeros((CONV_PAD, tc), F32)

        def fill(i, _):
            r0 = pl.multiple_of(i * rows, rows)
            hpad[pl.ds(CONV_PAD + r0, rows), :] = a_ref[pl.ds(r0, rows), :] * _sigmoid(g_ref[pl.ds(r0, rows), :])
            return 0
        lax.fori_loop(0, t // rows, fill, 0)

        def conv(i, _):
            r0 = pl.multiple_of(i * rows, rows)
            win = hpad[pl.ds(r0, rows + CONV_PAD), :]
            acc = jnp.zeros((rows, tc), F32) + cb_ref[...]
            for b in range(SUB):
                wb = win if b == 0 else pltpu.roll(win, b, 0)
                for a in range(CONV_PAD // SUB):
                    k = CONV_WIDTH - 1 - (SUB * a + b)
                    if k >= 0:
                        lo = CONV_PAD - SUB * a
                        acc = acc + wb[lo:lo + rows, :] * w_ref[k:k + 1, :]
            o_ref[pl.ds(r0, rows), :] = acc
            return 0
        lax.fori_loop(0, t // rows, conv, 0)

    return pl.pallas_call(
        body, grid=(B_WIDTH // tc,),
        in_specs=[pl.BlockSpec((t, tc), lambda j: (0, a_cb + j)), pl.BlockSpec((t, tc), lambda j: (0, g_cb + j)),
                  pl.BlockSpec((CONV_WIDTH, tc), lambda j: (0, j)), pl.BlockSpec((1, tc), lambda j: (0, j))],
        out_specs=pl.BlockSpec((t, tc), lambda j: (0, j)), out_shape=S((t, B_WIDTH), F32),
        scratch_shapes=[pltpu.VMEM((t + CONV_PAD, tc), F32)],
        compiler_params=_cp("parallel"), name=name)(proj, proj, w, cb)


def conv_bwd(name, proj, dhc, w):
    t = proj.shape[0]
    tc = LANES
    rows = _tile(t, CONV_ROWS_BWD)
    a_cb, g_cb = 2 * A_WIDTH // tc, (2 * A_WIDTH + B_WIDTH) // tc
    win_rows = rows + CONV_PAD

    def body(a_ref, g_ref, d_ref, w_ref, da_ref, dg_ref, dw_ref, dcb_ref, hpad, dpad, dwacc):
        hpad[0:CONV_PAD, :] = jnp.zeros((CONV_PAD, tc), F32)
        dpad[t:t + CONV_PAD, :] = jnp.zeros((CONV_PAD, tc), F32)
        dwacc[...] = jnp.zeros(dwacc.shape, F32)

        def fill(i, _):
            r0 = pl.multiple_of(i * rows, rows)
            hpad[pl.ds(CONV_PAD + r0, rows), :] = a_ref[pl.ds(r0, rows), :] * _sigmoid(g_ref[pl.ds(r0, rows), :])
            dpad[pl.ds(r0, rows), :] = d_ref[pl.ds(r0, rows), :]
            return 0
        lax.fori_loop(0, t // rows, fill, 0)

        def step(i, dcb):
            r0 = pl.multiple_of(i * rows, rows)
            hwin = hpad[pl.ds(r0, win_rows), :]
            dwin = dpad[pl.ds(r0, win_rows), :]
            dchunk = dwin[:rows, :]
            dh = jnp.zeros((rows, tc), F32)
            for b in range(SUB):
                hb = hwin if b == 0 else pltpu.roll(hwin, b, 0)
                db = dwin if b == 0 else pltpu.roll(dwin, win_rows - b, 0)
                for a in range(CONV_PAD // SUB):
                    k = CONV_WIDTH - 1 - (SUB * a + b)
                    if k >= 0:
                        dh = dh + db[SUB * a:SUB * a + rows, :] * w_ref[k:k + 1, :]
                        lo = CONV_PAD - SUB * a
                        prod = dchunk * hb[lo:lo + rows, :]
                        dwacc[k] += jnp.sum(prod.reshape(rows // 8, 8, tc), axis=0)
            a = a_ref[pl.ds(r0, rows), :]
            sg = _sigmoid(g_ref[pl.ds(r0, rows), :])
            da_ref[pl.ds(r0, rows), :] = (dh * sg).astype(da_ref.dtype)
            dg_ref[pl.ds(r0, rows), :] = (dh * a * sg * (1.0 - sg)).astype(dg_ref.dtype)
            return dcb + jnp.sum(dchunk, axis=0, keepdims=True)
        dcb = lax.fori_loop(0, t // rows, step, jnp.zeros((1, tc), F32))
        dcb_ref[...] = dcb
        for k in range(CONV_WIDTH):
            dw_ref[k:k + 1, :] = jnp.sum(dwacc[k], axis=0, keepdims=True)

    return pl.pallas_call(
        body, grid=(B_WIDTH // tc,),
        in_specs=[pl.BlockSpec((t, tc), lambda j: (0, a_cb + j)), pl.BlockSpec((t, tc), lambda j: (0, g_cb + j)),
                  pl.BlockSpec((t, tc), lambda j: (0, j)), pl.BlockSpec((CONV_WIDTH, tc), lambda j: (0, j))],
        out_specs=[pl.BlockSpec((t, tc), lambda j: (0, j)), pl.BlockSpec((t, tc), lambda j: (0, j)),
                   pl.BlockSpec((CONV_WIDTH, tc), lambda j: (0, j)), pl.BlockSpec((1, tc), lambda j: (0, j))],
        out_shape=[S((t, B_WIDTH), BF16), S((t, B_WIDTH), BF16), S((CONV_WIDTH, B_WIDTH), F32), S((1, B_WIDTH), F32)],
        scratch_shapes=[pltpu.VMEM((t + CONV_PAD, tc), F32), pltpu.VMEM((t + CONV_PAD, tc), F32),
                        pltpu.VMEM((CONV_WIDTH, 8, tc), F32)],
        compiler_params=_cp("parallel"), name=name)(proj, proj, dhc, w)


def ln_silu_bwd(name, hc, dxb, w_out, g, b):
    c = hc.shape[1]

    def fn(h, dxv, wv, gv, bv):
        dout = lax.dot_general(dxv, wv[A_WIDTH:, :], _NT, preferred_element_type=F32)
        y, rstd = _ln_plain(h)
        z = y * gv + bv
        s = _sigmoid(z)
        dz = dout * s * (1.0 + z * (1.0 - s))
        dyv = dz * gv
        dh = rstd * (dyv - jnp.mean(dyv, axis=-1, keepdims=True) - y * jnp.mean(dyv * y, axis=-1, keepdims=True))
        return [dh], [jnp.sum(dz * y, axis=0, keepdims=True), jnp.sum(dz, axis=0, keepdims=True)]

    return rows_call(name, fn, [hc, dxb], [w_out, g, b], [(c, F32)], [(1, c), (1, c)])


_NT = (((1,), (1,)), ((), ()))
_TN = (((0,), (0,)), ((), ()))


def attn_fwd(name, x, gain, wq, k, v, wo, tm=512):
    t, d = x.shape
    m = k.shape[0]
    tm = _tile(t, tm)
    scale = CA_HEAD_DIM ** -0.5

    def body(x_ref, g_ref, wq_ref, k_ref, v_ref, wo_ref, x1_ref, xn_ref, r_ref, q_ref, o_ref):
        xv = x_ref[...]
        rv = lax.rsqrt(jnp.mean(xv * xv, axis=-1, keepdims=True) + EPS)
        xn = (xv * rv * g_ref[...]).astype(BF16)
        xn_ref[...] = xn
        r_ref[...] = rv
        q_ref[...] = jnp.dot(xn, wq_ref[...], preferred_element_type=F32).astype(BF16)
        for h in range(CA_HEADS):
            cs = slice(h * CA_HEAD_DIM, (h + 1) * CA_HEAD_DIM)
            s = lax.dot_general(q_ref[:, cs], k_ref[:, cs], _NT, preferred_element_type=F32) * scale
            e = jnp.exp(s - jnp.max(s, axis=-1, keepdims=True))
            p = e / jnp.sum(e, axis=-1, keepdims=True)
            o_ref[:, cs] = jnp.dot(_bf(p), v_ref[:, cs], preferred_element_type=F32).astype(o_ref.dtype)
        x1_ref[...] = xv + jnp.dot(o_ref[...], wo_ref[...], preferred_element_type=F32)

    def whole(a):
        return pl.BlockSpec(a.shape, lambda i: (0, 0), pipeline_mode=pl.Buffered(1))

    rows = pl.BlockSpec((tm, d), lambda i: (i, 0))
    col = pl.BlockSpec((tm, 1), lambda i: (i, 0))
    return pl.pallas_call(
        body, grid=(t // tm,),
        in_specs=[rows, whole(gain), whole(wq), whole(k), whole(v), whole(wo)],
        out_specs=[rows, rows, col, rows, rows],
        out_shape=[S((t, d), F32), S((t, d), BF16), S((t, 1), F32), S((t, d), BF16), S((t, d), BF16)],
        compiler_params=_cp("parallel"), name=name)(x, gain, wq, k, v, wo)


def attn_bwd(name, dx, dxb, x, r, gain, q, k, v, wq, wo, tm=512):
    t, d = q.shape
    m = k.shape[0]
    tm = _tile(t, tm)
    scale = CA_HEAD_DIM ** -0.5

    def body(dx_ref, dxb_ref, x_ref, r_ref, g_ref, q_ref, k_ref, v_ref, wq_ref, wo_ref,
             dxo_ref, dxbo_ref, dq_ref, dk_ref, dv_ref, dg_ref, do_s):
        @pl.when(pl.program_id(0) == 0)
        def _():
            dk_ref[...] = jnp.zeros(dk_ref.shape, F32)
            dv_ref[...] = jnp.zeros(dv_ref.shape, F32)
            dg_ref[...] = jnp.zeros(dg_ref.shape, F32)

        do_s[...] = lax.dot_general(dxb_ref[...], wo_ref[...], _NT, preferred_element_type=F32).astype(BF16)
        for h in range(CA_HEADS):
            cs = slice(h * CA_HEAD_DIM, (h + 1) * CA_HEAD_DIM)
            qh, kh, vh, doh = q_ref[:, cs], k_ref[:, cs], v_ref[:, cs], do_s[:, cs]
            s = lax.dot_general(qh, kh, _NT, preferred_element_type=F32) * scale
            e = jnp.exp(s - jnp.max(s, axis=-1, keepdims=True))
            p = e / jnp.sum(e, axis=-1, keepdims=True)
            pb = _bf(p)
            dv_ref[:, cs] += lax.dot_general(pb, doh, _TN, preferred_element_type=F32)
            dp = lax.dot_general(doh, vh, _NT, preferred_element_type=F32)
            ds = _bf(p * (dp - jnp.sum(dp * p, axis=-1, keepdims=True)) * scale)
            dq_ref[:, cs] = jnp.dot(ds, kh, preferred_element_type=F32).astype(dq_ref.dtype)
            dk_ref[:, cs] += lax.dot_general(ds, qh, _TN, preferred_element_type=F32)
        dxn = lax.dot_general(dq_ref[...], wq_ref[...], _NT, preferred_element_type=F32)
        xh = x_ref[...] * r_ref[...]
        wv = dxn * g_ref[...]
        dxo = dx_ref[...] + r_ref[...] * (wv - xh * jnp.mean(wv * xh, axis=-1, keepdims=True))
        dxo_ref[...] = dxo
        dxbo_ref[...] = dxo.astype(BF16)
        dg_ref[...] += jnp.sum(dxn * xh, axis=0, keepdims=True)

    def whole(a):
        return pl.BlockSpec(a.shape, lambda i: (0, 0), pipeline_mode=pl.Buffered(1))

    rows = pl.BlockSpec((tm, d), lambda i: (i, 0))
    col = pl.BlockSpec((tm, 1), lambda i: (i, 0))
    acc = pl.BlockSpec((m, d), lambda i: (0, 0))
    return pl.pallas_call(
        body, grid=(t // tm,),
        in_specs=[rows, rows, rows, col, whole(gain), rows, whole(k), whole(v), whole(wq), whole(wo)],
        out_specs=[rows, rows, rows, acc, acc, pl.BlockSpec((1, d), lambda i: (0, 0))],
        out_shape=[S((t, d), F32), S((t, d), BF16), S((t, d), BF16), S((m, d), F32), S((m, d), F32), S((1, d), F32)],
        scratch_shapes=[pltpu.VMEM((tm, d), BF16)],
        compiler_params=_cp("arbitrary"), name=name)(dx, dxb, x, r, gain, q, k, v, wq, wo)


SUB = 8
S5_ROWS = 256


S5_BLOCKS = 4
BLOCK_CH = C_WIDTH // S5_BLOCKS
BLOCK_ST = N_STATE // S5_BLOCKS
_S5_BLOCKS = tuple((slice(BLOCK_CH * q, BLOCK_CH * (q + 1)), slice(BLOCK_ST * q, BLOCK_ST * (q + 1)),
                    slice(N_STATE + BLOCK_ST * q, N_STATE + BLOCK_ST * (q + 1))) for q in range(S5_BLOCKS))
_HI = lax.Precision.HIGHEST
_GP = (C_GROUPS, C_STATE)
_RP = (C_WIDTH, C_STATE)


def _zoh(lr, li, ldt):
    dt = jnp.exp(ldt)
    mag = jnp.exp(lr * dt)
    ar = mag * jnp.cos(li * dt)
    ai = mag * jnp.sin(li * dt)
    den = lr * lr + li * li
    qr = ((ar - 1.0) * lr + ai * li) / den
    qi = (ai * lr - (ar - 1.0) * li) / den
    return dt, ar, ai, den, qr, qi


def _per_channel(v):
    return jnp.broadcast_to(v[:, None, :], (C_GROUPS, C_GROUP_CH, C_STATE)).reshape(_RP)


def _same_group(shape, row_per_group, col_per_group):
    rows = lax.broadcasted_iota(jnp.int32, shape, 0) // row_per_group
    cols = lax.broadcasted_iota(jnp.int32, shape, 1) // col_per_group
    return rows == cols


def _spread(shape, axis):
    long = lax.broadcasted_iota(jnp.int32, shape, axis) % C_STATE
    short = lax.broadcasted_iota(jnp.int32, shape, 1 - axis)
    return long == short


def s5_discretise(name, lam_re, lam_im, log_dt, bt_re, bt_im):
    def body(lr_ref, li_ref, ldt_ref, btr_ref, bti_ref, a_ref, bbr_ref, bbi_ref):
        _, ar, ai, _, qr, qi = _zoh(lr_ref[...], li_ref[...], ldt_ref[...])
        a_ref[0] = ar
        a_ref[1] = ai
        q2r, q2i = _per_channel(qr), _per_channel(qi)
        btr, bti = btr_ref[...], bti_ref[...]
        bbr_ref[...] = q2r * btr - q2i * bti
        bbi_ref[...] = q2r * bti + q2i * btr

    return pl.pallas_call(body, out_shape=[S((2,) + _GP, F32), S(_RP, F32), S(_RP, F32)],
                          name=name)(lam_re, lam_im, log_dt, bt_re, bt_im)


def s5_operands(name, a, bbr, bbi, c2r, c2i, ctr, cti):
    ns = N_STATE

    def body(a_ref, bbr_ref, bbi_ref, c2r_ref, c2i_ref, ctr_ref, cti_ref, pw_ref, qw_ref, mb_ref, mc_ref, mct_ref):
        ar, ai = a_ref[0:1, :], a_ref[1:2, :]
        pows = [(ar, ai)]
        for _ in range(SUB - 1):
            pr, pi = pows[-1]
            pows.append((pr * ar - pi * ai, pr * ai + pi * ar))
        rows = lax.broadcasted_iota(jnp.int32, (SUB, ns), 0)

        def rows_of(v):
            return jnp.broadcast_to(v, (SUB, ns))

        for k, s in enumerate((1, 2, 4)):
            pr, pi = rows_of(pows[s - 1][0]), rows_of(pows[s - 1][1])
            pw_ref[k, 0] = jnp.where(rows >= s, pr, 0.0)
            pw_ref[k, 1] = jnp.where(rows >= s, pi, 0.0)
            qw_ref[k, 0] = jnp.where(rows + s <= SUB - 1, pr, 0.0)
            qw_ref[k, 1] = jnp.where(rows + s <= SUB - 1, -pi, 0.0)
        fr = fi = br = bi = jnp.zeros((SUB, ns), F32)
        for i in range(SUB):
            fr = jnp.where(rows == i, rows_of(pows[i][0]), fr)
            fi = jnp.where(rows == i, rows_of(pows[i][1]), fi)
            br = jnp.where(rows == i, rows_of(pows[SUB - 1 - i][0]), br)
            bi = jnp.where(rows == i, rows_of(-pows[SUB - 1 - i][1]), bi)
        pw_ref[3, 0], pw_ref[3, 1], qw_ref[3, 0], qw_ref[3, 1] = fr, fi, br, bi

        wide = _spread((C_STATE, ns), 1).astype(BF16)
        tall = _spread((ns, C_STATE), 0).astype(BF16)
        in_rows = _same_group((C_WIDTH, ns), C_GROUP_CH, C_STATE)
        in_cols = _same_group((ns, C_WIDTH), C_STATE, C_GROUP_CH)

        def across(v, sign=1.0):
            return jnp.where(in_rows, sign * jnp.dot(_bf(v), wide, preferred_element_type=F32), 0.0).astype(BF16)

        def down(vt, sign=1.0):
            return jnp.where(in_cols, sign * jnp.dot(tall, _bf(vt), preferred_element_type=F32), 0.0).astype(BF16)

        mb_ref[:, 0:ns] = across(bbr_ref[...])
        mb_ref[:, ns:2 * ns] = across(bbi_ref[...])
        mct_ref[:, 0:ns] = across(c2r_ref[...])
        mct_ref[:, ns:2 * ns] = across(c2i_ref[...], -1.0)
        mc_ref[0:ns, :] = down(ctr_ref[...])
        mc_ref[ns:2 * ns, :] = down(cti_ref[...], -1.0)

    return pl.pallas_call(
        body, out_shape=[S((4, 2, SUB, ns), F32), S((4, 2, SUB, ns), F32), S((C_WIDTH, 2 * ns), BF16),
                         S((2 * ns, C_WIDTH), BF16), S((C_WIDTH, 2 * ns), BF16)],
        compiler_params=pltpu.CompilerParams(vmem_limit_bytes=VMEM_LIMIT), name=name)(a, bbr, bbi, c2r, c2i, ctr, cti)


def s5_block_grads(name, u, lamb, xsb, dyb):
    t = u.shape[0]

    def mb_body(u_ref, lr_ref, li_ref, o_ref):
        ub = _bf(u_ref[...])
        o_ref[:, 0:BLOCK_ST] = lax.dot_general(ub, lr_ref[...], _TN, preferred_element_type=F32)
        o_ref[:, BLOCK_ST:2 * BLOCK_ST] = lax.dot_general(ub, li_ref[...], _TN, preferred_element_type=F32)

    d_mb = pl.pallas_call(
        mb_body, grid=(S5_BLOCKS,),
        in_specs=[pl.BlockSpec((t, BLOCK_CH), lambda q: (0, q)), pl.BlockSpec((t, BLOCK_ST), lambda q: (0, q)),
                  pl.BlockSpec((t, BLOCK_ST), lambda q: (0, S5_BLOCKS + q))],
        out_specs=pl.BlockSpec((BLOCK_CH, 2 * BLOCK_ST), lambda q: (q, 0)), out_shape=S((C_WIDTH, 2 * BLOCK_ST), F32),
        compiler_params=_cp("parallel"), name=name + "_b")(u, lamb, lamb)

    def mc_body(x_ref, dy_ref, o_ref):
        o_ref[...] = lax.dot_general(x_ref[...], dy_ref[...], _TN, preferred_element_type=F32)

    d_mc = pl.pallas_call(
        mc_body, grid=(2, S5_BLOCKS),
        in_specs=[pl.BlockSpec((t, BLOCK_ST), lambda p, q: (0, p * S5_BLOCKS + q)), pl.BlockSpec((t, BLOCK_CH), lambda p, q: (0, q))],
        out_specs=pl.BlockSpec((BLOCK_ST, BLOCK_CH), lambda p, q: (p * S5_BLOCKS + q, 0)),
        out_shape=S((2 * N_STATE, BLOCK_CH), F32), compiler_params=_cp("parallel", "parallel"), name=name + "_c")(xsb, dyb)
    return d_mb, d_mc


def s5_param_grads(name, d_mb, d_mc, da, lam_re, lam_im, log_dt, bt_re, bt_im):
    ns = N_STATE

    def body(dmb_ref, dmc_ref, da_ref, lr_ref, li_ref, ldt_ref, btr_ref, bti_ref,
             glr_ref, gli_ref, gdt_ref, gbr_ref, gbi_ref, gcr_ref, gci_ref):
        lr, li = lr_ref[...], li_ref[...]
        dt, ar, ai, den, qr, qi = _zoh(lr, li, ldt_ref[...])
        per_block = C_GROUPS // S5_BLOCKS
        wide = _spread((C_STATE, BLOCK_ST), 1).astype(F32)
        tall = _spread((BLOCK_ST, C_STATE), 0).astype(F32)
        rows = lax.broadcasted_iota(jnp.int32, (C_WIDTH, BLOCK_ST), 0) // C_GROUP_CH % per_block
        in_rows = rows == lax.broadcasted_iota(jnp.int32, (C_WIDTH, BLOCK_ST), 1) // C_STATE
        in_cols = _same_group((BLOCK_ST, BLOCK_CH), C_STATE, C_GROUP_CH)

        def fold_rows(v):
            return lax.dot_general(jnp.where(in_rows, v, 0.0), wide, (((1,), (1,)), ((), ())), precision=_HI,
                                   preferred_element_type=F32)

        def fold_cols(v):
            return lax.dot_general(jnp.where(in_cols, v, 0.0), tall, (((0,), (0,)), ((), ())), precision=_HI,
                                   preferred_element_type=F32)

        for cs, s_re, s_im in _S5_BLOCKS:
            gcr_ref[cs, :] = fold_cols(dmc_ref[s_re, :])
            gci_ref[cs, :] = -fold_cols(dmc_ref[s_im, :])
        gbbr = fold_rows(dmb_ref[:, 0:BLOCK_ST])
        gbbi = fold_rows(dmb_ref[:, BLOCK_ST:2 * BLOCK_ST])
        btr, bti = btr_ref[...], bti_ref[...]
        q2r, q2i = _per_channel(qr), _per_channel(qi)
        gbr_ref[...] = q2r * gbbr + q2i * gbbi
        gbi_ref[...] = q2r * gbbi - q2i * gbbr

        def per_group(v):
            return jnp.sum(v.reshape(C_GROUPS, C_GROUP_CH, C_STATE), axis=1)

        gqr = per_group(btr * gbbr + bti * gbbi)
        gqi = per_group(btr * gbbi - bti * gbbr)
        ilr, ili = lr / den, li / den
        gar = da_ref[0] + ilr * gqr - ili * gqi
        gai = da_ref[1] + ilr * gqi + ili * gqr
        sr = (qr * lr + qi * li) / den
        si = (qi * lr - qr * li) / den
        gzr = ar * gar + ai * gai
        gzi = ar * gai - ai * gar
        glr_ref[...] = -sr * gqr - si * gqi + dt * gzr
        gli_ref[...] = -sr * gqi + si * gqr + dt * gzi
        gdt_ref[...] = jnp.sum(lr * gzr + li * gzi, axis=1, keepdims=True) * dt

    return pl.pallas_call(
        body, out_shape=[S(_GP, F32), S(_GP, F32), S((C_GROUPS, 1), F32), S(_RP, F32), S(_RP, F32), S(_RP, F32), S(_RP, F32)],
        compiler_params=pltpu.CompilerParams(vmem_limit_bytes=VMEM_LIMIT), name=name,
    )(d_mb, d_mc, da, lam_re, lam_im, log_dt, bt_re, bt_im)


def _cmul_add(xr, xi, pr, pi, zr, zi):
    return xr + pr * zr - pi * zi, xi + pr * zi + pi * zr


def s5_fwd(name, x, gain, w_in, mb, mc, pw, dskip, w_out_t):
    t, d = x.shape
    tm = _tile(t, S5_ROWS)
    ns = N_STATE

    def body(x_ref, g_ref, wi_ref, mb_ref, mc_ref, pw_ref, d_ref, wo_ref,
             x1_ref, hn_ref, r_ref, u_ref, gy_ref, y_ref, xs_ref, xb_ref, o1_ref, o2_ref, carry):
        @pl.when(pl.program_id(0) == 0)
        def _():
            carry[...] = jnp.zeros(carry.shape, F32)

        xv = x_ref[...]
        rv = lax.rsqrt(jnp.mean(xv * xv, axis=-1, keepdims=True) + EPS)
        hn = (xv * rv * g_ref[...]).astype(BF16)
        hn_ref[...] = hn
        r_ref[...] = rv
        uv = jnp.dot(hn, wi_ref[...], preferred_element_type=F32)
        u_ref[...] = uv
        ub = _bf(uv)
        for cs, s_re, s_im in _S5_BLOCKS:
            xs_ref[:, s_re] = jnp.dot(ub[:, cs], mb_ref[cs, s_re], preferred_element_type=F32)
            xs_ref[:, s_im] = jnp.dot(ub[:, cs], mb_ref[cs, s_im], preferred_element_type=F32)

        def group(i, _):
            r0 = pl.multiple_of(i * SUB, SUB)
            xr = xs_ref[pl.ds(r0, SUB), 0:ns]
            xi = xs_ref[pl.ds(r0, SUB), ns:2 * ns]
            for k, s in enumerate((1, 2, 4)):
                xr, xi = _cmul_add(xr, xi, pw_ref[k, 0], pw_ref[k, 1], pltpu.roll(xr, s, 0), pltpu.roll(xi, s, 0))
            xr, xi = _cmul_add(xr, xi, pw_ref[3, 0], pw_ref[3, 1], carry[0], carry[1])
            xs_ref[pl.ds(r0, SUB), 0:ns] = xr
            xs_ref[pl.ds(r0, SUB), ns:2 * ns] = xi
            carry[0] = jnp.broadcast_to(xr[SUB - 1:SUB, :], (SUB, ns))
            carry[1] = jnp.broadcast_to(xi[SUB - 1:SUB, :], (SUB, ns))
            return 0
        lax.fori_loop(0, tm // SUB, group, 0)

        xb_ref[...] = _bf(xs_ref[...])
        for cs, s_re, s_im in _S5_BLOCKS:
            y = (jnp.dot(xb_ref[:, s_re], mc_ref[s_re, cs], preferred_element_type=F32)
                 + jnp.dot(xb_ref[:, s_im], mc_ref[s_im, cs], preferred_element_type=F32) + d_ref[:, cs] * uv[:, cs])
            y_ref[:, cs] = y
            gy_ref[:, cs] = _gelu(y).astype(gy_ref.dtype)
        o1 = lax.dot_general(gy_ref[...], wo_ref[0:d, :], _NT, preferred_element_type=F32)
        o2 = lax.dot_general(gy_ref[...], wo_ref[d:2 * d, :], _NT, preferred_element_type=F32)
        o1_ref[...] = o1.astype(BF16)
        o2_ref[...] = o2.astype(BF16)
        x1_ref[...] = xv + o1 * _sigmoid(o2)

    c = w_in.shape[1]
    rows = pl.BlockSpec((tm, d), lambda i: (i, 0))
    narrow = pl.BlockSpec((tm, c), lambda i: (i, 0))
    states = pl.BlockSpec((tm, 2 * ns), lambda i: (i, 0))
    return pl.pallas_call(
        body, grid=(t // tm,),
        in_specs=[rows, _whole(gain), _whole(w_in), _whole(mb), _whole(mc), _whole(pw), _whole(dskip), _whole(w_out_t)],
        out_specs=[rows, rows, pl.BlockSpec((tm, 1), lambda i: (i, 0)), narrow, narrow, narrow, states, states, rows, rows],
        out_shape=[S((t, d), F32), S((t, d), BF16), S((t, 1), F32), S((t, c), F32), S((t, c), BF16), S((t, c), F32),
                   S((t, 2 * ns), F32), S((t, 2 * ns), BF16), S((t, d), BF16), S((t, d), BF16)],
        scratch_shapes=[pltpu.VMEM((2, SUB, ns), F32)],
        compiler_params=_cp("arbitrary"), name=name)(x, gain, w_in, mb, mc, pw, dskip, w_out_t)


def s5_bwd(name, dgy, y, u, xs, mct, mbt, qw, dskip):
    t, c = u.shape
    tm = _tile(t, S5_ROWS)
    nt = t // tm
    ns = N_STATE
    ng = tm // SUB

    def body(dgy_ref, y_ref, u_ref, xs_ref, mct_ref, mbt_ref, qw_ref, d_ref,
             du_ref, dy_ref, lb_ref, da_ref, dd_ref, lam, carry):
        @pl.when(pl.program_id(0) == 0)
        def _():
            carry[...] = jnp.zeros(carry.shape, F32)
            da_ref[...] = jnp.zeros(da_ref.shape, F32)
            dd_ref[...] = jnp.zeros(dd_ref.shape, F32)

        uv = u_ref[...]
        dy = dgy_ref[...] * _gelu_grad(y_ref[...])
        dyb = _bf(dy)
        dy_ref[...] = dyb
        dd_ref[...] += jnp.sum(dy * uv, axis=0, keepdims=True)
        for cs, s_re, s_im in _S5_BLOCKS:
            lam[:, s_re] = jnp.dot(dyb[:, cs], mct_ref[cs, s_re], preferred_element_type=F32)
            lam[:, s_im] = jnp.dot(dyb[:, cs], mct_ref[cs, s_im], preferred_element_type=F32)
        last_row = lax.broadcasted_iota(jnp.int32, (SUB, ns), 0) == SUB - 1

        def group(j, _):
            i = ng - 1 - j
            r0 = pl.multiple_of(i * SUB, SUB)
            lr = lam[pl.ds(r0, SUB), 0:ns]
            li = lam[pl.ds(r0, SUB), ns:2 * ns]
            for k, s in enumerate((1, 2, 4)):
                lr, li = _cmul_add(lr, li, qw_ref[k, 0], qw_ref[k, 1],
                                   pltpu.roll(lr, SUB - s, 0), pltpu.roll(li, SUB - s, 0))
            cr, ci = carry[0], carry[1]
            lr, li = _cmul_add(lr, li, qw_ref[3, 0], qw_ref[3, 1], cr, ci)
            lam[pl.ds(r0, SUB), 0:ns] = lr
            lam[pl.ds(r0, SUB), ns:2 * ns] = li
            carry[0] = jnp.broadcast_to(lr[0:1, :], (SUB, ns))
            carry[1] = jnp.broadcast_to(li[0:1, :], (SUB, ns))
            nr = jnp.where(last_row, cr, pltpu.roll(lr, SUB - 1, 0))
            ni = jnp.where(last_row, ci, pltpu.roll(li, SUB - 1, 0))
            xr = xs_ref[pl.ds(r0, SUB), 0:ns]
            xi = xs_ref[pl.ds(r0, SUB), ns:2 * ns]
            da_ref[0] += nr * xr + ni * xi
            da_ref[1] += ni * xr - nr * xi
            return 0
        lax.fori_loop(0, ng, group, 0)

        lb_ref[...] = _bf(lam[...])
        for cs, s_re, s_im in _S5_BLOCKS:
            du = (jnp.dot(lb_ref[:, s_re], mbt_ref[s_re, cs], preferred_element_type=F32)
                  + jnp.dot(lb_ref[:, s_im], mbt_ref[s_im, cs], preferred_element_type=F32) + d_ref[:, cs] * dy[:, cs])
            du_ref[:, cs] = du.astype(du_ref.dtype)

    rev = lambda i: (nt - 1 - i, 0)
    return pl.pallas_call(
        body, grid=(nt,),
        in_specs=[pl.BlockSpec((tm, c), rev), pl.BlockSpec((tm, c), rev), pl.BlockSpec((tm, c), rev),
                  pl.BlockSpec((tm, 2 * ns), rev),
                  pl.BlockSpec(mct.shape, lambda i: (0, 0)), pl.BlockSpec(mbt.shape, lambda i: (0, 0)),
                  pl.BlockSpec(qw.shape, lambda i: (0, 0, 0, 0)), pl.BlockSpec((1, c), lambda i: (0, 0))],
        out_specs=[pl.BlockSpec((tm, c), rev), pl.BlockSpec((tm, c), rev), pl.BlockSpec((tm, 2 * ns), rev),
                   pl.BlockSpec((2, SUB, ns), lambda i: (0, 0, 0)), pl.BlockSpec((1, c), lambda i: (0, 0))],
        out_shape=[S((t, c), BF16), S((t, c), BF16), S((t, 2 * ns), BF16), S((2, SUB, ns), F32), S((1, c), F32)],
        scratch_shapes=[pltpu.VMEM((tm, 2 * ns), F32), pltpu.VMEM((2, SUB, ns), F32)],
        compiler_params=_cp("arbitrary"), name=name)(dgy, y, u, xs, mct, mbt, qw, dskip)


def _first(accs, *_):
    return [accs[0]]


def _rms_bwd_epi(accs, xv, base, rv, g):
    dv = accs[0]
    w = dv * g
    xh = xv * rv
    dx = base + rv * (w - xh * jnp.mean(w * xh, axis=-1, keepdims=True))
    return [dx, dx, jnp.sum(dv * xh, axis=0, keepdims=True)]


def mm_rms_bwd(name, pairs, x, r, gain, dres):
    t, d = x.shape
    return mm_nn(name, t, d, pairs, 1, _rms_bwd_epi, [F32, BF16], tiled=[x, dres], cols=[r], rowv=[gain], sums=[(1, d)])


def even_fwd(x, w, need_out):
    t = x.shape[0]
    proj, hn, r = mm_nn("e_in_f", t, IN_WIDTH, [(x, w["e_w_in_t"], 0, "t")], 1, _first, [F32], norm_gain=w["e_norm"])
    hc = conv_fwd("e_conv_f", proj, w["e_conv_w"], w["e_conv_b"])
    need_out(hc)
    x1, cat = even_out_fwd("e_out_f", proj, hc, x, w["e_gmlp_w"], w["e_gmlp_b"], w["e_conv_ln_g"], w["e_conv_ln_b"],
                           w["e_w_out"])
    return x1, (x, hn, r, proj, cat, hc)


def even_bwd_mixers(dxb, saved, w):
    x, hn, r, proj, cat, hc = saved
    t = x.shape[0]
    g_w_out = mm_tn("e_out_w", cat, dxb)
    dab, g_gw, g_gb = gmlp_bwd("e_gmlp_b", proj, dxb, w["e_w_out"], w["e_gmlp_w"], w["e_gmlp_b"])
    dhc, g_lg, g_lb = ln_silu_bwd("e_ln_b", hc, dxb, w["e_w_out"], w["e_conv_ln_g"], w["e_conv_ln_b"])
    dba, dbg, g_cw, g_cb = conv_bwd("e_conv_b", proj, dhc, w["e_conv_w"])
    g_w_in_t = jnp.concatenate([mm_tn("e_in_w0", dab, hn), mm_tn("e_in_w1", dba, hn), mm_tn("e_in_w2", dbg, hn)], axis=0)
    grads = dict(e_w_in_t=g_w_in_t, e_gmlp_w=g_gw[None], e_gmlp_b=g_gb.reshape(1, A_GROUPS, GMLP_BLOCK),
                 e_conv_w=g_cw[None], e_conv_b=g_cb, e_conv_ln_g=g_lg, e_conv_ln_b=g_lb, e_w_out=g_w_out)
    return (dab, dba, dbg), grads


def even_bwd_input(dx, dproj, saved, w):
    x, _, r = saved[:3]
    dab, dba, dbg = dproj
    w_in_t = w["e_w_in_t"]
    return mm_rms_bwd("e_in_b", [(dab, (w_in_t, 0), 0), (dba, (w_in_t, 2), 0), (dbg, (w_in_t, 3), 0)], x, r, w["e_norm"], dx)


def s5_setup(w, anchor=None):
    def rows(v):
        return v.transpose(0, 2, 1).reshape(_RP)

    log_dt = w["o_log_dt"].reshape(C_GROUPS, 1)
    if anchor is not None:
        log_dt = log_dt + anchor
    lam = (w["o_lam_re"], w["o_lam_im"], log_dt, rows(w["o_b_re"]), rows(w["o_b_im"]))
    a, bbr, bbi = s5_discretise("o_s5_zoh", *lam)
    c_re, c_im = w["o_c_re"], w["o_c_im"]
    pw, qw, mb, mc, mct = s5_operands("o_s5_ops", a.reshape(2, N_STATE), bbr, bbi, c_re.reshape(_RP), c_im.reshape(_RP),
                                      c_re.transpose(2, 0, 1).reshape(C_STATE, C_WIDTH),
                                      c_im.transpose(2, 0, 1).reshape(C_STATE, C_WIDTH))
    return dict(lam=lam, pw=pw, qw=qw, mb=mb, mc=mc, mct=mct, mbt=mb.T)


def odd_fwd(x, w, consts):
    x1, hn, r, u, gy, y, xs, xsb, o1, o2 = s5_fwd("o_s5_f", x, w["o_norm"], w["o_w_in"], consts["mb"], consts["mc"],
                                                  consts["pw"], w["o_d"], w["o_w_out_t"])
    return x1, (x, hn, r, u, gy, y, xs, xsb, o1, o2)


def odd_bwd(dx, dxb, saved, w, consts):
    x, hn, r, u, gy, y, xs, xsb, o1, o2 = saved
    t = x.shape[0]

    def gate_bwd(dv, a, b, wv):
        a = a.astype(F32)
        sg = _sigmoid(b.astype(F32))
        do12 = jnp.concatenate([dv * sg, dv * a * sg * (1.0 - sg)], axis=1).astype(BF16)
        return [do12, jnp.dot(do12, wv, preferred_element_type=F32)], []

    do12, dgy = rows_call("o_out_b", gate_bwd, [dx, o1, o2], [w["o_w_out_t"]], [(2 * D_MODEL, BF16), (C_WIDTH, F32)], [])
    g_w_out_t = mm_tn("o_out_w", do12, gy)
    du, dyb, lamb, da8, g_d = s5_bwd("o_s5_b", dgy, y, u, xs, consts["mct"], consts["mbt"], consts["qw"], w["o_d"])
    d_mb, d_mc = s5_block_grads("o_s5_w", u, lamb, xsb, dyb)
    da = jnp.sum(da8, axis=1).reshape((2,) + _GP)
    g_lr, g_li, g_dt, g_btr, g_bti, g_cr, g_ci = s5_param_grads("o_s5_pg", d_mb, d_mc, da, *consts["lam"])

    def states_first(v):
        return v.reshape(C_GROUPS, C_GROUP_CH, C_STATE).transpose(0, 2, 1)[None]

    g_w_in = mm_tn("o_in_w", hn, du)
    dx0, dx0b, g_norm = mm_rms_bwd("o_in_b", [(du, w["o_w_in"], 0, "t")], x, r, w["o_norm"], dx)
    grads = dict(o_norm=g_norm, o_w_in=g_w_in, o_lam_re=g_lr[None], o_lam_im=g_li[None], o_log_dt=g_dt.reshape(1, C_GROUPS),
                 o_b_re=states_first(g_btr), o_b_im=states_first(g_bti),
                 o_c_re=g_cr.reshape((1, C_GROUPS, C_GROUP_CH, C_STATE)), o_c_im=g_ci.reshape((1, C_GROUPS, C_GROUP_CH, C_STATE)),
                 o_d=g_d, o_w_out_t=g_w_out_t)
    return dx0, dx0b, grads


def ca_fwd(i, x, mem, w):
    t, m = x.shape[0], mem.shape[0]
    k, v, mn, rm = mm_nn(f"ca{i}_kv_f", m, D_MODEL, [(mem, w["ca_wk"][i], 0), (mem, w["ca_wv"][i], 1)], 2,
                         lambda accs: [accs[0], accs[1]], [BF16, BF16], norm_gain=w["ca_mem_norm"][i:i + 1])
    x1, xn, r, q, o = attn_fwd(f"ca{i}_attn_f", x, w["ca_norm"][i:i + 1], w["ca_wq"][i], k, v, w["ca_wo"][i])
    return x1, (x, xn, r, mn, rm, q, k, v, o)


def ca_bwd(i, dx, dxb, saved, mem, w):
    x, xn, r, mn, rm, q, k, v, o = saved
    t, m = x.shape[0], mem.shape[0]
    g_wo = mm_tn(f"ca{i}_o_w", o, dxb)
    dx0, dx0b, dq, dk, dv, g_norm = attn_bwd(f"ca{i}_attn_b", dx, dxb, x, r, w["ca_norm"][i:i + 1], q, k, v,
                                             w["ca_wq"][i], w["ca_wo"][i])
    g_wq = mm_tn(f"ca{i}_q_w", xn, dq)
    g_wk = mm_tn(f"ca{i}_k_w", mn, dk)
    g_wv = mm_tn(f"ca{i}_v_w", mn, dv)
    (dmn,) = mm_nn(f"ca{i}_kv_b", m, D_MODEL, [(dk, w["ca_wk"][i], 0, "t"), (dv, w["ca_wv"][i], 0, "t")], 1, _first, [F32])
    g_mnorm = rms_bwd_gain_only(f"ca{i}_mnorm_b", dmn, mem, rm)
    return dx0, dx0b, dict(ca_norm=g_norm, ca_mem_norm=g_mnorm, ca_wq=g_wq, ca_wk=g_wk, ca_wv=g_wv, ca_wo=g_wo)


FFN_ROWS = 512
FFN_CHUNK = 256


def _whole(a):
    return pl.BlockSpec(a.shape, lambda i: (0,) * a.ndim, pipeline_mode=pl.Buffered(1))


def ffn_fused_fwd(name, x, gain, wg_t, wu_t, wd, target=None, final_gain=None):
    t, d = x.shape
    hid = wd.shape[0]
    tm = _tile(t, FFN_ROWS)
    last = target is not None
    n_main = 4 if last else 1

    def body(*refs):
        x_ref, g_ref, wg_ref, wu_ref, wd_ref = refs[:5]
        rest = refs[5:]
        if last:
            tgt_ref, fg_ref = rest[:2]
            rest = rest[2:]
        main, (xn_ref, r_ref, dgate_ref, dup_ref, h_ref) = rest[:n_main], rest[n_main:]
        xv = x_ref[...]
        rv = lax.rsqrt(jnp.mean(xv * xv, axis=-1, keepdims=True) + EPS)
        xn = (xv * rv * g_ref[...]).astype(BF16)
        xn_ref[...] = xn
        r_ref[...] = rv
        for j in range(hid // FFN_CHUNK):
            cs = slice(j * FFN_CHUNK, (j + 1) * FFN_CHUNK)
            g = lax.dot_general(xn, wg_ref[cs, :], _NT, preferred_element_type=F32)
            u = lax.dot_general(xn, wu_ref[cs, :], _NT, preferred_element_type=F32)
            s = _sigmoid(g)
            silu = g * s
            dgate_ref[:, cs] = (u * (s + silu * (1.0 - s))).astype(BF16)
            dup_ref[:, cs] = silu.astype(BF16)
            h_ref[:, cs] = (silu * u).astype(BF16)
        acc = jnp.dot(h_ref[...], wd_ref[...], preferred_element_type=F32)
        if not last:
            main[0][...] = xv + acc
        else:
            dx, _, dgain, part = _final_loss_epi([acc], xv, tgt_ref[...], fg_ref[...])

            @pl.when(pl.program_id(0) == 0)
            def _():
                main[2][...] = jnp.zeros(main[2].shape, F32)
                main[3][...] = jnp.zeros(main[3].shape, F32)
            main[0][...] = dx
            main[1][...] = dx.astype(BF16)
            main[2][...] += dgain
            main[3][...] += part

    rows = pl.BlockSpec((tm, d), lambda i: (i, 0))
    wide = pl.BlockSpec((tm, hid), lambda i: (i, 0))
    col = pl.BlockSpec((tm, 1), lambda i: (i, 0))
    ins, in_specs = [x, gain, wg_t, wu_t, wd], [rows, _whole(gain), _whole(wg_t), _whole(wu_t), _whole(wd)]
    if last:
        ins += [target, final_gain]
        in_specs += [rows, _whole(final_gain)]
        out_specs = [rows, rows, pl.BlockSpec((1, d), lambda i: (0, 0)), pl.BlockSpec((1, 1), lambda i: (0, 0))]
        out_shape = [S((t, d), F32), S((t, d), BF16), S((1, d), F32), S((1, 1), F32)]
    else:
        out_specs, out_shape = [rows], [S((t, d), F32)]
    out_specs += [rows, col, wide, wide, wide]
    out_shape += [S((t, d), BF16), S((t, 1), F32)] + [S((t, hid), BF16)] * 3
    outs = pl.pallas_call(body, grid=(t // tm,), in_specs=in_specs, out_specs=out_specs, out_shape=out_shape,
                          compiler_params=_cp("arbitrary" if last else "parallel"), name=name)(*ins)
    return (tuple(outs[:4]) if last else outs[0]), outs[n_main:]


def ffn_fused_bwd(name, dx, dxb, x, r, gain, dgate, dup, wg_t, wu_t, wd):
    t, d = x.shape
    hid = wd.shape[0]
    tm = _tile(t, FFN_ROWS // 2)

    def body(dx_ref, dxb_ref, x_ref, r_ref, g_ref, dgate_ref, dup_ref, wg_ref, wu_ref, wd_ref,
             dxo_ref, dxbo_ref, dg_ref, du_ref, dgain_ref):
        @pl.when(pl.program_id(0) == 0)
        def _():
            dgain_ref[...] = jnp.zeros(dgain_ref.shape, F32)

        dxb = dxb_ref[...]
        for j in range(hid // FFN_CHUNK):
            cs = slice(j * FFN_CHUNK, (j + 1) * FFN_CHUNK)
            dh = lax.dot_general(dxb, wd_ref[cs, :], _NT, preferred_element_type=F32)
            dg_ref[:, cs] = (dh * dgate_ref[:, cs].astype(F32)).astype(BF16)
            du_ref[:, cs] = (dh * dup_ref[:, cs].astype(F32)).astype(BF16)
        dxn = (jnp.dot(dg_ref[...], wg_ref[...], preferred_element_type=F32)
               + jnp.dot(du_ref[...], wu_ref[...], preferred_element_type=F32))
        dxo, _, dgain = _rms_bwd_epi([dxn], x_ref[...], dx_ref[...], r_ref[...], g_ref[...])
        dxo_ref[...] = dxo
        dxbo_ref[...] = dxo.astype(BF16)
        dgain_ref[...] += dgain

    rows = pl.BlockSpec((tm, d), lambda i: (i, 0))
    wide = pl.BlockSpec((tm, hid), lambda i: (i, 0))
    col = pl.BlockSpec((tm, 1), lambda i: (i, 0))
    return pl.pallas_call(
        body, grid=(t // tm,),
        in_specs=[rows, rows, rows, col, _whole(gain), wide, wide, _whole(wg_t), _whole(wu_t), _whole(wd)],
        out_specs=[rows, rows, wide, wide, pl.BlockSpec((1, d), lambda i: (0, 0))],
        out_shape=[S((t, d), F32), S((t, d), BF16), S((t, hid), BF16), S((t, hid), BF16), S((1, d), F32)],
        compiler_params=_cp("arbitrary"), name=name)(dx, dxb, x, r, gain, dgate, dup, wg_t, wu_t, wd)


def ffn_fwd(i, x, w, target=None):
    out, (xn, r, dgate, dup, h) = ffn_fused_fwd(f"ffn{i}_f", x, w["ffn_norm"][i:i + 1], w["ffn_w_gate_t"][i],
                                                w["ffn_w_up_t"][i], w["ffn_w_down"][i], target,
                                                None if target is None else w["final_norm"])
    return out, (x, xn, r, dgate, dup, h)


def ffn_bwd(i, dx, dxb, saved, w):
    x, xn, r, dgate, dup, h = saved
    g_wd = mm_tn(f"ffn{i}_down_w", h, dxb)
    dx0, dx0b, dg, du, g_norm = ffn_fused_bwd(f"ffn{i}_b", dx, dxb, x, r, w["ffn_norm"][i:i + 1], dgate, dup,
                                              w["ffn_w_gate_t"][i], w["ffn_w_up_t"][i], w["ffn_w_down"][i])
    g_wg_t = mm_tn(f"ffn{i}_gate_w", dg, xn)
    g_wu_t = mm_tn(f"ffn{i}_up_w", du, xn)
    return dx0, dx0b, dict(ffn_norm=g_norm, ffn_w_gate_t=g_wg_t, ffn_w_up_t=g_wu_t, ffn_w_down=g_wd)


def local_step(x, mem, target, w, fetch=None, on_grads=None, anchor=None):
    consts = s5_setup(w, anchor)

    def need(stage, after):
        if fetch is not None:
            for k, v in fetch(stage, after).items():
                if isinstance(k, tuple):
                    w.setdefault(k[0], {})[k[1]] = v
                else:
                    w[k] = v

    need(0, consts["pw"])
    def early(stage, after):
        if fetch is not None:
            fetch(stage, after, True)

    def before_out(after):
        need(1, after)
        early(2, after)

    x1, s_e = even_fwd(x, w, before_out)
    need(2, x1)
    x2, s_c0 = ca_fwd(0, x1, mem, w)
    need(3, x2)
    x3, s_f0 = ffn_fwd(0, x2, w)
    need(4, x3)
    x4, s_o = odd_fwd(x3, w, consts)
    need(5, x4)
    early(6, x4)
    x5, s_c1 = ca_fwd(1, x4, mem, w)
    need(6, x5)
    (dx, dxb, g_final, loss), s_f1 = ffn_fwd(1, x5, w, target)

    def emit(stage, carry, plain, layered=None, layer=0):
        if on_grads is None:
            return carry
        out = dict(plain)
        out.update({(k, layer): v for k, v in (layered or {}).items()})
        return on_grads(stage, out, list(carry))

    dx, dxb, g_f1 = ffn_bwd(1, dx, dxb, s_f1, w)
    dx, dxb = emit(0, (dx, dxb), {}, g_f1, 1)
    dx, dxb, g_c1 = ca_bwd(1, dx, dxb, s_c1, mem, w)
    dx, dxb, g_o = odd_bwd(dx, dxb, s_o, w, consts)
    dx, dxb = emit(1, (dx, dxb), g_o, g_c1, 1)
    dx, dxb, g_f0 = ffn_bwd(0, dx, dxb, s_f0, w)
    dx, dxb = emit(2, (dx, dxb), {}, g_f0, 0)
    dx, dxb, g_c0 = ca_bwd(0, dx, dxb, s_c0, mem, w)
    dx, dxb = emit(3, (dx, dxb), {}, g_c0, 0)
    dproj, g_e = even_bwd_mixers(dxb, s_e, w)
    dproj = emit(4, dproj, {**g_e, "o_norm": g_o["o_norm"], "o_d": g_o["o_d"]})
    dx, dxb, g_e["e_norm"] = even_bwd_input(dx, dproj, s_e, w)

    grads = dict(g_e)
    grads.update(g_o)
    for g0, g1 in ((g_c0, g_c1), (g_f0, g_f1)):
        for k in g0:
            grads[k] = jnp.concatenate([g0[k], g1[k]], axis=0) if k.endswith("norm") else (g0[k], g1[k])
    grads["final_norm"] = g_final
    return loss, dx, grads


def _group(axes):
    pos = {a: lax.axis_index(a) for a in ("x", "y", "c")}
    me = 0
    for a in axes:
        me = me * 2 + pos[a]
    peers = []
    for mask in range(1, 2 ** len(axes)):
        peer = dict(pos)
        for bit, a in enumerate(axes):
            if (mask >> (len(axes) - 1 - bit)) & 1:
                peer[a] = 1 - pos[a]
        idx = 0
        for a in axes:
            idx = idx * 2 + peer[a]
        peers.append((idx, (peer["x"], peer["y"], peer["c"])))
    return me, peers


def _sibling():
    x, y, c = lax.axis_index("x"), lax.axis_index("y"), lax.axis_index("c")
    return c, (x, y, 1 - c)


_HBM =pl.BlockSpec(memory_space=pltpu.HBM)
_SEM = pl.BlockSpec(memory_space=pltpu.SEMAPHORE)
_EFFECT = pltpu.SideEffectType.DATAFLOW_SIDE_EFFECTING


def _gather_peers(direct):
    chip, _ = _group(("x", "y"))
    core = lax.axis_index("c")
    if direct:
        _, peers = _group(_ALL)
        return chip, core, [(idx // 2, idx % 2, dev) for idx, dev in peers]
    _, peers = _group(("x", "y"))
    return chip, core, [(idx, core, dev) for idx, dev in peers]


def gather_ici_start(name, groups, direct):
    flat = [b for g in groups for b in g]
    sizes = [len(g) for g in groups]
    k_ops, n_g = len(flat), len(groups)
    lands = [lax.empty((4, 2) + tuple(b.shape), b.dtype) for b in flat]
    fan = [N_DEV - 1 if d else 3 for d in direct]

    def body(*refs):
        src, land = refs[:k_ops], refs[k_ops:2 * k_ops]
        sems = refs[2 * k_ops:2 * k_ops + 3 * n_g]
        token = refs[-1]
        i = 0
        for g in range(n_g):
            send, recv, loc = sems[3 * g:3 * g + 3]
            chip, core, peers = _gather_peers(direct[g])
            for j in range(sizes[g]):
                pltpu.make_async_copy(src[i], land[i].at[chip, core], loc.at[j]).start()
                for k, (_, _, dev) in enumerate(peers):
                    s = fan[g] * j + k
                    pltpu.make_async_remote_copy(src_ref=src[i], dst_ref=land[i].at[chip, core], send_sem=send.at[s],
                                                 recv_sem=recv.at[s], device_id=dev, device_id_type=MESH).start()
                i += 1
        token[...] = jnp.zeros(token.shape, token.dtype)

    sem_shapes = []
    for s, f in zip(sizes, fan):
        sem_shapes += [pltpu.SemaphoreType.DMA((f * s,)), pltpu.SemaphoreType.DMA((f * s,)), pltpu.SemaphoreType.DMA((s,))]
    thru = [pltpu.HBM(a.shape, a.dtype) for a in flat + lands]
    outs = pl.pallas_call(
        body, name=name, out_shape=tuple(sem_shapes) + tuple(thru) + (S((8, LANES), F32),),
        in_specs=[_HBM] * (2 * k_ops), out_specs=[_SEM] * (3 * n_g) + [_HBM] * (2 * k_ops) + [pl.BlockSpec(memory_space=pltpu.VMEM)],
        input_output_aliases={i: 3 * n_g + i for i in range(2 * k_ops)},
        compiler_params=pltpu.CompilerParams(has_side_effects=_EFFECT),
    )(*[pltpu.with_memory_space_constraint(a, pltpu.HBM) for a in flat + lands])
    sems = [tuple(outs[3 * g:3 * g + 3]) for g in range(n_g)]
    srcs_thru, lands_thru, off = [], [], 3 * n_g
    for s in sizes:
        srcs_thru.append(list(outs[off:off + s]))
        off += s
    for s in sizes:
        lands_thru.append(list(outs[off:off + s]))
        off += s
    return sems, srcs_thru, lands_thru, outs[-1]


def gather_ici_wait(name, srcs, lands, sems, after, direct=False):
    n = len(srcs)

    def body(*refs):
        src, land = refs[:n], refs[n:2 * n]
        send, recv, loc = refs[2 * n:2 * n + 3]
        chip, core, peers = _gather_peers(direct)
        for j in range(n):
            for k, (pchip, pcore, dev) in enumerate(peers):
                s = len(peers) * j + k
                cp = pltpu.make_async_remote_copy(src_ref=src[j], dst_ref=land[j].at[pchip, pcore], send_sem=send.at[s],
                                                  recv_sem=recv.at[s], device_id=dev, device_id_type=MESH)
                cp.wait_send()
                cp.wait_recv()
            pltpu.make_async_copy(src[j], land[j].at[chip, core], loc.at[j]).wait()

    outs = pl.pallas_call(
        body, name=name, out_shape=tuple(pltpu.HBM(a.shape, a.dtype) for a in list(srcs) + list(lands)),
        in_specs=[_HBM] * (2 * n) + [_SEM] * 3 + [ANY], out_specs=[_HBM] * (2 * n),
        input_output_aliases={i: i for i in range(2 * n)},
        compiler_params=pltpu.CompilerParams(has_side_effects=_EFFECT),
    )(*srcs, *lands, *sems, after)
    return list(outs[n:])


def gather_d2d(name, bufs):
    k_ops = len(bufs)

    def body(*refs):
        in_refs, out_refs = refs[:k_ops], refs[k_ops:2 * k_ops]
        send_sems, recv_sems = refs[2 * k_ops:]
        core, sib = _sibling()
        sent, landed = [], []
        for i in range(k_ops):
            cp = pltpu.make_async_remote_copy(src_ref=in_refs[i].at[:, core], dst_ref=out_refs[i].at[:, core],
                                              send_sem=send_sems.at[i], recv_sem=recv_sems.at[i], device_id=sib, device_id_type=MESH)
            cp.start()
            sent.append(cp)
            landed.append(pltpu.make_async_remote_copy(src_ref=in_refs[i].at[:, core], dst_ref=out_refs[i].at[:, 1 - core],
                                                       send_sem=send_sems.at[i], recv_sem=recv_sems.at[i],
                                                       device_id=sib, device_id_type=MESH))
        for cp in landed:
            cp.wait_recv()
        for cp in sent:
            cp.wait_send()

    return pl.pallas_call(
        body, in_specs=[ANY] * k_ops, out_specs=[ANY] * k_ops, out_shape=[S(b.shape, b.dtype) for b in bufs],
        input_output_aliases={i: i for i in range(k_ops)},
        scratch_shapes=[pltpu.SemaphoreType.DMA((k_ops,)), pltpu.SemaphoreType.DMA((k_ops,))],
        name=name)(*bufs)


def gather_d2d_start(name, bufs):
    k_ops = len(bufs)

    def body(*refs):
        in_refs = refs[:k_ops]
        send, recv = refs[k_ops], refs[k_ops + 1]
        core, sib = _sibling()
        for i in range(k_ops):
            pltpu.make_async_remote_copy(src_ref=in_refs[i].at[:, core], dst_ref=in_refs[i].at[:, core], send_sem=send.at[i],
                                         recv_sem=recv.at[i], device_id=sib, device_id_type=MESH).start()

    outs = pl.pallas_call(
        body, name=name,
        out_shape=(pltpu.SemaphoreType.DMA((k_ops,)), pltpu.SemaphoreType.DMA((k_ops,))) + tuple(pltpu.HBM(b.shape, b.dtype) for b in bufs),
        in_specs=[_HBM] * k_ops, out_specs=[_SEM, _SEM] + [_HBM] * k_ops,
        input_output_aliases={i: 2 + i for i in range(k_ops)},
        compiler_params=pltpu.CompilerParams(has_side_effects=_EFFECT),
    )(*[pltpu.with_memory_space_constraint(b, pltpu.HBM) for b in bufs])
    return (outs[0], outs[1]), list(outs[2:])


def gather_d2d_wait(name, bufs, sems, after):
    k_ops = len(bufs)

    def body(*refs):
        in_refs = refs[:k_ops]
        send, recv = refs[k_ops], refs[k_ops + 1]
        core, sib = _sibling()
        for i in range(k_ops):
            cp = pltpu.make_async_remote_copy(src_ref=in_refs[i].at[:, core], dst_ref=in_refs[i].at[:, 1 - core], send_sem=send.at[i],
                                              recv_sem=recv.at[i], device_id=sib, device_id_type=MESH)
            cp.wait_send()
            cp.wait_recv()

    outs = pl.pallas_call(
        body, name=name, out_shape=tuple(pltpu.HBM(b.shape, b.dtype) for b in bufs),
        in_specs=[_HBM] * k_ops + [_SEM, _SEM, ANY], out_specs=[_HBM] * k_ops,
        input_output_aliases={i: i for i in range(k_ops)},
        compiler_params=pltpu.CompilerParams(has_side_effects=_EFFECT),
    )(*bufs, sems[0], sems[1], after)
    return list(outs)


_ALL = ("x", "y", "c")


def _unit_rows(units):
    offs, off = [], 0
    for u in units:
        offs.append(off)
        off += u.shape[1]
    return offs, off


def scatter_start(name, units, carry):
    n_u, n_c = len(units), len(carry)
    offs, rows = _unit_rows(units)
    land = lax.empty((N_DEV, rows) + tuple(units[0].shape[2:]), units[0].dtype)
    fan = N_DEV - 1

    def body(*refs):
        u_refs, land_ref = refs[:n_u], refs[n_u]
        send, recv, loc = refs[n_u + 1 + n_c:n_u + 4 + n_c]
        me, peers = _group(_ALL)
        for j in range(n_u):
            rs = pl.ds(offs[j], units[j].shape[1])
            pltpu.make_async_copy(u_refs[j].at[me], land_ref.at[me, rs], loc.at[j]).start()
            for k, (idx, dev) in enumerate(peers):
                pltpu.make_async_remote_copy(src_ref=u_refs[j].at[idx], dst_ref=land_ref.at[me, rs], send_sem=send.at[fan * j + k],
                                             recv_sem=recv.at[fan * j + k], device_id=dev, device_id_type=MESH).start()

    thru = list(units) + [land] + list(carry)
    outs = pl.pallas_call(
        body, name=name,
        out_shape=(pltpu.SemaphoreType.DMA((fan * n_u,)), pltpu.SemaphoreType.DMA((fan * n_u,)), pltpu.SemaphoreType.DMA((n_u,)))
        + tuple(pltpu.HBM(a.shape, a.dtype) for a in thru),
        in_specs=[_HBM] * len(thru), out_specs=[_SEM] * 3 + [_HBM] * len(thru),
        input_output_aliases={i: 3 + i for i in range(len(thru))},
        compiler_params=pltpu.CompilerParams(has_side_effects=_EFFECT),
    )(*[pltpu.with_memory_space_constraint(a, pltpu.HBM) for a in thru])
    return tuple(outs[:3]), list(outs[3:3 + n_u]), outs[3 + n_u], list(outs[4 + n_u:])


def scatter_wait(name, units, land, sems, after):
    n_u = len(units)
    offs, _ = _unit_rows(units)
    fan = N_DEV - 1

    def body(*refs):
        u_refs, land_ref = refs[:n_u], refs[n_u]
        send, recv, loc = refs[n_u + 1:n_u + 4]
        me, peers = _group(_ALL)
        for j in range(n_u):
            rs = pl.ds(offs[j], units[j].shape[1])
            for k, (idx, dev) in enumerate(peers):
                cp = pltpu.make_async_remote_copy(src_ref=u_refs[j].at[idx], dst_ref=land_ref.at[idx, rs], send_sem=send.at[fan * j + k],
                                                  recv_sem=recv.at[fan * j + k], device_id=dev, device_id_type=MESH)
                cp.wait_send()
                cp.wait_recv()
            pltpu.make_async_copy(u_refs[j].at[me], land_ref.at[me, rs], loc.at[j]).wait()

    thru = list(units) + [land]
    outs = pl.pallas_call(
        body, name=name, out_shape=tuple(pltpu.HBM(a.shape, a.dtype) for a in thru),
        in_specs=[_HBM] * len(thru) + [_SEM] * 3 + [ANY], out_specs=[_HBM] * len(thru),
        input_output_aliases={i: i for i in range(len(thru))},
        compiler_params=pltpu.CompilerParams(has_side_effects=_EFFECT),
    )(*thru, *sems, after)
    return outs[n_u]


def _row_tile(rows, cap=512):
    return next(t for t in range(cap - cap % 16, 0, -16) if rows % t == 0)


def sum_shares(name, recv, me):
    n, rows, c = recv.shape
    tr = _row_tile(rows)

    def body(me_ref, *refs):
        acc = refs[0][...].astype(F32)
        for r in refs[1:n]:
            acc = acc + r[...].astype(F32)
        refs[n][...] = acc

    def slot(mask):
        return pl.BlockSpec((None, tr, c), lambda i, me, mask=mask: (jnp.bitwise_xor(me[0], mask), i, 0))

    spec = pltpu.PrefetchScalarGridSpec(
        num_scalar_prefetch=1, grid=(rows // tr,), in_specs=[slot(k) for k in range(n)],
        out_specs=pl.BlockSpec((tr, c), lambda i, me: (i, 0)))
    return pl.pallas_call(body, grid_spec=spec, out_shape=S((rows, c), F32),
                          compiler_params=_cp("parallel"), name=name)(me, *([recv] * n))


def sum_slots(name, slots):
    n, r, c = slots.shape

    def body(s_ref, o_ref):
        acc = s_ref[0]
        for j in range(1, n):
            acc = acc + s_ref[j]
        o_ref[...] = acc

    return pl.pallas_call(body, out_shape=S((r, c), F32), compiler_params=pltpu.CompilerParams(vmem_limit_bytes=VMEM_LIMIT),
                          name=name)(slots)


def adamw_units(name, pieces, transposed, w, m, v):
    n_l, k, n = w.shape
    tk = _tile(k, 512) if transposed else k
    p_rows = n if transposed else k
    arrs = [p[0] if isinstance(p, tuple) else p for p in pieces]
    offs = [p[1] // p_rows if isinstance(p, tuple) else 0 for p in pieces]
    assert all(not isinstance(p, tuple) or p[1] % p_rows == 0 for p in pieces)
    c1 = 1.0 - ADAM_B1 ** ADAM_STEP
    c2 = 1.0 - ADAM_B2 ** ADAM_STEP

    def body(*refs):
        p_refs, (w_ref, m_ref, v_ref, g_ref, d_ref, m2_ref, v2_ref) = refs[:n_l], refs[n_l:]
        gv = p_refs[0][...]
        for j in range(1, n_l):
            gv = jnp.where(pl.program_id(0) == j, p_refs[j][...], gv)
        if transposed:
            gv = gv.T
        m2 = ADAM_B1 * m_ref[...] + (1.0 - ADAM_B1) * gv
        v2 = ADAM_B2 * v_ref[...] + (1.0 - ADAM_B2) * (gv * gv)
        g_ref[...] = gv
        m2_ref[...] = m2
        v2_ref[...] = v2
        d_ref[...] = -ADAM_LR * ((m2 / c1) / (jnp.sqrt(v2 / c2) + ADAM_EPS) + ADAM_WD * w_ref[...])

    def piece(o):
        if transposed:
            return pl.BlockSpec((n, tk), lambda l, i, o=o: (o, i))
        return pl.BlockSpec((k, n), lambda l, i, o=o: (o, 0))

    blk = pl.BlockSpec((None, tk, n), lambda l, i: (l, i, 0))
    return tuple(pl.pallas_call(body, grid=(n_l, k // tk), in_specs=[piece(o) for o in offs] + [blk] * 3, out_specs=[blk] * 4,
                                out_shape=[S(w.shape, F32)] * 4, compiler_params=_cp("parallel", "parallel"),
                                name=name)(*arrs, w, m, v))


def adamw_native(name, g, w, m, v, tr=512):
    shape = w.shape
    cols = shape[-1]
    rows = w.size // cols
    tr = _tile(rows, tr) if rows % 8 == 0 else rows
    c1 = 1.0 - ADAM_B1 ** ADAM_STEP
    c2 = 1.0 - ADAM_B2 ** ADAM_STEP

    def body(g_ref, w_ref, m_ref, v_ref, d_ref, m2_ref, v2_ref):
        gv = g_ref[...]
        m2 = ADAM_B1 * m_ref[...] + (1.0 - ADAM_B1) * gv
        v2 = ADAM_B2 * v_ref[...] + (1.0 - ADAM_B2) * (gv * gv)
        m2_ref[...] = m2
        v2_ref[...] = v2
        d_ref[...] = -ADAM_LR * ((m2 / c1) / (jnp.sqrt(v2 / c2) + ADAM_EPS) + ADAM_WD * w_ref[...])

    row = pl.BlockSpec((tr, cols), lambda i: (i, 0))
    outs = pl.pallas_call(body, grid=(rows // tr,), in_specs=[row] * 4, out_specs=[row] * 3,
                          out_shape=[S((rows, cols), F32)] * 3, compiler_params=_cp("parallel"),
                          name=name)(*[a.reshape(rows, cols) for a in (g, w, m, v)])
    return tuple(o.reshape(shape) for o in outs)


_REPLICATED = ("e_norm", "e_gmlp_w", "e_gmlp_b", "e_conv_b", "e_conv_ln_g", "e_conv_ln_b", "o_lam_re", "o_lam_im", "o_log_dt",
               "o_b_re", "o_b_im", "o_c_re", "o_c_im", "ca_norm", "ca_mem_norm", "ffn_norm", "final_norm")
_ORDER = ("e_norm", "e_w_in", "e_gmlp_w", "e_gmlp_b", "e_conv_w", "e_conv_b", "e_conv_ln_g", "e_conv_ln_b", "e_w_out",
          "o_norm", "o_w_in", "o_lam_re", "o_lam_im", "o_log_dt", "o_b_re", "o_b_im", "o_c_re", "o_c_im", "o_d", "o_w_out",
          "ca_norm", "ca_mem_norm", "ca_wq", "ca_wk", "ca_wv", "ca_wo", "ffn_norm", "ffn_w_gate", "ffn_w_up", "ffn_w_down",
          "final_norm")


def _rows128(a, multiple=8):
    flat = a.reshape(-1)
    rows = -(-flat.shape[0] // (LANES * multiple)) * multiple
    return jnp.pad(flat, (0, rows * LANES - flat.shape[0])).reshape(rows, LANES)


def _shard(full, axis):
    s = full.shape
    return jnp.moveaxis(full.reshape(s[:axis] + (N_DEV, s[axis] // N_DEV) + s[axis + 1:]), axis, 0)


_UNITS = (("e_w_in", 0, True), ("e_w_out", 0, False), ("o_w_in", 0, False), ("o_w_out", 0, True),
          *[(n, i, False) for n in ("ca_wq", "ca_wk", "ca_wv", "ca_wo") for i in (0, 1)],
          *[(n, i, tr) for n, tr in (("ffn_w_gate", True), ("ffn_w_up", True), ("ffn_w_down", False)) for i in (0, 1)])
_LAYERED = ("ca_wq", "ca_wk", "ca_wv", "ca_wo", "ffn_w_gate", "ffn_w_up", "ffn_w_down")
_SMALL_SHARDED = (("e_conv_w", 2), ("o_norm", 1), ("o_d", 1))
RS_ROW = 1024


def _unit_key(name, tr):
    return name + "_t" if tr else name


def _stage_of(name, layer):
    if name.startswith("e_"):
        return 0 if name == "e_w_in" else 1
    if name.startswith("o_"):
        return 4
    if name.startswith("ca_"):
        return 2 if layer == 0 else 5
    return 3 if layer == 0 else 6


GATHER_STAGES = 7
GATHER_DIRECT = (False, False, False, False, True, True, False)


def weight_fetcher(local):
    groups, meta = [[] for _ in range(GATHER_STAGES)], [[] for _ in range(GATHER_STAGES)]
    for name, layer, tr in _UNITS:
        blk = local[name][layer]
        st = _stage_of(name, layer)
        groups[st].append(_bf(blk.T if tr else blk))
        meta[st].append((name, layer, tr))
    small = jnp.concatenate([local[name].reshape(-1) for name, _ in _SMALL_SHARDED])
    groups[0].append(_rows128(small))
    direct = list(GATHER_DIRECT)
    sems, srcs, lands, token = gather_ici_start("ag_w_start", groups, direct)

    early = {}

    def fetch(stage, after, start_only=False):
        if start_only:
            landed = gather_ici_wait(f"ag_w_wait{stage}", srcs[stage], lands[stage], sems[stage], after, direct[stage])
            early[stage] = gather_d2d_start(f"ag_w_d2d{stage}_start", landed)
            return {}
        if stage in early:
            bufs = gather_d2d_wait(f"ag_w_d2d{stage}_wait", early[stage][1], early[stage][0], after)
        else:
            bufs = gather_ici_wait(f"ag_w_wait{stage}", srcs[stage], lands[stage], sems[stage], after, direct[stage])
            if not direct[stage]:
                bufs = gather_d2d(f"ag_w_d2d{stage}", bufs)
        got = {}
        for (name, layer, tr), blk, buf in zip(meta[stage], groups[stage], bufs):
            arr = buf.reshape((N_DEV * blk.shape[0],) + tuple(blk.shape[1:]))
            if name in _LAYERED:
                got[(_unit_key(name, tr), layer)] = arr
            else:
                got[_unit_key(name, tr)] = arr
        if stage == 0:
            flat = bufs[-1].reshape(N_DEV, -1)
            off = 0
            for name, axis in _SMALL_SHARDED:
                blk = local[name]
                seg = flat[:, off:off + blk.size].reshape((N_DEV,) + blk.shape)
                off += blk.size
                seg = jnp.moveaxis(seg, 0, axis)
                got[name] = seg.reshape(seg.shape[:axis] + (-1,) + seg.shape[axis + 2:])
            got["e_conv_w"] = got["e_conv_w"][0]
        return got

    return fetch, token


def _grad_stage_of(name, layer):
    if name.startswith("e_"):
        return 4
    if name.startswith("o_"):
        return 1
    if name.startswith("ca_"):
        return 3 if layer == 0 else 1
    return 2 if layer == 0 else 0


GRAD_STAGES = 5
SMALL_ROWS = 16


def gradient_reducer(local, mom, var):
    me = (4 * lax.axis_index("x") + 2 * lax.axis_index("y") + lax.axis_index("c")).astype(jnp.int32).reshape(1)
    pending = []

    def start(stage, grads, carry):
        def grad_of(unit):
            key = _unit_key(unit[0], unit[2])
            return grads[(key, unit[1])] if unit[0] in _LAYERED else grads[key]

        units = sorted([u for u in _UNITS if _grad_stage_of(u[0], u[1]) == stage], key=lambda u: -grad_of(u).size)
        parts, spans = [], []
        for unit in units:
            g = grad_of(unit)
            part = g.reshape(N_DEV, -1, RS_ROW)
            spans.append((part.shape[1], g.shape[0] // N_DEV, g.shape[1]))
            parts.append(part)
        if stage == GRAD_STAGES - 1:
            small = jnp.concatenate([_shard(grads[name], axis).reshape(N_DEV, -1) for name, axis in _SMALL_SHARDED], axis=1)
            small = jnp.pad(small, ((0, 0), (0, SMALL_ROWS * RS_ROW - small.shape[1])))
            parts.append(small.astype(BF16).reshape(N_DEV, SMALL_ROWS, RS_ROW))
        sems, sent, land, carry = scatter_start(f"rs_start{stage}", parts, carry)
        pending.append((stage, units, spans, sems, sent, land))
        return carry

    def finish(after):
        res, per_layer, small_flat = {}, {}, None
        for stage, units, spans, sems, sent, land in pending:
            land = scatter_wait(f"rs_wait{stage}", sent, land, sems, after)
            total = sum_shares(f"rs_sum{stage}", land, me)
            off = 0
            for (name, layer, tr), (rows, r, c) in zip(units, spans):
                piece = (total, off) if c == RS_ROW else total[off:off + rows].reshape(r, c)
                per_layer.setdefault(name, {})[layer] = (piece, tr)
                off += rows
            if stage == GRAD_STAGES - 1:
                small_flat = total[off:off + SMALL_ROWS].reshape(-1)
        for name, by_layer in per_layer.items():
            pieces = [by_layer[i][0] for i in sorted(by_layer)]
            res[name] = adamw_units("adamw_" + name, pieces, by_layer[0][1], local[name], mom[name], var[name])
        off = 0
        for name, _ in _SMALL_SHARDED:
            blk = local[name]
            g = small_flat[off:off + blk.size].reshape(blk.shape)
            off += blk.size
            res[name] = (g,) + adamw_native("adamw_" + name, g, blk, mom[name], var[name])
        return res

    return start, finish


def replicated_start(grads, loss):
    pack = jnp.concatenate([_rows128(grads[name]) for name in _REPLICATED] + [_rows128(loss)], axis=0)
    sems, srcs, lands, token = gather_ici_start("ag_g_start", [[pack]], [False])
    return sems[0], srcs[0], lands[0], token


def replicated_finish(handle, after, w, mom, var):
    sems, srcs, lands, _ = handle
    (buf,) = gather_d2d("ag_g_d2d", gather_ici_wait("ag_g_wait", srcs, lands, sems, after))
    rows = srcs[0].shape[0]
    total = sum_slots("ag_g_sum", buf.reshape(N_DEV, rows, LANES))
    res, off = {}, 0
    for name in _REPLICATED:
        n = w[name].size
        nr = -(-n // (LANES * 8)) * 8
        g = total[off:off + nr].reshape(-1)[:n].reshape(w[name].shape)
        off += nr
        res[name] = (g,) + adamw_native("adamw_" + name, g, w[name], mom[name], var[name])
    return res, total[off, 0]


def kernel(x, mem, e_norm, e_w_in, e_gmlp_w, e_gmlp_b, e_conv_w, e_conv_b, e_conv_ln_g, e_conv_ln_b, e_w_out, o_norm, o_w_in, o_lam_re, o_lam_im, o_log_dt, o_b_re, o_b_im, o_c_re, o_c_im, o_d, o_w_out, ca_norm, ca_mem_norm, ca_wq, ca_wk, ca_wv, ca_wo, ffn_norm, ffn_w_gate, ffn_w_up, ffn_w_down, final_norm, loss_target, m_e_norm, m_e_w_in, m_e_gmlp_w, m_e_gmlp_b, m_e_conv_w, m_e_conv_b, m_e_conv_ln_g, m_e_conv_ln_b, m_e_w_out, m_o_norm, m_o_w_in, m_o_lam_re, m_o_lam_im, m_o_log_dt, m_o_b_re, m_o_b_im, m_o_c_re, m_o_c_im, m_o_d, m_o_w_out, m_ca_norm, m_ca_mem_norm, m_ca_wq, m_ca_wk, m_ca_wv, m_ca_wo, m_ffn_norm, m_ffn_w_gate, m_ffn_w_up, m_ffn_w_down, m_final_norm, v_e_norm, v_e_w_in, v_e_gmlp_w, v_e_gmlp_b, v_e_conv_w, v_e_conv_b, v_e_conv_ln_g, v_e_conv_ln_b, v_e_w_out, v_o_norm, v_o_w_in, v_o_lam_re, v_o_lam_im, v_o_log_dt, v_o_b_re, v_o_b_im, v_o_c_re, v_o_c_im, v_o_d, v_o_w_out, v_ca_norm, v_ca_mem_norm, v_ca_wq, v_ca_wk, v_ca_wv, v_ca_wo, v_ffn_norm, v_ffn_w_gate, v_ffn_w_up, v_ffn_w_down, v_final_norm):
    given = dict(locals())
    local = {k: given[k] for k in _ORDER}
    mom = {k: given["m_" + k] for k in _ORDER}
    var = {k: given["v_" + k] for k in _ORDER}

    w = {}
    w.update({
        "e_norm": e_norm, "e_gmlp_w": e_gmlp_w[0], "e_gmlp_b": e_gmlp_b.reshape(A_GROUPS, GMLP_BLOCK, 1),
        "e_conv_b": e_conv_b, "e_conv_ln_g": e_conv_ln_g, "e_conv_ln_b": e_conv_ln_b,
        "o_lam_re": o_lam_re[0], "o_lam_im": o_lam_im[0], "o_log_dt": o_log_dt[0], "o_b_re": o_b_re[0], "o_b_im": o_b_im[0],
        "o_c_re": o_c_re[0], "o_c_im": o_c_im[0], "ca_norm": ca_norm, "ca_mem_norm": ca_mem_norm, "ffn_norm": ffn_norm,
        "final_norm": final_norm.reshape(1, D_MODEL),
    })
    start_reduce, finish_reduce = gradient_reducer(local, mom, var)
    fetch, token = weight_fetcher(local)
    loss_part, grad_x, grads = local_step(x[0], mem[0], loss_target[0], w, fetch, start_reduce, token[0:1, 0:1])
    grads["final_norm"] = grads["final_norm"].reshape(D_MODEL)

    handle = replicated_start(grads, loss_part)
    res = finish_reduce(handle[3])
    rep, loss = replicated_finish(handle, res["ffn_w_down"][1], local, mom, var)
    res.update(rep)
    return (loss, grad_x[None], *[res[k][0] for k in _ORDER], *[res[k][1] for k in _ORDER],
            *[res[k][2] for k in _ORDER], *[res[k][3] for k in _ORDER])
```

```python
import jax
import jax.numpy as jnp
from jax import lax
from jax.experimental import pallas as pl
from jax.experimental.pallas import tpu as pltpu

F32 = jnp.float32
BF16 = jnp.bfloat16
S = jax.ShapeDtypeStruct

D_MODEL = 1024
A_WIDTH = 512
A_GROUPS = 4
GMLP_BLOCK = 128
CHUNK = 64
B_WIDTH = 512
IN_WIDTH = 2 * A_WIDTH + 2 * B_WIDTH
CONV_WIDTH = 31
CONV_PAD = 32
C_WIDTH = 512
C_GROUP_CH = 16
C_GROUPS = 32
C_STATE = 64
N_STATE = C_GROUPS * C_STATE
CA_HEADS = 4
CA_HEAD_DIM = 256
EPS = 1e-6
ADAM_LR = 0.001
ADAM_B1 = 0.9
ADAM_B2 = 0.999
ADAM_EPS = 1e-08
ADAM_WD = 0.01
ADAM_STEP = 10
N_DEV = 8
LANES = 128
VMEM_LIMIT = 56 << 20
VMEM_BUDGET = 40 << 20
MM_TN_RESIDENT = 8 << 20
MESH = pl.DeviceIdType.MESH
ANY = pl.BlockSpec(memory_space=pl.ANY)


def _cp(*sem):
    return pltpu.CompilerParams(dimension_semantics=sem, vmem_limit_bytes=VMEM_LIMIT)


def _tile(n, pref):
    t = pref
    while n % t:
        t //= 2
    return t


def _bf(v):
    return v if v.dtype == BF16 else v.astype(BF16)


def _sigmoid(x):
    return 1.0 / (1.0 + jnp.exp(-x))


_GC = 0.7978845608028654


def _gelu(x):
    return 0.5 * x * (1.0 + jnp.tanh(_GC * (x + 0.044715 * x * x * x)))


def _gelu_grad(x):
    x2 = x * x
    t = jnp.tanh(_GC * (x + 0.044715 * x * x2))
    return 0.5 * (1.0 + t) + 0.5 * x * (1.0 - t * t) * _GC * (1.0 + 3.0 * 0.044715 * x2)


def _tspec(entry, tm):
    if isinstance(entry, tuple):
        arr, cb, width = entry
        return arr, pl.BlockSpec((tm, width), lambda i, cb=cb: (i, cb))
    return entry, pl.BlockSpec((tm, entry.shape[1]), lambda i: (i, 0))


def rows_call(name, fn, tiled, full, outs, accs, tm=256):
    pairs = [_tspec(e, tm) for e in tiled]
    arrs = [p[0] for p in pairs]
    rows = arrs[0].shape[0]
    tm = _tile(rows, tm)
    pairs = [_tspec(e, tm) for e in tiled]
    n_in = len(tiled) + len(full)
    n_out = len(outs)

    def body(*refs):
        vals = [r[...] for r in refs[:n_in]]
        o_refs = refs[n_in:n_in + n_out]
        a_refs = refs[n_in + n_out:]
        ov, av = fn(*vals)
        for r, v in zip(o_refs, ov):
            r[...] = v.astype(r.dtype)
        if a_refs:
            @pl.when(pl.program_id(0) == 0)
            def _():
                for r in a_refs:
                    r[...] = jnp.zeros(r.shape, r.dtype)
            for r, v in zip(a_refs, av):
                r[...] += v

    in_specs = [p[1] for p in pairs] + [pl.BlockSpec(a.shape, lambda i, nd=a.ndim: (0,) * nd) for a in full]
    out_specs = [pl.BlockSpec((tm, c), lambda i: (i, 0)) for c, _ in outs]
    out_specs += [pl.BlockSpec(s, lambda i, nd=len(s): (0,) * nd) for s in accs]
    out_shape = [S((rows, c), dt) for c, dt in outs] + [S(s, F32) for s in accs]
    return pl.pallas_call(body, grid=(rows // tm,), in_specs=in_specs, out_specs=out_specs, out_shape=out_shape,
                          compiler_params=_cp("arbitrary"), name=name)(*arrs, *full)


def mm_nn(name, m, n, pairs, n_acc, epi, outs, tiled=(), cols=(), rowv=(), sums=(), norm_gain=None):
    a_ops, a_slot, b_arrs, b_specs, idx, trans = [], [], [], [], [], []
    fixed = 0
    for pair in pairs:
        a, b, k = pair[:3]
        bt = len(pair) > 3
        arr, cb, kdim = a if isinstance(a, tuple) else (a, 0, a.shape[1])
        key = (id(arr), cb, kdim)
        if key not in [o[0] for o in a_ops]:
            a_ops.append((key, arr, cb, kdim))
        a_slot.append([o[0] for o in a_ops].index(key))
        b_arr, off = b if isinstance(b, tuple) else (b, 0)
        b_arrs.append(b_arr)
        if bt:
            assert off % n == 0 and b_arr.shape[1] == kdim
            b_specs.append(pl.BlockSpec((n, kdim), lambda i, o=off // n: (o, 0), pipeline_mode=pl.Buffered(1)))
        else:
            assert b_arr.shape[1] == n
            b_specs.append(pl.BlockSpec((kdim, n), lambda i, o=off: (o, 0), pipeline_mode=pl.Buffered(1)))
        fixed += kdim * n * b_arr.dtype.itemsize
        idx.append(k)
        trans.append(bt)
    per_row = sum(2 * kdim * arr.dtype.itemsize for _, arr, _, kdim in a_ops)
    per_row += sum(2 * n * t.dtype.itemsize for t in tiled) + sum(2 * n * jnp.dtype(dt).itemsize for dt in outs)
    cn = n if sums or cols else (512 if n % 512 == 0 else 256)
    per_row += (n_acc + 3) * cn * 4
    tm = next((t for t in (1024, 512, 256, 128) if m % t == 0 and fixed + t * per_row <= VMEM_BUDGET), _tile(m, 128))
    n_a, n_p, n_t = len(a_ops), len(pairs), len(tiled)
    n_in = n_a + n_p + n_t + len(cols) + len(rowv)
    normed = norm_gain is not None
    o0 = n_in + normed

    def body(*refs):
        a_vals = [None if normed and i == 0 else _bf(r[...]) for i, r in enumerate(refs[:n_a])]
        if normed:
            xv = refs[0][...]
            rv = lax.rsqrt(jnp.mean(xv * xv, axis=-1, keepdims=True) + EPS)
            a_vals[0] = (xv * rv * refs[n_in][...]).astype(BF16)
            refs[o0 + len(outs)][...] = a_vals[0]
            refs[o0 + len(outs) + 1][...] = rv
        for j in range(n // cn):
            cs = slice(j * cn, (j + 1) * cn)
            accs = [None] * n_acc
            for p in range(n_p):
                av, b_ref = a_vals[a_slot[p]], refs[n_a + p]
                if trans[p]:
                    d = lax.dot_general(av, _bf(b_ref[cs, :]), (((1,), (1,)), ((), ())), preferred_element_type=F32)
                else:
                    d = jnp.dot(av, _bf(b_ref[:, cs]), preferred_element_type=F32)
                accs[idx[p]] = d if accs[idx[p]] is None else accs[idx[p]] + d
            extra = [r[:, cs] for r in refs[n_a + n_p:n_a + n_p + n_t]] + [r[...] for r in refs[n_a + n_p + n_t:n_in - len(rowv)]]
            extra += [r[:, cs] for r in refs[n_in - len(rowv):n_in]]
            ov = epi(accs, *extra)
            for r, v in zip(refs[o0:o0 + len(outs)], ov):
                r[:, cs] = v.astype(r.dtype)
        sv = ov[len(outs):]
        if sums:
            s_refs = refs[o0 + len(outs) + 2 * normed:]

            @pl.when(pl.program_id(0) == 0)
            def _():
                for r in s_refs:
                    r[...] = jnp.zeros(r.shape, r.dtype)
            for r, v in zip(s_refs, sv):
                r[...] += v

    in_specs = [pl.BlockSpec((tm, kdim), lambda i, cb=cb: (i, cb)) for _, _, cb, kdim in a_ops] + b_specs
    in_specs += [pl.BlockSpec((tm, n), lambda i: (i, 0)) for _ in tiled]
    in_specs += [pl.BlockSpec((tm, 1), lambda i: (i, 0)) for _ in cols]
    in_specs += [pl.BlockSpec((1, n), lambda i: (0, 0)) for _ in rowv]
    out_specs = [pl.BlockSpec((tm, n), lambda i: (i, 0)) for _ in outs]
    out_shape = [S((m, n), dt) for dt in outs]
    gain = []
    if normed:
        k0 = a_ops[0][3]
        gain = [norm_gain]
        in_specs.append(pl.BlockSpec((1, k0), lambda i: (0, 0)))
        out_specs += [pl.BlockSpec((tm, k0), lambda i: (i, 0)), pl.BlockSpec((tm, 1), lambda i: (i, 0))]
        out_shape += [S((m, k0), BF16), S((m, 1), F32)]
    out_specs += [pl.BlockSpec(s, lambda i, nd=len(s): (0,) * nd) for s in sums]
    out_shape += [S(s, F32) for s in sums]
    return pl.pallas_call(body, grid=(m // tm,), in_specs=in_specs, out_specs=out_specs, out_shape=out_shape,
                          compiler_params=_cp("arbitrary" if sums else "parallel"),
                          name=name)(*[o[1] for o in a_ops], *b_arrs, *tiled, *cols, *rowv, *gain)


def mm_tn(name, a, b, out_dtype=BF16):
    if isinstance(a, tuple):
        a_arr, a_cb, m = a
    else:
        a_arr, a_cb, m = a, None, a.shape[1]
    if isinstance(b, tuple):
        b_arr, b_cb, n = b
    else:
        b_arr, b_cb, n = b, None, b.shape[1]
    t = a_arr.shape[0]
    whole_b = t * n * b_arr.dtype.itemsize <= MM_TN_RESIDENT and b_cb is None
    tn = n if whole_b else _tile(n, 512)
    tm = _tile(m, 512 if t * 512 * a_arr.dtype.itemsize * 2 + t * tn * b_arr.dtype.itemsize * 2 <= VMEM_BUDGET else 256)
    a_off = 0 if a_cb is None else a_cb * (m // tm)
    b_off = 0 if b_cb is None else b_cb * (n // tn)

    def body(a_ref, b_ref, o_ref):
        o_ref[...] = lax.dot_general(_bf(a_ref[...]), _bf(b_ref[...]), (((0,), (0,)), ((), ())),
                                     preferred_element_type=F32).astype(o_ref.dtype)

    if whole_b:
        b_spec = pl.BlockSpec((t, n), lambda i, j: (0, 0), pipeline_mode=pl.Buffered(1))
    else:
        b_spec = pl.BlockSpec((t, tn), lambda i, j: (0, j + b_off))
    return pl.pallas_call(
        body, grid=(m // tm, n // tn),
        in_specs=[pl.BlockSpec((t, tm), lambda i, j: (0, i + a_off)), b_spec],
        out_specs=pl.BlockSpec((tm, tn), lambda i, j: (i, j)), out_shape=S((m, n), out_dtype),
        compiler_params=_cp("parallel", "parallel"), name=name)(a_arr, b_arr)


def rms_bwd_gain_only(name, dxn, x, r):
    def fn(dv, xv, rv):
        return [], [jnp.sum(dv * xv * rv, axis=0, keepdims=True)]
    return rows_call(name, fn, [dxn, x, r], [], [], [(1, x.shape[1])])[0]


def _final_loss_epi(accs, res, tv, g):
    xv = res + accs[0]
    d = xv.shape[-1]
    r = lax.rsqrt(jnp.mean(xv * xv, axis=-1, keepdims=True) + EPS)
    xh = xv * r
    err = xh * g - tv
    dy = err * (1.0 / d)
    w = dy * g
    dx = r * (w - xh * jnp.mean(w * xh, axis=-1, keepdims=True))
    part = jnp.sum(jnp.sum(err * err, axis=-1, keepdims=True), axis=0, keepdims=True) * (0.5 / d)
    return [dx, dx, jnp.sum(dy * xh, axis=0, keepdims=True), part]


def _gmlp_mask():
    row = lax.broadcasted_iota(jnp.int32, (GMLP_BLOCK, GMLP_BLOCK), 0) // CHUNK
    col = lax.broadcasted_iota(jnp.int32, (GMLP_BLOCK, GMLP_BLOCK), 1) // CHUNK
    return col <= row


def _ln_plain(v):
    mu = jnp.mean(v, axis=-1, keepdims=True)
    vc = v - mu
    rstd = lax.rsqrt(jnp.mean(vc * vc, axis=-1, keepdims=True) + EPS)
    return vc * rstd, rstd


def even_out_fwd(name, proj, hc, x, w, b, ln_g, ln_b, w_out, tm=512):
    t, d = x.shape
    tm = _tile(t, tm)

    def body(au_ref, av_ref, hc_ref, x_ref, w_ref, b_ref, lg_ref, lb_ref, wo_ref, x1_ref, cat_ref):
        mask = _gmlp_mask()
        u = _gelu(au_ref[...])
        vn, _ = _ln_plain(_gelu(av_ref[...]))
        vnb = _bf(vn)
        for g in range(A_GROUPS):
            wg = _bf(jnp.where(mask, w_ref[g], 0.0))
            cs = slice(g * GMLP_BLOCK, (g + 1) * GMLP_BLOCK)
            for n in range(tm // GMLP_BLOCK):
                rs = slice(n * GMLP_BLOCK, (n + 1) * GMLP_BLOCK)
                sg = jnp.dot(wg, vnb[rs, cs], preferred_element_type=F32) + b_ref[g]
                cat_ref[rs, cs] = (u[rs, cs] * sg).astype(cat_ref.dtype)
        y, _ = _ln_plain(hc_ref[...])
        z = y * lg_ref[...] + lb_ref[...]
        cat_ref[:, A_WIDTH:] = (z * _sigmoid(z)).astype(cat_ref.dtype)
        x1_ref[...] = x_ref[...] + jnp.dot(cat_ref[...], wo_ref[...], preferred_element_type=F32)

    half = pl.BlockSpec((tm, A_WIDTH), lambda i: (i, 0))
    return pl.pallas_call(
        body, grid=(t // tm,),
        in_specs=[half, pl.BlockSpec((tm, A_WIDTH), lambda i: (i, 1)), half, pl.BlockSpec((tm, d), lambda i: (i, 0)),
                  _whole(w), _whole(b), _whole(ln_g), _whole(ln_b), _whole(w_out)],
        out_specs=[pl.BlockSpec((tm, d), lambda i: (i, 0)), pl.BlockSpec((tm, A_WIDTH + B_WIDTH), lambda i: (i, 0))],
        out_shape=[S((t, d), F32), S((t, A_WIDTH + B_WIDTH), BF16)],
        compiler_params=_cp("parallel"), name=name)(proj, proj, hc, x, w, b, ln_g, ln_b, w_out)


def gmlp_bwd(name, proj, dxb, w_out, w, b, tm=512):
    t = proj.shape[0]
    tm = _tile(t, tm)

    def body(au_ref, av_ref, dx_ref, wo_ref, w_ref, b_ref, dp_ref, dw_ref, db_ref):
        @pl.when(pl.program_id(0) == 0)
        def _():
            dw_ref[...] = jnp.zeros(dw_ref.shape, F32)
            db_ref[...] = jnp.zeros(db_ref.shape, F32)

        mask = _gmlp_mask()
        au = au_ref[...]
        av = av_ref[...]
        u = _gelu(au)
        vn, rstd = _ln_plain(_gelu(av))
        vnb = _bf(vn)
        dout = lax.dot_general(dx_ref[...], wo_ref[0:A_WIDTH, :], _NT, preferred_element_type=F32)
        dvn_cols = []
        for g in range(A_GROUPS):
            wm = jnp.where(mask, w_ref[g], 0.0)
            wg = _bf(wm)
            wgt = _bf(wm.T)
            cs = slice(g * GMLP_BLOCK, (g + 1) * GMLP_BLOCK)
            dwg = jnp.zeros((GMLP_BLOCK, GMLP_BLOCK), F32)
            dbg = jnp.zeros((GMLP_BLOCK, 1), F32)
            dvn_rows = []
            for n in range(tm // GMLP_BLOCK):
                rs = slice(n * GMLP_BLOCK, (n + 1) * GMLP_BLOCK)
                sg = jnp.dot(wg, vnb[rs, cs], preferred_element_type=F32) + b_ref[g]
                dp_ref[rs, cs] = (dout[rs, cs] * sg * _gelu_grad(au[rs, cs])).astype(dp_ref.dtype)
                dsg = dout[rs, cs] * u[rs, cs]
                dsgb = _bf(dsg)
                dbg = dbg + jnp.sum(dsg, axis=1, keepdims=True)
                dwg = dwg + lax.dot_general(dsgb, vnb[rs, cs], (((1,), (1,)), ((), ())), preferred_element_type=F32)
                dvn_rows.append(jnp.dot(wgt, dsgb, preferred_element_type=F32))
            dw_ref[g] += jnp.where(mask, dwg, 0.0)
            db_ref[g] += dbg
            dvn_cols.append(jnp.concatenate(dvn_rows, axis=0))
        dvn = jnp.concatenate(dvn_cols, axis=1)
        dv = rstd * (dvn - jnp.mean(dvn, axis=-1, keepdims=True) - vn * jnp.mean(dvn * vn, axis=-1, keepdims=True))
        dp_ref[:, A_WIDTH:] = (dv * _gelu_grad(av)).astype(dp_ref.dtype)

    return pl.pallas_call(
        body, grid=(t // tm,),
        in_specs=[pl.BlockSpec((tm, A_WIDTH), lambda i: (i, 0)), pl.BlockSpec((tm, A_WIDTH), lambda i: (i, 1)),
                  pl.BlockSpec((tm, dxb.shape[1]), lambda i: (i, 0)), pl.BlockSpec(w_out.shape, lambda i: (0, 0)),
                  pl.BlockSpec(w.shape, lambda i: (0, 0, 0)), pl.BlockSpec(b.shape, lambda i: (0, 0, 0))],
        out_specs=[pl.BlockSpec((tm, 2 * A_WIDTH), lambda i: (i, 0)),
                   pl.BlockSpec(w.shape, lambda i: (0, 0, 0)), pl.BlockSpec(b.shape, lambda i: (0, 0, 0))],
        out_shape=[S((t, 2 * A_WIDTH), BF16), S(w.shape, F32), S(b.shape, F32)],
        compiler_params=_cp("arbitrary"), name=name)(proj, proj, dxb, w_out, w, b)


CONV_ROWS = 256
CONV_ROWS_BWD = 64


def conv_fwd(name, proj, w, cb):
    t = proj.shape[0]
    tc = LANES
    rows = _tile(t, CONV_ROWS)
    a_cb, g_cb = 2 * A_WIDTH // tc, (2 * A_WIDTH + B_WIDTH) // tc

    def body(a_ref, g_ref, w_ref, cb_ref, o_ref, hpad):
        hpad[0:CONV_PAD, :] = jnp.zeros((CONV_PAD, tc), F32)

        def fill(i, _):
            r0 = pl.multiple_of(i * rows, rows)
            hpad[pl.ds(CONV_PAD + r0, rows), :] = a_ref[pl.ds(r0, rows), :] * _sigmoid(g_ref[pl.ds(r0, rows), :])
            return 0
        lax.fori_loop(0, t // rows, fill, 0)

        def conv(i, _):
            r0 = pl.multiple_of(i * rows, rows)
            win = hpad[pl.ds(r0, rows + CONV_PAD), :]
            acc = jnp.zeros((rows, tc), F32) + cb_ref[...]
            for b in range(SUB):
                wb = win if b == 0 else pltpu.roll(win, b, 0)
                for a in range(CONV_PAD // SUB):
                    k = CONV_WIDTH - 1 - (SUB * a + b)
                    if k >= 0:
                        lo = CONV_PAD - SUB * a
                        acc = acc + wb[lo:lo + rows, :] * w_ref[k:k + 1, :]
            o_ref[pl.ds(r0, rows), :] = acc
            return 0
        lax.fori_loop(0, t // rows, conv, 0)

    return pl.pallas_call(
        body, grid=(B_WIDTH // tc,),
        in_specs=[pl.BlockSpec((t, tc), lambda j: (0, a_cb + j)), pl.BlockSpec((t, tc), lambda j: (0, g_cb + j)),
                  pl.BlockSpec((CONV_WIDTH, tc), lambda j: (0, j)), pl.BlockSpec((1, tc), lambda j: (0, j))],
        out_specs=pl.BlockSpec((t, tc), lambda j: (0, j)), out_shape=S((t, B_WIDTH), F32),
        scratch_shapes=[pltpu.VMEM((t + CONV_PAD, tc), F32)],
        compiler_params=_cp("parallel"), name=name)(proj, proj, w, cb)


def conv_bwd(name, proj, dhc, w):
    t = proj.shape[0]
    tc = LANES
    rows = _tile(t, CONV_ROWS_BWD)
    a_cb, g_cb = 2 * A_WIDTH // tc, (2 * A_WIDTH + B_WIDTH) // tc
    win_rows = rows + CONV_PAD

    def body(a_ref, g_ref, d_ref, w_ref, da_ref, dg_ref, dw_ref, dcb_ref, hpad, dpad, dwacc):
        hpad[0:CONV_PAD, :] = jnp.zeros((CONV_PAD, tc), F32)
        dpad[t:t + CONV_PAD, :] = jnp.zeros((CONV_PAD, tc), F32)
        dwacc[...] = jnp.zeros(dwacc.shape, F32)

        def fill(i, _):
            r0 = pl.multiple_of(i * rows, rows)
            hpad[pl.ds(CONV_PAD + r0, rows), :] = a_ref[pl.ds(r0, rows), :] * _sigmoid(g_ref[pl.ds(r0, rows), :])
            dpad[pl.ds(r0, rows), :] = d_ref[pl.ds(r0, rows), :]
            return 0
        lax.fori_loop(0, t // rows, fill, 0)

        def step(i, dcb):
            r0 = pl.multiple_of(i * rows, rows)
            hwin = hpad[pl.ds(r0, win_rows), :]
            dwin = dpad[pl.ds(r0, win_rows), :]
            dchunk = dwin[:rows, :]
            dh = jnp.zeros((rows, tc), F32)
            for b in range(SUB):
                hb = hwin if b == 0 else pltpu.roll(hwin, b, 0)
                db = dwin if b == 0 else pltpu.roll(dwin, win_rows - b, 0)
                for a in range(CONV_PAD // SUB):
                    k = CONV_WIDTH - 1 - (SUB * a + b)
                    if k >= 0:
                        dh = dh + db[SUB * a:SUB * a + rows, :] * w_ref[k:k + 1, :]
                        lo = CONV_PAD - SUB * a
                        prod = dchunk * hb[lo:lo + rows, :]
                        dwacc[k] += jnp.sum(prod.reshape(rows // 8, 8, tc), axis=0)
            a = a_ref[pl.ds(r0, rows), :]
            sg = _sigmoid(g_ref[pl.ds(r0, rows), :])
            da_ref[pl.ds(r0, rows), :] = (dh * sg).astype(da_ref.dtype)
            dg_ref[pl.ds(r0, rows), :] = (dh * a * sg * (1.0 - sg)).astype(dg_ref.dtype)
            return dcb + jnp.sum(dchunk, axis=0, keepdims=True)
        dcb = lax.fori_loop(0, t // rows, step, jnp.zeros((1, tc), F32))
        dcb_ref[...] = dcb
        for k in range(CONV_WIDTH):
            dw_ref[k:k + 1, :] = jnp.sum(dwacc[k], axis=0, keepdims=True)

    return pl.pallas_call(
        body, grid=(B_WIDTH // tc,),
        in_specs=[pl.BlockSpec((t, tc), lambda j: (0, a_cb + j)), pl.BlockSpec((t, tc), lambda j: (0, g_cb + j)),
                  pl.BlockSpec((t, tc), lambda j: (0, j)), pl.BlockSpec((CONV_WIDTH, tc), lambda j: (0, j))],
        out_specs=[pl.BlockSpec((t, tc), lambda j: (0, j)), pl.BlockSpec((t, tc), lambda j: (0, j)),
                   pl.BlockSpec((CONV_WIDTH, tc), lambda j: (0, j)), pl.BlockSpec((1, tc), lambda j: (0, j))],
        out_shape=[S((t, B_WIDTH), BF16), S((t, B_WIDTH), BF16), S((CONV_WIDTH, B_WIDTH), F32), S((1, B_WIDTH), F32)],
        scratch_shapes=[pltpu.VMEM((t + CONV_PAD, tc), F32), pltpu.VMEM((t + CONV_PAD, tc), F32),
                        pltpu.VMEM((CONV_WIDTH, 8, tc), F32)],
        compiler_params=_cp("parallel"), name=name)(proj, proj, dhc, w)


def ln_silu_bwd(name, hc, dxb, w_out, g, b):
    c = hc.shape[1]

    def fn(h, dxv, wv, gv, bv):
        dout = lax.dot_general(dxv, wv[A_WIDTH:, :], _NT, preferred_element_type=F32)
        y, rstd = _ln_plain(h)
        z = y * gv + bv
        s = _sigmoid(z)
        dz = dout * s * (1.0 + z * (1.0 - s))
        dyv = dz * gv
        dh = rstd * (dyv - jnp.mean(dyv, axis=-1, keepdims=True) - y * jnp.mean(dyv * y, axis=-1, keepdims=True))
        return [dh], [jnp.sum(dz * y, axis=0, keepdims=True), jnp.sum(dz, axis=0, keepdims=True)]

    return rows_call(name, fn, [hc, dxb], [w_out, g, b], [(c, F32)], [(1, c), (1, c)])


_NT = (((1,), (1,)), ((), ()))
_TN = (((0,), (0,)), ((), ()))


def attn_fwd(name, x, gain, wq, k, v, wo, tm=512):
    t, d = x.shape
    m = k.shape[0]
    tm = _tile(t, tm)
    scale = CA_HEAD_DIM ** -0.5

    def body(x_ref, g_ref, wq_ref, k_ref, v_ref, wo_ref, x1_ref, xn_ref, r_ref, q_ref, o_ref):
        xv = x_ref[...]
        rv = lax.rsqrt(jnp.mean(xv * xv, axis=-1, keepdims=True) + EPS)
        xn = (xv * rv * g_ref[...]).astype(BF16)
        xn_ref[...] = xn
        r_ref[...] = rv
        q_ref[...] = jnp.dot(xn, wq_ref[...], preferred_element_type=F32).astype(BF16)
        for h in range(CA_HEADS):
            cs = slice(h * CA_HEAD_DIM, (h + 1) * CA_HEAD_DIM)
            s = lax.dot_general(q_ref[:, cs], k_ref[:, cs], _NT, preferred_element_type=F32) * scale
            e = jnp.exp(s - jnp.max(s, axis=-1, keepdims=True))
            p = e / jnp.sum(e, axis=-1, keepdims=True)
            o_ref[:, cs] = jnp.dot(_bf(p), v_ref[:, cs], preferred_element_type=F32).astype(o_ref.dtype)
        x1_ref[...] = xv + jnp.dot(o_ref[...], wo_ref[...], preferred_element_type=F32)

    def whole(a):
        return pl.BlockSpec(a.shape, lambda i: (0, 0), pipeline_mode=pl.Buffered(1))

    rows = pl.BlockSpec((tm, d), lambda i: (i, 0))
    col = pl.BlockSpec((tm, 1), lambda i: (i, 0))
    return pl.pallas_call(
        body, grid=(t // tm,),
        in_specs=[rows, whole(gain), whole(wq), whole(k), whole(v), whole(wo)],
        out_specs=[rows, rows, col, rows, rows],
        out_shape=[S((t, d), F32), S((t, d), BF16), S((t, 1), F32), S((t, d), BF16), S((t, d), BF16)],
        compiler_params=_cp("parallel"), name=name)(x, gain, wq, k, v, wo)


def attn_bwd(name, dx, dxb, x, r, gain, q, k, v, wq, wo, tm=512):
    t, d = q.shape
    m = k.shape[0]
    tm = _tile(t, tm)
    scale = CA_HEAD_DIM ** -0.5

    def body(dx_ref, dxb_ref, x_ref, r_ref, g_ref, q_ref, k_ref, v_ref, wq_ref, wo_ref,
             dxo_ref, dxbo_ref, dq_ref, dk_ref, dv_ref, dg_ref, do_s):
        @pl.when(pl.program_id(0) == 0)
        def _():
            dk_ref[...] = jnp.zeros(dk_ref.shape, F32)
            dv_ref[...] = jnp.zeros(dv_ref.shape, F32)
            dg_ref[...] = jnp.zeros(dg_ref.shape, F32)

        do_s[...] = lax.dot_general(dxb_ref[...], wo_ref[...], _NT, preferred_element_type=F32).astype(BF16)
        for h in range(CA_HEADS):
            cs = slice(h * CA_HEAD_DIM, (h + 1) * CA_HEAD_DIM)
            qh, kh, vh, doh = q_ref[:, cs], k_ref[:, cs], v_ref[:, cs], do_s[:, cs]
            s = lax.dot_general(qh, kh, _NT, preferred_element_type=F32) * scale
            e = jnp.exp(s - jnp.max(s, axis=-1, keepdims=True))
            p = e / jnp.sum(e, axis=-1, keepdims=True)
            pb = _bf(p)
            dv_ref[:, cs] += lax.dot_general(pb, doh, _TN, preferred_element_type=F32)
            dp = lax.dot_general(doh, vh, _NT, preferred_element_type=F32)
            ds = _bf(p * (dp - jnp.sum(dp * p, axis=-1, keepdims=True)) * scale)
            dq_ref[:, cs] = jnp.dot(ds, kh, preferred_element_type=F32).astype(dq_ref.dtype)
            dk_ref[:, cs] += lax.dot_general(ds, qh, _TN, preferred_element_type=F32)
        dxn = lax.dot_general(dq_ref[...], wq_ref[...], _NT, preferred_element_type=F32)
        xh = x_ref[...] * r_ref[...]
        wv = dxn * g_ref[...]
        dxo = dx_ref[...] + r_ref[...] * (wv - xh * jnp.mean(wv * xh, axis=-1, keepdims=True))
        dxo_ref[...] = dxo
        dxbo_ref[...] = dxo.astype(BF16)
        dg_ref[...] += jnp.sum(dxn * xh, axis=0, keepdims=True)

    def whole(a):
        return pl.BlockSpec(a.shape, lambda i: (0, 0), pipeline_mode=pl.Buffered(1))

    rows = pl.BlockSpec((tm, d), lambda i: (i, 0))
    col = pl.BlockSpec((tm, 1), lambda i: (i, 0))
    acc = pl.BlockSpec((m, d), lambda i: (0, 0))
    return pl.pallas_call(
        body, grid=(t // tm,),
        in_specs=[rows, rows, rows, col, whole(gain), rows, whole(k), whole(v), whole(wq), whole(wo)],
        out_specs=[rows, rows, rows, acc, acc, pl.BlockSpec((1, d), lambda i: (0, 0))],
        out_shape=[S((t, d), F32), S((t, d), BF16), S((t, d), BF16), S((m, d), F32), S((m, d), F32), S((1, d), F32)],
        scratch_shapes=[pltpu.VMEM((tm, d), BF16)],
        compiler_params=_cp("arbitrary"), name=name)(dx, dxb, x, r, gain, q, k, v, wq, wo)


SUB = 8
S5_ROWS = 256


S5_BLOCKS = 4
BLOCK_CH = C_WIDTH // S5_BLOCKS
BLOCK_ST = N_STATE // S5_BLOCKS
_S5_BLOCKS = tuple((slice(BLOCK_CH * q, BLOCK_CH * (q + 1)), slice(BLOCK_ST * q, BLOCK_ST * (q + 1)),
                    slice(N_STATE + BLOCK_ST * q, N_STATE + BLOCK_ST * (q + 1))) for q in range(S5_BLOCKS))
_HI = lax.Precision.HIGHEST
_GP = (C_GROUPS, C_STATE)
_RP = (C_WIDTH, C_STATE)


def _zoh(lr, li, ldt):
    dt = jnp.exp(ldt)
    mag = jnp.exp(lr * dt)
    ar = mag * jnp.cos(li * dt)
    ai = mag * jnp.sin(li * dt)
    den = lr * lr + li * li
    qr = ((ar - 1.0) * lr + ai * li) / den
    qi = (ai * lr - (ar - 1.0) * li) / den
    return dt, ar, ai, den, qr, qi


def _per_channel(v):
    return jnp.broadcast_to(v[:, None, :], (C_GROUPS, C_GROUP_CH, C_STATE)).reshape(_RP)


def _same_group(shape, row_per_group, col_per_group):
    rows = lax.broadcasted_iota(jnp.int32, shape, 0) // row_per_group
    cols = lax.broadcasted_iota(jnp.int32, shape, 1) // col_per_group
    return rows == cols


def _spread(shape, axis):
    long = lax.broadcasted_iota(jnp.int32, shape, axis) % C_STATE
    short = lax.broadcasted_iota(jnp.int32, shape, 1 - axis)
    return long == short


def s5_discretise(name, lam_re, lam_im, log_dt, bt_re, bt_im):
    def body(lr_ref, li_ref, ldt_ref, btr_ref, bti_ref, a_ref, bbr_ref, bbi_ref):
        _, ar, ai, _, qr, qi = _zoh(lr_ref[...], li_ref[...], ldt_ref[...])
        a_ref[0] = ar
        a_ref[1] = ai
        q2r, q2i = _per_channel(qr), _per_channel(qi)
        btr, bti = btr_ref[...], bti_ref[...]
        bbr_ref[...] = q2r * btr - q2i * bti
        bbi_ref[...] = q2r * bti + q2i * btr

    return pl.pallas_call(body, out_shape=[S((2,) + _GP, F32), S(_RP, F32), S(_RP, F32)],
                          name=name)(lam_re, lam_im, log_dt, bt_re, bt_im)


def s5_operands(name, a, bbr, bbi, c2r, c2i, ctr, cti):
    ns = N_STATE

    def body(a_ref, bbr_ref, bbi_ref, c2r_ref, c2i_ref, ctr_ref, cti_ref, pw_ref, qw_ref, mb_ref, mc_ref, mct_ref):
        ar, ai = a_ref[0:1, :], a_ref[1:2, :]
        pows = [(ar, ai)]
        for _ in range(SUB - 1):
            pr, pi = pows[-1]
            pows.append((pr * ar - pi * ai, pr * ai + pi * ar))
        rows = lax.broadcasted_iota(jnp.int32, (SUB, ns), 0)

        def rows_of(v):
            return jnp.broadcast_to(v, (SUB, ns))

        for k, s in enumerate((1, 2, 4)):
            pr, pi = rows_of(pows[s - 1][0]), rows_of(pows[s - 1][1])
            pw_ref[k, 0] = jnp.where(rows >= s, pr, 0.0)
            pw_ref[k, 1] = jnp.where(rows >= s, pi, 0.0)
            qw_ref[k, 0] = jnp.where(rows + s <= SUB - 1, pr, 0.0)
            qw_ref[k, 1] = jnp.where(rows + s <= SUB - 1, -pi, 0.0)
        fr = fi = br = bi = jnp.zeros((SUB, ns), F32)
        for i in range(SUB):
            fr = jnp.where(rows == i, rows_of(pows[i][0]), fr)
            fi = jnp.where(rows == i, rows_of(pows[i][1]), fi)
            br = jnp.where(rows == i, rows_of(pows[SUB - 1 - i][0]), br)
            bi = jnp.where(rows == i, rows_of(-pows[SUB - 1 - i][1]), bi)
        pw_ref[3, 0], pw_ref[3, 1], qw_ref[3, 0], qw_ref[3, 1] = fr, fi, br, bi

        wide = _spread((C_STATE, ns), 1).astype(BF16)
        tall = _spread((ns, C_STATE), 0).astype(BF16)
        in_rows = _same_group((C_WIDTH, ns), C_GROUP_CH, C_STATE)
        in_cols = _same_group((ns, C_WIDTH), C_STATE, C_GROUP_CH)

        def across(v, sign=1.0):
            return jnp.where(in_rows, sign * jnp.dot(_bf(v), wide, preferred_element_type=F32), 0.0).astype(BF16)

        def down(vt, sign=1.0):
            return jnp.where(in_cols, sign * jnp.dot(tall, _bf(vt), preferred_element_type=F32), 0.0).astype(BF16)

        mb_ref[:, 0:ns] = across(bbr_ref[...])
        mb_ref[:, ns:2 * ns] = across(bbi_ref[...])
        mct_ref[:, 0:ns] = across(c2r_ref[...])
        mct_ref[:, ns:2 * ns] = across(c2i_ref[...], -1.0)
        mc_ref[0:ns, :] = down(ctr_ref[...])
        mc_ref[ns:2 * ns, :] = down(cti_ref[...], -1.0)

    return pl.pallas_call(
        body, out_shape=[S((4, 2, SUB, ns), F32), S((4, 2, SUB, ns), F32), S((C_WIDTH, 2 * ns), BF16),
                         S((2 * ns, C_WIDTH), BF16), S((C_WIDTH, 2 * ns), BF16)],
        compiler_params=pltpu.CompilerParams(vmem_limit_bytes=VMEM_LIMIT), name=name)(a, bbr, bbi, c2r, c2i, ctr, cti)


def s5_block_grads(name, u, lamb, xsb, dyb):
    t = u.shape[0]

    def mb_body(u_ref, lr_ref, li_ref, o_ref):
        ub = _bf(u_ref[...])
        o_ref[:, 0:BLOCK_ST] = lax.dot_general(ub, lr_ref[...], _TN, preferred_element_type=F32)
        o_ref[:, BLOCK_ST:2 * BLOCK_ST] = lax.dot_general(ub, li_ref[...], _TN, preferred_element_type=F32)

    d_mb = pl.pallas_call(
        mb_body, grid=(S5_BLOCKS,),
        in_specs=[pl.BlockSpec((t, BLOCK_CH), lambda q: (0, q)), pl.BlockSpec((t, BLOCK_ST), lambda q: (0, q)),
                  pl.BlockSpec((t, BLOCK_ST), lambda q: (0, S5_BLOCKS + q))],
        out_specs=pl.BlockSpec((BLOCK_CH, 2 * BLOCK_ST), lambda q: (q, 0)), out_shape=S((C_WIDTH, 2 * BLOCK_ST), F32),
        compiler_params=_cp("parallel"), name=name + "_b")(u, lamb, lamb)

    def mc_body(x_ref, dy_ref, o_ref):
        o_ref[...] = lax.dot_general(x_ref[...], dy_ref[...], _TN, preferred_element_type=F32)

    d_mc = pl.pallas_call(
        mc_body, grid=(2, S5_BLOCKS),
        in_specs=[pl.BlockSpec((t, BLOCK_ST), lambda p, q: (0, p * S5_BLOCKS + q)), pl.BlockSpec((t, BLOCK_CH), lambda p, q: (0, q))],
        out_specs=pl.BlockSpec((BLOCK_ST, BLOCK_CH), lambda p, q: (p * S5_BLOCKS + q, 0)),
        out_shape=S((2 * N_STATE, BLOCK_CH), F32), compiler_params=_cp("parallel", "parallel"), name=name + "_c")(xsb, dyb)
    return d_mb, d_mc


def s5_param_grads(name, d_mb, d_mc, da, lam_re, lam_im, log_dt, bt_re, bt_im):
    ns = N_STATE

    def body(dmb_ref, dmc_ref, da_ref, lr_ref, li_ref, ldt_ref, btr_ref, bti_ref,
             glr_ref, gli_ref, gdt_ref, gbr_ref, gbi_ref, gcr_ref, gci_ref):
        lr, li = lr_ref[...], li_ref[...]
        dt, ar, ai, den, qr, qi = _zoh(lr, li, ldt_ref[...])
        per_block = C_GROUPS // S5_BLOCKS
        wide = _spread((C_STATE, BLOCK_ST), 1).astype(F32)
        tall = _spread((BLOCK_ST, C_STATE), 0).astype(F32)
        rows = lax.broadcasted_iota(jnp.int32, (C_WIDTH, BLOCK_ST), 0) // C_GROUP_CH % per_block
        in_rows = rows == lax.broadcasted_iota(jnp.int32, (C_WIDTH, BLOCK_ST), 1) // C_STATE
        in_cols = _same_group((BLOCK_ST, BLOCK_CH), C_STATE, C_GROUP_CH)

        def fold_rows(v):
            return lax.dot_general(jnp.where(in_rows, v, 0.0), wide, (((1,), (1,)), ((), ())), precision=_HI,
                                   preferred_element_type=F32)

        def fold_cols(v):
            return lax.dot_general(jnp.where(in_cols, v, 0.0), tall, (((0,), (0,)), ((), ())), precision=_HI,
                                   preferred_element_type=F32)

        for cs, s_re, s_im in _S5_BLOCKS:
            gcr_ref[cs, :] = fold_cols(dmc_ref[s_re, :])
            gci_ref[cs, :] = -fold_cols(dmc_ref[s_im, :])
        gbbr = fold_rows(dmb_ref[:, 0:BLOCK_ST])
        gbbi = fold_rows(dmb_ref[:, BLOCK_ST:2 * BLOCK_ST])
        btr, bti = btr_ref[...], bti_ref[...]
        q2r, q2i = _per_channel(qr), _per_channel(qi)
        gbr_ref[...] = q2r * gbbr + q2i * gbbi
        gbi_ref[...] = q2r * gbbi - q2i * gbbr

        def per_group(v):
            return jnp.sum(v.reshape(C_GROUPS, C_GROUP_CH, C_STATE), axis=1)

        gqr = per_group(btr * gbbr + bti * gbbi)
        gqi = per_group(btr * gbbi - bti * gbbr)
        ilr, ili = lr / den, li / den
        gar = da_ref[0] + ilr * gqr - ili * gqi
        gai = da_ref[1] + ilr * gqi + ili * gqr
        sr = (qr * lr + qi * li) / den
        si = (qi * lr - qr * li) / den
        gzr = ar * gar + ai * gai
        gzi = ar * gai - ai * gar
        glr_ref[...] = -sr * gqr - si * gqi + dt * gzr
        gli_ref[...] = -sr * gqi + si * gqr + dt * gzi
        gdt_ref[...] = jnp.sum(lr * gzr + li * gzi, axis=1, keepdims=True) * dt

    return pl.pallas_call(
        body, out_shape=[S(_GP, F32), S(_GP, F32), S((C_GROUPS, 1), F32), S(_RP, F32), S(_RP, F32), S(_RP, F32), S(_RP, F32)],
        compiler_params=pltpu.CompilerParams(vmem_limit_bytes=VMEM_LIMIT), name=name,
    )(d_mb, d_mc, da, lam_re, lam_im, log_dt, bt_re, bt_im)


def _cmul_add(xr, xi, pr, pi, zr, zi):
    return xr + pr * zr - pi * zi, xi + pr * zi + pi * zr


def s5_fwd(name, x, gain, w_in, mb, mc, pw, dskip, w_out_t):
    t, d = x.shape
    tm = _tile(t, S5_ROWS)
    ns = N_STATE

    def body(x_ref, g_ref, wi_ref, mb_ref, mc_ref, pw_ref, d_ref, wo_ref,
             x1_ref, hn_ref, r_ref, u_ref, gy_ref, y_ref, xs_ref, xb_ref, o1_ref, o2_ref, carry):
        @pl.when(pl.program_id(0) == 0)
        def _():
            carry[...] = jnp.zeros(carry.shape, F32)

        xv = x_ref[...]
        rv = lax.rsqrt(jnp.mean(xv * xv, axis=-1, keepdims=True) + EPS)
        hn = (xv * rv * g_ref[...]).astype(BF16)
        hn_ref[...] = hn
        r_ref[...] = rv
        uv = jnp.dot(hn, wi_ref[...], preferred_element_type=F32)
        u_ref[...] = uv
        ub = _bf(uv)
        for cs, s_re, s_im in _S5_BLOCKS:
            xs_ref[:, s_re] = jnp.dot(ub[:, cs], mb_ref[cs, s_re], preferred_element_type=F32)
            xs_ref[:, s_im] = jnp.dot(ub[:, cs], mb_ref[cs, s_im], preferred_element_type=F32)

        def group(i, _):
            r0 = pl.multiple_of(i * SUB, SUB)
            xr = xs_ref[pl.ds(r0, SUB), 0:ns]
            xi = xs_ref[pl.ds(r0, SUB), ns:2 * ns]
            for k, s in enumerate((1, 2, 4)):
                xr, xi = _cmul_add(xr, xi, pw_ref[k, 0], pw_ref[k, 1], pltpu.roll(xr, s, 0), pltpu.roll(xi, s, 0))
            xr, xi = _cmul_add(xr, xi, pw_ref[3, 0], pw_ref[3, 1], carry[0], carry[1])
            xs_ref[pl.ds(r0, SUB), 0:ns] = xr
            xs_ref[pl.ds(r0, SUB), ns:2 * ns] = xi
            carry[0] = jnp.broadcast_to(xr[SUB - 1:SUB, :], (SUB, ns))
            carry[1] = jnp.broadcast_to(xi[SUB - 1:SUB, :], (SUB, ns))
            return 0
        lax.fori_loop(0, tm // SUB, group, 0)

        xb_ref[...] = _bf(xs_ref[...])
        for cs, s_re, s_im in _S5_BLOCKS:
            y = (jnp.dot(xb_ref[:, s_re], mc_ref[s_re, cs], preferred_element_type=F32)
                 + jnp.dot(xb_ref[:, s_im], mc_ref[s_im, cs], preferred_element_type=F32) + d_ref[:, cs] * uv[:, cs])
            y_ref[:, cs] = y
            gy_ref[:, cs] = _gelu(y).astype(gy_ref.dtype)
        o1 = lax.dot_general(gy_ref[...], wo_ref[0:d, :], _NT, preferred_element_type=F32)
        o2 = lax.dot_general(gy_ref[...], wo_ref[d:2 * d, :], _NT, preferred_element_type=F32)
        o1_ref[...] = o1.astype(BF16)
        o2_ref[...] = o2.astype(BF16)
        x1_ref[...] = xv + o1 * _sigmoid(o2)

    c = w_in.shape[1]
    rows = pl.BlockSpec((tm, d), lambda i: (i, 0))
    narrow = pl.BlockSpec((tm, c), lambda i: (i, 0))
    states = pl.BlockSpec((tm, 2 * ns), lambda i: (i, 0))
    return pl.pallas_call(
        body, grid=(t // tm,),
        in_specs=[rows, _whole(gain), _whole(w_in), _whole(mb), _whole(mc), _whole(pw), _whole(dskip), _whole(w_out_t)],
        out_specs=[rows, rows, pl.BlockSpec((tm, 1), lambda i: (i, 0)), narrow, narrow, narrow, states, states, rows, rows],
        out_shape=[S((t, d), F32), S((t, d), BF16), S((t, 1), F32), S((t, c), F32), S((t, c), BF16), S((t, c), F32),
                   S((t, 2 * ns), F32), S((t, 2 * ns), BF16), S((t, d), BF16), S((t, d), BF16)],
        scratch_shapes=[pltpu.VMEM((2, SUB, ns), F32)],
        compiler_params=_cp("arbitrary"), name=name)(x, gain, w_in, mb, mc, pw, dskip, w_out_t)


def s5_bwd(name, dgy, y, u, xs, mct, mbt, qw, dskip):
    t, c = u.shape
    tm = _tile(t, S5_ROWS)
    nt = t // tm
    ns = N_STATE
    ng = tm // SUB

    def body(dgy_ref, y_ref, u_ref, xs_ref, mct_ref, mbt_ref, qw_ref, d_ref,
             du_ref, dy_ref, lb_ref, da_ref, dd_ref, lam, carry):
        @pl.when(pl.program_id(0) == 0)
        def _():
            carry[...] = jnp.zeros(carry.shape, F32)
            da_ref[...] = jnp.zeros(da_ref.shape, F32)
            dd_ref[...] = jnp.zeros(dd_ref.shape, F32)

        uv = u_ref[...]
        dy = dgy_ref[...] * _gelu_grad(y_ref[...])
        dyb = _bf(dy)
        dy_ref[...] = dyb
        dd_ref[...] += jnp.sum(dy * uv, axis=0, keepdims=True)
        for cs, s_re, s_im in _S5_BLOCKS:
            lam[:, s_re] = jnp.dot(dyb[:, cs], mct_ref[cs, s_re], preferred_element_type=F32)
            lam[:, s_im] = jnp.dot(dyb[:, cs], mct_ref[cs, s_im], preferred_element_type=F32)
        last_row = lax.broadcasted_iota(jnp.int32, (SUB, ns), 0) == SUB - 1

        def group(j, _):
            i = ng - 1 - j
            r0 = pl.multiple_of(i * SUB, SUB)
            lr = lam[pl.ds(r0, SUB), 0:ns]
            li = lam[pl.ds(r0, SUB), ns:2 * ns]
            for k, s in enumerate((1, 2, 4)):
                lr, li = _cmul_add(lr, li, qw_ref[k, 0], qw_ref[k, 1],
                                   pltpu.roll(lr, SUB - s, 0), pltpu.roll(li, SUB - s, 0))
            cr, ci = carry[0], carry[1]
            lr, li = _cmul_add(lr, li, qw_ref[3, 0], qw_ref[3, 1], cr, ci)
            lam[pl.ds(r0, SUB), 0:ns] = lr
            lam[pl.ds(r0, SUB), ns:2 * ns] = li
            carry[0] = jnp.broadcast_to(lr[0:1, :], (SUB, ns))
            carry[1] = jnp.broadcast_to(li[0:1, :], (SUB, ns))
            nr = jnp.where(last_row, cr, pltpu.roll(lr, SUB - 1, 0))
            ni = jnp.where(last_row, ci, pltpu.roll(li, SUB - 1, 0))
            xr = xs_ref[pl.ds(r0, SUB), 0:ns]
            xi = xs_ref[pl.ds(r0, SUB), ns:2 * ns]
            da_ref[0] += nr * xr + ni * xi
            da_ref[1] += ni * xr - nr * xi
            return 0
        lax.fori_loop(0, ng, group, 0)

        lb_ref[...] = _bf(lam[...])
        for cs, s_re, s_im in _S5_BLOCKS:
            du = (jnp.dot(lb_ref[:, s_re], mbt_ref[s_re, cs], preferred_element_type=F32)
                  + jnp.dot(lb_ref[:, s_im], mbt_ref[s_im, cs], preferred_element_type=F32) + d_ref[:, cs] * dy[:, cs])
            du_ref[:, cs] = du.astype(du_ref.dtype)

    rev = lambda i: (nt - 1 - i, 0)
    return pl.pallas_call(
        body, grid=(nt,),
        in_specs=[pl.BlockSpec((tm, c), rev), pl.BlockSpec((tm, c), rev), pl.BlockSpec((tm, c), rev),
                  pl.BlockSpec((tm, 2 * ns), rev),
                  pl.BlockSpec(mct.shape, lambda i: (0, 0)), pl.BlockSpec(mbt.shape, lambda i: (0, 0)),
                  pl.BlockSpec(qw.shape, lambda i: (0, 0, 0, 0)), pl.BlockSpec((1, c), lambda i: (0, 0))],
        out_specs=[pl.BlockSpec((tm, c), rev), pl.BlockSpec((tm, c), rev), pl.BlockSpec((tm, 2 * ns), rev),
                   pl.BlockSpec((2, SUB, ns), lambda i: (0, 0, 0)), pl.BlockSpec((1, c), lambda i: (0, 0))],
        out_shape=[S((t, c), BF16), S((t, c), BF16), S((t, 2 * ns), BF16), S((2, SUB, ns), F32), S((1, c), F32)],
        scratch_shapes=[pltpu.VMEM((tm, 2 * ns), F32), pltpu.VMEM((2, SUB, ns), F32)],
        compiler_params=_cp("arbitrary"), name=name)(dgy, y, u, xs, mct, mbt, qw, dskip)


def _first(accs, *_):
    return [accs[0]]


def _rms_bwd_epi(accs, xv, base, rv, g):
    dv = accs[0]
    w = dv * g
    xh = xv * rv
    dx = base + rv * (w - xh * jnp.mean(w * xh, axis=-1, keepdims=True))
    return [dx, dx, jnp.sum(dv * xh, axis=0, keepdims=True)]


def mm_rms_bwd(name, pairs, x, r, gain, dres):
    t, d = x.shape
    return mm_nn(name, t, d, pairs, 1, _rms_bwd_epi, [F32, BF16], tiled=[x, dres], cols=[r], rowv=[gain], sums=[(1, d)])


def even_fwd(x, w, need_out):
    t = x.shape[0]
    proj, hn, r = mm_nn("e_in_f", t, IN_WIDTH, [(x, w["e_w_in_t"], 0, "t")], 1, _first, [F32], norm_gain=w["e_norm"])
    hc = conv_fwd("e_conv_f", proj, w["e_conv_w"], w["e_conv_b"])
    need_out(hc)
    x1, cat = even_out_fwd("e_out_f", proj, hc, x, w["e_gmlp_w"], w["e_gmlp_b"], w["e_conv_ln_g"], w["e_conv_ln_b"],
                           w["e_w_out"])
    return x1, (x, hn, r, proj, cat, hc)


def even_bwd_mixers(dxb, saved, w):
    x, hn, r, proj, cat, hc = saved
    t = x.shape[0]
    g_w_out = mm_tn("e_out_w", cat, dxb)
    dab, g_gw, g_gb = gmlp_bwd("e_gmlp_b", proj, dxb, w["e_w_out"], w["e_gmlp_w"], w["e_gmlp_b"])
    dhc, g_lg, g_lb = ln_silu_bwd("e_ln_b", hc, dxb, w["e_w_out"], w["e_conv_ln_g"], w["e_conv_ln_b"])
    dba, dbg, g_cw, g_cb = conv_bwd("e_conv_b", proj, dhc, w["e_conv_w"])
    g_w_in_t = jnp.concatenate([mm_tn("e_in_w0", dab, hn), mm_tn("e_in_w1", dba, hn), mm_tn("e_in_w2", dbg, hn)], axis=0)
    grads = dict(e_w_in_t=g_w_in_t, e_gmlp_w=g_gw[None], e_gmlp_b=g_gb.reshape(1, A_GROUPS, GMLP_BLOCK),
                 e_conv_w=g_cw[None], e_conv_b=g_cb, e_conv_ln_g=g_lg, e_conv_ln_b=g_lb, e_w_out=g_w_out)
    return (dab, dba, dbg), grads


def even_bwd_input(dx, dproj, saved, w):
    x, _, r = saved[:3]
    dab, dba, dbg = dproj
    w_in_t = w["e_w_in_t"]
    return mm_rms_bwd("e_in_b", [(dab, (w_in_t, 0), 0), (dba, (w_in_t, 2), 0), (dbg, (w_in_t, 3), 0)], x, r, w["e_norm"], dx)


def s5_setup(w, anchor=None):
    def rows(v):
        return v.transpose(0, 2, 1).reshape(_RP)

    log_dt = w["o_log_dt"].reshape(C_GROUPS, 1)
    if anchor is not None:
        log_dt = log_dt + anchor
    lam = (w["o_lam_re"], w["o_lam_im"], log_dt, rows(w["o_b_re"]), rows(w["o_b_im"]))
    a, bbr, bbi = s5_discretise("o_s5_zoh", *lam)
    c_re, c_im = w["o_c_re"], w["o_c_im"]
    pw, qw, mb, mc, mct = s5_operands("o_s5_ops", a.reshape(2, N_STATE), bbr, bbi, c_re.reshape(_RP), c_im.reshape(_RP),
                                      c_re.transpose(2, 0, 1).reshape(C_STATE, C_WIDTH),
                                      c_im.transpose(2, 0, 1).reshape(C_STATE, C_WIDTH))
    return dict(lam=lam, pw=pw, qw=qw, mb=mb, mc=mc, mct=mct, mbt=mb.T)


def odd_fwd(x, w, consts):
    x1, hn, r, u, gy, y, xs, xsb, o1, o2 = s5_fwd("o_s5_f", x, w["o_norm"], w["o_w_in"], consts["mb"], consts["mc"],
                                                  consts["pw"], w["o_d"], w["o_w_out_t"])
    return x1, (x, hn, r, u, gy, y, xs, xsb, o1, o2)


def odd_bwd(dx, dxb, saved, w, consts):
    x, hn, r, u, gy, y, xs, xsb, o1, o2 = saved
    t = x.shape[0]

    def gate_bwd(dv, a, b, wv):
        a = a.astype(F32)
        sg = _sigmoid(b.astype(F32))
        do12 = jnp.concatenate([dv * sg, dv * a * sg * (1.0 - sg)], axis=1).astype(BF16)
        return [do12, jnp.dot(do12, wv, preferred_element_type=F32)], []

    do12, dgy = rows_call("o_out_b", gate_bwd, [dx, o1, o2], [w["o_w_out_t"]], [(2 * D_MODEL, BF16), (C_WIDTH, F32)], [])
    g_w_out_t = mm_tn("o_out_w", do12, gy)
    du, dyb, lamb, da8, g_d = s5_bwd("o_s5_b", dgy, y, u, xs, consts["mct"], consts["mbt"], consts["qw"], w["o_d"])
    d_mb, d_mc = s5_block_grads("o_s5_w", u, lamb, xsb, dyb)
    da = jnp.sum(da8, axis=1).reshape((2,) + _GP)
    g_lr, g_li, g_dt, g_btr, g_bti, g_cr, g_ci = s5_param_grads("o_s5_pg", d_mb, d_mc, da, *consts["lam"])

    def states_first(v):
        return v.reshape(C_GROUPS, C_GROUP_CH, C_STATE).transpose(0, 2, 1)[None]

    g_w_in = mm_tn("o_in_w", hn, du)
    dx0, dx0b, g_norm = mm_rms_bwd("o_in_b", [(du, w["o_w_in"], 0, "t")], x, r, w["o_norm"], dx)
    grads = dict(o_norm=g_norm, o_w_in=g_w_in, o_lam_re=g_lr[None], o_lam_im=g_li[None], o_log_dt=g_dt.reshape(1, C_GROUPS),
                 o_b_re=states_first(g_btr), o_b_im=states_first(g_bti),
                 o_c_re=g_cr.reshape((1, C_GROUPS, C_GROUP_CH, C_STATE)), o_c_im=g_ci.reshape((1, C_GROUPS, C_GROUP_CH, C_STATE)),
                 o_d=g_d, o_w_out_t=g_w_out_t)
    return dx0, dx0b, grads


def ca_fwd(i, x, mem, w):
    t, m = x.shape[0], mem.shape[0]
    k, v, mn, rm = mm_nn(f"ca{i}_kv_f", m, D_MODEL, [(mem, w["ca_wk"][i], 0), (mem, w["ca_wv"][i], 1)], 2,
                         lambda accs: [accs[0], accs[1]], [BF16, BF16], norm_gain=w["ca_mem_norm"][i:i + 1])
    x1, xn, r, q, o = attn_fwd(f"ca{i}_attn_f", x, w["ca_norm"][i:i + 1], w["ca_wq"][i], k, v, w["ca_wo"][i])
    return x1, (x, xn, r, mn, rm, q, k, v, o)


def ca_bwd(i, dx, dxb, saved, mem, w):
    x, xn, r, mn, rm, q, k, v, o = saved
    t, m = x.shape[0], mem.shape[0]
    g_wo = mm_tn(f"ca{i}_o_w", o, dxb)
    dx0, dx0b, dq, dk, dv, g_norm = attn_bwd(f"ca{i}_attn_b", dx, dxb, x, r, w["ca_norm"][i:i + 1], q, k, v,
                                             w["ca_wq"][i], w["ca_wo"][i])
    g_wq = mm_tn(f"ca{i}_q_w", xn, dq)
    g_wk = mm_tn(f"ca{i}_k_w", mn, dk)
    g_wv = mm_tn(f"ca{i}_v_w", mn, dv)
    (dmn,) = mm_nn(f"ca{i}_kv_b", m, D_MODEL, [(dk, w["ca_wk"][i], 0, "t"), (dv, w["ca_wv"][i], 0, "t")], 1, _first, [F32])
    g_mnorm = rms_bwd_gain_only(f"ca{i}_mnorm_b", dmn, mem, rm)
    return dx0, dx0b, dict(ca_norm=g_norm, ca_mem_norm=g_mnorm, ca_wq=g_wq, ca_wk=g_wk, ca_wv=g_wv, ca_wo=g_wo)


FFN_ROWS = 512
FFN_CHUNK = 256


def _whole(a):
    return pl.BlockSpec(a.shape, lambda i: (0,) * a.ndim, pipeline_mode=pl.Buffered(1))


def ffn_fused_fwd(name, x, gain, wg_t, wu_t, wd, target=None, final_gain=None):
    t, d = x.shape
    hid = wd.shape[0]
    tm = _tile(t, FFN_ROWS)
    last = target is not None
    n_main = 4 if last else 1

    def body(*refs):
        x_ref, g_ref, wg_ref, wu_ref, wd_ref = refs[:5]
        rest = refs[5:]
        if last:
            tgt_ref, fg_ref = rest[:2]
            rest = rest[2:]
        main, (xn_ref, r_ref, dgate_ref, dup_ref, h_ref) = rest[:n_main], rest[n_main:]
        xv = x_ref[...]
        rv = lax.rsqrt(jnp.mean(xv * xv, axis=-1, keepdims=True) + EPS)
        xn = (xv * rv * g_ref[...]).astype(BF16)
        xn_ref[...] = xn
        r_ref[...] = rv
        for j in range(hid // FFN_CHUNK):
            cs = slice(j * FFN_CHUNK, (j + 1) * FFN_CHUNK)
            g = lax.dot_general(xn, wg_ref[cs, :], _NT, preferred_element_type=F32)
            u = lax.dot_general(xn, wu_ref[cs, :], _NT, preferred_element_type=F32)
            s = _sigmoid(g)
            silu = g * s
            dgate_ref[:, cs] = (u * (s + silu * (1.0 - s))).astype(BF16)
            dup_ref[:, cs] = silu.astype(BF16)
            h_ref[:, cs] = (silu * u).astype(BF16)
        acc = jnp.dot(h_ref[...], wd_ref[...], preferred_element_type=F32)
        if not last:
            main[0][...] = xv + acc
        else:
            dx, _, dgain, part = _final_loss_epi([acc], xv, tgt_ref[...], fg_ref[...])

            @pl.when(pl.program_id(0) == 0)
            def _():
                main[2][...] = jnp.zeros(main[2].shape, F32)
                main[3][...] = jnp.zeros(main[3].shape, F32)
            main[0][...] = dx
            main[1][...] = dx.astype(BF16)
            main[2][...] += dgain
            main[3][...] += part

    rows = pl.BlockSpec((tm, d), lambda i: (i, 0))
    wide = pl.BlockSpec((tm, hid), lambda i: (i, 0))
    col = pl.BlockSpec((tm, 1), lambda i: (i, 0))
    ins, in_specs = [x, gain, wg_t, wu_t, wd], [rows, _whole(gain), _whole(wg_t), _whole(wu_t), _whole(wd)]
    if last:
        ins += [target, final_gain]
        in_specs += [rows, _whole(final_gain)]
        out_specs = [rows, rows, pl.BlockSpec((1, d), lambda i: (0, 0)), pl.BlockSpec((1, 1), lambda i: (0, 0))]
        out_shape = [S((t, d), F32), S((t, d), BF16), S((1, d), F32), S((1, 1), F32)]
    else:
        out_specs, out_shape = [rows], [S((t, d), F32)]
    out_specs += [rows, col, wide, wide, wide]
    out_shape += [S((t, d), BF16), S((t, 1), F32)] + [S((t, hid), BF16)] * 3
    outs = pl.pallas_call(body, grid=(t // tm,), in_specs=in_specs, out_specs=out_specs, out_shape=out_shape,
                          compiler_params=_cp("arbitrary" if last else "parallel"), name=name)(*ins)
    return (tuple(outs[:4]) if last else outs[0]), outs[n_main:]


def ffn_fused_bwd(name, dx, dxb, x, r, gain, dgate, dup, wg_t, wu_t, wd):
    t, d = x.shape
    hid = wd.shape[0]
    tm = _tile(t, FFN_ROWS // 2)

    def body(dx_ref, dxb_ref, x_ref, r_ref, g_ref, dgate_ref, dup_ref, wg_ref, wu_ref, wd_ref,
             dxo_ref, dxbo_ref, dg_ref, du_ref, dgain_ref):
        @pl.when(pl.program_id(0) == 0)
        def _():
            dgain_ref[...] = jnp.zeros(dgain_ref.shape, F32)

        dxb = dxb_ref[...]
        for j in range(hid // FFN_CHUNK):
            cs = slice(j * FFN_CHUNK, (j + 1) * FFN_CHUNK)
            dh = lax.dot_general(dxb, wd_ref[cs, :], _NT, preferred_element_type=F32)
            dg_ref[:, cs] = (dh * dgate_ref[:, cs].astype(F32)).astype(BF16)
            du_ref[:, cs] = (dh * dup_ref[:, cs].astype(F32)).astype(BF16)
        dxn = (jnp.dot(dg_ref[...], wg_ref[...], preferred_element_type=F32)
               + jnp.dot(du_ref[...], wu_ref[...], preferred_element_type=F32))
        dxo, _, dgain = _rms_bwd_epi([dxn], x_ref[...], dx_ref[...], r_ref[...], g_ref[...])
        dxo_ref[...] = dxo
        dxbo_ref[...] = dxo.astype(BF16)
        dgain_ref[...] += dgain

    rows = pl.BlockSpec((tm, d), lambda i: (i, 0))
    wide = pl.BlockSpec((tm, hid), lambda i: (i, 0))
    col = pl.BlockSpec((tm, 1), lambda i: (i, 0))
    return pl.pallas_call(
        body, grid=(t // tm,),
        in_specs=[rows, rows, rows, col, _whole(gain), wide, wide, _whole(wg_t), _whole(wu_t), _whole(wd)],
        out_specs=[rows, rows, wide, wide, pl.BlockSpec((1, d), lambda i: (0, 0))],
        out_shape=[S((t, d), F32), S((t, d), BF16), S((t, hid), BF16), S((t, hid), BF16), S((1, d), F32)],
        compiler_params=_cp("arbitrary"), name=name)(dx, dxb, x, r, gain, dgate, dup, wg_t, wu_t, wd)


def ffn_fwd(i, x, w, target=None):
    out, (xn, r, dgate, dup, h) = ffn_fused_fwd(f"ffn{i}_f", x, w["ffn_norm"][i:i + 1], w["ffn_w_gate_t"][i],
                                                w["ffn_w_up_t"][i], w["ffn_w_down"][i], target,
                                                None if target is None else w["final_norm"])
    return out, (x, xn, r, dgate, dup, h)


def ffn_bwd(i, dx, dxb, saved, w):
    x, xn, r, dgate, dup, h = saved
    g_wd = mm_tn(f"ffn{i}_down_w", h, dxb)
    dx0, dx0b, dg, du, g_norm = ffn_fused_bwd(f"ffn{i}_b", dx, dxb, x, r, w["ffn_norm"][i:i + 1], dgate, dup,
                                              w["ffn_w_gate_t"][i], w["ffn_w_up_t"][i], w["ffn_w_down"][i])
    g_wg_t = mm_tn(f"ffn{i}_gate_w", dg, xn)
    g_wu_t = mm_tn(f"ffn{i}_up_w", du, xn)
    return dx0, dx0b, dict(ffn_norm=g_norm, ffn_w_gate_t=g_wg_t, ffn_w_up_t=g_wu_t, ffn_w_down=g_wd)


def local_step(x, mem, target, w, fetch=None, on_grads=None, anchor=None):
    consts = s5_setup(w, anchor)

    def need(stage, after):
        if fetch is not None:
            for k, v in fetch(stage, after).items():
                if isinstance(k, tuple):
                    w.setdefault(k[0], {})[k[1]] = v
                else:
                    w[k] = v

    need(0, consts["pw"])
    def early(stage, after):
        if fetch is not None:
            fetch(stage, after, True)

    def before_out(after):
        need(1, after)
        early(2, after)

    x1, s_e = even_fwd(x, w, before_out)
    need(2, x1)
    x2, s_c0 = ca_fwd(0, x1, mem, w)
    need(3, x2)
    x3, s_f0 = ffn_fwd(0, x2, w)
    need(4, x3)
    x4, s_o = odd_fwd(x3, w, consts)
    need(5, x4)
    early(6, x4)
    x5, s_c1 = ca_fwd(1, x4, mem, w)
    need(6, x5)
    (dx, dxb, g_final, loss), s_f1 = ffn_fwd(1, x5, w, target)

    def emit(stage, carry, plain, layered=None, layer=0):
        if on_grads is None:
            return carry
        out = dict(plain)
        out.update({(k, layer): v for k, v in (layered or {}).items()})
        return on_grads(stage, out, list(carry))

    dx, dxb, g_f1 = ffn_bwd(1, dx, dxb, s_f1, w)
    dx, dxb = emit(0, (dx, dxb), {}, g_f1, 1)
    dx, dxb, g_c1 = ca_bwd(1, dx, dxb, s_c1, mem, w)
    dx, dxb, g_o = odd_bwd(dx, dxb, s_o, w, consts)
    dx, dxb = emit(1, (dx, dxb), g_o, g_c1, 1)
    dx, dxb, g_f0 = ffn_bwd(0, dx, dxb, s_f0, w)
    dx, dxb = emit(2, (dx, dxb), {}, g_f0, 0)
    dx, dxb, g_c0 = ca_bwd(0, dx, dxb, s_c0, mem, w)
    dx, dxb = emit(3, (dx, dxb), {}, g_c0, 0)
    dproj, g_e = even_bwd_mixers(dxb, s_e, w)
    dproj = emit(4, dproj, {**g_e, "o_norm": g_o["o_norm"], "o_d": g_o["o_d"]})
    dx, dxb, g_e["e_norm"] = even_bwd_input(dx, dproj, s_e, w)

    grads = dict(g_e)
    grads.update(g_o)
    for g0, g1 in ((g_c0, g_c1), (g_f0, g_f1)):
        for k in g0:
            grads[k] = jnp.concatenate([g0[k], g1[k]], axis=0) if k.endswith("norm") else (g0[k], g1[k])
    grads["final_norm"] = g_final
    return loss, dx, grads


def _group(axes):
    pos = {a: lax.axis_index(a) for a in ("x", "y", "c")}
    me = 0
    for a in axes:
        me = me * 2 + pos[a]
    peers = []
    for mask in range(1, 2 ** len(axes)):
        peer = dict(pos)
        for bit, a in enumerate(axes):
            if (mask >> (len(axes) - 1 - bit)) & 1:
                peer[a] = 1 - pos[a]
        idx = 0
        for a in axes:
            idx = idx * 2 + peer[a]
        peers.append((idx, (peer["x"], peer["y"], peer["c"])))
    return me, peers


def _sibling():
    x, y, c = lax.axis_index("x"), lax.axis_index("y"), lax.axis_index("c")
    return c, (x, y, 1 - c)


_HBM =pl.BlockSpec(memory_space=pltpu.HBM)
_SEM = pl.BlockSpec(memory_space=pltpu.SEMAPHORE)
_EFFECT = pltpu.SideEffectType.DATAFLOW_SIDE_EFFECTING


def _gather_peers(direct):
    chip, _ = _group(("x", "y"))
    core = lax.axis_index("c")
    if direct:
        _, peers = _group(_ALL)
        return chip, core, [(idx // 2, idx % 2, dev) for idx, dev in peers]
    _, peers = _group(("x", "y"))
    return chip, core, [(idx, core, dev) for idx, dev in peers]


def gather_ici_start(name, groups, direct):
    flat = [b for g in groups for b in g]
    sizes = [len(g) for g in groups]
    k_ops, n_g = len(flat), len(groups)
    lands = [lax.empty((4, 2) + tuple(b.shape), b.dtype) for b in flat]
    fan = [N_DEV - 1 if d else 3 for d in direct]

    def body(*refs):
        src, land = refs[:k_ops], refs[k_ops:2 * k_ops]
        sems = refs[2 * k_ops:2 * k_ops + 3 * n_g]
        token = refs[-1]
        i = 0
        for g in range(n_g):
            send, recv, loc = sems[3 * g:3 * g + 3]
            chip, core, peers = _gather_peers(direct[g])
            for j in range(sizes[g]):
                pltpu.make_async_copy(src[i], land[i].at[chip, core], loc.at[j]).start()
                for k, (_, _, dev) in enumerate(peers):
                    s = fan[g] * j + k
                    pltpu.make_async_remote_copy(src_ref=src[i], dst_ref=land[i].at[chip, core], send_sem=send.at[s],
                                                 recv_sem=recv.at[s], device_id=dev, device_id_type=MESH).start()
                i += 1
        token[...] = jnp.zeros(token.shape, token.dtype)

    sem_shapes = []
    for s, f in zip(sizes, fan):
        sem_shapes += [pltpu.SemaphoreType.DMA((f * s,)), pltpu.SemaphoreType.DMA((f * s,)), pltpu.SemaphoreType.DMA((s,))]
    thru = [pltpu.HBM(a.shape, a.dtype) for a in flat + lands]
    outs = pl.pallas_call(
        body, name=name, out_shape=tuple(sem_shapes) + tuple(thru) + (S((8, LANES), F32),),
        in_specs=[_HBM] * (2 * k_ops), out_specs=[_SEM] * (3 * n_g) + [_HBM] * (2 * k_ops) + [pl.BlockSpec(memory_space=pltpu.VMEM)],
        input_output_aliases={i: 3 * n_g + i for i in range(2 * k_ops)},
        compiler_params=pltpu.CompilerParams(has_side_effects=_EFFECT),
    )(*[pltpu.with_memory_space_constraint(a, pltpu.HBM) for a in flat + lands])
    sems = [tuple(outs[3 * g:3 * g + 3]) for g in range(n_g)]
    srcs_thru, lands_thru, off = [], [], 3 * n_g
    for s in sizes:
        srcs_thru.append(list(outs[off:off + s]))
        off += s
    for s in sizes:
        lands_thru.append(list(outs[off:off + s]))
        off += s
    return sems, srcs_thru, lands_thru, outs[-1]


def gather_ici_wait(name, srcs, lands, sems, after, direct=False):
    n = len(srcs)

    def body(*refs):
        src, land = refs[:n], refs[n:2 * n]
        send, recv, loc = refs[2 * n:2 * n + 3]
        chip, core, peers = _gather_peers(direct)
        for j in range(n):
            for k, (pchip, pcore, dev) in enumerate(peers):
                s = len(peers) * j + k
                cp = pltpu.make_async_remote_copy(src_ref=src[j], dst_ref=land[j].at[pchip, pcore], send_sem=send.at[s],
                                                  recv_sem=recv.at[s], device_id=dev, device_id_type=MESH)
                cp.wait_send()
                cp.wait_recv()
            pltpu.make_async_copy(src[j], land[j].at[chip, core], loc.at[j]).wait()

    outs = pl.pallas_call(
        body, name=name, out_shape=tuple(pltpu.HBM(a.shape, a.dtype) for a in list(srcs) + list(lands)),
        in_specs=[_HBM] * (2 * n) + [_SEM] * 3 + [ANY], out_specs=[_HBM] * (2 * n),
        input_output_aliases={i: i for i in range(2 * n)},
        compiler_params=pltpu.CompilerParams(has_side_effects=_EFFECT),
    )(*srcs, *lands, *sems, after)
    return list(outs[n:])


def gather_d2d(name, bufs):
    k_ops = len(bufs)

    def body(*refs):
        in_refs, out_refs = refs[:k_ops], refs[k_ops:2 * k_ops]
        send_sems, recv_sems = refs[2 * k_ops:]
        core, sib = _sibling()
        sent, landed = [], []
        for i in range(k_ops):
            cp = pltpu.make_async_remote_copy(src_ref=in_refs[i].at[:, core], dst_ref=out_refs[i].at[:, core],
                                              send_sem=send_sems.at[i], recv_sem=recv_sems.at[i], device_id=sib, device_id_type=MESH)
            cp.start()
            sent.append(cp)
            landed.append(pltpu.make_async_remote_copy(src_ref=in_refs[i].at[:, core], dst_ref=out_refs[i].at[:, 1 - core],
                                                       send_sem=send_sems.at[i], recv_sem=recv_sems.at[i],
                                                       device_id=sib, device_id_type=MESH))
        for cp in landed:
            cp.wait_recv()
        for cp in sent:
            cp.wait_send()

    return pl.pallas_call(
        body, in_specs=[ANY] * k_ops, out_specs=[ANY] * k_ops, out_shape=[S(b.shape, b.dtype) for b in bufs],
        input_output_aliases={i: i for i in range(k_ops)},
        scratch_shapes=[pltpu.SemaphoreType.DMA((k_ops,)), pltpu.SemaphoreType.DMA((k_ops,))],
        name=name)(*bufs)


def gather_d2d_start(name, bufs):
    k_ops = len(bufs)

    def body(*refs):
        in_refs = refs[:k_ops]
        send, recv = refs[k_ops], refs[k_ops + 1]
        core, sib = _sibling()
        for i in range(k_ops):
            pltpu.make_async_remote_copy(src_ref=in_refs[i].at[:, core], dst_ref=in_refs[i].at[:, core], send_sem=send.at[i],
                                         recv_sem=recv.at[i], device_id=sib, device_id_type=MESH).start()

    outs = pl.pallas_call(
        body, name=name,
        out_shape=(pltpu.SemaphoreType.DMA((k_ops,)), pltpu.SemaphoreType.DMA((k_ops,))) + tuple(pltpu.HBM(b.shape, b.dtype) for b in bufs),
        in_specs=[_HBM] * k_ops, out_specs=[_SEM, _SEM] + [_HBM] * k_ops,
        input_output_aliases={i: 2 + i for i in range(k_ops)},
        compiler_params=pltpu.CompilerParams(has_side_effects=_EFFECT),
    )(*[pltpu.with_memory_space_constraint(b, pltpu.HBM) for b in bufs])
    return (outs[0], outs[1]), list(outs[2:])


def gather_d2d_wait(name, bufs, sems, after):
    k_ops = len(bufs)

    def body(*refs):
        in_refs = refs[:k_ops]
        send, recv = refs[k_ops], refs[k_ops + 1]
        core, sib = _sibling()
        for i in range(k_ops):
            cp = pltpu.make_async_remote_copy(src_ref=in_refs[i].at[:, core], dst_ref=in_refs[i].at[:, 1 - core], send_sem=send.at[i],
                                              recv_sem=recv.at[i], device_id=sib, device_id_type=MESH)
            cp.wait_send()
            cp.wait_recv()

    outs = pl.pallas_call(
        body, name=name, out_shape=tuple(pltpu.HBM(b.shape, b.dtype) for b in bufs),
        in_specs=[_HBM] * k_ops + [_SEM, _SEM, ANY], out_specs=[_HBM] * k_ops,
        input_output_aliases={i: i for i in range(k_ops)},
        compiler_params=pltpu.CompilerParams(has_side_effects=_EFFECT),
    )(*bufs, sems[0], sems[1], after)
    return list(outs)


_ALL = ("x", "y", "c")


def _unit_rows(units):
    offs, off = [], 0
    for u in units:
        offs.append(off)
        off += u.shape[1]
    return offs, off


def scatter_start(name, units, carry):
    n_u, n_c = len(units), len(carry)
    offs, rows = _unit_rows(units)
    land = lax.empty((N_DEV, rows) + tuple(units[0].shape[2:]), units[0].dtype)
    fan = N_DEV - 1

    def body(*refs):
        u_refs, land_ref = refs[:n_u], refs[n_u]
        send, recv, loc = refs[n_u + 1 + n_c:n_u + 4 + n_c]
        me, peers = _group(_ALL)
        for j in range(n_u):
            rs = pl.ds(offs[j], units[j].shape[1])
            pltpu.make_async_copy(u_refs[j].at[me], land_ref.at[me, rs], loc.at[j]).start()
            for k, (idx, dev) in enumerate(peers):
                pltpu.make_async_remote_copy(src_ref=u_refs[j].at[idx], dst_ref=land_ref.at[me, rs], send_sem=send.at[fan * j + k],
                                             recv_sem=recv.at[fan * j + k], device_id=dev, device_id_type=MESH).start()

    thru = list(units) + [land] + list(carry)
    outs = pl.pallas_call(
        body, name=name,
        out_shape=(pltpu.SemaphoreType.DMA((fan * n_u,)), pltpu.SemaphoreType.DMA((fan * n_u,)), pltpu.SemaphoreType.DMA((n_u,)))
        + tuple(pltpu.HBM(a.shape, a.dtype) for a in thru),
        in_specs=[_HBM] * len(thru), out_specs=[_SEM] * 3 + [_HBM] * len(thru),
        input_output_aliases={i: 3 + i for i in range(len(thru))},
        compiler_params=pltpu.CompilerParams(has_side_effects=_EFFECT),
    )(*[pltpu.with_memory_space_constraint(a, pltpu.HBM) for a in thru])
    return tuple(outs[:3]), list(outs[3:3 + n_u]), outs[3 + n_u], list(outs[4 + n_u:])


def scatter_wait(name, units, land, sems, after):
    n_u = len(units)
    offs, _ = _unit_rows(units)
    fan = N_DEV - 1

    def body(*refs):
        u_refs, land_ref = refs[:n_u], refs[n_u]
        send, recv, loc = refs[n_u + 1:n_u + 4]
        me, peers = _group(_ALL)
        for j in range(n_u):
            rs = pl.ds(offs[j], units[j].shape[1])
            for k, (idx, dev) in enumerate(peers):
                cp = pltpu.make_async_remote_copy(src_ref=u_refs[j].at[idx], dst_ref=land_ref.at[idx, rs], send_sem=send.at[fan * j + k],
                                                  recv_sem=recv.at[fan * j + k], device_id=dev, device_id_type=MESH)
                cp.wait_send()
                cp.wait_recv()
            pltpu.make_async_copy(u_refs[j].at[me], land_ref.at[me, rs], loc.at[j]).wait()

    thru = list(units) + [land]
    outs = pl.pallas_call(
        body, name=name, out_shape=tuple(pltpu.HBM(a.shape, a.dtype) for a in thru),
        in_specs=[_HBM] * len(thru) + [_SEM] * 3 + [ANY], out_specs=[_HBM] * len(thru),
        input_output_aliases={i: i for i in range(len(thru))},
        compiler_params=pltpu.CompilerParams(has_side_effects=_EFFECT),
    )(*thru, *sems, after)
    return outs[n_u]


def _row_tile(rows, cap=512):
    return next(t for t in range(cap - cap % 16, 0, -16) if rows % t == 0)


def sum_shares(name, recv, me):
    n, rows, c = recv.shape
    tr = _row_tile(rows)

    def body(me_ref, *refs):
        acc = refs[0][...].astype(F32)
        for r in refs[1:n]:
            acc = acc + r[...].astype(F32)
        refs[n][...] = acc

    def slot(mask):
        return pl.BlockSpec((None, tr, c), lambda i, me, mask=mask: (jnp.bitwise_xor(me[0], mask), i, 0))

    spec = pltpu.PrefetchScalarGridSpec(
        num_scalar_prefetch=1, grid=(rows // tr,), in_specs=[slot(k) for k in range(n)],
        out_specs=pl.BlockSpec((tr, c), lambda i, me: (i, 0)))
    return pl.pallas_call(body, grid_spec=spec, out_shape=S((rows, c), F32),
                          compiler_params=_cp("parallel"), name=name)(me, *([recv] * n))


def sum_slots(name, slots):
    n, r, c = slots.shape

    def body(s_ref, o_ref):
        acc = s_ref[0]
        for j in range(1, n):
            acc = acc + s_ref[j]
        o_ref[...] = acc

    return pl.pallas_call(body, out_shape=S((r, c), F32), compiler_params=pltpu.CompilerParams(vmem_limit_bytes=VMEM_LIMIT),
                          name=name)(slots)


def adamw_units(name, pieces, transposed, w, m, v):
    n_l, k, n = w.shape
    tk = _tile(k, 512) if transposed else k
    p_rows = n if transposed else k
    arrs = [p[0] if isinstance(p, tuple) else p for p in pieces]
    offs = [p[1] // p_rows if isinstance(p, tuple) else 0 for p in pieces]
    assert all(not isinstance(p, tuple) or p[1] % p_rows == 0 for p in pieces)
    c1 = 1.0 - ADAM_B1 ** ADAM_STEP
    c2 = 1.0 - ADAM_B2 ** ADAM_STEP

    def body(*refs):
        p_refs, (w_ref, m_ref, v_ref, g_ref, d_ref, m2_ref, v2_ref) = refs[:n_l], refs[n_l:]
        gv = p_refs[0][...]
        for j in range(1, n_l):
            gv = jnp.where(pl.program_id(0) == j, p_refs[j][...], gv)
        if transposed:
            gv = gv.T
        m2 = ADAM_B1 * m_ref[...] + (1.0 - ADAM_B1) * gv
        v2 = ADAM_B2 * v_ref[...] + (1.0 - ADAM_B2) * (gv * gv)
        g_ref[...] = gv
        m2_ref[...] = m2
        v2_ref[...] = v2
        d_ref[...] = -ADAM_LR * ((m2 / c1) / (jnp.sqrt(v2 / c2) + ADAM_EPS) + ADAM_WD * w_ref[...])

    def piece(j, o):
        if transposed:
            return pl.BlockSpec((n, tk), lambda l, i, o=o, j=j: (o, jnp.where(l == j, i, 0)))
        return pl.BlockSpec((k, n), lambda l, i, o=o: (o, 0))

    blk = pl.BlockSpec((None, tk, n), lambda l, i: (l, i, 0))
    return tuple(pl.pallas_call(body, grid=(n_l, k // tk), in_specs=[piece(j, o) for j, o in enumerate(offs)] + [blk] * 3, out_specs=[blk] * 4,
                                out_shape=[S(w.shape, F32)] * 4, compiler_params=_cp("parallel", "parallel"),
                                name=name)(*arrs, w, m, v))


def adamw_native(name, g, w, m, v, tr=512):
    shape = w.shape
    cols = shape[-1]
    rows = w.size // cols
    tr = _tile(rows, tr) if rows % 8 == 0 else rows
    c1 = 1.0 - ADAM_B1 ** ADAM_STEP
    c2 = 1.0 - ADAM_B2 ** ADAM_STEP

    def body(g_ref, w_ref, m_ref, v_ref, d_ref, m2_ref, v2_ref):
        gv = g_ref[...]
        m2 = ADAM_B1 * m_ref[...] + (1.0 - ADAM_B1) * gv
        v2 = ADAM_B2 * v_ref[...] + (1.0 - ADAM_B2) * (gv * gv)
        m2_ref[...] = m2
        v2_ref[...] = v2
        d_ref[...] = -ADAM_LR * ((m2 / c1) / (jnp.sqrt(v2 / c2) + ADAM_EPS) + ADAM_WD * w_ref[...])

    row = pl.BlockSpec((tr, cols), lambda i: (i, 0))
    outs = pl.pallas_call(body, grid=(rows // tr,), in_specs=[row] * 4, out_specs=[row] * 3,
                          out_shape=[S((rows, cols), F32)] * 3, compiler_params=_cp("parallel"),
                          name=name)(*[a.reshape(rows, cols) for a in (g, w, m, v)])
    return tuple(o.reshape(shape) for o in outs)


_REPLICATED = ("e_norm", "e_gmlp_w", "e_gmlp_b", "e_conv_b", "e_conv_ln_g", "e_conv_ln_b", "o_lam_re", "o_lam_im", "o_log_dt",
               "o_b_re", "o_b_im", "o_c_re", "o_c_im", "ca_norm", "ca_mem_norm", "ffn_norm", "final_norm")
_ORDER = ("e_norm", "e_w_in", "e_gmlp_w", "e_gmlp_b", "e_conv_w", "e_conv_b", "e_conv_ln_g", "e_conv_ln_b", "e_w_out",
          "o_norm", "o_w_in", "o_lam_re", "o_lam_im", "o_log_dt", "o_b_re", "o_b_im", "o_c_re", "o_c_im", "o_d", "o_w_out",
          "ca_norm", "ca_mem_norm", "ca_wq", "ca_wk", "ca_wv", "ca_wo", "ffn_norm", "ffn_w_gate", "ffn_w_up", "ffn_w_down",
          "final_norm")


def _rows128(a, multiple=8):
    flat = a.reshape(-1)
    rows = -(-flat.shape[0] // (LANES * multiple)) * multiple
    return jnp.pad(flat, (0, rows * LANES - flat.shape[0])).reshape(rows, LANES)


def _shard(full, axis):
    s = full.shape
    return jnp.moveaxis(full.reshape(s[:axis] + (N_DEV, s[axis] // N_DEV) + s[axis + 1:]), axis, 0)


_UNITS = (("e_w_in", 0, True), ("e_w_out", 0, False), ("o_w_in", 0, False), ("o_w_out", 0, True),
          *[(n, i, False) for n in ("ca_wq", "ca_wk", "ca_wv", "ca_wo") for i in (0, 1)],
          *[(n, i, tr) for n, tr in (("ffn_w_gate", True), ("ffn_w_up", True), ("ffn_w_down", False)) for i in (0, 1)])
_LAYERED = ("ca_wq", "ca_wk", "ca_wv", "ca_wo", "ffn_w_gate", "ffn_w_up", "ffn_w_down")
_SMALL_SHARDED = (("e_conv_w", 2), ("o_norm", 1), ("o_d", 1))
RS_ROW = 1024


def _unit_key(name, tr):
    return name + "_t" if tr else name


def _stage_of(name, layer):
    if name.startswith("e_"):
        return 0 if name == "e_w_in" else 1
    if name.startswith("o_"):
        return 4
    if name.startswith("ca_"):
        return 2 if layer == 0 else 5
    return 3 if layer == 0 else 6


GATHER_STAGES = 7
GATHER_DIRECT = (False, False, False, False, True, True, False)


def weight_fetcher(local):
    groups, meta = [[] for _ in range(GATHER_STAGES)], [[] for _ in range(GATHER_STAGES)]
    for name, layer, tr in _UNITS:
        blk = local[name][layer]
        st = _stage_of(name, layer)
        groups[st].append(_bf(blk.T if tr else blk))
        meta[st].append((name, layer, tr))
    small = jnp.concatenate([local[name].reshape(-1) for name, _ in _SMALL_SHARDED])
    groups[0].append(_rows128(small))
    direct = list(GATHER_DIRECT)
    sems, srcs, lands, token = gather_ici_start("ag_w_start", groups, direct)

    early = {}

    def fetch(stage, after, start_only=False):
        if start_only:
            landed = gather_ici_wait(f"ag_w_wait{stage}", srcs[stage], lands[stage], sems[stage], after, direct[stage])
            early[stage] = gather_d2d_start(f"ag_w_d2d{stage}_start", landed)
            return {}
        if stage in early:
            bufs = gather_d2d_wait(f"ag_w_d2d{stage}_wait", early[stage][1], early[stage][0], after)
        else:
            bufs = gather_ici_wait(f"ag_w_wait{stage}", srcs[stage], lands[stage], sems[stage], after, direct[stage])
            if not direct[stage]:
                bufs = gather_d2d(f"ag_w_d2d{stage}", bufs)
        got = {}
        for (name, layer, tr), blk, buf in zip(meta[stage], groups[stage], bufs):
            arr = buf.reshape((N_DEV * blk.shape[0],) + tuple(blk.shape[1:]))
            if name in _LAYERED:
                got[(_unit_key(name, tr), layer)] = arr
            else:
                got[_unit_key(name, tr)] = arr
        if stage == 0:
            flat = bufs[-1].reshape(N_DEV, -1)
            off = 0
            for name, axis in _SMALL_SHARDED:
                blk = local[name]
                seg = flat[:, off:off + blk.size].reshape((N_DEV,) + blk.shape)
                off += blk.size
                seg = jnp.moveaxis(seg, 0, axis)
                got[name] = seg.reshape(seg.shape[:axis] + (-1,) + seg.shape[axis + 2:])
            got["e_conv_w"] = got["e_conv_w"][0]
        return got

    return fetch, token


def _grad_stage_of(name, layer):
    if name.startswith("e_"):
        return 4
    if name.startswith("o_"):
        return 1
    if name.startswith("ca_"):
        return 3 if layer == 0 else 1
    return 2 if layer == 0 else 0


GRAD_STAGES = 5
SMALL_ROWS = 16


def gradient_reducer(local, mom, var):
    me = (4 * lax.axis_index("x") + 2 * lax.axis_index("y") + lax.axis_index("c")).astype(jnp.int32).reshape(1)
    pending = []

    def start(stage, grads, carry):
        def grad_of(unit):
            key = _unit_key(unit[0], unit[2])
            return grads[(key, unit[1])] if unit[0] in _LAYERED else grads[key]

        units = sorted([u for u in _UNITS if _grad_stage_of(u[0], u[1]) == stage], key=lambda u: -grad_of(u).size)
        parts, spans = [], []
        for unit in units:
            g = grad_of(unit)
            part = g.reshape(N_DEV, -1, RS_ROW)
            spans.append((part.shape[1], g.shape[0] // N_DEV, g.shape[1]))
            parts.append(part)
        if stage == GRAD_STAGES - 1:
            small = jnp.concatenate([_shard(grads[name], axis).reshape(N_DEV, -1) for name, axis in _SMALL_SHARDED], axis=1)
            small = jnp.pad(small, ((0, 0), (0, SMALL_ROWS * RS_ROW - small.shape[1])))
            parts.append(small.astype(BF16).reshape(N_DEV, SMALL_ROWS, RS_ROW))
        sems, sent, land, carry = scatter_start(f"rs_start{stage}", parts, carry)
        pending.append((stage, units, spans, sems, sent, land))
        return carry

    def finish(after):
        res, per_layer, small_flat = {}, {}, None
        for stage, units, spans, sems, sent, land in pending:
            land = scatter_wait(f"rs_wait{stage}", sent, land, sems, after)
            total = sum_shares(f"rs_sum{stage}", land, me)
            off = 0
            for (name, layer, tr), (rows, r, c) in zip(units, spans):
                piece = (total, off) if c == RS_ROW else total[off:off + rows].reshape(r, c)
                per_layer.setdefault(name, {})[layer] = (piece, tr)
                off += rows
            if stage == GRAD_STAGES - 1:
                small_flat = total[off:off + SMALL_ROWS].reshape(-1)
        for name, by_layer in per_layer.items():
            pieces = [by_layer[i][0] for i in sorted(by_layer)]
            res[name] = adamw_units("adamw_" + name, pieces, by_layer[0][1], local[name], mom[name], var[name])
        off = 0
        for name, _ in _SMALL_SHARDED:
            blk = local[name]
            g = small_flat[off:off + blk.size].reshape(blk.shape)
            off += blk.size
            res[name] = (g,) + adamw_native("adamw_" + name, g, blk, mom[name], var[name])
        return res

    return start, finish


def replicated_start(grads, loss):
    pack = jnp.concatenate([_rows128(grads[name]) for name in _REPLICATED] + [_rows128(loss)], axis=0)
    sems, srcs, lands, token = gather_ici_start("ag_g_start", [[pack]], [False])
    return sems[0], srcs[0], lands[0], token


def replicated_finish(handle, after, w, mom, var):
    sems, srcs, lands, _ = handle
    (buf,) = gather_d2d("ag_g_d2d", gather_ici_wait("ag_g_wait", srcs, lands, sems, after))
    rows = srcs[0].shape[0]
    total = sum_slots("ag_g_sum", buf.reshape(N_DEV, rows, LANES))
    res, off = {}, 0
    for name in _REPLICATED:
        n = w[name].size
        nr = -(-n // (LANES * 8)) * 8
        g = total[off:off + nr].reshape(-1)[:n].reshape(w[name].shape)
        off += nr
        res[name] = (g,) + adamw_native("adamw_" + name, g, w[name], mom[name], var[name])
    return res, total[off, 0]


def kernel(x, mem, e_norm, e_w_in, e_gmlp_w, e_gmlp_b, e_conv_w, e_conv_b, e_conv_ln_g, e_conv_ln_b, e_w_out, o_norm, o_w_in, o_lam_re, o_lam_im, o_log_dt, o_b_re, o_b_im, o_c_re, o_c_im, o_d, o_w_out, ca_norm, ca_mem_norm, ca_wq, ca_wk, ca_wv, ca_wo, ffn_norm, ffn_w_gate, ffn_w_up, ffn_w_down, final_norm, loss_target, m_e_norm, m_e_w_in, m_e_gmlp_w, m_e_gmlp_b, m_e_conv_w, m_e_conv_b, m_e_conv_ln_g, m_e_conv_ln_b, m_e_w_out, m_o_norm, m_o_w_in, m_o_lam_re, m_o_lam_im, m_o_log_dt, m_o_b_re, m_o_b_im, m_o_c_re, m_o_c_im, m_o_d, m_o_w_out, m_ca_norm, m_ca_mem_norm, m_ca_wq, m_ca_wk, m_ca_wv, m_ca_wo, m_ffn_norm, m_ffn_w_gate, m_ffn_w_up, m_ffn_w_down, m_final_norm, v_e_norm, v_e_w_in, v_e_gmlp_w, v_e_gmlp_b, v_e_conv_w, v_e_conv_b, v_e_conv_ln_g, v_e_conv_ln_b, v_e_w_out, v_o_norm, v_o_w_in, v_o_lam_re, v_o_lam_im, v_o_log_dt, v_o_b_re, v_o_b_im, v_o_c_re, v_o_c_im, v_o_d, v_o_w_out, v_ca_norm, v_ca_mem_norm, v_ca_wq, v_ca_wk, v_ca_wv, v_ca_wo, v_ffn_norm, v_ffn_w_gate, v_ffn_w_up, v_ffn_w_down, v_final_norm):
    given = dict(locals())
    local = {k: given[k] for k in _ORDER}
    mom = {k: given["m_" + k] for k in _ORDER}
    var = {k: given["v_" + k] for k in _ORDER}

    w = {}
    w.update({
        "e_norm": e_norm, "e_gmlp_w": e_gmlp_w[0], "e_gmlp_b": e_gmlp_b.reshape(A_GROUPS, GMLP_BLOCK, 1),
        "e_conv_b": e_conv_b, "e_conv_ln_g": e_conv_ln_g, "e_conv_ln_b": e_conv_ln_b,
        "o_lam_re": o_lam_re[0], "o_lam_im": o_lam_im[0], "o_log_dt": o_log_dt[0], "o_b_re": o_b_re[0], "o_b_im": o_b_im[0],
        "o_c_re": o_c_re[0], "o_c_im": o_c_im[0], "ca_norm": ca_norm, "ca_mem_norm": ca_mem_norm, "ffn_norm": ffn_norm,
        "final_norm": final_norm.reshape(1, D_MODEL),
    })
    start_reduce, finish_reduce = gradient_reducer(local, mom, var)
    fetch, token = weight_fetcher(local)
    loss_part, grad_x, grads = local_step(x[0], mem[0], loss_target[0], w, fetch, start_reduce, token[0:1, 0:1])
    grads["final_norm"] = grads["final_norm"].reshape(D_MODEL)

    handle = replicated_start(grads, loss_part)
    res = finish_reduce(handle[3])
    rep, loss = replicated_finish(handle, res["ffn_w_down"][1], local, mom, var)
    res.update(rep)
    return (loss, grad_x[None], *[res[k][0] for k in _ORDER], *[res[k][1] for k in _ORDER],
            *[res[k][2] for k in _ORDER], *[res[k][3] for k in _ORDER])
```

```python
import jax
import jax.numpy as jnp
from jax import lax
from jax.experimental import pallas as pl
from jax.experimental.pallas import tpu as pltpu

F32 = jnp.float32
BF16 = jnp.bfloat16
S = jax.ShapeDtypeStruct

D_MODEL = 1024
A_WIDTH = 512
A_GROUPS = 4
GMLP_BLOCK = 128
CHUNK = 64
B_WIDTH = 512
IN_WIDTH = 2 * A_WIDTH + 2 * B_WIDTH
CONV_WIDTH = 31
CONV_PAD = 32
C_WIDTH = 512
C_GROUP_CH = 16
C_GROUPS = 32
C_STATE = 64
N_STATE = C_GROUPS * C_STATE
CA_HEADS = 4
CA_HEAD_DIM = 256
EPS = 1e-6
ADAM_LR = 0.001
ADAM_B1 = 0.9
ADAM_B2 = 0.999
ADAM_EPS = 1e-08
ADAM_WD = 0.01
ADAM_STEP = 10
N_DEV = 8
LANES = 128
VMEM_LIMIT = 56 << 20
VMEM_BUDGET = 40 << 20
MM_TN_RESIDENT = 8 << 20
MESH = pl.DeviceIdType.MESH
ANY = pl.BlockSpec(memory_space=pl.ANY)


def _cp(*sem):
    return pltpu.CompilerParams(dimension_semantics=sem, vmem_limit_bytes=VMEM_LIMIT)


def _tile(n, pref):
    t = pref
    while n % t:
        t //= 2
    return t


def _bf(v):
    return v if v.dtype == BF16 else v.astype(BF16)


def _sigmoid(x):
    return 1.0 / (1.0 + jnp.exp(-x))


_GC = 0.7978845608028654


def _gelu(x):
    return 0.5 * x * (1.0 + jnp.tanh(_GC * (x + 0.044715 * x * x * x)))


def _gelu_grad(x):
    x2 = x * x
    t = jnp.tanh(_GC * (x + 0.044715 * x * x2))
    return 0.5 * (1.0 + t) + 0.5 * x * (1.0 - t * t) * _GC * (1.0 + 3.0 * 0.044715 * x2)


def _tspec(entry, tm):
    if isinstance(entry, tuple):
        arr, cb, width = entry
        return arr, pl.BlockSpec((tm, width), lambda i, cb=cb: (i, cb))
    return entry, pl.BlockSpec((tm, entry.shape[1]), lambda i: (i, 0))


def rows_call(name, fn, tiled, full, outs, accs, tm=256):
    pairs = [_tspec(e, tm) for e in tiled]
    arrs = [p[0] for p in pairs]
    rows = arrs[0].shape[0]
    tm = _tile(rows, tm)
    pairs = [_tspec(e, tm) for e in tiled]
    n_in = len(tiled) + len(full)
    n_out = len(outs)

    def body(*refs):
        vals = [r[...] for r in refs[:n_in]]
        o_refs = refs[n_in:n_in + n_out]
        a_refs = refs[n_in + n_out:]
        ov, av = fn(*vals)
        for r, v in zip(o_refs, ov):
            r[...] = v.astype(r.dtype)
        if a_refs:
            @pl.when(pl.program_id(0) == 0)
            def _():
                for r in a_refs:
                    r[...] = jnp.zeros(r.shape, r.dtype)
            for r, v in zip(a_refs, av):
                r[...] += v

    in_specs = [p[1] for p in pairs] + [pl.BlockSpec(a.shape, lambda i, nd=a.ndim: (0,) * nd) for a in full]
    out_specs = [pl.BlockSpec((tm, c), lambda i: (i, 0)) for c, _ in outs]
    out_specs += [pl.BlockSpec(s, lambda i, nd=len(s): (0,) * nd) for s in accs]
    out_shape = [S((rows, c), dt) for c, dt in outs] + [S(s, F32) for s in accs]
    return pl.pallas_call(body, grid=(rows // tm,), in_specs=in_specs, out_specs=out_specs, out_shape=out_shape,
                          compiler_params=_cp("arbitrary"), name=name)(*arrs, *full)


def mm_nn(name, m, n, pairs, n_acc, epi, outs, tiled=(), cols=(), rowv=(), sums=(), norm_gain=None):
    a_ops, a_slot, b_arrs, b_specs, idx, trans = [], [], [], [], [], []
    fixed = 0
    for pair in pairs:
        a, b, k = pair[:3]
        bt = len(pair) > 3
        arr, cb, kdim = a if isinstance(a, tuple) else (a, 0, a.shape[1])
        key = (id(arr), cb, kdim)
        if key not in [o[0] for o in a_ops]:
            a_ops.append((key, arr, cb, kdim))
        a_slot.append([o[0] for o in a_ops].index(key))
        b_arr, off = b if isinstance(b, tuple) else (b, 0)
        b_arrs.append(b_arr)
        if bt:
            assert off % n == 0 and b_arr.shape[1] == kdim
            b_specs.append(pl.BlockSpec((n, kdim), lambda i, o=off // n: (o, 0), pipeline_mode=pl.Buffered(1)))
        else:
            assert b_arr.shape[1] == n
            b_specs.append(pl.BlockSpec((kdim, n), lambda i, o=off: (o, 0), pipeline_mode=pl.Buffered(1)))
        fixed += kdim * n * b_arr.dtype.itemsize
        idx.append(k)
        trans.append(bt)
    per_row = sum(2 * kdim * arr.dtype.itemsize for _, arr, _, kdim in a_ops)
    per_row += sum(2 * n * t.dtype.itemsize for t in tiled) + sum(2 * n * jnp.dtype(dt).itemsize for dt in outs)
    cn = n if sums or cols else (512 if n % 512 == 0 else 256)
    per_row += (n_acc + 3) * cn * 4
    tm = next((t for t in (1024, 512, 256, 128) if m % t == 0 and fixed + t * per_row <= VMEM_BUDGET), _tile(m, 128))
    n_a, n_p, n_t = len(a_ops), len(pairs), len(tiled)
    n_in = n_a + n_p + n_t + len(cols) + len(rowv)
    normed = norm_gain is not None
    o0 = n_in + normed

    def body(*refs):
        a_vals = [None if normed and i == 0 else _bf(r[...]) for i, r in enumerate(refs[:n_a])]
        if normed:
            xv = refs[0][...]
            rv = lax.rsqrt(jnp.mean(xv * xv, axis=-1, keepdims=True) + EPS)
            a_vals[0] = (xv * rv * refs[n_in][...]).astype(BF16)
            refs[o0 + len(outs)][...] = a_vals[0]
            refs[o0 + len(outs) + 1][...] = rv
        for j in range(n // cn):
            cs = slice(j * cn, (j + 1) * cn)
            accs = [None] * n_acc
            for p in range(n_p):
                av, b_ref = a_vals[a_slot[p]], refs[n_a + p]
                if trans[p]:
                    d = lax.dot_general(av, _bf(b_ref[cs, :]), (((1,), (1,)), ((), ())), preferred_element_type=F32)
                else:
                    d = jnp.dot(av, _bf(b_ref[:, cs]), preferred_element_type=F32)
                accs[idx[p]] = d if accs[idx[p]] is None else accs[idx[p]] + d
            extra = [r[:, cs] for r in refs[n_a + n_p:n_a + n_p + n_t]] + [r[...] for r in refs[n_a + n_p + n_t:n_in - len(rowv)]]
            extra += [r[:, cs] for r in refs[n_in - len(rowv):n_in]]
            ov = epi(accs, *extra)
            for r, v in zip(refs[o0:o0 + len(outs)], ov):
                r[:, cs] = v.astype(r.dtype)
        sv = ov[len(outs):]
        if sums:
            s_refs = refs[o0 + len(outs) + 2 * normed:]

            @pl.when(pl.program_id(0) == 0)
            def _():
                for r in s_refs:
                    r[...] = jnp.zeros(r.shape, r.dtype)
            for r, v in zip(s_refs, sv):
                r[...] += v

    in_specs = [pl.BlockSpec((tm, kdim), lambda i, cb=cb: (i, cb)) for _, _, cb, kdim in a_ops] + b_specs
    in_specs += [pl.BlockSpec((tm, n), lambda i: (i, 0)) for _ in tiled]
    in_specs += [pl.BlockSpec((tm, 1), lambda i: (i, 0)) for _ in cols]
    in_specs += [pl.BlockSpec((1, n), lambda i: (0, 0)) for _ in rowv]
    out_specs = [pl.BlockSpec((tm, n), lambda i: (i, 0)) for _ in outs]
    out_shape = [S((m, n), dt) for dt in outs]
    gain = []
    if normed:
        k0 = a_ops[0][3]
        gain = [norm_gain]
        in_specs.append(pl.BlockSpec((1, k0), lambda i: (0, 0)))
        out_specs += [pl.BlockSpec((tm, k0), lambda i: (i, 0)), pl.BlockSpec((tm, 1), lambda i: (i, 0))]
        out_shape += [S((m, k0), BF16), S((m, 1), F32)]
    out_specs += [pl.BlockSpec(s, lambda i, nd=len(s): (0,) * nd) for s in sums]
    out_shape += [S(s, F32) for s in sums]
    return pl.pallas_call(body, grid=(m // tm,), in_specs=in_specs, out_specs=out_specs, out_shape=out_shape,
                          compiler_params=_cp("arbitrary" if sums else "parallel"),
                          name=name)(*[o[1] for o in a_ops], *b_arrs, *tiled, *cols, *rowv, *gain)


def mm_tn(name, a, b, out_dtype=BF16):
    if isinstance(a, tuple):
        a_arr, a_cb, m = a
    else:
        a_arr, a_cb, m = a, None, a.shape[1]
    if isinstance(b, tuple):
        b_arr, b_cb, n = b
    else:
        b_arr, b_cb, n = b, None, b.shape[1]
    t = a_arr.shape[0]
    whole_b = t * n * b_arr.dtype.itemsize <= MM_TN_RESIDENT and b_cb is None
    tn = n if whole_b else _tile(n, 512)
    tm = _tile(m, 512 if t * 512 * a_arr.dtype.itemsize * 2 + t * tn * b_arr.dtype.itemsize * 2 <= VMEM_BUDGET else 256)
    a_off = 0 if a_cb is None else a_cb * (m // tm)
    b_off = 0 if b_cb is None else b_cb * (n // tn)

    def body(a_ref, b_ref, o_ref):
        o_ref[...] = lax.dot_general(_bf(a_ref[...]), _bf(b_ref[...]), (((0,), (0,)), ((), ())),
                                     preferred_element_type=F32).astype(o_ref.dtype)

    if whole_b:
        b_spec = pl.BlockSpec((t, n), lambda i, j: (0, 0), pipeline_mode=pl.Buffered(1))
    else:
        b_spec = pl.BlockSpec((t, tn), lambda i, j: (0, j + b_off))
    return pl.pallas_call(
        body, grid=(m // tm, n // tn),
        in_specs=[pl.BlockSpec((t, tm), lambda i, j: (0, i + a_off)), b_spec],
        out_specs=pl.BlockSpec((tm, tn), lambda i, j: (i, j)), out_shape=S((m, n), out_dtype),
        compiler_params=_cp("parallel", "parallel"), name=name)(a_arr, b_arr)


def rms_bwd_gain_only(name, dxn, x, r):
    def fn(dv, xv, rv):
        return [], [jnp.sum(dv * xv * rv, axis=0, keepdims=True)]
    return rows_call(name, fn, [dxn, x, r], [], [], [(1, x.shape[1])])[0]


def _final_loss_epi(accs, res, tv, g):
    xv = res + accs[0]
    d = xv.shape[-1]
    r = lax.rsqrt(jnp.mean(xv * xv, axis=-1, keepdims=True) + EPS)
    xh = xv * r
    err = xh * g - tv
    dy = err * (1.0 / d)
    w = dy * g
    dx = r * (w - xh * jnp.mean(w * xh, axis=-1, keepdims=True))
    part = jnp.sum(jnp.sum(err * err, axis=-1, keepdims=True), axis=0, keepdims=True) * (0.5 / d)
    return [dx, dx, jnp.sum(dy * xh, axis=0, keepdims=True), part]


def _gmlp_mask():
    row = lax.broadcasted_iota(jnp.int32, (GMLP_BLOCK, GMLP_BLOCK), 0) // CHUNK
    col = lax.broadcasted_iota(jnp.int32, (GMLP_BLOCK, GMLP_BLOCK), 1) // CHUNK
    return col <= row


def _ln_plain(v):
    mu = jnp.mean(v, axis=-1, keepdims=True)
    vc = v - mu
    rstd = lax.rsqrt(jnp.mean(vc * vc, axis=-1, keepdims=True) + EPS)
    return vc * rstd, rstd


def even_out_fwd(name, proj, hc, x, w, b, ln_g, ln_b, w_out, tm=512):
    t, d = x.shape
    tm = _tile(t, tm)

    def body(au_ref, av_ref, hc_ref, x_ref, w_ref, b_ref, lg_ref, lb_ref, wo_ref, x1_ref, cat_ref):
        mask = _gmlp_mask()
        u = _gelu(au_ref[...])
        vn, _ = _ln_plain(_gelu(av_ref[...]))
        vnb = _bf(vn)
        for g in range(A_GROUPS):
            wg = _bf(jnp.where(mask, w_ref[g], 0.0))
            cs = slice(g * GMLP_BLOCK, (g + 1) * GMLP_BLOCK)
            for n in range(tm // GMLP_BLOCK):
                rs = slice(n * GMLP_BLOCK, (n + 1) * GMLP_BLOCK)
                sg = jnp.dot(wg, vnb[rs, cs], preferred_element_type=F32) + b_ref[g]
                cat_ref[rs, cs] = (u[rs, cs] * sg).astype(cat_ref.dtype)
        y, _ = _ln_plain(hc_ref[...])
        z = y * lg_ref[...] + lb_ref[...]
        cat_ref[:, A_WIDTH:] = (z * _sigmoid(z)).astype(cat_ref.dtype)
        x1_ref[...] = x_ref[...] + jnp.dot(cat_ref[...], wo_ref[...], preferred_element_type=F32)

    half = pl.BlockSpec((tm, A_WIDTH), lambda i: (i, 0))
    return pl.pallas_call(
        body, grid=(t // tm,),
        in_specs=[half, pl.BlockSpec((tm, A_WIDTH), lambda i: (i, 1)), half, pl.BlockSpec((tm, d), lambda i: (i, 0)),
                  _whole(w), _whole(b), _whole(ln_g), _whole(ln_b), _whole(w_out)],
        out_specs=[pl.BlockSpec((tm, d), lambda i: (i, 0)), pl.BlockSpec((tm, A_WIDTH + B_WIDTH), lambda i: (i, 0))],
        out_shape=[S((t, d), F32), S((t, A_WIDTH + B_WIDTH), BF16)],
        compiler_params=_cp("parallel"), name=name)(proj, proj, hc, x, w, b, ln_g, ln_b, w_out)


def gmlp_bwd(name, proj, dxb, w_out, w, b, tm=512):
    t = proj.shape[0]
    tm = _tile(t, tm)

    def body(au_ref, av_ref, dx_ref, wo_ref, w_ref, b_ref, dp_ref, dw_ref, db_ref):
        @pl.when(pl.program_id(0) == 0)
        def _():
            dw_ref[...] = jnp.zeros(dw_ref.shape, F32)
            db_ref[...] = jnp.zeros(db_ref.shape, F32)

        mask = _gmlp_mask()
        au = au_ref[...]
        av = av_ref[...]
        u = _gelu(au)
        vn, rstd = _ln_plain(_gelu(av))
        vnb = _bf(vn)
        dout = lax.dot_general(dx_ref[...], wo_ref[0:A_WIDTH, :], _NT, preferred_element_type=F32)
        dvn_cols = []
        for g in range(A_GROUPS):
            wm = jnp.where(mask, w_ref[g], 0.0)
            wg = _bf(wm)
            wgt = _bf(wm.T)
            cs = slice(g * GMLP_BLOCK, (g + 1) * GMLP_BLOCK)
            dwg = jnp.zeros((GMLP_BLOCK, GMLP_BLOCK), F32)
            dbg = jnp.zeros((GMLP_BLOCK, 1), F32)
            dvn_rows = []
            for n in range(tm // GMLP_BLOCK):
                rs = slice(n * GMLP_BLOCK, (n + 1) * GMLP_BLOCK)
                sg = jnp.dot(wg, vnb[rs, cs], preferred_element_type=F32) + b_ref[g]
                dp_ref[rs, cs] = (dout[rs, cs] * sg * _gelu_grad(au[rs, cs])).astype(dp_ref.dtype)
                dsg = dout[rs, cs] * u[rs, cs]
                dsgb = _bf(dsg)
                dbg = dbg + jnp.sum(dsg, axis=1, keepdims=True)
                dwg = dwg + lax.dot_general(dsgb, vnb[rs, cs], (((1,), (1,)), ((), ())), preferred_element_type=F32)
                dvn_rows.append(jnp.dot(wgt, dsgb, preferred_element_type=F32))
            dw_ref[g] += jnp.where(mask, dwg, 0.0)
            db_ref[g] += dbg
            dvn_cols.append(jnp.concatenate(dvn_rows, axis=0))
        dvn = jnp.concatenate(dvn_cols, axis=1)
        dv = rstd * (dvn - jnp.mean(dvn, axis=-1, keepdims=True) - vn * jnp.mean(dvn * vn, axis=-1, keepdims=True))
        dp_ref[:, A_WIDTH:] = (dv * _gelu_grad(av)).astype(dp_ref.dtype)

    return pl.pallas_call(
        body, grid=(t // tm,),
        in_specs=[pl.BlockSpec((tm, A_WIDTH), lambda i: (i, 0)), pl.BlockSpec((tm, A_WIDTH), lambda i: (i, 1)),
                  pl.BlockSpec((tm, dxb.shape[1]), lambda i: (i, 0)), pl.BlockSpec(w_out.shape, lambda i: (0, 0)),
                  pl.BlockSpec(w.shape, lambda i: (0, 0, 0)), pl.BlockSpec(b.shape, lambda i: (0, 0, 0))],
        out_specs=[pl.BlockSpec((tm, 2 * A_WIDTH), lambda i: (i, 0)),
                   pl.BlockSpec(w.shape, lambda i: (0, 0, 0)), pl.BlockSpec(b.shape, lambda i: (0, 0, 0))],
        out_shape=[S((t, 2 * A_WIDTH), BF16), S(w.shape, F32), S(b.shape, F32)],
        compiler_params=_cp("arbitrary"), name=name)(proj, proj, dxb, w_out, w, b)


CONV_ROWS = 256
CONV_ROWS_BWD = 64


def conv_fwd(name, proj, w, cb):
    t = proj.shape[0]
    tc = LANES
    rows = _tile(t, CONV_ROWS)
    a_cb, g_cb = 2 * A_WIDTH // tc, (2 * A_WIDTH + B_WIDTH) // tc

    def body(a_ref, g_ref, w_ref, cb_ref, o_ref, hpad):
        hpad[0:CONV_PAD, :] = jnp.zeros((CONV_PAD, tc), F32)

        def fill(i, _):
            r0 = pl.multiple_of(i * rows, rows)
            hpad[pl.ds(CONV_PAD + r0, rows), :] = a_ref[pl.ds(r0, rows), :] * _sigmoid(g_ref[pl.ds(r0, rows), :])
            return 0
        lax.fori_loop(0, t // rows, fill, 0)

        def conv(i, _):
            r0 = pl.multiple_of(i * rows, rows)
            win = hpad[pl.ds(r0, rows + CONV_PAD), :]
            acc = jnp.zeros((rows, tc), F32) + cb_ref[...]
            for b in range(SUB):
                wb = win if b == 0 else pltpu.roll(win, b, 0)
                for a in range(CONV_PAD // SUB):
                    k = CONV_WIDTH - 1 - (SUB * a + b)
                    if k >= 0:
                        lo = CONV_PAD - SUB * a
                        acc = acc + wb[lo:lo + rows, :] * w_ref[k:k + 1, :]
            o_ref[pl.ds(r0, rows), :] = acc
            return 0
        lax.fori_loop(0, t // rows, conv, 0)

    return pl.pallas_call(
        body, grid=(B_WIDTH // tc,),
        in_specs=[pl.BlockSpec((t, tc), lambda j: (0, a_cb + j)), pl.BlockSpec((t, tc), lambda j: (0, g_cb + j)),
                  pl.BlockSpec((CONV_WIDTH, tc), lambda j: (0, j)), pl.BlockSpec((1, tc), lambda j: (0, j))],
        out_specs=pl.BlockSpec((t, tc), lambda j: (0, j)), out_shape=S((t, B_WIDTH), F32),
        scratch_shapes=[pltpu.VMEM((t + CONV_PAD, tc), F32)],
        compiler_params=_cp("parallel"), name=name)(proj, proj, w, cb)


def conv_bwd(name, proj, dhc, w):
    t = proj.shape[0]
    tc = LANES
    rows = _tile(t, CONV_ROWS_BWD)
    a_cb, g_cb = 2 * A_WIDTH // tc, (2 * A_WIDTH + B_WIDTH) // tc
    win_rows = rows + CONV_PAD

    def body(a_ref, g_ref, d_ref, w_ref, da_ref, dg_ref, dw_ref, dcb_ref, hpad, dpad, dwacc):
        hpad[0:CONV_PAD, :] = jnp.zeros((CONV_PAD, tc), F32)
        dpad[t:t + CONV_PAD, :] = jnp.zeros((CONV_PAD, tc), F32)
        dwacc[...] = jnp.zeros(dwacc.shape, F32)

        def fill(i, _):
            r0 = pl.multiple_of(i * rows, rows)
            hpad[pl.ds(CONV_PAD + r0, rows), :] = a_ref[pl.ds(r0, rows), :] * _sigmoid(g_ref[pl.ds(r0, rows), :])
            dpad[pl.ds(r0, rows), :] = d_ref[pl.ds(r0, rows), :]
            return 0
        lax.fori_loop(0, t // rows, fill, 0)

        def step(i, dcb):
            r0 = pl.multiple_of(i * rows, rows)
            hwin = hpad[pl.ds(r0, win_rows), :]
            dwin = dpad[pl.ds(r0, win_rows), :]
            dchunk = dwin[:rows, :]
            dh = jnp.zeros((rows, tc), F32)
            for b in range(SUB):
                hb = hwin if b == 0 else pltpu.roll(hwin, b, 0)
                db = dwin if b == 0 else pltpu.roll(dwin, win_rows - b, 0)
                for a in range(CONV_PAD // SUB):
                    k = CONV_WIDTH - 1 - (SUB * a + b)
                    if k >= 0:
                        dh = dh + db[SUB * a:SUB * a + rows, :] * w_ref[k:k + 1, :]
                        lo = CONV_PAD - SUB * a
                        prod = dchunk * hb[lo:lo + rows, :]
                        dwacc[k] += jnp.sum(prod.reshape(rows // 8, 8, tc), axis=0)
            a = a_ref[pl.ds(r0, rows), :]
            sg = _sigmoid(g_ref[pl.ds(r0, rows), :])
            da_ref[pl.ds(r0, rows), :] = (dh * sg).astype(da_ref.dtype)
            dg_ref[pl.ds(r0, rows), :] = (dh * a * sg * (1.0 - sg)).astype(dg_ref.dtype)
            return dcb + jnp.sum(dchunk, axis=0, keepdims=True)
        dcb = lax.fori_loop(0, t // rows, step, jnp.zeros((1, tc), F32))
        dcb_ref[...] = dcb
        for k in range(CONV_WIDTH):
            dw_ref[k:k + 1, :] = jnp.sum(dwacc[k], axis=0, keepdims=True)

    return pl.pallas_call(
        body, grid=(B_WIDTH // tc,),
        in_specs=[pl.BlockSpec((t, tc), lambda j: (0, a_cb + j)), pl.BlockSpec((t, tc), lambda j: (0, g_cb + j)),
                  pl.BlockSpec((t, tc), lambda j: (0, j)), pl.BlockSpec((CONV_WIDTH, tc), lambda j: (0, j))],
        out_specs=[pl.BlockSpec((t, tc), lambda j: (0, j)), pl.BlockSpec((t, tc), lambda j: (0, j)),
                   pl.BlockSpec((CONV_WIDTH, tc), lambda j: (0, j)), pl.BlockSpec((1, tc), lambda j: (0, j))],
        out_shape=[S((t, B_WIDTH), BF16), S((t, B_WIDTH), BF16), S((CONV_WIDTH, B_WIDTH), F32), S((1, B_WIDTH), F32)],
        scratch_shapes=[pltpu.VMEM((t + CONV_PAD, tc), F32), pltpu.VMEM((t + CONV_PAD, tc), F32),
                        pltpu.VMEM((CONV_WIDTH, 8, tc), F32)],
        compiler_params=_cp("parallel"), name=name)(proj, proj, dhc, w)


def ln_silu_bwd(name, hc, dxb, w_out, g, b):
    c = hc.shape[1]

    def fn(h, dxv, wv, gv, bv):
        dout = lax.dot_general(dxv, wv[A_WIDTH:, :], _NT, preferred_element_type=F32)
        y, rstd = _ln_plain(h)
        z = y * gv + bv
        s = _sigmoid(z)
        dz = dout * s * (1.0 + z * (1.0 - s))
        dyv = dz * gv
        dh = rstd * (dyv - jnp.mean(dyv, axis=-1, keepdims=True) - y * jnp.mean(dyv * y, axis=-1, keepdims=True))
        return [dh], [jnp.sum(dz * y, axis=0, keepdims=True), jnp.sum(dz, axis=0, keepdims=True)]

    return rows_call(name, fn, [hc, dxb], [w_out, g, b], [(c, F32)], [(1, c), (1, c)])


_NT = (((1,), (1,)), ((), ()))
_TN = (((0,), (0,)), ((), ()))


def attn_fwd(name, x, gain, wq, k, v, wo, tm=1024):
    t, d = x.shape
    m = k.shape[0]
    tm = _tile(t, tm)
    scale = CA_HEAD_DIM ** -0.5

    def body(x_ref, g_ref, wq_ref, k_ref, v_ref, wo_ref, x1_ref, xn_ref, r_ref, q_ref, o_ref):
        xv = x_ref[...]
        rv = lax.rsqrt(jnp.mean(xv * xv, axis=-1, keepdims=True) + EPS)
        xn = (xv * rv * g_ref[...]).astype(BF16)
        xn_ref[...] = xn
        r_ref[...] = rv
        q_ref[...] = jnp.dot(xn, wq_ref[...], preferred_element_type=F32).astype(BF16)
        for h in range(CA_HEADS):
            cs = slice(h * CA_HEAD_DIM, (h + 1) * CA_HEAD_DIM)
            s = lax.dot_general(q_ref[:, cs], k_ref[:, cs], _NT, preferred_element_type=F32) * scale
            e = jnp.exp(s - jnp.max(s, axis=-1, keepdims=True))
            p = e / jnp.sum(e, axis=-1, keepdims=True)
            o_ref[:, cs] = jnp.dot(_bf(p), v_ref[:, cs], preferred_element_type=F32).astype(o_ref.dtype)
        x1_ref[...] = xv + jnp.dot(o_ref[...], wo_ref[...], preferred_element_type=F32)

    def whole(a):
        return pl.BlockSpec(a.shape, lambda i: (0, 0), pipeline_mode=pl.Buffered(1))

    rows = pl.BlockSpec((tm, d), lambda i: (i, 0))
    col = pl.BlockSpec((tm, 1), lambda i: (i, 0))
    return pl.pallas_call(
        body, grid=(t // tm,),
        in_specs=[rows, whole(gain), whole(wq), whole(k), whole(v), whole(wo)],
        out_specs=[rows, rows, col, rows, rows],
        out_shape=[S((t, d), F32), S((t, d), BF16), S((t, 1), F32), S((t, d), BF16), S((t, d), BF16)],
        compiler_params=_cp("parallel"), name=name)(x, gain, wq, k, v, wo)


def attn_bwd(name, dx, dxb, x, r, gain, q, k, v, wq, wo, tm=512):
    t, d = q.shape
    m = k.shape[0]
    tm = _tile(t, tm)
    scale = CA_HEAD_DIM ** -0.5

    def body(dx_ref, dxb_ref, x_ref, r_ref, g_ref, q_ref, k_ref, v_ref, wq_ref, wo_ref,
             dxo_ref, dxbo_ref, dq_ref, dk_ref, dv_ref, dg_ref, do_s):
        @pl.when(pl.program_id(0) == 0)
        def _():
            dk_ref[...] = jnp.zeros(dk_ref.shape, F32)
            dv_ref[...] = jnp.zeros(dv_ref.shape, F32)
            dg_ref[...] = jnp.zeros(dg_ref.shape, F32)

        do_s[...] = lax.dot_general(dxb_ref[...], wo_ref[...], _NT, preferred_element_type=F32).astype(BF16)
        for h in range(CA_HEADS):
            cs = slice(h * CA_HEAD_DIM, (h + 1) * CA_HEAD_DIM)
            qh, kh, vh, doh = q_ref[:, cs], k_ref[:, cs], v_ref[:, cs], do_s[:, cs]
            s = lax.dot_general(qh, kh, _NT, preferred_element_type=F32) * scale
            e = jnp.exp(s - jnp.max(s, axis=-1, keepdims=True))
            p = e / jnp.sum(e, axis=-1, keepdims=True)
            pb = _bf(p)
            dv_ref[:, cs] += lax.dot_general(pb, doh, _TN, preferred_element_type=F32)
            dp = lax.dot_general(doh, vh, _NT, preferred_element_type=F32)
            ds = _bf(p * (dp - jnp.sum(dp * p, axis=-1, keepdims=True)) * scale)
            dq_ref[:, cs] = jnp.dot(ds, kh, preferred_element_type=F32).astype(dq_ref.dtype)
            dk_ref[:, cs] += lax.dot_general(ds, qh, _TN, preferred_element_type=F32)
        dxn = lax.dot_general(dq_ref[...], wq_ref[...], _NT, preferred_element_type=F32)
        xh = x_ref[...] * r_ref[...]
        wv = dxn * g_ref[...]
        dxo = dx_ref[...] + r_ref[...] * (wv - xh * jnp.mean(wv * xh, axis=-1, keepdims=True))
        dxo_ref[...] = dxo
        dxbo_ref[...] = dxo.astype(BF16)
        dg_ref[...] += jnp.sum(dxn * xh, axis=0, keepdims=True)

    def whole(a):
        return pl.BlockSpec(a.shape, lambda i: (0, 0), pipeline_mode=pl.Buffered(1))

    rows = pl.BlockSpec((tm, d), lambda i: (i, 0))
    col = pl.BlockSpec((tm, 1), lambda i: (i, 0))
    acc = pl.BlockSpec((m, d), lambda i: (0, 0))
    return pl.pallas_call(
        body, grid=(t // tm,),
        in_specs=[rows, rows, rows, col, whole(gain), rows, whole(k), whole(v), whole(wq), whole(wo)],
        out_specs=[rows, rows, rows, acc, acc, pl.BlockSpec((1, d), lambda i: (0, 0))],
        out_shape=[S((t, d), F32), S((t, d), BF16), S((t, d), BF16), S((m, d), F32), S((m, d), F32), S((1, d), F32)],
        scratch_shapes=[pltpu.VMEM((tm, d), BF16)],
        compiler_params=_cp("arbitrary"), name=name)(dx, dxb, x, r, gain, q, k, v, wq, wo)


SUB = 8
S5_ROWS = 256


S5_BLOCKS = 4
BLOCK_CH = C_WIDTH // S5_BLOCKS
BLOCK_ST = N_STATE // S5_BLOCKS
_S5_BLOCKS = tuple((slice(BLOCK_CH * q, BLOCK_CH * (q + 1)), slice(BLOCK_ST * q, BLOCK_ST * (q + 1)),
                    slice(N_STATE + BLOCK_ST * q, N_STATE + BLOCK_ST * (q + 1))) for q in range(S5_BLOCKS))
_HI = lax.Precision.HIGHEST
_GP = (C_GROUPS, C_STATE)
_RP = (C_WIDTH, C_STATE)


def _zoh(lr, li, ldt):
    dt = jnp.exp(ldt)
    mag = jnp.exp(lr * dt)
    ar = mag * jnp.cos(li * dt)
    ai = mag * jnp.sin(li * dt)
    den = lr * lr + li * li
    qr = ((ar - 1.0) * lr + ai * li) / den
    qi = (ai * lr - (ar - 1.0) * li) / den
    return dt, ar, ai, den, qr, qi


def _per_channel(v):
    return jnp.broadcast_to(v[:, None, :], (C_GROUPS, C_GROUP_CH, C_STATE)).reshape(_RP)


def _same_group(shape, row_per_group, col_per_group):
    rows = lax.broadcasted_iota(jnp.int32, shape, 0) // row_per_group
    cols = lax.broadcasted_iota(jnp.int32, shape, 1) // col_per_group
    return rows == cols


def _spread(shape, axis):
    long = lax.broadcasted_iota(jnp.int32, shape, axis) % C_STATE
    short = lax.broadcasted_iota(jnp.int32, shape, 1 - axis)
    return long == short


def s5_discretise(name, lam_re, lam_im, log_dt, bt_re, bt_im):
    def body(lr_ref, li_ref, ldt_ref, btr_ref, bti_ref, a_ref, bbr_ref, bbi_ref):
        _, ar, ai, _, qr, qi = _zoh(lr_ref[...], li_ref[...], ldt_ref[...])
        a_ref[0] = ar
        a_ref[1] = ai
        q2r, q2i = _per_channel(qr), _per_channel(qi)
        btr, bti = btr_ref[...], bti_ref[...]
        bbr_ref[...] = q2r * btr - q2i * bti
        bbi_ref[...] = q2r * bti + q2i * btr

    return pl.pallas_call(body, out_shape=[S((2,) + _GP, F32), S(_RP, F32), S(_RP, F32)],
                          name=name)(lam_re, lam_im, log_dt, bt_re, bt_im)


def s5_operands(name, a, bbr, bbi, c2r, c2i, ctr, cti):
    ns = N_STATE

    def body(a_ref, bbr_ref, bbi_ref, c2r_ref, c2i_ref, ctr_ref, cti_ref, pw_ref, qw_ref, mb_ref, mc_ref, mct_ref):
        ar, ai = a_ref[0:1, :], a_ref[1:2, :]
        pows = [(ar, ai)]
        for _ in range(SUB - 1):
            pr, pi = pows[-1]
            pows.append((pr * ar - pi * ai, pr * ai + pi * ar))
        rows = lax.broadcasted_iota(jnp.int32, (SUB, ns), 0)

        def rows_of(v):
            return jnp.broadcast_to(v, (SUB, ns))

        for k, s in enumerate((1, 2, 4)):
            pr, pi = rows_of(pows[s - 1][0]), rows_of(pows[s - 1][1])
            pw_ref[k, 0] = jnp.where(rows >= s, pr, 0.0)
            pw_ref[k, 1] = jnp.where(rows >= s, pi, 0.0)
            qw_ref[k, 0] = jnp.where(rows + s <= SUB - 1, pr, 0.0)
            qw_ref[k, 1] = jnp.where(rows + s <= SUB - 1, -pi, 0.0)
        fr = fi = br = bi = jnp.zeros((SUB, ns), F32)
        for i in range(SUB):
            fr = jnp.where(rows == i, rows_of(pows[i][0]), fr)
            fi = jnp.where(rows == i, rows_of(pows[i][1]), fi)
            br = jnp.where(rows == i, rows_of(pows[SUB - 1 - i][0]), br)
            bi = jnp.where(rows == i, rows_of(-pows[SUB - 1 - i][1]), bi)
        pw_ref[3, 0], pw_ref[3, 1], qw_ref[3, 0], qw_ref[3, 1] = fr, fi, br, bi

        wide = _spread((C_STATE, ns), 1).astype(BF16)
        tall = _spread((ns, C_STATE), 0).astype(BF16)
        in_rows = _same_group((C_WIDTH, ns), C_GROUP_CH, C_STATE)
        in_cols = _same_group((ns, C_WIDTH), C_STATE, C_GROUP_CH)

        def across(v, sign=1.0):
            return jnp.where(in_rows, sign * jnp.dot(_bf(v), wide, preferred_element_type=F32), 0.0).astype(BF16)

        def down(vt, sign=1.0):
            return jnp.where(in_cols, sign * jnp.dot(tall, _bf(vt), preferred_element_type=F32), 0.0).astype(BF16)

        mb_ref[:, 0:ns] = across(bbr_ref[...])
        mb_ref[:, ns:2 * ns] = across(bbi_ref[...])
        mct_ref[:, 0:ns] = across(c2r_ref[...])
        mct_ref[:, ns:2 * ns] = across(c2i_ref[...], -1.0)
        mc_ref[0:ns, :] = down(ctr_ref[...])
        mc_ref[ns:2 * ns, :] = down(cti_ref[...], -1.0)

    return pl.pallas_call(
        body, out_shape=[S((4, 2, SUB, ns), F32), S((4, 2, SUB, ns), F32), S((C_WIDTH, 2 * ns), BF16),
                         S((2 * ns, C_WIDTH), BF16), S((C_WIDTH, 2 * ns), BF16)],
        compiler_params=pltpu.CompilerParams(vmem_limit_bytes=VMEM_LIMIT), name=name)(a, bbr, bbi, c2r, c2i, ctr, cti)


def s5_block_grads(name, u, lamb, xsb, dyb):
    t = u.shape[0]

    def mb_body(u_ref, lr_ref, li_ref, o_ref):
        ub = _bf(u_ref[...])
        o_ref[:, 0:BLOCK_ST] = lax.dot_general(ub, lr_ref[...], _TN, preferred_element_type=F32)
        o_ref[:, BLOCK_ST:2 * BLOCK_ST] = lax.dot_general(ub, li_ref[...], _TN, preferred_element_type=F32)

    d_mb = pl.pallas_call(
        mb_body, grid=(S5_BLOCKS,),
        in_specs=[pl.BlockSpec((t, BLOCK_CH), lambda q: (0, q)), pl.BlockSpec((t, BLOCK_ST), lambda q: (0, q)),
                  pl.BlockSpec((t, BLOCK_ST), lambda q: (0, S5_BLOCKS + q))],
        out_specs=pl.BlockSpec((BLOCK_CH, 2 * BLOCK_ST), lambda q: (q, 0)), out_shape=S((C_WIDTH, 2 * BLOCK_ST), F32),
        compiler_params=_cp("parallel"), name=name + "_b")(u, lamb, lamb)

    def mc_body(x_ref, dy_ref, o_ref):
        o_ref[...] = lax.dot_general(x_ref[...], dy_ref[...], _TN, preferred_element_type=F32)

    d_mc = pl.pallas_call(
        mc_body, grid=(2, S5_BLOCKS),
        in_specs=[pl.BlockSpec((t, BLOCK_ST), lambda p, q: (0, p * S5_BLOCKS + q)), pl.BlockSpec((t, BLOCK_CH), lambda p, q: (0, q))],
        out_specs=pl.BlockSpec((BLOCK_ST, BLOCK_CH), lambda p, q: (p * S5_BLOCKS + q, 0)),
        out_shape=S((2 * N_STATE, BLOCK_CH), F32), compiler_params=_cp("parallel", "parallel"), name=name + "_c")(xsb, dyb)
    return d_mb, d_mc


def s5_param_grads(name, d_mb, d_mc, da, lam_re, lam_im, log_dt, bt_re, bt_im):
    ns = N_STATE

    def body(dmb_ref, dmc_ref, da_ref, lr_ref, li_ref, ldt_ref, btr_ref, bti_ref,
             glr_ref, gli_ref, gdt_ref, gbr_ref, gbi_ref, gcr_ref, gci_ref):
        lr, li = lr_ref[...], li_ref[...]
        dt, ar, ai, den, qr, qi = _zoh(lr, li, ldt_ref[...])
        per_block = C_GROUPS // S5_BLOCKS
        wide = _spread((C_STATE, BLOCK_ST), 1).astype(F32)
        tall = _spread((BLOCK_ST, C_STATE), 0).astype(F32)
        rows = lax.broadcasted_iota(jnp.int32, (C_WIDTH, BLOCK_ST), 0) // C_GROUP_CH % per_block
        in_rows = rows == lax.broadcasted_iota(jnp.int32, (C_WIDTH, BLOCK_ST), 1) // C_STATE
        in_cols = _same_group((BLOCK_ST, BLOCK_CH), C_STATE, C_GROUP_CH)

        def fold_rows(v):
            return lax.dot_general(jnp.where(in_rows, v, 0.0), wide, (((1,), (1,)), ((), ())), precision=_HI,
                                   preferred_element_type=F32)

        def fold_cols(v):
            return lax.dot_general(jnp.where(in_cols, v, 0.0), tall, (((0,), (0,)), ((), ())), precision=_HI,
                                   preferred_element_type=F32)

        for cs, s_re, s_im in _S5_BLOCKS:
            gcr_ref[cs, :] = fold_cols(dmc_ref[s_re, :])
            gci_ref[cs, :] = -fold_cols(dmc_ref[s_im, :])
        gbbr = fold_rows(dmb_ref[:, 0:BLOCK_ST])
        gbbi = fold_rows(dmb_ref[:, BLOCK_ST:2 * BLOCK_ST])
        btr, bti = btr_ref[...], bti_ref[...]
        q2r, q2i = _per_channel(qr), _per_channel(qi)
        gbr_ref[...] = q2r * gbbr + q2i * gbbi
        gbi_ref[...] = q2r * gbbi - q2i * gbbr

        def per_group(v):
            return jnp.sum(v.reshape(C_GROUPS, C_GROUP_CH, C_STATE), axis=1)

        gqr = per_group(btr * gbbr + bti * gbbi)
        gqi = per_group(btr * gbbi - bti * gbbr)
        ilr, ili = lr / den, li / den
        gar = da_ref[0] + ilr * gqr - ili * gqi
        gai = da_ref[1] + ilr * gqi + ili * gqr
        sr = (qr * lr + qi * li) / den
        si = (qi * lr - qr * li) / den
        gzr = ar * gar + ai * gai
        gzi = ar * gai - ai * gar
        glr_ref[...] = -sr * gqr - si * gqi + dt * gzr
        gli_ref[...] = -sr * gqi + si * gqr + dt * gzi
        gdt_ref[...] = jnp.sum(lr * gzr + li * gzi, axis=1, keepdims=True) * dt

    return pl.pallas_call(
        body, out_shape=[S(_GP, F32), S(_GP, F32), S((C_GROUPS, 1), F32), S(_RP, F32), S(_RP, F32), S(_RP, F32), S(_RP, F32)],
        compiler_params=pltpu.CompilerParams(vmem_limit_bytes=VMEM_LIMIT), name=name,
    )(d_mb, d_mc, da, lam_re, lam_im, log_dt, bt_re, bt_im)


def _cmul_add(xr, xi, pr, pi, zr, zi):
    return xr + pr * zr - pi * zi, xi + pr * zi + pi * zr


def s5_fwd(name, x, gain, w_in, mb, mc, pw, dskip, w_out_t):
    t, d = x.shape
    tm = _tile(t, S5_ROWS)
    ns = N_STATE

    def body(x_ref, g_ref, wi_ref, mb_ref, mc_ref, pw_ref, d_ref, wo_ref,
             x1_ref, hn_ref, r_ref, u_ref, gy_ref, y_ref, xs_ref, xb_ref, o1_ref, o2_ref, carry):
        @pl.when(pl.program_id(0) == 0)
        def _():
            carry[...] = jnp.zeros(carry.shape, F32)

        xv = x_ref[...]
        rv = lax.rsqrt(jnp.mean(xv * xv, axis=-1, keepdims=True) + EPS)
        hn = (xv * rv * g_ref[...]).astype(BF16)
        hn_ref[...] = hn
        r_ref[...] = rv
        uv = jnp.dot(hn, wi_ref[...], preferred_element_type=F32)
        u_ref[...] = uv
        ub = _bf(uv)
        for cs, s_re, s_im in _S5_BLOCKS:
            xs_ref[:, s_re] = jnp.dot(ub[:, cs], mb_ref[cs, s_re], preferred_element_type=F32)
            xs_ref[:, s_im] = jnp.dot(ub[:, cs], mb_ref[cs, s_im], preferred_element_type=F32)

        def group(i, _):
            r0 = pl.multiple_of(i * SUB, SUB)
            xr = xs_ref[pl.ds(r0, SUB), 0:ns]
            xi = xs_ref[pl.ds(r0, SUB), ns:2 * ns]
            for k, s in enumerate((1, 2, 4)):
                xr, xi = _cmul_add(xr, xi, pw_ref[k, 0], pw_ref[k, 1], pltpu.roll(xr, s, 0), pltpu.roll(xi, s, 0))
            xr, xi = _cmul_add(xr, xi, pw_ref[3, 0], pw_ref[3, 1], carry[0], carry[1])
            xs_ref[pl.ds(r0, SUB), 0:ns] = xr
            xs_ref[pl.ds(r0, SUB), ns:2 * ns] = xi
            carry[0] = jnp.broadcast_to(xr[SUB - 1:SUB, :], (SUB, ns))
            carry[1] = jnp.broadcast_to(xi[SUB - 1:SUB, :], (SUB, ns))
            return 0
        lax.fori_loop(0, tm // SUB, group, 0)

        xb_ref[...] = _bf(xs_ref[...])
        for cs, s_re, s_im in _S5_BLOCKS:
            y = (jnp.dot(xb_ref[:, s_re], mc_ref[s_re, cs], preferred_element_type=F32)
                 + jnp.dot(xb_ref[:, s_im], mc_ref[s_im, cs], preferred_element_type=F32) + d_ref[:, cs] * uv[:, cs])
            y_ref[:, cs] = y
            gy_ref[:, cs] = _gelu(y).astype(gy_ref.dtype)
        o1 = lax.dot_general(gy_ref[...], wo_ref[0:d, :], _NT, preferred_element_type=F32)
        o2 = lax.dot_general(gy_ref[...], wo_ref[d:2 * d, :], _NT, preferred_element_type=F32)
        o1_ref[...] = o1.astype(BF16)
        o2_ref[...] = o2.astype(BF16)
        x1_ref[...] = xv + o1 * _sigmoid(o2)

    c = w_in.shape[1]
    rows = pl.BlockSpec((tm, d), lambda i: (i, 0))
    narrow = pl.BlockSpec((tm, c), lambda i: (i, 0))
    states = pl.BlockSpec((tm, 2 * ns), lambda i: (i, 0))
    return pl.pallas_call(
        body, grid=(t // tm,),
        in_specs=[rows, _whole(gain), _whole(w_in), _whole(mb), _whole(mc), _whole(pw), _whole(dskip), _whole(w_out_t)],
        out_specs=[rows, rows, pl.BlockSpec((tm, 1), lambda i: (i, 0)), narrow, narrow, narrow, states, states, rows, rows],
        out_shape=[S((t, d), F32), S((t, d), BF16), S((t, 1), F32), S((t, c), F32), S((t, c), BF16), S((t, c), F32),
                   S((t, 2 * ns), F32), S((t, 2 * ns), BF16), S((t, d), BF16), S((t, d), BF16)],
        scratch_shapes=[pltpu.VMEM((2, SUB, ns), F32)],
        compiler_params=_cp("arbitrary"), name=name)(x, gain, w_in, mb, mc, pw, dskip, w_out_t)


def s5_bwd(name, dgy, y, u, xs, mct, mbt, qw, dskip):
    t, c = u.shape
    tm = _tile(t, S5_ROWS)
    nt = t // tm
    ns = N_STATE
    ng = tm // SUB

    def body(dgy_ref, y_ref, u_ref, xs_ref, mct_ref, mbt_ref, qw_ref, d_ref,
             du_ref, dy_ref, lb_ref, da_ref, dd_ref, lam, carry):
        @pl.when(pl.program_id(0) == 0)
        def _():
            carry[...] = jnp.zeros(carry.shape, F32)
            da_ref[...] = jnp.zeros(da_ref.shape, F32)
            dd_ref[...] = jnp.zeros(dd_ref.shape, F32)

        uv = u_ref[...]
        dy = dgy_ref[...] * _gelu_grad(y_ref[...])
        dyb = _bf(dy)
        dy_ref[...] = dyb
        dd_ref[...] += jnp.sum(dy * uv, axis=0, keepdims=True)
        for cs, s_re, s_im in _S5_BLOCKS:
            lam[:, s_re] = jnp.dot(dyb[:, cs], mct_ref[cs, s_re], preferred_element_type=F32)
            lam[:, s_im] = jnp.dot(dyb[:, cs], mct_ref[cs, s_im], preferred_element_type=F32)
        last_row = lax.broadcasted_iota(jnp.int32, (SUB, ns), 0) == SUB - 1

        def group(j, _):
            i = ng - 1 - j
            r0 = pl.multiple_of(i * SUB, SUB)
            lr = lam[pl.ds(r0, SUB), 0:ns]
            li = lam[pl.ds(r0, SUB), ns:2 * ns]
            for k, s in enumerate((1, 2, 4)):
                lr, li = _cmul_add(lr, li, qw_ref[k, 0], qw_ref[k, 1],
                                   pltpu.roll(lr, SUB - s, 0), pltpu.roll(li, SUB - s, 0))
            cr, ci = carry[0], carry[1]
            lr, li = _cmul_add(lr, li, qw_ref[3, 0], qw_ref[3, 1], cr, ci)
            lam[pl.ds(r0, SUB), 0:ns] = lr
            lam[pl.ds(r0, SUB), ns:2 * ns] = li
            carry[0] = jnp.broadcast_to(lr[0:1, :], (SUB, ns))
            carry[1] = jnp.broadcast_to(li[0:1, :], (SUB, ns))
            nr = jnp.where(last_row, cr, pltpu.roll(lr, SUB - 1, 0))
            ni = jnp.where(last_row, ci, pltpu.roll(li, SUB - 1, 0))
            xr = xs_ref[pl.ds(r0, SUB), 0:ns]
            xi = xs_ref[pl.ds(r0, SUB), ns:2 * ns]
            da_ref[0] += nr * xr + ni * xi
            da_ref[1] += ni * xr - nr * xi
            return 0
        lax.fori_loop(0, ng, group, 0)

        lb_ref[...] = _bf(lam[...])
        for cs, s_re, s_im in _S5_BLOCKS:
            du = (jnp.dot(lb_ref[:, s_re], mbt_ref[s_re, cs], preferred_element_type=F32)
                  + jnp.dot(lb_ref[:, s_im], mbt_ref[s_im, cs], preferred_element_type=F32) + d_ref[:, cs] * dy[:, cs])
            du_ref[:, cs] = du.astype(du_ref.dtype)

    rev = lambda i: (nt - 1 - i, 0)
    return pl.pallas_call(
        body, grid=(nt,),
        in_specs=[pl.BlockSpec((tm, c), rev), pl.BlockSpec((tm, c), rev), pl.BlockSpec((tm, c), rev),
                  pl.BlockSpec((tm, 2 * ns), rev),
                  pl.BlockSpec(mct.shape, lambda i: (0, 0)), pl.BlockSpec(mbt.shape, lambda i: (0, 0)),
                  pl.BlockSpec(qw.shape, lambda i: (0, 0, 0, 0)), pl.BlockSpec((1, c), lambda i: (0, 0))],
        out_specs=[pl.BlockSpec((tm, c), rev), pl.BlockSpec((tm, c), rev), pl.BlockSpec((tm, 2 * ns), rev),
                   pl.BlockSpec((2, SUB, ns), lambda i: (0, 0, 0)), pl.BlockSpec((1, c), lambda i: (0, 0))],
        out_shape=[S((t, c), BF16), S((t, c), BF16), S((t, 2 * ns), BF16), S((2, SUB, ns), F32), S((1, c), F32)],
        scratch_shapes=[pltpu.VMEM((tm, 2 * ns), F32), pltpu.VMEM((2, SUB, ns), F32)],
        compiler_params=_cp("arbitrary"), name=name)(dgy, y, u, xs, mct, mbt, qw, dskip)


def _first(accs, *_):
    return [accs[0]]


def _rms_bwd_epi(accs, xv, base, rv, g):
    dv = accs[0]
    w = dv * g
    xh = xv * rv
    dx = base + rv * (w - xh * jnp.mean(w * xh, axis=-1, keepdims=True))
    return [dx, dx, jnp.sum(dv * xh, axis=0, keepdims=True)]


def mm_rms_bwd(name, pairs, x, r, gain, dres):
    t, d = x.shape
    return mm_nn(name, t, d, pairs, 1, _rms_bwd_epi, [F32, BF16], tiled=[x, dres], cols=[r], rowv=[gain], sums=[(1, d)])


def even_fwd(x, w, need_out):
    t = x.shape[0]
    proj, hn, r = mm_nn("e_in_f", t, IN_WIDTH, [(x, w["e_w_in_t"], 0, "t")], 1, _first, [F32], norm_gain=w["e_norm"])
    hc = conv_fwd("e_conv_f", proj, w["e_conv_w"], w["e_conv_b"])
    need_out(hc)
    x1, cat = even_out_fwd("e_out_f", proj, hc, x, w["e_gmlp_w"], w["e_gmlp_b"], w["e_conv_ln_g"], w["e_conv_ln_b"],
                           w["e_w_out"])
    return x1, (x, hn, r, proj, cat, hc)


def even_bwd_mixers(dxb, saved, w):
    x, hn, r, proj, cat, hc = saved
    t = x.shape[0]
    g_w_out = mm_tn("e_out_w", cat, dxb)
    dab, g_gw, g_gb = gmlp_bwd("e_gmlp_b", proj, dxb, w["e_w_out"], w["e_gmlp_w"], w["e_gmlp_b"])
    dhc, g_lg, g_lb = ln_silu_bwd("e_ln_b", hc, dxb, w["e_w_out"], w["e_conv_ln_g"], w["e_conv_ln_b"])
    dba, dbg, g_cw, g_cb = conv_bwd("e_conv_b", proj, dhc, w["e_conv_w"])
    g_w_in_t = jnp.concatenate([mm_tn("e_in_w0", dab, hn), mm_tn("e_in_w1", dba, hn), mm_tn("e_in_w2", dbg, hn)], axis=0)
    grads = dict(e_w_in_t=g_w_in_t, e_gmlp_w=g_gw[None], e_gmlp_b=g_gb.reshape(1, A_GROUPS, GMLP_BLOCK),
                 e_conv_w=g_cw[None], e_conv_b=g_cb, e_conv_ln_g=g_lg, e_conv_ln_b=g_lb, e_w_out=g_w_out)
    return (dab, dba, dbg), grads


def even_bwd_input(dx, dproj, saved, w):
    x, _, r = saved[:3]
    dab, dba, dbg = dproj
    w_in_t = w["e_w_in_t"]
    return mm_rms_bwd("e_in_b", [(dab, (w_in_t, 0), 0), (dba, (w_in_t, 2), 0), (dbg, (w_in_t, 3), 0)], x, r, w["e_norm"], dx)


def s5_setup(w, anchor=None):
    def rows(v):
        return v.transpose(0, 2, 1).reshape(_RP)

    log_dt = w["o_log_dt"].reshape(C_GROUPS, 1)
    if anchor is not None:
        log_dt = log_dt + anchor
    lam = (w["o_lam_re"], w["o_lam_im"], log_dt, rows(w["o_b_re"]), rows(w["o_b_im"]))
    a, bbr, bbi = s5_discretise("o_s5_zoh", *lam)
    c_re, c_im = w["o_c_re"], w["o_c_im"]
    pw, qw, mb, mc, mct = s5_operands("o_s5_ops", a.reshape(2, N_STATE), bbr, bbi, c_re.reshape(_RP), c_im.reshape(_RP),
                                      c_re.transpose(2, 0, 1).reshape(C_STATE, C_WIDTH),
                                      c_im.transpose(2, 0, 1).reshape(C_STATE, C_WIDTH))
    return dict(lam=lam, pw=pw, qw=qw, mb=mb, mc=mc, mct=mct, mbt=mb.T)


def odd_fwd(x, w, consts):
    x1, hn, r, u, gy, y, xs, xsb, o1, o2 = s5_fwd("o_s5_f", x, w["o_norm"], w["o_w_in"], consts["mb"], consts["mc"],
                                                  consts["pw"], w["o_d"], w["o_w_out_t"])
    return x1, (x, hn, r, u, gy, y, xs, xsb, o1, o2)


def odd_bwd(dx, dxb, saved, w, consts):
    x, hn, r, u, gy, y, xs, xsb, o1, o2 = saved
    t = x.shape[0]

    def gate_bwd(dv, a, b, wv):
        a = a.astype(F32)
        sg = _sigmoid(b.astype(F32))
        do12 = jnp.concatenate([dv * sg, dv * a * sg * (1.0 - sg)], axis=1).astype(BF16)
        return [do12, jnp.dot(do12, wv, preferred_element_type=F32)], []

    do12, dgy = rows_call("o_out_b", gate_bwd, [dx, o1, o2], [w["o_w_out_t"]], [(2 * D_MODEL, BF16), (C_WIDTH, F32)], [])
    g_w_out_t = mm_tn("o_out_w", do12, gy)
    du, dyb, lamb, da8, g_d = s5_bwd("o_s5_b", dgy, y, u, xs, consts["mct"], consts["mbt"], consts["qw"], w["o_d"])
    d_mb, d_mc = s5_block_grads("o_s5_w", u, lamb, xsb, dyb)
    da = jnp.sum(da8, axis=1).reshape((2,) + _GP)
    g_lr, g_li, g_dt, g_btr, g_bti, g_cr, g_ci = s5_param_grads("o_s5_pg", d_mb, d_mc, da, *consts["lam"])

    def states_first(v):
        return v.reshape(C_GROUPS, C_GROUP_CH, C_STATE).transpose(0, 2, 1)[None]

    g_w_in = mm_tn("o_in_w", hn, du)
    dx0, dx0b, g_norm = mm_rms_bwd("o_in_b", [(du, w["o_w_in"], 0, "t")], x, r, w["o_norm"], dx)
    grads = dict(o_norm=g_norm, o_w_in=g_w_in, o_lam_re=g_lr[None], o_lam_im=g_li[None], o_log_dt=g_dt.reshape(1, C_GROUPS),
                 o_b_re=states_first(g_btr), o_b_im=states_first(g_bti),
                 o_c_re=g_cr.reshape((1, C_GROUPS, C_GROUP_CH, C_STATE)), o_c_im=g_ci.reshape((1, C_GROUPS, C_GROUP_CH, C_STATE)),
                 o_d=g_d, o_w_out_t=g_w_out_t)
    return dx0, dx0b, grads


def ca_fwd(i, x, mem, w):
    t, m = x.shape[0], mem.shape[0]
    k, v, mn, rm = mm_nn(f"ca{i}_kv_f", m, D_MODEL, [(mem, w["ca_wk"][i], 0), (mem, w["ca_wv"][i], 1)], 2,
                         lambda accs: [accs[0], accs[1]], [BF16, BF16], norm_gain=w["ca_mem_norm"][i:i + 1])
    x1, xn, r, q, o = attn_fwd(f"ca{i}_attn_f", x, w["ca_norm"][i:i + 1], w["ca_wq"][i], k, v, w["ca_wo"][i])
    return x1, (x, xn, r, mn, rm, q, k, v, o)


def ca_bwd(i, dx, dxb, saved, mem, w):
    x, xn, r, mn, rm, q, k, v, o = saved
    t, m = x.shape[0], mem.shape[0]
    g_wo = mm_tn(f"ca{i}_o_w", o, dxb)
    dx0, dx0b, dq, dk, dv, g_norm = attn_bwd(f"ca{i}_attn_b", dx, dxb, x, r, w["ca_norm"][i:i + 1], q, k, v,
                                             w["ca_wq"][i], w["ca_wo"][i])
    g_wq = mm_tn(f"ca{i}_q_w", xn, dq)
    g_wk = mm_tn(f"ca{i}_k_w", mn, dk)
    g_wv = mm_tn(f"ca{i}_v_w", mn, dv)
    (dmn,) = mm_nn(f"ca{i}_kv_b", m, D_MODEL, [(dk, w["ca_wk"][i], 0, "t"), (dv, w["ca_wv"][i], 0, "t")], 1, _first, [F32])
    g_mnorm = rms_bwd_gain_only(f"ca{i}_mnorm_b", dmn, mem, rm)
    return dx0, dx0b, dict(ca_norm=g_norm, ca_mem_norm=g_mnorm, ca_wq=g_wq, ca_wk=g_wk, ca_wv=g_wv, ca_wo=g_wo)


FFN_ROWS = 512
FFN_CHUNK = 256


def _whole(a):
    return pl.BlockSpec(a.shape, lambda i: (0,) * a.ndim, pipeline_mode=pl.Buffered(1))


def ffn_fused_fwd(name, x, gain, wg_t, wu_t, wd, target=None, final_gain=None):
    t, d = x.shape
    hid = wd.shape[0]
    tm = _tile(t, FFN_ROWS)
    last = target is not None
    n_main = 4 if last else 1

    def body(*refs):
        x_ref, g_ref, wg_ref, wu_ref, wd_ref = refs[:5]
        rest = refs[5:]
        if last:
            tgt_ref, fg_ref = rest[:2]
            rest = rest[2:]
        main, (xn_ref, r_ref, dgate_ref, dup_ref, h_ref) = rest[:n_main], rest[n_main:]
        xv = x_ref[...]
        rv = lax.rsqrt(jnp.mean(xv * xv, axis=-1, keepdims=True) + EPS)
        xn = (xv * rv * g_ref[...]).astype(BF16)
        xn_ref[...] = xn
        r_ref[...] = rv
        for j in range(hid // FFN_CHUNK):
            cs = slice(j * FFN_CHUNK, (j + 1) * FFN_CHUNK)
            g = lax.dot_general(xn, wg_ref[cs, :], _NT, preferred_element_type=F32)
            u = lax.dot_general(xn, wu_ref[cs, :], _NT, preferred_element_type=F32)
            s = _sigmoid(g)
            silu = g * s
            dgate_ref[:, cs] = (u * (s + silu * (1.0 - s))).astype(BF16)
            dup_ref[:, cs] = silu.astype(BF16)
            h_ref[:, cs] = (silu * u).astype(BF16)
        acc = jnp.dot(h_ref[...], wd_ref[...], preferred_element_type=F32)
        if not last:
            main[0][...] = xv + acc
        else:
            dx, _, dgain, part = _final_loss_epi([acc], xv, tgt_ref[...], fg_ref[...])

            @pl.when(pl.program_id(0) == 0)
            def _():
                main[2][...] = jnp.zeros(main[2].shape, F32)
                main[3][...] = jnp.zeros(main[3].shape, F32)
            main[0][...] = dx
            main[1][...] = dx.astype(BF16)
            main[2][...] += dgain
            main[3][...] += part

    rows = pl.BlockSpec((tm, d), lambda i: (i, 0))
    wide = pl.BlockSpec((tm, hid), lambda i: (i, 0))
    col = pl.BlockSpec((tm, 1), lambda i: (i, 0))
    ins, in_specs = [x, gain, wg_t, wu_t, wd], [rows, _whole(gain), _whole(wg_t), _whole(wu_t), _whole(wd)]
    if last:
        ins += [target, final_gain]
        in_specs += [rows, _whole(final_gain)]
        out_specs = [rows, rows, pl.BlockSpec((1, d), lambda i: (0, 0)), pl.BlockSpec((1, 1), lambda i: (0, 0))]
        out_shape = [S((t, d), F32), S((t, d), BF16), S((1, d), F32), S((1, 1), F32)]
    else:
        out_specs, out_shape = [rows], [S((t, d), F32)]
    out_specs += [rows, col, wide, wide, wide]
    out_shape += [S((t, d), BF16), S((t, 1), F32)] + [S((t, hid), BF16)] * 3
    outs = pl.pallas_call(body, grid=(t // tm,), in_specs=in_specs, out_specs=out_specs, out_shape=out_shape,
                          compiler_params=_cp("arbitrary" if last else "parallel"), name=name)(*ins)
    return (tuple(outs[:4]) if last else outs[0]), outs[n_main:]


def ffn_fused_bwd(name, dx, dxb, x, r, gain, dgate, dup, wg_t, wu_t, wd):
    t, d = x.shape
    hid = wd.shape[0]
    tm = _tile(t, FFN_ROWS // 2)

    def body(dx_ref, dxb_ref, x_ref, r_ref, g_ref, dgate_ref, dup_ref, wg_ref, wu_ref, wd_ref,
             dxo_ref, dxbo_ref, dg_ref, du_ref, dgain_ref):
        @pl.when(pl.program_id(0) == 0)
        def _():
            dgain_ref[...] = jnp.zeros(dgain_ref.shape, F32)

        dxb = dxb_ref[...]
        for j in range(hid // FFN_CHUNK):
            cs = slice(j * FFN_CHUNK, (j + 1) * FFN_CHUNK)
            dh = lax.dot_general(dxb, wd_ref[cs, :], _NT, preferred_element_type=F32)
            dg_ref[:, cs] = (dh * dgate_ref[:, cs].astype(F32)).astype(BF16)
            du_ref[:, cs] = (dh * dup_ref[:, cs].astype(F32)).astype(BF16)
        dxn = (jnp.dot(dg_ref[...], wg_ref[...], preferred_element_type=F32)
               + jnp.dot(du_ref[...], wu_ref[...], preferred_element_type=F32))
        dxo, _, dgain = _rms_bwd_epi([dxn], x_ref[...], dx_ref[...], r_ref[...], g_ref[...])
        dxo_ref[...] = dxo
        dxbo_ref[...] = dxo.astype(BF16)
        dgain_ref[...] += dgain

    rows = pl.BlockSpec((tm, d), lambda i: (i, 0))
    wide = pl.BlockSpec((tm, hid), lambda i: (i, 0))
    col = pl.BlockSpec((tm, 1), lambda i: (i, 0))
    return pl.pallas_call(
        body, grid=(t // tm,),
        in_specs=[rows, rows, rows, col, _whole(gain), wide, wide, _whole(wg_t), _whole(wu_t), _whole(wd)],
        out_specs=[rows, rows, wide, wide, pl.BlockSpec((1, d), lambda i: (0, 0))],
        out_shape=[S((t, d), F32), S((t, d), BF16), S((t, hid), BF16), S((t, hid), BF16), S((1, d), F32)],
        compiler_params=_cp("arbitrary"), name=name)(dx, dxb, x, r, gain, dgate, dup, wg_t, wu_t, wd)


def ffn_fwd(i, x, w, target=None):
    out, (xn, r, dgate, dup, h) = ffn_fused_fwd(f"ffn{i}_f", x, w["ffn_norm"][i:i + 1], w["ffn_w_gate_t"][i],
                                                w["ffn_w_up_t"][i], w["ffn_w_down"][i], target,
                                                None if target is None else w["final_norm"])
    return out, (x, xn, r, dgate, dup, h)


def ffn_bwd(i, dx, dxb, saved, w):
    x, xn, r, dgate, dup, h = saved
    g_wd = mm_tn(f"ffn{i}_down_w", h, dxb)
    dx0, dx0b, dg, du, g_norm = ffn_fused_bwd(f"ffn{i}_b", dx, dxb, x, r, w["ffn_norm"][i:i + 1], dgate, dup,
                                              w["ffn_w_gate_t"][i], w["ffn_w_up_t"][i], w["ffn_w_down"][i])
    g_wg_t = mm_tn(f"ffn{i}_gate_w", dg, xn)
    g_wu_t = mm_tn(f"ffn{i}_up_w", du, xn)
    return dx0, dx0b, dict(ffn_norm=g_norm, ffn_w_gate_t=g_wg_t, ffn_w_up_t=g_wu_t, ffn_w_down=g_wd)


def local_step(x, mem, target, w, fetch=None, on_grads=None, anchor=None):
    consts = s5_setup(w, anchor)

    def need(stage, after):
        if fetch is not None:
            for k, v in fetch(stage, after).items():
                if isinstance(k, tuple):
                    w.setdefault(k[0], {})[k[1]] = v
                else:
                    w[k] = v

    need(0, consts["pw"])
    def early(stage, after):
        if fetch is not None:
            fetch(stage, after, True)

    def before_out(after):
        need(1, after)
        early(2, after)

    x1, s_e = even_fwd(x, w, before_out)
    need(2, x1)
    x2, s_c0 = ca_fwd(0, x1, mem, w)
    need(3, x2)
    x3, s_f0 = ffn_fwd(0, x2, w)
    need(4, x3)
    x4, s_o = odd_fwd(x3, w, consts)
    need(5, x4)
    early(6, x4)
    x5, s_c1 = ca_fwd(1, x4, mem, w)
    need(6, x5)
    (dx, dxb, g_final, loss), s_f1 = ffn_fwd(1, x5, w, target)

    def emit(stage, carry, plain, layered=None, layer=0):
        if on_grads is None:
            return carry
        out = dict(plain)
        out.update({(k, layer): v for k, v in (layered or {}).items()})
        return on_grads(stage, out, list(carry))

    dx, dxb, g_f1 = ffn_bwd(1, dx, dxb, s_f1, w)
    dx, dxb = emit(0, (dx, dxb), {}, g_f1, 1)
    dx, dxb, g_c1 = ca_bwd(1, dx, dxb, s_c1, mem, w)
    dx, dxb, g_o = odd_bwd(dx, dxb, s_o, w, consts)
    dx, dxb = emit(1, (dx, dxb), g_o, g_c1, 1)
    dx, dxb, g_f0 = ffn_bwd(0, dx, dxb, s_f0, w)
    dx, dxb = emit(2, (dx, dxb), {}, g_f0, 0)
    dx, dxb, g_c0 = ca_bwd(0, dx, dxb, s_c0, mem, w)
    dx, dxb = emit(3, (dx, dxb), {}, g_c0, 0)
    dproj, g_e = even_bwd_mixers(dxb, s_e, w)
    dproj = emit(4, dproj, {**g_e, "o_norm": g_o["o_norm"], "o_d": g_o["o_d"]})
    dx, dxb, g_e["e_norm"] = even_bwd_input(dx, dproj, s_e, w)

    grads = dict(g_e)
    grads.update(g_o)
    for g0, g1 in ((g_c0, g_c1), (g_f0, g_f1)):
        for k in g0:
            grads[k] = jnp.concatenate([g0[k], g1[k]], axis=0) if k.endswith("norm") else (g0[k], g1[k])
    grads["final_norm"] = g_final
    return loss, dx, grads


def _group(axes):
    pos = {a: lax.axis_index(a) for a in ("x", "y", "c")}
    me = 0
    for a in axes:
        me = me * 2 + pos[a]
    peers = []
    for mask in range(1, 2 ** len(axes)):
        peer = dict(pos)
        for bit, a in enumerate(axes):
            if (mask >> (len(axes) - 1 - bit)) & 1:
                peer[a] = 1 - pos[a]
        idx = 0
        for a in axes:
            idx = idx * 2 + peer[a]
        peers.append((idx, (peer["x"], peer["y"], peer["c"])))
    return me, peers


def _sibling():
    x, y, c = lax.axis_index("x"), lax.axis_index("y"), lax.axis_index("c")
    return c, (x, y, 1 - c)


_HBM =pl.BlockSpec(memory_space=pltpu.HBM)
_SEM = pl.BlockSpec(memory_space=pltpu.SEMAPHORE)
_EFFECT = pltpu.SideEffectType.DATAFLOW_SIDE_EFFECTING


def _gather_peers(direct):
    chip, _ = _group(("x", "y"))
    core = lax.axis_index("c")
    if direct:
        _, peers = _group(_ALL)
        return chip, core, [(idx // 2, idx % 2, dev) for idx, dev in peers]
    _, peers = _group(("x", "y"))
    return chip, core, [(idx, core, dev) for idx, dev in peers]


def gather_ici_start(name, groups, direct):
    flat = [b for g in groups for b in g]
    sizes = [len(g) for g in groups]
    k_ops, n_g = len(flat), len(groups)
    lands = [lax.empty((4, 2) + tuple(b.shape), b.dtype) for b in flat]
    fan = [N_DEV - 1 if d else 3 for d in direct]

    def body(*refs):
        src, land = refs[:k_ops], refs[k_ops:2 * k_ops]
        sems = refs[2 * k_ops:2 * k_ops + 3 * n_g]
        token = refs[-1]
        i = 0
        for g in range(n_g):
            send, recv, loc = sems[3 * g:3 * g + 3]
            chip, core, peers = _gather_peers(direct[g])
            for j in range(sizes[g]):
                pltpu.make_async_copy(src[i], land[i].at[chip, core], loc.at[j]).start()
                for k, (_, _, dev) in enumerate(peers):
                    s = fan[g] * j + k
                    pltpu.make_async_remote_copy(src_ref=src[i], dst_ref=land[i].at[chip, core], send_sem=send.at[s],
                                                 recv_sem=recv.at[s], device_id=dev, device_id_type=MESH).start()
                i += 1
        token[...] = jnp.zeros(token.shape, token.dtype)

    sem_shapes = []
    for s, f in zip(sizes, fan):
        sem_shapes += [pltpu.SemaphoreType.DMA((f * s,)), pltpu.SemaphoreType.DMA((f * s,)), pltpu.SemaphoreType.DMA((s,))]
    thru = [pltpu.HBM(a.shape, a.dtype) for a in flat + lands]
    outs = pl.pallas_call(
        body, name=name, out_shape=tuple(sem_shapes) + tuple(thru) + (S((8, LANES), F32),),
        in_specs=[_HBM] * (2 * k_ops), out_specs=[_SEM] * (3 * n_g) + [_HBM] * (2 * k_ops) + [pl.BlockSpec(memory_space=pltpu.VMEM)],
        input_output_aliases={i: 3 * n_g + i for i in range(2 * k_ops)},
        compiler_params=pltpu.CompilerParams(has_side_effects=_EFFECT),
    )(*[pltpu.with_memory_space_constraint(a, pltpu.HBM) for a in flat + lands])
    sems = [tuple(outs[3 * g:3 * g + 3]) for g in range(n_g)]
    srcs_thru, lands_thru, off = [], [], 3 * n_g
    for s in sizes:
        srcs_thru.append(list(outs[off:off + s]))
        off += s
    for s in sizes:
        lands_thru.append(list(outs[off:off + s]))
        off += s
    return sems, srcs_thru, lands_thru, outs[-1]


def gather_ici_wait(name, srcs, lands, sems, after, direct=False):
    n = len(srcs)

    def body(*refs):
        src, land = refs[:n], refs[n:2 * n]
        send, recv, loc = refs[2 * n:2 * n + 3]
        chip, core, peers = _gather_peers(direct)
        for j in range(n):
            for k, (pchip, pcore, dev) in enumerate(peers):
                s = len(peers) * j + k
                cp = pltpu.make_async_remote_copy(src_ref=src[j], dst_ref=land[j].at[pchip, pcore], send_sem=send.at[s],
                                                  recv_sem=recv.at[s], device_id=dev, device_id_type=MESH)
                cp.wait_send()
                cp.wait_recv()
            pltpu.make_async_copy(src[j], land[j].at[chip, core], loc.at[j]).wait()

    outs = pl.pallas_call(
        body, name=name, out_shape=tuple(pltpu.HBM(a.shape, a.dtype) for a in list(srcs) + list(lands)),
        in_specs=[_HBM] * (2 * n) + [_SEM] * 3 + [ANY], out_specs=[_HBM] * (2 * n),
        input_output_aliases={i: i for i in range(2 * n)},
        compiler_params=pltpu.CompilerParams(has_side_effects=_EFFECT),
    )(*srcs, *lands, *sems, after)
    return list(outs[n:])


def gather_d2d(name, bufs):
    k_ops = len(bufs)

    def body(*refs):
        in_refs, out_refs = refs[:k_ops], refs[k_ops:2 * k_ops]
        send_sems, recv_sems = refs[2 * k_ops:]
        core, sib = _sibling()
        sent, landed = [], []
        for i in range(k_ops):
            cp = pltpu.make_async_remote_copy(src_ref=in_refs[i].at[:, core], dst_ref=out_refs[i].at[:, core],
                                              send_sem=send_sems.at[i], recv_sem=recv_sems.at[i], device_id=sib, device_id_type=MESH)
            cp.start()
            sent.append(cp)
            landed.append(pltpu.make_async_remote_copy(src_ref=in_refs[i].at[:, core], dst_ref=out_refs[i].at[:, 1 - core],
                                                       send_sem=send_sems.at[i], recv_sem=recv_sems.at[i],
                                                       device_id=sib, device_id_type=MESH))
        for cp in landed:
            cp.wait_recv()
        for cp in sent:
            cp.wait_send()

    return pl.pallas_call(
        body, in_specs=[ANY] * k_ops, out_specs=[ANY] * k_ops, out_shape=[S(b.shape, b.dtype) for b in bufs],
        input_output_aliases={i: i for i in range(k_ops)},
        scratch_shapes=[pltpu.SemaphoreType.DMA((k_ops,)), pltpu.SemaphoreType.DMA((k_ops,))],
        name=name)(*bufs)


def gather_d2d_start(name, bufs):
    k_ops = len(bufs)

    def body(*refs):
        in_refs = refs[:k_ops]
        send, recv = refs[k_ops], refs[k_ops + 1]
        core, sib = _sibling()
        for i in range(k_ops):
            pltpu.make_async_remote_copy(src_ref=in_refs[i].at[:, core], dst_ref=in_refs[i].at[:, core], send_sem=send.at[i],
                                         recv_sem=recv.at[i], device_id=sib, device_id_type=MESH).start()

    outs = pl.pallas_call(
        body, name=name,
        out_shape=(pltpu.SemaphoreType.DMA((k_ops,)), pltpu.SemaphoreType.DMA((k_ops,))) + tuple(pltpu.HBM(b.shape, b.dtype) for b in bufs),
        in_specs=[_HBM] * k_ops, out_specs=[_SEM, _SEM] + [_HBM] * k_ops,
        input_output_aliases={i: 2 + i for i in range(k_ops)},
        compiler_params=pltpu.CompilerParams(has_side_effects=_EFFECT),
    )(*[pltpu.with_memory_space_constraint(b, pltpu.HBM) for b in bufs])
    return (outs[0], outs[1]), list(outs[2:])


def gather_d2d_wait(name, bufs, sems, after):
    k_ops = len(bufs)

    def body(*refs):
        in_refs = refs[:k_ops]
        send, recv = refs[k_ops], refs[k_ops + 1]
        core, sib = _sibling()
        for i in range(k_ops):
            cp = pltpu.make_async_remote_copy(src_ref=in_refs[i].at[:, core], dst_ref=in_refs[i].at[:, 1 - core], send_sem=send.at[i],
                                              recv_sem=recv.at[i], device_id=sib, device_id_type=MESH)
            cp.wait_send()
            cp.wait_recv()

    outs = pl.pallas_call(
        body, name=name, out_shape=tuple(pltpu.HBM(b.shape, b.dtype) for b in bufs),
        in_specs=[_HBM] * k_ops + [_SEM, _SEM, ANY], out_specs=[_HBM] * k_ops,
        input_output_aliases={i: i for i in range(k_ops)},
        compiler_params=pltpu.CompilerParams(has_side_effects=_EFFECT),
    )(*bufs, sems[0], sems[1], after)
    return list(outs)


_ALL = ("x", "y", "c")


def _unit_rows(units):
    offs, off = [], 0
    for u in units:
        offs.append(off)
        off += u.shape[1]
    return offs, off


def scatter_start(name, units, carry):
    n_u, n_c = len(units), len(carry)
    offs, rows = _unit_rows(units)
    land = lax.empty((N_DEV, rows) + tuple(units[0].shape[2:]), units[0].dtype)
    fan = N_DEV - 1

    def body(*refs):
        u_refs, land_ref = refs[:n_u], refs[n_u]
        send, recv, loc = refs[n_u + 1 + n_c:n_u + 4 + n_c]
        me, peers = _group(_ALL)
        for j in range(n_u):
            rs = pl.ds(offs[j], units[j].shape[1])
            pltpu.make_async_copy(u_refs[j].at[me], land_ref.at[me, rs], loc.at[j]).start()
            for k, (idx, dev) in enumerate(peers):
                pltpu.make_async_remote_copy(src_ref=u_refs[j].at[idx], dst_ref=land_ref.at[me, rs], send_sem=send.at[fan * j + k],
                                             recv_sem=recv.at[fan * j + k], device_id=dev, device_id_type=MESH).start()

    thru = list(units) + [land] + list(carry)
    outs = pl.pallas_call(
        body, name=name,
        out_shape=(pltpu.SemaphoreType.DMA((fan * n_u,)), pltpu.SemaphoreType.DMA((fan * n_u,)), pltpu.SemaphoreType.DMA((n_u,)))
        + tuple(pltpu.HBM(a.shape, a.dtype) for a in thru),
        in_specs=[_HBM] * len(thru), out_specs=[_SEM] * 3 + [_HBM] * len(thru),
        input_output_aliases={i: 3 + i for i in range(len(thru))},
        compiler_params=pltpu.CompilerParams(has_side_effects=_EFFECT),
    )(*[pltpu.with_memory_space_constraint(a, pltpu.HBM) for a in thru])
    return tuple(outs[:3]), list(outs[3:3 + n_u]), outs[3 + n_u], list(outs[4 + n_u:])


def scatter_wait(name, units, land, sems, after):
    n_u = len(units)
    offs, _ = _unit_rows(units)
    fan = N_DEV - 1

    def body(*refs):
        u_refs, land_ref = refs[:n_u], refs[n_u]
        send, recv, loc = refs[n_u + 1:n_u + 4]
        me, peers = _group(_ALL)
        for j in range(n_u):
            rs = pl.ds(offs[j], units[j].shape[1])
            for k, (idx, dev) in enumerate(peers):
                cp = pltpu.make_async_remote_copy(src_ref=u_refs[j].at[idx], dst_ref=land_ref.at[idx, rs], send_sem=send.at[fan * j + k],
                                                  recv_sem=recv.at[fan * j + k], device_id=dev, device_id_type=MESH)
                cp.wait_send()
                cp.wait_recv()
            pltpu.make_async_copy(u_refs[j].at[me], land_ref.at[me, rs], loc.at[j]).wait()

    thru = list(units) + [land]
    outs = pl.pallas_call(
        body, name=name, out_shape=tuple(pltpu.HBM(a.shape, a.dtype) for a in thru),
        in_specs=[_HBM] * len(thru) + [_SEM] * 3 + [ANY], out_specs=[_HBM] * len(thru),
        input_output_aliases={i: i for i in range(len(thru))},
        compiler_params=pltpu.CompilerParams(has_side_effects=_EFFECT),
    )(*thru, *sems, after)
    return outs[n_u]


def _row_tile(rows, cap=512):
    return next(t for t in range(cap - cap % 16, 0, -16) if rows % t == 0)


def sum_shares(name, recv, me):
    n, rows, c = recv.shape
    tr = _row_tile(rows)

    def body(me_ref, *refs):
        acc = refs[0][...].astype(F32)
        for r in refs[1:n]:
            acc = acc + r[...].astype(F32)
        refs[n][...] = acc

    def slot(mask):
        return pl.BlockSpec((None, tr, c), lambda i, me, mask=mask: (jnp.bitwise_xor(me[0], mask), i, 0))

    spec = pltpu.PrefetchScalarGridSpec(
        num_scalar_prefetch=1, grid=(rows // tr,), in_specs=[slot(k) for k in range(n)],
        out_specs=pl.BlockSpec((tr, c), lambda i, me: (i, 0)))
    return pl.pallas_call(body, grid_spec=spec, out_shape=S((rows, c), F32),
                          compiler_params=_cp("parallel"), name=name)(me, *([recv] * n))


def sum_slots(name, slots):
    n, r, c = slots.shape

    def body(s_ref, o_ref):
        acc = s_ref[0]
        for j in range(1, n):
            acc = acc + s_ref[j]
        o_ref[...] = acc

    return pl.pallas_call(body, out_shape=S((r, c), F32), compiler_params=pltpu.CompilerParams(vmem_limit_bytes=VMEM_LIMIT),
                          name=name)(slots)


def adamw_units(name, pieces, transposed, w, m, v):
    n_l, k, n = w.shape
    tk = _tile(k, 512) if transposed else k
    p_rows = n if transposed else k
    arrs = [p[0] if isinstance(p, tuple) else p for p in pieces]
    offs = [p[1] // p_rows if isinstance(p, tuple) else 0 for p in pieces]
    assert all(not isinstance(p, tuple) or p[1] % p_rows == 0 for p in pieces)
    c1 = 1.0 - ADAM_B1 ** ADAM_STEP
    c2 = 1.0 - ADAM_B2 ** ADAM_STEP

    def body(*refs):
        p_refs, (w_ref, m_ref, v_ref, g_ref, d_ref, m2_ref, v2_ref) = refs[:n_l], refs[n_l:]
        gv = p_refs[0][...]
        for j in range(1, n_l):
            gv = jnp.where(pl.program_id(0) == j, p_refs[j][...], gv)
        if transposed:
            gv = gv.T
        m2 = ADAM_B1 * m_ref[...] + (1.0 - ADAM_B1) * gv
        v2 = ADAM_B2 * v_ref[...] + (1.0 - ADAM_B2) * (gv * gv)
        g_ref[...] = gv
        m2_ref[...] = m2
        v2_ref[...] = v2
        d_ref[...] = -ADAM_LR * ((m2 / c1) / (jnp.sqrt(v2 / c2) + ADAM_EPS) + ADAM_WD * w_ref[...])

    def piece(j, o):
        if transposed:
            return pl.BlockSpec((n, tk), lambda l, i, o=o, j=j: (o, jnp.where(l == j, i, 0)))
        return pl.BlockSpec((k, n), lambda l, i, o=o: (o, 0))

    blk = pl.BlockSpec((None, tk, n), lambda l, i: (l, i, 0))
    return tuple(pl.pallas_call(body, grid=(n_l, k // tk), in_specs=[piece(j, o) for j, o in enumerate(offs)] + [blk] * 3, out_specs=[blk] * 4,
                                out_shape=[S(w.shape, F32)] * 4, compiler_params=_cp("parallel", "parallel"),
                                name=name)(*arrs, w, m, v))


def adamw_native(name, g, w, m, v, tr=512):
    shape = w.shape
    cols = shape[-1]
    rows = w.size // cols
    tr = _tile(rows, tr) if rows % 8 == 0 else rows
    c1 = 1.0 - ADAM_B1 ** ADAM_STEP
    c2 = 1.0 - ADAM_B2 ** ADAM_STEP

    def body(g_ref, w_ref, m_ref, v_ref, d_ref, m2_ref, v2_ref):
        gv = g_ref[...]
        m2 = ADAM_B1 * m_ref[...] + (1.0 - ADAM_B1) * gv
        v2 = ADAM_B2 * v_ref[...] + (1.0 - ADAM_B2) * (gv * gv)
        m2_ref[...] = m2
        v2_ref[...] = v2
        d_ref[...] = -ADAM_LR * ((m2 / c1) / (jnp.sqrt(v2 / c2) + ADAM_EPS) + ADAM_WD * w_ref[...])

    row = pl.BlockSpec((tr, cols), lambda i: (i, 0))
    outs = pl.pallas_call(body, grid=(rows // tr,), in_specs=[row] * 4, out_specs=[row] * 3,
                          out_shape=[S((rows, cols), F32)] * 3, compiler_params=_cp("parallel"),
                          name=name)(*[a.reshape(rows, cols) for a in (g, w, m, v)])
    return tuple(o.reshape(shape) for o in outs)


_REPLICATED = ("e_norm", "e_gmlp_w", "e_gmlp_b", "e_conv_b", "e_conv_ln_g", "e_conv_ln_b", "o_lam_re", "o_lam_im", "o_log_dt",
               "o_b_re", "o_b_im", "o_c_re", "o_c_im", "ca_norm", "ca_mem_norm", "ffn_norm", "final_norm")
_ORDER = ("e_norm", "e_w_in", "e_gmlp_w", "e_gmlp_b", "e_conv_w", "e_conv_b", "e_conv_ln_g", "e_conv_ln_b", "e_w_out",
          "o_norm", "o_w_in", "o_lam_re", "o_lam_im", "o_log_dt", "o_b_re", "o_b_im", "o_c_re", "o_c_im", "o_d", "o_w_out",
          "ca_norm", "ca_mem_norm", "ca_wq", "ca_wk", "ca_wv", "ca_wo", "ffn_norm", "ffn_w_gate", "ffn_w_up", "ffn_w_down",
          "final_norm")


def _rows128(a, multiple=8):
    flat = a.reshape(-1)
    rows = -(-flat.shape[0] // (LANES * multiple)) * multiple
    return jnp.pad(flat, (0, rows * LANES - flat.shape[0])).reshape(rows, LANES)


def _shard(full, axis):
    s = full.shape
    return jnp.moveaxis(full.reshape(s[:axis] + (N_DEV, s[axis] // N_DEV) + s[axis + 1:]), axis, 0)


_UNITS = (("e_w_in", 0, True), ("e_w_out", 0, False), ("o_w_in", 0, False), ("o_w_out", 0, True),
          *[(n, i, False) for n in ("ca_wq", "ca_wk", "ca_wv", "ca_wo") for i in (0, 1)],
          *[(n, i, tr) for n, tr in (("ffn_w_gate", True), ("ffn_w_up", True), ("ffn_w_down", False)) for i in (0, 1)])
_LAYERED = ("ca_wq", "ca_wk", "ca_wv", "ca_wo", "ffn_w_gate", "ffn_w_up", "ffn_w_down")
_SMALL_SHARDED = (("e_conv_w", 2), ("o_norm", 1), ("o_d", 1))
RS_ROW = 1024


def _unit_key(name, tr):
    return name + "_t" if tr else name


def _stage_of(name, layer):
    if name.startswith("e_"):
        return 0 if name == "e_w_in" else 1
    if name.startswith("o_"):
        return 4
    if name.startswith("ca_"):
        return 2 if layer == 0 else 5
    return 3 if layer == 0 else 6


GATHER_STAGES = 7
GATHER_DIRECT = (False, False, False, False, True, True, False)


def weight_fetcher(local):
    groups, meta = [[] for _ in range(GATHER_STAGES)], [[] for _ in range(GATHER_STAGES)]
    for name, layer, tr in _UNITS:
        blk = local[name][layer]
        st = _stage_of(name, layer)
        groups[st].append(_bf(blk.T if tr else blk))
        meta[st].append((name, layer, tr))
    small = jnp.concatenate([local[name].reshape(-1) for name, _ in _SMALL_SHARDED])
    groups[0].append(_rows128(small))
    direct = list(GATHER_DIRECT)
    sems, srcs, lands, token = gather_ici_start("ag_w_start", groups, direct)

    early = {}

    def fetch(stage, after, start_only=False):
        if start_only:
            landed = gather_ici_wait(f"ag_w_wait{stage}", srcs[stage], lands[stage], sems[stage], after, direct[stage])
            early[stage] = gather_d2d_start(f"ag_w_d2d{stage}_start", landed)
            return {}
        if stage in early:
            bufs = gather_d2d_wait(f"ag_w_d2d{stage}_wait", early[stage][1], early[stage][0], after)
        else:
            bufs = gather_ici_wait(f"ag_w_wait{stage}", srcs[stage], lands[stage], sems[stage], after, direct[stage])
            if not direct[stage]:
                bufs = gather_d2d(f"ag_w_d2d{stage}", bufs)
        got = {}
        for (name, layer, tr), blk, buf in zip(meta[stage], groups[stage], bufs):
            arr = buf.reshape((N_DEV * blk.shape[0],) + tuple(blk.shape[1:]))
            if name in _LAYERED:
                got[(_unit_key(name, tr), layer)] = arr
            else:
                got[_unit_key(name, tr)] = arr
        if stage == 0:
            flat = bufs[-1].reshape(N_DEV, -1)
            off = 0
            for name, axis in _SMALL_SHARDED:
                blk = local[name]
                seg = flat[:, off:off + blk.size].reshape((N_DEV,) + blk.shape)
                off += blk.size
                seg = jnp.moveaxis(seg, 0, axis)
                got[name] = seg.reshape(seg.shape[:axis] + (-1,) + seg.shape[axis + 2:])
            got["e_conv_w"] = got["e_conv_w"][0]
        return got

    return fetch, token


def _grad_stage_of(name, layer):
    if name.startswith("e_"):
        return 4
    if name.startswith("o_"):
        return 1
    if name.startswith("ca_"):
        return 3 if layer == 0 else 1
    return 2 if layer == 0 else 0


GRAD_STAGES = 5
SMALL_ROWS = 16


def gradient_reducer(local, mom, var):
    me = (4 * lax.axis_index("x") + 2 * lax.axis_index("y") + lax.axis_index("c")).astype(jnp.int32).reshape(1)
    pending = []

    def start(stage, grads, carry):
        def grad_of(unit):
            key = _unit_key(unit[0], unit[2])
            return grads[(key, unit[1])] if unit[0] in _LAYERED else grads[key]

        units = sorted([u for u in _UNITS if _grad_stage_of(u[0], u[1]) == stage], key=lambda u: -grad_of(u).size)
        parts, spans = [], []
        for unit in units:
            g = grad_of(unit)
            part = g.reshape(N_DEV, -1, RS_ROW)
            spans.append((part.shape[1], g.shape[0] // N_DEV, g.shape[1]))
            parts.append(part)
        if stage == GRAD_STAGES - 1:
            small = jnp.concatenate([_shard(grads[name], axis).reshape(N_DEV, -1) for name, axis in _SMALL_SHARDED], axis=1)
            small = jnp.pad(small, ((0, 0), (0, SMALL_ROWS * RS_ROW - small.shape[1])))
            parts.append(small.astype(BF16).reshape(N_DEV, SMALL_ROWS, RS_ROW))
        sems, sent, land, carry = scatter_start(f"rs_start{stage}", parts, carry)
        pending.append((stage, units, spans, sems, sent, land))
        return carry

    def finish(after):
        res, per_layer, small_flat = {}, {}, None
        for stage, units, spans, sems, sent, land in pending:
            land = scatter_wait(f"rs_wait{stage}", sent, land, sems, after)
            total = sum_shares(f"rs_sum{stage}", land, me)
            off = 0
            for (name, layer, tr), (rows, r, c) in zip(units, spans):
                piece = (total, off) if c == RS_ROW else total[off:off + rows].reshape(r, c)
                per_layer.setdefault(name, {})[layer] = (piece, tr)
                off += rows
            if stage == GRAD_STAGES - 1:
                small_flat = total[off:off + SMALL_ROWS].reshape(-1)
        for name, by_layer in per_layer.items():
            pieces = [by_layer[i][0] for i in sorted(by_layer)]
            res[name] = adamw_units("adamw_" + name, pieces, by_layer[0][1], local[name], mom[name], var[name])
        off = 0
        for name, _ in _SMALL_SHARDED:
            blk = local[name]
            g = small_flat[off:off + blk.size].reshape(blk.shape)
            off += blk.size
            res[name] = (g,) + adamw_native("adamw_" + name, g, blk, mom[name], var[name])
        return res

    return start, finish


def replicated_start(grads, loss):
    pack = jnp.concatenate([_rows128(grads[name]) for name in _REPLICATED] + [_rows128(loss)], axis=0)
    sems, srcs, lands, token = gather_ici_start("ag_g_start", [[pack]], [False])
    return sems[0], srcs[0], lands[0], token


def replicated_finish(handle, after, w, mom, var):
    sems, srcs, lands, _ = handle
    (buf,) = gather_d2d("ag_g_d2d", gather_ici_wait("ag_g_wait", srcs, lands, sems, after))
    rows = srcs[0].shape[0]
    total = sum_slots("ag_g_sum", buf.reshape(N_DEV, rows, LANES))
    res, off = {}, 0
    for name in _REPLICATED:
        n = w[name].size
        nr = -(-n // (LANES * 8)) * 8
        g = total[off:off + nr].reshape(-1)[:n].reshape(w[name].shape)
        off += nr
        res[name] = (g,) + adamw_native("adamw_" + name, g, w[name], mom[name], var[name])
    return res, total[off, 0]


def kernel(x, mem, e_norm, e_w_in, e_gmlp_w, e_gmlp_b, e_conv_w, e_conv_b, e_conv_ln_g, e_conv_ln_b, e_w_out, o_norm, o_w_in, o_lam_re, o_lam_im, o_log_dt, o_b_re, o_b_im, o_c_re, o_c_im, o_d, o_w_out, ca_norm, ca_mem_norm, ca_wq, ca_wk, ca_wv, ca_wo, ffn_norm, ffn_w_gate, ffn_w_up, ffn_w_down, final_norm, loss_target, m_e_norm, m_e_w_in, m_e_gmlp_w, m_e_gmlp_b, m_e_conv_w, m_e_conv_b, m_e_conv_ln_g, m_e_conv_ln_b, m_e_w_out, m_o_norm, m_o_w_in, m_o_lam_re, m_o_lam_im, m_o_log_dt, m_o_b_re, m_o_b_im, m_o_c_re, m_o_c_im, m_o_d, m_o_w_out, m_ca_norm, m_ca_mem_norm, m_ca_wq, m_ca_wk, m_ca_wv, m_ca_wo, m_ffn_norm, m_ffn_w_gate, m_ffn_w_up, m_ffn_w_down, m_final_norm, v_e_norm, v_e_w_in, v_e_gmlp_w, v_e_gmlp_b, v_e_conv_w, v_e_conv_b, v_e_conv_ln_g, v_e_conv_ln_b, v_e_w_out, v_o_norm, v_o_w_in, v_o_lam_re, v_o_lam_im, v_o_log_dt, v_o_b_re, v_o_b_im, v_o_c_re, v_o_c_im, v_o_d, v_o_w_out, v_ca_norm, v_ca_mem_norm, v_ca_wq, v_ca_wk, v_ca_wv, v_ca_wo, v_ffn_norm, v_ffn_w_gate, v_ffn_w_up, v_ffn_w_down, v_final_norm):
    given = dict(locals())
    local = {k: given[k] for k in _ORDER}
    mom = {k: given["m_" + k] for k in _ORDER}
    var = {k: given["v_" + k] for k in _ORDER}

    w = {}
    w.update({
        "e_norm": e_norm, "e_gmlp_w": e_gmlp_w[0], "e_gmlp_b": e_gmlp_b.reshape(A_GROUPS, GMLP_BLOCK, 1),
        "e_conv_b": e_conv_b, "e_conv_ln_g": e_conv_ln_g, "e_conv_ln_b": e_conv_ln_b,
        "o_lam_re": o_lam_re[0], "o_lam_im": o_lam_im[0], "o_log_dt": o_log_dt[0], "o_b_re": o_b_re[0], "o_b_im": o_b_im[0],
        "o_c_re": o_c_re[0], "o_c_im": o_c_im[0], "ca_norm": ca_norm, "ca_mem_norm": ca_mem_norm, "ffn_norm": ffn_norm,
        "final_norm": final_norm.reshape(1, D_MODEL),
    })
    start_reduce, finish_reduce = gradient_reducer(local, mom, var)
    fetch, token = weight_fetcher(local)
    loss_part, grad_x, grads = local_step(x[0], mem[0], loss_target[0], w, fetch, start_reduce, token[0:1, 0:1])
    grads["final_norm"] = grads["final_norm"].reshape(D_MODEL)

    handle = replicated_start(grads, loss_part)
    res = finish_reduce(handle[3])
    rep, loss = replicated_finish(handle, res["ffn_w_down"][1], local, mom, var)
    res.update(rep)
    return (loss, grad_x[None], *[res[k][0] for k in _ORDER], *[res[k][1] for k in _ORDER],
            *[res[k][2] for k in _ORDER], *[res[k][3] for k in _ORDER])
```
